```python
import math
import jax, jax.numpy as jnp
from jax import lax
import numpy as np

D_MODEL = 1024
BATCH = 8
SEQ = 4096
DEPTH = 2

D_MIX = D_MODEL
A_HEADS = 4
A_HEAD_DIM = 128
A_WIDTH = A_HEADS * A_HEAD_DIM
CONV_K = 4
CHUNK = 64
B_Q_HEADS = 8
B_KV_HEADS = 2
B_HEAD_DIM = 64
B_GROUP = B_Q_HEADS // B_KV_HEADS
B_WIDTH = B_Q_HEADS * B_HEAD_DIM
B_KV_WIDTH = B_KV_HEADS * B_HEAD_DIM
WINDOW = 128
BLOCK = 128
IN_SIZES = (A_WIDTH, A_WIDTH, A_WIDTH, A_WIDTH, A_HEADS, A_HEADS,
            B_WIDTH, B_KV_WIDTH, B_KV_WIDTH, B_WIDTH)
IN_COLS = sum(IN_SIZES)
DEEPNORM_ALPHA = (2 * DEPTH) ** 0.25
DEEPNORM_BETA = (8 * DEPTH) ** -0.25
LN_EPS = 1e-5
RMS_EPS = 1e-6
L2_EPS = 1e-6

kernel_name = "hybrid_deltanet_swa_sink_alibi_deepnorm"


def _alibi_slopes(n_heads):
    return jnp.asarray([2.0 ** (-8.0 * (h + 1) / n_heads) for h in range(n_heads)], dtype=jnp.float32)


def _layernorm(x, g, b):
    xf = x.astype(jnp.float32)
    mu = jnp.mean(xf, axis=-1, keepdims=True)
    var = jnp.mean(jnp.square(xf - mu), axis=-1, keepdims=True)
    y = (xf - mu) * lax.rsqrt(var + LN_EPS) * g.astype(jnp.float32) + b.astype(jnp.float32)
    return y.astype(x.dtype)


def _l2norm(t):
    return t * lax.rsqrt(jnp.sum(jnp.square(t), axis=-1, keepdims=True) + L2_EPS)


def _short_conv(x, w):
    c = x.shape[-1]
    return lax.conv_general_dilated(
        x, w[:, None, :].astype(x.dtype), window_strides=(1,), padding=[(CONV_K - 1, 0)],
        dimension_numbers=('NWC', 'WIO', 'NWC'), feature_group_count=c)


def _gated_delta_rule(q, k, v, g, beta):
    bsz, t_len, h, dk = q.shape
    dv = v.shape[-1]
    n = t_len // CHUNK

    def chunks(t):
        t = t.reshape((bsz, n, CHUNK, h) + t.shape[3:])
        return jnp.moveaxis(t, 3, 1)

    q, k, v, g, beta = chunks(q), chunks(k), chunks(v), chunks(g), chunks(beta)
    g = jnp.cumsum(g, axis=-1)
    causal = jnp.tril(jnp.ones((CHUNK, CHUNK), dtype=bool))
    strict = jnp.tril(jnp.ones((CHUNK, CHUNK), dtype=bool), -1)
    decay = jnp.exp(jnp.where(causal, g[..., :, None] - g[..., None, :], -jnp.inf))
    k_beta = k * beta[..., None]
    a_mat = jnp.where(strict, jnp.einsum('bhncd,bhnsd->bhncs', k_beta, k) * decay, 0.0)
    eye = jnp.eye(CHUNK, dtype=a_mat.dtype)
    t_mat = lax.linalg.triangular_solve(eye + a_mat, jnp.broadcast_to(eye, a_mat.shape),
                                        left_side=True, lower=True)
    u = jnp.einsum('bhncs,bhnse->bhnce', t_mat, v * beta[..., None])
    w = jnp.einsum('bhncs,bhnsd->bhncd', t_mat, k_beta * jnp.exp(g)[..., None])
    qk = jnp.where(causal, jnp.einsum('bhncd,bhnsd->bhncs', q, k) * decay, 0.0)
    q_dec = q * jnp.exp(g)[..., None]
    k_dec = k * jnp.exp(g[..., -1:] - g)[..., None]
    g_tot = jnp.exp(g[..., -1])
    xs = tuple(jnp.moveaxis(t, 2, 0) for t in (q_dec, k_dec, u, w, qk, g_tot))

    def step(s, inp):
        q_c, k_c, u_c, w_c, qk_c, gt = inp
        v_new = u_c - jnp.einsum('bhcd,bhde->bhce', w_c, s)
        o = jnp.einsum('bhcd,bhde->bhce', q_c, s) + jnp.einsum('bhcs,bhse->bhce', qk_c, v_new)
        s = s * gt[..., None, None] + jnp.einsum('bhcd,bhce->bhde', k_c, v_new)
        return s, o

    s0 = jnp.zeros((bsz, h, dk, dv), jnp.float32)
    _, o = lax.scan(step, s0, xs)
    o = jnp.moveaxis(o, 0, 2)
    return jnp.moveaxis(o, 1, 3).reshape(bsz, t_len, h, dv)


def _deltanet_group(q, k, v, z, b, a, conv_w, a_log, dt_bias, norm_w):
    bsz, t_len, _ = q.shape
    qkv = jax.nn.silu(_short_conv(jnp.concatenate([q, k, v], axis=-1), conv_w))
    q, k, v = jnp.split(qkv, 3, axis=-1)
    heads = lambda t: t.reshape(bsz, t_len, A_HEADS, A_HEAD_DIM).astype(jnp.float32)
    q = _l2norm(heads(q)) * (A_HEAD_DIM ** -0.5)
    k = _l2norm(heads(k))
    v = heads(v)
    beta = jax.nn.sigmoid(b.astype(jnp.float32))
    g = -jnp.exp(a_log.astype(jnp.float32)) * jax.nn.softplus(
        a.astype(jnp.float32) + dt_bias.astype(jnp.float32))
    o = _gated_delta_rule(q, k, v, g, beta)
    o = o * lax.rsqrt(jnp.mean(jnp.square(o), axis=-1, keepdims=True) + RMS_EPS) * norm_w.astype(jnp.float32)
    return o.reshape(bsz, t_len, A_WIDTH).astype(z.dtype) * jax.nn.silu(z)


def _swa_group(q, k, v, z, sinks):
    bsz, t_len, _ = q.shape
    n = t_len // BLOCK
    q = q.reshape(bsz, n, BLOCK, B_KV_HEADS, B_GROUP, B_HEAD_DIM)
    k = k.reshape(bsz, t_len, B_KV_HEADS, B_HEAD_DIM)
    v = v.reshape(bsz, t_len, B_KV_HEADS, B_HEAD_DIM)

    def band(t):
        prev = jnp.pad(t, ((0, 0), (BLOCK, 0), (0, 0), (0, 0)))[:, :t_len]
        shp = (bsz, n, BLOCK, B_KV_HEADS, B_HEAD_DIM)
        return jnp.concatenate([prev.reshape(shp), t.reshape(shp)], axis=2)

    kb, vb = band(k), band(v)
    s = jnp.einsum('bnqhgd,bnshd->bhgnqs', q, kb).astype(jnp.float32) * (B_HEAD_DIM ** -0.5)
    q_idx = jnp.arange(BLOCK)[:, None]
    s_idx = jnp.arange(2 * BLOCK)[None, :]
    dist = q_idx + BLOCK - s_idx
    key_pos = jnp.arange(n)[:, None] * BLOCK - BLOCK + jnp.arange(2 * BLOCK)[None, :]
    mask = ((dist >= 0) & (dist < WINDOW))[None] & (key_pos >= 0)[:, None, :]
    slopes = _alibi_slopes(B_Q_HEADS).reshape(B_KV_HEADS, B_GROUP)
    s = s - slopes[:, :, None, None, None] * dist.astype(jnp.float32)
    s = jnp.where(mask, s, -jnp.inf)
    sink = sinks.astype(jnp.float32).reshape(B_KV_HEADS, B_GROUP)[:, :, None, None]
    m = jnp.maximum(jnp.max(s, axis=-1), sink)
    p = jnp.exp(s - m[..., None])
    p = p / (jnp.sum(p, axis=-1, keepdims=True) + jnp.exp(sink - m)[..., None])
    o = jnp.einsum('bhgnqs,bnshd->bnqhgd', p.astype(vb.dtype), vb)
    return o.reshape(bsz, t_len, B_WIDTH) * jax.nn.silu(z)


def _layer(x, w_in, conv_w, a_log, dt_bias, norm_w, sinks, w_out, ln_g, ln_b):
    h = jnp.einsum('btd,dc->btc', x, w_in)
    offsets = [sum(IN_SIZES[:i]) for i in range(1, len(IN_SIZES))]
    qa, ka, va, za, ba, aa, qb, kb, vb, zb = jnp.split(h, offsets, axis=-1)
    ya = _deltanet_group(qa, ka, va, za, ba, aa, conv_w, a_log, dt_bias, norm_w)
    yb = _swa_group(qb, kb, vb, zb, sinks)
    y = jnp.einsum('btc,cd->btd', jnp.concatenate([ya, yb], axis=-1), w_out)
    return _layernorm(DEEPNORM_ALPHA * x + y, ln_g, ln_b)


def _fwd_setup_inputs(seed: int = 0) -> dict:
    key = jax.random.key(seed)
    ks = jax.random.split(key, 10)
    x = jax.random.normal(ks[0], (BATCH, SEQ, D_MODEL), jnp.float32)
    col_scale = np.concatenate([
        np.ones(2 * A_WIDTH), np.full(A_WIDTH, DEEPNORM_BETA), np.ones(A_WIDTH + 2 * A_HEADS),
        np.ones(B_WIDTH + B_KV_WIDTH), np.full(B_KV_WIDTH, DEEPNORM_BETA), np.ones(B_WIDTH)]).astype(np.float32)
    w_in = jax.random.normal(ks[1], (DEPTH, D_MODEL, IN_COLS), jnp.float32) * (D_MODEL ** -0.5) * jnp.asarray(col_scale)
    conv_w = jax.random.normal(ks[2], (DEPTH, CONV_K, 3 * A_WIDTH), jnp.float32) * (CONV_K ** -0.5)
    a_log = jnp.log(jax.random.uniform(ks[3], (DEPTH, A_HEADS), jnp.float32, 1.0, 16.0))
    dt = jnp.exp(jax.random.uniform(ks[4], (DEPTH, A_HEADS), jnp.float32, math.log(1e-3), math.log(1e-1)))
    dt_bias = dt + jnp.log(-jnp.expm1(-dt))
    norm_w = 1.0 + 0.02 * jax.random.normal(ks[5], (DEPTH, A_HEAD_DIM), jnp.float32)
    sinks = 0.5 * jax.random.normal(ks[6], (DEPTH, B_Q_HEADS), jnp.float32)
    w_out = jax.random.normal(ks[7], (DEPTH, D_MIX, D_MODEL), jnp.float32) * (D_MIX ** -0.5) * DEEPNORM_BETA
    ln_g = 1.0 + 0.02 * jax.random.normal(ks[8], (DEPTH, D_MODEL), jnp.float32)
    ln_b = 0.02 * jax.random.normal(ks[9], (DEPTH, D_MODEL), jnp.float32)
    return {"x": x, "w_in": w_in, "conv_w": conv_w, "a_log": a_log, "dt_bias": dt_bias,
            "norm_w": norm_w, "sinks": sinks, "w_out": w_out, "ln_g": ln_g, "ln_b": ln_b}


def _fwd_reference(x, w_in, conv_w, a_log, dt_bias, norm_w, sinks, w_out, ln_g, ln_b):
    for l in range(DEPTH):
        x = _layer(x, w_in[l], conv_w[l], a_log[l], dt_bias[l], norm_w[l], sinks[l],
                   w_out[l], ln_g[l], ln_b[l])
    return x


import jax as _jax
import jax.numpy as _jnp

TWIN_FORMAT = 'train_step'
FWD_PARAMS = ['x', 'w_in', 'conv_w', 'a_log', 'dt_bias', 'norm_w', 'sinks', 'w_out', 'ln_g', 'ln_b']
TWIN_WEIGHTS = ['w_in', 'conv_w', 'a_log', 'dt_bias', 'norm_w', 'sinks', 'w_out', 'ln_g', 'ln_b']
TWIN_DIFF_INPUT = 'x'
TWIN_INPUTS = ['x', 'w_in', 'conv_w', 'a_log', 'dt_bias', 'norm_w', 'sinks', 'w_out', 'ln_g', 'ln_b', 'loss_target', 'm_w_in', 'm_conv_w', 'm_a_log', 'm_dt_bias', 'm_norm_w', 'm_sinks', 'm_w_out', 'm_ln_g', 'm_ln_b', 'v_w_in', 'v_conv_w', 'v_a_log', 'v_dt_bias', 'v_norm_w', 'v_sinks', 'v_w_out', 'v_ln_g', 'v_ln_b']
TWIN_OUTPUTS = ['loss', 'grad_x', 'grad_w_in', 'grad_conv_w', 'grad_a_log', 'grad_dt_bias', 'grad_norm_w', 'grad_sinks', 'grad_w_out', 'grad_ln_g', 'grad_ln_b', 'delta_w_in', 'delta_conv_w', 'delta_a_log', 'delta_dt_bias', 'delta_norm_w', 'delta_sinks', 'delta_w_out', 'delta_ln_g', 'delta_ln_b', 'new_m_w_in', 'new_m_conv_w', 'new_m_a_log', 'new_m_dt_bias', 'new_m_norm_w', 'new_m_sinks', 'new_m_w_out', 'new_m_ln_g', 'new_m_ln_b', 'new_v_w_in', 'new_v_conv_w', 'new_v_a_log', 'new_v_dt_bias', 'new_v_norm_w', 'new_v_sinks', 'new_v_w_out', 'new_v_ln_g', 'new_v_ln_b']
TWIN_LEAF_KINDS = {'loss': 'loss', 'grad_x': 'grad_x', 'grad_w_in': 'grad_w', 'grad_conv_w': 'grad_w', 'grad_a_log': 'grad_w', 'grad_dt_bias': 'grad_w', 'grad_norm_w': 'grad_w', 'grad_sinks': 'grad_w', 'grad_w_out': 'grad_w', 'grad_ln_g': 'grad_w', 'grad_ln_b': 'grad_w', 'delta_w_in': 'delta_w', 'delta_conv_w': 'delta_w', 'delta_a_log': 'delta_w', 'delta_dt_bias': 'delta_w', 'delta_norm_w': 'delta_w', 'delta_sinks': 'delta_w', 'delta_w_out': 'delta_w', 'delta_ln_g': 'delta_w', 'delta_ln_b': 'delta_w', 'new_m_w_in': 'new_m', 'new_m_conv_w': 'new_m', 'new_m_a_log': 'new_m', 'new_m_dt_bias': 'new_m', 'new_m_norm_w': 'new_m', 'new_m_sinks': 'new_m', 'new_m_w_out': 'new_m', 'new_m_ln_g': 'new_m', 'new_m_ln_b': 'new_m', 'new_v_w_in': 'new_v', 'new_v_conv_w': 'new_v', 'new_v_a_log': 'new_v', 'new_v_dt_bias': 'new_v', 'new_v_norm_w': 'new_v', 'new_v_sinks': 'new_v', 'new_v_w_out': 'new_v', 'new_v_ln_g': 'new_v', 'new_v_ln_b': 'new_v'}


def _forward(args):
    return _fwd_reference(*[args[k] for k in FWD_PARAMS])


def _output_shape():
    def fwd():
        inp = _fwd_setup_inputs(0)
        return _fwd_reference(*[inp[k] for k in FWD_PARAMS])
    out = _jax.eval_shape(fwd)
    return out.shape, out.dtype

N_MICROBATCH = 1
ADAM_LR = 0.001
ADAM_B1 = 0.9
ADAM_B2 = 0.999
ADAM_EPS = 1e-08
ADAM_WD = 0.01
ADAM_STEP = 10
PER_EXAMPLE_BATCH_AXIS = {'x': 0, 'loss_target': 0}
SHARED_INPUTS = []
_WEIGHT_DTYPES = {'w_in': _jnp.float32, 'conv_w': _jnp.float32, 'a_log': _jnp.float32, 'dt_bias': _jnp.float32, 'norm_w': _jnp.float32, 'sinks': _jnp.float32, 'w_out': _jnp.float32, 'ln_g': _jnp.float32, 'ln_b': _jnp.float32}
MOMENT_SCALE = {'w_in': 3.365811e-02, 'conv_w': 3.007763e-02, 'a_log': 1.763462e-01, 'dt_bias': 1.739213e-01, 'norm_w': 7.624680e-02, 'sinks': 5.927547e-03, 'w_out': 5.421767e-02, 'ln_g': 2.266173e+01, 'ln_b': 9.784442e-01}


def _to_microbatches(a, axis):
    t = _jnp.moveaxis(a, axis, 0)
    t = t.reshape((N_MICROBATCH, t.shape[0] // N_MICROBATCH) + t.shape[1:])
    return _jnp.moveaxis(t, 1, axis + 1)


def setup_inputs(seed: int = 0) -> dict:
    inp = _fwd_setup_inputs(seed)
    key = _jax.random.fold_in(_jax.random.key(seed), 7919)
    shape, _ = _output_shape()
    out = dict(inp)
    out["loss_target"] = _jax.random.normal(_jax.random.fold_in(key, 0), shape, _jnp.float32)
    for i, name in enumerate(TWIN_WEIGHTS):
        w = inp[name].astype(_jnp.float32)
        if MOMENT_SCALE is None:
            s = _jnp.sqrt(_jnp.mean(_jnp.square(w)) + 1e-30)
        else:
            s = MOMENT_SCALE[name]
        km, kv = _jax.random.split(_jax.random.fold_in(key, i + 1))
        out[name] = w
        out["m_" + name] = s * _jax.random.normal(km, w.shape, _jnp.float32)
        out["v_" + name] = (s * s) * _jax.random.uniform(kv, w.shape, _jnp.float32, 0.5, 1.5)
    if N_MICROBATCH > 1:
        for name, axis in PER_EXAMPLE_BATCH_AXIS.items():
            out[name] = _to_microbatches(out[name], axis)
    return {'x': out['x'], 'w_in': out['w_in'], 'conv_w': out['conv_w'], 'a_log': out['a_log'], 'dt_bias': out['dt_bias'], 'norm_w': out['norm_w'], 'sinks': out['sinks'], 'w_out': out['w_out'], 'ln_g': out['ln_g'], 'ln_b': out['ln_b'], 'loss_target': out['loss_target'], 'm_w_in': out['m_w_in'], 'm_conv_w': out['m_conv_w'], 'm_a_log': out['m_a_log'], 'm_dt_bias': out['m_dt_bias'], 'm_norm_w': out['m_norm_w'], 'm_sinks': out['m_sinks'], 'm_w_out': out['m_w_out'], 'm_ln_g': out['m_ln_g'], 'm_ln_b': out['m_ln_b'], 'v_w_in': out['v_w_in'], 'v_conv_w': out['v_conv_w'], 'v_a_log': out['v_a_log'], 'v_dt_bias': out['v_dt_bias'], 'v_norm_w': out['v_norm_w'], 'v_sinks': out['v_sinks'], 'v_w_out': out['v_w_out'], 'v_ln_g': out['v_ln_g'], 'v_ln_b': out['v_ln_b']}


def _loss(weights, diff, rest, loss_target):
    with _jax.named_scope("forward"):
        args = {**rest, TWIN_DIFF_INPUT: diff, **{k: w.astype(_WEIGHT_DTYPES[k]) for k, w in weights.items()}}
        y = _forward(args)
    with _jax.named_scope("loss_head"):
        err = _jnp.square(y.astype(_jnp.float32) - loss_target)
        return 0.5 * _jnp.sum(_jnp.mean(err, axis=-1)) if err.ndim else 0.5 * err


def _adamw(w, g, m, v):
    m = ADAM_B1 * m + (1.0 - ADAM_B1) * g
    v = ADAM_B2 * v + (1.0 - ADAM_B2) * _jnp.square(g)
    m_hat = m / (1.0 - ADAM_B1 ** ADAM_STEP)
    v_hat = v / (1.0 - ADAM_B2 ** ADAM_STEP)
    delta = -ADAM_LR * (m_hat / (_jnp.sqrt(v_hat) + ADAM_EPS) + ADAM_WD * w)
    return delta, m, v


def reference(x, w_in, conv_w, a_log, dt_bias, norm_w, sinks, w_out, ln_g, ln_b, loss_target, m_w_in, m_conv_w, m_a_log, m_dt_bias, m_norm_w, m_sinks, m_w_out, m_ln_g, m_ln_b, v_w_in, v_conv_w, v_a_log, v_dt_bias, v_norm_w, v_sinks, v_w_out, v_ln_g, v_ln_b):
    given = dict(x=x, w_in=w_in, conv_w=conv_w, a_log=a_log, dt_bias=dt_bias, norm_w=norm_w, sinks=sinks, w_out=w_out, ln_g=ln_g, ln_b=ln_b, loss_target=loss_target, m_w_in=m_w_in, m_conv_w=m_conv_w, m_a_log=m_a_log, m_dt_bias=m_dt_bias, m_norm_w=m_norm_w, m_sinks=m_sinks, m_w_out=m_w_out, m_ln_g=m_ln_g, m_ln_b=m_ln_b, v_w_in=v_w_in, v_conv_w=v_conv_w, v_a_log=v_a_log, v_dt_bias=v_dt_bias, v_norm_w=v_norm_w, v_sinks=v_sinks, v_w_out=v_w_out, v_ln_g=v_ln_g, v_ln_b=v_ln_b)
    weights = {n: given[n] for n in TWIN_WEIGHTS}
    shared = {n: given[n] for n in SHARED_INPUTS}
    per_example = {n: given[n] for n in ['x']}
    grad_fn = _jax.value_and_grad(_loss, argnums=(0, 1))

    def one_microbatch(ex, loss_target):
        ex = dict(ex)
        diff = ex.pop(TWIN_DIFF_INPUT)
        return grad_fn(weights, diff, {**shared, **ex}, loss_target)

    if N_MICROBATCH == 1:
        loss, (grad_w, grad_x) = one_microbatch(per_example, given["loss_target"])
    else:
        def body(carry, xs):
            loss_sum, grad_sum = carry
            l_k, (gw_k, gx_k) = one_microbatch(xs[0], xs[1])
            with _jax.named_scope("update"):
                return (loss_sum + l_k, _jax.tree.map(_jnp.add, grad_sum, gw_k)), gx_k

        init = (_jnp.zeros((), _jnp.float32), _jax.tree.map(_jnp.zeros_like, weights))
        (loss, grad_w), grad_x = _jax.lax.scan(body, init, (per_example, given["loss_target"]))
    with _jax.named_scope("update"):
        delta_w, new_m, new_v = {}, {}, {}
        for n in TWIN_WEIGHTS:
            delta_w[n], new_m[n], new_v[n] = _adamw(weights[n], grad_w[n], given["m_" + n], given["v_" + n])
    return (loss, grad_x, *[grad_w[n] for n in TWIN_WEIGHTS], *[delta_w[n] for n in TWIN_WEIGHTS],
            *[new_m[n] for n in TWIN_WEIGHTS], *[new_v[n] for n in TWIN_WEIGHTS])
```

```python
import functools
import math

import jax
import jax.numpy as jnp
from jax import lax
from jax.experimental import pallas as pl
from jax.experimental.pallas import tpu as pltpu

F32 = jnp.float32
BF16 = jnp.bfloat16
HI = lax.Precision.HIGHEST

D_MODEL = 1024
DEPTH = 2
A_HEADS = 4
A_HEAD_DIM = 128
A_WIDTH = 512
CONV_K = 4
CHUNK = 64
B_Q_HEADS = 8
B_KV_HEADS = 2
B_HEAD_DIM = 64
B_GROUP = 4
B_WIDTH = 512
B_KV_WIDTH = 128
BLOCK = 128
IN_COLS = 3336
DEEPNORM_ALPHA = (2 * DEPTH) ** 0.25
LN_EPS = 1e-5
RMS_EPS = 1e-6
L2_EPS = 1e-6
ADAM_LR = 0.001
ADAM_B1 = 0.9
ADAM_B2 = 0.999
ADAM_EPS = 1e-08
ADAM_WD = 0.01
ADAM_STEP = 10

N_SHARD = 4
IN_SHARD = IN_COLS // N_SHARD
OUT_SHARD = D_MODEL // N_SHARD
CONV_SHARD = 3 * A_WIDTH // N_SHARD

P_COLS = 3456
C_PRE = 0
C_ZA = 1536
C_QB = 2048
C_ZB = 2560
C_KB = 3072
C_VB = 3200
C_BG = 3328
LANE = 128
SUBLANE = 8
VMEM_LIMIT = 56 * 1024 * 1024
ALIBI = tuple(2.0 ** (-8.0 * (h + 1) / B_Q_HEADS) for h in range(B_Q_HEADS))
NEG = -1e30


def _cp(*sem):
    return pltpu.CompilerParams(dimension_semantics=sem, vmem_limit_bytes=VMEM_LIMIT)


def _dot(a, b):
    return jnp.dot(a.astype(BF16), b.astype(BF16), preferred_element_type=F32)


def _dot_nt(a, b):
    return lax.dot_general(a.astype(BF16), b.astype(BF16), (((1,), (1,)), ((), ())),
                           preferred_element_type=F32)


def _dot_tn(a, b):
    return lax.dot_general(a.astype(BF16), b.astype(BF16), (((0,), (0,)), ((), ())),
                           preferred_element_type=F32)


def _dot_hi(a, b):
    return jnp.dot(a, b, precision=HI, preferred_element_type=F32)


def _sigmoid(x):
    return jax.nn.sigmoid(x)


def _silu(x):
    return x * _sigmoid(x)


def _dsilu(x):
    s = _sigmoid(x)
    return s * (1.0 + x * (1.0 - s))


def _softplus(x):
    return jnp.maximum(x, 0.0) + jnp.log(1.0 + jnp.exp(-jnp.abs(x)))


def _shift_down(cur, before, s):
    if s == 0:
        return cur
    r = pltpu.roll(cur, s, 0)
    rb = pltpu.roll(before, s, 0)
    row = lax.broadcasted_iota(jnp.int32, before.shape, 0)
    head = jnp.where(row < s, rb, r[0:SUBLANE])
    return jnp.concatenate([head, r[SUBLANE:]], axis=0)


def _shift_up(cur, after, s):
    if s == 0:
        return cur
    n = cur.shape[0]
    r = pltpu.roll(cur, n - s, 0)
    ra = pltpu.roll(after, SUBLANE - s, 0)
    row = lax.broadcasted_iota(jnp.int32, after.shape, 0)
    tail = jnp.where(row >= SUBLANE - s, ra, r[n - SUBLANE:])
    return jnp.concatenate([r[:n - SUBLANE], tail], axis=0)


def _conv_fwd(cur, before, w):
    acc = cur * w[CONV_K - 1:CONV_K, :]
    for s in range(1, CONV_K):
        acc = acc + _shift_down(cur, before, s) * w[CONV_K - 1 - s:CONV_K - s, :]
    return acc


def _matmul(a, b, *, tm, tn, name):
    m, k = a.shape
    n = b.shape[1]

    def body(a_ref, b_ref, o_ref):
        o_ref[...] = jnp.dot(a_ref[...].astype(BF16), b_ref[...], preferred_element_type=F32)

    return pl.pallas_call(
        body, name=name, grid=(m // tm, n // tn),
        in_specs=[pl.BlockSpec((tm, k), lambda i, j: (i, 0)), pl.BlockSpec((k, tn), lambda i, j: (0, j))],
        out_specs=pl.BlockSpec((tm, tn), lambda i, j: (i, j)),
        out_shape=jax.ShapeDtypeStruct((m, n), F32),
        compiler_params=_cp("parallel", "parallel"))(a, b)


def _dn_pre(h, conv_w, par, *, tt, name):
    t = h.shape[0]
    cw = 3 * A_WIDTH
    hb = tt // SUBLANE

    def body(pre_ref, halo_ref, bgi_ref, cw_ref, par_ref, q_ref, k_ref, v_ref, bg_ref):
        i = pl.program_id(0)
        cur = pre_ref[...]
        before = jnp.where(i > 0, halo_ref[...], 0.0)
        s = _silu(_conv_fwd(cur, before, cw_ref[...]))
        for hd in range(A_HEADS):
            sl = slice(hd * LANE, (hd + 1) * LANE)
            tq = s[:, hd * LANE:(hd + 1) * LANE]
            q_ref[:, sl] = tq * (lax.rsqrt(jnp.sum(tq * tq, -1, keepdims=True) + L2_EPS) * (A_HEAD_DIM ** -0.5))
            tk = s[:, A_WIDTH + hd * LANE:A_WIDTH + (hd + 1) * LANE]
            k_ref[:, sl] = tk * lax.rsqrt(jnp.sum(tk * tk, -1, keepdims=True) + L2_EPS)
        v_ref[...] = s[:, 2 * A_WIDTH:]
        raw = bgi_ref[...]
        lane = lax.broadcasted_iota(jnp.int32, raw.shape, 1)
        g = -jnp.exp(par_ref[0:1, :]) * _softplus(raw + par_ref[1:2, :])
        bg_ref[...] = jnp.where(lane < A_HEADS, _sigmoid(raw), jnp.where(lane < 2 * A_HEADS, g, 0.0))

    wide = jax.ShapeDtypeStruct((t, A_WIDTH), F32)
    return pl.pallas_call(
        body, name=name, grid=(t // tt,),
        in_specs=[pl.BlockSpec((tt, cw), lambda i: (i, 0)),
                  pl.BlockSpec((SUBLANE, cw), lambda i: (jnp.maximum(i * hb - 1, 0), 0)),
                  pl.BlockSpec((tt, LANE), lambda i: (i, C_BG // LANE)),
                  pl.BlockSpec((CONV_K, cw), lambda i: (0, 0)),
                  pl.BlockSpec((SUBLANE, LANE), lambda i: (0, 0))],
        out_specs=[pl.BlockSpec((tt, A_WIDTH), lambda i: (i, 0))] * 3 + [pl.BlockSpec((tt, LANE), lambda i: (i, 0))],
        out_shape=[wide, wide, wide, jax.ShapeDtypeStruct((t, LANE), F32)],
        compiler_params=_cp("parallel"))(h, h, h, conv_w, par)


def _chunk_masks():
    r = lax.broadcasted_iota(jnp.int32, (CHUNK, CHUNK), 0)
    c = lax.broadcasted_iota(jnp.int32, (CHUNK, CHUNK), 1)
    return r >= c, r > c, r == c, r <= c


def _tri_inv(a, eye):
    x = -a
    tm = eye + x
    p = x
    for _ in range(5):
        p = _dot_hi(p, p)
        tm = tm + _dot_hi(tm, p)
    return tm


def _gc_forms(bg, causal, upper):
    lower = causal.astype(F32)
    gc_all = _dot_hi(lower, bg)
    ones = jnp.ones((CHUNK, CHUNK), F32)
    rows = []
    for hd in range(A_HEADS):
        gcol_raw = bg[:, A_HEADS + hd:A_HEADS + hd + 1]
        rows.append(_dot_hi(ones, jnp.where(upper, gcol_raw, 0.0)))
    return gc_all, rows


def _dn_scan_fwd(q, k, v, bg, *, name):
    t = q.shape[0]
    n = t // CHUNK

    def body(q_ref, k_ref, v_ref, bg_ref, o_ref, s_ref, tm_ref, state):
        @pl.when(pl.program_id(0) == 0)
        def _():
            state[...] = jnp.zeros_like(state)

        causal, strict, diag, upper = _chunk_masks()
        eye = diag.astype(F32)
        bg_v = bg_ref[...]
        gc_all, rows = _gc_forms(bg_v, causal, upper)
        for hd in range(A_HEADS):
            sl = slice(hd * LANE, (hd + 1) * LANE)
            qh, kh, vh = q_ref[:, sl], k_ref[:, sl], v_ref[:, sl]
            beta = bg_v[:, hd:hd + 1]
            gcol = gc_all[:, A_HEADS + hd:A_HEADS + hd + 1]
            dm = jnp.exp(jnp.where(causal, gcol - rows[hd], NEG))
            e = jnp.exp(gcol)
            kb = kh * beta
            a = jnp.where(strict, _dot_nt(kb, kh) * dm, 0.0)
            tmat = _tri_inv(a, eye)
            u = _dot(tmat, vh * beta)
            w = _dot(tmat, kb * e)
            qk = jnp.where(causal, _dot_nt(qh, kh) * dm, 0.0)
            s_old = state[hd]
            s_ref[0, hd] = s_old
            vn = u - _dot(w, s_old)
            o_ref[:, sl] = _dot(qh * e, s_old) + _dot(qk, vn)
            glast = gcol[CHUNK - 1:CHUNK, :]
            kd = kh * jnp.exp(glast - gcol)
            state[hd] = s_old * jnp.exp(glast) + _dot_tn(kd, vn)
            tm_ref[:, hd * CHUNK:(hd + 1) * CHUNK] = tmat

    blk = pl.BlockSpec((CHUNK, A_WIDTH), lambda i: (i, 0))
    return pl.pallas_call(
        body, name=name, grid=(n,),
        in_specs=[blk, blk, blk, pl.BlockSpec((CHUNK, LANE), lambda i: (i, 0))],
        out_specs=[blk, pl.BlockSpec((1, A_HEADS, LANE, LANE), lambda i: (i, 0, 0, 0)),
                   pl.BlockSpec((CHUNK, A_HEADS * CHUNK), lambda i: (i, 0))],
        out_shape=[jax.ShapeDtypeStruct((t, A_WIDTH), F32),
                   jax.ShapeDtypeStruct((n, A_HEADS, LANE, LANE), F32),
                   jax.ShapeDtypeStruct((t, A_HEADS * CHUNK), F32)],
        scratch_shapes=[pltpu.VMEM((A_HEADS, LANE, LANE), F32)],
        compiler_params=_cp("arbitrary"))(q, k, v, bg)


def _swa_probs(qh, kband, sink, slope, n_blk):
    s = _dot_nt(qh, kband) * (B_HEAD_DIM ** -0.5)
    qi = lax.broadcasted_iota(jnp.int32, (BLOCK, 2 * BLOCK), 0)
    si = lax.broadcasted_iota(jnp.int32, (BLOCK, 2 * BLOCK), 1)
    dist = qi + BLOCK - si
    mask = (dist >= 0) & (dist < BLOCK) & ((si >= BLOCK) | (n_blk > 0))
    s = jnp.where(mask, s - slope * dist.astype(F32), NEG)
    m = jnp.maximum(jnp.max(s, axis=-1, keepdims=True), sink)
    p = jnp.where(mask, jnp.exp(s - m), 0.0)
    ps = jnp.exp(sink - m)
    inv = 1.0 / (jnp.sum(p, axis=-1, keepdims=True) + ps)
    return p * inv, ps * inv


def _swa_specs():
    qspec = lambda c0: pl.BlockSpec((BLOCK, B_WIDTH), lambda i: (i, c0 // B_WIDTH))
    cur = lambda c0: pl.BlockSpec((BLOCK, LANE), lambda i: (i, c0 // LANE))
    prev = lambda c0: pl.BlockSpec((BLOCK, LANE), lambda i: (jnp.maximum(i - 1, 0), c0 // LANE))
    return qspec, cur, prev


def _swa_fwd(h, sinks_b, *, name):
    t = h.shape[0]
    qspec, cur, prev = _swa_specs()

    def body(q_ref, kc_ref, kp_ref, vc_ref, vp_ref, sk_ref, o_ref):
        n_blk = pl.program_id(0)
        kband = jnp.concatenate([kp_ref[...], kc_ref[...]], axis=0)
        vband = jnp.concatenate([vp_ref[...], vc_ref[...]], axis=0)
        for hq in range(B_Q_HEADS):
            hk = hq // B_GROUP
            ksl = slice(hk * B_HEAD_DIM, (hk + 1) * B_HEAD_DIM)
            qsl = slice(hq * B_HEAD_DIM, (hq + 1) * B_HEAD_DIM)
            p, _ = _swa_probs(q_ref[:, qsl], kband[:, ksl], sk_ref[hq:hq + 1, 0:1], ALIBI[hq], n_blk)
            o_ref[:, qsl] = _dot(p, vband[:, ksl])

    return pl.pallas_call(
        body, name=name, grid=(t // BLOCK,),
        in_specs=[qspec(C_QB), cur(C_KB), prev(C_KB), cur(C_VB), prev(C_VB),
                  pl.BlockSpec((B_Q_HEADS, LANE), lambda i: (0, 0))],
        out_specs=pl.BlockSpec((BLOCK, B_WIDTH), lambda i: (i, 0)),
        out_shape=jax.ShapeDtypeStruct((t, B_WIDTH), F32),
        compiler_params=_cp("parallel"))(h, h, h, h, h, sinks_b)


def _rms_gate(o, za, nw):
    outs = []
    for hd in range(A_HEADS):
        oh = o[:, hd * LANE:(hd + 1) * LANE]
        r = lax.rsqrt(jnp.mean(oh * oh, -1, keepdims=True) + RMS_EPS)
        outs.append(oh * r * nw)
    return jnp.concatenate(outs, axis=1) * _silu(za)


def _out_ln(x, oa, ob, h, norm_w, w_out, ln_g, ln_b, *, tm, name):
    t = x.shape[0]

    def body(x_ref, oa_ref, ob_ref, za_ref, zb_ref, nw_ref, w_ref, g_ref, b_ref, xn_ref, mx_ref, r_ref):
        ya = _rms_gate(oa_ref[...], za_ref[...], nw_ref[...])
        yb = ob_ref[...] * _silu(zb_ref[...])
        mixed = jnp.concatenate([ya, yb], axis=1).astype(BF16)
        mx_ref[...] = mixed
        r = DEEPNORM_ALPHA * x_ref[...] + jnp.dot(mixed, w_ref[...], preferred_element_type=F32)
        r_ref[...] = r
        mu = jnp.mean(r, -1, keepdims=True)
        xc = r - mu
        var = jnp.mean(xc * xc, -1, keepdims=True)
        xn_ref[...] = xc * lax.rsqrt(var + LN_EPS) * g_ref[...] + b_ref[...]

    row = lambda w, c: pl.BlockSpec((tm, w), lambda i: (i, c))
    full = lambda a, b: pl.BlockSpec((a, b), lambda i: (0, 0))
    return pl.pallas_call(
        body, name=name, grid=(t // tm,),
        in_specs=[row(D_MODEL, 0), row(A_WIDTH, 0), row(B_WIDTH, 0), row(A_WIDTH, C_ZA // A_WIDTH),
                  row(B_WIDTH, C_ZB // B_WIDTH), full(1, LANE), full(D_MODEL, D_MODEL), full(1, D_MODEL), full(1, D_MODEL)],
        out_specs=[row(D_MODEL, 0), row(D_MODEL, 0), row(D_MODEL, 0)],
        out_shape=[jax.ShapeDtypeStruct((t, D_MODEL), F32), jax.ShapeDtypeStruct((t, D_MODEL), BF16),
                   jax.ShapeDtypeStruct((t, D_MODEL), F32)],
        compiler_params=_cp("parallel"))(x, oa, ob, h, h, norm_w, w_out, ln_g, ln_b)


def _layer_fwd(x, wp, conv_w, par, sinks_b, norm_w, w_out_bf, ln_g, ln_b, l):
    h = _matmul(x, wp, tm=512, tn=1152, name=f"in_proj_{l}")
    q, k, v, bg = _dn_pre(h, conv_w, par, tt=512, name=f"dn_pre_{l}")
    oa, s_all, tmat = _dn_scan_fwd(q, k, v, bg, name=f"dn_scan_{l}")
    ob = _swa_fwd(h, sinks_b, name=f"swa_fwd_{l}")
    xn, mixed, r = _out_ln(x, oa, ob, h, norm_w, w_out_bf, ln_g, ln_b, tm=256, name=f"out_ln_{l}")
    return xn, dict(x=x, h=h, q=q, k=k, v=v, bg=bg, oa=oa, s_all=s_all, tmat=tmat, mixed=mixed, r=r)


def _loss_grad(xn, target, *, tm, name):
    t = xn.shape[0]

    def body(x_ref, t_ref, d_ref, l_ref):
        @pl.when(pl.program_id(0) == 0)
        def _():
            l_ref[...] = jnp.zeros_like(l_ref)

        err = x_ref[...] - t_ref[...]
        d_ref[...] = err * (1.0 / D_MODEL)
        l_ref[...] += 0.5 / D_MODEL * jnp.sum(err * err)

    row = pl.BlockSpec((tm, D_MODEL), lambda i: (i, 0))
    return pl.pallas_call(
        body, name=name, grid=(t // tm,), in_specs=[row, row],
        out_specs=[row, pl.BlockSpec((SUBLANE, LANE), lambda i: (0, 0))],
        out_shape=[jax.ShapeDtypeStruct((t, D_MODEL), F32), jax.ShapeDtypeStruct((SUBLANE, LANE), F32)],
        compiler_params=_cp("arbitrary"))(xn, target)


def _ln_out_bwd(dxn, r, mixed, ln_g, w_out, *, tm, name):
    t = dxn.shape[0]

    def body(dxn_ref, r_ref, mx_ref, g_ref, w_ref, dr_ref, dm_ref, dw_ref, dg_ref, db_ref):
        @pl.when(pl.program_id(0) == 0)
        def _():
            dw_ref[...] = jnp.zeros_like(dw_ref)
            dg_ref[...] = jnp.zeros_like(dg_ref)
            db_ref[...] = jnp.zeros_like(db_ref)

        rr = r_ref[...]
        xc = rr - jnp.mean(rr, -1, keepdims=True)
        rstd = lax.rsqrt(jnp.mean(xc * xc, -1, keepdims=True) + LN_EPS)
        xhat = xc * rstd
        dxn_v = dxn_ref[...]
        dxh = dxn_v * g_ref[...]
        dr = rstd * (dxh - jnp.mean(dxh, -1, keepdims=True) - xhat * jnp.mean(dxh * xhat, -1, keepdims=True))
        dr_ref[...] = dr
        dg_ref[...] += jnp.sum(dxn_v * xhat, axis=0, keepdims=True)
        db_ref[...] += jnp.sum(dxn_v, axis=0, keepdims=True)
        drb = dr.astype(BF16)
        dm_ref[...] = _dot_nt(drb, w_ref[...])
        dw_ref[...] += _dot_tn(mx_ref[...], drb)

    row = pl.BlockSpec((tm, D_MODEL), lambda i: (i, 0))
    full = lambda a, b: pl.BlockSpec((a, b), lambda i: (0, 0))
    big = jax.ShapeDtypeStruct((t, D_MODEL), F32)
    vec = jax.ShapeDtypeStruct((1, D_MODEL), F32)
    return pl.pallas_call(
        body, name=name, grid=(t // tm,),
        in_specs=[row, row, row, full(1, D_MODEL), full(D_MODEL, D_MODEL)],
        out_specs=[row, row, full(D_MODEL, D_MODEL), full(1, D_MODEL), full(1, D_MODEL)],
        out_shape=[big, big, jax.ShapeDtypeStruct((D_MODEL, D_MODEL), F32), vec, vec],
        compiler_params=_cp("arbitrary"))(dxn, r, mixed, ln_g, w_out)


def _dn_post_bwd(dm, oa, h, norm_w, *, tm, name):
    t = oa.shape[0]

    def body(dy_ref, o_ref, za_ref, nw_ref, do_ref, dza_ref, dnw_ref):
        @pl.when(pl.program_id(0) == 0)
        def _():
            dnw_ref[...] = jnp.zeros_like(dnw_ref)

        nw = nw_ref[...]
        dnw = jnp.zeros_like(nw)
        for hd in range(A_HEADS):
            sl = slice(hd * LANE, (hd + 1) * LANE)
            oh, za, dy = o_ref[:, sl], za_ref[:, sl], dy_ref[:, sl]
            rs = lax.rsqrt(jnp.mean(oh * oh, -1, keepdims=True) + RMS_EPS)
            nrm = oh * rs
            dza_ref[:, sl] = dy * nrm * nw * _dsilu(za)
            dn = dy * _silu(za)
            dnw = dnw + jnp.sum(dn * nrm, axis=0, keepdims=True)
            dnn = dn * nw
            do_ref[:, sl] = rs * dnn - oh * (rs * rs * rs) * jnp.mean(dnn * oh, -1, keepdims=True)
        dnw_ref[...] += dnw

    row = lambda c: pl.BlockSpec((tm, A_WIDTH), lambda i: (i, c))
    wide = jax.ShapeDtypeStruct((t, A_WIDTH), F32)
    return pl.pallas_call(
        body, name=name, grid=(t // tm,),
        in_specs=[row(0), row(0), row(C_ZA // A_WIDTH), pl.BlockSpec((1, LANE), lambda i: (0, 0))],
        out_specs=[row(0), row(0), pl.BlockSpec((1, LANE), lambda i: (0, 0))],
        out_shape=[wide, wide, jax.ShapeDtypeStruct((1, LANE), F32)],
        compiler_params=_cp("arbitrary"))(dm, oa, h, norm_w)


def _dn_scan_bwd(q, k, v, bg, tmat, s_all, do, *, name):
    t = q.shape[0]
    n = t // CHUNK

    def body(q_ref, k_ref, v_ref, bg_ref, tm_ref, s_ref, do_ref, dq_ref, dk_ref, dv_ref, dbg_ref, dstate):
        @pl.when(pl.program_id(0) == 0)
        def _():
            dstate[...] = jnp.zeros_like(dstate)

        causal, strict, _, upper = _chunk_masks()
        bg_v = bg_ref[...]
        gc_all, rows = _gc_forms(bg_v, causal, upper)
        lane = lax.broadcasted_iota(jnp.int32, (CHUNK, LANE), 1)
        rowi = lax.broadcasted_iota(jnp.int32, (CHUNK, 1), 0)
        ones = jnp.ones((CHUNK, LANE), F32)
        acc = jnp.zeros((CHUNK, LANE), F32)
        for hd in range(A_HEADS):
            sl = slice(hd * LANE, (hd + 1) * LANE)
            qh, kh, vh, doh = q_ref[:, sl], k_ref[:, sl], v_ref[:, sl], do_ref[:, sl]
            tmh = tm_ref[:, hd * CHUNK:(hd + 1) * CHUNK]
            s_old = s_ref[0, hd]
            ds_out = dstate[hd]
            beta = bg_v[:, hd:hd + 1]
            gcol = gc_all[:, A_HEADS + hd:A_HEADS + hd + 1]
            dmat = jnp.exp(jnp.where(causal, gcol - rows[hd], NEG))
            e = jnp.exp(gcol)
            glast = gcol[CHUNK - 1:CHUNK, :]
            elast = jnp.exp(glast)
            ek = jnp.exp(glast - gcol)
            kb, vb = kh * beta, vh * beta
            kbe = kb * e
            a = jnp.where(strict, _dot_nt(kb, kh) * dmat, 0.0)
            p = jnp.where(causal, _dot_nt(qh, kh) * dmat, 0.0)
            u = _dot(tmh, vb)
            w = _dot(tmh, kbe)
            vn = u - _dot(w, s_old)
            qd, kd = qh * e, kh * ek

            dvn = _dot_tn(p, doh) + _dot(kd, ds_out)
            dp = jnp.where(causal, _dot_nt(doh, vn), 0.0)
            dqd = _dot_nt(doh, s_old)
            dstate[hd] = _dot_tn(qd, doh) + elast * ds_out - _dot_tn(w, dvn)
            dgt = jnp.sum(s_old * ds_out, keepdims=True)
            dkd = _dot_nt(vn, ds_out)
            dw = -_dot_nt(dvn, s_old)
            dt = _dot_nt(dvn, vb) + _dot_nt(dw, kbe)
            dvb = _dot_tn(tmh, dvn)
            dkbe = _dot_tn(tmh, dw)
            da = jnp.where(strict, -_dot_tn(tmh, _dot_nt(dt, tmh)), 0.0)
            dma, dmp = da * dmat, dp * dmat
            dkb = _dot(dma, kh) + dkbe * e
            dk = _dot_tn(dma, kb) + _dot_tn(dmp, qh) + dkd * ek + dkb * beta
            dq_ref[:, sl] = _dot(dmp, kh) + dqd * e
            dk_ref[:, sl] = dk
            dv_ref[:, sl] = dvb * beta
            gmat = da * a + dp * p
            colsum = lax.dot_general(gmat, ones, (((0,), (0,)), ((), ())), precision=HI,
                                     preferred_element_type=F32)[:, 0:1]
            rk = jnp.sum(dkd * kh, -1, keepdims=True) * ek
            de = jnp.sum(dqd * qh, -1, keepdims=True) + jnp.sum(dkbe * kb, -1, keepdims=True)
            dglast = jnp.sum(rk, keepdims=True) + dgt * elast
            dgc = (jnp.sum(gmat, -1, keepdims=True) - colsum + de * e - rk
                   + jnp.where(rowi == CHUNK - 1, dglast, 0.0))
            dbeta = jnp.sum(dkb * kh, -1, keepdims=True) + jnp.sum(dvb * vh, -1, keepdims=True)
            acc = acc + jnp.where(lane == hd, dbeta, 0.0) + jnp.where(lane == A_HEADS + hd, dgc, 0.0)
        rc = _dot_hi(upper.astype(F32), acc)
        dbg_ref[...] = jnp.where((lane >= A_HEADS) & (lane < 2 * A_HEADS), rc, acc)

    blk = pl.BlockSpec((CHUNK, A_WIDTH), lambda i: (n - 1 - i, 0))
    wide = jax.ShapeDtypeStruct((t, A_WIDTH), F32)
    return pl.pallas_call(
        body, name=name, grid=(n,),
        in_specs=[blk, blk, blk, pl.BlockSpec((CHUNK, LANE), lambda i: (n - 1 - i, 0)),
                  pl.BlockSpec((CHUNK, A_HEADS * CHUNK), lambda i: (n - 1 - i, 0)),
                  pl.BlockSpec((1, A_HEADS, LANE, LANE), lambda i: (n - 1 - i, 0, 0, 0)), blk],
        out_specs=[blk, blk, blk, pl.BlockSpec((CHUNK, LANE), lambda i: (n - 1 - i, 0))],
        out_shape=[wide, wide, wide, jax.ShapeDtypeStruct((t, LANE), F32)],
        scratch_shapes=[pltpu.VMEM((A_HEADS, LANE, LANE), F32)],
        compiler_params=_cp("arbitrary"))(q, k, v, bg, tmat, s_all, do)


def _dn_pre_bwd(h, conv_w, par, dq, dk, dv, dbg, *, tt, name):
    t = h.shape[0]
    cw = 3 * A_WIDTH
    hb = tt // SUBLANE

    def body(pre_ref, halo_ref, bgi_ref, cw_ref, par_ref, dq_ref, dk_ref, dv_ref, dbg_ref, dc_ref, dbgi_ref, dpar_ref):
        i = pl.program_id(0)

        @pl.when(i == 0)
        def _():
            dpar_ref[...] = jnp.zeros_like(dpar_ref)

        cur = pre_ref[...]
        before = jnp.where(i > 0, halo_ref[...], 0.0)
        c = _conv_fwd(cur, before, cw_ref[...])
        s = _silu(c)
        ds = _dsilu(c)
        for hd in range(A_HEADS):
            sl = slice(hd * LANE, (hd + 1) * LANE)
            for base, d_ref, scale in ((0, dq_ref, A_HEAD_DIM ** -0.5), (A_WIDTH, dk_ref, 1.0)):
                csl = slice(base + hd * LANE, base + (hd + 1) * LANE)
                tq = s[:, base + hd * LANE:base + (hd + 1) * LANE]
                dy = d_ref[:, sl]
                rq = lax.rsqrt(jnp.sum(tq * tq, -1, keepdims=True) + L2_EPS)
                dtq = scale * (rq * dy - tq * (rq * rq * rq) * jnp.sum(dy * tq, -1, keepdims=True))
                dc_ref[:, csl] = dtq * ds[:, base + hd * LANE:base + (hd + 1) * LANE]
        dc_ref[:, 2 * A_WIDTH:] = dv_ref[...] * ds[:, 2 * A_WIDTH:]
        raw = bgi_ref[...]
        dbg_v = dbg_ref[...]
        lane = lax.broadcasted_iota(jnp.int32, raw.shape, 1)
        is_b = lane < A_HEADS
        is_a = (lane >= A_HEADS) & (lane < 2 * A_HEADS)
        beta = _sigmoid(raw)
        z = raw + par_ref[1:2, :]
        neg_ea = -jnp.exp(par_ref[0:1, :])
        g = neg_ea * _softplus(z)
        da = dbg_v * neg_ea * _sigmoid(z)
        dbgi_ref[...] = jnp.where(is_b, dbg_v * beta * (1.0 - beta), jnp.where(is_a, da, 0.0))
        dpar_ref[0:1, :] += jnp.sum(jnp.where(is_a, dbg_v * g, 0.0), axis=0, keepdims=True)
        dpar_ref[1:2, :] += jnp.sum(jnp.where(is_a, da, 0.0), axis=0, keepdims=True)

    wide = pl.BlockSpec((tt, A_WIDTH), lambda i: (i, 0))
    return pl.pallas_call(
        body, name=name, grid=(t // tt,),
        in_specs=[pl.BlockSpec((tt, cw), lambda i: (i, 0)),
                  pl.BlockSpec((SUBLANE, cw), lambda i: (jnp.maximum(i * hb - 1, 0), 0)),
                  pl.BlockSpec((tt, LANE), lambda i: (i, C_BG // LANE)),
                  pl.BlockSpec((CONV_K, cw), lambda i: (0, 0)),
                  pl.BlockSpec((SUBLANE, LANE), lambda i: (0, 0)),
                  wide, wide, wide, pl.BlockSpec((tt, LANE), lambda i: (i, 0))],
        out_specs=[pl.BlockSpec((tt, cw), lambda i: (i, 0)), pl.BlockSpec((tt, LANE), lambda i: (i, 0)),
                   pl.BlockSpec((SUBLANE, LANE), lambda i: (0, 0))],
        out_shape=[jax.ShapeDtypeStruct((t, cw), F32), jax.ShapeDtypeStruct((t, LANE), F32),
                   jax.ShapeDtypeStruct((SUBLANE, LANE), F32)],
        compiler_params=_cp("arbitrary"))(h, h, h, conv_w, par, dq, dk, dv, dbg)


def _conv_bwd(dc, h, conv_w, *, tt, name):
    t = dc.shape[0]
    cw = 3 * A_WIDTH
    hb = tt // SUBLANE
    nb = t // tt

    def body(dc_ref, after_ref, pre_ref, before_ref, cw_ref, dpre_ref, dcw_ref):
        i = pl.program_id(0)

        @pl.when(i == 0)
        def _():
            dcw_ref[...] = jnp.zeros_like(dcw_ref)

        dcv = dc_ref[...]
        after = jnp.where(i < nb - 1, after_ref[...], 0.0)
        cur = pre_ref[...]
        before = jnp.where(i > 0, before_ref[...], 0.0)
        w = cw_ref[...]
        acc = dcv * w[CONV_K - 1:CONV_K, :]
        dcw_ref[CONV_K - 1:CONV_K, :] += jnp.sum(dcv * cur, axis=0, keepdims=True)
        for s in range(1, CONV_K):
            j = CONV_K - 1 - s
            acc = acc + _shift_up(dcv, after, s) * w[j:j + 1, :]
            dcw_ref[j:j + 1, :] += jnp.sum(dcv * _shift_down(cur, before, s), axis=0, keepdims=True)
        dpre_ref[...] = acc

    return pl.pallas_call(
        body, name=name, grid=(nb,),
        in_specs=[pl.BlockSpec((tt, cw), lambda i: (i, 0)),
                  pl.BlockSpec((SUBLANE, cw), lambda i: (jnp.minimum((i + 1) * hb, t // SUBLANE - 1), 0)),
                  pl.BlockSpec((tt, cw), lambda i: (i, 0)),
                  pl.BlockSpec((SUBLANE, cw), lambda i: (jnp.maximum(i * hb - 1, 0), 0)),
                  pl.BlockSpec((CONV_K, cw), lambda i: (0, 0))],
        out_specs=[pl.BlockSpec((tt, cw), lambda i: (i, 0)), pl.BlockSpec((SUBLANE, cw), lambda i: (0, 0))],
        out_shape=[jax.ShapeDtypeStruct((t, cw), F32), jax.ShapeDtypeStruct((SUBLANE, cw), F32)],
        compiler_params=_cp("arbitrary"))(dc, dc, h, h, conv_w)


def _swa_bwd(h, dm, sinks_b, *, name):
    t = h.shape[0]
    qspec, cur, prev = _swa_specs()

    def body(q_ref, kc_ref, kp_ref, vc_ref, vp_ref, zb_ref, dy_ref, sk_ref, dq_ref, dzb_ref, dk_ref, dv_ref, dsk_ref):
        n_blk = pl.program_id(0)

        @pl.when(n_blk == 0)
        def _():
            dk_ref[...] = jnp.zeros_like(dk_ref)
            dv_ref[...] = jnp.zeros_like(dv_ref)
            dsk_ref[...] = jnp.zeros_like(dsk_ref)

        kband = jnp.concatenate([kp_ref[...], kc_ref[...]], axis=0)
        vband = jnp.concatenate([vp_ref[...], vc_ref[...]], axis=0)
        scale = B_HEAD_DIM ** -0.5
        dk_acc = [jnp.zeros((2 * BLOCK, B_HEAD_DIM), F32) for _ in range(B_KV_HEADS)]
        dv_acc = [jnp.zeros((2 * BLOCK, B_HEAD_DIM), F32) for _ in range(B_KV_HEADS)]
        for hq in range(B_Q_HEADS):
            hk = hq // B_GROUP
            ksl = slice(hk * B_HEAD_DIM, (hk + 1) * B_HEAD_DIM)
            qsl = slice(hq * B_HEAD_DIM, (hq + 1) * B_HEAD_DIM)
            qh = q_ref[:, qsl]
            p, ps = _swa_probs(qh, kband[:, ksl], sk_ref[hq:hq + 1, 0:1], ALIBI[hq], n_blk)
            o = _dot(p, vband[:, ksl])
            zb, dy = zb_ref[:, qsl], dy_ref[:, qsl]
            dzb_ref[:, qsl] = dy * o * _dsilu(zb)
            do = dy * _silu(zb)
            dp = _dot_nt(do, vband[:, ksl])
            delta = jnp.sum(do * o, -1, keepdims=True)
            ds = p * (dp - delta)
            dq_ref[:, qsl] = _dot(ds, kband[:, ksl]) * scale
            dk_acc[hk] = dk_acc[hk] + _dot_tn(ds, qh) * scale
            dv_acc[hk] = dv_acc[hk] + _dot_tn(p, do)
            dsk_ref[hq:hq + 1, :] += -jnp.sum(ps * delta, keepdims=True)
        dkb = jnp.concatenate(dk_acc, axis=1)
        dvb = jnp.concatenate(dv_acc, axis=1)
        at_cur = pl.ds(pl.multiple_of(n_blk * BLOCK, BLOCK), BLOCK)
        at_prev = pl.ds(pl.multiple_of(jnp.maximum(n_blk - 1, 0) * BLOCK, BLOCK), BLOCK)
        dk_ref[at_prev, :] += dkb[:BLOCK]
        dv_ref[at_prev, :] += dvb[:BLOCK]
        dk_ref[at_cur, :] += dkb[BLOCK:]
        dv_ref[at_cur, :] += dvb[BLOCK:]

    wide = jax.ShapeDtypeStruct((t, B_WIDTH), F32)
    narrow = jax.ShapeDtypeStruct((t, B_KV_WIDTH), F32)
    res = lambda a, b: pl.BlockSpec((a, b), lambda i: (0, 0))
    return pl.pallas_call(
        body, name=name, grid=(t // BLOCK,),
        in_specs=[qspec(C_QB), cur(C_KB), prev(C_KB), cur(C_VB), prev(C_VB), qspec(C_ZB),
                  pl.BlockSpec((BLOCK, B_WIDTH), lambda i: (i, 1)), res(B_Q_HEADS, LANE)],
        out_specs=[pl.BlockSpec((BLOCK, B_WIDTH), lambda i: (i, 0))] * 2
        + [res(t, B_KV_WIDTH), res(t, B_KV_WIDTH), res(B_Q_HEADS, LANE)],
        out_shape=[wide, wide, narrow, narrow, jax.ShapeDtypeStruct((B_Q_HEADS, LANE), F32)],
        compiler_params=_cp("arbitrary"))(h, h, h, h, h, h, dm, sinks_b)


def _matmul_tn(a, b, *, tk, tn, name):
    t, m = a.shape
    n = b.shape[1]

    def body(a_ref, b_ref, o_ref):
        @pl.when(pl.program_id(1) == 0)
        def _():
            o_ref[...] = jnp.zeros_like(o_ref)

        o_ref[...] += _dot_tn(a_ref[...], b_ref[...])

    return pl.pallas_call(
        body, name=name, grid=(n // tn, t // tk),
        in_specs=[pl.BlockSpec((tk, m), lambda j, kk: (kk, 0)), pl.BlockSpec((tk, tn), lambda j, kk: (kk, j))],
        out_specs=pl.BlockSpec((m, tn), lambda j, kk: (0, j)),
        out_shape=jax.ShapeDtypeStruct((m, n), F32),
        compiler_params=_cp("parallel", "arbitrary"))(a, b)


def _matmul_nt_add(a, b, dr, *, tm, name):
    t, n = a.shape
    m = b.shape[0]

    def body(a_ref, b_ref, r_ref, o_ref):
        o_ref[...] = _dot_nt(a_ref[...], b_ref[...]) + DEEPNORM_ALPHA * r_ref[...]

    return pl.pallas_call(
        body, name=name, grid=(t // tm,),
        in_specs=[pl.BlockSpec((tm, n), lambda i: (i, 0)), pl.BlockSpec((m, n), lambda i: (0, 0)),
                  pl.BlockSpec((tm, m), lambda i: (i, 0))],
        out_specs=pl.BlockSpec((tm, m), lambda i: (i, 0)),
        out_shape=jax.ShapeDtypeStruct((t, m), F32),
        compiler_params=_cp("parallel"))(a, b, dr)


def _layer_bwd(dxn, res, wp, conv_w, par, sinks_b, norm_w, w_out_bf, ln_g, l):
    dr, dm, dw_out, dln_g, dln_b = _ln_out_bwd(dxn, res["r"], res["mixed"], ln_g, w_out_bf, tm=256, name=f"ln_out_bwd_{l}")
    h = res["h"]
    do, dza, dnw = _dn_post_bwd(dm, res["oa"], h, norm_w, tm=512, name=f"dn_post_bwd_{l}")
    dq, dk, dv, dbg = _dn_scan_bwd(res["q"], res["k"], res["v"], res["bg"], res["tmat"], res["s_all"], do,
                                   name=f"dn_scan_bwd_{l}")
    dc, dbgi, dpar = _dn_pre_bwd(h, conv_w, par, dq, dk, dv, dbg, tt=512, name=f"dn_pre_bwd_{l}")
    dpre, dcw = _conv_bwd(dc, h, conv_w, tt=512, name=f"conv_bwd_{l}")
    dqb, dzb, dkb, dvb, dsk = _swa_bwd(h, dm, sinks_b, name=f"swa_bwd_{l}")
    dh = jnp.concatenate([dpre, dza, dqb, dzb, dkb, dvb, dbgi], axis=1)
    dwp = _matmul_tn(res["x"], dh, tk=512, tn=1152, name=f"in_proj_dw_{l}")
    dx = _matmul_nt_add(dh, wp, dr, tm=256, name=f"in_proj_dx_{l}")
    grads = dict(w_in=dwp, conv_w=dcw[:CONV_K], a_log=dpar[0, A_HEADS:2 * A_HEADS], dt_bias=dpar[1, A_HEADS:2 * A_HEADS],
                 norm_w=dnw[0], sinks=dsk[:, 0], w_out=dw_out, ln_g=dln_g[0], ln_b=dln_b[0])
    return dx, grads


def _local_step(x, target, wp, conv_w, a_log, dt_bias, norm_w, sinks, w_out_bf, ln_g, ln_b):
    per_layer = []
    saved = []
    for l in range(DEPTH):
        args = (wp[l], conv_w[l], _gate_params(a_log[l], dt_bias[l]),
                jnp.broadcast_to(sinks[l][:, None], (B_Q_HEADS, LANE)), norm_w[l][None], w_out_bf[l])
        per_layer.append(args)
        x, res = _layer_fwd(x, *args, ln_g[l][None], ln_b[l][None], l)
        saved.append(res)
    dx, loss_tile = _loss_grad(x, target, tm=512, name="loss_grad")
    grads = [None] * DEPTH
    for l in reversed(range(DEPTH)):
        dx, grads[l] = _layer_bwd(dx, saved[l], *per_layer[l], ln_g[l][None], l)
    return loss_tile, dx, grads


_ANY = pl.BlockSpec(memory_space=pl.ANY)
_MESH = pl.DeviceIdType.MESH


def _pair_exchange(arrays, *, name):
    n = len(arrays)

    def body(*refs):
        src, dst, (send_sems, recv_sems) = refs[:n], refs[n:2 * n], refs[2 * n:]
        sibling = (lax.axis_index("x"), lax.axis_index("y"), 1 - lax.axis_index("c"))
        copies = [pltpu.make_async_remote_copy(src_ref=src[k], dst_ref=dst[k], send_sem=send_sems.at[k],
                                               recv_sem=recv_sems.at[k], device_id=sibling, device_id_type=_MESH)
                  for k in range(n)]
        for cp in copies:
            cp.start()
        for cp in copies:
            cp.wait()

    return pl.pallas_call(
        body, name=name, in_specs=[_ANY] * n, out_specs=[_ANY] * n,
        out_shape=[jax.ShapeDtypeStruct(a.shape, a.dtype) for a in arrays],
        scratch_shapes=[pltpu.SemaphoreType.DMA((n,)), pltpu.SemaphoreType.DMA((n,))])(*arrays)


def _chip_exchange(arrays, scatter, *, name):
    n = len(arrays)
    shapes = [a.shape if sc else (N_SHARD,) + a.shape for a, sc in zip(arrays, scatter)]

    def body(*refs):
        src, dst = refs[:n], refs[n:2 * n]
        send_sems, recv_sems, local_sems = refs[2 * n:]
        x, y, c = lax.axis_index("x"), lax.axis_index("y"), lax.axis_index("c")
        me = 2 * x + y
        chips = [(1 - x, y), (x, 1 - y), (1 - x, 1 - y)]
        local = [pltpu.make_async_copy(src[k].at[me] if scatter[k] else src[k], dst[k].at[me], local_sems.at[k])
                 for k in range(n)]
        for cp in local:
            cp.start()
        sends = []
        for k in range(n):
            for j, (px, py) in enumerate(chips):
                sends.append(pltpu.make_async_remote_copy(
                    src_ref=src[k].at[2 * px + py] if scatter[k] else src[k], dst_ref=dst[k].at[me],
                    send_sem=send_sems.at[3 * k + j], recv_sem=recv_sems.at[3 * k + j],
                    device_id=(px, py, c), device_id_type=_MESH))
        for cp in sends:
            cp.start()
        for k in range(n):
            for j, (px, py) in enumerate(chips):
                pltpu.make_async_remote_copy(
                    src_ref=src[k].at[2 * px + py] if scatter[k] else src[k], dst_ref=dst[k].at[2 * px + py],
                    send_sem=send_sems.at[3 * k + j], recv_sem=recv_sems.at[3 * k + j],
                    device_id=(px, py, c), device_id_type=_MESH).wait_recv()
        for cp in sends:
            cp.wait_send()
        for cp in local:
            cp.wait()

    return pl.pallas_call(
        body, name=name, in_specs=[_ANY] * n, out_specs=[_ANY] * n,
        out_shape=[jax.ShapeDtypeStruct(s, a.dtype) for s, a in zip(shapes, arrays)],
        scratch_shapes=[pltpu.SemaphoreType.DMA((3 * n,)), pltpu.SemaphoreType.DMA((3 * n,)),
                        pltpu.SemaphoreType.DMA((n,))])(*arrays)


def _rows_view(a):
    return a.reshape((-1, a.shape[-1]))


def _tile_rows(rows):
    for tm in (512, 256, 128, 64, 32, 16, 8):
        if rows % tm == 0:
            return tm
    return rows


def _add2(a, b, *, name):
    a2, b2 = _rows_view(a), _rows_view(b)
    rows, cols = a2.shape
    tm = _tile_rows(rows)

    def body(a_ref, b_ref, o_ref):
        o_ref[...] = a_ref[...] + b_ref[...]

    blk = pl.BlockSpec((tm, cols), lambda i: (i, 0))
    out = pl.pallas_call(body, name=name, grid=(rows // tm,), in_specs=[blk, blk], out_specs=blk,
                         out_shape=jax.ShapeDtypeStruct(a2.shape, a2.dtype), compiler_params=_cp("parallel"))(a2, b2)
    return out.reshape(a.shape)


def _sum4(a, *, name):
    a3 = a.reshape((N_SHARD, -1, a.shape[-1]))
    _, rows, cols = a3.shape
    tm = _tile_rows(rows)

    def body(a_ref, o_ref):
        o_ref[...] = ((a_ref[0] + a_ref[1]) + a_ref[2]) + a_ref[3]

    out = pl.pallas_call(body, name=name, grid=(rows // tm,),
                         in_specs=[pl.BlockSpec((N_SHARD, tm, cols), lambda i: (0, i, 0))],
                         out_specs=pl.BlockSpec((tm, cols), lambda i: (i, 0)),
                         out_shape=jax.ShapeDtypeStruct((rows, cols), a.dtype), compiler_params=_cp("parallel"))(a3)
    return out.reshape(a.shape[1:])


def _adamw(w, g, m, v, *, name):
    w2, g2, m2, v2 = (_rows_view(t) for t in (w, g, m, v))
    rows, cols = w2.shape
    tm = _tile_rows(rows)

    def body(w_ref, g_ref, m_ref, v_ref, d_ref, mo_ref, vo_ref):
        gv = g_ref[...]
        mn = ADAM_B1 * m_ref[...] + (1.0 - ADAM_B1) * gv
        vn = ADAM_B2 * v_ref[...] + (1.0 - ADAM_B2) * (gv * gv)
        mo_ref[...] = mn
        vo_ref[...] = vn
        m_hat = mn / (1.0 - ADAM_B1 ** ADAM_STEP)
        v_hat = vn / (1.0 - ADAM_B2 ** ADAM_STEP)
        d_ref[...] = -ADAM_LR * (m_hat / (jnp.sqrt(v_hat) + ADAM_EPS) + ADAM_WD * w_ref[...])

    blk = pl.BlockSpec((tm, cols), lambda i: (i, 0))
    shp = jax.ShapeDtypeStruct(w2.shape, F32)
    outs = pl.pallas_call(body, name=name, grid=(rows // tm,), in_specs=[blk] * 4, out_specs=[blk] * 3,
                          out_shape=[shp] * 3, compiler_params=_cp("parallel"))(w2, g2, m2, v2)
    return [o.reshape(w.shape) for o in outs]


def _to_kernel_cols(w):
    pad = jnp.zeros(w.shape[:-1] + (LANE - 2 * A_HEADS,), w.dtype)
    return jnp.concatenate([w[..., 0:2048], w[..., 2056:2568], w[..., 2824:3336], w[..., 2568:2696],
                            w[..., 2696:2824], w[..., 2048:2056], pad], axis=-1)


def _from_kernel_cols(w):
    return jnp.concatenate([w[..., 0:2048], w[..., C_BG:C_BG + 2 * A_HEADS], w[..., C_QB:C_QB + B_WIDTH],
                            w[..., C_KB:C_KB + B_KV_WIDTH], w[..., C_VB:C_VB + B_KV_WIDTH],
                            w[..., C_ZB:C_ZB + B_WIDTH]], axis=-1)


def _gate_params(a_log, dt_bias):
    par = jnp.zeros((SUBLANE, LANE), F32)
    par = par.at[0, A_HEADS:2 * A_HEADS].set(a_log)
    return par.at[1, A_HEADS:2 * A_HEADS].set(dt_bias)


SMALL = ("conv_w", "a_log", "dt_bias", "norm_w", "sinks", "ln_g", "ln_b")


def _pack(parts, cols):
    flat = jnp.concatenate([p.reshape(-1) for p in parts])
    rows = -(-flat.shape[0] // cols)
    return jnp.pad(flat, (0, rows * cols - flat.shape[0])).reshape(rows, cols)


def _unpack(packed, shapes):
    flat = packed.reshape(-1)
    out, at = [], 0
    for s in shapes:
        n = math.prod(s)
        out.append(flat[at:at + n].reshape(s))
        at += n
    return out


def kernel(x, w_in, conv_w, a_log, dt_bias, norm_w, sinks, w_out, ln_g, ln_b, loss_target, m_w_in, m_conv_w, m_a_log, m_dt_bias, m_norm_w, m_sinks, m_w_out, m_ln_g, m_ln_b, v_w_in, v_conv_w, v_a_log, v_dt_bias, v_norm_w, v_sinks, v_w_out, v_ln_g, v_ln_b):
    xi, yi, ci = lax.axis_index("x"), lax.axis_index("y"), lax.axis_index("c")
    me = 2 * xi + yi

    g_in, g_out, g_conv = _chip_exchange([w_in.astype(BF16), w_out.astype(BF16), conv_w], [False] * 3,
                                         name="gather_weights")
    wp = _to_kernel_cols(jnp.moveaxis(g_in, 0, 2).reshape(DEPTH, D_MODEL, IN_COLS))
    w_out_full = jnp.moveaxis(g_out, 0, 1).reshape(DEPTH, D_MODEL, D_MODEL)
    conv_full = jnp.moveaxis(g_conv, 0, 2).reshape(DEPTH, CONV_K, 3 * A_WIDTH)

    loss_tile, dx, grads = _local_step(x[0], loss_target[0], wp, conv_full, a_log, dt_bias, norm_w, sinks,
                                       w_out_full, ln_g, ln_b)
    loss = lax.psum(loss_tile[0, 0], ("x", "y", "c"))

    gin = jnp.stack([_from_kernel_cols(g["w_in"]) for g in grads])
    gin = jnp.moveaxis(gin.reshape(DEPTH, D_MODEL, N_SHARD, IN_SHARD), 2, 1)
    gout = jnp.stack([g["w_out"] for g in grads]).reshape(DEPTH, N_SHARD, OUT_SHARD, D_MODEL)
    small_shapes = [(DEPTH,) + grads[0][nm].shape for nm in SMALL]
    gsmall = _pack([jnp.stack([g[nm] for g in grads]) for nm in SMALL], D_MODEL)

    pick = lambda a, i: lax.dynamic_index_in_dim(a, i, 0, keepdims=False)
    r_in, r_out, r_small = _pair_exchange([pick(gin, 1 - ci), pick(gout, 1 - ci), gsmall], name="pair_reduce")
    p_in = _add2(pick(gin, ci), r_in, name="pair_add_in")
    p_out = _add2(pick(gout, ci), r_out, name="pair_add_out")
    p_small = _add2(gsmall, r_small, name="pair_add_small")
    q_in, q_out, q_small = _chip_exchange([p_in, p_out, p_small], [True, True, False], name="chip_reduce")
    s_in = _sum4(q_in, name="chip_sum_in")
    s_out = _sum4(q_out, name="chip_sum_out")
    s_small = _sum4(q_small, name="chip_sum_small")
    o_in, o_out = _pair_exchange([s_in, s_out], name="pair_share")
    by_layer = lambda mine, other: jnp.where(ci == 0, jnp.stack([mine, other]), jnp.stack([other, mine]))
    grad_in = by_layer(s_in, o_in)
    grad_out = by_layer(s_out, o_out)
    gs = dict(zip(SMALL, _unpack(s_small, small_shapes)))
    gs["conv_w"] = lax.dynamic_slice_in_dim(gs["conv_w"], me * CONV_SHARD, CONV_SHARD, axis=2)

    d_in, nm_in, nv_in = _adamw(w_in, grad_in, m_w_in, v_w_in, name="adamw_in")
    d_out, nm_out, nv_out = _adamw(w_out, grad_out, m_w_out, v_w_out, name="adamw_out")
    ws = dict(conv_w=conv_w, a_log=a_log, dt_bias=dt_bias, norm_w=norm_w, sinks=sinks, ln_g=ln_g, ln_b=ln_b)
    ms = dict(conv_w=m_conv_w, a_log=m_a_log, dt_bias=m_dt_bias, norm_w=m_norm_w, sinks=m_sinks, ln_g=m_ln_g, ln_b=m_ln_b)
    vs = dict(conv_w=v_conv_w, a_log=v_a_log, dt_bias=v_dt_bias, norm_w=v_norm_w, sinks=v_sinks, ln_g=v_ln_g, ln_b=v_ln_b)
    shard_shapes = [ws[nm].shape for nm in SMALL]
    packed = [_pack([d[nm] for nm in SMALL], LANE) for d in (ws, gs, ms, vs)]
    d_s, nm_s, nv_s = (dict(zip(SMALL, _unpack(o, shard_shapes))) for o in _adamw(*packed, name="adamw_small"))

    def in_order(big_in, small, big_out):
        return (big_in, small["conv_w"], small["a_log"], small["dt_bias"], small["norm_w"], small["sinks"], big_out,
                small["ln_g"], small["ln_b"])

    return (loss, dx[None], *in_order(grad_in, gs, grad_out), *in_order(d_in, d_s, d_out),
            *in_order(nm_in, nm_s, nm_out), *in_order(nv_in, nv_s, nv_out))
```

```python
import functools
import math

import jax
import jax.numpy as jnp
from jax import lax
from jax.experimental import pallas as pl
from jax.experimental.pallas import tpu as pltpu

F32 = jnp.float32
BF16 = jnp.bfloat16
HI = lax.Precision.HIGHEST

D_MODEL = 1024
DEPTH = 2
A_HEADS = 4
A_HEAD_DIM = 128
A_WIDTH = 512
CONV_K = 4
CHUNK = 64
B_Q_HEADS = 8
B_KV_HEADS = 2
B_HEAD_DIM = 64
B_GROUP = 4
B_WIDTH = 512
B_KV_WIDTH = 128
BLOCK = 128
IN_COLS = 3336
DEEPNORM_ALPHA = (2 * DEPTH) ** 0.25
LN_EPS = 1e-5
RMS_EPS = 1e-6
L2_EPS = 1e-6
ADAM_LR = 0.001
ADAM_B1 = 0.9
ADAM_B2 = 0.999
ADAM_EPS = 1e-08
ADAM_WD = 0.01
ADAM_STEP = 10

N_SHARD = 4
IN_SHARD = IN_COLS // N_SHARD
OUT_SHARD = D_MODEL // N_SHARD
CONV_SHARD = 3 * A_WIDTH // N_SHARD

P_COLS = 3456
C_PRE = 0
C_ZA = 1536
C_QB = 2048
C_ZB = 2560
C_KB = 3072
C_VB = 3200
C_BG = 3328
LANE = 128
SUBLANE = 8
VMEM_LIMIT = 56 * 1024 * 1024
ALIBI = tuple(2.0 ** (-8.0 * (h + 1) / B_Q_HEADS) for h in range(B_Q_HEADS))
NEG = -1e30


def _cp(*sem):
    return pltpu.CompilerParams(dimension_semantics=sem, vmem_limit_bytes=VMEM_LIMIT)


def _dot(a, b):
    return jnp.dot(a.astype(BF16), b.astype(BF16), preferred_element_type=F32)


def _dot_nt(a, b):
    return lax.dot_general(a.astype(BF16), b.astype(BF16), (((1,), (1,)), ((), ())),
                           preferred_element_type=F32)


def _dot_tn(a, b):
    return lax.dot_general(a.astype(BF16), b.astype(BF16), (((0,), (0,)), ((), ())),
                           preferred_element_type=F32)


def _dot_hi(a, b):
    return jnp.dot(a, b, precision=HI, preferred_element_type=F32)


def _sigmoid(x):
    return jax.nn.sigmoid(x)


def _silu(x):
    return x * _sigmoid(x)


def _dsilu(x):
    s = _sigmoid(x)
    return s * (1.0 + x * (1.0 - s))


def _softplus(x):
    return jnp.maximum(x, 0.0) + jnp.log(1.0 + jnp.exp(-jnp.abs(x)))


def _shift_down(cur, before, s):
    if s == 0:
        return cur
    r = pltpu.roll(cur, s, 0)
    rb = pltpu.roll(before, s, 0)
    row = lax.broadcasted_iota(jnp.int32, before.shape, 0)
    head = jnp.where(row < s, rb, r[0:SUBLANE])
    return jnp.concatenate([head, r[SUBLANE:]], axis=0)


def _shift_up(cur, after, s):
    if s == 0:
        return cur
    n = cur.shape[0]
    r = pltpu.roll(cur, n - s, 0)
    ra = pltpu.roll(after, SUBLANE - s, 0)
    row = lax.broadcasted_iota(jnp.int32, after.shape, 0)
    tail = jnp.where(row >= SUBLANE - s, ra, r[n - SUBLANE:])
    return jnp.concatenate([r[:n - SUBLANE], tail], axis=0)


def _conv_fwd(cur, before, w):
    acc = cur * w[CONV_K - 1:CONV_K, :]
    for s in range(1, CONV_K):
        acc = acc + _shift_down(cur, before, s) * w[CONV_K - 1 - s:CONV_K - s, :]
    return acc


def _matmul(a, b, *, tm, tn, name):
    m, k = a.shape
    n = b.shape[1]

    def body(a_ref, b_ref, o_ref):
        o_ref[...] = jnp.dot(a_ref[...].astype(BF16), b_ref[...], preferred_element_type=F32)

    return pl.pallas_call(
        body, name=name, grid=(m // tm, n // tn),
        in_specs=[pl.BlockSpec((tm, k), lambda i, j: (i, 0)), pl.BlockSpec((k, tn), lambda i, j: (0, j))],
        out_specs=pl.BlockSpec((tm, tn), lambda i, j: (i, j)),
        out_shape=jax.ShapeDtypeStruct((m, n), F32),
        compiler_params=_cp("parallel", "parallel"))(a, b)


def _dn_pre(h, conv_w, par, *, tt, name):
    t = h.shape[0]
    cw = 3 * A_WIDTH
    hb = tt // SUBLANE

    def body(pre_ref, halo_ref, bgi_ref, cw_ref, par_ref, q_ref, k_ref, v_ref, bg_ref, bgt_ref):
        i = pl.program_id(0)
        cur = pre_ref[...]
        before = jnp.where(i > 0, halo_ref[...], 0.0)
        s = _silu(_conv_fwd(cur, before, cw_ref[...]))
        for hd in range(A_HEADS):
            sl = slice(hd * LANE, (hd + 1) * LANE)
            tq = s[:, hd * LANE:(hd + 1) * LANE]
            q_ref[:, sl] = tq * (lax.rsqrt(jnp.sum(tq * tq, -1, keepdims=True) + L2_EPS) * (A_HEAD_DIM ** -0.5))
            tk = s[:, A_WIDTH + hd * LANE:A_WIDTH + (hd + 1) * LANE]
            k_ref[:, sl] = tk * lax.rsqrt(jnp.sum(tk * tk, -1, keepdims=True) + L2_EPS)
        v_ref[...] = s[:, 2 * A_WIDTH:]
        raw = bgi_ref[...]
        lane = lax.broadcasted_iota(jnp.int32, raw.shape, 1)
        is_a = (lane >= A_HEADS) & (lane < 2 * A_HEADS)
        g = jnp.where(is_a, -jnp.exp(par_ref[0:1, :]) * _softplus(raw + par_ref[1:2, :]), 0.0)
        gc = _dot_hi(_chunk_tri(tt, lower=True), g)
        bg = jnp.where(lane < A_HEADS, _sigmoid(raw), gc)
        bg_ref[...] = bg
        bgt_ref[...] = jnp.transpose(bg)[0:SUBLANE, :]

    wide = jax.ShapeDtypeStruct((t, A_WIDTH), F32)
    return pl.pallas_call(
        body, name=name, grid=(t // tt,),
        in_specs=[pl.BlockSpec((tt, cw), lambda i: (i, 0)),
                  pl.BlockSpec((SUBLANE, cw), lambda i: (jnp.maximum(i * hb - 1, 0), 0)),
                  pl.BlockSpec((tt, LANE), lambda i: (i, C_BG // LANE)),
                  pl.BlockSpec((CONV_K, cw), lambda i: (0, 0)),
                  pl.BlockSpec((SUBLANE, LANE), lambda i: (0, 0))],
        out_specs=[pl.BlockSpec((tt, A_WIDTH), lambda i: (i, 0))] * 3
        + [pl.BlockSpec((tt, LANE), lambda i: (i, 0)), pl.BlockSpec((SUBLANE, tt), lambda i: (0, i))],
        out_shape=[wide, wide, wide, jax.ShapeDtypeStruct((t, LANE), F32), jax.ShapeDtypeStruct((SUBLANE, t), F32)],
        compiler_params=_cp("parallel"))(h, h, h, conv_w, par)


def _chunk_tri(n, lower):
    r = lax.broadcasted_iota(jnp.int32, (n, n), 0)
    c = lax.broadcasted_iota(jnp.int32, (n, n), 1)
    shift = CHUNK.bit_length() - 1
    same = jnp.right_shift(r, shift) == jnp.right_shift(c, shift)
    return (same & ((c <= r) if lower else (c >= r))).astype(F32)


def _chunk_masks():
    r = lax.broadcasted_iota(jnp.int32, (CHUNK, CHUNK), 0)
    c = lax.broadcasted_iota(jnp.int32, (CHUNK, CHUNK), 1)
    return r >= c, r > c, r == c


def _split(a):
    hi = a.astype(BF16)
    return hi, (a - hi.astype(F32)).astype(BF16)


def _dot3(a, b):
    (ah, al), (bh, bl) = a, b
    d = lambda p, q: jnp.dot(p, q, preferred_element_type=F32)
    return d(ah, bh) + (d(ah, bl) + d(al, bh))


def _tri_inv_many(a_list, eye):
    d = lambda p, q: jnp.dot(p, q, preferred_element_type=F32)
    p = [(-a).astype(BF16) for a in a_list]
    tm = [eye - a for a in a_list]
    for _ in range(5):
        pf = [d(pi, pi) for pi in p]
        p = [x.astype(BF16) for x in pf]
        tm = [t + d(t.astype(BF16), pi) for t, pi in zip(tm, p)]
    ms = [_split(eye + a) for a in a_list]
    res = [eye - _dot3(m, _split(t)) for m, t in zip(ms, tm)]
    return [t + d(t.astype(BF16), r.astype(BF16)) for t, r in zip(tm, res)]


def _chunk_gates(bg_v, bgt_v, hd):
    return (bg_v[:, hd:hd + 1], bg_v[:, A_HEADS + hd:A_HEADS + hd + 1],
            None if bgt_v is None else bgt_v[A_HEADS + hd:A_HEADS + hd + 1, :])


WY_ROWS = 256
SCAN_ROWS = 128
WY_GROUP = 2


def _dn_wy(q, k, v, bg, bgt, *, name):
    t = q.shape[0]
    rows = WY_ROWS

    def body(q_ref, k_ref, v_ref, bg_ref, bgt_ref, u_ref, w_ref, tm_ref, qk_ref):
        causal, strict, diag = _chunk_masks()
        eye = diag.astype(F32)
        for c0 in range(0, rows // CHUNK, WY_GROUP):
            items = [(c, hd) for c in range(c0, c0 + WY_GROUP) for hd in range(A_HEADS)]
            rs = lambda c: slice(c * CHUNK, (c + 1) * CHUNK)
            sl = lambda hd: slice(hd * LANE, (hd + 1) * LANE)
            hs = lambda hd: slice(hd * CHUNK, (hd + 1) * CHUNK)
            gates = [_chunk_gates(bg_ref[rs(c), :], bgt_ref[:, rs(c)], hd) for c, hd in items]
            dms = [jnp.exp(jnp.where(causal, gcol - grow, NEG)) for _, gcol, grow in gates]
            kbs = [k_ref[rs(c), sl(hd)] * g[0] for (c, hd), g in zip(items, gates)]
            a_list = [jnp.where(strict, _dot_nt(kb, k_ref[rs(c), sl(hd)]) * dm, 0.0)
                      for (c, hd), kb, dm in zip(items, kbs, dms)]
            for (c, hd), dm in zip(items, dms):
                qk_ref[rs(c), hs(hd)] = jnp.where(
                    causal, _dot_nt(q_ref[rs(c), sl(hd)], k_ref[rs(c), sl(hd)]) * dm, 0.0)
            tms = _tri_inv_many(a_list, eye)
            for (c, hd), g, kb, tmat in zip(items, gates, kbs, tms):
                tm_ref[rs(c), hs(hd)] = tmat
                u_ref[rs(c), sl(hd)] = _dot(tmat, v_ref[rs(c), sl(hd)] * g[0])
                w_ref[rs(c), sl(hd)] = _dot(tmat, kb * jnp.exp(g[1])).astype(BF16)

    blk = pl.BlockSpec((rows, A_WIDTH), lambda i: (i, 0))
    half = pl.BlockSpec((rows, A_HEADS * CHUNK), lambda i: (i, 0))
    return pl.pallas_call(
        body, name=name, grid=(t // rows,),
        in_specs=[blk, blk, blk, pl.BlockSpec((rows, LANE), lambda i: (i, 0)),
                  pl.BlockSpec((SUBLANE, rows), lambda i: (0, i))],
        out_specs=[blk, blk, half, half],
        out_shape=[jax.ShapeDtypeStruct((t, A_WIDTH), F32), jax.ShapeDtypeStruct((t, A_WIDTH), BF16),
                   jax.ShapeDtypeStruct((t, A_HEADS * CHUNK), F32), jax.ShapeDtypeStruct((t, A_HEADS * CHUNK), F32)],
        compiler_params=_cp("parallel"))(q, k, v, bg, bgt)


def _dn_scan_fwd(q, k, u, w, qk, bg, *, name):
    t = q.shape[0]
    rows = SCAN_ROWS
    per = rows // CHUNK

    def body(q_ref, k_ref, u_ref, w_ref, qk_ref, bg_ref, o_ref, vn_ref, s_ref, state):
        @pl.when(pl.program_id(0) == 0)
        def _():
            state[...] = jnp.zeros_like(state)

        heads = range(A_HEADS)
        sl = lambda hd: slice(hd * LANE, (hd + 1) * LANE)
        s_cur = [state[hd] for hd in heads]
        for c in range(per):
            rs = slice(c * CHUNK, (c + 1) * CHUNK)
            bg_v = bg_ref[rs, :]
            gcols = [_chunk_gates(bg_v, None, hd)[1] for hd in heads]
            glasts = [gc[CHUNK - 1:CHUNK, :] for gc in gcols]
            for hd in heads:
                s_ref[c, hd] = s_cur[hd]
            vns = [u_ref[rs, sl(hd)] - _dot(w_ref[rs, sl(hd)], s_cur[hd]) for hd in heads]
            qss = [_dot(q_ref[rs, sl(hd)] * jnp.exp(gcols[hd]), s_cur[hd]) for hd in heads]
            s_cur = [s_cur[hd] * jnp.exp(glasts[hd])
                     + _dot_tn(k_ref[rs, sl(hd)] * jnp.exp(glasts[hd] - gcols[hd]), vns[hd]) for hd in heads]
            for hd in heads:
                vn_ref[rs, sl(hd)] = vns[hd]
                o_ref[rs, sl(hd)] = qss[hd] + _dot(qk_ref[rs, hd * CHUNK:(hd + 1) * CHUNK], vns[hd])
        for hd in heads:
            state[hd] = s_cur[hd]

    blk = pl.BlockSpec((rows, A_WIDTH), lambda i: (i, 0))
    half = pl.BlockSpec((rows, A_HEADS * CHUNK), lambda i: (i, 0))
    wide = jax.ShapeDtypeStruct((t, A_WIDTH), F32)
    return pl.pallas_call(
        body, name=name, grid=(t // rows,),
        in_specs=[blk, blk, blk, blk, half, pl.BlockSpec((rows, LANE), lambda i: (i, 0))],
        out_specs=[blk, blk, pl.BlockSpec((per, A_HEADS, LANE, LANE), lambda i: (i, 0, 0, 0))],
        out_shape=[wide, wide, jax.ShapeDtypeStruct((t // CHUNK, A_HEADS, LANE, LANE), F32)],
        scratch_shapes=[pltpu.VMEM((A_HEADS, LANE, LANE), F32)],
        compiler_params=_cp("arbitrary"))(q, k, u, w, qk, bg)


def _swa_probs(qh, kband, sink, slope, n_blk):
    s = _dot_nt(qh, kband) * (B_HEAD_DIM ** -0.5)
    qi = lax.broadcasted_iota(jnp.int32, (BLOCK, 2 * BLOCK), 0)
    si = lax.broadcasted_iota(jnp.int32, (BLOCK, 2 * BLOCK), 1)
    dist = qi + BLOCK - si
    mask = (dist >= 0) & (dist < BLOCK) & ((si >= BLOCK) | (n_blk > 0))
    s = jnp.where(mask, s - slope * dist.astype(F32), NEG)
    m = jnp.maximum(jnp.max(s, axis=-1, keepdims=True), sink)
    p = jnp.where(mask, jnp.exp(s - m), 0.0)
    ps = jnp.exp(sink - m)
    inv = 1.0 / (jnp.sum(p, axis=-1, keepdims=True) + ps)
    return p * inv, ps * inv


def _swa_specs():
    qspec = lambda c0: pl.BlockSpec((BLOCK, B_WIDTH), lambda i: (i, c0 // B_WIDTH))
    cur = lambda c0: pl.BlockSpec((BLOCK, LANE), lambda i: (i, c0 // LANE))
    prev = lambda c0: pl.BlockSpec((BLOCK, LANE), lambda i: (jnp.maximum(i - 1, 0), c0 // LANE))
    return qspec, cur, prev


def _swa_fwd(h, sinks_b, *, name):
    t = h.shape[0]
    qspec, cur, prev = _swa_specs()

    def body(q_ref, kc_ref, kp_ref, vc_ref, vp_ref, sk_ref, o_ref):
        n_blk = pl.program_id(0)
        kband = jnp.concatenate([kp_ref[...], kc_ref[...]], axis=0)
        vband = jnp.concatenate([vp_ref[...], vc_ref[...]], axis=0)
        for hq in range(B_Q_HEADS):
            hk = hq // B_GROUP
            ksl = slice(hk * B_HEAD_DIM, (hk + 1) * B_HEAD_DIM)
            qsl = slice(hq * B_HEAD_DIM, (hq + 1) * B_HEAD_DIM)
            p, _ = _swa_probs(q_ref[:, qsl], kband[:, ksl], sk_ref[hq:hq + 1, 0:1], ALIBI[hq], n_blk)
            o_ref[:, qsl] = _dot(p, vband[:, ksl])

    return pl.pallas_call(
        body, name=name, grid=(t // BLOCK,),
        in_specs=[qspec(C_QB), cur(C_KB), prev(C_KB), cur(C_VB), prev(C_VB),
                  pl.BlockSpec((B_Q_HEADS, LANE), lambda i: (0, 0))],
        out_specs=pl.BlockSpec((BLOCK, B_WIDTH), lambda i: (i, 0)),
        out_shape=jax.ShapeDtypeStruct((t, B_WIDTH), F32),
        compiler_params=_cp("parallel"))(h, h, h, h, h, sinks_b)


def _rms_gate(o, za, nw):
    outs = []
    for hd in range(A_HEADS):
        oh = o[:, hd * LANE:(hd + 1) * LANE]
        r = lax.rsqrt(jnp.mean(oh * oh, -1, keepdims=True) + RMS_EPS)
        outs.append(oh * r * nw)
    return jnp.concatenate(outs, axis=1) * _silu(za)


def _out_ln(x, oa, ob, h, norm_w, w_out, ln_g, ln_b, *, tm, name):
    t = x.shape[0]

    def body(x_ref, oa_ref, ob_ref, za_ref, zb_ref, nw_ref, w_ref, g_ref, b_ref, xn_ref, mx_ref, r_ref):
        ya = _rms_gate(oa_ref[...], za_ref[...], nw_ref[...])
        yb = ob_ref[...] * _silu(zb_ref[...])
        mixed = jnp.concatenate([ya, yb], axis=1).astype(BF16)
        mx_ref[...] = mixed
        r = DEEPNORM_ALPHA * x_ref[...] + jnp.dot(mixed, w_ref[...], preferred_element_type=F32)
        r_ref[...] = r
        mu = jnp.mean(r, -1, keepdims=True)
        xc = r - mu
        var = jnp.mean(xc * xc, -1, keepdims=True)
        xn_ref[...] = xc * lax.rsqrt(var + LN_EPS) * g_ref[...] + b_ref[...]

    row = lambda w, c: pl.BlockSpec((tm, w), lambda i: (i, c))
    full = lambda a, b: pl.BlockSpec((a, b), lambda i: (0, 0))
    return pl.pallas_call(
        body, name=name, grid=(t // tm,),
        in_specs=[row(D_MODEL, 0), row(A_WIDTH, 0), row(B_WIDTH, 0), row(A_WIDTH, C_ZA // A_WIDTH),
                  row(B_WIDTH, C_ZB // B_WIDTH), full(1, LANE), full(D_MODEL, D_MODEL), full(1, D_MODEL), full(1, D_MODEL)],
        out_specs=[row(D_MODEL, 0), row(D_MODEL, 0), row(D_MODEL, 0)],
        out_shape=[jax.ShapeDtypeStruct((t, D_MODEL), F32), jax.ShapeDtypeStruct((t, D_MODEL), BF16),
                   jax.ShapeDtypeStruct((t, D_MODEL), F32)],
        compiler_params=_cp("parallel"))(x, oa, ob, h, h, norm_w, w_out, ln_g, ln_b)


def _layer_fwd(x, wp, conv_w, par, sinks_b, norm_w, w_out_bf, ln_g, ln_b, l):
    h = _matmul(x, wp, tm=512, tn=1152, name=f"in_proj_{l}")
    q, k, v, bg, bgt = _dn_pre(h, conv_w, par, tt=512, name=f"dn_pre_{l}")
    u, w, tmat, qk = _dn_wy(q, k, v, bg, bgt, name=f"dn_wy_{l}")
    oa, vn, s_all = _dn_scan_fwd(q, k, u, w, qk, bg, name=f"dn_scan_{l}")
    ob = _swa_fwd(h, sinks_b, name=f"swa_fwd_{l}")
    xn, mixed, r = _out_ln(x, oa, ob, h, norm_w, w_out_bf, ln_g, ln_b, tm=256, name=f"out_ln_{l}")
    return xn, dict(x=x, h=h, q=q, k=k, v=v, bg=bg, bgt=bgt, w=w, tmat=tmat, qk=qk, vn=vn, oa=oa, s_all=s_all,
                    mixed=mixed, r=r)


def _loss_grad(xn, target, *, tm, name):
    t = xn.shape[0]

    def body(x_ref, t_ref, d_ref, l_ref):
        @pl.when(pl.program_id(0) == 0)
        def _():
            l_ref[...] = jnp.zeros_like(l_ref)

        err = x_ref[...] - t_ref[...]
        d_ref[...] = err * (1.0 / D_MODEL)
        l_ref[...] += 0.5 / D_MODEL * jnp.sum(err * err)

    row = pl.BlockSpec((tm, D_MODEL), lambda i: (i, 0))
    return pl.pallas_call(
        body, name=name, grid=(t // tm,), in_specs=[row, row],
        out_specs=[row, pl.BlockSpec((SUBLANE, LANE), lambda i: (0, 0))],
        out_shape=[jax.ShapeDtypeStruct((t, D_MODEL), F32), jax.ShapeDtypeStruct((SUBLANE, LANE), F32)],
        compiler_params=_cp("arbitrary"))(xn, target)


def _ln_out_bwd(dxn, r, mixed, ln_g, w_out, *, tm, name):
    t = dxn.shape[0]

    def body(dxn_ref, r_ref, mx_ref, g_ref, w_ref, dr_ref, dm_ref, dw_ref, dg_ref, db_ref):
        @pl.when(pl.program_id(0) == 0)
        def _():
            dw_ref[...] = jnp.zeros_like(dw_ref)
            dg_ref[...] = jnp.zeros_like(dg_ref)
            db_ref[...] = jnp.zeros_like(db_ref)

        rr = r_ref[...]
        xc = rr - jnp.mean(rr, -1, keepdims=True)
        rstd = lax.rsqrt(jnp.mean(xc * xc, -1, keepdims=True) + LN_EPS)
        xhat = xc * rstd
        dxn_v = dxn_ref[...]
        dxh = dxn_v * g_ref[...]
        dr = rstd * (dxh - jnp.mean(dxh, -1, keepdims=True) - xhat * jnp.mean(dxh * xhat, -1, keepdims=True))
        dr_ref[...] = dr
        dg_ref[...] += jnp.sum(dxn_v * xhat, axis=0, keepdims=True)
        db_ref[...] += jnp.sum(dxn_v, axis=0, keepdims=True)
        drb = dr.astype(BF16)
        dm_ref[...] = _dot_nt(drb, w_ref[...])
        dw_ref[...] += _dot_tn(mx_ref[...], drb)

    row = pl.BlockSpec((tm, D_MODEL), lambda i: (i, 0))
    full = lambda a, b: pl.BlockSpec((a, b), lambda i: (0, 0))
    big = jax.ShapeDtypeStruct((t, D_MODEL), F32)
    vec = jax.ShapeDtypeStruct((1, D_MODEL), F32)
    return pl.pallas_call(
        body, name=name, grid=(t // tm,),
        in_specs=[row, row, row, full(1, D_MODEL), full(D_MODEL, D_MODEL)],
        out_specs=[row, row, full(D_MODEL, D_MODEL), full(1, D_MODEL), full(1, D_MODEL)],
        out_shape=[big, big, jax.ShapeDtypeStruct((D_MODEL, D_MODEL), F32), vec, vec],
        compiler_params=_cp("arbitrary"))(dxn, r, mixed, ln_g, w_out)


def _dn_post_bwd(dm, oa, h, norm_w, *, tm, name):
    t = oa.shape[0]

    def body(dy_ref, o_ref, za_ref, nw_ref, do_ref, dza_ref, dnw_ref):
        @pl.when(pl.program_id(0) == 0)
        def _():
            dnw_ref[...] = jnp.zeros_like(dnw_ref)

        nw = nw_ref[...]
        dnw = jnp.zeros_like(nw)
        for hd in range(A_HEADS):
            sl = slice(hd * LANE, (hd + 1) * LANE)
            oh, za, dy = o_ref[:, sl], za_ref[:, sl], dy_ref[:, sl]
            rs = lax.rsqrt(jnp.mean(oh * oh, -1, keepdims=True) + RMS_EPS)
            nrm = oh * rs
            dza_ref[:, sl] = dy * nrm * nw * _dsilu(za)
            dn = dy * _silu(za)
            dnw = dnw + jnp.sum(dn * nrm, axis=0, keepdims=True)
            dnn = dn * nw
            do_ref[:, sl] = rs * dnn - oh * (rs * rs * rs) * jnp.mean(dnn * oh, -1, keepdims=True)
        dnw_ref[...] += dnw

    row = lambda c: pl.BlockSpec((tm, A_WIDTH), lambda i: (i, c))
    wide = jax.ShapeDtypeStruct((t, A_WIDTH), F32)
    return pl.pallas_call(
        body, name=name, grid=(t // tm,),
        in_specs=[row(0), row(0), row(C_ZA // A_WIDTH), pl.BlockSpec((1, LANE), lambda i: (0, 0))],
        out_specs=[row(0), row(0), pl.BlockSpec((1, LANE), lambda i: (0, 0))],
        out_shape=[wide, wide, jax.ShapeDtypeStruct((1, LANE), F32)],
        compiler_params=_cp("arbitrary"))(dm, oa, h, norm_w)


def _dn_scan_bwd(q, k, w, qk, bg, do, *, name):
    t = q.shape[0]
    rows = SCAN_ROWS
    per = rows // CHUNK
    n = t // rows

    def body(q_ref, k_ref, w_ref, qk_ref, bg_ref, do_ref, dvn_ref, ds_ref, dstate):
        @pl.when(pl.program_id(0) == 0)
        def _():
            dstate[...] = jnp.zeros_like(dstate)

        heads = range(A_HEADS)
        sl = lambda hd: slice(hd * LANE, (hd + 1) * LANE)
        ds_cur = [dstate[hd] for hd in heads]
        for c in reversed(range(per)):
            rs = slice(c * CHUNK, (c + 1) * CHUNK)
            bg_v = bg_ref[rs, :]
            gcols = [_chunk_gates(bg_v, None, hd)[1] for hd in heads]
            glasts = [gc[CHUNK - 1:CHUNK, :] for gc in gcols]
            for hd in heads:
                ds_ref[c, hd] = ds_cur[hd]
            pdo = [_dot_tn(qk_ref[rs, hd * CHUNK:(hd + 1) * CHUNK], do_ref[rs, sl(hd)]) for hd in heads]
            qdo = [_dot_tn(q_ref[rs, sl(hd)] * jnp.exp(gcols[hd]), do_ref[rs, sl(hd)]) for hd in heads]
            dvns = [pdo[hd] + _dot(k_ref[rs, sl(hd)] * jnp.exp(glasts[hd] - gcols[hd]), ds_cur[hd]) for hd in heads]
            ds_cur = [qdo[hd] + jnp.exp(glasts[hd]) * ds_cur[hd] - _dot_tn(w_ref[rs, sl(hd)], dvns[hd])
                      for hd in heads]
            for hd in heads:
                dvn_ref[rs, sl(hd)] = dvns[hd]
        for hd in heads:
            dstate[hd] = ds_cur[hd]

    blk = pl.BlockSpec((rows, A_WIDTH), lambda i: (n - 1 - i, 0))
    return pl.pallas_call(
        body, name=name, grid=(n,),
        in_specs=[blk, blk, blk, pl.BlockSpec((rows, A_HEADS * CHUNK), lambda i: (n - 1 - i, 0)),
                  pl.BlockSpec((rows, LANE), lambda i: (n - 1 - i, 0)), blk],
        out_specs=[blk, pl.BlockSpec((per, A_HEADS, LANE, LANE), lambda i: (n - 1 - i, 0, 0, 0))],
        out_shape=[jax.ShapeDtypeStruct((t, A_WIDTH), F32),
                   jax.ShapeDtypeStruct((t // CHUNK, A_HEADS, LANE, LANE), F32)],
        scratch_shapes=[pltpu.VMEM((A_HEADS, LANE, LANE), F32)],
        compiler_params=_cp("arbitrary"))(q, k, w, qk, bg, do)


def _dn_chunk_bwd(q, k, v, vn, tmat, qk, bg, bgt, s_all, ds_all, dvn, do, *, name):
    t = q.shape[0]
    rows = WY_ROWS
    per = rows // CHUNK

    def body(q_ref, k_ref, v_ref, vn_ref, tm_ref, qk_ref, bg_ref, bgt_ref, s_ref, ds_ref, dvn_ref, do_ref,
             dq_ref, dk_ref, dv_ref, dbg_ref, dbgt_ref):
        causal, strict, _ = _chunk_masks()
        lane = lax.broadcasted_iota(jnp.int32, (CHUNK, LANE), 1)
        rowi = lax.broadcasted_iota(jnp.int32, (CHUNK, 1), 0)
        sub = lax.broadcasted_iota(jnp.int32, (SUBLANE, CHUNK), 0)
        rs = lambda c: slice(c * CHUNK, (c + 1) * CHUNK)
        sl = lambda hd: slice(hd * LANE, (hd + 1) * LANE)
        hs = lambda hd: slice(hd * CHUNK, (hd + 1) * CHUNK)
        for c0 in range(0, per, WY_GROUP):
            items = [(c, hd) for c in range(c0, c0 + WY_GROUP) for hd in range(A_HEADS)]
            at = lambda ref: [ref[rs(c), sl(hd)] for c, hd in items]
            qs, ks, vs, dos, vns, dvns = at(q_ref), at(k_ref), at(v_ref), at(do_ref), at(vn_ref), at(dvn_ref)
            tmhs = [tm_ref[rs(c), hs(hd)] for c, hd in items]
            ps = [qk_ref[rs(c), hs(hd)] for c, hd in items]
            gates = [_chunk_gates(bg_ref[rs(c), :], bgt_ref[:, rs(c)], hd) for c, hd in items]
            betas = [g[0] for g in gates]
            gcols = [g[1] for g in gates]
            dmats = [jnp.exp(jnp.where(causal, g[1] - g[2], NEG)) for g in gates]
            es = [jnp.exp(gc) for gc in gcols]
            glasts = [gc[CHUNK - 1:CHUNK, :] for gc in gcols]
            eks = [jnp.exp(gl - gc) for gl, gc in zip(glasts, gcols)]
            kbs = [kh * b for kh, b in zip(ks, betas)]
            vbs = [vh * b for vh, b in zip(vs, betas)]
            kbes = [kb * e for kb, e in zip(kbs, es)]

            a_s = [jnp.where(strict, _dot_nt(kb, kh) * dm, 0.0) for kb, kh, dm in zip(kbs, ks, dmats)]
            dps = [jnp.where(causal, _dot_nt(doh, vnh), 0.0) for doh, vnh in zip(dos, vns)]
            dqds = [_dot_nt(doh, s_ref[c, hd]) for doh, (c, hd) in zip(dos, items)]
            dkds = [_dot_nt(vnh, ds_ref[c, hd]) for vnh, (c, hd) in zip(vns, items)]
            dws = [-_dot_nt(dvnh, s_ref[c, hd]) for dvnh, (c, hd) in zip(dvns, items)]
            dvbs = [_dot_tn(tmh, dvnh) for tmh, dvnh in zip(tmhs, dvns)]
            dgts = [jnp.sum(s_ref[c, hd] * ds_ref[c, hd], keepdims=True) for c, hd in items]
            dts = [_dot_nt(dvnh, vb) + _dot_nt(dw, kbe) for dvnh, vb, dw, kbe in zip(dvns, vbs, dws, kbes)]
            dkbes = [_dot_tn(tmh, dw) for tmh, dw in zip(tmhs, dws)]
            xs = [_dot_nt(dt, tmh) for dt, tmh in zip(dts, tmhs)]
            das = [jnp.where(strict, -_dot_tn(tmh, x), 0.0) for tmh, x in zip(tmhs, xs)]
            dmas = [da * dm for da, dm in zip(das, dmats)]
            dmps = [dp * dm for dp, dm in zip(dps, dmats)]
            dkbs = [_dot(dma, kh) + dkbe * e for dma, kh, dkbe, e in zip(dmas, ks, dkbes, es)]
            for i, (c, hd) in enumerate(items):
                dq_ref[rs(c), sl(hd)] = _dot(dmps[i], ks[i]) + dqds[i] * es[i]
                dk_ref[rs(c), sl(hd)] = (_dot_tn(dmas[i], kbs[i]) + _dot_tn(dmps[i], qs[i]) + dkds[i] * eks[i]
                                         + dkbs[i] * betas[i])
                dv_ref[rs(c), sl(hd)] = dvbs[i] * betas[i]
            for c in range(c0, c0 + WY_GROUP):
                acc = jnp.zeros((CHUNK, LANE), F32)
                acc_t = jnp.zeros((SUBLANE, CHUNK), F32)
                for i, (ci, hd) in enumerate(items):
                    if ci != c:
                        continue
                    gmat = das[i] * a_s[i] + dps[i] * ps[i]
                    rk = jnp.sum(dkds[i] * ks[i], -1, keepdims=True) * eks[i]
                    de = (jnp.sum(dqds[i] * qs[i], -1, keepdims=True)
                          + jnp.sum(dkbes[i] * kbs[i], -1, keepdims=True))
                    dglast = jnp.sum(rk, keepdims=True) + dgts[i] * jnp.exp(glasts[i])
                    dgc = (jnp.sum(gmat, -1, keepdims=True) + de * es[i] - rk
                           + jnp.where(rowi == CHUNK - 1, dglast, 0.0))
                    dbeta = (jnp.sum(dkbs[i] * ks[i], -1, keepdims=True)
                             + jnp.sum(dvbs[i] * vs[i], -1, keepdims=True))
                    acc = acc + jnp.where(lane == hd, dbeta, 0.0) + jnp.where(lane == A_HEADS + hd, dgc, 0.0)
                    acc_t = acc_t + jnp.where(sub == A_HEADS + hd, -jnp.sum(gmat, axis=0, keepdims=True), 0.0)
                dbg_ref[rs(c), :] = acc
                dbgt_ref[:, rs(c)] = acc_t

    blk = pl.BlockSpec((rows, A_WIDTH), lambda i: (i, 0))
    half = pl.BlockSpec((rows, A_HEADS * CHUNK), lambda i: (i, 0))
    col = pl.BlockSpec((rows, LANE), lambda i: (i, 0))
    rowf = pl.BlockSpec((SUBLANE, rows), lambda i: (0, i))
    st = pl.BlockSpec((per, A_HEADS, LANE, LANE), lambda i: (i, 0, 0, 0))
    wide = jax.ShapeDtypeStruct((t, A_WIDTH), F32)
    return pl.pallas_call(
        body, name=name, grid=(t // rows,),
        in_specs=[blk, blk, blk, blk, half, half, col, rowf, st, st, blk, blk],
        out_specs=[blk, blk, blk, col, rowf],
        out_shape=[wide, wide, wide, jax.ShapeDtypeStruct((t, LANE), F32), jax.ShapeDtypeStruct((SUBLANE, t), F32)],
        compiler_params=_cp("parallel"))(q, k, v, vn, tmat, qk, bg, bgt, s_all, ds_all, dvn, do)


def _dn_pre_bwd(h, conv_w, par, dq, dk, dv, dbg, dbgt, *, tt, name):
    t = h.shape[0]
    cw = 3 * A_WIDTH
    hb = tt // SUBLANE

    def body(pre_ref, halo_ref, bgi_ref, cw_ref, par_ref, dq_ref, dk_ref, dv_ref, dbg_ref, dbgt_ref,
             dc_ref, dbgi_ref, dpar_ref):
        i = pl.program_id(0)

        @pl.when(i == 0)
        def _():
            dpar_ref[...] = jnp.zeros_like(dpar_ref)

        cur = pre_ref[...]
        before = jnp.where(i > 0, halo_ref[...], 0.0)
        c = _conv_fwd(cur, before, cw_ref[...])
        s = _silu(c)
        ds = _dsilu(c)
        for hd in range(A_HEADS):
            sl = slice(hd * LANE, (hd + 1) * LANE)
            for base, d_ref, scale in ((0, dq_ref, A_HEAD_DIM ** -0.5), (A_WIDTH, dk_ref, 1.0)):
                csl = slice(base + hd * LANE, base + (hd + 1) * LANE)
                tq = s[:, base + hd * LANE:base + (hd + 1) * LANE]
                dy = d_ref[:, sl]
                rq = lax.rsqrt(jnp.sum(tq * tq, -1, keepdims=True) + L2_EPS)
                dtq = scale * (rq * dy - tq * (rq * rq * rq) * jnp.sum(dy * tq, -1, keepdims=True))
                dc_ref[:, csl] = dtq * ds[:, base + hd * LANE:base + (hd + 1) * LANE]
        dc_ref[:, 2 * A_WIDTH:] = dv_ref[...] * ds[:, 2 * A_WIDTH:]
        raw = bgi_ref[...]
        lane = lax.broadcasted_iota(jnp.int32, raw.shape, 1)
        is_b = lane < A_HEADS
        is_a = (lane >= A_HEADS) & (lane < 2 * A_HEADS)
        rows_t = jnp.concatenate([dbgt_ref[...], jnp.zeros((LANE - SUBLANE, tt), F32)], axis=0)
        dbg_v = dbg_ref[...] + jnp.where(is_a, jnp.transpose(rows_t), 0.0)
        dbg_v = jnp.where(is_a, _dot_hi(_chunk_tri(tt, lower=False), jnp.where(is_a, dbg_v, 0.0)), dbg_v)
        beta = _sigmoid(raw)
        z = raw + par_ref[1:2, :]
        neg_ea = -jnp.exp(par_ref[0:1, :])
        g = neg_ea * _softplus(z)
        da = dbg_v * neg_ea * _sigmoid(z)
        dbgi_ref[...] = jnp.where(is_b, dbg_v * beta * (1.0 - beta), jnp.where(is_a, da, 0.0))
        dpar_ref[0:1, :] += jnp.sum(jnp.where(is_a, dbg_v * g, 0.0), axis=0, keepdims=True)
        dpar_ref[1:2, :] += jnp.sum(jnp.where(is_a, da, 0.0), axis=0, keepdims=True)

    wide = pl.BlockSpec((tt, A_WIDTH), lambda i: (i, 0))
    return pl.pallas_call(
        body, name=name, grid=(t // tt,),
        in_specs=[pl.BlockSpec((tt, cw), lambda i: (i, 0)),
                  pl.BlockSpec((SUBLANE, cw), lambda i: (jnp.maximum(i * hb - 1, 0), 0)),
                  pl.BlockSpec((tt, LANE), lambda i: (i, C_BG // LANE)),
                  pl.BlockSpec((CONV_K, cw), lambda i: (0, 0)),
                  pl.BlockSpec((SUBLANE, LANE), lambda i: (0, 0)),
                  wide, wide, wide, pl.BlockSpec((tt, LANE), lambda i: (i, 0)),
                  pl.BlockSpec((SUBLANE, tt), lambda i: (0, i))],
        out_specs=[pl.BlockSpec((tt, cw), lambda i: (i, 0)), pl.BlockSpec((tt, LANE), lambda i: (i, 0)),
                   pl.BlockSpec((SUBLANE, LANE), lambda i: (0, 0))],
        out_shape=[jax.ShapeDtypeStruct((t, cw), F32), jax.ShapeDtypeStruct((t, LANE), F32),
                   jax.ShapeDtypeStruct((SUBLANE, LANE), F32)],
        compiler_params=_cp("arbitrary"))(h, h, h, conv_w, par, dq, dk, dv, dbg, dbgt)


def _conv_bwd(dc, h, conv_w, *, tt, name):
    t = dc.shape[0]
    cw = 3 * A_WIDTH
    hb = tt // SUBLANE
    nb = t // tt

    def body(dc_ref, after_ref, pre_ref, before_ref, cw_ref, dpre_ref, dcw_ref):
        i = pl.program_id(0)

        @pl.when(i == 0)
        def _():
            dcw_ref[...] = jnp.zeros_like(dcw_ref)

        dcv = dc_ref[...]
        after = jnp.where(i < nb - 1, after_ref[...], 0.0)
        cur = pre_ref[...]
        before = jnp.where(i > 0, before_ref[...], 0.0)
        w = cw_ref[...]
        acc = dcv * w[CONV_K - 1:CONV_K, :]
        dcw_ref[CONV_K - 1:CONV_K, :] += jnp.sum(dcv * cur, axis=0, keepdims=True)
        for s in range(1, CONV_K):
            j = CONV_K - 1 - s
            acc = acc + _shift_up(dcv, after, s) * w[j:j + 1, :]
            dcw_ref[j:j + 1, :] += jnp.sum(dcv * _shift_down(cur, before, s), axis=0, keepdims=True)
        dpre_ref[...] = acc

    return pl.pallas_call(
        body, name=name, grid=(nb,),
        in_specs=[pl.BlockSpec((tt, cw), lambda i: (i, 0)),
                  pl.BlockSpec((SUBLANE, cw), lambda i: (jnp.minimum((i + 1) * hb, t // SUBLANE - 1), 0)),
                  pl.BlockSpec((tt, cw), lambda i: (i, 0)),
                  pl.BlockSpec((SUBLANE, cw), lambda i: (jnp.maximum(i * hb - 1, 0), 0)),
                  pl.BlockSpec((CONV_K, cw), lambda i: (0, 0))],
        out_specs=[pl.BlockSpec((tt, cw), lambda i: (i, 0)), pl.BlockSpec((SUBLANE, cw), lambda i: (0, 0))],
        out_shape=[jax.ShapeDtypeStruct((t, cw), F32), jax.ShapeDtypeStruct((SUBLANE, cw), F32)],
        compiler_params=_cp("arbitrary"))(dc, dc, h, h, conv_w)


def _swa_bwd(h, dm, sinks_b, *, name):
    t = h.shape[0]
    qspec, cur, prev = _swa_specs()

    def body(q_ref, kc_ref, kp_ref, vc_ref, vp_ref, zb_ref, dy_ref, sk_ref, dq_ref, dzb_ref, dk_ref, dv_ref, dsk_ref):
        n_blk = pl.program_id(0)

        @pl.when(n_blk == 0)
        def _():
            dk_ref[...] = jnp.zeros_like(dk_ref)
            dv_ref[...] = jnp.zeros_like(dv_ref)
            dsk_ref[...] = jnp.zeros_like(dsk_ref)

        kband = jnp.concatenate([kp_ref[...], kc_ref[...]], axis=0)
        vband = jnp.concatenate([vp_ref[...], vc_ref[...]], axis=0)
        scale = B_HEAD_DIM ** -0.5
        dk_acc = [jnp.zeros((2 * BLOCK, B_HEAD_DIM), F32) for _ in range(B_KV_HEADS)]
        dv_acc = [jnp.zeros((2 * BLOCK, B_HEAD_DIM), F32) for _ in range(B_KV_HEADS)]
        for hq in range(B_Q_HEADS):
            hk = hq // B_GROUP
            ksl = slice(hk * B_HEAD_DIM, (hk + 1) * B_HEAD_DIM)
            qsl = slice(hq * B_HEAD_DIM, (hq + 1) * B_HEAD_DIM)
            qh = q_ref[:, qsl]
            p, ps = _swa_probs(qh, kband[:, ksl], sk_ref[hq:hq + 1, 0:1], ALIBI[hq], n_blk)
            o = _dot(p, vband[:, ksl])
            zb, dy = zb_ref[:, qsl], dy_ref[:, qsl]
            dzb_ref[:, qsl] = dy * o * _dsilu(zb)
            do = dy * _silu(zb)
            dp = _dot_nt(do, vband[:, ksl])
            delta = jnp.sum(do * o, -1, keepdims=True)
            ds = p * (dp - delta)
            dq_ref[:, qsl] = _dot(ds, kband[:, ksl]) * scale
            dk_acc[hk] = dk_acc[hk] + _dot_tn(ds, qh) * scale
            dv_acc[hk] = dv_acc[hk] + _dot_tn(p, do)
            dsk_ref[hq:hq + 1, :] += -jnp.sum(ps * delta, keepdims=True)
        dkb = jnp.concatenate(dk_acc, axis=1)
        dvb = jnp.concatenate(dv_acc, axis=1)
        at_cur = pl.ds(pl.multiple_of(n_blk * BLOCK, BLOCK), BLOCK)
        at_prev = pl.ds(pl.multiple_of(jnp.maximum(n_blk - 1, 0) * BLOCK, BLOCK), BLOCK)
        dk_ref[at_prev, :] += dkb[:BLOCK]
        dv_ref[at_prev, :] += dvb[:BLOCK]
        dk_ref[at_cur, :] += dkb[BLOCK:]
        dv_ref[at_cur, :] += dvb[BLOCK:]

    wide = jax.ShapeDtypeStruct((t, B_WIDTH), F32)
    narrow = jax.ShapeDtypeStruct((t, B_KV_WIDTH), F32)
    res = lambda a, b: pl.BlockSpec((a, b), lambda i: (0, 0))
    return pl.pallas_call(
        body, name=name, grid=(t // BLOCK,),
        in_specs=[qspec(C_QB), cur(C_KB), prev(C_KB), cur(C_VB), prev(C_VB), qspec(C_ZB),
                  pl.BlockSpec((BLOCK, B_WIDTH), lambda i: (i, 1)), res(B_Q_HEADS, LANE)],
        out_specs=[pl.BlockSpec((BLOCK, B_WIDTH), lambda i: (i, 0))] * 2
        + [res(t, B_KV_WIDTH), res(t, B_KV_WIDTH), res(B_Q_HEADS, LANE)],
        out_shape=[wide, wide, narrow, narrow, jax.ShapeDtypeStruct((B_Q_HEADS, LANE), F32)],
        compiler_params=_cp("arbitrary"))(h, h, h, h, h, h, dm, sinks_b)


def _matmul_tn(a, b, *, tk, tn, name):
    t, m = a.shape
    n = b.shape[1]

    def body(a_ref, b_ref, o_ref):
        @pl.when(pl.program_id(1) == 0)
        def _():
            o_ref[...] = jnp.zeros_like(o_ref)

        o_ref[...] += _dot_tn(a_ref[...], b_ref[...])

    return pl.pallas_call(
        body, name=name, grid=(n // tn, t // tk),
        in_specs=[pl.BlockSpec((tk, m), lambda j, kk: (kk, 0)), pl.BlockSpec((tk, tn), lambda j, kk: (kk, j))],
        out_specs=pl.BlockSpec((m, tn), lambda j, kk: (0, j)),
        out_shape=jax.ShapeDtypeStruct((m, n), F32),
        compiler_params=_cp("parallel", "arbitrary"))(a, b)


def _matmul_nt_add(a, b, dr, *, tm, name):
    t, n = a.shape
    m = b.shape[0]

    def body(a_ref, b_ref, r_ref, o_ref):
        o_ref[...] = _dot_nt(a_ref[...], b_ref[...]) + DEEPNORM_ALPHA * r_ref[...]

    return pl.pallas_call(
        body, name=name, grid=(t // tm,),
        in_specs=[pl.BlockSpec((tm, n), lambda i: (i, 0)), pl.BlockSpec((m, n), lambda i: (0, 0)),
                  pl.BlockSpec((tm, m), lambda i: (i, 0))],
        out_specs=pl.BlockSpec((tm, m), lambda i: (i, 0)),
        out_shape=jax.ShapeDtypeStruct((t, m), F32),
        compiler_params=_cp("parallel"))(a, b, dr)


def _layer_bwd(dxn, res, wp, conv_w, par, sinks_b, norm_w, w_out_bf, ln_g, l):
    dr, dm, dw_out, dln_g, dln_b = _ln_out_bwd(dxn, res["r"], res["mixed"], ln_g, w_out_bf, tm=256, name=f"ln_out_bwd_{l}")
    h = res["h"]
    do, dza, dnw = _dn_post_bwd(dm, res["oa"], h, norm_w, tm=512, name=f"dn_post_bwd_{l}")
    dvn, ds_all = _dn_scan_bwd(res["q"], res["k"], res["w"], res["qk"], res["bg"], do, name=f"dn_scan_bwd_{l}")
    dq, dk, dv, dbg, dbgt = _dn_chunk_bwd(res["q"], res["k"], res["v"], res["vn"], res["tmat"], res["qk"], res["bg"],
                                          res["bgt"], res["s_all"], ds_all, dvn, do, name=f"dn_chunk_bwd_{l}")
    dc, dbgi, dpar = _dn_pre_bwd(h, conv_w, par, dq, dk, dv, dbg, dbgt, tt=512, name=f"dn_pre_bwd_{l}")
    dpre, dcw = _conv_bwd(dc, h, conv_w, tt=512, name=f"conv_bwd_{l}")
    dqb, dzb, dkb, dvb, dsk = _swa_bwd(h, dm, sinks_b, name=f"swa_bwd_{l}")
    dh = jnp.concatenate([dpre, dza, dqb, dzb, dkb, dvb, dbgi], axis=1)
    dwp = _matmul_tn(res["x"], dh, tk=512, tn=1152, name=f"in_proj_dw_{l}")
    dx = _matmul_nt_add(dh, wp, dr, tm=256, name=f"in_proj_dx_{l}")
    grads = dict(w_in=dwp, conv_w=dcw[:CONV_K], a_log=dpar[0, A_HEADS:2 * A_HEADS], dt_bias=dpar[1, A_HEADS:2 * A_HEADS],
                 norm_w=dnw[0], sinks=dsk[:, 0], w_out=dw_out, ln_g=dln_g[0], ln_b=dln_b[0])
    return dx, grads


def _local_step(x, target, wp, conv_w, a_log, dt_bias, norm_w, sinks, w_out_bf, ln_g, ln_b):
    per_layer = []
    saved = []
    for l in range(DEPTH):
        args = (wp[l], conv_w[l], _gate_params(a_log[l], dt_bias[l]),
                jnp.broadcast_to(sinks[l][:, None], (B_Q_HEADS, LANE)), norm_w[l][None], w_out_bf[l])
        per_layer.append(args)
        x, res = _layer_fwd(x, *args, ln_g[l][None], ln_b[l][None], l)
        saved.append(res)
    dx, loss_tile = _loss_grad(x, target, tm=512, name="loss_grad")
    grads = [None] * DEPTH
    for l in reversed(range(DEPTH)):
        dx, grads[l] = _layer_bwd(dx, saved[l], *per_layer[l], ln_g[l][None], l)
    return loss_tile, dx, grads


_ANY = pl.BlockSpec(memory_space=pl.ANY)
_MESH = pl.DeviceIdType.MESH


def _pair_exchange(arrays, *, name):
    n = len(arrays)

    def body(*refs):
        src, dst, (send_sems, recv_sems) = refs[:n], refs[n:2 * n], refs[2 * n:]
        sibling = (lax.axis_index("x"), lax.axis_index("y"), 1 - lax.axis_index("c"))
        copies = [pltpu.make_async_remote_copy(src_ref=src[k], dst_ref=dst[k], send_sem=send_sems.at[k],
                                               recv_sem=recv_sems.at[k], device_id=sibling, device_id_type=_MESH)
                  for k in range(n)]
        for cp in copies:
            cp.start()
        for cp in copies:
            cp.wait()

    return pl.pallas_call(
        body, name=name, in_specs=[_ANY] * n, out_specs=[_ANY] * n,
        out_shape=[jax.ShapeDtypeStruct(a.shape, a.dtype) for a in arrays],
        scratch_shapes=[pltpu.SemaphoreType.DMA((n,)), pltpu.SemaphoreType.DMA((n,))])(*arrays)


def _chip_exchange(arrays, scatter, *, name):
    n = len(arrays)
    shapes = [a.shape if sc else (N_SHARD,) + a.shape for a, sc in zip(arrays, scatter)]

    def body(*refs):
        src, dst = refs[:n], refs[n:2 * n]
        send_sems, recv_sems, local_sems = refs[2 * n:]
        x, y, c = lax.axis_index("x"), lax.axis_index("y"), lax.axis_index("c")
        me = 2 * x + y
        chips = [(1 - x, y), (x, 1 - y), (1 - x, 1 - y)]
        local = [pltpu.make_async_copy(src[k].at[me] if scatter[k] else src[k], dst[k].at[me], local_sems.at[k])
                 for k in range(n)]
        for cp in local:
            cp.start()
        sends = []
        for k in range(n):
            for j, (px, py) in enumerate(chips):
                sends.append(pltpu.make_async_remote_copy(
                    src_ref=src[k].at[2 * px + py] if scatter[k] else src[k], dst_ref=dst[k].at[me],
                    send_sem=send_sems.at[3 * k + j], recv_sem=recv_sems.at[3 * k + j],
                    device_id=(px, py, c), device_id_type=_MESH))
        for cp in sends:
            cp.start()
        for k in range(n):
            for j, (px, py) in enumerate(chips):
                pltpu.make_async_remote_copy(
                    src_ref=src[k].at[2 * px + py] if scatter[k] else src[k], dst_ref=dst[k].at[2 * px + py],
                    send_sem=send_sems.at[3 * k + j], recv_sem=recv_sems.at[3 * k + j],
                    device_id=(px, py, c), device_id_type=_MESH).wait_recv()
        for cp in sends:
            cp.wait_send()
        for cp in local:
            cp.wait()

    return pl.pallas_call(
        body, name=name, in_specs=[_ANY] * n, out_specs=[_ANY] * n,
        out_shape=[jax.ShapeDtypeStruct(s, a.dtype) for s, a in zip(shapes, arrays)],
        scratch_shapes=[pltpu.SemaphoreType.DMA((3 * n,)), pltpu.SemaphoreType.DMA((3 * n,)),
                        pltpu.SemaphoreType.DMA((n,))])(*arrays)


def _rows_view(a):
    return a.reshape((-1, a.shape[-1]))


def _tile_rows(rows):
    for tm in (512, 256, 128, 64, 32, 16, 8):
        if rows % tm == 0:
            return tm
    return rows


def _add2(a, b, *, name):
    a2, b2 = _rows_view(a), _rows_view(b)
    rows, cols = a2.shape
    tm = _tile_rows(rows)

    def body(a_ref, b_ref, o_ref):
        o_ref[...] = a_ref[...] + b_ref[...]

    blk = pl.BlockSpec((tm, cols), lambda i: (i, 0))
    out = pl.pallas_call(body, name=name, grid=(rows // tm,), in_specs=[blk, blk], out_specs=blk,
                         out_shape=jax.ShapeDtypeStruct(a2.shape, a2.dtype), compiler_params=_cp("parallel"))(a2, b2)
    return out.reshape(a.shape)


def _sum4(a, *, name):
    a3 = a.reshape((N_SHARD, -1, a.shape[-1]))
    _, rows, cols = a3.shape
    tm = _tile_rows(rows)

    def body(a_ref, o_ref):
        o_ref[...] = ((a_ref[0] + a_ref[1]) + a_ref[2]) + a_ref[3]

    out = pl.pallas_call(body, name=name, grid=(rows // tm,),
                         in_specs=[pl.BlockSpec((N_SHARD, tm, cols), lambda i: (0, i, 0))],
                         out_specs=pl.BlockSpec((tm, cols), lambda i: (i, 0)),
                         out_shape=jax.ShapeDtypeStruct((rows, cols), a.dtype), compiler_params=_cp("parallel"))(a3)
    return out.reshape(a.shape[1:])


def _adamw(w, g, m, v, *, name):
    w2, g2, m2, v2 = (_rows_view(t) for t in (w, g, m, v))
    rows, cols = w2.shape
    tm = _tile_rows(rows)

    def body(w_ref, g_ref, m_ref, v_ref, d_ref, mo_ref, vo_ref):
        gv = g_ref[...]
        mn = ADAM_B1 * m_ref[...] + (1.0 - ADAM_B1) * gv
        vn = ADAM_B2 * v_ref[...] + (1.0 - ADAM_B2) * (gv * gv)
        mo_ref[...] = mn
        vo_ref[...] = vn
        m_hat = mn / (1.0 - ADAM_B1 ** ADAM_STEP)
        v_hat = vn / (1.0 - ADAM_B2 ** ADAM_STEP)
        d_ref[...] = -ADAM_LR * (m_hat / (jnp.sqrt(v_hat) + ADAM_EPS) + ADAM_WD * w_ref[...])

    blk = pl.BlockSpec((tm, cols), lambda i: (i, 0))
    shp = jax.ShapeDtypeStruct(w2.shape, F32)
    outs = pl.pallas_call(body, name=name, grid=(rows // tm,), in_specs=[blk] * 4, out_specs=[blk] * 3,
                          out_shape=[shp] * 3, compiler_params=_cp("parallel"))(w2, g2, m2, v2)
    return [o.reshape(w.shape) for o in outs]


def _to_kernel_cols(w):
    pad = jnp.zeros(w.shape[:-1] + (LANE - 2 * A_HEADS,), w.dtype)
    return jnp.concatenate([w[..., 0:2048], w[..., 2056:2568], w[..., 2824:3336], w[..., 2568:2696],
                            w[..., 2696:2824], w[..., 2048:2056], pad], axis=-1)


def _from_kernel_cols(w):
    return jnp.concatenate([w[..., 0:2048], w[..., C_BG:C_BG + 2 * A_HEADS], w[..., C_QB:C_QB + B_WIDTH],
                            w[..., C_KB:C_KB + B_KV_WIDTH], w[..., C_VB:C_VB + B_KV_WIDTH],
                            w[..., C_ZB:C_ZB + B_WIDTH]], axis=-1)


def _gate_params(a_log, dt_bias):
    par = jnp.zeros((SUBLANE, LANE), F32)
    par = par.at[0, A_HEADS:2 * A_HEADS].set(a_log)
    return par.at[1, A_HEADS:2 * A_HEADS].set(dt_bias)


SMALL = ("conv_w", "a_log", "dt_bias", "norm_w", "sinks", "ln_g", "ln_b")


def _pack(parts, cols):
    flat = jnp.concatenate([p.reshape(-1) for p in parts])
    rows = -(-flat.shape[0] // cols)
    return jnp.pad(flat, (0, rows * cols - flat.shape[0])).reshape(rows, cols)


def _unpack(packed, shapes):
    flat = packed.reshape(-1)
    out, at = [], 0
    for s in shapes:
        n = math.prod(s)
        out.append(flat[at:at + n].reshape(s))
        at += n
    return out


def kernel(x, w_in, conv_w, a_log, dt_bias, norm_w, sinks, w_out, ln_g, ln_b, loss_target, m_w_in, m_conv_w, m_a_log, m_dt_bias, m_norm_w, m_sinks, m_w_out, m_ln_g, m_ln_b, v_w_in, v_conv_w, v_a_log, v_dt_bias, v_norm_w, v_sinks, v_w_out, v_ln_g, v_ln_b):
    xi, yi, ci = lax.axis_index("x"), lax.axis_index("y"), lax.axis_index("c")
    me = 2 * xi + yi

    g_in, g_out, g_conv = _chip_exchange([w_in.astype(BF16), w_out.astype(BF16), conv_w], [False] * 3,
                                         name="gather_weights")
    wp = _to_kernel_cols(jnp.moveaxis(g_in, 0, 2).reshape(DEPTH, D_MODEL, IN_COLS))
    w_out_full = jnp.moveaxis(g_out, 0, 1).reshape(DEPTH, D_MODEL, D_MODEL)
    conv_full = jnp.moveaxis(g_conv, 0, 2).reshape(DEPTH, CONV_K, 3 * A_WIDTH)

    loss_tile, dx, grads = _local_step(x[0], loss_target[0], wp, conv_full, a_log, dt_bias, norm_w, sinks,
                                       w_out_full, ln_g, ln_b)
    loss = lax.psum(loss_tile[0, 0], ("x", "y", "c"))

    gin = jnp.stack([_from_kernel_cols(g["w_in"]) for g in grads])
    gin = jnp.moveaxis(gin.reshape(DEPTH, D_MODEL, N_SHARD, IN_SHARD), 2, 1)
    gout = jnp.stack([g["w_out"] for g in grads]).reshape(DEPTH, N_SHARD, OUT_SHARD, D_MODEL)
    small_shapes = [(DEPTH,) + grads[0][nm].shape for nm in SMALL]
    gsmall = _pack([jnp.stack([g[nm] for g in grads]) for nm in SMALL], D_MODEL)

    pick = lambda a, i: lax.dynamic_index_in_dim(a, i, 0, keepdims=False)
    r_in, r_out, r_small = _pair_exchange([pick(gin, 1 - ci), pick(gout, 1 - ci), gsmall], name="pair_reduce")
    p_in = _add2(pick(gin, ci), r_in, name="pair_add_in")
    p_out = _add2(pick(gout, ci), r_out, name="pair_add_out")
    p_small = _add2(gsmall, r_small, name="pair_add_small")
    q_in, q_out, q_small = _chip_exchange([p_in, p_out, p_small], [True, True, False], name="chip_reduce")
    s_in = _sum4(q_in, name="chip_sum_in")
    s_out = _sum4(q_out, name="chip_sum_out")
    s_small = _sum4(q_small, name="chip_sum_small")
    o_in, o_out = _pair_exchange([s_in, s_out], name="pair_share")
    by_layer = lambda mine, other: jnp.where(ci == 0, jnp.stack([mine, other]), jnp.stack([other, mine]))
    grad_in = by_layer(s_in, o_in)
    grad_out = by_layer(s_out, o_out)
    gs = dict(zip(SMALL, _unpack(s_small, small_shapes)))
    gs["conv_w"] = lax.dynamic_slice_in_dim(gs["conv_w"], me * CONV_SHARD, CONV_SHARD, axis=2)

    d_in, nm_in, nv_in = _adamw(w_in, grad_in, m_w_in, v_w_in, name="adamw_in")
    d_out, nm_out, nv_out = _adamw(w_out, grad_out, m_w_out, v_w_out, name="adamw_out")
    ws = dict(conv_w=conv_w, a_log=a_log, dt_bias=dt_bias, norm_w=norm_w, sinks=sinks, ln_g=ln_g, ln_b=ln_b)
    ms = dict(conv_w=m_conv_w, a_log=m_a_log, dt_bias=m_dt_bias, norm_w=m_norm_w, sinks=m_sinks, ln_g=m_ln_g, ln_b=m_ln_b)
    vs = dict(conv_w=v_conv_w, a_log=v_a_log, dt_bias=v_dt_bias, norm_w=v_norm_w, sinks=v_sinks, ln_g=v_ln_g, ln_b=v_ln_b)
    shard_shapes = [ws[nm].shape for nm in SMALL]
    packed = [_pack([d[nm] for nm in SMALL], LANE) for d in (ws, gs, ms, vs)]
    d_s, nm_s, nv_s = (dict(zip(SMALL, _unpack(o, shard_shapes))) for o in _adamw(*packed, name="adamw_small"))

    def in_order(big_in, small, big_out):
        return (big_in, small["conv_w"], small["a_log"], small["dt_bias"], small["norm_w"], small["sinks"], big_out,
                small["ln_g"], small["ln_b"])

    return (loss, dx[None], *in_order(grad_in, gs, grad_out), *in_order(d_in, d_s, d_out),
            *in_order(nm_in, nm_s, nm_out), *in_order(nv_in, nv_s, nv_out))
```

```python
import functools
import math

import jax
import jax.numpy as jnp
from jax import lax
from jax.experimental import pallas as pl
from jax.experimental.pallas import tpu as pltpu

F32 = jnp.float32
BF16 = jnp.bfloat16
HI = lax.Precision.HIGHEST

D_MODEL = 1024
DEPTH = 2
A_HEADS = 4
A_HEAD_DIM = 128
A_WIDTH = 512
CONV_K = 4
CHUNK = 64
B_Q_HEADS = 8
B_KV_HEADS = 2
B_HEAD_DIM = 64
B_GROUP = 4
B_WIDTH = 512
B_KV_WIDTH = 128
BLOCK = 128
IN_COLS = 3336
DEEPNORM_ALPHA = (2 * DEPTH) ** 0.25
LN_EPS = 1e-5
RMS_EPS = 1e-6
L2_EPS = 1e-6
ADAM_LR = 0.001
ADAM_B1 = 0.9
ADAM_B2 = 0.999
ADAM_EPS = 1e-08
ADAM_WD = 0.01
ADAM_STEP = 10

N_SHARD = 4
IN_SHARD = IN_COLS // N_SHARD
OUT_SHARD = D_MODEL // N_SHARD
CONV_SHARD = 3 * A_WIDTH // N_SHARD

P_COLS = 3456
C_PRE = 0
C_ZA = 1536
C_QB = 2048
C_ZB = 2560
C_KB = 3072
C_VB = 3200
C_BG = 3328
LANE = 128
SUBLANE = 8
VMEM_LIMIT = 56 * 1024 * 1024
ALIBI = tuple(2.0 ** (-8.0 * (h + 1) / B_Q_HEADS) for h in range(B_Q_HEADS))
NEG = -1e30


def _cp(*sem):
    return pltpu.CompilerParams(dimension_semantics=sem, vmem_limit_bytes=VMEM_LIMIT)


def _dot(a, b):
    return jnp.dot(a.astype(BF16), b.astype(BF16), preferred_element_type=F32)


def _dot_nt(a, b):
    return lax.dot_general(a.astype(BF16), b.astype(BF16), (((1,), (1,)), ((), ())),
                           preferred_element_type=F32)


def _dot_tn(a, b):
    return lax.dot_general(a.astype(BF16), b.astype(BF16), (((0,), (0,)), ((), ())),
                           preferred_element_type=F32)


def _dot_hi(a, b):
    return jnp.dot(a, b, precision=HI, preferred_element_type=F32)


def _sigmoid(x):
    return jax.nn.sigmoid(x)


def _silu(x):
    return x * _sigmoid(x)


def _dsilu(x):
    s = _sigmoid(x)
    return s * (1.0 + x * (1.0 - s))


def _softplus(x):
    return jnp.maximum(x, 0.0) + jnp.log(1.0 + jnp.exp(-jnp.abs(x)))


def _shift_down(cur, before, s):
    if s == 0:
        return cur
    r = pltpu.roll(cur, s, 0)
    rb = pltpu.roll(before, s, 0)
    row = lax.broadcasted_iota(jnp.int32, before.shape, 0)
    head = jnp.where(row < s, rb, r[0:SUBLANE])
    return jnp.concatenate([head, r[SUBLANE:]], axis=0)


def _shift_up(cur, after, s):
    if s == 0:
        return cur
    n = cur.shape[0]
    r = pltpu.roll(cur, n - s, 0)
    ra = pltpu.roll(after, SUBLANE - s, 0)
    row = lax.broadcasted_iota(jnp.int32, after.shape, 0)
    tail = jnp.where(row >= SUBLANE - s, ra, r[n - SUBLANE:])
    return jnp.concatenate([r[:n - SUBLANE], tail], axis=0)


def _conv_fwd(cur, before, w):
    acc = cur * w[CONV_K - 1:CONV_K, :]
    for s in range(1, CONV_K):
        acc = acc + _shift_down(cur, before, s) * w[CONV_K - 1 - s:CONV_K - s, :]
    return acc


def _matmul(a, b, *, tm, tn, name):
    m, k = a.shape
    n = b.shape[1]

    def body(a_ref, b_ref, o_ref):
        o_ref[...] = jnp.dot(a_ref[...].astype(BF16), b_ref[...], preferred_element_type=F32)

    return pl.pallas_call(
        body, name=name, grid=(m // tm, n // tn),
        in_specs=[pl.BlockSpec((tm, k), lambda i, j: (i, 0)), pl.BlockSpec((k, tn), lambda i, j: (0, j))],
        out_specs=pl.BlockSpec((tm, tn), lambda i, j: (i, j)),
        out_shape=jax.ShapeDtypeStruct((m, n), F32),
        compiler_params=_cp("parallel", "parallel"))(a, b)


def _dn_pre(h, conv_w, par, *, tt, name):
    t = h.shape[0]
    cw = 3 * A_WIDTH
    hb = tt // SUBLANE

    def body(pre_ref, halo_ref, bgi_ref, cw_ref, par_ref, q_ref, k_ref, v_ref, bg_ref, bgt_ref):
        i = pl.program_id(0)
        cur = pre_ref[...]
        before = jnp.where(i > 0, halo_ref[...], 0.0)
        s = _silu(_conv_fwd(cur, before, cw_ref[...]))
        for hd in range(A_HEADS):
            sl = slice(hd * LANE, (hd + 1) * LANE)
            tq = s[:, hd * LANE:(hd + 1) * LANE]
            q_ref[:, sl] = tq * (lax.rsqrt(jnp.sum(tq * tq, -1, keepdims=True) + L2_EPS) * (A_HEAD_DIM ** -0.5))
            tk = s[:, A_WIDTH + hd * LANE:A_WIDTH + (hd + 1) * LANE]
            k_ref[:, sl] = tk * lax.rsqrt(jnp.sum(tk * tk, -1, keepdims=True) + L2_EPS)
        v_ref[...] = s[:, 2 * A_WIDTH:]
        raw = bgi_ref[...]
        lane = lax.broadcasted_iota(jnp.int32, raw.shape, 1)
        is_a = (lane >= A_HEADS) & (lane < 2 * A_HEADS)
        g = jnp.where(is_a, -jnp.exp(par_ref[0:1, :]) * _softplus(raw + par_ref[1:2, :]), 0.0)
        gc = _dot_hi(_chunk_tri(tt, lower=True), g)
        bg = jnp.where(lane < A_HEADS, _sigmoid(raw), gc)
        bg_ref[...] = bg
        bgt_ref[...] = jnp.transpose(bg)[0:SUBLANE, :]

    wide = jax.ShapeDtypeStruct((t, A_WIDTH), F32)
    return pl.pallas_call(
        body, name=name, grid=(t // tt,),
        in_specs=[pl.BlockSpec((tt, cw), lambda i: (i, 0)),
                  pl.BlockSpec((SUBLANE, cw), lambda i: (jnp.maximum(i * hb - 1, 0), 0)),
                  pl.BlockSpec((tt, LANE), lambda i: (i, C_BG // LANE)),
                  pl.BlockSpec((CONV_K, cw), lambda i: (0, 0)),
                  pl.BlockSpec((SUBLANE, LANE), lambda i: (0, 0))],
        out_specs=[pl.BlockSpec((tt, A_WIDTH), lambda i: (i, 0))] * 3
        + [pl.BlockSpec((tt, LANE), lambda i: (i, 0)), pl.BlockSpec((SUBLANE, tt), lambda i: (0, i))],
        out_shape=[wide, wide, wide, jax.ShapeDtypeStruct((t, LANE), F32), jax.ShapeDtypeStruct((SUBLANE, t), F32)],
        compiler_params=_cp("parallel"))(h, h, h, conv_w, par)


def _chunk_tri(n, lower):
    r = lax.broadcasted_iota(jnp.int32, (n, n), 0)
    c = lax.broadcasted_iota(jnp.int32, (n, n), 1)
    shift = CHUNK.bit_length() - 1
    same = jnp.right_shift(r, shift) == jnp.right_shift(c, shift)
    return (same & ((c <= r) if lower else (c >= r))).astype(F32)


def _chunk_masks():
    r = lax.broadcasted_iota(jnp.int32, (CHUNK, CHUNK), 0)
    c = lax.broadcasted_iota(jnp.int32, (CHUNK, CHUNK), 1)
    return r >= c, r > c, r == c


def _split(a):
    hi = a.astype(BF16)
    return hi, (a - hi.astype(F32)).astype(BF16)


def _dot3(a, b):
    (ah, al), (bh, bl) = a, b
    d = lambda p, q: jnp.dot(p, q, preferred_element_type=F32)
    return d(ah, bh) + (d(ah, bl) + d(al, bh))


def _tri_inv_many(a_list, eye):
    d = lambda p, q: jnp.dot(p, q, preferred_element_type=F32)
    p = [(-a).astype(BF16) for a in a_list]
    tm = [eye - a for a in a_list]
    for _ in range(5):
        pf = [d(pi, pi) for pi in p]
        p = [x.astype(BF16) for x in pf]
        tm = [t + d(t.astype(BF16), pi) for t, pi in zip(tm, p)]
    ms = [_split(eye + a) for a in a_list]
    res = [eye - _dot3(m, _split(t)) for m, t in zip(ms, tm)]
    return [t + d(t.astype(BF16), r.astype(BF16)) for t, r in zip(tm, res)]


def _chunk_gates(bg_v, bgt_v, hd):
    return (bg_v[:, hd:hd + 1], bg_v[:, A_HEADS + hd:A_HEADS + hd + 1],
            None if bgt_v is None else bgt_v[A_HEADS + hd:A_HEADS + hd + 1, :])


WY_ROWS = 256
SCAN_ROWS = 128
WY_GROUP = 2


def _dn_wy(q, k, v, bg, bgt, *, name):
    t = q.shape[0]
    rows = WY_ROWS

    def body(q_ref, k_ref, v_ref, bg_ref, bgt_ref, u_ref, w_ref, tm_ref, qk_ref):
        causal, strict, diag = _chunk_masks()
        eye = diag.astype(F32)
        for c0 in range(0, rows // CHUNK, WY_GROUP):
            items = [(c, hd) for c in range(c0, c0 + WY_GROUP) for hd in range(A_HEADS)]
            rs = lambda c: slice(c * CHUNK, (c + 1) * CHUNK)
            sl = lambda hd: slice(hd * LANE, (hd + 1) * LANE)
            hs = lambda hd: slice(hd * CHUNK, (hd + 1) * CHUNK)
            gates = [_chunk_gates(bg_ref[rs(c), :], bgt_ref[:, rs(c)], hd) for c, hd in items]
            dms = [jnp.exp(jnp.where(causal, gcol - grow, NEG)) for _, gcol, grow in gates]
            kbs = [k_ref[rs(c), sl(hd)] * g[0] for (c, hd), g in zip(items, gates)]
            a_list = [jnp.where(strict, _dot_nt(kb, k_ref[rs(c), sl(hd)]) * dm, 0.0)
                      for (c, hd), kb, dm in zip(items, kbs, dms)]
            for (c, hd), dm in zip(items, dms):
                qk_ref[rs(c), hs(hd)] = jnp.where(
                    causal, _dot_nt(q_ref[rs(c), sl(hd)], k_ref[rs(c), sl(hd)]) * dm, 0.0)
            tms = _tri_inv_many(a_list, eye)
            for (c, hd), g, kb, tmat in zip(items, gates, kbs, tms):
                tm_ref[rs(c), hs(hd)] = tmat
                u_ref[rs(c), sl(hd)] = _dot(tmat, v_ref[rs(c), sl(hd)] * g[0])
                w_ref[rs(c), sl(hd)] = _dot(tmat, kb * jnp.exp(g[1])).astype(BF16)

    blk = pl.BlockSpec((rows, A_WIDTH), lambda i: (i, 0))
    half = pl.BlockSpec((rows, A_HEADS * CHUNK), lambda i: (i, 0))
    return pl.pallas_call(
        body, name=name, grid=(t // rows,),
        in_specs=[blk, blk, blk, pl.BlockSpec((rows, LANE), lambda i: (i, 0)),
                  pl.BlockSpec((SUBLANE, rows), lambda i: (0, i))],
        out_specs=[blk, blk, half, half],
        out_shape=[jax.ShapeDtypeStruct((t, A_WIDTH), F32), jax.ShapeDtypeStruct((t, A_WIDTH), BF16),
                   jax.ShapeDtypeStruct((t, A_HEADS * CHUNK), F32), jax.ShapeDtypeStruct((t, A_HEADS * CHUNK), F32)],
        compiler_params=_cp("parallel"))(q, k, v, bg, bgt)


def _dn_scan_fwd(q, k, u, w, qk, bg, *, name):
    t = q.shape[0]
    rows = SCAN_ROWS
    per = rows // CHUNK

    def body(q_ref, k_ref, u_ref, w_ref, qk_ref, bg_ref, o_ref, vn_ref, s_ref, state):
        @pl.when(pl.program_id(0) == 0)
        def _():
            state[...] = jnp.zeros_like(state)

        heads = range(A_HEADS)
        sl = lambda hd: slice(hd * LANE, (hd + 1) * LANE)
        s_cur = [state[hd] for hd in heads]
        for c in range(per):
            rs = slice(c * CHUNK, (c + 1) * CHUNK)
            bg_v = bg_ref[rs, :]
            gcols = [_chunk_gates(bg_v, None, hd)[1] for hd in heads]
            glasts = [gc[CHUNK - 1:CHUNK, :] for gc in gcols]
            for hd in heads:
                s_ref[c, hd] = s_cur[hd]
            vns = [u_ref[rs, sl(hd)] - _dot(w_ref[rs, sl(hd)], s_cur[hd]) for hd in heads]
            qss = [_dot(q_ref[rs, sl(hd)] * jnp.exp(gcols[hd]), s_cur[hd]) for hd in heads]
            s_cur = [s_cur[hd] * jnp.exp(glasts[hd])
                     + _dot_tn(k_ref[rs, sl(hd)] * jnp.exp(glasts[hd] - gcols[hd]), vns[hd]) for hd in heads]
            for hd in heads:
                vn_ref[rs, sl(hd)] = vns[hd]
                o_ref[rs, sl(hd)] = qss[hd] + _dot(qk_ref[rs, hd * CHUNK:(hd + 1) * CHUNK], vns[hd])
        for hd in heads:
            state[hd] = s_cur[hd]

    blk = pl.BlockSpec((rows, A_WIDTH), lambda i: (i, 0))
    half = pl.BlockSpec((rows, A_HEADS * CHUNK), lambda i: (i, 0))
    wide = jax.ShapeDtypeStruct((t, A_WIDTH), F32)
    return pl.pallas_call(
        body, name=name, grid=(t // rows,),
        in_specs=[blk, blk, blk, blk, half, pl.BlockSpec((rows, LANE), lambda i: (i, 0))],
        out_specs=[blk, blk, pl.BlockSpec((per, A_HEADS, LANE, LANE), lambda i: (i, 0, 0, 0))],
        out_shape=[wide, wide, jax.ShapeDtypeStruct((t // CHUNK, A_HEADS, LANE, LANE), F32)],
        scratch_shapes=[pltpu.VMEM((A_HEADS, LANE, LANE), F32)],
        compiler_params=_cp("arbitrary"))(q, k, u, w, qk, bg)


def _swa_neg_dist(n_blk):
    qi = lax.broadcasted_iota(jnp.int32, (BLOCK, 2 * BLOCK), 0)
    si = lax.broadcasted_iota(jnp.int32, (BLOCK, 2 * BLOCK), 1)
    dist = qi + BLOCK - si
    mask = (dist >= 0) & (dist < BLOCK) & ((si >= BLOCK) | (n_blk > 0))
    return jnp.where(mask, -dist.astype(F32), NEG)


def _stack_heads(ref, hk):
    return jnp.concatenate([ref[:, h * B_HEAD_DIM:(h + 1) * B_HEAD_DIM]
                            for h in range(hk * B_GROUP, (hk + 1) * B_GROUP)], axis=0)


def _swa_group_probs(q_ref, sk_ref, kh, vh, neg_dist, hk):
    heads = range(hk * B_GROUP, (hk + 1) * B_GROUP)
    qs = _stack_heads(q_ref, hk) * (B_HEAD_DIM ** -0.5)
    bias = jnp.concatenate([ALIBI[h] * neg_dist for h in heads], axis=0)
    sink = jnp.concatenate([jnp.broadcast_to(sk_ref[h:h + 1, 0:1], (BLOCK, 1)) for h in heads], axis=0)
    s = _dot_nt(qs, kh) + bias
    m = jnp.maximum(jnp.max(s, axis=-1, keepdims=True), sink)
    p = jnp.exp(s - m)
    vext = jnp.concatenate([vh.astype(BF16), jnp.ones((2 * BLOCK, B_HEAD_DIM), BF16)], axis=1)
    oe = jnp.dot(p.astype(BF16), vext, preferred_element_type=F32)
    ps = jnp.exp(sink - m)
    inv = 1.0 / (oe[:, B_HEAD_DIM:B_HEAD_DIM + 1] + ps)
    return qs, p * inv, ps * inv, oe[:, :B_HEAD_DIM] * inv


def _swa_specs():
    qspec = lambda c0: pl.BlockSpec((BLOCK, B_WIDTH), lambda i: (i, c0 // B_WIDTH))
    cur = lambda c0: pl.BlockSpec((BLOCK, LANE), lambda i: (i, c0 // LANE))
    prev = lambda c0: pl.BlockSpec((BLOCK, LANE), lambda i: (jnp.maximum(i - 1, 0), c0 // LANE))
    return qspec, cur, prev


def _swa_fwd(h, sinks_b, *, name):
    t = h.shape[0]
    qspec, cur, prev = _swa_specs()

    def body(q_ref, kc_ref, kp_ref, vc_ref, vp_ref, sk_ref, o_ref):
        n_blk = pl.program_id(0)
        kband = jnp.concatenate([kp_ref[...], kc_ref[...]], axis=0)
        vband = jnp.concatenate([vp_ref[...], vc_ref[...]], axis=0)
        neg_dist = _swa_neg_dist(n_blk)
        for hk in range(B_KV_HEADS):
            ksl = slice(hk * B_HEAD_DIM, (hk + 1) * B_HEAD_DIM)
            _, _, _, o = _swa_group_probs(q_ref, sk_ref, kband[:, ksl], vband[:, ksl], neg_dist, hk)
            for g in range(B_GROUP):
                hq = hk * B_GROUP + g
                o_ref[:, hq * B_HEAD_DIM:(hq + 1) * B_HEAD_DIM] = o[g * BLOCK:(g + 1) * BLOCK]

    return pl.pallas_call(
        body, name=name, grid=(t // BLOCK,),
        in_specs=[qspec(C_QB), cur(C_KB), prev(C_KB), cur(C_VB), prev(C_VB),
                  pl.BlockSpec((B_Q_HEADS, LANE), lambda i: (0, 0))],
        out_specs=pl.BlockSpec((BLOCK, B_WIDTH), lambda i: (i, 0)),
        out_shape=jax.ShapeDtypeStruct((t, B_WIDTH), F32),
        compiler_params=_cp("parallel"))(h, h, h, h, h, sinks_b)


def _rms_gate(o, za, nw):
    outs = []
    for hd in range(A_HEADS):
        oh = o[:, hd * LANE:(hd + 1) * LANE]
        r = lax.rsqrt(jnp.mean(oh * oh, -1, keepdims=True) + RMS_EPS)
        outs.append(oh * r * nw)
    return jnp.concatenate(outs, axis=1) * _silu(za)


def _out_ln(x, oa, ob, h, norm_w, w_out, ln_g, ln_b, *, tm, name):
    t = x.shape[0]

    def body(x_ref, oa_ref, ob_ref, za_ref, zb_ref, nw_ref, w_ref, g_ref, b_ref, xn_ref, mx_ref, r_ref):
        ya = _rms_gate(oa_ref[...], za_ref[...], nw_ref[...])
        yb = ob_ref[...] * _silu(zb_ref[...])
        mixed = jnp.concatenate([ya, yb], axis=1).astype(BF16)
        mx_ref[...] = mixed
        r = DEEPNORM_ALPHA * x_ref[...] + jnp.dot(mixed, w_ref[...], preferred_element_type=F32)
        r_ref[...] = r
        mu = jnp.mean(r, -1, keepdims=True)
        xc = r - mu
        var = jnp.mean(xc * xc, -1, keepdims=True)
        xn_ref[...] = xc * lax.rsqrt(var + LN_EPS) * g_ref[...] + b_ref[...]

    row = lambda w, c: pl.BlockSpec((tm, w), lambda i: (i, c))
    full = lambda a, b: pl.BlockSpec((a, b), lambda i: (0, 0))
    return pl.pallas_call(
        body, name=name, grid=(t // tm,),
        in_specs=[row(D_MODEL, 0), row(A_WIDTH, 0), row(B_WIDTH, 0), row(A_WIDTH, C_ZA // A_WIDTH),
                  row(B_WIDTH, C_ZB // B_WIDTH), full(1, LANE), full(D_MODEL, D_MODEL), full(1, D_MODEL), full(1, D_MODEL)],
        out_specs=[row(D_MODEL, 0), row(D_MODEL, 0), row(D_MODEL, 0)],
        out_shape=[jax.ShapeDtypeStruct((t, D_MODEL), F32), jax.ShapeDtypeStruct((t, D_MODEL), BF16),
                   jax.ShapeDtypeStruct((t, D_MODEL), F32)],
        compiler_params=_cp("parallel"))(x, oa, ob, h, h, norm_w, w_out, ln_g, ln_b)


def _layer_fwd(x, wp, conv_w, par, sinks_b, norm_w, w_out_bf, ln_g, ln_b, l):
    h = _matmul(x, wp, tm=512, tn=1152, name=f"in_proj_{l}")
    q, k, v, bg, bgt = _dn_pre(h, conv_w, par, tt=512, name=f"dn_pre_{l}")
    u, w, tmat, qk = _dn_wy(q, k, v, bg, bgt, name=f"dn_wy_{l}")
    oa, vn, s_all = _dn_scan_fwd(q, k, u, w, qk, bg, name=f"dn_scan_{l}")
    ob = _swa_fwd(h, sinks_b, name=f"swa_fwd_{l}")
    xn, mixed, r = _out_ln(x, oa, ob, h, norm_w, w_out_bf, ln_g, ln_b, tm=256, name=f"out_ln_{l}")
    return xn, dict(x=x, h=h, q=q, k=k, v=v, bg=bg, bgt=bgt, w=w, tmat=tmat, qk=qk, vn=vn, oa=oa, s_all=s_all,
                    mixed=mixed, r=r)


def _loss_grad(xn, target, *, tm, name):
    t = xn.shape[0]

    def body(x_ref, t_ref, d_ref, l_ref):
        @pl.when(pl.program_id(0) == 0)
        def _():
            l_ref[...] = jnp.zeros_like(l_ref)

        err = x_ref[...] - t_ref[...]
        d_ref[...] = err * (1.0 / D_MODEL)
        l_ref[...] += 0.5 / D_MODEL * jnp.sum(err * err)

    row = pl.BlockSpec((tm, D_MODEL), lambda i: (i, 0))
    return pl.pallas_call(
        body, name=name, grid=(t // tm,), in_specs=[row, row],
        out_specs=[row, pl.BlockSpec((SUBLANE, LANE), lambda i: (0, 0))],
        out_shape=[jax.ShapeDtypeStruct((t, D_MODEL), F32), jax.ShapeDtypeStruct((SUBLANE, LANE), F32)],
        compiler_params=_cp("arbitrary"))(xn, target)


def _ln_out_bwd(dxn, r, mixed, ln_g, w_out, *, tm, name):
    t = dxn.shape[0]

    def body(dxn_ref, r_ref, mx_ref, g_ref, w_ref, dr_ref, dm_ref, dw_ref, dg_ref, db_ref):
        @pl.when(pl.program_id(0) == 0)
        def _():
            dw_ref[...] = jnp.zeros_like(dw_ref)
            dg_ref[...] = jnp.zeros_like(dg_ref)
            db_ref[...] = jnp.zeros_like(db_ref)

        rr = r_ref[...]
        xc = rr - jnp.mean(rr, -1, keepdims=True)
        rstd = lax.rsqrt(jnp.mean(xc * xc, -1, keepdims=True) + LN_EPS)
        xhat = xc * rstd
        dxn_v = dxn_ref[...]
        dxh = dxn_v * g_ref[...]
        dr = rstd * (dxh - jnp.mean(dxh, -1, keepdims=True) - xhat * jnp.mean(dxh * xhat, -1, keepdims=True))
        dr_ref[...] = dr
        dg_ref[...] += jnp.sum(dxn_v * xhat, axis=0, keepdims=True)
        db_ref[...] += jnp.sum(dxn_v, axis=0, keepdims=True)
        drb = dr.astype(BF16)
        dm_ref[...] = _dot_nt(drb, w_ref[...])
        dw_ref[...] += _dot_tn(mx_ref[...], drb)

    row = pl.BlockSpec((tm, D_MODEL), lambda i: (i, 0))
    full = lambda a, b: pl.BlockSpec((a, b), lambda i: (0, 0))
    big = jax.ShapeDtypeStruct((t, D_MODEL), F32)
    vec = jax.ShapeDtypeStruct((1, D_MODEL), F32)
    return pl.pallas_call(
        body, name=name, grid=(t // tm,),
        in_specs=[row, row, row, full(1, D_MODEL), full(D_MODEL, D_MODEL)],
        out_specs=[row, row, full(D_MODEL, D_MODEL), full(1, D_MODEL), full(1, D_MODEL)],
        out_shape=[big, big, jax.ShapeDtypeStruct((D_MODEL, D_MODEL), F32), vec, vec],
        compiler_params=_cp("arbitrary"))(dxn, r, mixed, ln_g, w_out)


def _dn_post_bwd(dm, oa, h, norm_w, *, tm, name):
    t = oa.shape[0]

    def body(dy_ref, o_ref, za_ref, nw_ref, do_ref, dza_ref, dnw_ref):
        @pl.when(pl.program_id(0) == 0)
        def _():
            dnw_ref[...] = jnp.zeros_like(dnw_ref)

        nw = nw_ref[...]
        dnw = jnp.zeros_like(nw)
        for hd in range(A_HEADS):
            sl = slice(hd * LANE, (hd + 1) * LANE)
            oh, za, dy = o_ref[:, sl], za_ref[:, sl], dy_ref[:, sl]
            rs = lax.rsqrt(jnp.mean(oh * oh, -1, keepdims=True) + RMS_EPS)
            nrm = oh * rs
            dza_ref[:, sl] = dy * nrm * nw * _dsilu(za)
            dn = dy * _silu(za)
            dnw = dnw + jnp.sum(dn * nrm, axis=0, keepdims=True)
            dnn = dn * nw
            do_ref[:, sl] = rs * dnn - oh * (rs * rs * rs) * jnp.mean(dnn * oh, -1, keepdims=True)
        dnw_ref[...] += dnw

    row = lambda c: pl.BlockSpec((tm, A_WIDTH), lambda i: (i, c))
    wide = jax.ShapeDtypeStruct((t, A_WIDTH), F32)
    return pl.pallas_call(
        body, name=name, grid=(t // tm,),
        in_specs=[row(0), row(0), row(C_ZA // A_WIDTH), pl.BlockSpec((1, LANE), lambda i: (0, 0))],
        out_specs=[row(0), row(0), pl.BlockSpec((1, LANE), lambda i: (0, 0))],
        out_shape=[wide, wide, jax.ShapeDtypeStruct((1, LANE), F32)],
        compiler_params=_cp("arbitrary"))(dm, oa, h, norm_w)


def _dn_scan_bwd(q, k, w, qk, bg, do, *, name):
    t = q.shape[0]
    rows = SCAN_ROWS
    per = rows // CHUNK
    n = t // rows

    def body(q_ref, k_ref, w_ref, qk_ref, bg_ref, do_ref, dvn_ref, ds_ref, dstate):
        @pl.when(pl.program_id(0) == 0)
        def _():
            dstate[...] = jnp.zeros_like(dstate)

        heads = range(A_HEADS)
        sl = lambda hd: slice(hd * LANE, (hd + 1) * LANE)
        ds_cur = [dstate[hd] for hd in heads]
        for c in reversed(range(per)):
            rs = slice(c * CHUNK, (c + 1) * CHUNK)
            bg_v = bg_ref[rs, :]
            gcols = [_chunk_gates(bg_v, None, hd)[1] for hd in heads]
            glasts = [gc[CHUNK - 1:CHUNK, :] for gc in gcols]
            for hd in heads:
                ds_ref[c, hd] = ds_cur[hd]
            pdo = [_dot_tn(qk_ref[rs, hd * CHUNK:(hd + 1) * CHUNK], do_ref[rs, sl(hd)]) for hd in heads]
            qdo = [_dot_tn(q_ref[rs, sl(hd)] * jnp.exp(gcols[hd]), do_ref[rs, sl(hd)]) for hd in heads]
            dvns = [pdo[hd] + _dot(k_ref[rs, sl(hd)] * jnp.exp(glasts[hd] - gcols[hd]), ds_cur[hd]) for hd in heads]
            ds_cur = [qdo[hd] + jnp.exp(glasts[hd]) * ds_cur[hd] - _dot_tn(w_ref[rs, sl(hd)], dvns[hd])
                      for hd in heads]
            for hd in heads:
                dvn_ref[rs, sl(hd)] = dvns[hd]
        for hd in heads:
            dstate[hd] = ds_cur[hd]

    blk = pl.BlockSpec((rows, A_WIDTH), lambda i: (n - 1 - i, 0))
    return pl.pallas_call(
        body, name=name, grid=(n,),
        in_specs=[blk, blk, blk, pl.BlockSpec((rows, A_HEADS * CHUNK), lambda i: (n - 1 - i, 0)),
                  pl.BlockSpec((rows, LANE), lambda i: (n - 1 - i, 0)), blk],
        out_specs=[blk, pl.BlockSpec((per, A_HEADS, LANE, LANE), lambda i: (n - 1 - i, 0, 0, 0))],
        out_shape=[jax.ShapeDtypeStruct((t, A_WIDTH), F32),
                   jax.ShapeDtypeStruct((t // CHUNK, A_HEADS, LANE, LANE), F32)],
        scratch_shapes=[pltpu.VMEM((A_HEADS, LANE, LANE), F32)],
        compiler_params=_cp("arbitrary"))(q, k, w, qk, bg, do)


def _dn_chunk_bwd(q, k, v, vn, tmat, qk, bg, bgt, s_all, ds_all, dvn, do, *, name):
    t = q.shape[0]
    rows = WY_ROWS
    per = rows // CHUNK

    def body(q_ref, k_ref, v_ref, vn_ref, tm_ref, qk_ref, bg_ref, bgt_ref, s_ref, ds_ref, dvn_ref, do_ref,
             dq_ref, dk_ref, dv_ref, dbg_ref, dbgt_ref):
        causal, strict, _ = _chunk_masks()
        lane = lax.broadcasted_iota(jnp.int32, (CHUNK, LANE), 1)
        rowi = lax.broadcasted_iota(jnp.int32, (CHUNK, 1), 0)
        sub = lax.broadcasted_iota(jnp.int32, (SUBLANE, CHUNK), 0)
        rs = lambda c: slice(c * CHUNK, (c + 1) * CHUNK)
        sl = lambda hd: slice(hd * LANE, (hd + 1) * LANE)
        hs = lambda hd: slice(hd * CHUNK, (hd + 1) * CHUNK)
        for c0 in range(0, per, WY_GROUP):
            items = [(c, hd) for c in range(c0, c0 + WY_GROUP) for hd in range(A_HEADS)]
            at = lambda ref: [ref[rs(c), sl(hd)] for c, hd in items]
            qs, ks, vs, dos, vns, dvns = at(q_ref), at(k_ref), at(v_ref), at(do_ref), at(vn_ref), at(dvn_ref)
            tmhs = [tm_ref[rs(c), hs(hd)] for c, hd in items]
            ps = [qk_ref[rs(c), hs(hd)] for c, hd in items]
            gates = [_chunk_gates(bg_ref[rs(c), :], bgt_ref[:, rs(c)], hd) for c, hd in items]
            betas = [g[0] for g in gates]
            gcols = [g[1] for g in gates]
            dmats = [jnp.exp(jnp.where(causal, g[1] - g[2], NEG)) for g in gates]
            es = [jnp.exp(gc) for gc in gcols]
            glasts = [gc[CHUNK - 1:CHUNK, :] for gc in gcols]
            eks = [jnp.exp(gl - gc) for gl, gc in zip(glasts, gcols)]
            kbs = [kh * b for kh, b in zip(ks, betas)]
            vbs = [vh * b for vh, b in zip(vs, betas)]
            kbes = [kb * e for kb, e in zip(kbs, es)]

            a_s = [jnp.where(strict, _dot_nt(kb, kh) * dm, 0.0) for kb, kh, dm in zip(kbs, ks, dmats)]
            dps = [jnp.where(causal, _dot_nt(doh, vnh), 0.0) for doh, vnh in zip(dos, vns)]
            dqds = [_dot_nt(doh, s_ref[c, hd]) for doh, (c, hd) in zip(dos, items)]
            dkds = [_dot_nt(vnh, ds_ref[c, hd]) for vnh, (c, hd) in zip(vns, items)]
            dws = [-_dot_nt(dvnh, s_ref[c, hd]) for dvnh, (c, hd) in zip(dvns, items)]
            dvbs = [_dot_tn(tmh, dvnh) for tmh, dvnh in zip(tmhs, dvns)]
            dgts = [jnp.sum(s_ref[c, hd] * ds_ref[c, hd], keepdims=True) for c, hd in items]
            dts = [_dot_nt(dvnh, vb) + _dot_nt(dw, kbe) for dvnh, vb, dw, kbe in zip(dvns, vbs, dws, kbes)]
            dkbes = [_dot_tn(tmh, dw) for tmh, dw in zip(tmhs, dws)]
            xs = [_dot_nt(dt, tmh) for dt, tmh in zip(dts, tmhs)]
            das = [jnp.where(strict, -_dot_tn(tmh, x), 0.0) for tmh, x in zip(tmhs, xs)]
            dmas = [da * dm for da, dm in zip(das, dmats)]
            dmps = [dp * dm for dp, dm in zip(dps, dmats)]
            dkbs = [_dot(dma, kh) + dkbe * e for dma, kh, dkbe, e in zip(dmas, ks, dkbes, es)]
            for i, (c, hd) in enumerate(items):
                dq_ref[rs(c), sl(hd)] = _dot(dmps[i], ks[i]) + dqds[i] * es[i]
                dk_ref[rs(c), sl(hd)] = (_dot_tn(dmas[i], kbs[i]) + _dot_tn(dmps[i], qs[i]) + dkds[i] * eks[i]
                                         + dkbs[i] * betas[i])
                dv_ref[rs(c), sl(hd)] = dvbs[i] * betas[i]
            for c in range(c0, c0 + WY_GROUP):
                acc = jnp.zeros((CHUNK, LANE), F32)
                acc_t = jnp.zeros((SUBLANE, CHUNK), F32)
                for i, (ci, hd) in enumerate(items):
                    if ci != c:
                        continue
                    gmat = das[i] * a_s[i] + dps[i] * ps[i]
                    rk = jnp.sum(dkds[i] * ks[i], -1, keepdims=True) * eks[i]
                    de = (jnp.sum(dqds[i] * qs[i], -1, keepdims=True)
                          + jnp.sum(dkbes[i] * kbs[i], -1, keepdims=True))
                    dglast = jnp.sum(rk, keepdims=True) + dgts[i] * jnp.exp(glasts[i])
                    dgc = (jnp.sum(gmat, -1, keepdims=True) + de * es[i] - rk
                           + jnp.where(rowi == CHUNK - 1, dglast, 0.0))
                    dbeta = (jnp.sum(dkbs[i] * ks[i], -1, keepdims=True)
                             + jnp.sum(dvbs[i] * vs[i], -1, keepdims=True))
                    acc = acc + jnp.where(lane == hd, dbeta, 0.0) + jnp.where(lane == A_HEADS + hd, dgc, 0.0)
                    acc_t = acc_t + jnp.where(sub == A_HEADS + hd, -jnp.sum(gmat, axis=0, keepdims=True), 0.0)
                dbg_ref[rs(c), :] = acc
                dbgt_ref[:, rs(c)] = acc_t

    blk = pl.BlockSpec((rows, A_WIDTH), lambda i: (i, 0))
    half = pl.BlockSpec((rows, A_HEADS * CHUNK), lambda i: (i, 0))
    col = pl.BlockSpec((rows, LANE), lambda i: (i, 0))
    rowf = pl.BlockSpec((SUBLANE, rows), lambda i: (0, i))
    st = pl.BlockSpec((per, A_HEADS, LANE, LANE), lambda i: (i, 0, 0, 0))
    wide = jax.ShapeDtypeStruct((t, A_WIDTH), F32)
    return pl.pallas_call(
        body, name=name, grid=(t // rows,),
        in_specs=[blk, blk, blk, blk, half, half, col, rowf, st, st, blk, blk],
        out_specs=[blk, blk, blk, col, rowf],
        out_shape=[wide, wide, wide, jax.ShapeDtypeStruct((t, LANE), F32), jax.ShapeDtypeStruct((SUBLANE, t), F32)],
        compiler_params=_cp("parallel"))(q, k, v, vn, tmat, qk, bg, bgt, s_all, ds_all, dvn, do)


def _dn_pre_bwd(h, conv_w, par, dq, dk, dv, dbg, dbgt, *, tt, name):
    t = h.shape[0]
    cw = 3 * A_WIDTH
    hb = tt // SUBLANE

    def body(pre_ref, halo_ref, bgi_ref, cw_ref, par_ref, dq_ref, dk_ref, dv_ref, dbg_ref, dbgt_ref,
             dc_ref, dbgi_ref, dpar_ref):
        i = pl.program_id(0)

        @pl.when(i == 0)
        def _():
            dpar_ref[...] = jnp.zeros_like(dpar_ref)

        cur = pre_ref[...]
        before = jnp.where(i > 0, halo_ref[...], 0.0)
        c = _conv_fwd(cur, before, cw_ref[...])
        s = _silu(c)
        ds = _dsilu(c)
        for hd in range(A_HEADS):
            sl = slice(hd * LANE, (hd + 1) * LANE)
            for base, d_ref, scale in ((0, dq_ref, A_HEAD_DIM ** -0.5), (A_WIDTH, dk_ref, 1.0)):
                csl = slice(base + hd * LANE, base + (hd + 1) * LANE)
                tq = s[:, base + hd * LANE:base + (hd + 1) * LANE]
                dy = d_ref[:, sl]
                rq = lax.rsqrt(jnp.sum(tq * tq, -1, keepdims=True) + L2_EPS)
                dtq = scale * (rq * dy - tq * (rq * rq * rq) * jnp.sum(dy * tq, -1, keepdims=True))
                dc_ref[:, csl] = dtq * ds[:, base + hd * LANE:base + (hd + 1) * LANE]
        dc_ref[:, 2 * A_WIDTH:] = dv_ref[...] * ds[:, 2 * A_WIDTH:]
        raw = bgi_ref[...]
        lane = lax.broadcasted_iota(jnp.int32, raw.shape, 1)
        is_b = lane < A_HEADS
        is_a = (lane >= A_HEADS) & (lane < 2 * A_HEADS)
        rows_t = jnp.concatenate([dbgt_ref[...], jnp.zeros((LANE - SUBLANE, tt), F32)], axis=0)
        dbg_v = dbg_ref[...] + jnp.where(is_a, jnp.transpose(rows_t), 0.0)
        dbg_v = jnp.where(is_a, _dot_hi(_chunk_tri(tt, lower=False), jnp.where(is_a, dbg_v, 0.0)), dbg_v)
        beta = _sigmoid(raw)
        z = raw + par_ref[1:2, :]
        neg_ea = -jnp.exp(par_ref[0:1, :])
        g = neg_ea * _softplus(z)
        da = dbg_v * neg_ea * _sigmoid(z)
        dbgi_ref[...] = jnp.where(is_b, dbg_v * beta * (1.0 - beta), jnp.where(is_a, da, 0.0))
        dpar_ref[0:1, :] += jnp.sum(jnp.where(is_a, dbg_v * g, 0.0), axis=0, keepdims=True)
        dpar_ref[1:2, :] += jnp.sum(jnp.where(is_a, da, 0.0), axis=0, keepdims=True)

    wide = pl.BlockSpec((tt, A_WIDTH), lambda i: (i, 0))
    return pl.pallas_call(
        body, name=name, grid=(t // tt,),
        in_specs=[pl.BlockSpec((tt, cw), lambda i: (i, 0)),
                  pl.BlockSpec((SUBLANE, cw), lambda i: (jnp.maximum(i * hb - 1, 0), 0)),
                  pl.BlockSpec((tt, LANE), lambda i: (i, C_BG // LANE)),
                  pl.BlockSpec((CONV_K, cw), lambda i: (0, 0)),
                  pl.BlockSpec((SUBLANE, LANE), lambda i: (0, 0)),
                  wide, wide, wide, pl.BlockSpec((tt, LANE), lambda i: (i, 0)),
                  pl.BlockSpec((SUBLANE, tt), lambda i: (0, i))],
        out_specs=[pl.BlockSpec((tt, cw), lambda i: (i, 0)), pl.BlockSpec((tt, LANE), lambda i: (i, 0)),
                   pl.BlockSpec((SUBLANE, LANE), lambda i: (0, 0))],
        out_shape=[jax.ShapeDtypeStruct((t, cw), F32), jax.ShapeDtypeStruct((t, LANE), F32),
                   jax.ShapeDtypeStruct((SUBLANE, LANE), F32)],
        compiler_params=_cp("arbitrary"))(h, h, h, conv_w, par, dq, dk, dv, dbg, dbgt)


def _conv_bwd(dc, h, conv_w, *, tt, name):
    t = dc.shape[0]
    cw = 3 * A_WIDTH
    hb = tt // SUBLANE
    nb = t // tt

    def body(dc_ref, after_ref, pre_ref, before_ref, cw_ref, dpre_ref, dcw_ref):
        i = pl.program_id(0)

        @pl.when(i == 0)
        def _():
            dcw_ref[...] = jnp.zeros_like(dcw_ref)

        dcv = dc_ref[...]
        after = jnp.where(i < nb - 1, after_ref[...], 0.0)
        cur = pre_ref[...]
        before = jnp.where(i > 0, before_ref[...], 0.0)
        w = cw_ref[...]
        acc = dcv * w[CONV_K - 1:CONV_K, :]
        dcw_ref[CONV_K - 1:CONV_K, :] += jnp.sum(dcv * cur, axis=0, keepdims=True)
        for s in range(1, CONV_K):
            j = CONV_K - 1 - s
            acc = acc + _shift_up(dcv, after, s) * w[j:j + 1, :]
            dcw_ref[j:j + 1, :] += jnp.sum(dcv * _shift_down(cur, before, s), axis=0, keepdims=True)
        dpre_ref[...] = acc

    return pl.pallas_call(
        body, name=name, grid=(nb,),
        in_specs=[pl.BlockSpec((tt, cw), lambda i: (i, 0)),
                  pl.BlockSpec((SUBLANE, cw), lambda i: (jnp.minimum((i + 1) * hb, t // SUBLANE - 1), 0)),
                  pl.BlockSpec((tt, cw), lambda i: (i, 0)),
                  pl.BlockSpec((SUBLANE, cw), lambda i: (jnp.maximum(i * hb - 1, 0), 0)),
                  pl.BlockSpec((CONV_K, cw), lambda i: (0, 0))],
        out_specs=[pl.BlockSpec((tt, cw), lambda i: (i, 0)), pl.BlockSpec((SUBLANE, cw), lambda i: (0, 0))],
        out_shape=[jax.ShapeDtypeStruct((t, cw), F32), jax.ShapeDtypeStruct((SUBLANE, cw), F32)],
        compiler_params=_cp("arbitrary"))(dc, dc, h, h, conv_w)


def _swa_bwd(h, dm, sinks_b, *, name):
    t = h.shape[0]
    qspec, cur, prev = _swa_specs()

    def body(q_ref, kc_ref, kp_ref, vc_ref, vp_ref, zb_ref, dy_ref, sk_ref, dq_ref, dzb_ref, dk_ref, dv_ref, dsk_ref):
        n_blk = pl.program_id(0)

        @pl.when(n_blk == 0)
        def _():
            dk_ref[...] = jnp.zeros_like(dk_ref)
            dv_ref[...] = jnp.zeros_like(dv_ref)
            dsk_ref[...] = jnp.zeros_like(dsk_ref)

        kband = jnp.concatenate([kp_ref[...], kc_ref[...]], axis=0)
        vband = jnp.concatenate([vp_ref[...], vc_ref[...]], axis=0)
        scale = B_HEAD_DIM ** -0.5
        neg_dist = _swa_neg_dist(n_blk)
        dk_acc, dv_acc = [], []
        for hk in range(B_KV_HEADS):
            ksl = slice(hk * B_HEAD_DIM, (hk + 1) * B_HEAD_DIM)
            kh, vh = kband[:, ksl], vband[:, ksl]
            qs, p, ps, o = _swa_group_probs(q_ref, sk_ref, kh, vh, neg_dist, hk)
            zb, dy = _stack_heads(zb_ref, hk), _stack_heads(dy_ref, hk)
            dzb = dy * o * _dsilu(zb)
            do = dy * _silu(zb)
            delta = jnp.sum(do * o, -1, keepdims=True)
            ds = p * (_dot_nt(do, vh) - delta)
            dq = _dot(ds, kh) * scale
            dk_acc.append(_dot_tn(ds, qs))
            dv_acc.append(_dot_tn(p, do))
            dsink = ps * delta
            for g in range(B_GROUP):
                hq = hk * B_GROUP + g
                rows = slice(g * BLOCK, (g + 1) * BLOCK)
                qsl = slice(hq * B_HEAD_DIM, (hq + 1) * B_HEAD_DIM)
                dzb_ref[:, qsl] = dzb[rows]
                dq_ref[:, qsl] = dq[rows]
                dsk_ref[hq:hq + 1, :] += -jnp.sum(dsink[rows], keepdims=True)
        dkb = jnp.concatenate(dk_acc, axis=1)
        dvb = jnp.concatenate(dv_acc, axis=1)
        at_cur = pl.ds(pl.multiple_of(n_blk * BLOCK, BLOCK), BLOCK)
        at_prev = pl.ds(pl.multiple_of(jnp.maximum(n_blk - 1, 0) * BLOCK, BLOCK), BLOCK)
        dk_ref[at_prev, :] += dkb[:BLOCK]
        dv_ref[at_prev, :] += dvb[:BLOCK]
        dk_ref[at_cur, :] += dkb[BLOCK:]
        dv_ref[at_cur, :] += dvb[BLOCK:]

    wide = jax.ShapeDtypeStruct((t, B_WIDTH), F32)
    narrow = jax.ShapeDtypeStruct((t, B_KV_WIDTH), F32)
    res = lambda a, b: pl.BlockSpec((a, b), lambda i: (0, 0))
    return pl.pallas_call(
        body, name=name, grid=(t // BLOCK,),
        in_specs=[qspec(C_QB), cur(C_KB), prev(C_KB), cur(C_VB), prev(C_VB), qspec(C_ZB),
                  pl.BlockSpec((BLOCK, B_WIDTH), lambda i: (i, 1)), res(B_Q_HEADS, LANE)],
        out_specs=[pl.BlockSpec((BLOCK, B_WIDTH), lambda i: (i, 0))] * 2
        + [res(t, B_KV_WIDTH), res(t, B_KV_WIDTH), res(B_Q_HEADS, LANE)],
        out_shape=[wide, wide, narrow, narrow, jax.ShapeDtypeStruct((B_Q_HEADS, LANE), F32)],
        compiler_params=_cp("arbitrary"))(h, h, h, h, h, h, dm, sinks_b)


def _matmul_tn(a, b, *, tk, tn, name):
    t, m = a.shape
    n = b.shape[1]

    def body(a_ref, b_ref, o_ref):
        @pl.when(pl.program_id(1) == 0)
        def _():
            o_ref[...] = jnp.zeros_like(o_ref)

        o_ref[...] += _dot_tn(a_ref[...], b_ref[...])

    return pl.pallas_call(
        body, name=name, grid=(n // tn, t // tk),
        in_specs=[pl.BlockSpec((tk, m), lambda j, kk: (kk, 0)), pl.BlockSpec((tk, tn), lambda j, kk: (kk, j))],
        out_specs=pl.BlockSpec((m, tn), lambda j, kk: (0, j)),
        out_shape=jax.ShapeDtypeStruct((m, n), F32),
        compiler_params=_cp("parallel", "arbitrary"))(a, b)


def _matmul_nt_add(a, b, dr, *, tm, name):
    t, n = a.shape
    m = b.shape[0]

    def body(a_ref, b_ref, r_ref, o_ref):
        o_ref[...] = _dot_nt(a_ref[...], b_ref[...]) + DEEPNORM_ALPHA * r_ref[...]

    return pl.pallas_call(
        body, name=name, grid=(t // tm,),
        in_specs=[pl.BlockSpec((tm, n), lambda i: (i, 0)), pl.BlockSpec((m, n), lambda i: (0, 0)),
                  pl.BlockSpec((tm, m), lambda i: (i, 0))],
        out_specs=pl.BlockSpec((tm, m), lambda i: (i, 0)),
        out_shape=jax.ShapeDtypeStruct((t, m), F32),
        compiler_params=_cp("parallel"))(a, b, dr)


def _layer_bwd(dxn, res, wp, conv_w, par, sinks_b, norm_w, w_out_bf, ln_g, l):
    dr, dm, dw_out, dln_g, dln_b = _ln_out_bwd(dxn, res["r"], res["mixed"], ln_g, w_out_bf, tm=256, name=f"ln_out_bwd_{l}")
    h = res["h"]
    do, dza, dnw = _dn_post_bwd(dm, res["oa"], h, norm_w, tm=512, name=f"dn_post_bwd_{l}")
    dvn, ds_all = _dn_scan_bwd(res["q"], res["k"], res["w"], res["qk"], res["bg"], do, name=f"dn_scan_bwd_{l}")
    dq, dk, dv, dbg, dbgt = _dn_chunk_bwd(res["q"], res["k"], res["v"], res["vn"], res["tmat"], res["qk"], res["bg"],
                                          res["bgt"], res["s_all"], ds_all, dvn, do, name=f"dn_chunk_bwd_{l}")
    dc, dbgi, dpar = _dn_pre_bwd(h, conv_w, par, dq, dk, dv, dbg, dbgt, tt=512, name=f"dn_pre_bwd_{l}")
    dpre, dcw = _conv_bwd(dc, h, conv_w, tt=512, name=f"conv_bwd_{l}")
    dqb, dzb, dkb, dvb, dsk = _swa_bwd(h, dm, sinks_b, name=f"swa_bwd_{l}")
    dh = jnp.concatenate([dpre, dza, dqb, dzb, dkb, dvb, dbgi], axis=1)
    dwp = _matmul_tn(res["x"], dh, tk=512, tn=1152, name=f"in_proj_dw_{l}")
    dx = _matmul_nt_add(dh, wp, dr, tm=256, name=f"in_proj_dx_{l}")
    grads = dict(w_in=dwp, conv_w=dcw[:CONV_K], a_log=dpar[0, A_HEADS:2 * A_HEADS], dt_bias=dpar[1, A_HEADS:2 * A_HEADS],
                 norm_w=dnw[0], sinks=dsk[:, 0], w_out=dw_out, ln_g=dln_g[0], ln_b=dln_b[0])
    return dx, grads


def _local_step(x, target, wp, conv_w, a_log, dt_bias, norm_w, sinks, w_out_bf, ln_g, ln_b):
    per_layer = []
    saved = []
    for l in range(DEPTH):
        args = (wp[l], conv_w[l], _gate_params(a_log[l], dt_bias[l]),
                jnp.broadcast_to(sinks[l][:, None], (B_Q_HEADS, LANE)), norm_w[l][None], w_out_bf[l])
        per_layer.append(args)
        x, res = _layer_fwd(x, *args, ln_g[l][None], ln_b[l][None], l)
        saved.append(res)
    dx, loss_tile = _loss_grad(x, target, tm=512, name="loss_grad")
    grads = [None] * DEPTH
    for l in reversed(range(DEPTH)):
        dx, grads[l] = _layer_bwd(dx, saved[l], *per_layer[l], ln_g[l][None], l)
    return loss_tile, dx, grads


_ANY = pl.BlockSpec(memory_space=pl.ANY)
_MESH = pl.DeviceIdType.MESH


def _pair_exchange(arrays, *, name):
    n = len(arrays)

    def body(*refs):
        src, dst, (send_sems, recv_sems) = refs[:n], refs[n:2 * n], refs[2 * n:]
        sibling = (lax.axis_index("x"), lax.axis_index("y"), 1 - lax.axis_index("c"))
        copies = [pltpu.make_async_remote_copy(src_ref=src[k], dst_ref=dst[k], send_sem=send_sems.at[k],
                                               recv_sem=recv_sems.at[k], device_id=sibling, device_id_type=_MESH)
                  for k in range(n)]
        for cp in copies:
            cp.start()
        for cp in copies:
            cp.wait()

    return pl.pallas_call(
        body, name=name, in_specs=[_ANY] * n, out_specs=[_ANY] * n,
        out_shape=[jax.ShapeDtypeStruct(a.shape, a.dtype) for a in arrays],
        scratch_shapes=[pltpu.SemaphoreType.DMA((n,)), pltpu.SemaphoreType.DMA((n,))])(*arrays)


def _chip_exchange(arrays, scatter, *, name):
    n = len(arrays)
    shapes = [a.shape if sc else (N_SHARD,) + a.shape for a, sc in zip(arrays, scatter)]

    def body(*refs):
        src, dst = refs[:n], refs[n:2 * n]
        send_sems, recv_sems, local_sems = refs[2 * n:]
        x, y, c = lax.axis_index("x"), lax.axis_index("y"), lax.axis_index("c")
        me = 2 * x + y
        chips = [(1 - x, y), (x, 1 - y), (1 - x, 1 - y)]
        local = [pltpu.make_async_copy(src[k].at[me] if scatter[k] else src[k], dst[k].at[me], local_sems.at[k])
                 for k in range(n)]
        for cp in local:
            cp.start()
        sends = []
        for k in range(n):
            for j, (px, py) in enumerate(chips):
                sends.append(pltpu.make_async_remote_copy(
                    src_ref=src[k].at[2 * px + py] if scatter[k] else src[k], dst_ref=dst[k].at[me],
                    send_sem=send_sems.at[3 * k + j], recv_sem=recv_sems.at[3 * k + j],
                    device_id=(px, py, c), device_id_type=_MESH))
        for cp in sends:
            cp.start()
        for k in range(n):
            for j, (px, py) in enumerate(chips):
                pltpu.make_async_remote_copy(
                    src_ref=src[k].at[2 * px + py] if scatter[k] else src[k], dst_ref=dst[k].at[2 * px + py],
                    send_sem=send_sems.at[3 * k + j], recv_sem=recv_sems.at[3 * k + j],
                    device_id=(px, py, c), device_id_type=_MESH).wait_recv()
        for cp in sends:
            cp.wait_send()
        for cp in local:
            cp.wait()

    return pl.pallas_call(
        body, name=name, in_specs=[_ANY] * n, out_specs=[_ANY] * n,
        out_shape=[jax.ShapeDtypeStruct(s, a.dtype) for s, a in zip(shapes, arrays)],
        scratch_shapes=[pltpu.SemaphoreType.DMA((3 * n,)), pltpu.SemaphoreType.DMA((3 * n,)),
                        pltpu.SemaphoreType.DMA((n,))])(*arrays)


def _rows_view(a):
    return a.reshape((-1, a.shape[-1]))


def _tile_rows(rows):
    for tm in (512, 256, 128, 64, 32, 16, 8):
        if rows % tm == 0:
            return tm
    return rows


def _add2(a, b, *, name):
    a2, b2 = _rows_view(a), _rows_view(b)
    rows, cols = a2.shape
    tm = _tile_rows(rows)

    def body(a_ref, b_ref, o_ref):
        o_ref[...] = a_ref[...] + b_ref[...]

    blk = pl.BlockSpec((tm, cols), lambda i: (i, 0))
    out = pl.pallas_call(body, name=name, grid=(rows // tm,), in_specs=[blk, blk], out_specs=blk,
                         out_shape=jax.ShapeDtypeStruct(a2.shape, a2.dtype), compiler_params=_cp("parallel"))(a2, b2)
    return out.reshape(a.shape)


def _sum4(a, *, name):
    a3 = a.reshape((N_SHARD, -1, a.shape[-1]))
    _, rows, cols = a3.shape
    tm = _tile_rows(rows)

    def body(a_ref, o_ref):
        o_ref[...] = ((a_ref[0] + a_ref[1]) + a_ref[2]) + a_ref[3]

    out = pl.pallas_call(body, name=name, grid=(rows // tm,),
                         in_specs=[pl.BlockSpec((N_SHARD, tm, cols), lambda i: (0, i, 0))],
                         out_specs=pl.BlockSpec((tm, cols), lambda i: (i, 0)),
                         out_shape=jax.ShapeDtypeStruct((rows, cols), a.dtype), compiler_params=_cp("parallel"))(a3)
    return out.reshape(a.shape[1:])


def _adamw(w, g, m, v, *, name):
    w2, g2, m2, v2 = (_rows_view(t) for t in (w, g, m, v))
    rows, cols = w2.shape
    tm = _tile_rows(rows)

    def body(w_ref, g_ref, m_ref, v_ref, d_ref, mo_ref, vo_ref):
        gv = g_ref[...]
        mn = ADAM_B1 * m_ref[...] + (1.0 - ADAM_B1) * gv
        vn = ADAM_B2 * v_ref[...] + (1.0 - ADAM_B2) * (gv * gv)
        mo_ref[...] = mn
        vo_ref[...] = vn
        m_hat = mn / (1.0 - ADAM_B1 ** ADAM_STEP)
        v_hat = vn / (1.0 - ADAM_B2 ** ADAM_STEP)
        d_ref[...] = -ADAM_LR * (m_hat / (jnp.sqrt(v_hat) + ADAM_EPS) + ADAM_WD * w_ref[...])

    blk = pl.BlockSpec((tm, cols), lambda i: (i, 0))
    shp = jax.ShapeDtypeStruct(w2.shape, F32)
    outs = pl.pallas_call(body, name=name, grid=(rows // tm,), in_specs=[blk] * 4, out_specs=[blk] * 3,
                          out_shape=[shp] * 3, compiler_params=_cp("parallel"))(w2, g2, m2, v2)
    return [o.reshape(w.shape) for o in outs]


def _to_kernel_cols(w):
    pad = jnp.zeros(w.shape[:-1] + (LANE - 2 * A_HEADS,), w.dtype)
    return jnp.concatenate([w[..., 0:2048], w[..., 2056:2568], w[..., 2824:3336], w[..., 2568:2696],
                            w[..., 2696:2824], w[..., 2048:2056], pad], axis=-1)


def _from_kernel_cols(w):
    return jnp.concatenate([w[..., 0:2048], w[..., C_BG:C_BG + 2 * A_HEADS], w[..., C_QB:C_QB + B_WIDTH],
                            w[..., C_KB:C_KB + B_KV_WIDTH], w[..., C_VB:C_VB + B_KV_WIDTH],
                            w[..., C_ZB:C_ZB + B_WIDTH]], axis=-1)


def _gate_params(a_log, dt_bias):
    par = jnp.zeros((SUBLANE, LANE), F32)
    par = par.at[0, A_HEADS:2 * A_HEADS].set(a_log)
    return par.at[1, A_HEADS:2 * A_HEADS].set(dt_bias)


SMALL = ("conv_w", "a_log", "dt_bias", "norm_w", "sinks", "ln_g", "ln_b")


def _pack(parts, cols):
    flat = jnp.concatenate([p.reshape(-1) for p in parts])
    rows = -(-flat.shape[0] // cols)
    return jnp.pad(flat, (0, rows * cols - flat.shape[0])).reshape(rows, cols)


def _unpack(packed, shapes):
    flat = packed.reshape(-1)
    out, at = [], 0
    for s in shapes:
        n = math.prod(s)
        out.append(flat[at:at + n].reshape(s))
        at += n
    return out


def kernel(x, w_in, conv_w, a_log, dt_bias, norm_w, sinks, w_out, ln_g, ln_b, loss_target, m_w_in, m_conv_w, m_a_log, m_dt_bias, m_norm_w, m_sinks, m_w_out, m_ln_g, m_ln_b, v_w_in, v_conv_w, v_a_log, v_dt_bias, v_norm_w, v_sinks, v_w_out, v_ln_g, v_ln_b):
    xi, yi, ci = lax.axis_index("x"), lax.axis_index("y"), lax.axis_index("c")
    me = 2 * xi + yi

    g_in, g_out, g_conv = _chip_exchange([w_in.astype(BF16), w_out.astype(BF16), conv_w], [False] * 3,
                                         name="gather_weights")
    wp = _to_kernel_cols(jnp.moveaxis(g_in, 0, 2).reshape(DEPTH, D_MODEL, IN_COLS))
    w_out_full = jnp.moveaxis(g_out, 0, 1).reshape(DEPTH, D_MODEL, D_MODEL)
    conv_full = jnp.moveaxis(g_conv, 0, 2).reshape(DEPTH, CONV_K, 3 * A_WIDTH)

    loss_tile, dx, grads = _local_step(x[0], loss_target[0], wp, conv_full, a_log, dt_bias, norm_w, sinks,
                                       w_out_full, ln_g, ln_b)
    loss = lax.psum(loss_tile[0, 0], ("x", "y", "c"))

    gin = jnp.stack([_from_kernel_cols(g["w_in"]) for g in grads])
    gin = jnp.moveaxis(gin.reshape(DEPTH, D_MODEL, N_SHARD, IN_SHARD), 2, 1)
    gout = jnp.stack([g["w_out"] for g in grads]).reshape(DEPTH, N_SHARD, OUT_SHARD, D_MODEL)
    small_shapes = [(DEPTH,) + grads[0][nm].shape for nm in SMALL]
    gsmall = _pack([jnp.stack([g[nm] for g in grads]) for nm in SMALL], D_MODEL)

    pick = lambda a, i: lax.dynamic_index_in_dim(a, i, 0, keepdims=False)
    r_in, r_out, r_small = _pair_exchange([pick(gin, 1 - ci), pick(gout, 1 - ci), gsmall], name="pair_reduce")
    p_in = _add2(pick(gin, ci), r_in, name="pair_add_in")
    p_out = _add2(pick(gout, ci), r_out, name="pair_add_out")
    p_small = _add2(gsmall, r_small, name="pair_add_small")
    q_in, q_out, q_small = _chip_exchange([p_in, p_out, p_small], [True, True, False], name="chip_reduce")
    s_in = _sum4(q_in, name="chip_sum_in")
    s_out = _sum4(q_out, name="chip_sum_out")
    s_small = _sum4(q_small, name="chip_sum_small")
    o_in, o_out = _pair_exchange([s_in, s_out], name="pair_share")
    by_layer = lambda mine, other: jnp.where(ci == 0, jnp.stack([mine, other]), jnp.stack([other, mine]))
    grad_in = by_layer(s_in, o_in)
    grad_out = by_layer(s_out, o_out)
    gs = dict(zip(SMALL, _unpack(s_small, small_shapes)))
    gs["conv_w"] = lax.dynamic_slice_in_dim(gs["conv_w"], me * CONV_SHARD, CONV_SHARD, axis=2)

    d_in, nm_in, nv_in = _adamw(w_in, grad_in, m_w_in, v_w_in, name="adamw_in")
    d_out, nm_out, nv_out = _adamw(w_out, grad_out, m_w_out, v_w_out, name="adamw_out")
    ws = dict(conv_w=conv_w, a_log=a_log, dt_bias=dt_bias, norm_w=norm_w, sinks=sinks, ln_g=ln_g, ln_b=ln_b)
    ms = dict(conv_w=m_conv_w, a_log=m_a_log, dt_bias=m_dt_bias, norm_w=m_norm_w, sinks=m_sinks, ln_g=m_ln_g, ln_b=m_ln_b)
    vs = dict(conv_w=v_conv_w, a_log=v_a_log, dt_bias=v_dt_bias, norm_w=v_norm_w, sinks=v_sinks, ln_g=v_ln_g, ln_b=v_ln_b)
    shard_shapes = [ws[nm].shape for nm in SMALL]
    packed = [_pack([d[nm] for nm in SMALL], LANE) for d in (ws, gs, ms, vs)]
    d_s, nm_s, nv_s = (dict(zip(SMALL, _unpack(o, shard_shapes))) for o in _adamw(*packed, name="adamw_small"))

    def in_order(big_in, small, big_out):
        return (big_in, small["conv_w"], small["a_log"], small["dt_bias"], small["norm_w"], small["sinks"], big_out,
                small["ln_g"], small["ln_b"])

    return (loss, dx[None], *in_order(grad_in, gs, grad_out), *in_order(d_in, d_s, d_out),
            *in_order(nm_in, nm_s, nm_out), *in_order(nv_in, nv_s, nv_out))
```

```python
import functools
import math

import jax
import jax.numpy as jnp
from jax import lax
from jax.experimental import pallas as pl
from jax.experimental.pallas import tpu as pltpu

F32 = jnp.float32
BF16 = jnp.bfloat16
HI = lax.Precision.HIGHEST

D_MODEL = 1024
DEPTH = 2
A_HEADS = 4
A_HEAD_DIM = 128
A_WIDTH = 512
CONV_K = 4
CHUNK = 64
B_Q_HEADS = 8
B_KV_HEADS = 2
B_HEAD_DIM = 64
B_GROUP = 4
B_WIDTH = 512
B_KV_WIDTH = 128
BLOCK = 128
IN_COLS = 3336
DEEPNORM_ALPHA = (2 * DEPTH) ** 0.25
LN_EPS = 1e-5
RMS_EPS = 1e-6
L2_EPS = 1e-6
ADAM_LR = 0.001
ADAM_B1 = 0.9
ADAM_B2 = 0.999
ADAM_EPS = 1e-08
ADAM_WD = 0.01
ADAM_STEP = 10

N_SHARD = 4
IN_SHARD = IN_COLS // N_SHARD
OUT_SHARD = D_MODEL // N_SHARD
CONV_SHARD = 3 * A_WIDTH // N_SHARD

P_COLS = 3456
C_PRE = 0
C_ZA = 1536
C_QB = 2048
C_ZB = 2560
C_KB = 3072
C_VB = 3200
C_BG = 3328
DH_MAIN = C_KB
LANE = 128
SUBLANE = 8
VMEM_LIMIT = 56 * 1024 * 1024
ALIBI = tuple(2.0 ** (-8.0 * (h + 1) / B_Q_HEADS) for h in range(B_Q_HEADS))
NEG = -1e30


def _cp(*sem):
    return pltpu.CompilerParams(dimension_semantics=sem, vmem_limit_bytes=VMEM_LIMIT)


def _dot(a, b):
    return jnp.dot(a.astype(BF16), b.astype(BF16), preferred_element_type=F32)


def _dot_nt(a, b):
    return lax.dot_general(a.astype(BF16), b.astype(BF16), (((1,), (1,)), ((), ())),
                           preferred_element_type=F32)


def _dot_tn(a, b):
    return lax.dot_general(a.astype(BF16), b.astype(BF16), (((0,), (0,)), ((), ())),
                           preferred_element_type=F32)


def _dot_hi(a, b):
    return jnp.dot(a, b, precision=HI, preferred_element_type=F32)


def _sigmoid(x):
    return jax.nn.sigmoid(x)


def _silu(x):
    return x * _sigmoid(x)


def _dsilu(x):
    s = _sigmoid(x)
    return s * (1.0 + x * (1.0 - s))


def _softplus(x):
    return jnp.maximum(x, 0.0) + jnp.log(1.0 + jnp.exp(-jnp.abs(x)))


def _shift_down(cur, before, s):
    if s == 0:
        return cur
    r = pltpu.roll(cur, s, 0)
    rb = pltpu.roll(before, s, 0)
    row = lax.broadcasted_iota(jnp.int32, before.shape, 0)
    head = jnp.where(row < s, rb, r[0:SUBLANE])
    return jnp.concatenate([head, r[SUBLANE:]], axis=0)


def _shift_up(cur, after, s):
    if s == 0:
        return cur
    n = cur.shape[0]
    r = pltpu.roll(cur, n - s, 0)
    ra = pltpu.roll(after, SUBLANE - s, 0)
    row = lax.broadcasted_iota(jnp.int32, after.shape, 0)
    tail = jnp.where(row >= SUBLANE - s, ra, r[n - SUBLANE:])
    return jnp.concatenate([r[:n - SUBLANE], tail], axis=0)


def _conv_fwd(cur, before, w):
    acc = cur * w[CONV_K - 1:CONV_K, :]
    for s in range(1, CONV_K):
        acc = acc + _shift_down(cur, before, s) * w[CONV_K - 1 - s:CONV_K - s, :]
    return acc


def _matmul_nt(a, bt, *, tm, tn, name):
    m, k = a.shape
    n = bt.shape[0]

    def body(a_ref, b_ref, o_ref):
        o_ref[...] = _dot_nt(a_ref[...], b_ref[...])

    return pl.pallas_call(
        body, name=name, grid=(m // tm, n // tn),
        in_specs=[pl.BlockSpec((tm, k), lambda i, j: (i, 0)), pl.BlockSpec((tn, k), lambda i, j: (j, 0))],
        out_specs=pl.BlockSpec((tm, tn), lambda i, j: (i, j)),
        out_shape=jax.ShapeDtypeStruct((m, n), F32),
        compiler_params=_cp("parallel", "parallel"))(a, bt)


def _dn_pre(h, conv_w, par, *, tt, name):
    t = h.shape[0]
    cw = 3 * A_WIDTH
    hb = tt // SUBLANE

    def body(pre_ref, halo_ref, bgi_ref, cw_ref, par_ref, q_ref, k_ref, v_ref, bg_ref, bgt_ref):
        i = pl.program_id(0)
        cur = pre_ref[...]
        before = jnp.where(i > 0, halo_ref[...], 0.0)
        s = _silu(_conv_fwd(cur, before, cw_ref[...]))
        for hd in range(A_HEADS):
            sl = slice(hd * LANE, (hd + 1) * LANE)
            tq = s[:, hd * LANE:(hd + 1) * LANE]
            q_ref[:, sl] = tq * (lax.rsqrt(jnp.sum(tq * tq, -1, keepdims=True) + L2_EPS) * (A_HEAD_DIM ** -0.5))
            tk = s[:, A_WIDTH + hd * LANE:A_WIDTH + (hd + 1) * LANE]
            k_ref[:, sl] = tk * lax.rsqrt(jnp.sum(tk * tk, -1, keepdims=True) + L2_EPS)
        v_ref[...] = s[:, 2 * A_WIDTH:]
        raw = bgi_ref[...]
        lane = lax.broadcasted_iota(jnp.int32, raw.shape, 1)
        is_a = (lane >= A_HEADS) & (lane < 2 * A_HEADS)
        g = jnp.where(is_a, -jnp.exp(par_ref[0:1, :]) * _softplus(raw + par_ref[1:2, :]), 0.0)
        gc = _dot_hi(_chunk_tri(tt, lower=True), g)
        bg = jnp.where(lane < A_HEADS, _sigmoid(raw), gc)
        bg_ref[...] = bg
        bgt_ref[...] = jnp.transpose(bg)[0:SUBLANE, :]

    wide = jax.ShapeDtypeStruct((t, A_WIDTH), F32)
    return pl.pallas_call(
        body, name=name, grid=(t // tt,),
        in_specs=[pl.BlockSpec((tt, cw), lambda i: (i, 0)),
                  pl.BlockSpec((SUBLANE, cw), lambda i: (jnp.maximum(i * hb - 1, 0), 0)),
                  pl.BlockSpec((tt, LANE), lambda i: (i, C_BG // LANE)),
                  pl.BlockSpec((CONV_K, cw), lambda i: (0, 0)),
                  pl.BlockSpec((SUBLANE, LANE), lambda i: (0, 0))],
        out_specs=[pl.BlockSpec((tt, A_WIDTH), lambda i: (i, 0))] * 3
        + [pl.BlockSpec((tt, LANE), lambda i: (i, 0)), pl.BlockSpec((SUBLANE, tt), lambda i: (0, i))],
        out_shape=[wide, wide, wide, jax.ShapeDtypeStruct((t, LANE), F32), jax.ShapeDtypeStruct((SUBLANE, t), F32)],
        compiler_params=_cp("parallel"))(h, h, h, conv_w, par)


def _chunk_tri(n, lower):
    r = lax.broadcasted_iota(jnp.int32, (n, n), 0)
    c = lax.broadcasted_iota(jnp.int32, (n, n), 1)
    shift = CHUNK.bit_length() - 1
    same = jnp.right_shift(r, shift) == jnp.right_shift(c, shift)
    return (same & ((c <= r) if lower else (c >= r))).astype(F32)


def _chunk_masks():
    r = lax.broadcasted_iota(jnp.int32, (CHUNK, CHUNK), 0)
    c = lax.broadcasted_iota(jnp.int32, (CHUNK, CHUNK), 1)
    return r >= c, r > c, r == c


def _split(a):
    hi = a.astype(BF16)
    return hi, (a - hi.astype(F32)).astype(BF16)


def _dot3(a, b):
    (ah, al), (bh, bl) = a, b
    d = lambda p, q: jnp.dot(p, q, preferred_element_type=F32)
    return d(ah, bh) + (d(ah, bl) + d(al, bh))


def _tri_inv_many(a_list, eye):
    d = lambda p, q: jnp.dot(p, q, preferred_element_type=F32)
    p = [(-a).astype(BF16) for a in a_list]
    tm = [eye - a for a in a_list]
    for _ in range(5):
        pf = [d(pi, pi) for pi in p]
        p = [x.astype(BF16) for x in pf]
        tm = [t + d(t.astype(BF16), pi) for t, pi in zip(tm, p)]
    ms = [_split(eye + a) for a in a_list]
    res = [eye - _dot3(m, _split(t)) for m, t in zip(ms, tm)]
    return [t + d(t.astype(BF16), r.astype(BF16)) for t, r in zip(tm, res)]


def _chunk_gates(bg_v, bgt_v, hd):
    return (bg_v[:, hd:hd + 1], bg_v[:, A_HEADS + hd:A_HEADS + hd + 1],
            None if bgt_v is None else bgt_v[A_HEADS + hd:A_HEADS + hd + 1, :])


WY_ROWS = 256
SCAN_ROWS = 128
WY_GROUP = 2


def _dn_wy(q, k, v, bg, bgt, *, name):
    t = q.shape[0]
    rows = WY_ROWS

    def body(q_ref, k_ref, v_ref, bg_ref, bgt_ref, u_ref, w_ref, tm_ref, qk_ref):
        causal, strict, diag = _chunk_masks()
        eye = diag.astype(F32)
        for c0 in range(0, rows // CHUNK, WY_GROUP):
            items = [(c, hd) for c in range(c0, c0 + WY_GROUP) for hd in range(A_HEADS)]
            rs = lambda c: slice(c * CHUNK, (c + 1) * CHUNK)
            sl = lambda hd: slice(hd * LANE, (hd + 1) * LANE)
            hs = lambda hd: slice(hd * CHUNK, (hd + 1) * CHUNK)
            gates = [_chunk_gates(bg_ref[rs(c), :], bgt_ref[:, rs(c)], hd) for c, hd in items]
            dms = [jnp.exp(jnp.where(causal, gcol - grow, NEG)) for _, gcol, grow in gates]
            kbs = [k_ref[rs(c), sl(hd)] * g[0] for (c, hd), g in zip(items, gates)]
            a_list = [jnp.where(strict, _dot_nt(kb, k_ref[rs(c), sl(hd)]) * dm, 0.0)
                      for (c, hd), kb, dm in zip(items, kbs, dms)]
            for (c, hd), dm in zip(items, dms):
                qk_ref[rs(c), hs(hd)] = jnp.where(
                    causal, _dot_nt(q_ref[rs(c), sl(hd)], k_ref[rs(c), sl(hd)]) * dm, 0.0)
            tms = _tri_inv_many(a_list, eye)
            for (c, hd), g, kb, tmat in zip(items, gates, kbs, tms):
                tm_ref[rs(c), hs(hd)] = tmat
                u_ref[rs(c), sl(hd)] = _dot(tmat, v_ref[rs(c), sl(hd)] * g[0])
                w_ref[rs(c), sl(hd)] = _dot(tmat, kb * jnp.exp(g[1])).astype(BF16)

    blk = pl.BlockSpec((rows, A_WIDTH), lambda i: (i, 0))
    half = pl.BlockSpec((rows, A_HEADS * CHUNK), lambda i: (i, 0))
    return pl.pallas_call(
        body, name=name, grid=(t // rows,),
        in_specs=[blk, blk, blk, pl.BlockSpec((rows, LANE), lambda i: (i, 0)),
                  pl.BlockSpec((SUBLANE, rows), lambda i: (0, i))],
        out_specs=[blk, blk, half, half],
        out_shape=[jax.ShapeDtypeStruct((t, A_WIDTH), F32), jax.ShapeDtypeStruct((t, A_WIDTH), BF16),
                   jax.ShapeDtypeStruct((t, A_HEADS * CHUNK), F32), jax.ShapeDtypeStruct((t, A_HEADS * CHUNK), F32)],
        compiler_params=_cp("parallel"))(q, k, v, bg, bgt)


def _dn_scan_fwd(q, k, u, w, qk, bg, *, name):
    t = q.shape[0]
    rows = SCAN_ROWS
    per = rows // CHUNK

    def body(q_ref, k_ref, u_ref, w_ref, qk_ref, bg_ref, o_ref, vn_ref, s_ref, state):
        @pl.when(pl.program_id(0) == 0)
        def _():
            state[...] = jnp.zeros_like(state)

        heads = range(A_HEADS)
        sl = lambda hd: slice(hd * LANE, (hd + 1) * LANE)
        s_cur = [state[hd] for hd in heads]
        for c in range(per):
            rs = slice(c * CHUNK, (c + 1) * CHUNK)
            bg_v = bg_ref[rs, :]
            gcols = [_chunk_gates(bg_v, None, hd)[1] for hd in heads]
            glasts = [gc[CHUNK - 1:CHUNK, :] for gc in gcols]
            for hd in heads:
                s_ref[c, hd] = s_cur[hd]
            vns = [u_ref[rs, sl(hd)] - _dot(w_ref[rs, sl(hd)], s_cur[hd]) for hd in heads]
            qss = [_dot(q_ref[rs, sl(hd)] * jnp.exp(gcols[hd]), s_cur[hd]) for hd in heads]
            s_cur = [s_cur[hd] * jnp.exp(glasts[hd])
                     + _dot_tn(k_ref[rs, sl(hd)] * jnp.exp(glasts[hd] - gcols[hd]), vns[hd]) for hd in heads]
            for hd in heads:
                vn_ref[rs, sl(hd)] = vns[hd]
                o_ref[rs, sl(hd)] = qss[hd] + _dot(qk_ref[rs, hd * CHUNK:(hd + 1) * CHUNK], vns[hd])
        for hd in heads:
            state[hd] = s_cur[hd]

    blk = pl.BlockSpec((rows, A_WIDTH), lambda i: (i, 0))
    half = pl.BlockSpec((rows, A_HEADS * CHUNK), lambda i: (i, 0))
    wide = jax.ShapeDtypeStruct((t, A_WIDTH), F32)
    return pl.pallas_call(
        body, name=name, grid=(t // rows,),
        in_specs=[blk, blk, blk, blk, half, pl.BlockSpec((rows, LANE), lambda i: (i, 0))],
        out_specs=[blk, blk, pl.BlockSpec((per, A_HEADS, LANE, LANE), lambda i: (i, 0, 0, 0))],
        out_shape=[wide, wide, jax.ShapeDtypeStruct((t // CHUNK, A_HEADS, LANE, LANE), F32)],
        scratch_shapes=[pltpu.VMEM((A_HEADS, LANE, LANE), F32)],
        compiler_params=_cp("arbitrary"))(q, k, u, w, qk, bg)


def _swa_neg_dist(n_blk):
    qi = lax.broadcasted_iota(jnp.int32, (BLOCK, 2 * BLOCK), 0)
    si = lax.broadcasted_iota(jnp.int32, (BLOCK, 2 * BLOCK), 1)
    dist = qi + BLOCK - si
    mask = (dist >= 0) & (dist < BLOCK) & ((si >= BLOCK) | (n_blk > 0))
    return jnp.where(mask, -dist.astype(F32), NEG)


def _stack_heads(ref, hk):
    return jnp.concatenate([ref[:, h * B_HEAD_DIM:(h + 1) * B_HEAD_DIM]
                            for h in range(hk * B_GROUP, (hk + 1) * B_GROUP)], axis=0)


def _swa_group_probs(q_ref, sk_ref, kh, vh, neg_dist, hk):
    heads = range(hk * B_GROUP, (hk + 1) * B_GROUP)
    qs = _stack_heads(q_ref, hk) * (B_HEAD_DIM ** -0.5)
    bias = jnp.concatenate([ALIBI[h] * neg_dist for h in heads], axis=0)
    sink = jnp.concatenate([jnp.broadcast_to(sk_ref[h:h + 1, 0:1], (BLOCK, 1)) for h in heads], axis=0)
    s = _dot_nt(qs, kh) + bias
    m = jnp.maximum(jnp.max(s, axis=-1, keepdims=True), sink)
    p = jnp.exp(s - m)
    vext = jnp.concatenate([vh.astype(BF16), jnp.ones((2 * BLOCK, B_HEAD_DIM), BF16)], axis=1)
    oe = jnp.dot(p.astype(BF16), vext, preferred_element_type=F32)
    ps = jnp.exp(sink - m)
    inv = 1.0 / (oe[:, B_HEAD_DIM:B_HEAD_DIM + 1] + ps)
    return qs, p * inv, ps * inv, oe[:, :B_HEAD_DIM] * inv


def _swa_specs():
    qspec = lambda c0: pl.BlockSpec((BLOCK, B_WIDTH), lambda i: (i, c0 // B_WIDTH))
    cur = lambda c0: pl.BlockSpec((BLOCK, LANE), lambda i: (i, c0 // LANE))
    prev = lambda c0: pl.BlockSpec((BLOCK, LANE), lambda i: (jnp.maximum(i - 1, 0), c0 // LANE))
    return qspec, cur, prev


def _swa_fwd(h, sinks_b, *, name):
    t = h.shape[0]
    qspec, cur, prev = _swa_specs()

    def body(q_ref, kc_ref, kp_ref, vc_ref, vp_ref, sk_ref, o_ref):
        n_blk = pl.program_id(0)
        kband = jnp.concatenate([kp_ref[...], kc_ref[...]], axis=0)
        vband = jnp.concatenate([vp_ref[...], vc_ref[...]], axis=0)
        neg_dist = _swa_neg_dist(n_blk)
        for hk in range(B_KV_HEADS):
            ksl = slice(hk * B_HEAD_DIM, (hk + 1) * B_HEAD_DIM)
            _, _, _, o = _swa_group_probs(q_ref, sk_ref, kband[:, ksl], vband[:, ksl], neg_dist, hk)
            for g in range(B_GROUP):
                hq = hk * B_GROUP + g
                o_ref[:, hq * B_HEAD_DIM:(hq + 1) * B_HEAD_DIM] = o[g * BLOCK:(g + 1) * BLOCK]

    return pl.pallas_call(
        body, name=name, grid=(t // BLOCK,),
        in_specs=[qspec(C_QB), cur(C_KB), prev(C_KB), cur(C_VB), prev(C_VB),
                  pl.BlockSpec((B_Q_HEADS, LANE), lambda i: (0, 0))],
        out_specs=pl.BlockSpec((BLOCK, B_WIDTH), lambda i: (i, 0)),
        out_shape=jax.ShapeDtypeStruct((t, B_WIDTH), F32),
        compiler_params=_cp("parallel"))(h, h, h, h, h, sinks_b)


def _rms_gate(o, za, nw):
    outs = []
    for hd in range(A_HEADS):
        oh = o[:, hd * LANE:(hd + 1) * LANE]
        r = lax.rsqrt(jnp.mean(oh * oh, -1, keepdims=True) + RMS_EPS)
        outs.append(oh * r * nw)
    return jnp.concatenate(outs, axis=1) * _silu(za)


def _out_ln(x, oa, ob, h, norm_w, w_out, ln_g, ln_b, *, tm, name):
    t = x.shape[0]

    def body(x_ref, oa_ref, ob_ref, za_ref, zb_ref, nw_ref, w_ref, g_ref, b_ref, xn_ref, mx_ref, r_ref):
        ya = _rms_gate(oa_ref[...], za_ref[...], nw_ref[...])
        yb = ob_ref[...] * _silu(zb_ref[...])
        mixed = jnp.concatenate([ya, yb], axis=1).astype(BF16)
        mx_ref[...] = mixed
        r = DEEPNORM_ALPHA * x_ref[...] + jnp.dot(mixed, w_ref[...], preferred_element_type=F32)
        r_ref[...] = r
        mu = jnp.mean(r, -1, keepdims=True)
        xc = r - mu
        var = jnp.mean(xc * xc, -1, keepdims=True)
        xn_ref[...] = xc * lax.rsqrt(var + LN_EPS) * g_ref[...] + b_ref[...]

    row = lambda w, c: pl.BlockSpec((tm, w), lambda i: (i, c))
    full = lambda a, b: pl.BlockSpec((a, b), lambda i: (0, 0))
    return pl.pallas_call(
        body, name=name, grid=(t // tm,),
        in_specs=[row(D_MODEL, 0), row(A_WIDTH, 0), row(B_WIDTH, 0), row(A_WIDTH, C_ZA // A_WIDTH),
                  row(B_WIDTH, C_ZB // B_WIDTH), full(1, LANE), full(D_MODEL, D_MODEL), full(1, D_MODEL), full(1, D_MODEL)],
        out_specs=[row(D_MODEL, 0), row(D_MODEL, 0), row(D_MODEL, 0)],
        out_shape=[jax.ShapeDtypeStruct((t, D_MODEL), F32), jax.ShapeDtypeStruct((t, D_MODEL), BF16),
                   jax.ShapeDtypeStruct((t, D_MODEL), F32)],
        compiler_params=_cp("parallel"))(x, oa, ob, h, h, norm_w, w_out, ln_g, ln_b)


def _layer_fwd(x, wt, conv_w, par, sinks_b, norm_w, w_out_bf, ln_g, ln_b, l):
    h = _matmul_nt(x, wt, tm=512, tn=1152, name=f"in_proj_{l}")
    q, k, v, bg, bgt = _dn_pre(h, conv_w, par, tt=512, name=f"dn_pre_{l}")
    u, w, tmat, qk = _dn_wy(q, k, v, bg, bgt, name=f"dn_wy_{l}")
    oa, vn, s_all = _dn_scan_fwd(q, k, u, w, qk, bg, name=f"dn_scan_{l}")
    ob = _swa_fwd(h, sinks_b, name=f"swa_fwd_{l}")
    xn, mixed, r = _out_ln(x, oa, ob, h, norm_w, w_out_bf, ln_g, ln_b, tm=256, name=f"out_ln_{l}")
    return xn, dict(x=x, h=h, q=q, k=k, v=v, bg=bg, bgt=bgt, w=w, tmat=tmat, qk=qk, vn=vn, oa=oa, s_all=s_all,
                    mixed=mixed, r=r)


def _loss_grad(xn, target, *, tm, name):
    t = xn.shape[0]

    def body(x_ref, t_ref, d_ref, l_ref):
        @pl.when(pl.program_id(0) == 0)
        def _():
            l_ref[...] = jnp.zeros_like(l_ref)

        err = x_ref[...] - t_ref[...]
        d_ref[...] = err * (1.0 / D_MODEL)
        l_ref[...] += 0.5 / D_MODEL * jnp.sum(err * err)

    row = pl.BlockSpec((tm, D_MODEL), lambda i: (i, 0))
    return pl.pallas_call(
        body, name=name, grid=(t // tm,), in_specs=[row, row],
        out_specs=[row, pl.BlockSpec((SUBLANE, LANE), lambda i: (0, 0))],
        out_shape=[jax.ShapeDtypeStruct((t, D_MODEL), F32), jax.ShapeDtypeStruct((SUBLANE, LANE), F32)],
        compiler_params=_cp("arbitrary"))(xn, target)


def _ln_out_bwd(dxn, r, mixed, ln_g, w_out, *, tm, name):
    t = dxn.shape[0]

    def body(dxn_ref, r_ref, mx_ref, g_ref, w_ref, dr_ref, dm_ref, dw_ref, dg_ref, db_ref):
        @pl.when(pl.program_id(0) == 0)
        def _():
            dw_ref[...] = jnp.zeros_like(dw_ref)
            dg_ref[...] = jnp.zeros_like(dg_ref)
            db_ref[...] = jnp.zeros_like(db_ref)

        rr = r_ref[...]
        xc = rr - jnp.mean(rr, -1, keepdims=True)
        rstd = lax.rsqrt(jnp.mean(xc * xc, -1, keepdims=True) + LN_EPS)
        xhat = xc * rstd
        dxn_v = dxn_ref[...]
        dxh = dxn_v * g_ref[...]
        dr = rstd * (dxh - jnp.mean(dxh, -1, keepdims=True) - xhat * jnp.mean(dxh * xhat, -1, keepdims=True))
        dr_ref[...] = dr
        dg_ref[...] += jnp.sum(dxn_v * xhat, axis=0, keepdims=True)
        db_ref[...] += jnp.sum(dxn_v, axis=0, keepdims=True)
        drb = dr.astype(BF16)
        dm_ref[...] = _dot_nt(drb, w_ref[...])
        dw_ref[...] += _dot_tn(mx_ref[...], drb)

    row = pl.BlockSpec((tm, D_MODEL), lambda i: (i, 0))
    full = lambda a, b: pl.BlockSpec((a, b), lambda i: (0, 0))
    big = jax.ShapeDtypeStruct((t, D_MODEL), F32)
    vec = jax.ShapeDtypeStruct((1, D_MODEL), F32)
    return pl.pallas_call(
        body, name=name, grid=(t // tm,),
        in_specs=[row, row, row, full(1, D_MODEL), full(D_MODEL, D_MODEL)],
        out_specs=[row, row, full(D_MODEL, D_MODEL), full(1, D_MODEL), full(1, D_MODEL)],
        out_shape=[big, big, jax.ShapeDtypeStruct((D_MODEL, D_MODEL), F32), vec, vec],
        compiler_params=_cp("arbitrary"))(dxn, r, mixed, ln_g, w_out)


def _dn_post_bwd(dm, oa, h, norm_w, *, tm, name):
    t = oa.shape[0]

    def body(dy_ref, o_ref, za_ref, nw_ref, do_ref, dza_ref, dnw_ref):
        @pl.when(pl.program_id(0) == 0)
        def _():
            dnw_ref[...] = jnp.zeros_like(dnw_ref)

        nw = nw_ref[...]
        dnw = jnp.zeros_like(nw)
        for hd in range(A_HEADS):
            sl = slice(hd * LANE, (hd + 1) * LANE)
            oh, za, dy = o_ref[:, sl], za_ref[:, sl], dy_ref[:, sl]
            rs = lax.rsqrt(jnp.mean(oh * oh, -1, keepdims=True) + RMS_EPS)
            nrm = oh * rs
            dza_ref[:, sl] = dy * nrm * nw * _dsilu(za)
            dn = dy * _silu(za)
            dnw = dnw + jnp.sum(dn * nrm, axis=0, keepdims=True)
            dnn = dn * nw
            do_ref[:, sl] = rs * dnn - oh * (rs * rs * rs) * jnp.mean(dnn * oh, -1, keepdims=True)
        dnw_ref[...] += dnw

    row = lambda c: pl.BlockSpec((tm, A_WIDTH), lambda i: (i, c))
    wide = jax.ShapeDtypeStruct((t, A_WIDTH), F32)
    return pl.pallas_call(
        body, name=name, grid=(t // tm,),
        in_specs=[row(0), row(0), row(C_ZA // A_WIDTH), pl.BlockSpec((1, LANE), lambda i: (0, 0))],
        out_specs=[row(0), row(C_ZA // A_WIDTH), pl.BlockSpec((1, LANE), lambda i: (0, 0))],
        out_shape=[wide, jax.ShapeDtypeStruct((t, DH_MAIN), F32), jax.ShapeDtypeStruct((1, LANE), F32)],
        compiler_params=_cp("arbitrary"))(dm, oa, h, norm_w)


def _dn_scan_bwd(q, k, w, qk, bg, do, *, name):
    t = q.shape[0]
    rows = SCAN_ROWS
    per = rows // CHUNK
    n = t // rows

    def body(q_ref, k_ref, w_ref, qk_ref, bg_ref, do_ref, dvn_ref, ds_ref, dstate):
        @pl.when(pl.program_id(0) == 0)
        def _():
            dstate[...] = jnp.zeros_like(dstate)

        heads = range(A_HEADS)
        sl = lambda hd: slice(hd * LANE, (hd + 1) * LANE)
        ds_cur = [dstate[hd] for hd in heads]
        for c in reversed(range(per)):
            rs = slice(c * CHUNK, (c + 1) * CHUNK)
            bg_v = bg_ref[rs, :]
            gcols = [_chunk_gates(bg_v, None, hd)[1] for hd in heads]
            glasts = [gc[CHUNK - 1:CHUNK, :] for gc in gcols]
            for hd in heads:
                ds_ref[c, hd] = ds_cur[hd]
            pdo = [_dot_tn(qk_ref[rs, hd * CHUNK:(hd + 1) * CHUNK], do_ref[rs, sl(hd)]) for hd in heads]
            qdo = [_dot_tn(q_ref[rs, sl(hd)] * jnp.exp(gcols[hd]), do_ref[rs, sl(hd)]) for hd in heads]
            dvns = [pdo[hd] + _dot(k_ref[rs, sl(hd)] * jnp.exp(glasts[hd] - gcols[hd]), ds_cur[hd]) for hd in heads]
            ds_cur = [qdo[hd] + jnp.exp(glasts[hd]) * ds_cur[hd] - _dot_tn(w_ref[rs, sl(hd)], dvns[hd])
                      for hd in heads]
            for hd in heads:
                dvn_ref[rs, sl(hd)] = dvns[hd]
        for hd in heads:
            dstate[hd] = ds_cur[hd]

    blk = pl.BlockSpec((rows, A_WIDTH), lambda i: (n - 1 - i, 0))
    return pl.pallas_call(
        body, name=name, grid=(n,),
        in_specs=[blk, blk, blk, pl.BlockSpec((rows, A_HEADS * CHUNK), lambda i: (n - 1 - i, 0)),
                  pl.BlockSpec((rows, LANE), lambda i: (n - 1 - i, 0)), blk],
        out_specs=[blk, pl.BlockSpec((per, A_HEADS, LANE, LANE), lambda i: (n - 1 - i, 0, 0, 0))],
        out_shape=[jax.ShapeDtypeStruct((t, A_WIDTH), F32),
                   jax.ShapeDtypeStruct((t // CHUNK, A_HEADS, LANE, LANE), F32)],
        scratch_shapes=[pltpu.VMEM((A_HEADS, LANE, LANE), F32)],
        compiler_params=_cp("arbitrary"))(q, k, w, qk, bg, do)


def _dn_chunk_bwd(q, k, v, vn, tmat, qk, bg, bgt, s_all, ds_all, dvn, do, *, name):
    t = q.shape[0]
    rows = WY_ROWS
    per = rows // CHUNK

    def body(q_ref, k_ref, v_ref, vn_ref, tm_ref, qk_ref, bg_ref, bgt_ref, s_ref, ds_ref, dvn_ref, do_ref,
             dq_ref, dk_ref, dv_ref, dbg_ref, dbgt_ref):
        causal, strict, _ = _chunk_masks()
        lane = lax.broadcasted_iota(jnp.int32, (CHUNK, LANE), 1)
        rowi = lax.broadcasted_iota(jnp.int32, (CHUNK, 1), 0)
        sub = lax.broadcasted_iota(jnp.int32, (SUBLANE, CHUNK), 0)
        rs = lambda c: slice(c * CHUNK, (c + 1) * CHUNK)
        sl = lambda hd: slice(hd * LANE, (hd + 1) * LANE)
        hs = lambda hd: slice(hd * CHUNK, (hd + 1) * CHUNK)
        for c0 in range(0, per, WY_GROUP):
            items = [(c, hd) for c in range(c0, c0 + WY_GROUP) for hd in range(A_HEADS)]
            at = lambda ref: [ref[rs(c), sl(hd)] for c, hd in items]
            qs, ks, vs, dos, vns, dvns = at(q_ref), at(k_ref), at(v_ref), at(do_ref), at(vn_ref), at(dvn_ref)
            tmhs = [tm_ref[rs(c), hs(hd)] for c, hd in items]
            ps = [qk_ref[rs(c), hs(hd)] for c, hd in items]
            gates = [_chunk_gates(bg_ref[rs(c), :], bgt_ref[:, rs(c)], hd) for c, hd in items]
            betas = [g[0] for g in gates]
            gcols = [g[1] for g in gates]
            dmats = [jnp.exp(jnp.where(causal, g[1] - g[2], NEG)) for g in gates]
            es = [jnp.exp(gc) for gc in gcols]
            glasts = [gc[CHUNK - 1:CHUNK, :] for gc in gcols]
            eks = [jnp.exp(gl - gc) for gl, gc in zip(glasts, gcols)]
            kbs = [kh * b for kh, b in zip(ks, betas)]
            vbs = [vh * b for vh, b in zip(vs, betas)]
            kbes = [kb * e for kb, e in zip(kbs, es)]

            a_s = [jnp.where(strict, _dot_nt(kb, kh) * dm, 0.0) for kb, kh, dm in zip(kbs, ks, dmats)]
            dps = [jnp.where(causal, _dot_nt(doh, vnh), 0.0) for doh, vnh in zip(dos, vns)]
            dqds = [_dot_nt(doh, s_ref[c, hd]) for doh, (c, hd) in zip(dos, items)]
            dkds = [_dot_nt(vnh, ds_ref[c, hd]) for vnh, (c, hd) in zip(vns, items)]
            dws = [-_dot_nt(dvnh, s_ref[c, hd]) for dvnh, (c, hd) in zip(dvns, items)]
            dvbs = [_dot_tn(tmh, dvnh) for tmh, dvnh in zip(tmhs, dvns)]
            dgts = [jnp.sum(s_ref[c, hd] * ds_ref[c, hd], keepdims=True) for c, hd in items]
            dts = [_dot_nt(dvnh, vb) + _dot_nt(dw, kbe) for dvnh, vb, dw, kbe in zip(dvns, vbs, dws, kbes)]
            dkbes = [_dot_tn(tmh, dw) for tmh, dw in zip(tmhs, dws)]
            xs = [_dot_nt(dt, tmh) for dt, tmh in zip(dts, tmhs)]
            das = [jnp.where(strict, -_dot_tn(tmh, x), 0.0) for tmh, x in zip(tmhs, xs)]
            dmas = [da * dm for da, dm in zip(das, dmats)]
            dmps = [dp * dm for dp, dm in zip(dps, dmats)]
            dkbs = [_dot(dma, kh) + dkbe * e for dma, kh, dkbe, e in zip(dmas, ks, dkbes, es)]
            for i, (c, hd) in enumerate(items):
                dq_ref[rs(c), sl(hd)] = _dot(dmps[i], ks[i]) + dqds[i] * es[i]
                dk_ref[rs(c), sl(hd)] = (_dot_tn(dmas[i], kbs[i]) + _dot_tn(dmps[i], qs[i]) + dkds[i] * eks[i]
                                         + dkbs[i] * betas[i])
                dv_ref[rs(c), sl(hd)] = dvbs[i] * betas[i]
            for c in range(c0, c0 + WY_GROUP):
                acc = jnp.zeros((CHUNK, LANE), F32)
                acc_t = jnp.zeros((SUBLANE, CHUNK), F32)
                for i, (ci, hd) in enumerate(items):
                    if ci != c:
                        continue
                    gmat = das[i] * a_s[i] + dps[i] * ps[i]
                    rk = jnp.sum(dkds[i] * ks[i], -1, keepdims=True) * eks[i]
                    de = (jnp.sum(dqds[i] * qs[i], -1, keepdims=True)
                          + jnp.sum(dkbes[i] * kbs[i], -1, keepdims=True))
                    dglast = jnp.sum(rk, keepdims=True) + dgts[i] * jnp.exp(glasts[i])
                    dgc = (jnp.sum(gmat, -1, keepdims=True) + de * es[i] - rk
                           + jnp.where(rowi == CHUNK - 1, dglast, 0.0))
                    dbeta = (jnp.sum(dkbs[i] * ks[i], -1, keepdims=True)
                             + jnp.sum(dvbs[i] * vs[i], -1, keepdims=True))
                    acc = acc + jnp.where(lane == hd, dbeta, 0.0) + jnp.where(lane == A_HEADS + hd, dgc, 0.0)
                    acc_t = acc_t + jnp.where(sub == A_HEADS + hd, -jnp.sum(gmat, axis=0, keepdims=True), 0.0)
                dbg_ref[rs(c), :] = acc
                dbgt_ref[:, rs(c)] = acc_t

    blk = pl.BlockSpec((rows, A_WIDTH), lambda i: (i, 0))
    half = pl.BlockSpec((rows, A_HEADS * CHUNK), lambda i: (i, 0))
    col = pl.BlockSpec((rows, LANE), lambda i: (i, 0))
    rowf = pl.BlockSpec((SUBLANE, rows), lambda i: (0, i))
    st = pl.BlockSpec((per, A_HEADS, LANE, LANE), lambda i: (i, 0, 0, 0))
    wide = jax.ShapeDtypeStruct((t, A_WIDTH), F32)
    return pl.pallas_call(
        body, name=name, grid=(t // rows,),
        in_specs=[blk, blk, blk, blk, half, half, col, rowf, st, st, blk, blk],
        out_specs=[blk, blk, blk, col, rowf],
        out_shape=[wide, wide, wide, jax.ShapeDtypeStruct((t, LANE), F32), jax.ShapeDtypeStruct((SUBLANE, t), F32)],
        compiler_params=_cp("parallel"))(q, k, v, vn, tmat, qk, bg, bgt, s_all, ds_all, dvn, do)


def _dn_pre_bwd(h, conv_w, par, dq, dk, dv, dbg, dbgt, *, tt, name):
    t = h.shape[0]
    cw = 3 * A_WIDTH
    hb = tt // SUBLANE

    def body(pre_ref, halo_ref, bgi_ref, cw_ref, par_ref, dq_ref, dk_ref, dv_ref, dbg_ref, dbgt_ref,
             dc_ref, dbgi_ref, dpar_ref):
        i = pl.program_id(0)

        @pl.when(i == 0)
        def _():
            dpar_ref[...] = jnp.zeros_like(dpar_ref)

        cur = pre_ref[...]
        before = jnp.where(i > 0, halo_ref[...], 0.0)
        c = _conv_fwd(cur, before, cw_ref[...])
        s = _silu(c)
        ds = _dsilu(c)
        for hd in range(A_HEADS):
            sl = slice(hd * LANE, (hd + 1) * LANE)
            for base, d_ref, scale in ((0, dq_ref, A_HEAD_DIM ** -0.5), (A_WIDTH, dk_ref, 1.0)):
                csl = slice(base + hd * LANE, base + (hd + 1) * LANE)
                tq = s[:, base + hd * LANE:base + (hd + 1) * LANE]
                dy = d_ref[:, sl]
                rq = lax.rsqrt(jnp.sum(tq * tq, -1, keepdims=True) + L2_EPS)
                dtq = scale * (rq * dy - tq * (rq * rq * rq) * jnp.sum(dy * tq, -1, keepdims=True))
                dc_ref[:, csl] = dtq * ds[:, base + hd * LANE:base + (hd + 1) * LANE]
        dc_ref[:, 2 * A_WIDTH:] = dv_ref[...] * ds[:, 2 * A_WIDTH:]
        raw = bgi_ref[...]
        lane = lax.broadcasted_iota(jnp.int32, raw.shape, 1)
        is_b = lane < A_HEADS
        is_a = (lane >= A_HEADS) & (lane < 2 * A_HEADS)
        rows_t = jnp.concatenate([dbgt_ref[...], jnp.zeros((LANE - SUBLANE, tt), F32)], axis=0)
        dbg_v = dbg_ref[...] + jnp.where(is_a, jnp.transpose(rows_t), 0.0)
        dbg_v = jnp.where(is_a, _dot_hi(_chunk_tri(tt, lower=False), jnp.where(is_a, dbg_v, 0.0)), dbg_v)
        beta = _sigmoid(raw)
        z = raw + par_ref[1:2, :]
        neg_ea = -jnp.exp(par_ref[0:1, :])
        g = neg_ea * _softplus(z)
        da = dbg_v * neg_ea * _sigmoid(z)
        dbgi_ref[...] = jnp.where(is_b, dbg_v * beta * (1.0 - beta), jnp.where(is_a, da, 0.0))
        dpar_ref[0:1, :] += jnp.sum(jnp.where(is_a, dbg_v * g, 0.0), axis=0, keepdims=True)
        dpar_ref[1:2, :] += jnp.sum(jnp.where(is_a, da, 0.0), axis=0, keepdims=True)

    wide = pl.BlockSpec((tt, A_WIDTH), lambda i: (i, 0))
    return pl.pallas_call(
        body, name=name, grid=(t // tt,),
        in_specs=[pl.BlockSpec((tt, cw), lambda i: (i, 0)),
                  pl.BlockSpec((SUBLANE, cw), lambda i: (jnp.maximum(i * hb - 1, 0), 0)),
                  pl.BlockSpec((tt, LANE), lambda i: (i, C_BG // LANE)),
                  pl.BlockSpec((CONV_K, cw), lambda i: (0, 0)),
                  pl.BlockSpec((SUBLANE, LANE), lambda i: (0, 0)),
                  wide, wide, wide, pl.BlockSpec((tt, LANE), lambda i: (i, 0)),
                  pl.BlockSpec((SUBLANE, tt), lambda i: (0, i))],
        out_specs=[pl.BlockSpec((tt, cw), lambda i: (i, 0)), pl.BlockSpec((tt, LANE), lambda i: (i, 0)),
                   pl.BlockSpec((SUBLANE, LANE), lambda i: (0, 0))],
        out_shape=[jax.ShapeDtypeStruct((t, cw), F32), jax.ShapeDtypeStruct((t, LANE), F32),
                   jax.ShapeDtypeStruct((SUBLANE, LANE), F32)],
        compiler_params=_cp("arbitrary"))(h, h, h, conv_w, par, dq, dk, dv, dbg, dbgt)


def _conv_bwd(dc, h, conv_w, dh, *, tt, name):
    t = dc.shape[0]
    cw = 3 * A_WIDTH
    hb = tt // SUBLANE
    nb = t // tt

    def body(dc_ref, after_ref, pre_ref, before_ref, cw_ref, dh_in_ref, dpre_ref, dcw_ref):
        i = pl.program_id(0)

        @pl.when(i == 0)
        def _():
            dcw_ref[...] = jnp.zeros_like(dcw_ref)

        dcv = dc_ref[...]
        after = jnp.where(i < nb - 1, after_ref[...], 0.0)
        cur = pre_ref[...]
        before = jnp.where(i > 0, before_ref[...], 0.0)
        w = cw_ref[...]
        acc = dcv * w[CONV_K - 1:CONV_K, :]
        dcw_ref[CONV_K - 1:CONV_K, :] += jnp.sum(dcv * cur, axis=0, keepdims=True)
        for s in range(1, CONV_K):
            j = CONV_K - 1 - s
            acc = acc + _shift_up(dcv, after, s) * w[j:j + 1, :]
            dcw_ref[j:j + 1, :] += jnp.sum(dcv * _shift_down(cur, before, s), axis=0, keepdims=True)
        dpre_ref[...] = acc

    return pl.pallas_call(
        body, name=name, grid=(nb,),
        in_specs=[pl.BlockSpec((tt, cw), lambda i: (i, 0)),
                  pl.BlockSpec((SUBLANE, cw), lambda i: (jnp.minimum((i + 1) * hb, t // SUBLANE - 1), 0)),
                  pl.BlockSpec((tt, cw), lambda i: (i, 0)),
                  pl.BlockSpec((SUBLANE, cw), lambda i: (jnp.maximum(i * hb - 1, 0), 0)),
                  pl.BlockSpec((CONV_K, cw), lambda i: (0, 0)), _ANY],
        out_specs=[pl.BlockSpec((tt, cw), lambda i: (i, 0)), pl.BlockSpec((SUBLANE, cw), lambda i: (0, 0))],
        out_shape=[jax.ShapeDtypeStruct(dh.shape, F32), jax.ShapeDtypeStruct((SUBLANE, cw), F32)],
        input_output_aliases={5: 0},
        compiler_params=_cp("arbitrary"))(dc, dc, h, h, conv_w, dh)


def _swa_bwd(h, dm, sinks_b, dh, *, name):
    t = h.shape[0]
    qspec, cur, prev = _swa_specs()

    def body(q_ref, kc_ref, kp_ref, vc_ref, vp_ref, zb_ref, dy_ref, sk_ref, dh_in_ref, dqz_ref, dk_ref, dv_ref, dsk_ref):
        n_blk = pl.program_id(0)

        @pl.when(n_blk == 0)
        def _():
            dk_ref[...] = jnp.zeros_like(dk_ref)
            dv_ref[...] = jnp.zeros_like(dv_ref)
            dsk_ref[...] = jnp.zeros_like(dsk_ref)

        kband = jnp.concatenate([kp_ref[...], kc_ref[...]], axis=0)
        vband = jnp.concatenate([vp_ref[...], vc_ref[...]], axis=0)
        scale = B_HEAD_DIM ** -0.5
        neg_dist = _swa_neg_dist(n_blk)
        dk_acc, dv_acc = [], []
        for hk in range(B_KV_HEADS):
            ksl = slice(hk * B_HEAD_DIM, (hk + 1) * B_HEAD_DIM)
            kh, vh = kband[:, ksl], vband[:, ksl]
            qs, p, ps, o = _swa_group_probs(q_ref, sk_ref, kh, vh, neg_dist, hk)
            zb, dy = _stack_heads(zb_ref, hk), _stack_heads(dy_ref, hk)
            dzb = dy * o * _dsilu(zb)
            do = dy * _silu(zb)
            delta = jnp.sum(do * o, -1, keepdims=True)
            ds = p * (_dot_nt(do, vh) - delta)
            dq = _dot(ds, kh) * scale
            dk_acc.append(_dot_tn(ds, qs))
            dv_acc.append(_dot_tn(p, do))
            dsink = ps * delta
            for g in range(B_GROUP):
                hq = hk * B_GROUP + g
                rows = slice(g * BLOCK, (g + 1) * BLOCK)
                qsl = slice(hq * B_HEAD_DIM, (hq + 1) * B_HEAD_DIM)
                dqz_ref[:, qsl] = dq[rows]
                dqz_ref[:, B_WIDTH + hq * B_HEAD_DIM:B_WIDTH + (hq + 1) * B_HEAD_DIM] = dzb[rows]
                dsk_ref[hq:hq + 1, :] += -jnp.sum(dsink[rows], keepdims=True)
        dkb = jnp.concatenate(dk_acc, axis=1)
        dvb = jnp.concatenate(dv_acc, axis=1)
        at_cur = pl.ds(pl.multiple_of(n_blk * BLOCK, BLOCK), BLOCK)
        at_prev = pl.ds(pl.multiple_of(jnp.maximum(n_blk - 1, 0) * BLOCK, BLOCK), BLOCK)
        dk_ref[at_prev, :] += dkb[:BLOCK]
        dv_ref[at_prev, :] += dvb[:BLOCK]
        dk_ref[at_cur, :] += dkb[BLOCK:]
        dv_ref[at_cur, :] += dvb[BLOCK:]

    narrow = jax.ShapeDtypeStruct((t, B_KV_WIDTH), F32)
    res = lambda a, b: pl.BlockSpec((a, b), lambda i: (0, 0))
    return pl.pallas_call(
        body, name=name, grid=(t // BLOCK,),
        in_specs=[qspec(C_QB), cur(C_KB), prev(C_KB), cur(C_VB), prev(C_VB), qspec(C_ZB),
                  pl.BlockSpec((BLOCK, B_WIDTH), lambda i: (i, 1)), res(B_Q_HEADS, LANE), _ANY],
        out_specs=[pl.BlockSpec((BLOCK, 2 * B_WIDTH), lambda i: (i, C_QB // (2 * B_WIDTH))),
                   res(t, B_KV_WIDTH), res(t, B_KV_WIDTH), res(B_Q_HEADS, LANE)],
        out_shape=[jax.ShapeDtypeStruct(dh.shape, F32), narrow, narrow, jax.ShapeDtypeStruct((B_Q_HEADS, LANE), F32)],
        input_output_aliases={8: 0},
        compiler_params=_cp("arbitrary"))(h, h, h, h, h, h, dm, sinks_b, dh)


def _matmul_tn(a, b, *, tk, tm, name):
    t, m = a.shape
    n = b.shape[1]

    def body(a_ref, b_ref, o_ref):
        @pl.when(pl.program_id(1) == 0)
        def _():
            o_ref[...] = jnp.zeros_like(o_ref)

        o_ref[...] += _dot_tn(a_ref[...], b_ref[...])

    return pl.pallas_call(
        body, name=name, grid=(m // tm, t // tk),
        in_specs=[pl.BlockSpec((tk, tm), lambda j, kk: (kk, j)), pl.BlockSpec((tk, n), lambda j, kk: (kk, 0))],
        out_specs=pl.BlockSpec((tm, n), lambda j, kk: (j, 0)),
        out_shape=jax.ShapeDtypeStruct((m, n), F32),
        compiler_params=_cp("parallel", "arbitrary"))(a, b)


def _in_proj_dx(dh_main, dh_tail, wt, dr, *, tm, name):
    t, n_main = dh_main.shape
    n_tail = dh_tail.shape[1]

    def body(a_ref, t_ref, wa_ref, wt_ref, r_ref, o_ref):
        o_ref[...] = _dot(a_ref[...], wa_ref[...]) + _dot(t_ref[...], wt_ref[...]) + DEEPNORM_ALPHA * r_ref[...]

    row = lambda w: pl.BlockSpec((tm, w), lambda i: (i, 0))
    return pl.pallas_call(
        body, name=name, grid=(t // tm,),
        in_specs=[row(n_main), row(n_tail), pl.BlockSpec((n_main, D_MODEL), lambda i: (0, 0)),
                  pl.BlockSpec((n_tail, D_MODEL), lambda i: (n_main // n_tail, 0)), row(D_MODEL)],
        out_specs=row(D_MODEL),
        out_shape=jax.ShapeDtypeStruct((t, D_MODEL), F32),
        compiler_params=_cp("parallel"))(dh_main, dh_tail, wt, wt, dr)


def _layer_bwd(dxn, res, wt, conv_w, par, sinks_b, norm_w, w_out_bf, ln_g, l):
    dr, dm, dw_out, dln_g, dln_b = _ln_out_bwd(dxn, res["r"], res["mixed"], ln_g, w_out_bf, tm=256, name=f"ln_out_bwd_{l}")
    h = res["h"]
    do, dh, dnw = _dn_post_bwd(dm, res["oa"], h, norm_w, tm=512, name=f"dn_post_bwd_{l}")
    dvn, ds_all = _dn_scan_bwd(res["q"], res["k"], res["w"], res["qk"], res["bg"], do, name=f"dn_scan_bwd_{l}")
    dq, dk, dv, dbg, dbgt = _dn_chunk_bwd(res["q"], res["k"], res["v"], res["vn"], res["tmat"], res["qk"], res["bg"],
                                          res["bgt"], res["s_all"], ds_all, dvn, do, name=f"dn_chunk_bwd_{l}")
    dc, dbgi, dpar = _dn_pre_bwd(h, conv_w, par, dq, dk, dv, dbg, dbgt, tt=512, name=f"dn_pre_bwd_{l}")
    dh, dcw = _conv_bwd(dc, h, conv_w, dh, tt=512, name=f"conv_bwd_{l}")
    dh, dkb, dvb, dsk = _swa_bwd(h, dm, sinks_b, dh, name=f"swa_bwd_{l}")
    dh_tail = jnp.concatenate([dkb, dvb, dbgi], axis=1)
    dwt_main = _matmul_tn(dh, res["x"], tk=512, tm=768, name=f"in_proj_dw_{l}")
    dwt_tail = _matmul_tn(dh_tail, res["x"], tk=512, tm=P_COLS - DH_MAIN, name=f"in_proj_dw_tail_{l}")
    dx = _in_proj_dx(dh, dh_tail, wt, dr, tm=256, name=f"in_proj_dx_{l}")
    grads = dict(w_in=(dwt_main, dwt_tail), conv_w=dcw[:CONV_K], a_log=dpar[0, A_HEADS:2 * A_HEADS],
                 dt_bias=dpar[1, A_HEADS:2 * A_HEADS], norm_w=dnw[0], sinks=dsk[:, 0], w_out=dw_out,
                 ln_g=dln_g[0], ln_b=dln_b[0])
    return dx, grads


def _local_step(x, target, wp, conv_w, a_log, dt_bias, norm_w, sinks, w_out_bf, ln_g, ln_b):
    per_layer = []
    saved = []
    for l in range(DEPTH):
        args = (wp[l], conv_w[l], _gate_params(a_log[l], dt_bias[l]),
                jnp.broadcast_to(sinks[l][:, None], (B_Q_HEADS, LANE)), norm_w[l][None], w_out_bf[l])
        per_layer.append(args)
        x, res = _layer_fwd(x, *args, ln_g[l][None], ln_b[l][None], l)
        saved.append(res)
    dx, loss_tile = _loss_grad(x, target, tm=512, name="loss_grad")
    grads = [None] * DEPTH
    for l in reversed(range(DEPTH)):
        dx, grads[l] = _layer_bwd(dx, saved[l], *per_layer[l], ln_g[l][None], l)
    return loss_tile, dx, grads


_ANY = pl.BlockSpec(memory_space=pl.ANY)
_MESH = pl.DeviceIdType.MESH


def _pair_exchange(arrays, by_layer, *, name):
    n = len(arrays)

    def body(*refs):
        src, dst, (send_sems, recv_sems) = refs[:n], refs[n:2 * n], refs[2 * n:]
        c = lax.axis_index("c")
        sibling = (lax.axis_index("x"), lax.axis_index("y"), 1 - c)
        copies = [pltpu.make_async_remote_copy(src_ref=src[k].at[1 - c] if by_layer[k] else src[k], dst_ref=dst[k],
                                               send_sem=send_sems.at[k], recv_sem=recv_sems.at[k],
                                               device_id=sibling, device_id_type=_MESH)
                  for k in range(n)]
        for cp in copies:
            cp.start()
        for cp in copies:
            cp.wait()

    return pl.pallas_call(
        body, name=name, in_specs=[_ANY] * n, out_specs=[_ANY] * n,
        out_shape=[jax.ShapeDtypeStruct(a.shape[1:] if bl else a.shape, a.dtype) for a, bl in zip(arrays, by_layer)],
        scratch_shapes=[pltpu.SemaphoreType.DMA((n,)), pltpu.SemaphoreType.DMA((n,))])(*arrays)


def _chip_exchange(arrays, scatter, *, name):
    n = len(arrays)
    shapes = [a.shape if sc else (N_SHARD,) + a.shape for a, sc in zip(arrays, scatter)]

    def body(*refs):
        src, dst = refs[:n], refs[n:2 * n]
        send_sems, recv_sems, local_sems = refs[2 * n:]
        x, y, c = lax.axis_index("x"), lax.axis_index("y"), lax.axis_index("c")
        me = 2 * x + y
        chips = [(1 - x, y), (x, 1 - y), (1 - x, 1 - y)]
        local = [pltpu.make_async_copy(src[k].at[me] if scatter[k] else src[k], dst[k].at[me], local_sems.at[k])
                 for k in range(n)]
        for cp in local:
            cp.start()
        sends = []
        for k in range(n):
            for j, (px, py) in enumerate(chips):
                sends.append(pltpu.make_async_remote_copy(
                    src_ref=src[k].at[2 * px + py] if scatter[k] else src[k], dst_ref=dst[k].at[me],
                    send_sem=send_sems.at[3 * k + j], recv_sem=recv_sems.at[3 * k + j],
                    device_id=(px, py, c), device_id_type=_MESH))
        for cp in sends:
            cp.start()
        for k in range(n):
            for j, (px, py) in enumerate(chips):
                pltpu.make_async_remote_copy(
                    src_ref=src[k].at[2 * px + py] if scatter[k] else src[k], dst_ref=dst[k].at[2 * px + py],
                    send_sem=send_sems.at[3 * k + j], recv_sem=recv_sems.at[3 * k + j],
                    device_id=(px, py, c), device_id_type=_MESH).wait_recv()
        for cp in sends:
            cp.wait_send()
        for cp in local:
            cp.wait()

    return pl.pallas_call(
        body, name=name, in_specs=[_ANY] * n, out_specs=[_ANY] * n,
        out_shape=[jax.ShapeDtypeStruct(s, a.dtype) for s, a in zip(shapes, arrays)],
        scratch_shapes=[pltpu.SemaphoreType.DMA((3 * n,)), pltpu.SemaphoreType.DMA((3 * n,)),
                        pltpu.SemaphoreType.DMA((n,))])(*arrays)


def _elementwise(fn, ins, n_out, block, *, name, layer=None):
    shape = ins[-1].shape
    grid = tuple(s // b for s, b in zip(shape, block))
    n_in = len(ins)

    def body(*refs):
        refs = refs[1:] if layer is not None else refs
        outs = fn(*[r[...] for r in refs[:n_in]])
        for o_ref, val in zip(refs[n_in:], outs):
            o_ref[...] = val

    shp = [jax.ShapeDtypeStruct(shape, F32)] * n_out
    cp = _cp(*["parallel"] * 3)
    if layer is None:
        spec = pl.BlockSpec(block, lambda i, j, k: (i, j, k))
        return pl.pallas_call(body, name=name, grid=grid, in_specs=[spec] * n_in, out_specs=[spec] * n_out,
                              out_shape=shp, compiler_params=cp)(*ins)
    spec = pl.BlockSpec(block, lambda i, j, k, lr: (i, j, k))
    first = pl.BlockSpec((None,) + tuple(block), lambda i, j, k, lr: (lr[0], i, j, k))
    return pl.pallas_call(
        body, name=name, out_shape=shp, compiler_params=cp,
        grid_spec=pltpu.PrefetchScalarGridSpec(num_scalar_prefetch=1, grid=grid, in_specs=[first] + [spec] * (n_in - 1),
                                               out_specs=[spec] * n_out))(jnp.reshape(layer, (1,)).astype(jnp.int32), *ins)


def _add2(a, b, block, *, name, layer=None):
    return _elementwise(lambda p, q: (p + q,), [a, b], 1, block, name=name, layer=layer)[0]


def _sum4(a, tc, *, name):
    _, rows, cols = a.shape

    def body(a_ref, o_ref):
        o_ref[...] = ((a_ref[0] + a_ref[1]) + a_ref[2]) + a_ref[3]

    return pl.pallas_call(body, name=name, grid=(cols // tc,),
                          in_specs=[pl.BlockSpec((N_SHARD, rows, tc), lambda i: (0, 0, i))],
                          out_specs=pl.BlockSpec((rows, tc), lambda i: (0, i)),
                          out_shape=jax.ShapeDtypeStruct((rows, cols), a.dtype), compiler_params=_cp("parallel"))(a)


def _adamw_math(w, g, m, v):
    mn = ADAM_B1 * m + (1.0 - ADAM_B1) * g
    vn = ADAM_B2 * v + (1.0 - ADAM_B2) * (g * g)
    m_hat = mn / (1.0 - ADAM_B1 ** ADAM_STEP)
    v_hat = vn / (1.0 - ADAM_B2 ** ADAM_STEP)
    return -ADAM_LR * (m_hat / (jnp.sqrt(v_hat) + ADAM_EPS) + ADAM_WD * w), mn, vn


def _adamw(w, g, m, v, block, *, name):
    return _elementwise(_adamw_math, [w, g, m, v], 3, block, name=name)


def _to_kernel_order(wt):
    pad = jnp.zeros((LANE - 2 * A_HEADS,) + wt.shape[1:], wt.dtype)
    return jnp.concatenate([wt[0:2048], wt[2056:2568], wt[2824:3336], wt[2568:2696], wt[2696:2824],
                            wt[2048:2056], pad], axis=0)


def _from_kernel_order(main, tail):
    return jnp.concatenate([main[0:2048], tail[C_BG - DH_MAIN:C_BG - DH_MAIN + 2 * A_HEADS],
                            main[C_QB:C_QB + B_WIDTH], tail[0:B_KV_WIDTH], tail[B_KV_WIDTH:2 * B_KV_WIDTH],
                            main[C_ZB:C_ZB + B_WIDTH]], axis=0)


def _gate_params(a_log, dt_bias):
    par = jnp.zeros((SUBLANE, LANE), F32)
    par = par.at[0, A_HEADS:2 * A_HEADS].set(a_log)
    return par.at[1, A_HEADS:2 * A_HEADS].set(dt_bias)


SMALL = ("conv_w", "a_log", "dt_bias", "norm_w", "sinks", "ln_g", "ln_b")


def _pack(parts, cols):
    flat = jnp.concatenate([p.reshape(-1) for p in parts])
    rows = -(-flat.shape[0] // cols)
    return jnp.pad(flat, (0, rows * cols - flat.shape[0])).reshape(rows, cols)


def _unpack(packed, shapes):
    flat = packed.reshape(-1)
    out, at = [], 0
    for s in shapes:
        n = math.prod(s)
        out.append(flat[at:at + n].reshape(s))
        at += n
    return out


def kernel(x, w_in, conv_w, a_log, dt_bias, norm_w, sinks, w_out, ln_g, ln_b, loss_target, m_w_in, m_conv_w, m_a_log, m_dt_bias, m_norm_w, m_sinks, m_w_out, m_ln_g, m_ln_b, v_w_in, v_conv_w, v_a_log, v_dt_bias, v_norm_w, v_sinks, v_w_out, v_ln_g, v_ln_b):
    xi, yi, ci = lax.axis_index("x"), lax.axis_index("y"), lax.axis_index("c")
    me = 2 * xi + yi

    to_t = lambda a: jnp.transpose(a, (2, 0, 1))
    from_t = lambda a: jnp.transpose(a, (1, 2, 0))

    g_in, g_out, g_conv = _chip_exchange([to_t(w_in).astype(BF16), w_out.astype(BF16), conv_w], [False] * 3,
                                         name="gather_weights")
    wt_all = g_in.reshape(IN_COLS, DEPTH, D_MODEL)
    wt = [_to_kernel_order(wt_all[:, l]) for l in range(DEPTH)]
    w_out_full = jnp.moveaxis(g_out, 0, 1).reshape(DEPTH, D_MODEL, D_MODEL)
    conv_full = jnp.moveaxis(g_conv, 0, 2).reshape(DEPTH, CONV_K, 3 * A_WIDTH)

    loss_tile, dx, grads = _local_step(x[0], loss_target[0], wt, conv_full, a_log, dt_bias, norm_w, sinks,
                                       w_out_full, ln_g, ln_b)
    loss = lax.psum(loss_tile[0, 0], ("x", "y", "c"))

    gin = jnp.stack([_from_kernel_order(*g["w_in"]).reshape(N_SHARD, IN_SHARD, D_MODEL) for g in grads])
    gout = jnp.stack([g["w_out"] for g in grads]).reshape(DEPTH, N_SHARD, OUT_SHARD, D_MODEL)
    small_shapes = [(DEPTH,) + grads[0][nm].shape for nm in SMALL]
    gsmall = _pack([jnp.stack([g[nm] for g in grads]) for nm in SMALL], D_MODEL)[None]
    in_blk, out_blk = (1, IN_SHARD, D_MODEL), (1, OUT_SHARD, D_MODEL)

    r_in, r_out, r_small = _pair_exchange([gin, gout, gsmall], [True, True, False], name="pair_reduce")
    p_in = _add2(gin, r_in, in_blk, name="pair_add_in", layer=ci)
    p_out = _add2(gout, r_out, out_blk, name="pair_add_out", layer=ci)
    p_small = _add2(gsmall, r_small, gsmall.shape, name="pair_add_small")[0]
    q_in, q_out, q_small = _chip_exchange([p_in, p_out, p_small], [True, True, False], name="chip_reduce")
    s_in = _sum4(q_in, 2 * LANE, name="chip_sum_in")
    s_out = _sum4(q_out, D_MODEL, name="chip_sum_out")
    s_small = _sum4(q_small, D_MODEL, name="chip_sum_small")
    o_in, o_out = _pair_exchange([s_in, s_out], [False, False], name="pair_share")
    by_layer = lambda mine, other, ax: jnp.where(ci == 0, jnp.stack([mine, other], ax), jnp.stack([other, mine], ax))
    grad_in_t = by_layer(s_in, o_in, 1)
    grad_out = by_layer(s_out, o_out, 0)
    gs = dict(zip(SMALL, _unpack(s_small, small_shapes)))
    gs["conv_w"] = lax.dynamic_slice_in_dim(gs["conv_w"], me * CONV_SHARD, CONV_SHARD, axis=2)

    adam_in_blk = (IN_SHARD // 6, DEPTH, D_MODEL)
    d_in, nm_in, nv_in = (from_t(o) for o in _adamw(to_t(w_in), grad_in_t, to_t(m_w_in), to_t(v_w_in), adam_in_blk,
                                                    name="adamw_in"))
    d_out, nm_out, nv_out = _adamw(w_out, grad_out, m_w_out, v_w_out, out_blk, name="adamw_out")
    ws = dict(conv_w=conv_w, a_log=a_log, dt_bias=dt_bias, norm_w=norm_w, sinks=sinks, ln_g=ln_g, ln_b=ln_b)
    ms = dict(conv_w=m_conv_w, a_log=m_a_log, dt_bias=m_dt_bias, norm_w=m_norm_w, sinks=m_sinks, ln_g=m_ln_g, ln_b=m_ln_b)
    vs = dict(conv_w=v_conv_w, a_log=v_a_log, dt_bias=v_dt_bias, norm_w=v_norm_w, sinks=v_sinks, ln_g=v_ln_g, ln_b=v_ln_b)
    shard_shapes = [ws[nm].shape for nm in SMALL]
    packed = [_pack([d[nm] for nm in SMALL], LANE)[None] for d in (ws, gs, ms, vs)]
    d_s, nm_s, nv_s = (dict(zip(SMALL, _unpack(o, shard_shapes)))
                       for o in _adamw(*packed, packed[0].shape, name="adamw_small"))

    def in_order(big_in, small, big_out):
        return (big_in, small["conv_w"], small["a_log"], small["dt_bias"], small["norm_w"], small["sinks"], big_out,
                small["ln_g"], small["ln_b"])

    return (loss, dx[None], *in_order(from_t(grad_in_t), gs, grad_out), *in_order(d_in, d_s, d_out),
            *in_order(nm_in, nm_s, nm_out), *in_order(nv_in, nv_s, nv_out))
```

```python
import math

import jax
import jax.numpy as jnp
from jax import lax
from jax.experimental import pallas as pl
from jax.experimental.pallas import tpu as pltpu

F32 = jnp.float32
BF16 = jnp.bfloat16
HI = lax.Precision.HIGHEST

D_MODEL = 1024
DEPTH = 2
A_HEADS = 4
A_HEAD_DIM = 128
A_WIDTH = 512
CONV_K = 4
CHUNK = 64
B_Q_HEADS = 8
B_KV_HEADS = 2
B_HEAD_DIM = 64
B_GROUP = 4
B_WIDTH = 512
B_KV_WIDTH = 128
BLOCK = 128
IN_COLS = 3336
DEEPNORM_ALPHA = (2 * DEPTH) ** 0.25
LN_EPS = 1e-5
RMS_EPS = 1e-6
L2_EPS = 1e-6
ADAM_LR = 0.001
ADAM_B1 = 0.9
ADAM_B2 = 0.999
ADAM_EPS = 1e-08
ADAM_WD = 0.01
ADAM_STEP = 10

N_SHARD = 4
IN_SHARD = IN_COLS // N_SHARD
OUT_SHARD = D_MODEL // N_SHARD
CONV_SHARD = 3 * A_WIDTH // N_SHARD
IN_PAD = -(-IN_SHARD // 16) * 16

P_COLS = 3456
C_PRE = 0
C_ZA = 1536
C_QB = 2048
C_ZB = 2560
C_KB = 3072
C_VB = 3200
C_BG = 3328
DH_MAIN = C_KB
LANE = 128
SUBLANE = 8
VMEM_LIMIT = 56 * 1024 * 1024
ALIBI = tuple(2.0 ** (-8.0 * (h + 1) / B_Q_HEADS) for h in range(B_Q_HEADS))
NEG = -1e30


def _cp(*sem):
    return pltpu.CompilerParams(dimension_semantics=sem, vmem_limit_bytes=VMEM_LIMIT)


def _dot(a, b):
    return jnp.dot(a.astype(BF16), b.astype(BF16), preferred_element_type=F32)


def _dot_nt(a, b):
    return lax.dot_general(a.astype(BF16), b.astype(BF16), (((1,), (1,)), ((), ())),
                           preferred_element_type=F32)


def _dot_tn(a, b):
    return lax.dot_general(a.astype(BF16), b.astype(BF16), (((0,), (0,)), ((), ())),
                           preferred_element_type=F32)


def _dot_hi(a, b):
    return jnp.dot(a, b, precision=HI, preferred_element_type=F32)


def _sigmoid(x):
    return jax.nn.sigmoid(x)


def _silu(x):
    return x * _sigmoid(x)


def _dsilu(x):
    s = _sigmoid(x)
    return s * (1.0 + x * (1.0 - s))


def _softplus(x):
    return jnp.maximum(x, 0.0) + jnp.log(1.0 + jnp.exp(-jnp.abs(x)))


def _shift_down(cur, before, s):
    if s == 0:
        return cur
    r = pltpu.roll(cur, s, 0)
    rb = pltpu.roll(before, s, 0)
    row = lax.broadcasted_iota(jnp.int32, before.shape, 0)
    head = jnp.where(row < s, rb, r[0:SUBLANE])
    return jnp.concatenate([head, r[SUBLANE:]], axis=0)


def _shift_up(cur, after, s):
    if s == 0:
        return cur
    n = cur.shape[0]
    r = pltpu.roll(cur, n - s, 0)
    ra = pltpu.roll(after, SUBLANE - s, 0)
    row = lax.broadcasted_iota(jnp.int32, after.shape, 0)
    tail = jnp.where(row >= SUBLANE - s, ra, r[n - SUBLANE:])
    return jnp.concatenate([r[:n - SUBLANE], tail], axis=0)


def _conv_fwd(cur, before, w):
    acc = cur * w[CONV_K - 1:CONV_K, :]
    for s in range(1, CONV_K):
        acc = acc + _shift_down(cur, before, s) * w[CONV_K - 1 - s:CONV_K - s, :]
    return acc


def _matmul_nt(a, bt, *, tm, tn, name):
    m, k = a.shape
    n = bt.shape[0]

    def body(a_ref, b_ref, o_ref):
        o_ref[...] = _dot_nt(a_ref[...], b_ref[...])

    return pl.pallas_call(
        body, name=name, grid=(m // tm, n // tn),
        in_specs=[pl.BlockSpec((tm, k), lambda i, j: (i, 0)), pl.BlockSpec((tn, k), lambda i, j: (j, 0))],
        out_specs=pl.BlockSpec((tm, tn), lambda i, j: (i, j)),
        out_shape=jax.ShapeDtypeStruct((m, n), F32),
        compiler_params=_cp("parallel", "parallel"))(a, bt)


def _dn_pre(h, conv_w, par, *, tt, name):
    t = h.shape[0]
    cw = 3 * A_WIDTH
    hb = tt // SUBLANE

    def body(pre_ref, halo_ref, bgi_ref, cw_ref, par_ref, q_ref, k_ref, v_ref, bg_ref, bgt_ref):
        i = pl.program_id(0)
        cur = pre_ref[...]
        before = jnp.where(i > 0, halo_ref[...], 0.0)
        s = _silu(_conv_fwd(cur, before, cw_ref[...]))
        for hd in range(A_HEADS):
            sl = slice(hd * LANE, (hd + 1) * LANE)
            tq = s[:, hd * LANE:(hd + 1) * LANE]
            q_ref[:, sl] = tq * (lax.rsqrt(jnp.sum(tq * tq, -1, keepdims=True) + L2_EPS) * (A_HEAD_DIM ** -0.5))
            tk = s[:, A_WIDTH + hd * LANE:A_WIDTH + (hd + 1) * LANE]
            k_ref[:, sl] = tk * lax.rsqrt(jnp.sum(tk * tk, -1, keepdims=True) + L2_EPS)
        v_ref[...] = s[:, 2 * A_WIDTH:]
        raw = bgi_ref[...]
        lane = lax.broadcasted_iota(jnp.int32, raw.shape, 1)
        is_a = (lane >= A_HEADS) & (lane < 2 * A_HEADS)
        g = jnp.where(is_a, -jnp.exp(par_ref[0:1, :]) * _softplus(raw + par_ref[1:2, :]), 0.0)
        gc = _dot_hi(_chunk_tri(tt, lower=True), g)
        bg = jnp.where(lane < A_HEADS, _sigmoid(raw), gc)
        bg_ref[...] = bg
        bgt_ref[...] = jnp.transpose(bg)[0:SUBLANE, :]

    wide = jax.ShapeDtypeStruct((t, A_WIDTH), F32)
    return pl.pallas_call(
        body, name=name, grid=(t // tt,),
        in_specs=[pl.BlockSpec((tt, cw), lambda i: (i, 0)),
                  pl.BlockSpec((SUBLANE, cw), lambda i: (jnp.maximum(i * hb - 1, 0), 0)),
                  pl.BlockSpec((tt, LANE), lambda i: (i, C_BG // LANE)),
                  pl.BlockSpec((CONV_K, cw), lambda i: (0, 0)),
                  pl.BlockSpec((SUBLANE, LANE), lambda i: (0, 0))],
        out_specs=[pl.BlockSpec((tt, A_WIDTH), lambda i: (i, 0))] * 3
        + [pl.BlockSpec((tt, LANE), lambda i: (i, 0)), pl.BlockSpec((SUBLANE, tt), lambda i: (0, i))],
        out_shape=[wide, wide, wide, jax.ShapeDtypeStruct((t, LANE), F32), jax.ShapeDtypeStruct((SUBLANE, t), F32)],
        compiler_params=_cp("parallel"))(h, h, h, conv_w, par)


def _chunk_tri(n, lower):
    r = lax.broadcasted_iota(jnp.int32, (n, n), 0)
    c = lax.broadcasted_iota(jnp.int32, (n, n), 1)
    shift = CHUNK.bit_length() - 1
    same = jnp.right_shift(r, shift) == jnp.right_shift(c, shift)
    return (same & ((c <= r) if lower else (c >= r))).astype(F32)


def _chunk_masks():
    r = lax.broadcasted_iota(jnp.int32, (CHUNK, CHUNK), 0)
    c = lax.broadcasted_iota(jnp.int32, (CHUNK, CHUNK), 1)
    return r >= c, r > c, r == c


def _split(a):
    hi = a.astype(BF16)
    return hi, (a - hi.astype(F32)).astype(BF16)


def _dot3(a, b):
    (ah, al), (bh, bl) = a, b
    d = lambda p, q: jnp.dot(p, q, preferred_element_type=F32)
    return d(ah, bh) + (d(ah, bl) + d(al, bh))


def _tri_inv_many(a_list, eye):
    d = lambda p, q: jnp.dot(p, q, preferred_element_type=F32)
    p = [(-a).astype(BF16) for a in a_list]
    tm = [eye - a for a in a_list]
    for _ in range(5):
        pf = [d(pi, pi) for pi in p]
        p = [x.astype(BF16) for x in pf]
        tm = [t + d(t.astype(BF16), pi) for t, pi in zip(tm, p)]
    ms = [_split(eye + a) for a in a_list]
    res = [eye - _dot3(m, _split(t)) for m, t in zip(ms, tm)]
    return [t + d(t.astype(BF16), r.astype(BF16)) for t, r in zip(tm, res)]


def _chunk_gates(bg_v, bgt_v, hd):
    return (bg_v[:, hd:hd + 1], bg_v[:, A_HEADS + hd:A_HEADS + hd + 1],
            None if bgt_v is None else bgt_v[A_HEADS + hd:A_HEADS + hd + 1, :])


WY_ROWS = 256
SCAN_ROWS = 128
WY_GROUP = 2


def _dn_wy(q, k, v, bg, bgt, *, name):
    t = q.shape[0]
    rows = WY_ROWS

    def body(q_ref, k_ref, v_ref, bg_ref, bgt_ref, u_ref, w_ref, tm_ref, qk_ref):
        causal, strict, diag = _chunk_masks()
        eye = diag.astype(F32)
        for c0 in range(0, rows // CHUNK, WY_GROUP):
            items = [(c, hd) for c in range(c0, c0 + WY_GROUP) for hd in range(A_HEADS)]
            rs = lambda c: slice(c * CHUNK, (c + 1) * CHUNK)
            sl = lambda hd: slice(hd * LANE, (hd + 1) * LANE)
            hs = lambda hd: slice(hd * CHUNK, (hd + 1) * CHUNK)
            gates = [_chunk_gates(bg_ref[rs(c), :], bgt_ref[:, rs(c)], hd) for c, hd in items]
            dms = [jnp.exp(jnp.where(causal, gcol - grow, NEG)) for _, gcol, grow in gates]
            kbs = [k_ref[rs(c), sl(hd)] * g[0] for (c, hd), g in zip(items, gates)]
            a_list = [jnp.where(strict, _dot_nt(kb, k_ref[rs(c), sl(hd)]) * dm, 0.0)
                      for (c, hd), kb, dm in zip(items, kbs, dms)]
            for (c, hd), dm in zip(items, dms):
                qk_ref[rs(c), hs(hd)] = jnp.where(
                    causal, _dot_nt(q_ref[rs(c), sl(hd)], k_ref[rs(c), sl(hd)]) * dm, 0.0)
            tms = _tri_inv_many(a_list, eye)
            for (c, hd), g, kb, tmat in zip(items, gates, kbs, tms):
                tm_ref[rs(c), hs(hd)] = tmat
                u_ref[rs(c), sl(hd)] = _dot(tmat, v_ref[rs(c), sl(hd)] * g[0])
                w_ref[rs(c), sl(hd)] = _dot(tmat, kb * jnp.exp(g[1])).astype(BF16)

    blk = pl.BlockSpec((rows, A_WIDTH), lambda i: (i, 0))
    half = pl.BlockSpec((rows, A_HEADS * CHUNK), lambda i: (i, 0))
    return pl.pallas_call(
        body, name=name, grid=(t // rows,),
        in_specs=[blk, blk, blk, pl.BlockSpec((rows, LANE), lambda i: (i, 0)),
                  pl.BlockSpec((SUBLANE, rows), lambda i: (0, i))],
        out_specs=[blk, blk, half, half],
        out_shape=[jax.ShapeDtypeStruct((t, A_WIDTH), F32), jax.ShapeDtypeStruct((t, A_WIDTH), BF16),
                   jax.ShapeDtypeStruct((t, A_HEADS * CHUNK), F32), jax.ShapeDtypeStruct((t, A_HEADS * CHUNK), F32)],
        compiler_params=_cp("parallel"))(q, k, v, bg, bgt)


def _dn_scan_fwd(q, k, u, w, qk, bg, *, name):
    t = q.shape[0]
    rows = SCAN_ROWS
    per = rows // CHUNK

    def body(q_ref, k_ref, u_ref, w_ref, qk_ref, bg_ref, o_ref, vn_ref, s_ref, state):
        @pl.when(pl.program_id(0) == 0)
        def _():
            state[...] = jnp.zeros_like(state)

        heads = range(A_HEADS)
        sl = lambda hd: slice(hd * LANE, (hd + 1) * LANE)
        s_cur = [state[hd] for hd in heads]
        for c in range(per):
            rs = slice(c * CHUNK, (c + 1) * CHUNK)
            bg_v = bg_ref[rs, :]
            gcols = [_chunk_gates(bg_v, None, hd)[1] for hd in heads]
            glasts = [gc[CHUNK - 1:CHUNK, :] for gc in gcols]
            for hd in heads:
                s_ref[c, hd] = s_cur[hd]
            vns = [u_ref[rs, sl(hd)] - _dot(w_ref[rs, sl(hd)], s_cur[hd]) for hd in heads]
            qss = [_dot(q_ref[rs, sl(hd)] * jnp.exp(gcols[hd]), s_cur[hd]) for hd in heads]
            s_cur = [s_cur[hd] * jnp.exp(glasts[hd])
                     + _dot_tn(k_ref[rs, sl(hd)] * jnp.exp(glasts[hd] - gcols[hd]), vns[hd]) for hd in heads]
            for hd in heads:
                vn_ref[rs, sl(hd)] = vns[hd]
                o_ref[rs, sl(hd)] = qss[hd] + _dot(qk_ref[rs, hd * CHUNK:(hd + 1) * CHUNK], vns[hd])
        for hd in heads:
            state[hd] = s_cur[hd]

    blk = pl.BlockSpec((rows, A_WIDTH), lambda i: (i, 0))
    half = pl.BlockSpec((rows, A_HEADS * CHUNK), lambda i: (i, 0))
    wide = jax.ShapeDtypeStruct((t, A_WIDTH), F32)
    return pl.pallas_call(
        body, name=name, grid=(t // rows,),
        in_specs=[blk, blk, blk, blk, half, pl.BlockSpec((rows, LANE), lambda i: (i, 0))],
        out_specs=[blk, blk, pl.BlockSpec((per, A_HEADS, LANE, LANE), lambda i: (i, 0, 0, 0))],
        out_shape=[wide, wide, jax.ShapeDtypeStruct((t // CHUNK, A_HEADS, LANE, LANE), F32)],
        scratch_shapes=[pltpu.VMEM((A_HEADS, LANE, LANE), F32)],
        compiler_params=_cp("arbitrary"))(q, k, u, w, qk, bg)


def _swa_neg_dist(n_blk):
    qi = lax.broadcasted_iota(jnp.int32, (BLOCK, 2 * BLOCK), 0)
    si = lax.broadcasted_iota(jnp.int32, (BLOCK, 2 * BLOCK), 1)
    dist = qi + BLOCK - si
    mask = (dist >= 0) & (dist < BLOCK) & ((si >= BLOCK) | (n_blk > 0))
    return jnp.where(mask, -dist.astype(F32), NEG)


def _stack_heads(ref, hk):
    return jnp.concatenate([ref[:, h * B_HEAD_DIM:(h + 1) * B_HEAD_DIM]
                            for h in range(hk * B_GROUP, (hk + 1) * B_GROUP)], axis=0)


def _swa_group_probs(q_ref, sk_ref, kh, vh, neg_dist, hk):
    heads = range(hk * B_GROUP, (hk + 1) * B_GROUP)
    qs = _stack_heads(q_ref, hk) * (B_HEAD_DIM ** -0.5)
    bias = jnp.concatenate([ALIBI[h] * neg_dist for h in heads], axis=0)
    sink = jnp.concatenate([jnp.broadcast_to(sk_ref[h:h + 1, 0:1], (BLOCK, 1)) for h in heads], axis=0)
    s = _dot_nt(qs, kh) + bias
    m = jnp.maximum(jnp.max(s, axis=-1, keepdims=True), sink)
    p = jnp.exp(s - m)
    vext = jnp.concatenate([vh.astype(BF16), jnp.ones((2 * BLOCK, B_HEAD_DIM), BF16)], axis=1)
    oe = jnp.dot(p.astype(BF16), vext, preferred_element_type=F32)
    ps = jnp.exp(sink - m)
    inv = 1.0 / (oe[:, B_HEAD_DIM:B_HEAD_DIM + 1] + ps)
    return qs, p * inv, ps * inv, oe[:, :B_HEAD_DIM] * inv


def _swa_specs():
    qspec = lambda c0: pl.BlockSpec((BLOCK, B_WIDTH), lambda i: (i, c0 // B_WIDTH))
    cur = lambda c0: pl.BlockSpec((BLOCK, LANE), lambda i: (i, c0 // LANE))
    prev = lambda c0: pl.BlockSpec((BLOCK, LANE), lambda i: (jnp.maximum(i - 1, 0), c0 // LANE))
    return qspec, cur, prev


def _carried(carry, refs, n_in, n_out, steps):
    if carry is None:
        return refs
    ci, co = len(carry.ins), len(carry.outs)
    own = refs[:n_in] + refs[n_in + ci:n_in + ci + n_out] + refs[n_in + ci + n_out + co:len(refs) - 3]
    parts = refs[n_in:n_in + ci], refs[n_in + ci + n_out:n_in + ci + n_out + co], refs[len(refs) - 3:]

    @pl.when(pl.program_id(0) == 0)
    def _():
        carry.start(*parts)

    @pl.when(pl.program_id(0) == steps - 1)
    def _():
        carry.finish(*parts)

    return own


def _carry_specs(carry):
    if carry is None:
        return [], [], [], [], []
    return (list(carry.ins), [_ANY] * len(carry.ins), [_ANY] * len(carry.outs), list(carry.outs), carry.scratch())


def _swa_fwd(h, sinks_b, *, name, carry=None):
    t = h.shape[0]
    qspec, cur, prev = _swa_specs()
    c_ins, c_in_specs, c_out_specs, c_outs, c_scratch = _carry_specs(carry)

    def body(*refs):
        q_ref, kc_ref, kp_ref, vc_ref, vp_ref, sk_ref, o_ref = _carried(carry, refs, 6, 1, t // BLOCK)
        n_blk = pl.program_id(0)
        kband = jnp.concatenate([kp_ref[...], kc_ref[...]], axis=0)
        vband = jnp.concatenate([vp_ref[...], vc_ref[...]], axis=0)
        neg_dist = _swa_neg_dist(n_blk)
        for hk in range(B_KV_HEADS):
            ksl = slice(hk * B_HEAD_DIM, (hk + 1) * B_HEAD_DIM)
            _, _, _, o = _swa_group_probs(q_ref, sk_ref, kband[:, ksl], vband[:, ksl], neg_dist, hk)
            for g in range(B_GROUP):
                hq = hk * B_GROUP + g
                o_ref[:, hq * B_HEAD_DIM:(hq + 1) * B_HEAD_DIM] = o[g * BLOCK:(g + 1) * BLOCK]

    outs = pl.pallas_call(
        body, name=name, grid=(t // BLOCK,),
        in_specs=[qspec(C_QB), cur(C_KB), prev(C_KB), cur(C_VB), prev(C_VB),
                  pl.BlockSpec((B_Q_HEADS, LANE), lambda i: (0, 0))] + c_in_specs,
        out_specs=[pl.BlockSpec((BLOCK, B_WIDTH), lambda i: (i, 0))] + c_out_specs,
        out_shape=[jax.ShapeDtypeStruct((t, B_WIDTH), F32)] + c_outs,
        scratch_shapes=c_scratch,
        compiler_params=_cp("arbitrary"))(h, h, h, h, h, sinks_b, *c_ins)
    return outs[0], outs[1:]


def _rms_gate(o, za, nw):
    outs = []
    for hd in range(A_HEADS):
        oh = o[:, hd * LANE:(hd + 1) * LANE]
        r = lax.rsqrt(jnp.mean(oh * oh, -1, keepdims=True) + RMS_EPS)
        outs.append(oh * r * nw)
    return jnp.concatenate(outs, axis=1) * _silu(za)


def _out_ln(x, oa, ob, h, norm_w, w_out, ln_g, ln_b, *, tm, name):
    t = x.shape[0]

    def body(x_ref, oa_ref, ob_ref, za_ref, zb_ref, nw_ref, w_ref, g_ref, b_ref, xn_ref, mx_ref, r_ref):
        ya = _rms_gate(oa_ref[...], za_ref[...], nw_ref[...])
        yb = ob_ref[...] * _silu(zb_ref[...])
        mixed = jnp.concatenate([ya, yb], axis=1).astype(BF16)
        mx_ref[...] = mixed
        r = DEEPNORM_ALPHA * x_ref[...] + jnp.dot(mixed, w_ref[...], preferred_element_type=F32)
        r_ref[...] = r
        mu = jnp.mean(r, -1, keepdims=True)
        xc = r - mu
        var = jnp.mean(xc * xc, -1, keepdims=True)
        xn_ref[...] = xc * lax.rsqrt(var + LN_EPS) * g_ref[...] + b_ref[...]

    row = lambda w, c: pl.BlockSpec((tm, w), lambda i: (i, c))
    full = lambda a, b: pl.BlockSpec((a, b), lambda i: (0, 0))
    return pl.pallas_call(
        body, name=name, grid=(t // tm,),
        in_specs=[row(D_MODEL, 0), row(A_WIDTH, 0), row(B_WIDTH, 0), row(A_WIDTH, C_ZA // A_WIDTH),
                  row(B_WIDTH, C_ZB // B_WIDTH), full(1, LANE), full(D_MODEL, D_MODEL), full(1, D_MODEL), full(1, D_MODEL)],
        out_specs=[row(D_MODEL, 0), row(D_MODEL, 0), row(D_MODEL, 0)],
        out_shape=[jax.ShapeDtypeStruct((t, D_MODEL), F32), jax.ShapeDtypeStruct((t, D_MODEL), BF16),
                   jax.ShapeDtypeStruct((t, D_MODEL), F32)],
        compiler_params=_cp("parallel"))(x, oa, ob, h, h, norm_w, w_out, ln_g, ln_b)


def _layer_fwd(x, wt, conv_w, par, sinks_b, norm_w, w_out_bf, ln_g, ln_b, l, carry=None):
    h = _matmul_nt(x, wt, tm=512, tn=1152, name=f"in_proj_{l}")
    q, k, v, bg, bgt = _dn_pre(h, conv_w, par, tt=512, name=f"dn_pre_{l}")
    u, w, tmat, qk = _dn_wy(q, k, v, bg, bgt, name=f"dn_wy_{l}")
    oa, vn, s_all = _dn_scan_fwd(q, k, u, w, qk, bg, name=f"dn_scan_{l}")
    ob, carried = _swa_fwd(h, sinks_b, name=f"swa_fwd_{l}", carry=carry)
    xn, mixed, r = _out_ln(x, oa, ob, h, norm_w, w_out_bf, ln_g, ln_b, tm=256, name=f"out_ln_{l}")
    res = dict(x=x, h=h, q=q, k=k, v=v, bg=bg, bgt=bgt, w=w, tmat=tmat, qk=qk, vn=vn, oa=oa, s_all=s_all,
               mixed=mixed, r=r)
    return xn, res, carried


def _loss_grad(xn, target, *, tm, name):
    t = xn.shape[0]

    def body(x_ref, t_ref, d_ref, l_ref):
        @pl.when(pl.program_id(0) == 0)
        def _():
            l_ref[...] = jnp.zeros_like(l_ref)

        err = x_ref[...] - t_ref[...]
        d_ref[...] = err * (1.0 / D_MODEL)
        l_ref[...] += 0.5 / D_MODEL * jnp.sum(err * err)

    row = pl.BlockSpec((tm, D_MODEL), lambda i: (i, 0))
    return pl.pallas_call(
        body, name=name, grid=(t // tm,), in_specs=[row, row],
        out_specs=[row, pl.BlockSpec((SUBLANE, LANE), lambda i: (0, 0))],
        out_shape=[jax.ShapeDtypeStruct((t, D_MODEL), F32), jax.ShapeDtypeStruct((SUBLANE, LANE), F32)],
        compiler_params=_cp("arbitrary"))(xn, target)


def _ln_out_bwd(dxn, r, mixed, ln_g, w_out, *, tm, name):
    t = dxn.shape[0]

    def body(dxn_ref, r_ref, mx_ref, g_ref, w_ref, dr_ref, dm_ref, dw_ref, dg_ref, db_ref):
        @pl.when(pl.program_id(0) == 0)
        def _():
            dw_ref[...] = jnp.zeros_like(dw_ref)
            dg_ref[...] = jnp.zeros_like(dg_ref)
            db_ref[...] = jnp.zeros_like(db_ref)

        rr = r_ref[...]
        xc = rr - jnp.mean(rr, -1, keepdims=True)
        rstd = lax.rsqrt(jnp.mean(xc * xc, -1, keepdims=True) + LN_EPS)
        xhat = xc * rstd
        dxn_v = dxn_ref[...]
        dxh = dxn_v * g_ref[...]
        dr = rstd * (dxh - jnp.mean(dxh, -1, keepdims=True) - xhat * jnp.mean(dxh * xhat, -1, keepdims=True))
        dr_ref[...] = dr
        dg_ref[...] += jnp.sum(dxn_v * xhat, axis=0, keepdims=True)
        db_ref[...] += jnp.sum(dxn_v, axis=0, keepdims=True)
        drb = dr.astype(BF16)
        dm_ref[...] = _dot_nt(drb, w_ref[...])
        dw_ref[...] += _dot_tn(mx_ref[...], drb)

    row = pl.BlockSpec((tm, D_MODEL), lambda i: (i, 0))
    full = lambda a, b: pl.BlockSpec((a, b), lambda i: (0, 0))
    big = jax.ShapeDtypeStruct((t, D_MODEL), F32)
    vec = jax.ShapeDtypeStruct((1, D_MODEL), F32)
    return pl.pallas_call(
        body, name=name, grid=(t // tm,),
        in_specs=[row, row, row, full(1, D_MODEL), full(D_MODEL, D_MODEL)],
        out_specs=[row, row, full(D_MODEL, D_MODEL), full(1, D_MODEL), full(1, D_MODEL)],
        out_shape=[big, big, jax.ShapeDtypeStruct((D_MODEL, D_MODEL), F32), vec, vec],
        compiler_params=_cp("arbitrary"))(dxn, r, mixed, ln_g, w_out)


def _dn_post_bwd(dm, oa, h, norm_w, *, tm, name):
    t = oa.shape[0]

    def body(dy_ref, o_ref, za_ref, nw_ref, do_ref, dza_ref, dnw_ref):
        @pl.when(pl.program_id(0) == 0)
        def _():
            dnw_ref[...] = jnp.zeros_like(dnw_ref)

        nw = nw_ref[...]
        dnw = jnp.zeros_like(nw)
        for hd in range(A_HEADS):
            sl = slice(hd * LANE, (hd + 1) * LANE)
            oh, za, dy = o_ref[:, sl], za_ref[:, sl], dy_ref[:, sl]
            rs = lax.rsqrt(jnp.mean(oh * oh, -1, keepdims=True) + RMS_EPS)
            nrm = oh * rs
            dza_ref[:, sl] = dy * nrm * nw * _dsilu(za)
            dn = dy * _silu(za)
            dnw = dnw + jnp.sum(dn * nrm, axis=0, keepdims=True)
            dnn = dn * nw
            do_ref[:, sl] = rs * dnn - oh * (rs * rs * rs) * jnp.mean(dnn * oh, -1, keepdims=True)
        dnw_ref[...] += dnw

    row = lambda c: pl.BlockSpec((tm, A_WIDTH), lambda i: (i, c))
    wide = jax.ShapeDtypeStruct((t, A_WIDTH), F32)
    return pl.pallas_call(
        body, name=name, grid=(t // tm,),
        in_specs=[row(0), row(0), row(C_ZA // A_WIDTH), pl.BlockSpec((1, LANE), lambda i: (0, 0))],
        out_specs=[row(0), row(C_ZA // A_WIDTH), pl.BlockSpec((1, LANE), lambda i: (0, 0))],
        out_shape=[wide, jax.ShapeDtypeStruct((t, DH_MAIN), F32), jax.ShapeDtypeStruct((1, LANE), F32)],
        compiler_params=_cp("arbitrary"))(dm, oa, h, norm_w)


def _dn_scan_bwd(q, k, w, qk, bg, do, *, name):
    t = q.shape[0]
    rows = SCAN_ROWS
    per = rows // CHUNK
    n = t // rows

    def body(q_ref, k_ref, w_ref, qk_ref, bg_ref, do_ref, dvn_ref, ds_ref, dstate):
        @pl.when(pl.program_id(0) == 0)
        def _():
            dstate[...] = jnp.zeros_like(dstate)

        heads = range(A_HEADS)
        sl = lambda hd: slice(hd * LANE, (hd + 1) * LANE)
        ds_cur = [dstate[hd] for hd in heads]
        for c in reversed(range(per)):
            rs = slice(c * CHUNK, (c + 1) * CHUNK)
            bg_v = bg_ref[rs, :]
            gcols = [_chunk_gates(bg_v, None, hd)[1] for hd in heads]
            glasts = [gc[CHUNK - 1:CHUNK, :] for gc in gcols]
            for hd in heads:
                ds_ref[c, hd] = ds_cur[hd]
            pdo = [_dot_tn(qk_ref[rs, hd * CHUNK:(hd + 1) * CHUNK], do_ref[rs, sl(hd)]) for hd in heads]
            qdo = [_dot_tn(q_ref[rs, sl(hd)] * jnp.exp(gcols[hd]), do_ref[rs, sl(hd)]) for hd in heads]
            dvns = [pdo[hd] + _dot(k_ref[rs, sl(hd)] * jnp.exp(glasts[hd] - gcols[hd]), ds_cur[hd]) for hd in heads]
            ds_cur = [qdo[hd] + jnp.exp(glasts[hd]) * ds_cur[hd] - _dot_tn(w_ref[rs, sl(hd)], dvns[hd])
                      for hd in heads]
            for hd in heads:
                dvn_ref[rs, sl(hd)] = dvns[hd]
        for hd in heads:
            dstate[hd] = ds_cur[hd]

    blk = pl.BlockSpec((rows, A_WIDTH), lambda i: (n - 1 - i, 0))
    return pl.pallas_call(
        body, name=name, grid=(n,),
        in_specs=[blk, blk, blk, pl.BlockSpec((rows, A_HEADS * CHUNK), lambda i: (n - 1 - i, 0)),
                  pl.BlockSpec((rows, LANE), lambda i: (n - 1 - i, 0)), blk],
        out_specs=[blk, pl.BlockSpec((per, A_HEADS, LANE, LANE), lambda i: (n - 1 - i, 0, 0, 0))],
        out_shape=[jax.ShapeDtypeStruct((t, A_WIDTH), F32),
                   jax.ShapeDtypeStruct((t // CHUNK, A_HEADS, LANE, LANE), F32)],
        scratch_shapes=[pltpu.VMEM((A_HEADS, LANE, LANE), F32)],
        compiler_params=_cp("arbitrary"))(q, k, w, qk, bg, do)


def _dn_chunk_bwd(q, k, v, vn, tmat, qk, bg, bgt, s_all, ds_all, dvn, do, *, name):
    t = q.shape[0]
    rows = WY_ROWS
    per = rows // CHUNK

    def body(q_ref, k_ref, v_ref, vn_ref, tm_ref, qk_ref, bg_ref, bgt_ref, s_ref, ds_ref, dvn_ref, do_ref,
             dq_ref, dk_ref, dv_ref, dbg_ref, dbgt_ref):
        causal, strict, _ = _chunk_masks()
        lane = lax.broadcasted_iota(jnp.int32, (CHUNK, LANE), 1)
        rowi = lax.broadcasted_iota(jnp.int32, (CHUNK, 1), 0)
        sub = lax.broadcasted_iota(jnp.int32, (SUBLANE, CHUNK), 0)
        rs = lambda c: slice(c * CHUNK, (c + 1) * CHUNK)
        sl = lambda hd: slice(hd * LANE, (hd + 1) * LANE)
        hs = lambda hd: slice(hd * CHUNK, (hd + 1) * CHUNK)
        for c0 in range(0, per, WY_GROUP):
            items = [(c, hd) for c in range(c0, c0 + WY_GROUP) for hd in range(A_HEADS)]
            at = lambda ref: [ref[rs(c), sl(hd)] for c, hd in items]
            qs, ks, vs, dos, vns, dvns = at(q_ref), at(k_ref), at(v_ref), at(do_ref), at(vn_ref), at(dvn_ref)
            tmhs = [tm_ref[rs(c), hs(hd)] for c, hd in items]
            ps = [qk_ref[rs(c), hs(hd)] for c, hd in items]
            gates = [_chunk_gates(bg_ref[rs(c), :], bgt_ref[:, rs(c)], hd) for c, hd in items]
            betas = [g[0] for g in gates]
            gcols = [g[1] for g in gates]
            dmats = [jnp.exp(jnp.where(causal, g[1] - g[2], NEG)) for g in gates]
            es = [jnp.exp(gc) for gc in gcols]
            glasts = [gc[CHUNK - 1:CHUNK, :] for gc in gcols]
            eks = [jnp.exp(gl - gc) for gl, gc in zip(glasts, gcols)]
            kbs = [kh * b for kh, b in zip(ks, betas)]
            vbs = [vh * b for vh, b in zip(vs, betas)]
            kbes = [kb * e for kb, e in zip(kbs, es)]

            a_s = [jnp.where(strict, _dot_nt(kb, kh) * dm, 0.0) for kb, kh, dm in zip(kbs, ks, dmats)]
            dps = [jnp.where(causal, _dot_nt(doh, vnh), 0.0) for doh, vnh in zip(dos, vns)]
            dqds = [_dot_nt(doh, s_ref[c, hd]) for doh, (c, hd) in zip(dos, items)]
            dkds = [_dot_nt(vnh, ds_ref[c, hd]) for vnh, (c, hd) in zip(vns, items)]
            dws = [-_dot_nt(dvnh, s_ref[c, hd]) for dvnh, (c, hd) in zip(dvns, items)]
            dvbs = [_dot_tn(tmh, dvnh) for tmh, dvnh in zip(tmhs, dvns)]
            dgts = [jnp.sum(s_ref[c, hd] * ds_ref[c, hd], keepdims=True) for c, hd in items]
            dts = [_dot_nt(dvnh, vb) + _dot_nt(dw, kbe) for dvnh, vb, dw, kbe in zip(dvns, vbs, dws, kbes)]
            dkbes = [_dot_tn(tmh, dw) for tmh, dw in zip(tmhs, dws)]
            xs = [_dot_nt(dt, tmh) for dt, tmh in zip(dts, tmhs)]
            das = [jnp.where(strict, -_dot_tn(tmh, x), 0.0) for tmh, x in zip(tmhs, xs)]
            dmas = [da * dm for da, dm in zip(das, dmats)]
            dmps = [dp * dm for dp, dm in zip(dps, dmats)]
            dkbs = [_dot(dma, kh) + dkbe * e for dma, kh, dkbe, e in zip(dmas, ks, dkbes, es)]
            for i, (c, hd) in enumerate(items):
                dq_ref[rs(c), sl(hd)] = _dot(dmps[i], ks[i]) + dqds[i] * es[i]
                dk_ref[rs(c), sl(hd)] = (_dot_tn(dmas[i], kbs[i]) + _dot_tn(dmps[i], qs[i]) + dkds[i] * eks[i]
                                         + dkbs[i] * betas[i])
                dv_ref[rs(c), sl(hd)] = dvbs[i] * betas[i]
            for c in range(c0, c0 + WY_GROUP):
                acc = jnp.zeros((CHUNK, LANE), F32)
                acc_t = jnp.zeros((SUBLANE, CHUNK), F32)
                for i, (ci, hd) in enumerate(items):
                    if ci != c:
                        continue
                    gmat = das[i] * a_s[i] + dps[i] * ps[i]
                    rk = jnp.sum(dkds[i] * ks[i], -1, keepdims=True) * eks[i]
                    de = (jnp.sum(dqds[i] * qs[i], -1, keepdims=True)
                          + jnp.sum(dkbes[i] * kbs[i], -1, keepdims=True))
                    dglast = jnp.sum(rk, keepdims=True) + dgts[i] * jnp.exp(glasts[i])
                    dgc = (jnp.sum(gmat, -1, keepdims=True) + de * es[i] - rk
                           + jnp.where(rowi == CHUNK - 1, dglast, 0.0))
                    dbeta = (jnp.sum(dkbs[i] * ks[i], -1, keepdims=True)
                             + jnp.sum(dvbs[i] * vs[i], -1, keepdims=True))
                    acc = acc + jnp.where(lane == hd, dbeta, 0.0) + jnp.where(lane == A_HEADS + hd, dgc, 0.0)
                    acc_t = acc_t + jnp.where(sub == A_HEADS + hd, -jnp.sum(gmat, axis=0, keepdims=True), 0.0)
                dbg_ref[rs(c), :] = acc
                dbgt_ref[:, rs(c)] = acc_t

    blk = pl.BlockSpec((rows, A_WIDTH), lambda i: (i, 0))
    half = pl.BlockSpec((rows, A_HEADS * CHUNK), lambda i: (i, 0))
    col = pl.BlockSpec((rows, LANE), lambda i: (i, 0))
    rowf = pl.BlockSpec((SUBLANE, rows), lambda i: (0, i))
    st = pl.BlockSpec((per, A_HEADS, LANE, LANE), lambda i: (i, 0, 0, 0))
    wide = jax.ShapeDtypeStruct((t, A_WIDTH), F32)
    return pl.pallas_call(
        body, name=name, grid=(t // rows,),
        in_specs=[blk, blk, blk, blk, half, half, col, rowf, st, st, blk, blk],
        out_specs=[blk, blk, blk, col, rowf],
        out_shape=[wide, wide, wide, jax.ShapeDtypeStruct((t, LANE), F32), jax.ShapeDtypeStruct((SUBLANE, t), F32)],
        compiler_params=_cp("parallel"))(q, k, v, vn, tmat, qk, bg, bgt, s_all, ds_all, dvn, do)


def _dn_pre_bwd(h, conv_w, par, dq, dk, dv, dbg, dbgt, *, tt, name):
    t = h.shape[0]
    cw = 3 * A_WIDTH
    hb = tt // SUBLANE

    def body(pre_ref, halo_ref, bgi_ref, cw_ref, par_ref, dq_ref, dk_ref, dv_ref, dbg_ref, dbgt_ref,
             dc_ref, dbgi_ref, dpar_ref):
        i = pl.program_id(0)

        @pl.when(i == 0)
        def _():
            dpar_ref[...] = jnp.zeros_like(dpar_ref)

        cur = pre_ref[...]
        before = jnp.where(i > 0, halo_ref[...], 0.0)
        c = _conv_fwd(cur, before, cw_ref[...])
        s = _silu(c)
        ds = _dsilu(c)
        for hd in range(A_HEADS):
            sl = slice(hd * LANE, (hd + 1) * LANE)
            for base, d_ref, scale in ((0, dq_ref, A_HEAD_DIM ** -0.5), (A_WIDTH, dk_ref, 1.0)):
                csl = slice(base + hd * LANE, base + (hd + 1) * LANE)
                tq = s[:, base + hd * LANE:base + (hd + 1) * LANE]
                dy = d_ref[:, sl]
                rq = lax.rsqrt(jnp.sum(tq * tq, -1, keepdims=True) + L2_EPS)
                dtq = scale * (rq * dy - tq * (rq * rq * rq) * jnp.sum(dy * tq, -1, keepdims=True))
                dc_ref[:, csl] = dtq * ds[:, base + hd * LANE:base + (hd + 1) * LANE]
        dc_ref[:, 2 * A_WIDTH:] = dv_ref[...] * ds[:, 2 * A_WIDTH:]
        raw = bgi_ref[...]
        lane = lax.broadcasted_iota(jnp.int32, raw.shape, 1)
        is_b = lane < A_HEADS
        is_a = (lane >= A_HEADS) & (lane < 2 * A_HEADS)
        rows_t = jnp.concatenate([dbgt_ref[...], jnp.zeros((LANE - SUBLANE, tt), F32)], axis=0)
        dbg_v = dbg_ref[...] + jnp.where(is_a, jnp.transpose(rows_t), 0.0)
        dbg_v = jnp.where(is_a, _dot_hi(_chunk_tri(tt, lower=False), jnp.where(is_a, dbg_v, 0.0)), dbg_v)
        beta = _sigmoid(raw)
        z = raw + par_ref[1:2, :]
        neg_ea = -jnp.exp(par_ref[0:1, :])
        g = neg_ea * _softplus(z)
        da = dbg_v * neg_ea * _sigmoid(z)
        dbgi_ref[...] = jnp.where(is_b, dbg_v * beta * (1.0 - beta), jnp.where(is_a, da, 0.0))
        dpar_ref[0:1, :] += jnp.sum(jnp.where(is_a, dbg_v * g, 0.0), axis=0, keepdims=True)
        dpar_ref[1:2, :] += jnp.sum(jnp.where(is_a, da, 0.0), axis=0, keepdims=True)

    wide = pl.BlockSpec((tt, A_WIDTH), lambda i: (i, 0))
    return pl.pallas_call(
        body, name=name, grid=(t // tt,),
        in_specs=[pl.BlockSpec((tt, cw), lambda i: (i, 0)),
                  pl.BlockSpec((SUBLANE, cw), lambda i: (jnp.maximum(i * hb - 1, 0), 0)),
                  pl.BlockSpec((tt, LANE), lambda i: (i, C_BG // LANE)),
                  pl.BlockSpec((CONV_K, cw), lambda i: (0, 0)),
                  pl.BlockSpec((SUBLANE, LANE), lambda i: (0, 0)),
                  wide, wide, wide, pl.BlockSpec((tt, LANE), lambda i: (i, 0)),
                  pl.BlockSpec((SUBLANE, tt), lambda i: (0, i))],
        out_specs=[pl.BlockSpec((tt, cw), lambda i: (i, 0)), pl.BlockSpec((tt, LANE), lambda i: (i, 0)),
                   pl.BlockSpec((SUBLANE, LANE), lambda i: (0, 0))],
        out_shape=[jax.ShapeDtypeStruct((t, cw), F32), jax.ShapeDtypeStruct((t, LANE), F32),
                   jax.ShapeDtypeStruct((SUBLANE, LANE), F32)],
        compiler_params=_cp("arbitrary"))(h, h, h, conv_w, par, dq, dk, dv, dbg, dbgt)


def _conv_bwd(dc, h, conv_w, dh, *, tt, name):
    t = dc.shape[0]
    cw = 3 * A_WIDTH
    hb = tt // SUBLANE
    nb = t // tt

    def body(dc_ref, after_ref, pre_ref, before_ref, cw_ref, dh_in_ref, dpre_ref, dcw_ref):
        i = pl.program_id(0)

        @pl.when(i == 0)
        def _():
            dcw_ref[...] = jnp.zeros_like(dcw_ref)

        dcv = dc_ref[...]
        after = jnp.where(i < nb - 1, after_ref[...], 0.0)
        cur = pre_ref[...]
        before = jnp.where(i > 0, before_ref[...], 0.0)
        w = cw_ref[...]
        acc = dcv * w[CONV_K - 1:CONV_K, :]
        dcw_ref[CONV_K - 1:CONV_K, :] += jnp.sum(dcv * cur, axis=0, keepdims=True)
        for s in range(1, CONV_K):
            j = CONV_K - 1 - s
            acc = acc + _shift_up(dcv, after, s) * w[j:j + 1, :]
            dcw_ref[j:j + 1, :] += jnp.sum(dcv * _shift_down(cur, before, s), axis=0, keepdims=True)
        dpre_ref[...] = acc

    return pl.pallas_call(
        body, name=name, grid=(nb,),
        in_specs=[pl.BlockSpec((tt, cw), lambda i: (i, 0)),
                  pl.BlockSpec((SUBLANE, cw), lambda i: (jnp.minimum((i + 1) * hb, t // SUBLANE - 1), 0)),
                  pl.BlockSpec((tt, cw), lambda i: (i, 0)),
                  pl.BlockSpec((SUBLANE, cw), lambda i: (jnp.maximum(i * hb - 1, 0), 0)),
                  pl.BlockSpec((CONV_K, cw), lambda i: (0, 0)), _ANY],
        out_specs=[pl.BlockSpec((tt, cw), lambda i: (i, 0)), pl.BlockSpec((SUBLANE, cw), lambda i: (0, 0))],
        out_shape=[jax.ShapeDtypeStruct(dh.shape, F32), jax.ShapeDtypeStruct((SUBLANE, cw), F32)],
        input_output_aliases={5: 0},
        compiler_params=_cp("arbitrary"))(dc, dc, h, h, conv_w, dh)


def _swa_bwd(h, dm, sinks_b, dh, *, name, carry=None):
    t = h.shape[0]
    qspec, cur, prev = _swa_specs()
    c_ins, c_in_specs, c_out_specs, c_outs, c_scratch = _carry_specs(carry)

    def body(*refs):
        (q_ref, kc_ref, kp_ref, vc_ref, vp_ref, zb_ref, dy_ref, sk_ref, dh_in_ref,
         dqz_ref, dk_ref, dv_ref, dsk_ref) = _carried(carry, refs, 9, 4, t // BLOCK)
        n_blk = pl.program_id(0)

        @pl.when(n_blk == 0)
        def _():
            dk_ref[...] = jnp.zeros_like(dk_ref)
            dv_ref[...] = jnp.zeros_like(dv_ref)
            dsk_ref[...] = jnp.zeros_like(dsk_ref)

        kband = jnp.concatenate([kp_ref[...], kc_ref[...]], axis=0)
        vband = jnp.concatenate([vp_ref[...], vc_ref[...]], axis=0)
        scale = B_HEAD_DIM ** -0.5
        neg_dist = _swa_neg_dist(n_blk)
        dk_acc, dv_acc = [], []
        for hk in range(B_KV_HEADS):
            ksl = slice(hk * B_HEAD_DIM, (hk + 1) * B_HEAD_DIM)
            kh, vh = kband[:, ksl], vband[:, ksl]
            qs, p, ps, o = _swa_group_probs(q_ref, sk_ref, kh, vh, neg_dist, hk)
            zb, dy = _stack_heads(zb_ref, hk), _stack_heads(dy_ref, hk)
            dzb = dy * o * _dsilu(zb)
            do = dy * _silu(zb)
            delta = jnp.sum(do * o, -1, keepdims=True)
            ds = p * (_dot_nt(do, vh) - delta)
            dq = _dot(ds, kh) * scale
            dk_acc.append(_dot_tn(ds, qs))
            dv_acc.append(_dot_tn(p, do))
            dsink = ps * delta
            for g in range(B_GROUP):
                hq = hk * B_GROUP + g
                rows = slice(g * BLOCK, (g + 1) * BLOCK)
                qsl = slice(hq * B_HEAD_DIM, (hq + 1) * B_HEAD_DIM)
                dqz_ref[:, qsl] = dq[rows]
                dqz_ref[:, B_WIDTH + hq * B_HEAD_DIM:B_WIDTH + (hq + 1) * B_HEAD_DIM] = dzb[rows]
                dsk_ref[hq:hq + 1, :] += -jnp.sum(dsink[rows], keepdims=True)
        dkb = jnp.concatenate(dk_acc, axis=1)
        dvb = jnp.concatenate(dv_acc, axis=1)
        at_cur = pl.ds(pl.multiple_of(n_blk * BLOCK, BLOCK), BLOCK)
        at_prev = pl.ds(pl.multiple_of(jnp.maximum(n_blk - 1, 0) * BLOCK, BLOCK), BLOCK)
        dk_ref[at_prev, :] += dkb[:BLOCK]
        dv_ref[at_prev, :] += dvb[:BLOCK]
        dk_ref[at_cur, :] += dkb[BLOCK:]
        dv_ref[at_cur, :] += dvb[BLOCK:]

    narrow = jax.ShapeDtypeStruct((t, B_KV_WIDTH), F32)
    res = lambda a, b: pl.BlockSpec((a, b), lambda i: (0, 0))
    outs = pl.pallas_call(
        body, name=name, grid=(t // BLOCK,),
        in_specs=[qspec(C_QB), cur(C_KB), prev(C_KB), cur(C_VB), prev(C_VB), qspec(C_ZB),
                  pl.BlockSpec((BLOCK, B_WIDTH), lambda i: (i, 1)), res(B_Q_HEADS, LANE), _ANY] + c_in_specs,
        out_specs=[pl.BlockSpec((BLOCK, 2 * B_WIDTH), lambda i: (i, C_QB // (2 * B_WIDTH))),
                   res(t, B_KV_WIDTH), res(t, B_KV_WIDTH), res(B_Q_HEADS, LANE)] + c_out_specs,
        out_shape=[jax.ShapeDtypeStruct(dh.shape, F32), narrow, narrow,
                   jax.ShapeDtypeStruct((B_Q_HEADS, LANE), F32)] + c_outs,
        scratch_shapes=c_scratch,
        input_output_aliases={8: 0},
        compiler_params=_cp("arbitrary"))(h, h, h, h, h, h, dm, sinks_b, dh, *c_ins)
    return outs[:4], outs[4:]


def _matmul_tn(a, b, *, tk, tm, name):
    t, m = a.shape
    n = b.shape[1]

    def body(a_ref, b_ref, o_ref):
        @pl.when(pl.program_id(1) == 0)
        def _():
            o_ref[...] = jnp.zeros_like(o_ref)

        o_ref[...] += _dot_tn(a_ref[...], b_ref[...])

    return pl.pallas_call(
        body, name=name, grid=(m // tm, t // tk),
        in_specs=[pl.BlockSpec((tk, tm), lambda j, kk: (kk, j)), pl.BlockSpec((tk, n), lambda j, kk: (kk, 0))],
        out_specs=pl.BlockSpec((tm, n), lambda j, kk: (j, 0)),
        out_shape=jax.ShapeDtypeStruct((m, n), F32),
        compiler_params=_cp("parallel", "arbitrary"))(a, b)


def _in_proj_dx(dh_main, dh_tail, wt, dr, *, tm, name):
    t, n_main = dh_main.shape
    n_tail = dh_tail.shape[1]

    def body(a_ref, t_ref, wa_ref, wt_ref, r_ref, o_ref):
        o_ref[...] = _dot(a_ref[...], wa_ref[...]) + _dot(t_ref[...], wt_ref[...]) + DEEPNORM_ALPHA * r_ref[...]

    row = lambda w: pl.BlockSpec((tm, w), lambda i: (i, 0))
    return pl.pallas_call(
        body, name=name, grid=(t // tm,),
        in_specs=[row(n_main), row(n_tail), pl.BlockSpec((n_main, D_MODEL), lambda i: (0, 0)),
                  pl.BlockSpec((n_tail, D_MODEL), lambda i: (n_main // n_tail, 0)), row(D_MODEL)],
        out_specs=row(D_MODEL),
        out_shape=jax.ShapeDtypeStruct((t, D_MODEL), F32),
        compiler_params=_cp("parallel"))(dh_main, dh_tail, wt, wt, dr)


def _layer_bwd(dxn, res, wt, conv_w, par, sinks_b, norm_w, w_out_bf, ln_g, l, carry=None):
    dr, dm, dw_out, dln_g, dln_b = _ln_out_bwd(dxn, res["r"], res["mixed"], ln_g, w_out_bf, tm=256, name=f"ln_out_bwd_{l}")
    h = res["h"]
    do, dh, dnw = _dn_post_bwd(dm, res["oa"], h, norm_w, tm=512, name=f"dn_post_bwd_{l}")
    dvn, ds_all = _dn_scan_bwd(res["q"], res["k"], res["w"], res["qk"], res["bg"], do, name=f"dn_scan_bwd_{l}")
    dq, dk, dv, dbg, dbgt = _dn_chunk_bwd(res["q"], res["k"], res["v"], res["vn"], res["tmat"], res["qk"], res["bg"],
                                          res["bgt"], res["s_all"], ds_all, dvn, do, name=f"dn_chunk_bwd_{l}")
    dc, dbgi, dpar = _dn_pre_bwd(h, conv_w, par, dq, dk, dv, dbg, dbgt, tt=512, name=f"dn_pre_bwd_{l}")
    dh, dcw = _conv_bwd(dc, h, conv_w, dh, tt=512, name=f"conv_bwd_{l}")
    (dh, dkb, dvb, dsk), carried = _swa_bwd(h, dm, sinks_b, dh, name=f"swa_bwd_{l}", carry=carry)
    dh_tail = jnp.concatenate([dkb, dvb, dbgi], axis=1)
    dwt_main = _matmul_tn(dh, res["x"], tk=512, tm=768, name=f"in_proj_dw_{l}")
    dwt_tail = _matmul_tn(dh_tail, res["x"], tk=512, tm=P_COLS - DH_MAIN, name=f"in_proj_dw_tail_{l}")
    dx = _in_proj_dx(dh, dh_tail, wt, dr, tm=256, name=f"in_proj_dx_{l}")
    grads = dict(w_in=(dwt_main, dwt_tail), conv_w=dcw[:CONV_K], a_log=dpar[0, A_HEADS:2 * A_HEADS],
                 dt_bias=dpar[1, A_HEADS:2 * A_HEADS], norm_w=dnw[0], sinks=dsk[:, 0], w_out=dw_out,
                 ln_g=dln_g[0], ln_b=dln_b[0])
    return dx, grads, carried


def _layer_args(wt, conv_w, a_log, dt_bias, sinks, norm_w, w_out_bf):
    return (wt, conv_w, _gate_params(a_log, dt_bias), jnp.broadcast_to(sinks[:, None], (B_Q_HEADS, LANE)),
            norm_w[None], w_out_bf)


def _local_step(x, target, args0, args1, ln_g, ln_b, gather1=None, reduce1=None):
    assert DEPTH == 2
    x1, res0, got = _layer_fwd(x, *args0, ln_g[0][None], ln_b[0][None], 0, carry=gather1)
    if gather1 is not None:
        args1 = args1(got)
    x2, res1, _ = _layer_fwd(x1, *args1, ln_g[1][None], ln_b[1][None], 1)
    dx, loss_tile = _loss_grad(x2, target, tm=512, name="loss_grad")
    dx, grads1, _ = _layer_bwd(dx, res1, *args1, ln_g[1][None], 1)
    carry = None if reduce1 is None else reduce1(grads1)
    dx, grads0, landed = _layer_bwd(dx, res0, *args0, ln_g[0][None], 0, carry=carry)
    return loss_tile, dx, [grads0, grads1], landed


_ANY = pl.BlockSpec(memory_space=pl.ANY)
_MESH = pl.DeviceIdType.MESH


HALF = D_MODEL // 2


class _Exchange:
    def __init__(self, ins, outs, n_remote, n_local, plan):
        self.ins, self.outs, self.n_remote, self.n_local, self.plan = tuple(ins), tuple(outs), n_remote, n_local, plan

    def scratch(self):
        return [pltpu.SemaphoreType.DMA((self.n_remote,)), pltpu.SemaphoreType.DMA((self.n_remote,)),
                pltpu.SemaphoreType.DMA((max(self.n_local, 1),))]

    def _copies(self, in_refs, out_refs, sems):
        send_sems, recv_sems, local_sems = sems
        local, sends, recvs = self.plan(in_refs, out_refs)
        loc = [pltpu.make_async_copy(s, d, local_sems.at[i]) for i, (s, d) in enumerate(local)]
        out = [pltpu.make_async_remote_copy(src_ref=s, dst_ref=d, send_sem=send_sems.at[i], recv_sem=recv_sems.at[i],
                                            device_id=peer, device_id_type=_MESH) for i, (s, d, peer) in enumerate(sends)]
        arrive = [pltpu.make_async_remote_copy(src_ref=s, dst_ref=recvs[i], send_sem=send_sems.at[i],
                                               recv_sem=recv_sems.at[i], device_id=peer, device_id_type=_MESH)
                  for i, (s, _, peer) in enumerate(sends)]
        return loc, out, arrive

    def start(self, in_refs, out_refs, sems):
        loc, out, _ = self._copies(in_refs, out_refs, sems)
        for cp in loc + out:
            cp.start()

    def finish(self, in_refs, out_refs, sems):
        loc, out, arrive = self._copies(in_refs, out_refs, sems)
        for cp in arrive:
            cp.wait_recv()
        for cp in out:
            cp.wait_send()
        for cp in loc:
            cp.wait()


def _run_exchange(ex, *, name):
    n_in, n_out = len(ex.ins), len(ex.outs)

    def body(*refs):
        parts = refs[:n_in], refs[n_in:n_in + n_out], refs[n_in + n_out:]
        ex.start(*parts)
        ex.finish(*parts)

    return pl.pallas_call(body, name=name, in_specs=[_ANY] * n_in, out_specs=[_ANY] * n_out, out_shape=list(ex.outs),
                          scratch_shapes=ex.scratch())(*ex.ins)


def _place():
    x, y, c = lax.axis_index("x"), lax.axis_index("y"), lax.axis_index("c")
    return x, y, c, [(1 - x, y), (x, 1 - y), (1 - x, 1 - y)]


def _gather_exchange(arrays):
    n = len(arrays)

    def plan(src, dst):
        x, y, c, chips = _place()
        me = 2 * x + y
        local = [(src[k], dst[k].at[me]) for k in range(n)]
        sends = [(src[k], dst[k].at[me], (px, py, c)) for k in range(n) for px, py in chips]
        recvs = [dst[k].at[2 * px + py] for k in range(n) for px, py in chips]
        return local, sends, recvs

    return _Exchange(arrays, [jax.ShapeDtypeStruct((N_SHARD,) + a.shape, a.dtype) for a in arrays], 3 * n, n, plan)


def _half(core):
    return pl.ds(pl.multiple_of(core * HALF, HALF), HALF)


def _reduce_scatter_exchange(g, small=None):
    ins = [g] if small is None else [g, small]
    outs = [jax.ShapeDtypeStruct((7,) + g.shape[1:2] + (HALF,), g.dtype)]
    if small is not None:
        outs.append(jax.ShapeDtypeStruct((8,) + small.shape, small.dtype))

    def plan(src, dst):
        x, y, c, chips = _place()
        me = 2 * x + y
        peers = [(px, py, c if t == 0 else 1 - c) for px, py in chips for t in (0, 1)] + [(x, y, 1 - c)]
        sends = [(src[0].at[2 * px + py, :, _half(pc)], dst[0].at[k], (px, py, pc)) for k, (px, py, pc) in enumerate(peers)]
        recvs = [dst[0].at[k] for k in range(7)]
        local = []
        if small is not None:
            mine = 4 * x + 2 * y + c
            local = [(src[1], dst[1].at[mine])]
            sends += [(src[1], dst[1].at[mine], peer) for peer in peers]
            recvs += [dst[1].at[4 * px + 2 * py + pc] for px, py, pc in peers]
        return local, sends, recvs

    return _Exchange(ins, outs, 7 * len(ins), len(ins) - 1, plan)


def _share_exchange(halves):
    n = len(halves)
    rows = halves[0].shape[0]

    def plan(src, dst):
        x, y, c, _ = _place()
        local = [(src[k], dst[0].at[k, :, _half(c)]) for k in range(n)]
        sends = [(src[k], dst[0].at[k, :, _half(c)], (x, y, 1 - c)) for k in range(n)]
        recvs = [dst[0].at[k, :, _half(1 - c)] for k in range(n)]
        return local, sends, recvs

    return _Exchange(halves, [jax.ShapeDtypeStruct((n, rows, D_MODEL), halves[0].dtype)], n, n, plan)


def _sum_scatter(g, land, me, core, *, tc, name):
    rows = g.shape[1]
    per = HALF // tc

    def body(where_ref, g_ref, land_ref, o_ref):
        acc = g_ref[...]
        for k in range(7):
            acc = acc + land_ref[k].astype(F32)
        o_ref[...] = acc

    return pl.pallas_call(
        body, name=name, out_shape=jax.ShapeDtypeStruct((rows, HALF), F32), compiler_params=_cp("parallel"),
        grid_spec=pltpu.PrefetchScalarGridSpec(
            num_scalar_prefetch=1, grid=(per,),
            in_specs=[pl.BlockSpec((None, rows, tc), lambda i, w: (w[0], 0, w[1] * per + i)),
                      pl.BlockSpec((7, rows, tc), lambda i, w: (0, 0, i))],
            out_specs=pl.BlockSpec((rows, tc), lambda i, w: (0, i))))(
        jnp.stack([me, core]).astype(jnp.int32), g, land)


def _sum_slots(a, *, name):
    n = a.shape[0]

    def body(a_ref, o_ref):
        acc = a_ref[0]
        for k in range(1, n):
            acc = acc + a_ref[k]
        o_ref[...] = acc

    return pl.pallas_call(body, name=name, out_shape=jax.ShapeDtypeStruct(a.shape[1:], a.dtype))(a)


def _elementwise(fn, ins, n_out, block, *, name):
    shape = ins[0].shape
    grid = tuple(s // b for s, b in zip(shape, block))
    n_in = len(ins)

    def body(*refs):
        outs = fn(*[r[...] for r in refs[:n_in]])
        for o_ref, val in zip(refs[n_in:], outs):
            o_ref[...] = val

    spec = pl.BlockSpec(block, lambda i, j, k: (i, j, k))
    return pl.pallas_call(body, name=name, grid=grid, in_specs=[spec] * n_in, out_specs=[spec] * n_out,
                          out_shape=[jax.ShapeDtypeStruct(shape, F32)] * n_out,
                          compiler_params=_cp(*["parallel"] * 3))(*ins)


def _adamw_math(w, g, m, v):
    mn = ADAM_B1 * m + (1.0 - ADAM_B1) * g
    vn = ADAM_B2 * v + (1.0 - ADAM_B2) * (g * g)
    m_hat = mn / (1.0 - ADAM_B1 ** ADAM_STEP)
    v_hat = vn / (1.0 - ADAM_B2 ** ADAM_STEP)
    return -ADAM_LR * (m_hat / (jnp.sqrt(v_hat) + ADAM_EPS) + ADAM_WD * w), mn, vn


def _adamw(w, g, m, v, block, *, name):
    return _elementwise(_adamw_math, [w, g, m, v], 3, block, name=name)


def _to_kernel_order(wt):
    gates = jnp.pad(wt[2048:2056], ((0, LANE - 2 * A_HEADS), (0, 0)))
    return jnp.concatenate([wt[0:2048], wt[2056:2568], wt[2824:3336], wt[2568:2696], wt[2696:2824], gates], axis=0)


def _from_kernel_order(main, tail):
    return jnp.concatenate([main[0:2048], tail[C_BG - DH_MAIN:C_BG - DH_MAIN + 2 * A_HEADS],
                            main[C_QB:C_QB + B_WIDTH], tail[0:B_KV_WIDTH], tail[B_KV_WIDTH:2 * B_KV_WIDTH],
                            main[C_ZB:C_ZB + B_WIDTH]], axis=0)


def _gate_params(a_log, dt_bias):
    par = jnp.zeros((SUBLANE, LANE), F32)
    par = par.at[0, A_HEADS:2 * A_HEADS].set(a_log)
    return par.at[1, A_HEADS:2 * A_HEADS].set(dt_bias)


SMALL = ("conv_w", "a_log", "dt_bias", "norm_w", "sinks", "ln_g", "ln_b")


def _pack(parts, cols):
    flat = jnp.concatenate([p.reshape(-1) for p in parts])
    rows = -(-flat.shape[0] // cols)
    return jnp.pad(flat, (0, rows * cols - flat.shape[0])).reshape(rows, cols)


def _unpack(packed, shapes):
    flat = packed.reshape(-1)
    out, at = [], 0
    for s in shapes:
        n = math.prod(s)
        out.append(flat[at:at + n].reshape(s))
        at += n
    return out


def kernel(x, w_in, conv_w, a_log, dt_bias, norm_w, sinks, w_out, ln_g, ln_b, loss_target, m_w_in, m_conv_w, m_a_log, m_dt_bias, m_norm_w, m_sinks, m_w_out, m_ln_g, m_ln_b, v_w_in, v_conv_w, v_a_log, v_dt_bias, v_norm_w, v_sinks, v_w_out, v_ln_g, v_ln_b):
    xi, yi, ci = lax.axis_index("x"), lax.axis_index("y"), lax.axis_index("c")
    me = 2 * xi + yi

    to_t = lambda a: jnp.transpose(a, (2, 0, 1))
    from_t = lambda a: jnp.transpose(a, (1, 2, 0))

    wt_shard = to_t(w_in)

    def pack_weights(l):
        rows = jnp.pad(wt_shard[:, l], ((0, IN_PAD - IN_SHARD), (0, 0)))
        return jnp.concatenate([rows, w_out[l]], axis=0).astype(BF16)

    def unpack_weights(got):
        wt = _to_kernel_order(got[:, :IN_SHARD].reshape(IN_COLS, D_MODEL))
        return wt, got[:, IN_PAD:].reshape(D_MODEL, D_MODEL)

    got0, g_conv = _run_exchange(_gather_exchange([pack_weights(0), conv_w]), name="gather_weights_0")
    conv_full = jnp.moveaxis(g_conv, 0, 2).reshape(DEPTH, CONV_K, 3 * A_WIDTH)
    layer_args = lambda l, got: _layer_args(unpack_weights(got)[0], conv_full[l], a_log[l], dt_bias[l], sinks[l],
                                            norm_w[l], unpack_weights(got)[1])

    def pack_grads(g):
        gin = _from_kernel_order(*g["w_in"]).reshape(N_SHARD, IN_SHARD, D_MODEL)
        gin = jnp.pad(gin, ((0, 0), (0, IN_PAD - IN_SHARD), (0, 0)))
        return jnp.concatenate([gin, g["w_out"].reshape(N_SHARD, OUT_SHARD, D_MODEL)], axis=1).astype(BF16)

    packed = {}

    def reduce1(grads1):
        packed[1] = pack_grads(grads1)
        return _reduce_scatter_exchange(packed[1])

    loss_tile, dx, grads, landed1 = _local_step(
        x[0], loss_target[0], layer_args(0, got0), lambda got: layer_args(1, got[0]), ln_g, ln_b,
        gather1=_gather_exchange([pack_weights(1)]), reduce1=reduce1)
    loss = lax.psum(loss_tile[0, 0], ("x", "y", "c"))

    packed[0] = pack_grads(grads[0])
    small_shapes = [(DEPTH,) + grads[0][nm].shape for nm in SMALL]
    gsmall = _pack([jnp.stack([g[nm] for g in grads]) for nm in SMALL], D_MODEL)
    landed0, landed_small = _run_exchange(_reduce_scatter_exchange(packed[0], gsmall), name="reduce_0")
    halves = [_sum_scatter(packed[l], land, me, ci, tc=2 * LANE, name=f"reduce_sum_{l}")
              for l, land in ((0, landed0), (1, landed1[0]))]
    s_small = _sum_slots(landed_small, name="reduce_sum_small")
    full = _run_exchange(_share_exchange(halves), name="pair_share")[0]
    grad_in_t = jnp.transpose(full[:, :IN_SHARD], (1, 0, 2))
    grad_out = full[:, IN_PAD:]
    out_blk = (1, OUT_SHARD, D_MODEL)
    gs = dict(zip(SMALL, _unpack(s_small, small_shapes)))
    gs["conv_w"] = lax.dynamic_slice_in_dim(gs["conv_w"], me * CONV_SHARD, CONV_SHARD, axis=2)

    adam_in_blk = (IN_SHARD // 6, DEPTH, D_MODEL)
    d_in, nm_in, nv_in = (from_t(o) for o in _adamw(to_t(w_in), grad_in_t, to_t(m_w_in), to_t(v_w_in), adam_in_blk,
                                                    name="adamw_in"))
    d_out, nm_out, nv_out = _adamw(w_out, grad_out, m_w_out, v_w_out, out_blk, name="adamw_out")
    ws = dict(conv_w=conv_w, a_log=a_log, dt_bias=dt_bias, norm_w=norm_w, sinks=sinks, ln_g=ln_g, ln_b=ln_b)
    ms = dict(conv_w=m_conv_w, a_log=m_a_log, dt_bias=m_dt_bias, norm_w=m_norm_w, sinks=m_sinks, ln_g=m_ln_g, ln_b=m_ln_b)
    vs = dict(conv_w=v_conv_w, a_log=v_a_log, dt_bias=v_dt_bias, norm_w=v_norm_w, sinks=v_sinks, ln_g=v_ln_g, ln_b=v_ln_b)
    shard_shapes = [ws[nm].shape for nm in SMALL]
    packed = [_pack([d[nm] for nm in SMALL], LANE)[None] for d in (ws, gs, ms, vs)]
    d_s, nm_s, nv_s = (dict(zip(SMALL, _unpack(o, shard_shapes)))
                       for o in _adamw(*packed, packed[0].shape, name="adamw_small"))

    def in_order(big_in, small, big_out):
        return (big_in, small["conv_w"], small["a_log"], small["dt_bias"], small["norm_w"], small["sinks"], big_out,
                small["ln_g"], small["ln_b"])

    return (loss, dx[None], *in_order(from_t(grad_in_t), gs, grad_out), *in_order(d_in, d_s, d_out),
            *in_order(nm_in, nm_s, nm_out), *in_order(nv_in, nv_s, nv_out))
```

```python
import math

import jax
import jax.numpy as jnp
from jax import lax
from jax.experimental import pallas as pl
from jax.experimental.pallas import tpu as pltpu

F32 = jnp.float32
BF16 = jnp.bfloat16
HI = lax.Precision.HIGHEST

D_MODEL = 1024
DEPTH = 2
A_HEADS = 4
A_HEAD_DIM = 128
A_WIDTH = 512
CONV_K = 4
CHUNK = 64
B_Q_HEADS = 8
B_KV_HEADS = 2
B_HEAD_DIM = 64
B_GROUP = 4
B_WIDTH = 512
B_KV_WIDTH = 128
BLOCK = 128
IN_COLS = 3336
DEEPNORM_ALPHA = (2 * DEPTH) ** 0.25
LN_EPS = 1e-5
RMS_EPS = 1e-6
L2_EPS = 1e-6
ADAM_LR = 0.001
ADAM_B1 = 0.9
ADAM_B2 = 0.999
ADAM_EPS = 1e-08
ADAM_WD = 0.01
ADAM_STEP = 10

N_SHARD = 4
IN_SHARD = IN_COLS // N_SHARD
OUT_SHARD = D_MODEL // N_SHARD
CONV_SHARD = 3 * A_WIDTH // N_SHARD
IN_PAD = -(-IN_SHARD // 16) * 16

P_COLS = 3456
C_PRE = 0
C_ZA = 1536
C_QB = 2048
C_ZB = 2560
C_KB = 3072
C_VB = 3200
C_BG = 3328
DH_MAIN = C_KB
LANE = 128
SUBLANE = 8
VMEM_LIMIT = 56 * 1024 * 1024
ALIBI = tuple(2.0 ** (-8.0 * (h + 1) / B_Q_HEADS) for h in range(B_Q_HEADS))
NEG = -1e30


def _cp(*sem):
    return pltpu.CompilerParams(dimension_semantics=sem, vmem_limit_bytes=VMEM_LIMIT)


def _dot(a, b):
    return jnp.dot(a.astype(BF16), b.astype(BF16), preferred_element_type=F32)


def _dot_nt(a, b):
    return lax.dot_general(a.astype(BF16), b.astype(BF16), (((1,), (1,)), ((), ())),
                           preferred_element_type=F32)


def _dot_tn(a, b):
    return lax.dot_general(a.astype(BF16), b.astype(BF16), (((0,), (0,)), ((), ())),
                           preferred_element_type=F32)


def _dot_hi(a, b):
    return jnp.dot(a, b, precision=HI, preferred_element_type=F32)


def _sigmoid(x):
    return jax.nn.sigmoid(x)


def _silu(x):
    return x * _sigmoid(x)


def _dsilu(x):
    s = _sigmoid(x)
    return s * (1.0 + x * (1.0 - s))


def _softplus(x):
    return jnp.maximum(x, 0.0) + jnp.log(1.0 + jnp.exp(-jnp.abs(x)))


def _shift_down(cur, before, s):
    if s == 0:
        return cur
    r = pltpu.roll(cur, s, 0)
    rb = pltpu.roll(before, s, 0)
    row = lax.broadcasted_iota(jnp.int32, before.shape, 0)
    head = jnp.where(row < s, rb, r[0:SUBLANE])
    return jnp.concatenate([head, r[SUBLANE:]], axis=0)


def _shift_up(cur, after, s):
    if s == 0:
        return cur
    n = cur.shape[0]
    r = pltpu.roll(cur, n - s, 0)
    ra = pltpu.roll(after, SUBLANE - s, 0)
    row = lax.broadcasted_iota(jnp.int32, after.shape, 0)
    tail = jnp.where(row >= SUBLANE - s, ra, r[n - SUBLANE:])
    return jnp.concatenate([r[:n - SUBLANE], tail], axis=0)


def _conv_fwd(cur, before, w):
    acc = cur * w[CONV_K - 1:CONV_K, :]
    for s in range(1, CONV_K):
        acc = acc + _shift_down(cur, before, s) * w[CONV_K - 1 - s:CONV_K - s, :]
    return acc


def _matmul_nt(a, bt, *, tm, tn, name):
    m, k = a.shape
    n = bt.shape[0]

    def body(a_ref, b_ref, o_ref):
        o_ref[...] = _dot_nt(a_ref[...], b_ref[...])

    return pl.pallas_call(
        body, name=name, grid=(m // tm, n // tn),
        in_specs=[pl.BlockSpec((tm, k), lambda i, j: (i, 0)), pl.BlockSpec((tn, k), lambda i, j: (j, 0))],
        out_specs=pl.BlockSpec((tm, tn), lambda i, j: (i, j)),
        out_shape=jax.ShapeDtypeStruct((m, n), F32),
        compiler_params=_cp("parallel", "parallel"))(a, bt)


def _dn_pre(h, conv_w, par, *, tt, name):
    t = h.shape[0]
    cw = 3 * A_WIDTH
    hb = tt // SUBLANE

    def body(pre_ref, halo_ref, bgi_ref, cw_ref, par_ref, q_ref, k_ref, v_ref, bg_ref, bgt_ref):
        i = pl.program_id(0)
        cur = pre_ref[...]
        before = jnp.where(i > 0, halo_ref[...], 0.0)
        s = _silu(_conv_fwd(cur, before, cw_ref[...]))
        for hd in range(A_HEADS):
            sl = slice(hd * LANE, (hd + 1) * LANE)
            tq = s[:, hd * LANE:(hd + 1) * LANE]
            q_ref[:, sl] = tq * (lax.rsqrt(jnp.sum(tq * tq, -1, keepdims=True) + L2_EPS) * (A_HEAD_DIM ** -0.5))
            tk = s[:, A_WIDTH + hd * LANE:A_WIDTH + (hd + 1) * LANE]
            k_ref[:, sl] = tk * lax.rsqrt(jnp.sum(tk * tk, -1, keepdims=True) + L2_EPS)
        v_ref[...] = s[:, 2 * A_WIDTH:]
        raw = bgi_ref[...]
        lane = lax.broadcasted_iota(jnp.int32, raw.shape, 1)
        is_a = (lane >= A_HEADS) & (lane < 2 * A_HEADS)
        g = jnp.where(is_a, -jnp.exp(par_ref[0:1, :]) * _softplus(raw + par_ref[1:2, :]), 0.0)
        gc = _dot_hi(_chunk_tri(tt, lower=True), g)
        bg = jnp.where(lane < A_HEADS, _sigmoid(raw), gc)
        bg_ref[...] = bg
        bgt_ref[...] = jnp.transpose(bg)[0:SUBLANE, :]

    wide = jax.ShapeDtypeStruct((t, A_WIDTH), F32)
    return pl.pallas_call(
        body, name=name, grid=(t // tt,),
        in_specs=[pl.BlockSpec((tt, cw), lambda i: (i, 0)),
                  pl.BlockSpec((SUBLANE, cw), lambda i: (jnp.maximum(i * hb - 1, 0), 0)),
                  pl.BlockSpec((tt, LANE), lambda i: (i, C_BG // LANE)),
                  pl.BlockSpec((CONV_K, cw), lambda i: (0, 0)),
                  pl.BlockSpec((SUBLANE, LANE), lambda i: (0, 0))],
        out_specs=[pl.BlockSpec((tt, A_WIDTH), lambda i: (i, 0))] * 3
        + [pl.BlockSpec((tt, LANE), lambda i: (i, 0)), pl.BlockSpec((SUBLANE, tt), lambda i: (0, i))],
        out_shape=[wide, wide, wide, jax.ShapeDtypeStruct((t, LANE), F32), jax.ShapeDtypeStruct((SUBLANE, t), F32)],
        compiler_params=_cp("parallel"))(h, h, h, conv_w, par)


def _chunk_tri(n, lower):
    r = lax.broadcasted_iota(jnp.int32, (n, n), 0)
    c = lax.broadcasted_iota(jnp.int32, (n, n), 1)
    shift = CHUNK.bit_length() - 1
    same = jnp.right_shift(r, shift) == jnp.right_shift(c, shift)
    return (same & ((c <= r) if lower else (c >= r))).astype(F32)


def _chunk_masks():
    r = lax.broadcasted_iota(jnp.int32, (CHUNK, CHUNK), 0)
    c = lax.broadcasted_iota(jnp.int32, (CHUNK, CHUNK), 1)
    return r >= c, r > c, r == c


def _split(a):
    hi = a.astype(BF16)
    return hi, (a - hi.astype(F32)).astype(BF16)


def _dot3(a, b):
    (ah, al), (bh, bl) = a, b
    d = lambda p, q: jnp.dot(p, q, preferred_element_type=F32)
    return d(ah, bh) + (d(ah, bl) + d(al, bh))


def _tri_inv_many(a_list, eye):
    d = lambda p, q: jnp.dot(p, q, preferred_element_type=F32)
    p = [(-a).astype(BF16) for a in a_list]
    tm = [eye - a for a in a_list]
    for _ in range(5):
        pf = [d(pi, pi) for pi in p]
        p = [x.astype(BF16) for x in pf]
        tm = [t + d(t.astype(BF16), pi) for t, pi in zip(tm, p)]
    ms = [_split(eye + a) for a in a_list]
    res = [eye - _dot3(m, _split(t)) for m, t in zip(ms, tm)]
    return [t + d(t.astype(BF16), r.astype(BF16)) for t, r in zip(tm, res)]


def _chunk_gates(bg_v, bgt_v, hd):
    return (bg_v[:, hd:hd + 1], bg_v[:, A_HEADS + hd:A_HEADS + hd + 1],
            None if bgt_v is None else bgt_v[A_HEADS + hd:A_HEADS + hd + 1, :])


WY_ROWS = 256
SCAN_ROWS = 128
WY_GROUP = 2


def _dn_wy(q, k, v, bg, bgt, *, name):
    t = q.shape[0]
    rows = WY_ROWS

    def body(q_ref, k_ref, v_ref, bg_ref, bgt_ref, u_ref, w_ref, tm_ref, qk_ref):
        causal, strict, diag = _chunk_masks()
        eye = diag.astype(F32)
        for c0 in range(0, rows // CHUNK, WY_GROUP):
            items = [(c, hd) for c in range(c0, c0 + WY_GROUP) for hd in range(A_HEADS)]
            rs = lambda c: slice(c * CHUNK, (c + 1) * CHUNK)
            sl = lambda hd: slice(hd * LANE, (hd + 1) * LANE)
            hs = lambda hd: slice(hd * CHUNK, (hd + 1) * CHUNK)
            gates = [_chunk_gates(bg_ref[rs(c), :], bgt_ref[:, rs(c)], hd) for c, hd in items]
            dms = [jnp.exp(jnp.where(causal, gcol - grow, NEG)) for _, gcol, grow in gates]
            kbs = [k_ref[rs(c), sl(hd)] * g[0] for (c, hd), g in zip(items, gates)]
            a_list = [jnp.where(strict, _dot_nt(kb, k_ref[rs(c), sl(hd)]) * dm, 0.0)
                      for (c, hd), kb, dm in zip(items, kbs, dms)]
            for (c, hd), dm in zip(items, dms):
                qk_ref[rs(c), hs(hd)] = jnp.where(
                    causal, _dot_nt(q_ref[rs(c), sl(hd)], k_ref[rs(c), sl(hd)]) * dm, 0.0)
            tms = _tri_inv_many(a_list, eye)
            for (c, hd), g, kb, tmat in zip(items, gates, kbs, tms):
                tm_ref[rs(c), hs(hd)] = tmat
                u_ref[rs(c), sl(hd)] = _dot(tmat, v_ref[rs(c), sl(hd)] * g[0])
                w_ref[rs(c), sl(hd)] = _dot(tmat, kb * jnp.exp(g[1])).astype(BF16)

    blk = pl.BlockSpec((rows, A_WIDTH), lambda i: (i, 0))
    half = pl.BlockSpec((rows, A_HEADS * CHUNK), lambda i: (i, 0))
    return pl.pallas_call(
        body, name=name, grid=(t // rows,),
        in_specs=[blk, blk, blk, pl.BlockSpec((rows, LANE), lambda i: (i, 0)),
                  pl.BlockSpec((SUBLANE, rows), lambda i: (0, i))],
        out_specs=[blk, blk, half, half],
        out_shape=[jax.ShapeDtypeStruct((t, A_WIDTH), F32), jax.ShapeDtypeStruct((t, A_WIDTH), BF16),
                   jax.ShapeDtypeStruct((t, A_HEADS * CHUNK), F32), jax.ShapeDtypeStruct((t, A_HEADS * CHUNK), F32)],
        compiler_params=_cp("parallel"))(q, k, v, bg, bgt)


def _dn_scan_fwd(q, k, u, w, qk, bg, *, name):
    t = q.shape[0]
    rows = SCAN_ROWS
    per = rows // CHUNK

    def body(q_ref, k_ref, u_ref, w_ref, qk_ref, bg_ref, o_ref, vn_ref, s_ref, state):
        @pl.when(pl.program_id(0) == 0)
        def _():
            state[...] = jnp.zeros_like(state)

        heads = range(A_HEADS)
        sl = lambda hd: slice(hd * LANE, (hd + 1) * LANE)
        s_cur = [state[hd] for hd in heads]
        for c in range(per):
            rs = slice(c * CHUNK, (c + 1) * CHUNK)
            bg_v = bg_ref[rs, :]
            gcols = [_chunk_gates(bg_v, None, hd)[1] for hd in heads]
            glasts = [gc[CHUNK - 1:CHUNK, :] for gc in gcols]
            for hd in heads:
                s_ref[c, hd] = s_cur[hd]
            vns = [u_ref[rs, sl(hd)] - _dot(w_ref[rs, sl(hd)], s_cur[hd]) for hd in heads]
            qss = [_dot(q_ref[rs, sl(hd)] * jnp.exp(gcols[hd]), s_cur[hd]) for hd in heads]
            s_cur = [s_cur[hd] * jnp.exp(glasts[hd])
                     + _dot_tn(k_ref[rs, sl(hd)] * jnp.exp(glasts[hd] - gcols[hd]), vns[hd]) for hd in heads]
            for hd in heads:
                vn_ref[rs, sl(hd)] = vns[hd]
                o_ref[rs, sl(hd)] = qss[hd] + _dot(qk_ref[rs, hd * CHUNK:(hd + 1) * CHUNK], vns[hd])
        for hd in heads:
            state[hd] = s_cur[hd]

    blk = pl.BlockSpec((rows, A_WIDTH), lambda i: (i, 0))
    half = pl.BlockSpec((rows, A_HEADS * CHUNK), lambda i: (i, 0))
    wide = jax.ShapeDtypeStruct((t, A_WIDTH), F32)
    return pl.pallas_call(
        body, name=name, grid=(t // rows,),
        in_specs=[blk, blk, blk, blk, half, pl.BlockSpec((rows, LANE), lambda i: (i, 0))],
        out_specs=[blk, blk, pl.BlockSpec((per, A_HEADS, LANE, LANE), lambda i: (i, 0, 0, 0))],
        out_shape=[wide, wide, jax.ShapeDtypeStruct((t // CHUNK, A_HEADS, LANE, LANE), F32)],
        scratch_shapes=[pltpu.VMEM((A_HEADS, LANE, LANE), F32)],
        compiler_params=_cp("arbitrary"))(q, k, u, w, qk, bg)


def _swa_neg_dist(n_blk):
    qi = lax.broadcasted_iota(jnp.int32, (BLOCK, 2 * BLOCK), 0)
    si = lax.broadcasted_iota(jnp.int32, (BLOCK, 2 * BLOCK), 1)
    dist = qi + BLOCK - si
    mask = (dist >= 0) & (dist < BLOCK) & ((si >= BLOCK) | (n_blk > 0))
    return jnp.where(mask, -dist.astype(F32), NEG)


def _stack_heads(ref, hk):
    return jnp.concatenate([ref[:, h * B_HEAD_DIM:(h + 1) * B_HEAD_DIM]
                            for h in range(hk * B_GROUP, (hk + 1) * B_GROUP)], axis=0)


def _swa_group_probs(q_ref, sk_ref, kh, vh, neg_dist, hk):
    heads = range(hk * B_GROUP, (hk + 1) * B_GROUP)
    qs = _stack_heads(q_ref, hk) * (B_HEAD_DIM ** -0.5)
    bias = jnp.concatenate([ALIBI[h] * neg_dist for h in heads], axis=0)
    sink = jnp.concatenate([jnp.broadcast_to(sk_ref[h:h + 1, 0:1], (BLOCK, 1)) for h in heads], axis=0)
    s = _dot_nt(qs, kh) + bias
    m = jnp.maximum(jnp.max(s, axis=-1, keepdims=True), sink)
    p = jnp.exp(s - m)
    vext = jnp.concatenate([vh.astype(BF16), jnp.ones((2 * BLOCK, B_HEAD_DIM), BF16)], axis=1)
    oe = jnp.dot(p.astype(BF16), vext, preferred_element_type=F32)
    ps = jnp.exp(sink - m)
    inv = 1.0 / (oe[:, B_HEAD_DIM:B_HEAD_DIM + 1] + ps)
    return qs, p * inv, ps * inv, oe[:, :B_HEAD_DIM] * inv


def _swa_specs():
    qspec = lambda c0: pl.BlockSpec((BLOCK, B_WIDTH), lambda i: (i, c0 // B_WIDTH))
    cur = lambda c0: pl.BlockSpec((BLOCK, LANE), lambda i: (i, c0 // LANE))
    prev = lambda c0: pl.BlockSpec((BLOCK, LANE), lambda i: (jnp.maximum(i - 1, 0), c0 // LANE))
    return qspec, cur, prev


def _carried(carry, refs, n_in, n_out, steps):
    if carry is None:
        return refs
    ci, co = len(carry.ins), len(carry.outs)
    own = refs[:n_in] + refs[n_in + ci:n_in + ci + n_out] + refs[n_in + ci + n_out + co:len(refs) - 3]
    parts = refs[n_in:n_in + ci], refs[n_in + ci + n_out:n_in + ci + n_out + co], refs[len(refs) - 3:]

    @pl.when(pl.program_id(0) == 0)
    def _():
        carry.start(*parts)

    @pl.when(pl.program_id(0) == steps - 1)
    def _():
        carry.finish(*parts)

    return own


def _carry_specs(carry):
    if carry is None:
        return [], [], [], [], []
    return (list(carry.ins), [_ANY] * len(carry.ins), [_ANY] * len(carry.outs), list(carry.outs), carry.scratch())


def _swa_fwd(h, sinks_b, *, name, carry=None):
    t = h.shape[0]
    qspec, cur, prev = _swa_specs()
    c_ins, c_in_specs, c_out_specs, c_outs, c_scratch = _carry_specs(carry)

    def body(*refs):
        q_ref, kc_ref, kp_ref, vc_ref, vp_ref, sk_ref, o_ref = _carried(carry, refs, 6, 1, t // BLOCK)
        n_blk = pl.program_id(0)
        kband = jnp.concatenate([kp_ref[...], kc_ref[...]], axis=0)
        vband = jnp.concatenate([vp_ref[...], vc_ref[...]], axis=0)
        neg_dist = _swa_neg_dist(n_blk)
        for hk in range(B_KV_HEADS):
            ksl = slice(hk * B_HEAD_DIM, (hk + 1) * B_HEAD_DIM)
            _, _, _, o = _swa_group_probs(q_ref, sk_ref, kband[:, ksl], vband[:, ksl], neg_dist, hk)
            for g in range(B_GROUP):
                hq = hk * B_GROUP + g
                o_ref[:, hq * B_HEAD_DIM:(hq + 1) * B_HEAD_DIM] = o[g * BLOCK:(g + 1) * BLOCK]

    outs = pl.pallas_call(
        body, name=name, grid=(t // BLOCK,),
        in_specs=[qspec(C_QB), cur(C_KB), prev(C_KB), cur(C_VB), prev(C_VB),
                  pl.BlockSpec((B_Q_HEADS, LANE), lambda i: (0, 0))] + c_in_specs,
        out_specs=[pl.BlockSpec((BLOCK, B_WIDTH), lambda i: (i, 0))] + c_out_specs,
        out_shape=[jax.ShapeDtypeStruct((t, B_WIDTH), F32)] + c_outs,
        scratch_shapes=c_scratch,
        compiler_params=_cp("arbitrary"))(h, h, h, h, h, sinks_b, *c_ins)
    return outs[0], outs[1:]


def _rms_gate(o, za, nw):
    outs = []
    for hd in range(A_HEADS):
        oh = o[:, hd * LANE:(hd + 1) * LANE]
        r = lax.rsqrt(jnp.mean(oh * oh, -1, keepdims=True) + RMS_EPS)
        outs.append(oh * r * nw)
    return jnp.concatenate(outs, axis=1) * _silu(za)


def _out_ln(x, oa, ob, h, norm_w, w_out, ln_g, ln_b, *, tm, name):
    t = x.shape[0]

    def body(x_ref, oa_ref, ob_ref, za_ref, zb_ref, nw_ref, w_ref, g_ref, b_ref, xn_ref, mx_ref, r_ref):
        ya = _rms_gate(oa_ref[...], za_ref[...], nw_ref[...])
        yb = ob_ref[...] * _silu(zb_ref[...])
        mixed = jnp.concatenate([ya, yb], axis=1).astype(BF16)
        mx_ref[...] = mixed
        r = DEEPNORM_ALPHA * x_ref[...] + jnp.dot(mixed, w_ref[...], preferred_element_type=F32)
        r_ref[...] = r
        mu = jnp.mean(r, -1, keepdims=True)
        xc = r - mu
        var = jnp.mean(xc * xc, -1, keepdims=True)
        xn_ref[...] = xc * lax.rsqrt(var + LN_EPS) * g_ref[...] + b_ref[...]

    row = lambda w, c: pl.BlockSpec((tm, w), lambda i: (i, c))
    full = lambda a, b: pl.BlockSpec((a, b), lambda i: (0, 0))
    return pl.pallas_call(
        body, name=name, grid=(t // tm,),
        in_specs=[row(D_MODEL, 0), row(A_WIDTH, 0), row(B_WIDTH, 0), row(A_WIDTH, C_ZA // A_WIDTH),
                  row(B_WIDTH, C_ZB // B_WIDTH), full(1, LANE), full(D_MODEL, D_MODEL), full(1, D_MODEL), full(1, D_MODEL)],
        out_specs=[row(D_MODEL, 0), row(D_MODEL, 0), row(D_MODEL, 0)],
        out_shape=[jax.ShapeDtypeStruct((t, D_MODEL), F32), jax.ShapeDtypeStruct((t, D_MODEL), BF16),
                   jax.ShapeDtypeStruct((t, D_MODEL), F32)],
        compiler_params=_cp("parallel"))(x, oa, ob, h, h, norm_w, w_out, ln_g, ln_b)


def _layer_fwd(x, wt, conv_w, par, sinks_b, norm_w, w_out_bf, ln_g, ln_b, l, carry=None):
    h = _matmul_nt(x, wt, tm=512, tn=P_COLS, name=f"in_proj_{l}")
    q, k, v, bg, bgt = _dn_pre(h, conv_w, par, tt=512, name=f"dn_pre_{l}")
    u, w, tmat, qk = _dn_wy(q, k, v, bg, bgt, name=f"dn_wy_{l}")
    oa, vn, s_all = _dn_scan_fwd(q, k, u, w, qk, bg, name=f"dn_scan_{l}")
    ob, carried = _swa_fwd(h, sinks_b, name=f"swa_fwd_{l}", carry=carry)
    xn, mixed, r = _out_ln(x, oa, ob, h, norm_w, w_out_bf, ln_g, ln_b, tm=256, name=f"out_ln_{l}")
    res = dict(x=x, h=h, q=q, k=k, v=v, bg=bg, bgt=bgt, w=w, tmat=tmat, qk=qk, vn=vn, oa=oa, s_all=s_all,
               mixed=mixed, r=r)
    return xn, res, carried


def _loss_grad(xn, target, *, tm, name):
    t = xn.shape[0]

    def body(x_ref, t_ref, d_ref, l_ref):
        @pl.when(pl.program_id(0) == 0)
        def _():
            l_ref[...] = jnp.zeros_like(l_ref)

        err = x_ref[...] - t_ref[...]
        d_ref[...] = err * (1.0 / D_MODEL)
        l_ref[...] += 0.5 / D_MODEL * jnp.sum(err * err)

    row = pl.BlockSpec((tm, D_MODEL), lambda i: (i, 0))
    return pl.pallas_call(
        body, name=name, grid=(t // tm,), in_specs=[row, row],
        out_specs=[row, pl.BlockSpec((SUBLANE, LANE), lambda i: (0, 0))],
        out_shape=[jax.ShapeDtypeStruct((t, D_MODEL), F32), jax.ShapeDtypeStruct((SUBLANE, LANE), F32)],
        compiler_params=_cp("arbitrary"))(xn, target)


def _ln_out_bwd(dxn, r, mixed, ln_g, w_out, *, tm, name):
    t = dxn.shape[0]

    def body(dxn_ref, r_ref, mx_ref, g_ref, w_ref, dr_ref, dm_ref, dw_ref, dg_ref, db_ref):
        @pl.when(pl.program_id(0) == 0)
        def _():
            dw_ref[...] = jnp.zeros_like(dw_ref)
            dg_ref[...] = jnp.zeros_like(dg_ref)
            db_ref[...] = jnp.zeros_like(db_ref)

        rr = r_ref[...]
        xc = rr - jnp.mean(rr, -1, keepdims=True)
        rstd = lax.rsqrt(jnp.mean(xc * xc, -1, keepdims=True) + LN_EPS)
        xhat = xc * rstd
        dxn_v = dxn_ref[...]
        dxh = dxn_v * g_ref[...]
        dr = rstd * (dxh - jnp.mean(dxh, -1, keepdims=True) - xhat * jnp.mean(dxh * xhat, -1, keepdims=True))
        dr_ref[...] = dr
        dg_ref[...] += jnp.sum(dxn_v * xhat, axis=0, keepdims=True)
        db_ref[...] += jnp.sum(dxn_v, axis=0, keepdims=True)
        drb = dr.astype(BF16)
        dm_ref[...] = _dot_nt(drb, w_ref[...])
        dw_ref[...] += _dot_tn(mx_ref[...], drb)

    row = pl.BlockSpec((tm, D_MODEL), lambda i: (i, 0))
    full = lambda a, b: pl.BlockSpec((a, b), lambda i: (0, 0))
    big = jax.ShapeDtypeStruct((t, D_MODEL), F32)
    vec = jax.ShapeDtypeStruct((1, D_MODEL), F32)
    return pl.pallas_call(
        body, name=name, grid=(t // tm,),
        in_specs=[row, row, row, full(1, D_MODEL), full(D_MODEL, D_MODEL)],
        out_specs=[row, row, full(D_MODEL, D_MODEL), full(1, D_MODEL), full(1, D_MODEL)],
        out_shape=[big, big, jax.ShapeDtypeStruct((D_MODEL, D_MODEL), F32), vec, vec],
        compiler_params=_cp("arbitrary"))(dxn, r, mixed, ln_g, w_out)


def _dn_post_bwd(dm, oa, h, norm_w, *, tm, name):
    t = oa.shape[0]

    def body(dy_ref, o_ref, za_ref, nw_ref, do_ref, dza_ref, dnw_ref):
        @pl.when(pl.program_id(0) == 0)
        def _():
            dnw_ref[...] = jnp.zeros_like(dnw_ref)

        nw = nw_ref[...]
        dnw = jnp.zeros_like(nw)
        for hd in range(A_HEADS):
            sl = slice(hd * LANE, (hd + 1) * LANE)
            oh, za, dy = o_ref[:, sl], za_ref[:, sl], dy_ref[:, sl]
            rs = lax.rsqrt(jnp.mean(oh * oh, -1, keepdims=True) + RMS_EPS)
            nrm = oh * rs
            dza_ref[:, sl] = dy * nrm * nw * _dsilu(za)
            dn = dy * _silu(za)
            dnw = dnw + jnp.sum(dn * nrm, axis=0, keepdims=True)
            dnn = dn * nw
            do_ref[:, sl] = rs * dnn - oh * (rs * rs * rs) * jnp.mean(dnn * oh, -1, keepdims=True)
        dnw_ref[...] += dnw

    row = lambda c: pl.BlockSpec((tm, A_WIDTH), lambda i: (i, c))
    wide = jax.ShapeDtypeStruct((t, A_WIDTH), F32)
    return pl.pallas_call(
        body, name=name, grid=(t // tm,),
        in_specs=[row(0), row(0), row(C_ZA // A_WIDTH), pl.BlockSpec((1, LANE), lambda i: (0, 0))],
        out_specs=[row(0), row(C_ZA // A_WIDTH), pl.BlockSpec((1, LANE), lambda i: (0, 0))],
        out_shape=[wide, jax.ShapeDtypeStruct((t, DH_MAIN), F32), jax.ShapeDtypeStruct((1, LANE), F32)],
        compiler_params=_cp("arbitrary"))(dm, oa, h, norm_w)


def _dn_scan_bwd(q, k, w, qk, bg, do, *, name):
    t = q.shape[0]
    rows = SCAN_ROWS
    per = rows // CHUNK
    n = t // rows

    def body(q_ref, k_ref, w_ref, qk_ref, bg_ref, do_ref, dvn_ref, ds_ref, dstate):
        @pl.when(pl.program_id(0) == 0)
        def _():
            dstate[...] = jnp.zeros_like(dstate)

        heads = range(A_HEADS)
        sl = lambda hd: slice(hd * LANE, (hd + 1) * LANE)
        ds_cur = [dstate[hd] for hd in heads]
        for c in reversed(range(per)):
            rs = slice(c * CHUNK, (c + 1) * CHUNK)
            bg_v = bg_ref[rs, :]
            gcols = [_chunk_gates(bg_v, None, hd)[1] for hd in heads]
            glasts = [gc[CHUNK - 1:CHUNK, :] for gc in gcols]
            for hd in heads:
                ds_ref[c, hd] = ds_cur[hd]
            pdo = [_dot_tn(qk_ref[rs, hd * CHUNK:(hd + 1) * CHUNK], do_ref[rs, sl(hd)]) for hd in heads]
            qdo = [_dot_tn(q_ref[rs, sl(hd)] * jnp.exp(gcols[hd]), do_ref[rs, sl(hd)]) for hd in heads]
            dvns = [pdo[hd] + _dot(k_ref[rs, sl(hd)] * jnp.exp(glasts[hd] - gcols[hd]), ds_cur[hd]) for hd in heads]
            ds_cur = [qdo[hd] + jnp.exp(glasts[hd]) * ds_cur[hd] - _dot_tn(w_ref[rs, sl(hd)], dvns[hd])
                      for hd in heads]
            for hd in heads:
                dvn_ref[rs, sl(hd)] = dvns[hd]
        for hd in heads:
            dstate[hd] = ds_cur[hd]

    blk = pl.BlockSpec((rows, A_WIDTH), lambda i: (n - 1 - i, 0))
    return pl.pallas_call(
        body, name=name, grid=(n,),
        in_specs=[blk, blk, blk, pl.BlockSpec((rows, A_HEADS * CHUNK), lambda i: (n - 1 - i, 0)),
                  pl.BlockSpec((rows, LANE), lambda i: (n - 1 - i, 0)), blk],
        out_specs=[blk, pl.BlockSpec((per, A_HEADS, LANE, LANE), lambda i: (n - 1 - i, 0, 0, 0))],
        out_shape=[jax.ShapeDtypeStruct((t, A_WIDTH), F32),
                   jax.ShapeDtypeStruct((t // CHUNK, A_HEADS, LANE, LANE), F32)],
        scratch_shapes=[pltpu.VMEM((A_HEADS, LANE, LANE), F32)],
        compiler_params=_cp("arbitrary"))(q, k, w, qk, bg, do)


def _dn_chunk_bwd(q, k, v, vn, tmat, qk, bg, bgt, s_all, ds_all, dvn, do, *, name):
    t = q.shape[0]
    rows = WY_ROWS
    per = rows // CHUNK

    def body(q_ref, k_ref, v_ref, vn_ref, tm_ref, qk_ref, bg_ref, bgt_ref, s_ref, ds_ref, dvn_ref, do_ref,
             dq_ref, dk_ref, dv_ref, dbg_ref, dbgt_ref):
        causal, strict, _ = _chunk_masks()
        lane = lax.broadcasted_iota(jnp.int32, (CHUNK, LANE), 1)
        rowi = lax.broadcasted_iota(jnp.int32, (CHUNK, 1), 0)
        sub = lax.broadcasted_iota(jnp.int32, (SUBLANE, CHUNK), 0)
        rs = lambda c: slice(c * CHUNK, (c + 1) * CHUNK)
        sl = lambda hd: slice(hd * LANE, (hd + 1) * LANE)
        hs = lambda hd: slice(hd * CHUNK, (hd + 1) * CHUNK)
        for c0 in range(0, per, WY_GROUP):
            items = [(c, hd) for c in range(c0, c0 + WY_GROUP) for hd in range(A_HEADS)]
            at = lambda ref: [ref[rs(c), sl(hd)] for c, hd in items]
            qs, ks, vs, dos, vns, dvns = at(q_ref), at(k_ref), at(v_ref), at(do_ref), at(vn_ref), at(dvn_ref)
            tmhs = [tm_ref[rs(c), hs(hd)] for c, hd in items]
            ps = [qk_ref[rs(c), hs(hd)] for c, hd in items]
            gates = [_chunk_gates(bg_ref[rs(c), :], bgt_ref[:, rs(c)], hd) for c, hd in items]
            betas = [g[0] for g in gates]
            gcols = [g[1] for g in gates]
            dmats = [jnp.exp(jnp.where(causal, g[1] - g[2], NEG)) for g in gates]
            es = [jnp.exp(gc) for gc in gcols]
            glasts = [gc[CHUNK - 1:CHUNK, :] for gc in gcols]
            eks = [jnp.exp(gl - gc) for gl, gc in zip(glasts, gcols)]
            kbs = [kh * b for kh, b in zip(ks, betas)]
            vbs = [vh * b for vh, b in zip(vs, betas)]
            kbes = [kb * e for kb, e in zip(kbs, es)]

            a_s = [jnp.where(strict, _dot_nt(kb, kh) * dm, 0.0) for kb, kh, dm in zip(kbs, ks, dmats)]
            dps = [jnp.where(causal, _dot_nt(doh, vnh), 0.0) for doh, vnh in zip(dos, vns)]
            dqds = [_dot_nt(doh, s_ref[c, hd]) for doh, (c, hd) in zip(dos, items)]
            dkds = [_dot_nt(vnh, ds_ref[c, hd]) for vnh, (c, hd) in zip(vns, items)]
            dws = [-_dot_nt(dvnh, s_ref[c, hd]) for dvnh, (c, hd) in zip(dvns, items)]
            dvbs = [_dot_tn(tmh, dvnh) for tmh, dvnh in zip(tmhs, dvns)]
            dgts = [jnp.sum(s_ref[c, hd] * ds_ref[c, hd], keepdims=True) for c, hd in items]
            dts = [_dot_nt(dvnh, vb) + _dot_nt(dw, kbe) for dvnh, vb, dw, kbe in zip(dvns, vbs, dws, kbes)]
            dkbes = [_dot_tn(tmh, dw) for tmh, dw in zip(tmhs, dws)]
            xs = [_dot_nt(dt, tmh) for dt, tmh in zip(dts, tmhs)]
            das = [jnp.where(strict, -_dot_tn(tmh, x), 0.0) for tmh, x in zip(tmhs, xs)]
            dmas = [da * dm for da, dm in zip(das, dmats)]
            dmps = [dp * dm for dp, dm in zip(dps, dmats)]
            dkbs = [_dot(dma, kh) + dkbe * e for dma, kh, dkbe, e in zip(dmas, ks, dkbes, es)]
            for i, (c, hd) in enumerate(items):
                dq_ref[rs(c), sl(hd)] = _dot(dmps[i], ks[i]) + dqds[i] * es[i]
                dk_ref[rs(c), sl(hd)] = (_dot_tn(dmas[i], kbs[i]) + _dot_tn(dmps[i], qs[i]) + dkds[i] * eks[i]
                                         + dkbs[i] * betas[i])
                dv_ref[rs(c), sl(hd)] = dvbs[i] * betas[i]
            for c in range(c0, c0 + WY_GROUP):
                acc = jnp.zeros((CHUNK, LANE), F32)
                acc_t = jnp.zeros((SUBLANE, CHUNK), F32)
                for i, (ci, hd) in enumerate(items):
                    if ci != c:
                        continue
                    gmat = das[i] * a_s[i] + dps[i] * ps[i]
                    rk = jnp.sum(dkds[i] * ks[i], -1, keepdims=True) * eks[i]
                    de = (jnp.sum(dqds[i] * qs[i], -1, keepdims=True)
                          + jnp.sum(dkbes[i] * kbs[i], -1, keepdims=True))
                    dglast = jnp.sum(rk, keepdims=True) + dgts[i] * jnp.exp(glasts[i])
                    dgc = (jnp.sum(gmat, -1, keepdims=True) + de * es[i] - rk
                           + jnp.where(rowi == CHUNK - 1, dglast, 0.0))
                    dbeta = (jnp.sum(dkbs[i] * ks[i], -1, keepdims=True)
                             + jnp.sum(dvbs[i] * vs[i], -1, keepdims=True))
                    acc = acc + jnp.where(lane == hd, dbeta, 0.0) + jnp.where(lane == A_HEADS + hd, dgc, 0.0)
                    acc_t = acc_t + jnp.where(sub == A_HEADS + hd, -jnp.sum(gmat, axis=0, keepdims=True), 0.0)
                dbg_ref[rs(c), :] = acc
                dbgt_ref[:, rs(c)] = acc_t

    blk = pl.BlockSpec((rows, A_WIDTH), lambda i: (i, 0))
    half = pl.BlockSpec((rows, A_HEADS * CHUNK), lambda i: (i, 0))
    col = pl.BlockSpec((rows, LANE), lambda i: (i, 0))
    rowf = pl.BlockSpec((SUBLANE, rows), lambda i: (0, i))
    st = pl.BlockSpec((per, A_HEADS, LANE, LANE), lambda i: (i, 0, 0, 0))
    wide = jax.ShapeDtypeStruct((t, A_WIDTH), F32)
    return pl.pallas_call(
        body, name=name, grid=(t // rows,),
        in_specs=[blk, blk, blk, blk, half, half, col, rowf, st, st, blk, blk],
        out_specs=[blk, blk, blk, col, rowf],
        out_shape=[wide, wide, wide, jax.ShapeDtypeStruct((t, LANE), F32), jax.ShapeDtypeStruct((SUBLANE, t), F32)],
        compiler_params=_cp("parallel"))(q, k, v, vn, tmat, qk, bg, bgt, s_all, ds_all, dvn, do)


def _dn_pre_bwd(h, conv_w, par, dq, dk, dv, dbg, dbgt, *, tt, name):
    t = h.shape[0]
    cw = 3 * A_WIDTH
    hb = tt // SUBLANE

    def body(pre_ref, halo_ref, bgi_ref, cw_ref, par_ref, dq_ref, dk_ref, dv_ref, dbg_ref, dbgt_ref,
             dc_ref, dbgi_ref, dpar_ref):
        i = pl.program_id(0)

        @pl.when(i == 0)
        def _():
            dpar_ref[...] = jnp.zeros_like(dpar_ref)

        cur = pre_ref[...]
        before = jnp.where(i > 0, halo_ref[...], 0.0)
        c = _conv_fwd(cur, before, cw_ref[...])
        s = _silu(c)
        ds = _dsilu(c)
        for hd in range(A_HEADS):
            sl = slice(hd * LANE, (hd + 1) * LANE)
            for base, d_ref, scale in ((0, dq_ref, A_HEAD_DIM ** -0.5), (A_WIDTH, dk_ref, 1.0)):
                csl = slice(base + hd * LANE, base + (hd + 1) * LANE)
                tq = s[:, base + hd * LANE:base + (hd + 1) * LANE]
                dy = d_ref[:, sl]
                rq = lax.rsqrt(jnp.sum(tq * tq, -1, keepdims=True) + L2_EPS)
                dtq = scale * (rq * dy - tq * (rq * rq * rq) * jnp.sum(dy * tq, -1, keepdims=True))
                dc_ref[:, csl] = dtq * ds[:, base + hd * LANE:base + (hd + 1) * LANE]
        dc_ref[:, 2 * A_WIDTH:] = dv_ref[...] * ds[:, 2 * A_WIDTH:]
        raw = bgi_ref[...]
        lane = lax.broadcasted_iota(jnp.int32, raw.shape, 1)
        is_b = lane < A_HEADS
        is_a = (lane >= A_HEADS) & (lane < 2 * A_HEADS)
        rows_t = jnp.concatenate([dbgt_ref[...], jnp.zeros((LANE - SUBLANE, tt), F32)], axis=0)
        dbg_v = dbg_ref[...] + jnp.where(is_a, jnp.transpose(rows_t), 0.0)
        dbg_v = jnp.where(is_a, _dot_hi(_chunk_tri(tt, lower=False), jnp.where(is_a, dbg_v, 0.0)), dbg_v)
        beta = _sigmoid(raw)
        z = raw + par_ref[1:2, :]
        neg_ea = -jnp.exp(par_ref[0:1, :])
        g = neg_ea * _softplus(z)
        da = dbg_v * neg_ea * _sigmoid(z)
        dbgi_ref[...] = jnp.where(is_b, dbg_v * beta * (1.0 - beta), jnp.where(is_a, da, 0.0))
        dpar_ref[0:1, :] += jnp.sum(jnp.where(is_a, dbg_v * g, 0.0), axis=0, keepdims=True)
        dpar_ref[1:2, :] += jnp.sum(jnp.where(is_a, da, 0.0), axis=0, keepdims=True)

    wide = pl.BlockSpec((tt, A_WIDTH), lambda i: (i, 0))
    return pl.pallas_call(
        body, name=name, grid=(t // tt,),
        in_specs=[pl.BlockSpec((tt, cw), lambda i: (i, 0)),
                  pl.BlockSpec((SUBLANE, cw), lambda i: (jnp.maximum(i * hb - 1, 0), 0)),
                  pl.BlockSpec((tt, LANE), lambda i: (i, C_BG // LANE)),
                  pl.BlockSpec((CONV_K, cw), lambda i: (0, 0)),
                  pl.BlockSpec((SUBLANE, LANE), lambda i: (0, 0)),
                  wide, wide, wide, pl.BlockSpec((tt, LANE), lambda i: (i, 0)),
                  pl.BlockSpec((SUBLANE, tt), lambda i: (0, i))],
        out_specs=[pl.BlockSpec((tt, cw), lambda i: (i, 0)), pl.BlockSpec((tt, LANE), lambda i: (i, 0)),
                   pl.BlockSpec((SUBLANE, LANE), lambda i: (0, 0))],
        out_shape=[jax.ShapeDtypeStruct((t, cw), F32), jax.ShapeDtypeStruct((t, LANE), F32),
                   jax.ShapeDtypeStruct((SUBLANE, LANE), F32)],
        compiler_params=_cp("arbitrary"))(h, h, h, conv_w, par, dq, dk, dv, dbg, dbgt)


def _conv_bwd(dc, h, conv_w, dh, *, tt, name):
    t = dc.shape[0]
    cw = 3 * A_WIDTH
    hb = tt // SUBLANE
    nb = t // tt

    def body(dc_ref, after_ref, pre_ref, before_ref, cw_ref, dh_in_ref, dpre_ref, dcw_ref):
        i = pl.program_id(0)

        @pl.when(i == 0)
        def _():
            dcw_ref[...] = jnp.zeros_like(dcw_ref)

        dcv = dc_ref[...]
        after = jnp.where(i < nb - 1, after_ref[...], 0.0)
        cur = pre_ref[...]
        before = jnp.where(i > 0, before_ref[...], 0.0)
        w = cw_ref[...]
        acc = dcv * w[CONV_K - 1:CONV_K, :]
        dcw_ref[CONV_K - 1:CONV_K, :] += jnp.sum(dcv * cur, axis=0, keepdims=True)
        for s in range(1, CONV_K):
            j = CONV_K - 1 - s
            acc = acc + _shift_up(dcv, after, s) * w[j:j + 1, :]
            dcw_ref[j:j + 1, :] += jnp.sum(dcv * _shift_down(cur, before, s), axis=0, keepdims=True)
        dpre_ref[...] = acc

    return pl.pallas_call(
        body, name=name, grid=(nb,),
        in_specs=[pl.BlockSpec((tt, cw), lambda i: (i, 0)),
                  pl.BlockSpec((SUBLANE, cw), lambda i: (jnp.minimum((i + 1) * hb, t // SUBLANE - 1), 0)),
                  pl.BlockSpec((tt, cw), lambda i: (i, 0)),
                  pl.BlockSpec((SUBLANE, cw), lambda i: (jnp.maximum(i * hb - 1, 0), 0)),
                  pl.BlockSpec((CONV_K, cw), lambda i: (0, 0)), _ANY],
        out_specs=[pl.BlockSpec((tt, cw), lambda i: (i, 0)), pl.BlockSpec((SUBLANE, cw), lambda i: (0, 0))],
        out_shape=[jax.ShapeDtypeStruct(dh.shape, F32), jax.ShapeDtypeStruct((SUBLANE, cw), F32)],
        input_output_aliases={5: 0},
        compiler_params=_cp("arbitrary"))(dc, dc, h, h, conv_w, dh)


def _swa_bwd(h, dm, sinks_b, dh, *, name, carry=None):
    t = h.shape[0]
    qspec, cur, prev = _swa_specs()
    c_ins, c_in_specs, c_out_specs, c_outs, c_scratch = _carry_specs(carry)

    def body(*refs):
        (q_ref, kc_ref, kp_ref, vc_ref, vp_ref, zb_ref, dy_ref, sk_ref, dh_in_ref,
         dqz_ref, dk_ref, dv_ref, dsk_ref) = _carried(carry, refs, 9, 4, t // BLOCK)
        n_blk = pl.program_id(0)

        @pl.when(n_blk == 0)
        def _():
            dk_ref[...] = jnp.zeros_like(dk_ref)
            dv_ref[...] = jnp.zeros_like(dv_ref)
            dsk_ref[...] = jnp.zeros_like(dsk_ref)

        kband = jnp.concatenate([kp_ref[...], kc_ref[...]], axis=0)
        vband = jnp.concatenate([vp_ref[...], vc_ref[...]], axis=0)
        scale = B_HEAD_DIM ** -0.5
        neg_dist = _swa_neg_dist(n_blk)
        dk_acc, dv_acc = [], []
        for hk in range(B_KV_HEADS):
            ksl = slice(hk * B_HEAD_DIM, (hk + 1) * B_HEAD_DIM)
            kh, vh = kband[:, ksl], vband[:, ksl]
            qs, p, ps, o = _swa_group_probs(q_ref, sk_ref, kh, vh, neg_dist, hk)
            zb, dy = _stack_heads(zb_ref, hk), _stack_heads(dy_ref, hk)
            dzb = dy * o * _dsilu(zb)
            do = dy * _silu(zb)
            delta = jnp.sum(do * o, -1, keepdims=True)
            ds = p * (_dot_nt(do, vh) - delta)
            dq = _dot(ds, kh) * scale
            dk_acc.append(_dot_tn(ds, qs))
            dv_acc.append(_dot_tn(p, do))
            dsink = ps * delta
            for g in range(B_GROUP):
                hq = hk * B_GROUP + g
                rows = slice(g * BLOCK, (g + 1) * BLOCK)
                qsl = slice(hq * B_HEAD_DIM, (hq + 1) * B_HEAD_DIM)
                dqz_ref[:, qsl] = dq[rows]
                dqz_ref[:, B_WIDTH + hq * B_HEAD_DIM:B_WIDTH + (hq + 1) * B_HEAD_DIM] = dzb[rows]
                dsk_ref[hq:hq + 1, :] += -jnp.sum(dsink[rows], keepdims=True)
        dkb = jnp.concatenate(dk_acc, axis=1)
        dvb = jnp.concatenate(dv_acc, axis=1)
        at_cur = pl.ds(pl.multiple_of(n_blk * BLOCK, BLOCK), BLOCK)
        at_prev = pl.ds(pl.multiple_of(jnp.maximum(n_blk - 1, 0) * BLOCK, BLOCK), BLOCK)
        dk_ref[at_prev, :] += dkb[:BLOCK]
        dv_ref[at_prev, :] += dvb[:BLOCK]
        dk_ref[at_cur, :] += dkb[BLOCK:]
        dv_ref[at_cur, :] += dvb[BLOCK:]

    narrow = jax.ShapeDtypeStruct((t, B_KV_WIDTH), F32)
    res = lambda a, b: pl.BlockSpec((a, b), lambda i: (0, 0))
    outs = pl.pallas_call(
        body, name=name, grid=(t // BLOCK,),
        in_specs=[qspec(C_QB), cur(C_KB), prev(C_KB), cur(C_VB), prev(C_VB), qspec(C_ZB),
                  pl.BlockSpec((BLOCK, B_WIDTH), lambda i: (i, 1)), res(B_Q_HEADS, LANE), _ANY] + c_in_specs,
        out_specs=[pl.BlockSpec((BLOCK, 2 * B_WIDTH), lambda i: (i, C_QB // (2 * B_WIDTH))),
                   res(t, B_KV_WIDTH), res(t, B_KV_WIDTH), res(B_Q_HEADS, LANE)] + c_out_specs,
        out_shape=[jax.ShapeDtypeStruct(dh.shape, F32), narrow, narrow,
                   jax.ShapeDtypeStruct((B_Q_HEADS, LANE), F32)] + c_outs,
        scratch_shapes=c_scratch,
        input_output_aliases={8: 0},
        compiler_params=_cp("arbitrary"))(h, h, h, h, h, h, dm, sinks_b, dh, *c_ins)
    return outs[:4], outs[4:]


def _matmul_tn(a, b, *, tk, tm, name):
    t, m = a.shape
    n = b.shape[1]

    def body(a_ref, b_ref, o_ref):
        @pl.when(pl.program_id(1) == 0)
        def _():
            o_ref[...] = jnp.zeros_like(o_ref)

        o_ref[...] += _dot_tn(a_ref[...], b_ref[...])

    return pl.pallas_call(
        body, name=name, grid=(m // tm, t // tk),
        in_specs=[pl.BlockSpec((tk, tm), lambda j, kk: (kk, j)), pl.BlockSpec((tk, n), lambda j, kk: (kk, 0))],
        out_specs=pl.BlockSpec((tm, n), lambda j, kk: (j, 0)),
        out_shape=jax.ShapeDtypeStruct((m, n), F32),
        compiler_params=_cp("parallel", "arbitrary"))(a, b)


def _in_proj_dx(dh_main, dh_tail, wt, dr, *, tm, name, carry=None):
    t, n_main = dh_main.shape
    n_tail = dh_tail.shape[1]
    c_ins, c_in_specs, c_out_specs, c_outs, c_scratch = _carry_specs(carry)

    def body(*refs):
        a_ref, t_ref, wa_ref, wt_ref, r_ref, o_ref = _carried(carry, refs, 5, 1, t // tm)
        o_ref[...] = _dot(a_ref[...], wa_ref[...]) + _dot(t_ref[...], wt_ref[...]) + DEEPNORM_ALPHA * r_ref[...]

    row = lambda w: pl.BlockSpec((tm, w), lambda i: (i, 0))
    outs = pl.pallas_call(
        body, name=name, grid=(t // tm,),
        in_specs=[row(n_main), row(n_tail), pl.BlockSpec((n_main, D_MODEL), lambda i: (0, 0)),
                  pl.BlockSpec((n_tail, D_MODEL), lambda i: (n_main // n_tail, 0)), row(D_MODEL)] + c_in_specs,
        out_specs=[row(D_MODEL)] + c_out_specs,
        out_shape=[jax.ShapeDtypeStruct((t, D_MODEL), F32)] + c_outs,
        scratch_shapes=c_scratch,
        compiler_params=_cp("arbitrary"))(dh_main, dh_tail, wt, wt, dr, *c_ins)
    return outs[0], outs[1:]


def _layer_bwd(dxn, res, wt, conv_w, par, sinks_b, norm_w, w_out_bf, ln_g, l, carry=None, carry_dx=None):
    dr, dm, dw_out, dln_g, dln_b = _ln_out_bwd(dxn, res["r"], res["mixed"], ln_g, w_out_bf, tm=256, name=f"ln_out_bwd_{l}")
    h = res["h"]
    do, dh, dnw = _dn_post_bwd(dm, res["oa"], h, norm_w, tm=512, name=f"dn_post_bwd_{l}")
    dvn, ds_all = _dn_scan_bwd(res["q"], res["k"], res["w"], res["qk"], res["bg"], do, name=f"dn_scan_bwd_{l}")
    dq, dk, dv, dbg, dbgt = _dn_chunk_bwd(res["q"], res["k"], res["v"], res["vn"], res["tmat"], res["qk"], res["bg"],
                                          res["bgt"], res["s_all"], ds_all, dvn, do, name=f"dn_chunk_bwd_{l}")
    dc, dbgi, dpar = _dn_pre_bwd(h, conv_w, par, dq, dk, dv, dbg, dbgt, tt=512, name=f"dn_pre_bwd_{l}")
    dh, dcw = _conv_bwd(dc, h, conv_w, dh, tt=512, name=f"conv_bwd_{l}")
    (dh, dkb, dvb, dsk), carried = _swa_bwd(h, dm, sinks_b, dh, name=f"swa_bwd_{l}", carry=carry)
    dh_tail = jnp.concatenate([dkb, dvb, dbgi], axis=1)
    dwt_main = _matmul_tn(dh, res["x"], tk=512, tm=768, name=f"in_proj_dw_{l}")
    dwt_tail = _matmul_tn(dh_tail, res["x"], tk=512, tm=P_COLS - DH_MAIN, name=f"in_proj_dw_tail_{l}")
    grads = dict(w_in=(dwt_main, dwt_tail), conv_w=dcw[:CONV_K], a_log=dpar[0, A_HEADS:2 * A_HEADS],
                 dt_bias=dpar[1, A_HEADS:2 * A_HEADS], norm_w=dnw[0], sinks=dsk[:, 0], w_out=dw_out,
                 ln_g=dln_g[0], ln_b=dln_b[0])
    dx, carried_dx = _in_proj_dx(dh, dh_tail, wt, dr, tm=256, name=f"in_proj_dx_{l}",
                                 carry=None if carry_dx is None else carry_dx(grads))
    return dx, grads, carried, carried_dx


def _layer_args(wt, conv_w, a_log, dt_bias, sinks, norm_w, w_out_bf):
    return (wt, conv_w, _gate_params(a_log, dt_bias), jnp.broadcast_to(sinks[:, None], (B_Q_HEADS, LANE)),
            norm_w[None], w_out_bf)


def _local_step(x, target, args0, args1, ln_g, ln_b, gather1=None, reduce1=None, reduce0=None):
    assert DEPTH == 2
    x1, res0, got = _layer_fwd(x, *args0, ln_g[0][None], ln_b[0][None], 0, carry=gather1)
    if gather1 is not None:
        args1 = args1(got)
    x2, res1, _ = _layer_fwd(x1, *args1, ln_g[1][None], ln_b[1][None], 1)
    dx, loss_tile = _loss_grad(x2, target, tm=512, name="loss_grad")
    dx, grads1, _, _ = _layer_bwd(dx, res1, *args1, ln_g[1][None], 1)
    carry = None if reduce1 is None else reduce1(grads1)
    carry_dx = None if reduce0 is None else (lambda grads0: reduce0(grads0, grads1))
    dx, grads0, landed1, landed0 = _layer_bwd(dx, res0, *args0, ln_g[0][None], 0, carry=carry, carry_dx=carry_dx)
    return loss_tile, dx, [grads0, grads1], landed1, landed0


_ANY = pl.BlockSpec(memory_space=pl.ANY)
_MESH = pl.DeviceIdType.MESH


HALF = D_MODEL // 2


class _Exchange:
    def __init__(self, ins, outs, n_remote, n_local, plan):
        self.ins, self.outs, self.n_remote, self.n_local, self.plan = tuple(ins), tuple(outs), n_remote, n_local, plan

    def scratch(self):
        return [pltpu.SemaphoreType.DMA((self.n_remote,)), pltpu.SemaphoreType.DMA((self.n_remote,)),
                pltpu.SemaphoreType.DMA((max(self.n_local, 1),))]

    def _copies(self, in_refs, out_refs, sems, arriving):
        send_sems, recv_sems, local_sems = sems
        local, sends, recvs = self.plan(in_refs, out_refs)
        loc = [pltpu.make_async_copy(s, d, local_sems.at[i]) for i, (s, d) in enumerate(local)]
        rem = [pltpu.make_async_remote_copy(src_ref=s, dst_ref=recvs[i] if arriving else d, send_sem=send_sems.at[i],
                                            recv_sem=recv_sems.at[i], device_id=peer, device_id_type=_MESH)
               for i, (s, d, peer) in enumerate(sends)]
        return loc, rem

    def start(self, in_refs, out_refs, sems):
        loc, rem = self._copies(in_refs, out_refs, sems, arriving=False)
        for cp in loc + rem:
            cp.start()

    def finish(self, in_refs, out_refs, sems):
        loc, rem = self._copies(in_refs, out_refs, sems, arriving=True)
        for cp in rem:
            cp.wait_recv()
        for cp in rem:
            cp.wait_send()
        for cp in loc:
            cp.wait()


def _run_exchange(ex, *, name):
    n_in, n_out = len(ex.ins), len(ex.outs)

    def body(*refs):
        parts = refs[:n_in], refs[n_in:n_in + n_out], refs[n_in + n_out:]
        ex.start(*parts)
        ex.finish(*parts)

    return pl.pallas_call(body, name=name, in_specs=[_ANY] * n_in, out_specs=[_ANY] * n_out, out_shape=list(ex.outs),
                          scratch_shapes=ex.scratch())(*ex.ins)


def _place():
    x, y, c = lax.axis_index("x"), lax.axis_index("y"), lax.axis_index("c")
    return x, y, c, [(1 - x, y), (x, 1 - y), (1 - x, 1 - y)]


def _gather_exchange(arrays):
    n = len(arrays)

    def plan(src, dst):
        x, y, c, chips = _place()
        me = 2 * x + y
        local = [(src[k], dst[k].at[me]) for k in range(n)]
        sends = [(src[k], dst[k].at[me], (px, py, c)) for k in range(n) for px, py in chips]
        recvs = [dst[k].at[2 * px + py] for k in range(n) for px, py in chips]
        return local, sends, recvs

    return _Exchange(arrays, [jax.ShapeDtypeStruct((N_SHARD,) + a.shape, a.dtype) for a in arrays], 3 * n, n, plan)


def _half(core):
    return pl.ds(pl.multiple_of(core * HALF, HALF), HALF)


def _reduce_scatter_exchange(g, small=None):
    ins = [g] if small is None else [g, small]
    outs = [jax.ShapeDtypeStruct((7,) + g.shape[1:2] + (HALF,), g.dtype)]
    if small is not None:
        outs.append(jax.ShapeDtypeStruct((8,) + small.shape, small.dtype))

    def plan(src, dst):
        x, y, c, chips = _place()
        me = 2 * x + y
        peers = [(px, py, c if t == 0 else 1 - c) for px, py in chips for t in (0, 1)] + [(x, y, 1 - c)]
        sends = [(src[0].at[2 * px + py, :, _half(pc)], dst[0].at[k], (px, py, pc)) for k, (px, py, pc) in enumerate(peers)]
        recvs = [dst[0].at[k] for k in range(7)]
        local = []
        if small is not None:
            mine = 4 * x + 2 * y + c
            local = [(src[1], dst[1].at[mine])]
            sends += [(src[1], dst[1].at[mine], peer) for peer in peers]
            recvs += [dst[1].at[4 * px + 2 * py + pc] for px, py, pc in peers]
        return local, sends, recvs

    return _Exchange(ins, outs, 7 * len(ins), len(ins) - 1, plan)


def _share_exchange(arrays):
    n = len(arrays)

    def plan(src, dst):
        x, y, c, _ = _place()
        return [], [(src[k], dst[k], (x, y, 1 - c)) for k in range(n)], [dst[k] for k in range(n)]

    return _Exchange(arrays, [jax.ShapeDtypeStruct(a.shape, a.dtype) for a in arrays], n, 0, plan)


def _sum_scatter(g, land, me, core, *, tc, name):
    rows = g.shape[1]
    per = HALF // tc

    def body(where_ref, g_ref, land_ref, o_ref):
        acc = g_ref[...]
        for k in range(7):
            acc = acc + land_ref[k].astype(F32)
        o_ref[...] = acc

    return pl.pallas_call(
        body, name=name, out_shape=jax.ShapeDtypeStruct((rows, HALF), F32), compiler_params=_cp("parallel"),
        grid_spec=pltpu.PrefetchScalarGridSpec(
            num_scalar_prefetch=1, grid=(per,),
            in_specs=[pl.BlockSpec((None, rows, tc), lambda i, w: (w[0], 0, w[1] * per + i)),
                      pl.BlockSpec((7, rows, tc), lambda i, w: (0, 0, i))],
            out_specs=pl.BlockSpec((rows, tc), lambda i, w: (0, i))))(
        jnp.stack([me, core]).astype(jnp.int32), g, land)


def _sum_slots(a, *, name):
    n = a.shape[0]

    def body(a_ref, o_ref):
        acc = a_ref[0]
        for k in range(1, n):
            acc = acc + a_ref[k]
        o_ref[...] = acc

    return pl.pallas_call(body, name=name, out_shape=jax.ShapeDtypeStruct(a.shape[1:], a.dtype))(a)


def _elementwise(fn, ins, n_out, block, *, name):
    shape = ins[0].shape
    grid = tuple(s // b for s, b in zip(shape, block))
    n_in = len(ins)

    def body(*refs):
        outs = fn(*[r[...] for r in refs[:n_in]])
        for o_ref, val in zip(refs[n_in:], outs):
            o_ref[...] = val

    spec = pl.BlockSpec(block, lambda i, j, k: (i, j, k))
    return pl.pallas_call(body, name=name, grid=grid, in_specs=[spec] * n_in, out_specs=[spec] * n_out,
                          out_shape=[jax.ShapeDtypeStruct(shape, F32)] * n_out,
                          compiler_params=_cp(*["parallel"] * 3))(*ins)


def _adamw_math(w, g, m, v):
    mn = ADAM_B1 * m + (1.0 - ADAM_B1) * g
    vn = ADAM_B2 * v + (1.0 - ADAM_B2) * (g * g)
    m_hat = mn / (1.0 - ADAM_B1 ** ADAM_STEP)
    v_hat = vn / (1.0 - ADAM_B2 ** ADAM_STEP)
    return -ADAM_LR * (m_hat / (jnp.sqrt(v_hat) + ADAM_EPS) + ADAM_WD * w), mn, vn


def _adamw(w, g, m, v, block, *, name):
    return _elementwise(_adamw_math, [w, g, m, v], 3, block, name=name)


def _to_kernel_order(wt):
    gates = jnp.pad(wt[2048:2056], ((0, LANE - 2 * A_HEADS), (0, 0)))
    return jnp.concatenate([wt[0:2048], wt[2056:2568], wt[2824:3336], wt[2568:2696], wt[2696:2824], gates], axis=0)


def _from_kernel_order(main, tail):
    return jnp.concatenate([main[0:2048], tail[C_BG - DH_MAIN:C_BG - DH_MAIN + 2 * A_HEADS],
                            main[C_QB:C_QB + B_WIDTH], tail[0:B_KV_WIDTH], tail[B_KV_WIDTH:2 * B_KV_WIDTH],
                            main[C_ZB:C_ZB + B_WIDTH]], axis=0)


def _gate_params(a_log, dt_bias):
    par = jnp.zeros((SUBLANE, LANE), F32)
    par = par.at[0, A_HEADS:2 * A_HEADS].set(a_log)
    return par.at[1, A_HEADS:2 * A_HEADS].set(dt_bias)


SMALL = ("conv_w", "a_log", "dt_bias", "norm_w", "sinks", "ln_g", "ln_b")


def _pack(parts, cols):
    flat = jnp.concatenate([p.reshape(-1) for p in parts])
    rows = -(-flat.shape[0] // cols)
    return jnp.pad(flat, (0, rows * cols - flat.shape[0])).reshape(rows, cols)


def _unpack(packed, shapes):
    flat = packed.reshape(-1)
    out, at = [], 0
    for s in shapes:
        n = math.prod(s)
        out.append(flat[at:at + n].reshape(s))
        at += n
    return out


def kernel(x, w_in, conv_w, a_log, dt_bias, norm_w, sinks, w_out, ln_g, ln_b, loss_target, m_w_in, m_conv_w, m_a_log, m_dt_bias, m_norm_w, m_sinks, m_w_out, m_ln_g, m_ln_b, v_w_in, v_conv_w, v_a_log, v_dt_bias, v_norm_w, v_sinks, v_w_out, v_ln_g, v_ln_b):
    xi, yi, ci = lax.axis_index("x"), lax.axis_index("y"), lax.axis_index("c")
    me = 2 * xi + yi

    to_t = lambda a: jnp.transpose(a, (2, 0, 1))
    from_t = lambda a: jnp.transpose(a, (1, 2, 0))

    wt_shard = to_t(w_in)

    def pack_weights(l):
        rows = jnp.pad(wt_shard[:, l], ((0, IN_PAD - IN_SHARD), (0, 0)))
        return jnp.concatenate([rows, w_out[l]], axis=0).astype(BF16)

    def unpack_weights(got):
        wt = _to_kernel_order(got[:, :IN_SHARD].reshape(IN_COLS, D_MODEL))
        return wt, got[:, IN_PAD:].reshape(D_MODEL, D_MODEL)

    got0, g_conv = _run_exchange(_gather_exchange([pack_weights(0), conv_w]), name="gather_weights_0")
    conv_full = jnp.moveaxis(g_conv, 0, 2).reshape(DEPTH, CONV_K, 3 * A_WIDTH)
    layer_args = lambda l, got: _layer_args(unpack_weights(got)[0], conv_full[l], a_log[l], dt_bias[l], sinks[l],
                                            norm_w[l], unpack_weights(got)[1])

    def pack_grads(g):
        gin = _from_kernel_order(*g["w_in"]).reshape(N_SHARD, IN_SHARD, D_MODEL)
        gin = jnp.pad(gin, ((0, 0), (0, IN_PAD - IN_SHARD), (0, 0)))
        return jnp.concatenate([gin, g["w_out"].reshape(N_SHARD, OUT_SHARD, D_MODEL)], axis=1).astype(BF16)

    packed = {}

    def reduce1(grads1):
        packed[1] = pack_grads(grads1)
        return _reduce_scatter_exchange(packed[1])

    def reduce0(grads0, grads1):
        packed[0] = pack_grads(grads0)
        gsmall = _pack([jnp.stack([g[nm] for g in (grads0, grads1)]) for nm in SMALL], D_MODEL)
        return _reduce_scatter_exchange(packed[0], gsmall)

    loss_tile, dx, grads, landed1, (landed0, landed_small) = _local_step(
        x[0], loss_target[0], layer_args(0, got0), lambda got: layer_args(1, got[0]), ln_g, ln_b,
        gather1=_gather_exchange([pack_weights(1)]), reduce1=reduce1, reduce0=reduce0)
    loss = lax.psum(loss_tile[0, 0], ("x", "y", "c"))

    small_shapes = [(DEPTH,) + grads[0][nm].shape for nm in SMALL]
    halves = [_sum_scatter(packed[l], land, me, ci, tc=2 * LANE, name=f"reduce_sum_{l}")
              for l, land in ((0, landed0), (1, landed1[0]))]
    s_small = _sum_slots(landed_small, name="reduce_sum_small")
    others = _run_exchange(_share_exchange(halves), name="pair_share")
    full = [jnp.where(ci == 0, jnp.concatenate([mine, other], axis=1), jnp.concatenate([other, mine], axis=1))
            for mine, other in zip(halves, others)]
    grad_in_t = jnp.stack([f[:IN_SHARD] for f in full], axis=1)
    grad_out = jnp.stack([f[IN_PAD:] for f in full])
    out_blk = (1, OUT_SHARD, D_MODEL)
    gs = dict(zip(SMALL, _unpack(s_small, small_shapes)))
    gs["conv_w"] = lax.dynamic_slice_in_dim(gs["conv_w"], me * CONV_SHARD, CONV_SHARD, axis=2)

    adam_in_blk = (IN_SHARD // 6, DEPTH, D_MODEL)
    d_in, nm_in, nv_in = (from_t(o) for o in _adamw(to_t(w_in), grad_in_t, to_t(m_w_in), to_t(v_w_in), adam_in_blk,
                                                    name="adamw_in"))
    d_out, nm_out, nv_out = _adamw(w_out, grad_out, m_w_out, v_w_out, out_blk, name="adamw_out")
    ws = dict(conv_w=conv_w, a_log=a_log, dt_bias=dt_bias, norm_w=norm_w, sinks=sinks, ln_g=ln_g, ln_b=ln_b)
    ms = dict(conv_w=m_conv_w, a_log=m_a_log, dt_bias=m_dt_bias, norm_w=m_norm_w, sinks=m_sinks, ln_g=m_ln_g, ln_b=m_ln_b)
    vs = dict(conv_w=v_conv_w, a_log=v_a_log, dt_bias=v_dt_bias, norm_w=v_norm_w, sinks=v_sinks, ln_g=v_ln_g, ln_b=v_ln_b)
    shard_shapes = [ws[nm].shape for nm in SMALL]
    packed = [_pack([d[nm] for nm in SMALL], LANE)[None] for d in (ws, gs, ms, vs)]
    d_s, nm_s, nv_s = (dict(zip(SMALL, _unpack(o, shard_shapes)))
                       for o in _adamw(*packed, packed[0].shape, name="adamw_small"))

    def in_order(big_in, small, big_out):
        return (big_in, small["conv_w"], small["a_log"], small["dt_bias"], small["norm_w"], small["sinks"], big_out,
                small["ln_g"], small["ln_b"])

    return (loss, dx[None], *in_order(from_t(grad_in_t), gs, grad_out), *in_order(d_in, d_s, d_out),
            *in_order(nm_in, nm_s, nm_out), *in_order(nv_in, nv_s, nv_out))
```

```python
import math

import jax
import jax.numpy as jnp
from jax import lax
from jax.experimental import pallas as pl
from jax.experimental.pallas import tpu as pltpu

F32 = jnp.float32
BF16 = jnp.bfloat16
HI = lax.Precision.HIGHEST

D_MODEL = 1024
DEPTH = 2
A_HEADS = 4
A_HEAD_DIM = 128
A_WIDTH = 512
CONV_K = 4
CHUNK = 64
B_Q_HEADS = 8
B_KV_HEADS = 2
B_HEAD_DIM = 64
B_GROUP = 4
B_WIDTH = 512
B_KV_WIDTH = 128
BLOCK = 128
IN_COLS = 3336
DEEPNORM_ALPHA = (2 * DEPTH) ** 0.25
LN_EPS = 1e-5
RMS_EPS = 1e-6
L2_EPS = 1e-6
ADAM_LR = 0.001
ADAM_B1 = 0.9
ADAM_B2 = 0.999
ADAM_EPS = 1e-08
ADAM_WD = 0.01
ADAM_STEP = 10

N_SHARD = 4
IN_SHARD = IN_COLS // N_SHARD
OUT_SHARD = D_MODEL // N_SHARD
CONV_SHARD = 3 * A_WIDTH // N_SHARD
IN_PAD = -(-IN_SHARD // 16) * 16
PACK_SPLIT = 432

P_COLS = 3456
C_PRE = 0
C_ZA = 1536
C_QB = 2048
C_ZB = 2560
C_KB = 3072
C_VB = 3200
C_BG = 3328
DH_MAIN = C_KB
LANE = 128
SUBLANE = 8
VMEM_LIMIT = 56 * 1024 * 1024
ALIBI = tuple(2.0 ** (-8.0 * (h + 1) / B_Q_HEADS) for h in range(B_Q_HEADS))
NEG = -1e30


def _cp(*sem):
    return pltpu.CompilerParams(dimension_semantics=sem, vmem_limit_bytes=VMEM_LIMIT)


def _dot(a, b):
    return jnp.dot(a.astype(BF16), b.astype(BF16), preferred_element_type=F32)


def _dot_nt(a, b):
    return lax.dot_general(a.astype(BF16), b.astype(BF16), (((1,), (1,)), ((), ())),
                           preferred_element_type=F32)


def _dot_tn(a, b):
    return lax.dot_general(a.astype(BF16), b.astype(BF16), (((0,), (0,)), ((), ())),
                           preferred_element_type=F32)


def _dot_hi(a, b):
    return jnp.dot(a, b, precision=HI, preferred_element_type=F32)


def _sigmoid(x):
    return jax.nn.sigmoid(x)


def _silu(x):
    return x * _sigmoid(x)


def _dsilu(x):
    s = _sigmoid(x)
    return s * (1.0 + x * (1.0 - s))


def _softplus(x):
    return jnp.maximum(x, 0.0) + jnp.log(1.0 + jnp.exp(-jnp.abs(x)))


def _shift_down(cur, before, s):
    if s == 0:
        return cur
    r = pltpu.roll(cur, s, 0)
    rb = pltpu.roll(before, s, 0)
    row = lax.broadcasted_iota(jnp.int32, before.shape, 0)
    head = jnp.where(row < s, rb, r[0:SUBLANE])
    return jnp.concatenate([head, r[SUBLANE:]], axis=0)


def _shift_up(cur, after, s):
    if s == 0:
        return cur
    n = cur.shape[0]
    r = pltpu.roll(cur, n - s, 0)
    ra = pltpu.roll(after, SUBLANE - s, 0)
    row = lax.broadcasted_iota(jnp.int32, after.shape, 0)
    tail = jnp.where(row >= SUBLANE - s, ra, r[n - SUBLANE:])
    return jnp.concatenate([r[:n - SUBLANE], tail], axis=0)


def _conv_fwd(cur, before, w):
    acc = cur * w[CONV_K - 1:CONV_K, :]
    for s in range(1, CONV_K):
        acc = acc + _shift_down(cur, before, s) * w[CONV_K - 1 - s:CONV_K - s, :]
    return acc


def _matmul_nt(a, bt, *, tm, name, carry=None):
    m, k = a.shape
    n = bt.shape[0]
    c_ins, c_in_specs, c_out_specs, c_outs, c_scratch = _carry_specs(carry)

    def body(*refs):
        a_ref, b_ref, o_ref = _carried(carry, refs, 2, 1, m // tm)
        o_ref[...] = _dot_nt(a_ref[...], b_ref[...])

    outs = pl.pallas_call(
        body, name=name, grid=(m // tm,),
        in_specs=[pl.BlockSpec((tm, k), lambda i: (i, 0)), pl.BlockSpec((n, k), lambda i: (0, 0))] + c_in_specs,
        out_specs=[pl.BlockSpec((tm, n), lambda i: (i, 0))] + c_out_specs,
        out_shape=[jax.ShapeDtypeStruct((m, n), F32)] + c_outs,
        scratch_shapes=c_scratch,
        compiler_params=_cp("arbitrary"))(a, bt, *c_ins)
    return outs[0], outs[1:]


def _dn_pre(h, conv_w, par, *, tt, name):
    t = h.shape[0]
    cw = 3 * A_WIDTH
    hb = tt // SUBLANE

    def body(pre_ref, halo_ref, bgi_ref, cw_ref, par_ref, q_ref, k_ref, v_ref, bg_ref, bgt_ref):
        i = pl.program_id(0)
        cur = pre_ref[...]
        before = jnp.where(i > 0, halo_ref[...], 0.0)
        s = _silu(_conv_fwd(cur, before, cw_ref[...]))
        for hd in range(A_HEADS):
            sl = slice(hd * LANE, (hd + 1) * LANE)
            tq = s[:, hd * LANE:(hd + 1) * LANE]
            q_ref[:, sl] = tq * (lax.rsqrt(jnp.sum(tq * tq, -1, keepdims=True) + L2_EPS) * (A_HEAD_DIM ** -0.5))
            tk = s[:, A_WIDTH + hd * LANE:A_WIDTH + (hd + 1) * LANE]
            k_ref[:, sl] = tk * lax.rsqrt(jnp.sum(tk * tk, -1, keepdims=True) + L2_EPS)
        v_ref[...] = s[:, 2 * A_WIDTH:]
        raw = bgi_ref[...]
        lane = lax.broadcasted_iota(jnp.int32, raw.shape, 1)
        is_a = (lane >= A_HEADS) & (lane < 2 * A_HEADS)
        g = jnp.where(is_a, -jnp.exp(par_ref[0:1, :]) * _softplus(raw + par_ref[1:2, :]), 0.0)
        gc = _dot_hi(_chunk_tri(tt, lower=True), g)
        bg = jnp.where(lane < A_HEADS, _sigmoid(raw), gc)
        bg_ref[...] = bg
        bgt_ref[...] = jnp.transpose(bg)[0:SUBLANE, :]

    wide = jax.ShapeDtypeStruct((t, A_WIDTH), F32)
    return pl.pallas_call(
        body, name=name, grid=(t // tt,),
        in_specs=[pl.BlockSpec((tt, cw), lambda i: (i, 0)),
                  pl.BlockSpec((SUBLANE, cw), lambda i: (jnp.maximum(i * hb - 1, 0), 0)),
                  pl.BlockSpec((tt, LANE), lambda i: (i, C_BG // LANE)),
                  pl.BlockSpec((CONV_K, cw), lambda i: (0, 0)),
                  pl.BlockSpec((SUBLANE, LANE), lambda i: (0, 0))],
        out_specs=[pl.BlockSpec((tt, A_WIDTH), lambda i: (i, 0))] * 3
        + [pl.BlockSpec((tt, LANE), lambda i: (i, 0)), pl.BlockSpec((SUBLANE, tt), lambda i: (0, i))],
        out_shape=[wide, wide, wide, jax.ShapeDtypeStruct((t, LANE), F32), jax.ShapeDtypeStruct((SUBLANE, t), F32)],
        compiler_params=_cp("parallel"))(h, h, h, conv_w, par)


def _chunk_tri(n, lower):
    r = lax.broadcasted_iota(jnp.int32, (n, n), 0)
    c = lax.broadcasted_iota(jnp.int32, (n, n), 1)
    shift = CHUNK.bit_length() - 1
    same = jnp.right_shift(r, shift) == jnp.right_shift(c, shift)
    return (same & ((c <= r) if lower else (c >= r))).astype(F32)


def _chunk_masks():
    r = lax.broadcasted_iota(jnp.int32, (CHUNK, CHUNK), 0)
    c = lax.broadcasted_iota(jnp.int32, (CHUNK, CHUNK), 1)
    return r >= c, r > c, r == c


def _split(a):
    hi = a.astype(BF16)
    return hi, (a - hi.astype(F32)).astype(BF16)


def _dot3(a, b):
    (ah, al), (bh, bl) = a, b
    d = lambda p, q: jnp.dot(p, q, preferred_element_type=F32)
    return d(ah, bh) + (d(ah, bl) + d(al, bh))


def _tri_inv_many(a_list, eye):
    d = lambda p, q: jnp.dot(p, q, preferred_element_type=F32)
    p = [(-a).astype(BF16) for a in a_list]
    tm = [eye - a for a in a_list]
    for _ in range(5):
        pf = [d(pi, pi) for pi in p]
        p = [x.astype(BF16) for x in pf]
        tm = [t + d(t.astype(BF16), pi) for t, pi in zip(tm, p)]
    ms = [_split(eye + a) for a in a_list]
    res = [eye - _dot3(m, _split(t)) for m, t in zip(ms, tm)]
    return [t + d(t.astype(BF16), r.astype(BF16)) for t, r in zip(tm, res)]


def _chunk_gates(bg_v, bgt_v, hd):
    return (bg_v[:, hd:hd + 1], bg_v[:, A_HEADS + hd:A_HEADS + hd + 1],
            None if bgt_v is None else bgt_v[A_HEADS + hd:A_HEADS + hd + 1, :])


WY_ROWS = 256
SCAN_ROWS = 128
WY_GROUP = 2


def _dn_wy(q, k, v, bg, bgt, *, name, carry=None):
    t = q.shape[0]
    rows = WY_ROWS

    c_ins, c_in_specs, c_out_specs, c_outs, c_scratch = _carry_specs(carry)

    def body(*refs):
        q_ref, k_ref, v_ref, bg_ref, bgt_ref, u_ref, w_ref, tm_ref, qk_ref = _carried(carry, refs, 5, 4, t // rows)
        causal, strict, diag = _chunk_masks()
        eye = diag.astype(F32)
        for c0 in range(0, rows // CHUNK, WY_GROUP):
            items = [(c, hd) for c in range(c0, c0 + WY_GROUP) for hd in range(A_HEADS)]
            rs = lambda c: slice(c * CHUNK, (c + 1) * CHUNK)
            sl = lambda hd: slice(hd * LANE, (hd + 1) * LANE)
            hs = lambda hd: slice(hd * CHUNK, (hd + 1) * CHUNK)
            gates = [_chunk_gates(bg_ref[rs(c), :], bgt_ref[:, rs(c)], hd) for c, hd in items]
            dms = [jnp.exp(jnp.where(causal, gcol - grow, NEG)) for _, gcol, grow in gates]
            kbs = [k_ref[rs(c), sl(hd)] * g[0] for (c, hd), g in zip(items, gates)]
            a_list = [jnp.where(strict, _dot_nt(kb, k_ref[rs(c), sl(hd)]) * dm, 0.0)
                      for (c, hd), kb, dm in zip(items, kbs, dms)]
            for (c, hd), dm in zip(items, dms):
                qk_ref[rs(c), hs(hd)] = jnp.where(
                    causal, _dot_nt(q_ref[rs(c), sl(hd)], k_ref[rs(c), sl(hd)]) * dm, 0.0)
            tms = _tri_inv_many(a_list, eye)
            for (c, hd), g, kb, tmat in zip(items, gates, kbs, tms):
                tm_ref[rs(c), hs(hd)] = tmat
                u_ref[rs(c), sl(hd)] = _dot(tmat, v_ref[rs(c), sl(hd)] * g[0])
                w_ref[rs(c), sl(hd)] = _dot(tmat, kb * jnp.exp(g[1])).astype(BF16)

    blk = pl.BlockSpec((rows, A_WIDTH), lambda i: (i, 0))
    half = pl.BlockSpec((rows, A_HEADS * CHUNK), lambda i: (i, 0))
    outs = pl.pallas_call(
        body, name=name, grid=(t // rows,),
        in_specs=[blk, blk, blk, pl.BlockSpec((rows, LANE), lambda i: (i, 0)),
                  pl.BlockSpec((SUBLANE, rows), lambda i: (0, i))] + c_in_specs,
        out_specs=[blk, blk, half, half] + c_out_specs,
        out_shape=[jax.ShapeDtypeStruct((t, A_WIDTH), F32), jax.ShapeDtypeStruct((t, A_WIDTH), BF16),
                   jax.ShapeDtypeStruct((t, A_HEADS * CHUNK), F32),
                   jax.ShapeDtypeStruct((t, A_HEADS * CHUNK), F32)] + c_outs,
        scratch_shapes=c_scratch,
        compiler_params=_cp("arbitrary"))(q, k, v, bg, bgt, *c_ins)
    return outs[:4], outs[4:]


def _dn_scan_fwd(q, k, u, w, qk, bg, *, name):
    t = q.shape[0]
    rows = SCAN_ROWS
    per = rows // CHUNK

    def body(q_ref, k_ref, u_ref, w_ref, qk_ref, bg_ref, o_ref, vn_ref, s_ref, state):
        @pl.when(pl.program_id(0) == 0)
        def _():
            state[...] = jnp.zeros_like(state)

        heads = range(A_HEADS)
        sl = lambda hd: slice(hd * LANE, (hd + 1) * LANE)
        s_cur = [state[hd] for hd in heads]
        for c in range(per):
            rs = slice(c * CHUNK, (c + 1) * CHUNK)
            bg_v = bg_ref[rs, :]
            gcols = [_chunk_gates(bg_v, None, hd)[1] for hd in heads]
            glasts = [gc[CHUNK - 1:CHUNK, :] for gc in gcols]
            for hd in heads:
                s_ref[c, hd] = s_cur[hd]
            vns = [u_ref[rs, sl(hd)] - _dot(w_ref[rs, sl(hd)], s_cur[hd]) for hd in heads]
            qss = [_dot(q_ref[rs, sl(hd)] * jnp.exp(gcols[hd]), s_cur[hd]) for hd in heads]
            s_cur = [s_cur[hd] * jnp.exp(glasts[hd])
                     + _dot_tn(k_ref[rs, sl(hd)] * jnp.exp(glasts[hd] - gcols[hd]), vns[hd]) for hd in heads]
            for hd in heads:
                vn_ref[rs, sl(hd)] = vns[hd]
                o_ref[rs, sl(hd)] = qss[hd] + _dot(qk_ref[rs, hd * CHUNK:(hd + 1) * CHUNK], vns[hd])
        for hd in heads:
            state[hd] = s_cur[hd]

    blk = pl.BlockSpec((rows, A_WIDTH), lambda i: (i, 0))
    half = pl.BlockSpec((rows, A_HEADS * CHUNK), lambda i: (i, 0))
    wide = jax.ShapeDtypeStruct((t, A_WIDTH), F32)
    return pl.pallas_call(
        body, name=name, grid=(t // rows,),
        in_specs=[blk, blk, blk, blk, half, pl.BlockSpec((rows, LANE), lambda i: (i, 0))],
        out_specs=[blk, blk, pl.BlockSpec((per, A_HEADS, LANE, LANE), lambda i: (i, 0, 0, 0))],
        out_shape=[wide, wide, jax.ShapeDtypeStruct((t // CHUNK, A_HEADS, LANE, LANE), F32)],
        scratch_shapes=[pltpu.VMEM((A_HEADS, LANE, LANE), F32)],
        compiler_params=_cp("arbitrary"))(q, k, u, w, qk, bg)


def _swa_neg_dist(n_blk):
    qi = lax.broadcasted_iota(jnp.int32, (BLOCK, 2 * BLOCK), 0)
    si = lax.broadcasted_iota(jnp.int32, (BLOCK, 2 * BLOCK), 1)
    dist = qi + BLOCK - si
    mask = (dist >= 0) & (dist < BLOCK) & ((si >= BLOCK) | (n_blk > 0))
    return jnp.where(mask, -dist.astype(F32), NEG)


def _stack_heads(ref, hk):
    return jnp.concatenate([ref[:, h * B_HEAD_DIM:(h + 1) * B_HEAD_DIM]
                            for h in range(hk * B_GROUP, (hk + 1) * B_GROUP)], axis=0)


def _swa_group_probs(q_ref, sk_ref, kband, vband, neg_dist):
    hks = range(B_KV_HEADS)
    heads = lambda hk: range(hk * B_GROUP, (hk + 1) * B_GROUP)
    ksl = lambda hk: slice(hk * B_HEAD_DIM, (hk + 1) * B_HEAD_DIM)
    ones = jnp.ones((2 * BLOCK, B_HEAD_DIM), BF16)
    qs = [_stack_heads(q_ref, hk) * (B_HEAD_DIM ** -0.5) for hk in hks]
    sink = [jnp.concatenate([jnp.broadcast_to(sk_ref[h:h + 1, 0:1], (BLOCK, 1)) for h in heads(hk)], axis=0)
            for hk in hks]
    s = [_dot_nt(qs[hk], kband[:, ksl(hk)]) + jnp.concatenate([ALIBI[h] * neg_dist for h in heads(hk)], axis=0)
         for hk in hks]
    m = [jnp.maximum(jnp.max(s[hk], axis=-1, keepdims=True), sink[hk]) for hk in hks]
    p = [jnp.exp(s[hk] - m[hk]) for hk in hks]
    oe = [jnp.dot(p[hk].astype(BF16), jnp.concatenate([vband[:, ksl(hk)].astype(BF16), ones], axis=1),
                  preferred_element_type=F32) for hk in hks]
    ps = [jnp.exp(sink[hk] - m[hk]) for hk in hks]
    inv = [1.0 / (oe[hk][:, B_HEAD_DIM:B_HEAD_DIM + 1] + ps[hk]) for hk in hks]
    return [(qs[hk], p[hk] * inv[hk], ps[hk] * inv[hk], oe[hk][:, :B_HEAD_DIM] * inv[hk]) for hk in hks]


def _swa_specs():
    qspec = lambda c0: pl.BlockSpec((BLOCK, B_WIDTH), lambda i: (i, c0 // B_WIDTH))
    cur = lambda c0: pl.BlockSpec((BLOCK, LANE), lambda i: (i, c0 // LANE))
    prev = lambda c0: pl.BlockSpec((BLOCK, LANE), lambda i: (jnp.maximum(i - 1, 0), c0 // LANE))
    return qspec, cur, prev


def _carried(carry, refs, n_in, n_out, steps):
    if carry is None:
        return refs
    ci, co = len(carry.ins), len(carry.outs)
    own = refs[:n_in] + refs[n_in + ci:n_in + ci + n_out] + refs[n_in + ci + n_out + co:len(refs) - 3]
    parts = refs[n_in:n_in + ci], refs[n_in + ci + n_out:n_in + ci + n_out + co], refs[len(refs) - 3:]

    @pl.when(pl.program_id(0) == 0)
    def _():
        carry.start(*parts)

    @pl.when(pl.program_id(0) == steps - 1)
    def _():
        carry.finish(*parts)

    return own


def _carry_specs(carry):
    if carry is None:
        return [], [], [], [], []
    return (list(carry.ins), [_ANY] * len(carry.ins), [_ANY] * len(carry.outs), list(carry.outs), carry.scratch())


def _swa_fwd(h, sinks_b, *, name, carry=None):
    t = h.shape[0]
    qspec, cur, prev = _swa_specs()
    c_ins, c_in_specs, c_out_specs, c_outs, c_scratch = _carry_specs(carry)

    def body(*refs):
        q_ref, kc_ref, kp_ref, vc_ref, vp_ref, sk_ref, o_ref = _carried(carry, refs, 6, 1, t // BLOCK)
        n_blk = pl.program_id(0)
        kband = jnp.concatenate([kp_ref[...], kc_ref[...]], axis=0)
        vband = jnp.concatenate([vp_ref[...], vc_ref[...]], axis=0)
        groups = _swa_group_probs(q_ref, sk_ref, kband, vband, _swa_neg_dist(n_blk))
        for hk, (_, _, _, o) in enumerate(groups):
            for g in range(B_GROUP):
                hq = hk * B_GROUP + g
                o_ref[:, hq * B_HEAD_DIM:(hq + 1) * B_HEAD_DIM] = o[g * BLOCK:(g + 1) * BLOCK]

    outs = pl.pallas_call(
        body, name=name, grid=(t // BLOCK,),
        in_specs=[qspec(C_QB), cur(C_KB), prev(C_KB), cur(C_VB), prev(C_VB),
                  pl.BlockSpec((B_Q_HEADS, LANE), lambda i: (0, 0))] + c_in_specs,
        out_specs=[pl.BlockSpec((BLOCK, B_WIDTH), lambda i: (i, 0))] + c_out_specs,
        out_shape=[jax.ShapeDtypeStruct((t, B_WIDTH), F32)] + c_outs,
        scratch_shapes=c_scratch,
        compiler_params=_cp("arbitrary"))(h, h, h, h, h, sinks_b, *c_ins)
    return outs[0], outs[1:]


def _rms_gate(o, za, nw):
    outs = []
    for hd in range(A_HEADS):
        oh = o[:, hd * LANE:(hd + 1) * LANE]
        r = lax.rsqrt(jnp.mean(oh * oh, -1, keepdims=True) + RMS_EPS)
        outs.append(oh * r * nw)
    return jnp.concatenate(outs, axis=1) * _silu(za)


def _out_ln(x, oa, ob, h, norm_w, w_out, ln_g, ln_b, *, tm, name):
    t = x.shape[0]

    def body(x_ref, oa_ref, ob_ref, za_ref, zb_ref, nw_ref, w_ref, g_ref, b_ref, xn_ref, mx_ref, r_ref):
        ya = _rms_gate(oa_ref[...], za_ref[...], nw_ref[...])
        yb = ob_ref[...] * _silu(zb_ref[...])
        mixed = jnp.concatenate([ya, yb], axis=1).astype(BF16)
        mx_ref[...] = mixed
        r = DEEPNORM_ALPHA * x_ref[...] + jnp.dot(mixed, w_ref[...], preferred_element_type=F32)
        r_ref[...] = r
        mu = jnp.mean(r, -1, keepdims=True)
        xc = r - mu
        var = jnp.mean(xc * xc, -1, keepdims=True)
        xn_ref[...] = xc * lax.rsqrt(var + LN_EPS) * g_ref[...] + b_ref[...]

    row = lambda w, c: pl.BlockSpec((tm, w), lambda i: (i, c))
    full = lambda a, b: pl.BlockSpec((a, b), lambda i: (0, 0))
    return pl.pallas_call(
        body, name=name, grid=(t // tm,),
        in_specs=[row(D_MODEL, 0), row(A_WIDTH, 0), row(B_WIDTH, 0), row(A_WIDTH, C_ZA // A_WIDTH),
                  row(B_WIDTH, C_ZB // B_WIDTH), full(1, LANE), full(D_MODEL, D_MODEL), full(1, D_MODEL), full(1, D_MODEL)],
        out_specs=[row(D_MODEL, 0), row(D_MODEL, 0), row(D_MODEL, 0)],
        out_shape=[jax.ShapeDtypeStruct((t, D_MODEL), F32), jax.ShapeDtypeStruct((t, D_MODEL), BF16),
                   jax.ShapeDtypeStruct((t, D_MODEL), F32)],
        compiler_params=_cp("parallel"))(x, oa, ob, h, h, norm_w, w_out, ln_g, ln_b)


def _layer_fwd(x, wt, conv_w, par, sinks_b, norm_w, w_out_bf, ln_g, ln_b, l, carries=None):
    carries = carries or {}
    h, got_in = _matmul_nt(x, wt, tm=512, name=f"in_proj_{l}", carry=carries.get("in_proj"))
    if callable(w_out_bf):
        w_out_bf = w_out_bf(got_in)
    q, k, v, bg, bgt = _dn_pre(h, conv_w, par, tt=512, name=f"dn_pre_{l}")
    (u, w, tmat, qk), got_wy = _dn_wy(q, k, v, bg, bgt, name=f"dn_wy_{l}", carry=carries.get("dn_wy"))
    oa, vn, s_all = _dn_scan_fwd(q, k, u, w, qk, bg, name=f"dn_scan_{l}")
    ob, got_swa = _swa_fwd(h, sinks_b, name=f"swa_fwd_{l}", carry=carries.get("swa"))
    xn, mixed, r = _out_ln(x, oa, ob, h, norm_w, w_out_bf, ln_g, ln_b, tm=256, name=f"out_ln_{l}")
    res = dict(x=x, h=h, q=q, k=k, v=v, bg=bg, bgt=bgt, w=w, tmat=tmat, qk=qk, vn=vn, oa=oa, s_all=s_all,
               mixed=mixed, r=r, w_out=w_out_bf)
    return xn, res, dict(in_proj=got_in, dn_wy=got_wy, swa=got_swa)


def _loss_grad(xn, target, *, tm, name):
    t = xn.shape[0]

    def body(x_ref, t_ref, d_ref, l_ref):
        @pl.when(pl.program_id(0) == 0)
        def _():
            l_ref[...] = jnp.zeros_like(l_ref)

        err = x_ref[...] - t_ref[...]
        d_ref[...] = err * (1.0 / D_MODEL)
        l_ref[...] += 0.5 / D_MODEL * jnp.sum(err * err)

    row = pl.BlockSpec((tm, D_MODEL), lambda i: (i, 0))
    return pl.pallas_call(
        body, name=name, grid=(t // tm,), in_specs=[row, row],
        out_specs=[row, pl.BlockSpec((SUBLANE, LANE), lambda i: (0, 0))],
        out_shape=[jax.ShapeDtypeStruct((t, D_MODEL), F32), jax.ShapeDtypeStruct((SUBLANE, LANE), F32)],
        compiler_params=_cp("arbitrary"))(xn, target)


def _ln_out_bwd(dxn, r, mixed, ln_g, w_out, *, tm, name):
    t = dxn.shape[0]

    def body(dxn_ref, r_ref, mx_ref, g_ref, w_ref, dr_ref, dm_ref, dw_ref, dg_ref, db_ref):
        @pl.when(pl.program_id(0) == 0)
        def _():
            dw_ref[...] = jnp.zeros_like(dw_ref)
            dg_ref[...] = jnp.zeros_like(dg_ref)
            db_ref[...] = jnp.zeros_like(db_ref)

        rr = r_ref[...]
        xc = rr - jnp.mean(rr, -1, keepdims=True)
        rstd = lax.rsqrt(jnp.mean(xc * xc, -1, keepdims=True) + LN_EPS)
        xhat = xc * rstd
        dxn_v = dxn_ref[...]
        dxh = dxn_v * g_ref[...]
        dr = rstd * (dxh - jnp.mean(dxh, -1, keepdims=True) - xhat * jnp.mean(dxh * xhat, -1, keepdims=True))
        dr_ref[...] = dr
        dg_ref[...] += jnp.sum(dxn_v * xhat, axis=0, keepdims=True)
        db_ref[...] += jnp.sum(dxn_v, axis=0, keepdims=True)
        drb = dr.astype(BF16)
        dm_ref[...] = _dot_nt(drb, w_ref[...])
        dw_ref[...] += _dot_tn(mx_ref[...], drb)

    row = pl.BlockSpec((tm, D_MODEL), lambda i: (i, 0))
    full = lambda a, b: pl.BlockSpec((a, b), lambda i: (0, 0))
    big = jax.ShapeDtypeStruct((t, D_MODEL), F32)
    vec = jax.ShapeDtypeStruct((1, D_MODEL), F32)
    return pl.pallas_call(
        body, name=name, grid=(t // tm,),
        in_specs=[row, row, row, full(1, D_MODEL), full(D_MODEL, D_MODEL)],
        out_specs=[row, row, full(D_MODEL, D_MODEL), full(1, D_MODEL), full(1, D_MODEL)],
        out_shape=[big, big, jax.ShapeDtypeStruct((D_MODEL, D_MODEL), F32), vec, vec],
        compiler_params=_cp("arbitrary"))(dxn, r, mixed, ln_g, w_out)


def _dn_post_bwd(dm, oa, h, norm_w, *, tm, name):
    t = oa.shape[0]

    def body(dy_ref, o_ref, za_ref, nw_ref, do_ref, dza_ref, dnw_ref):
        @pl.when(pl.program_id(0) == 0)
        def _():
            dnw_ref[...] = jnp.zeros_like(dnw_ref)

        nw = nw_ref[...]
        dnw = jnp.zeros_like(nw)
        for hd in range(A_HEADS):
            sl = slice(hd * LANE, (hd + 1) * LANE)
            oh, za, dy = o_ref[:, sl], za_ref[:, sl], dy_ref[:, sl]
            rs = lax.rsqrt(jnp.mean(oh * oh, -1, keepdims=True) + RMS_EPS)
            nrm = oh * rs
            dza_ref[:, sl] = dy * nrm * nw * _dsilu(za)
            dn = dy * _silu(za)
            dnw = dnw + jnp.sum(dn * nrm, axis=0, keepdims=True)
            dnn = dn * nw
            do_ref[:, sl] = rs * dnn - oh * (rs * rs * rs) * jnp.mean(dnn * oh, -1, keepdims=True)
        dnw_ref[...] += dnw

    row = lambda c: pl.BlockSpec((tm, A_WIDTH), lambda i: (i, c))
    wide = jax.ShapeDtypeStruct((t, A_WIDTH), F32)
    return pl.pallas_call(
        body, name=name, grid=(t // tm,),
        in_specs=[row(0), row(0), row(C_ZA // A_WIDTH), pl.BlockSpec((1, LANE), lambda i: (0, 0))],
        out_specs=[row(0), row(C_ZA // A_WIDTH), pl.BlockSpec((1, LANE), lambda i: (0, 0))],
        out_shape=[wide, jax.ShapeDtypeStruct((t, DH_MAIN), F32), jax.ShapeDtypeStruct((1, LANE), F32)],
        compiler_params=_cp("arbitrary"))(dm, oa, h, norm_w)


def _dn_scan_bwd(q, k, w, qk, bg, do, *, name):
    t = q.shape[0]
    rows = SCAN_ROWS
    per = rows // CHUNK
    n = t // rows

    def body(q_ref, k_ref, w_ref, qk_ref, bg_ref, do_ref, dvn_ref, ds_ref, dstate):
        @pl.when(pl.program_id(0) == 0)
        def _():
            dstate[...] = jnp.zeros_like(dstate)

        heads = range(A_HEADS)
        sl = lambda hd: slice(hd * LANE, (hd + 1) * LANE)
        ds_cur = [dstate[hd] for hd in heads]
        for c in reversed(range(per)):
            rs = slice(c * CHUNK, (c + 1) * CHUNK)
            bg_v = bg_ref[rs, :]
            gcols = [_chunk_gates(bg_v, None, hd)[1] for hd in heads]
            glasts = [gc[CHUNK - 1:CHUNK, :] for gc in gcols]
            for hd in heads:
                ds_ref[c, hd] = ds_cur[hd]
            pdo = [_dot_tn(qk_ref[rs, hd * CHUNK:(hd + 1) * CHUNK], do_ref[rs, sl(hd)]) for hd in heads]
            qdo = [_dot_tn(q_ref[rs, sl(hd)] * jnp.exp(gcols[hd]), do_ref[rs, sl(hd)]) for hd in heads]
            dvns = [pdo[hd] + _dot(k_ref[rs, sl(hd)] * jnp.exp(glasts[hd] - gcols[hd]), ds_cur[hd]) for hd in heads]
            ds_cur = [qdo[hd] + jnp.exp(glasts[hd]) * ds_cur[hd] - _dot_tn(w_ref[rs, sl(hd)], dvns[hd])
                      for hd in heads]
            for hd in heads:
                dvn_ref[rs, sl(hd)] = dvns[hd]
        for hd in heads:
            dstate[hd] = ds_cur[hd]

    blk = pl.BlockSpec((rows, A_WIDTH), lambda i: (n - 1 - i, 0))
    return pl.pallas_call(
        body, name=name, grid=(n,),
        in_specs=[blk, blk, blk, pl.BlockSpec((rows, A_HEADS * CHUNK), lambda i: (n - 1 - i, 0)),
                  pl.BlockSpec((rows, LANE), lambda i: (n - 1 - i, 0)), blk],
        out_specs=[blk, pl.BlockSpec((per, A_HEADS, LANE, LANE), lambda i: (n - 1 - i, 0, 0, 0))],
        out_shape=[jax.ShapeDtypeStruct((t, A_WIDTH), F32),
                   jax.ShapeDtypeStruct((t // CHUNK, A_HEADS, LANE, LANE), F32)],
        scratch_shapes=[pltpu.VMEM((A_HEADS, LANE, LANE), F32)],
        compiler_params=_cp("arbitrary"))(q, k, w, qk, bg, do)


def _dn_chunk_bwd(q, k, v, vn, tmat, qk, bg, bgt, s_all, ds_all, dvn, do, *, name):
    t = q.shape[0]
    rows = WY_ROWS
    per = rows // CHUNK

    def body(q_ref, k_ref, v_ref, vn_ref, tm_ref, qk_ref, bg_ref, bgt_ref, s_ref, ds_ref, dvn_ref, do_ref,
             dq_ref, dk_ref, dv_ref, dbg_ref, dbgt_ref):
        causal, strict, _ = _chunk_masks()
        lane = lax.broadcasted_iota(jnp.int32, (CHUNK, LANE), 1)
        rowi = lax.broadcasted_iota(jnp.int32, (CHUNK, 1), 0)
        sub = lax.broadcasted_iota(jnp.int32, (SUBLANE, CHUNK), 0)
        rs = lambda c: slice(c * CHUNK, (c + 1) * CHUNK)
        sl = lambda hd: slice(hd * LANE, (hd + 1) * LANE)
        hs = lambda hd: slice(hd * CHUNK, (hd + 1) * CHUNK)
        for c0 in range(0, per, WY_GROUP):
            items = [(c, hd) for c in range(c0, c0 + WY_GROUP) for hd in range(A_HEADS)]
            at = lambda ref: [ref[rs(c), sl(hd)] for c, hd in items]
            qs, ks, vs, dos, vns, dvns = at(q_ref), at(k_ref), at(v_ref), at(do_ref), at(vn_ref), at(dvn_ref)
            tmhs = [tm_ref[rs(c), hs(hd)] for c, hd in items]
            ps = [qk_ref[rs(c), hs(hd)] for c, hd in items]
            gates = [_chunk_gates(bg_ref[rs(c), :], bgt_ref[:, rs(c)], hd) for c, hd in items]
            betas = [g[0] for g in gates]
            gcols = [g[1] for g in gates]
            dmats = [jnp.exp(jnp.where(causal, g[1] - g[2], NEG)) for g in gates]
            es = [jnp.exp(gc) for gc in gcols]
            glasts = [gc[CHUNK - 1:CHUNK, :] for gc in gcols]
            eks = [jnp.exp(gl - gc) for gl, gc in zip(glasts, gcols)]
            kbs = [kh * b for kh, b in zip(ks, betas)]
            vbs = [vh * b for vh, b in zip(vs, betas)]
            kbes = [kb * e for kb, e in zip(kbs, es)]

            a_s = [jnp.where(strict, _dot_nt(kb, kh) * dm, 0.0) for kb, kh, dm in zip(kbs, ks, dmats)]
            dps = [jnp.where(causal, _dot_nt(doh, vnh), 0.0) for doh, vnh in zip(dos, vns)]
            dqds = [_dot_nt(doh, s_ref[c, hd]) for doh, (c, hd) in zip(dos, items)]
            dkds = [_dot_nt(vnh, ds_ref[c, hd]) for vnh, (c, hd) in zip(vns, items)]
            dws = [-_dot_nt(dvnh, s_ref[c, hd]) for dvnh, (c, hd) in zip(dvns, items)]
            dvbs = [_dot_tn(tmh, dvnh) for tmh, dvnh in zip(tmhs, dvns)]
            dgts = [jnp.sum(s_ref[c, hd] * ds_ref[c, hd], keepdims=True) for c, hd in items]
            dts = [_dot_nt(dvnh, vb) + _dot_nt(dw, kbe) for dvnh, vb, dw, kbe in zip(dvns, vbs, dws, kbes)]
            dkbes = [_dot_tn(tmh, dw) for tmh, dw in zip(tmhs, dws)]
            xs = [_dot_nt(dt, tmh) for dt, tmh in zip(dts, tmhs)]
            das = [jnp.where(strict, -_dot_tn(tmh, x), 0.0) for tmh, x in zip(tmhs, xs)]
            dmas = [da * dm for da, dm in zip(das, dmats)]
            dmps = [dp * dm for dp, dm in zip(dps, dmats)]
            dkbs = [_dot(dma, kh) + dkbe * e for dma, kh, dkbe, e in zip(dmas, ks, dkbes, es)]
            for i, (c, hd) in enumerate(items):
                dq_ref[rs(c), sl(hd)] = _dot(dmps[i], ks[i]) + dqds[i] * es[i]
                dk_ref[rs(c), sl(hd)] = (_dot_tn(dmas[i], kbs[i]) + _dot_tn(dmps[i], qs[i]) + dkds[i] * eks[i]
                                         + dkbs[i] * betas[i])
                dv_ref[rs(c), sl(hd)] = dvbs[i] * betas[i]
            for c in range(c0, c0 + WY_GROUP):
                acc = jnp.zeros((CHUNK, LANE), F32)
                acc_t = jnp.zeros((SUBLANE, CHUNK), F32)
                for i, (ci, hd) in enumerate(items):
                    if ci != c:
                        continue
                    gmat = das[i] * a_s[i] + dps[i] * ps[i]
                    rk = jnp.sum(dkds[i] * ks[i], -1, keepdims=True) * eks[i]
                    de = (jnp.sum(dqds[i] * qs[i], -1, keepdims=True)
                          + jnp.sum(dkbes[i] * kbs[i], -1, keepdims=True))
                    dglast = jnp.sum(rk, keepdims=True) + dgts[i] * jnp.exp(glasts[i])
                    dgc = (jnp.sum(gmat, -1, keepdims=True) + de * es[i] - rk
                           + jnp.where(rowi == CHUNK - 1, dglast, 0.0))
                    dbeta = (jnp.sum(dkbs[i] * ks[i], -1, keepdims=True)
                             + jnp.sum(dvbs[i] * vs[i], -1, keepdims=True))
                    acc = acc + jnp.where(lane == hd, dbeta, 0.0) + jnp.where(lane == A_HEADS + hd, dgc, 0.0)
                    acc_t = acc_t + jnp.where(sub == A_HEADS + hd, -jnp.sum(gmat, axis=0, keepdims=True), 0.0)
                dbg_ref[rs(c), :] = acc
                dbgt_ref[:, rs(c)] = acc_t

    blk = pl.BlockSpec((rows, A_WIDTH), lambda i: (i, 0))
    half = pl.BlockSpec((rows, A_HEADS * CHUNK), lambda i: (i, 0))
    col = pl.BlockSpec((rows, LANE), lambda i: (i, 0))
    rowf = pl.BlockSpec((SUBLANE, rows), lambda i: (0, i))
    st = pl.BlockSpec((per, A_HEADS, LANE, LANE), lambda i: (i, 0, 0, 0))
    wide = jax.ShapeDtypeStruct((t, A_WIDTH), F32)
    return pl.pallas_call(
        body, name=name, grid=(t // rows,),
        in_specs=[blk, blk, blk, blk, half, half, col, rowf, st, st, blk, blk],
        out_specs=[blk, blk, blk, col, rowf],
        out_shape=[wide, wide, wide, jax.ShapeDtypeStruct((t, LANE), F32), jax.ShapeDtypeStruct((SUBLANE, t), F32)],
        compiler_params=_cp("parallel"))(q, k, v, vn, tmat, qk, bg, bgt, s_all, ds_all, dvn, do)


def _dn_pre_bwd(h, conv_w, par, dq, dk, dv, dbg, dbgt, *, tt, name):
    t = h.shape[0]
    cw = 3 * A_WIDTH
    hb = tt // SUBLANE

    def body(pre_ref, halo_ref, bgi_ref, cw_ref, par_ref, dq_ref, dk_ref, dv_ref, dbg_ref, dbgt_ref,
             dc_ref, dbgi_ref, dpar_ref):
        i = pl.program_id(0)

        @pl.when(i == 0)
        def _():
            dpar_ref[...] = jnp.zeros_like(dpar_ref)

        cur = pre_ref[...]
        before = jnp.where(i > 0, halo_ref[...], 0.0)
        c = _conv_fwd(cur, before, cw_ref[...])
        s = _silu(c)
        ds = _dsilu(c)
        for hd in range(A_HEADS):
            sl = slice(hd * LANE, (hd + 1) * LANE)
            for base, d_ref, scale in ((0, dq_ref, A_HEAD_DIM ** -0.5), (A_WIDTH, dk_ref, 1.0)):
                csl = slice(base + hd * LANE, base + (hd + 1) * LANE)
                tq = s[:, base + hd * LANE:base + (hd + 1) * LANE]
                dy = d_ref[:, sl]
                rq = lax.rsqrt(jnp.sum(tq * tq, -1, keepdims=True) + L2_EPS)
                dtq = scale * (rq * dy - tq * (rq * rq * rq) * jnp.sum(dy * tq, -1, keepdims=True))
                dc_ref[:, csl] = dtq * ds[:, base + hd * LANE:base + (hd + 1) * LANE]
        dc_ref[:, 2 * A_WIDTH:] = dv_ref[...] * ds[:, 2 * A_WIDTH:]
        raw = bgi_ref[...]
        lane = lax.broadcasted_iota(jnp.int32, raw.shape, 1)
        is_b = lane < A_HEADS
        is_a = (lane >= A_HEADS) & (lane < 2 * A_HEADS)
        rows_t = jnp.concatenate([dbgt_ref[...], jnp.zeros((LANE - SUBLANE, tt), F32)], axis=0)
        dbg_v = dbg_ref[...] + jnp.where(is_a, jnp.transpose(rows_t), 0.0)
        dbg_v = jnp.where(is_a, _dot_hi(_chunk_tri(tt, lower=False), jnp.where(is_a, dbg_v, 0.0)), dbg_v)
        beta = _sigmoid(raw)
        z = raw + par_ref[1:2, :]
        neg_ea = -jnp.exp(par_ref[0:1, :])
        g = neg_ea * _softplus(z)
        da = dbg_v * neg_ea * _sigmoid(z)
        dbgi_ref[...] = jnp.where(is_b, dbg_v * beta * (1.0 - beta), jnp.where(is_a, da, 0.0))
        dpar_ref[0:1, :] += jnp.sum(jnp.where(is_a, dbg_v * g, 0.0), axis=0, keepdims=True)
        dpar_ref[1:2, :] += jnp.sum(jnp.where(is_a, da, 0.0), axis=0, keepdims=True)

    wide = pl.BlockSpec((tt, A_WIDTH), lambda i: (i, 0))
    return pl.pallas_call(
        body, name=name, grid=(t // tt,),
        in_specs=[pl.BlockSpec((tt, cw), lambda i: (i, 0)),
                  pl.BlockSpec((SUBLANE, cw), lambda i: (jnp.maximum(i * hb - 1, 0), 0)),
                  pl.BlockSpec((tt, LANE), lambda i: (i, C_BG // LANE)),
                  pl.BlockSpec((CONV_K, cw), lambda i: (0, 0)),
                  pl.BlockSpec((SUBLANE, LANE), lambda i: (0, 0)),
                  wide, wide, wide, pl.BlockSpec((tt, LANE), lambda i: (i, 0)),
                  pl.BlockSpec((SUBLANE, tt), lambda i: (0, i))],
        out_specs=[pl.BlockSpec((tt, cw), lambda i: (i, 0)), pl.BlockSpec((tt, LANE), lambda i: (i, 0)),
                   pl.BlockSpec((SUBLANE, LANE), lambda i: (0, 0))],
        out_shape=[jax.ShapeDtypeStruct((t, cw), F32), jax.ShapeDtypeStruct((t, LANE), F32),
                   jax.ShapeDtypeStruct((SUBLANE, LANE), F32)],
        compiler_params=_cp("arbitrary"))(h, h, h, conv_w, par, dq, dk, dv, dbg, dbgt)


def _conv_bwd(dc, h, conv_w, dh, *, tt, name):
    t = dc.shape[0]
    cw = 3 * A_WIDTH
    hb = tt // SUBLANE
    nb = t // tt

    def body(dc_ref, after_ref, pre_ref, before_ref, cw_ref, dh_in_ref, dpre_ref, dcw_ref):
        i = pl.program_id(0)

        @pl.when(i == 0)
        def _():
            dcw_ref[...] = jnp.zeros_like(dcw_ref)

        dcv = dc_ref[...]
        after = jnp.where(i < nb - 1, after_ref[...], 0.0)
        cur = pre_ref[...]
        before = jnp.where(i > 0, before_ref[...], 0.0)
        w = cw_ref[...]
        acc = dcv * w[CONV_K - 1:CONV_K, :]
        dcw_ref[CONV_K - 1:CONV_K, :] += jnp.sum(dcv * cur, axis=0, keepdims=True)
        for s in range(1, CONV_K):
            j = CONV_K - 1 - s
            acc = acc + _shift_up(dcv, after, s) * w[j:j + 1, :]
            dcw_ref[j:j + 1, :] += jnp.sum(dcv * _shift_down(cur, before, s), axis=0, keepdims=True)
        dpre_ref[...] = acc

    return pl.pallas_call(
        body, name=name, grid=(nb,),
        in_specs=[pl.BlockSpec((tt, cw), lambda i: (i, 0)),
                  pl.BlockSpec((SUBLANE, cw), lambda i: (jnp.minimum((i + 1) * hb, t // SUBLANE - 1), 0)),
                  pl.BlockSpec((tt, cw), lambda i: (i, 0)),
                  pl.BlockSpec((SUBLANE, cw), lambda i: (jnp.maximum(i * hb - 1, 0), 0)),
                  pl.BlockSpec((CONV_K, cw), lambda i: (0, 0)), _ANY],
        out_specs=[pl.BlockSpec((tt, cw), lambda i: (i, 0)), pl.BlockSpec((SUBLANE, cw), lambda i: (0, 0))],
        out_shape=[jax.ShapeDtypeStruct(dh.shape, F32), jax.ShapeDtypeStruct((SUBLANE, cw), F32)],
        input_output_aliases={5: 0},
        compiler_params=_cp("arbitrary"))(dc, dc, h, h, conv_w, dh)


def _swa_bwd(h, dm, sinks_b, dh, *, name, carry=None):
    t = h.shape[0]
    qspec, cur, prev = _swa_specs()
    c_ins, c_in_specs, c_out_specs, c_outs, c_scratch = _carry_specs(carry)

    def body(*refs):
        (q_ref, kc_ref, kp_ref, vc_ref, vp_ref, zb_ref, dy_ref, sk_ref, dh_in_ref,
         dqz_ref, dk_ref, dv_ref, dsk_ref) = _carried(carry, refs, 9, 4, t // BLOCK)
        n_blk = pl.program_id(0)

        @pl.when(n_blk == 0)
        def _():
            dk_ref[...] = jnp.zeros_like(dk_ref)
            dv_ref[...] = jnp.zeros_like(dv_ref)
            dsk_ref[...] = jnp.zeros_like(dsk_ref)

        kband = jnp.concatenate([kp_ref[...], kc_ref[...]], axis=0)
        vband = jnp.concatenate([vp_ref[...], vc_ref[...]], axis=0)
        scale = B_HEAD_DIM ** -0.5
        hks = range(B_KV_HEADS)
        ksl = lambda hk: slice(hk * B_HEAD_DIM, (hk + 1) * B_HEAD_DIM)
        groups = _swa_group_probs(q_ref, sk_ref, kband, vband, _swa_neg_dist(n_blk))
        zbs = [_stack_heads(zb_ref, hk) for hk in hks]
        dys = [_stack_heads(dy_ref, hk) for hk in hks]
        dos = [dys[hk] * _silu(zbs[hk]) for hk in hks]
        deltas = [jnp.sum(dos[hk] * groups[hk][3], -1, keepdims=True) for hk in hks]
        dss = [groups[hk][1] * (_dot_nt(dos[hk], vband[:, ksl(hk)]) - deltas[hk]) for hk in hks]
        dqs = [_dot(dss[hk], kband[:, ksl(hk)]) * scale for hk in hks]
        dk_acc = [_dot_tn(dss[hk], groups[hk][0]) for hk in hks]
        dv_acc = [_dot_tn(groups[hk][1], dos[hk]) for hk in hks]
        for hk in hks:
            dzb = dys[hk] * groups[hk][3] * _dsilu(zbs[hk])
            dsink = groups[hk][2] * deltas[hk]
            for g in range(B_GROUP):
                hq = hk * B_GROUP + g
                rows = slice(g * BLOCK, (g + 1) * BLOCK)
                qsl = slice(hq * B_HEAD_DIM, (hq + 1) * B_HEAD_DIM)
                dqz_ref[:, qsl] = dqs[hk][rows]
                dqz_ref[:, B_WIDTH + hq * B_HEAD_DIM:B_WIDTH + (hq + 1) * B_HEAD_DIM] = dzb[rows]
                dsk_ref[hq:hq + 1, :] += -jnp.sum(dsink[rows], keepdims=True)
        dkb = jnp.concatenate(dk_acc, axis=1)
        dvb = jnp.concatenate(dv_acc, axis=1)
        at_cur = pl.ds(pl.multiple_of(n_blk * BLOCK, BLOCK), BLOCK)
        at_prev = pl.ds(pl.multiple_of(jnp.maximum(n_blk - 1, 0) * BLOCK, BLOCK), BLOCK)
        dk_ref[at_prev, :] += dkb[:BLOCK]
        dv_ref[at_prev, :] += dvb[:BLOCK]
        dk_ref[at_cur, :] += dkb[BLOCK:]
        dv_ref[at_cur, :] += dvb[BLOCK:]

    narrow = jax.ShapeDtypeStruct((t, B_KV_WIDTH), F32)
    res = lambda a, b: pl.BlockSpec((a, b), lambda i: (0, 0))
    outs = pl.pallas_call(
        body, name=name, grid=(t // BLOCK,),
        in_specs=[qspec(C_QB), cur(C_KB), prev(C_KB), cur(C_VB), prev(C_VB), qspec(C_ZB),
                  pl.BlockSpec((BLOCK, B_WIDTH), lambda i: (i, 1)), res(B_Q_HEADS, LANE), _ANY] + c_in_specs,
        out_specs=[pl.BlockSpec((BLOCK, 2 * B_WIDTH), lambda i: (i, C_QB // (2 * B_WIDTH))),
                   res(t, B_KV_WIDTH), res(t, B_KV_WIDTH), res(B_Q_HEADS, LANE)] + c_out_specs,
        out_shape=[jax.ShapeDtypeStruct(dh.shape, F32), narrow, narrow,
                   jax.ShapeDtypeStruct((B_Q_HEADS, LANE), F32)] + c_outs,
        scratch_shapes=c_scratch,
        input_output_aliases={8: 0},
        compiler_params=_cp("arbitrary"))(h, h, h, h, h, h, dm, sinks_b, dh, *c_ins)
    return outs[:4], outs[4:]


def _matmul_tn(a, b, *, tk, tm, name):
    t, m = a.shape
    n = b.shape[1]

    def body(a_ref, b_ref, o_ref):
        @pl.when(pl.program_id(1) == 0)
        def _():
            o_ref[...] = jnp.zeros_like(o_ref)

        o_ref[...] += _dot_tn(a_ref[...], b_ref[...])

    return pl.pallas_call(
        body, name=name, grid=(m // tm, t // tk),
        in_specs=[pl.BlockSpec((tk, tm), lambda j, kk: (kk, j)), pl.BlockSpec((tk, n), lambda j, kk: (kk, 0))],
        out_specs=pl.BlockSpec((tm, n), lambda j, kk: (j, 0)),
        out_shape=jax.ShapeDtypeStruct((m, n), F32),
        compiler_params=_cp("parallel", "arbitrary"))(a, b)


def _in_proj_dx(dh_main, dh_tail, wt, dr, *, tm, name, carry=None):
    t, n_main = dh_main.shape
    n_tail = dh_tail.shape[1]
    c_ins, c_in_specs, c_out_specs, c_outs, c_scratch = _carry_specs(carry)

    def body(*refs):
        a_ref, t_ref, wa_ref, wt_ref, r_ref, o_ref = _carried(carry, refs, 5, 1, t // tm)
        o_ref[...] = _dot(a_ref[...], wa_ref[...]) + _dot(t_ref[...], wt_ref[...]) + DEEPNORM_ALPHA * r_ref[...]

    row = lambda w: pl.BlockSpec((tm, w), lambda i: (i, 0))
    outs = pl.pallas_call(
        body, name=name, grid=(t // tm,),
        in_specs=[row(n_main), row(n_tail), pl.BlockSpec((n_main, D_MODEL), lambda i: (0, 0)),
                  pl.BlockSpec((n_tail, D_MODEL), lambda i: (n_main // n_tail, 0)), row(D_MODEL)] + c_in_specs,
        out_specs=[row(D_MODEL)] + c_out_specs,
        out_shape=[jax.ShapeDtypeStruct((t, D_MODEL), F32)] + c_outs,
        scratch_shapes=c_scratch,
        compiler_params=_cp("arbitrary"))(dh_main, dh_tail, wt, wt, dr, *c_ins)
    return outs[0], outs[1:]


def _layer_bwd(dxn, res, wt, conv_w, par, sinks_b, norm_w, w_out_bf, ln_g, l, carry=None, carry_dx=None):
    w_out_bf = res["w_out"]
    dr, dm, dw_out, dln_g, dln_b = _ln_out_bwd(dxn, res["r"], res["mixed"], ln_g, w_out_bf, tm=256, name=f"ln_out_bwd_{l}")
    h = res["h"]
    do, dh, dnw = _dn_post_bwd(dm, res["oa"], h, norm_w, tm=512, name=f"dn_post_bwd_{l}")
    dvn, ds_all = _dn_scan_bwd(res["q"], res["k"], res["w"], res["qk"], res["bg"], do, name=f"dn_scan_bwd_{l}")
    dq, dk, dv, dbg, dbgt = _dn_chunk_bwd(res["q"], res["k"], res["v"], res["vn"], res["tmat"], res["qk"], res["bg"],
                                          res["bgt"], res["s_all"], ds_all, dvn, do, name=f"dn_chunk_bwd_{l}")
    dc, dbgi, dpar = _dn_pre_bwd(h, conv_w, par, dq, dk, dv, dbg, dbgt, tt=512, name=f"dn_pre_bwd_{l}")
    dh, dcw = _conv_bwd(dc, h, conv_w, dh, tt=512, name=f"conv_bwd_{l}")
    (dh, dkb, dvb, dsk), carried = _swa_bwd(h, dm, sinks_b, dh, name=f"swa_bwd_{l}", carry=carry)
    dh_tail = jnp.concatenate([dkb, dvb, dbgi], axis=1)
    dwt_main = _matmul_tn(dh, res["x"], tk=512, tm=768, name=f"in_proj_dw_{l}")
    dwt_tail = _matmul_tn(dh_tail, res["x"], tk=512, tm=P_COLS - DH_MAIN, name=f"in_proj_dw_tail_{l}")
    grads = dict(w_in=(dwt_main, dwt_tail), conv_w=dcw[:CONV_K], a_log=dpar[0, A_HEADS:2 * A_HEADS],
                 dt_bias=dpar[1, A_HEADS:2 * A_HEADS], norm_w=dnw[0], sinks=dsk[:, 0], w_out=dw_out,
                 ln_g=dln_g[0], ln_b=dln_b[0])
    dx, carried_dx = _in_proj_dx(dh, dh_tail, wt, dr, tm=256, name=f"in_proj_dx_{l}",
                                 carry=None if carry_dx is None else carry_dx(grads))
    return dx, grads, carried, carried_dx


def _layer_args(wt, conv_w, a_log, dt_bias, sinks, norm_w, w_out_bf):
    return (wt, conv_w, _gate_params(a_log, dt_bias), jnp.broadcast_to(sinks[:, None], (B_Q_HEADS, LANE)),
            norm_w[None], w_out_bf)


def _local_step(x, target, args0, args1, ln_g, ln_b, gathers=None, reduce1=None, reduce0=None):
    assert DEPTH == 2
    x1, res0, got = _layer_fwd(x, *args0, ln_g[0][None], ln_b[0][None], 0, carries=gathers)
    if gathers is not None:
        args1 = args1(got)
    x2, res1, _ = _layer_fwd(x1, *args1, ln_g[1][None], ln_b[1][None], 1)
    dx, loss_tile = _loss_grad(x2, target, tm=512, name="loss_grad")
    dx, grads1, _, _ = _layer_bwd(dx, res1, *args1, ln_g[1][None], 1)
    carry = None if reduce1 is None else reduce1(grads1)
    carry_dx = None if reduce0 is None else (lambda grads0: reduce0(grads0, grads1))
    dx, grads0, landed1, landed0 = _layer_bwd(dx, res0, *args0, ln_g[0][None], 0, carry=carry, carry_dx=carry_dx)
    return loss_tile, dx, [grads0, grads1], landed1, landed0


_ANY = pl.BlockSpec(memory_space=pl.ANY)
_MESH = pl.DeviceIdType.MESH


HALF = D_MODEL // 2


class _Exchange:
    def __init__(self, ins, outs, n_remote, n_local, plan):
        self.ins, self.outs, self.n_remote, self.n_local, self.plan = tuple(ins), tuple(outs), n_remote, n_local, plan

    def scratch(self):
        return [pltpu.SemaphoreType.DMA((self.n_remote,)), pltpu.SemaphoreType.DMA((self.n_remote,)),
                pltpu.SemaphoreType.DMA((max(self.n_local, 1),))]

    def _copies(self, in_refs, out_refs, sems, arriving):
        send_sems, recv_sems, local_sems = sems
        local, sends, recvs = self.plan(in_refs, out_refs)
        loc = [pltpu.make_async_copy(s, d, local_sems.at[i]) for i, (s, d) in enumerate(local)]
        rem = [pltpu.make_async_remote_copy(src_ref=s, dst_ref=recvs[i] if arriving else d, send_sem=send_sems.at[i],
                                            recv_sem=recv_sems.at[i], device_id=peer, device_id_type=_MESH)
               for i, (s, d, peer) in enumerate(sends)]
        return loc, rem

    def start(self, in_refs, out_refs, sems):
        loc, rem = self._copies(in_refs, out_refs, sems, arriving=False)
        for cp in loc + rem:
            cp.start()

    def finish(self, in_refs, out_refs, sems):
        loc, rem = self._copies(in_refs, out_refs, sems, arriving=True)
        for cp in rem:
            cp.wait_recv()
        for cp in rem:
            cp.wait_send()
        for cp in loc:
            cp.wait()


def _run_exchange(ex, *, name):
    n_in, n_out = len(ex.ins), len(ex.outs)

    def body(*refs):
        parts = refs[:n_in], refs[n_in:n_in + n_out], refs[n_in + n_out:]
        ex.start(*parts)
        ex.finish(*parts)

    return pl.pallas_call(body, name=name, in_specs=[_ANY] * n_in, out_specs=[_ANY] * n_out, out_shape=list(ex.outs),
                          scratch_shapes=ex.scratch())(*ex.ins)


def _place():
    x, y, c = lax.axis_index("x"), lax.axis_index("y"), lax.axis_index("c")
    return x, y, c, [(1 - x, y), (x, 1 - y), (1 - x, 1 - y)]


def _gather_exchange(arrays):
    n = len(arrays)

    def plan(src, dst):
        x, y, c, chips = _place()
        me = 2 * x + y
        local = [(src[k], dst[k].at[me]) for k in range(n)]
        sends = [(src[k], dst[k].at[me], (px, py, c)) for k in range(n) for px, py in chips]
        recvs = [dst[k].at[2 * px + py] for k in range(n) for px, py in chips]
        return local, sends, recvs

    return _Exchange(arrays, [jax.ShapeDtypeStruct((N_SHARD,) + a.shape, a.dtype) for a in arrays], 3 * n, n, plan)


def _half(core):
    return pl.ds(pl.multiple_of(core * HALF, HALF), HALF)


def _reduce_scatter_exchange(g, small=None):
    ins = [g] if small is None else [g, small]
    outs = [jax.ShapeDtypeStruct((7,) + g.shape[1:2] + (HALF,), g.dtype)]
    if small is not None:
        outs.append(jax.ShapeDtypeStruct((8,) + small.shape, small.dtype))

    def plan(src, dst):
        x, y, c, chips = _place()
        me = 2 * x + y
        peers = [(px, py, c if t == 0 else 1 - c) for px, py in chips for t in (0, 1)] + [(x, y, 1 - c)]
        sends = [(src[0].at[2 * px + py, :, _half(pc)], dst[0].at[k], (px, py, pc)) for k, (px, py, pc) in enumerate(peers)]
        recvs = [dst[0].at[k] for k in range(7)]
        local = []
        if small is not None:
            mine = 4 * x + 2 * y + c
            local = [(src[1], dst[1].at[mine])]
            sends += [(src[1], dst[1].at[mine], peer) for peer in peers]
            recvs += [dst[1].at[4 * px + 2 * py + pc] for px, py, pc in peers]
        return local, sends, recvs

    return _Exchange(ins, outs, 7 * len(ins), len(ins) - 1, plan)


def _share_exchange(arrays):
    n = len(arrays)

    def plan(src, dst):
        x, y, c, _ = _place()
        return [], [(src[k], dst[k], (x, y, 1 - c)) for k in range(n)], [dst[k] for k in range(n)]

    return _Exchange(arrays, [jax.ShapeDtypeStruct(a.shape, a.dtype) for a in arrays], n, 0, plan)


def _sum_scatter(g, land, me, core, *, tc, name):
    rows = g.shape[1]
    per = HALF // tc

    def body(where_ref, g_ref, land_ref, o_ref):
        acc = g_ref[...]
        for k in range(7):
            acc = acc + land_ref[k].astype(F32)
        o_ref[...] = acc

    return pl.pallas_call(
        body, name=name, out_shape=jax.ShapeDtypeStruct((rows, HALF), F32), compiler_params=_cp("parallel"),
        grid_spec=pltpu.PrefetchScalarGridSpec(
            num_scalar_prefetch=1, grid=(per,),
            in_specs=[pl.BlockSpec((None, rows, tc), lambda i, w: (w[0], 0, w[1] * per + i)),
                      pl.BlockSpec((7, rows, tc), lambda i, w: (0, 0, i))],
            out_specs=pl.BlockSpec((rows, tc), lambda i, w: (0, i))))(
        jnp.stack([me, core]).astype(jnp.int32), g, land)


def _sum_slots(a, *, name):
    n = a.shape[0]

    def body(a_ref, o_ref):
        acc = a_ref[0]
        for k in range(1, n):
            acc = acc + a_ref[k]
        o_ref[...] = acc

    return pl.pallas_call(body, name=name, out_shape=jax.ShapeDtypeStruct(a.shape[1:], a.dtype))(a)


def _elementwise(fn, ins, n_out, block, *, name):
    shape = ins[0].shape
    grid = tuple(s // b for s, b in zip(shape, block))
    n_in = len(ins)

    def body(*refs):
        outs = fn(*[r[...] for r in refs[:n_in]])
        for o_ref, val in zip(refs[n_in:], outs):
            o_ref[...] = val

    spec = pl.BlockSpec(block, lambda i, j, k: (i, j, k))
    return pl.pallas_call(body, name=name, grid=grid, in_specs=[spec] * n_in, out_specs=[spec] * n_out,
                          out_shape=[jax.ShapeDtypeStruct(shape, F32)] * n_out,
                          compiler_params=_cp(*["parallel"] * 3))(*ins)


def _adamw_math(w, g, m, v):
    mn = ADAM_B1 * m + (1.0 - ADAM_B1) * g
    vn = ADAM_B2 * v + (1.0 - ADAM_B2) * (g * g)
    m_hat = mn / (1.0 - ADAM_B1 ** ADAM_STEP)
    v_hat = vn / (1.0 - ADAM_B2 ** ADAM_STEP)
    return -ADAM_LR * (m_hat / (jnp.sqrt(v_hat) + ADAM_EPS) + ADAM_WD * w), mn, vn


def _adamw(w, g, m, v, block, *, name):
    return _elementwise(_adamw_math, [w, g, m, v], 3, block, name=name)


def _adamw_small(ws, gs, ms, vs, *, name):
    n = len(ws)

    def body(*refs):
        w, g, m, v, outs = refs[:n], refs[n:2 * n], refs[2 * n:3 * n], refs[3 * n:4 * n], refs[4 * n:]
        for k in range(n):
            for slot, val in enumerate(_adamw_math(w[k][...], g[k][...], m[k][...], v[k][...])):
                outs[slot * n + k][...] = val

    outs = pl.pallas_call(body, name=name, out_shape=[jax.ShapeDtypeStruct(a.shape, F32) for a in ws] * 3)(
        *ws, *gs, *ms, *vs)
    return outs[:n], outs[n:2 * n], outs[2 * n:]


def _to_kernel_order(wt):
    gates = jnp.pad(wt[2048:2056], ((0, LANE - 2 * A_HEADS), (0, 0)))
    return jnp.concatenate([wt[0:2048], wt[2056:2568], wt[2824:3336], wt[2568:2696], wt[2696:2824], gates], axis=0)


def _from_kernel_order(main, tail):
    return jnp.concatenate([main[0:2048], tail[C_BG - DH_MAIN:C_BG - DH_MAIN + 2 * A_HEADS],
                            main[C_QB:C_QB + B_WIDTH], tail[0:B_KV_WIDTH], tail[B_KV_WIDTH:2 * B_KV_WIDTH],
                            main[C_ZB:C_ZB + B_WIDTH]], axis=0)


def _gate_params(a_log, dt_bias):
    return jnp.pad(jnp.stack([a_log, dt_bias]), ((0, SUBLANE - 2), (A_HEADS, LANE - 2 * A_HEADS)))


SMALL = ("conv_w", "a_log", "dt_bias", "norm_w", "sinks", "ln_g", "ln_b")


def _pack(parts, cols):
    flat = jnp.concatenate([p.reshape(-1) for p in parts])
    rows = -(-flat.shape[0] // cols)
    return jnp.pad(flat, (0, rows * cols - flat.shape[0])).reshape(rows, cols)


def _unpack(packed, shapes):
    flat = packed.reshape(-1)
    out, at = [], 0
    for s in shapes:
        n = math.prod(s)
        out.append(flat[at:at + n].reshape(s))
        at += n
    return out


def kernel(x, w_in, conv_w, a_log, dt_bias, norm_w, sinks, w_out, ln_g, ln_b, loss_target, m_w_in, m_conv_w, m_a_log, m_dt_bias, m_norm_w, m_sinks, m_w_out, m_ln_g, m_ln_b, v_w_in, v_conv_w, v_a_log, v_dt_bias, v_norm_w, v_sinks, v_w_out, v_ln_g, v_ln_b):
    xi, yi, ci = lax.axis_index("x"), lax.axis_index("y"), lax.axis_index("c")
    me = 2 * xi + yi

    to_t = lambda a: jnp.transpose(a, (2, 0, 1))
    from_t = lambda a: jnp.transpose(a, (1, 2, 0))

    wt_shard = to_t(w_in)

    def pack_weights(l):
        rows = jnp.pad(wt_shard[:, l], ((0, IN_PAD - IN_SHARD), (0, 0)))
        return jnp.concatenate([rows, w_out[l]], axis=0).astype(BF16)

    pack0, pack1 = pack_weights(0), pack_weights(1)
    got_in0, g_conv = _run_exchange(_gather_exchange([pack0[:IN_PAD], conv_w]), name="gather_weights_0")
    conv_full = jnp.moveaxis(g_conv, 0, 2).reshape(DEPTH, CONV_K, 3 * A_WIDTH)
    gathers = dict(in_proj=_gather_exchange([pack0[IN_PAD:]]), dn_wy=_gather_exchange([pack1[:PACK_SPLIT]]),
                   swa=_gather_exchange([pack1[PACK_SPLIT:]]))
    w_in_of = lambda rows: _to_kernel_order(rows.reshape(IN_COLS, D_MODEL))
    w_out_of = lambda rows: rows.reshape(D_MODEL, D_MODEL)
    args0 = _layer_args(w_in_of(got_in0[:, :IN_SHARD]), conv_full[0], a_log[0], dt_bias[0], sinks[0], norm_w[0],
                        lambda got: w_out_of(got[0]))

    def args1(got):
        first, rest = got["dn_wy"][0], got["swa"][0]
        rows = jnp.concatenate([first, rest[:, :IN_SHARD - PACK_SPLIT]], axis=1)
        return _layer_args(w_in_of(rows), conv_full[1], a_log[1], dt_bias[1], sinks[1], norm_w[1],
                           w_out_of(rest[:, IN_PAD - PACK_SPLIT:]))

    def pack_grads(g):
        gin = _from_kernel_order(*g["w_in"]).reshape(N_SHARD, IN_SHARD, D_MODEL)
        gin = jnp.pad(gin, ((0, 0), (0, IN_PAD - IN_SHARD), (0, 0)))
        return jnp.concatenate([gin, g["w_out"].reshape(N_SHARD, OUT_SHARD, D_MODEL)], axis=1).astype(BF16)

    packed = {}

    def reduce1(grads1):
        packed[1] = pack_grads(grads1)
        return _reduce_scatter_exchange(packed[1])

    def reduce0(grads0, grads1):
        packed[0] = pack_grads(grads0)
        gsmall = _pack([jnp.stack([g[nm] for g in (grads0, grads1)]) for nm in SMALL], D_MODEL)
        return _reduce_scatter_exchange(packed[0], gsmall)

    loss_tile, dx, grads, landed1, (landed0, landed_small) = _local_step(
        x[0], loss_target[0], args0, args1, ln_g, ln_b, gathers=gathers, reduce1=reduce1, reduce0=reduce0)
    loss = lax.psum(loss_tile[0, 0], ("x", "y", "c"))

    small_shapes = [(DEPTH,) + grads[0][nm].shape for nm in SMALL]
    halves = [_sum_scatter(packed[l], land, me, ci, tc=2 * LANE, name=f"reduce_sum_{l}")
              for l, land in ((0, landed0), (1, landed1[0]))]
    s_small = _sum_slots(landed_small, name="reduce_sum_small")
    others = _run_exchange(_share_exchange(halves), name="pair_share")
    full = [jnp.where(ci == 0, jnp.concatenate([mine, other], axis=1), jnp.concatenate([other, mine], axis=1))
            for mine, other in zip(halves, others)]
    grad_in_t = jnp.stack([f[:IN_SHARD] for f in full], axis=1)
    grad_out = jnp.stack([f[IN_PAD:] for f in full])
    out_blk = (1, OUT_SHARD, D_MODEL)
    gs = dict(zip(SMALL, _unpack(s_small, small_shapes)))
    gs["conv_w"] = lax.dynamic_slice_in_dim(gs["conv_w"], me * CONV_SHARD, CONV_SHARD, axis=2)

    adam_in_blk = (IN_SHARD // 6, DEPTH, D_MODEL)
    d_in, nm_in, nv_in = (from_t(o) for o in _adamw(to_t(w_in), grad_in_t, to_t(m_w_in), to_t(v_w_in), adam_in_blk,
                                                    name="adamw_in"))
    d_out, nm_out, nv_out = _adamw(w_out, grad_out, m_w_out, v_w_out, out_blk, name="adamw_out")
    ws = dict(conv_w=conv_w, a_log=a_log, dt_bias=dt_bias, norm_w=norm_w, sinks=sinks, ln_g=ln_g, ln_b=ln_b)
    ms = dict(conv_w=m_conv_w, a_log=m_a_log, dt_bias=m_dt_bias, norm_w=m_norm_w, sinks=m_sinks, ln_g=m_ln_g, ln_b=m_ln_b)
    vs = dict(conv_w=v_conv_w, a_log=v_a_log, dt_bias=v_dt_bias, norm_w=v_norm_w, sinks=v_sinks, ln_g=v_ln_g, ln_b=v_ln_b)
    d_s, nm_s, nv_s = (dict(zip(SMALL, o)) for o in _adamw_small(*[[d[nm] for nm in SMALL] for d in (ws, gs, ms, vs)],
                                                                 name="adamw_small"))

    def in_order(big_in, small, big_out):
        return (big_in, small["conv_w"], small["a_log"], small["dt_bias"], small["norm_w"], small["sinks"], big_out,
                small["ln_g"], small["ln_b"])

    return (loss, dx[None], *in_order(from_t(grad_in_t), gs, grad_out), *in_order(d_in, d_s, d_out),
            *in_order(nm_in, nm_s, nm_out), *in_order(nv_in, nv_s, nv_out))
```

```python
import math

import jax
import jax.numpy as jnp
from jax import lax
from jax.experimental import pallas as pl
from jax.experimental.pallas import tpu as pltpu

F32 = jnp.float32
BF16 = jnp.bfloat16
HI = lax.Precision.HIGHEST

D_MODEL = 1024
DEPTH = 2
A_HEADS = 4
A_HEAD_DIM = 128
A_WIDTH = 512
CONV_K = 4
CHUNK = 64
B_Q_HEADS = 8
B_KV_HEADS = 2
B_HEAD_DIM = 64
B_GROUP = 4
B_WIDTH = 512
B_KV_WIDTH = 128
BLOCK = 128
IN_COLS = 3336
DEEPNORM_ALPHA = (2 * DEPTH) ** 0.25
LN_EPS = 1e-5
RMS_EPS = 1e-6
L2_EPS = 1e-6
ADAM_LR = 0.001
ADAM_B1 = 0.9
ADAM_B2 = 0.999
ADAM_EPS = 1e-08
ADAM_WD = 0.01
ADAM_STEP = 10

N_SHARD = 4
IN_SHARD = IN_COLS // N_SHARD
OUT_SHARD = D_MODEL // N_SHARD
CONV_SHARD = 3 * A_WIDTH // N_SHARD
IN_PAD = -(-IN_SHARD // 32) * 32
PACK_SPLIT = 432

P_COLS = 3456
C_PRE = 0
C_ZA = 1536
C_QB = 2048
C_ZB = 2560
C_KB = 3072
C_VB = 3200
C_BG = 3328
DH_MAIN = C_KB
LANE = 128
SUBLANE = 8
VMEM_LIMIT = 56 * 1024 * 1024
ALIBI = tuple(2.0 ** (-8.0 * (h + 1) / B_Q_HEADS) for h in range(B_Q_HEADS))
NEG = -1e30


def _cp(*sem):
    return pltpu.CompilerParams(dimension_semantics=sem, vmem_limit_bytes=VMEM_LIMIT)


def _dot(a, b):
    return jnp.dot(a.astype(BF16), b.astype(BF16), preferred_element_type=F32)


def _dot_nt(a, b):
    return lax.dot_general(a.astype(BF16), b.astype(BF16), (((1,), (1,)), ((), ())),
                           preferred_element_type=F32)


def _dot_tn(a, b):
    return lax.dot_general(a.astype(BF16), b.astype(BF16), (((0,), (0,)), ((), ())),
                           preferred_element_type=F32)


def _dot_hi(a, b):
    return jnp.dot(a, b, precision=HI, preferred_element_type=F32)


def _sigmoid(x):
    return jax.nn.sigmoid(x)


def _silu(x):
    return x * _sigmoid(x)


def _dsilu(x):
    s = _sigmoid(x)
    return s * (1.0 + x * (1.0 - s))


def _softplus(x):
    return jnp.maximum(x, 0.0) + jnp.log(1.0 + jnp.exp(-jnp.abs(x)))


def _shift_down(cur, before, s):
    if s == 0:
        return cur
    r = pltpu.roll(cur, s, 0)
    rb = pltpu.roll(before, s, 0)
    row = lax.broadcasted_iota(jnp.int32, before.shape, 0)
    head = jnp.where(row < s, rb, r[0:SUBLANE])
    return jnp.concatenate([head, r[SUBLANE:]], axis=0)


def _shift_up(cur, after, s):
    if s == 0:
        return cur
    n = cur.shape[0]
    r = pltpu.roll(cur, n - s, 0)
    ra = pltpu.roll(after, SUBLANE - s, 0)
    row = lax.broadcasted_iota(jnp.int32, after.shape, 0)
    tail = jnp.where(row >= SUBLANE - s, ra, r[n - SUBLANE:])
    return jnp.concatenate([r[:n - SUBLANE], tail], axis=0)


def _conv_fwd(cur, before, w):
    acc = cur * w[CONV_K - 1:CONV_K, :]
    for s in range(1, CONV_K):
        acc = acc + _shift_down(cur, before, s) * w[CONV_K - 1 - s:CONV_K - s, :]
    return acc


def _matmul_nt(a, bt, *, tm, name, carry=None):
    m, k = a.shape
    n = bt.shape[0]
    c_ins, c_in_specs, c_out_specs, c_outs, c_scratch = _carry_specs(carry)

    def body(*refs):
        a_ref, b_ref, o_ref = _carried(carry, refs, 2, 1, m // tm)
        o_ref[...] = _dot_nt(a_ref[...], b_ref[...])

    outs = pl.pallas_call(
        body, name=name, grid=(m // tm,),
        in_specs=[pl.BlockSpec((tm, k), lambda i: (i, 0)), pl.BlockSpec((n, k), lambda i: (0, 0))] + c_in_specs,
        out_specs=[pl.BlockSpec((tm, n), lambda i: (i, 0))] + c_out_specs,
        out_shape=[jax.ShapeDtypeStruct((m, n), F32)] + c_outs,
        scratch_shapes=c_scratch,
        compiler_params=_cp("arbitrary"))(a, bt, *c_ins)
    return outs[0], outs[1:]


def _dn_pre(h, conv_w, par, *, tt, name):
    t = h.shape[0]
    cw = 3 * A_WIDTH
    hb = tt // SUBLANE

    def body(pre_ref, halo_ref, bgi_ref, cw_ref, par_ref, q_ref, k_ref, v_ref, bg_ref, bgt_ref):
        i = pl.program_id(0)
        cur = pre_ref[...]
        before = jnp.where(i > 0, halo_ref[...], 0.0)
        s = _silu(_conv_fwd(cur, before, cw_ref[...]))
        for hd in range(A_HEADS):
            sl = slice(hd * LANE, (hd + 1) * LANE)
            tq = s[:, hd * LANE:(hd + 1) * LANE]
            q_ref[:, sl] = tq * (lax.rsqrt(jnp.sum(tq * tq, -1, keepdims=True) + L2_EPS) * (A_HEAD_DIM ** -0.5))
            tk = s[:, A_WIDTH + hd * LANE:A_WIDTH + (hd + 1) * LANE]
            k_ref[:, sl] = tk * lax.rsqrt(jnp.sum(tk * tk, -1, keepdims=True) + L2_EPS)
        v_ref[...] = s[:, 2 * A_WIDTH:]
        raw = bgi_ref[...]
        lane = lax.broadcasted_iota(jnp.int32, raw.shape, 1)
        is_a = (lane >= A_HEADS) & (lane < 2 * A_HEADS)
        g = jnp.where(is_a, -jnp.exp(par_ref[0:1, :]) * _softplus(raw + par_ref[1:2, :]), 0.0)
        gc = _dot_hi(_chunk_tri(tt, lower=True), g)
        bg = jnp.where(lane < A_HEADS, _sigmoid(raw), gc)
        bg_ref[...] = bg
        bgt_ref[...] = jnp.transpose(bg)[0:SUBLANE, :]

    wide = jax.ShapeDtypeStruct((t, A_WIDTH), F32)
    return pl.pallas_call(
        body, name=name, grid=(t // tt,),
        in_specs=[pl.BlockSpec((tt, cw), lambda i: (i, 0)),
                  pl.BlockSpec((SUBLANE, cw), lambda i: (jnp.maximum(i * hb - 1, 0), 0)),
                  pl.BlockSpec((tt, LANE), lambda i: (i, C_BG // LANE)),
                  pl.BlockSpec((CONV_K, cw), lambda i: (0, 0)),
                  pl.BlockSpec((SUBLANE, LANE), lambda i: (0, 0))],
        out_specs=[pl.BlockSpec((tt, A_WIDTH), lambda i: (i, 0))] * 3
        + [pl.BlockSpec((tt, LANE), lambda i: (i, 0)), pl.BlockSpec((SUBLANE, tt), lambda i: (0, i))],
        out_shape=[wide, wide, wide, jax.ShapeDtypeStruct((t, LANE), F32), jax.ShapeDtypeStruct((SUBLANE, t), F32)],
        compiler_params=_cp("parallel"))(h, h, h, conv_w, par)


def _chunk_tri(n, lower):
    r = lax.broadcasted_iota(jnp.int32, (n, n), 0)
    c = lax.broadcasted_iota(jnp.int32, (n, n), 1)
    shift = CHUNK.bit_length() - 1
    same = jnp.right_shift(r, shift) == jnp.right_shift(c, shift)
    return (same & ((c <= r) if lower else (c >= r))).astype(F32)


def _chunk_masks():
    r = lax.broadcasted_iota(jnp.int32, (CHUNK, CHUNK), 0)
    c = lax.broadcasted_iota(jnp.int32, (CHUNK, CHUNK), 1)
    return r >= c, r > c, r == c


def _split(a):
    hi = a.astype(BF16)
    return hi, (a - hi.astype(F32)).astype(BF16)


def _dot3(a, b):
    (ah, al), (bh, bl) = a, b
    d = lambda p, q: jnp.dot(p, q, preferred_element_type=F32)
    return d(ah, bh) + (d(ah, bl) + d(al, bh))


def _tri_inv_many(a_list, eye):
    d = lambda p, q: jnp.dot(p, q, preferred_element_type=F32)
    p = [(-a).astype(BF16) for a in a_list]
    tm = [eye - a for a in a_list]
    for _ in range(5):
        pf = [d(pi, pi) for pi in p]
        p = [x.astype(BF16) for x in pf]
        tm = [t + d(t.astype(BF16), pi) for t, pi in zip(tm, p)]
    ms = [_split(eye + a) for a in a_list]
    res = [eye - _dot3(m, _split(t)) for m, t in zip(ms, tm)]
    return [t + d(t.astype(BF16), r.astype(BF16)) for t, r in zip(tm, res)]


def _chunk_gates(bg_v, bgt_v, hd):
    return (bg_v[:, hd:hd + 1], bg_v[:, A_HEADS + hd:A_HEADS + hd + 1],
            None if bgt_v is None else bgt_v[A_HEADS + hd:A_HEADS + hd + 1, :])


WY_ROWS = 512
SCAN_ROWS = 128
WY_GROUP = 8


def _dn_wy(q, k, v, bg, bgt, *, name, carry=None):
    t = q.shape[0]
    rows = WY_ROWS

    c_ins, c_in_specs, c_out_specs, c_outs, c_scratch = _carry_specs(carry)

    def body(*refs):
        q_ref, k_ref, v_ref, bg_ref, bgt_ref, u_ref, w_ref, tm_ref, qk_ref = _carried(carry, refs, 5, 4, t // rows)
        causal, strict, diag = _chunk_masks()
        eye = diag.astype(F32)
        for c0 in range(0, rows // CHUNK, WY_GROUP):
            items = [(c, hd) for c in range(c0, c0 + WY_GROUP) for hd in range(A_HEADS)]
            rs = lambda c: slice(c * CHUNK, (c + 1) * CHUNK)
            sl = lambda hd: slice(hd * LANE, (hd + 1) * LANE)
            hs = lambda hd: slice(hd * CHUNK, (hd + 1) * CHUNK)
            gates = [_chunk_gates(bg_ref[rs(c), :], bgt_ref[:, rs(c)], hd) for c, hd in items]
            dms = [jnp.exp(jnp.where(causal, gcol - grow, NEG)) for _, gcol, grow in gates]
            kbs = [k_ref[rs(c), sl(hd)] * g[0] for (c, hd), g in zip(items, gates)]
            a_list = [jnp.where(strict, _dot_nt(kb, k_ref[rs(c), sl(hd)]) * dm, 0.0)
                      for (c, hd), kb, dm in zip(items, kbs, dms)]
            for (c, hd), dm in zip(items, dms):
                qk_ref[rs(c), hs(hd)] = jnp.where(
                    causal, _dot_nt(q_ref[rs(c), sl(hd)], k_ref[rs(c), sl(hd)]) * dm, 0.0)
            tms = _tri_inv_many(a_list, eye)
            for (c, hd), g, kb, tmat in zip(items, gates, kbs, tms):
                tm_ref[rs(c), hs(hd)] = tmat
                u_ref[rs(c), sl(hd)] = _dot(tmat, v_ref[rs(c), sl(hd)] * g[0])
                w_ref[rs(c), sl(hd)] = _dot(tmat, kb * jnp.exp(g[1])).astype(BF16)

    blk = pl.BlockSpec((rows, A_WIDTH), lambda i: (i, 0))
    half = pl.BlockSpec((rows, A_HEADS * CHUNK), lambda i: (i, 0))
    outs = pl.pallas_call(
        body, name=name, grid=(t // rows,),
        in_specs=[blk, blk, blk, pl.BlockSpec((rows, LANE), lambda i: (i, 0)),
                  pl.BlockSpec((SUBLANE, rows), lambda i: (0, i))] + c_in_specs,
        out_specs=[blk, blk, half, half] + c_out_specs,
        out_shape=[jax.ShapeDtypeStruct((t, A_WIDTH), F32), jax.ShapeDtypeStruct((t, A_WIDTH), BF16),
                   jax.ShapeDtypeStruct((t, A_HEADS * CHUNK), F32),
                   jax.ShapeDtypeStruct((t, A_HEADS * CHUNK), F32)] + c_outs,
        scratch_shapes=c_scratch,
        compiler_params=_cp("arbitrary"))(q, k, v, bg, bgt, *c_ins)
    return outs[:4], outs[4:]


def _dn_scan_fwd(q, k, u, w, qk, bg, *, name):
    t = q.shape[0]
    rows = SCAN_ROWS
    per = rows // CHUNK

    def body(q_ref, k_ref, u_ref, w_ref, qk_ref, bg_ref, o_ref, vn_ref, s_ref, state):
        @pl.when(pl.program_id(0) == 0)
        def _():
            state[...] = jnp.zeros_like(state)

        heads = range(A_HEADS)
        sl = lambda hd: slice(hd * LANE, (hd + 1) * LANE)
        s_cur = [state[hd] for hd in heads]
        for c in range(per):
            rs = slice(c * CHUNK, (c + 1) * CHUNK)
            bg_v = bg_ref[rs, :]
            gcols = [_chunk_gates(bg_v, None, hd)[1] for hd in heads]
            glasts = [gc[CHUNK - 1:CHUNK, :] for gc in gcols]
            for hd in heads:
                s_ref[c, hd] = s_cur[hd]
            vns = [u_ref[rs, sl(hd)] - _dot(w_ref[rs, sl(hd)], s_cur[hd]) for hd in heads]
            qss = [_dot(q_ref[rs, sl(hd)] * jnp.exp(gcols[hd]), s_cur[hd]) for hd in heads]
            s_cur = [s_cur[hd] * jnp.exp(glasts[hd])
                     + _dot_tn(k_ref[rs, sl(hd)] * jnp.exp(glasts[hd] - gcols[hd]), vns[hd]) for hd in heads]
            for hd in heads:
                vn_ref[rs, sl(hd)] = vns[hd]
                o_ref[rs, sl(hd)] = qss[hd] + _dot(qk_ref[rs, hd * CHUNK:(hd + 1) * CHUNK], vns[hd])
        for hd in heads:
            state[hd] = s_cur[hd]

    blk = pl.BlockSpec((rows, A_WIDTH), lambda i: (i, 0))
    half = pl.BlockSpec((rows, A_HEADS * CHUNK), lambda i: (i, 0))
    wide = jax.ShapeDtypeStruct((t, A_WIDTH), F32)
    return pl.pallas_call(
        body, name=name, grid=(t // rows,),
        in_specs=[blk, blk, blk, blk, half, pl.BlockSpec((rows, LANE), lambda i: (i, 0))],
        out_specs=[blk, blk, pl.BlockSpec((per, A_HEADS, LANE, LANE), lambda i: (i, 0, 0, 0))],
        out_shape=[wide, wide, jax.ShapeDtypeStruct((t // CHUNK, A_HEADS, LANE, LANE), F32)],
        scratch_shapes=[pltpu.VMEM((A_HEADS, LANE, LANE), F32)],
        compiler_params=_cp("arbitrary"))(q, k, u, w, qk, bg)


def _swa_neg_dist(n_blk):
    qi = lax.broadcasted_iota(jnp.int32, (BLOCK, 2 * BLOCK), 0)
    si = lax.broadcasted_iota(jnp.int32, (BLOCK, 2 * BLOCK), 1)
    dist = qi + BLOCK - si
    mask = (dist >= 0) & (dist < BLOCK) & ((si >= BLOCK) | (n_blk > 0))
    return jnp.where(mask, -dist.astype(F32), NEG)


def _stack_heads(ref, hk):
    return jnp.concatenate([ref[:, h * B_HEAD_DIM:(h + 1) * B_HEAD_DIM]
                            for h in range(hk * B_GROUP, (hk + 1) * B_GROUP)], axis=0)


def _swa_group_probs(q_ref, sk_ref, kband, vband, neg_dist):
    hks = range(B_KV_HEADS)
    heads = lambda hk: range(hk * B_GROUP, (hk + 1) * B_GROUP)
    ksl = lambda hk: slice(hk * B_HEAD_DIM, (hk + 1) * B_HEAD_DIM)
    ones = jnp.ones((2 * BLOCK, B_HEAD_DIM), BF16)
    qs = [_stack_heads(q_ref, hk) * (B_HEAD_DIM ** -0.5) for hk in hks]
    sink = [jnp.concatenate([jnp.broadcast_to(sk_ref[h:h + 1, 0:1], (BLOCK, 1)) for h in heads(hk)], axis=0)
            for hk in hks]
    s = [_dot_nt(qs[hk], kband[:, ksl(hk)]) + jnp.concatenate([ALIBI[h] * neg_dist for h in heads(hk)], axis=0)
         for hk in hks]
    m = [jnp.maximum(jnp.max(s[hk], axis=-1, keepdims=True), sink[hk]) for hk in hks]
    p = [jnp.exp(s[hk] - m[hk]) for hk in hks]
    oe = [jnp.dot(p[hk].astype(BF16), jnp.concatenate([vband[:, ksl(hk)].astype(BF16), ones], axis=1),
                  preferred_element_type=F32) for hk in hks]
    ps = [jnp.exp(sink[hk] - m[hk]) for hk in hks]
    inv = [1.0 / (oe[hk][:, B_HEAD_DIM:B_HEAD_DIM + 1] + ps[hk]) for hk in hks]
    return [(qs[hk], p[hk] * inv[hk], ps[hk] * inv[hk], oe[hk][:, :B_HEAD_DIM] * inv[hk]) for hk in hks]


def _swa_specs():
    qspec = lambda c0: pl.BlockSpec((BLOCK, B_WIDTH), lambda i: (i, c0 // B_WIDTH))
    cur = lambda c0: pl.BlockSpec((BLOCK, LANE), lambda i: (i, c0 // LANE))
    prev = lambda c0: pl.BlockSpec((BLOCK, LANE), lambda i: (jnp.maximum(i - 1, 0), c0 // LANE))
    return qspec, cur, prev


def _carried(carry, refs, n_in, n_out, steps):
    if carry is None:
        return refs
    ci, co = len(carry.ins), len(carry.outs)
    own = refs[:n_in] + refs[n_in + ci:n_in + ci + n_out] + refs[n_in + ci + n_out + co:len(refs) - 3]
    parts = refs[n_in:n_in + ci], refs[n_in + ci + n_out:n_in + ci + n_out + co], refs[len(refs) - 3:]

    @pl.when(pl.program_id(0) == 0)
    def _():
        carry.start(*parts)

    @pl.when(pl.program_id(0) == steps - 1)
    def _():
        carry.finish(*parts)

    return own


def _carry_specs(carry):
    if carry is None:
        return [], [], [], [], []
    return (list(carry.ins), [_ANY] * len(carry.ins), [_ANY] * len(carry.outs), list(carry.outs), carry.scratch())


def _swa_fwd(h, sinks_b, *, name, carry=None):
    t = h.shape[0]
    qspec, cur, prev = _swa_specs()
    c_ins, c_in_specs, c_out_specs, c_outs, c_scratch = _carry_specs(carry)

    def body(*refs):
        q_ref, kc_ref, kp_ref, vc_ref, vp_ref, sk_ref, o_ref = _carried(carry, refs, 6, 1, t // BLOCK)
        n_blk = pl.program_id(0)
        kband = jnp.concatenate([kp_ref[...], kc_ref[...]], axis=0)
        vband = jnp.concatenate([vp_ref[...], vc_ref[...]], axis=0)
        groups = _swa_group_probs(q_ref, sk_ref, kband, vband, _swa_neg_dist(n_blk))
        for hk, (_, _, _, o) in enumerate(groups):
            for g in range(B_GROUP):
                hq = hk * B_GROUP + g
                o_ref[:, hq * B_HEAD_DIM:(hq + 1) * B_HEAD_DIM] = o[g * BLOCK:(g + 1) * BLOCK]

    outs = pl.pallas_call(
        body, name=name, grid=(t // BLOCK,),
        in_specs=[qspec(C_QB), cur(C_KB), prev(C_KB), cur(C_VB), prev(C_VB),
                  pl.BlockSpec((B_Q_HEADS, LANE), lambda i: (0, 0))] + c_in_specs,
        out_specs=[pl.BlockSpec((BLOCK, B_WIDTH), lambda i: (i, 0))] + c_out_specs,
        out_shape=[jax.ShapeDtypeStruct((t, B_WIDTH), F32)] + c_outs,
        scratch_shapes=c_scratch,
        compiler_params=_cp("arbitrary"))(h, h, h, h, h, sinks_b, *c_ins)
    return outs[0], outs[1:]


def _rms_gate(o, za, nw):
    outs = []
    for hd in range(A_HEADS):
        oh = o[:, hd * LANE:(hd + 1) * LANE]
        r = lax.rsqrt(jnp.mean(oh * oh, -1, keepdims=True) + RMS_EPS)
        outs.append(oh * r * nw)
    return jnp.concatenate(outs, axis=1) * _silu(za)


def _out_ln(x, oa, ob, h, norm_w, w_out, ln_g, ln_b, *, tm, name):
    t = x.shape[0]

    def body(x_ref, oa_ref, ob_ref, za_ref, zb_ref, nw_ref, w_ref, g_ref, b_ref, xn_ref, mx_ref, r_ref):
        ya = _rms_gate(oa_ref[...], za_ref[...], nw_ref[...])
        yb = ob_ref[...] * _silu(zb_ref[...])
        mixed = jnp.concatenate([ya, yb], axis=1).astype(BF16)
        mx_ref[...] = mixed
        r = DEEPNORM_ALPHA * x_ref[...] + jnp.dot(mixed, w_ref[...], preferred_element_type=F32)
        r_ref[...] = r
        mu = jnp.mean(r, -1, keepdims=True)
        xc = r - mu
        var = jnp.mean(xc * xc, -1, keepdims=True)
        xn_ref[...] = xc * lax.rsqrt(var + LN_EPS) * g_ref[...] + b_ref[...]

    row = lambda w, c: pl.BlockSpec((tm, w), lambda i: (i, c))
    full = lambda a, b: pl.BlockSpec((a, b), lambda i: (0, 0))
    return pl.pallas_call(
        body, name=name, grid=(t // tm,),
        in_specs=[row(D_MODEL, 0), row(A_WIDTH, 0), row(B_WIDTH, 0), row(A_WIDTH, C_ZA // A_WIDTH),
                  row(B_WIDTH, C_ZB // B_WIDTH), full(1, LANE), full(D_MODEL, D_MODEL), full(1, D_MODEL), full(1, D_MODEL)],
        out_specs=[row(D_MODEL, 0), row(D_MODEL, 0), row(D_MODEL, 0)],
        out_shape=[jax.ShapeDtypeStruct((t, D_MODEL), F32), jax.ShapeDtypeStruct((t, D_MODEL), BF16),
                   jax.ShapeDtypeStruct((t, D_MODEL), F32)],
        compiler_params=_cp("parallel"))(x, oa, ob, h, h, norm_w, w_out, ln_g, ln_b)


def _layer_fwd(x, wt, conv_w, par, sinks_b, norm_w, w_out_bf, ln_g, ln_b, l, carries=None):
    carries = carries or {}
    h, got_in = _matmul_nt(x, wt, tm=512, name=f"in_proj_{l}", carry=carries.get("in_proj"))
    if callable(w_out_bf):
        w_out_bf = w_out_bf(got_in)
    q, k, v, bg, bgt = _dn_pre(h, conv_w, par, tt=512, name=f"dn_pre_{l}")
    (u, w, tmat, qk), got_wy = _dn_wy(q, k, v, bg, bgt, name=f"dn_wy_{l}", carry=carries.get("dn_wy"))
    oa, vn, s_all = _dn_scan_fwd(q, k, u, w, qk, bg, name=f"dn_scan_{l}")
    ob, got_swa = _swa_fwd(h, sinks_b, name=f"swa_fwd_{l}", carry=carries.get("swa"))
    xn, mixed, r = _out_ln(x, oa, ob, h, norm_w, w_out_bf, ln_g, ln_b, tm=256, name=f"out_ln_{l}")
    res = dict(x=x, h=h, q=q, k=k, v=v, bg=bg, bgt=bgt, w=w, tmat=tmat, qk=qk, vn=vn, oa=oa, s_all=s_all,
               mixed=mixed, r=r, w_out=w_out_bf)
    return xn, res, dict(in_proj=got_in, dn_wy=got_wy, swa=got_swa)


def _loss_grad(xn, target, *, tm, name):
    t = xn.shape[0]

    def body(x_ref, t_ref, d_ref, l_ref):
        @pl.when(pl.program_id(0) == 0)
        def _():
            l_ref[...] = jnp.zeros_like(l_ref)

        err = x_ref[...] - t_ref[...]
        d_ref[...] = err * (1.0 / D_MODEL)
        l_ref[...] += 0.5 / D_MODEL * jnp.sum(err * err)

    row = pl.BlockSpec((tm, D_MODEL), lambda i: (i, 0))
    return pl.pallas_call(
        body, name=name, grid=(t // tm,), in_specs=[row, row],
        out_specs=[row, pl.BlockSpec((SUBLANE, LANE), lambda i: (0, 0))],
        out_shape=[jax.ShapeDtypeStruct((t, D_MODEL), F32), jax.ShapeDtypeStruct((SUBLANE, LANE), F32)],
        compiler_params=_cp("arbitrary"))(xn, target)


def _ln_out_bwd(dxn, r, mixed, ln_g, w_out, *, tm, name):
    t = dxn.shape[0]

    def body(dxn_ref, r_ref, mx_ref, g_ref, w_ref, dr_ref, dm_ref, dw_ref, dg_ref, db_ref):
        @pl.when(pl.program_id(0) == 0)
        def _():
            dw_ref[...] = jnp.zeros_like(dw_ref)
            dg_ref[...] = jnp.zeros_like(dg_ref)
            db_ref[...] = jnp.zeros_like(db_ref)

        rr = r_ref[...]
        xc = rr - jnp.mean(rr, -1, keepdims=True)
        rstd = lax.rsqrt(jnp.mean(xc * xc, -1, keepdims=True) + LN_EPS)
        xhat = xc * rstd
        dxn_v = dxn_ref[...]
        dxh = dxn_v * g_ref[...]
        dr = rstd * (dxh - jnp.mean(dxh, -1, keepdims=True) - xhat * jnp.mean(dxh * xhat, -1, keepdims=True))
        dr_ref[...] = dr
        dg_ref[...] += jnp.sum(dxn_v * xhat, axis=0, keepdims=True)
        db_ref[...] += jnp.sum(dxn_v, axis=0, keepdims=True)
        drb = dr.astype(BF16)
        dm_ref[...] = _dot_nt(drb, w_ref[...])
        dw_ref[...] += _dot_tn(mx_ref[...], drb)

    row = pl.BlockSpec((tm, D_MODEL), lambda i: (i, 0))
    full = lambda a, b: pl.BlockSpec((a, b), lambda i: (0, 0))
    big = jax.ShapeDtypeStruct((t, D_MODEL), F32)
    vec = jax.ShapeDtypeStruct((1, D_MODEL), F32)
    return pl.pallas_call(
        body, name=name, grid=(t // tm,),
        in_specs=[row, row, row, full(1, D_MODEL), full(D_MODEL, D_MODEL)],
        out_specs=[row, row, full(D_MODEL, D_MODEL), full(1, D_MODEL), full(1, D_MODEL)],
        out_shape=[big, big, jax.ShapeDtypeStruct((D_MODEL, D_MODEL), F32), vec, vec],
        compiler_params=_cp("arbitrary"))(dxn, r, mixed, ln_g, w_out)


def _dn_post_bwd(dm, oa, h, norm_w, *, tm, name):
    t = oa.shape[0]

    def body(dy_ref, o_ref, za_ref, nw_ref, do_ref, dza_ref, dnw_ref):
        @pl.when(pl.program_id(0) == 0)
        def _():
            dnw_ref[...] = jnp.zeros_like(dnw_ref)

        nw = nw_ref[...]
        dnw = jnp.zeros_like(nw)
        for hd in range(A_HEADS):
            sl = slice(hd * LANE, (hd + 1) * LANE)
            oh, za, dy = o_ref[:, sl], za_ref[:, sl], dy_ref[:, sl]
            rs = lax.rsqrt(jnp.mean(oh * oh, -1, keepdims=True) + RMS_EPS)
            nrm = oh * rs
            dza_ref[:, sl] = dy * nrm * nw * _dsilu(za)
            dn = dy * _silu(za)
            dnw = dnw + jnp.sum(dn * nrm, axis=0, keepdims=True)
            dnn = dn * nw
            do_ref[:, sl] = rs * dnn - oh * (rs * rs * rs) * jnp.mean(dnn * oh, -1, keepdims=True)
        dnw_ref[...] += dnw

    row = lambda c: pl.BlockSpec((tm, A_WIDTH), lambda i: (i, c))
    wide = jax.ShapeDtypeStruct((t, A_WIDTH), F32)
    return pl.pallas_call(
        body, name=name, grid=(t // tm,),
        in_specs=[row(0), row(0), row(C_ZA // A_WIDTH), pl.BlockSpec((1, LANE), lambda i: (0, 0))],
        out_specs=[row(0), row(C_ZA // A_WIDTH), pl.BlockSpec((1, LANE), lambda i: (0, 0))],
        out_shape=[wide, jax.ShapeDtypeStruct((t, DH_MAIN), F32), jax.ShapeDtypeStruct((1, LANE), F32)],
        compiler_params=_cp("arbitrary"))(dm, oa, h, norm_w)


def _dn_scan_bwd(q, k, w, qk, bg, do, *, name):
    t = q.shape[0]
    rows = SCAN_ROWS
    per = rows // CHUNK
    n = t // rows

    def body(q_ref, k_ref, w_ref, qk_ref, bg_ref, do_ref, dvn_ref, ds_ref, dstate):
        @pl.when(pl.program_id(0) == 0)
        def _():
            dstate[...] = jnp.zeros_like(dstate)

        heads = range(A_HEADS)
        sl = lambda hd: slice(hd * LANE, (hd + 1) * LANE)
        ds_cur = [dstate[hd] for hd in heads]
        for c in reversed(range(per)):
            rs = slice(c * CHUNK, (c + 1) * CHUNK)
            bg_v = bg_ref[rs, :]
            gcols = [_chunk_gates(bg_v, None, hd)[1] for hd in heads]
            glasts = [gc[CHUNK - 1:CHUNK, :] for gc in gcols]
            for hd in heads:
                ds_ref[c, hd] = ds_cur[hd]
            pdo = [_dot_tn(qk_ref[rs, hd * CHUNK:(hd + 1) * CHUNK], do_ref[rs, sl(hd)]) for hd in heads]
            qdo = [_dot_tn(q_ref[rs, sl(hd)] * jnp.exp(gcols[hd]), do_ref[rs, sl(hd)]) for hd in heads]
            dvns = [pdo[hd] + _dot(k_ref[rs, sl(hd)] * jnp.exp(glasts[hd] - gcols[hd]), ds_cur[hd]) for hd in heads]
            ds_cur = [qdo[hd] + jnp.exp(glasts[hd]) * ds_cur[hd] - _dot_tn(w_ref[rs, sl(hd)], dvns[hd])
                      for hd in heads]
            for hd in heads:
                dvn_ref[rs, sl(hd)] = dvns[hd]
        for hd in heads:
            dstate[hd] = ds_cur[hd]

    blk = pl.BlockSpec((rows, A_WIDTH), lambda i: (n - 1 - i, 0))
    return pl.pallas_call(
        body, name=name, grid=(n,),
        in_specs=[blk, blk, blk, pl.BlockSpec((rows, A_HEADS * CHUNK), lambda i: (n - 1 - i, 0)),
                  pl.BlockSpec((rows, LANE), lambda i: (n - 1 - i, 0)), blk],
        out_specs=[blk, pl.BlockSpec((per, A_HEADS, LANE, LANE), lambda i: (n - 1 - i, 0, 0, 0))],
        out_shape=[jax.ShapeDtypeStruct((t, A_WIDTH), F32),
                   jax.ShapeDtypeStruct((t // CHUNK, A_HEADS, LANE, LANE), F32)],
        scratch_shapes=[pltpu.VMEM((A_HEADS, LANE, LANE), F32)],
        compiler_params=_cp("arbitrary"))(q, k, w, qk, bg, do)


def _dn_chunk_bwd(q, k, v, vn, tmat, qk, bg, bgt, s_all, ds_all, dvn, do, *, name):
    t = q.shape[0]
    rows = WY_ROWS
    per = rows // CHUNK

    def body(q_ref, k_ref, v_ref, vn_ref, tm_ref, qk_ref, bg_ref, bgt_ref, s_ref, ds_ref, dvn_ref, do_ref,
             dq_ref, dk_ref, dv_ref, dbg_ref, dbgt_ref):
        causal, strict, _ = _chunk_masks()
        lane = lax.broadcasted_iota(jnp.int32, (CHUNK, LANE), 1)
        rowi = lax.broadcasted_iota(jnp.int32, (CHUNK, 1), 0)
        sub = lax.broadcasted_iota(jnp.int32, (SUBLANE, CHUNK), 0)
        rs = lambda c: slice(c * CHUNK, (c + 1) * CHUNK)
        sl = lambda hd: slice(hd * LANE, (hd + 1) * LANE)
        hs = lambda hd: slice(hd * CHUNK, (hd + 1) * CHUNK)
        for c0 in range(0, per, WY_GROUP):
            items = [(c, hd) for c in range(c0, c0 + WY_GROUP) for hd in range(A_HEADS)]
            at = lambda ref: [ref[rs(c), sl(hd)] for c, hd in items]
            qs, ks, vs, dos, vns, dvns = at(q_ref), at(k_ref), at(v_ref), at(do_ref), at(vn_ref), at(dvn_ref)
            tmhs = [tm_ref[rs(c), hs(hd)] for c, hd in items]
            ps = [qk_ref[rs(c), hs(hd)] for c, hd in items]
            gates = [_chunk_gates(bg_ref[rs(c), :], bgt_ref[:, rs(c)], hd) for c, hd in items]
            betas = [g[0] for g in gates]
            gcols = [g[1] for g in gates]
            dmats = [jnp.exp(jnp.where(causal, g[1] - g[2], NEG)) for g in gates]
            es = [jnp.exp(gc) for gc in gcols]
            glasts = [gc[CHUNK - 1:CHUNK, :] for gc in gcols]
            eks = [jnp.exp(gl - gc) for gl, gc in zip(glasts, gcols)]
            kbs = [kh * b for kh, b in zip(ks, betas)]
            vbs = [vh * b for vh, b in zip(vs, betas)]
            kbes = [kb * e for kb, e in zip(kbs, es)]

            a_s = [jnp.where(strict, _dot_nt(kb, kh) * dm, 0.0) for kb, kh, dm in zip(kbs, ks, dmats)]
            dps = [jnp.where(causal, _dot_nt(doh, vnh), 0.0) for doh, vnh in zip(dos, vns)]
            dqds = [_dot_nt(doh, s_ref[c, hd]) for doh, (c, hd) in zip(dos, items)]
            dkds = [_dot_nt(vnh, ds_ref[c, hd]) for vnh, (c, hd) in zip(vns, items)]
            dws = [-_dot_nt(dvnh, s_ref[c, hd]) for dvnh, (c, hd) in zip(dvns, items)]
            dvbs = [_dot_tn(tmh, dvnh) for tmh, dvnh in zip(tmhs, dvns)]
            dgts = [jnp.sum(s_ref[c, hd] * ds_ref[c, hd], keepdims=True) for c, hd in items]
            dts = [_dot_nt(dvnh, vb) + _dot_nt(dw, kbe) for dvnh, vb, dw, kbe in zip(dvns, vbs, dws, kbes)]
            dkbes = [_dot_tn(tmh, dw) for tmh, dw in zip(tmhs, dws)]
            xs = [_dot_nt(dt, tmh) for dt, tmh in zip(dts, tmhs)]
            das = [jnp.where(strict, -_dot_tn(tmh, x), 0.0) for tmh, x in zip(tmhs, xs)]
            dmas = [da * dm for da, dm in zip(das, dmats)]
            dmps = [dp * dm for dp, dm in zip(dps, dmats)]
            dkbs = [_dot(dma, kh) + dkbe * e for dma, kh, dkbe, e in zip(dmas, ks, dkbes, es)]
            for i, (c, hd) in enumerate(items):
                dq_ref[rs(c), sl(hd)] = _dot(dmps[i], ks[i]) + dqds[i] * es[i]
                dk_ref[rs(c), sl(hd)] = (_dot_tn(dmas[i], kbs[i]) + _dot_tn(dmps[i], qs[i]) + dkds[i] * eks[i]
                                         + dkbs[i] * betas[i])
                dv_ref[rs(c), sl(hd)] = dvbs[i] * betas[i]
            for c in range(c0, c0 + WY_GROUP):
                acc = jnp.zeros((CHUNK, LANE), F32)
                acc_t = jnp.zeros((SUBLANE, CHUNK), F32)
                for i, (ci, hd) in enumerate(items):
                    if ci != c:
                        continue
                    gmat = das[i] * a_s[i] + dps[i] * ps[i]
                    rk = jnp.sum(dkds[i] * ks[i], -1, keepdims=True) * eks[i]
                    de = (jnp.sum(dqds[i] * qs[i], -1, keepdims=True)
                          + jnp.sum(dkbes[i] * kbs[i], -1, keepdims=True))
                    dglast = jnp.sum(rk, keepdims=True) + dgts[i] * jnp.exp(glasts[i])
                    dgc = (jnp.sum(gmat, -1, keepdims=True) + de * es[i] - rk
                           + jnp.where(rowi == CHUNK - 1, dglast, 0.0))
                    dbeta = (jnp.sum(dkbs[i] * ks[i], -1, keepdims=True)
                             + jnp.sum(dvbs[i] * vs[i], -1, keepdims=True))
                    acc = acc + jnp.where(lane == hd, dbeta, 0.0) + jnp.where(lane == A_HEADS + hd, dgc, 0.0)
                    acc_t = acc_t + jnp.where(sub == A_HEADS + hd, -jnp.sum(gmat, axis=0, keepdims=True), 0.0)
                dbg_ref[rs(c), :] = acc
                dbgt_ref[:, rs(c)] = acc_t

    blk = pl.BlockSpec((rows, A_WIDTH), lambda i: (i, 0))
    half = pl.BlockSpec((rows, A_HEADS * CHUNK), lambda i: (i, 0))
    col = pl.BlockSpec((rows, LANE), lambda i: (i, 0))
    rowf = pl.BlockSpec((SUBLANE, rows), lambda i: (0, i))
    st = pl.BlockSpec((per, A_HEADS, LANE, LANE), lambda i: (i, 0, 0, 0))
    wide = jax.ShapeDtypeStruct((t, A_WIDTH), F32)
    return pl.pallas_call(
        body, name=name, grid=(t // rows,),
        in_specs=[blk, blk, blk, blk, half, half, col, rowf, st, st, blk, blk],
        out_specs=[blk, blk, blk, col, rowf],
        out_shape=[wide, wide, wide, jax.ShapeDtypeStruct((t, LANE), F32), jax.ShapeDtypeStruct((SUBLANE, t), F32)],
        compiler_params=_cp("parallel"))(q, k, v, vn, tmat, qk, bg, bgt, s_all, ds_all, dvn, do)


def _dn_pre_bwd(h, conv_w, par, dq, dk, dv, dbg, dbgt, *, tt, name):
    t = h.shape[0]
    cw = 3 * A_WIDTH
    hb = tt // SUBLANE

    def body(pre_ref, halo_ref, bgi_ref, cw_ref, par_ref, dq_ref, dk_ref, dv_ref, dbg_ref, dbgt_ref,
             dc_ref, dbgi_ref, dpar_ref):
        i = pl.program_id(0)

        @pl.when(i == 0)
        def _():
            dpar_ref[...] = jnp.zeros_like(dpar_ref)

        cur = pre_ref[...]
        before = jnp.where(i > 0, halo_ref[...], 0.0)
        c = _conv_fwd(cur, before, cw_ref[...])
        s = _silu(c)
        ds = _dsilu(c)
        for hd in range(A_HEADS):
            sl = slice(hd * LANE, (hd + 1) * LANE)
            for base, d_ref, scale in ((0, dq_ref, A_HEAD_DIM ** -0.5), (A_WIDTH, dk_ref, 1.0)):
                csl = slice(base + hd * LANE, base + (hd + 1) * LANE)
                tq = s[:, base + hd * LANE:base + (hd + 1) * LANE]
                dy = d_ref[:, sl]
                rq = lax.rsqrt(jnp.sum(tq * tq, -1, keepdims=True) + L2_EPS)
                dtq = scale * (rq * dy - tq * (rq * rq * rq) * jnp.sum(dy * tq, -1, keepdims=True))
                dc_ref[:, csl] = dtq * ds[:, base + hd * LANE:base + (hd + 1) * LANE]
        dc_ref[:, 2 * A_WIDTH:] = dv_ref[...] * ds[:, 2 * A_WIDTH:]
        raw = bgi_ref[...]
        lane = lax.broadcasted_iota(jnp.int32, raw.shape, 1)
        is_b = lane < A_HEADS
        is_a = (lane >= A_HEADS) & (lane < 2 * A_HEADS)
        rows_t = jnp.concatenate([dbgt_ref[...], jnp.zeros((LANE - SUBLANE, tt), F32)], axis=0)
        dbg_v = dbg_ref[...] + jnp.where(is_a, jnp.transpose(rows_t), 0.0)
        dbg_v = jnp.where(is_a, _dot_hi(_chunk_tri(tt, lower=False), jnp.where(is_a, dbg_v, 0.0)), dbg_v)
        beta = _sigmoid(raw)
        z = raw + par_ref[1:2, :]
        neg_ea = -jnp.exp(par_ref[0:1, :])
        g = neg_ea * _softplus(z)
        da = dbg_v * neg_ea * _sigmoid(z)
        dbgi_ref[...] = jnp.where(is_b, dbg_v * beta * (1.0 - beta), jnp.where(is_a, da, 0.0))
        dpar_ref[0:1, :] += jnp.sum(jnp.where(is_a, dbg_v * g, 0.0), axis=0, keepdims=True)
        dpar_ref[1:2, :] += jnp.sum(jnp.where(is_a, da, 0.0), axis=0, keepdims=True)

    wide = pl.BlockSpec((tt, A_WIDTH), lambda i: (i, 0))
    return pl.pallas_call(
        body, name=name, grid=(t // tt,),
        in_specs=[pl.BlockSpec((tt, cw), lambda i: (i, 0)),
                  pl.BlockSpec((SUBLANE, cw), lambda i: (jnp.maximum(i * hb - 1, 0), 0)),
                  pl.BlockSpec((tt, LANE), lambda i: (i, C_BG // LANE)),
                  pl.BlockSpec((CONV_K, cw), lambda i: (0, 0)),
                  pl.BlockSpec((SUBLANE, LANE), lambda i: (0, 0)),
                  wide, wide, wide, pl.BlockSpec((tt, LANE), lambda i: (i, 0)),
                  pl.BlockSpec((SUBLANE, tt), lambda i: (0, i))],
        out_specs=[pl.BlockSpec((tt, cw), lambda i: (i, 0)), pl.BlockSpec((tt, LANE), lambda i: (i, 0)),
                   pl.BlockSpec((SUBLANE, LANE), lambda i: (0, 0))],
        out_shape=[jax.ShapeDtypeStruct((t, cw), F32), jax.ShapeDtypeStruct((t, LANE), F32),
                   jax.ShapeDtypeStruct((SUBLANE, LANE), F32)],
        compiler_params=_cp("arbitrary"))(h, h, h, conv_w, par, dq, dk, dv, dbg, dbgt)


def _conv_bwd(dc, h, conv_w, dh, *, tt, name):
    t = dc.shape[0]
    cw = 3 * A_WIDTH
    hb = tt // SUBLANE
    nb = t // tt

    def body(dc_ref, after_ref, pre_ref, before_ref, cw_ref, dh_in_ref, dpre_ref, dcw_ref):
        i = pl.program_id(0)

        @pl.when(i == 0)
        def _():
            dcw_ref[...] = jnp.zeros_like(dcw_ref)

        dcv = dc_ref[...]
        after = jnp.where(i < nb - 1, after_ref[...], 0.0)
        cur = pre_ref[...]
        before = jnp.where(i > 0, before_ref[...], 0.0)
        w = cw_ref[...]
        acc = dcv * w[CONV_K - 1:CONV_K, :]
        dcw_ref[CONV_K - 1:CONV_K, :] += jnp.sum(dcv * cur, axis=0, keepdims=True)
        for s in range(1, CONV_K):
            j = CONV_K - 1 - s
            acc = acc + _shift_up(dcv, after, s) * w[j:j + 1, :]
            dcw_ref[j:j + 1, :] += jnp.sum(dcv * _shift_down(cur, before, s), axis=0, keepdims=True)
        dpre_ref[...] = acc

    return pl.pallas_call(
        body, name=name, grid=(nb,),
        in_specs=[pl.BlockSpec((tt, cw), lambda i: (i, 0)),
                  pl.BlockSpec((SUBLANE, cw), lambda i: (jnp.minimum((i + 1) * hb, t // SUBLANE - 1), 0)),
                  pl.BlockSpec((tt, cw), lambda i: (i, 0)),
                  pl.BlockSpec((SUBLANE, cw), lambda i: (jnp.maximum(i * hb - 1, 0), 0)),
                  pl.BlockSpec((CONV_K, cw), lambda i: (0, 0)), _ANY],
        out_specs=[pl.BlockSpec((tt, cw), lambda i: (i, 0)), pl.BlockSpec((SUBLANE, cw), lambda i: (0, 0))],
        out_shape=[jax.ShapeDtypeStruct(dh.shape, F32), jax.ShapeDtypeStruct((SUBLANE, cw), F32)],
        input_output_aliases={5: 0},
        compiler_params=_cp("arbitrary"))(dc, dc, h, h, conv_w, dh)


def _swa_bwd(h, dm, sinks_b, dh, *, name, carry=None):
    t = h.shape[0]
    qspec, cur, prev = _swa_specs()
    c_ins, c_in_specs, c_out_specs, c_outs, c_scratch = _carry_specs(carry)

    def body(*refs):
        (q_ref, kc_ref, kp_ref, vc_ref, vp_ref, zb_ref, dy_ref, sk_ref, dh_in_ref,
         dqz_ref, dk_ref, dv_ref, dsk_ref) = _carried(carry, refs, 9, 4, t // BLOCK)
        n_blk = pl.program_id(0)

        @pl.when(n_blk == 0)
        def _():
            dk_ref[...] = jnp.zeros_like(dk_ref)
            dv_ref[...] = jnp.zeros_like(dv_ref)
            dsk_ref[...] = jnp.zeros_like(dsk_ref)

        kband = jnp.concatenate([kp_ref[...], kc_ref[...]], axis=0)
        vband = jnp.concatenate([vp_ref[...], vc_ref[...]], axis=0)
        scale = B_HEAD_DIM ** -0.5
        hks = range(B_KV_HEADS)
        ksl = lambda hk: slice(hk * B_HEAD_DIM, (hk + 1) * B_HEAD_DIM)
        groups = _swa_group_probs(q_ref, sk_ref, kband, vband, _swa_neg_dist(n_blk))
        zbs = [_stack_heads(zb_ref, hk) for hk in hks]
        dys = [_stack_heads(dy_ref, hk) for hk in hks]
        dos = [dys[hk] * _silu(zbs[hk]) for hk in hks]
        deltas = [jnp.sum(dos[hk] * groups[hk][3], -1, keepdims=True) for hk in hks]
        dss = [groups[hk][1] * (_dot_nt(dos[hk], vband[:, ksl(hk)]) - deltas[hk]) for hk in hks]
        dqs = [_dot(dss[hk], kband[:, ksl(hk)]) * scale for hk in hks]
        dk_acc = [_dot_tn(dss[hk], groups[hk][0]) for hk in hks]
        dv_acc = [_dot_tn(groups[hk][1], dos[hk]) for hk in hks]
        for hk in hks:
            dzb = dys[hk] * groups[hk][3] * _dsilu(zbs[hk])
            dsink = groups[hk][2] * deltas[hk]
            for g in range(B_GROUP):
                hq = hk * B_GROUP + g
                rows = slice(g * BLOCK, (g + 1) * BLOCK)
                qsl = slice(hq * B_HEAD_DIM, (hq + 1) * B_HEAD_DIM)
                dqz_ref[:, qsl] = dqs[hk][rows]
                dqz_ref[:, B_WIDTH + hq * B_HEAD_DIM:B_WIDTH + (hq + 1) * B_HEAD_DIM] = dzb[rows]
                dsk_ref[hq:hq + 1, :] += -jnp.sum(dsink[rows], keepdims=True)
        dkb = jnp.concatenate(dk_acc, axis=1)
        dvb = jnp.concatenate(dv_acc, axis=1)
        at_cur = pl.ds(pl.multiple_of(n_blk * BLOCK, BLOCK), BLOCK)
        at_prev = pl.ds(pl.multiple_of(jnp.maximum(n_blk - 1, 0) * BLOCK, BLOCK), BLOCK)
        dk_ref[at_prev, :] += dkb[:BLOCK]
        dv_ref[at_prev, :] += dvb[:BLOCK]
        dk_ref[at_cur, :] += dkb[BLOCK:]
        dv_ref[at_cur, :] += dvb[BLOCK:]

    narrow = jax.ShapeDtypeStruct((t, B_KV_WIDTH), F32)
    res = lambda a, b: pl.BlockSpec((a, b), lambda i: (0, 0))
    outs = pl.pallas_call(
        body, name=name, grid=(t // BLOCK,),
        in_specs=[qspec(C_QB), cur(C_KB), prev(C_KB), cur(C_VB), prev(C_VB), qspec(C_ZB),
                  pl.BlockSpec((BLOCK, B_WIDTH), lambda i: (i, 1)), res(B_Q_HEADS, LANE), _ANY] + c_in_specs,
        out_specs=[pl.BlockSpec((BLOCK, 2 * B_WIDTH), lambda i: (i, C_QB // (2 * B_WIDTH))),
                   res(t, B_KV_WIDTH), res(t, B_KV_WIDTH), res(B_Q_HEADS, LANE)] + c_out_specs,
        out_shape=[jax.ShapeDtypeStruct(dh.shape, F32), narrow, narrow,
                   jax.ShapeDtypeStruct((B_Q_HEADS, LANE), F32)] + c_outs,
        scratch_shapes=c_scratch,
        input_output_aliases={8: 0},
        compiler_params=_cp("arbitrary"))(h, h, h, h, h, h, dm, sinks_b, dh, *c_ins)
    return outs[:4], outs[4:]


def _matmul_tn(a, b, *, tk, tm, name):
    t, m = a.shape
    n = b.shape[1]

    def body(a_ref, b_ref, o_ref):
        @pl.when(pl.program_id(1) == 0)
        def _():
            o_ref[...] = jnp.zeros_like(o_ref)

        o_ref[...] += _dot_tn(a_ref[...], b_ref[...])

    return pl.pallas_call(
        body, name=name, grid=(m // tm, t // tk),
        in_specs=[pl.BlockSpec((tk, tm), lambda j, kk: (kk, j)), pl.BlockSpec((tk, n), lambda j, kk: (kk, 0))],
        out_specs=pl.BlockSpec((tm, n), lambda j, kk: (j, 0)),
        out_shape=jax.ShapeDtypeStruct((m, n), F32),
        compiler_params=_cp("parallel", "arbitrary"))(a, b)


def _in_proj_dx(dh_main, dh_tail, wt, dr, *, tm, name, carry=None):
    t, n_main = dh_main.shape
    n_tail = dh_tail.shape[1]
    c_ins, c_in_specs, c_out_specs, c_outs, c_scratch = _carry_specs(carry)

    def body(*refs):
        a_ref, t_ref, wa_ref, wt_ref, r_ref, o_ref = _carried(carry, refs, 5, 1, t // tm)
        o_ref[...] = _dot(a_ref[...], wa_ref[...]) + _dot(t_ref[...], wt_ref[...]) + DEEPNORM_ALPHA * r_ref[...]

    row = lambda w: pl.BlockSpec((tm, w), lambda i: (i, 0))
    outs = pl.pallas_call(
        body, name=name, grid=(t // tm,),
        in_specs=[row(n_main), row(n_tail), pl.BlockSpec((n_main, D_MODEL), lambda i: (0, 0)),
                  pl.BlockSpec((n_tail, D_MODEL), lambda i: (n_main // n_tail, 0)), row(D_MODEL)] + c_in_specs,
        out_specs=[row(D_MODEL)] + c_out_specs,
        out_shape=[jax.ShapeDtypeStruct((t, D_MODEL), F32)] + c_outs,
        scratch_shapes=c_scratch,
        compiler_params=_cp("arbitrary"))(dh_main, dh_tail, wt, wt, dr, *c_ins)
    return outs[0], outs[1:]


def _layer_bwd(dxn, res, wt, conv_w, par, sinks_b, norm_w, w_out_bf, ln_g, l, carry=None, carry_dx=None):
    w_out_bf = res["w_out"]
    dr, dm, dw_out, dln_g, dln_b = _ln_out_bwd(dxn, res["r"], res["mixed"], ln_g, w_out_bf, tm=256, name=f"ln_out_bwd_{l}")
    h = res["h"]
    do, dh, dnw = _dn_post_bwd(dm, res["oa"], h, norm_w, tm=512, name=f"dn_post_bwd_{l}")
    dvn, ds_all = _dn_scan_bwd(res["q"], res["k"], res["w"], res["qk"], res["bg"], do, name=f"dn_scan_bwd_{l}")
    dq, dk, dv, dbg, dbgt = _dn_chunk_bwd(res["q"], res["k"], res["v"], res["vn"], res["tmat"], res["qk"], res["bg"],
                                          res["bgt"], res["s_all"], ds_all, dvn, do, name=f"dn_chunk_bwd_{l}")
    dc, dbgi, dpar = _dn_pre_bwd(h, conv_w, par, dq, dk, dv, dbg, dbgt, tt=512, name=f"dn_pre_bwd_{l}")
    dh, dcw = _conv_bwd(dc, h, conv_w, dh, tt=512, name=f"conv_bwd_{l}")
    (dh, dkb, dvb, dsk), carried = _swa_bwd(h, dm, sinks_b, dh, name=f"swa_bwd_{l}", carry=carry)
    dh_tail = jnp.concatenate([dkb, dvb, dbgi], axis=1)
    dwt_main = _matmul_tn(dh, res["x"], tk=512, tm=768, name=f"in_proj_dw_{l}")
    dwt_tail = _matmul_tn(dh_tail, res["x"], tk=512, tm=P_COLS - DH_MAIN, name=f"in_proj_dw_tail_{l}")
    grads = dict(w_in=(dwt_main, dwt_tail), conv_w=dcw[:CONV_K], a_log=dpar[0, A_HEADS:2 * A_HEADS],
                 dt_bias=dpar[1, A_HEADS:2 * A_HEADS], norm_w=dnw[0], sinks=dsk[:, 0], w_out=dw_out,
                 ln_g=dln_g[0], ln_b=dln_b[0])
    dx, carried_dx = _in_proj_dx(dh, dh_tail, wt, dr, tm=256, name=f"in_proj_dx_{l}",
                                 carry=None if carry_dx is None else carry_dx(grads))
    return dx, grads, carried, carried_dx


def _layer_args(wt, conv_w, a_log, dt_bias, sinks, norm_w, w_out_bf):
    return (wt, conv_w, _gate_params(a_log, dt_bias), jnp.broadcast_to(sinks[:, None], (B_Q_HEADS, LANE)),
            norm_w[None], w_out_bf)


def _local_step(x, target, args0, args1, ln_g, ln_b, gathers=None, reduce1=None, reduce0=None):
    assert DEPTH == 2
    x1, res0, got = _layer_fwd(x, *args0, ln_g[0][None], ln_b[0][None], 0, carries=gathers)
    if gathers is not None:
        args1 = args1(got)
    x2, res1, _ = _layer_fwd(x1, *args1, ln_g[1][None], ln_b[1][None], 1)
    dx, loss_tile = _loss_grad(x2, target, tm=512, name="loss_grad")
    dx, grads1, _, _ = _layer_bwd(dx, res1, *args1, ln_g[1][None], 1)
    carry = None if reduce1 is None else reduce1(grads1)
    carry_dx = None if reduce0 is None else (lambda grads0: reduce0(grads0, grads1))
    dx, grads0, landed1, landed0 = _layer_bwd(dx, res0, *args0, ln_g[0][None], 0, carry=carry, carry_dx=carry_dx)
    return loss_tile, dx, [grads0, grads1], landed1, landed0


_ANY = pl.BlockSpec(memory_space=pl.ANY)
_MESH = pl.DeviceIdType.MESH


HALF = D_MODEL // 2


class _Exchange:
    def __init__(self, ins, outs, n_remote, n_local, plan):
        self.ins, self.outs, self.n_remote, self.n_local, self.plan = tuple(ins), tuple(outs), n_remote, n_local, plan

    def scratch(self):
        return [pltpu.SemaphoreType.DMA((self.n_remote,)), pltpu.SemaphoreType.DMA((self.n_remote,)),
                pltpu.SemaphoreType.DMA((max(self.n_local, 1),))]

    def _copies(self, in_refs, out_refs, sems, arriving):
        send_sems, recv_sems, local_sems = sems
        local, sends, recvs = self.plan(in_refs, out_refs)
        loc = [pltpu.make_async_copy(s, d, local_sems.at[i]) for i, (s, d) in enumerate(local)]
        rem = [pltpu.make_async_remote_copy(src_ref=s, dst_ref=recvs[i] if arriving else d, send_sem=send_sems.at[i],
                                            recv_sem=recv_sems.at[i], device_id=peer, device_id_type=_MESH)
               for i, (s, d, peer) in enumerate(sends)]
        return loc, rem

    def start(self, in_refs, out_refs, sems):
        loc, rem = self._copies(in_refs, out_refs, sems, arriving=False)
        for cp in loc + rem:
            cp.start()

    def finish(self, in_refs, out_refs, sems):
        loc, rem = self._copies(in_refs, out_refs, sems, arriving=True)
        for cp in rem:
            cp.wait_recv()
        for cp in rem:
            cp.wait_send()
        for cp in loc:
            cp.wait()


def _run_exchange(ex, *, name):
    n_in, n_out = len(ex.ins), len(ex.outs)

    def body(*refs):
        parts = refs[:n_in], refs[n_in:n_in + n_out], refs[n_in + n_out:]
        ex.start(*parts)
        ex.finish(*parts)

    return pl.pallas_call(body, name=name, in_specs=[_ANY] * n_in, out_specs=[_ANY] * n_out, out_shape=list(ex.outs),
                          scratch_shapes=ex.scratch())(*ex.ins)


def _place():
    x, y, c = lax.axis_index("x"), lax.axis_index("y"), lax.axis_index("c")
    return x, y, c, [(1 - x, y), (x, 1 - y), (1 - x, 1 - y)]


def _gather_exchange(arrays):
    n = len(arrays)

    def plan(src, dst):
        x, y, c, chips = _place()
        me = 2 * x + y
        local = [(src[k], dst[k].at[me]) for k in range(n)]
        sends = [(src[k], dst[k].at[me], (px, py, c)) for k in range(n) for px, py in chips]
        recvs = [dst[k].at[2 * px + py] for k in range(n) for px, py in chips]
        return local, sends, recvs

    return _Exchange(arrays, [jax.ShapeDtypeStruct((N_SHARD,) + a.shape, a.dtype) for a in arrays], 3 * n, n, plan)


def _gather_two_level(pack, conv_w, *, name):
    rows = pack.shape[0]
    part_rows = rows // 2

    def body(pack_ref, conv_ref, land_ref, conv_land_ref, send1, recv1, send2, recv2, csend, crecv, local_sems):
        x, y, c, chips = _place()
        me = 2 * x + y
        sibling = (x, y, 1 - c)
        part = lambda core: pl.ds(pl.multiple_of(core * part_rows, 16), part_rows)
        remote = lambda src, dst, ss, rs, to: pltpu.make_async_remote_copy(
            src_ref=src, dst_ref=dst, send_sem=ss, recv_sem=rs, device_id=to, device_id_type=_MESH)
        local = [pltpu.make_async_copy(pack_ref, land_ref.at[me], local_sems.at[0]),
                 pltpu.make_async_copy(conv_ref, conv_land_ref.at[me], local_sems.at[1])]
        for cp in local:
            cp.start()
        first = [remote(pack_ref.at[part(c)], land_ref.at[me, part(c)], send1.at[j], recv1.at[j], (px, py, c))
                 for j, (px, py) in enumerate(chips)]
        convs = [remote(conv_ref, conv_land_ref.at[me], csend.at[j], crecv.at[j], (px, py, c))
                 for j, (px, py) in enumerate(chips)]
        for cp in first + convs:
            cp.start()
        passed = []
        for j, (px, py) in enumerate(chips):
            slot = 2 * px + py
            remote(pack_ref.at[part(c)], land_ref.at[slot, part(c)], send1.at[j], recv1.at[j], (px, py, c)).wait_recv()
            cp = remote(land_ref.at[slot, part(c)], land_ref.at[slot, part(c)], send2.at[j], recv2.at[j], sibling)
            cp.start()
            passed.append(cp)
        for j, (px, py) in enumerate(chips):
            slot = 2 * px + py
            remote(land_ref.at[slot, part(1 - c)], land_ref.at[slot, part(1 - c)], send2.at[j], recv2.at[j],
                   sibling).wait_recv()
            remote(conv_ref, conv_land_ref.at[slot], csend.at[j], crecv.at[j], (px, py, c)).wait_recv()
        for cp in first + convs + passed:
            cp.wait_send()
        for cp in local:
            cp.wait()

    sems = [pltpu.SemaphoreType.DMA((3,))] * 6 + [pltpu.SemaphoreType.DMA((2,))]
    return pl.pallas_call(
        body, name=name, in_specs=[_ANY, _ANY], out_specs=[_ANY, _ANY],
        out_shape=[jax.ShapeDtypeStruct((N_SHARD,) + pack.shape, pack.dtype),
                   jax.ShapeDtypeStruct((N_SHARD,) + conv_w.shape, conv_w.dtype)],
        scratch_shapes=sems)(pack, conv_w)


def _half(core):
    return pl.ds(pl.multiple_of(core * HALF, HALF), HALF)


def _reduce_scatter_exchange(g, small=None):
    ins = [g] if small is None else [g, small]
    outs = [jax.ShapeDtypeStruct((7,) + g.shape[1:2] + (HALF,), g.dtype)]
    if small is not None:
        outs.append(jax.ShapeDtypeStruct((8,) + small.shape, small.dtype))

    def plan(src, dst):
        x, y, c, chips = _place()
        me = 2 * x + y
        peers = [(px, py, c if t == 0 else 1 - c) for px, py in chips for t in (0, 1)] + [(x, y, 1 - c)]
        sends = [(src[0].at[2 * px + py, :, _half(pc)], dst[0].at[k], (px, py, pc)) for k, (px, py, pc) in enumerate(peers)]
        recvs = [dst[0].at[k] for k in range(7)]
        local = []
        if small is not None:
            mine = 4 * x + 2 * y + c
            local = [(src[1], dst[1].at[mine])]
            sends += [(src[1], dst[1].at[mine], peer) for peer in peers]
            recvs += [dst[1].at[4 * px + 2 * py + pc] for px, py, pc in peers]
        return local, sends, recvs

    return _Exchange(ins, outs, 7 * len(ins), len(ins) - 1, plan)


def _pair_window_exchange(g):
    def plan(src, dst):
        x, y, c, _ = _place()
        return [], [(src[0].at[:, :, _half(1 - c)], dst[0], (x, y, 1 - c))], [dst[0]]

    return _Exchange([g], [jax.ShapeDtypeStruct(g.shape[:2] + (HALF,), g.dtype)], 1, 0, plan)


def _chip_scatter_exchange(p, small):
    def plan(src, dst):
        x, y, c, chips = _place()
        mine = 4 * x + 2 * y + c
        peers = [(px, py, c if t == 0 else 1 - c) for px, py in chips for t in (0, 1)] + [(x, y, 1 - c)]
        sends = [(src[0].at[2 * px + py], dst[0].at[j], (px, py, c)) for j, (px, py) in enumerate(chips)]
        recvs = [dst[0].at[j] for j in range(3)]
        sends += [(src[1], dst[1].at[mine], peer) for peer in peers]
        recvs += [dst[1].at[4 * px + 2 * py + pc] for px, py, pc in peers]
        return [(src[1], dst[1].at[mine])], sends, recvs

    outs = [jax.ShapeDtypeStruct((3,) + p.shape[1:], p.dtype), jax.ShapeDtypeStruct((8,) + small.shape, small.dtype)]
    return _Exchange([p, small], outs, 10, 1, plan)


def _share_exchange(arrays):
    n = len(arrays)

    def plan(src, dst):
        x, y, c, _ = _place()
        return [], [(src[k], dst[k], (x, y, 1 - c)) for k in range(n)], [dst[k] for k in range(n)]

    return _Exchange(arrays, [jax.ShapeDtypeStruct(a.shape, a.dtype) for a in arrays], n, 0, plan)


def _sum_scatter(g, land, me, core, *, tc, name):
    rows = g.shape[1]
    per = HALF // tc

    def body(where_ref, g_ref, land_ref, o_ref):
        acc = g_ref[...]
        for k in range(7):
            acc = acc + land_ref[k].astype(F32)
        o_ref[...] = acc

    return pl.pallas_call(
        body, name=name, out_shape=jax.ShapeDtypeStruct((rows, HALF), F32), compiler_params=_cp("parallel"),
        grid_spec=pltpu.PrefetchScalarGridSpec(
            num_scalar_prefetch=1, grid=(per,),
            in_specs=[pl.BlockSpec((None, rows, tc), lambda i, w: (w[0], 0, w[1] * per + i)),
                      pl.BlockSpec((7, rows, tc), lambda i, w: (0, 0, i))],
            out_specs=pl.BlockSpec((rows, tc), lambda i, w: (0, i))))(
        jnp.stack([me, core]).astype(jnp.int32), g, land)


def _pair_add(g, land, core, *, name):
    n, rows, _ = g.shape

    def body(core_ref, g_ref, land_ref, o_ref):
        o_ref[...] = (g_ref[...].astype(F32) + land_ref[...].astype(F32)).astype(o_ref.dtype)

    blk = pl.BlockSpec((1, rows, HALF), lambda i, w: (i, 0, 0))
    return pl.pallas_call(
        body, name=name, out_shape=jax.ShapeDtypeStruct((n, rows, HALF), g.dtype), compiler_params=_cp("parallel"),
        grid_spec=pltpu.PrefetchScalarGridSpec(
            num_scalar_prefetch=1, grid=(n,),
            in_specs=[pl.BlockSpec((1, rows, HALF), lambda i, w: (i, 0, w[0])), blk], out_specs=blk))(
        jnp.reshape(core, (1,)).astype(jnp.int32), g, land)


def _sum_chips(p, land, me, *, tc, name):
    rows = p.shape[1]

    def body(me_ref, p_ref, land_ref, o_ref):
        acc = p_ref[...].astype(F32)
        for k in range(3):
            acc = acc + land_ref[k].astype(F32)
        o_ref[...] = acc

    return pl.pallas_call(
        body, name=name, out_shape=jax.ShapeDtypeStruct((rows, HALF), F32), compiler_params=_cp("parallel"),
        grid_spec=pltpu.PrefetchScalarGridSpec(
            num_scalar_prefetch=1, grid=(HALF // tc,),
            in_specs=[pl.BlockSpec((None, rows, tc), lambda i, w: (w[0], 0, i)),
                      pl.BlockSpec((3, rows, tc), lambda i, w: (0, 0, i))],
            out_specs=pl.BlockSpec((rows, tc), lambda i, w: (0, i))))(
        jnp.reshape(me, (1,)).astype(jnp.int32), p, land)


def _sum_slots(a, *, name):
    n = a.shape[0]

    def body(a_ref, o_ref):
        acc = a_ref[0]
        for k in range(1, n):
            acc = acc + a_ref[k]
        o_ref[...] = acc

    return pl.pallas_call(body, name=name, out_shape=jax.ShapeDtypeStruct(a.shape[1:], a.dtype))(a)


def _elementwise(fn, ins, n_out, block, *, name):
    shape = ins[0].shape
    grid = tuple(s // b for s, b in zip(shape, block))
    n_in = len(ins)

    def body(*refs):
        outs = fn(*[r[...] for r in refs[:n_in]])
        for o_ref, val in zip(refs[n_in:], outs):
            o_ref[...] = val

    spec = pl.BlockSpec(block, lambda i, j, k: (i, j, k))
    return pl.pallas_call(body, name=name, grid=grid, in_specs=[spec] * n_in, out_specs=[spec] * n_out,
                          out_shape=[jax.ShapeDtypeStruct(shape, F32)] * n_out,
                          compiler_params=_cp(*["parallel"] * 3))(*ins)


def _adamw_math(w, g, m, v):
    mn = ADAM_B1 * m + (1.0 - ADAM_B1) * g
    vn = ADAM_B2 * v + (1.0 - ADAM_B2) * (g * g)
    m_hat = mn / (1.0 - ADAM_B1 ** ADAM_STEP)
    v_hat = vn / (1.0 - ADAM_B2 ** ADAM_STEP)
    return -ADAM_LR * (m_hat / (jnp.sqrt(v_hat) + ADAM_EPS) + ADAM_WD * w), mn, vn


def _adamw(w, g, m, v, block, *, name):
    return _elementwise(_adamw_math, [w, g, m, v], 3, block, name=name)


def _adamw_small(ws, gs, ms, vs, *, name):
    n = len(ws)

    def body(*refs):
        w, g, m, v, outs = refs[:n], refs[n:2 * n], refs[2 * n:3 * n], refs[3 * n:4 * n], refs[4 * n:]
        for k in range(n):
            for slot, val in enumerate(_adamw_math(w[k][...], g[k][...], m[k][...], v[k][...])):
                outs[slot * n + k][...] = val

    outs = pl.pallas_call(body, name=name, out_shape=[jax.ShapeDtypeStruct(a.shape, F32) for a in ws] * 3)(
        *ws, *gs, *ms, *vs)
    return outs[:n], outs[n:2 * n], outs[2 * n:]


def _to_kernel_order(wt):
    gates = jnp.pad(wt[2048:2056], ((0, LANE - 2 * A_HEADS), (0, 0)))
    return jnp.concatenate([wt[0:2048], wt[2056:2568], wt[2824:3336], wt[2568:2696], wt[2696:2824], gates], axis=0)


def _from_kernel_order(main, tail):
    return jnp.concatenate([main[0:2048], tail[C_BG - DH_MAIN:C_BG - DH_MAIN + 2 * A_HEADS],
                            main[C_QB:C_QB + B_WIDTH], tail[0:B_KV_WIDTH], tail[B_KV_WIDTH:2 * B_KV_WIDTH],
                            main[C_ZB:C_ZB + B_WIDTH]], axis=0)


def _gate_params(a_log, dt_bias):
    return jnp.pad(jnp.stack([a_log, dt_bias]), ((0, SUBLANE - 2), (A_HEADS, LANE - 2 * A_HEADS)))


SMALL = ("conv_w", "a_log", "dt_bias", "norm_w", "sinks", "ln_g", "ln_b")


def _pack(parts, cols):
    flat = jnp.concatenate([p.reshape(-1) for p in parts])
    rows = -(-flat.shape[0] // cols)
    return jnp.pad(flat, (0, rows * cols - flat.shape[0])).reshape(rows, cols)


def _unpack(packed, shapes):
    flat = packed.reshape(-1)
    out, at = [], 0
    for s in shapes:
        n = math.prod(s)
        out.append(flat[at:at + n].reshape(s))
        at += n
    return out


def kernel(x, w_in, conv_w, a_log, dt_bias, norm_w, sinks, w_out, ln_g, ln_b, loss_target, m_w_in, m_conv_w, m_a_log, m_dt_bias, m_norm_w, m_sinks, m_w_out, m_ln_g, m_ln_b, v_w_in, v_conv_w, v_a_log, v_dt_bias, v_norm_w, v_sinks, v_w_out, v_ln_g, v_ln_b):
    xi, yi, ci = lax.axis_index("x"), lax.axis_index("y"), lax.axis_index("c")
    me = 2 * xi + yi

    to_t = lambda a: jnp.transpose(a, (2, 0, 1))
    from_t = lambda a: jnp.transpose(a, (1, 2, 0))

    wt_shard = to_t(w_in)

    def pack_weights(l):
        rows = jnp.pad(wt_shard[:, l], ((0, IN_PAD - IN_SHARD), (0, 0)))
        return jnp.concatenate([rows, w_out[l]], axis=0).astype(BF16)

    pack0, pack1 = pack_weights(0), pack_weights(1)
    got_in0, g_conv = _gather_two_level(pack0[:IN_PAD], conv_w, name="gather_weights_0")
    conv_full = jnp.moveaxis(g_conv, 0, 2).reshape(DEPTH, CONV_K, 3 * A_WIDTH)
    gathers = dict(in_proj=_gather_exchange([pack0[IN_PAD:]]), dn_wy=_gather_exchange([pack1[:PACK_SPLIT]]),
                   swa=_gather_exchange([pack1[PACK_SPLIT:]]))
    w_in_of = lambda rows: _to_kernel_order(rows.reshape(IN_COLS, D_MODEL))
    w_out_of = lambda rows: rows.reshape(D_MODEL, D_MODEL)
    args0 = _layer_args(w_in_of(got_in0[:, :IN_SHARD]), conv_full[0], a_log[0], dt_bias[0], sinks[0], norm_w[0],
                        lambda got: w_out_of(got[0]))

    def args1(got):
        first, rest = got["dn_wy"][0], got["swa"][0]
        rows = jnp.concatenate([first, rest[:, :IN_SHARD - PACK_SPLIT]], axis=1)
        return _layer_args(w_in_of(rows), conv_full[1], a_log[1], dt_bias[1], sinks[1], norm_w[1],
                           w_out_of(rest[:, IN_PAD - PACK_SPLIT:]))

    def pack_grads(g):
        gin = _from_kernel_order(*g["w_in"]).reshape(N_SHARD, IN_SHARD, D_MODEL)
        gin = jnp.pad(gin, ((0, 0), (0, IN_PAD - IN_SHARD), (0, 0)))
        return jnp.concatenate([gin, g["w_out"].reshape(N_SHARD, OUT_SHARD, D_MODEL)], axis=1).astype(BF16)

    packed = {}

    def reduce1(grads1):
        packed[1] = pack_grads(grads1)
        return _reduce_scatter_exchange(packed[1])

    def reduce0(grads0, grads1):
        g0 = pack_grads(grads0)
        from_sibling = _run_exchange(_pair_window_exchange(g0), name="pair_reduce_0")[0]
        packed[0] = _pair_add(g0, from_sibling, ci, name="pair_add_0")
        gsmall = _pack([jnp.stack([g[nm] for g in (grads0, grads1)]) for nm in SMALL], D_MODEL)
        return _chip_scatter_exchange(packed[0], gsmall)

    loss_tile, dx, grads, landed1, (landed0, landed_small) = _local_step(
        x[0], loss_target[0], args0, args1, ln_g, ln_b, gathers=gathers, reduce1=reduce1, reduce0=reduce0)
    loss = lax.psum(loss_tile[0, 0], ("x", "y", "c"))

    small_shapes = [(DEPTH,) + grads[0][nm].shape for nm in SMALL]
    halves = [_sum_chips(packed[0], landed0, me, tc=2 * LANE, name="reduce_sum_0"),
              _sum_scatter(packed[1], landed1[0], me, ci, tc=2 * LANE, name="reduce_sum_1")]
    s_small = _sum_slots(landed_small, name="reduce_sum_small")
    others = _run_exchange(_share_exchange(halves), name="pair_share")
    full = [jnp.where(ci == 0, jnp.concatenate([mine, other], axis=1), jnp.concatenate([other, mine], axis=1))
            for mine, other in zip(halves, others)]
    grad_in_t = jnp.stack([f[:IN_SHARD] for f in full], axis=1)
    grad_out = jnp.stack([f[IN_PAD:] for f in full])
    out_blk = (1, OUT_SHARD, D_MODEL)
    gs = dict(zip(SMALL, _unpack(s_small, small_shapes)))
    gs["conv_w"] = lax.dynamic_slice_in_dim(gs["conv_w"], me * CONV_SHARD, CONV_SHARD, axis=2)

    adam_in_blk = (IN_SHARD // 6, DEPTH, D_MODEL)
    d_in, nm_in, nv_in = (from_t(o) for o in _adamw(to_t(w_in), grad_in_t, to_t(m_w_in), to_t(v_w_in), adam_in_blk,
                                                    name="adamw_in"))
    d_out, nm_out, nv_out = _adamw(w_out, grad_out, m_w_out, v_w_out, out_blk, name="adamw_out")
    ws = dict(conv_w=conv_w, a_log=a_log, dt_bias=dt_bias, norm_w=norm_w, sinks=sinks, ln_g=ln_g, ln_b=ln_b)
    ms = dict(conv_w=m_conv_w, a_log=m_a_log, dt_bias=m_dt_bias, norm_w=m_norm_w, sinks=m_sinks, ln_g=m_ln_g, ln_b=m_ln_b)
    vs = dict(conv_w=v_conv_w, a_log=v_a_log, dt_bias=v_dt_bias, norm_w=v_norm_w, sinks=v_sinks, ln_g=v_ln_g, ln_b=v_ln_b)
    d_s, nm_s, nv_s = (dict(zip(SMALL, o)) for o in _adamw_small(*[[d[nm] for nm in SMALL] for d in (ws, gs, ms, vs)],
                                                                 name="adamw_small"))

    def in_order(big_in, small, big_out):
        return (big_in, small["conv_w"], small["a_log"], small["dt_bias"], small["norm_w"], small["sinks"], big_out,
                small["ln_g"], small["ln_b"])

    return (loss, dx[None], *in_order(from_t(grad_in_t), gs, grad_out), *in_order(d_in, d_s, d_out),
            *in_order(nm_in, nm_s, nm_out), *in_order(nv_in, nv_s, nv_out))
```

```python
import math

import jax
import jax.numpy as jnp
from jax import lax
from jax.experimental import pallas as pl
from jax.experimental.pallas import tpu as pltpu

F32 = jnp.float32
BF16 = jnp.bfloat16
HI = lax.Precision.HIGHEST

D_MODEL = 1024
DEPTH = 2
A_HEADS = 4
A_HEAD_DIM = 128
A_WIDTH = 512
CONV_K = 4
CHUNK = 64
B_Q_HEADS = 8
B_KV_HEADS = 2
B_HEAD_DIM = 64
B_GROUP = 4
B_WIDTH = 512
B_KV_WIDTH = 128
BLOCK = 128
IN_COLS = 3336
DEEPNORM_ALPHA = (2 * DEPTH) ** 0.25
LN_EPS = 1e-5
RMS_EPS = 1e-6
L2_EPS = 1e-6
ADAM_LR = 0.001
ADAM_B1 = 0.9
ADAM_B2 = 0.999
ADAM_EPS = 1e-08
ADAM_WD = 0.01
ADAM_STEP = 10

N_SHARD = 4
IN_SHARD = IN_COLS // N_SHARD
OUT_SHARD = D_MODEL // N_SHARD
CONV_SHARD = 3 * A_WIDTH // N_SHARD
IN_PAD = -(-IN_SHARD // 32) * 32
PACK_SPLIT = 432

P_COLS = 3456
C_PRE = 0
C_ZA = 1536
C_QB = 2048
C_ZB = 2560
C_KB = 3072
C_VB = 3200
C_BG = 3328
DH_MAIN = C_KB
LANE = 128
SUBLANE = 8
VMEM_LIMIT = 56 * 1024 * 1024
ALIBI = tuple(2.0 ** (-8.0 * (h + 1) / B_Q_HEADS) for h in range(B_Q_HEADS))
NEG = -1e30


def _cp(*sem):
    return pltpu.CompilerParams(dimension_semantics=sem, vmem_limit_bytes=VMEM_LIMIT)


def _dot(a, b):
    return jnp.dot(a.astype(BF16), b.astype(BF16), preferred_element_type=F32)


def _dot_nt(a, b):
    return lax.dot_general(a.astype(BF16), b.astype(BF16), (((1,), (1,)), ((), ())),
                           preferred_element_type=F32)


def _dot_tn(a, b):
    return lax.dot_general(a.astype(BF16), b.astype(BF16), (((0,), (0,)), ((), ())),
                           preferred_element_type=F32)


def _dot_hi(a, b):
    return jnp.dot(a, b, precision=HI, preferred_element_type=F32)


def _sigmoid(x):
    return jax.nn.sigmoid(x)


def _silu(x):
    return x * _sigmoid(x)


def _dsilu(x):
    s = _sigmoid(x)
    return s * (1.0 + x * (1.0 - s))


def _softplus(x):
    return jnp.maximum(x, 0.0) + jnp.log(1.0 + jnp.exp(-jnp.abs(x)))


def _shift_down(cur, before, s):
    if s == 0:
        return cur
    r = pltpu.roll(cur, s, 0)
    rb = pltpu.roll(before, s, 0)
    row = lax.broadcasted_iota(jnp.int32, before.shape, 0)
    head = jnp.where(row < s, rb, r[0:SUBLANE])
    return jnp.concatenate([head, r[SUBLANE:]], axis=0)


def _shift_up(cur, after, s):
    if s == 0:
        return cur
    n = cur.shape[0]
    r = pltpu.roll(cur, n - s, 0)
    ra = pltpu.roll(after, SUBLANE - s, 0)
    row = lax.broadcasted_iota(jnp.int32, after.shape, 0)
    tail = jnp.where(row >= SUBLANE - s, ra, r[n - SUBLANE:])
    return jnp.concatenate([r[:n - SUBLANE], tail], axis=0)


def _conv_fwd(cur, before, w):
    acc = cur * w[CONV_K - 1:CONV_K, :]
    for s in range(1, CONV_K):
        acc = acc + _shift_down(cur, before, s) * w[CONV_K - 1 - s:CONV_K - s, :]
    return acc


def _matmul_nt(a, bt, *, tm, name, carry=None):
    m, k = a.shape
    n = bt.shape[0]
    c_ins, c_in_specs, c_out_specs, c_outs, c_scratch = _carry_specs(carry)

    def body(*refs):
        a_ref, b_ref, o_ref = _carried(carry, refs, 2, 1, m // tm)
        o_ref[...] = _dot_nt(a_ref[...], b_ref[...])

    outs = pl.pallas_call(
        body, name=name, grid=(m // tm,),
        in_specs=[pl.BlockSpec((tm, k), lambda i: (i, 0)), pl.BlockSpec((n, k), lambda i: (0, 0))] + c_in_specs,
        out_specs=[pl.BlockSpec((tm, n), lambda i: (i, 0))] + c_out_specs,
        out_shape=[jax.ShapeDtypeStruct((m, n), F32)] + c_outs,
        scratch_shapes=c_scratch,
        compiler_params=_cp("arbitrary"))(a, bt, *c_ins)
    return outs[0], outs[1:]


def _dn_pre(h, conv_w, par, *, tt, name):
    t = h.shape[0]
    cw = 3 * A_WIDTH
    hb = tt // SUBLANE

    def body(pre_ref, halo_ref, bgi_ref, cw_ref, par_ref, q_ref, k_ref, v_ref, bg_ref, bgt_ref):
        i = pl.program_id(0)
        cur = pre_ref[...]
        before = jnp.where(i > 0, halo_ref[...], 0.0)
        s = _silu(_conv_fwd(cur, before, cw_ref[...]))
        for hd in range(A_HEADS):
            sl = slice(hd * LANE, (hd + 1) * LANE)
            tq = s[:, hd * LANE:(hd + 1) * LANE]
            q_ref[:, sl] = tq * (lax.rsqrt(jnp.sum(tq * tq, -1, keepdims=True) + L2_EPS) * (A_HEAD_DIM ** -0.5))
            tk = s[:, A_WIDTH + hd * LANE:A_WIDTH + (hd + 1) * LANE]
            k_ref[:, sl] = tk * lax.rsqrt(jnp.sum(tk * tk, -1, keepdims=True) + L2_EPS)
        v_ref[...] = s[:, 2 * A_WIDTH:]
        raw = bgi_ref[...]
        lane = lax.broadcasted_iota(jnp.int32, raw.shape, 1)
        is_a = (lane >= A_HEADS) & (lane < 2 * A_HEADS)
        g = jnp.where(is_a, -jnp.exp(par_ref[0:1, :]) * _softplus(raw + par_ref[1:2, :]), 0.0)
        gc = _dot_hi(_chunk_tri(tt, lower=True), g)
        bg = jnp.where(lane < A_HEADS, _sigmoid(raw), gc)
        bg_ref[...] = bg
        bgt_ref[...] = jnp.transpose(bg)[0:SUBLANE, :]

    wide = jax.ShapeDtypeStruct((t, A_WIDTH), F32)
    return pl.pallas_call(
        body, name=name, grid=(t // tt,),
        in_specs=[pl.BlockSpec((tt, cw), lambda i: (i, 0)),
                  pl.BlockSpec((SUBLANE, cw), lambda i: (jnp.maximum(i * hb - 1, 0), 0)),
                  pl.BlockSpec((tt, LANE), lambda i: (i, C_BG // LANE)),
                  pl.BlockSpec((CONV_K, cw), lambda i: (0, 0)),
                  pl.BlockSpec((SUBLANE, LANE), lambda i: (0, 0))],
        out_specs=[pl.BlockSpec((tt, A_WIDTH), lambda i: (i, 0))] * 3
        + [pl.BlockSpec((tt, LANE), lambda i: (i, 0)), pl.BlockSpec((SUBLANE, tt), lambda i: (0, i))],
        out_shape=[wide, wide, wide, jax.ShapeDtypeStruct((t, LANE), F32), jax.ShapeDtypeStruct((SUBLANE, t), F32)],
        compiler_params=_cp("parallel"))(h, h, h, conv_w, par)


def _chunk_tri(n, lower):
    r = lax.broadcasted_iota(jnp.int32, (n, n), 0)
    c = lax.broadcasted_iota(jnp.int32, (n, n), 1)
    shift = CHUNK.bit_length() - 1
    same = jnp.right_shift(r, shift) == jnp.right_shift(c, shift)
    return (same & ((c <= r) if lower else (c >= r))).astype(F32)


def _chunk_masks():
    r = lax.broadcasted_iota(jnp.int32, (CHUNK, CHUNK), 0)
    c = lax.broadcasted_iota(jnp.int32, (CHUNK, CHUNK), 1)
    return r >= c, r > c, r == c


def _split(a):
    hi = a.astype(BF16)
    return hi, (a - hi.astype(F32)).astype(BF16)


def _dot3(a, b):
    (ah, al), (bh, bl) = a, b
    d = lambda p, q: jnp.dot(p, q, preferred_element_type=F32)
    return d(ah, bh) + (d(ah, bl) + d(al, bh))


def _tri_inv_many(a_list, eye):
    d = lambda p, q: jnp.dot(p, q, preferred_element_type=F32)
    p = [(-a).astype(BF16) for a in a_list]
    tm = [eye - a for a in a_list]
    for _ in range(5):
        pf = [d(pi, pi) for pi in p]
        p = [x.astype(BF16) for x in pf]
        tm = [t + d(t.astype(BF16), pi) for t, pi in zip(tm, p)]
    ms = [_split(eye + a) for a in a_list]
    res = [eye - _dot3(m, _split(t)) for m, t in zip(ms, tm)]
    return [t + d(t.astype(BF16), r.astype(BF16)) for t, r in zip(tm, res)]


def _chunk_gates(bg_v, bgt_v, hd):
    return (bg_v[:, hd:hd + 1], bg_v[:, A_HEADS + hd:A_HEADS + hd + 1],
            None if bgt_v is None else bgt_v[A_HEADS + hd:A_HEADS + hd + 1, :])


WY_ROWS = 512
SCAN_ROWS = 128
WY_GROUP = 8


def _dn_wy(q, k, v, bg, bgt, *, name, carry=None):
    t = q.shape[0]
    rows = WY_ROWS

    c_ins, c_in_specs, c_out_specs, c_outs, c_scratch = _carry_specs(carry)

    def body(*refs):
        q_ref, k_ref, v_ref, bg_ref, bgt_ref, u_ref, w_ref, tm_ref, qk_ref = _carried(carry, refs, 5, 4, t // rows)
        causal, strict, diag = _chunk_masks()
        eye = diag.astype(F32)
        for c0 in range(0, rows // CHUNK, WY_GROUP):
            items = [(c, hd) for c in range(c0, c0 + WY_GROUP) for hd in range(A_HEADS)]
            rs = lambda c: slice(c * CHUNK, (c + 1) * CHUNK)
            sl = lambda hd: slice(hd * LANE, (hd + 1) * LANE)
            hs = lambda hd: slice(hd * CHUNK, (hd + 1) * CHUNK)
            gates = [_chunk_gates(bg_ref[rs(c), :], bgt_ref[:, rs(c)], hd) for c, hd in items]
            dms = [jnp.exp(jnp.where(causal, gcol - grow, NEG)) for _, gcol, grow in gates]
            kbs = [k_ref[rs(c), sl(hd)] * g[0] for (c, hd), g in zip(items, gates)]
            a_list = [jnp.where(strict, _dot_nt(kb, k_ref[rs(c), sl(hd)]) * dm, 0.0)
                      for (c, hd), kb, dm in zip(items, kbs, dms)]
            for (c, hd), dm in zip(items, dms):
                qk_ref[rs(c), hs(hd)] = jnp.where(
                    causal, _dot_nt(q_ref[rs(c), sl(hd)], k_ref[rs(c), sl(hd)]) * dm, 0.0)
            tms = _tri_inv_many(a_list, eye)
            for (c, hd), g, kb, tmat in zip(items, gates, kbs, tms):
                tm_ref[rs(c), hs(hd)] = tmat
                u_ref[rs(c), sl(hd)] = _dot(tmat, v_ref[rs(c), sl(hd)] * g[0])
                w_ref[rs(c), sl(hd)] = _dot(tmat, kb * jnp.exp(g[1])).astype(BF16)

    blk = pl.BlockSpec((rows, A_WIDTH), lambda i: (i, 0))
    half = pl.BlockSpec((rows, A_HEADS * CHUNK), lambda i: (i, 0))
    outs = pl.pallas_call(
        body, name=name, grid=(t // rows,),
        in_specs=[blk, blk, blk, pl.BlockSpec((rows, LANE), lambda i: (i, 0)),
                  pl.BlockSpec((SUBLANE, rows), lambda i: (0, i))] + c_in_specs,
        out_specs=[blk, blk, half, half] + c_out_specs,
        out_shape=[jax.ShapeDtypeStruct((t, A_WIDTH), F32), jax.ShapeDtypeStruct((t, A_WIDTH), BF16),
                   jax.ShapeDtypeStruct((t, A_HEADS * CHUNK), F32),
                   jax.ShapeDtypeStruct((t, A_HEADS * CHUNK), F32)] + c_outs,
        scratch_shapes=c_scratch,
        compiler_params=_cp("arbitrary"))(q, k, v, bg, bgt, *c_ins)
    return outs[:4], outs[4:]


def _dn_scan_fwd(q, k, u, w, qk, bg, *, name):
    t = q.shape[0]
    rows = SCAN_ROWS
    per = rows // CHUNK

    def body(q_ref, k_ref, u_ref, w_ref, qk_ref, bg_ref, o_ref, vn_ref, s_ref, state):
        @pl.when(pl.program_id(0) == 0)
        def _():
            state[...] = jnp.zeros_like(state)

        heads = range(A_HEADS)
        sl = lambda hd: slice(hd * LANE, (hd + 1) * LANE)
        s_cur = [state[hd] for hd in heads]
        for c in range(per):
            rs = slice(c * CHUNK, (c + 1) * CHUNK)
            bg_v = bg_ref[rs, :]
            gcols = [_chunk_gates(bg_v, None, hd)[1] for hd in heads]
            glasts = [gc[CHUNK - 1:CHUNK, :] for gc in gcols]
            for hd in heads:
                s_ref[c, hd] = s_cur[hd]
            vns = [u_ref[rs, sl(hd)] - _dot(w_ref[rs, sl(hd)], s_cur[hd]) for hd in heads]
            qss = [_dot(q_ref[rs, sl(hd)] * jnp.exp(gcols[hd]), s_cur[hd]) for hd in heads]
            s_cur = [s_cur[hd] * jnp.exp(glasts[hd])
                     + _dot_tn(k_ref[rs, sl(hd)] * jnp.exp(glasts[hd] - gcols[hd]), vns[hd]) for hd in heads]
            for hd in heads:
                vn_ref[rs, sl(hd)] = vns[hd]
                o_ref[rs, sl(hd)] = qss[hd] + _dot(qk_ref[rs, hd * CHUNK:(hd + 1) * CHUNK], vns[hd])
        for hd in heads:
            state[hd] = s_cur[hd]

    blk = pl.BlockSpec((rows, A_WIDTH), lambda i: (i, 0))
    half = pl.BlockSpec((rows, A_HEADS * CHUNK), lambda i: (i, 0))
    wide = jax.ShapeDtypeStruct((t, A_WIDTH), F32)
    return pl.pallas_call(
        body, name=name, grid=(t // rows,),
        in_specs=[blk, blk, blk, blk, half, pl.BlockSpec((rows, LANE), lambda i: (i, 0))],
        out_specs=[blk, blk, pl.BlockSpec((per, A_HEADS, LANE, LANE), lambda i: (i, 0, 0, 0))],
        out_shape=[wide, wide, jax.ShapeDtypeStruct((t // CHUNK, A_HEADS, LANE, LANE), F32)],
        scratch_shapes=[pltpu.VMEM((A_HEADS, LANE, LANE), F32)],
        compiler_params=_cp("arbitrary"))(q, k, u, w, qk, bg)


def _swa_neg_dist(n_blk):
    qi = lax.broadcasted_iota(jnp.int32, (BLOCK, 2 * BLOCK), 0)
    si = lax.broadcasted_iota(jnp.int32, (BLOCK, 2 * BLOCK), 1)
    dist = qi + BLOCK - si
    mask = (dist >= 0) & (dist < BLOCK) & ((si >= BLOCK) | (n_blk > 0))
    return jnp.where(mask, -dist.astype(F32), NEG)


def _stack_heads(ref, hk):
    return jnp.concatenate([ref[:, h * B_HEAD_DIM:(h + 1) * B_HEAD_DIM]
                            for h in range(hk * B_GROUP, (hk + 1) * B_GROUP)], axis=0)


def _swa_group_probs(q_ref, sk_ref, kband, vband, neg_dist):
    hks = range(B_KV_HEADS)
    heads = lambda hk: range(hk * B_GROUP, (hk + 1) * B_GROUP)
    ksl = lambda hk: slice(hk * B_HEAD_DIM, (hk + 1) * B_HEAD_DIM)
    ones = jnp.ones((2 * BLOCK, B_HEAD_DIM), BF16)
    qs = [_stack_heads(q_ref, hk) * (B_HEAD_DIM ** -0.5) for hk in hks]
    sink = [jnp.concatenate([jnp.broadcast_to(sk_ref[h:h + 1, 0:1], (BLOCK, 1)) for h in heads(hk)], axis=0)
            for hk in hks]
    s = [_dot_nt(qs[hk], kband[:, ksl(hk)]) + jnp.concatenate([ALIBI[h] * neg_dist for h in heads(hk)], axis=0)
         for hk in hks]
    m = [jnp.maximum(jnp.max(s[hk], axis=-1, keepdims=True), sink[hk]) for hk in hks]
    p = [jnp.exp(s[hk] - m[hk]) for hk in hks]
    oe = [jnp.dot(p[hk].astype(BF16), jnp.concatenate([vband[:, ksl(hk)].astype(BF16), ones], axis=1),
                  preferred_element_type=F32) for hk in hks]
    ps = [jnp.exp(sink[hk] - m[hk]) for hk in hks]
    inv = [1.0 / (oe[hk][:, B_HEAD_DIM:B_HEAD_DIM + 1] + ps[hk]) for hk in hks]
    return [(qs[hk], p[hk] * inv[hk], ps[hk] * inv[hk], oe[hk][:, :B_HEAD_DIM] * inv[hk]) for hk in hks]


def _swa_specs():
    qspec = lambda c0: pl.BlockSpec((BLOCK, B_WIDTH), lambda i: (i, c0 // B_WIDTH))
    cur = lambda c0: pl.BlockSpec((BLOCK, LANE), lambda i: (i, c0 // LANE))
    prev = lambda c0: pl.BlockSpec((BLOCK, LANE), lambda i: (jnp.maximum(i - 1, 0), c0 // LANE))
    return qspec, cur, prev


def _carried(carry, refs, n_in, n_out, steps):
    if carry is None:
        return refs
    ci, co = len(carry.ins), len(carry.outs)
    own = refs[:n_in] + refs[n_in + ci:n_in + ci + n_out] + refs[n_in + ci + n_out + co:len(refs) - 3]
    parts = refs[n_in:n_in + ci], refs[n_in + ci + n_out:n_in + ci + n_out + co], refs[len(refs) - 3:]

    @pl.when(pl.program_id(0) == 0)
    def _():
        carry.start(*parts)

    @pl.when(pl.program_id(0) == steps - 1)
    def _():
        carry.finish(*parts)

    return own


def _carry_specs(carry):
    if carry is None:
        return [], [], [], [], []
    return (list(carry.ins), [_ANY] * len(carry.ins), [_ANY] * len(carry.outs), list(carry.outs), carry.scratch())


def _swa_fwd(h, sinks_b, *, name, carry=None):
    t = h.shape[0]
    qspec, cur, prev = _swa_specs()
    c_ins, c_in_specs, c_out_specs, c_outs, c_scratch = _carry_specs(carry)

    def body(*refs):
        q_ref, kc_ref, kp_ref, vc_ref, vp_ref, sk_ref, o_ref = _carried(carry, refs, 6, 1, t // BLOCK)
        n_blk = pl.program_id(0)
        kband = jnp.concatenate([kp_ref[...], kc_ref[...]], axis=0)
        vband = jnp.concatenate([vp_ref[...], vc_ref[...]], axis=0)
        groups = _swa_group_probs(q_ref, sk_ref, kband, vband, _swa_neg_dist(n_blk))
        for hk, (_, _, _, o) in enumerate(groups):
            for g in range(B_GROUP):
                hq = hk * B_GROUP + g
                o_ref[:, hq * B_HEAD_DIM:(hq + 1) * B_HEAD_DIM] = o[g * BLOCK:(g + 1) * BLOCK]

    outs = pl.pallas_call(
        body, name=name, grid=(t // BLOCK,),
        in_specs=[qspec(C_QB), cur(C_KB), prev(C_KB), cur(C_VB), prev(C_VB),
                  pl.BlockSpec((B_Q_HEADS, LANE), lambda i: (0, 0))] + c_in_specs,
        out_specs=[pl.BlockSpec((BLOCK, B_WIDTH), lambda i: (i, 0))] + c_out_specs,
        out_shape=[jax.ShapeDtypeStruct((t, B_WIDTH), F32)] + c_outs,
        scratch_shapes=c_scratch,
        compiler_params=_cp("arbitrary"))(h, h, h, h, h, sinks_b, *c_ins)
    return outs[0], outs[1:]


def _rms_gate(o, za, nw):
    outs = []
    for hd in range(A_HEADS):
        oh = o[:, hd * LANE:(hd + 1) * LANE]
        r = lax.rsqrt(jnp.mean(oh * oh, -1, keepdims=True) + RMS_EPS)
        outs.append(oh * r * nw)
    return jnp.concatenate(outs, axis=1) * _silu(za)


def _out_ln(x, oa, ob, h, norm_w, w_out, ln_g, ln_b, *, tm, name, target=None):
    t = x.shape[0]
    last = target is not None

    def body(*refs):
        x_ref, oa_ref, ob_ref, za_ref, zb_ref, nw_ref, w_ref, g_ref, b_ref = refs[:9]
        xn_ref, mx_ref, r_ref = refs[9 + last:12 + last]
        ya = _rms_gate(oa_ref[...], za_ref[...], nw_ref[...])
        yb = ob_ref[...] * _silu(zb_ref[...])
        mixed = jnp.concatenate([ya, yb], axis=1).astype(BF16)
        mx_ref[...] = mixed
        r = DEEPNORM_ALPHA * x_ref[...] + jnp.dot(mixed, w_ref[...], preferred_element_type=F32)
        r_ref[...] = r
        mu = jnp.mean(r, -1, keepdims=True)
        xc = r - mu
        var = jnp.mean(xc * xc, -1, keepdims=True)
        xn = xc * lax.rsqrt(var + LN_EPS) * g_ref[...] + b_ref[...]
        if not last:
            xn_ref[...] = xn
            return
        loss_ref = refs[13]

        @pl.when(pl.program_id(0) == 0)
        def _():
            loss_ref[...] = jnp.zeros_like(loss_ref)

        err = xn - refs[9][...]
        xn_ref[...] = err * (1.0 / D_MODEL)
        loss_ref[...] += 0.5 / D_MODEL * jnp.sum(err * err)

    row = lambda w, c: pl.BlockSpec((tm, w), lambda i: (i, c))
    full = lambda a, b: pl.BlockSpec((a, b), lambda i: (0, 0))
    wide = jax.ShapeDtypeStruct((t, D_MODEL), F32)
    return pl.pallas_call(
        body, name=name, grid=(t // tm,),
        in_specs=[row(D_MODEL, 0), row(A_WIDTH, 0), row(B_WIDTH, 0), row(A_WIDTH, C_ZA // A_WIDTH),
                  row(B_WIDTH, C_ZB // B_WIDTH), full(1, LANE), full(D_MODEL, D_MODEL), full(1, D_MODEL),
                  full(1, D_MODEL)] + [row(D_MODEL, 0)] * last,
        out_specs=[row(D_MODEL, 0), row(D_MODEL, 0), row(D_MODEL, 0)] + [full(SUBLANE, LANE)] * last,
        out_shape=[wide, jax.ShapeDtypeStruct((t, D_MODEL), BF16), wide]
        + [jax.ShapeDtypeStruct((SUBLANE, LANE), F32)] * last,
        compiler_params=_cp("arbitrary" if last else "parallel"))(
        x, oa, ob, h, h, norm_w, w_out, ln_g, ln_b, *([target] if last else []))


def _layer_fwd(x, wt, conv_w, par, sinks_b, norm_w, w_out_bf, ln_g, ln_b, l, carries=None, target=None):
    carries = carries or {}
    h, got_in = _matmul_nt(x, wt, tm=512, name=f"in_proj_{l}", carry=carries.get("in_proj"))
    if callable(w_out_bf):
        w_out_bf = w_out_bf(got_in)
    q, k, v, bg, bgt = _dn_pre(h, conv_w, par, tt=512, name=f"dn_pre_{l}")
    (u, w, tmat, qk), got_wy = _dn_wy(q, k, v, bg, bgt, name=f"dn_wy_{l}", carry=carries.get("dn_wy"))
    oa, vn, s_all = _dn_scan_fwd(q, k, u, w, qk, bg, name=f"dn_scan_{l}")
    ob, got_swa = _swa_fwd(h, sinks_b, name=f"swa_fwd_{l}", carry=carries.get("swa"))
    xn, mixed, r, *loss = _out_ln(x, oa, ob, h, norm_w, w_out_bf, ln_g, ln_b, tm=256, name=f"out_ln_{l}", target=target)
    if loss:
        xn = (xn, loss[0])
    res = dict(x=x, h=h, q=q, k=k, v=v, bg=bg, bgt=bgt, w=w, tmat=tmat, qk=qk, vn=vn, oa=oa, s_all=s_all,
               mixed=mixed, r=r, w_out=w_out_bf)
    return xn, res, dict(in_proj=got_in, dn_wy=got_wy, swa=got_swa)


def _ln_out_bwd(dxn, r, mixed, ln_g, w_out, *, tm, name):
    t = dxn.shape[0]

    def body(dxn_ref, r_ref, mx_ref, g_ref, w_ref, dr_ref, dm_ref, dw_ref, dg_ref, db_ref):
        @pl.when(pl.program_id(0) == 0)
        def _():
            dw_ref[...] = jnp.zeros_like(dw_ref)
            dg_ref[...] = jnp.zeros_like(dg_ref)
            db_ref[...] = jnp.zeros_like(db_ref)

        rr = r_ref[...]
        xc = rr - jnp.mean(rr, -1, keepdims=True)
        rstd = lax.rsqrt(jnp.mean(xc * xc, -1, keepdims=True) + LN_EPS)
        xhat = xc * rstd
        dxn_v = dxn_ref[...]
        dxh = dxn_v * g_ref[...]
        dr = rstd * (dxh - jnp.mean(dxh, -1, keepdims=True) - xhat * jnp.mean(dxh * xhat, -1, keepdims=True))
        dr_ref[...] = dr
        dg_ref[...] += jnp.sum(dxn_v * xhat, axis=0, keepdims=True)
        db_ref[...] += jnp.sum(dxn_v, axis=0, keepdims=True)
        drb = dr.astype(BF16)
        dm_ref[...] = _dot_nt(drb, w_ref[...])
        dw_ref[...] += _dot_tn(mx_ref[...], drb)

    row = pl.BlockSpec((tm, D_MODEL), lambda i: (i, 0))
    full = lambda a, b: pl.BlockSpec((a, b), lambda i: (0, 0))
    big = jax.ShapeDtypeStruct((t, D_MODEL), F32)
    vec = jax.ShapeDtypeStruct((1, D_MODEL), F32)
    return pl.pallas_call(
        body, name=name, grid=(t // tm,),
        in_specs=[row, row, row, full(1, D_MODEL), full(D_MODEL, D_MODEL)],
        out_specs=[row, row, full(D_MODEL, D_MODEL), full(1, D_MODEL), full(1, D_MODEL)],
        out_shape=[big, big, jax.ShapeDtypeStruct((D_MODEL, D_MODEL), F32), vec, vec],
        compiler_params=_cp("arbitrary"))(dxn, r, mixed, ln_g, w_out)


def _dn_post_bwd(dm, oa, h, norm_w, *, tm, name):
    t = oa.shape[0]

    def body(dy_ref, o_ref, za_ref, nw_ref, do_ref, dza_ref, dnw_ref):
        @pl.when(pl.program_id(0) == 0)
        def _():
            dnw_ref[...] = jnp.zeros_like(dnw_ref)

        nw = nw_ref[...]
        dnw = jnp.zeros_like(nw)
        for hd in range(A_HEADS):
            sl = slice(hd * LANE, (hd + 1) * LANE)
            oh, za, dy = o_ref[:, sl], za_ref[:, sl], dy_ref[:, sl]
            rs = lax.rsqrt(jnp.mean(oh * oh, -1, keepdims=True) + RMS_EPS)
            nrm = oh * rs
            dza_ref[:, sl] = dy * nrm * nw * _dsilu(za)
            dn = dy * _silu(za)
            dnw = dnw + jnp.sum(dn * nrm, axis=0, keepdims=True)
            dnn = dn * nw
            do_ref[:, sl] = rs * dnn - oh * (rs * rs * rs) * jnp.mean(dnn * oh, -1, keepdims=True)
        dnw_ref[...] += dnw

    row = lambda c: pl.BlockSpec((tm, A_WIDTH), lambda i: (i, c))
    wide = jax.ShapeDtypeStruct((t, A_WIDTH), F32)
    return pl.pallas_call(
        body, name=name, grid=(t // tm,),
        in_specs=[row(0), row(0), row(C_ZA // A_WIDTH), pl.BlockSpec((1, LANE), lambda i: (0, 0))],
        out_specs=[row(0), row(C_ZA // A_WIDTH), pl.BlockSpec((1, LANE), lambda i: (0, 0))],
        out_shape=[wide, jax.ShapeDtypeStruct((t, DH_MAIN), F32), jax.ShapeDtypeStruct((1, LANE), F32)],
        compiler_params=_cp("arbitrary"))(dm, oa, h, norm_w)


def _dn_scan_bwd(q, k, w, qk, bg, do, *, name):
    t = q.shape[0]
    rows = SCAN_ROWS
    per = rows // CHUNK
    n = t // rows

    def body(q_ref, k_ref, w_ref, qk_ref, bg_ref, do_ref, dvn_ref, ds_ref, dstate):
        @pl.when(pl.program_id(0) == 0)
        def _():
            dstate[...] = jnp.zeros_like(dstate)

        heads = range(A_HEADS)
        sl = lambda hd: slice(hd * LANE, (hd + 1) * LANE)
        ds_cur = [dstate[hd] for hd in heads]
        for c in reversed(range(per)):
            rs = slice(c * CHUNK, (c + 1) * CHUNK)
            bg_v = bg_ref[rs, :]
            gcols = [_chunk_gates(bg_v, None, hd)[1] for hd in heads]
            glasts = [gc[CHUNK - 1:CHUNK, :] for gc in gcols]
            for hd in heads:
                ds_ref[c, hd] = ds_cur[hd]
            pdo = [_dot_tn(qk_ref[rs, hd * CHUNK:(hd + 1) * CHUNK], do_ref[rs, sl(hd)]) for hd in heads]
            qdo = [_dot_tn(q_ref[rs, sl(hd)] * jnp.exp(gcols[hd]), do_ref[rs, sl(hd)]) for hd in heads]
            dvns = [pdo[hd] + _dot(k_ref[rs, sl(hd)] * jnp.exp(glasts[hd] - gcols[hd]), ds_cur[hd]) for hd in heads]
            ds_cur = [qdo[hd] + jnp.exp(glasts[hd]) * ds_cur[hd] - _dot_tn(w_ref[rs, sl(hd)], dvns[hd])
                      for hd in heads]
            for hd in heads:
                dvn_ref[rs, sl(hd)] = dvns[hd]
        for hd in heads:
            dstate[hd] = ds_cur[hd]

    blk = pl.BlockSpec((rows, A_WIDTH), lambda i: (n - 1 - i, 0))
    return pl.pallas_call(
        body, name=name, grid=(n,),
        in_specs=[blk, blk, blk, pl.BlockSpec((rows, A_HEADS * CHUNK), lambda i: (n - 1 - i, 0)),
                  pl.BlockSpec((rows, LANE), lambda i: (n - 1 - i, 0)), blk],
        out_specs=[blk, pl.BlockSpec((per, A_HEADS, LANE, LANE), lambda i: (n - 1 - i, 0, 0, 0))],
        out_shape=[jax.ShapeDtypeStruct((t, A_WIDTH), F32),
                   jax.ShapeDtypeStruct((t // CHUNK, A_HEADS, LANE, LANE), F32)],
        scratch_shapes=[pltpu.VMEM((A_HEADS, LANE, LANE), F32)],
        compiler_params=_cp("arbitrary"))(q, k, w, qk, bg, do)


def _dn_chunk_bwd(q, k, v, vn, tmat, qk, bg, bgt, s_all, ds_all, dvn, do, *, name):
    t = q.shape[0]
    rows = WY_ROWS
    per = rows // CHUNK

    def body(q_ref, k_ref, v_ref, vn_ref, tm_ref, qk_ref, bg_ref, bgt_ref, s_ref, ds_ref, dvn_ref, do_ref,
             dq_ref, dk_ref, dv_ref, dbg_ref, dbgt_ref):
        causal, strict, _ = _chunk_masks()
        lane = lax.broadcasted_iota(jnp.int32, (CHUNK, LANE), 1)
        rowi = lax.broadcasted_iota(jnp.int32, (CHUNK, 1), 0)
        sub = lax.broadcasted_iota(jnp.int32, (SUBLANE, CHUNK), 0)
        rs = lambda c: slice(c * CHUNK, (c + 1) * CHUNK)
        sl = lambda hd: slice(hd * LANE, (hd + 1) * LANE)
        hs = lambda hd: slice(hd * CHUNK, (hd + 1) * CHUNK)
        for c0 in range(0, per, WY_GROUP):
            items = [(c, hd) for c in range(c0, c0 + WY_GROUP) for hd in range(A_HEADS)]
            at = lambda ref: [ref[rs(c), sl(hd)] for c, hd in items]
            qs, ks, vs, dos, vns, dvns = at(q_ref), at(k_ref), at(v_ref), at(do_ref), at(vn_ref), at(dvn_ref)
            tmhs = [tm_ref[rs(c), hs(hd)] for c, hd in items]
            ps = [qk_ref[rs(c), hs(hd)] for c, hd in items]
            gates = [_chunk_gates(bg_ref[rs(c), :], bgt_ref[:, rs(c)], hd) for c, hd in items]
            betas = [g[0] for g in gates]
            gcols = [g[1] for g in gates]
            dmats = [jnp.exp(jnp.where(causal, g[1] - g[2], NEG)) for g in gates]
            es = [jnp.exp(gc) for gc in gcols]
            glasts = [gc[CHUNK - 1:CHUNK, :] for gc in gcols]
            eks = [jnp.exp(gl - gc) for gl, gc in zip(glasts, gcols)]
            kbs = [kh * b for kh, b in zip(ks, betas)]
            vbs = [vh * b for vh, b in zip(vs, betas)]
            kbes = [kb * e for kb, e in zip(kbs, es)]

            a_s = [jnp.where(strict, _dot_nt(kb, kh) * dm, 0.0) for kb, kh, dm in zip(kbs, ks, dmats)]
            dps = [jnp.where(causal, _dot_nt(doh, vnh), 0.0) for doh, vnh in zip(dos, vns)]
            dqds = [_dot_nt(doh, s_ref[c, hd]) for doh, (c, hd) in zip(dos, items)]
            dkds = [_dot_nt(vnh, ds_ref[c, hd]) for vnh, (c, hd) in zip(vns, items)]
            dws = [-_dot_nt(dvnh, s_ref[c, hd]) for dvnh, (c, hd) in zip(dvns, items)]
            dvbs = [_dot_tn(tmh, dvnh) for tmh, dvnh in zip(tmhs, dvns)]
            dgts = [jnp.sum(s_ref[c, hd] * ds_ref[c, hd], keepdims=True) for c, hd in items]
            dts = [_dot_nt(dvnh, vb) + _dot_nt(dw, kbe) for dvnh, vb, dw, kbe in zip(dvns, vbs, dws, kbes)]
            dkbes = [_dot_tn(tmh, dw) for tmh, dw in zip(tmhs, dws)]
            xs = [_dot_nt(dt, tmh) for dt, tmh in zip(dts, tmhs)]
            das = [jnp.where(strict, -_dot_tn(tmh, x), 0.0) for tmh, x in zip(tmhs, xs)]
            dmas = [da * dm for da, dm in zip(das, dmats)]
            dmps = [dp * dm for dp, dm in zip(dps, dmats)]
            dkbs = [_dot(dma, kh) + dkbe * e for dma, kh, dkbe, e in zip(dmas, ks, dkbes, es)]
            for i, (c, hd) in enumerate(items):
                dq_ref[rs(c), sl(hd)] = _dot(dmps[i], ks[i]) + dqds[i] * es[i]
                dk_ref[rs(c), sl(hd)] = (_dot_tn(dmas[i], kbs[i]) + _dot_tn(dmps[i], qs[i]) + dkds[i] * eks[i]
                                         + dkbs[i] * betas[i])
                dv_ref[rs(c), sl(hd)] = dvbs[i] * betas[i]
            for c in range(c0, c0 + WY_GROUP):
                acc = jnp.zeros((CHUNK, LANE), F32)
                acc_t = jnp.zeros((SUBLANE, CHUNK), F32)
                for i, (ci, hd) in enumerate(items):
                    if ci != c:
                        continue
                    gmat = das[i] * a_s[i] + dps[i] * ps[i]
                    rk = jnp.sum(dkds[i] * ks[i], -1, keepdims=True) * eks[i]
                    de = (jnp.sum(dqds[i] * qs[i], -1, keepdims=True)
                          + jnp.sum(dkbes[i] * kbs[i], -1, keepdims=True))
                    dglast = jnp.sum(rk, keepdims=True) + dgts[i] * jnp.exp(glasts[i])
                    dgc = (jnp.sum(gmat, -1, keepdims=True) + de * es[i] - rk
                           + jnp.where(rowi == CHUNK - 1, dglast, 0.0))
                    dbeta = (jnp.sum(dkbs[i] * ks[i], -1, keepdims=True)
                             + jnp.sum(dvbs[i] * vs[i], -1, keepdims=True))
                    acc = acc + jnp.where(lane == hd, dbeta, 0.0) + jnp.where(lane == A_HEADS + hd, dgc, 0.0)
                    acc_t = acc_t + jnp.where(sub == A_HEADS + hd, -jnp.sum(gmat, axis=0, keepdims=True), 0.0)
                dbg_ref[rs(c), :] = acc
                dbgt_ref[:, rs(c)] = acc_t

    blk = pl.BlockSpec((rows, A_WIDTH), lambda i: (i, 0))
    half = pl.BlockSpec((rows, A_HEADS * CHUNK), lambda i: (i, 0))
    col = pl.BlockSpec((rows, LANE), lambda i: (i, 0))
    rowf = pl.BlockSpec((SUBLANE, rows), lambda i: (0, i))
    st = pl.BlockSpec((per, A_HEADS, LANE, LANE), lambda i: (i, 0, 0, 0))
    wide = jax.ShapeDtypeStruct((t, A_WIDTH), F32)
    return pl.pallas_call(
        body, name=name, grid=(t // rows,),
        in_specs=[blk, blk, blk, blk, half, half, col, rowf, st, st, blk, blk],
        out_specs=[blk, blk, blk, col, rowf],
        out_shape=[wide, wide, wide, jax.ShapeDtypeStruct((t, LANE), F32), jax.ShapeDtypeStruct((SUBLANE, t), F32)],
        compiler_params=_cp("parallel"))(q, k, v, vn, tmat, qk, bg, bgt, s_all, ds_all, dvn, do)


def _dn_pre_bwd(h, conv_w, par, dq, dk, dv, dbg, dbgt, *, tt, name):
    t = h.shape[0]
    cw = 3 * A_WIDTH
    hb = tt // SUBLANE

    def body(pre_ref, halo_ref, bgi_ref, cw_ref, par_ref, dq_ref, dk_ref, dv_ref, dbg_ref, dbgt_ref,
             dc_ref, dbgi_ref, dpar_ref):
        i = pl.program_id(0)

        @pl.when(i == 0)
        def _():
            dpar_ref[...] = jnp.zeros_like(dpar_ref)

        cur = pre_ref[...]
        before = jnp.where(i > 0, halo_ref[...], 0.0)
        c = _conv_fwd(cur, before, cw_ref[...])
        s = _silu(c)
        ds = _dsilu(c)
        for hd in range(A_HEADS):
            sl = slice(hd * LANE, (hd + 1) * LANE)
            for base, d_ref, scale in ((0, dq_ref, A_HEAD_DIM ** -0.5), (A_WIDTH, dk_ref, 1.0)):
                csl = slice(base + hd * LANE, base + (hd + 1) * LANE)
                tq = s[:, base + hd * LANE:base + (hd + 1) * LANE]
                dy = d_ref[:, sl]
                rq = lax.rsqrt(jnp.sum(tq * tq, -1, keepdims=True) + L2_EPS)
                dtq = scale * (rq * dy - tq * (rq * rq * rq) * jnp.sum(dy * tq, -1, keepdims=True))
                dc_ref[:, csl] = dtq * ds[:, base + hd * LANE:base + (hd + 1) * LANE]
        dc_ref[:, 2 * A_WIDTH:] = dv_ref[...] * ds[:, 2 * A_WIDTH:]
        raw = bgi_ref[...]
        lane = lax.broadcasted_iota(jnp.int32, raw.shape, 1)
        is_b = lane < A_HEADS
        is_a = (lane >= A_HEADS) & (lane < 2 * A_HEADS)
        rows_t = jnp.concatenate([dbgt_ref[...], jnp.zeros((LANE - SUBLANE, tt), F32)], axis=0)
        dbg_v = dbg_ref[...] + jnp.where(is_a, jnp.transpose(rows_t), 0.0)
        dbg_v = jnp.where(is_a, _dot_hi(_chunk_tri(tt, lower=False), jnp.where(is_a, dbg_v, 0.0)), dbg_v)
        beta = _sigmoid(raw)
        z = raw + par_ref[1:2, :]
        neg_ea = -jnp.exp(par_ref[0:1, :])
        g = neg_ea * _softplus(z)
        da = dbg_v * neg_ea * _sigmoid(z)
        dbgi_ref[...] = jnp.where(is_b, dbg_v * beta * (1.0 - beta), jnp.where(is_a, da, 0.0))
        dpar_ref[0:1, :] += jnp.sum(jnp.where(is_a, dbg_v * g, 0.0), axis=0, keepdims=True)
        dpar_ref[1:2, :] += jnp.sum(jnp.where(is_a, da, 0.0), axis=0, keepdims=True)

    wide = pl.BlockSpec((tt, A_WIDTH), lambda i: (i, 0))
    return pl.pallas_call(
        body, name=name, grid=(t // tt,),
        in_specs=[pl.BlockSpec((tt, cw), lambda i: (i, 0)),
                  pl.BlockSpec((SUBLANE, cw), lambda i: (jnp.maximum(i * hb - 1, 0), 0)),
                  pl.BlockSpec((tt, LANE), lambda i: (i, C_BG // LANE)),
                  pl.BlockSpec((CONV_K, cw), lambda i: (0, 0)),
                  pl.BlockSpec((SUBLANE, LANE), lambda i: (0, 0)),
                  wide, wide, wide, pl.BlockSpec((tt, LANE), lambda i: (i, 0)),
                  pl.BlockSpec((SUBLANE, tt), lambda i: (0, i))],
        out_specs=[pl.BlockSpec((tt, cw), lambda i: (i, 0)), pl.BlockSpec((tt, LANE), lambda i: (i, 0)),
                   pl.BlockSpec((SUBLANE, LANE), lambda i: (0, 0))],
        out_shape=[jax.ShapeDtypeStruct((t, cw), F32), jax.ShapeDtypeStruct((t, LANE), F32),
                   jax.ShapeDtypeStruct((SUBLANE, LANE), F32)],
        compiler_params=_cp("arbitrary"))(h, h, h, conv_w, par, dq, dk, dv, dbg, dbgt)


def _conv_bwd(dc, h, conv_w, dh, *, tt, name):
    t = dc.shape[0]
    cw = 3 * A_WIDTH
    hb = tt // SUBLANE
    nb = t // tt

    def body(dc_ref, after_ref, pre_ref, before_ref, cw_ref, dh_in_ref, dpre_ref, dcw_ref):
        i = pl.program_id(0)

        @pl.when(i == 0)
        def _():
            dcw_ref[...] = jnp.zeros_like(dcw_ref)

        dcv = dc_ref[...]
        after = jnp.where(i < nb - 1, after_ref[...], 0.0)
        cur = pre_ref[...]
        before = jnp.where(i > 0, before_ref[...], 0.0)
        w = cw_ref[...]
        acc = dcv * w[CONV_K - 1:CONV_K, :]
        dcw_ref[CONV_K - 1:CONV_K, :] += jnp.sum(dcv * cur, axis=0, keepdims=True)
        for s in range(1, CONV_K):
            j = CONV_K - 1 - s
            acc = acc + _shift_up(dcv, after, s) * w[j:j + 1, :]
            dcw_ref[j:j + 1, :] += jnp.sum(dcv * _shift_down(cur, before, s), axis=0, keepdims=True)
        dpre_ref[...] = acc

    return pl.pallas_call(
        body, name=name, grid=(nb,),
        in_specs=[pl.BlockSpec((tt, cw), lambda i: (i, 0)),
                  pl.BlockSpec((SUBLANE, cw), lambda i: (jnp.minimum((i + 1) * hb, t // SUBLANE - 1), 0)),
                  pl.BlockSpec((tt, cw), lambda i: (i, 0)),
                  pl.BlockSpec((SUBLANE, cw), lambda i: (jnp.maximum(i * hb - 1, 0), 0)),
                  pl.BlockSpec((CONV_K, cw), lambda i: (0, 0)), _ANY],
        out_specs=[pl.BlockSpec((tt, cw), lambda i: (i, 0)), pl.BlockSpec((SUBLANE, cw), lambda i: (0, 0))],
        out_shape=[jax.ShapeDtypeStruct(dh.shape, F32), jax.ShapeDtypeStruct((SUBLANE, cw), F32)],
        input_output_aliases={5: 0},
        compiler_params=_cp("arbitrary"))(dc, dc, h, h, conv_w, dh)


def _swa_bwd(h, dm, sinks_b, dh, *, name, carry=None):
    t = h.shape[0]
    qspec, cur, prev = _swa_specs()
    c_ins, c_in_specs, c_out_specs, c_outs, c_scratch = _carry_specs(carry)

    def body(*refs):
        (q_ref, kc_ref, kp_ref, vc_ref, vp_ref, zb_ref, dy_ref, sk_ref, dh_in_ref,
         dqz_ref, dk_ref, dv_ref, dsk_ref) = _carried(carry, refs, 9, 4, t // BLOCK)
        n_blk = pl.program_id(0)

        @pl.when(n_blk == 0)
        def _():
            dk_ref[...] = jnp.zeros_like(dk_ref)
            dv_ref[...] = jnp.zeros_like(dv_ref)
            dsk_ref[...] = jnp.zeros_like(dsk_ref)

        kband = jnp.concatenate([kp_ref[...], kc_ref[...]], axis=0)
        vband = jnp.concatenate([vp_ref[...], vc_ref[...]], axis=0)
        scale = B_HEAD_DIM ** -0.5
        hks = range(B_KV_HEADS)
        ksl = lambda hk: slice(hk * B_HEAD_DIM, (hk + 1) * B_HEAD_DIM)
        groups = _swa_group_probs(q_ref, sk_ref, kband, vband, _swa_neg_dist(n_blk))
        zbs = [_stack_heads(zb_ref, hk) for hk in hks]
        dys = [_stack_heads(dy_ref, hk) for hk in hks]
        dos = [dys[hk] * _silu(zbs[hk]) for hk in hks]
        deltas = [jnp.sum(dos[hk] * groups[hk][3], -1, keepdims=True) for hk in hks]
        dss = [groups[hk][1] * (_dot_nt(dos[hk], vband[:, ksl(hk)]) - deltas[hk]) for hk in hks]
        dqs = [_dot(dss[hk], kband[:, ksl(hk)]) * scale for hk in hks]
        dk_acc = [_dot_tn(dss[hk], groups[hk][0]) for hk in hks]
        dv_acc = [_dot_tn(groups[hk][1], dos[hk]) for hk in hks]
        for hk in hks:
            dzb = dys[hk] * groups[hk][3] * _dsilu(zbs[hk])
            dsink = groups[hk][2] * deltas[hk]
            for g in range(B_GROUP):
                hq = hk * B_GROUP + g
                rows = slice(g * BLOCK, (g + 1) * BLOCK)
                qsl = slice(hq * B_HEAD_DIM, (hq + 1) * B_HEAD_DIM)
                dqz_ref[:, qsl] = dqs[hk][rows]
                dqz_ref[:, B_WIDTH + hq * B_HEAD_DIM:B_WIDTH + (hq + 1) * B_HEAD_DIM] = dzb[rows]
                dsk_ref[hq:hq + 1, :] += -jnp.sum(dsink[rows], keepdims=True)
        dkb = jnp.concatenate(dk_acc, axis=1)
        dvb = jnp.concatenate(dv_acc, axis=1)
        at_cur = pl.ds(pl.multiple_of(n_blk * BLOCK, BLOCK), BLOCK)
        at_prev = pl.ds(pl.multiple_of(jnp.maximum(n_blk - 1, 0) * BLOCK, BLOCK), BLOCK)
        dk_ref[at_prev, :] += dkb[:BLOCK]
        dv_ref[at_prev, :] += dvb[:BLOCK]
        dk_ref[at_cur, :] += dkb[BLOCK:]
        dv_ref[at_cur, :] += dvb[BLOCK:]

    narrow = jax.ShapeDtypeStruct((t, B_KV_WIDTH), F32)
    res = lambda a, b: pl.BlockSpec((a, b), lambda i: (0, 0))
    outs = pl.pallas_call(
        body, name=name, grid=(t // BLOCK,),
        in_specs=[qspec(C_QB), cur(C_KB), prev(C_KB), cur(C_VB), prev(C_VB), qspec(C_ZB),
                  pl.BlockSpec((BLOCK, B_WIDTH), lambda i: (i, 1)), res(B_Q_HEADS, LANE), _ANY] + c_in_specs,
        out_specs=[pl.BlockSpec((BLOCK, 2 * B_WIDTH), lambda i: (i, C_QB // (2 * B_WIDTH))),
                   res(t, B_KV_WIDTH), res(t, B_KV_WIDTH), res(B_Q_HEADS, LANE)] + c_out_specs,
        out_shape=[jax.ShapeDtypeStruct(dh.shape, F32), narrow, narrow,
                   jax.ShapeDtypeStruct((B_Q_HEADS, LANE), F32)] + c_outs,
        scratch_shapes=c_scratch,
        input_output_aliases={8: 0},
        compiler_params=_cp("arbitrary"))(h, h, h, h, h, h, dm, sinks_b, dh, *c_ins)
    return outs[:4], outs[4:]


def _matmul_tn(a, b, *, tk, tm, name):
    t, m = a.shape
    n = b.shape[1]

    def body(a_ref, b_ref, o_ref):
        @pl.when(pl.program_id(1) == 0)
        def _():
            o_ref[...] = jnp.zeros_like(o_ref)

        o_ref[...] += _dot_tn(a_ref[...], b_ref[...])

    return pl.pallas_call(
        body, name=name, grid=(m // tm, t // tk),
        in_specs=[pl.BlockSpec((tk, tm), lambda j, kk: (kk, j)), pl.BlockSpec((tk, n), lambda j, kk: (kk, 0))],
        out_specs=pl.BlockSpec((tm, n), lambda j, kk: (j, 0)),
        out_shape=jax.ShapeDtypeStruct((m, n), F32),
        compiler_params=_cp("parallel", "arbitrary"))(a, b)


def _in_proj_dx(dh_main, dh_tail, wt, dr, *, tm, name, carry=None):
    t, n_main = dh_main.shape
    n_tail = dh_tail.shape[1]
    c_ins, c_in_specs, c_out_specs, c_outs, c_scratch = _carry_specs(carry)

    def body(*refs):
        a_ref, t_ref, wa_ref, wt_ref, r_ref, o_ref = _carried(carry, refs, 5, 1, t // tm)
        o_ref[...] = _dot(a_ref[...], wa_ref[...]) + _dot(t_ref[...], wt_ref[...]) + DEEPNORM_ALPHA * r_ref[...]

    row = lambda w: pl.BlockSpec((tm, w), lambda i: (i, 0))
    outs = pl.pallas_call(
        body, name=name, grid=(t // tm,),
        in_specs=[row(n_main), row(n_tail), pl.BlockSpec((n_main, D_MODEL), lambda i: (0, 0)),
                  pl.BlockSpec((n_tail, D_MODEL), lambda i: (n_main // n_tail, 0)), row(D_MODEL)] + c_in_specs,
        out_specs=[row(D_MODEL)] + c_out_specs,
        out_shape=[jax.ShapeDtypeStruct((t, D_MODEL), F32)] + c_outs,
        scratch_shapes=c_scratch,
        compiler_params=_cp("arbitrary"))(dh_main, dh_tail, wt, wt, dr, *c_ins)
    return outs[0], outs[1:]


def _layer_bwd(dxn, res, wt, conv_w, par, sinks_b, norm_w, w_out_bf, ln_g, l, carry=None, carry_dx=None):
    w_out_bf = res["w_out"]
    dr, dm, dw_out, dln_g, dln_b = _ln_out_bwd(dxn, res["r"], res["mixed"], ln_g, w_out_bf, tm=256, name=f"ln_out_bwd_{l}")
    h = res["h"]
    do, dh, dnw = _dn_post_bwd(dm, res["oa"], h, norm_w, tm=512, name=f"dn_post_bwd_{l}")
    dvn, ds_all = _dn_scan_bwd(res["q"], res["k"], res["w"], res["qk"], res["bg"], do, name=f"dn_scan_bwd_{l}")
    dq, dk, dv, dbg, dbgt = _dn_chunk_bwd(res["q"], res["k"], res["v"], res["vn"], res["tmat"], res["qk"], res["bg"],
                                          res["bgt"], res["s_all"], ds_all, dvn, do, name=f"dn_chunk_bwd_{l}")
    dc, dbgi, dpar = _dn_pre_bwd(h, conv_w, par, dq, dk, dv, dbg, dbgt, tt=512, name=f"dn_pre_bwd_{l}")
    dh, dcw = _conv_bwd(dc, h, conv_w, dh, tt=512, name=f"conv_bwd_{l}")
    (dh, dkb, dvb, dsk), carried = _swa_bwd(h, dm, sinks_b, dh, name=f"swa_bwd_{l}", carry=carry)
    dh_tail = jnp.concatenate([dkb, dvb, dbgi], axis=1)
    dwt_main = _matmul_tn(dh, res["x"], tk=512, tm=768, name=f"in_proj_dw_{l}")
    dwt_tail = _matmul_tn(dh_tail, res["x"], tk=512, tm=P_COLS - DH_MAIN, name=f"in_proj_dw_tail_{l}")
    grads = dict(w_in=(dwt_main, dwt_tail), conv_w=dcw[:CONV_K], a_log=dpar[0, A_HEADS:2 * A_HEADS],
                 dt_bias=dpar[1, A_HEADS:2 * A_HEADS], norm_w=dnw[0], sinks=dsk[:, 0], w_out=dw_out,
                 ln_g=dln_g[0], ln_b=dln_b[0])
    dx, carried_dx = _in_proj_dx(dh, dh_tail, wt, dr, tm=256, name=f"in_proj_dx_{l}",
                                 carry=None if carry_dx is None else carry_dx(grads))
    return dx, grads, carried, carried_dx


def _layer_args(wt, conv_w, a_log, dt_bias, sinks, norm_w, w_out_bf):
    return (wt, conv_w, _gate_params(a_log, dt_bias), jnp.broadcast_to(sinks[:, None], (B_Q_HEADS, LANE)),
            norm_w[None], w_out_bf)


def _local_step(x, target, args0, args1, ln_g, ln_b, gathers=None, reduce1=None, reduce0=None):
    assert DEPTH == 2
    x1, res0, got = _layer_fwd(x, *args0, ln_g[0][None], ln_b[0][None], 0, carries=gathers)
    if gathers is not None:
        args1 = args1(got)
    (dx, loss_tile), res1, _ = _layer_fwd(x1, *args1, ln_g[1][None], ln_b[1][None], 1, target=target)
    dx, grads1, _, _ = _layer_bwd(dx, res1, *args1, ln_g[1][None], 1)
    carry = None if reduce1 is None else reduce1(grads1)
    carry_dx = None if reduce0 is None else (lambda grads0: reduce0(grads0, grads1, loss_tile))
    dx, grads0, landed1, landed0 = _layer_bwd(dx, res0, *args0, ln_g[0][None], 0, carry=carry, carry_dx=carry_dx)
    return loss_tile, dx, [grads0, grads1], landed1, landed0


_ANY = pl.BlockSpec(memory_space=pl.ANY)
_MESH = pl.DeviceIdType.MESH


HALF = D_MODEL // 2


class _Exchange:
    def __init__(self, ins, outs, n_remote, n_local, plan):
        self.ins, self.outs, self.n_remote, self.n_local, self.plan = tuple(ins), tuple(outs), n_remote, n_local, plan

    def scratch(self):
        return [pltpu.SemaphoreType.DMA((self.n_remote,)), pltpu.SemaphoreType.DMA((self.n_remote,)),
                pltpu.SemaphoreType.DMA((max(self.n_local, 1),))]

    def _copies(self, in_refs, out_refs, sems, arriving):
        send_sems, recv_sems, local_sems = sems
        local, sends, recvs = self.plan(in_refs, out_refs)
        loc = [pltpu.make_async_copy(s, d, local_sems.at[i]) for i, (s, d) in enumerate(local)]
        rem = [pltpu.make_async_remote_copy(src_ref=s, dst_ref=recvs[i] if arriving else d, send_sem=send_sems.at[i],
                                            recv_sem=recv_sems.at[i], device_id=peer, device_id_type=_MESH)
               for i, (s, d, peer) in enumerate(sends)]
        return loc, rem

    def start(self, in_refs, out_refs, sems):
        loc, rem = self._copies(in_refs, out_refs, sems, arriving=False)
        for cp in loc + rem:
            cp.start()

    def finish(self, in_refs, out_refs, sems):
        loc, rem = self._copies(in_refs, out_refs, sems, arriving=True)
        for cp in rem:
            cp.wait_recv()
        for cp in rem:
            cp.wait_send()
        for cp in loc:
            cp.wait()


def _run_exchange(ex, *, name):
    n_in, n_out = len(ex.ins), len(ex.outs)

    def body(*refs):
        parts = refs[:n_in], refs[n_in:n_in + n_out], refs[n_in + n_out:]
        ex.start(*parts)
        ex.finish(*parts)

    return pl.pallas_call(body, name=name, in_specs=[_ANY] * n_in, out_specs=[_ANY] * n_out, out_shape=list(ex.outs),
                          scratch_shapes=ex.scratch())(*ex.ins)


def _place():
    x, y, c = lax.axis_index("x"), lax.axis_index("y"), lax.axis_index("c")
    return x, y, c, [(1 - x, y), (x, 1 - y), (1 - x, 1 - y)]


def _gather_exchange(arrays):
    n = len(arrays)

    def plan(src, dst):
        x, y, c, chips = _place()
        me = 2 * x + y
        local = [(src[k], dst[k].at[me]) for k in range(n)]
        sends = [(src[k], dst[k].at[me], (px, py, c)) for k in range(n) for px, py in chips]
        recvs = [dst[k].at[2 * px + py] for k in range(n) for px, py in chips]
        return local, sends, recvs

    return _Exchange(arrays, [jax.ShapeDtypeStruct((N_SHARD,) + a.shape, a.dtype) for a in arrays], 3 * n, n, plan)


def _gather_two_level(pack, conv_w, *, name):
    rows = pack.shape[0]
    part_rows = rows // 2

    def body(pack_ref, conv_ref, land_ref, conv_land_ref, send1, recv1, send2, recv2, csend, crecv, local_sems):
        x, y, c, chips = _place()
        me = 2 * x + y
        sibling = (x, y, 1 - c)
        part = lambda core: pl.ds(pl.multiple_of(core * part_rows, 16), part_rows)
        remote = lambda src, dst, ss, rs, to: pltpu.make_async_remote_copy(
            src_ref=src, dst_ref=dst, send_sem=ss, recv_sem=rs, device_id=to, device_id_type=_MESH)
        local = [pltpu.make_async_copy(pack_ref, land_ref.at[me], local_sems.at[0]),
                 pltpu.make_async_copy(conv_ref, conv_land_ref.at[me], local_sems.at[1])]
        for cp in local:
            cp.start()
        first = [remote(pack_ref.at[part(c)], land_ref.at[me, part(c)], send1.at[j], recv1.at[j], (px, py, c))
                 for j, (px, py) in enumerate(chips)]
        convs = [remote(conv_ref, conv_land_ref.at[me], csend.at[j], crecv.at[j], (px, py, c))
                 for j, (px, py) in enumerate(chips)]
        for cp in first + convs:
            cp.start()
        passed = []
        for j, (px, py) in enumerate(chips):
            slot = 2 * px + py
            remote(pack_ref.at[part(c)], land_ref.at[slot, part(c)], send1.at[j], recv1.at[j], (px, py, c)).wait_recv()
            cp = remote(land_ref.at[slot, part(c)], land_ref.at[slot, part(c)], send2.at[j], recv2.at[j], sibling)
            cp.start()
            passed.append(cp)
        for j, (px, py) in enumerate(chips):
            slot = 2 * px + py
            remote(land_ref.at[slot, part(1 - c)], land_ref.at[slot, part(1 - c)], send2.at[j], recv2.at[j],
                   sibling).wait_recv()
            remote(conv_ref, conv_land_ref.at[slot], csend.at[j], crecv.at[j], (px, py, c)).wait_recv()
        for cp in first + convs + passed:
            cp.wait_send()
        for cp in local:
            cp.wait()

    sems = [pltpu.SemaphoreType.DMA((3,))] * 6 + [pltpu.SemaphoreType.DMA((2,))]
    return pl.pallas_call(
        body, name=name, in_specs=[_ANY, _ANY], out_specs=[_ANY, _ANY],
        out_shape=[jax.ShapeDtypeStruct((N_SHARD,) + pack.shape, pack.dtype),
                   jax.ShapeDtypeStruct((N_SHARD,) + conv_w.shape, conv_w.dtype)],
        scratch_shapes=sems)(pack, conv_w)


def _half(core):
    return pl.ds(pl.multiple_of(core * HALF, HALF), HALF)


def _reduce_scatter_exchange(g, small=None):
    ins = [g] if small is None else [g, small]
    outs = [jax.ShapeDtypeStruct((7,) + g.shape[1:2] + (HALF,), g.dtype)]
    if small is not None:
        outs.append(jax.ShapeDtypeStruct((8,) + small.shape, small.dtype))

    def plan(src, dst):
        x, y, c, chips = _place()
        me = 2 * x + y
        peers = [(px, py, c if t == 0 else 1 - c) for px, py in chips for t in (0, 1)] + [(x, y, 1 - c)]
        sends = [(src[0].at[2 * px + py, :, _half(pc)], dst[0].at[k], (px, py, pc)) for k, (px, py, pc) in enumerate(peers)]
        recvs = [dst[0].at[k] for k in range(7)]
        local = []
        if small is not None:
            mine = 4 * x + 2 * y + c
            local = [(src[1], dst[1].at[mine])]
            sends += [(src[1], dst[1].at[mine], peer) for peer in peers]
            recvs += [dst[1].at[4 * px + 2 * py + pc] for px, py, pc in peers]
        return local, sends, recvs

    return _Exchange(ins, outs, 7 * len(ins), len(ins) - 1, plan)


def _pair_window_exchange(g):
    def plan(src, dst):
        x, y, c, _ = _place()
        return [], [(src[0].at[:, :, _half(1 - c)], dst[0], (x, y, 1 - c))], [dst[0]]

    return _Exchange([g], [jax.ShapeDtypeStruct(g.shape[:2] + (HALF,), g.dtype)], 1, 0, plan)


def _chip_scatter_exchange(p, small):
    def plan(src, dst):
        x, y, c, chips = _place()
        mine = 4 * x + 2 * y + c
        peers = [(px, py, c if t == 0 else 1 - c) for px, py in chips for t in (0, 1)] + [(x, y, 1 - c)]
        sends = [(src[0].at[2 * px + py], dst[0].at[j], (px, py, c)) for j, (px, py) in enumerate(chips)]
        recvs = [dst[0].at[j] for j in range(3)]
        sends += [(src[1], dst[1].at[mine], peer) for peer in peers]
        recvs += [dst[1].at[4 * px + 2 * py + pc] for px, py, pc in peers]
        return [(src[1], dst[1].at[mine])], sends, recvs

    outs = [jax.ShapeDtypeStruct((3,) + p.shape[1:], p.dtype), jax.ShapeDtypeStruct((8,) + small.shape, small.dtype)]
    return _Exchange([p, small], outs, 10, 1, plan)


def _share_exchange(arrays):
    n = len(arrays)

    def plan(src, dst):
        x, y, c, _ = _place()
        return [], [(src[k], dst[k], (x, y, 1 - c)) for k in range(n)], [dst[k] for k in range(n)]

    return _Exchange(arrays, [jax.ShapeDtypeStruct(a.shape, a.dtype) for a in arrays], n, 0, plan)


def _sum_scatter(g, land, me, core, *, tc, name):
    rows = g.shape[1]
    per = HALF // tc

    def body(where_ref, g_ref, land_ref, o_ref):
        acc = g_ref[...]
        for k in range(7):
            acc = acc + land_ref[k].astype(F32)
        o_ref[...] = acc

    return pl.pallas_call(
        body, name=name, out_shape=jax.ShapeDtypeStruct((rows, HALF), F32), compiler_params=_cp("parallel"),
        grid_spec=pltpu.PrefetchScalarGridSpec(
            num_scalar_prefetch=1, grid=(per,),
            in_specs=[pl.BlockSpec((None, rows, tc), lambda i, w: (w[0], 0, w[1] * per + i)),
                      pl.BlockSpec((7, rows, tc), lambda i, w: (0, 0, i))],
            out_specs=pl.BlockSpec((rows, tc), lambda i, w: (0, i))))(
        jnp.stack([me, core]).astype(jnp.int32), g, land)


def _pair_add(g, land, core, *, name):
    n, rows, _ = g.shape

    def body(core_ref, g_ref, land_ref, o_ref):
        o_ref[...] = (g_ref[...].astype(F32) + land_ref[...].astype(F32)).astype(o_ref.dtype)

    blk = pl.BlockSpec((1, rows, HALF), lambda i, w: (i, 0, 0))
    return pl.pallas_call(
        body, name=name, out_shape=jax.ShapeDtypeStruct((n, rows, HALF), g.dtype), compiler_params=_cp("parallel"),
        grid_spec=pltpu.PrefetchScalarGridSpec(
            num_scalar_prefetch=1, grid=(n,),
            in_specs=[pl.BlockSpec((1, rows, HALF), lambda i, w: (i, 0, w[0])), blk], out_specs=blk))(
        jnp.reshape(core, (1,)).astype(jnp.int32), g, land)


def _sum_chips(p, land, me, *, tc, name):
    rows = p.shape[1]

    def body(me_ref, p_ref, land_ref, o_ref):
        acc = p_ref[...].astype(F32)
        for k in range(3):
            acc = acc + land_ref[k].astype(F32)
        o_ref[...] = acc

    return pl.pallas_call(
        body, name=name, out_shape=jax.ShapeDtypeStruct((rows, HALF), F32), compiler_params=_cp("parallel"),
        grid_spec=pltpu.PrefetchScalarGridSpec(
            num_scalar_prefetch=1, grid=(HALF // tc,),
            in_specs=[pl.BlockSpec((None, rows, tc), lambda i, w: (w[0], 0, i)),
                      pl.BlockSpec((3, rows, tc), lambda i, w: (0, 0, i))],
            out_specs=pl.BlockSpec((rows, tc), lambda i, w: (0, i))))(
        jnp.reshape(me, (1,)).astype(jnp.int32), p, land)


def _sum_slots(a, *, name):
    n = a.shape[0]

    def body(a_ref, o_ref):
        acc = a_ref[0]
        for k in range(1, n):
            acc = acc + a_ref[k]
        o_ref[...] = acc

    return pl.pallas_call(body, name=name, out_shape=jax.ShapeDtypeStruct(a.shape[1:], a.dtype))(a)


def _elementwise(fn, ins, n_out, block, *, name):
    shape = ins[0].shape
    grid = tuple(s // b for s, b in zip(shape, block))
    n_in = len(ins)

    def body(*refs):
        outs = fn(*[r[...] for r in refs[:n_in]])
        for o_ref, val in zip(refs[n_in:], outs):
            o_ref[...] = val

    spec = pl.BlockSpec(block, lambda i, j, k: (i, j, k))
    return pl.pallas_call(body, name=name, grid=grid, in_specs=[spec] * n_in, out_specs=[spec] * n_out,
                          out_shape=[jax.ShapeDtypeStruct(shape, F32)] * n_out,
                          compiler_params=_cp(*["parallel"] * 3))(*ins)


def _adamw_math(w, g, m, v):
    mn = ADAM_B1 * m + (1.0 - ADAM_B1) * g
    vn = ADAM_B2 * v + (1.0 - ADAM_B2) * (g * g)
    m_hat = mn / (1.0 - ADAM_B1 ** ADAM_STEP)
    v_hat = vn / (1.0 - ADAM_B2 ** ADAM_STEP)
    return -ADAM_LR * (m_hat / (jnp.sqrt(v_hat) + ADAM_EPS) + ADAM_WD * w), mn, vn


def _adamw(w, g, m, v, block, *, name):
    return _elementwise(_adamw_math, [w, g, m, v], 3, block, name=name)


def _adamw_small(ws, gs, ms, vs, *, name):
    n = len(ws)

    def body(*refs):
        w, g, m, v, outs = refs[:n], refs[n:2 * n], refs[2 * n:3 * n], refs[3 * n:4 * n], refs[4 * n:]
        for k in range(n):
            for slot, val in enumerate(_adamw_math(w[k][...], g[k][...], m[k][...], v[k][...])):
                outs[slot * n + k][...] = val

    outs = pl.pallas_call(body, name=name, out_shape=[jax.ShapeDtypeStruct(a.shape, F32) for a in ws] * 3)(
        *ws, *gs, *ms, *vs)
    return outs[:n], outs[n:2 * n], outs[2 * n:]


def _to_kernel_order(wt):
    gates = jnp.pad(wt[2048:2056], ((0, LANE - 2 * A_HEADS), (0, 0)))
    return jnp.concatenate([wt[0:2048], wt[2056:2568], wt[2824:3336], wt[2568:2696], wt[2696:2824], gates], axis=0)


def _from_kernel_order(main, tail):
    return jnp.concatenate([main[0:2048], tail[C_BG - DH_MAIN:C_BG - DH_MAIN + 2 * A_HEADS],
                            main[C_QB:C_QB + B_WIDTH], tail[0:B_KV_WIDTH], tail[B_KV_WIDTH:2 * B_KV_WIDTH],
                            main[C_ZB:C_ZB + B_WIDTH]], axis=0)


def _gate_params(a_log, dt_bias):
    return jnp.pad(jnp.stack([a_log, dt_bias]), ((0, SUBLANE - 2), (A_HEADS, LANE - 2 * A_HEADS)))


SMALL = ("conv_w", "a_log", "dt_bias", "norm_w", "sinks", "ln_g", "ln_b")


def _pack(parts, cols):
    flat = jnp.concatenate([p.reshape(-1) for p in parts])
    rows = -(-flat.shape[0] // cols)
    return jnp.pad(flat, (0, rows * cols - flat.shape[0])).reshape(rows, cols)


def _unpack(packed, shapes):
    flat = packed.reshape(-1)
    out, at = [], 0
    for s in shapes:
        n = math.prod(s)
        out.append(flat[at:at + n].reshape(s))
        at += n
    return out


def kernel(x, w_in, conv_w, a_log, dt_bias, norm_w, sinks, w_out, ln_g, ln_b, loss_target, m_w_in, m_conv_w, m_a_log, m_dt_bias, m_norm_w, m_sinks, m_w_out, m_ln_g, m_ln_b, v_w_in, v_conv_w, v_a_log, v_dt_bias, v_norm_w, v_sinks, v_w_out, v_ln_g, v_ln_b):
    xi, yi, ci = lax.axis_index("x"), lax.axis_index("y"), lax.axis_index("c")
    me = 2 * xi + yi

    to_t = lambda a: jnp.transpose(a, (2, 0, 1))
    from_t = lambda a: jnp.transpose(a, (1, 2, 0))

    wt_shard = to_t(w_in)

    def pack_weights(l):
        rows = jnp.pad(wt_shard[:, l], ((0, IN_PAD - IN_SHARD), (0, 0)))
        return jnp.concatenate([rows, w_out[l]], axis=0).astype(BF16)

    pack0, pack1 = pack_weights(0), pack_weights(1)
    got_in0, g_conv = _gather_two_level(pack0[:IN_PAD], conv_w, name="gather_weights_0")
    conv_full = jnp.moveaxis(g_conv, 0, 2).reshape(DEPTH, CONV_K, 3 * A_WIDTH)
    gathers = dict(in_proj=_gather_exchange([pack0[IN_PAD:]]), dn_wy=_gather_exchange([pack1[:PACK_SPLIT]]),
                   swa=_gather_exchange([pack1[PACK_SPLIT:]]))
    w_in_of = lambda rows: _to_kernel_order(rows.reshape(IN_COLS, D_MODEL))
    w_out_of = lambda rows: rows.reshape(D_MODEL, D_MODEL)
    args0 = _layer_args(w_in_of(got_in0[:, :IN_SHARD]), conv_full[0], a_log[0], dt_bias[0], sinks[0], norm_w[0],
                        lambda got: w_out_of(got[0]))

    def args1(got):
        first, rest = got["dn_wy"][0], got["swa"][0]
        rows = jnp.concatenate([first, rest[:, :IN_SHARD - PACK_SPLIT]], axis=1)
        return _layer_args(w_in_of(rows), conv_full[1], a_log[1], dt_bias[1], sinks[1], norm_w[1],
                           w_out_of(rest[:, IN_PAD - PACK_SPLIT:]))

    def pack_grads(g):
        gin = _from_kernel_order(*g["w_in"]).reshape(N_SHARD, IN_SHARD, D_MODEL)
        gin = jnp.pad(gin, ((0, 0), (0, IN_PAD - IN_SHARD), (0, 0)))
        return jnp.concatenate([gin, g["w_out"].reshape(N_SHARD, OUT_SHARD, D_MODEL)], axis=1).astype(BF16)

    packed = {}

    def reduce1(grads1):
        packed[1] = pack_grads(grads1)
        return _reduce_scatter_exchange(packed[1])

    def reduce0(grads0, grads1, loss_tile):
        g0 = pack_grads(grads0)
        from_sibling = _run_exchange(_pair_window_exchange(g0), name="pair_reduce_0")[0]
        packed[0] = _pair_add(g0, from_sibling, ci, name="pair_add_0")
        gsmall = _pack([jnp.stack([g[nm] for g in (grads0, grads1)]) for nm in SMALL] + [loss_tile[0, 0:1]], D_MODEL)
        return _chip_scatter_exchange(packed[0], gsmall)

    _, dx, grads, landed1, (landed0, landed_small) = _local_step(
        x[0], loss_target[0], args0, args1, ln_g, ln_b, gathers=gathers, reduce1=reduce1, reduce0=reduce0)

    small_shapes = [(DEPTH,) + grads[0][nm].shape for nm in SMALL]
    halves = [_sum_chips(packed[0], landed0, me, tc=2 * LANE, name="reduce_sum_0"),
              _sum_scatter(packed[1], landed1[0], me, ci, tc=2 * LANE, name="reduce_sum_1")]
    s_small = _sum_slots(landed_small, name="reduce_sum_small")
    others = _run_exchange(_share_exchange(halves), name="pair_share")
    full = [jnp.where(ci == 0, jnp.concatenate([mine, other], axis=1), jnp.concatenate([other, mine], axis=1))
            for mine, other in zip(halves, others)]
    grad_in_t = jnp.stack([f[:IN_SHARD] for f in full], axis=1)
    grad_out = jnp.stack([f[IN_PAD:] for f in full])
    out_blk = (1, OUT_SHARD, D_MODEL)
    *small_grads, loss = _unpack(s_small, small_shapes + [()])
    gs = dict(zip(SMALL, small_grads))
    gs["conv_w"] = lax.dynamic_slice_in_dim(gs["conv_w"], me * CONV_SHARD, CONV_SHARD, axis=2)

    adam_in_blk = (IN_SHARD // 6, DEPTH, D_MODEL)
    d_in, nm_in, nv_in = (from_t(o) for o in _adamw(to_t(w_in), grad_in_t, to_t(m_w_in), to_t(v_w_in), adam_in_blk,
                                                    name="adamw_in"))
    d_out, nm_out, nv_out = _adamw(w_out, grad_out, m_w_out, v_w_out, out_blk, name="adamw_out")
    ws = dict(conv_w=conv_w, a_log=a_log, dt_bias=dt_bias, norm_w=norm_w, sinks=sinks, ln_g=ln_g, ln_b=ln_b)
    ms = dict(conv_w=m_conv_w, a_log=m_a_log, dt_bias=m_dt_bias, norm_w=m_norm_w, sinks=m_sinks, ln_g=m_ln_g, ln_b=m_ln_b)
    vs = dict(conv_w=v_conv_w, a_log=v_a_log, dt_bias=v_dt_bias, norm_w=v_norm_w, sinks=v_sinks, ln_g=v_ln_g, ln_b=v_ln_b)
    d_s, nm_s, nv_s = (dict(zip(SMALL, o)) for o in _adamw_small(*[[d[nm] for nm in SMALL] for d in (ws, gs, ms, vs)],
                                                                 name="adamw_small"))

    def in_order(big_in, small, big_out):
        return (big_in, small["conv_w"], small["a_log"], small["dt_bias"], small["norm_w"], small["sinks"], big_out,
                small["ln_g"], small["ln_b"])

    return (loss, dx[None], *in_order(from_t(grad_in_t), gs, grad_out), *in_order(d_in, d_s, d_out),
            *in_order(nm_in, nm_s, nm_out), *in_order(nv_in, nv_s, nv_out))
```

```python
import math

import jax
import jax.numpy as jnp
from jax import lax
from jax.experimental import pallas as pl
from jax.experimental.pallas import tpu as pltpu

F32 = jnp.float32
BF16 = jnp.bfloat16
HI = lax.Precision.HIGHEST

D_MODEL = 1024
DEPTH = 2
A_HEADS = 4
A_HEAD_DIM = 128
A_WIDTH = 512
CONV_K = 4
CHUNK = 64
B_Q_HEADS = 8
B_KV_HEADS = 2
B_HEAD_DIM = 64
B_GROUP = 4
B_WIDTH = 512
B_KV_WIDTH = 128
BLOCK = 128
IN_COLS = 3336
DEEPNORM_ALPHA = (2 * DEPTH) ** 0.25
LN_EPS = 1e-5
RMS_EPS = 1e-6
L2_EPS = 1e-6
ADAM_LR = 0.001
ADAM_B1 = 0.9
ADAM_B2 = 0.999
ADAM_EPS = 1e-08
ADAM_WD = 0.01
ADAM_STEP = 10

N_SHARD = 4
IN_SHARD = IN_COLS // N_SHARD
OUT_SHARD = D_MODEL // N_SHARD
CONV_SHARD = 3 * A_WIDTH // N_SHARD
IN_PAD = -(-IN_SHARD // 32) * 32
PACK_SPLIT = 432

P_COLS = 3456
C_PRE = 0
C_ZA = 1536
C_QB = 2048
C_ZB = 2560
C_KB = 3072
C_VB = 3200
C_BG = 3328
DH_MAIN = C_KB
LANE = 128
SUBLANE = 8
VMEM_LIMIT = 56 * 1024 * 1024
ALIBI = tuple(2.0 ** (-8.0 * (h + 1) / B_Q_HEADS) for h in range(B_Q_HEADS))
NEG = -1e30


def _cp(*sem):
    return pltpu.CompilerParams(dimension_semantics=sem, vmem_limit_bytes=VMEM_LIMIT)


def _dot(a, b):
    return jnp.dot(a.astype(BF16), b.astype(BF16), preferred_element_type=F32)


def _dot_nt(a, b):
    return lax.dot_general(a.astype(BF16), b.astype(BF16), (((1,), (1,)), ((), ())),
                           preferred_element_type=F32)


def _dot_tn(a, b):
    return lax.dot_general(a.astype(BF16), b.astype(BF16), (((0,), (0,)), ((), ())),
                           preferred_element_type=F32)


def _dot_hi(a, b):
    return jnp.dot(a, b, precision=HI, preferred_element_type=F32)


def _sigmoid(x):
    return jax.nn.sigmoid(x)


def _silu(x):
    return x * _sigmoid(x)


def _dsilu(x):
    s = _sigmoid(x)
    return s * (1.0 + x * (1.0 - s))


def _softplus(x):
    return jnp.maximum(x, 0.0) + jnp.log(1.0 + jnp.exp(-jnp.abs(x)))


def _shift_down(cur, before, s):
    if s == 0:
        return cur
    r = pltpu.roll(cur, s, 0)
    rb = pltpu.roll(before, s, 0)
    row = lax.broadcasted_iota(jnp.int32, before.shape, 0)
    head = jnp.where(row < s, rb, r[0:SUBLANE])
    return jnp.concatenate([head, r[SUBLANE:]], axis=0)


def _shift_up(cur, after, s):
    if s == 0:
        return cur
    n = cur.shape[0]
    r = pltpu.roll(cur, n - s, 0)
    ra = pltpu.roll(after, SUBLANE - s, 0)
    row = lax.broadcasted_iota(jnp.int32, after.shape, 0)
    tail = jnp.where(row >= SUBLANE - s, ra, r[n - SUBLANE:])
    return jnp.concatenate([r[:n - SUBLANE], tail], axis=0)


def _conv_fwd(cur, before, w):
    acc = cur * w[CONV_K - 1:CONV_K, :]
    for s in range(1, CONV_K):
        acc = acc + _shift_down(cur, before, s) * w[CONV_K - 1 - s:CONV_K - s, :]
    return acc


def _matmul_nt(a, bt, *, tm, name, carry=None):
    m, k = a.shape
    n = bt.shape[0]
    c_ins, c_in_specs, c_out_specs, c_outs, c_scratch = _carry_specs(carry)

    def body(*refs):
        a_ref, b_ref, o_ref = _carried(carry, refs, 2, 1, m // tm)
        o_ref[...] = _dot_nt(a_ref[...], b_ref[...])

    outs = pl.pallas_call(
        body, name=name, grid=(m // tm,),
        in_specs=[pl.BlockSpec((tm, k), lambda i: (i, 0)), pl.BlockSpec((n, k), lambda i: (0, 0))] + c_in_specs,
        out_specs=[pl.BlockSpec((tm, n), lambda i: (i, 0))] + c_out_specs,
        out_shape=[jax.ShapeDtypeStruct((m, n), F32)] + c_outs,
        scratch_shapes=c_scratch,
        compiler_params=_cp("arbitrary"))(a, bt, *c_ins)
    return outs[0], outs[1:]


def _dn_pre(h, conv_w, par, *, tt, name):
    t = h.shape[0]
    cw = 3 * A_WIDTH
    hb = tt // SUBLANE

    def body(pre_ref, halo_ref, bgi_ref, cw_ref, par_ref, q_ref, k_ref, v_ref, bg_ref, bgt_ref):
        i = pl.program_id(0)
        cur = pre_ref[...]
        before = jnp.where(i > 0, halo_ref[...], 0.0)
        s = _silu(_conv_fwd(cur, before, cw_ref[...]))
        for hd in range(A_HEADS):
            sl = slice(hd * LANE, (hd + 1) * LANE)
            tq = s[:, hd * LANE:(hd + 1) * LANE]
            q_ref[:, sl] = tq * (lax.rsqrt(jnp.sum(tq * tq, -1, keepdims=True) + L2_EPS) * (A_HEAD_DIM ** -0.5))
            tk = s[:, A_WIDTH + hd * LANE:A_WIDTH + (hd + 1) * LANE]
            k_ref[:, sl] = tk * lax.rsqrt(jnp.sum(tk * tk, -1, keepdims=True) + L2_EPS)
        v_ref[...] = s[:, 2 * A_WIDTH:]
        raw = bgi_ref[...]
        lane = lax.broadcasted_iota(jnp.int32, raw.shape, 1)
        is_a = (lane >= A_HEADS) & (lane < 2 * A_HEADS)
        g = jnp.where(is_a, -jnp.exp(par_ref[0:1, :]) * _softplus(raw + par_ref[1:2, :]), 0.0)
        gc = _dot_hi(_chunk_tri(tt, lower=True), g)
        bg = jnp.where(lane < A_HEADS, _sigmoid(raw), gc)
        bg_ref[...] = bg
        bgt_ref[...] = jnp.transpose(bg)[0:SUBLANE, :]

    wide = jax.ShapeDtypeStruct((t, A_WIDTH), F32)
    return pl.pallas_call(
        body, name=name, grid=(t // tt,),
        in_specs=[pl.BlockSpec((tt, cw), lambda i: (i, 0)),
                  pl.BlockSpec((SUBLANE, cw), lambda i: (jnp.maximum(i * hb - 1, 0), 0)),
                  pl.BlockSpec((tt, LANE), lambda i: (i, C_BG // LANE)),
                  pl.BlockSpec((CONV_K, cw), lambda i: (0, 0)),
                  pl.BlockSpec((SUBLANE, LANE), lambda i: (0, 0))],
        out_specs=[pl.BlockSpec((tt, A_WIDTH), lambda i: (i, 0))] * 3
        + [pl.BlockSpec((tt, LANE), lambda i: (i, 0)), pl.BlockSpec((SUBLANE, tt), lambda i: (0, i))],
        out_shape=[wide, wide, wide, jax.ShapeDtypeStruct((t, LANE), F32), jax.ShapeDtypeStruct((SUBLANE, t), F32)],
        compiler_params=_cp("parallel"))(h, h, h, conv_w, par)


def _chunk_tri(n, lower):
    r = lax.broadcasted_iota(jnp.int32, (n, n), 0)
    c = lax.broadcasted_iota(jnp.int32, (n, n), 1)
    shift = CHUNK.bit_length() - 1
    same = jnp.right_shift(r, shift) == jnp.right_shift(c, shift)
    return (same & ((c <= r) if lower else (c >= r))).astype(F32)


def _chunk_masks():
    r = lax.broadcasted_iota(jnp.int32, (CHUNK, CHUNK), 0)
    c = lax.broadcasted_iota(jnp.int32, (CHUNK, CHUNK), 1)
    return r >= c, r > c, r == c


def _split(a):
    hi = a.astype(BF16)
    return hi, (a - hi.astype(F32)).astype(BF16)


def _dot3(a, b):
    (ah, al), (bh, bl) = a, b
    d = lambda p, q: jnp.dot(p, q, preferred_element_type=F32)
    return d(ah, bh) + (d(ah, bl) + d(al, bh))


def _tri_inv_many(a_list, eye):
    d = lambda p, q: jnp.dot(p, q, preferred_element_type=F32)
    p = [(-a).astype(BF16) for a in a_list]
    tm = [eye - a for a in a_list]
    for _ in range(5):
        pf = [d(pi, pi) for pi in p]
        p = [x.astype(BF16) for x in pf]
        tm = [t + d(t.astype(BF16), pi) for t, pi in zip(tm, p)]
    ms = [_split(eye + a) for a in a_list]
    res = [eye - _dot3(m, _split(t)) for m, t in zip(ms, tm)]
    return [t + d(t.astype(BF16), r.astype(BF16)) for t, r in zip(tm, res)]


def _chunk_gates(bg_v, bgt_v, hd):
    return (bg_v[:, hd:hd + 1], bg_v[:, A_HEADS + hd:A_HEADS + hd + 1],
            None if bgt_v is None else bgt_v[A_HEADS + hd:A_HEADS + hd + 1, :])


WY_ROWS = 512
SCAN_ROWS = 128
WY_GROUP = 8


def _dn_wy(q, k, v, bg, bgt, *, name, carry=None):
    t = q.shape[0]
    rows = WY_ROWS

    c_ins, c_in_specs, c_out_specs, c_outs, c_scratch = _carry_specs(carry)

    def body(*refs):
        q_ref, k_ref, v_ref, bg_ref, bgt_ref, u_ref, w_ref, tm_ref, qk_ref = _carried(carry, refs, 5, 4, t // rows)
        causal, strict, diag = _chunk_masks()
        eye = diag.astype(F32)
        for c0 in range(0, rows // CHUNK, WY_GROUP):
            items = [(c, hd) for c in range(c0, c0 + WY_GROUP) for hd in range(A_HEADS)]
            rs = lambda c: slice(c * CHUNK, (c + 1) * CHUNK)
            sl = lambda hd: slice(hd * LANE, (hd + 1) * LANE)
            hs = lambda hd: slice(hd * CHUNK, (hd + 1) * CHUNK)
            gates = [_chunk_gates(bg_ref[rs(c), :], bgt_ref[:, rs(c)], hd) for c, hd in items]
            dms = [jnp.exp(jnp.where(causal, gcol - grow, NEG)) for _, gcol, grow in gates]
            kbs = [k_ref[rs(c), sl(hd)] * g[0] for (c, hd), g in zip(items, gates)]
            a_list = [jnp.where(strict, _dot_nt(kb, k_ref[rs(c), sl(hd)]) * dm, 0.0)
                      for (c, hd), kb, dm in zip(items, kbs, dms)]
            for (c, hd), dm in zip(items, dms):
                qk_ref[rs(c), hs(hd)] = jnp.where(
                    causal, _dot_nt(q_ref[rs(c), sl(hd)], k_ref[rs(c), sl(hd)]) * dm, 0.0)
            tms = _tri_inv_many(a_list, eye)
            for (c, hd), g, kb, tmat in zip(items, gates, kbs, tms):
                tm_ref[rs(c), hs(hd)] = tmat
                u_ref[rs(c), sl(hd)] = _dot(tmat, v_ref[rs(c), sl(hd)] * g[0])
                w_ref[rs(c), sl(hd)] = _dot(tmat, kb * jnp.exp(g[1])).astype(BF16)

    blk = pl.BlockSpec((rows, A_WIDTH), lambda i: (i, 0))
    half = pl.BlockSpec((rows, A_HEADS * CHUNK), lambda i: (i, 0))
    outs = pl.pallas_call(
        body, name=name, grid=(t // rows,),
        in_specs=[blk, blk, blk, pl.BlockSpec((rows, LANE), lambda i: (i, 0)),
                  pl.BlockSpec((SUBLANE, rows), lambda i: (0, i))] + c_in_specs,
        out_specs=[blk, blk, half, half] + c_out_specs,
        out_shape=[jax.ShapeDtypeStruct((t, A_WIDTH), F32), jax.ShapeDtypeStruct((t, A_WIDTH), BF16),
                   jax.ShapeDtypeStruct((t, A_HEADS * CHUNK), F32),
                   jax.ShapeDtypeStruct((t, A_HEADS * CHUNK), F32)] + c_outs,
        scratch_shapes=c_scratch,
        compiler_params=_cp("arbitrary"))(q, k, v, bg, bgt, *c_ins)
    return outs[:4], outs[4:]


def _dn_scan_fwd(q, k, u, w, qk, bg, *, name):
    t = q.shape[0]
    rows = SCAN_ROWS
    per = rows // CHUNK

    def body(q_ref, k_ref, u_ref, w_ref, qk_ref, bg_ref, o_ref, vn_ref, s_ref, state):
        @pl.when(pl.program_id(0) == 0)
        def _():
            state[...] = jnp.zeros_like(state)

        heads = range(A_HEADS)
        sl = lambda hd: slice(hd * LANE, (hd + 1) * LANE)
        s_cur = [state[hd] for hd in heads]
        for c in range(per):
            rs = slice(c * CHUNK, (c + 1) * CHUNK)
            bg_v = bg_ref[rs, :]
            gcols = [_chunk_gates(bg_v, None, hd)[1] for hd in heads]
            glasts = [gc[CHUNK - 1:CHUNK, :] for gc in gcols]
            for hd in heads:
                s_ref[c, hd] = s_cur[hd].astype(BF16)
            vns = [u_ref[rs, sl(hd)] - _dot(w_ref[rs, sl(hd)], s_cur[hd]) for hd in heads]
            qss = [_dot(q_ref[rs, sl(hd)] * jnp.exp(gcols[hd]), s_cur[hd]) for hd in heads]
            s_cur = [s_cur[hd] * jnp.exp(glasts[hd])
                     + _dot_tn(k_ref[rs, sl(hd)] * jnp.exp(glasts[hd] - gcols[hd]), vns[hd]) for hd in heads]
            for hd in heads:
                vn_ref[rs, sl(hd)] = vns[hd]
                o_ref[rs, sl(hd)] = qss[hd] + _dot(qk_ref[rs, hd * CHUNK:(hd + 1) * CHUNK], vns[hd])
        for hd in heads:
            state[hd] = s_cur[hd]

    blk = pl.BlockSpec((rows, A_WIDTH), lambda i: (i, 0))
    half = pl.BlockSpec((rows, A_HEADS * CHUNK), lambda i: (i, 0))
    wide = jax.ShapeDtypeStruct((t, A_WIDTH), F32)
    return pl.pallas_call(
        body, name=name, grid=(t // rows,),
        in_specs=[blk, blk, blk, blk, half, pl.BlockSpec((rows, LANE), lambda i: (i, 0))],
        out_specs=[blk, blk, pl.BlockSpec((per, A_HEADS, LANE, LANE), lambda i: (i, 0, 0, 0))],
        out_shape=[wide, wide, jax.ShapeDtypeStruct((t // CHUNK, A_HEADS, LANE, LANE), BF16)],
        scratch_shapes=[pltpu.VMEM((A_HEADS, LANE, LANE), F32)],
        compiler_params=_cp("arbitrary"))(q, k, u, w, qk, bg)


def _swa_neg_dist(n_blk):
    qi = lax.broadcasted_iota(jnp.int32, (BLOCK, 2 * BLOCK), 0)
    si = lax.broadcasted_iota(jnp.int32, (BLOCK, 2 * BLOCK), 1)
    dist = qi + BLOCK - si
    mask = (dist >= 0) & (dist < BLOCK) & ((si >= BLOCK) | (n_blk > 0))
    return jnp.where(mask, -dist.astype(F32), NEG)


def _stack_heads(ref, hk):
    return jnp.concatenate([ref[:, h * B_HEAD_DIM:(h + 1) * B_HEAD_DIM]
                            for h in range(hk * B_GROUP, (hk + 1) * B_GROUP)], axis=0)


def _swa_group_probs(q_ref, sk_ref, kband, vband, neg_dist):
    hks = range(B_KV_HEADS)
    heads = lambda hk: range(hk * B_GROUP, (hk + 1) * B_GROUP)
    ksl = lambda hk: slice(hk * B_HEAD_DIM, (hk + 1) * B_HEAD_DIM)
    ones = jnp.ones((2 * BLOCK, B_HEAD_DIM), BF16)
    qs = [_stack_heads(q_ref, hk) * (B_HEAD_DIM ** -0.5) for hk in hks]
    sink = [jnp.concatenate([jnp.broadcast_to(sk_ref[h:h + 1, 0:1], (BLOCK, 1)) for h in heads(hk)], axis=0)
            for hk in hks]
    s = [_dot_nt(qs[hk], kband[:, ksl(hk)]) + jnp.concatenate([ALIBI[h] * neg_dist for h in heads(hk)], axis=0)
         for hk in hks]
    m = [jnp.maximum(jnp.max(s[hk], axis=-1, keepdims=True), sink[hk]) for hk in hks]
    p = [jnp.exp(s[hk] - m[hk]) for hk in hks]
    oe = [jnp.dot(p[hk].astype(BF16), jnp.concatenate([vband[:, ksl(hk)].astype(BF16), ones], axis=1),
                  preferred_element_type=F32) for hk in hks]
    ps = [jnp.exp(sink[hk] - m[hk]) for hk in hks]
    inv = [1.0 / (oe[hk][:, B_HEAD_DIM:B_HEAD_DIM + 1] + ps[hk]) for hk in hks]
    return [(qs[hk], p[hk] * inv[hk], ps[hk] * inv[hk], oe[hk][:, :B_HEAD_DIM] * inv[hk]) for hk in hks]


def _swa_specs():
    qspec = lambda c0: pl.BlockSpec((BLOCK, B_WIDTH), lambda i: (i, c0 // B_WIDTH))
    cur = lambda c0: pl.BlockSpec((BLOCK, LANE), lambda i: (i, c0 // LANE))
    prev = lambda c0: pl.BlockSpec((BLOCK, LANE), lambda i: (jnp.maximum(i - 1, 0), c0 // LANE))
    return qspec, cur, prev


def _carried(carry, refs, n_in, n_out, steps):
    if carry is None:
        return refs
    ci, co = len(carry.ins), len(carry.outs)
    own = refs[:n_in] + refs[n_in + ci:n_in + ci + n_out] + refs[n_in + ci + n_out + co:len(refs) - 3]
    parts = refs[n_in:n_in + ci], refs[n_in + ci + n_out:n_in + ci + n_out + co], refs[len(refs) - 3:]

    @pl.when(pl.program_id(0) == 0)
    def _():
        carry.start(*parts)

    @pl.when(pl.program_id(0) == steps - 1)
    def _():
        carry.finish(*parts)

    return own


def _carry_specs(carry):
    if carry is None:
        return [], [], [], [], []
    return (list(carry.ins), [_ANY] * len(carry.ins), [_ANY] * len(carry.outs), list(carry.outs), carry.scratch())


def _swa_fwd(h, sinks_b, *, name, carry=None):
    t = h.shape[0]
    qspec, cur, prev = _swa_specs()
    c_ins, c_in_specs, c_out_specs, c_outs, c_scratch = _carry_specs(carry)

    def body(*refs):
        q_ref, kc_ref, kp_ref, vc_ref, vp_ref, sk_ref, o_ref = _carried(carry, refs, 6, 1, t // BLOCK)
        n_blk = pl.program_id(0)
        kband = jnp.concatenate([kp_ref[...], kc_ref[...]], axis=0)
        vband = jnp.concatenate([vp_ref[...], vc_ref[...]], axis=0)
        groups = _swa_group_probs(q_ref, sk_ref, kband, vband, _swa_neg_dist(n_blk))
        for hk, (_, _, _, o) in enumerate(groups):
            for g in range(B_GROUP):
                hq = hk * B_GROUP + g
                o_ref[:, hq * B_HEAD_DIM:(hq + 1) * B_HEAD_DIM] = o[g * BLOCK:(g + 1) * BLOCK]

    outs = pl.pallas_call(
        body, name=name, grid=(t // BLOCK,),
        in_specs=[qspec(C_QB), cur(C_KB), prev(C_KB), cur(C_VB), prev(C_VB),
                  pl.BlockSpec((B_Q_HEADS, LANE), lambda i: (0, 0))] + c_in_specs,
        out_specs=[pl.BlockSpec((BLOCK, B_WIDTH), lambda i: (i, 0))] + c_out_specs,
        out_shape=[jax.ShapeDtypeStruct((t, B_WIDTH), F32)] + c_outs,
        scratch_shapes=c_scratch,
        compiler_params=_cp("arbitrary"))(h, h, h, h, h, sinks_b, *c_ins)
    return outs[0], outs[1:]


def _rms_gate(o, za, nw):
    outs = []
    for hd in range(A_HEADS):
        oh = o[:, hd * LANE:(hd + 1) * LANE]
        r = lax.rsqrt(jnp.mean(oh * oh, -1, keepdims=True) + RMS_EPS)
        outs.append(oh * r * nw)
    return jnp.concatenate(outs, axis=1) * _silu(za)


def _out_ln(x, oa, ob, h, norm_w, w_out, ln_g, ln_b, *, tm, name, target=None):
    t = x.shape[0]
    last = target is not None

    def body(*refs):
        x_ref, oa_ref, ob_ref, za_ref, zb_ref, nw_ref, w_ref, g_ref, b_ref = refs[:9]
        xn_ref, mx_ref, r_ref = refs[9 + last:12 + last]
        ya = _rms_gate(oa_ref[...], za_ref[...], nw_ref[...])
        yb = ob_ref[...] * _silu(zb_ref[...])
        mixed = jnp.concatenate([ya, yb], axis=1).astype(BF16)
        mx_ref[...] = mixed
        r = DEEPNORM_ALPHA * x_ref[...] + jnp.dot(mixed, w_ref[...], preferred_element_type=F32)
        r_ref[...] = r
        mu = jnp.mean(r, -1, keepdims=True)
        xc = r - mu
        var = jnp.mean(xc * xc, -1, keepdims=True)
        xn = xc * lax.rsqrt(var + LN_EPS) * g_ref[...] + b_ref[...]
        if not last:
            xn_ref[...] = xn
            return
        loss_ref = refs[13]

        @pl.when(pl.program_id(0) == 0)
        def _():
            loss_ref[...] = jnp.zeros_like(loss_ref)

        err = xn - refs[9][...]
        xn_ref[...] = err * (1.0 / D_MODEL)
        loss_ref[...] += 0.5 / D_MODEL * jnp.sum(err * err)

    row = lambda w, c: pl.BlockSpec((tm, w), lambda i: (i, c))
    full = lambda a, b: pl.BlockSpec((a, b), lambda i: (0, 0))
    wide = jax.ShapeDtypeStruct((t, D_MODEL), F32)
    return pl.pallas_call(
        body, name=name, grid=(t // tm,),
        in_specs=[row(D_MODEL, 0), row(A_WIDTH, 0), row(B_WIDTH, 0), row(A_WIDTH, C_ZA // A_WIDTH),
                  row(B_WIDTH, C_ZB // B_WIDTH), full(1, LANE), full(D_MODEL, D_MODEL), full(1, D_MODEL),
                  full(1, D_MODEL)] + [row(D_MODEL, 0)] * last,
        out_specs=[row(D_MODEL, 0), row(D_MODEL, 0), row(D_MODEL, 0)] + [full(SUBLANE, LANE)] * last,
        out_shape=[wide, jax.ShapeDtypeStruct((t, D_MODEL), BF16), wide]
        + [jax.ShapeDtypeStruct((SUBLANE, LANE), F32)] * last,
        compiler_params=_cp("arbitrary" if last else "parallel"))(
        x, oa, ob, h, h, norm_w, w_out, ln_g, ln_b, *([target] if last else []))


def _layer_fwd(x, wt, conv_w, par, sinks_b, norm_w, w_out_bf, ln_g, ln_b, l, carries=None, target=None):
    carries = carries or {}
    h, got_in = _matmul_nt(x, wt, tm=512, name=f"in_proj_{l}", carry=carries.get("in_proj"))
    if callable(w_out_bf):
        w_out_bf = w_out_bf(got_in)
    q, k, v, bg, bgt = _dn_pre(h, conv_w, par, tt=512, name=f"dn_pre_{l}")
    (u, w, tmat, qk), got_wy = _dn_wy(q, k, v, bg, bgt, name=f"dn_wy_{l}", carry=carries.get("dn_wy"))
    oa, vn, s_all = _dn_scan_fwd(q, k, u, w, qk, bg, name=f"dn_scan_{l}")
    ob, got_swa = _swa_fwd(h, sinks_b, name=f"swa_fwd_{l}", carry=carries.get("swa"))
    xn, mixed, r, *loss = _out_ln(x, oa, ob, h, norm_w, w_out_bf, ln_g, ln_b, tm=256, name=f"out_ln_{l}", target=target)
    if loss:
        xn = (xn, loss[0])
    res = dict(x=x, h=h, q=q, k=k, v=v, bg=bg, bgt=bgt, w=w, tmat=tmat, qk=qk, vn=vn, oa=oa, s_all=s_all,
               mixed=mixed, r=r, w_out=w_out_bf)
    return xn, res, dict(in_proj=got_in, dn_wy=got_wy, swa=got_swa)


def _ln_out_bwd(dxn, r, mixed, ln_g, w_out, *, tm, name):
    t = dxn.shape[0]

    def body(dxn_ref, r_ref, mx_ref, g_ref, w_ref, dr_ref, dm_ref, dw_ref, dg_ref, db_ref):
        @pl.when(pl.program_id(0) == 0)
        def _():
            dw_ref[...] = jnp.zeros_like(dw_ref)
            dg_ref[...] = jnp.zeros_like(dg_ref)
            db_ref[...] = jnp.zeros_like(db_ref)

        rr = r_ref[...]
        xc = rr - jnp.mean(rr, -1, keepdims=True)
        rstd = lax.rsqrt(jnp.mean(xc * xc, -1, keepdims=True) + LN_EPS)
        xhat = xc * rstd
        dxn_v = dxn_ref[...]
        dxh = dxn_v * g_ref[...]
        dr = rstd * (dxh - jnp.mean(dxh, -1, keepdims=True) - xhat * jnp.mean(dxh * xhat, -1, keepdims=True))
        dr_ref[...] = dr
        dg_ref[...] += jnp.sum(dxn_v * xhat, axis=0, keepdims=True)
        db_ref[...] += jnp.sum(dxn_v, axis=0, keepdims=True)
        drb = dr.astype(BF16)
        dm_ref[...] = _dot_nt(drb, w_ref[...])
        dw_ref[...] += _dot_tn(mx_ref[...], drb)

    row = pl.BlockSpec((tm, D_MODEL), lambda i: (i, 0))
    full = lambda a, b: pl.BlockSpec((a, b), lambda i: (0, 0))
    big = jax.ShapeDtypeStruct((t, D_MODEL), F32)
    vec = jax.ShapeDtypeStruct((1, D_MODEL), F32)
    return pl.pallas_call(
        body, name=name, grid=(t // tm,),
        in_specs=[row, row, row, full(1, D_MODEL), full(D_MODEL, D_MODEL)],
        out_specs=[row, row, full(D_MODEL, D_MODEL), full(1, D_MODEL), full(1, D_MODEL)],
        out_shape=[big, big, jax.ShapeDtypeStruct((D_MODEL, D_MODEL), F32), vec, vec],
        compiler_params=_cp("arbitrary"))(dxn, r, mixed, ln_g, w_out)


def _dn_post_bwd(dm, oa, h, norm_w, *, tm, name):
    t = oa.shape[0]

    def body(dy_ref, o_ref, za_ref, nw_ref, do_ref, dza_ref, dnw_ref):
        @pl.when(pl.program_id(0) == 0)
        def _():
            dnw_ref[...] = jnp.zeros_like(dnw_ref)

        nw = nw_ref[...]
        dnw = jnp.zeros_like(nw)
        for hd in range(A_HEADS):
            sl = slice(hd * LANE, (hd + 1) * LANE)
            oh, za, dy = o_ref[:, sl], za_ref[:, sl], dy_ref[:, sl]
            rs = lax.rsqrt(jnp.mean(oh * oh, -1, keepdims=True) + RMS_EPS)
            nrm = oh * rs
            dza_ref[:, sl] = dy * nrm * nw * _dsilu(za)
            dn = dy * _silu(za)
            dnw = dnw + jnp.sum(dn * nrm, axis=0, keepdims=True)
            dnn = dn * nw
            do_ref[:, sl] = rs * dnn - oh * (rs * rs * rs) * jnp.mean(dnn * oh, -1, keepdims=True)
        dnw_ref[...] += dnw

    row = lambda c: pl.BlockSpec((tm, A_WIDTH), lambda i: (i, c))
    wide = jax.ShapeDtypeStruct((t, A_WIDTH), F32)
    return pl.pallas_call(
        body, name=name, grid=(t // tm,),
        in_specs=[row(0), row(0), row(C_ZA // A_WIDTH), pl.BlockSpec((1, LANE), lambda i: (0, 0))],
        out_specs=[row(0), row(C_ZA // A_WIDTH), pl.BlockSpec((1, LANE), lambda i: (0, 0))],
        out_shape=[wide, jax.ShapeDtypeStruct((t, DH_MAIN), F32), jax.ShapeDtypeStruct((1, LANE), F32)],
        compiler_params=_cp("arbitrary"))(dm, oa, h, norm_w)


def _dn_scan_bwd(q, k, w, qk, bg, do, *, name):
    t = q.shape[0]
    rows = SCAN_ROWS
    per = rows // CHUNK
    n = t // rows

    def body(q_ref, k_ref, w_ref, qk_ref, bg_ref, do_ref, dvn_ref, ds_ref, dstate):
        @pl.when(pl.program_id(0) == 0)
        def _():
            dstate[...] = jnp.zeros_like(dstate)

        heads = range(A_HEADS)
        sl = lambda hd: slice(hd * LANE, (hd + 1) * LANE)
        ds_cur = [dstate[hd] for hd in heads]
        for c in reversed(range(per)):
            rs = slice(c * CHUNK, (c + 1) * CHUNK)
            bg_v = bg_ref[rs, :]
            gcols = [_chunk_gates(bg_v, None, hd)[1] for hd in heads]
            glasts = [gc[CHUNK - 1:CHUNK, :] for gc in gcols]
            for hd in heads:
                ds_ref[c, hd] = ds_cur[hd].astype(BF16)
            pdo = [_dot_tn(qk_ref[rs, hd * CHUNK:(hd + 1) * CHUNK], do_ref[rs, sl(hd)]) for hd in heads]
            qdo = [_dot_tn(q_ref[rs, sl(hd)] * jnp.exp(gcols[hd]), do_ref[rs, sl(hd)]) for hd in heads]
            dvns = [pdo[hd] + _dot(k_ref[rs, sl(hd)] * jnp.exp(glasts[hd] - gcols[hd]), ds_cur[hd]) for hd in heads]
            ds_cur = [qdo[hd] + jnp.exp(glasts[hd]) * ds_cur[hd] - _dot_tn(w_ref[rs, sl(hd)], dvns[hd])
                      for hd in heads]
            for hd in heads:
                dvn_ref[rs, sl(hd)] = dvns[hd]
        for hd in heads:
            dstate[hd] = ds_cur[hd]

    blk = pl.BlockSpec((rows, A_WIDTH), lambda i: (n - 1 - i, 0))
    return pl.pallas_call(
        body, name=name, grid=(n,),
        in_specs=[blk, blk, blk, pl.BlockSpec((rows, A_HEADS * CHUNK), lambda i: (n - 1 - i, 0)),
                  pl.BlockSpec((rows, LANE), lambda i: (n - 1 - i, 0)), blk],
        out_specs=[blk, pl.BlockSpec((per, A_HEADS, LANE, LANE), lambda i: (n - 1 - i, 0, 0, 0))],
        out_shape=[jax.ShapeDtypeStruct((t, A_WIDTH), F32),
                   jax.ShapeDtypeStruct((t // CHUNK, A_HEADS, LANE, LANE), BF16)],
        scratch_shapes=[pltpu.VMEM((A_HEADS, LANE, LANE), F32)],
        compiler_params=_cp("arbitrary"))(q, k, w, qk, bg, do)


def _dn_chunk_bwd(q, k, v, vn, tmat, qk, bg, bgt, s_all, ds_all, dvn, do, *, name):
    t = q.shape[0]
    rows = WY_ROWS
    per = rows // CHUNK

    def body(q_ref, k_ref, v_ref, vn_ref, tm_ref, qk_ref, bg_ref, bgt_ref, s_ref, ds_ref, dvn_ref, do_ref,
             dq_ref, dk_ref, dv_ref, dbg_ref, dbgt_ref):
        causal, strict, _ = _chunk_masks()
        lane = lax.broadcasted_iota(jnp.int32, (CHUNK, LANE), 1)
        rowi = lax.broadcasted_iota(jnp.int32, (CHUNK, 1), 0)
        sub = lax.broadcasted_iota(jnp.int32, (SUBLANE, CHUNK), 0)
        rs = lambda c: slice(c * CHUNK, (c + 1) * CHUNK)
        sl = lambda hd: slice(hd * LANE, (hd + 1) * LANE)
        hs = lambda hd: slice(hd * CHUNK, (hd + 1) * CHUNK)
        for c0 in range(0, per, WY_GROUP):
            items = [(c, hd) for c in range(c0, c0 + WY_GROUP) for hd in range(A_HEADS)]
            at = lambda ref: [ref[rs(c), sl(hd)] for c, hd in items]
            qs, ks, vs, dos, vns, dvns = at(q_ref), at(k_ref), at(v_ref), at(do_ref), at(vn_ref), at(dvn_ref)
            tmhs = [tm_ref[rs(c), hs(hd)] for c, hd in items]
            ps = [qk_ref[rs(c), hs(hd)] for c, hd in items]
            gates = [_chunk_gates(bg_ref[rs(c), :], bgt_ref[:, rs(c)], hd) for c, hd in items]
            betas = [g[0] for g in gates]
            gcols = [g[1] for g in gates]
            dmats = [jnp.exp(jnp.where(causal, g[1] - g[2], NEG)) for g in gates]
            es = [jnp.exp(gc) for gc in gcols]
            glasts = [gc[CHUNK - 1:CHUNK, :] for gc in gcols]
            eks = [jnp.exp(gl - gc) for gl, gc in zip(glasts, gcols)]
            kbs = [kh * b for kh, b in zip(ks, betas)]
            vbs = [vh * b for vh, b in zip(vs, betas)]
            kbes = [kb * e for kb, e in zip(kbs, es)]

            a_s = [jnp.where(strict, _dot_nt(kb, kh) * dm, 0.0) for kb, kh, dm in zip(kbs, ks, dmats)]
            dps = [jnp.where(causal, _dot_nt(doh, vnh), 0.0) for doh, vnh in zip(dos, vns)]
            dqds = [_dot_nt(doh, s_ref[c, hd]) for doh, (c, hd) in zip(dos, items)]
            dkds = [_dot_nt(vnh, ds_ref[c, hd]) for vnh, (c, hd) in zip(vns, items)]
            dws = [-_dot_nt(dvnh, s_ref[c, hd]) for dvnh, (c, hd) in zip(dvns, items)]
            dvbs = [_dot_tn(tmh, dvnh) for tmh, dvnh in zip(tmhs, dvns)]
            dgts = [jnp.sum(s_ref[c, hd].astype(F32) * ds_ref[c, hd].astype(F32), keepdims=True) for c, hd in items]
            dts = [_dot_nt(dvnh, vb) + _dot_nt(dw, kbe) for dvnh, vb, dw, kbe in zip(dvns, vbs, dws, kbes)]
            dkbes = [_dot_tn(tmh, dw) for tmh, dw in zip(tmhs, dws)]
            xs = [_dot_nt(dt, tmh) for dt, tmh in zip(dts, tmhs)]
            das = [jnp.where(strict, -_dot_tn(tmh, x), 0.0) for tmh, x in zip(tmhs, xs)]
            dmas = [da * dm for da, dm in zip(das, dmats)]
            dmps = [dp * dm for dp, dm in zip(dps, dmats)]
            dkbs = [_dot(dma, kh) + dkbe * e for dma, kh, dkbe, e in zip(dmas, ks, dkbes, es)]
            for i, (c, hd) in enumerate(items):
                dq_ref[rs(c), sl(hd)] = _dot(dmps[i], ks[i]) + dqds[i] * es[i]
                dk_ref[rs(c), sl(hd)] = (_dot_tn(dmas[i], kbs[i]) + _dot_tn(dmps[i], qs[i]) + dkds[i] * eks[i]
                                         + dkbs[i] * betas[i])
                dv_ref[rs(c), sl(hd)] = dvbs[i] * betas[i]
            for c in range(c0, c0 + WY_GROUP):
                acc = jnp.zeros((CHUNK, LANE), F32)
                acc_t = jnp.zeros((SUBLANE, CHUNK), F32)
                for i, (ci, hd) in enumerate(items):
                    if ci != c:
                        continue
                    gmat = das[i] * a_s[i] + dps[i] * ps[i]
                    rk = jnp.sum(dkds[i] * ks[i], -1, keepdims=True) * eks[i]
                    de = (jnp.sum(dqds[i] * qs[i], -1, keepdims=True)
                          + jnp.sum(dkbes[i] * kbs[i], -1, keepdims=True))
                    dglast = jnp.sum(rk, keepdims=True) + dgts[i] * jnp.exp(glasts[i])
                    dgc = (jnp.sum(gmat, -1, keepdims=True) + de * es[i] - rk
                           + jnp.where(rowi == CHUNK - 1, dglast, 0.0))
                    dbeta = (jnp.sum(dkbs[i] * ks[i], -1, keepdims=True)
                             + jnp.sum(dvbs[i] * vs[i], -1, keepdims=True))
                    acc = acc + jnp.where(lane == hd, dbeta, 0.0) + jnp.where(lane == A_HEADS + hd, dgc, 0.0)
                    acc_t = acc_t + jnp.where(sub == A_HEADS + hd, -jnp.sum(gmat, axis=0, keepdims=True), 0.0)
                dbg_ref[rs(c), :] = acc
                dbgt_ref[:, rs(c)] = acc_t

    blk = pl.BlockSpec((rows, A_WIDTH), lambda i: (i, 0))
    half = pl.BlockSpec((rows, A_HEADS * CHUNK), lambda i: (i, 0))
    col = pl.BlockSpec((rows, LANE), lambda i: (i, 0))
    rowf = pl.BlockSpec((SUBLANE, rows), lambda i: (0, i))
    st = pl.BlockSpec((per, A_HEADS, LANE, LANE), lambda i: (i, 0, 0, 0))
    wide = jax.ShapeDtypeStruct((t, A_WIDTH), F32)
    return pl.pallas_call(
        body, name=name, grid=(t // rows,),
        in_specs=[blk, blk, blk, blk, half, half, col, rowf, st, st, blk, blk],
        out_specs=[blk, blk, blk, col, rowf],
        out_shape=[wide, wide, wide, jax.ShapeDtypeStruct((t, LANE), F32), jax.ShapeDtypeStruct((SUBLANE, t), F32)],
        compiler_params=_cp("parallel"))(q, k, v, vn, tmat, qk, bg, bgt, s_all, ds_all, dvn, do)


def _dn_pre_bwd(h, conv_w, par, dq, dk, dv, dbg, dbgt, *, tt, name):
    t = h.shape[0]
    cw = 3 * A_WIDTH
    hb = tt // SUBLANE

    def body(pre_ref, halo_ref, bgi_ref, cw_ref, par_ref, dq_ref, dk_ref, dv_ref, dbg_ref, dbgt_ref,
             dc_ref, dbgi_ref, dpar_ref):
        i = pl.program_id(0)

        @pl.when(i == 0)
        def _():
            dpar_ref[...] = jnp.zeros_like(dpar_ref)

        cur = pre_ref[...]
        before = jnp.where(i > 0, halo_ref[...], 0.0)
        c = _conv_fwd(cur, before, cw_ref[...])
        s = _silu(c)
        ds = _dsilu(c)
        for hd in range(A_HEADS):
            sl = slice(hd * LANE, (hd + 1) * LANE)
            for base, d_ref, scale in ((0, dq_ref, A_HEAD_DIM ** -0.5), (A_WIDTH, dk_ref, 1.0)):
                csl = slice(base + hd * LANE, base + (hd + 1) * LANE)
                tq = s[:, base + hd * LANE:base + (hd + 1) * LANE]
                dy = d_ref[:, sl]
                rq = lax.rsqrt(jnp.sum(tq * tq, -1, keepdims=True) + L2_EPS)
                dtq = scale * (rq * dy - tq * (rq * rq * rq) * jnp.sum(dy * tq, -1, keepdims=True))
                dc_ref[:, csl] = dtq * ds[:, base + hd * LANE:base + (hd + 1) * LANE]
        dc_ref[:, 2 * A_WIDTH:] = dv_ref[...] * ds[:, 2 * A_WIDTH:]
        raw = bgi_ref[...]
        lane = lax.broadcasted_iota(jnp.int32, raw.shape, 1)
        is_b = lane < A_HEADS
        is_a = (lane >= A_HEADS) & (lane < 2 * A_HEADS)
        rows_t = jnp.concatenate([dbgt_ref[...], jnp.zeros((LANE - SUBLANE, tt), F32)], axis=0)
        dbg_v = dbg_ref[...] + jnp.where(is_a, jnp.transpose(rows_t), 0.0)
        dbg_v = jnp.where(is_a, _dot_hi(_chunk_tri(tt, lower=False), jnp.where(is_a, dbg_v, 0.0)), dbg_v)
        beta = _sigmoid(raw)
        z = raw + par_ref[1:2, :]
        neg_ea = -jnp.exp(par_ref[0:1, :])
        g = neg_ea * _softplus(z)
        da = dbg_v * neg_ea * _sigmoid(z)
        dbgi_ref[...] = jnp.where(is_b, dbg_v * beta * (1.0 - beta), jnp.where(is_a, da, 0.0))
        dpar_ref[0:1, :] += jnp.sum(jnp.where(is_a, dbg_v * g, 0.0), axis=0, keepdims=True)
        dpar_ref[1:2, :] += jnp.sum(jnp.where(is_a, da, 0.0), axis=0, keepdims=True)

    wide = pl.BlockSpec((tt, A_WIDTH), lambda i: (i, 0))
    return pl.pallas_call(
        body, name=name, grid=(t // tt,),
        in_specs=[pl.BlockSpec((tt, cw), lambda i: (i, 0)),
                  pl.BlockSpec((SUBLANE, cw), lambda i: (jnp.maximum(i * hb - 1, 0), 0)),
                  pl.BlockSpec((tt, LANE), lambda i: (i, C_BG // LANE)),
                  pl.BlockSpec((CONV_K, cw), lambda i: (0, 0)),
                  pl.BlockSpec((SUBLANE, LANE), lambda i: (0, 0)),
                  wide, wide, wide, pl.BlockSpec((tt, LANE), lambda i: (i, 0)),
                  pl.BlockSpec((SUBLANE, tt), lambda i: (0, i))],
        out_specs=[pl.BlockSpec((tt, cw), lambda i: (i, 0)), pl.BlockSpec((tt, LANE), lambda i: (i, 0)),
                   pl.BlockSpec((SUBLANE, LANE), lambda i: (0, 0))],
        out_shape=[jax.ShapeDtypeStruct((t, cw), F32), jax.ShapeDtypeStruct((t, LANE), F32),
                   jax.ShapeDtypeStruct((SUBLANE, LANE), F32)],
        compiler_params=_cp("arbitrary"))(h, h, h, conv_w, par, dq, dk, dv, dbg, dbgt)


def _conv_bwd(dc, h, conv_w, dh, *, tt, name):
    t = dc.shape[0]
    cw = 3 * A_WIDTH
    hb = tt // SUBLANE
    nb = t // tt

    def body(dc_ref, after_ref, pre_ref, before_ref, cw_ref, dh_in_ref, dpre_ref, dcw_ref):
        i = pl.program_id(0)

        @pl.when(i == 0)
        def _():
            dcw_ref[...] = jnp.zeros_like(dcw_ref)

        dcv = dc_ref[...]
        after = jnp.where(i < nb - 1, after_ref[...], 0.0)
        cur = pre_ref[...]
        before = jnp.where(i > 0, before_ref[...], 0.0)
        w = cw_ref[...]
        acc = dcv * w[CONV_K - 1:CONV_K, :]
        dcw_ref[CONV_K - 1:CONV_K, :] += jnp.sum(dcv * cur, axis=0, keepdims=True)
        for s in range(1, CONV_K):
            j = CONV_K - 1 - s
            acc = acc + _shift_up(dcv, after, s) * w[j:j + 1, :]
            dcw_ref[j:j + 1, :] += jnp.sum(dcv * _shift_down(cur, before, s), axis=0, keepdims=True)
        dpre_ref[...] = acc

    return pl.pallas_call(
        body, name=name, grid=(nb,),
        in_specs=[pl.BlockSpec((tt, cw), lambda i: (i, 0)),
                  pl.BlockSpec((SUBLANE, cw), lambda i: (jnp.minimum((i + 1) * hb, t // SUBLANE - 1), 0)),
                  pl.BlockSpec((tt, cw), lambda i: (i, 0)),
                  pl.BlockSpec((SUBLANE, cw), lambda i: (jnp.maximum(i * hb - 1, 0), 0)),
                  pl.BlockSpec((CONV_K, cw), lambda i: (0, 0)), _ANY],
        out_specs=[pl.BlockSpec((tt, cw), lambda i: (i, 0)), pl.BlockSpec((SUBLANE, cw), lambda i: (0, 0))],
        out_shape=[jax.ShapeDtypeStruct(dh.shape, F32), jax.ShapeDtypeStruct((SUBLANE, cw), F32)],
        input_output_aliases={5: 0},
        compiler_params=_cp("arbitrary"))(dc, dc, h, h, conv_w, dh)


def _swa_bwd(h, dm, sinks_b, dh, *, name, carry=None):
    t = h.shape[0]
    qspec, cur, prev = _swa_specs()
    c_ins, c_in_specs, c_out_specs, c_outs, c_scratch = _carry_specs(carry)

    def body(*refs):
        (q_ref, kc_ref, kp_ref, vc_ref, vp_ref, zb_ref, dy_ref, sk_ref, dh_in_ref,
         dqz_ref, dk_ref, dv_ref, dsk_ref) = _carried(carry, refs, 9, 4, t // BLOCK)
        n_blk = pl.program_id(0)

        @pl.when(n_blk == 0)
        def _():
            dk_ref[...] = jnp.zeros_like(dk_ref)
            dv_ref[...] = jnp.zeros_like(dv_ref)
            dsk_ref[...] = jnp.zeros_like(dsk_ref)

        kband = jnp.concatenate([kp_ref[...], kc_ref[...]], axis=0)
        vband = jnp.concatenate([vp_ref[...], vc_ref[...]], axis=0)
        scale = B_HEAD_DIM ** -0.5
        hks = range(B_KV_HEADS)
        ksl = lambda hk: slice(hk * B_HEAD_DIM, (hk + 1) * B_HEAD_DIM)
        groups = _swa_group_probs(q_ref, sk_ref, kband, vband, _swa_neg_dist(n_blk))
        zbs = [_stack_heads(zb_ref, hk) for hk in hks]
        dys = [_stack_heads(dy_ref, hk) for hk in hks]
        dos = [dys[hk] * _silu(zbs[hk]) for hk in hks]
        deltas = [jnp.sum(dos[hk] * groups[hk][3], -1, keepdims=True) for hk in hks]
        dss = [groups[hk][1] * (_dot_nt(dos[hk], vband[:, ksl(hk)]) - deltas[hk]) for hk in hks]
        dqs = [_dot(dss[hk], kband[:, ksl(hk)]) * scale for hk in hks]
        dk_acc = [_dot_tn(dss[hk], groups[hk][0]) for hk in hks]
        dv_acc = [_dot_tn(groups[hk][1], dos[hk]) for hk in hks]
        for hk in hks:
            dzb = dys[hk] * groups[hk][3] * _dsilu(zbs[hk])
            dsink = groups[hk][2] * deltas[hk]
            for g in range(B_GROUP):
                hq = hk * B_GROUP + g
                rows = slice(g * BLOCK, (g + 1) * BLOCK)
                qsl = slice(hq * B_HEAD_DIM, (hq + 1) * B_HEAD_DIM)
                dqz_ref[:, qsl] = dqs[hk][rows]
                dqz_ref[:, B_WIDTH + hq * B_HEAD_DIM:B_WIDTH + (hq + 1) * B_HEAD_DIM] = dzb[rows]
                dsk_ref[hq:hq + 1, :] += -jnp.sum(dsink[rows], keepdims=True)
        dkb = jnp.concatenate(dk_acc, axis=1)
        dvb = jnp.concatenate(dv_acc, axis=1)
        at_cur = pl.ds(pl.multiple_of(n_blk * BLOCK, BLOCK), BLOCK)
        at_prev = pl.ds(pl.multiple_of(jnp.maximum(n_blk - 1, 0) * BLOCK, BLOCK), BLOCK)
        dk_ref[at_prev, :] += dkb[:BLOCK]
        dv_ref[at_prev, :] += dvb[:BLOCK]
        dk_ref[at_cur, :] += dkb[BLOCK:]
        dv_ref[at_cur, :] += dvb[BLOCK:]

    narrow = jax.ShapeDtypeStruct((t, B_KV_WIDTH), F32)
    res = lambda a, b: pl.BlockSpec((a, b), lambda i: (0, 0))
    outs = pl.pallas_call(
        body, name=name, grid=(t // BLOCK,),
        in_specs=[qspec(C_QB), cur(C_KB), prev(C_KB), cur(C_VB), prev(C_VB), qspec(C_ZB),
                  pl.BlockSpec((BLOCK, B_WIDTH), lambda i: (i, 1)), res(B_Q_HEADS, LANE), _ANY] + c_in_specs,
        out_specs=[pl.BlockSpec((BLOCK, 2 * B_WIDTH), lambda i: (i, C_QB // (2 * B_WIDTH))),
                   res(t, B_KV_WIDTH), res(t, B_KV_WIDTH), res(B_Q_HEADS, LANE)] + c_out_specs,
        out_shape=[jax.ShapeDtypeStruct(dh.shape, F32), narrow, narrow,
                   jax.ShapeDtypeStruct((B_Q_HEADS, LANE), F32)] + c_outs,
        scratch_shapes=c_scratch,
        input_output_aliases={8: 0},
        compiler_params=_cp("arbitrary"))(h, h, h, h, h, h, dm, sinks_b, dh, *c_ins)
    return outs[:4], outs[4:]


def _matmul_tn(a, b, *, tk, tm, name):
    t, m = a.shape
    n = b.shape[1]

    def body(a_ref, b_ref, o_ref):
        @pl.when(pl.program_id(1) == 0)
        def _():
            o_ref[...] = jnp.zeros_like(o_ref)

        o_ref[...] += _dot_tn(a_ref[...], b_ref[...])

    return pl.pallas_call(
        body, name=name, grid=(m // tm, t // tk),
        in_specs=[pl.BlockSpec((tk, tm), lambda j, kk: (kk, j)), pl.BlockSpec((tk, n), lambda j, kk: (kk, 0))],
        out_specs=pl.BlockSpec((tm, n), lambda j, kk: (j, 0)),
        out_shape=jax.ShapeDtypeStruct((m, n), F32),
        compiler_params=_cp("parallel", "arbitrary"))(a, b)


def _in_proj_dx(dh_main, dh_tail, wt, dr, *, tm, name, carry=None):
    t, n_main = dh_main.shape
    n_tail = dh_tail.shape[1]
    c_ins, c_in_specs, c_out_specs, c_outs, c_scratch = _carry_specs(carry)

    def body(*refs):
        a_ref, t_ref, wa_ref, wt_ref, r_ref, o_ref = _carried(carry, refs, 5, 1, t // tm)
        o_ref[...] = _dot(a_ref[...], wa_ref[...]) + _dot(t_ref[...], wt_ref[...]) + DEEPNORM_ALPHA * r_ref[...]

    row = lambda w: pl.BlockSpec((tm, w), lambda i: (i, 0))
    outs = pl.pallas_call(
        body, name=name, grid=(t // tm,),
        in_specs=[row(n_main), row(n_tail), pl.BlockSpec((n_main, D_MODEL), lambda i: (0, 0)),
                  pl.BlockSpec((n_tail, D_MODEL), lambda i: (n_main // n_tail, 0)), row(D_MODEL)] + c_in_specs,
        out_specs=[row(D_MODEL)] + c_out_specs,
        out_shape=[jax.ShapeDtypeStruct((t, D_MODEL), F32)] + c_outs,
        scratch_shapes=c_scratch,
        compiler_params=_cp("arbitrary"))(dh_main, dh_tail, wt, wt, dr, *c_ins)
    return outs[0], outs[1:]


def _layer_bwd(dxn, res, wt, conv_w, par, sinks_b, norm_w, w_out_bf, ln_g, l, carry=None, carry_dx=None):
    w_out_bf = res["w_out"]
    dr, dm, dw_out, dln_g, dln_b = _ln_out_bwd(dxn, res["r"], res["mixed"], ln_g, w_out_bf, tm=256, name=f"ln_out_bwd_{l}")
    h = res["h"]
    do, dh, dnw = _dn_post_bwd(dm, res["oa"], h, norm_w, tm=512, name=f"dn_post_bwd_{l}")
    dvn, ds_all = _dn_scan_bwd(res["q"], res["k"], res["w"], res["qk"], res["bg"], do, name=f"dn_scan_bwd_{l}")
    dq, dk, dv, dbg, dbgt = _dn_chunk_bwd(res["q"], res["k"], res["v"], res["vn"], res["tmat"], res["qk"], res["bg"],
                                          res["bgt"], res["s_all"], ds_all, dvn, do, name=f"dn_chunk_bwd_{l}")
    dc, dbgi, dpar = _dn_pre_bwd(h, conv_w, par, dq, dk, dv, dbg, dbgt, tt=512, name=f"dn_pre_bwd_{l}")
    dh, dcw = _conv_bwd(dc, h, conv_w, dh, tt=512, name=f"conv_bwd_{l}")
    (dh, dkb, dvb, dsk), carried = _swa_bwd(h, dm, sinks_b, dh, name=f"swa_bwd_{l}", carry=carry)
    dh_tail = jnp.concatenate([dkb, dvb, dbgi], axis=1)
    dwt_main = _matmul_tn(dh, res["x"], tk=512, tm=768, name=f"in_proj_dw_{l}")
    dwt_tail = _matmul_tn(dh_tail, res["x"], tk=512, tm=P_COLS - DH_MAIN, name=f"in_proj_dw_tail_{l}")
    grads = dict(w_in=(dwt_main, dwt_tail), conv_w=dcw[:CONV_K], a_log=dpar[0, A_HEADS:2 * A_HEADS],
                 dt_bias=dpar[1, A_HEADS:2 * A_HEADS], norm_w=dnw[0], sinks=dsk[:, 0], w_out=dw_out,
                 ln_g=dln_g[0], ln_b=dln_b[0])
    dx, carried_dx = _in_proj_dx(dh, dh_tail, wt, dr, tm=256, name=f"in_proj_dx_{l}",
                                 carry=None if carry_dx is None else carry_dx(grads))
    return dx, grads, carried, carried_dx


def _layer_args(wt, conv_w, a_log, dt_bias, sinks, norm_w, w_out_bf):
    return (wt, conv_w, _gate_params(a_log, dt_bias), jnp.broadcast_to(sinks[:, None], (B_Q_HEADS, LANE)),
            norm_w[None], w_out_bf)


def _local_step(x, target, args0, args1, ln_g, ln_b, gathers=None, reduce1=None, reduce0=None):
    assert DEPTH == 2
    x1, res0, got = _layer_fwd(x, *args0, ln_g[0][None], ln_b[0][None], 0, carries=gathers)
    if gathers is not None:
        args1 = args1(got)
    (dx, loss_tile), res1, _ = _layer_fwd(x1, *args1, ln_g[1][None], ln_b[1][None], 1, target=target)
    dx, grads1, _, _ = _layer_bwd(dx, res1, *args1, ln_g[1][None], 1)
    carry = None if reduce1 is None else reduce1(grads1)
    carry_dx = None if reduce0 is None else (lambda grads0: reduce0(grads0, grads1, loss_tile))
    dx, grads0, landed1, landed0 = _layer_bwd(dx, res0, *args0, ln_g[0][None], 0, carry=carry, carry_dx=carry_dx)
    return loss_tile, dx, [grads0, grads1], landed1, landed0


_ANY = pl.BlockSpec(memory_space=pl.ANY)
_MESH = pl.DeviceIdType.MESH


HALF = D_MODEL // 2


class _Exchange:
    def __init__(self, ins, outs, n_remote, n_local, plan):
        self.ins, self.outs, self.n_remote, self.n_local, self.plan = tuple(ins), tuple(outs), n_remote, n_local, plan

    def scratch(self):
        return [pltpu.SemaphoreType.DMA((self.n_remote,)), pltpu.SemaphoreType.DMA((self.n_remote,)),
                pltpu.SemaphoreType.DMA((max(self.n_local, 1),))]

    def _copies(self, in_refs, out_refs, sems, arriving):
        send_sems, recv_sems, local_sems = sems
        local, sends, recvs = self.plan(in_refs, out_refs)
        loc = [pltpu.make_async_copy(s, d, local_sems.at[i]) for i, (s, d) in enumerate(local)]
        rem = [pltpu.make_async_remote_copy(src_ref=s, dst_ref=recvs[i] if arriving else d, send_sem=send_sems.at[i],
                                            recv_sem=recv_sems.at[i], device_id=peer, device_id_type=_MESH)
               for i, (s, d, peer) in enumerate(sends)]
        return loc, rem

    def start(self, in_refs, out_refs, sems):
        loc, rem = self._copies(in_refs, out_refs, sems, arriving=False)
        for cp in loc + rem:
            cp.start()

    def finish(self, in_refs, out_refs, sems):
        loc, rem = self._copies(in_refs, out_refs, sems, arriving=True)
        for cp in rem:
            cp.wait_recv()
        for cp in rem:
            cp.wait_send()
        for cp in loc:
            cp.wait()


def _run_exchange(ex, *, name):
    n_in, n_out = len(ex.ins), len(ex.outs)

    def body(*refs):
        parts = refs[:n_in], refs[n_in:n_in + n_out], refs[n_in + n_out:]
        ex.start(*parts)
        ex.finish(*parts)

    return pl.pallas_call(body, name=name, in_specs=[_ANY] * n_in, out_specs=[_ANY] * n_out, out_shape=list(ex.outs),
                          scratch_shapes=ex.scratch())(*ex.ins)


def _place():
    x, y, c = lax.axis_index("x"), lax.axis_index("y"), lax.axis_index("c")
    return x, y, c, [(1 - x, y), (x, 1 - y), (1 - x, 1 - y)]


def _gather_exchange(arrays):
    n = len(arrays)

    def plan(src, dst):
        x, y, c, chips = _place()
        me = 2 * x + y
        local = [(src[k], dst[k].at[me]) for k in range(n)]
        sends = [(src[k], dst[k].at[me], (px, py, c)) for k in range(n) for px, py in chips]
        recvs = [dst[k].at[2 * px + py] for k in range(n) for px, py in chips]
        return local, sends, recvs

    return _Exchange(arrays, [jax.ShapeDtypeStruct((N_SHARD,) + a.shape, a.dtype) for a in arrays], 3 * n, n, plan)


def _gather_two_level(pack, conv_w, *, name):
    rows = pack.shape[0]
    part_rows = rows // 2

    def body(pack_ref, conv_ref, land_ref, conv_land_ref, send1, recv1, send2, recv2, csend, crecv, local_sems):
        x, y, c, chips = _place()
        me = 2 * x + y
        sibling = (x, y, 1 - c)
        part = lambda core: pl.ds(pl.multiple_of(core * part_rows, 16), part_rows)
        remote = lambda src, dst, ss, rs, to: pltpu.make_async_remote_copy(
            src_ref=src, dst_ref=dst, send_sem=ss, recv_sem=rs, device_id=to, device_id_type=_MESH)
        local = [pltpu.make_async_copy(pack_ref, land_ref.at[me], local_sems.at[0]),
                 pltpu.make_async_copy(conv_ref, conv_land_ref.at[me], local_sems.at[1])]
        for cp in local:
            cp.start()
        first = [remote(pack_ref.at[part(c)], land_ref.at[me, part(c)], send1.at[j], recv1.at[j], (px, py, c))
                 for j, (px, py) in enumerate(chips)]
        convs = [remote(conv_ref, conv_land_ref.at[me], csend.at[j], crecv.at[j], (px, py, c))
                 for j, (px, py) in enumerate(chips)]
        for cp in first + convs:
            cp.start()
        passed = []
        for j, (px, py) in enumerate(chips):
            slot = 2 * px + py
            remote(pack_ref.at[part(c)], land_ref.at[slot, part(c)], send1.at[j], recv1.at[j], (px, py, c)).wait_recv()
            cp = remote(land_ref.at[slot, part(c)], land_ref.at[slot, part(c)], send2.at[j], recv2.at[j], sibling)
            cp.start()
            passed.append(cp)
        for j, (px, py) in enumerate(chips):
            slot = 2 * px + py
            remote(land_ref.at[slot, part(1 - c)], land_ref.at[slot, part(1 - c)], send2.at[j], recv2.at[j],
                   sibling).wait_recv()
            remote(conv_ref, conv_land_ref.at[slot], csend.at[j], crecv.at[j], (px, py, c)).wait_recv()
        for cp in first + convs + passed:
            cp.wait_send()
        for cp in local:
            cp.wait()

    sems = [pltpu.SemaphoreType.DMA((3,))] * 6 + [pltpu.SemaphoreType.DMA((2,))]
    return pl.pallas_call(
        body, name=name, in_specs=[_ANY, _ANY], out_specs=[_ANY, _ANY],
        out_shape=[jax.ShapeDtypeStruct((N_SHARD,) + pack.shape, pack.dtype),
                   jax.ShapeDtypeStruct((N_SHARD,) + conv_w.shape, conv_w.dtype)],
        scratch_shapes=sems)(pack, conv_w)


def _half(core):
    return pl.ds(pl.multiple_of(core * HALF, HALF), HALF)


def _reduce_scatter_exchange(g, small=None):
    ins = [g] if small is None else [g, small]
    outs = [jax.ShapeDtypeStruct((7,) + g.shape[1:2] + (HALF,), g.dtype)]
    if small is not None:
        outs.append(jax.ShapeDtypeStruct((8,) + small.shape, small.dtype))

    def plan(src, dst):
        x, y, c, chips = _place()
        me = 2 * x + y
        peers = [(px, py, c if t == 0 else 1 - c) for px, py in chips for t in (0, 1)] + [(x, y, 1 - c)]
        sends = [(src[0].at[2 * px + py, :, _half(pc)], dst[0].at[k], (px, py, pc)) for k, (px, py, pc) in enumerate(peers)]
        recvs = [dst[0].at[k] for k in range(7)]
        local = []
        if small is not None:
            mine = 4 * x + 2 * y + c
            local = [(src[1], dst[1].at[mine])]
            sends += [(src[1], dst[1].at[mine], peer) for peer in peers]
            recvs += [dst[1].at[4 * px + 2 * py + pc] for px, py, pc in peers]
        return local, sends, recvs

    return _Exchange(ins, outs, 7 * len(ins), len(ins) - 1, plan)


def _pair_window_exchange(g):
    def plan(src, dst):
        x, y, c, _ = _place()
        return [], [(src[0].at[:, :, _half(1 - c)], dst[0], (x, y, 1 - c))], [dst[0]]

    return _Exchange([g], [jax.ShapeDtypeStruct(g.shape[:2] + (HALF,), g.dtype)], 1, 0, plan)


def _chip_scatter_exchange(p, small):
    def plan(src, dst):
        x, y, c, chips = _place()
        mine = 4 * x + 2 * y + c
        peers = [(px, py, c if t == 0 else 1 - c) for px, py in chips for t in (0, 1)] + [(x, y, 1 - c)]
        sends = [(src[0].at[2 * px + py], dst[0].at[j], (px, py, c)) for j, (px, py) in enumerate(chips)]
        recvs = [dst[0].at[j] for j in range(3)]
        sends += [(src[1], dst[1].at[mine], peer) for peer in peers]
        recvs += [dst[1].at[4 * px + 2 * py + pc] for px, py, pc in peers]
        return [(src[1], dst[1].at[mine])], sends, recvs

    outs = [jax.ShapeDtypeStruct((3,) + p.shape[1:], p.dtype), jax.ShapeDtypeStruct((8,) + small.shape, small.dtype)]
    return _Exchange([p, small], outs, 10, 1, plan)


def _share_exchange(arrays):
    n = len(arrays)

    def plan(src, dst):
        x, y, c, _ = _place()
        return [], [(src[k], dst[k], (x, y, 1 - c)) for k in range(n)], [dst[k] for k in range(n)]

    return _Exchange(arrays, [jax.ShapeDtypeStruct(a.shape, a.dtype) for a in arrays], n, 0, plan)


def _sum_scatter(g, land, me, core, *, tc, name):
    rows = g.shape[1]
    per = HALF // tc

    def body(where_ref, g_ref, land_ref, o_ref):
        acc = g_ref[...]
        for k in range(7):
            acc = acc + land_ref[k].astype(F32)
        o_ref[...] = acc

    return pl.pallas_call(
        body, name=name, out_shape=jax.ShapeDtypeStruct((rows, HALF), F32), compiler_params=_cp("parallel"),
        grid_spec=pltpu.PrefetchScalarGridSpec(
            num_scalar_prefetch=1, grid=(per,),
            in_specs=[pl.BlockSpec((None, rows, tc), lambda i, w: (w[0], 0, w[1] * per + i)),
                      pl.BlockSpec((7, rows, tc), lambda i, w: (0, 0, i))],
            out_specs=pl.BlockSpec((rows, tc), lambda i, w: (0, i))))(
        jnp.stack([me, core]).astype(jnp.int32), g, land)


def _pair_add(g, land, core, *, name):
    n, rows, _ = g.shape

    def body(core_ref, g_ref, land_ref, o_ref):
        o_ref[...] = (g_ref[...].astype(F32) + land_ref[...].astype(F32)).astype(o_ref.dtype)

    blk = pl.BlockSpec((1, rows, HALF), lambda i, w: (i, 0, 0))
    return pl.pallas_call(
        body, name=name, out_shape=jax.ShapeDtypeStruct((n, rows, HALF), g.dtype), compiler_params=_cp("parallel"),
        grid_spec=pltpu.PrefetchScalarGridSpec(
            num_scalar_prefetch=1, grid=(n,),
            in_specs=[pl.BlockSpec((1, rows, HALF), lambda i, w: (i, 0, w[0])), blk], out_specs=blk))(
        jnp.reshape(core, (1,)).astype(jnp.int32), g, land)


def _sum_chips(p, land, me, *, tc, name):
    rows = p.shape[1]

    def body(me_ref, p_ref, land_ref, o_ref):
        acc = p_ref[...].astype(F32)
        for k in range(3):
            acc = acc + land_ref[k].astype(F32)
        o_ref[...] = acc

    return pl.pallas_call(
        body, name=name, out_shape=jax.ShapeDtypeStruct((rows, HALF), F32), compiler_params=_cp("parallel"),
        grid_spec=pltpu.PrefetchScalarGridSpec(
            num_scalar_prefetch=1, grid=(HALF // tc,),
            in_specs=[pl.BlockSpec((None, rows, tc), lambda i, w: (w[0], 0, i)),
                      pl.BlockSpec((3, rows, tc), lambda i, w: (0, 0, i))],
            out_specs=pl.BlockSpec((rows, tc), lambda i, w: (0, i))))(
        jnp.reshape(me, (1,)).astype(jnp.int32), p, land)


def _sum_slots(a, *, name):
    n = a.shape[0]

    def body(a_ref, o_ref):
        acc = a_ref[0]
        for k in range(1, n):
            acc = acc + a_ref[k]
        o_ref[...] = acc

    return pl.pallas_call(body, name=name, out_shape=jax.ShapeDtypeStruct(a.shape[1:], a.dtype))(a)


def _elementwise(fn, ins, n_out, block, *, name):
    shape = ins[0].shape
    grid = tuple(s // b for s, b in zip(shape, block))
    n_in = len(ins)

    def body(*refs):
        outs = fn(*[r[...] for r in refs[:n_in]])
        for o_ref, val in zip(refs[n_in:], outs):
            o_ref[...] = val

    spec = pl.BlockSpec(block, lambda i, j, k: (i, j, k))
    return pl.pallas_call(body, name=name, grid=grid, in_specs=[spec] * n_in, out_specs=[spec] * n_out,
                          out_shape=[jax.ShapeDtypeStruct(shape, F32)] * n_out,
                          compiler_params=_cp(*["parallel"] * 3))(*ins)


def _adamw_math(w, g, m, v):
    mn = ADAM_B1 * m + (1.0 - ADAM_B1) * g
    vn = ADAM_B2 * v + (1.0 - ADAM_B2) * (g * g)
    m_hat = mn / (1.0 - ADAM_B1 ** ADAM_STEP)
    v_hat = vn / (1.0 - ADAM_B2 ** ADAM_STEP)
    return -ADAM_LR * (m_hat / (jnp.sqrt(v_hat) + ADAM_EPS) + ADAM_WD * w), mn, vn


def _adamw(w, g, m, v, block, *, name):
    return _elementwise(_adamw_math, [w, g, m, v], 3, block, name=name)


def _adamw_layers(w, g_layers, m, v, *, tc, name):
    rows, layers, cols = w.shape

    def body(*refs):
        w_ref, m_ref, v_ref = refs[:3]
        g_refs = refs[3:3 + layers]
        g_out, d_out, m_out, v_out = refs[3 + layers:]
        for l in range(layers):
            g = g_refs[l][...]
            d, mn, vn = _adamw_math(w_ref[:, l, :], g, m_ref[:, l, :], v_ref[:, l, :])
            g_out[:, l, :] = g
            d_out[:, l, :] = d
            m_out[:, l, :] = mn
            v_out[:, l, :] = vn

    blk3 = pl.BlockSpec((rows, layers, tc), lambda i: (0, 0, i))
    blk2 = pl.BlockSpec((rows, tc), lambda i: (0, i))
    return pl.pallas_call(body, name=name, grid=(cols // tc,), in_specs=[blk3] * 3 + [blk2] * layers,
                          out_specs=[blk3] * 4, out_shape=[jax.ShapeDtypeStruct(w.shape, F32)] * 4,
                          compiler_params=_cp("parallel"))(w, m, v, *g_layers)


def _adamw_small(ws, gs, ms, vs, *, name):
    n = len(ws)

    def body(*refs):
        w, g, m, v, outs = refs[:n], refs[n:2 * n], refs[2 * n:3 * n], refs[3 * n:4 * n], refs[4 * n:]
        for k in range(n):
            for slot, val in enumerate(_adamw_math(w[k][...], g[k][...], m[k][...], v[k][...])):
                outs[slot * n + k][...] = val

    outs = pl.pallas_call(body, name=name, out_shape=[jax.ShapeDtypeStruct(a.shape, F32) for a in ws] * 3)(
        *ws, *gs, *ms, *vs)
    return outs[:n], outs[n:2 * n], outs[2 * n:]


def _to_kernel_order(wt):
    gates = jnp.pad(wt[2048:2056], ((0, LANE - 2 * A_HEADS), (0, 0)))
    return jnp.concatenate([wt[0:2048], wt[2056:2568], wt[2824:3336], wt[2568:2696], wt[2696:2824], gates], axis=0)


def _from_kernel_order(main, tail):
    return jnp.concatenate([main[0:2048], tail[C_BG - DH_MAIN:C_BG - DH_MAIN + 2 * A_HEADS],
                            main[C_QB:C_QB + B_WIDTH], tail[0:B_KV_WIDTH], tail[B_KV_WIDTH:2 * B_KV_WIDTH],
                            main[C_ZB:C_ZB + B_WIDTH]], axis=0)


def _gate_params(a_log, dt_bias):
    return jnp.pad(jnp.stack([a_log, dt_bias]), ((0, SUBLANE - 2), (A_HEADS, LANE - 2 * A_HEADS)))


SMALL = ("conv_w", "a_log", "dt_bias", "norm_w", "sinks", "ln_g", "ln_b")


def _pack(parts, cols):
    flat = jnp.concatenate([p.reshape(-1) for p in parts])
    rows = -(-flat.shape[0] // cols)
    return jnp.pad(flat, (0, rows * cols - flat.shape[0])).reshape(rows, cols)


def _unpack(packed, shapes):
    flat = packed.reshape(-1)
    out, at = [], 0
    for s in shapes:
        n = math.prod(s)
        out.append(flat[at:at + n].reshape(s))
        at += n
    return out


def kernel(x, w_in, conv_w, a_log, dt_bias, norm_w, sinks, w_out, ln_g, ln_b, loss_target, m_w_in, m_conv_w, m_a_log, m_dt_bias, m_norm_w, m_sinks, m_w_out, m_ln_g, m_ln_b, v_w_in, v_conv_w, v_a_log, v_dt_bias, v_norm_w, v_sinks, v_w_out, v_ln_g, v_ln_b):
    xi, yi, ci = lax.axis_index("x"), lax.axis_index("y"), lax.axis_index("c")
    me = 2 * xi + yi

    to_t = lambda a: jnp.transpose(a, (2, 0, 1))
    from_t = lambda a: jnp.transpose(a, (1, 2, 0))

    wt_shard = to_t(w_in)

    def pack_weights(l):
        rows = jnp.pad(wt_shard[:, l], ((0, IN_PAD - IN_SHARD), (0, 0)))
        return jnp.concatenate([rows, w_out[l]], axis=0).astype(BF16)

    pack0, pack1 = pack_weights(0), pack_weights(1)
    got_in0, g_conv = _gather_two_level(pack0[:IN_PAD], conv_w, name="gather_weights_0")
    conv_full = jnp.moveaxis(g_conv, 0, 2).reshape(DEPTH, CONV_K, 3 * A_WIDTH)
    gathers = dict(in_proj=_gather_exchange([pack0[IN_PAD:]]), dn_wy=_gather_exchange([pack1[:PACK_SPLIT]]),
                   swa=_gather_exchange([pack1[PACK_SPLIT:]]))
    w_in_of = lambda rows: _to_kernel_order(rows.reshape(IN_COLS, D_MODEL))
    w_out_of = lambda rows: rows.reshape(D_MODEL, D_MODEL)
    args0 = _layer_args(w_in_of(got_in0[:, :IN_SHARD]), conv_full[0], a_log[0], dt_bias[0], sinks[0], norm_w[0],
                        lambda got: w_out_of(got[0]))

    def args1(got):
        first, rest = got["dn_wy"][0], got["swa"][0]
        rows = jnp.concatenate([first, rest[:, :IN_SHARD - PACK_SPLIT]], axis=1)
        return _layer_args(w_in_of(rows), conv_full[1], a_log[1], dt_bias[1], sinks[1], norm_w[1],
                           w_out_of(rest[:, IN_PAD - PACK_SPLIT:]))

    def pack_grads(g):
        gin = _from_kernel_order(*g["w_in"]).reshape(N_SHARD, IN_SHARD, D_MODEL)
        gin = jnp.pad(gin, ((0, 0), (0, IN_PAD - IN_SHARD), (0, 0)))
        return jnp.concatenate([gin, g["w_out"].reshape(N_SHARD, OUT_SHARD, D_MODEL)], axis=1).astype(BF16)

    packed = {}

    def reduce1(grads1):
        packed[1] = pack_grads(grads1)
        return _reduce_scatter_exchange(packed[1])

    def reduce0(grads0, grads1, loss_tile):
        g0 = pack_grads(grads0)
        from_sibling = _run_exchange(_pair_window_exchange(g0), name="pair_reduce_0")[0]
        packed[0] = _pair_add(g0, from_sibling, ci, name="pair_add_0")
        gsmall = _pack([jnp.stack([g[nm] for g in (grads0, grads1)]) for nm in SMALL] + [loss_tile[0, 0:1]], D_MODEL)
        return _chip_scatter_exchange(packed[0], gsmall)

    _, dx, grads, landed1, (landed0, landed_small) = _local_step(
        x[0], loss_target[0], args0, args1, ln_g, ln_b, gathers=gathers, reduce1=reduce1, reduce0=reduce0)

    small_shapes = [(DEPTH,) + grads[0][nm].shape for nm in SMALL]
    halves = [_sum_chips(packed[0], landed0, me, tc=2 * LANE, name="reduce_sum_0"),
              _sum_scatter(packed[1], landed1[0], me, ci, tc=2 * LANE, name="reduce_sum_1")]
    s_small = _sum_slots(landed_small, name="reduce_sum_small")
    others = _run_exchange(_share_exchange(halves), name="pair_share")
    full = [jnp.where(ci == 0, jnp.concatenate([mine, other], axis=1), jnp.concatenate([other, mine], axis=1))
            for mine, other in zip(halves, others)]
    grad_in_layers = [f[:IN_SHARD] for f in full]
    grad_out = jnp.stack([f[IN_PAD:] for f in full])
    out_blk = (1, OUT_SHARD, D_MODEL)
    *small_grads, loss = _unpack(s_small, small_shapes + [()])
    gs = dict(zip(SMALL, small_grads))
    gs["conv_w"] = lax.dynamic_slice_in_dim(gs["conv_w"], me * CONV_SHARD, CONV_SHARD, axis=2)

    grad_in, d_in, nm_in, nv_in = (from_t(o) for o in _adamw_layers(to_t(w_in), grad_in_layers, to_t(m_w_in),
                                                                      to_t(v_w_in), tc=2 * LANE, name="adamw_in"))
    d_out, nm_out, nv_out = _adamw(w_out, grad_out, m_w_out, v_w_out, out_blk, name="adamw_out")
    ws = dict(conv_w=conv_w, a_log=a_log, dt_bias=dt_bias, norm_w=norm_w, sinks=sinks, ln_g=ln_g, ln_b=ln_b)
    ms = dict(conv_w=m_conv_w, a_log=m_a_log, dt_bias=m_dt_bias, norm_w=m_norm_w, sinks=m_sinks, ln_g=m_ln_g, ln_b=m_ln_b)
    vs = dict(conv_w=v_conv_w, a_log=v_a_log, dt_bias=v_dt_bias, norm_w=v_norm_w, sinks=v_sinks, ln_g=v_ln_g, ln_b=v_ln_b)
    d_s, nm_s, nv_s = (dict(zip(SMALL, o)) for o in _adamw_small(*[[d[nm] for nm in SMALL] for d in (ws, gs, ms, vs)],
                                                                 name="adamw_small"))

    def in_order(big_in, small, big_out):
        return (big_in, small["conv_w"], small["a_log"], small["dt_bias"], small["norm_w"], small["sinks"], big_out,
                small["ln_g"], small["ln_b"])

    return (loss, dx[None], *in_order(grad_in, gs, grad_out), *in_order(d_in, d_s, d_out),
            *in_order(nm_in, nm_s, nm_out), *in_order(nv_in, nv_s, nv_out))
```

```python
import math

import jax
import jax.numpy as jnp
from jax import lax
from jax.experimental import pallas as pl
from jax.experimental.pallas import tpu as pltpu

F32 = jnp.float32
BF16 = jnp.bfloat16
HI = lax.Precision.HIGHEST

D_MODEL = 1024
DEPTH = 2
A_HEADS = 4
A_HEAD_DIM = 128
A_WIDTH = 512
CONV_K = 4
CHUNK = 64
B_Q_HEADS = 8
B_KV_HEADS = 2
B_HEAD_DIM = 64
B_GROUP = 4
B_WIDTH = 512
B_KV_WIDTH = 128
BLOCK = 128
IN_COLS = 3336
DEEPNORM_ALPHA = (2 * DEPTH) ** 0.25
LN_EPS = 1e-5
RMS_EPS = 1e-6
L2_EPS = 1e-6
ADAM_LR = 0.001
ADAM_B1 = 0.9
ADAM_B2 = 0.999
ADAM_EPS = 1e-08
ADAM_WD = 0.01
ADAM_STEP = 10

N_SHARD = 4
IN_SHARD = IN_COLS // N_SHARD
OUT_SHARD = D_MODEL // N_SHARD
CONV_SHARD = 3 * A_WIDTH // N_SHARD
IN_PAD = -(-IN_SHARD // 96) * 96

P_COLS = 3456
C_PRE = 0
C_ZA = 1536
C_QB = 2048
C_ZB = 2560
C_KB = 3072
C_VB = 3200
C_BG = 3328
DH_MAIN = C_KB
LANE = 128
SUBLANE = 8
VMEM_LIMIT = 56 * 1024 * 1024
ALIBI = tuple(2.0 ** (-8.0 * (h + 1) / B_Q_HEADS) for h in range(B_Q_HEADS))
NEG = -1e30


def _cp(*sem):
    return pltpu.CompilerParams(dimension_semantics=sem, vmem_limit_bytes=VMEM_LIMIT)


def _dot(a, b):
    return jnp.dot(a.astype(BF16), b.astype(BF16), preferred_element_type=F32)


def _dot_nt(a, b):
    return lax.dot_general(a.astype(BF16), b.astype(BF16), (((1,), (1,)), ((), ())),
                           preferred_element_type=F32)


def _dot_tn(a, b):
    return lax.dot_general(a.astype(BF16), b.astype(BF16), (((0,), (0,)), ((), ())),
                           preferred_element_type=F32)


def _dot_hi(a, b):
    return jnp.dot(a, b, precision=HI, preferred_element_type=F32)


def _sigmoid(x):
    return jax.nn.sigmoid(x)


def _silu(x):
    return x * _sigmoid(x)


def _dsilu(x):
    s = _sigmoid(x)
    return s * (1.0 + x * (1.0 - s))


def _softplus(x):
    return jnp.maximum(x, 0.0) + jnp.log(1.0 + jnp.exp(-jnp.abs(x)))


def _shift_down(cur, before, s):
    if s == 0:
        return cur
    r = pltpu.roll(cur, s, 0)
    rb = pltpu.roll(before, s, 0)
    row = lax.broadcasted_iota(jnp.int32, before.shape, 0)
    head = jnp.where(row < s, rb, r[0:SUBLANE])
    return jnp.concatenate([head, r[SUBLANE:]], axis=0)


def _shift_up(cur, after, s):
    if s == 0:
        return cur
    n = cur.shape[0]
    r = pltpu.roll(cur, n - s, 0)
    ra = pltpu.roll(after, SUBLANE - s, 0)
    row = lax.broadcasted_iota(jnp.int32, after.shape, 0)
    tail = jnp.where(row >= SUBLANE - s, ra, r[n - SUBLANE:])
    return jnp.concatenate([r[:n - SUBLANE], tail], axis=0)


def _conv_fwd(cur, before, w):
    acc = cur * w[CONV_K - 1:CONV_K, :]
    for s in range(1, CONV_K):
        acc = acc + _shift_down(cur, before, s) * w[CONV_K - 1 - s:CONV_K - s, :]
    return acc


def _matmul_nt(a, bt, *, tm, name, carry=None):
    m, k = a.shape
    n = bt.shape[0]
    c_ins, c_in_specs, c_out_specs, c_outs, c_scratch = _carry_specs(carry)

    def body(*refs):
        a_ref, b_ref, o_ref = _carried(carry, refs, 2, 1, m // tm)
        o_ref[...] = _dot_nt(a_ref[...], b_ref[...])

    outs = pl.pallas_call(
        body, name=name, grid=(m // tm,),
        in_specs=[pl.BlockSpec((tm, k), lambda i: (i, 0)), pl.BlockSpec((n, k), lambda i: (0, 0))] + c_in_specs,
        out_specs=[pl.BlockSpec((tm, n), lambda i: (i, 0))] + c_out_specs,
        out_shape=[jax.ShapeDtypeStruct((m, n), F32)] + c_outs,
        scratch_shapes=c_scratch,
        compiler_params=_cp("arbitrary"))(a, bt, *c_ins)
    return outs[0], outs[1:]


def _dn_pre(h, conv_w, par, *, tt, name, carry=None):
    t = h.shape[0]
    cw = 3 * A_WIDTH
    hb = tt // SUBLANE

    c_ins, c_in_specs, c_out_specs, c_outs, c_scratch = _carry_specs(carry)

    def body(*refs):
        (pre_ref, halo_ref, bgi_ref, cw_ref, par_ref,
         q_ref, k_ref, v_ref, bg_ref, bgt_ref) = _carried(carry, refs, 5, 5, t // tt)
        i = pl.program_id(0)
        cur = pre_ref[...]
        before = jnp.where(i > 0, halo_ref[...], 0.0)
        s = _silu(_conv_fwd(cur, before, cw_ref[...]))
        for hd in range(A_HEADS):
            sl = slice(hd * LANE, (hd + 1) * LANE)
            tq = s[:, hd * LANE:(hd + 1) * LANE]
            q_ref[:, sl] = tq * (lax.rsqrt(jnp.sum(tq * tq, -1, keepdims=True) + L2_EPS) * (A_HEAD_DIM ** -0.5))
            tk = s[:, A_WIDTH + hd * LANE:A_WIDTH + (hd + 1) * LANE]
            k_ref[:, sl] = tk * lax.rsqrt(jnp.sum(tk * tk, -1, keepdims=True) + L2_EPS)
        v_ref[...] = s[:, 2 * A_WIDTH:]
        raw = bgi_ref[...]
        lane = lax.broadcasted_iota(jnp.int32, raw.shape, 1)
        is_a = (lane >= A_HEADS) & (lane < 2 * A_HEADS)
        g = jnp.where(is_a, -jnp.exp(par_ref[0:1, :]) * _softplus(raw + par_ref[1:2, :]), 0.0)
        gc = _dot_hi(_chunk_tri(tt, lower=True), g)
        bg = jnp.where(lane < A_HEADS, _sigmoid(raw), gc)
        bg_ref[...] = bg
        bgt_ref[...] = jnp.transpose(bg)[0:SUBLANE, :]

    wide = jax.ShapeDtypeStruct((t, A_WIDTH), F32)
    outs = pl.pallas_call(
        body, name=name, grid=(t // tt,),
        in_specs=[pl.BlockSpec((tt, cw), lambda i: (i, 0)),
                  pl.BlockSpec((SUBLANE, cw), lambda i: (jnp.maximum(i * hb - 1, 0), 0)),
                  pl.BlockSpec((tt, LANE), lambda i: (i, C_BG // LANE)),
                  pl.BlockSpec((CONV_K, cw), lambda i: (0, 0)),
                  pl.BlockSpec((SUBLANE, LANE), lambda i: (0, 0))] + c_in_specs,
        out_specs=[pl.BlockSpec((tt, A_WIDTH), lambda i: (i, 0))] * 3
        + [pl.BlockSpec((tt, LANE), lambda i: (i, 0)), pl.BlockSpec((SUBLANE, tt), lambda i: (0, i))] + c_out_specs,
        out_shape=[wide, wide, wide, jax.ShapeDtypeStruct((t, LANE), F32),
                   jax.ShapeDtypeStruct((SUBLANE, t), F32)] + c_outs,
        scratch_shapes=c_scratch,
        compiler_params=_cp("arbitrary"))(h, h, h, conv_w, par, *c_ins)
    return outs[:5], outs[5:]


def _chunk_tri(n, lower):
    r = lax.broadcasted_iota(jnp.int32, (n, n), 0)
    c = lax.broadcasted_iota(jnp.int32, (n, n), 1)
    shift = CHUNK.bit_length() - 1
    same = jnp.right_shift(r, shift) == jnp.right_shift(c, shift)
    return (same & ((c <= r) if lower else (c >= r))).astype(F32)


def _chunk_masks():
    r = lax.broadcasted_iota(jnp.int32, (CHUNK, CHUNK), 0)
    c = lax.broadcasted_iota(jnp.int32, (CHUNK, CHUNK), 1)
    return r >= c, r > c, r == c


def _split(a):
    hi = a.astype(BF16)
    return hi, (a - hi.astype(F32)).astype(BF16)


def _dot3(a, b):
    (ah, al), (bh, bl) = a, b
    d = lambda p, q: jnp.dot(p, q, preferred_element_type=F32)
    return d(ah, bh) + (d(ah, bl) + d(al, bh))


def _tri_inv_many(a_list, eye):
    d = lambda p, q: jnp.dot(p, q, preferred_element_type=F32)
    p = [(-a).astype(BF16) for a in a_list]
    tm = [eye - a for a in a_list]
    for _ in range(5):
        pf = [d(pi, pi) for pi in p]
        p = [x.astype(BF16) for x in pf]
        tm = [t + d(t.astype(BF16), pi) for t, pi in zip(tm, p)]
    ms = [_split(eye + a) for a in a_list]
    res = [eye - _dot3(m, _split(t)) for m, t in zip(ms, tm)]
    return [t + d(t.astype(BF16), r.astype(BF16)) for t, r in zip(tm, res)]


def _chunk_gates(bg_v, bgt_v, hd):
    return (bg_v[:, hd:hd + 1], bg_v[:, A_HEADS + hd:A_HEADS + hd + 1],
            None if bgt_v is None else bgt_v[A_HEADS + hd:A_HEADS + hd + 1, :])


WY_ROWS = 512
SCAN_ROWS = 128
WY_GROUP = 8


def _dn_wy(q, k, v, bg, bgt, *, name, carry=None):
    t = q.shape[0]
    rows = WY_ROWS

    c_ins, c_in_specs, c_out_specs, c_outs, c_scratch = _carry_specs(carry)

    def body(*refs):
        q_ref, k_ref, v_ref, bg_ref, bgt_ref, u_ref, w_ref, tm_ref, qk_ref = _carried(carry, refs, 5, 4, t // rows)
        causal, strict, diag = _chunk_masks()
        eye = diag.astype(F32)
        for c0 in range(0, rows // CHUNK, WY_GROUP):
            items = [(c, hd) for c in range(c0, c0 + WY_GROUP) for hd in range(A_HEADS)]
            rs = lambda c: slice(c * CHUNK, (c + 1) * CHUNK)
            sl = lambda hd: slice(hd * LANE, (hd + 1) * LANE)
            hs = lambda hd: slice(hd * CHUNK, (hd + 1) * CHUNK)
            gates = [_chunk_gates(bg_ref[rs(c), :], bgt_ref[:, rs(c)], hd) for c, hd in items]
            dms = [jnp.exp(jnp.where(causal, gcol - grow, NEG)) for _, gcol, grow in gates]
            kbs = [k_ref[rs(c), sl(hd)] * g[0] for (c, hd), g in zip(items, gates)]
            a_list = [jnp.where(strict, _dot_nt(kb, k_ref[rs(c), sl(hd)]) * dm, 0.0)
                      for (c, hd), kb, dm in zip(items, kbs, dms)]
            for (c, hd), dm in zip(items, dms):
                qk_ref[rs(c), hs(hd)] = jnp.where(
                    causal, _dot_nt(q_ref[rs(c), sl(hd)], k_ref[rs(c), sl(hd)]) * dm, 0.0)
            tms = _tri_inv_many(a_list, eye)
            for (c, hd), g, kb, tmat in zip(items, gates, kbs, tms):
                tm_ref[rs(c), hs(hd)] = tmat
                u_ref[rs(c), sl(hd)] = _dot(tmat, v_ref[rs(c), sl(hd)] * g[0])
                w_ref[rs(c), sl(hd)] = _dot(tmat, kb * jnp.exp(g[1])).astype(BF16)

    blk = pl.BlockSpec((rows, A_WIDTH), lambda i: (i, 0))
    half = pl.BlockSpec((rows, A_HEADS * CHUNK), lambda i: (i, 0))
    outs = pl.pallas_call(
        body, name=name, grid=(t // rows,),
        in_specs=[blk, blk, blk, pl.BlockSpec((rows, LANE), lambda i: (i, 0)),
                  pl.BlockSpec((SUBLANE, rows), lambda i: (0, i))] + c_in_specs,
        out_specs=[blk, blk, half, half] + c_out_specs,
        out_shape=[jax.ShapeDtypeStruct((t, A_WIDTH), F32), jax.ShapeDtypeStruct((t, A_WIDTH), BF16),
                   jax.ShapeDtypeStruct((t, A_HEADS * CHUNK), F32),
                   jax.ShapeDtypeStruct((t, A_HEADS * CHUNK), F32)] + c_outs,
        scratch_shapes=c_scratch,
        compiler_params=_cp("arbitrary"))(q, k, v, bg, bgt, *c_ins)
    return outs[:4], outs[4:]


def _dn_scan_fwd(q, k, u, w, qk, bg, *, name, carry=None):
    t = q.shape[0]
    rows = SCAN_ROWS
    per = rows // CHUNK
    c_ins, c_in_specs, c_out_specs, c_outs, c_scratch = _carry_specs(carry)

    def body(*refs):
        q_ref, k_ref, u_ref, w_ref, qk_ref, bg_ref, o_ref, vn_ref, s_ref, state = _carried(carry, refs, 6, 3, t // rows)

        @pl.when(pl.program_id(0) == 0)
        def _():
            state[...] = jnp.zeros_like(state)

        heads = range(A_HEADS)
        sl = lambda hd: slice(hd * LANE, (hd + 1) * LANE)
        s_cur = [state[hd] for hd in heads]
        for c in range(per):
            rs = slice(c * CHUNK, (c + 1) * CHUNK)
            bg_v = bg_ref[rs, :]
            gcols = [_chunk_gates(bg_v, None, hd)[1] for hd in heads]
            glasts = [gc[CHUNK - 1:CHUNK, :] for gc in gcols]
            for hd in heads:
                s_ref[c, hd] = s_cur[hd].astype(BF16)
            vns = [u_ref[rs, sl(hd)] - _dot(w_ref[rs, sl(hd)], s_cur[hd]) for hd in heads]
            qss = [_dot(q_ref[rs, sl(hd)] * jnp.exp(gcols[hd]), s_cur[hd]) for hd in heads]
            s_cur = [s_cur[hd] * jnp.exp(glasts[hd])
                     + _dot_tn(k_ref[rs, sl(hd)] * jnp.exp(glasts[hd] - gcols[hd]), vns[hd]) for hd in heads]
            for hd in heads:
                vn_ref[rs, sl(hd)] = vns[hd]
                o_ref[rs, sl(hd)] = qss[hd] + _dot(qk_ref[rs, hd * CHUNK:(hd + 1) * CHUNK], vns[hd])
        for hd in heads:
            state[hd] = s_cur[hd]

    blk = pl.BlockSpec((rows, A_WIDTH), lambda i: (i, 0))
    half = pl.BlockSpec((rows, A_HEADS * CHUNK), lambda i: (i, 0))
    wide = jax.ShapeDtypeStruct((t, A_WIDTH), F32)
    outs = pl.pallas_call(
        body, name=name, grid=(t // rows,),
        in_specs=[blk, blk, blk, blk, half, pl.BlockSpec((rows, LANE), lambda i: (i, 0))] + c_in_specs,
        out_specs=[blk, blk, pl.BlockSpec((per, A_HEADS, LANE, LANE), lambda i: (i, 0, 0, 0))] + c_out_specs,
        out_shape=[wide, wide, jax.ShapeDtypeStruct((t // CHUNK, A_HEADS, LANE, LANE), BF16)] + c_outs,
        scratch_shapes=[pltpu.VMEM((A_HEADS, LANE, LANE), F32)] + c_scratch,
        compiler_params=_cp("arbitrary"))(q, k, u, w, qk, bg, *c_ins)
    return outs[:3], outs[3:]


def _swa_neg_dist(n_blk):
    qi = lax.broadcasted_iota(jnp.int32, (BLOCK, 2 * BLOCK), 0)
    si = lax.broadcasted_iota(jnp.int32, (BLOCK, 2 * BLOCK), 1)
    dist = qi + BLOCK - si
    mask = (dist >= 0) & (dist < BLOCK) & ((si >= BLOCK) | (n_blk > 0))
    return jnp.where(mask, -dist.astype(F32), NEG)


def _stack_heads(ref, hk):
    return jnp.concatenate([ref[:, h * B_HEAD_DIM:(h + 1) * B_HEAD_DIM]
                            for h in range(hk * B_GROUP, (hk + 1) * B_GROUP)], axis=0)


def _swa_group_probs(q_ref, sk_ref, kband, vband, neg_dist):
    hks = range(B_KV_HEADS)
    heads = lambda hk: range(hk * B_GROUP, (hk + 1) * B_GROUP)
    ksl = lambda hk: slice(hk * B_HEAD_DIM, (hk + 1) * B_HEAD_DIM)
    ones = jnp.ones((2 * BLOCK, B_HEAD_DIM), BF16)
    qs = [_stack_heads(q_ref, hk) * (B_HEAD_DIM ** -0.5) for hk in hks]
    sink = [jnp.concatenate([jnp.broadcast_to(sk_ref[h:h + 1, 0:1], (BLOCK, 1)) for h in heads(hk)], axis=0)
            for hk in hks]
    s = [_dot_nt(qs[hk], kband[:, ksl(hk)]) + jnp.concatenate([ALIBI[h] * neg_dist for h in heads(hk)], axis=0)
         for hk in hks]
    m = [jnp.maximum(jnp.max(s[hk], axis=-1, keepdims=True), sink[hk]) for hk in hks]
    p = [jnp.exp(s[hk] - m[hk]) for hk in hks]
    oe = [jnp.dot(p[hk].astype(BF16), jnp.concatenate([vband[:, ksl(hk)].astype(BF16), ones], axis=1),
                  preferred_element_type=F32) for hk in hks]
    ps = [jnp.exp(sink[hk] - m[hk]) for hk in hks]
    inv = [1.0 / (oe[hk][:, B_HEAD_DIM:B_HEAD_DIM + 1] + ps[hk]) for hk in hks]
    return [(qs[hk], p[hk] * inv[hk], ps[hk] * inv[hk], oe[hk][:, :B_HEAD_DIM] * inv[hk]) for hk in hks]


def _swa_specs():
    qspec = lambda c0: pl.BlockSpec((BLOCK, B_WIDTH), lambda i: (i, c0 // B_WIDTH))
    cur = lambda c0: pl.BlockSpec((BLOCK, LANE), lambda i: (i, c0 // LANE))
    prev = lambda c0: pl.BlockSpec((BLOCK, LANE), lambda i: (jnp.maximum(i - 1, 0), c0 // LANE))
    return qspec, cur, prev


def _carried(carry, refs, n_in, n_out, steps):
    if carry is None:
        return refs
    ci, co = len(carry.ins), len(carry.outs)
    own = refs[:n_in] + refs[n_in + ci:n_in + ci + n_out] + refs[n_in + ci + n_out + co:len(refs) - 3]
    parts = refs[n_in:n_in + ci], refs[n_in + ci + n_out:n_in + ci + n_out + co], refs[len(refs) - 3:]

    @pl.when(pl.program_id(0) == 0)
    def _():
        carry.start(*parts)

    @pl.when(pl.program_id(0) == steps - 1)
    def _():
        carry.finish(*parts)

    return own


def _carry_specs(carry):
    if carry is None:
        return [], [], [], [], []
    return (list(carry.ins), [_ANY] * len(carry.ins), [_ANY] * len(carry.outs), list(carry.outs), carry.scratch())


def _swa_fwd(h, sinks_b, *, name, carry=None):
    t = h.shape[0]
    qspec, cur, prev = _swa_specs()
    c_ins, c_in_specs, c_out_specs, c_outs, c_scratch = _carry_specs(carry)

    def body(*refs):
        q_ref, kc_ref, kp_ref, vc_ref, vp_ref, sk_ref, o_ref = _carried(carry, refs, 6, 1, t // BLOCK)
        n_blk = pl.program_id(0)
        kband = jnp.concatenate([kp_ref[...], kc_ref[...]], axis=0)
        vband = jnp.concatenate([vp_ref[...], vc_ref[...]], axis=0)
        groups = _swa_group_probs(q_ref, sk_ref, kband, vband, _swa_neg_dist(n_blk))
        for hk, (_, _, _, o) in enumerate(groups):
            for g in range(B_GROUP):
                hq = hk * B_GROUP + g
                o_ref[:, hq * B_HEAD_DIM:(hq + 1) * B_HEAD_DIM] = o[g * BLOCK:(g + 1) * BLOCK]

    outs = pl.pallas_call(
        body, name=name, grid=(t // BLOCK,),
        in_specs=[qspec(C_QB), cur(C_KB), prev(C_KB), cur(C_VB), prev(C_VB),
                  pl.BlockSpec((B_Q_HEADS, LANE), lambda i: (0, 0))] + c_in_specs,
        out_specs=[pl.BlockSpec((BLOCK, B_WIDTH), lambda i: (i, 0))] + c_out_specs,
        out_shape=[jax.ShapeDtypeStruct((t, B_WIDTH), F32)] + c_outs,
        scratch_shapes=c_scratch,
        compiler_params=_cp("arbitrary"))(h, h, h, h, h, sinks_b, *c_ins)
    return outs[0], outs[1:]


def _rms_gate(o, za, nw):
    outs = []
    for hd in range(A_HEADS):
        oh = o[:, hd * LANE:(hd + 1) * LANE]
        r = lax.rsqrt(jnp.mean(oh * oh, -1, keepdims=True) + RMS_EPS)
        outs.append(oh * r * nw)
    return jnp.concatenate(outs, axis=1) * _silu(za)


def _out_ln(x, oa, ob, h, norm_w, w_out, ln_g, ln_b, *, tm, name, target=None):
    t = x.shape[0]
    last = target is not None

    def body(*refs):
        x_ref, oa_ref, ob_ref, za_ref, zb_ref, nw_ref, w_ref, g_ref, b_ref = refs[:9]
        xn_ref, mx_ref, r_ref = refs[9 + last:12 + last]
        ya = _rms_gate(oa_ref[...], za_ref[...], nw_ref[...])
        yb = ob_ref[...] * _silu(zb_ref[...])
        mixed = jnp.concatenate([ya, yb], axis=1).astype(BF16)
        mx_ref[...] = mixed
        r = DEEPNORM_ALPHA * x_ref[...] + jnp.dot(mixed, w_ref[...], preferred_element_type=F32)
        r_ref[...] = r
        mu = jnp.mean(r, -1, keepdims=True)
        xc = r - mu
        var = jnp.mean(xc * xc, -1, keepdims=True)
        xn = xc * lax.rsqrt(var + LN_EPS) * g_ref[...] + b_ref[...]
        if not last:
            xn_ref[...] = xn
            return
        loss_ref = refs[13]

        @pl.when(pl.program_id(0) == 0)
        def _():
            loss_ref[...] = jnp.zeros_like(loss_ref)

        err = xn - refs[9][...]
        xn_ref[...] = err * (1.0 / D_MODEL)
        loss_ref[...] += 0.5 / D_MODEL * jnp.sum(err * err)

    row = lambda w, c: pl.BlockSpec((tm, w), lambda i: (i, c))
    full = lambda a, b: pl.BlockSpec((a, b), lambda i: (0, 0))
    wide = jax.ShapeDtypeStruct((t, D_MODEL), F32)
    return pl.pallas_call(
        body, name=name, grid=(t // tm,),
        in_specs=[row(D_MODEL, 0), row(A_WIDTH, 0), row(B_WIDTH, 0), row(A_WIDTH, C_ZA // A_WIDTH),
                  row(B_WIDTH, C_ZB // B_WIDTH), full(1, LANE), full(D_MODEL, D_MODEL), full(1, D_MODEL),
                  full(1, D_MODEL)] + [row(D_MODEL, 0)] * last,
        out_specs=[row(D_MODEL, 0), row(D_MODEL, 0), row(D_MODEL, 0)] + [full(SUBLANE, LANE)] * last,
        out_shape=[wide, jax.ShapeDtypeStruct((t, D_MODEL), BF16), wide]
        + [jax.ShapeDtypeStruct((SUBLANE, LANE), F32)] * last,
        compiler_params=_cp("arbitrary" if last else "parallel"))(
        x, oa, ob, h, h, norm_w, w_out, ln_g, ln_b, *([target] if last else []))


def _layer_fwd(x, wt, conv_w, par, sinks_b, norm_w, w_out_bf, ln_g, ln_b, l, carries=None, target=None):
    carries = carries or {}
    h, got_in = _matmul_nt(x, wt, tm=512, name=f"in_proj_{l}", carry=carries.get("in_proj"))
    if callable(w_out_bf):
        w_out_bf = w_out_bf(got_in)
    (q, k, v, bg, bgt), got_pre = _dn_pre(h, conv_w, par, tt=512, name=f"dn_pre_{l}", carry=carries.get("dn_pre"))
    (u, w, tmat, qk), got_wy = _dn_wy(q, k, v, bg, bgt, name=f"dn_wy_{l}", carry=carries.get("dn_wy"))
    (oa, vn, s_all), got_scan = _dn_scan_fwd(q, k, u, w, qk, bg, name=f"dn_scan_{l}", carry=carries.get("dn_scan"))
    ob, got_swa = _swa_fwd(h, sinks_b, name=f"swa_fwd_{l}", carry=carries.get("swa"))
    xn, mixed, r, *loss = _out_ln(x, oa, ob, h, norm_w, w_out_bf, ln_g, ln_b, tm=256, name=f"out_ln_{l}", target=target)
    if loss:
        xn = (xn, loss[0])
    res = dict(x=x, h=h, q=q, k=k, v=v, bg=bg, bgt=bgt, w=w, tmat=tmat, qk=qk, vn=vn, oa=oa, s_all=s_all,
               mixed=mixed, r=r, w_out=w_out_bf)
    return xn, res, dict(in_proj=got_in, dn_pre=got_pre, dn_wy=got_wy, dn_scan=got_scan, swa=got_swa)


def _ln_out_bwd(dxn, r, mixed, ln_g, w_out, *, tm, name):
    t = dxn.shape[0]

    def body(dxn_ref, r_ref, mx_ref, g_ref, w_ref, dr_ref, dm_ref, dw_ref, dg_ref, db_ref):
        @pl.when(pl.program_id(0) == 0)
        def _():
            dw_ref[...] = jnp.zeros_like(dw_ref)
            dg_ref[...] = jnp.zeros_like(dg_ref)
            db_ref[...] = jnp.zeros_like(db_ref)

        rr = r_ref[...]
        xc = rr - jnp.mean(rr, -1, keepdims=True)
        rstd = lax.rsqrt(jnp.mean(xc * xc, -1, keepdims=True) + LN_EPS)
        xhat = xc * rstd
        dxn_v = dxn_ref[...]
        dxh = dxn_v * g_ref[...]
        dr = rstd * (dxh - jnp.mean(dxh, -1, keepdims=True) - xhat * jnp.mean(dxh * xhat, -1, keepdims=True))
        dr_ref[...] = dr
        dg_ref[...] += jnp.sum(dxn_v * xhat, axis=0, keepdims=True)
        db_ref[...] += jnp.sum(dxn_v, axis=0, keepdims=True)
        drb = dr.astype(BF16)
        dm_ref[...] = _dot_nt(drb, w_ref[...])
        dw_ref[...] += _dot_tn(mx_ref[...], drb)

    row = pl.BlockSpec((tm, D_MODEL), lambda i: (i, 0))
    full = lambda a, b: pl.BlockSpec((a, b), lambda i: (0, 0))
    big = jax.ShapeDtypeStruct((t, D_MODEL), F32)
    vec = jax.ShapeDtypeStruct((1, D_MODEL), F32)
    return pl.pallas_call(
        body, name=name, grid=(t // tm,),
        in_specs=[row, row, row, full(1, D_MODEL), full(D_MODEL, D_MODEL)],
        out_specs=[row, row, full(D_MODEL, D_MODEL), full(1, D_MODEL), full(1, D_MODEL)],
        out_shape=[big, big, jax.ShapeDtypeStruct((D_MODEL, D_MODEL), F32), vec, vec],
        compiler_params=_cp("arbitrary"))(dxn, r, mixed, ln_g, w_out)


def _dn_post_bwd(dm, oa, h, norm_w, *, tm, name):
    t = oa.shape[0]

    def body(dy_ref, o_ref, za_ref, nw_ref, do_ref, dza_ref, dnw_ref):
        @pl.when(pl.program_id(0) == 0)
        def _():
            dnw_ref[...] = jnp.zeros_like(dnw_ref)

        nw = nw_ref[...]
        dnw = jnp.zeros_like(nw)
        for hd in range(A_HEADS):
            sl = slice(hd * LANE, (hd + 1) * LANE)
            oh, za, dy = o_ref[:, sl], za_ref[:, sl], dy_ref[:, sl]
            rs = lax.rsqrt(jnp.mean(oh * oh, -1, keepdims=True) + RMS_EPS)
            nrm = oh * rs
            dza_ref[:, sl] = dy * nrm * nw * _dsilu(za)
            dn = dy * _silu(za)
            dnw = dnw + jnp.sum(dn * nrm, axis=0, keepdims=True)
            dnn = dn * nw
            do_ref[:, sl] = rs * dnn - oh * (rs * rs * rs) * jnp.mean(dnn * oh, -1, keepdims=True)
        dnw_ref[...] += dnw

    row = lambda c: pl.BlockSpec((tm, A_WIDTH), lambda i: (i, c))
    wide = jax.ShapeDtypeStruct((t, A_WIDTH), F32)
    return pl.pallas_call(
        body, name=name, grid=(t // tm,),
        in_specs=[row(0), row(0), row(C_ZA // A_WIDTH), pl.BlockSpec((1, LANE), lambda i: (0, 0))],
        out_specs=[row(0), row(C_ZA // A_WIDTH), pl.BlockSpec((1, LANE), lambda i: (0, 0))],
        out_shape=[wide, jax.ShapeDtypeStruct((t, DH_MAIN), F32), jax.ShapeDtypeStruct((1, LANE), F32)],
        compiler_params=_cp("arbitrary"))(dm, oa, h, norm_w)


def _dn_scan_bwd(q, k, w, qk, bg, do, *, name):
    t = q.shape[0]
    rows = SCAN_ROWS
    per = rows // CHUNK
    n = t // rows

    def body(q_ref, k_ref, w_ref, qk_ref, bg_ref, do_ref, dvn_ref, ds_ref, dstate):
        @pl.when(pl.program_id(0) == 0)
        def _():
            dstate[...] = jnp.zeros_like(dstate)

        heads = range(A_HEADS)
        sl = lambda hd: slice(hd * LANE, (hd + 1) * LANE)
        ds_cur = [dstate[hd] for hd in heads]
        for c in reversed(range(per)):
            rs = slice(c * CHUNK, (c + 1) * CHUNK)
            bg_v = bg_ref[rs, :]
            gcols = [_chunk_gates(bg_v, None, hd)[1] for hd in heads]
            glasts = [gc[CHUNK - 1:CHUNK, :] for gc in gcols]
            for hd in heads:
                ds_ref[c, hd] = ds_cur[hd].astype(BF16)
            pdo = [_dot_tn(qk_ref[rs, hd * CHUNK:(hd + 1) * CHUNK], do_ref[rs, sl(hd)]) for hd in heads]
            qdo = [_dot_tn(q_ref[rs, sl(hd)] * jnp.exp(gcols[hd]), do_ref[rs, sl(hd)]) for hd in heads]
            dvns = [pdo[hd] + _dot(k_ref[rs, sl(hd)] * jnp.exp(glasts[hd] - gcols[hd]), ds_cur[hd]) for hd in heads]
            ds_cur = [qdo[hd] + jnp.exp(glasts[hd]) * ds_cur[hd] - _dot_tn(w_ref[rs, sl(hd)], dvns[hd])
                      for hd in heads]
            for hd in heads:
                dvn_ref[rs, sl(hd)] = dvns[hd]
        for hd in heads:
            dstate[hd] = ds_cur[hd]

    blk = pl.BlockSpec((rows, A_WIDTH), lambda i: (n - 1 - i, 0))
    return pl.pallas_call(
        body, name=name, grid=(n,),
        in_specs=[blk, blk, blk, pl.BlockSpec((rows, A_HEADS * CHUNK), lambda i: (n - 1 - i, 0)),
                  pl.BlockSpec((rows, LANE), lambda i: (n - 1 - i, 0)), blk],
        out_specs=[blk, pl.BlockSpec((per, A_HEADS, LANE, LANE), lambda i: (n - 1 - i, 0, 0, 0))],
        out_shape=[jax.ShapeDtypeStruct((t, A_WIDTH), F32),
                   jax.ShapeDtypeStruct((t // CHUNK, A_HEADS, LANE, LANE), BF16)],
        scratch_shapes=[pltpu.VMEM((A_HEADS, LANE, LANE), F32)],
        compiler_params=_cp("arbitrary"))(q, k, w, qk, bg, do)


def _dn_chunk_bwd(q, k, v, vn, tmat, qk, bg, bgt, s_all, ds_all, dvn, do, *, name):
    t = q.shape[0]
    rows = WY_ROWS
    per = rows // CHUNK

    def body(q_ref, k_ref, v_ref, vn_ref, tm_ref, qk_ref, bg_ref, bgt_ref, s_ref, ds_ref, dvn_ref, do_ref,
             dq_ref, dk_ref, dv_ref, dbg_ref, dbgt_ref):
        causal, strict, _ = _chunk_masks()
        lane = lax.broadcasted_iota(jnp.int32, (CHUNK, LANE), 1)
        rowi = lax.broadcasted_iota(jnp.int32, (CHUNK, 1), 0)
        sub = lax.broadcasted_iota(jnp.int32, (SUBLANE, CHUNK), 0)
        rs = lambda c: slice(c * CHUNK, (c + 1) * CHUNK)
        sl = lambda hd: slice(hd * LANE, (hd + 1) * LANE)
        hs = lambda hd: slice(hd * CHUNK, (hd + 1) * CHUNK)
        for c0 in range(0, per, WY_GROUP):
            items = [(c, hd) for c in range(c0, c0 + WY_GROUP) for hd in range(A_HEADS)]
            at = lambda ref: [ref[rs(c), sl(hd)] for c, hd in items]
            qs, ks, vs, dos, vns, dvns = at(q_ref), at(k_ref), at(v_ref), at(do_ref), at(vn_ref), at(dvn_ref)
            tmhs = [tm_ref[rs(c), hs(hd)] for c, hd in items]
            ps = [qk_ref[rs(c), hs(hd)] for c, hd in items]
            gates = [_chunk_gates(bg_ref[rs(c), :], bgt_ref[:, rs(c)], hd) for c, hd in items]
            betas = [g[0] for g in gates]
            gcols = [g[1] for g in gates]
            dmats = [jnp.exp(jnp.where(causal, g[1] - g[2], NEG)) for g in gates]
            es = [jnp.exp(gc) for gc in gcols]
            glasts = [gc[CHUNK - 1:CHUNK, :] for gc in gcols]
            eks = [jnp.exp(gl - gc) for gl, gc in zip(glasts, gcols)]
            kbs = [kh * b for kh, b in zip(ks, betas)]
            vbs = [vh * b for vh, b in zip(vs, betas)]
            kbes = [kb * e for kb, e in zip(kbs, es)]

            a_s = [jnp.where(strict, _dot_nt(kb, kh) * dm, 0.0) for kb, kh, dm in zip(kbs, ks, dmats)]
            dps = [jnp.where(causal, _dot_nt(doh, vnh), 0.0) for doh, vnh in zip(dos, vns)]
            dqds = [_dot_nt(doh, s_ref[c, hd]) for doh, (c, hd) in zip(dos, items)]
            dkds = [_dot_nt(vnh, ds_ref[c, hd]) for vnh, (c, hd) in zip(vns, items)]
            dws = [-_dot_nt(dvnh, s_ref[c, hd]) for dvnh, (c, hd) in zip(dvns, items)]
            dvbs = [_dot_tn(tmh, dvnh) for tmh, dvnh in zip(tmhs, dvns)]
            dgts = [jnp.sum(s_ref[c, hd].astype(F32) * ds_ref[c, hd].astype(F32), keepdims=True) for c, hd in items]
            dts = [_dot_nt(dvnh, vb) + _dot_nt(dw, kbe) for dvnh, vb, dw, kbe in zip(dvns, vbs, dws, kbes)]
            dkbes = [_dot_tn(tmh, dw) for tmh, dw in zip(tmhs, dws)]
            xs = [_dot_nt(dt, tmh) for dt, tmh in zip(dts, tmhs)]
            das = [jnp.where(strict, -_dot_tn(tmh, x), 0.0) for tmh, x in zip(tmhs, xs)]
            dmas = [da * dm for da, dm in zip(das, dmats)]
            dmps = [dp * dm for dp, dm in zip(dps, dmats)]
            dkbs = [_dot(dma, kh) + dkbe * e for dma, kh, dkbe, e in zip(dmas, ks, dkbes, es)]
            for i, (c, hd) in enumerate(items):
                dq_ref[rs(c), sl(hd)] = _dot(dmps[i], ks[i]) + dqds[i] * es[i]
                dk_ref[rs(c), sl(hd)] = (_dot_tn(dmas[i], kbs[i]) + _dot_tn(dmps[i], qs[i]) + dkds[i] * eks[i]
                                         + dkbs[i] * betas[i])
                dv_ref[rs(c), sl(hd)] = dvbs[i] * betas[i]
            for c in range(c0, c0 + WY_GROUP):
                acc = jnp.zeros((CHUNK, LANE), F32)
                acc_t = jnp.zeros((SUBLANE, CHUNK), F32)
                for i, (ci, hd) in enumerate(items):
                    if ci != c:
                        continue
                    gmat = das[i] * a_s[i] + dps[i] * ps[i]
                    rk = jnp.sum(dkds[i] * ks[i], -1, keepdims=True) * eks[i]
                    de = (jnp.sum(dqds[i] * qs[i], -1, keepdims=True)
                          + jnp.sum(dkbes[i] * kbs[i], -1, keepdims=True))
                    dglast = jnp.sum(rk, keepdims=True) + dgts[i] * jnp.exp(glasts[i])
                    dgc = (jnp.sum(gmat, -1, keepdims=True) + de * es[i] - rk
                           + jnp.where(rowi == CHUNK - 1, dglast, 0.0))
                    dbeta = (jnp.sum(dkbs[i] * ks[i], -1, keepdims=True)
                             + jnp.sum(dvbs[i] * vs[i], -1, keepdims=True))
                    acc = acc + jnp.where(lane == hd, dbeta, 0.0) + jnp.where(lane == A_HEADS + hd, dgc, 0.0)
                    acc_t = acc_t + jnp.where(sub == A_HEADS + hd, -jnp.sum(gmat, axis=0, keepdims=True), 0.0)
                dbg_ref[rs(c), :] = acc
                dbgt_ref[:, rs(c)] = acc_t

    blk = pl.BlockSpec((rows, A_WIDTH), lambda i: (i, 0))
    half = pl.BlockSpec((rows, A_HEADS * CHUNK), lambda i: (i, 0))
    col = pl.BlockSpec((rows, LANE), lambda i: (i, 0))
    rowf = pl.BlockSpec((SUBLANE, rows), lambda i: (0, i))
    st = pl.BlockSpec((per, A_HEADS, LANE, LANE), lambda i: (i, 0, 0, 0))
    wide = jax.ShapeDtypeStruct((t, A_WIDTH), F32)
    return pl.pallas_call(
        body, name=name, grid=(t // rows,),
        in_specs=[blk, blk, blk, blk, half, half, col, rowf, st, st, blk, blk],
        out_specs=[blk, blk, blk, col, rowf],
        out_shape=[wide, wide, wide, jax.ShapeDtypeStruct((t, LANE), F32), jax.ShapeDtypeStruct((SUBLANE, t), F32)],
        compiler_params=_cp("parallel"))(q, k, v, vn, tmat, qk, bg, bgt, s_all, ds_all, dvn, do)


def _dn_pre_bwd(h, conv_w, par, dq, dk, dv, dbg, dbgt, *, tt, name):
    t = h.shape[0]
    cw = 3 * A_WIDTH
    hb = tt // SUBLANE

    def body(pre_ref, halo_ref, bgi_ref, cw_ref, par_ref, dq_ref, dk_ref, dv_ref, dbg_ref, dbgt_ref,
             dc_ref, dbgi_ref, dpar_ref):
        i = pl.program_id(0)

        @pl.when(i == 0)
        def _():
            dpar_ref[...] = jnp.zeros_like(dpar_ref)

        cur = pre_ref[...]
        before = jnp.where(i > 0, halo_ref[...], 0.0)
        c = _conv_fwd(cur, before, cw_ref[...])
        s = _silu(c)
        ds = _dsilu(c)
        for hd in range(A_HEADS):
            sl = slice(hd * LANE, (hd + 1) * LANE)
            for base, d_ref, scale in ((0, dq_ref, A_HEAD_DIM ** -0.5), (A_WIDTH, dk_ref, 1.0)):
                csl = slice(base + hd * LANE, base + (hd + 1) * LANE)
                tq = s[:, base + hd * LANE:base + (hd + 1) * LANE]
                dy = d_ref[:, sl]
                rq = lax.rsqrt(jnp.sum(tq * tq, -1, keepdims=True) + L2_EPS)
                dtq = scale * (rq * dy - tq * (rq * rq * rq) * jnp.sum(dy * tq, -1, keepdims=True))
                dc_ref[:, csl] = dtq * ds[:, base + hd * LANE:base + (hd + 1) * LANE]
        dc_ref[:, 2 * A_WIDTH:] = dv_ref[...] * ds[:, 2 * A_WIDTH:]
        raw = bgi_ref[...]
        lane = lax.broadcasted_iota(jnp.int32, raw.shape, 1)
        is_b = lane < A_HEADS
        is_a = (lane >= A_HEADS) & (lane < 2 * A_HEADS)
        rows_t = jnp.concatenate([dbgt_ref[...], jnp.zeros((LANE - SUBLANE, tt), F32)], axis=0)
        dbg_v = dbg_ref[...] + jnp.where(is_a, jnp.transpose(rows_t), 0.0)
        dbg_v = jnp.where(is_a, _dot_hi(_chunk_tri(tt, lower=False), jnp.where(is_a, dbg_v, 0.0)), dbg_v)
        beta = _sigmoid(raw)
        z = raw + par_ref[1:2, :]
        neg_ea = -jnp.exp(par_ref[0:1, :])
        g = neg_ea * _softplus(z)
        da = dbg_v * neg_ea * _sigmoid(z)
        dbgi_ref[...] = jnp.where(is_b, dbg_v * beta * (1.0 - beta), jnp.where(is_a, da, 0.0))
        dpar_ref[0:1, :] += jnp.sum(jnp.where(is_a, dbg_v * g, 0.0), axis=0, keepdims=True)
        dpar_ref[1:2, :] += jnp.sum(jnp.where(is_a, da, 0.0), axis=0, keepdims=True)

    wide = pl.BlockSpec((tt, A_WIDTH), lambda i: (i, 0))
    return pl.pallas_call(
        body, name=name, grid=(t // tt,),
        in_specs=[pl.BlockSpec((tt, cw), lambda i: (i, 0)),
                  pl.BlockSpec((SUBLANE, cw), lambda i: (jnp.maximum(i * hb - 1, 0), 0)),
                  pl.BlockSpec((tt, LANE), lambda i: (i, C_BG // LANE)),
                  pl.BlockSpec((CONV_K, cw), lambda i: (0, 0)),
                  pl.BlockSpec((SUBLANE, LANE), lambda i: (0, 0)),
                  wide, wide, wide, pl.BlockSpec((tt, LANE), lambda i: (i, 0)),
                  pl.BlockSpec((SUBLANE, tt), lambda i: (0, i))],
        out_specs=[pl.BlockSpec((tt, cw), lambda i: (i, 0)), pl.BlockSpec((tt, LANE), lambda i: (i, 0)),
                   pl.BlockSpec((SUBLANE, LANE), lambda i: (0, 0))],
        out_shape=[jax.ShapeDtypeStruct((t, cw), F32), jax.ShapeDtypeStruct((t, LANE), F32),
                   jax.ShapeDtypeStruct((SUBLANE, LANE), F32)],
        compiler_params=_cp("arbitrary"))(h, h, h, conv_w, par, dq, dk, dv, dbg, dbgt)


def _conv_bwd(dc, h, conv_w, dh, *, tt, name):
    t = dc.shape[0]
    cw = 3 * A_WIDTH
    hb = tt // SUBLANE
    nb = t // tt

    def body(dc_ref, after_ref, pre_ref, before_ref, cw_ref, dh_in_ref, dpre_ref, dcw_ref):
        i = pl.program_id(0)

        @pl.when(i == 0)
        def _():
            dcw_ref[...] = jnp.zeros_like(dcw_ref)

        dcv = dc_ref[...]
        after = jnp.where(i < nb - 1, after_ref[...], 0.0)
        cur = pre_ref[...]
        before = jnp.where(i > 0, before_ref[...], 0.0)
        w = cw_ref[...]
        acc = dcv * w[CONV_K - 1:CONV_K, :]
        dcw_ref[CONV_K - 1:CONV_K, :] += jnp.sum(dcv * cur, axis=0, keepdims=True)
        for s in range(1, CONV_K):
            j = CONV_K - 1 - s
            acc = acc + _shift_up(dcv, after, s) * w[j:j + 1, :]
            dcw_ref[j:j + 1, :] += jnp.sum(dcv * _shift_down(cur, before, s), axis=0, keepdims=True)
        dpre_ref[...] = acc

    return pl.pallas_call(
        body, name=name, grid=(nb,),
        in_specs=[pl.BlockSpec((tt, cw), lambda i: (i, 0)),
                  pl.BlockSpec((SUBLANE, cw), lambda i: (jnp.minimum((i + 1) * hb, t // SUBLANE - 1), 0)),
                  pl.BlockSpec((tt, cw), lambda i: (i, 0)),
                  pl.BlockSpec((SUBLANE, cw), lambda i: (jnp.maximum(i * hb - 1, 0), 0)),
                  pl.BlockSpec((CONV_K, cw), lambda i: (0, 0)), _ANY],
        out_specs=[pl.BlockSpec((tt, cw), lambda i: (i, 0)), pl.BlockSpec((SUBLANE, cw), lambda i: (0, 0))],
        out_shape=[jax.ShapeDtypeStruct(dh.shape, F32), jax.ShapeDtypeStruct((SUBLANE, cw), F32)],
        input_output_aliases={5: 0},
        compiler_params=_cp("arbitrary"))(dc, dc, h, h, conv_w, dh)


def _swa_bwd(h, dm, sinks_b, dh, *, name, carry=None):
    t = h.shape[0]
    qspec, cur, prev = _swa_specs()
    c_ins, c_in_specs, c_out_specs, c_outs, c_scratch = _carry_specs(carry)

    def body(*refs):
        (q_ref, kc_ref, kp_ref, vc_ref, vp_ref, zb_ref, dy_ref, sk_ref, dh_in_ref,
         dqz_ref, dk_ref, dv_ref, dsk_ref) = _carried(carry, refs, 9, 4, t // BLOCK)
        n_blk = pl.program_id(0)

        @pl.when(n_blk == 0)
        def _():
            dk_ref[...] = jnp.zeros_like(dk_ref)
            dv_ref[...] = jnp.zeros_like(dv_ref)
            dsk_ref[...] = jnp.zeros_like(dsk_ref)

        kband = jnp.concatenate([kp_ref[...], kc_ref[...]], axis=0)
        vband = jnp.concatenate([vp_ref[...], vc_ref[...]], axis=0)
        scale = B_HEAD_DIM ** -0.5
        hks = range(B_KV_HEADS)
        ksl = lambda hk: slice(hk * B_HEAD_DIM, (hk + 1) * B_HEAD_DIM)
        groups = _swa_group_probs(q_ref, sk_ref, kband, vband, _swa_neg_dist(n_blk))
        zbs = [_stack_heads(zb_ref, hk) for hk in hks]
        dys = [_stack_heads(dy_ref, hk) for hk in hks]
        dos = [dys[hk] * _silu(zbs[hk]) for hk in hks]
        deltas = [jnp.sum(dos[hk] * groups[hk][3], -1, keepdims=True) for hk in hks]
        dss = [groups[hk][1] * (_dot_nt(dos[hk], vband[:, ksl(hk)]) - deltas[hk]) for hk in hks]
        dqs = [_dot(dss[hk], kband[:, ksl(hk)]) * scale for hk in hks]
        dk_acc = [_dot_tn(dss[hk], groups[hk][0]) for hk in hks]
        dv_acc = [_dot_tn(groups[hk][1], dos[hk]) for hk in hks]
        for hk in hks:
            dzb = dys[hk] * groups[hk][3] * _dsilu(zbs[hk])
            dsink = groups[hk][2] * deltas[hk]
            for g in range(B_GROUP):
                hq = hk * B_GROUP + g
                rows = slice(g * BLOCK, (g + 1) * BLOCK)
                qsl = slice(hq * B_HEAD_DIM, (hq + 1) * B_HEAD_DIM)
                dqz_ref[:, qsl] = dqs[hk][rows]
                dqz_ref[:, B_WIDTH + hq * B_HEAD_DIM:B_WIDTH + (hq + 1) * B_HEAD_DIM] = dzb[rows]
                dsk_ref[hq:hq + 1, :] += -jnp.sum(dsink[rows], keepdims=True)
        dkb = jnp.concatenate(dk_acc, axis=1)
        dvb = jnp.concatenate(dv_acc, axis=1)
        at_cur = pl.ds(pl.multiple_of(n_blk * BLOCK, BLOCK), BLOCK)
        at_prev = pl.ds(pl.multiple_of(jnp.maximum(n_blk - 1, 0) * BLOCK, BLOCK), BLOCK)
        dk_ref[at_prev, :] += dkb[:BLOCK]
        dv_ref[at_prev, :] += dvb[:BLOCK]
        dk_ref[at_cur, :] += dkb[BLOCK:]
        dv_ref[at_cur, :] += dvb[BLOCK:]

    narrow = jax.ShapeDtypeStruct((t, B_KV_WIDTH), F32)
    res = lambda a, b: pl.BlockSpec((a, b), lambda i: (0, 0))
    outs = pl.pallas_call(
        body, name=name, grid=(t // BLOCK,),
        in_specs=[qspec(C_QB), cur(C_KB), prev(C_KB), cur(C_VB), prev(C_VB), qspec(C_ZB),
                  pl.BlockSpec((BLOCK, B_WIDTH), lambda i: (i, 1)), res(B_Q_HEADS, LANE), _ANY] + c_in_specs,
        out_specs=[pl.BlockSpec((BLOCK, 2 * B_WIDTH), lambda i: (i, C_QB // (2 * B_WIDTH))),
                   res(t, B_KV_WIDTH), res(t, B_KV_WIDTH), res(B_Q_HEADS, LANE)] + c_out_specs,
        out_shape=[jax.ShapeDtypeStruct(dh.shape, F32), narrow, narrow,
                   jax.ShapeDtypeStruct((B_Q_HEADS, LANE), F32)] + c_outs,
        scratch_shapes=c_scratch,
        input_output_aliases={8: 0},
        compiler_params=_cp("arbitrary"))(h, h, h, h, h, h, dm, sinks_b, dh, *c_ins)
    return outs[:4], outs[4:]


def _matmul_tn(a, b, *, tk, tm, name):
    t, m = a.shape
    n = b.shape[1]

    def body(a_ref, b_ref, o_ref):
        @pl.when(pl.program_id(1) == 0)
        def _():
            o_ref[...] = jnp.zeros_like(o_ref)

        o_ref[...] += _dot_tn(a_ref[...], b_ref[...])

    return pl.pallas_call(
        body, name=name, grid=(m // tm, t // tk),
        in_specs=[pl.BlockSpec((tk, tm), lambda j, kk: (kk, j)), pl.BlockSpec((tk, n), lambda j, kk: (kk, 0))],
        out_specs=pl.BlockSpec((tm, n), lambda j, kk: (j, 0)),
        out_shape=jax.ShapeDtypeStruct((m, n), F32),
        compiler_params=_cp("parallel", "arbitrary"))(a, b)


def _in_proj_dx(dh_main, dh_tail, wt, dr, *, tm, name, carry=None):
    t, n_main = dh_main.shape
    n_tail = dh_tail.shape[1]
    c_ins, c_in_specs, c_out_specs, c_outs, c_scratch = _carry_specs(carry)

    def body(*refs):
        a_ref, t_ref, wa_ref, wt_ref, r_ref, o_ref = _carried(carry, refs, 5, 1, t // tm)
        o_ref[...] = _dot(a_ref[...], wa_ref[...]) + _dot(t_ref[...], wt_ref[...]) + DEEPNORM_ALPHA * r_ref[...]

    row = lambda w: pl.BlockSpec((tm, w), lambda i: (i, 0))
    outs = pl.pallas_call(
        body, name=name, grid=(t // tm,),
        in_specs=[row(n_main), row(n_tail), pl.BlockSpec((n_main, D_MODEL), lambda i: (0, 0)),
                  pl.BlockSpec((n_tail, D_MODEL), lambda i: (n_main // n_tail, 0)), row(D_MODEL)] + c_in_specs,
        out_specs=[row(D_MODEL)] + c_out_specs,
        out_shape=[jax.ShapeDtypeStruct((t, D_MODEL), F32)] + c_outs,
        scratch_shapes=c_scratch,
        compiler_params=_cp("arbitrary"))(dh_main, dh_tail, wt, wt, dr, *c_ins)
    return outs[0], outs[1:]


def _layer_bwd(dxn, res, wt, conv_w, par, sinks_b, norm_w, w_out_bf, ln_g, l, carry=None, carry_dx=None):
    w_out_bf = res["w_out"]
    dr, dm, dw_out, dln_g, dln_b = _ln_out_bwd(dxn, res["r"], res["mixed"], ln_g, w_out_bf, tm=256, name=f"ln_out_bwd_{l}")
    h = res["h"]
    do, dh, dnw = _dn_post_bwd(dm, res["oa"], h, norm_w, tm=512, name=f"dn_post_bwd_{l}")
    dvn, ds_all = _dn_scan_bwd(res["q"], res["k"], res["w"], res["qk"], res["bg"], do, name=f"dn_scan_bwd_{l}")
    dq, dk, dv, dbg, dbgt = _dn_chunk_bwd(res["q"], res["k"], res["v"], res["vn"], res["tmat"], res["qk"], res["bg"],
                                          res["bgt"], res["s_all"], ds_all, dvn, do, name=f"dn_chunk_bwd_{l}")
    dc, dbgi, dpar = _dn_pre_bwd(h, conv_w, par, dq, dk, dv, dbg, dbgt, tt=512, name=f"dn_pre_bwd_{l}")
    dh, dcw = _conv_bwd(dc, h, conv_w, dh, tt=512, name=f"conv_bwd_{l}")
    (dh, dkb, dvb, dsk), carried = _swa_bwd(h, dm, sinks_b, dh, name=f"swa_bwd_{l}", carry=carry)
    dh_tail = jnp.concatenate([dkb, dvb, dbgi], axis=1)
    dwt_main = _matmul_tn(dh, res["x"], tk=512, tm=768, name=f"in_proj_dw_{l}")
    dwt_tail = _matmul_tn(dh_tail, res["x"], tk=512, tm=P_COLS - DH_MAIN, name=f"in_proj_dw_tail_{l}")
    grads = dict(w_in=(dwt_main, dwt_tail), conv_w=dcw[:CONV_K], a_log=dpar[0, A_HEADS:2 * A_HEADS],
                 dt_bias=dpar[1, A_HEADS:2 * A_HEADS], norm_w=dnw[0], sinks=dsk[:, 0], w_out=dw_out,
                 ln_g=dln_g[0], ln_b=dln_b[0])
    dx, carried_dx = _in_proj_dx(dh, dh_tail, wt, dr, tm=256, name=f"in_proj_dx_{l}",
                                 carry=None if carry_dx is None else carry_dx(grads))
    return dx, grads, carried, carried_dx


def _layer_args(wt, conv_w, a_log, dt_bias, sinks, norm_w, w_out_bf):
    return (wt, conv_w, _gate_params(a_log, dt_bias), jnp.broadcast_to(sinks[:, None], (B_Q_HEADS, LANE)),
            norm_w[None], w_out_bf)


def _local_step(x, target, args0, args1, ln_g, ln_b, gathers=None, reduce1=None, reduce0=None):
    assert DEPTH == 2
    x1, res0, got = _layer_fwd(x, *args0, ln_g[0][None], ln_b[0][None], 0, carries=gathers)
    if gathers is not None:
        args1 = args1(got)
    (dx, loss_tile), res1, _ = _layer_fwd(x1, *args1, ln_g[1][None], ln_b[1][None], 1, target=target)
    dx, grads1, _, _ = _layer_bwd(dx, res1, *args1, ln_g[1][None], 1)
    carry = None if reduce1 is None else reduce1(grads1)
    carry_dx = None if reduce0 is None else (lambda grads0: reduce0(grads0, grads1, loss_tile))
    dx, grads0, landed1, landed0 = _layer_bwd(dx, res0, *args0, ln_g[0][None], 0, carry=carry, carry_dx=carry_dx)
    return loss_tile, dx, [grads0, grads1], landed1, landed0


_ANY = pl.BlockSpec(memory_space=pl.ANY)
_MESH = pl.DeviceIdType.MESH


HALF = D_MODEL // 2


class _Exchange:
    def __init__(self, ins, outs, n_remote, n_local, plan):
        self.ins, self.outs, self.n_remote, self.n_local, self.plan = tuple(ins), tuple(outs), n_remote, n_local, plan

    def scratch(self):
        return [pltpu.SemaphoreType.DMA((self.n_remote,)), pltpu.SemaphoreType.DMA((self.n_remote,)),
                pltpu.SemaphoreType.DMA((max(self.n_local, 1),))]

    def _copies(self, in_refs, out_refs, sems, arriving):
        send_sems, recv_sems, local_sems = sems
        local, sends, recvs = self.plan(in_refs, out_refs)
        loc = [pltpu.make_async_copy(s, d, local_sems.at[i]) for i, (s, d) in enumerate(local)]
        rem = [pltpu.make_async_remote_copy(src_ref=s, dst_ref=recvs[i] if arriving else d, send_sem=send_sems.at[i],
                                            recv_sem=recv_sems.at[i], device_id=peer, device_id_type=_MESH)
               for i, (s, d, peer) in enumerate(sends)]
        return loc, rem

    def start(self, in_refs, out_refs, sems):
        loc, rem = self._copies(in_refs, out_refs, sems, arriving=False)
        for cp in loc + rem:
            cp.start()

    def finish(self, in_refs, out_refs, sems):
        loc, rem = self._copies(in_refs, out_refs, sems, arriving=True)
        for cp in rem:
            cp.wait_recv()
        for cp in rem:
            cp.wait_send()
        for cp in loc:
            cp.wait()


def _run_exchange(ex, *, name):
    n_in, n_out = len(ex.ins), len(ex.outs)

    def body(*refs):
        parts = refs[:n_in], refs[n_in:n_in + n_out], refs[n_in + n_out:]
        ex.start(*parts)
        ex.finish(*parts)

    return pl.pallas_call(body, name=name, in_specs=[_ANY] * n_in, out_specs=[_ANY] * n_out, out_shape=list(ex.outs),
                          scratch_shapes=ex.scratch())(*ex.ins)


def _place():
    x, y, c = lax.axis_index("x"), lax.axis_index("y"), lax.axis_index("c")
    return x, y, c, [(1 - x, y), (x, 1 - y), (1 - x, 1 - y)]


def _gather_exchange(arrays):
    n = len(arrays)

    def plan(src, dst):
        x, y, c, chips = _place()
        me = 2 * x + y
        local = [(src[k], dst[k].at[me]) for k in range(n)]
        sends = [(src[k], dst[k].at[me], (px, py, c)) for k in range(n) for px, py in chips]
        recvs = [dst[k].at[2 * px + py] for k in range(n) for px, py in chips]
        return local, sends, recvs

    return _Exchange(arrays, [jax.ShapeDtypeStruct((N_SHARD,) + a.shape, a.dtype) for a in arrays], 3 * n, n, plan)


def _gather_two_level(pack, conv_w, *, name):
    rows = pack.shape[0]
    part_rows = rows // 2

    def body(pack_ref, conv_ref, land_ref, conv_land_ref, send1, recv1, send2, recv2, csend, crecv, local_sems):
        x, y, c, chips = _place()
        me = 2 * x + y
        sibling = (x, y, 1 - c)
        part = lambda core: pl.ds(pl.multiple_of(core * part_rows, 16), part_rows)
        remote = lambda src, dst, ss, rs, to: pltpu.make_async_remote_copy(
            src_ref=src, dst_ref=dst, send_sem=ss, recv_sem=rs, device_id=to, device_id_type=_MESH)
        local = [pltpu.make_async_copy(pack_ref, land_ref.at[me], local_sems.at[0]),
                 pltpu.make_async_copy(conv_ref, conv_land_ref.at[me], local_sems.at[1])]
        for cp in local:
            cp.start()
        first = [remote(pack_ref.at[part(c)], land_ref.at[me, part(c)], send1.at[j], recv1.at[j], (px, py, c))
                 for j, (px, py) in enumerate(chips)]
        convs = [remote(conv_ref, conv_land_ref.at[me], csend.at[j], crecv.at[j], (px, py, c))
                 for j, (px, py) in enumerate(chips)]
        for cp in first + convs:
            cp.start()
        passed = []
        for j, (px, py) in enumerate(chips):
            slot = 2 * px + py
            remote(pack_ref.at[part(c)], land_ref.at[slot, part(c)], send1.at[j], recv1.at[j], (px, py, c)).wait_recv()
            cp = remote(land_ref.at[slot, part(c)], land_ref.at[slot, part(c)], send2.at[j], recv2.at[j], sibling)
            cp.start()
            passed.append(cp)
        for j, (px, py) in enumerate(chips):
            slot = 2 * px + py
            remote(land_ref.at[slot, part(1 - c)], land_ref.at[slot, part(1 - c)], send2.at[j], recv2.at[j],
                   sibling).wait_recv()
            remote(conv_ref, conv_land_ref.at[slot], csend.at[j], crecv.at[j], (px, py, c)).wait_recv()
        for cp in first + convs + passed:
            cp.wait_send()
        for cp in local:
            cp.wait()

    sems = [pltpu.SemaphoreType.DMA((3,))] * 6 + [pltpu.SemaphoreType.DMA((2,))]
    return pl.pallas_call(
        body, name=name, in_specs=[_ANY, _ANY], out_specs=[_ANY, _ANY],
        out_shape=[jax.ShapeDtypeStruct((N_SHARD,) + pack.shape, pack.dtype),
                   jax.ShapeDtypeStruct((N_SHARD,) + conv_w.shape, conv_w.dtype)],
        scratch_shapes=sems)(pack, conv_w)


def _half(core):
    return pl.ds(pl.multiple_of(core * HALF, HALF), HALF)


def _reduce_scatter_exchange(g, small=None):
    ins = [g] if small is None else [g, small]
    outs = [jax.ShapeDtypeStruct((7,) + g.shape[1:2] + (HALF,), g.dtype)]
    if small is not None:
        outs.append(jax.ShapeDtypeStruct((8,) + small.shape, small.dtype))

    def plan(src, dst):
        x, y, c, chips = _place()
        me = 2 * x + y
        peers = [(px, py, c if t == 0 else 1 - c) for px, py in chips for t in (0, 1)] + [(x, y, 1 - c)]
        sends = [(src[0].at[2 * px + py, :, _half(pc)], dst[0].at[k], (px, py, pc)) for k, (px, py, pc) in enumerate(peers)]
        recvs = [dst[0].at[k] for k in range(7)]
        local = []
        if small is not None:
            mine = 4 * x + 2 * y + c
            local = [(src[1], dst[1].at[mine])]
            sends += [(src[1], dst[1].at[mine], peer) for peer in peers]
            recvs += [dst[1].at[4 * px + 2 * py + pc] for px, py, pc in peers]
        return local, sends, recvs

    return _Exchange(ins, outs, 7 * len(ins), len(ins) - 1, plan)


def _pair_window_exchange(g):
    def plan(src, dst):
        x, y, c, _ = _place()
        return [], [(src[0].at[:, :, _half(1 - c)], dst[0], (x, y, 1 - c))], [dst[0]]

    return _Exchange([g], [jax.ShapeDtypeStruct(g.shape[:2] + (HALF,), g.dtype)], 1, 0, plan)


def _chip_scatter_exchange(p, small):
    def plan(src, dst):
        x, y, c, chips = _place()
        mine = 4 * x + 2 * y + c
        peers = [(px, py, c if t == 0 else 1 - c) for px, py in chips for t in (0, 1)] + [(x, y, 1 - c)]
        sends = [(src[0].at[2 * px + py], dst[0].at[j], (px, py, c)) for j, (px, py) in enumerate(chips)]
        recvs = [dst[0].at[j] for j in range(3)]
        sends += [(src[1], dst[1].at[mine], peer) for peer in peers]
        recvs += [dst[1].at[4 * px + 2 * py + pc] for px, py, pc in peers]
        return [(src[1], dst[1].at[mine])], sends, recvs

    outs = [jax.ShapeDtypeStruct((3,) + p.shape[1:], p.dtype), jax.ShapeDtypeStruct((8,) + small.shape, small.dtype)]
    return _Exchange([p, small], outs, 10, 1, plan)


def _share_exchange(arrays):
    n = len(arrays)

    def plan(src, dst):
        x, y, c, _ = _place()
        return [], [(src[k], dst[k], (x, y, 1 - c)) for k in range(n)], [dst[k] for k in range(n)]

    return _Exchange(arrays, [jax.ShapeDtypeStruct(a.shape, a.dtype) for a in arrays], n, 0, plan)


def _sum_scatter(g, land, me, core, *, tc, name):
    rows = g.shape[1]
    per = HALF // tc

    def body(where_ref, g_ref, land_ref, o_ref):
        acc = g_ref[...]
        for k in range(7):
            acc = acc + land_ref[k].astype(F32)
        o_ref[...] = acc

    return pl.pallas_call(
        body, name=name, out_shape=jax.ShapeDtypeStruct((rows, HALF), F32), compiler_params=_cp("parallel"),
        grid_spec=pltpu.PrefetchScalarGridSpec(
            num_scalar_prefetch=1, grid=(per,),
            in_specs=[pl.BlockSpec((None, rows, tc), lambda i, w: (w[0], 0, w[1] * per + i)),
                      pl.BlockSpec((7, rows, tc), lambda i, w: (0, 0, i))],
            out_specs=pl.BlockSpec((rows, tc), lambda i, w: (0, i))))(
        jnp.stack([me, core]).astype(jnp.int32), g, land)


def _pair_add(g, land, core, *, name):
    n, rows, _ = g.shape

    def body(core_ref, g_ref, land_ref, o_ref):
        o_ref[...] = (g_ref[...].astype(F32) + land_ref[...].astype(F32)).astype(o_ref.dtype)

    blk = pl.BlockSpec((1, rows, HALF), lambda i, w: (i, 0, 0))
    return pl.pallas_call(
        body, name=name, out_shape=jax.ShapeDtypeStruct((n, rows, HALF), g.dtype), compiler_params=_cp("parallel"),
        grid_spec=pltpu.PrefetchScalarGridSpec(
            num_scalar_prefetch=1, grid=(n,),
            in_specs=[pl.BlockSpec((1, rows, HALF), lambda i, w: (i, 0, w[0])), blk], out_specs=blk))(
        jnp.reshape(core, (1,)).astype(jnp.int32), g, land)


def _sum_chips(p, land, me, *, tc, name):
    rows = p.shape[1]

    def body(me_ref, p_ref, land_ref, o_ref):
        acc = p_ref[...].astype(F32)
        for k in range(3):
            acc = acc + land_ref[k].astype(F32)
        o_ref[...] = acc

    return pl.pallas_call(
        body, name=name, out_shape=jax.ShapeDtypeStruct((rows, HALF), F32), compiler_params=_cp("parallel"),
        grid_spec=pltpu.PrefetchScalarGridSpec(
            num_scalar_prefetch=1, grid=(HALF // tc,),
            in_specs=[pl.BlockSpec((None, rows, tc), lambda i, w: (w[0], 0, i)),
                      pl.BlockSpec((3, rows, tc), lambda i, w: (0, 0, i))],
            out_specs=pl.BlockSpec((rows, tc), lambda i, w: (0, i))))(
        jnp.reshape(me, (1,)).astype(jnp.int32), p, land)


def _sum_slots(a, *, name):
    n = a.shape[0]

    def body(a_ref, o_ref):
        acc = a_ref[0]
        for k in range(1, n):
            acc = acc + a_ref[k]
        o_ref[...] = acc

    return pl.pallas_call(body, name=name, out_shape=jax.ShapeDtypeStruct(a.shape[1:], a.dtype))(a)


def _elementwise(fn, ins, n_out, block, *, name):
    shape = ins[0].shape
    grid = tuple(s // b for s, b in zip(shape, block))
    n_in = len(ins)

    def body(*refs):
        outs = fn(*[r[...] for r in refs[:n_in]])
        for o_ref, val in zip(refs[n_in:], outs):
            o_ref[...] = val

    spec = pl.BlockSpec(block, lambda i, j, k: (i, j, k))
    return pl.pallas_call(body, name=name, grid=grid, in_specs=[spec] * n_in, out_specs=[spec] * n_out,
                          out_shape=[jax.ShapeDtypeStruct(shape, F32)] * n_out,
                          compiler_params=_cp(*["parallel"] * 3))(*ins)


def _adamw_math(w, g, m, v):
    mn = ADAM_B1 * m + (1.0 - ADAM_B1) * g
    vn = ADAM_B2 * v + (1.0 - ADAM_B2) * (g * g)
    m_hat = mn / (1.0 - ADAM_B1 ** ADAM_STEP)
    v_hat = vn / (1.0 - ADAM_B2 ** ADAM_STEP)
    return -ADAM_LR * (m_hat / (jnp.sqrt(v_hat) + ADAM_EPS) + ADAM_WD * w), mn, vn


def _adamw(w, g, m, v, block, *, name):
    return _elementwise(_adamw_math, [w, g, m, v], 3, block, name=name)


def _interleave_layers(layers, *, tc, name):
    rows, cols = layers[0].shape
    n = len(layers)

    def body(*refs):
        for l in range(n):
            refs[n][:, l, :] = refs[l][...]

    return pl.pallas_call(body, name=name, grid=(cols // tc,),
                          in_specs=[pl.BlockSpec((rows, tc), lambda i: (0, i))] * n,
                          out_specs=pl.BlockSpec((rows, n, tc), lambda i: (0, 0, i)),
                          out_shape=jax.ShapeDtypeStruct((rows, n, cols), layers[0].dtype),
                          compiler_params=_cp("parallel"))(*layers)


def _adamw_small(ws, gs, ms, vs, *, name):
    n = len(ws)

    def body(*refs):
        w, g, m, v, outs = refs[:n], refs[n:2 * n], refs[2 * n:3 * n], refs[3 * n:4 * n], refs[4 * n:]
        for k in range(n):
            for slot, val in enumerate(_adamw_math(w[k][...], g[k][...], m[k][...], v[k][...])):
                outs[slot * n + k][...] = val

    outs = pl.pallas_call(body, name=name, out_shape=[jax.ShapeDtypeStruct(a.shape, F32) for a in ws] * 3)(
        *ws, *gs, *ms, *vs)
    return outs[:n], outs[n:2 * n], outs[2 * n:]


def _to_kernel_order(wt):
    gates = jnp.pad(wt[2048:2056], ((0, LANE - 2 * A_HEADS), (0, 0)))
    return jnp.concatenate([wt[0:2048], wt[2056:2568], wt[2824:3336], wt[2568:2696], wt[2696:2824], gates], axis=0)


def _from_kernel_order(main, tail):
    return jnp.concatenate([main[0:2048], tail[C_BG - DH_MAIN:C_BG - DH_MAIN + 2 * A_HEADS],
                            main[C_QB:C_QB + B_WIDTH], tail[0:B_KV_WIDTH], tail[B_KV_WIDTH:2 * B_KV_WIDTH],
                            main[C_ZB:C_ZB + B_WIDTH]], axis=0)


def _gate_params(a_log, dt_bias):
    return jnp.pad(jnp.stack([a_log, dt_bias]), ((0, SUBLANE - 2), (A_HEADS, LANE - 2 * A_HEADS)))


SMALL = ("conv_w", "a_log", "dt_bias", "norm_w", "sinks", "ln_g", "ln_b")


def _pack(parts, cols):
    flat = jnp.concatenate([p.reshape(-1) for p in parts])
    rows = -(-flat.shape[0] // cols)
    return jnp.pad(flat, (0, rows * cols - flat.shape[0])).reshape(rows, cols)


def _unpack(packed, shapes):
    flat = packed.reshape(-1)
    out, at = [], 0
    for s in shapes:
        n = math.prod(s)
        out.append(flat[at:at + n].reshape(s))
        at += n
    return out


def kernel(x, w_in, conv_w, a_log, dt_bias, norm_w, sinks, w_out, ln_g, ln_b, loss_target, m_w_in, m_conv_w, m_a_log, m_dt_bias, m_norm_w, m_sinks, m_w_out, m_ln_g, m_ln_b, v_w_in, v_conv_w, v_a_log, v_dt_bias, v_norm_w, v_sinks, v_w_out, v_ln_g, v_ln_b):
    xi, yi, ci = lax.axis_index("x"), lax.axis_index("y"), lax.axis_index("c")
    me = 2 * xi + yi

    to_t = lambda a: jnp.transpose(a, (2, 0, 1))
    from_t = lambda a: jnp.transpose(a, (1, 2, 0))

    wt_shard = to_t(w_in)

    def pack_weights(l):
        rows = jnp.pad(wt_shard[:, l], ((0, IN_PAD - IN_SHARD), (0, 0)))
        return jnp.concatenate([rows, w_out[l]], axis=0).astype(BF16)

    pack0, pack1 = pack_weights(0), pack_weights(1)
    got_in0, g_conv = _gather_two_level(pack0[:IN_PAD], conv_w, name="gather_weights_0")
    conv_full = jnp.moveaxis(g_conv, 0, 2).reshape(DEPTH, CONV_K, 3 * A_WIDTH)
    piece = IN_PAD // 3
    carriers = ("dn_pre", "dn_wy", "dn_scan")
    gathers = {nm: _gather_exchange([pack1[i * piece:(i + 1) * piece]]) for i, nm in enumerate(carriers)}
    gathers.update(in_proj=_gather_exchange([pack0[IN_PAD:]]), swa=_gather_exchange([pack1[IN_PAD:]]))
    w_in_of = lambda rows: _to_kernel_order(rows[:, :IN_SHARD].reshape(IN_COLS, D_MODEL))
    w_out_of = lambda rows: rows.reshape(D_MODEL, D_MODEL)
    args0 = _layer_args(w_in_of(got_in0), conv_full[0], a_log[0], dt_bias[0], sinks[0], norm_w[0],
                        lambda got: w_out_of(got[0]))

    def args1(got):
        rows = jnp.concatenate([got[nm][0] for nm in carriers], axis=1)
        return _layer_args(w_in_of(rows), conv_full[1], a_log[1], dt_bias[1], sinks[1], norm_w[1],
                           w_out_of(got["swa"][0]))

    def pack_grads(g):
        gin = _from_kernel_order(*g["w_in"]).reshape(N_SHARD, IN_SHARD, D_MODEL)
        gin = jnp.pad(gin, ((0, 0), (0, IN_PAD - IN_SHARD), (0, 0)))
        return jnp.concatenate([gin, g["w_out"].reshape(N_SHARD, OUT_SHARD, D_MODEL)], axis=1).astype(BF16)

    packed = {}

    def reduce1(grads1):
        packed[1] = pack_grads(grads1)
        return _reduce_scatter_exchange(packed[1])

    def reduce0(grads0, grads1, loss_tile):
        g0 = pack_grads(grads0)
        from_sibling = _run_exchange(_pair_window_exchange(g0), name="pair_reduce_0")[0]
        packed[0] = _pair_add(g0, from_sibling, ci, name="pair_add_0")
        gsmall = _pack([jnp.stack([g[nm] for g in (grads0, grads1)]) for nm in SMALL] + [loss_tile[0, 0:1]], D_MODEL)
        return _chip_scatter_exchange(packed[0], gsmall)

    _, dx, grads, landed1, (landed0, landed_small) = _local_step(
        x[0], loss_target[0], args0, args1, ln_g, ln_b, gathers=gathers, reduce1=reduce1, reduce0=reduce0)

    small_shapes = [(DEPTH,) + grads[0][nm].shape for nm in SMALL]
    halves = [_sum_chips(packed[0], landed0, me, tc=2 * LANE, name="reduce_sum_0"),
              _sum_scatter(packed[1], landed1[0], me, ci, tc=2 * LANE, name="reduce_sum_1")]
    s_small = _sum_slots(landed_small, name="reduce_sum_small")
    others = _run_exchange(_share_exchange(halves), name="pair_share")
    full = [jnp.where(ci == 0, jnp.concatenate([mine, other], axis=1), jnp.concatenate([other, mine], axis=1))
            for mine, other in zip(halves, others)]
    grad_in_layers = [f[:IN_SHARD] for f in full]
    grad_out = jnp.stack([f[IN_PAD:] for f in full])
    out_blk = (1, OUT_SHARD, D_MODEL)
    *small_grads, loss = _unpack(s_small, small_shapes + [()])
    gs = dict(zip(SMALL, small_grads))
    gs["conv_w"] = lax.dynamic_slice_in_dim(gs["conv_w"], me * CONV_SHARD, CONV_SHARD, axis=2)

    grad_in_t = _interleave_layers(grad_in_layers, tc=2 * LANE, name="grad_in_layers")
    d_in, nm_in, nv_in = (from_t(o) for o in _adamw(to_t(w_in), grad_in_t, to_t(m_w_in), to_t(v_w_in),
                                                    (IN_SHARD // 6, DEPTH, D_MODEL), name="adamw_in"))
    grad_in = from_t(grad_in_t)
    d_out, nm_out, nv_out = _adamw(w_out, grad_out, m_w_out, v_w_out, out_blk, name="adamw_out")
    ws = dict(conv_w=conv_w, a_log=a_log, dt_bias=dt_bias, norm_w=norm_w, sinks=sinks, ln_g=ln_g, ln_b=ln_b)
    ms = dict(conv_w=m_conv_w, a_log=m_a_log, dt_bias=m_dt_bias, norm_w=m_norm_w, sinks=m_sinks, ln_g=m_ln_g, ln_b=m_ln_b)
    vs = dict(conv_w=v_conv_w, a_log=v_a_log, dt_bias=v_dt_bias, norm_w=v_norm_w, sinks=v_sinks, ln_g=v_ln_g, ln_b=v_ln_b)
    d_s, nm_s, nv_s = (dict(zip(SMALL, o)) for o in _adamw_small(*[[d[nm] for nm in SMALL] for d in (ws, gs, ms, vs)],
                                                                 name="adamw_small"))

    def in_order(big_in, small, big_out):
        return (big_in, small["conv_w"], small["a_log"], small["dt_bias"], small["norm_w"], small["sinks"], big_out,
                small["ln_g"], small["ln_b"])

    return (loss, dx[None], *in_order(grad_in, gs, grad_out), *in_order(d_in, d_s, d_out),
            *in_order(nm_in, nm_s, nm_out), *in_order(nv_in, nv_s, nv_out))
```

```python
import math

import jax
import jax.numpy as jnp
from jax import lax
from jax.experimental import pallas as pl
from jax.experimental.pallas import tpu as pltpu

F32 = jnp.float32
BF16 = jnp.bfloat16
HI = lax.Precision.HIGHEST

D_MODEL = 1024
DEPTH = 2
A_HEADS = 4
A_HEAD_DIM = 128
A_WIDTH = 512
CONV_K = 4
CHUNK = 64
B_Q_HEADS = 8
B_KV_HEADS = 2
B_HEAD_DIM = 64
B_GROUP = 4
B_WIDTH = 512
B_KV_WIDTH = 128
BLOCK = 128
IN_COLS = 3336
DEEPNORM_ALPHA = (2 * DEPTH) ** 0.25
LN_EPS = 1e-5
RMS_EPS = 1e-6
L2_EPS = 1e-6
ADAM_LR = 0.001
ADAM_B1 = 0.9
ADAM_B2 = 0.999
ADAM_EPS = 1e-08
ADAM_WD = 0.01
ADAM_STEP = 10

N_SHARD = 4
IN_SHARD = IN_COLS // N_SHARD
OUT_SHARD = D_MODEL // N_SHARD
CONV_SHARD = 3 * A_WIDTH // N_SHARD
IN_PAD = -(-IN_SHARD // 96) * 96

P_COLS = 3456
C_PRE = 0
C_ZA = 1536
C_QB = 2048
C_ZB = 2560
C_KB = 3072
C_VB = 3200
C_BG = 3328
DH_MAIN = C_KB
LANE = 128
SUBLANE = 8
VMEM_LIMIT = 56 * 1024 * 1024
ALIBI = tuple(2.0 ** (-8.0 * (h + 1) / B_Q_HEADS) for h in range(B_Q_HEADS))
NEG = -1e30


def _cp(*sem):
    return pltpu.CompilerParams(dimension_semantics=sem, vmem_limit_bytes=VMEM_LIMIT)


def _dot(a, b):
    return jnp.dot(a.astype(BF16), b.astype(BF16), preferred_element_type=F32)


def _dot_nt(a, b):
    return lax.dot_general(a.astype(BF16), b.astype(BF16), (((1,), (1,)), ((), ())),
                           preferred_element_type=F32)


def _dot_tn(a, b):
    return lax.dot_general(a.astype(BF16), b.astype(BF16), (((0,), (0,)), ((), ())),
                           preferred_element_type=F32)


def _dot_hi(a, b):
    return jnp.dot(a, b, precision=HI, preferred_element_type=F32)


def _sigmoid(x):
    return jax.nn.sigmoid(x)


def _silu(x):
    return x * _sigmoid(x)


def _dsilu(x):
    s = _sigmoid(x)
    return s * (1.0 + x * (1.0 - s))


def _softplus(x):
    return jnp.maximum(x, 0.0) + jnp.log(1.0 + jnp.exp(-jnp.abs(x)))


def _shift_down(cur, before, s):
    if s == 0:
        return cur
    r = pltpu.roll(cur, s, 0)
    rb = pltpu.roll(before, s, 0)
    row = lax.broadcasted_iota(jnp.int32, before.shape, 0)
    head = jnp.where(row < s, rb, r[0:SUBLANE])
    return jnp.concatenate([head, r[SUBLANE:]], axis=0)


def _shift_up(cur, after, s):
    if s == 0:
        return cur
    n = cur.shape[0]
    r = pltpu.roll(cur, n - s, 0)
    ra = pltpu.roll(after, SUBLANE - s, 0)
    row = lax.broadcasted_iota(jnp.int32, after.shape, 0)
    tail = jnp.where(row >= SUBLANE - s, ra, r[n - SUBLANE:])
    return jnp.concatenate([r[:n - SUBLANE], tail], axis=0)


def _conv_fwd(cur, before, w):
    acc = cur * w[CONV_K - 1:CONV_K, :]
    for s in range(1, CONV_K):
        acc = acc + _shift_down(cur, before, s) * w[CONV_K - 1 - s:CONV_K - s, :]
    return acc


def _matmul_nt(a, bt, *, tm, name, carry=None):
    m, k = a.shape
    n = bt.shape[0]
    c_ins, c_in_specs, c_out_specs, c_outs, c_scratch = _carry_specs(carry)

    def body(*refs):
        a_ref, b_ref, o_ref = _carried(carry, refs, 2, 1, m // tm)
        o_ref[...] = _dot_nt(a_ref[...], b_ref[...])

    outs = pl.pallas_call(
        body, name=name, grid=(m // tm,),
        in_specs=[pl.BlockSpec((tm, k), lambda i: (i, 0)), pl.BlockSpec((n, k), lambda i: (0, 0))] + c_in_specs,
        out_specs=[pl.BlockSpec((tm, n), lambda i: (i, 0))] + c_out_specs,
        out_shape=[jax.ShapeDtypeStruct((m, n), F32)] + c_outs,
        scratch_shapes=c_scratch,
        compiler_params=_cp("arbitrary"))(a, bt, *c_ins)
    return outs[0], outs[1:]


def _dn_pre(h, conv_w, par, *, tt, name, carry=None):
    t = h.shape[0]
    cw = 3 * A_WIDTH
    hb = tt // SUBLANE

    c_ins, c_in_specs, c_out_specs, c_outs, c_scratch = _carry_specs(carry)

    def body(*refs):
        (pre_ref, halo_ref, bgi_ref, cw_ref, par_ref,
         q_ref, k_ref, v_ref, bg_ref, bgt_ref) = _carried(carry, refs, 5, 5, t // tt)
        i = pl.program_id(0)
        cur = pre_ref[...]
        before = jnp.where(i > 0, halo_ref[...], 0.0)
        s = _silu(_conv_fwd(cur, before, cw_ref[...]))
        for hd in range(A_HEADS):
            sl = slice(hd * LANE, (hd + 1) * LANE)
            tq = s[:, hd * LANE:(hd + 1) * LANE]
            q_ref[:, sl] = tq * (lax.rsqrt(jnp.sum(tq * tq, -1, keepdims=True) + L2_EPS) * (A_HEAD_DIM ** -0.5))
            tk = s[:, A_WIDTH + hd * LANE:A_WIDTH + (hd + 1) * LANE]
            k_ref[:, sl] = tk * lax.rsqrt(jnp.sum(tk * tk, -1, keepdims=True) + L2_EPS)
        v_ref[...] = s[:, 2 * A_WIDTH:]
        raw = bgi_ref[...]
        lane = lax.broadcasted_iota(jnp.int32, raw.shape, 1)
        is_a = (lane >= A_HEADS) & (lane < 2 * A_HEADS)
        g = jnp.where(is_a, -jnp.exp(par_ref[0:1, :]) * _softplus(raw + par_ref[1:2, :]), 0.0)
        gc = _dot_hi(_chunk_tri(tt, lower=True), g)
        bg = jnp.where(lane < A_HEADS, _sigmoid(raw), gc)
        bg_ref[...] = bg
        bgt_ref[...] = jnp.transpose(bg)[0:SUBLANE, :]

    wide = jax.ShapeDtypeStruct((t, A_WIDTH), F32)
    outs = pl.pallas_call(
        body, name=name, grid=(t // tt,),
        in_specs=[pl.BlockSpec((tt, cw), lambda i: (i, 0)),
                  pl.BlockSpec((SUBLANE, cw), lambda i: (jnp.maximum(i * hb - 1, 0), 0)),
                  pl.BlockSpec((tt, LANE), lambda i: (i, C_BG // LANE)),
                  pl.BlockSpec((CONV_K, cw), lambda i: (0, 0)),
                  pl.BlockSpec((SUBLANE, LANE), lambda i: (0, 0))] + c_in_specs,
        out_specs=[pl.BlockSpec((tt, A_WIDTH), lambda i: (i, 0))] * 3
        + [pl.BlockSpec((tt, LANE), lambda i: (i, 0)), pl.BlockSpec((SUBLANE, tt), lambda i: (0, i))] + c_out_specs,
        out_shape=[wide, wide, wide, jax.ShapeDtypeStruct((t, LANE), F32),
                   jax.ShapeDtypeStruct((SUBLANE, t), F32)] + c_outs,
        scratch_shapes=c_scratch,
        compiler_params=_cp("arbitrary"))(h, h, h, conv_w, par, *c_ins)
    return outs[:5], outs[5:]


def _chunk_tri(n, lower):
    r = lax.broadcasted_iota(jnp.int32, (n, n), 0)
    c = lax.broadcasted_iota(jnp.int32, (n, n), 1)
    shift = CHUNK.bit_length() - 1
    same = jnp.right_shift(r, shift) == jnp.right_shift(c, shift)
    return (same & ((c <= r) if lower else (c >= r))).astype(F32)


def _chunk_masks():
    r = lax.broadcasted_iota(jnp.int32, (CHUNK, CHUNK), 0)
    c = lax.broadcasted_iota(jnp.int32, (CHUNK, CHUNK), 1)
    return r >= c, r > c, r == c


def _split(a):
    hi = a.astype(BF16)
    return hi, (a - hi.astype(F32)).astype(BF16)


def _dot3(a, b):
    (ah, al), (bh, bl) = a, b
    d = lambda p, q: jnp.dot(p, q, preferred_element_type=F32)
    return d(ah, bh) + (d(ah, bl) + d(al, bh))


def _tri_inv_many(a_list, eye):
    d = lambda p, q: jnp.dot(p, q, preferred_element_type=F32)
    p = [(-a).astype(BF16) for a in a_list]
    tm = [eye - a for a in a_list]
    for _ in range(5):
        pf = [d(pi, pi) for pi in p]
        p = [x.astype(BF16) for x in pf]
        tm = [t + d(t.astype(BF16), pi) for t, pi in zip(tm, p)]
    ms = [_split(eye + a) for a in a_list]
    res = [eye - _dot3(m, _split(t)) for m, t in zip(ms, tm)]
    return [t + d(t.astype(BF16), r.astype(BF16)) for t, r in zip(tm, res)]


def _chunk_gates(bg_v, bgt_v, hd):
    return (bg_v[:, hd:hd + 1], bg_v[:, A_HEADS + hd:A_HEADS + hd + 1],
            None if bgt_v is None else bgt_v[A_HEADS + hd:A_HEADS + hd + 1, :])


WY_ROWS = 512
SCAN_ROWS = 256
WY_GROUP = 8


def _dn_wy(q, k, v, bg, bgt, *, name, carry=None):
    t = q.shape[0]
    rows = WY_ROWS

    c_ins, c_in_specs, c_out_specs, c_outs, c_scratch = _carry_specs(carry)

    def body(*refs):
        q_ref, k_ref, v_ref, bg_ref, bgt_ref, u_ref, w_ref, tm_ref, qk_ref = _carried(carry, refs, 5, 4, t // rows)
        causal, strict, diag = _chunk_masks()
        eye = diag.astype(F32)
        for c0 in range(0, rows // CHUNK, WY_GROUP):
            items = [(c, hd) for c in range(c0, c0 + WY_GROUP) for hd in range(A_HEADS)]
            rs = lambda c: slice(c * CHUNK, (c + 1) * CHUNK)
            sl = lambda hd: slice(hd * LANE, (hd + 1) * LANE)
            hs = lambda hd: slice(hd * CHUNK, (hd + 1) * CHUNK)
            gates = [_chunk_gates(bg_ref[rs(c), :], bgt_ref[:, rs(c)], hd) for c, hd in items]
            dms = [jnp.exp(jnp.where(causal, gcol - grow, NEG)) for _, gcol, grow in gates]
            kbs = [k_ref[rs(c), sl(hd)] * g[0] for (c, hd), g in zip(items, gates)]
            a_list = [jnp.where(strict, _dot_nt(kb, k_ref[rs(c), sl(hd)]) * dm, 0.0)
                      for (c, hd), kb, dm in zip(items, kbs, dms)]
            for (c, hd), dm in zip(items, dms):
                qk_ref[rs(c), hs(hd)] = jnp.where(
                    causal, _dot_nt(q_ref[rs(c), sl(hd)], k_ref[rs(c), sl(hd)]) * dm, 0.0)
            tms = _tri_inv_many(a_list, eye)
            for (c, hd), g, kb, tmat in zip(items, gates, kbs, tms):
                tm_ref[rs(c), hs(hd)] = tmat
                u_ref[rs(c), sl(hd)] = _dot(tmat, v_ref[rs(c), sl(hd)] * g[0])
                w_ref[rs(c), sl(hd)] = _dot(tmat, kb * jnp.exp(g[1])).astype(BF16)

    blk = pl.BlockSpec((rows, A_WIDTH), lambda i: (i, 0))
    half = pl.BlockSpec((rows, A_HEADS * CHUNK), lambda i: (i, 0))
    outs = pl.pallas_call(
        body, name=name, grid=(t // rows,),
        in_specs=[blk, blk, blk, pl.BlockSpec((rows, LANE), lambda i: (i, 0)),
                  pl.BlockSpec((SUBLANE, rows), lambda i: (0, i))] + c_in_specs,
        out_specs=[blk, blk, half, half] + c_out_specs,
        out_shape=[jax.ShapeDtypeStruct((t, A_WIDTH), F32), jax.ShapeDtypeStruct((t, A_WIDTH), BF16),
                   jax.ShapeDtypeStruct((t, A_HEADS * CHUNK), F32),
                   jax.ShapeDtypeStruct((t, A_HEADS * CHUNK), F32)] + c_outs,
        scratch_shapes=c_scratch,
        compiler_params=_cp("arbitrary"))(q, k, v, bg, bgt, *c_ins)
    return outs[:4], outs[4:]


def _dn_scan_fwd(q, k, u, w, qk, bg, *, name, carry=None):
    t = q.shape[0]
    rows = SCAN_ROWS
    per = rows // CHUNK
    c_ins, c_in_specs, c_out_specs, c_outs, c_scratch = _carry_specs(carry)

    def body(*refs):
        q_ref, k_ref, u_ref, w_ref, qk_ref, bg_ref, o_ref, vn_ref, s_ref, state = _carried(carry, refs, 6, 3, t // rows)

        @pl.when(pl.program_id(0) == 0)
        def _():
            state[...] = jnp.zeros_like(state)

        heads = range(A_HEADS)
        sl = lambda hd: slice(hd * LANE, (hd + 1) * LANE)
        s_cur = [state[hd] for hd in heads]
        for c in range(per):
            rs = slice(c * CHUNK, (c + 1) * CHUNK)
            bg_v = bg_ref[rs, :]
            gcols = [_chunk_gates(bg_v, None, hd)[1] for hd in heads]
            glasts = [gc[CHUNK - 1:CHUNK, :] for gc in gcols]
            for hd in heads:
                s_ref[c, hd] = s_cur[hd].astype(BF16)
            vns = [u_ref[rs, sl(hd)] - _dot(w_ref[rs, sl(hd)], s_cur[hd]) for hd in heads]
            qss = [_dot(q_ref[rs, sl(hd)] * jnp.exp(gcols[hd]), s_cur[hd]) for hd in heads]
            s_cur = [s_cur[hd] * jnp.exp(glasts[hd])
                     + _dot_tn(k_ref[rs, sl(hd)] * jnp.exp(glasts[hd] - gcols[hd]), vns[hd]) for hd in heads]
            for hd in heads:
                vn_ref[rs, sl(hd)] = vns[hd]
                o_ref[rs, sl(hd)] = qss[hd] + _dot(qk_ref[rs, hd * CHUNK:(hd + 1) * CHUNK], vns[hd])
        for hd in heads:
            state[hd] = s_cur[hd]

    blk = pl.BlockSpec((rows, A_WIDTH), lambda i: (i, 0))
    half = pl.BlockSpec((rows, A_HEADS * CHUNK), lambda i: (i, 0))
    wide = jax.ShapeDtypeStruct((t, A_WIDTH), F32)
    outs = pl.pallas_call(
        body, name=name, grid=(t // rows,),
        in_specs=[blk, blk, blk, blk, half, pl.BlockSpec((rows, LANE), lambda i: (i, 0))] + c_in_specs,
        out_specs=[blk, blk, pl.BlockSpec((per, A_HEADS, LANE, LANE), lambda i: (i, 0, 0, 0))] + c_out_specs,
        out_shape=[wide, wide, jax.ShapeDtypeStruct((t // CHUNK, A_HEADS, LANE, LANE), BF16)] + c_outs,
        scratch_shapes=[pltpu.VMEM((A_HEADS, LANE, LANE), F32)] + c_scratch,
        compiler_params=_cp("arbitrary"))(q, k, u, w, qk, bg, *c_ins)
    return outs[:3], outs[3:]


def _swa_neg_dist(n_blk):
    qi = lax.broadcasted_iota(jnp.int32, (BLOCK, 2 * BLOCK), 0)
    si = lax.broadcasted_iota(jnp.int32, (BLOCK, 2 * BLOCK), 1)
    dist = qi + BLOCK - si
    mask = (dist >= 0) & (dist < BLOCK) & ((si >= BLOCK) | (n_blk > 0))
    return jnp.where(mask, -dist.astype(F32), NEG)


def _stack_heads(ref, hk):
    return jnp.concatenate([ref[:, h * B_HEAD_DIM:(h + 1) * B_HEAD_DIM]
                            for h in range(hk * B_GROUP, (hk + 1) * B_GROUP)], axis=0)


def _swa_group_probs(q_ref, sk_ref, kband, vband, neg_dist):
    hks = range(B_KV_HEADS)
    heads = lambda hk: range(hk * B_GROUP, (hk + 1) * B_GROUP)
    ksl = lambda hk: slice(hk * B_HEAD_DIM, (hk + 1) * B_HEAD_DIM)
    ones = jnp.ones((2 * BLOCK, B_HEAD_DIM), BF16)
    qs = [_stack_heads(q_ref, hk) * (B_HEAD_DIM ** -0.5) for hk in hks]
    sink = [jnp.concatenate([jnp.broadcast_to(sk_ref[h:h + 1, 0:1], (BLOCK, 1)) for h in heads(hk)], axis=0)
            for hk in hks]
    s = [_dot_nt(qs[hk], kband[:, ksl(hk)]) + jnp.concatenate([ALIBI[h] * neg_dist for h in heads(hk)], axis=0)
         for hk in hks]
    m = [jnp.maximum(jnp.max(s[hk], axis=-1, keepdims=True), sink[hk]) for hk in hks]
    p = [jnp.exp(s[hk] - m[hk]) for hk in hks]
    oe = [jnp.dot(p[hk].astype(BF16), jnp.concatenate([vband[:, ksl(hk)].astype(BF16), ones], axis=1),
                  preferred_element_type=F32) for hk in hks]
    ps = [jnp.exp(sink[hk] - m[hk]) for hk in hks]
    inv = [1.0 / (oe[hk][:, B_HEAD_DIM:B_HEAD_DIM + 1] + ps[hk]) for hk in hks]
    return [(qs[hk], p[hk] * inv[hk], ps[hk] * inv[hk], oe[hk][:, :B_HEAD_DIM] * inv[hk]) for hk in hks]


def _swa_specs():
    qspec = lambda c0: pl.BlockSpec((BLOCK, B_WIDTH), lambda i: (i, c0 // B_WIDTH))
    cur = lambda c0: pl.BlockSpec((BLOCK, LANE), lambda i: (i, c0 // LANE))
    prev = lambda c0: pl.BlockSpec((BLOCK, LANE), lambda i: (jnp.maximum(i - 1, 0), c0 // LANE))
    return qspec, cur, prev


def _carried(carry, refs, n_in, n_out, steps):
    if carry is None:
        return refs
    ci, co = len(carry.ins), len(carry.outs)
    own = refs[:n_in] + refs[n_in + ci:n_in + ci + n_out] + refs[n_in + ci + n_out + co:len(refs) - 3]
    parts = refs[n_in:n_in + ci], refs[n_in + ci + n_out:n_in + ci + n_out + co], refs[len(refs) - 3:]

    @pl.when(pl.program_id(0) == 0)
    def _():
        carry.start(*parts)

    @pl.when(pl.program_id(0) == steps - 1)
    def _():
        carry.finish(*parts)

    return own


def _carry_specs(carry):
    if carry is None:
        return [], [], [], [], []
    return (list(carry.ins), [_ANY] * len(carry.ins), [_ANY] * len(carry.outs), list(carry.outs), carry.scratch())


def _swa_fwd(h, sinks_b, *, name, carry=None):
    t = h.shape[0]
    qspec, cur, prev = _swa_specs()
    c_ins, c_in_specs, c_out_specs, c_outs, c_scratch = _carry_specs(carry)

    def body(*refs):
        q_ref, kc_ref, kp_ref, vc_ref, vp_ref, sk_ref, o_ref = _carried(carry, refs, 6, 1, t // BLOCK)
        n_blk = pl.program_id(0)
        kband = jnp.concatenate([kp_ref[...], kc_ref[...]], axis=0)
        vband = jnp.concatenate([vp_ref[...], vc_ref[...]], axis=0)
        groups = _swa_group_probs(q_ref, sk_ref, kband, vband, _swa_neg_dist(n_blk))
        for hk, (_, _, _, o) in enumerate(groups):
            for g in range(B_GROUP):
                hq = hk * B_GROUP + g
                o_ref[:, hq * B_HEAD_DIM:(hq + 1) * B_HEAD_DIM] = o[g * BLOCK:(g + 1) * BLOCK]

    outs = pl.pallas_call(
        body, name=name, grid=(t // BLOCK,),
        in_specs=[qspec(C_QB), cur(C_KB), prev(C_KB), cur(C_VB), prev(C_VB),
                  pl.BlockSpec((B_Q_HEADS, LANE), lambda i: (0, 0))] + c_in_specs,
        out_specs=[pl.BlockSpec((BLOCK, B_WIDTH), lambda i: (i, 0))] + c_out_specs,
        out_shape=[jax.ShapeDtypeStruct((t, B_WIDTH), F32)] + c_outs,
        scratch_shapes=c_scratch,
        compiler_params=_cp("arbitrary"))(h, h, h, h, h, sinks_b, *c_ins)
    return outs[0], outs[1:]


def _rms_gate(o, za, nw):
    outs = []
    for hd in range(A_HEADS):
        oh = o[:, hd * LANE:(hd + 1) * LANE]
        r = lax.rsqrt(jnp.mean(oh * oh, -1, keepdims=True) + RMS_EPS)
        outs.append(oh * r * nw)
    return jnp.concatenate(outs, axis=1) * _silu(za)


def _out_ln(x, oa, ob, h, norm_w, w_out, ln_g, ln_b, *, tm, name, target=None):
    t = x.shape[0]
    last = target is not None

    def body(*refs):
        x_ref, oa_ref, ob_ref, za_ref, zb_ref, nw_ref, w_ref, g_ref, b_ref = refs[:9]
        xn_ref, mx_ref, r_ref = refs[9 + last:12 + last]
        ya = _rms_gate(oa_ref[...], za_ref[...], nw_ref[...])
        yb = ob_ref[...] * _silu(zb_ref[...])
        mixed = jnp.concatenate([ya, yb], axis=1).astype(BF16)
        mx_ref[...] = mixed
        r = DEEPNORM_ALPHA * x_ref[...] + jnp.dot(mixed, w_ref[...], preferred_element_type=F32)
        r_ref[...] = r
        mu = jnp.mean(r, -1, keepdims=True)
        xc = r - mu
        var = jnp.mean(xc * xc, -1, keepdims=True)
        xn = xc * lax.rsqrt(var + LN_EPS) * g_ref[...] + b_ref[...]
        if not last:
            xn_ref[...] = xn
            return
        loss_ref = refs[13]

        @pl.when(pl.program_id(0) == 0)
        def _():
            loss_ref[...] = jnp.zeros_like(loss_ref)

        err = xn - refs[9][...]
        xn_ref[...] = err * (1.0 / D_MODEL)
        loss_ref[...] += 0.5 / D_MODEL * jnp.sum(err * err)

    row = lambda w, c: pl.BlockSpec((tm, w), lambda i: (i, c))
    full = lambda a, b: pl.BlockSpec((a, b), lambda i: (0, 0))
    wide = jax.ShapeDtypeStruct((t, D_MODEL), F32)
    return pl.pallas_call(
        body, name=name, grid=(t // tm,),
        in_specs=[row(D_MODEL, 0), row(A_WIDTH, 0), row(B_WIDTH, 0), row(A_WIDTH, C_ZA // A_WIDTH),
                  row(B_WIDTH, C_ZB // B_WIDTH), full(1, LANE), full(D_MODEL, D_MODEL), full(1, D_MODEL),
                  full(1, D_MODEL)] + [row(D_MODEL, 0)] * last,
        out_specs=[row(D_MODEL, 0), row(D_MODEL, 0), row(D_MODEL, 0)] + [full(SUBLANE, LANE)] * last,
        out_shape=[wide, jax.ShapeDtypeStruct((t, D_MODEL), BF16), wide]
        + [jax.ShapeDtypeStruct((SUBLANE, LANE), F32)] * last,
        compiler_params=_cp("arbitrary" if last else "parallel"))(
        x, oa, ob, h, h, norm_w, w_out, ln_g, ln_b, *([target] if last else []))


def _layer_fwd(x, wt, conv_w, par, sinks_b, norm_w, w_out_bf, ln_g, ln_b, l, carries=None, target=None):
    carries = carries or {}
    h, got_in = _matmul_nt(x, wt, tm=512, name=f"in_proj_{l}", carry=carries.get("in_proj"))
    if callable(w_out_bf):
        w_out_bf = w_out_bf(got_in)
    (q, k, v, bg, bgt), got_pre = _dn_pre(h, conv_w, par, tt=512, name=f"dn_pre_{l}", carry=carries.get("dn_pre"))
    (u, w, tmat, qk), got_wy = _dn_wy(q, k, v, bg, bgt, name=f"dn_wy_{l}", carry=carries.get("dn_wy"))
    (oa, vn, s_all), got_scan = _dn_scan_fwd(q, k, u, w, qk, bg, name=f"dn_scan_{l}", carry=carries.get("dn_scan"))
    ob, got_swa = _swa_fwd(h, sinks_b, name=f"swa_fwd_{l}", carry=carries.get("swa"))
    xn, mixed, r, *loss = _out_ln(x, oa, ob, h, norm_w, w_out_bf, ln_g, ln_b, tm=512, name=f"out_ln_{l}", target=target)
    if loss:
        xn = (xn, loss[0])
    res = dict(x=x, h=h, q=q, k=k, v=v, bg=bg, bgt=bgt, w=w, tmat=tmat, qk=qk, vn=vn, oa=oa, s_all=s_all,
               mixed=mixed, r=r, w_out=w_out_bf)
    return xn, res, dict(in_proj=got_in, dn_pre=got_pre, dn_wy=got_wy, dn_scan=got_scan, swa=got_swa)


def _ln_out_bwd(dxn, r, mixed, ln_g, w_out, *, tm, name):
    t = dxn.shape[0]

    def body(dxn_ref, r_ref, mx_ref, g_ref, w_ref, dr_ref, dm_ref, dw_ref, dg_ref, db_ref):
        @pl.when(pl.program_id(0) == 0)
        def _():
            dw_ref[...] = jnp.zeros_like(dw_ref)
            dg_ref[...] = jnp.zeros_like(dg_ref)
            db_ref[...] = jnp.zeros_like(db_ref)

        rr = r_ref[...]
        xc = rr - jnp.mean(rr, -1, keepdims=True)
        rstd = lax.rsqrt(jnp.mean(xc * xc, -1, keepdims=True) + LN_EPS)
        xhat = xc * rstd
        dxn_v = dxn_ref[...]
        dxh = dxn_v * g_ref[...]
        dr = rstd * (dxh - jnp.mean(dxh, -1, keepdims=True) - xhat * jnp.mean(dxh * xhat, -1, keepdims=True))
        dr_ref[...] = dr
        dg_ref[...] += jnp.sum(dxn_v * xhat, axis=0, keepdims=True)
        db_ref[...] += jnp.sum(dxn_v, axis=0, keepdims=True)
        drb = dr.astype(BF16)
        dm_ref[...] = _dot_nt(drb, w_ref[...])
        dw_ref[...] += _dot_tn(mx_ref[...], drb)

    row = pl.BlockSpec((tm, D_MODEL), lambda i: (i, 0))
    full = lambda a, b: pl.BlockSpec((a, b), lambda i: (0, 0))
    big = jax.ShapeDtypeStruct((t, D_MODEL), F32)
    vec = jax.ShapeDtypeStruct((1, D_MODEL), F32)
    return pl.pallas_call(
        body, name=name, grid=(t // tm,),
        in_specs=[row, row, row, full(1, D_MODEL), full(D_MODEL, D_MODEL)],
        out_specs=[row, row, full(D_MODEL, D_MODEL), full(1, D_MODEL), full(1, D_MODEL)],
        out_shape=[big, big, jax.ShapeDtypeStruct((D_MODEL, D_MODEL), F32), vec, vec],
        compiler_params=_cp("arbitrary"))(dxn, r, mixed, ln_g, w_out)


def _dn_post_bwd(dm, oa, h, norm_w, *, tm, name):
    t = oa.shape[0]

    def body(dy_ref, o_ref, za_ref, nw_ref, do_ref, dza_ref, dnw_ref):
        @pl.when(pl.program_id(0) == 0)
        def _():
            dnw_ref[...] = jnp.zeros_like(dnw_ref)

        nw = nw_ref[...]
        dnw = jnp.zeros_like(nw)
        for hd in range(A_HEADS):
            sl = slice(hd * LANE, (hd + 1) * LANE)
            oh, za, dy = o_ref[:, sl], za_ref[:, sl], dy_ref[:, sl]
            rs = lax.rsqrt(jnp.mean(oh * oh, -1, keepdims=True) + RMS_EPS)
            nrm = oh * rs
            dza_ref[:, sl] = dy * nrm * nw * _dsilu(za)
            dn = dy * _silu(za)
            dnw = dnw + jnp.sum(dn * nrm, axis=0, keepdims=True)
            dnn = dn * nw
            do_ref[:, sl] = rs * dnn - oh * (rs * rs * rs) * jnp.mean(dnn * oh, -1, keepdims=True)
        dnw_ref[...] += dnw

    row = lambda c: pl.BlockSpec((tm, A_WIDTH), lambda i: (i, c))
    wide = jax.ShapeDtypeStruct((t, A_WIDTH), F32)
    return pl.pallas_call(
        body, name=name, grid=(t // tm,),
        in_specs=[row(0), row(0), row(C_ZA // A_WIDTH), pl.BlockSpec((1, LANE), lambda i: (0, 0))],
        out_specs=[row(0), row(C_ZA // A_WIDTH), pl.BlockSpec((1, LANE), lambda i: (0, 0))],
        out_shape=[wide, jax.ShapeDtypeStruct((t, DH_MAIN), F32), jax.ShapeDtypeStruct((1, LANE), F32)],
        compiler_params=_cp("arbitrary"))(dm, oa, h, norm_w)


def _dn_scan_bwd(q, k, w, qk, bg, do, *, name):
    t = q.shape[0]
    rows = SCAN_ROWS
    per = rows // CHUNK
    n = t // rows

    def body(q_ref, k_ref, w_ref, qk_ref, bg_ref, do_ref, dvn_ref, ds_ref, dstate):
        @pl.when(pl.program_id(0) == 0)
        def _():
            dstate[...] = jnp.zeros_like(dstate)

        heads = range(A_HEADS)
        sl = lambda hd: slice(hd * LANE, (hd + 1) * LANE)
        ds_cur = [dstate[hd] for hd in heads]
        for c in reversed(range(per)):
            rs = slice(c * CHUNK, (c + 1) * CHUNK)
            bg_v = bg_ref[rs, :]
            gcols = [_chunk_gates(bg_v, None, hd)[1] for hd in heads]
            glasts = [gc[CHUNK - 1:CHUNK, :] for gc in gcols]
            for hd in heads:
                ds_ref[c, hd] = ds_cur[hd].astype(BF16)
            pdo = [_dot_tn(qk_ref[rs, hd * CHUNK:(hd + 1) * CHUNK], do_ref[rs, sl(hd)]) for hd in heads]
            qdo = [_dot_tn(q_ref[rs, sl(hd)] * jnp.exp(gcols[hd]), do_ref[rs, sl(hd)]) for hd in heads]
            dvns = [pdo[hd] + _dot(k_ref[rs, sl(hd)] * jnp.exp(glasts[hd] - gcols[hd]), ds_cur[hd]) for hd in heads]
            ds_cur = [qdo[hd] + jnp.exp(glasts[hd]) * ds_cur[hd] - _dot_tn(w_ref[rs, sl(hd)], dvns[hd])
                      for hd in heads]
            for hd in heads:
                dvn_ref[rs, sl(hd)] = dvns[hd]
        for hd in heads:
            dstate[hd] = ds_cur[hd]

    blk = pl.BlockSpec((rows, A_WIDTH), lambda i: (n - 1 - i, 0))
    return pl.pallas_call(
        body, name=name, grid=(n,),
        in_specs=[blk, blk, blk, pl.BlockSpec((rows, A_HEADS * CHUNK), lambda i: (n - 1 - i, 0)),
                  pl.BlockSpec((rows, LANE), lambda i: (n - 1 - i, 0)), blk],
        out_specs=[blk, pl.BlockSpec((per, A_HEADS, LANE, LANE), lambda i: (n - 1 - i, 0, 0, 0))],
        out_shape=[jax.ShapeDtypeStruct((t, A_WIDTH), F32),
                   jax.ShapeDtypeStruct((t // CHUNK, A_HEADS, LANE, LANE), BF16)],
        scratch_shapes=[pltpu.VMEM((A_HEADS, LANE, LANE), F32)],
        compiler_params=_cp("arbitrary"))(q, k, w, qk, bg, do)


def _dn_chunk_bwd(q, k, v, vn, tmat, qk, bg, bgt, s_all, ds_all, dvn, do, *, name):
    t = q.shape[0]
    rows = WY_ROWS
    per = rows // CHUNK

    def body(q_ref, k_ref, v_ref, vn_ref, tm_ref, qk_ref, bg_ref, bgt_ref, s_ref, ds_ref, dvn_ref, do_ref,
             dq_ref, dk_ref, dv_ref, dbg_ref, dbgt_ref):
        causal, strict, _ = _chunk_masks()
        lane = lax.broadcasted_iota(jnp.int32, (CHUNK, LANE), 1)
        rowi = lax.broadcasted_iota(jnp.int32, (CHUNK, 1), 0)
        sub = lax.broadcasted_iota(jnp.int32, (SUBLANE, CHUNK), 0)
        rs = lambda c: slice(c * CHUNK, (c + 1) * CHUNK)
        sl = lambda hd: slice(hd * LANE, (hd + 1) * LANE)
        hs = lambda hd: slice(hd * CHUNK, (hd + 1) * CHUNK)
        for c0 in range(0, per, WY_GROUP):
            items = [(c, hd) for c in range(c0, c0 + WY_GROUP) for hd in range(A_HEADS)]
            at = lambda ref: [ref[rs(c), sl(hd)] for c, hd in items]
            qs, ks, vs, dos, vns, dvns = at(q_ref), at(k_ref), at(v_ref), at(do_ref), at(vn_ref), at(dvn_ref)
            tmhs = [tm_ref[rs(c), hs(hd)] for c, hd in items]
            ps = [qk_ref[rs(c), hs(hd)] for c, hd in items]
            gates = [_chunk_gates(bg_ref[rs(c), :], bgt_ref[:, rs(c)], hd) for c, hd in items]
            betas = [g[0] for g in gates]
            gcols = [g[1] for g in gates]
            dmats = [jnp.exp(jnp.where(causal, g[1] - g[2], NEG)) for g in gates]
            es = [jnp.exp(gc) for gc in gcols]
            glasts = [gc[CHUNK - 1:CHUNK, :] for gc in gcols]
            eks = [jnp.exp(gl - gc) for gl, gc in zip(glasts, gcols)]
            kbs = [kh * b for kh, b in zip(ks, betas)]
            vbs = [vh * b for vh, b in zip(vs, betas)]
            kbes = [kb * e for kb, e in zip(kbs, es)]

            a_s = [jnp.where(strict, _dot_nt(kb, kh) * dm, 0.0) for kb, kh, dm in zip(kbs, ks, dmats)]
            dps = [jnp.where(causal, _dot_nt(doh, vnh), 0.0) for doh, vnh in zip(dos, vns)]
            dqds = [_dot_nt(doh, s_ref[c, hd]) for doh, (c, hd) in zip(dos, items)]
            dkds = [_dot_nt(vnh, ds_ref[c, hd]) for vnh, (c, hd) in zip(vns, items)]
            dws = [-_dot_nt(dvnh, s_ref[c, hd]) for dvnh, (c, hd) in zip(dvns, items)]
            dvbs = [_dot_tn(tmh, dvnh) for tmh, dvnh in zip(tmhs, dvns)]
            dgts = [jnp.sum(s_ref[c, hd].astype(F32) * ds_ref[c, hd].astype(F32), keepdims=True) for c, hd in items]
            dts = [_dot_nt(dvnh, vb) + _dot_nt(dw, kbe) for dvnh, vb, dw, kbe in zip(dvns, vbs, dws, kbes)]
            dkbes = [_dot_tn(tmh, dw) for tmh, dw in zip(tmhs, dws)]
            xs = [_dot_nt(dt, tmh) for dt, tmh in zip(dts, tmhs)]
            das = [jnp.where(strict, -_dot_tn(tmh, x), 0.0) for tmh, x in zip(tmhs, xs)]
            dmas = [da * dm for da, dm in zip(das, dmats)]
            dmps = [dp * dm for dp, dm in zip(dps, dmats)]
            dkbs = [_dot(dma, kh) + dkbe * e for dma, kh, dkbe, e in zip(dmas, ks, dkbes, es)]
            for i, (c, hd) in enumerate(items):
                dq_ref[rs(c), sl(hd)] = _dot(dmps[i], ks[i]) + dqds[i] * es[i]
                dk_ref[rs(c), sl(hd)] = (_dot_tn(dmas[i], kbs[i]) + _dot_tn(dmps[i], qs[i]) + dkds[i] * eks[i]
                                         + dkbs[i] * betas[i])
                dv_ref[rs(c), sl(hd)] = dvbs[i] * betas[i]
            for c in range(c0, c0 + WY_GROUP):
                acc = jnp.zeros((CHUNK, LANE), F32)
                acc_t = jnp.zeros((SUBLANE, CHUNK), F32)
                for i, (ci, hd) in enumerate(items):
                    if ci != c:
                        continue
                    gmat = das[i] * a_s[i] + dps[i] * ps[i]
                    rk = jnp.sum(dkds[i] * ks[i], -1, keepdims=True) * eks[i]
                    de = (jnp.sum(dqds[i] * qs[i], -1, keepdims=True)
                          + jnp.sum(dkbes[i] * kbs[i], -1, keepdims=True))
                    dglast = jnp.sum(rk, keepdims=True) + dgts[i] * jnp.exp(glasts[i])
                    dgc = (jnp.sum(gmat, -1, keepdims=True) + de * es[i] - rk
                           + jnp.where(rowi == CHUNK - 1, dglast, 0.0))
                    dbeta = (jnp.sum(dkbs[i] * ks[i], -1, keepdims=True)
                             + jnp.sum(dvbs[i] * vs[i], -1, keepdims=True))
                    acc = acc + jnp.where(lane == hd, dbeta, 0.0) + jnp.where(lane == A_HEADS + hd, dgc, 0.0)
                    acc_t = acc_t + jnp.where(sub == A_HEADS + hd, -jnp.sum(gmat, axis=0, keepdims=True), 0.0)
                dbg_ref[rs(c), :] = acc
                dbgt_ref[:, rs(c)] = acc_t

    blk = pl.BlockSpec((rows, A_WIDTH), lambda i: (i, 0))
    half = pl.BlockSpec((rows, A_HEADS * CHUNK), lambda i: (i, 0))
    col = pl.BlockSpec((rows, LANE), lambda i: (i, 0))
    rowf = pl.BlockSpec((SUBLANE, rows), lambda i: (0, i))
    st = pl.BlockSpec((per, A_HEADS, LANE, LANE), lambda i: (i, 0, 0, 0))
    wide = jax.ShapeDtypeStruct((t, A_WIDTH), F32)
    return pl.pallas_call(
        body, name=name, grid=(t // rows,),
        in_specs=[blk, blk, blk, blk, half, half, col, rowf, st, st, blk, blk],
        out_specs=[blk, blk, blk, col, rowf],
        out_shape=[wide, wide, wide, jax.ShapeDtypeStruct((t, LANE), F32), jax.ShapeDtypeStruct((SUBLANE, t), F32)],
        compiler_params=_cp("parallel"))(q, k, v, vn, tmat, qk, bg, bgt, s_all, ds_all, dvn, do)


def _dn_pre_bwd(h, conv_w, par, dq, dk, dv, dbg, dbgt, *, tt, name):
    t = h.shape[0]
    cw = 3 * A_WIDTH
    hb = tt // SUBLANE

    def body(pre_ref, halo_ref, bgi_ref, cw_ref, par_ref, dq_ref, dk_ref, dv_ref, dbg_ref, dbgt_ref,
             dc_ref, dbgi_ref, dpar_ref):
        i = pl.program_id(0)

        @pl.when(i == 0)
        def _():
            dpar_ref[...] = jnp.zeros_like(dpar_ref)

        cur = pre_ref[...]
        before = jnp.where(i > 0, halo_ref[...], 0.0)
        c = _conv_fwd(cur, before, cw_ref[...])
        s = _silu(c)
        ds = _dsilu(c)
        for hd in range(A_HEADS):
            sl = slice(hd * LANE, (hd + 1) * LANE)
            for base, d_ref, scale in ((0, dq_ref, A_HEAD_DIM ** -0.5), (A_WIDTH, dk_ref, 1.0)):
                csl = slice(base + hd * LANE, base + (hd + 1) * LANE)
                tq = s[:, base + hd * LANE:base + (hd + 1) * LANE]
                dy = d_ref[:, sl]
                rq = lax.rsqrt(jnp.sum(tq * tq, -1, keepdims=True) + L2_EPS)
                dtq = scale * (rq * dy - tq * (rq * rq * rq) * jnp.sum(dy * tq, -1, keepdims=True))
                dc_ref[:, csl] = dtq * ds[:, base + hd * LANE:base + (hd + 1) * LANE]
        dc_ref[:, 2 * A_WIDTH:] = dv_ref[...] * ds[:, 2 * A_WIDTH:]
        raw = bgi_ref[...]
        lane = lax.broadcasted_iota(jnp.int32, raw.shape, 1)
        is_b = lane < A_HEADS
        is_a = (lane >= A_HEADS) & (lane < 2 * A_HEADS)
        rows_t = jnp.concatenate([dbgt_ref[...], jnp.zeros((LANE - SUBLANE, tt), F32)], axis=0)
        dbg_v = dbg_ref[...] + jnp.where(is_a, jnp.transpose(rows_t), 0.0)
        dbg_v = jnp.where(is_a, _dot_hi(_chunk_tri(tt, lower=False), jnp.where(is_a, dbg_v, 0.0)), dbg_v)
        beta = _sigmoid(raw)
        z = raw + par_ref[1:2, :]
        neg_ea = -jnp.exp(par_ref[0:1, :])
        g = neg_ea * _softplus(z)
        da = dbg_v * neg_ea * _sigmoid(z)
        dbgi_ref[...] = jnp.where(is_b, dbg_v * beta * (1.0 - beta), jnp.where(is_a, da, 0.0))
        dpar_ref[0:1, :] += jnp.sum(jnp.where(is_a, dbg_v * g, 0.0), axis=0, keepdims=True)
        dpar_ref[1:2, :] += jnp.sum(jnp.where(is_a, da, 0.0), axis=0, keepdims=True)

    wide = pl.BlockSpec((tt, A_WIDTH), lambda i: (i, 0))
    return pl.pallas_call(
        body, name=name, grid=(t // tt,),
        in_specs=[pl.BlockSpec((tt, cw), lambda i: (i, 0)),
                  pl.BlockSpec((SUBLANE, cw), lambda i: (jnp.maximum(i * hb - 1, 0), 0)),
                  pl.BlockSpec((tt, LANE), lambda i: (i, C_BG // LANE)),
                  pl.BlockSpec((CONV_K, cw), lambda i: (0, 0)),
                  pl.BlockSpec((SUBLANE, LANE), lambda i: (0, 0)),
                  wide, wide, wide, pl.BlockSpec((tt, LANE), lambda i: (i, 0)),
                  pl.BlockSpec((SUBLANE, tt), lambda i: (0, i))],
        out_specs=[pl.BlockSpec((tt, cw), lambda i: (i, 0)), pl.BlockSpec((tt, LANE), lambda i: (i, 0)),
                   pl.BlockSpec((SUBLANE, LANE), lambda i: (0, 0))],
        out_shape=[jax.ShapeDtypeStruct((t, cw), F32), jax.ShapeDtypeStruct((t, LANE), F32),
                   jax.ShapeDtypeStruct((SUBLANE, LANE), F32)],
        compiler_params=_cp("arbitrary"))(h, h, h, conv_w, par, dq, dk, dv, dbg, dbgt)


def _conv_bwd(dc, h, conv_w, dh, *, tt, name):
    t = dc.shape[0]
    cw = 3 * A_WIDTH
    hb = tt // SUBLANE
    nb = t // tt

    def body(dc_ref, after_ref, pre_ref, before_ref, cw_ref, dh_in_ref, dpre_ref, dcw_ref):
        i = pl.program_id(0)

        @pl.when(i == 0)
        def _():
            dcw_ref[...] = jnp.zeros_like(dcw_ref)

        dcv = dc_ref[...]
        after = jnp.where(i < nb - 1, after_ref[...], 0.0)
        cur = pre_ref[...]
        before = jnp.where(i > 0, before_ref[...], 0.0)
        w = cw_ref[...]
        acc = dcv * w[CONV_K - 1:CONV_K, :]
        dcw_ref[CONV_K - 1:CONV_K, :] += jnp.sum(dcv * cur, axis=0, keepdims=True)
        for s in range(1, CONV_K):
            j = CONV_K - 1 - s
            acc = acc + _shift_up(dcv, after, s) * w[j:j + 1, :]
            dcw_ref[j:j + 1, :] += jnp.sum(dcv * _shift_down(cur, before, s), axis=0, keepdims=True)
        dpre_ref[...] = acc

    return pl.pallas_call(
        body, name=name, grid=(nb,),
        in_specs=[pl.BlockSpec((tt, cw), lambda i: (i, 0)),
                  pl.BlockSpec((SUBLANE, cw), lambda i: (jnp.minimum((i + 1) * hb, t // SUBLANE - 1), 0)),
                  pl.BlockSpec((tt, cw), lambda i: (i, 0)),
                  pl.BlockSpec((SUBLANE, cw), lambda i: (jnp.maximum(i * hb - 1, 0), 0)),
                  pl.BlockSpec((CONV_K, cw), lambda i: (0, 0)), _ANY],
        out_specs=[pl.BlockSpec((tt, cw), lambda i: (i, 0)), pl.BlockSpec((SUBLANE, cw), lambda i: (0, 0))],
        out_shape=[jax.ShapeDtypeStruct(dh.shape, F32), jax.ShapeDtypeStruct((SUBLANE, cw), F32)],
        input_output_aliases={5: 0},
        compiler_params=_cp("arbitrary"))(dc, dc, h, h, conv_w, dh)


def _swa_bwd(h, dm, sinks_b, dh, *, name, carry=None):
    t = h.shape[0]
    qspec, cur, prev = _swa_specs()
    c_ins, c_in_specs, c_out_specs, c_outs, c_scratch = _carry_specs(carry)

    def body(*refs):
        (q_ref, kc_ref, kp_ref, vc_ref, vp_ref, zb_ref, dy_ref, sk_ref, dh_in_ref,
         dqz_ref, dk_ref, dv_ref, dsk_ref) = _carried(carry, refs, 9, 4, t // BLOCK)
        n_blk = pl.program_id(0)

        @pl.when(n_blk == 0)
        def _():
            dk_ref[...] = jnp.zeros_like(dk_ref)
            dv_ref[...] = jnp.zeros_like(dv_ref)
            dsk_ref[...] = jnp.zeros_like(dsk_ref)

        kband = jnp.concatenate([kp_ref[...], kc_ref[...]], axis=0)
        vband = jnp.concatenate([vp_ref[...], vc_ref[...]], axis=0)
        scale = B_HEAD_DIM ** -0.5
        hks = range(B_KV_HEADS)
        ksl = lambda hk: slice(hk * B_HEAD_DIM, (hk + 1) * B_HEAD_DIM)
        groups = _swa_group_probs(q_ref, sk_ref, kband, vband, _swa_neg_dist(n_blk))
        zbs = [_stack_heads(zb_ref, hk) for hk in hks]
        dys = [_stack_heads(dy_ref, hk) for hk in hks]
        dos = [dys[hk] * _silu(zbs[hk]) for hk in hks]
        deltas = [jnp.sum(dos[hk] * groups[hk][3], -1, keepdims=True) for hk in hks]
        dss = [groups[hk][1] * (_dot_nt(dos[hk], vband[:, ksl(hk)]) - deltas[hk]) for hk in hks]
        dqs = [_dot(dss[hk], kband[:, ksl(hk)]) * scale for hk in hks]
        dk_acc = [_dot_tn(dss[hk], groups[hk][0]) for hk in hks]
        dv_acc = [_dot_tn(groups[hk][1], dos[hk]) for hk in hks]
        for hk in hks:
            dzb = dys[hk] * groups[hk][3] * _dsilu(zbs[hk])
            dsink = groups[hk][2] * deltas[hk]
            for g in range(B_GROUP):
                hq = hk * B_GROUP + g
                rows = slice(g * BLOCK, (g + 1) * BLOCK)
                qsl = slice(hq * B_HEAD_DIM, (hq + 1) * B_HEAD_DIM)
                dqz_ref[:, qsl] = dqs[hk][rows]
                dqz_ref[:, B_WIDTH + hq * B_HEAD_DIM:B_WIDTH + (hq + 1) * B_HEAD_DIM] = dzb[rows]
                dsk_ref[hq:hq + 1, :] += -jnp.sum(dsink[rows], keepdims=True)
        dkb = jnp.concatenate(dk_acc, axis=1)
        dvb = jnp.concatenate(dv_acc, axis=1)
        at_cur = pl.ds(pl.multiple_of(n_blk * BLOCK, BLOCK), BLOCK)
        at_prev = pl.ds(pl.multiple_of(jnp.maximum(n_blk - 1, 0) * BLOCK, BLOCK), BLOCK)
        dk_ref[at_prev, :] += dkb[:BLOCK]
        dv_ref[at_prev, :] += dvb[:BLOCK]
        dk_ref[at_cur, :] += dkb[BLOCK:]
        dv_ref[at_cur, :] += dvb[BLOCK:]

    narrow = jax.ShapeDtypeStruct((t, B_KV_WIDTH), F32)
    res = lambda a, b: pl.BlockSpec((a, b), lambda i: (0, 0))
    outs = pl.pallas_call(
        body, name=name, grid=(t // BLOCK,),
        in_specs=[qspec(C_QB), cur(C_KB), prev(C_KB), cur(C_VB), prev(C_VB), qspec(C_ZB),
                  pl.BlockSpec((BLOCK, B_WIDTH), lambda i: (i, 1)), res(B_Q_HEADS, LANE), _ANY] + c_in_specs,
        out_specs=[pl.BlockSpec((BLOCK, 2 * B_WIDTH), lambda i: (i, C_QB // (2 * B_WIDTH))),
                   res(t, B_KV_WIDTH), res(t, B_KV_WIDTH), res(B_Q_HEADS, LANE)] + c_out_specs,
        out_shape=[jax.ShapeDtypeStruct(dh.shape, F32), narrow, narrow,
                   jax.ShapeDtypeStruct((B_Q_HEADS, LANE), F32)] + c_outs,
        scratch_shapes=c_scratch,
        input_output_aliases={8: 0},
        compiler_params=_cp("arbitrary"))(h, h, h, h, h, h, dm, sinks_b, dh, *c_ins)
    return outs[:4], outs[4:]


def _matmul_tn(a, b, *, tk, tm, name):
    t, m = a.shape
    n = b.shape[1]

    def body(a_ref, b_ref, o_ref):
        @pl.when(pl.program_id(1) == 0)
        def _():
            o_ref[...] = jnp.zeros_like(o_ref)

        o_ref[...] += _dot_tn(a_ref[...], b_ref[...])

    return pl.pallas_call(
        body, name=name, grid=(m // tm, t // tk),
        in_specs=[pl.BlockSpec((tk, tm), lambda j, kk: (kk, j)), pl.BlockSpec((tk, n), lambda j, kk: (kk, 0))],
        out_specs=pl.BlockSpec((tm, n), lambda j, kk: (j, 0)),
        out_shape=jax.ShapeDtypeStruct((m, n), F32),
        compiler_params=_cp("parallel", "arbitrary"))(a, b)


def _in_proj_dx(dh_main, dh_tail, wt, dr, *, tm, name, carry=None):
    t, n_main = dh_main.shape
    n_tail = dh_tail.shape[1]
    c_ins, c_in_specs, c_out_specs, c_outs, c_scratch = _carry_specs(carry)

    def body(*refs):
        a_ref, t_ref, wa_ref, wt_ref, r_ref, o_ref = _carried(carry, refs, 5, 1, t // tm)
        o_ref[...] = _dot(a_ref[...], wa_ref[...]) + _dot(t_ref[...], wt_ref[...]) + DEEPNORM_ALPHA * r_ref[...]

    row = lambda w: pl.BlockSpec((tm, w), lambda i: (i, 0))
    outs = pl.pallas_call(
        body, name=name, grid=(t // tm,),
        in_specs=[row(n_main), row(n_tail), pl.BlockSpec((n_main, D_MODEL), lambda i: (0, 0)),
                  pl.BlockSpec((n_tail, D_MODEL), lambda i: (n_main // n_tail, 0)), row(D_MODEL)] + c_in_specs,
        out_specs=[row(D_MODEL)] + c_out_specs,
        out_shape=[jax.ShapeDtypeStruct((t, D_MODEL), F32)] + c_outs,
        scratch_shapes=c_scratch,
        compiler_params=_cp("arbitrary"))(dh_main, dh_tail, wt, wt, dr, *c_ins)
    return outs[0], outs[1:]


def _layer_bwd(dxn, res, wt, conv_w, par, sinks_b, norm_w, w_out_bf, ln_g, l, carry=None, carry_dx=None):
    w_out_bf = res["w_out"]
    dr, dm, dw_out, dln_g, dln_b = _ln_out_bwd(dxn, res["r"], res["mixed"], ln_g, w_out_bf, tm=256, name=f"ln_out_bwd_{l}")
    h = res["h"]
    do, dh, dnw = _dn_post_bwd(dm, res["oa"], h, norm_w, tm=512, name=f"dn_post_bwd_{l}")
    dvn, ds_all = _dn_scan_bwd(res["q"], res["k"], res["w"], res["qk"], res["bg"], do, name=f"dn_scan_bwd_{l}")
    dq, dk, dv, dbg, dbgt = _dn_chunk_bwd(res["q"], res["k"], res["v"], res["vn"], res["tmat"], res["qk"], res["bg"],
                                          res["bgt"], res["s_all"], ds_all, dvn, do, name=f"dn_chunk_bwd_{l}")
    dc, dbgi, dpar = _dn_pre_bwd(h, conv_w, par, dq, dk, dv, dbg, dbgt, tt=512, name=f"dn_pre_bwd_{l}")
    dh, dcw = _conv_bwd(dc, h, conv_w, dh, tt=512, name=f"conv_bwd_{l}")
    (dh, dkb, dvb, dsk), carried = _swa_bwd(h, dm, sinks_b, dh, name=f"swa_bwd_{l}", carry=carry)
    dh_tail = jnp.concatenate([dkb, dvb, dbgi], axis=1)
    dwt_main = _matmul_tn(dh, res["x"], tk=512, tm=DH_MAIN, name=f"in_proj_dw_{l}")
    dwt_tail = _matmul_tn(dh_tail, res["x"], tk=512, tm=P_COLS - DH_MAIN, name=f"in_proj_dw_tail_{l}")
    grads = dict(w_in=(dwt_main, dwt_tail), conv_w=dcw[:CONV_K], a_log=dpar[0, A_HEADS:2 * A_HEADS],
                 dt_bias=dpar[1, A_HEADS:2 * A_HEADS], norm_w=dnw[0], sinks=dsk[:, 0], w_out=dw_out,
                 ln_g=dln_g[0], ln_b=dln_b[0])
    dx, carried_dx = _in_proj_dx(dh, dh_tail, wt, dr, tm=512, name=f"in_proj_dx_{l}",
                                 carry=None if carry_dx is None else carry_dx(grads))
    return dx, grads, carried, carried_dx


def _layer_args(wt, conv_w, a_log, dt_bias, sinks, norm_w, w_out_bf):
    return (wt, conv_w, _gate_params(a_log, dt_bias), jnp.broadcast_to(sinks[:, None], (B_Q_HEADS, LANE)),
            norm_w[None], w_out_bf)


def _local_step(x, target, args0, args1, ln_g, ln_b, gathers=None, reduce1=None, reduce0=None):
    assert DEPTH == 2
    x1, res0, got = _layer_fwd(x, *args0, ln_g[0][None], ln_b[0][None], 0, carries=gathers)
    if gathers is not None:
        args1 = args1(got)
    (dx, loss_tile), res1, _ = _layer_fwd(x1, *args1, ln_g[1][None], ln_b[1][None], 1, target=target)
    dx, grads1, _, _ = _layer_bwd(dx, res1, *args1, ln_g[1][None], 1)
    carry = None if reduce1 is None else reduce1(grads1)
    carry_dx = None if reduce0 is None else (lambda grads0: reduce0(grads0, grads1, loss_tile))
    dx, grads0, landed1, landed0 = _layer_bwd(dx, res0, *args0, ln_g[0][None], 0, carry=carry, carry_dx=carry_dx)
    return loss_tile, dx, [grads0, grads1], landed1, landed0


_ANY = pl.BlockSpec(memory_space=pl.ANY)
_MESH = pl.DeviceIdType.MESH


HALF = D_MODEL // 2


class _Exchange:
    def __init__(self, ins, outs, n_remote, n_local, plan):
        self.ins, self.outs, self.n_remote, self.n_local, self.plan = tuple(ins), tuple(outs), n_remote, n_local, plan

    def scratch(self):
        return [pltpu.SemaphoreType.DMA((self.n_remote,)), pltpu.SemaphoreType.DMA((self.n_remote,)),
                pltpu.SemaphoreType.DMA((max(self.n_local, 1),))]

    def _copies(self, in_refs, out_refs, sems, arriving):
        send_sems, recv_sems, local_sems = sems
        local, sends, recvs = self.plan(in_refs, out_refs)
        loc = [pltpu.make_async_copy(s, d, local_sems.at[i]) for i, (s, d) in enumerate(local)]
        rem = [pltpu.make_async_remote_copy(src_ref=s, dst_ref=recvs[i] if arriving else d, send_sem=send_sems.at[i],
                                            recv_sem=recv_sems.at[i], device_id=peer, device_id_type=_MESH)
               for i, (s, d, peer) in enumerate(sends)]
        return loc, rem

    def start(self, in_refs, out_refs, sems):
        loc, rem = self._copies(in_refs, out_refs, sems, arriving=False)
        for cp in loc + rem:
            cp.start()

    def finish(self, in_refs, out_refs, sems):
        loc, rem = self._copies(in_refs, out_refs, sems, arriving=True)
        for cp in rem:
            cp.wait_recv()
        for cp in rem:
            cp.wait_send()
        for cp in loc:
            cp.wait()


def _run_exchange(ex, *, name):
    n_in, n_out = len(ex.ins), len(ex.outs)

    def body(*refs):
        parts = refs[:n_in], refs[n_in:n_in + n_out], refs[n_in + n_out:]
        ex.start(*parts)
        ex.finish(*parts)

    return pl.pallas_call(body, name=name, in_specs=[_ANY] * n_in, out_specs=[_ANY] * n_out, out_shape=list(ex.outs),
                          scratch_shapes=ex.scratch())(*ex.ins)


def _place():
    x, y, c = lax.axis_index("x"), lax.axis_index("y"), lax.axis_index("c")
    return x, y, c, [(1 - x, y), (x, 1 - y), (1 - x, 1 - y)]


def _gather_exchange(arrays):
    n = len(arrays)

    def plan(src, dst):
        x, y, c, chips = _place()
        me = 2 * x + y
        local = [(src[k], dst[k].at[me]) for k in range(n)]
        sends = [(src[k], dst[k].at[me], (px, py, c)) for k in range(n) for px, py in chips]
        recvs = [dst[k].at[2 * px + py] for k in range(n) for px, py in chips]
        return local, sends, recvs

    return _Exchange(arrays, [jax.ShapeDtypeStruct((N_SHARD,) + a.shape, a.dtype) for a in arrays], 3 * n, n, plan)


def _gather_two_level(pack, conv_w, *, name):
    rows = pack.shape[0]
    part_rows = rows // 2

    def body(pack_ref, conv_ref, land_ref, conv_land_ref, send1, recv1, send2, recv2, csend, crecv, local_sems):
        x, y, c, chips = _place()
        me = 2 * x + y
        sibling = (x, y, 1 - c)
        part = lambda core: pl.ds(pl.multiple_of(core * part_rows, 16), part_rows)
        remote = lambda src, dst, ss, rs, to: pltpu.make_async_remote_copy(
            src_ref=src, dst_ref=dst, send_sem=ss, recv_sem=rs, device_id=to, device_id_type=_MESH)
        local = [pltpu.make_async_copy(pack_ref, land_ref.at[me], local_sems.at[0]),
                 pltpu.make_async_copy(conv_ref, conv_land_ref.at[me], local_sems.at[1])]
        for cp in local:
            cp.start()
        first = [remote(pack_ref.at[part(c)], land_ref.at[me, part(c)], send1.at[j], recv1.at[j], (px, py, c))
                 for j, (px, py) in enumerate(chips)]
        convs = [remote(conv_ref, conv_land_ref.at[me], csend.at[j], crecv.at[j], (px, py, c))
                 for j, (px, py) in enumerate(chips)]
        for cp in first + convs:
            cp.start()
        passed = []
        for j, (px, py) in enumerate(chips):
            slot = 2 * px + py
            remote(pack_ref.at[part(c)], land_ref.at[slot, part(c)], send1.at[j], recv1.at[j], (px, py, c)).wait_recv()
            cp = remote(land_ref.at[slot, part(c)], land_ref.at[slot, part(c)], send2.at[j], recv2.at[j], sibling)
            cp.start()
            passed.append(cp)
        for j, (px, py) in enumerate(chips):
            slot = 2 * px + py
            remote(land_ref.at[slot, part(1 - c)], land_ref.at[slot, part(1 - c)], send2.at[j], recv2.at[j],
                   sibling).wait_recv()
            remote(conv_ref, conv_land_ref.at[slot], csend.at[j], crecv.at[j], (px, py, c)).wait_recv()
        for cp in first + convs + passed:
            cp.wait_send()
        for cp in local:
            cp.wait()

    sems = [pltpu.SemaphoreType.DMA((3,))] * 6 + [pltpu.SemaphoreType.DMA((2,))]
    return pl.pallas_call(
        body, name=name, in_specs=[_ANY, _ANY], out_specs=[_ANY, _ANY],
        out_shape=[jax.ShapeDtypeStruct((N_SHARD,) + pack.shape, pack.dtype),
                   jax.ShapeDtypeStruct((N_SHARD,) + conv_w.shape, conv_w.dtype)],
        scratch_shapes=sems)(pack, conv_w)


def _half(core):
    return pl.ds(pl.multiple_of(core * HALF, HALF), HALF)


def _reduce_scatter_exchange(g, small=None):
    ins = [g] if small is None else [g, small]
    outs = [jax.ShapeDtypeStruct((7,) + g.shape[1:2] + (HALF,), g.dtype)]
    if small is not None:
        outs.append(jax.ShapeDtypeStruct((8,) + small.shape, small.dtype))

    def plan(src, dst):
        x, y, c, chips = _place()
        me = 2 * x + y
        peers = [(px, py, c if t == 0 else 1 - c) for px, py in chips for t in (0, 1)] + [(x, y, 1 - c)]
        sends = [(src[0].at[2 * px + py, :, _half(pc)], dst[0].at[k], (px, py, pc)) for k, (px, py, pc) in enumerate(peers)]
        recvs = [dst[0].at[k] for k in range(7)]
        local = []
        if small is not None:
            mine = 4 * x + 2 * y + c
            local = [(src[1], dst[1].at[mine])]
            sends += [(src[1], dst[1].at[mine], peer) for peer in peers]
            recvs += [dst[1].at[4 * px + 2 * py + pc] for px, py, pc in peers]
        return local, sends, recvs

    return _Exchange(ins, outs, 7 * len(ins), len(ins) - 1, plan)


def _pair_window_exchange(g):
    def plan(src, dst):
        x, y, c, _ = _place()
        return [], [(src[0].at[:, :, _half(1 - c)], dst[0], (x, y, 1 - c))], [dst[0]]

    return _Exchange([g], [jax.ShapeDtypeStruct(g.shape[:2] + (HALF,), g.dtype)], 1, 0, plan)


def _chip_scatter_exchange(p, small):
    def plan(src, dst):
        x, y, c, chips = _place()
        mine = 4 * x + 2 * y + c
        peers = [(px, py, c if t == 0 else 1 - c) for px, py in chips for t in (0, 1)] + [(x, y, 1 - c)]
        sends = [(src[0].at[2 * px + py], dst[0].at[j], (px, py, c)) for j, (px, py) in enumerate(chips)]
        recvs = [dst[0].at[j] for j in range(3)]
        sends += [(src[1], dst[1].at[mine], peer) for peer in peers]
        recvs += [dst[1].at[4 * px + 2 * py + pc] for px, py, pc in peers]
        return [(src[1], dst[1].at[mine])], sends, recvs

    outs = [jax.ShapeDtypeStruct((3,) + p.shape[1:], p.dtype), jax.ShapeDtypeStruct((8,) + small.shape, small.dtype)]
    return _Exchange([p, small], outs, 10, 1, plan)


def _share_exchange(arrays):
    n = len(arrays)

    def plan(src, dst):
        x, y, c, _ = _place()
        return [], [(src[k], dst[k], (x, y, 1 - c)) for k in range(n)], [dst[k] for k in range(n)]

    return _Exchange(arrays, [jax.ShapeDtypeStruct(a.shape, a.dtype) for a in arrays], n, 0, plan)


def _sum_scatter(g, land, me, core, *, tc, name):
    rows = g.shape[1]
    per = HALF // tc

    def body(where_ref, g_ref, land_ref, o_ref):
        acc = g_ref[...]
        for k in range(7):
            acc = acc + land_ref[k].astype(F32)
        o_ref[...] = acc

    return pl.pallas_call(
        body, name=name, out_shape=jax.ShapeDtypeStruct((rows, HALF), F32), compiler_params=_cp("parallel"),
        grid_spec=pltpu.PrefetchScalarGridSpec(
            num_scalar_prefetch=1, grid=(per,),
            in_specs=[pl.BlockSpec((None, rows, tc), lambda i, w: (w[0], 0, w[1] * per + i)),
                      pl.BlockSpec((7, rows, tc), lambda i, w: (0, 0, i))],
            out_specs=pl.BlockSpec((rows, tc), lambda i, w: (0, i))))(
        jnp.stack([me, core]).astype(jnp.int32), g, land)


def _pair_add(g, land, core, *, name):
    n, rows, _ = g.shape

    def body(core_ref, g_ref, land_ref, o_ref):
        o_ref[...] = (g_ref[...].astype(F32) + land_ref[...].astype(F32)).astype(o_ref.dtype)

    blk = pl.BlockSpec((1, rows, HALF), lambda i, w: (i, 0, 0))
    return pl.pallas_call(
        body, name=name, out_shape=jax.ShapeDtypeStruct((n, rows, HALF), g.dtype), compiler_params=_cp("parallel"),
        grid_spec=pltpu.PrefetchScalarGridSpec(
            num_scalar_prefetch=1, grid=(n,),
            in_specs=[pl.BlockSpec((1, rows, HALF), lambda i, w: (i, 0, w[0])), blk], out_specs=blk))(
        jnp.reshape(core, (1,)).astype(jnp.int32), g, land)


def _sum_chips(p, land, me, *, tc, name):
    rows = p.shape[1]

    def body(me_ref, p_ref, land_ref, o_ref):
        acc = p_ref[...].astype(F32)
        for k in range(3):
            acc = acc + land_ref[k].astype(F32)
        o_ref[...] = acc

    return pl.pallas_call(
        body, name=name, out_shape=jax.ShapeDtypeStruct((rows, HALF), F32), compiler_params=_cp("parallel"),
        grid_spec=pltpu.PrefetchScalarGridSpec(
            num_scalar_prefetch=1, grid=(HALF // tc,),
            in_specs=[pl.BlockSpec((None, rows, tc), lambda i, w: (w[0], 0, i)),
                      pl.BlockSpec((3, rows, tc), lambda i, w: (0, 0, i))],
            out_specs=pl.BlockSpec((rows, tc), lambda i, w: (0, i))))(
        jnp.reshape(me, (1,)).astype(jnp.int32), p, land)


def _sum_slots(a, *, name):
    n = a.shape[0]

    def body(a_ref, o_ref):
        acc = a_ref[0]
        for k in range(1, n):
            acc = acc + a_ref[k]
        o_ref[...] = acc

    return pl.pallas_call(body, name=name, out_shape=jax.ShapeDtypeStruct(a.shape[1:], a.dtype))(a)


def _elementwise(fn, ins, n_out, block, *, name):
    shape = ins[0].shape
    grid = tuple(s // b for s, b in zip(shape, block))
    n_in = len(ins)

    def body(*refs):
        outs = fn(*[r[...] for r in refs[:n_in]])
        for o_ref, val in zip(refs[n_in:], outs):
            o_ref[...] = val

    spec = pl.BlockSpec(block, lambda i, j, k: (i, j, k))
    return pl.pallas_call(body, name=name, grid=grid, in_specs=[spec] * n_in, out_specs=[spec] * n_out,
                          out_shape=[jax.ShapeDtypeStruct(shape, F32)] * n_out,
                          compiler_params=_cp(*["parallel"] * 3))(*ins)


def _adamw_math(w, g, m, v):
    mn = ADAM_B1 * m + (1.0 - ADAM_B1) * g
    vn = ADAM_B2 * v + (1.0 - ADAM_B2) * (g * g)
    m_hat = mn / (1.0 - ADAM_B1 ** ADAM_STEP)
    v_hat = vn / (1.0 - ADAM_B2 ** ADAM_STEP)
    return -ADAM_LR * (m_hat / (jnp.sqrt(v_hat) + ADAM_EPS) + ADAM_WD * w), mn, vn


def _adamw(w, g, m, v, block, *, name):
    return _elementwise(_adamw_math, [w, g, m, v], 3, block, name=name)


def _interleave_layers(layers, *, tc, name):
    rows, cols = layers[0].shape
    n = len(layers)

    def body(*refs):
        for l in range(n):
            refs[n][:, l, :] = refs[l][...]

    return pl.pallas_call(body, name=name, grid=(cols // tc,),
                          in_specs=[pl.BlockSpec((rows, tc), lambda i: (0, i))] * n,
                          out_specs=pl.BlockSpec((rows, n, tc), lambda i: (0, 0, i)),
                          out_shape=jax.ShapeDtypeStruct((rows, n, cols), layers[0].dtype),
                          compiler_params=_cp("parallel"))(*layers)


def _adamw_small(ws, gs, ms, vs, *, name):
    n = len(ws)

    def body(*refs):
        w, g, m, v, outs = refs[:n], refs[n:2 * n], refs[2 * n:3 * n], refs[3 * n:4 * n], refs[4 * n:]
        for k in range(n):
            for slot, val in enumerate(_adamw_math(w[k][...], g[k][...], m[k][...], v[k][...])):
                outs[slot * n + k][...] = val

    outs = pl.pallas_call(body, name=name, out_shape=[jax.ShapeDtypeStruct(a.shape, F32) for a in ws] * 3)(
        *ws, *gs, *ms, *vs)
    return outs[:n], outs[n:2 * n], outs[2 * n:]


def _to_kernel_order(wt):
    gates = jnp.pad(wt[2048:2056], ((0, LANE - 2 * A_HEADS), (0, 0)))
    return jnp.concatenate([wt[0:2048], wt[2056:2568], wt[2824:3336], wt[2568:2696], wt[2696:2824], gates], axis=0)


def _from_kernel_order(main, tail):
    return jnp.concatenate([main[0:2048], tail[C_BG - DH_MAIN:C_BG - DH_MAIN + 2 * A_HEADS],
                            main[C_QB:C_QB + B_WIDTH], tail[0:B_KV_WIDTH], tail[B_KV_WIDTH:2 * B_KV_WIDTH],
                            main[C_ZB:C_ZB + B_WIDTH]], axis=0)


def _gate_params(a_log, dt_bias):
    return jnp.pad(jnp.stack([a_log, dt_bias]), ((0, SUBLANE - 2), (A_HEADS, LANE - 2 * A_HEADS)))


SMALL = ("conv_w", "a_log", "dt_bias", "norm_w", "sinks", "ln_g", "ln_b")


def _pack(parts, cols):
    flat = jnp.concatenate([p.reshape(-1) for p in parts])
    rows = -(-flat.shape[0] // cols)
    return jnp.pad(flat, (0, rows * cols - flat.shape[0])).reshape(rows, cols)


def _unpack(packed, shapes):
    flat = packed.reshape(-1)
    out, at = [], 0
    for s in shapes:
        n = math.prod(s)
        out.append(flat[at:at + n].reshape(s))
        at += n
    return out


def kernel(x, w_in, conv_w, a_log, dt_bias, norm_w, sinks, w_out, ln_g, ln_b, loss_target, m_w_in, m_conv_w, m_a_log, m_dt_bias, m_norm_w, m_sinks, m_w_out, m_ln_g, m_ln_b, v_w_in, v_conv_w, v_a_log, v_dt_bias, v_norm_w, v_sinks, v_w_out, v_ln_g, v_ln_b):
    xi, yi, ci = lax.axis_index("x"), lax.axis_index("y"), lax.axis_index("c")
    me = 2 * xi + yi

    to_t = lambda a: jnp.transpose(a, (2, 0, 1))
    from_t = lambda a: jnp.transpose(a, (1, 2, 0))

    wt_shard = to_t(w_in)

    def pack_weights(l):
        rows = jnp.pad(wt_shard[:, l], ((0, IN_PAD - IN_SHARD), (0, 0)))
        return jnp.concatenate([rows, w_out[l]], axis=0).astype(BF16)

    pack0, pack1 = pack_weights(0), pack_weights(1)
    got_in0, g_conv = _gather_two_level(pack0[:IN_PAD], conv_w, name="gather_weights_0")
    conv_full = jnp.moveaxis(g_conv, 0, 2).reshape(DEPTH, CONV_K, 3 * A_WIDTH)
    piece = IN_PAD // 3
    carriers = ("dn_pre", "dn_wy", "dn_scan")
    gathers = {nm: _gather_exchange([pack1[i * piece:(i + 1) * piece]]) for i, nm in enumerate(carriers)}
    gathers.update(in_proj=_gather_exchange([pack0[IN_PAD:]]), swa=_gather_exchange([pack1[IN_PAD:]]))
    w_in_of = lambda rows: _to_kernel_order(rows[:, :IN_SHARD].reshape(IN_COLS, D_MODEL))
    w_out_of = lambda rows: rows.reshape(D_MODEL, D_MODEL)
    args0 = _layer_args(w_in_of(got_in0), conv_full[0], a_log[0], dt_bias[0], sinks[0], norm_w[0],
                        lambda got: w_out_of(got[0]))

    def args1(got):
        rows = jnp.concatenate([got[nm][0] for nm in carriers], axis=1)
        return _layer_args(w_in_of(rows), conv_full[1], a_log[1], dt_bias[1], sinks[1], norm_w[1],
                           w_out_of(got["swa"][0]))

    def pack_grads(g):
        gin = _from_kernel_order(*g["w_in"]).reshape(N_SHARD, IN_SHARD, D_MODEL)
        gin = jnp.pad(gin, ((0, 0), (0, IN_PAD - IN_SHARD), (0, 0)))
        return jnp.concatenate([gin, g["w_out"].reshape(N_SHARD, OUT_SHARD, D_MODEL)], axis=1).astype(BF16)

    packed = {}

    def reduce1(grads1):
        packed[1] = pack_grads(grads1)
        return _reduce_scatter_exchange(packed[1])

    def reduce0(grads0, grads1, loss_tile):
        g0 = pack_grads(grads0)
        from_sibling = _run_exchange(_pair_window_exchange(g0), name="pair_reduce_0")[0]
        packed[0] = _pair_add(g0, from_sibling, ci, name="pair_add_0")
        gsmall = _pack([jnp.stack([g[nm] for g in (grads0, grads1)]) for nm in SMALL] + [loss_tile[0, 0:1]], D_MODEL)
        return _chip_scatter_exchange(packed[0], gsmall)

    _, dx, grads, landed1, (landed0, landed_small) = _local_step(
        x[0], loss_target[0], args0, args1, ln_g, ln_b, gathers=gathers, reduce1=reduce1, reduce0=reduce0)

    small_shapes = [(DEPTH,) + grads[0][nm].shape for nm in SMALL]
    halves = [_sum_chips(packed[0], landed0, me, tc=2 * LANE, name="reduce_sum_0"),
              _sum_scatter(packed[1], landed1[0], me, ci, tc=2 * LANE, name="reduce_sum_1")]
    s_small = _sum_slots(landed_small, name="reduce_sum_small")
    others = _run_exchange(_share_exchange(halves), name="pair_share")
    full = [jnp.where(ci == 0, jnp.concatenate([mine, other], axis=1), jnp.concatenate([other, mine], axis=1))
            for mine, other in zip(halves, others)]
    grad_in_layers = [f[:IN_SHARD] for f in full]
    grad_out = jnp.stack([f[IN_PAD:] for f in full])
    out_blk = (1, OUT_SHARD, D_MODEL)
    *small_grads, loss = _unpack(s_small, small_shapes + [()])
    gs = dict(zip(SMALL, small_grads))
    gs["conv_w"] = lax.dynamic_slice_in_dim(gs["conv_w"], me * CONV_SHARD, CONV_SHARD, axis=2)

    grad_in_t = _interleave_layers(grad_in_layers, tc=2 * LANE, name="grad_in_layers")
    d_in, nm_in, nv_in = (from_t(o) for o in _adamw(to_t(w_in), grad_in_t, to_t(m_w_in), to_t(v_w_in),
                                                    (IN_SHARD // 6, DEPTH, D_MODEL), name="adamw_in"))
    grad_in = from_t(grad_in_t)
    d_out, nm_out, nv_out = _adamw(w_out, grad_out, m_w_out, v_w_out, out_blk, name="adamw_out")
    ws = dict(conv_w=conv_w, a_log=a_log, dt_bias=dt_bias, norm_w=norm_w, sinks=sinks, ln_g=ln_g, ln_b=ln_b)
    ms = dict(conv_w=m_conv_w, a_log=m_a_log, dt_bias=m_dt_bias, norm_w=m_norm_w, sinks=m_sinks, ln_g=m_ln_g, ln_b=m_ln_b)
    vs = dict(conv_w=v_conv_w, a_log=v_a_log, dt_bias=v_dt_bias, norm_w=v_norm_w, sinks=v_sinks, ln_g=v_ln_g, ln_b=v_ln_b)
    d_s, nm_s, nv_s = (dict(zip(SMALL, o)) for o in _adamw_small(*[[d[nm] for nm in SMALL] for d in (ws, gs, ms, vs)],
                                                                 name="adamw_small"))

    def in_order(big_in, small, big_out):
        return (big_in, small["conv_w"], small["a_log"], small["dt_bias"], small["norm_w"], small["sinks"], big_out,
                small["ln_g"], small["ln_b"])

    return (loss, dx[None], *in_order(grad_in, gs, grad_out), *in_order(d_in, d_s, d_out),
            *in_order(nm_in, nm_s, nm_out), *in_order(nv_in, nv_s, nv_out))
```

```python
import math

import jax
import jax.numpy as jnp
from jax import lax
from jax.experimental import pallas as pl
from jax.experimental.pallas import tpu as pltpu

F32 = jnp.float32
BF16 = jnp.bfloat16
HI = lax.Precision.HIGHEST

D_MODEL = 1024
DEPTH = 2
A_HEADS = 4
A_HEAD_DIM = 128
A_WIDTH = 512
CONV_K = 4
CHUNK = 64
B_Q_HEADS = 8
B_KV_HEADS = 2
B_HEAD_DIM = 64
B_GROUP = 4
B_WIDTH = 512
B_KV_WIDTH = 128
BLOCK = 128
IN_COLS = 3336
DEEPNORM_ALPHA = (2 * DEPTH) ** 0.25
LN_EPS = 1e-5
RMS_EPS = 1e-6
L2_EPS = 1e-6
ADAM_LR = 0.001
ADAM_B1 = 0.9
ADAM_B2 = 0.999
ADAM_EPS = 1e-08
ADAM_WD = 0.01
ADAM_STEP = 10

N_SHARD = 4
IN_SHARD = IN_COLS // N_SHARD
OUT_SHARD = D_MODEL // N_SHARD
CONV_SHARD = 3 * A_WIDTH // N_SHARD
IN_PAD = -(-IN_SHARD // 96) * 96

P_COLS = 3456
C_PRE = 0
C_ZA = 1536
C_QB = 2048
C_ZB = 2560
C_KB = 3072
C_VB = 3200
C_BG = 3328
DH_MAIN = C_KB
LANE = 128
SUBLANE = 8
VMEM_LIMIT = 56 * 1024 * 1024
ALIBI = tuple(2.0 ** (-8.0 * (h + 1) / B_Q_HEADS) for h in range(B_Q_HEADS))
NEG = -1e30


def _cp(*sem):
    return pltpu.CompilerParams(dimension_semantics=sem, vmem_limit_bytes=VMEM_LIMIT)


def _dot(a, b):
    return jnp.dot(a.astype(BF16), b.astype(BF16), preferred_element_type=F32)


def _dot_nt(a, b):
    return lax.dot_general(a.astype(BF16), b.astype(BF16), (((1,), (1,)), ((), ())),
                           preferred_element_type=F32)


def _dot_tn(a, b):
    return lax.dot_general(a.astype(BF16), b.astype(BF16), (((0,), (0,)), ((), ())),
                           preferred_element_type=F32)


def _dot_hi(a, b):
    return jnp.dot(a, b, precision=HI, preferred_element_type=F32)


def _sigmoid(x):
    return jax.nn.sigmoid(x)


def _silu(x):
    return x * _sigmoid(x)


def _dsilu(x):
    s = _sigmoid(x)
    return s * (1.0 + x * (1.0 - s))


def _softplus(x):
    return jnp.maximum(x, 0.0) + jnp.log(1.0 + jnp.exp(-jnp.abs(x)))


def _shift_down(cur, before, s):
    if s == 0:
        return cur
    r = pltpu.roll(cur, s, 0)
    rb = pltpu.roll(before, s, 0)
    row = lax.broadcasted_iota(jnp.int32, before.shape, 0)
    head = jnp.where(row < s, rb, r[0:SUBLANE])
    return jnp.concatenate([head, r[SUBLANE:]], axis=0)


def _shift_up(cur, after, s):
    if s == 0:
        return cur
    n = cur.shape[0]
    r = pltpu.roll(cur, n - s, 0)
    ra = pltpu.roll(after, SUBLANE - s, 0)
    row = lax.broadcasted_iota(jnp.int32, after.shape, 0)
    tail = jnp.where(row >= SUBLANE - s, ra, r[n - SUBLANE:])
    return jnp.concatenate([r[:n - SUBLANE], tail], axis=0)


def _conv_fwd(cur, before, w):
    acc = cur * w[CONV_K - 1:CONV_K, :]
    for s in range(1, CONV_K):
        acc = acc + _shift_down(cur, before, s) * w[CONV_K - 1 - s:CONV_K - s, :]
    return acc


def _matmul_nt(a, bt, *, tm, name, carry=None):
    m, k = a.shape
    n = bt.shape[0]
    c_ins, c_in_specs, c_out_specs, c_outs, c_scratch = _carry_specs(carry)

    def body(*refs):
        a_ref, b_ref, o_ref = _carried(carry, refs, 2, 1, m // tm)
        o_ref[...] = _dot_nt(a_ref[...], b_ref[...])

    outs = pl.pallas_call(
        body, name=name, grid=(m // tm,),
        in_specs=[pl.BlockSpec((tm, k), lambda i: (i, 0)), pl.BlockSpec((n, k), lambda i: (0, 0))] + c_in_specs,
        out_specs=[pl.BlockSpec((tm, n), lambda i: (i, 0))] + c_out_specs,
        out_shape=[jax.ShapeDtypeStruct((m, n), F32)] + c_outs,
        scratch_shapes=c_scratch,
        compiler_params=_cp("arbitrary"))(a, bt, *c_ins)
    return outs[0], outs[1:]


def _dn_pre(h, conv_w, par, *, tt, name, carry=None):
    t = h.shape[0]
    cw = 3 * A_WIDTH
    hb = tt // SUBLANE

    c_ins, c_in_specs, c_out_specs, c_outs, c_scratch = _carry_specs(carry)

    def body(*refs):
        (pre_ref, halo_ref, bgi_ref, cw_ref, par_ref,
         q_ref, k_ref, v_ref, bg_ref, bgt_ref) = _carried(carry, refs, 5, 5, t // tt)
        i = pl.program_id(0)
        cur = pre_ref[...]
        before = jnp.where(i > 0, halo_ref[...], 0.0)
        s = _silu(_conv_fwd(cur, before, cw_ref[...]))
        for hd in range(A_HEADS):
            sl = slice(hd * LANE, (hd + 1) * LANE)
            tq = s[:, hd * LANE:(hd + 1) * LANE]
            q_ref[:, sl] = tq * (lax.rsqrt(jnp.sum(tq * tq, -1, keepdims=True) + L2_EPS) * (A_HEAD_DIM ** -0.5))
            tk = s[:, A_WIDTH + hd * LANE:A_WIDTH + (hd + 1) * LANE]
            k_ref[:, sl] = tk * lax.rsqrt(jnp.sum(tk * tk, -1, keepdims=True) + L2_EPS)
        v_ref[...] = s[:, 2 * A_WIDTH:]
        raw = bgi_ref[...]
        lane = lax.broadcasted_iota(jnp.int32, raw.shape, 1)
        is_a = (lane >= A_HEADS) & (lane < 2 * A_HEADS)
        g = jnp.where(is_a, -jnp.exp(par_ref[0:1, :]) * _softplus(raw + par_ref[1:2, :]), 0.0)
        gc = _dot_hi(_chunk_tri(tt, lower=True), g)
        bg = jnp.where(lane < A_HEADS, _sigmoid(raw), gc)
        bg_ref[...] = bg
        bgt_ref[...] = jnp.transpose(bg)[0:SUBLANE, :]

    wide = jax.ShapeDtypeStruct((t, A_WIDTH), F32)
    outs = pl.pallas_call(
        body, name=name, grid=(t // tt,),
        in_specs=[pl.BlockSpec((tt, cw), lambda i: (i, 0)),
                  pl.BlockSpec((SUBLANE, cw), lambda i: (jnp.maximum(i * hb - 1, 0), 0)),
                  pl.BlockSpec((tt, LANE), lambda i: (i, C_BG // LANE)),
                  pl.BlockSpec((CONV_K, cw), lambda i: (0, 0)),
                  pl.BlockSpec((SUBLANE, LANE), lambda i: (0, 0))] + c_in_specs,
        out_specs=[pl.BlockSpec((tt, A_WIDTH), lambda i: (i, 0))] * 3
        + [pl.BlockSpec((tt, LANE), lambda i: (i, 0)), pl.BlockSpec((SUBLANE, tt), lambda i: (0, i))] + c_out_specs,
        out_shape=[wide, wide, wide, jax.ShapeDtypeStruct((t, LANE), F32),
                   jax.ShapeDtypeStruct((SUBLANE, t), F32)] + c_outs,
        scratch_shapes=c_scratch,
        compiler_params=_cp("arbitrary"))(h, h, h, conv_w, par, *c_ins)
    return outs[:5], outs[5:]


def _chunk_tri(n, lower):
    r = lax.broadcasted_iota(jnp.int32, (n, n), 0)
    c = lax.broadcasted_iota(jnp.int32, (n, n), 1)
    shift = CHUNK.bit_length() - 1
    same = jnp.right_shift(r, shift) == jnp.right_shift(c, shift)
    return (same & ((c <= r) if lower else (c >= r))).astype(F32)


def _chunk_masks():
    r = lax.broadcasted_iota(jnp.int32, (CHUNK, CHUNK), 0)
    c = lax.broadcasted_iota(jnp.int32, (CHUNK, CHUNK), 1)
    return r >= c, r > c, r == c


def _split(a):
    hi = a.astype(BF16)
    return hi, (a - hi.astype(F32)).astype(BF16)


def _dot3(a, b):
    (ah, al), (bh, bl) = a, b
    d = lambda p, q: jnp.dot(p, q, preferred_element_type=F32)
    return d(ah, bh) + (d(ah, bl) + d(al, bh))


def _tri_inv_many(a_list, eye):
    d = lambda p, q: jnp.dot(p, q, preferred_element_type=F32)
    p = [(-a).astype(BF16) for a in a_list]
    tm = [eye - a for a in a_list]
    for _ in range(5):
        pf = [d(pi, pi) for pi in p]
        p = [x.astype(BF16) for x in pf]
        tm = [t + d(t.astype(BF16), pi) for t, pi in zip(tm, p)]
    ms = [_split(eye + a) for a in a_list]
    res = [eye - _dot3(m, _split(t)) for m, t in zip(ms, tm)]
    return [t + d(t.astype(BF16), r.astype(BF16)) for t, r in zip(tm, res)]


def _chunk_gates(bg_v, bgt_v, hd):
    return (bg_v[:, hd:hd + 1], bg_v[:, A_HEADS + hd:A_HEADS + hd + 1],
            None if bgt_v is None else bgt_v[A_HEADS + hd:A_HEADS + hd + 1, :])


WY_ROWS = 512
SCAN_ROWS = 512
WY_GROUP = 8


def _dn_wy(q, k, v, bg, bgt, *, name, carry=None):
    t = q.shape[0]
    rows = WY_ROWS

    c_ins, c_in_specs, c_out_specs, c_outs, c_scratch = _carry_specs(carry)

    def body(*refs):
        q_ref, k_ref, v_ref, bg_ref, bgt_ref, u_ref, w_ref, tm_ref, qk_ref = _carried(carry, refs, 5, 4, t // rows)
        causal, strict, diag = _chunk_masks()
        eye = diag.astype(F32)
        for c0 in range(0, rows // CHUNK, WY_GROUP):
            items = [(c, hd) for c in range(c0, c0 + WY_GROUP) for hd in range(A_HEADS)]
            rs = lambda c: slice(c * CHUNK, (c + 1) * CHUNK)
            sl = lambda hd: slice(hd * LANE, (hd + 1) * LANE)
            hs = lambda hd: slice(hd * CHUNK, (hd + 1) * CHUNK)
            gates = [_chunk_gates(bg_ref[rs(c), :], bgt_ref[:, rs(c)], hd) for c, hd in items]
            dms = [jnp.exp(jnp.where(causal, gcol - grow, NEG)) for _, gcol, grow in gates]
            kbs = [k_ref[rs(c), sl(hd)] * g[0] for (c, hd), g in zip(items, gates)]
            a_list = [jnp.where(strict, _dot_nt(kb, k_ref[rs(c), sl(hd)]) * dm, 0.0)
                      for (c, hd), kb, dm in zip(items, kbs, dms)]
            for (c, hd), dm in zip(items, dms):
                qk_ref[rs(c), hs(hd)] = jnp.where(
                    causal, _dot_nt(q_ref[rs(c), sl(hd)], k_ref[rs(c), sl(hd)]) * dm, 0.0)
            tms = _tri_inv_many(a_list, eye)
            for (c, hd), g, kb, tmat in zip(items, gates, kbs, tms):
                tm_ref[rs(c), hs(hd)] = tmat
                u_ref[rs(c), sl(hd)] = _dot(tmat, v_ref[rs(c), sl(hd)] * g[0])
                w_ref[rs(c), sl(hd)] = _dot(tmat, kb * jnp.exp(g[1])).astype(BF16)

    blk = pl.BlockSpec((rows, A_WIDTH), lambda i: (i, 0))
    half = pl.BlockSpec((rows, A_HEADS * CHUNK), lambda i: (i, 0))
    outs = pl.pallas_call(
        body, name=name, grid=(t // rows,),
        in_specs=[blk, blk, blk, pl.BlockSpec((rows, LANE), lambda i: (i, 0)),
                  pl.BlockSpec((SUBLANE, rows), lambda i: (0, i))] + c_in_specs,
        out_specs=[blk, blk, half, half] + c_out_specs,
        out_shape=[jax.ShapeDtypeStruct((t, A_WIDTH), F32), jax.ShapeDtypeStruct((t, A_WIDTH), BF16),
                   jax.ShapeDtypeStruct((t, A_HEADS * CHUNK), F32),
                   jax.ShapeDtypeStruct((t, A_HEADS * CHUNK), F32)] + c_outs,
        scratch_shapes=c_scratch,
        compiler_params=_cp("arbitrary"))(q, k, v, bg, bgt, *c_ins)
    return outs[:4], outs[4:]


def _dn_scan_fwd(q, k, u, w, qk, bg, *, name, carry=None):
    t = q.shape[0]
    rows = SCAN_ROWS
    per = rows // CHUNK
    c_ins, c_in_specs, c_out_specs, c_outs, c_scratch = _carry_specs(carry)

    def body(*refs):
        q_ref, k_ref, u_ref, w_ref, qk_ref, bg_ref, o_ref, vn_ref, s_ref, state = _carried(carry, refs, 6, 3, t // rows)

        @pl.when(pl.program_id(0) == 0)
        def _():
            state[...] = jnp.zeros_like(state)

        heads = range(A_HEADS)
        sl = lambda hd: slice(hd * LANE, (hd + 1) * LANE)
        s_cur = [state[hd] for hd in heads]
        for c in range(per):
            rs = slice(c * CHUNK, (c + 1) * CHUNK)
            bg_v = bg_ref[rs, :]
            gcols = [_chunk_gates(bg_v, None, hd)[1] for hd in heads]
            glasts = [gc[CHUNK - 1:CHUNK, :] for gc in gcols]
            for hd in heads:
                s_ref[c, hd] = s_cur[hd].astype(BF16)
            vns = [u_ref[rs, sl(hd)] - _dot(w_ref[rs, sl(hd)], s_cur[hd]) for hd in heads]
            qss = [_dot(q_ref[rs, sl(hd)] * jnp.exp(gcols[hd]), s_cur[hd]) for hd in heads]
            s_cur = [s_cur[hd] * jnp.exp(glasts[hd])
                     + _dot_tn(k_ref[rs, sl(hd)] * jnp.exp(glasts[hd] - gcols[hd]), vns[hd]) for hd in heads]
            for hd in heads:
                vn_ref[rs, sl(hd)] = vns[hd]
                o_ref[rs, sl(hd)] = qss[hd] + _dot(qk_ref[rs, hd * CHUNK:(hd + 1) * CHUNK], vns[hd])
        for hd in heads:
            state[hd] = s_cur[hd]

    blk = pl.BlockSpec((rows, A_WIDTH), lambda i: (i, 0))
    half = pl.BlockSpec((rows, A_HEADS * CHUNK), lambda i: (i, 0))
    wide = jax.ShapeDtypeStruct((t, A_WIDTH), F32)
    outs = pl.pallas_call(
        body, name=name, grid=(t // rows,),
        in_specs=[blk, blk, blk, blk, half, pl.BlockSpec((rows, LANE), lambda i: (i, 0))] + c_in_specs,
        out_specs=[blk, blk, pl.BlockSpec((per, A_HEADS, LANE, LANE), lambda i: (i, 0, 0, 0))] + c_out_specs,
        out_shape=[wide, wide, jax.ShapeDtypeStruct((t // CHUNK, A_HEADS, LANE, LANE), BF16)] + c_outs,
        scratch_shapes=[pltpu.VMEM((A_HEADS, LANE, LANE), F32)] + c_scratch,
        compiler_params=_cp("arbitrary"))(q, k, u, w, qk, bg, *c_ins)
    return outs[:3], outs[3:]


def _swa_neg_dist(n_blk):
    qi = lax.broadcasted_iota(jnp.int32, (BLOCK, 2 * BLOCK), 0)
    si = lax.broadcasted_iota(jnp.int32, (BLOCK, 2 * BLOCK), 1)
    dist = qi + BLOCK - si
    mask = (dist >= 0) & (dist < BLOCK) & ((si >= BLOCK) | (n_blk > 0))
    return jnp.where(mask, -dist.astype(F32), NEG)


def _stack_heads(ref, hk):
    return jnp.concatenate([ref[:, h * B_HEAD_DIM:(h + 1) * B_HEAD_DIM]
                            for h in range(hk * B_GROUP, (hk + 1) * B_GROUP)], axis=0)


def _swa_group_probs(q_ref, sk_ref, kband, vband, neg_dist):
    hks = range(B_KV_HEADS)
    heads = lambda hk: range(hk * B_GROUP, (hk + 1) * B_GROUP)
    ksl = lambda hk: slice(hk * B_HEAD_DIM, (hk + 1) * B_HEAD_DIM)
    ones = jnp.ones((2 * BLOCK, B_HEAD_DIM), BF16)
    qs = [_stack_heads(q_ref, hk) * (B_HEAD_DIM ** -0.5) for hk in hks]
    sink = [jnp.concatenate([jnp.broadcast_to(sk_ref[h:h + 1, 0:1], (BLOCK, 1)) for h in heads(hk)], axis=0)
            for hk in hks]
    s = [_dot_nt(qs[hk], kband[:, ksl(hk)]) + jnp.concatenate([ALIBI[h] * neg_dist for h in heads(hk)], axis=0)
         for hk in hks]
    m = [jnp.maximum(jnp.max(s[hk], axis=-1, keepdims=True), sink[hk]) for hk in hks]
    p = [jnp.exp(s[hk] - m[hk]) for hk in hks]
    oe = [jnp.dot(p[hk].astype(BF16), jnp.concatenate([vband[:, ksl(hk)].astype(BF16), ones], axis=1),
                  preferred_element_type=F32) for hk in hks]
    ps = [jnp.exp(sink[hk] - m[hk]) for hk in hks]
    inv = [1.0 / (oe[hk][:, B_HEAD_DIM:B_HEAD_DIM + 1] + ps[hk]) for hk in hks]
    return [(qs[hk], p[hk] * inv[hk], ps[hk] * inv[hk], oe[hk][:, :B_HEAD_DIM] * inv[hk]) for hk in hks]


def _swa_specs():
    qspec = lambda c0: pl.BlockSpec((BLOCK, B_WIDTH), lambda i: (i, c0 // B_WIDTH))
    cur = lambda c0: pl.BlockSpec((BLOCK, LANE), lambda i: (i, c0 // LANE))
    prev = lambda c0: pl.BlockSpec((BLOCK, LANE), lambda i: (jnp.maximum(i - 1, 0), c0 // LANE))
    return qspec, cur, prev


def _carried(carry, refs, n_in, n_out, steps):
    if carry is None:
        return refs
    ci, co = len(carry.ins), len(carry.outs)
    own = refs[:n_in] + refs[n_in + ci:n_in + ci + n_out] + refs[n_in + ci + n_out + co:len(refs) - 3]
    parts = refs[n_in:n_in + ci], refs[n_in + ci + n_out:n_in + ci + n_out + co], refs[len(refs) - 3:]

    @pl.when(pl.program_id(0) == 0)
    def _():
        carry.start(*parts)

    @pl.when(pl.program_id(0) == steps - 1)
    def _():
        carry.finish(*parts)

    return own


def _carry_specs(carry):
    if carry is None:
        return [], [], [], [], []
    return (list(carry.ins), [_ANY] * len(carry.ins), [_ANY] * len(carry.outs), list(carry.outs), carry.scratch())


def _swa_fwd(h, sinks_b, *, name, carry=None):
    t = h.shape[0]
    qspec, cur, prev = _swa_specs()
    c_ins, c_in_specs, c_out_specs, c_outs, c_scratch = _carry_specs(carry)

    def body(*refs):
        q_ref, kc_ref, kp_ref, vc_ref, vp_ref, sk_ref, o_ref = _carried(carry, refs, 6, 1, t // BLOCK)
        n_blk = pl.program_id(0)
        kband = jnp.concatenate([kp_ref[...], kc_ref[...]], axis=0)
        vband = jnp.concatenate([vp_ref[...], vc_ref[...]], axis=0)
        groups = _swa_group_probs(q_ref, sk_ref, kband, vband, _swa_neg_dist(n_blk))
        for hk, (_, _, _, o) in enumerate(groups):
            for g in range(B_GROUP):
                hq = hk * B_GROUP + g
                o_ref[:, hq * B_HEAD_DIM:(hq + 1) * B_HEAD_DIM] = o[g * BLOCK:(g + 1) * BLOCK]

    outs = pl.pallas_call(
        body, name=name, grid=(t // BLOCK,),
        in_specs=[qspec(C_QB), cur(C_KB), prev(C_KB), cur(C_VB), prev(C_VB),
                  pl.BlockSpec((B_Q_HEADS, LANE), lambda i: (0, 0))] + c_in_specs,
        out_specs=[pl.BlockSpec((BLOCK, B_WIDTH), lambda i: (i, 0))] + c_out_specs,
        out_shape=[jax.ShapeDtypeStruct((t, B_WIDTH), F32)] + c_outs,
        scratch_shapes=c_scratch,
        compiler_params=_cp("arbitrary"))(h, h, h, h, h, sinks_b, *c_ins)
    return outs[0], outs[1:]


def _rms_gate(o, za, nw):
    outs = []
    for hd in range(A_HEADS):
        oh = o[:, hd * LANE:(hd + 1) * LANE]
        r = lax.rsqrt(jnp.mean(oh * oh, -1, keepdims=True) + RMS_EPS)
        outs.append(oh * r * nw)
    return jnp.concatenate(outs, axis=1) * _silu(za)


def _out_ln(x, oa, ob, h, norm_w, w_out, ln_g, ln_b, *, tm, name, target=None):
    t = x.shape[0]
    last = target is not None

    def body(*refs):
        x_ref, oa_ref, ob_ref, za_ref, zb_ref, nw_ref, w_ref, g_ref, b_ref = refs[:9]
        xn_ref, mx_ref, r_ref = refs[9 + last:12 + last]
        ya = _rms_gate(oa_ref[...], za_ref[...], nw_ref[...])
        yb = ob_ref[...] * _silu(zb_ref[...])
        mixed = jnp.concatenate([ya, yb], axis=1).astype(BF16)
        mx_ref[...] = mixed
        r = DEEPNORM_ALPHA * x_ref[...] + jnp.dot(mixed, w_ref[...], preferred_element_type=F32)
        r_ref[...] = r
        mu = jnp.mean(r, -1, keepdims=True)
        xc = r - mu
        var = jnp.mean(xc * xc, -1, keepdims=True)
        xn = xc * lax.rsqrt(var + LN_EPS) * g_ref[...] + b_ref[...]
        if not last:
            xn_ref[...] = xn
            return
        loss_ref = refs[13]

        @pl.when(pl.program_id(0) == 0)
        def _():
            loss_ref[...] = jnp.zeros_like(loss_ref)

        err = xn - refs[9][...]
        xn_ref[...] = err * (1.0 / D_MODEL)
        loss_ref[...] += 0.5 / D_MODEL * jnp.sum(err * err)

    row = lambda w, c: pl.BlockSpec((tm, w), lambda i: (i, c))
    full = lambda a, b: pl.BlockSpec((a, b), lambda i: (0, 0))
    wide = jax.ShapeDtypeStruct((t, D_MODEL), F32)
    return pl.pallas_call(
        body, name=name, grid=(t // tm,),
        in_specs=[row(D_MODEL, 0), row(A_WIDTH, 0), row(B_WIDTH, 0), row(A_WIDTH, C_ZA // A_WIDTH),
                  row(B_WIDTH, C_ZB // B_WIDTH), full(1, LANE), full(D_MODEL, D_MODEL), full(1, D_MODEL),
                  full(1, D_MODEL)] + [row(D_MODEL, 0)] * last,
        out_specs=[row(D_MODEL, 0), row(D_MODEL, 0), row(D_MODEL, 0)] + [full(SUBLANE, LANE)] * last,
        out_shape=[wide, jax.ShapeDtypeStruct((t, D_MODEL), BF16), wide]
        + [jax.ShapeDtypeStruct((SUBLANE, LANE), F32)] * last,
        compiler_params=_cp("arbitrary" if last else "parallel"))(
        x, oa, ob, h, h, norm_w, w_out, ln_g, ln_b, *([target] if last else []))


def _layer_fwd(x, wt, conv_w, par, sinks_b, norm_w, w_out_bf, ln_g, ln_b, l, carries=None, target=None):
    carries = carries or {}
    h, got_in = _matmul_nt(x, wt, tm=min(1024, x.shape[0]), name=f"in_proj_{l}", carry=carries.get("in_proj"))
    if callable(w_out_bf):
        w_out_bf = w_out_bf(got_in)
    (q, k, v, bg, bgt), got_pre = _dn_pre(h, conv_w, par, tt=512, name=f"dn_pre_{l}", carry=carries.get("dn_pre"))
    (u, w, tmat, qk), got_wy = _dn_wy(q, k, v, bg, bgt, name=f"dn_wy_{l}", carry=carries.get("dn_wy"))
    (oa, vn, s_all), got_scan = _dn_scan_fwd(q, k, u, w, qk, bg, name=f"dn_scan_{l}", carry=carries.get("dn_scan"))
    ob, got_swa = _swa_fwd(h, sinks_b, name=f"swa_fwd_{l}", carry=carries.get("swa"))
    xn, mixed, r, *loss = _out_ln(x, oa, ob, h, norm_w, w_out_bf, ln_g, ln_b, tm=512, name=f"out_ln_{l}", target=target)
    if loss:
        xn = (xn, loss[0])
    res = dict(x=x, h=h, q=q, k=k, v=v, bg=bg, bgt=bgt, w=w, tmat=tmat, qk=qk, vn=vn, oa=oa, s_all=s_all,
               mixed=mixed, r=r, w_out=w_out_bf)
    return xn, res, dict(in_proj=got_in, dn_pre=got_pre, dn_wy=got_wy, dn_scan=got_scan, swa=got_swa)


def _ln_out_bwd(dxn, r, mixed, ln_g, w_out, *, tm, name):
    t = dxn.shape[0]

    def body(dxn_ref, r_ref, mx_ref, g_ref, w_ref, dr_ref, dm_ref, dw_ref, dg_ref, db_ref):
        @pl.when(pl.program_id(0) == 0)
        def _():
            dw_ref[...] = jnp.zeros_like(dw_ref)
            dg_ref[...] = jnp.zeros_like(dg_ref)
            db_ref[...] = jnp.zeros_like(db_ref)

        rr = r_ref[...]
        xc = rr - jnp.mean(rr, -1, keepdims=True)
        rstd = lax.rsqrt(jnp.mean(xc * xc, -1, keepdims=True) + LN_EPS)
        xhat = xc * rstd
        dxn_v = dxn_ref[...]
        dxh = dxn_v * g_ref[...]
        dr = rstd * (dxh - jnp.mean(dxh, -1, keepdims=True) - xhat * jnp.mean(dxh * xhat, -1, keepdims=True))
        dr_ref[...] = dr
        dg_ref[...] += jnp.sum(dxn_v * xhat, axis=0, keepdims=True)
        db_ref[...] += jnp.sum(dxn_v, axis=0, keepdims=True)
        drb = dr.astype(BF16)
        dm_ref[...] = _dot_nt(drb, w_ref[...])
        dw_ref[...] += _dot_tn(mx_ref[...], drb)

    row = pl.BlockSpec((tm, D_MODEL), lambda i: (i, 0))
    full = lambda a, b: pl.BlockSpec((a, b), lambda i: (0, 0))
    big = jax.ShapeDtypeStruct((t, D_MODEL), F32)
    vec = jax.ShapeDtypeStruct((1, D_MODEL), F32)
    return pl.pallas_call(
        body, name=name, grid=(t // tm,),
        in_specs=[row, row, row, full(1, D_MODEL), full(D_MODEL, D_MODEL)],
        out_specs=[row, row, full(D_MODEL, D_MODEL), full(1, D_MODEL), full(1, D_MODEL)],
        out_shape=[big, big, jax.ShapeDtypeStruct((D_MODEL, D_MODEL), F32), vec, vec],
        compiler_params=_cp("arbitrary"))(dxn, r, mixed, ln_g, w_out)


def _dn_post_bwd(dm, oa, h, norm_w, *, tm, name):
    t = oa.shape[0]

    def body(dy_ref, o_ref, za_ref, nw_ref, do_ref, dza_ref, dnw_ref):
        @pl.when(pl.program_id(0) == 0)
        def _():
            dnw_ref[...] = jnp.zeros_like(dnw_ref)

        nw = nw_ref[...]
        dnw = jnp.zeros_like(nw)
        for hd in range(A_HEADS):
            sl = slice(hd * LANE, (hd + 1) * LANE)
            oh, za, dy = o_ref[:, sl], za_ref[:, sl], dy_ref[:, sl]
            rs = lax.rsqrt(jnp.mean(oh * oh, -1, keepdims=True) + RMS_EPS)
            nrm = oh * rs
            dza_ref[:, sl] = dy * nrm * nw * _dsilu(za)
            dn = dy * _silu(za)
            dnw = dnw + jnp.sum(dn * nrm, axis=0, keepdims=True)
            dnn = dn * nw
            do_ref[:, sl] = rs * dnn - oh * (rs * rs * rs) * jnp.mean(dnn * oh, -1, keepdims=True)
        dnw_ref[...] += dnw

    row = lambda c: pl.BlockSpec((tm, A_WIDTH), lambda i: (i, c))
    wide = jax.ShapeDtypeStruct((t, A_WIDTH), F32)
    return pl.pallas_call(
        body, name=name, grid=(t // tm,),
        in_specs=[row(0), row(0), row(C_ZA // A_WIDTH), pl.BlockSpec((1, LANE), lambda i: (0, 0))],
        out_specs=[row(0), row(C_ZA // A_WIDTH), pl.BlockSpec((1, LANE), lambda i: (0, 0))],
        out_shape=[wide, jax.ShapeDtypeStruct((t, DH_MAIN), F32), jax.ShapeDtypeStruct((1, LANE), F32)],
        compiler_params=_cp("arbitrary"))(dm, oa, h, norm_w)


def _dn_scan_bwd(q, k, w, qk, bg, do, *, name):
    t = q.shape[0]
    rows = SCAN_ROWS
    per = rows // CHUNK
    n = t // rows

    def body(q_ref, k_ref, w_ref, qk_ref, bg_ref, do_ref, dvn_ref, ds_ref, dstate):
        @pl.when(pl.program_id(0) == 0)
        def _():
            dstate[...] = jnp.zeros_like(dstate)

        heads = range(A_HEADS)
        sl = lambda hd: slice(hd * LANE, (hd + 1) * LANE)
        ds_cur = [dstate[hd] for hd in heads]
        for c in reversed(range(per)):
            rs = slice(c * CHUNK, (c + 1) * CHUNK)
            bg_v = bg_ref[rs, :]
            gcols = [_chunk_gates(bg_v, None, hd)[1] for hd in heads]
            glasts = [gc[CHUNK - 1:CHUNK, :] for gc in gcols]
            for hd in heads:
                ds_ref[c, hd] = ds_cur[hd].astype(BF16)
            pdo = [_dot_tn(qk_ref[rs, hd * CHUNK:(hd + 1) * CHUNK], do_ref[rs, sl(hd)]) for hd in heads]
            qdo = [_dot_tn(q_ref[rs, sl(hd)] * jnp.exp(gcols[hd]), do_ref[rs, sl(hd)]) for hd in heads]
            dvns = [pdo[hd] + _dot(k_ref[rs, sl(hd)] * jnp.exp(glasts[hd] - gcols[hd]), ds_cur[hd]) for hd in heads]
            ds_cur = [qdo[hd] + jnp.exp(glasts[hd]) * ds_cur[hd] - _dot_tn(w_ref[rs, sl(hd)], dvns[hd])
                      for hd in heads]
            for hd in heads:
                dvn_ref[rs, sl(hd)] = dvns[hd]
        for hd in heads:
            dstate[hd] = ds_cur[hd]

    blk = pl.BlockSpec((rows, A_WIDTH), lambda i: (n - 1 - i, 0))
    return pl.pallas_call(
        body, name=name, grid=(n,),
        in_specs=[blk, blk, blk, pl.BlockSpec((rows, A_HEADS * CHUNK), lambda i: (n - 1 - i, 0)),
                  pl.BlockSpec((rows, LANE), lambda i: (n - 1 - i, 0)), blk],
        out_specs=[blk, pl.BlockSpec((per, A_HEADS, LANE, LANE), lambda i: (n - 1 - i, 0, 0, 0))],
        out_shape=[jax.ShapeDtypeStruct((t, A_WIDTH), F32),
                   jax.ShapeDtypeStruct((t // CHUNK, A_HEADS, LANE, LANE), BF16)],
        scratch_shapes=[pltpu.VMEM((A_HEADS, LANE, LANE), F32)],
        compiler_params=_cp("arbitrary"))(q, k, w, qk, bg, do)


def _dn_chunk_bwd(q, k, v, vn, tmat, qk, bg, bgt, s_all, ds_all, dvn, do, *, name):
    t = q.shape[0]
    rows = WY_ROWS
    per = rows // CHUNK

    def body(q_ref, k_ref, v_ref, vn_ref, tm_ref, qk_ref, bg_ref, bgt_ref, s_ref, ds_ref, dvn_ref, do_ref,
             dq_ref, dk_ref, dv_ref, dbg_ref, dbgt_ref):
        causal, strict, _ = _chunk_masks()
        lane = lax.broadcasted_iota(jnp.int32, (CHUNK, LANE), 1)
        rowi = lax.broadcasted_iota(jnp.int32, (CHUNK, 1), 0)
        sub = lax.broadcasted_iota(jnp.int32, (SUBLANE, CHUNK), 0)
        rs = lambda c: slice(c * CHUNK, (c + 1) * CHUNK)
        sl = lambda hd: slice(hd * LANE, (hd + 1) * LANE)
        hs = lambda hd: slice(hd * CHUNK, (hd + 1) * CHUNK)
        for c0 in range(0, per, WY_GROUP):
            items = [(c, hd) for c in range(c0, c0 + WY_GROUP) for hd in range(A_HEADS)]
            at = lambda ref: [ref[rs(c), sl(hd)] for c, hd in items]
            qs, ks, vs, dos, vns, dvns = at(q_ref), at(k_ref), at(v_ref), at(do_ref), at(vn_ref), at(dvn_ref)
            tmhs = [tm_ref[rs(c), hs(hd)] for c, hd in items]
            ps = [qk_ref[rs(c), hs(hd)] for c, hd in items]
            gates = [_chunk_gates(bg_ref[rs(c), :], bgt_ref[:, rs(c)], hd) for c, hd in items]
            betas = [g[0] for g in gates]
            gcols = [g[1] for g in gates]
            dmats = [jnp.exp(jnp.where(causal, g[1] - g[2], NEG)) for g in gates]
            es = [jnp.exp(gc) for gc in gcols]
            glasts = [gc[CHUNK - 1:CHUNK, :] for gc in gcols]
            eks = [jnp.exp(gl - gc) for gl, gc in zip(glasts, gcols)]
            kbs = [kh * b for kh, b in zip(ks, betas)]
            vbs = [vh * b for vh, b in zip(vs, betas)]
            kbes = [kb * e for kb, e in zip(kbs, es)]

            a_s = [jnp.where(strict, _dot_nt(kb, kh) * dm, 0.0) for kb, kh, dm in zip(kbs, ks, dmats)]
            dps = [jnp.where(causal, _dot_nt(doh, vnh), 0.0) for doh, vnh in zip(dos, vns)]
            dqds = [_dot_nt(doh, s_ref[c, hd]) for doh, (c, hd) in zip(dos, items)]
            dkds = [_dot_nt(vnh, ds_ref[c, hd]) for vnh, (c, hd) in zip(vns, items)]
            dws = [-_dot_nt(dvnh, s_ref[c, hd]) for dvnh, (c, hd) in zip(dvns, items)]
            dvbs = [_dot_tn(tmh, dvnh) for tmh, dvnh in zip(tmhs, dvns)]
            dgts = [jnp.sum(s_ref[c, hd].astype(F32) * ds_ref[c, hd].astype(F32), keepdims=True) for c, hd in items]
            dts = [_dot_nt(dvnh, vb) + _dot_nt(dw, kbe) for dvnh, vb, dw, kbe in zip(dvns, vbs, dws, kbes)]
            dkbes = [_dot_tn(tmh, dw) for tmh, dw in zip(tmhs, dws)]
            xs = [_dot_nt(dt, tmh) for dt, tmh in zip(dts, tmhs)]
            das = [jnp.where(strict, -_dot_tn(tmh, x), 0.0) for tmh, x in zip(tmhs, xs)]
            dmas = [da * dm for da, dm in zip(das, dmats)]
            dmps = [dp * dm for dp, dm in zip(dps, dmats)]
            dkbs = [_dot(dma, kh) + dkbe * e for dma, kh, dkbe, e in zip(dmas, ks, dkbes, es)]
            for i, (c, hd) in enumerate(items):
                dq_ref[rs(c), sl(hd)] = _dot(dmps[i], ks[i]) + dqds[i] * es[i]
                dk_ref[rs(c), sl(hd)] = (_dot_tn(dmas[i], kbs[i]) + _dot_tn(dmps[i], qs[i]) + dkds[i] * eks[i]
                                         + dkbs[i] * betas[i])
                dv_ref[rs(c), sl(hd)] = dvbs[i] * betas[i]
            for c in range(c0, c0 + WY_GROUP):
                acc = jnp.zeros((CHUNK, LANE), F32)
                acc_t = jnp.zeros((SUBLANE, CHUNK), F32)
                for i, (ci, hd) in enumerate(items):
                    if ci != c:
                        continue
                    gmat = das[i] * a_s[i] + dps[i] * ps[i]
                    rk = jnp.sum(dkds[i] * ks[i], -1, keepdims=True) * eks[i]
                    de = (jnp.sum(dqds[i] * qs[i], -1, keepdims=True)
                          + jnp.sum(dkbes[i] * kbs[i], -1, keepdims=True))
                    dglast = jnp.sum(rk, keepdims=True) + dgts[i] * jnp.exp(glasts[i])
                    dgc = (jnp.sum(gmat, -1, keepdims=True) + de * es[i] - rk
                           + jnp.where(rowi == CHUNK - 1, dglast, 0.0))
                    dbeta = (jnp.sum(dkbs[i] * ks[i], -1, keepdims=True)
                             + jnp.sum(dvbs[i] * vs[i], -1, keepdims=True))
                    acc = acc + jnp.where(lane == hd, dbeta, 0.0) + jnp.where(lane == A_HEADS + hd, dgc, 0.0)
                    acc_t = acc_t + jnp.where(sub == A_HEADS + hd, -jnp.sum(gmat, axis=0, keepdims=True), 0.0)
                dbg_ref[rs(c), :] = acc
                dbgt_ref[:, rs(c)] = acc_t

    blk = pl.BlockSpec((rows, A_WIDTH), lambda i: (i, 0))
    half = pl.BlockSpec((rows, A_HEADS * CHUNK), lambda i: (i, 0))
    col = pl.BlockSpec((rows, LANE), lambda i: (i, 0))
    rowf = pl.BlockSpec((SUBLANE, rows), lambda i: (0, i))
    st = pl.BlockSpec((per, A_HEADS, LANE, LANE), lambda i: (i, 0, 0, 0))
    wide = jax.ShapeDtypeStruct((t, A_WIDTH), F32)
    return pl.pallas_call(
        body, name=name, grid=(t // rows,),
        in_specs=[blk, blk, blk, blk, half, half, col, rowf, st, st, blk, blk],
        out_specs=[blk, blk, blk, col, rowf],
        out_shape=[wide, wide, wide, jax.ShapeDtypeStruct((t, LANE), F32), jax.ShapeDtypeStruct((SUBLANE, t), F32)],
        compiler_params=_cp("parallel"))(q, k, v, vn, tmat, qk, bg, bgt, s_all, ds_all, dvn, do)


def _dn_pre_bwd(h, conv_w, par, dq, dk, dv, dbg, dbgt, *, tt, name):
    t = h.shape[0]
    cw = 3 * A_WIDTH
    hb = tt // SUBLANE

    def body(pre_ref, halo_ref, bgi_ref, cw_ref, par_ref, dq_ref, dk_ref, dv_ref, dbg_ref, dbgt_ref,
             dc_ref, dbgi_ref, dpar_ref):
        i = pl.program_id(0)

        @pl.when(i == 0)
        def _():
            dpar_ref[...] = jnp.zeros_like(dpar_ref)

        cur = pre_ref[...]
        before = jnp.where(i > 0, halo_ref[...], 0.0)
        c = _conv_fwd(cur, before, cw_ref[...])
        s = _silu(c)
        ds = _dsilu(c)
        for hd in range(A_HEADS):
            sl = slice(hd * LANE, (hd + 1) * LANE)
            for base, d_ref, scale in ((0, dq_ref, A_HEAD_DIM ** -0.5), (A_WIDTH, dk_ref, 1.0)):
                csl = slice(base + hd * LANE, base + (hd + 1) * LANE)
                tq = s[:, base + hd * LANE:base + (hd + 1) * LANE]
                dy = d_ref[:, sl]
                rq = lax.rsqrt(jnp.sum(tq * tq, -1, keepdims=True) + L2_EPS)
                dtq = scale * (rq * dy - tq * (rq * rq * rq) * jnp.sum(dy * tq, -1, keepdims=True))
                dc_ref[:, csl] = dtq * ds[:, base + hd * LANE:base + (hd + 1) * LANE]
        dc_ref[:, 2 * A_WIDTH:] = dv_ref[...] * ds[:, 2 * A_WIDTH:]
        raw = bgi_ref[...]
        lane = lax.broadcasted_iota(jnp.int32, raw.shape, 1)
        is_b = lane < A_HEADS
        is_a = (lane >= A_HEADS) & (lane < 2 * A_HEADS)
        rows_t = jnp.concatenate([dbgt_ref[...], jnp.zeros((LANE - SUBLANE, tt), F32)], axis=0)
        dbg_v = dbg_ref[...] + jnp.where(is_a, jnp.transpose(rows_t), 0.0)
        dbg_v = jnp.where(is_a, _dot_hi(_chunk_tri(tt, lower=False), jnp.where(is_a, dbg_v, 0.0)), dbg_v)
        beta = _sigmoid(raw)
        z = raw + par_ref[1:2, :]
        neg_ea = -jnp.exp(par_ref[0:1, :])
        g = neg_ea * _softplus(z)
        da = dbg_v * neg_ea * _sigmoid(z)
        dbgi_ref[...] = jnp.where(is_b, dbg_v * beta * (1.0 - beta), jnp.where(is_a, da, 0.0))
        dpar_ref[0:1, :] += jnp.sum(jnp.where(is_a, dbg_v * g, 0.0), axis=0, keepdims=True)
        dpar_ref[1:2, :] += jnp.sum(jnp.where(is_a, da, 0.0), axis=0, keepdims=True)

    wide = pl.BlockSpec((tt, A_WIDTH), lambda i: (i, 0))
    return pl.pallas_call(
        body, name=name, grid=(t // tt,),
        in_specs=[pl.BlockSpec((tt, cw), lambda i: (i, 0)),
                  pl.BlockSpec((SUBLANE, cw), lambda i: (jnp.maximum(i * hb - 1, 0), 0)),
                  pl.BlockSpec((tt, LANE), lambda i: (i, C_BG // LANE)),
                  pl.BlockSpec((CONV_K, cw), lambda i: (0, 0)),
                  pl.BlockSpec((SUBLANE, LANE), lambda i: (0, 0)),
                  wide, wide, wide, pl.BlockSpec((tt, LANE), lambda i: (i, 0)),
                  pl.BlockSpec((SUBLANE, tt), lambda i: (0, i))],
        out_specs=[pl.BlockSpec((tt, cw), lambda i: (i, 0)), pl.BlockSpec((tt, LANE), lambda i: (i, 0)),
                   pl.BlockSpec((SUBLANE, LANE), lambda i: (0, 0))],
        out_shape=[jax.ShapeDtypeStruct((t, cw), F32), jax.ShapeDtypeStruct((t, LANE), F32),
                   jax.ShapeDtypeStruct((SUBLANE, LANE), F32)],
        compiler_params=_cp("arbitrary"))(h, h, h, conv_w, par, dq, dk, dv, dbg, dbgt)


def _conv_bwd(dc, h, conv_w, dh, *, tt, name):
    t = dc.shape[0]
    cw = 3 * A_WIDTH
    hb = tt // SUBLANE
    nb = t // tt

    def body(dc_ref, after_ref, pre_ref, before_ref, cw_ref, dh_in_ref, dpre_ref, dcw_ref):
        i = pl.program_id(0)

        @pl.when(i == 0)
        def _():
            dcw_ref[...] = jnp.zeros_like(dcw_ref)

        dcv = dc_ref[...]
        after = jnp.where(i < nb - 1, after_ref[...], 0.0)
        cur = pre_ref[...]
        before = jnp.where(i > 0, before_ref[...], 0.0)
        w = cw_ref[...]
        acc = dcv * w[CONV_K - 1:CONV_K, :]
        dcw_ref[CONV_K - 1:CONV_K, :] += jnp.sum(dcv * cur, axis=0, keepdims=True)
        for s in range(1, CONV_K):
            j = CONV_K - 1 - s
            acc = acc + _shift_up(dcv, after, s) * w[j:j + 1, :]
            dcw_ref[j:j + 1, :] += jnp.sum(dcv * _shift_down(cur, before, s), axis=0, keepdims=True)
        dpre_ref[...] = acc

    return pl.pallas_call(
        body, name=name, grid=(nb,),
        in_specs=[pl.BlockSpec((tt, cw), lambda i: (i, 0)),
                  pl.BlockSpec((SUBLANE, cw), lambda i: (jnp.minimum((i + 1) * hb, t // SUBLANE - 1), 0)),
                  pl.BlockSpec((tt, cw), lambda i: (i, 0)),
                  pl.BlockSpec((SUBLANE, cw), lambda i: (jnp.maximum(i * hb - 1, 0), 0)),
                  pl.BlockSpec((CONV_K, cw), lambda i: (0, 0)), _ANY],
        out_specs=[pl.BlockSpec((tt, cw), lambda i: (i, 0)), pl.BlockSpec((SUBLANE, cw), lambda i: (0, 0))],
        out_shape=[jax.ShapeDtypeStruct(dh.shape, F32), jax.ShapeDtypeStruct((SUBLANE, cw), F32)],
        input_output_aliases={5: 0},
        compiler_params=_cp("arbitrary"))(dc, dc, h, h, conv_w, dh)


def _swa_bwd(h, dm, sinks_b, dh, *, name, carry=None):
    t = h.shape[0]
    qspec, cur, prev = _swa_specs()
    c_ins, c_in_specs, c_out_specs, c_outs, c_scratch = _carry_specs(carry)

    def body(*refs):
        (q_ref, kc_ref, kp_ref, vc_ref, vp_ref, zb_ref, dy_ref, sk_ref, dh_in_ref,
         dqz_ref, dk_ref, dv_ref, dsk_ref) = _carried(carry, refs, 9, 4, t // BLOCK)
        n_blk = pl.program_id(0)

        @pl.when(n_blk == 0)
        def _():
            dk_ref[...] = jnp.zeros_like(dk_ref)
            dv_ref[...] = jnp.zeros_like(dv_ref)
            dsk_ref[...] = jnp.zeros_like(dsk_ref)

        kband = jnp.concatenate([kp_ref[...], kc_ref[...]], axis=0)
        vband = jnp.concatenate([vp_ref[...], vc_ref[...]], axis=0)
        scale = B_HEAD_DIM ** -0.5
        hks = range(B_KV_HEADS)
        ksl = lambda hk: slice(hk * B_HEAD_DIM, (hk + 1) * B_HEAD_DIM)
        groups = _swa_group_probs(q_ref, sk_ref, kband, vband, _swa_neg_dist(n_blk))
        zbs = [_stack_heads(zb_ref, hk) for hk in hks]
        dys = [_stack_heads(dy_ref, hk) for hk in hks]
        dos = [dys[hk] * _silu(zbs[hk]) for hk in hks]
        deltas = [jnp.sum(dos[hk] * groups[hk][3], -1, keepdims=True) for hk in hks]
        dss = [groups[hk][1] * (_dot_nt(dos[hk], vband[:, ksl(hk)]) - deltas[hk]) for hk in hks]
        dqs = [_dot(dss[hk], kband[:, ksl(hk)]) * scale for hk in hks]
        dk_acc = [_dot_tn(dss[hk], groups[hk][0]) for hk in hks]
        dv_acc = [_dot_tn(groups[hk][1], dos[hk]) for hk in hks]
        for hk in hks:
            dzb = dys[hk] * groups[hk][3] * _dsilu(zbs[hk])
            dsink = groups[hk][2] * deltas[hk]
            for g in range(B_GROUP):
                hq = hk * B_GROUP + g
                rows = slice(g * BLOCK, (g + 1) * BLOCK)
                qsl = slice(hq * B_HEAD_DIM, (hq + 1) * B_HEAD_DIM)
                dqz_ref[:, qsl] = dqs[hk][rows]
                dqz_ref[:, B_WIDTH + hq * B_HEAD_DIM:B_WIDTH + (hq + 1) * B_HEAD_DIM] = dzb[rows]
                dsk_ref[hq:hq + 1, :] += -jnp.sum(dsink[rows], keepdims=True)
        dkb = jnp.concatenate(dk_acc, axis=1)
        dvb = jnp.concatenate(dv_acc, axis=1)
        at_cur = pl.ds(pl.multiple_of(n_blk * BLOCK, BLOCK), BLOCK)
        at_prev = pl.ds(pl.multiple_of(jnp.maximum(n_blk - 1, 0) * BLOCK, BLOCK), BLOCK)
        dk_ref[at_prev, :] += dkb[:BLOCK]
        dv_ref[at_prev, :] += dvb[:BLOCK]
        dk_ref[at_cur, :] += dkb[BLOCK:]
        dv_ref[at_cur, :] += dvb[BLOCK:]

    narrow = jax.ShapeDtypeStruct((t, B_KV_WIDTH), F32)
    res = lambda a, b: pl.BlockSpec((a, b), lambda i: (0, 0))
    outs = pl.pallas_call(
        body, name=name, grid=(t // BLOCK,),
        in_specs=[qspec(C_QB), cur(C_KB), prev(C_KB), cur(C_VB), prev(C_VB), qspec(C_ZB),
                  pl.BlockSpec((BLOCK, B_WIDTH), lambda i: (i, 1)), res(B_Q_HEADS, LANE), _ANY] + c_in_specs,
        out_specs=[pl.BlockSpec((BLOCK, 2 * B_WIDTH), lambda i: (i, C_QB // (2 * B_WIDTH))),
                   res(t, B_KV_WIDTH), res(t, B_KV_WIDTH), res(B_Q_HEADS, LANE)] + c_out_specs,
        out_shape=[jax.ShapeDtypeStruct(dh.shape, F32), narrow, narrow,
                   jax.ShapeDtypeStruct((B_Q_HEADS, LANE), F32)] + c_outs,
        scratch_shapes=c_scratch,
        input_output_aliases={8: 0},
        compiler_params=_cp("arbitrary"))(h, h, h, h, h, h, dm, sinks_b, dh, *c_ins)
    return outs[:4], outs[4:]


def _matmul_tn(a, b, *, tk, tm, name):
    t, m = a.shape
    n = b.shape[1]

    def body(a_ref, b_ref, o_ref):
        @pl.when(pl.program_id(1) == 0)
        def _():
            o_ref[...] = jnp.zeros_like(o_ref)

        o_ref[...] += _dot_tn(a_ref[...], b_ref[...])

    return pl.pallas_call(
        body, name=name, grid=(m // tm, t // tk),
        in_specs=[pl.BlockSpec((tk, tm), lambda j, kk: (kk, j)), pl.BlockSpec((tk, n), lambda j, kk: (kk, 0))],
        out_specs=pl.BlockSpec((tm, n), lambda j, kk: (j, 0)),
        out_shape=jax.ShapeDtypeStruct((m, n), F32),
        compiler_params=_cp("parallel", "arbitrary"))(a, b)


def _in_proj_dx(dh_main, dh_tail, wt, dr, *, tm, name, carry=None):
    t, n_main = dh_main.shape
    n_tail = dh_tail.shape[1]
    c_ins, c_in_specs, c_out_specs, c_outs, c_scratch = _carry_specs(carry)

    def body(*refs):
        a_ref, t_ref, wa_ref, wt_ref, r_ref, o_ref = _carried(carry, refs, 5, 1, t // tm)
        o_ref[...] = _dot(a_ref[...], wa_ref[...]) + _dot(t_ref[...], wt_ref[...]) + DEEPNORM_ALPHA * r_ref[...]

    row = lambda w: pl.BlockSpec((tm, w), lambda i: (i, 0))
    outs = pl.pallas_call(
        body, name=name, grid=(t // tm,),
        in_specs=[row(n_main), row(n_tail), pl.BlockSpec((n_main, D_MODEL), lambda i: (0, 0)),
                  pl.BlockSpec((n_tail, D_MODEL), lambda i: (n_main // n_tail, 0)), row(D_MODEL)] + c_in_specs,
        out_specs=[row(D_MODEL)] + c_out_specs,
        out_shape=[jax.ShapeDtypeStruct((t, D_MODEL), F32)] + c_outs,
        scratch_shapes=c_scratch,
        compiler_params=_cp("arbitrary"))(dh_main, dh_tail, wt, wt, dr, *c_ins)
    return outs[0], outs[1:]


def _layer_bwd(dxn, res, wt, conv_w, par, sinks_b, norm_w, w_out_bf, ln_g, l, carry=None, carry_dx=None):
    w_out_bf = res["w_out"]
    dr, dm, dw_out, dln_g, dln_b = _ln_out_bwd(dxn, res["r"], res["mixed"], ln_g, w_out_bf, tm=512, name=f"ln_out_bwd_{l}")
    h = res["h"]
    do, dh, dnw = _dn_post_bwd(dm, res["oa"], h, norm_w, tm=512, name=f"dn_post_bwd_{l}")
    dvn, ds_all = _dn_scan_bwd(res["q"], res["k"], res["w"], res["qk"], res["bg"], do, name=f"dn_scan_bwd_{l}")
    dq, dk, dv, dbg, dbgt = _dn_chunk_bwd(res["q"], res["k"], res["v"], res["vn"], res["tmat"], res["qk"], res["bg"],
                                          res["bgt"], res["s_all"], ds_all, dvn, do, name=f"dn_chunk_bwd_{l}")
    dc, dbgi, dpar = _dn_pre_bwd(h, conv_w, par, dq, dk, dv, dbg, dbgt, tt=512, name=f"dn_pre_bwd_{l}")
    dh, dcw = _conv_bwd(dc, h, conv_w, dh, tt=512, name=f"conv_bwd_{l}")
    (dh, dkb, dvb, dsk), carried = _swa_bwd(h, dm, sinks_b, dh, name=f"swa_bwd_{l}", carry=carry)
    dh_tail = jnp.concatenate([dkb, dvb, dbgi], axis=1)
    dwt_main = _matmul_tn(dh, res["x"], tk=512, tm=DH_MAIN, name=f"in_proj_dw_{l}")
    dwt_tail = _matmul_tn(dh_tail, res["x"], tk=512, tm=P_COLS - DH_MAIN, name=f"in_proj_dw_tail_{l}")
    grads = dict(w_in=(dwt_main, dwt_tail), conv_w=dcw[:CONV_K], a_log=dpar[0, A_HEADS:2 * A_HEADS],
                 dt_bias=dpar[1, A_HEADS:2 * A_HEADS], norm_w=dnw[0], sinks=dsk[:, 0], w_out=dw_out,
                 ln_g=dln_g[0], ln_b=dln_b[0])
    dx, carried_dx = _in_proj_dx(dh, dh_tail, wt, dr, tm=512, name=f"in_proj_dx_{l}",
                                 carry=None if carry_dx is None else carry_dx(grads))
    return dx, grads, carried, carried_dx


def _layer_args(wt, conv_w, a_log, dt_bias, sinks, norm_w, w_out_bf):
    return (wt, conv_w, _gate_params(a_log, dt_bias), jnp.broadcast_to(sinks[:, None], (B_Q_HEADS, LANE)),
            norm_w[None], w_out_bf)


def _local_step(x, target, args0, args1, ln_g, ln_b, gathers=None, reduce1=None, reduce0=None):
    assert DEPTH == 2
    x1, res0, got = _layer_fwd(x, *args0, ln_g[0][None], ln_b[0][None], 0, carries=gathers)
    if gathers is not None:
        args1 = args1(got)
    (dx, loss_tile), res1, _ = _layer_fwd(x1, *args1, ln_g[1][None], ln_b[1][None], 1, target=target)
    dx, grads1, _, _ = _layer_bwd(dx, res1, *args1, ln_g[1][None], 1)
    carry = None if reduce1 is None else reduce1(grads1)
    carry_dx = None if reduce0 is None else (lambda grads0: reduce0(grads0, grads1, loss_tile))
    dx, grads0, landed1, landed0 = _layer_bwd(dx, res0, *args0, ln_g[0][None], 0, carry=carry, carry_dx=carry_dx)
    return loss_tile, dx, [grads0, grads1], landed1, landed0


_ANY = pl.BlockSpec(memory_space=pl.ANY)
_MESH = pl.DeviceIdType.MESH


HALF = D_MODEL // 2


class _Exchange:
    def __init__(self, ins, outs, n_remote, n_local, plan):
        self.ins, self.outs, self.n_remote, self.n_local, self.plan = tuple(ins), tuple(outs), n_remote, n_local, plan

    def scratch(self):
        return [pltpu.SemaphoreType.DMA((self.n_remote,)), pltpu.SemaphoreType.DMA((self.n_remote,)),
                pltpu.SemaphoreType.DMA((max(self.n_local, 1),))]

    def _copies(self, in_refs, out_refs, sems, arriving):
        send_sems, recv_sems, local_sems = sems
        local, sends, recvs = self.plan(in_refs, out_refs)
        loc = [pltpu.make_async_copy(s, d, local_sems.at[i]) for i, (s, d) in enumerate(local)]
        rem = [pltpu.make_async_remote_copy(src_ref=s, dst_ref=recvs[i] if arriving else d, send_sem=send_sems.at[i],
                                            recv_sem=recv_sems.at[i], device_id=peer, device_id_type=_MESH)
               for i, (s, d, peer) in enumerate(sends)]
        return loc, rem

    def start(self, in_refs, out_refs, sems):
        loc, rem = self._copies(in_refs, out_refs, sems, arriving=False)
        for cp in loc + rem:
            cp.start()

    def finish(self, in_refs, out_refs, sems):
        loc, rem = self._copies(in_refs, out_refs, sems, arriving=True)
        for cp in rem:
            cp.wait_recv()
        for cp in rem:
            cp.wait_send()
        for cp in loc:
            cp.wait()


def _run_exchange(ex, *, name):
    n_in, n_out = len(ex.ins), len(ex.outs)

    def body(*refs):
        parts = refs[:n_in], refs[n_in:n_in + n_out], refs[n_in + n_out:]
        ex.start(*parts)
        ex.finish(*parts)

    return pl.pallas_call(body, name=name, in_specs=[_ANY] * n_in, out_specs=[_ANY] * n_out, out_shape=list(ex.outs),
                          scratch_shapes=ex.scratch())(*ex.ins)


def _place():
    x, y, c = lax.axis_index("x"), lax.axis_index("y"), lax.axis_index("c")
    return x, y, c, [(1 - x, y), (x, 1 - y), (1 - x, 1 - y)]


def _gather_exchange(arrays):
    n = len(arrays)

    def plan(src, dst):
        x, y, c, chips = _place()
        me = 2 * x + y
        local = [(src[k], dst[k].at[me]) for k in range(n)]
        sends = [(src[k], dst[k].at[me], (px, py, c)) for k in range(n) for px, py in chips]
        recvs = [dst[k].at[2 * px + py] for k in range(n) for px, py in chips]
        return local, sends, recvs

    return _Exchange(arrays, [jax.ShapeDtypeStruct((N_SHARD,) + a.shape, a.dtype) for a in arrays], 3 * n, n, plan)


def _gather_two_level(pack, conv_w, *, name):
    rows = pack.shape[0]
    part_rows = rows // 2

    def body(pack_ref, conv_ref, land_ref, conv_land_ref, send1, recv1, send2, recv2, csend, crecv, local_sems):
        x, y, c, chips = _place()
        me = 2 * x + y
        sibling = (x, y, 1 - c)
        part = lambda core: pl.ds(pl.multiple_of(core * part_rows, 16), part_rows)
        remote = lambda src, dst, ss, rs, to: pltpu.make_async_remote_copy(
            src_ref=src, dst_ref=dst, send_sem=ss, recv_sem=rs, device_id=to, device_id_type=_MESH)
        local = [pltpu.make_async_copy(pack_ref, land_ref.at[me], local_sems.at[0]),
                 pltpu.make_async_copy(conv_ref, conv_land_ref.at[me], local_sems.at[1])]
        for cp in local:
            cp.start()
        first = [remote(pack_ref.at[part(c)], land_ref.at[me, part(c)], send1.at[j], recv1.at[j], (px, py, c))
                 for j, (px, py) in enumerate(chips)]
        convs = [remote(conv_ref, conv_land_ref.at[me], csend.at[j], crecv.at[j], (px, py, c))
                 for j, (px, py) in enumerate(chips)]
        for cp in first + convs:
            cp.start()
        passed = []
        for j, (px, py) in enumerate(chips):
            slot = 2 * px + py
            remote(pack_ref.at[part(c)], land_ref.at[slot, part(c)], send1.at[j], recv1.at[j], (px, py, c)).wait_recv()
            cp = remote(land_ref.at[slot, part(c)], land_ref.at[slot, part(c)], send2.at[j], recv2.at[j], sibling)
            cp.start()
            passed.append(cp)
        for j, (px, py) in enumerate(chips):
            slot = 2 * px + py
            remote(land_ref.at[slot, part(1 - c)], land_ref.at[slot, part(1 - c)], send2.at[j], recv2.at[j],
                   sibling).wait_recv()
            remote(conv_ref, conv_land_ref.at[slot], csend.at[j], crecv.at[j], (px, py, c)).wait_recv()
        for cp in first + convs + passed:
            cp.wait_send()
        for cp in local:
            cp.wait()

    sems = [pltpu.SemaphoreType.DMA((3,))] * 6 + [pltpu.SemaphoreType.DMA((2,))]
    return pl.pallas_call(
        body, name=name, in_specs=[_ANY, _ANY], out_specs=[_ANY, _ANY],
        out_shape=[jax.ShapeDtypeStruct((N_SHARD,) + pack.shape, pack.dtype),
                   jax.ShapeDtypeStruct((N_SHARD,) + conv_w.shape, conv_w.dtype)],
        scratch_shapes=sems)(pack, conv_w)


def _half(core):
    return pl.ds(pl.multiple_of(core * HALF, HALF), HALF)


def _reduce_scatter_exchange(g, small=None):
    ins = [g] if small is None else [g, small]
    outs = [jax.ShapeDtypeStruct((7,) + g.shape[1:2] + (HALF,), g.dtype)]
    if small is not None:
        outs.append(jax.ShapeDtypeStruct((8,) + small.shape, small.dtype))

    def plan(src, dst):
        x, y, c, chips = _place()
        me = 2 * x + y
        peers = [(px, py, c if t == 0 else 1 - c) for px, py in chips for t in (0, 1)] + [(x, y, 1 - c)]
        sends = [(src[0].at[2 * px + py, :, _half(pc)], dst[0].at[k], (px, py, pc)) for k, (px, py, pc) in enumerate(peers)]
        recvs = [dst[0].at[k] for k in range(7)]
        local = []
        if small is not None:
            mine = 4 * x + 2 * y + c
            local = [(src[1], dst[1].at[mine])]
            sends += [(src[1], dst[1].at[mine], peer) for peer in peers]
            recvs += [dst[1].at[4 * px + 2 * py + pc] for px, py, pc in peers]
        return local, sends, recvs

    return _Exchange(ins, outs, 7 * len(ins), len(ins) - 1, plan)


def _pair_window_exchange(g):
    def plan(src, dst):
        x, y, c, _ = _place()
        return [], [(src[0].at[:, :, _half(1 - c)], dst[0], (x, y, 1 - c))], [dst[0]]

    return _Exchange([g], [jax.ShapeDtypeStruct(g.shape[:2] + (HALF,), g.dtype)], 1, 0, plan)


def _chip_scatter_exchange(p, small):
    def plan(src, dst):
        x, y, c, chips = _place()
        mine = 4 * x + 2 * y + c
        peers = [(px, py, c if t == 0 else 1 - c) for px, py in chips for t in (0, 1)] + [(x, y, 1 - c)]
        sends = [(src[0].at[2 * px + py], dst[0].at[j], (px, py, c)) for j, (px, py) in enumerate(chips)]
        recvs = [dst[0].at[j] for j in range(3)]
        sends += [(src[1], dst[1].at[mine], peer) for peer in peers]
        recvs += [dst[1].at[4 * px + 2 * py + pc] for px, py, pc in peers]
        return [(src[1], dst[1].at[mine])], sends, recvs

    outs = [jax.ShapeDtypeStruct((3,) + p.shape[1:], p.dtype), jax.ShapeDtypeStruct((8,) + small.shape, small.dtype)]
    return _Exchange([p, small], outs, 10, 1, plan)


def _share_exchange(arrays):
    n = len(arrays)

    def plan(src, dst):
        x, y, c, _ = _place()
        return [], [(src[k], dst[k], (x, y, 1 - c)) for k in range(n)], [dst[k] for k in range(n)]

    return _Exchange(arrays, [jax.ShapeDtypeStruct(a.shape, a.dtype) for a in arrays], n, 0, plan)


def _sum_scatter(g, land, me, core, *, tc, name):
    rows = g.shape[1]
    per = HALF // tc

    def body(where_ref, g_ref, land_ref, o_ref):
        acc = g_ref[...]
        for k in range(7):
            acc = acc + land_ref[k].astype(F32)
        o_ref[...] = acc

    return pl.pallas_call(
        body, name=name, out_shape=jax.ShapeDtypeStruct((rows, HALF), F32), compiler_params=_cp("parallel"),
        grid_spec=pltpu.PrefetchScalarGridSpec(
            num_scalar_prefetch=1, grid=(per,),
            in_specs=[pl.BlockSpec((None, rows, tc), lambda i, w: (w[0], 0, w[1] * per + i)),
                      pl.BlockSpec((7, rows, tc), lambda i, w: (0, 0, i))],
            out_specs=pl.BlockSpec((rows, tc), lambda i, w: (0, i))))(
        jnp.stack([me, core]).astype(jnp.int32), g, land)


def _pair_add(g, land, core, *, name):
    n, rows, _ = g.shape

    def body(core_ref, g_ref, land_ref, o_ref):
        o_ref[...] = (g_ref[...].astype(F32) + land_ref[...].astype(F32)).astype(o_ref.dtype)

    blk = pl.BlockSpec((1, rows, HALF), lambda i, w: (i, 0, 0))
    return pl.pallas_call(
        body, name=name, out_shape=jax.ShapeDtypeStruct((n, rows, HALF), g.dtype), compiler_params=_cp("parallel"),
        grid_spec=pltpu.PrefetchScalarGridSpec(
            num_scalar_prefetch=1, grid=(n,),
            in_specs=[pl.BlockSpec((1, rows, HALF), lambda i, w: (i, 0, w[0])), blk], out_specs=blk))(
        jnp.reshape(core, (1,)).astype(jnp.int32), g, land)


def _sum_chips(p, land, me, *, tc, name):
    rows = p.shape[1]

    def body(me_ref, p_ref, land_ref, o_ref):
        acc = p_ref[...].astype(F32)
        for k in range(3):
            acc = acc + land_ref[k].astype(F32)
        o_ref[...] = acc

    return pl.pallas_call(
        body, name=name, out_shape=jax.ShapeDtypeStruct((rows, HALF), F32), compiler_params=_cp("parallel"),
        grid_spec=pltpu.PrefetchScalarGridSpec(
            num_scalar_prefetch=1, grid=(HALF // tc,),
            in_specs=[pl.BlockSpec((None, rows, tc), lambda i, w: (w[0], 0, i)),
                      pl.BlockSpec((3, rows, tc), lambda i, w: (0, 0, i))],
            out_specs=pl.BlockSpec((rows, tc), lambda i, w: (0, i))))(
        jnp.reshape(me, (1,)).astype(jnp.int32), p, land)


def _sum_slots(a, *, name):
    n = a.shape[0]

    def body(a_ref, o_ref):
        acc = a_ref[0]
        for k in range(1, n):
            acc = acc + a_ref[k]
        o_ref[...] = acc

    return pl.pallas_call(body, name=name, out_shape=jax.ShapeDtypeStruct(a.shape[1:], a.dtype))(a)


def _elementwise(fn, ins, n_out, block, *, name):
    shape = ins[0].shape
    grid = tuple(s // b for s, b in zip(shape, block))
    n_in = len(ins)

    def body(*refs):
        outs = fn(*[r[...] for r in refs[:n_in]])
        for o_ref, val in zip(refs[n_in:], outs):
            o_ref[...] = val

    spec = pl.BlockSpec(block, lambda i, j, k: (i, j, k))
    return pl.pallas_call(body, name=name, grid=grid, in_specs=[spec] * n_in, out_specs=[spec] * n_out,
                          out_shape=[jax.ShapeDtypeStruct(shape, F32)] * n_out,
                          compiler_params=_cp(*["parallel"] * 3))(*ins)


def _adamw_math(w, g, m, v):
    mn = ADAM_B1 * m + (1.0 - ADAM_B1) * g
    vn = ADAM_B2 * v + (1.0 - ADAM_B2) * (g * g)
    m_hat = mn / (1.0 - ADAM_B1 ** ADAM_STEP)
    v_hat = vn / (1.0 - ADAM_B2 ** ADAM_STEP)
    return -ADAM_LR * (m_hat / (jnp.sqrt(v_hat) + ADAM_EPS) + ADAM_WD * w), mn, vn


def _adamw(w, g, m, v, block, *, name):
    return _elementwise(_adamw_math, [w, g, m, v], 3, block, name=name)


def _interleave_layers(layers, *, tc, name):
    rows, cols = layers[0].shape
    n = len(layers)

    def body(*refs):
        for l in range(n):
            refs[n][:, l, :] = refs[l][...]

    return pl.pallas_call(body, name=name, grid=(cols // tc,),
                          in_specs=[pl.BlockSpec((rows, tc), lambda i: (0, i))] * n,
                          out_specs=pl.BlockSpec((rows, n, tc), lambda i: (0, 0, i)),
                          out_shape=jax.ShapeDtypeStruct((rows, n, cols), layers[0].dtype),
                          compiler_params=_cp("parallel"))(*layers)


def _adamw_small(ws, gs, ms, vs, *, name):
    n = len(ws)

    def body(*refs):
        w, g, m, v, outs = refs[:n], refs[n:2 * n], refs[2 * n:3 * n], refs[3 * n:4 * n], refs[4 * n:]
        for k in range(n):
            for slot, val in enumerate(_adamw_math(w[k][...], g[k][...], m[k][...], v[k][...])):
                outs[slot * n + k][...] = val

    outs = pl.pallas_call(body, name=name, out_shape=[jax.ShapeDtypeStruct(a.shape, F32) for a in ws] * 3)(
        *ws, *gs, *ms, *vs)
    return outs[:n], outs[n:2 * n], outs[2 * n:]


def _to_kernel_order(wt):
    gates = jnp.pad(wt[2048:2056], ((0, LANE - 2 * A_HEADS), (0, 0)))
    return jnp.concatenate([wt[0:2048], wt[2056:2568], wt[2824:3336], wt[2568:2696], wt[2696:2824], gates], axis=0)


def _from_kernel_order(main, tail):
    return jnp.concatenate([main[0:2048], tail[C_BG - DH_MAIN:C_BG - DH_MAIN + 2 * A_HEADS],
                            main[C_QB:C_QB + B_WIDTH], tail[0:B_KV_WIDTH], tail[B_KV_WIDTH:2 * B_KV_WIDTH],
                            main[C_ZB:C_ZB + B_WIDTH]], axis=0)


def _gate_params(a_log, dt_bias):
    return jnp.pad(jnp.stack([a_log, dt_bias]), ((0, SUBLANE - 2), (A_HEADS, LANE - 2 * A_HEADS)))


SMALL = ("conv_w", "a_log", "dt_bias", "norm_w", "sinks", "ln_g", "ln_b")


def _pack(parts, cols):
    flat = jnp.concatenate([p.reshape(-1) for p in parts])
    rows = -(-flat.shape[0] // cols)
    return jnp.pad(flat, (0, rows * cols - flat.shape[0])).reshape(rows, cols)


def _unpack(packed, shapes):
    flat = packed.reshape(-1)
    out, at = [], 0
    for s in shapes:
        n = math.prod(s)
        out.append(flat[at:at + n].reshape(s))
        at += n
    return out


def kernel(x, w_in, conv_w, a_log, dt_bias, norm_w, sinks, w_out, ln_g, ln_b, loss_target, m_w_in, m_conv_w, m_a_log, m_dt_bias, m_norm_w, m_sinks, m_w_out, m_ln_g, m_ln_b, v_w_in, v_conv_w, v_a_log, v_dt_bias, v_norm_w, v_sinks, v_w_out, v_ln_g, v_ln_b):
    xi, yi, ci = lax.axis_index("x"), lax.axis_index("y"), lax.axis_index("c")
    me = 2 * xi + yi

    to_t = lambda a: jnp.transpose(a, (2, 0, 1))
    from_t = lambda a: jnp.transpose(a, (1, 2, 0))

    wt_shard = to_t(w_in)

    def pack_weights(l):
        rows = jnp.pad(wt_shard[:, l], ((0, IN_PAD - IN_SHARD), (0, 0)))
        return jnp.concatenate([rows, w_out[l]], axis=0).astype(BF16)

    pack0, pack1 = pack_weights(0), pack_weights(1)
    got_in0, g_conv = _gather_two_level(pack0[:IN_PAD], conv_w, name="gather_weights_0")
    conv_full = jnp.moveaxis(g_conv, 0, 2).reshape(DEPTH, CONV_K, 3 * A_WIDTH)
    piece = IN_PAD // 3
    carriers = ("dn_pre", "dn_wy", "dn_scan")
    gathers = {nm: _gather_exchange([pack1[i * piece:(i + 1) * piece]]) for i, nm in enumerate(carriers)}
    gathers.update(in_proj=_gather_exchange([pack0[IN_PAD:]]), swa=_gather_exchange([pack1[IN_PAD:]]))
    w_in_of = lambda rows: _to_kernel_order(rows[:, :IN_SHARD].reshape(IN_COLS, D_MODEL))
    w_out_of = lambda rows: rows.reshape(D_MODEL, D_MODEL)
    args0 = _layer_args(w_in_of(got_in0), conv_full[0], a_log[0], dt_bias[0], sinks[0], norm_w[0],
                        lambda got: w_out_of(got[0]))

    def args1(got):
        rows = jnp.concatenate([got[nm][0] for nm in carriers], axis=1)
        return _layer_args(w_in_of(rows), conv_full[1], a_log[1], dt_bias[1], sinks[1], norm_w[1],
                           w_out_of(got["swa"][0]))

    def pack_grads(g):
        gin = _from_kernel_order(*g["w_in"]).reshape(N_SHARD, IN_SHARD, D_MODEL)
        gin = jnp.pad(gin, ((0, 0), (0, IN_PAD - IN_SHARD), (0, 0)))
        return jnp.concatenate([gin, g["w_out"].reshape(N_SHARD, OUT_SHARD, D_MODEL)], axis=1).astype(BF16)

    packed = {}

    def reduce1(grads1):
        packed[1] = pack_grads(grads1)
        return _reduce_scatter_exchange(packed[1])

    def reduce0(grads0, grads1, loss_tile):
        g0 = pack_grads(grads0)
        from_sibling = _run_exchange(_pair_window_exchange(g0), name="pair_reduce_0")[0]
        packed[0] = _pair_add(g0, from_sibling, ci, name="pair_add_0")
        gsmall = _pack([jnp.stack([g[nm] for g in (grads0, grads1)]) for nm in SMALL] + [loss_tile[0, 0:1]], D_MODEL)
        return _chip_scatter_exchange(packed[0], gsmall)

    _, dx, grads, landed1, (landed0, landed_small) = _local_step(
        x[0], loss_target[0], args0, args1, ln_g, ln_b, gathers=gathers, reduce1=reduce1, reduce0=reduce0)

    small_shapes = [(DEPTH,) + grads[0][nm].shape for nm in SMALL]
    halves = [_sum_chips(packed[0], landed0, me, tc=2 * LANE, name="reduce_sum_0"),
              _sum_scatter(packed[1], landed1[0], me, ci, tc=2 * LANE, name="reduce_sum_1")]
    s_small = _sum_slots(landed_small, name="reduce_sum_small")
    others = _run_exchange(_share_exchange(halves), name="pair_share")
    full = [jnp.where(ci == 0, jnp.concatenate([mine, other], axis=1), jnp.concatenate([other, mine], axis=1))
            for mine, other in zip(halves, others)]
    grad_in_layers = [f[:IN_SHARD] for f in full]
    grad_out = jnp.stack([f[IN_PAD:] for f in full])
    out_blk = (1, OUT_SHARD, D_MODEL)
    *small_grads, loss = _unpack(s_small, small_shapes + [()])
    gs = dict(zip(SMALL, small_grads))
    gs["conv_w"] = lax.dynamic_slice_in_dim(gs["conv_w"], me * CONV_SHARD, CONV_SHARD, axis=2)

    grad_in_t = _interleave_layers(grad_in_layers, tc=2 * LANE, name="grad_in_layers")
    d_in, nm_in, nv_in = (from_t(o) for o in _adamw(to_t(w_in), grad_in_t, to_t(m_w_in), to_t(v_w_in),
                                                    (IN_SHARD // 6, DEPTH, D_MODEL), name="adamw_in"))
    grad_in = from_t(grad_in_t)
    d_out, nm_out, nv_out = _adamw(w_out, grad_out, m_w_out, v_w_out, out_blk, name="adamw_out")
    ws = dict(conv_w=conv_w, a_log=a_log, dt_bias=dt_bias, norm_w=norm_w, sinks=sinks, ln_g=ln_g, ln_b=ln_b)
    ms = dict(conv_w=m_conv_w, a_log=m_a_log, dt_bias=m_dt_bias, norm_w=m_norm_w, sinks=m_sinks, ln_g=m_ln_g, ln_b=m_ln_b)
    vs = dict(conv_w=v_conv_w, a_log=v_a_log, dt_bias=v_dt_bias, norm_w=v_norm_w, sinks=v_sinks, ln_g=v_ln_g, ln_b=v_ln_b)
    d_s, nm_s, nv_s = (dict(zip(SMALL, o)) for o in _adamw_small(*[[d[nm] for nm in SMALL] for d in (ws, gs, ms, vs)],
                                                                 name="adamw_small"))

    def in_order(big_in, small, big_out):
        return (big_in, small["conv_w"], small["a_log"], small["dt_bias"], small["norm_w"], small["sinks"], big_out,
                small["ln_g"], small["ln_b"])

    return (loss, dx[None], *in_order(grad_in, gs, grad_out), *in_order(d_in, d_s, d_out),
            *in_order(nm_in, nm_s, nm_out), *in_order(nv_in, nv_s, nv_out))
```

```python
import math

import jax
import jax.numpy as jnp
from jax import lax
from jax.experimental import pallas as pl
from jax.experimental.pallas import tpu as pltpu

F32 = jnp.float32
BF16 = jnp.bfloat16
HI = lax.Precision.HIGHEST

D_MODEL = 1024
DEPTH = 2
A_HEADS = 4
A_HEAD_DIM = 128
A_WIDTH = 512
CONV_K = 4
CHUNK = 64
B_Q_HEADS = 8
B_KV_HEADS = 2
B_HEAD_DIM = 64
B_GROUP = 4
B_WIDTH = 512
B_KV_WIDTH = 128
BLOCK = 128
IN_COLS = 3336
DEEPNORM_ALPHA = (2 * DEPTH) ** 0.25
LN_EPS = 1e-5
RMS_EPS = 1e-6
L2_EPS = 1e-6
ADAM_LR = 0.001
ADAM_B1 = 0.9
ADAM_B2 = 0.999
ADAM_EPS = 1e-08
ADAM_WD = 0.01
ADAM_STEP = 10

N_SHARD = 4
IN_SHARD = IN_COLS // N_SHARD
OUT_SHARD = D_MODEL // N_SHARD
CONV_SHARD = 3 * A_WIDTH // N_SHARD
IN_PAD = -(-IN_SHARD // 96) * 96

P_COLS = 3456
C_PRE = 0
C_ZA = 1536
C_QB = 2048
C_ZB = 2560
C_KB = 3072
C_VB = 3200
C_BG = 3328
DH_MAIN = C_KB
LANE = 128
SUBLANE = 8
HALO = 16
VMEM_LIMIT = 56 * 1024 * 1024
ALIBI = tuple(2.0 ** (-8.0 * (h + 1) / B_Q_HEADS) for h in range(B_Q_HEADS))
NEG = -1e30


def _cp(*sem):
    return pltpu.CompilerParams(dimension_semantics=sem, vmem_limit_bytes=VMEM_LIMIT)


def _dot(a, b):
    return jnp.dot(a.astype(BF16), b.astype(BF16), preferred_element_type=F32)


def _dot_nt(a, b):
    return lax.dot_general(a.astype(BF16), b.astype(BF16), (((1,), (1,)), ((), ())),
                           preferred_element_type=F32)


def _dot_tn(a, b):
    return lax.dot_general(a.astype(BF16), b.astype(BF16), (((0,), (0,)), ((), ())),
                           preferred_element_type=F32)


def _dot_hi(a, b):
    return jnp.dot(a, b, precision=HI, preferred_element_type=F32)


def _sigmoid(x):
    return jax.nn.sigmoid(x)


def _silu(x):
    return x * _sigmoid(x)


def _dsilu(x):
    s = _sigmoid(x)
    return s * (1.0 + x * (1.0 - s))


def _softplus(x):
    return jnp.maximum(x, 0.0) + jnp.log(1.0 + jnp.exp(-jnp.abs(x)))


def _shift_down(cur, before, s):
    if s == 0:
        return cur
    r = pltpu.roll(cur, s, 0)
    rb = pltpu.roll(before, s, 0)
    row = lax.broadcasted_iota(jnp.int32, before.shape, 0)
    head = jnp.where(row < s, rb, r[0:SUBLANE])
    return jnp.concatenate([head, r[SUBLANE:]], axis=0)


def _shift_up(cur, after, s):
    if s == 0:
        return cur
    n = cur.shape[0]
    r = pltpu.roll(cur, n - s, 0)
    ra = pltpu.roll(after, SUBLANE - s, 0)
    row = lax.broadcasted_iota(jnp.int32, after.shape, 0)
    tail = jnp.where(row >= SUBLANE - s, ra, r[n - SUBLANE:])
    return jnp.concatenate([r[:n - SUBLANE], tail], axis=0)


def _conv_fwd(cur, before, w):
    acc = cur * w[CONV_K - 1:CONV_K, :]
    for s in range(1, CONV_K):
        acc = acc + _shift_down(cur, before, s) * w[CONV_K - 1 - s:CONV_K - s, :]
    return acc


def _matmul_nt(a, bt, *, tm, name, carry=None):
    m, k = a.shape
    n = bt.shape[0]
    c_ins, c_in_specs, c_out_specs, c_outs, c_scratch = _carry_specs(carry)

    def body(*refs):
        a_ref, b_ref, o_ref = _carried(carry, refs, 2, 1, m // tm)
        o_ref[...] = _dot_nt(a_ref[...], b_ref[...]).astype(o_ref.dtype)

    outs = pl.pallas_call(
        body, name=name, grid=(m // tm,),
        in_specs=[pl.BlockSpec((tm, k), lambda i: (i, 0)), pl.BlockSpec((n, k), lambda i: (0, 0))] + c_in_specs,
        out_specs=[pl.BlockSpec((tm, n), lambda i: (i, 0))] + c_out_specs,
        out_shape=[jax.ShapeDtypeStruct((m, n), BF16)] + c_outs,
        scratch_shapes=c_scratch,
        compiler_params=_cp("arbitrary"))(a, bt, *c_ins)
    return outs[0], outs[1:]


def _dn_pre(h, conv_w, par, *, tt, name, carry=None):
    t = h.shape[0]
    cw = 3 * A_WIDTH
    hb = tt // HALO

    c_ins, c_in_specs, c_out_specs, c_outs, c_scratch = _carry_specs(carry)

    def body(*refs):
        (pre_ref, halo_ref, bgi_ref, cw_ref, par_ref,
         q_ref, k_ref, v_ref, bg_ref, bgt_ref) = _carried(carry, refs, 5, 5, t // tt)
        i = pl.program_id(0)
        cur = pre_ref[...].astype(F32)
        before = jnp.where(i > 0, halo_ref[...].astype(F32)[HALO - SUBLANE:], 0.0)
        s = _silu(_conv_fwd(cur, before, cw_ref[...]))
        for hd in range(A_HEADS):
            sl = slice(hd * LANE, (hd + 1) * LANE)
            tq = s[:, hd * LANE:(hd + 1) * LANE]
            q_ref[:, sl] = tq * (lax.rsqrt(jnp.sum(tq * tq, -1, keepdims=True) + L2_EPS) * (A_HEAD_DIM ** -0.5))
            tk = s[:, A_WIDTH + hd * LANE:A_WIDTH + (hd + 1) * LANE]
            k_ref[:, sl] = tk * lax.rsqrt(jnp.sum(tk * tk, -1, keepdims=True) + L2_EPS)
        v_ref[...] = s[:, 2 * A_WIDTH:]
        raw = bgi_ref[...].astype(F32)
        lane = lax.broadcasted_iota(jnp.int32, raw.shape, 1)
        is_a = (lane >= A_HEADS) & (lane < 2 * A_HEADS)
        g = jnp.where(is_a, -jnp.exp(par_ref[0:1, :]) * _softplus(raw + par_ref[1:2, :]), 0.0)
        gc = _dot_hi(_chunk_tri(tt, lower=True), g)
        bg = jnp.where(lane < A_HEADS, _sigmoid(raw), gc)
        bg_ref[...] = bg
        bgt_ref[...] = jnp.transpose(bg)[0:SUBLANE, :]

    wide = jax.ShapeDtypeStruct((t, A_WIDTH), F32)
    outs = pl.pallas_call(
        body, name=name, grid=(t // tt,),
        in_specs=[pl.BlockSpec((tt, cw), lambda i: (i, 0)),
                  pl.BlockSpec((HALO, cw), lambda i: (jnp.maximum(i * hb - 1, 0), 0)),
                  pl.BlockSpec((tt, LANE), lambda i: (i, C_BG // LANE)),
                  pl.BlockSpec((CONV_K, cw), lambda i: (0, 0)),
                  pl.BlockSpec((SUBLANE, LANE), lambda i: (0, 0))] + c_in_specs,
        out_specs=[pl.BlockSpec((tt, A_WIDTH), lambda i: (i, 0))] * 3
        + [pl.BlockSpec((tt, LANE), lambda i: (i, 0)), pl.BlockSpec((SUBLANE, tt), lambda i: (0, i))] + c_out_specs,
        out_shape=[wide, wide, wide, jax.ShapeDtypeStruct((t, LANE), F32),
                   jax.ShapeDtypeStruct((SUBLANE, t), F32)] + c_outs,
        scratch_shapes=c_scratch,
        compiler_params=_cp("arbitrary"))(h, h, h, conv_w, par, *c_ins)
    return outs[:5], outs[5:]


def _chunk_tri(n, lower):
    r = lax.broadcasted_iota(jnp.int32, (n, n), 0)
    c = lax.broadcasted_iota(jnp.int32, (n, n), 1)
    shift = CHUNK.bit_length() - 1
    same = jnp.right_shift(r, shift) == jnp.right_shift(c, shift)
    return (same & ((c <= r) if lower else (c >= r))).astype(F32)


def _chunk_masks():
    r = lax.broadcasted_iota(jnp.int32, (CHUNK, CHUNK), 0)
    c = lax.broadcasted_iota(jnp.int32, (CHUNK, CHUNK), 1)
    return r >= c, r > c, r == c


def _split(a):
    hi = a.astype(BF16)
    return hi, (a - hi.astype(F32)).astype(BF16)


def _dot3(a, b):
    (ah, al), (bh, bl) = a, b
    d = lambda p, q: jnp.dot(p, q, preferred_element_type=F32)
    return d(ah, bh) + (d(ah, bl) + d(al, bh))


def _tri_inv_many(a_list, eye):
    d = lambda p, q: jnp.dot(p, q, preferred_element_type=F32)
    p = [(-a).astype(BF16) for a in a_list]
    tm = [eye - a for a in a_list]
    for _ in range(5):
        pf = [d(pi, pi) for pi in p]
        p = [x.astype(BF16) for x in pf]
        tm = [t + d(t.astype(BF16), pi) for t, pi in zip(tm, p)]
    ms = [_split(eye + a) for a in a_list]
    res = [eye - _dot3(m, _split(t)) for m, t in zip(ms, tm)]
    return [t + d(t.astype(BF16), r.astype(BF16)) for t, r in zip(tm, res)]


def _chunk_gates(bg_v, bgt_v, hd):
    return (bg_v[:, hd:hd + 1], bg_v[:, A_HEADS + hd:A_HEADS + hd + 1],
            None if bgt_v is None else bgt_v[A_HEADS + hd:A_HEADS + hd + 1, :])


WY_ROWS = 512
SCAN_ROWS = 512
WY_GROUP = 8


def _dn_wy(q, k, v, bg, bgt, *, name, carry=None):
    t = q.shape[0]
    rows = WY_ROWS

    c_ins, c_in_specs, c_out_specs, c_outs, c_scratch = _carry_specs(carry)

    def body(*refs):
        q_ref, k_ref, v_ref, bg_ref, bgt_ref, u_ref, w_ref, tm_ref, qk_ref = _carried(carry, refs, 5, 4, t // rows)
        causal, strict, diag = _chunk_masks()
        eye = diag.astype(F32)
        for c0 in range(0, rows // CHUNK, WY_GROUP):
            items = [(c, hd) for c in range(c0, c0 + WY_GROUP) for hd in range(A_HEADS)]
            rs = lambda c: slice(c * CHUNK, (c + 1) * CHUNK)
            sl = lambda hd: slice(hd * LANE, (hd + 1) * LANE)
            hs = lambda hd: slice(hd * CHUNK, (hd + 1) * CHUNK)
            gates = [_chunk_gates(bg_ref[rs(c), :], bgt_ref[:, rs(c)], hd) for c, hd in items]
            dms = [jnp.exp(jnp.where(causal, gcol - grow, NEG)) for _, gcol, grow in gates]
            kbs = [k_ref[rs(c), sl(hd)] * g[0] for (c, hd), g in zip(items, gates)]
            a_list = [jnp.where(strict, _dot_nt(kb, k_ref[rs(c), sl(hd)]) * dm, 0.0)
                      for (c, hd), kb, dm in zip(items, kbs, dms)]
            for (c, hd), dm in zip(items, dms):
                qk_ref[rs(c), hs(hd)] = jnp.where(
                    causal, _dot_nt(q_ref[rs(c), sl(hd)], k_ref[rs(c), sl(hd)]) * dm, 0.0)
            tms = _tri_inv_many(a_list, eye)
            for (c, hd), g, kb, tmat in zip(items, gates, kbs, tms):
                tm_ref[rs(c), hs(hd)] = tmat
                u_ref[rs(c), sl(hd)] = _dot(tmat, v_ref[rs(c), sl(hd)] * g[0])
                w_ref[rs(c), sl(hd)] = _dot(tmat, kb * jnp.exp(g[1])).astype(BF16)

    blk = pl.BlockSpec((rows, A_WIDTH), lambda i: (i, 0))
    half = pl.BlockSpec((rows, A_HEADS * CHUNK), lambda i: (i, 0))
    outs = pl.pallas_call(
        body, name=name, grid=(t // rows,),
        in_specs=[blk, blk, blk, pl.BlockSpec((rows, LANE), lambda i: (i, 0)),
                  pl.BlockSpec((SUBLANE, rows), lambda i: (0, i))] + c_in_specs,
        out_specs=[blk, blk, half, half] + c_out_specs,
        out_shape=[jax.ShapeDtypeStruct((t, A_WIDTH), F32), jax.ShapeDtypeStruct((t, A_WIDTH), BF16),
                   jax.ShapeDtypeStruct((t, A_HEADS * CHUNK), F32),
                   jax.ShapeDtypeStruct((t, A_HEADS * CHUNK), F32)] + c_outs,
        scratch_shapes=c_scratch,
        compiler_params=_cp("arbitrary"))(q, k, v, bg, bgt, *c_ins)
    return outs[:4], outs[4:]


def _dn_scan_fwd(q, k, u, w, qk, bg, *, name, carry=None):
    t = q.shape[0]
    rows = SCAN_ROWS
    per = rows // CHUNK
    c_ins, c_in_specs, c_out_specs, c_outs, c_scratch = _carry_specs(carry)

    def body(*refs):
        q_ref, k_ref, u_ref, w_ref, qk_ref, bg_ref, o_ref, vn_ref, s_ref, state = _carried(carry, refs, 6, 3, t // rows)

        @pl.when(pl.program_id(0) == 0)
        def _():
            state[...] = jnp.zeros_like(state)

        heads = range(A_HEADS)
        sl = lambda hd: slice(hd * LANE, (hd + 1) * LANE)
        s_cur = [state[hd] for hd in heads]
        for c in range(per):
            rs = slice(c * CHUNK, (c + 1) * CHUNK)
            bg_v = bg_ref[rs, :]
            gcols = [_chunk_gates(bg_v, None, hd)[1] for hd in heads]
            glasts = [gc[CHUNK - 1:CHUNK, :] for gc in gcols]
            for hd in heads:
                s_ref[c, hd] = s_cur[hd].astype(BF16)
            vns = [u_ref[rs, sl(hd)] - _dot(w_ref[rs, sl(hd)], s_cur[hd]) for hd in heads]
            qss = [_dot(q_ref[rs, sl(hd)] * jnp.exp(gcols[hd]), s_cur[hd]) for hd in heads]
            s_cur = [s_cur[hd] * jnp.exp(glasts[hd])
                     + _dot_tn(k_ref[rs, sl(hd)] * jnp.exp(glasts[hd] - gcols[hd]), vns[hd]) for hd in heads]
            for hd in heads:
                vn_ref[rs, sl(hd)] = vns[hd]
                o_ref[rs, sl(hd)] = qss[hd] + _dot(qk_ref[rs, hd * CHUNK:(hd + 1) * CHUNK], vns[hd])
        for hd in heads:
            state[hd] = s_cur[hd]

    blk = pl.BlockSpec((rows, A_WIDTH), lambda i: (i, 0))
    half = pl.BlockSpec((rows, A_HEADS * CHUNK), lambda i: (i, 0))
    wide = jax.ShapeDtypeStruct((t, A_WIDTH), F32)
    outs = pl.pallas_call(
        body, name=name, grid=(t // rows,),
        in_specs=[blk, blk, blk, blk, half, pl.BlockSpec((rows, LANE), lambda i: (i, 0))] + c_in_specs,
        out_specs=[blk, blk, pl.BlockSpec((per, A_HEADS, LANE, LANE), lambda i: (i, 0, 0, 0))] + c_out_specs,
        out_shape=[wide, wide, jax.ShapeDtypeStruct((t // CHUNK, A_HEADS, LANE, LANE), BF16)] + c_outs,
        scratch_shapes=[pltpu.VMEM((A_HEADS, LANE, LANE), F32)] + c_scratch,
        compiler_params=_cp("arbitrary"))(q, k, u, w, qk, bg, *c_ins)
    return outs[:3], outs[3:]


def _swa_neg_dist(n_blk):
    qi = lax.broadcasted_iota(jnp.int32, (BLOCK, 2 * BLOCK), 0)
    si = lax.broadcasted_iota(jnp.int32, (BLOCK, 2 * BLOCK), 1)
    dist = qi + BLOCK - si
    mask = (dist >= 0) & (dist < BLOCK) & ((si >= BLOCK) | (n_blk > 0))
    return jnp.where(mask, -dist.astype(F32), NEG)


def _stack_heads(ref, hk):
    return jnp.concatenate([ref[:, h * B_HEAD_DIM:(h + 1) * B_HEAD_DIM].astype(F32)
                            for h in range(hk * B_GROUP, (hk + 1) * B_GROUP)], axis=0)


def _swa_group_probs(q_ref, sk_ref, kband, vband, neg_dist):
    hks = range(B_KV_HEADS)
    heads = lambda hk: range(hk * B_GROUP, (hk + 1) * B_GROUP)
    ksl = lambda hk: slice(hk * B_HEAD_DIM, (hk + 1) * B_HEAD_DIM)
    ones = jnp.ones((2 * BLOCK, B_HEAD_DIM), BF16)
    qs = [_stack_heads(q_ref, hk) * (B_HEAD_DIM ** -0.5) for hk in hks]
    sink = [jnp.concatenate([jnp.broadcast_to(sk_ref[h:h + 1, 0:1], (BLOCK, 1)) for h in heads(hk)], axis=0)
            for hk in hks]
    s = [_dot_nt(qs[hk], kband[:, ksl(hk)]) + jnp.concatenate([ALIBI[h] * neg_dist for h in heads(hk)], axis=0)
         for hk in hks]
    m = [jnp.maximum(jnp.max(s[hk], axis=-1, keepdims=True), sink[hk]) for hk in hks]
    p = [jnp.exp(s[hk] - m[hk]) for hk in hks]
    oe = [jnp.dot(p[hk].astype(BF16), jnp.concatenate([vband[:, ksl(hk)].astype(BF16), ones], axis=1),
                  preferred_element_type=F32) for hk in hks]
    ps = [jnp.exp(sink[hk] - m[hk]) for hk in hks]
    inv = [1.0 / (oe[hk][:, B_HEAD_DIM:B_HEAD_DIM + 1] + ps[hk]) for hk in hks]
    return [(qs[hk], p[hk] * inv[hk], ps[hk] * inv[hk], oe[hk][:, :B_HEAD_DIM] * inv[hk]) for hk in hks]


def _swa_specs():
    qspec = lambda c0: pl.BlockSpec((BLOCK, B_WIDTH), lambda i: (i, c0 // B_WIDTH))
    cur = lambda c0: pl.BlockSpec((BLOCK, LANE), lambda i: (i, c0 // LANE))
    prev = lambda c0: pl.BlockSpec((BLOCK, LANE), lambda i: (jnp.maximum(i - 1, 0), c0 // LANE))
    return qspec, cur, prev


def _carried(carry, refs, n_in, n_out, steps):
    if carry is None:
        return refs
    ci, co = len(carry.ins), len(carry.outs)
    own = refs[:n_in] + refs[n_in + ci:n_in + ci + n_out] + refs[n_in + ci + n_out + co:len(refs) - 3]
    parts = refs[n_in:n_in + ci], refs[n_in + ci + n_out:n_in + ci + n_out + co], refs[len(refs) - 3:]

    @pl.when(pl.program_id(0) == 0)
    def _():
        carry.start(*parts)

    @pl.when(pl.program_id(0) == steps - 1)
    def _():
        carry.finish(*parts)

    return own


def _carry_specs(carry):
    if carry is None:
        return [], [], [], [], []
    return (list(carry.ins), [_ANY] * len(carry.ins), [_ANY] * len(carry.outs), list(carry.outs), carry.scratch())


def _swa_fwd(h, sinks_b, *, name, carry=None):
    t = h.shape[0]
    qspec, cur, prev = _swa_specs()
    c_ins, c_in_specs, c_out_specs, c_outs, c_scratch = _carry_specs(carry)

    def body(*refs):
        q_ref, kc_ref, kp_ref, vc_ref, vp_ref, sk_ref, o_ref = _carried(carry, refs, 6, 1, t // BLOCK)
        n_blk = pl.program_id(0)
        kband = jnp.concatenate([kp_ref[...], kc_ref[...]], axis=0)
        vband = jnp.concatenate([vp_ref[...], vc_ref[...]], axis=0)
        groups = _swa_group_probs(q_ref, sk_ref, kband, vband, _swa_neg_dist(n_blk))
        for hk, (_, _, _, o) in enumerate(groups):
            for g in range(B_GROUP):
                hq = hk * B_GROUP + g
                o_ref[:, hq * B_HEAD_DIM:(hq + 1) * B_HEAD_DIM] = o[g * BLOCK:(g + 1) * BLOCK]

    outs = pl.pallas_call(
        body, name=name, grid=(t // BLOCK,),
        in_specs=[qspec(C_QB), cur(C_KB), prev(C_KB), cur(C_VB), prev(C_VB),
                  pl.BlockSpec((B_Q_HEADS, LANE), lambda i: (0, 0))] + c_in_specs,
        out_specs=[pl.BlockSpec((BLOCK, B_WIDTH), lambda i: (i, 0))] + c_out_specs,
        out_shape=[jax.ShapeDtypeStruct((t, B_WIDTH), F32)] + c_outs,
        scratch_shapes=c_scratch,
        compiler_params=_cp("arbitrary"))(h, h, h, h, h, sinks_b, *c_ins)
    return outs[0], outs[1:]


def _rms_gate(o, za, nw):
    outs = []
    for hd in range(A_HEADS):
        oh = o[:, hd * LANE:(hd + 1) * LANE]
        r = lax.rsqrt(jnp.mean(oh * oh, -1, keepdims=True) + RMS_EPS)
        outs.append(oh * r * nw)
    return jnp.concatenate(outs, axis=1) * _silu(za)


def _out_ln(x, oa, ob, h, norm_w, w_out, ln_g, ln_b, *, tm, name, target=None):
    t = x.shape[0]
    last = target is not None

    def body(*refs):
        x_ref, oa_ref, ob_ref, za_ref, zb_ref, nw_ref, w_ref, g_ref, b_ref = refs[:9]
        xn_ref, mx_ref, r_ref = refs[9 + last:12 + last]
        ya = _rms_gate(oa_ref[...], za_ref[...].astype(F32), nw_ref[...])
        yb = ob_ref[...] * _silu(zb_ref[...].astype(F32))
        mixed = jnp.concatenate([ya, yb], axis=1).astype(BF16)
        mx_ref[...] = mixed
        r = DEEPNORM_ALPHA * x_ref[...] + jnp.dot(mixed, w_ref[...], preferred_element_type=F32)
        r_ref[...] = r
        mu = jnp.mean(r, -1, keepdims=True)
        xc = r - mu
        var = jnp.mean(xc * xc, -1, keepdims=True)
        xn = xc * lax.rsqrt(var + LN_EPS) * g_ref[...] + b_ref[...]
        if not last:
            xn_ref[...] = xn
            return
        loss_ref = refs[13]

        @pl.when(pl.program_id(0) == 0)
        def _():
            loss_ref[...] = jnp.zeros_like(loss_ref)

        err = xn - refs[9][...]
        xn_ref[...] = err * (1.0 / D_MODEL)
        loss_ref[...] += 0.5 / D_MODEL * jnp.sum(err * err)

    row = lambda w, c: pl.BlockSpec((tm, w), lambda i: (i, c))
    full = lambda a, b: pl.BlockSpec((a, b), lambda i: (0, 0))
    wide = jax.ShapeDtypeStruct((t, D_MODEL), F32)
    return pl.pallas_call(
        body, name=name, grid=(t // tm,),
        in_specs=[row(D_MODEL, 0), row(A_WIDTH, 0), row(B_WIDTH, 0), row(A_WIDTH, C_ZA // A_WIDTH),
                  row(B_WIDTH, C_ZB // B_WIDTH), full(1, LANE), full(D_MODEL, D_MODEL), full(1, D_MODEL),
                  full(1, D_MODEL)] + [row(D_MODEL, 0)] * last,
        out_specs=[row(D_MODEL, 0), row(D_MODEL, 0), row(D_MODEL, 0)] + [full(SUBLANE, LANE)] * last,
        out_shape=[wide, jax.ShapeDtypeStruct((t, D_MODEL), BF16), wide]
        + [jax.ShapeDtypeStruct((SUBLANE, LANE), F32)] * last,
        compiler_params=_cp("arbitrary" if last else "parallel"))(
        x, oa, ob, h, h, norm_w, w_out, ln_g, ln_b, *([target] if last else []))


def _layer_fwd(x, wt, conv_w, par, sinks_b, norm_w, w_out_bf, ln_g, ln_b, l, carries=None, target=None):
    carries = carries or {}
    h, got_in = _matmul_nt(x, wt, tm=512, name=f"in_proj_{l}", carry=carries.get("in_proj"))
    if callable(w_out_bf):
        w_out_bf = w_out_bf(got_in)
    (q, k, v, bg, bgt), got_pre = _dn_pre(h, conv_w, par, tt=512, name=f"dn_pre_{l}", carry=carries.get("dn_pre"))
    (u, w, tmat, qk), got_wy = _dn_wy(q, k, v, bg, bgt, name=f"dn_wy_{l}", carry=carries.get("dn_wy"))
    (oa, vn, s_all), got_scan = _dn_scan_fwd(q, k, u, w, qk, bg, name=f"dn_scan_{l}", carry=carries.get("dn_scan"))
    ob, got_swa = _swa_fwd(h, sinks_b, name=f"swa_fwd_{l}", carry=carries.get("swa"))
    xn, mixed, r, *loss = _out_ln(x, oa, ob, h, norm_w, w_out_bf, ln_g, ln_b, tm=512, name=f"out_ln_{l}", target=target)
    if loss:
        xn = (xn, loss[0])
    res = dict(x=x, h=h, q=q, k=k, v=v, bg=bg, bgt=bgt, w=w, tmat=tmat, qk=qk, vn=vn, oa=oa, s_all=s_all,
               mixed=mixed, r=r, w_out=w_out_bf)
    return xn, res, dict(in_proj=got_in, dn_pre=got_pre, dn_wy=got_wy, dn_scan=got_scan, swa=got_swa)


def _ln_out_bwd(dxn, r, mixed, ln_g, w_out, *, tm, name):
    t = dxn.shape[0]

    def body(dxn_ref, r_ref, mx_ref, g_ref, w_ref, dr_ref, dm_ref, dw_ref, dg_ref, db_ref):
        @pl.when(pl.program_id(0) == 0)
        def _():
            dw_ref[...] = jnp.zeros_like(dw_ref)
            dg_ref[...] = jnp.zeros_like(dg_ref)
            db_ref[...] = jnp.zeros_like(db_ref)

        rr = r_ref[...]
        xc = rr - jnp.mean(rr, -1, keepdims=True)
        rstd = lax.rsqrt(jnp.mean(xc * xc, -1, keepdims=True) + LN_EPS)
        xhat = xc * rstd
        dxn_v = dxn_ref[...]
        dxh = dxn_v * g_ref[...]
        dr = rstd * (dxh - jnp.mean(dxh, -1, keepdims=True) - xhat * jnp.mean(dxh * xhat, -1, keepdims=True))
        dr_ref[...] = dr
        dg_ref[...] += jnp.sum(dxn_v * xhat, axis=0, keepdims=True)
        db_ref[...] += jnp.sum(dxn_v, axis=0, keepdims=True)
        drb = dr.astype(BF16)
        dm_ref[...] = _dot_nt(drb, w_ref[...])
        dw_ref[...] += _dot_tn(mx_ref[...], drb)

    row = pl.BlockSpec((tm, D_MODEL), lambda i: (i, 0))
    full = lambda a, b: pl.BlockSpec((a, b), lambda i: (0, 0))
    big = jax.ShapeDtypeStruct((t, D_MODEL), F32)
    vec = jax.ShapeDtypeStruct((1, D_MODEL), F32)
    return pl.pallas_call(
        body, name=name, grid=(t // tm,),
        in_specs=[row, row, row, full(1, D_MODEL), full(D_MODEL, D_MODEL)],
        out_specs=[row, row, full(D_MODEL, D_MODEL), full(1, D_MODEL), full(1, D_MODEL)],
        out_shape=[big, big, jax.ShapeDtypeStruct((D_MODEL, D_MODEL), F32), vec, vec],
        compiler_params=_cp("arbitrary"))(dxn, r, mixed, ln_g, w_out)


def _dn_post_bwd(dm, oa, h, norm_w, *, tm, name):
    t = oa.shape[0]

    def body(dy_ref, o_ref, za_ref, nw_ref, do_ref, dza_ref, dnw_ref):
        @pl.when(pl.program_id(0) == 0)
        def _():
            dnw_ref[...] = jnp.zeros_like(dnw_ref)

        nw = nw_ref[...]
        dnw = jnp.zeros_like(nw)
        for hd in range(A_HEADS):
            sl = slice(hd * LANE, (hd + 1) * LANE)
            oh, za, dy = o_ref[:, sl], za_ref[:, sl].astype(F32), dy_ref[:, sl]
            rs = lax.rsqrt(jnp.mean(oh * oh, -1, keepdims=True) + RMS_EPS)
            nrm = oh * rs
            dza_ref[:, sl] = dy * nrm * nw * _dsilu(za)
            dn = dy * _silu(za)
            dnw = dnw + jnp.sum(dn * nrm, axis=0, keepdims=True)
            dnn = dn * nw
            do_ref[:, sl] = rs * dnn - oh * (rs * rs * rs) * jnp.mean(dnn * oh, -1, keepdims=True)
        dnw_ref[...] += dnw

    row = lambda c: pl.BlockSpec((tm, A_WIDTH), lambda i: (i, c))
    wide = jax.ShapeDtypeStruct((t, A_WIDTH), F32)
    return pl.pallas_call(
        body, name=name, grid=(t // tm,),
        in_specs=[row(0), row(0), row(C_ZA // A_WIDTH), pl.BlockSpec((1, LANE), lambda i: (0, 0))],
        out_specs=[row(0), row(C_ZA // A_WIDTH), pl.BlockSpec((1, LANE), lambda i: (0, 0))],
        out_shape=[wide, jax.ShapeDtypeStruct((t, DH_MAIN), F32), jax.ShapeDtypeStruct((1, LANE), F32)],
        compiler_params=_cp("arbitrary"))(dm, oa, h, norm_w)


def _dn_scan_bwd(q, k, w, qk, bg, do, *, name):
    t = q.shape[0]
    rows = SCAN_ROWS
    per = rows // CHUNK
    n = t // rows

    def body(q_ref, k_ref, w_ref, qk_ref, bg_ref, do_ref, dvn_ref, ds_ref, dstate):
        @pl.when(pl.program_id(0) == 0)
        def _():
            dstate[...] = jnp.zeros_like(dstate)

        heads = range(A_HEADS)
        sl = lambda hd: slice(hd * LANE, (hd + 1) * LANE)
        ds_cur = [dstate[hd] for hd in heads]
        for c in reversed(range(per)):
            rs = slice(c * CHUNK, (c + 1) * CHUNK)
            bg_v = bg_ref[rs, :]
            gcols = [_chunk_gates(bg_v, None, hd)[1] for hd in heads]
            glasts = [gc[CHUNK - 1:CHUNK, :] for gc in gcols]
            for hd in heads:
                ds_ref[c, hd] = ds_cur[hd].astype(BF16)
            pdo = [_dot_tn(qk_ref[rs, hd * CHUNK:(hd + 1) * CHUNK], do_ref[rs, sl(hd)]) for hd in heads]
            qdo = [_dot_tn(q_ref[rs, sl(hd)] * jnp.exp(gcols[hd]), do_ref[rs, sl(hd)]) for hd in heads]
            dvns = [pdo[hd] + _dot(k_ref[rs, sl(hd)] * jnp.exp(glasts[hd] - gcols[hd]), ds_cur[hd]) for hd in heads]
            ds_cur = [qdo[hd] + jnp.exp(glasts[hd]) * ds_cur[hd] - _dot_tn(w_ref[rs, sl(hd)], dvns[hd])
                      for hd in heads]
            for hd in heads:
                dvn_ref[rs, sl(hd)] = dvns[hd]
        for hd in heads:
            dstate[hd] = ds_cur[hd]

    blk = pl.BlockSpec((rows, A_WIDTH), lambda i: (n - 1 - i, 0))
    return pl.pallas_call(
        body, name=name, grid=(n,),
        in_specs=[blk, blk, blk, pl.BlockSpec((rows, A_HEADS * CHUNK), lambda i: (n - 1 - i, 0)),
                  pl.BlockSpec((rows, LANE), lambda i: (n - 1 - i, 0)), blk],
        out_specs=[blk, pl.BlockSpec((per, A_HEADS, LANE, LANE), lambda i: (n - 1 - i, 0, 0, 0))],
        out_shape=[jax.ShapeDtypeStruct((t, A_WIDTH), F32),
                   jax.ShapeDtypeStruct((t // CHUNK, A_HEADS, LANE, LANE), BF16)],
        scratch_shapes=[pltpu.VMEM((A_HEADS, LANE, LANE), F32)],
        compiler_params=_cp("arbitrary"))(q, k, w, qk, bg, do)


def _dn_chunk_bwd(q, k, v, vn, tmat, qk, bg, bgt, s_all, ds_all, dvn, do, *, name):
    t = q.shape[0]
    rows = WY_ROWS
    per = rows // CHUNK

    def body(q_ref, k_ref, v_ref, vn_ref, tm_ref, qk_ref, bg_ref, bgt_ref, s_ref, ds_ref, dvn_ref, do_ref,
             dq_ref, dk_ref, dv_ref, dbg_ref, dbgt_ref):
        causal, strict, _ = _chunk_masks()
        lane = lax.broadcasted_iota(jnp.int32, (CHUNK, LANE), 1)
        rowi = lax.broadcasted_iota(jnp.int32, (CHUNK, 1), 0)
        sub = lax.broadcasted_iota(jnp.int32, (SUBLANE, CHUNK), 0)
        rs = lambda c: slice(c * CHUNK, (c + 1) * CHUNK)
        sl = lambda hd: slice(hd * LANE, (hd + 1) * LANE)
        hs = lambda hd: slice(hd * CHUNK, (hd + 1) * CHUNK)
        for c0 in range(0, per, WY_GROUP):
            items = [(c, hd) for c in range(c0, c0 + WY_GROUP) for hd in range(A_HEADS)]
            at = lambda ref: [ref[rs(c), sl(hd)] for c, hd in items]
            qs, ks, vs, dos, vns, dvns = at(q_ref), at(k_ref), at(v_ref), at(do_ref), at(vn_ref), at(dvn_ref)
            tmhs = [tm_ref[rs(c), hs(hd)] for c, hd in items]
            ps = [qk_ref[rs(c), hs(hd)] for c, hd in items]
            gates = [_chunk_gates(bg_ref[rs(c), :], bgt_ref[:, rs(c)], hd) for c, hd in items]
            betas = [g[0] for g in gates]
            gcols = [g[1] for g in gates]
            dmats = [jnp.exp(jnp.where(causal, g[1] - g[2], NEG)) for g in gates]
            es = [jnp.exp(gc) for gc in gcols]
            glasts = [gc[CHUNK - 1:CHUNK, :] for gc in gcols]
            eks = [jnp.exp(gl - gc) for gl, gc in zip(glasts, gcols)]
            kbs = [kh * b for kh, b in zip(ks, betas)]
            vbs = [vh * b for vh, b in zip(vs, betas)]
            kbes = [kb * e for kb, e in zip(kbs, es)]

            a_s = [jnp.where(strict, _dot_nt(kb, kh) * dm, 0.0) for kb, kh, dm in zip(kbs, ks, dmats)]
            dps = [jnp.where(causal, _dot_nt(doh, vnh), 0.0) for doh, vnh in zip(dos, vns)]
            dqds = [_dot_nt(doh, s_ref[c, hd]) for doh, (c, hd) in zip(dos, items)]
            dkds = [_dot_nt(vnh, ds_ref[c, hd]) for vnh, (c, hd) in zip(vns, items)]
            dws = [-_dot_nt(dvnh, s_ref[c, hd]) for dvnh, (c, hd) in zip(dvns, items)]
            dvbs = [_dot_tn(tmh, dvnh) for tmh, dvnh in zip(tmhs, dvns)]
            dgts = [jnp.sum(s_ref[c, hd].astype(F32) * ds_ref[c, hd].astype(F32), keepdims=True) for c, hd in items]
            dts = [_dot_nt(dvnh, vb) + _dot_nt(dw, kbe) for dvnh, vb, dw, kbe in zip(dvns, vbs, dws, kbes)]
            dkbes = [_dot_tn(tmh, dw) for tmh, dw in zip(tmhs, dws)]
            xs = [_dot_nt(dt, tmh) for dt, tmh in zip(dts, tmhs)]
            das = [jnp.where(strict, -_dot_tn(tmh, x), 0.0) for tmh, x in zip(tmhs, xs)]
            dmas = [da * dm for da, dm in zip(das, dmats)]
            dmps = [dp * dm for dp, dm in zip(dps, dmats)]
            dkbs = [_dot(dma, kh) + dkbe * e for dma, kh, dkbe, e in zip(dmas, ks, dkbes, es)]
            for i, (c, hd) in enumerate(items):
                dq_ref[rs(c), sl(hd)] = _dot(dmps[i], ks[i]) + dqds[i] * es[i]
                dk_ref[rs(c), sl(hd)] = (_dot_tn(dmas[i], kbs[i]) + _dot_tn(dmps[i], qs[i]) + dkds[i] * eks[i]
                                         + dkbs[i] * betas[i])
                dv_ref[rs(c), sl(hd)] = dvbs[i] * betas[i]
            for c in range(c0, c0 + WY_GROUP):
                acc = jnp.zeros((CHUNK, LANE), F32)
                acc_t = jnp.zeros((SUBLANE, CHUNK), F32)
                for i, (ci, hd) in enumerate(items):
                    if ci != c:
                        continue
                    gmat = das[i] * a_s[i] + dps[i] * ps[i]
                    rk = jnp.sum(dkds[i] * ks[i], -1, keepdims=True) * eks[i]
                    de = (jnp.sum(dqds[i] * qs[i], -1, keepdims=True)
                          + jnp.sum(dkbes[i] * kbs[i], -1, keepdims=True))
                    dglast = jnp.sum(rk, keepdims=True) + dgts[i] * jnp.exp(glasts[i])
                    dgc = (jnp.sum(gmat, -1, keepdims=True) + de * es[i] - rk
                           + jnp.where(rowi == CHUNK - 1, dglast, 0.0))
                    dbeta = (jnp.sum(dkbs[i] * ks[i], -1, keepdims=True)
                             + jnp.sum(dvbs[i] * vs[i], -1, keepdims=True))
                    acc = acc + jnp.where(lane == hd, dbeta, 0.0) + jnp.where(lane == A_HEADS + hd, dgc, 0.0)
                    acc_t = acc_t + jnp.where(sub == A_HEADS + hd, -jnp.sum(gmat, axis=0, keepdims=True), 0.0)
                dbg_ref[rs(c), :] = acc
                dbgt_ref[:, rs(c)] = acc_t

    blk = pl.BlockSpec((rows, A_WIDTH), lambda i: (i, 0))
    half = pl.BlockSpec((rows, A_HEADS * CHUNK), lambda i: (i, 0))
    col = pl.BlockSpec((rows, LANE), lambda i: (i, 0))
    rowf = pl.BlockSpec((SUBLANE, rows), lambda i: (0, i))
    st = pl.BlockSpec((per, A_HEADS, LANE, LANE), lambda i: (i, 0, 0, 0))
    wide = jax.ShapeDtypeStruct((t, A_WIDTH), F32)
    return pl.pallas_call(
        body, name=name, grid=(t // rows,),
        in_specs=[blk, blk, blk, blk, half, half, col, rowf, st, st, blk, blk],
        out_specs=[blk, blk, blk, col, rowf],
        out_shape=[wide, wide, wide, jax.ShapeDtypeStruct((t, LANE), F32), jax.ShapeDtypeStruct((SUBLANE, t), F32)],
        compiler_params=_cp("parallel"))(q, k, v, vn, tmat, qk, bg, bgt, s_all, ds_all, dvn, do)


def _dn_pre_bwd(h, conv_w, par, dq, dk, dv, dbg, dbgt, *, tt, name):
    t = h.shape[0]
    cw = 3 * A_WIDTH
    hb = tt // HALO

    def body(pre_ref, halo_ref, bgi_ref, cw_ref, par_ref, dq_ref, dk_ref, dv_ref, dbg_ref, dbgt_ref,
             dc_ref, dbgi_ref, dpar_ref):
        i = pl.program_id(0)

        @pl.when(i == 0)
        def _():
            dpar_ref[...] = jnp.zeros_like(dpar_ref)

        cur = pre_ref[...].astype(F32)
        before = jnp.where(i > 0, halo_ref[...].astype(F32)[HALO - SUBLANE:], 0.0)
        c = _conv_fwd(cur, before, cw_ref[...])
        s = _silu(c)
        ds = _dsilu(c)
        for hd in range(A_HEADS):
            sl = slice(hd * LANE, (hd + 1) * LANE)
            for base, d_ref, scale in ((0, dq_ref, A_HEAD_DIM ** -0.5), (A_WIDTH, dk_ref, 1.0)):
                csl = slice(base + hd * LANE, base + (hd + 1) * LANE)
                tq = s[:, base + hd * LANE:base + (hd + 1) * LANE]
                dy = d_ref[:, sl]
                rq = lax.rsqrt(jnp.sum(tq * tq, -1, keepdims=True) + L2_EPS)
                dtq = scale * (rq * dy - tq * (rq * rq * rq) * jnp.sum(dy * tq, -1, keepdims=True))
                dc_ref[:, csl] = dtq * ds[:, base + hd * LANE:base + (hd + 1) * LANE]
        dc_ref[:, 2 * A_WIDTH:] = dv_ref[...] * ds[:, 2 * A_WIDTH:]
        raw = bgi_ref[...].astype(F32)
        lane = lax.broadcasted_iota(jnp.int32, raw.shape, 1)
        is_b = lane < A_HEADS
        is_a = (lane >= A_HEADS) & (lane < 2 * A_HEADS)
        rows_t = jnp.concatenate([dbgt_ref[...], jnp.zeros((LANE - SUBLANE, tt), F32)], axis=0)
        dbg_v = dbg_ref[...] + jnp.where(is_a, jnp.transpose(rows_t), 0.0)
        dbg_v = jnp.where(is_a, _dot_hi(_chunk_tri(tt, lower=False), jnp.where(is_a, dbg_v, 0.0)), dbg_v)
        beta = _sigmoid(raw)
        z = raw + par_ref[1:2, :]
        neg_ea = -jnp.exp(par_ref[0:1, :])
        g = neg_ea * _softplus(z)
        da = dbg_v * neg_ea * _sigmoid(z)
        dbgi_ref[...] = jnp.where(is_b, dbg_v * beta * (1.0 - beta), jnp.where(is_a, da, 0.0))
        dpar_ref[0:1, :] += jnp.sum(jnp.where(is_a, dbg_v * g, 0.0), axis=0, keepdims=True)
        dpar_ref[1:2, :] += jnp.sum(jnp.where(is_a, da, 0.0), axis=0, keepdims=True)

    wide = pl.BlockSpec((tt, A_WIDTH), lambda i: (i, 0))
    return pl.pallas_call(
        body, name=name, grid=(t // tt,),
        in_specs=[pl.BlockSpec((tt, cw), lambda i: (i, 0)),
                  pl.BlockSpec((HALO, cw), lambda i: (jnp.maximum(i * hb - 1, 0), 0)),
                  pl.BlockSpec((tt, LANE), lambda i: (i, C_BG // LANE)),
                  pl.BlockSpec((CONV_K, cw), lambda i: (0, 0)),
                  pl.BlockSpec((SUBLANE, LANE), lambda i: (0, 0)),
                  wide, wide, wide, pl.BlockSpec((tt, LANE), lambda i: (i, 0)),
                  pl.BlockSpec((SUBLANE, tt), lambda i: (0, i))],
        out_specs=[pl.BlockSpec((tt, cw), lambda i: (i, 0)), pl.BlockSpec((tt, LANE), lambda i: (i, 0)),
                   pl.BlockSpec((SUBLANE, LANE), lambda i: (0, 0))],
        out_shape=[jax.ShapeDtypeStruct((t, cw), F32), jax.ShapeDtypeStruct((t, LANE), F32),
                   jax.ShapeDtypeStruct((SUBLANE, LANE), F32)],
        compiler_params=_cp("arbitrary"))(h, h, h, conv_w, par, dq, dk, dv, dbg, dbgt)


def _conv_bwd(dc, h, conv_w, dh, *, tt, name):
    t = dc.shape[0]
    cw = 3 * A_WIDTH
    hb = tt // HALO
    nb = t // tt

    def body(dc_ref, after_ref, pre_ref, before_ref, cw_ref, dh_in_ref, dpre_ref, dcw_ref):
        i = pl.program_id(0)

        @pl.when(i == 0)
        def _():
            dcw_ref[...] = jnp.zeros_like(dcw_ref)

        dcv = dc_ref[...]
        after = jnp.where(i < nb - 1, after_ref[...], 0.0)
        cur = pre_ref[...].astype(F32)
        before = jnp.where(i > 0, before_ref[...].astype(F32)[HALO - SUBLANE:], 0.0)
        w = cw_ref[...]
        acc = dcv * w[CONV_K - 1:CONV_K, :]
        dcw_ref[CONV_K - 1:CONV_K, :] += jnp.sum(dcv * cur, axis=0, keepdims=True)
        for s in range(1, CONV_K):
            j = CONV_K - 1 - s
            acc = acc + _shift_up(dcv, after, s) * w[j:j + 1, :]
            dcw_ref[j:j + 1, :] += jnp.sum(dcv * _shift_down(cur, before, s), axis=0, keepdims=True)
        dpre_ref[...] = acc

    return pl.pallas_call(
        body, name=name, grid=(nb,),
        in_specs=[pl.BlockSpec((tt, cw), lambda i: (i, 0)),
                  pl.BlockSpec((SUBLANE, cw), lambda i: (jnp.minimum((i + 1) * (tt // SUBLANE), t // SUBLANE - 1), 0)),
                  pl.BlockSpec((tt, cw), lambda i: (i, 0)),
                  pl.BlockSpec((HALO, cw), lambda i: (jnp.maximum(i * hb - 1, 0), 0)),
                  pl.BlockSpec((CONV_K, cw), lambda i: (0, 0)), _ANY],
        out_specs=[pl.BlockSpec((tt, cw), lambda i: (i, 0)), pl.BlockSpec((SUBLANE, cw), lambda i: (0, 0))],
        out_shape=[jax.ShapeDtypeStruct(dh.shape, F32), jax.ShapeDtypeStruct((SUBLANE, cw), F32)],
        input_output_aliases={5: 0},
        compiler_params=_cp("arbitrary"))(dc, dc, h, h, conv_w, dh)


def _swa_bwd(h, dm, sinks_b, dh, *, name, carry=None):
    t = h.shape[0]
    qspec, cur, prev = _swa_specs()
    c_ins, c_in_specs, c_out_specs, c_outs, c_scratch = _carry_specs(carry)

    def body(*refs):
        (q_ref, kc_ref, kp_ref, vc_ref, vp_ref, zb_ref, dy_ref, sk_ref, dh_in_ref,
         dqz_ref, dk_ref, dv_ref, dsk_ref) = _carried(carry, refs, 9, 4, t // BLOCK)
        n_blk = pl.program_id(0)

        @pl.when(n_blk == 0)
        def _():
            dk_ref[...] = jnp.zeros_like(dk_ref)
            dv_ref[...] = jnp.zeros_like(dv_ref)
            dsk_ref[...] = jnp.zeros_like(dsk_ref)

        kband = jnp.concatenate([kp_ref[...], kc_ref[...]], axis=0)
        vband = jnp.concatenate([vp_ref[...], vc_ref[...]], axis=0)
        scale = B_HEAD_DIM ** -0.5
        hks = range(B_KV_HEADS)
        ksl = lambda hk: slice(hk * B_HEAD_DIM, (hk + 1) * B_HEAD_DIM)
        groups = _swa_group_probs(q_ref, sk_ref, kband, vband, _swa_neg_dist(n_blk))
        zbs = [_stack_heads(zb_ref, hk) for hk in hks]
        dys = [_stack_heads(dy_ref, hk) for hk in hks]
        dos = [dys[hk] * _silu(zbs[hk]) for hk in hks]
        deltas = [jnp.sum(dos[hk] * groups[hk][3], -1, keepdims=True) for hk in hks]
        dss = [groups[hk][1] * (_dot_nt(dos[hk], vband[:, ksl(hk)]) - deltas[hk]) for hk in hks]
        dqs = [_dot(dss[hk], kband[:, ksl(hk)]) * scale for hk in hks]
        dk_acc = [_dot_tn(dss[hk], groups[hk][0]) for hk in hks]
        dv_acc = [_dot_tn(groups[hk][1], dos[hk]) for hk in hks]
        for hk in hks:
            dzb = dys[hk] * groups[hk][3] * _dsilu(zbs[hk])
            dsink = groups[hk][2] * deltas[hk]
            for g in range(B_GROUP):
                hq = hk * B_GROUP + g
                rows = slice(g * BLOCK, (g + 1) * BLOCK)
                qsl = slice(hq * B_HEAD_DIM, (hq + 1) * B_HEAD_DIM)
                dqz_ref[:, qsl] = dqs[hk][rows]
                dqz_ref[:, B_WIDTH + hq * B_HEAD_DIM:B_WIDTH + (hq + 1) * B_HEAD_DIM] = dzb[rows]
                dsk_ref[hq:hq + 1, :] += -jnp.sum(dsink[rows], keepdims=True)
        dkb = jnp.concatenate(dk_acc, axis=1)
        dvb = jnp.concatenate(dv_acc, axis=1)
        at_cur = pl.ds(pl.multiple_of(n_blk * BLOCK, BLOCK), BLOCK)
        at_prev = pl.ds(pl.multiple_of(jnp.maximum(n_blk - 1, 0) * BLOCK, BLOCK), BLOCK)
        dk_ref[at_prev, :] += dkb[:BLOCK]
        dv_ref[at_prev, :] += dvb[:BLOCK]
        dk_ref[at_cur, :] += dkb[BLOCK:]
        dv_ref[at_cur, :] += dvb[BLOCK:]

    narrow = jax.ShapeDtypeStruct((t, B_KV_WIDTH), F32)
    res = lambda a, b: pl.BlockSpec((a, b), lambda i: (0, 0))
    outs = pl.pallas_call(
        body, name=name, grid=(t // BLOCK,),
        in_specs=[qspec(C_QB), cur(C_KB), prev(C_KB), cur(C_VB), prev(C_VB), qspec(C_ZB),
                  pl.BlockSpec((BLOCK, B_WIDTH), lambda i: (i, 1)), res(B_Q_HEADS, LANE), _ANY] + c_in_specs,
        out_specs=[pl.BlockSpec((BLOCK, 2 * B_WIDTH), lambda i: (i, C_QB // (2 * B_WIDTH))),
                   res(t, B_KV_WIDTH), res(t, B_KV_WIDTH), res(B_Q_HEADS, LANE)] + c_out_specs,
        out_shape=[jax.ShapeDtypeStruct(dh.shape, F32), narrow, narrow,
                   jax.ShapeDtypeStruct((B_Q_HEADS, LANE), F32)] + c_outs,
        scratch_shapes=c_scratch,
        input_output_aliases={8: 0},
        compiler_params=_cp("arbitrary"))(h, h, h, h, h, h, dm, sinks_b, dh, *c_ins)
    return outs[:4], outs[4:]


def _matmul_tn(a, b, *, tk, tm, name):
    t, m = a.shape
    n = b.shape[1]

    def body(a_ref, b_ref, o_ref):
        @pl.when(pl.program_id(1) == 0)
        def _():
            o_ref[...] = jnp.zeros_like(o_ref)

        o_ref[...] += _dot_tn(a_ref[...], b_ref[...])

    return pl.pallas_call(
        body, name=name, grid=(m // tm, t // tk),
        in_specs=[pl.BlockSpec((tk, tm), lambda j, kk: (kk, j)), pl.BlockSpec((tk, n), lambda j, kk: (kk, 0))],
        out_specs=pl.BlockSpec((tm, n), lambda j, kk: (j, 0)),
        out_shape=jax.ShapeDtypeStruct((m, n), F32),
        compiler_params=_cp("parallel", "arbitrary"))(a, b)


def _in_proj_dx(dh_main, dh_tail, wt, dr, *, tm, name, carry=None):
    t, n_main = dh_main.shape
    n_tail = dh_tail.shape[1]
    c_ins, c_in_specs, c_out_specs, c_outs, c_scratch = _carry_specs(carry)

    def body(*refs):
        a_ref, t_ref, wa_ref, wt_ref, r_ref, o_ref = _carried(carry, refs, 5, 1, t // tm)
        o_ref[...] = _dot(a_ref[...], wa_ref[...]) + _dot(t_ref[...], wt_ref[...]) + DEEPNORM_ALPHA * r_ref[...]

    row = lambda w: pl.BlockSpec((tm, w), lambda i: (i, 0))
    outs = pl.pallas_call(
        body, name=name, grid=(t // tm,),
        in_specs=[row(n_main), row(n_tail), pl.BlockSpec((n_main, D_MODEL), lambda i: (0, 0)),
                  pl.BlockSpec((n_tail, D_MODEL), lambda i: (n_main // n_tail, 0)), row(D_MODEL)] + c_in_specs,
        out_specs=[row(D_MODEL)] + c_out_specs,
        out_shape=[jax.ShapeDtypeStruct((t, D_MODEL), F32)] + c_outs,
        scratch_shapes=c_scratch,
        compiler_params=_cp("arbitrary"))(dh_main, dh_tail, wt, wt, dr, *c_ins)
    return outs[0], outs[1:]


def _layer_bwd(dxn, res, wt, conv_w, par, sinks_b, norm_w, w_out_bf, ln_g, l, carry=None, carry_dx=None):
    w_out_bf = res["w_out"]
    dr, dm, dw_out, dln_g, dln_b = _ln_out_bwd(dxn, res["r"], res["mixed"], ln_g, w_out_bf, tm=512, name=f"ln_out_bwd_{l}")
    h = res["h"]
    do, dh, dnw = _dn_post_bwd(dm, res["oa"], h, norm_w, tm=512, name=f"dn_post_bwd_{l}")
    dvn, ds_all = _dn_scan_bwd(res["q"], res["k"], res["w"], res["qk"], res["bg"], do, name=f"dn_scan_bwd_{l}")
    dq, dk, dv, dbg, dbgt = _dn_chunk_bwd(res["q"], res["k"], res["v"], res["vn"], res["tmat"], res["qk"], res["bg"],
                                          res["bgt"], res["s_all"], ds_all, dvn, do, name=f"dn_chunk_bwd_{l}")
    dc, dbgi, dpar = _dn_pre_bwd(h, conv_w, par, dq, dk, dv, dbg, dbgt, tt=512, name=f"dn_pre_bwd_{l}")
    dh, dcw = _conv_bwd(dc, h, conv_w, dh, tt=512, name=f"conv_bwd_{l}")
    (dh, dkb, dvb, dsk), carried = _swa_bwd(h, dm, sinks_b, dh, name=f"swa_bwd_{l}", carry=carry)
    dh_tail = jnp.concatenate([dkb, dvb, dbgi], axis=1)
    dwt_main = _matmul_tn(dh, res["x"], tk=512, tm=DH_MAIN, name=f"in_proj_dw_{l}")
    dwt_tail = _matmul_tn(dh_tail, res["x"], tk=512, tm=P_COLS - DH_MAIN, name=f"in_proj_dw_tail_{l}")
    grads = dict(w_in=(dwt_main, dwt_tail), conv_w=dcw[:CONV_K], a_log=dpar[0, A_HEADS:2 * A_HEADS],
                 dt_bias=dpar[1, A_HEADS:2 * A_HEADS], norm_w=dnw[0], sinks=dsk[:, 0], w_out=dw_out,
                 ln_g=dln_g[0], ln_b=dln_b[0])
    dx, carried_dx = _in_proj_dx(dh, dh_tail, wt, dr, tm=512, name=f"in_proj_dx_{l}",
                                 carry=None if carry_dx is None else carry_dx(grads))
    return dx, grads, carried, carried_dx


def _layer_args(wt, conv_w, a_log, dt_bias, sinks, norm_w, w_out_bf):
    return (wt, conv_w, _gate_params(a_log, dt_bias), jnp.broadcast_to(sinks[:, None], (B_Q_HEADS, LANE)),
            norm_w[None], w_out_bf)


def _local_step(x, target, args0, args1, ln_g, ln_b, gathers=None, reduce1=None, reduce0=None):
    assert DEPTH == 2
    x1, res0, got = _layer_fwd(x, *args0, ln_g[0][None], ln_b[0][None], 0, carries=gathers)
    if gathers is not None:
        args1 = args1(got)
    (dx, loss_tile), res1, _ = _layer_fwd(x1, *args1, ln_g[1][None], ln_b[1][None], 1, target=target)
    dx, grads1, _, _ = _layer_bwd(dx, res1, *args1, ln_g[1][None], 1)
    carry = None if reduce1 is None else reduce1(grads1)
    carry_dx = None if reduce0 is None else (lambda grads0: reduce0(grads0, grads1, loss_tile))
    dx, grads0, landed1, landed0 = _layer_bwd(dx, res0, *args0, ln_g[0][None], 0, carry=carry, carry_dx=carry_dx)
    return loss_tile, dx, [grads0, grads1], landed1, landed0


_ANY = pl.BlockSpec(memory_space=pl.ANY)
_MESH = pl.DeviceIdType.MESH


HALF = D_MODEL // 2


class _Exchange:
    def __init__(self, ins, outs, n_remote, n_local, plan):
        self.ins, self.outs, self.n_remote, self.n_local, self.plan = tuple(ins), tuple(outs), n_remote, n_local, plan

    def scratch(self):
        return [pltpu.SemaphoreType.DMA((self.n_remote,)), pltpu.SemaphoreType.DMA((self.n_remote,)),
                pltpu.SemaphoreType.DMA((max(self.n_local, 1),))]

    def _copies(self, in_refs, out_refs, sems, arriving):
        send_sems, recv_sems, local_sems = sems
        local, sends, recvs = self.plan(in_refs, out_refs)
        loc = [pltpu.make_async_copy(s, d, local_sems.at[i]) for i, (s, d) in enumerate(local)]
        rem = [pltpu.make_async_remote_copy(src_ref=s, dst_ref=recvs[i] if arriving else d, send_sem=send_sems.at[i],
                                            recv_sem=recv_sems.at[i], device_id=peer, device_id_type=_MESH)
               for i, (s, d, peer) in enumerate(sends)]
        return loc, rem

    def start(self, in_refs, out_refs, sems):
        loc, rem = self._copies(in_refs, out_refs, sems, arriving=False)
        for cp in loc + rem:
            cp.start()

    def finish(self, in_refs, out_refs, sems):
        loc, rem = self._copies(in_refs, out_refs, sems, arriving=True)
        for cp in rem:
            cp.wait_recv()
        for cp in rem:
            cp.wait_send()
        for cp in loc:
            cp.wait()


def _run_exchange(ex, *, name):
    n_in, n_out = len(ex.ins), len(ex.outs)

    def body(*refs):
        parts = refs[:n_in], refs[n_in:n_in + n_out], refs[n_in + n_out:]
        ex.start(*parts)
        ex.finish(*parts)

    return pl.pallas_call(body, name=name, in_specs=[_ANY] * n_in, out_specs=[_ANY] * n_out, out_shape=list(ex.outs),
                          scratch_shapes=ex.scratch())(*ex.ins)


def _place():
    x, y, c = lax.axis_index("x"), lax.axis_index("y"), lax.axis_index("c")
    return x, y, c, [(1 - x, y), (x, 1 - y), (1 - x, 1 - y)]


def _gather_exchange(arrays):
    n = len(arrays)

    def plan(src, dst):
        x, y, c, chips = _place()
        me = 2 * x + y
        local = [(src[k], dst[k].at[me]) for k in range(n)]
        sends = [(src[k], dst[k].at[me], (px, py, c)) for k in range(n) for px, py in chips]
        recvs = [dst[k].at[2 * px + py] for k in range(n) for px, py in chips]
        return local, sends, recvs

    return _Exchange(arrays, [jax.ShapeDtypeStruct((N_SHARD,) + a.shape, a.dtype) for a in arrays], 3 * n, n, plan)


def _gather_two_level(pack, conv_w, *, name):
    rows = pack.shape[0]
    part_rows = rows // 2

    def body(pack_ref, conv_ref, land_ref, conv_land_ref, send1, recv1, send2, recv2, csend, crecv, local_sems):
        x, y, c, chips = _place()
        me = 2 * x + y
        sibling = (x, y, 1 - c)
        part = lambda core: pl.ds(pl.multiple_of(core * part_rows, 16), part_rows)
        remote = lambda src, dst, ss, rs, to: pltpu.make_async_remote_copy(
            src_ref=src, dst_ref=dst, send_sem=ss, recv_sem=rs, device_id=to, device_id_type=_MESH)
        local = [pltpu.make_async_copy(pack_ref, land_ref.at[me], local_sems.at[0]),
                 pltpu.make_async_copy(conv_ref, conv_land_ref.at[me], local_sems.at[1])]
        for cp in local:
            cp.start()
        first = [remote(pack_ref.at[part(c)], land_ref.at[me, part(c)], send1.at[j], recv1.at[j], (px, py, c))
                 for j, (px, py) in enumerate(chips)]
        convs = [remote(conv_ref, conv_land_ref.at[me], csend.at[j], crecv.at[j], (px, py, c))
                 for j, (px, py) in enumerate(chips)]
        for cp in first + convs:
            cp.start()
        passed = []
        for j, (px, py) in enumerate(chips):
            slot = 2 * px + py
            remote(pack_ref.at[part(c)], land_ref.at[slot, part(c)], send1.at[j], recv1.at[j], (px, py, c)).wait_recv()
            cp = remote(land_ref.at[slot, part(c)], land_ref.at[slot, part(c)], send2.at[j], recv2.at[j], sibling)
            cp.start()
            passed.append(cp)
        for j, (px, py) in enumerate(chips):
            slot = 2 * px + py
            remote(land_ref.at[slot, part(1 - c)], land_ref.at[slot, part(1 - c)], send2.at[j], recv2.at[j],
                   sibling).wait_recv()
            remote(conv_ref, conv_land_ref.at[slot], csend.at[j], crecv.at[j], (px, py, c)).wait_recv()
        for cp in first + convs + passed:
            cp.wait_send()
        for cp in local:
            cp.wait()

    sems = [pltpu.SemaphoreType.DMA((3,))] * 6 + [pltpu.SemaphoreType.DMA((2,))]
    return pl.pallas_call(
        body, name=name, in_specs=[_ANY, _ANY], out_specs=[_ANY, _ANY],
        out_shape=[jax.ShapeDtypeStruct((N_SHARD,) + pack.shape, pack.dtype),
                   jax.ShapeDtypeStruct((N_SHARD,) + conv_w.shape, conv_w.dtype)],
        scratch_shapes=sems)(pack, conv_w)


def _half(core):
    return pl.ds(pl.multiple_of(core * HALF, HALF), HALF)


def _reduce_scatter_exchange(g, small=None):
    ins = [g] if small is None else [g, small]
    outs = [jax.ShapeDtypeStruct((7,) + g.shape[1:2] + (HALF,), g.dtype)]
    if small is not None:
        outs.append(jax.ShapeDtypeStruct((8,) + small.shape, small.dtype))

    def plan(src, dst):
        x, y, c, chips = _place()
        me = 2 * x + y
        peers = [(px, py, c if t == 0 else 1 - c) for px, py in chips for t in (0, 1)] + [(x, y, 1 - c)]
        sends = [(src[0].at[2 * px + py, :, _half(pc)], dst[0].at[k], (px, py, pc)) for k, (px, py, pc) in enumerate(peers)]
        recvs = [dst[0].at[k] for k in range(7)]
        local = []
        if small is not None:
            mine = 4 * x + 2 * y + c
            local = [(src[1], dst[1].at[mine])]
            sends += [(src[1], dst[1].at[mine], peer) for peer in peers]
            recvs += [dst[1].at[4 * px + 2 * py + pc] for px, py, pc in peers]
        return local, sends, recvs

    return _Exchange(ins, outs, 7 * len(ins), len(ins) - 1, plan)


def _pair_window_exchange(g):
    def plan(src, dst):
        x, y, c, _ = _place()
        return [], [(src[0].at[:, :, _half(1 - c)], dst[0], (x, y, 1 - c))], [dst[0]]

    return _Exchange([g], [jax.ShapeDtypeStruct(g.shape[:2] + (HALF,), g.dtype)], 1, 0, plan)


def _chip_scatter_exchange(p, small):
    def plan(src, dst):
        x, y, c, chips = _place()
        mine = 4 * x + 2 * y + c
        peers = [(px, py, c if t == 0 else 1 - c) for px, py in chips for t in (0, 1)] + [(x, y, 1 - c)]
        sends = [(src[0].at[2 * px + py], dst[0].at[j], (px, py, c)) for j, (px, py) in enumerate(chips)]
        recvs = [dst[0].at[j] for j in range(3)]
        sends += [(src[1], dst[1].at[mine], peer) for peer in peers]
        recvs += [dst[1].at[4 * px + 2 * py + pc] for px, py, pc in peers]
        return [(src[1], dst[1].at[mine])], sends, recvs

    outs = [jax.ShapeDtypeStruct((3,) + p.shape[1:], p.dtype), jax.ShapeDtypeStruct((8,) + small.shape, small.dtype)]
    return _Exchange([p, small], outs, 10, 1, plan)


def _share_exchange(arrays):
    n = len(arrays)

    def plan(src, dst):
        x, y, c, _ = _place()
        return [], [(src[k], dst[k], (x, y, 1 - c)) for k in range(n)], [dst[k] for k in range(n)]

    return _Exchange(arrays, [jax.ShapeDtypeStruct(a.shape, a.dtype) for a in arrays], n, 0, plan)


def _sum_scatter(g, land, me, core, *, tc, name):
    rows = g.shape[1]
    per = HALF // tc

    def body(where_ref, g_ref, land_ref, o_ref):
        acc = g_ref[...]
        for k in range(7):
            acc = acc + land_ref[k].astype(F32)
        o_ref[...] = acc

    return pl.pallas_call(
        body, name=name, out_shape=jax.ShapeDtypeStruct((rows, HALF), F32), compiler_params=_cp("parallel"),
        grid_spec=pltpu.PrefetchScalarGridSpec(
            num_scalar_prefetch=1, grid=(per,),
            in_specs=[pl.BlockSpec((None, rows, tc), lambda i, w: (w[0], 0, w[1] * per + i)),
                      pl.BlockSpec((7, rows, tc), lambda i, w: (0, 0, i))],
            out_specs=pl.BlockSpec((rows, tc), lambda i, w: (0, i))))(
        jnp.stack([me, core]).astype(jnp.int32), g, land)


def _pair_add(g, land, core, *, name):
    n, rows, _ = g.shape

    def body(core_ref, g_ref, land_ref, o_ref):
        o_ref[...] = (g_ref[...].astype(F32) + land_ref[...].astype(F32)).astype(o_ref.dtype)

    blk = pl.BlockSpec((1, rows, HALF), lambda i, w: (i, 0, 0))
    return pl.pallas_call(
        body, name=name, out_shape=jax.ShapeDtypeStruct((n, rows, HALF), g.dtype), compiler_params=_cp("parallel"),
        grid_spec=pltpu.PrefetchScalarGridSpec(
            num_scalar_prefetch=1, grid=(n,),
            in_specs=[pl.BlockSpec((1, rows, HALF), lambda i, w: (i, 0, w[0])), blk], out_specs=blk))(
        jnp.reshape(core, (1,)).astype(jnp.int32), g, land)


def _sum_chips(p, land, me, *, tc, name):
    rows = p.shape[1]

    def body(me_ref, p_ref, land_ref, o_ref):
        acc = p_ref[...].astype(F32)
        for k in range(3):
            acc = acc + land_ref[k].astype(F32)
        o_ref[...] = acc

    return pl.pallas_call(
        body, name=name, out_shape=jax.ShapeDtypeStruct((rows, HALF), F32), compiler_params=_cp("parallel"),
        grid_spec=pltpu.PrefetchScalarGridSpec(
            num_scalar_prefetch=1, grid=(HALF // tc,),
            in_specs=[pl.BlockSpec((None, rows, tc), lambda i, w: (w[0], 0, i)),
                      pl.BlockSpec((3, rows, tc), lambda i, w: (0, 0, i))],
            out_specs=pl.BlockSpec((rows, tc), lambda i, w: (0, i))))(
        jnp.reshape(me, (1,)).astype(jnp.int32), p, land)


def _sum_slots(a, *, name):
    n = a.shape[0]

    def body(a_ref, o_ref):
        acc = a_ref[0]
        for k in range(1, n):
            acc = acc + a_ref[k]
        o_ref[...] = acc

    return pl.pallas_call(body, name=name, out_shape=jax.ShapeDtypeStruct(a.shape[1:], a.dtype))(a)


def _elementwise(fn, ins, n_out, block, *, name):
    shape = ins[0].shape
    grid = tuple(s // b for s, b in zip(shape, block))
    n_in = len(ins)

    def body(*refs):
        outs = fn(*[r[...] for r in refs[:n_in]])
        for o_ref, val in zip(refs[n_in:], outs):
            o_ref[...] = val

    spec = pl.BlockSpec(block, lambda i, j, k: (i, j, k))
    return pl.pallas_call(body, name=name, grid=grid, in_specs=[spec] * n_in, out_specs=[spec] * n_out,
                          out_shape=[jax.ShapeDtypeStruct(shape, F32)] * n_out,
                          compiler_params=_cp(*["parallel"] * 3))(*ins)


def _adamw_math(w, g, m, v):
    mn = ADAM_B1 * m + (1.0 - ADAM_B1) * g
    vn = ADAM_B2 * v + (1.0 - ADAM_B2) * (g * g)
    m_hat = mn / (1.0 - ADAM_B1 ** ADAM_STEP)
    v_hat = vn / (1.0 - ADAM_B2 ** ADAM_STEP)
    return -ADAM_LR * (m_hat / (jnp.sqrt(v_hat) + ADAM_EPS) + ADAM_WD * w), mn, vn


def _adamw(w, g, m, v, block, *, name):
    return _elementwise(_adamw_math, [w, g, m, v], 3, block, name=name)


def _interleave_layers(layers, *, tc, name):
    rows, cols = layers[0].shape
    n = len(layers)

    def body(*refs):
        for l in range(n):
            refs[n][:, l, :] = refs[l][...]

    return pl.pallas_call(body, name=name, grid=(cols // tc,),
                          in_specs=[pl.BlockSpec((rows, tc), lambda i: (0, i))] * n,
                          out_specs=pl.BlockSpec((rows, n, tc), lambda i: (0, 0, i)),
                          out_shape=jax.ShapeDtypeStruct((rows, n, cols), layers[0].dtype),
                          compiler_params=_cp("parallel"))(*layers)


def _adamw_small(ws, gs, ms, vs, *, name):
    n = len(ws)

    def body(*refs):
        w, g, m, v, outs = refs[:n], refs[n:2 * n], refs[2 * n:3 * n], refs[3 * n:4 * n], refs[4 * n:]
        for k in range(n):
            for slot, val in enumerate(_adamw_math(w[k][...], g[k][...], m[k][...], v[k][...])):
                outs[slot * n + k][...] = val

    outs = pl.pallas_call(body, name=name, out_shape=[jax.ShapeDtypeStruct(a.shape, F32) for a in ws] * 3)(
        *ws, *gs, *ms, *vs)
    return outs[:n], outs[n:2 * n], outs[2 * n:]


def _to_kernel_order(wt):
    gates = jnp.pad(wt[2048:2056], ((0, LANE - 2 * A_HEADS), (0, 0)))
    return jnp.concatenate([wt[0:2048], wt[2056:2568], wt[2824:3336], wt[2568:2696], wt[2696:2824], gates], axis=0)


def _from_kernel_order(main, tail):
    return jnp.concatenate([main[0:2048], tail[C_BG - DH_MAIN:C_BG - DH_MAIN + 2 * A_HEADS],
                            main[C_QB:C_QB + B_WIDTH], tail[0:B_KV_WIDTH], tail[B_KV_WIDTH:2 * B_KV_WIDTH],
                            main[C_ZB:C_ZB + B_WIDTH]], axis=0)


def _gate_params(a_log, dt_bias):
    return jnp.pad(jnp.stack([a_log, dt_bias]), ((0, SUBLANE - 2), (A_HEADS, LANE - 2 * A_HEADS)))


SMALL = ("conv_w", "a_log", "dt_bias", "norm_w", "sinks", "ln_g", "ln_b")


def _pack(parts, cols):
    flat = jnp.concatenate([p.reshape(-1) for p in parts])
    rows = -(-flat.shape[0] // cols)
    return jnp.pad(flat, (0, rows * cols - flat.shape[0])).reshape(rows, cols)


def _unpack(packed, shapes):
    flat = packed.reshape(-1)
    out, at = [], 0
    for s in shapes:
        n = math.prod(s)
        out.append(flat[at:at + n].reshape(s))
        at += n
    return out


def kernel(x, w_in, conv_w, a_log, dt_bias, norm_w, sinks, w_out, ln_g, ln_b, loss_target, m_w_in, m_conv_w, m_a_log, m_dt_bias, m_norm_w, m_sinks, m_w_out, m_ln_g, m_ln_b, v_w_in, v_conv_w, v_a_log, v_dt_bias, v_norm_w, v_sinks, v_w_out, v_ln_g, v_ln_b):
    xi, yi, ci = lax.axis_index("x"), lax.axis_index("y"), lax.axis_index("c")
    me = 2 * xi + yi

    to_t = lambda a: jnp.transpose(a, (2, 0, 1))
    from_t = lambda a: jnp.transpose(a, (1, 2, 0))

    wt_shard = to_t(w_in)

    def pack_weights(l):
        rows = jnp.pad(wt_shard[:, l], ((0, IN_PAD - IN_SHARD), (0, 0)))
        return jnp.concatenate([rows, w_out[l]], axis=0).astype(BF16)

    pack0, pack1 = pack_weights(0), pack_weights(1)
    got_in0, g_conv = _gather_two_level(pack0[:IN_PAD], conv_w, name="gather_weights_0")
    conv_full = jnp.moveaxis(g_conv, 0, 2).reshape(DEPTH, CONV_K, 3 * A_WIDTH)
    piece = IN_PAD // 3
    carriers = ("dn_pre", "dn_wy", "dn_scan")
    gathers = {nm: _gather_exchange([pack1[i * piece:(i + 1) * piece]]) for i, nm in enumerate(carriers)}
    gathers.update(in_proj=_gather_exchange([pack0[IN_PAD:]]), swa=_gather_exchange([pack1[IN_PAD:]]))
    w_in_of = lambda rows: _to_kernel_order(rows[:, :IN_SHARD].reshape(IN_COLS, D_MODEL))
    w_out_of = lambda rows: rows.reshape(D_MODEL, D_MODEL)
    args0 = _layer_args(w_in_of(got_in0), conv_full[0], a_log[0], dt_bias[0], sinks[0], norm_w[0],
                        lambda got: w_out_of(got[0]))

    def args1(got):
        rows = jnp.concatenate([got[nm][0] for nm in carriers], axis=1)
        return _layer_args(w_in_of(rows), conv_full[1], a_log[1], dt_bias[1], sinks[1], norm_w[1],
                           w_out_of(got["swa"][0]))

    def pack_grads(g):
        gin = _from_kernel_order(*g["w_in"]).reshape(N_SHARD, IN_SHARD, D_MODEL)
        gin = jnp.pad(gin, ((0, 0), (0, IN_PAD - IN_SHARD), (0, 0)))
        return jnp.concatenate([gin, g["w_out"].reshape(N_SHARD, OUT_SHARD, D_MODEL)], axis=1).astype(BF16)

    packed = {}

    def reduce1(grads1):
        packed[1] = pack_grads(grads1)
        return _reduce_scatter_exchange(packed[1])

    def reduce0(grads0, grads1, loss_tile):
        g0 = pack_grads(grads0)
        from_sibling = _run_exchange(_pair_window_exchange(g0), name="pair_reduce_0")[0]
        packed[0] = _pair_add(g0, from_sibling, ci, name="pair_add_0")
        gsmall = _pack([jnp.stack([g[nm] for g in (grads0, grads1)]) for nm in SMALL] + [loss_tile[0, 0:1]], D_MODEL)
        return _chip_scatter_exchange(packed[0], gsmall)

    _, dx, grads, landed1, (landed0, landed_small) = _local_step(
        x[0], loss_target[0], args0, args1, ln_g, ln_b, gathers=gathers, reduce1=reduce1, reduce0=reduce0)

    small_shapes = [(DEPTH,) + grads[0][nm].shape for nm in SMALL]
    halves = [_sum_chips(packed[0], landed0, me, tc=2 * LANE, name="reduce_sum_0"),
              _sum_scatter(packed[1], landed1[0], me, ci, tc=2 * LANE, name="reduce_sum_1")]
    s_small = _sum_slots(landed_small, name="reduce_sum_small")
    others = _run_exchange(_share_exchange(halves), name="pair_share")
    full = [jnp.where(ci == 0, jnp.concatenate([mine, other], axis=1), jnp.concatenate([other, mine], axis=1))
            for mine, other in zip(halves, others)]
    grad_in_layers = [f[:IN_SHARD] for f in full]
    grad_out = jnp.stack([f[IN_PAD:] for f in full])
    out_blk = (1, OUT_SHARD, D_MODEL)
    *small_grads, loss = _unpack(s_small, small_shapes + [()])
    gs = dict(zip(SMALL, small_grads))
    gs["conv_w"] = lax.dynamic_slice_in_dim(gs["conv_w"], me * CONV_SHARD, CONV_SHARD, axis=2)

    grad_in_t = _interleave_layers(grad_in_layers, tc=2 * LANE, name="grad_in_layers")
    d_in, nm_in, nv_in = (from_t(o) for o in _adamw(to_t(w_in), grad_in_t, to_t(m_w_in), to_t(v_w_in),
                                                    (IN_SHARD // 6, DEPTH, D_MODEL), name="adamw_in"))
    grad_in = from_t(grad_in_t)
    d_out, nm_out, nv_out = _adamw(w_out, grad_out, m_w_out, v_w_out, out_blk, name="adamw_out")
    ws = dict(conv_w=conv_w, a_log=a_log, dt_bias=dt_bias, norm_w=norm_w, sinks=sinks, ln_g=ln_g, ln_b=ln_b)
    ms = dict(conv_w=m_conv_w, a_log=m_a_log, dt_bias=m_dt_bias, norm_w=m_norm_w, sinks=m_sinks, ln_g=m_ln_g, ln_b=m_ln_b)
    vs = dict(conv_w=v_conv_w, a_log=v_a_log, dt_bias=v_dt_bias, norm_w=v_norm_w, sinks=v_sinks, ln_g=v_ln_g, ln_b=v_ln_b)
    d_s, nm_s, nv_s = (dict(zip(SMALL, o)) for o in _adamw_small(*[[d[nm] for nm in SMALL] for d in (ws, gs, ms, vs)],
                                                                 name="adamw_small"))

    def in_order(big_in, small, big_out):
        return (big_in, small["conv_w"], small["a_log"], small["dt_bias"], small["norm_w"], small["sinks"], big_out,
                small["ln_g"], small["ln_b"])

    return (loss, dx[None], *in_order(grad_in, gs, grad_out), *in_order(d_in, d_s, d_out),
            *in_order(nm_in, nm_s, nm_out), *in_order(nv_in, nv_s, nv_out))
```

```python
import math

import jax
import jax.numpy as jnp
from jax import lax
from jax.experimental import pallas as pl
from jax.experimental.pallas import tpu as pltpu

F32 = jnp.float32
BF16 = jnp.bfloat16
HI = lax.Precision.HIGHEST

D_MODEL = 1024
DEPTH = 2
A_HEADS = 4
A_HEAD_DIM = 128
A_WIDTH = 512
CONV_K = 4
CHUNK = 64
B_Q_HEADS = 8
B_KV_HEADS = 2
B_HEAD_DIM = 64
B_GROUP = 4
B_WIDTH = 512
B_KV_WIDTH = 128
BLOCK = 128
IN_COLS = 3336
DEEPNORM_ALPHA = (2 * DEPTH) ** 0.25
LN_EPS = 1e-5
RMS_EPS = 1e-6
L2_EPS = 1e-6
ADAM_LR = 0.001
ADAM_B1 = 0.9
ADAM_B2 = 0.999
ADAM_EPS = 1e-08
ADAM_WD = 0.01
ADAM_STEP = 10

N_SHARD = 4
IN_SHARD = IN_COLS // N_SHARD
OUT_SHARD = D_MODEL // N_SHARD
CONV_SHARD = 3 * A_WIDTH // N_SHARD
IN_PAD = -(-IN_SHARD // 96) * 96

P_COLS = 3456
C_PRE = 0
C_ZA = 1536
C_QB = 2048
C_ZB = 2560
C_KB = 3072
C_VB = 3200
C_BG = 3328
DH_MAIN = C_KB
LANE = 128
SUBLANE = 8
HALO = 16
VMEM_LIMIT = 56 * 1024 * 1024
ALIBI = tuple(2.0 ** (-8.0 * (h + 1) / B_Q_HEADS) for h in range(B_Q_HEADS))
NEG = -1e30


def _cp(*sem):
    return pltpu.CompilerParams(dimension_semantics=sem, vmem_limit_bytes=VMEM_LIMIT)


def _dot(a, b):
    return jnp.dot(a.astype(BF16), b.astype(BF16), preferred_element_type=F32)


def _dot_nt(a, b):
    return lax.dot_general(a.astype(BF16), b.astype(BF16), (((1,), (1,)), ((), ())),
                           preferred_element_type=F32)


def _dot_tn(a, b):
    return lax.dot_general(a.astype(BF16), b.astype(BF16), (((0,), (0,)), ((), ())),
                           preferred_element_type=F32)


def _dot_hi(a, b):
    return jnp.dot(a, b, precision=HI, preferred_element_type=F32)


def _sigmoid(x):
    return jax.nn.sigmoid(x)


def _silu(x):
    return x * _sigmoid(x)


def _dsilu(x):
    s = _sigmoid(x)
    return s * (1.0 + x * (1.0 - s))


def _softplus(x):
    return jnp.maximum(x, 0.0) + jnp.log(1.0 + jnp.exp(-jnp.abs(x)))


def _shift_down(cur, before, s):
    if s == 0:
        return cur
    r = pltpu.roll(cur, s, 0)
    rb = pltpu.roll(before, s, 0)
    row = lax.broadcasted_iota(jnp.int32, before.shape, 0)
    head = jnp.where(row < s, rb, r[0:SUBLANE])
    return jnp.concatenate([head, r[SUBLANE:]], axis=0)


def _shift_up(cur, after, s):
    if s == 0:
        return cur
    n = cur.shape[0]
    r = pltpu.roll(cur, n - s, 0)
    ra = pltpu.roll(after, SUBLANE - s, 0)
    row = lax.broadcasted_iota(jnp.int32, after.shape, 0)
    tail = jnp.where(row >= SUBLANE - s, ra, r[n - SUBLANE:])
    return jnp.concatenate([r[:n - SUBLANE], tail], axis=0)


def _conv_fwd(cur, before, w):
    acc = cur * w[CONV_K - 1:CONV_K, :]
    for s in range(1, CONV_K):
        acc = acc + _shift_down(cur, before, s) * w[CONV_K - 1 - s:CONV_K - s, :]
    return acc


def _matmul_nt(a, bt, *, tm, name, carry=None):
    m, k = a.shape
    n = bt.shape[0]
    c_ins, c_in_specs, c_out_specs, c_outs, c_scratch = _carry_specs(carry)

    def body(*refs):
        a_ref, b_ref, o_ref = _carried(carry, refs, 2, 1, m // tm)
        o_ref[...] = _dot_nt(a_ref[...], b_ref[...]).astype(o_ref.dtype)

    outs = pl.pallas_call(
        body, name=name, grid=(m // tm,),
        in_specs=[pl.BlockSpec((tm, k), lambda i: (i, 0)), pl.BlockSpec((n, k), lambda i: (0, 0))] + c_in_specs,
        out_specs=[pl.BlockSpec((tm, n), lambda i: (i, 0))] + c_out_specs,
        out_shape=[jax.ShapeDtypeStruct((m, n), BF16)] + c_outs,
        scratch_shapes=c_scratch,
        compiler_params=_cp("arbitrary"))(a, bt, *c_ins)
    return outs[0], outs[1:]


def _dn_pre(h, conv_w, par, *, tt, name, carry=None):
    t = h.shape[0]
    cw = 3 * A_WIDTH
    hb = tt // HALO

    c_ins, c_in_specs, c_out_specs, c_outs, c_scratch = _carry_specs(carry)

    def body(*refs):
        (pre_ref, halo_ref, bgi_ref, cw_ref, par_ref,
         q_ref, k_ref, v_ref, bg_ref, bgt_ref) = _carried(carry, refs, 5, 5, t // tt)
        i = pl.program_id(0)
        cur = pre_ref[...].astype(F32)
        before = jnp.where(i > 0, halo_ref[...].astype(F32)[HALO - SUBLANE:], 0.0)
        s = _silu(_conv_fwd(cur, before, cw_ref[...]))
        for hd in range(A_HEADS):
            sl = slice(hd * LANE, (hd + 1) * LANE)
            tq = s[:, hd * LANE:(hd + 1) * LANE]
            q_ref[:, sl] = tq * (lax.rsqrt(jnp.sum(tq * tq, -1, keepdims=True) + L2_EPS) * (A_HEAD_DIM ** -0.5))
            tk = s[:, A_WIDTH + hd * LANE:A_WIDTH + (hd + 1) * LANE]
            k_ref[:, sl] = tk * lax.rsqrt(jnp.sum(tk * tk, -1, keepdims=True) + L2_EPS)
        v_ref[...] = s[:, 2 * A_WIDTH:]
        raw = bgi_ref[...].astype(F32)
        lane = lax.broadcasted_iota(jnp.int32, raw.shape, 1)
        is_a = (lane >= A_HEADS) & (lane < 2 * A_HEADS)
        g = jnp.where(is_a, -jnp.exp(par_ref[0:1, :]) * _softplus(raw + par_ref[1:2, :]), 0.0)
        gc = _dot_hi(_chunk_tri(tt, lower=True), g)
        bg = jnp.where(lane < A_HEADS, _sigmoid(raw), gc)
        bg_ref[...] = bg
        bgt_ref[...] = jnp.transpose(bg)[0:SUBLANE, :]

    wide = jax.ShapeDtypeStruct((t, A_WIDTH), F32)
    outs = pl.pallas_call(
        body, name=name, grid=(t // tt,),
        in_specs=[pl.BlockSpec((tt, cw), lambda i: (i, 0)),
                  pl.BlockSpec((HALO, cw), lambda i: (jnp.maximum(i * hb - 1, 0), 0)),
                  pl.BlockSpec((tt, LANE), lambda i: (i, C_BG // LANE)),
                  pl.BlockSpec((CONV_K, cw), lambda i: (0, 0)),
                  pl.BlockSpec((SUBLANE, LANE), lambda i: (0, 0))] + c_in_specs,
        out_specs=[pl.BlockSpec((tt, A_WIDTH), lambda i: (i, 0))] * 3
        + [pl.BlockSpec((tt, LANE), lambda i: (i, 0)), pl.BlockSpec((SUBLANE, tt), lambda i: (0, i))] + c_out_specs,
        out_shape=[wide, wide, wide, jax.ShapeDtypeStruct((t, LANE), F32),
                   jax.ShapeDtypeStruct((SUBLANE, t), F32)] + c_outs,
        scratch_shapes=c_scratch,
        compiler_params=_cp("arbitrary"))(h, h, h, conv_w, par, *c_ins)
    return outs[:5], outs[5:]


def _chunk_tri(n, lower):
    r = lax.broadcasted_iota(jnp.int32, (n, n), 0)
    c = lax.broadcasted_iota(jnp.int32, (n, n), 1)
    shift = CHUNK.bit_length() - 1
    same = jnp.right_shift(r, shift) == jnp.right_shift(c, shift)
    return (same & ((c <= r) if lower else (c >= r))).astype(F32)


def _chunk_masks():
    r = lax.broadcasted_iota(jnp.int32, (CHUNK, CHUNK), 0)
    c = lax.broadcasted_iota(jnp.int32, (CHUNK, CHUNK), 1)
    return r >= c, r > c, r == c


def _split(a):
    hi = a.astype(BF16)
    return hi, (a - hi.astype(F32)).astype(BF16)


def _dot3(a, b):
    (ah, al), (bh, bl) = a, b
    d = lambda p, q: jnp.dot(p, q, preferred_element_type=F32)
    return d(ah, bh) + (d(ah, bl) + d(al, bh))


def _tri_inv_many(a_list, eye):
    d = lambda p, q: jnp.dot(p, q, preferred_element_type=F32)
    p = [(-a).astype(BF16) for a in a_list]
    tm = [eye - a for a in a_list]
    for _ in range(5):
        pf = [d(pi, pi) for pi in p]
        p = [x.astype(BF16) for x in pf]
        tm = [t + d(t.astype(BF16), pi) for t, pi in zip(tm, p)]
    ms = [_split(eye + a) for a in a_list]
    res = [eye - _dot3(m, _split(t)) for m, t in zip(ms, tm)]
    return [t + d(t.astype(BF16), r.astype(BF16)) for t, r in zip(tm, res)]


def _chunk_gates(bg_v, bgt_v, hd):
    return (bg_v[:, hd:hd + 1], bg_v[:, A_HEADS + hd:A_HEADS + hd + 1],
            None if bgt_v is None else bgt_v[A_HEADS + hd:A_HEADS + hd + 1, :])


WY_ROWS = 512
SCAN_ROWS = 512
WY_GROUP = 8


def _dn_wy(q, k, v, bg, bgt, *, name, carry=None):
    t = q.shape[0]
    rows = WY_ROWS

    c_ins, c_in_specs, c_out_specs, c_outs, c_scratch = _carry_specs(carry)

    def body(*refs):
        q_ref, k_ref, v_ref, bg_ref, bgt_ref, u_ref, w_ref, tm_ref, qk_ref = _carried(carry, refs, 5, 4, t // rows)
        causal, strict, diag = _chunk_masks()
        eye = diag.astype(F32)
        for c0 in range(0, rows // CHUNK, WY_GROUP):
            items = [(c, hd) for c in range(c0, c0 + WY_GROUP) for hd in range(A_HEADS)]
            rs = lambda c: slice(c * CHUNK, (c + 1) * CHUNK)
            sl = lambda hd: slice(hd * LANE, (hd + 1) * LANE)
            hs = lambda hd: slice(hd * CHUNK, (hd + 1) * CHUNK)
            gates = [_chunk_gates(bg_ref[rs(c), :], bgt_ref[:, rs(c)], hd) for c, hd in items]
            dms = [jnp.exp(jnp.where(causal, gcol - grow, NEG)) for _, gcol, grow in gates]
            kbs = [k_ref[rs(c), sl(hd)] * g[0] for (c, hd), g in zip(items, gates)]
            a_list = [jnp.where(strict, _dot_nt(kb, k_ref[rs(c), sl(hd)]) * dm, 0.0)
                      for (c, hd), kb, dm in zip(items, kbs, dms)]
            for (c, hd), dm in zip(items, dms):
                qk_ref[rs(c), hs(hd)] = jnp.where(
                    causal, _dot_nt(q_ref[rs(c), sl(hd)], k_ref[rs(c), sl(hd)]) * dm, 0.0)
            tms = _tri_inv_many(a_list, eye)
            for (c, hd), g, kb, tmat in zip(items, gates, kbs, tms):
                tm_ref[rs(c), hs(hd)] = tmat
                u_ref[rs(c), sl(hd)] = _dot(tmat, v_ref[rs(c), sl(hd)] * g[0])
                w_ref[rs(c), sl(hd)] = _dot(tmat, kb * jnp.exp(g[1])).astype(BF16)

    blk = pl.BlockSpec((rows, A_WIDTH), lambda i: (i, 0))
    half = pl.BlockSpec((rows, A_HEADS * CHUNK), lambda i: (i, 0))
    outs = pl.pallas_call(
        body, name=name, grid=(t // rows,),
        in_specs=[blk, blk, blk, pl.BlockSpec((rows, LANE), lambda i: (i, 0)),
                  pl.BlockSpec((SUBLANE, rows), lambda i: (0, i))] + c_in_specs,
        out_specs=[blk, blk, half, half] + c_out_specs,
        out_shape=[jax.ShapeDtypeStruct((t, A_WIDTH), F32), jax.ShapeDtypeStruct((t, A_WIDTH), BF16),
                   jax.ShapeDtypeStruct((t, A_HEADS * CHUNK), F32),
                   jax.ShapeDtypeStruct((t, A_HEADS * CHUNK), F32)] + c_outs,
        scratch_shapes=c_scratch,
        compiler_params=_cp("arbitrary"))(q, k, v, bg, bgt, *c_ins)
    return outs[:4], outs[4:]


def _dn_scan_fwd(q, k, u, w, qk, bg, *, name, carry=None):
    t = q.shape[0]
    rows = SCAN_ROWS
    per = rows // CHUNK
    c_ins, c_in_specs, c_out_specs, c_outs, c_scratch = _carry_specs(carry)

    def body(*refs):
        q_ref, k_ref, u_ref, w_ref, qk_ref, bg_ref, o_ref, vn_ref, s_ref, state = _carried(carry, refs, 6, 3, t // rows)

        @pl.when(pl.program_id(0) == 0)
        def _():
            state[...] = jnp.zeros_like(state)

        heads = range(A_HEADS)
        sl = lambda hd: slice(hd * LANE, (hd + 1) * LANE)
        s_cur = [state[hd] for hd in heads]
        for c in range(per):
            rs = slice(c * CHUNK, (c + 1) * CHUNK)
            bg_v = bg_ref[rs, :]
            gcols = [_chunk_gates(bg_v, None, hd)[1] for hd in heads]
            glasts = [gc[CHUNK - 1:CHUNK, :] for gc in gcols]
            for hd in heads:
                s_ref[c, hd] = s_cur[hd].astype(BF16)
            vns = [u_ref[rs, sl(hd)] - _dot(w_ref[rs, sl(hd)], s_cur[hd]) for hd in heads]
            qss = [_dot(q_ref[rs, sl(hd)] * jnp.exp(gcols[hd]), s_cur[hd]) for hd in heads]
            s_cur = [s_cur[hd] * jnp.exp(glasts[hd])
                     + _dot_tn(k_ref[rs, sl(hd)] * jnp.exp(glasts[hd] - gcols[hd]), vns[hd]) for hd in heads]
            for hd in heads:
                vn_ref[rs, sl(hd)] = vns[hd]
                o_ref[rs, sl(hd)] = qss[hd] + _dot(qk_ref[rs, hd * CHUNK:(hd + 1) * CHUNK], vns[hd])
        for hd in heads:
            state[hd] = s_cur[hd]

    blk = pl.BlockSpec((rows, A_WIDTH), lambda i: (i, 0))
    half = pl.BlockSpec((rows, A_HEADS * CHUNK), lambda i: (i, 0))
    wide = jax.ShapeDtypeStruct((t, A_WIDTH), F32)
    outs = pl.pallas_call(
        body, name=name, grid=(t // rows,),
        in_specs=[blk, blk, blk, blk, half, pl.BlockSpec((rows, LANE), lambda i: (i, 0))] + c_in_specs,
        out_specs=[blk, blk, pl.BlockSpec((per, A_HEADS, LANE, LANE), lambda i: (i, 0, 0, 0))] + c_out_specs,
        out_shape=[wide, wide, jax.ShapeDtypeStruct((t // CHUNK, A_HEADS, LANE, LANE), BF16)] + c_outs,
        scratch_shapes=[pltpu.VMEM((A_HEADS, LANE, LANE), F32)] + c_scratch,
        compiler_params=_cp("arbitrary"))(q, k, u, w, qk, bg, *c_ins)
    return outs[:3], outs[3:]


def _swa_neg_dist(n_blk):
    qi = lax.broadcasted_iota(jnp.int32, (BLOCK, 2 * BLOCK), 0)
    si = lax.broadcasted_iota(jnp.int32, (BLOCK, 2 * BLOCK), 1)
    dist = qi + BLOCK - si
    mask = (dist >= 0) & (dist < BLOCK) & ((si >= BLOCK) | (n_blk > 0))
    return jnp.where(mask, -dist.astype(F32), NEG)


def _stack_heads(ref, hk):
    return jnp.concatenate([ref[:, h * B_HEAD_DIM:(h + 1) * B_HEAD_DIM].astype(F32)
                            for h in range(hk * B_GROUP, (hk + 1) * B_GROUP)], axis=0)


def _swa_group_probs(q_ref, sk_ref, kband, vband, neg_dist):
    hks = range(B_KV_HEADS)
    heads = lambda hk: range(hk * B_GROUP, (hk + 1) * B_GROUP)
    ksl = lambda hk: slice(hk * B_HEAD_DIM, (hk + 1) * B_HEAD_DIM)
    ones = jnp.ones((2 * BLOCK, B_HEAD_DIM), BF16)
    qs = [_stack_heads(q_ref, hk) * (B_HEAD_DIM ** -0.5) for hk in hks]
    sink = [jnp.concatenate([jnp.broadcast_to(sk_ref[h:h + 1, 0:1], (BLOCK, 1)) for h in heads(hk)], axis=0)
            for hk in hks]
    s = [_dot_nt(qs[hk], kband[:, ksl(hk)]) + jnp.concatenate([ALIBI[h] * neg_dist for h in heads(hk)], axis=0)
         for hk in hks]
    m = [jnp.maximum(jnp.max(s[hk], axis=-1, keepdims=True), sink[hk]) for hk in hks]
    p = [jnp.exp(s[hk] - m[hk]) for hk in hks]
    oe = [jnp.dot(p[hk].astype(BF16), jnp.concatenate([vband[:, ksl(hk)].astype(BF16), ones], axis=1),
                  preferred_element_type=F32) for hk in hks]
    ps = [jnp.exp(sink[hk] - m[hk]) for hk in hks]
    inv = [1.0 / (oe[hk][:, B_HEAD_DIM:B_HEAD_DIM + 1] + ps[hk]) for hk in hks]
    return [(qs[hk], p[hk] * inv[hk], ps[hk] * inv[hk], oe[hk][:, :B_HEAD_DIM] * inv[hk]) for hk in hks]


def _swa_specs():
    qspec = lambda c0: pl.BlockSpec((BLOCK, B_WIDTH), lambda i: (i, c0 // B_WIDTH))
    cur = lambda c0: pl.BlockSpec((BLOCK, LANE), lambda i: (i, c0 // LANE))
    prev = lambda c0: pl.BlockSpec((BLOCK, LANE), lambda i: (jnp.maximum(i - 1, 0), c0 // LANE))
    return qspec, cur, prev


def _carried(carry, refs, n_in, n_out, steps):
    if carry is None:
        return refs
    ci, co = len(carry.ins), len(carry.outs)
    own = refs[:n_in] + refs[n_in + ci:n_in + ci + n_out] + refs[n_in + ci + n_out + co:len(refs) - 3]
    parts = refs[n_in:n_in + ci], refs[n_in + ci + n_out:n_in + ci + n_out + co], refs[len(refs) - 3:]

    @pl.when(pl.program_id(0) == 0)
    def _():
        carry.start(*parts)

    @pl.when(pl.program_id(0) == steps - 1)
    def _():
        carry.finish(*parts)

    return own


def _carry_specs(carry):
    if carry is None:
        return [], [], [], [], []
    return (list(carry.ins), [_ANY] * len(carry.ins), [_ANY] * len(carry.outs), list(carry.outs), carry.scratch())


def _swa_fwd(h, sinks_b, *, name, carry=None):
    t = h.shape[0]
    qspec, cur, prev = _swa_specs()
    c_ins, c_in_specs, c_out_specs, c_outs, c_scratch = _carry_specs(carry)

    def body(*refs):
        q_ref, kc_ref, kp_ref, vc_ref, vp_ref, sk_ref, o_ref = _carried(carry, refs, 6, 1, t // BLOCK)
        n_blk = pl.program_id(0)
        kband = jnp.concatenate([kp_ref[...], kc_ref[...]], axis=0)
        vband = jnp.concatenate([vp_ref[...], vc_ref[...]], axis=0)
        groups = _swa_group_probs(q_ref, sk_ref, kband, vband, _swa_neg_dist(n_blk))
        for hk, (_, _, _, o) in enumerate(groups):
            for g in range(B_GROUP):
                hq = hk * B_GROUP + g
                o_ref[:, hq * B_HEAD_DIM:(hq + 1) * B_HEAD_DIM] = o[g * BLOCK:(g + 1) * BLOCK]

    outs = pl.pallas_call(
        body, name=name, grid=(t // BLOCK,),
        in_specs=[qspec(C_QB), cur(C_KB), prev(C_KB), cur(C_VB), prev(C_VB),
                  pl.BlockSpec((B_Q_HEADS, LANE), lambda i: (0, 0))] + c_in_specs,
        out_specs=[pl.BlockSpec((BLOCK, B_WIDTH), lambda i: (i, 0))] + c_out_specs,
        out_shape=[jax.ShapeDtypeStruct((t, B_WIDTH), F32)] + c_outs,
        scratch_shapes=c_scratch,
        compiler_params=_cp("arbitrary"))(h, h, h, h, h, sinks_b, *c_ins)
    return outs[0], outs[1:]


def _rms_gate(o, za, nw):
    outs = []
    for hd in range(A_HEADS):
        oh = o[:, hd * LANE:(hd + 1) * LANE]
        r = lax.rsqrt(jnp.mean(oh * oh, -1, keepdims=True) + RMS_EPS)
        outs.append(oh * r * nw)
    return jnp.concatenate(outs, axis=1) * _silu(za)


def _out_ln(x, oa, ob, h, norm_w, w_out, ln_g, ln_b, *, tm, name, target=None):
    t = x.shape[0]
    last = target is not None

    def body(*refs):
        x_ref, oa_ref, ob_ref, za_ref, zb_ref, nw_ref, w_ref, g_ref, b_ref = refs[:9]
        xn_ref, mx_ref, r_ref = refs[9 + last:12 + last]
        ya = _rms_gate(oa_ref[...], za_ref[...].astype(F32), nw_ref[...])
        yb = ob_ref[...] * _silu(zb_ref[...].astype(F32))
        mixed = jnp.concatenate([ya, yb], axis=1).astype(BF16)
        mx_ref[...] = mixed
        r = DEEPNORM_ALPHA * x_ref[...] + jnp.dot(mixed, w_ref[...], preferred_element_type=F32)
        r_ref[...] = r
        mu = jnp.mean(r, -1, keepdims=True)
        xc = r - mu
        var = jnp.mean(xc * xc, -1, keepdims=True)
        xn = xc * lax.rsqrt(var + LN_EPS) * g_ref[...] + b_ref[...]
        if not last:
            xn_ref[...] = xn
            return
        loss_ref = refs[13]

        @pl.when(pl.program_id(0) == 0)
        def _():
            loss_ref[...] = jnp.zeros_like(loss_ref)

        err = xn - refs[9][...]
        xn_ref[...] = err * (1.0 / D_MODEL)
        loss_ref[...] += 0.5 / D_MODEL * jnp.sum(err * err)

    row = lambda w, c: pl.BlockSpec((tm, w), lambda i: (i, c))
    full = lambda a, b: pl.BlockSpec((a, b), lambda i: (0, 0))
    wide = jax.ShapeDtypeStruct((t, D_MODEL), F32)
    return pl.pallas_call(
        body, name=name, grid=(t // tm,),
        in_specs=[row(D_MODEL, 0), row(A_WIDTH, 0), row(B_WIDTH, 0), row(A_WIDTH, C_ZA // A_WIDTH),
                  row(B_WIDTH, C_ZB // B_WIDTH), full(1, LANE), full(D_MODEL, D_MODEL), full(1, D_MODEL),
                  full(1, D_MODEL)] + [row(D_MODEL, 0)] * last,
        out_specs=[row(D_MODEL, 0), row(D_MODEL, 0), row(D_MODEL, 0)] + [full(SUBLANE, LANE)] * last,
        out_shape=[wide, jax.ShapeDtypeStruct((t, D_MODEL), BF16), wide]
        + [jax.ShapeDtypeStruct((SUBLANE, LANE), F32)] * last,
        compiler_params=_cp("arbitrary" if last else "parallel"))(
        x, oa, ob, h, h, norm_w, w_out, ln_g, ln_b, *([target] if last else []))


def _layer_fwd(x, wt, conv_w, par, sinks_b, norm_w, w_out_bf, ln_g, ln_b, l, carries=None, target=None):
    carries = carries or {}
    h, got_in = _matmul_nt(x, wt, tm=512, name=f"in_proj_{l}", carry=carries.get("in_proj"))
    if callable(w_out_bf):
        w_out_bf = w_out_bf(got_in)
    (q, k, v, bg, bgt), got_pre = _dn_pre(h, conv_w, par, tt=512, name=f"dn_pre_{l}", carry=carries.get("dn_pre"))
    (u, w, tmat, qk), got_wy = _dn_wy(q, k, v, bg, bgt, name=f"dn_wy_{l}", carry=carries.get("dn_wy"))
    (oa, vn, s_all), got_scan = _dn_scan_fwd(q, k, u, w, qk, bg, name=f"dn_scan_{l}", carry=carries.get("dn_scan"))
    ob, got_swa = _swa_fwd(h, sinks_b, name=f"swa_fwd_{l}", carry=carries.get("swa"))
    xn, mixed, r, *loss = _out_ln(x, oa, ob, h, norm_w, w_out_bf, ln_g, ln_b, tm=512, name=f"out_ln_{l}", target=target)
    if loss:
        xn = (xn, loss[0])
    res = dict(x=x, h=h, q=q, k=k, v=v, bg=bg, bgt=bgt, w=w, tmat=tmat, qk=qk, vn=vn, oa=oa, s_all=s_all,
               mixed=mixed, r=r, w_out=w_out_bf)
    return xn, res, dict(in_proj=got_in, dn_pre=got_pre, dn_wy=got_wy, dn_scan=got_scan, swa=got_swa)


def _ln_out_bwd(dxn, r, mixed, ln_g, w_out, *, tm, name):
    t = dxn.shape[0]

    def body(dxn_ref, r_ref, mx_ref, g_ref, w_ref, dr_ref, dm_ref, dw_ref, dg_ref, db_ref):
        @pl.when(pl.program_id(0) == 0)
        def _():
            dw_ref[...] = jnp.zeros_like(dw_ref)
            dg_ref[...] = jnp.zeros_like(dg_ref)
            db_ref[...] = jnp.zeros_like(db_ref)

        rr = r_ref[...]
        xc = rr - jnp.mean(rr, -1, keepdims=True)
        rstd = lax.rsqrt(jnp.mean(xc * xc, -1, keepdims=True) + LN_EPS)
        xhat = xc * rstd
        dxn_v = dxn_ref[...]
        dxh = dxn_v * g_ref[...]
        dr = rstd * (dxh - jnp.mean(dxh, -1, keepdims=True) - xhat * jnp.mean(dxh * xhat, -1, keepdims=True))
        dr_ref[...] = dr
        dg_ref[...] += jnp.sum(dxn_v * xhat, axis=0, keepdims=True)
        db_ref[...] += jnp.sum(dxn_v, axis=0, keepdims=True)
        drb = dr.astype(BF16)
        dm_ref[...] = _dot_nt(drb, w_ref[...])
        dw_ref[...] += _dot_tn(mx_ref[...], drb)

    row = pl.BlockSpec((tm, D_MODEL), lambda i: (i, 0))
    full = lambda a, b: pl.BlockSpec((a, b), lambda i: (0, 0))
    big = jax.ShapeDtypeStruct((t, D_MODEL), F32)
    vec = jax.ShapeDtypeStruct((1, D_MODEL), F32)
    return pl.pallas_call(
        body, name=name, grid=(t // tm,),
        in_specs=[row, row, row, full(1, D_MODEL), full(D_MODEL, D_MODEL)],
        out_specs=[row, row, full(D_MODEL, D_MODEL), full(1, D_MODEL), full(1, D_MODEL)],
        out_shape=[big, big, jax.ShapeDtypeStruct((D_MODEL, D_MODEL), F32), vec, vec],
        compiler_params=_cp("arbitrary"))(dxn, r, mixed, ln_g, w_out)


def _dn_post_bwd(dm, oa, h, norm_w, *, tm, name):
    t = oa.shape[0]

    def body(dy_ref, o_ref, za_ref, nw_ref, do_ref, dza_ref, dnw_ref):
        @pl.when(pl.program_id(0) == 0)
        def _():
            dnw_ref[...] = jnp.zeros_like(dnw_ref)

        nw = nw_ref[...]
        dnw = jnp.zeros_like(nw)
        for hd in range(A_HEADS):
            sl = slice(hd * LANE, (hd + 1) * LANE)
            oh, za, dy = o_ref[:, sl], za_ref[:, sl].astype(F32), dy_ref[:, sl]
            rs = lax.rsqrt(jnp.mean(oh * oh, -1, keepdims=True) + RMS_EPS)
            nrm = oh * rs
            dza_ref[:, sl] = dy * nrm * nw * _dsilu(za)
            dn = dy * _silu(za)
            dnw = dnw + jnp.sum(dn * nrm, axis=0, keepdims=True)
            dnn = dn * nw
            do_ref[:, sl] = rs * dnn - oh * (rs * rs * rs) * jnp.mean(dnn * oh, -1, keepdims=True)
        dnw_ref[...] += dnw

    row = lambda c: pl.BlockSpec((tm, A_WIDTH), lambda i: (i, c))
    wide = jax.ShapeDtypeStruct((t, A_WIDTH), F32)
    return pl.pallas_call(
        body, name=name, grid=(t // tm,),
        in_specs=[row(0), row(0), row(C_ZA // A_WIDTH), pl.BlockSpec((1, LANE), lambda i: (0, 0))],
        out_specs=[row(0), row(C_ZA // A_WIDTH), pl.BlockSpec((1, LANE), lambda i: (0, 0))],
        out_shape=[wide, jax.ShapeDtypeStruct((t, DH_MAIN), F32), jax.ShapeDtypeStruct((1, LANE), F32)],
        compiler_params=_cp("arbitrary"))(dm, oa, h, norm_w)


def _dn_scan_bwd(q, k, w, qk, bg, do, *, name):
    t = q.shape[0]
    rows = SCAN_ROWS
    per = rows // CHUNK
    n = t // rows

    def body(q_ref, k_ref, w_ref, qk_ref, bg_ref, do_ref, dvn_ref, ds_ref, dstate):
        @pl.when(pl.program_id(0) == 0)
        def _():
            dstate[...] = jnp.zeros_like(dstate)

        heads = range(A_HEADS)
        sl = lambda hd: slice(hd * LANE, (hd + 1) * LANE)
        ds_cur = [dstate[hd] for hd in heads]
        for c in reversed(range(per)):
            rs = slice(c * CHUNK, (c + 1) * CHUNK)
            bg_v = bg_ref[rs, :]
            gcols = [_chunk_gates(bg_v, None, hd)[1] for hd in heads]
            glasts = [gc[CHUNK - 1:CHUNK, :] for gc in gcols]
            for hd in heads:
                ds_ref[c, hd] = ds_cur[hd].astype(BF16)
            pdo = [_dot_tn(qk_ref[rs, hd * CHUNK:(hd + 1) * CHUNK], do_ref[rs, sl(hd)]) for hd in heads]
            qdo = [_dot_tn(q_ref[rs, sl(hd)] * jnp.exp(gcols[hd]), do_ref[rs, sl(hd)]) for hd in heads]
            dvns = [pdo[hd] + _dot(k_ref[rs, sl(hd)] * jnp.exp(glasts[hd] - gcols[hd]), ds_cur[hd]) for hd in heads]
            ds_cur = [qdo[hd] + jnp.exp(glasts[hd]) * ds_cur[hd] - _dot_tn(w_ref[rs, sl(hd)], dvns[hd])
                      for hd in heads]
            for hd in heads:
                dvn_ref[rs, sl(hd)] = dvns[hd]
        for hd in heads:
            dstate[hd] = ds_cur[hd]

    blk = pl.BlockSpec((rows, A_WIDTH), lambda i: (n - 1 - i, 0))
    return pl.pallas_call(
        body, name=name, grid=(n,),
        in_specs=[blk, blk, blk, pl.BlockSpec((rows, A_HEADS * CHUNK), lambda i: (n - 1 - i, 0)),
                  pl.BlockSpec((rows, LANE), lambda i: (n - 1 - i, 0)), blk],
        out_specs=[blk, pl.BlockSpec((per, A_HEADS, LANE, LANE), lambda i: (n - 1 - i, 0, 0, 0))],
        out_shape=[jax.ShapeDtypeStruct((t, A_WIDTH), F32),
                   jax.ShapeDtypeStruct((t // CHUNK, A_HEADS, LANE, LANE), BF16)],
        scratch_shapes=[pltpu.VMEM((A_HEADS, LANE, LANE), F32)],
        compiler_params=_cp("arbitrary"))(q, k, w, qk, bg, do)


def _dn_chunk_bwd(q, k, v, vn, tmat, qk, bg, bgt, s_all, ds_all, dvn, do, *, name):
    t = q.shape[0]
    rows = WY_ROWS
    per = rows // CHUNK

    def body(q_ref, k_ref, v_ref, vn_ref, tm_ref, qk_ref, bg_ref, bgt_ref, s_ref, ds_ref, dvn_ref, do_ref,
             dq_ref, dk_ref, dv_ref, dbg_ref, dbgt_ref):
        causal, strict, _ = _chunk_masks()
        lane = lax.broadcasted_iota(jnp.int32, (CHUNK, LANE), 1)
        rowi = lax.broadcasted_iota(jnp.int32, (CHUNK, 1), 0)
        sub = lax.broadcasted_iota(jnp.int32, (SUBLANE, CHUNK), 0)
        rs = lambda c: slice(c * CHUNK, (c + 1) * CHUNK)
        sl = lambda hd: slice(hd * LANE, (hd + 1) * LANE)
        hs = lambda hd: slice(hd * CHUNK, (hd + 1) * CHUNK)
        for c0 in range(0, per, WY_GROUP):
            items = [(c, hd) for c in range(c0, c0 + WY_GROUP) for hd in range(A_HEADS)]
            at = lambda ref: [ref[rs(c), sl(hd)] for c, hd in items]
            qs, ks, vs, dos, vns, dvns = at(q_ref), at(k_ref), at(v_ref), at(do_ref), at(vn_ref), at(dvn_ref)
            tmhs = [tm_ref[rs(c), hs(hd)] for c, hd in items]
            ps = [qk_ref[rs(c), hs(hd)] for c, hd in items]
            gates = [_chunk_gates(bg_ref[rs(c), :], bgt_ref[:, rs(c)], hd) for c, hd in items]
            betas = [g[0] for g in gates]
            gcols = [g[1] for g in gates]
            dmats = [jnp.exp(jnp.where(causal, g[1] - g[2], NEG)) for g in gates]
            es = [jnp.exp(gc) for gc in gcols]
            glasts = [gc[CHUNK - 1:CHUNK, :] for gc in gcols]
            eks = [jnp.exp(gl - gc) for gl, gc in zip(glasts, gcols)]
            kbs = [kh * b for kh, b in zip(ks, betas)]
            vbs = [vh * b for vh, b in zip(vs, betas)]
            kbes = [kb * e for kb, e in zip(kbs, es)]

            a_s = [jnp.where(strict, _dot_nt(kb, kh) * dm, 0.0) for kb, kh, dm in zip(kbs, ks, dmats)]
            dps = [jnp.where(causal, _dot_nt(doh, vnh), 0.0) for doh, vnh in zip(dos, vns)]
            dqds = [_dot_nt(doh, s_ref[c, hd]) for doh, (c, hd) in zip(dos, items)]
            dkds = [_dot_nt(vnh, ds_ref[c, hd]) for vnh, (c, hd) in zip(vns, items)]
            dws = [-_dot_nt(dvnh, s_ref[c, hd]) for dvnh, (c, hd) in zip(dvns, items)]
            dvbs = [_dot_tn(tmh, dvnh) for tmh, dvnh in zip(tmhs, dvns)]
            dgts = [jnp.sum(s_ref[c, hd].astype(F32) * ds_ref[c, hd].astype(F32), keepdims=True) for c, hd in items]
            dts = [_dot_nt(dvnh, vb) + _dot_nt(dw, kbe) for dvnh, vb, dw, kbe in zip(dvns, vbs, dws, kbes)]
            dkbes = [_dot_tn(tmh, dw) for tmh, dw in zip(tmhs, dws)]
            xs = [_dot_nt(dt, tmh) for dt, tmh in zip(dts, tmhs)]
            das = [jnp.where(strict, -_dot_tn(tmh, x), 0.0) for tmh, x in zip(tmhs, xs)]
            dmas = [da * dm for da, dm in zip(das, dmats)]
            dmps = [dp * dm for dp, dm in zip(dps, dmats)]
            dkbs = [_dot(dma, kh) + dkbe * e for dma, kh, dkbe, e in zip(dmas, ks, dkbes, es)]
            for i, (c, hd) in enumerate(items):
                dq_ref[rs(c), sl(hd)] = _dot(dmps[i], ks[i]) + dqds[i] * es[i]
                dk_ref[rs(c), sl(hd)] = (_dot_tn(dmas[i], kbs[i]) + _dot_tn(dmps[i], qs[i]) + dkds[i] * eks[i]
                                         + dkbs[i] * betas[i])
                dv_ref[rs(c), sl(hd)] = dvbs[i] * betas[i]
            for c in range(c0, c0 + WY_GROUP):
                acc = jnp.zeros((CHUNK, LANE), F32)
                acc_t = jnp.zeros((SUBLANE, CHUNK), F32)
                for i, (ci, hd) in enumerate(items):
                    if ci != c:
                        continue
                    gmat = das[i] * a_s[i] + dps[i] * ps[i]
                    rk = jnp.sum(dkds[i] * ks[i], -1, keepdims=True) * eks[i]
                    de = (jnp.sum(dqds[i] * qs[i], -1, keepdims=True)
                          + jnp.sum(dkbes[i] * kbs[i], -1, keepdims=True))
                    dglast = jnp.sum(rk, keepdims=True) + dgts[i] * jnp.exp(glasts[i])
                    dgc = (jnp.sum(gmat, -1, keepdims=True) + de * es[i] - rk
                           + jnp.where(rowi == CHUNK - 1, dglast, 0.0))
                    dbeta = (jnp.sum(dkbs[i] * ks[i], -1, keepdims=True)
                             + jnp.sum(dvbs[i] * vs[i], -1, keepdims=True))
                    acc = acc + jnp.where(lane == hd, dbeta, 0.0) + jnp.where(lane == A_HEADS + hd, dgc, 0.0)
                    acc_t = acc_t + jnp.where(sub == A_HEADS + hd, -jnp.sum(gmat, axis=0, keepdims=True), 0.0)
                dbg_ref[rs(c), :] = acc
                dbgt_ref[:, rs(c)] = acc_t

    blk = pl.BlockSpec((rows, A_WIDTH), lambda i: (i, 0))
    half = pl.BlockSpec((rows, A_HEADS * CHUNK), lambda i: (i, 0))
    col = pl.BlockSpec((rows, LANE), lambda i: (i, 0))
    rowf = pl.BlockSpec((SUBLANE, rows), lambda i: (0, i))
    st = pl.BlockSpec((per, A_HEADS, LANE, LANE), lambda i: (i, 0, 0, 0))
    wide = jax.ShapeDtypeStruct((t, A_WIDTH), F32)
    return pl.pallas_call(
        body, name=name, grid=(t // rows,),
        in_specs=[blk, blk, blk, blk, half, half, col, rowf, st, st, blk, blk],
        out_specs=[blk, blk, blk, col, rowf],
        out_shape=[wide, wide, wide, jax.ShapeDtypeStruct((t, LANE), F32), jax.ShapeDtypeStruct((SUBLANE, t), F32)],
        compiler_params=_cp("parallel"))(q, k, v, vn, tmat, qk, bg, bgt, s_all, ds_all, dvn, do)


def _dn_pre_bwd(h, conv_w, par, dq, dk, dv, dbg, dbgt, *, tt, name):
    t = h.shape[0]
    cw = 3 * A_WIDTH
    hb = tt // HALO

    def body(pre_ref, halo_ref, bgi_ref, cw_ref, par_ref, dq_ref, dk_ref, dv_ref, dbg_ref, dbgt_ref,
             dc_ref, dbgi_ref, dpar_ref):
        i = pl.program_id(0)

        @pl.when(i == 0)
        def _():
            dpar_ref[...] = jnp.zeros_like(dpar_ref)

        cur = pre_ref[...].astype(F32)
        before = jnp.where(i > 0, halo_ref[...].astype(F32)[HALO - SUBLANE:], 0.0)
        c = _conv_fwd(cur, before, cw_ref[...])
        s = _silu(c)
        ds = _dsilu(c)
        for hd in range(A_HEADS):
            sl = slice(hd * LANE, (hd + 1) * LANE)
            for base, d_ref, scale in ((0, dq_ref, A_HEAD_DIM ** -0.5), (A_WIDTH, dk_ref, 1.0)):
                csl = slice(base + hd * LANE, base + (hd + 1) * LANE)
                tq = s[:, base + hd * LANE:base + (hd + 1) * LANE]
                dy = d_ref[:, sl]
                rq = lax.rsqrt(jnp.sum(tq * tq, -1, keepdims=True) + L2_EPS)
                dtq = scale * (rq * dy - tq * (rq * rq * rq) * jnp.sum(dy * tq, -1, keepdims=True))
                dc_ref[:, csl] = dtq * ds[:, base + hd * LANE:base + (hd + 1) * LANE]
        dc_ref[:, 2 * A_WIDTH:] = dv_ref[...] * ds[:, 2 * A_WIDTH:]
        raw = bgi_ref[...].astype(F32)
        lane = lax.broadcasted_iota(jnp.int32, raw.shape, 1)
        is_b = lane < A_HEADS
        is_a = (lane >= A_HEADS) & (lane < 2 * A_HEADS)
        rows_t = jnp.concatenate([dbgt_ref[...], jnp.zeros((LANE - SUBLANE, tt), F32)], axis=0)
        dbg_v = dbg_ref[...] + jnp.where(is_a, jnp.transpose(rows_t), 0.0)
        dbg_v = jnp.where(is_a, _dot_hi(_chunk_tri(tt, lower=False), jnp.where(is_a, dbg_v, 0.0)), dbg_v)
        beta = _sigmoid(raw)
        z = raw + par_ref[1:2, :]
        neg_ea = -jnp.exp(par_ref[0:1, :])
        g = neg_ea * _softplus(z)
        da = dbg_v * neg_ea * _sigmoid(z)
        dbgi_ref[...] = jnp.where(is_b, dbg_v * beta * (1.0 - beta), jnp.where(is_a, da, 0.0))
        dpar_ref[0:1, :] += jnp.sum(jnp.where(is_a, dbg_v * g, 0.0), axis=0, keepdims=True)
        dpar_ref[1:2, :] += jnp.sum(jnp.where(is_a, da, 0.0), axis=0, keepdims=True)

    wide = pl.BlockSpec((tt, A_WIDTH), lambda i: (i, 0))
    return pl.pallas_call(
        body, name=name, grid=(t // tt,),
        in_specs=[pl.BlockSpec((tt, cw), lambda i: (i, 0)),
                  pl.BlockSpec((HALO, cw), lambda i: (jnp.maximum(i * hb - 1, 0), 0)),
                  pl.BlockSpec((tt, LANE), lambda i: (i, C_BG // LANE)),
                  pl.BlockSpec((CONV_K, cw), lambda i: (0, 0)),
                  pl.BlockSpec((SUBLANE, LANE), lambda i: (0, 0)),
                  wide, wide, wide, pl.BlockSpec((tt, LANE), lambda i: (i, 0)),
                  pl.BlockSpec((SUBLANE, tt), lambda i: (0, i))],
        out_specs=[pl.BlockSpec((tt, cw), lambda i: (i, 0)), pl.BlockSpec((tt, LANE), lambda i: (i, 0)),
                   pl.BlockSpec((SUBLANE, LANE), lambda i: (0, 0))],
        out_shape=[jax.ShapeDtypeStruct((t, cw), F32), jax.ShapeDtypeStruct((t, LANE), F32),
                   jax.ShapeDtypeStruct((SUBLANE, LANE), F32)],
        compiler_params=_cp("arbitrary"))(h, h, h, conv_w, par, dq, dk, dv, dbg, dbgt)


def _conv_bwd(dc, h, conv_w, dh, *, tt, name):
    t = dc.shape[0]
    cw = 3 * A_WIDTH
    hb = tt // HALO
    nb = t // tt

    def body(dc_ref, after_ref, pre_ref, before_ref, cw_ref, dh_in_ref, dpre_ref, dcw_ref):
        i = pl.program_id(0)

        @pl.when(i == 0)
        def _():
            dcw_ref[...] = jnp.zeros_like(dcw_ref)

        dcv = dc_ref[...]
        after = jnp.where(i < nb - 1, after_ref[...], 0.0)
        cur = pre_ref[...].astype(F32)
        before = jnp.where(i > 0, before_ref[...].astype(F32)[HALO - SUBLANE:], 0.0)
        w = cw_ref[...]
        acc = dcv * w[CONV_K - 1:CONV_K, :]
        dcw_ref[CONV_K - 1:CONV_K, :] += jnp.sum(dcv * cur, axis=0, keepdims=True)
        for s in range(1, CONV_K):
            j = CONV_K - 1 - s
            acc = acc + _shift_up(dcv, after, s) * w[j:j + 1, :]
            dcw_ref[j:j + 1, :] += jnp.sum(dcv * _shift_down(cur, before, s), axis=0, keepdims=True)
        dpre_ref[...] = acc

    return pl.pallas_call(
        body, name=name, grid=(nb,),
        in_specs=[pl.BlockSpec((tt, cw), lambda i: (i, 0)),
                  pl.BlockSpec((SUBLANE, cw), lambda i: (jnp.minimum((i + 1) * (tt // SUBLANE), t // SUBLANE - 1), 0)),
                  pl.BlockSpec((tt, cw), lambda i: (i, 0)),
                  pl.BlockSpec((HALO, cw), lambda i: (jnp.maximum(i * hb - 1, 0), 0)),
                  pl.BlockSpec((CONV_K, cw), lambda i: (0, 0)), _ANY],
        out_specs=[pl.BlockSpec((tt, cw), lambda i: (i, 0)), pl.BlockSpec((SUBLANE, cw), lambda i: (0, 0))],
        out_shape=[jax.ShapeDtypeStruct(dh.shape, F32), jax.ShapeDtypeStruct((SUBLANE, cw), F32)],
        input_output_aliases={5: 0},
        compiler_params=_cp("arbitrary"))(dc, dc, h, h, conv_w, dh)


def _swa_bwd(h, dm, sinks_b, dh, *, name, carry=None):
    t = h.shape[0]
    qspec, cur, prev = _swa_specs()
    c_ins, c_in_specs, c_out_specs, c_outs, c_scratch = _carry_specs(carry)

    def body(*refs):
        (q_ref, kc_ref, kp_ref, vc_ref, vp_ref, zb_ref, dy_ref, sk_ref, dh_in_ref,
         dqz_ref, dk_ref, dv_ref, dsk_ref) = _carried(carry, refs, 9, 4, t // BLOCK)
        n_blk = pl.program_id(0)

        @pl.when(n_blk == 0)
        def _():
            dk_ref[...] = jnp.zeros_like(dk_ref)
            dv_ref[...] = jnp.zeros_like(dv_ref)
            dsk_ref[...] = jnp.zeros_like(dsk_ref)

        kband = jnp.concatenate([kp_ref[...], kc_ref[...]], axis=0)
        vband = jnp.concatenate([vp_ref[...], vc_ref[...]], axis=0)
        scale = B_HEAD_DIM ** -0.5
        hks = range(B_KV_HEADS)
        ksl = lambda hk: slice(hk * B_HEAD_DIM, (hk + 1) * B_HEAD_DIM)
        groups = _swa_group_probs(q_ref, sk_ref, kband, vband, _swa_neg_dist(n_blk))
        zbs = [_stack_heads(zb_ref, hk) for hk in hks]
        dys = [_stack_heads(dy_ref, hk) for hk in hks]
        dos = [dys[hk] * _silu(zbs[hk]) for hk in hks]
        deltas = [jnp.sum(dos[hk] * groups[hk][3], -1, keepdims=True) for hk in hks]
        dss = [groups[hk][1] * (_dot_nt(dos[hk], vband[:, ksl(hk)]) - deltas[hk]) for hk in hks]
        dqs = [_dot(dss[hk], kband[:, ksl(hk)]) * scale for hk in hks]
        dk_acc = [_dot_tn(dss[hk], groups[hk][0]) for hk in hks]
        dv_acc = [_dot_tn(groups[hk][1], dos[hk]) for hk in hks]
        for hk in hks:
            dzb = dys[hk] * groups[hk][3] * _dsilu(zbs[hk])
            dsink = groups[hk][2] * deltas[hk]
            for g in range(B_GROUP):
                hq = hk * B_GROUP + g
                rows = slice(g * BLOCK, (g + 1) * BLOCK)
                qsl = slice(hq * B_HEAD_DIM, (hq + 1) * B_HEAD_DIM)
                dqz_ref[:, qsl] = dqs[hk][rows]
                dqz_ref[:, B_WIDTH + hq * B_HEAD_DIM:B_WIDTH + (hq + 1) * B_HEAD_DIM] = dzb[rows]
                dsk_ref[hq:hq + 1, :] += -jnp.sum(dsink[rows], keepdims=True)
        dkb = jnp.concatenate(dk_acc, axis=1)
        dvb = jnp.concatenate(dv_acc, axis=1)
        at_cur = pl.ds(pl.multiple_of(n_blk * BLOCK, BLOCK), BLOCK)
        at_prev = pl.ds(pl.multiple_of(jnp.maximum(n_blk - 1, 0) * BLOCK, BLOCK), BLOCK)
        dk_ref[at_prev, :] += dkb[:BLOCK]
        dv_ref[at_prev, :] += dvb[:BLOCK]
        dk_ref[at_cur, :] += dkb[BLOCK:]
        dv_ref[at_cur, :] += dvb[BLOCK:]

    narrow = jax.ShapeDtypeStruct((t, B_KV_WIDTH), F32)
    res = lambda a, b: pl.BlockSpec((a, b), lambda i: (0, 0))
    outs = pl.pallas_call(
        body, name=name, grid=(t // BLOCK,),
        in_specs=[qspec(C_QB), cur(C_KB), prev(C_KB), cur(C_VB), prev(C_VB), qspec(C_ZB),
                  pl.BlockSpec((BLOCK, B_WIDTH), lambda i: (i, 1)), res(B_Q_HEADS, LANE), _ANY] + c_in_specs,
        out_specs=[pl.BlockSpec((BLOCK, 2 * B_WIDTH), lambda i: (i, C_QB // (2 * B_WIDTH))),
                   res(t, B_KV_WIDTH), res(t, B_KV_WIDTH), res(B_Q_HEADS, LANE)] + c_out_specs,
        out_shape=[jax.ShapeDtypeStruct(dh.shape, F32), narrow, narrow,
                   jax.ShapeDtypeStruct((B_Q_HEADS, LANE), F32)] + c_outs,
        scratch_shapes=c_scratch,
        input_output_aliases={8: 0},
        compiler_params=_cp("arbitrary"))(h, h, h, h, h, h, dm, sinks_b, dh, *c_ins)
    return outs[:4], outs[4:]


def _in_proj_dw(dh_main, dh_tail, x, *, tk, name):
    t, n = x.shape

    def body(a_ref, t_ref, x_ref, o_ref, ot_ref):
        @pl.when(pl.program_id(0) == 0)
        def _():
            o_ref[...] = jnp.zeros_like(o_ref)
            ot_ref[...] = jnp.zeros_like(ot_ref)

        xb = x_ref[...].astype(BF16)
        o_ref[...] += _dot_tn(a_ref[...], xb)
        ot_ref[...] += _dot_tn(t_ref[...], xb)

    row = lambda a: pl.BlockSpec((tk, a.shape[1]), lambda kk: (kk, 0))
    acc = lambda a: pl.BlockSpec((a.shape[1], n), lambda kk: (0, 0))
    return pl.pallas_call(
        body, name=name, grid=(t // tk,), in_specs=[row(dh_main), row(dh_tail), row(x)],
        out_specs=[acc(dh_main), acc(dh_tail)],
        out_shape=[jax.ShapeDtypeStruct((a.shape[1], n), F32) for a in (dh_main, dh_tail)],
        compiler_params=_cp("arbitrary"))(dh_main, dh_tail, x)


def _in_proj_dx(dh_main, dh_tail, wt, dr, *, tm, name, carry=None):
    t, n_main = dh_main.shape
    n_tail = dh_tail.shape[1]
    c_ins, c_in_specs, c_out_specs, c_outs, c_scratch = _carry_specs(carry)

    def body(*refs):
        a_ref, t_ref, wa_ref, wt_ref, r_ref, o_ref = _carried(carry, refs, 5, 1, t // tm)
        o_ref[...] = _dot(a_ref[...], wa_ref[...]) + _dot(t_ref[...], wt_ref[...]) + DEEPNORM_ALPHA * r_ref[...]

    row = lambda w: pl.BlockSpec((tm, w), lambda i: (i, 0))
    outs = pl.pallas_call(
        body, name=name, grid=(t // tm,),
        in_specs=[row(n_main), row(n_tail), pl.BlockSpec((n_main, D_MODEL), lambda i: (0, 0)),
                  pl.BlockSpec((n_tail, D_MODEL), lambda i: (n_main // n_tail, 0)), row(D_MODEL)] + c_in_specs,
        out_specs=[row(D_MODEL)] + c_out_specs,
        out_shape=[jax.ShapeDtypeStruct((t, D_MODEL), F32)] + c_outs,
        scratch_shapes=c_scratch,
        compiler_params=_cp("arbitrary"))(dh_main, dh_tail, wt, wt, dr, *c_ins)
    return outs[0], outs[1:]


def _layer_bwd(dxn, res, wt, conv_w, par, sinks_b, norm_w, w_out_bf, ln_g, l, carry=None, carry_dx=None):
    w_out_bf = res["w_out"]
    dr, dm, dw_out, dln_g, dln_b = _ln_out_bwd(dxn, res["r"], res["mixed"], ln_g, w_out_bf, tm=512, name=f"ln_out_bwd_{l}")
    h = res["h"]
    do, dh, dnw = _dn_post_bwd(dm, res["oa"], h, norm_w, tm=512, name=f"dn_post_bwd_{l}")
    dvn, ds_all = _dn_scan_bwd(res["q"], res["k"], res["w"], res["qk"], res["bg"], do, name=f"dn_scan_bwd_{l}")
    dq, dk, dv, dbg, dbgt = _dn_chunk_bwd(res["q"], res["k"], res["v"], res["vn"], res["tmat"], res["qk"], res["bg"],
                                          res["bgt"], res["s_all"], ds_all, dvn, do, name=f"dn_chunk_bwd_{l}")
    dc, dbgi, dpar = _dn_pre_bwd(h, conv_w, par, dq, dk, dv, dbg, dbgt, tt=512, name=f"dn_pre_bwd_{l}")
    dh, dcw = _conv_bwd(dc, h, conv_w, dh, tt=512, name=f"conv_bwd_{l}")
    (dh, dkb, dvb, dsk), carried = _swa_bwd(h, dm, sinks_b, dh, name=f"swa_bwd_{l}", carry=carry)
    dh_tail = jnp.concatenate([dkb, dvb, dbgi], axis=1)
    dwt_main, dwt_tail = _in_proj_dw(dh, dh_tail, res["x"], tk=512, name=f"in_proj_dw_{l}")
    grads = dict(w_in=(dwt_main, dwt_tail), conv_w=dcw[:CONV_K], a_log=dpar[0, A_HEADS:2 * A_HEADS],
                 dt_bias=dpar[1, A_HEADS:2 * A_HEADS], norm_w=dnw[0], sinks=dsk[:, 0], w_out=dw_out,
                 ln_g=dln_g[0], ln_b=dln_b[0])
    dx, carried_dx = _in_proj_dx(dh, dh_tail, wt, dr, tm=512, name=f"in_proj_dx_{l}",
                                 carry=None if carry_dx is None else carry_dx(grads))
    return dx, grads, carried, carried_dx


def _layer_args(wt, conv_w, a_log, dt_bias, sinks, norm_w, w_out_bf):
    return (wt, conv_w, _gate_params(a_log, dt_bias), jnp.broadcast_to(sinks[:, None], (B_Q_HEADS, LANE)),
            norm_w[None], w_out_bf)


def _local_step(x, target, args0, args1, ln_g, ln_b, gathers=None, reduce1=None, reduce0=None):
    assert DEPTH == 2
    x1, res0, got = _layer_fwd(x, *args0, ln_g[0][None], ln_b[0][None], 0, carries=gathers)
    if gathers is not None:
        args1 = args1(got)
    (dx, loss_tile), res1, _ = _layer_fwd(x1, *args1, ln_g[1][None], ln_b[1][None], 1, target=target)
    dx, grads1, _, _ = _layer_bwd(dx, res1, *args1, ln_g[1][None], 1)
    carry = None if reduce1 is None else reduce1(grads1)
    carry_dx = None if reduce0 is None else (lambda grads0: reduce0(grads0, grads1, loss_tile))
    dx, grads0, landed1, landed0 = _layer_bwd(dx, res0, *args0, ln_g[0][None], 0, carry=carry, carry_dx=carry_dx)
    return loss_tile, dx, [grads0, grads1], landed1, landed0


_ANY = pl.BlockSpec(memory_space=pl.ANY)
_MESH = pl.DeviceIdType.MESH


HALF = D_MODEL // 2


class _Exchange:
    def __init__(self, ins, outs, n_remote, n_local, plan):
        self.ins, self.outs, self.n_remote, self.n_local, self.plan = tuple(ins), tuple(outs), n_remote, n_local, plan

    def scratch(self):
        return [pltpu.SemaphoreType.DMA((self.n_remote,)), pltpu.SemaphoreType.DMA((self.n_remote,)),
                pltpu.SemaphoreType.DMA((max(self.n_local, 1),))]

    def _copies(self, in_refs, out_refs, sems, arriving):
        send_sems, recv_sems, local_sems = sems
        local, sends, recvs = self.plan(in_refs, out_refs)
        loc = [pltpu.make_async_copy(s, d, local_sems.at[i]) for i, (s, d) in enumerate(local)]
        rem = [pltpu.make_async_remote_copy(src_ref=s, dst_ref=recvs[i] if arriving else d, send_sem=send_sems.at[i],
                                            recv_sem=recv_sems.at[i], device_id=peer, device_id_type=_MESH)
               for i, (s, d, peer) in enumerate(sends)]
        return loc, rem

    def start(self, in_refs, out_refs, sems):
        loc, rem = self._copies(in_refs, out_refs, sems, arriving=False)
        for cp in loc + rem:
            cp.start()

    def finish(self, in_refs, out_refs, sems):
        loc, rem = self._copies(in_refs, out_refs, sems, arriving=True)
        for cp in rem:
            cp.wait_recv()
        for cp in rem:
            cp.wait_send()
        for cp in loc:
            cp.wait()


def _run_exchange(ex, *, name):
    n_in, n_out = len(ex.ins), len(ex.outs)

    def body(*refs):
        parts = refs[:n_in], refs[n_in:n_in + n_out], refs[n_in + n_out:]
        ex.start(*parts)
        ex.finish(*parts)

    return pl.pallas_call(body, name=name, in_specs=[_ANY] * n_in, out_specs=[_ANY] * n_out, out_shape=list(ex.outs),
                          scratch_shapes=ex.scratch())(*ex.ins)


def _place():
    x, y, c = lax.axis_index("x"), lax.axis_index("y"), lax.axis_index("c")
    return x, y, c, [(1 - x, y), (x, 1 - y), (1 - x, 1 - y)]


def _gather_exchange(arrays):
    n = len(arrays)

    def plan(src, dst):
        x, y, c, chips = _place()
        me = 2 * x + y
        local = [(src[k], dst[k].at[me]) for k in range(n)]
        sends = [(src[k], dst[k].at[me], (px, py, c)) for k in range(n) for px, py in chips]
        recvs = [dst[k].at[2 * px + py] for k in range(n) for px, py in chips]
        return local, sends, recvs

    return _Exchange(arrays, [jax.ShapeDtypeStruct((N_SHARD,) + a.shape, a.dtype) for a in arrays], 3 * n, n, plan)


def _gather_two_level(pack, conv_w, *, name):
    rows = pack.shape[0]
    part_rows = rows // 2

    def body(pack_ref, conv_ref, land_ref, conv_land_ref, send1, recv1, send2, recv2, csend, crecv, local_sems):
        x, y, c, chips = _place()
        me = 2 * x + y
        sibling = (x, y, 1 - c)
        part = lambda core: pl.ds(pl.multiple_of(core * part_rows, 16), part_rows)
        remote = lambda src, dst, ss, rs, to: pltpu.make_async_remote_copy(
            src_ref=src, dst_ref=dst, send_sem=ss, recv_sem=rs, device_id=to, device_id_type=_MESH)
        local = [pltpu.make_async_copy(pack_ref, land_ref.at[me], local_sems.at[0]),
                 pltpu.make_async_copy(conv_ref, conv_land_ref.at[me], local_sems.at[1])]
        for cp in local:
            cp.start()
        first = [remote(pack_ref.at[part(c)], land_ref.at[me, part(c)], send1.at[j], recv1.at[j], (px, py, c))
                 for j, (px, py) in enumerate(chips)]
        convs = [remote(conv_ref, conv_land_ref.at[me], csend.at[j], crecv.at[j], (px, py, c))
                 for j, (px, py) in enumerate(chips)]
        for cp in first + convs:
            cp.start()
        passed = []
        for j, (px, py) in enumerate(chips):
            slot = 2 * px + py
            remote(pack_ref.at[part(c)], land_ref.at[slot, part(c)], send1.at[j], recv1.at[j], (px, py, c)).wait_recv()
            cp = remote(land_ref.at[slot, part(c)], land_ref.at[slot, part(c)], send2.at[j], recv2.at[j], sibling)
            cp.start()
            passed.append(cp)
        for j, (px, py) in enumerate(chips):
            slot = 2 * px + py
            remote(land_ref.at[slot, part(1 - c)], land_ref.at[slot, part(1 - c)], send2.at[j], recv2.at[j],
                   sibling).wait_recv()
            remote(conv_ref, conv_land_ref.at[slot], csend.at[j], crecv.at[j], (px, py, c)).wait_recv()
        for cp in first + convs + passed:
            cp.wait_send()
        for cp in local:
            cp.wait()

    sems = [pltpu.SemaphoreType.DMA((3,))] * 6 + [pltpu.SemaphoreType.DMA((2,))]
    return pl.pallas_call(
        body, name=name, in_specs=[_ANY, _ANY], out_specs=[_ANY, _ANY],
        out_shape=[jax.ShapeDtypeStruct((N_SHARD,) + pack.shape, pack.dtype),
                   jax.ShapeDtypeStruct((N_SHARD,) + conv_w.shape, conv_w.dtype)],
        scratch_shapes=sems)(pack, conv_w)


def _half(core):
    return pl.ds(pl.multiple_of(core * HALF, HALF), HALF)


def _reduce_scatter_exchange(g):
    def plan(src, dst):
        x, y, c, chips = _place()
        peers = [(px, py, c if t == 0 else 1 - c) for px, py in chips for t in (0, 1)] + [(x, y, 1 - c)]
        sends = [(src[0].at[2 * px + py, :, _half(pc)], dst[0].at[k], (px, py, pc)) for k, (px, py, pc) in enumerate(peers)]
        return [], sends, [dst[0].at[k] for k in range(7)]

    return _Exchange([g], [jax.ShapeDtypeStruct((7,) + g.shape[1:2] + (HALF,), g.dtype)], 7, 0, plan)


def _pair_window_exchange(g):
    def plan(src, dst):
        x, y, c, _ = _place()
        return [], [(src[0].at[:, :, _half(1 - c)], dst[0], (x, y, 1 - c))], [dst[0]]

    return _Exchange([g], [jax.ShapeDtypeStruct(g.shape[:2] + (HALF,), g.dtype)], 1, 0, plan)


def _chip_scatter_exchange(p, small):
    def plan(src, dst):
        x, y, c, chips = _place()
        mine = 4 * x + 2 * y + c
        peers = [(px, py, c if t == 0 else 1 - c) for px, py in chips for t in (0, 1)] + [(x, y, 1 - c)]
        sends = [(src[0].at[2 * px + py], dst[0].at[j], (px, py, c)) for j, (px, py) in enumerate(chips)]
        recvs = [dst[0].at[j] for j in range(3)]
        sends += [(src[1], dst[1].at[mine], peer) for peer in peers]
        recvs += [dst[1].at[4 * px + 2 * py + pc] for px, py, pc in peers]
        return [(src[1], dst[1].at[mine])], sends, recvs

    outs = [jax.ShapeDtypeStruct((3,) + p.shape[1:], p.dtype), jax.ShapeDtypeStruct((8,) + small.shape, small.dtype)]
    return _Exchange([p, small], outs, 10, 1, plan)


def _share_exchange(arrays):
    n = len(arrays)

    def plan(src, dst):
        x, y, c, _ = _place()
        return [], [(src[k], dst[k], (x, y, 1 - c)) for k in range(n)], [dst[k] for k in range(n)]

    return _Exchange(arrays, [jax.ShapeDtypeStruct(a.shape, a.dtype) for a in arrays], n, 0, plan)


def _sum_scatter(g, land, me, core, *, tc, name):
    rows = g.shape[1]
    per = HALF // tc

    def body(where_ref, g_ref, land_ref, o_ref):
        acc = g_ref[...]
        for k in range(7):
            acc = acc + land_ref[k].astype(F32)
        o_ref[...] = acc

    return pl.pallas_call(
        body, name=name, out_shape=jax.ShapeDtypeStruct((rows, HALF), F32), compiler_params=_cp("parallel"),
        grid_spec=pltpu.PrefetchScalarGridSpec(
            num_scalar_prefetch=1, grid=(per,),
            in_specs=[pl.BlockSpec((None, rows, tc), lambda i, w: (w[0], 0, w[1] * per + i)),
                      pl.BlockSpec((7, rows, tc), lambda i, w: (0, 0, i))],
            out_specs=pl.BlockSpec((rows, tc), lambda i, w: (0, i))))(
        jnp.stack([me, core]).astype(jnp.int32), g, land)


def _pair_add(g, land, core, *, name):
    n, rows, _ = g.shape

    def body(core_ref, g_ref, land_ref, o_ref):
        o_ref[...] = (g_ref[...].astype(F32) + land_ref[...].astype(F32)).astype(o_ref.dtype)

    blk = pl.BlockSpec((1, rows, HALF), lambda i, w: (i, 0, 0))
    return pl.pallas_call(
        body, name=name, out_shape=jax.ShapeDtypeStruct((n, rows, HALF), g.dtype), compiler_params=_cp("parallel"),
        grid_spec=pltpu.PrefetchScalarGridSpec(
            num_scalar_prefetch=1, grid=(n,),
            in_specs=[pl.BlockSpec((1, rows, HALF), lambda i, w: (i, 0, w[0])), blk], out_specs=blk))(
        jnp.reshape(core, (1,)).astype(jnp.int32), g, land)


def _sum_chips(p, land, me, *, tc, name):
    rows = p.shape[1]

    def body(me_ref, p_ref, land_ref, o_ref):
        acc = p_ref[...].astype(F32)
        for k in range(3):
            acc = acc + land_ref[k].astype(F32)
        o_ref[...] = acc

    return pl.pallas_call(
        body, name=name, out_shape=jax.ShapeDtypeStruct((rows, HALF), F32), compiler_params=_cp("parallel"),
        grid_spec=pltpu.PrefetchScalarGridSpec(
            num_scalar_prefetch=1, grid=(HALF // tc,),
            in_specs=[pl.BlockSpec((None, rows, tc), lambda i, w: (w[0], 0, i)),
                      pl.BlockSpec((3, rows, tc), lambda i, w: (0, 0, i))],
            out_specs=pl.BlockSpec((rows, tc), lambda i, w: (0, i))))(
        jnp.reshape(me, (1,)).astype(jnp.int32), p, land)


def _sum_slots(a, *, name):
    n = a.shape[0]

    def body(a_ref, o_ref):
        acc = a_ref[0]
        for k in range(1, n):
            acc = acc + a_ref[k]
        o_ref[...] = acc

    return pl.pallas_call(body, name=name, out_shape=jax.ShapeDtypeStruct(a.shape[1:], a.dtype))(a)


def _elementwise(fn, ins, n_out, block, *, name):
    shape = ins[0].shape
    grid = tuple(s // b for s, b in zip(shape, block))
    n_in = len(ins)

    def body(*refs):
        outs = fn(*[r[...] for r in refs[:n_in]])
        for o_ref, val in zip(refs[n_in:], outs):
            o_ref[...] = val

    spec = pl.BlockSpec(block, lambda i, j, k: (i, j, k))
    return pl.pallas_call(body, name=name, grid=grid, in_specs=[spec] * n_in, out_specs=[spec] * n_out,
                          out_shape=[jax.ShapeDtypeStruct(shape, F32)] * n_out,
                          compiler_params=_cp(*["parallel"] * 3))(*ins)


def _adamw_math(w, g, m, v):
    mn = ADAM_B1 * m + (1.0 - ADAM_B1) * g
    vn = ADAM_B2 * v + (1.0 - ADAM_B2) * (g * g)
    m_hat = mn / (1.0 - ADAM_B1 ** ADAM_STEP)
    v_hat = vn / (1.0 - ADAM_B2 ** ADAM_STEP)
    return -ADAM_LR * (m_hat / (jnp.sqrt(v_hat) + ADAM_EPS) + ADAM_WD * w), mn, vn


def _adamw(w, g, m, v, block, *, name):
    return _elementwise(_adamw_math, [w, g, m, v], 3, block, name=name)


def _interleave_layers(layers, *, tc, name):
    rows, cols = layers[0].shape
    n = len(layers)

    def body(*refs):
        for l in range(n):
            refs[n][:, l, :] = refs[l][...]

    return pl.pallas_call(body, name=name, grid=(cols // tc,),
                          in_specs=[pl.BlockSpec((rows, tc), lambda i: (0, i))] * n,
                          out_specs=pl.BlockSpec((rows, n, tc), lambda i: (0, 0, i)),
                          out_shape=jax.ShapeDtypeStruct((rows, n, cols), layers[0].dtype),
                          compiler_params=_cp("parallel"))(*layers)


def _adamw_small(ws, gs, ms, vs, *, name):
    n = len(ws)

    def body(*refs):
        w, g, m, v, outs = refs[:n], refs[n:2 * n], refs[2 * n:3 * n], refs[3 * n:4 * n], refs[4 * n:]
        for k in range(n):
            for slot, val in enumerate(_adamw_math(w[k][...], g[k][...], m[k][...], v[k][...])):
                outs[slot * n + k][...] = val

    outs = pl.pallas_call(body, name=name, out_shape=[jax.ShapeDtypeStruct(a.shape, F32) for a in ws] * 3)(
        *ws, *gs, *ms, *vs)
    return outs[:n], outs[n:2 * n], outs[2 * n:]


def _to_kernel_order(wt):
    gates = jnp.pad(wt[2048:2056], ((0, LANE - 2 * A_HEADS), (0, 0)))
    return jnp.concatenate([wt[0:2048], wt[2056:2568], wt[2824:3336], wt[2568:2696], wt[2696:2824], gates], axis=0)


def _from_kernel_order(main, tail):
    return jnp.concatenate([main[0:2048], tail[C_BG - DH_MAIN:C_BG - DH_MAIN + 2 * A_HEADS],
                            main[C_QB:C_QB + B_WIDTH], tail[0:B_KV_WIDTH], tail[B_KV_WIDTH:2 * B_KV_WIDTH],
                            main[C_ZB:C_ZB + B_WIDTH]], axis=0)


def _gate_params(a_log, dt_bias):
    return jnp.pad(jnp.stack([a_log, dt_bias]), ((0, SUBLANE - 2), (A_HEADS, LANE - 2 * A_HEADS)))


SMALL = ("conv_w", "a_log", "dt_bias", "norm_w", "sinks", "ln_g", "ln_b")


def _pack(parts, cols):
    flat = jnp.concatenate([p.reshape(-1) for p in parts])
    rows = -(-flat.shape[0] // cols)
    return jnp.pad(flat, (0, rows * cols - flat.shape[0])).reshape(rows, cols)


def _unpack(packed, shapes):
    flat = packed.reshape(-1)
    out, at = [], 0
    for s in shapes:
        n = math.prod(s)
        out.append(flat[at:at + n].reshape(s))
        at += n
    return out


def kernel(x, w_in, conv_w, a_log, dt_bias, norm_w, sinks, w_out, ln_g, ln_b, loss_target, m_w_in, m_conv_w, m_a_log, m_dt_bias, m_norm_w, m_sinks, m_w_out, m_ln_g, m_ln_b, v_w_in, v_conv_w, v_a_log, v_dt_bias, v_norm_w, v_sinks, v_w_out, v_ln_g, v_ln_b):
    xi, yi, ci = lax.axis_index("x"), lax.axis_index("y"), lax.axis_index("c")
    me = 2 * xi + yi

    to_t = lambda a: jnp.transpose(a, (2, 0, 1))
    from_t = lambda a: jnp.transpose(a, (1, 2, 0))

    wt_shard = to_t(w_in)

    def pack_weights(l):
        rows = jnp.pad(wt_shard[:, l], ((0, IN_PAD - IN_SHARD), (0, 0)))
        return jnp.concatenate([rows, w_out[l]], axis=0).astype(BF16)

    pack0, pack1 = pack_weights(0), pack_weights(1)
    got_in0, g_conv = _gather_two_level(pack0[:IN_PAD], conv_w, name="gather_weights_0")
    conv_full = jnp.moveaxis(g_conv, 0, 2).reshape(DEPTH, CONV_K, 3 * A_WIDTH)
    piece = IN_PAD // 3
    carriers = ("dn_pre", "dn_wy", "dn_scan")
    gathers = {nm: _gather_exchange([pack1[i * piece:(i + 1) * piece]]) for i, nm in enumerate(carriers)}
    gathers.update(in_proj=_gather_exchange([pack0[IN_PAD:]]), swa=_gather_exchange([pack1[IN_PAD:]]))
    w_in_of = lambda rows: _to_kernel_order(rows[:, :IN_SHARD].reshape(IN_COLS, D_MODEL))
    w_out_of = lambda rows: rows.reshape(D_MODEL, D_MODEL)
    args0 = _layer_args(w_in_of(got_in0), conv_full[0], a_log[0], dt_bias[0], sinks[0], norm_w[0],
                        lambda got: w_out_of(got[0]))

    def args1(got):
        rows = jnp.concatenate([got[nm][0] for nm in carriers], axis=1)
        return _layer_args(w_in_of(rows), conv_full[1], a_log[1], dt_bias[1], sinks[1], norm_w[1],
                           w_out_of(got["swa"][0]))

    def pack_grads(g):
        gin = _from_kernel_order(*g["w_in"]).reshape(N_SHARD, IN_SHARD, D_MODEL)
        gin = jnp.pad(gin, ((0, 0), (0, IN_PAD - IN_SHARD), (0, 0)))
        return jnp.concatenate([gin, g["w_out"].reshape(N_SHARD, OUT_SHARD, D_MODEL)], axis=1).astype(BF16)

    packed = {}

    def reduce1(grads1):
        packed[1] = pack_grads(grads1)
        return _reduce_scatter_exchange(packed[1])

    def reduce0(grads0, grads1, loss_tile):
        g0 = pack_grads(grads0)
        from_sibling = _run_exchange(_pair_window_exchange(g0), name="pair_reduce_0")[0]
        packed[0] = _pair_add(g0, from_sibling, ci, name="pair_add_0")
        gsmall = _pack([jnp.stack([g[nm] for g in (grads0, grads1)]) for nm in SMALL] + [loss_tile[0, 0:1]], D_MODEL)
        return _chip_scatter_exchange(packed[0], gsmall)

    _, dx, grads, landed1, (landed0, landed_small) = _local_step(
        x[0], loss_target[0], args0, args1, ln_g, ln_b, gathers=gathers, reduce1=reduce1, reduce0=reduce0)

    small_shapes = [(DEPTH,) + grads[0][nm].shape for nm in SMALL]
    halves = [_sum_chips(packed[0], landed0, me, tc=2 * LANE, name="reduce_sum_0"),
              _sum_scatter(packed[1], landed1[0], me, ci, tc=2 * LANE, name="reduce_sum_1")]
    s_small = _sum_slots(landed_small, name="reduce_sum_small")
    others = _run_exchange(_share_exchange(halves), name="pair_share")
    full = [jnp.where(ci == 0, jnp.concatenate([mine, other], axis=1), jnp.concatenate([other, mine], axis=1))
            for mine, other in zip(halves, others)]
    grad_in_layers = [f[:IN_SHARD] for f in full]
    grad_out = jnp.stack([f[IN_PAD:] for f in full])
    out_blk = (1, OUT_SHARD, D_MODEL)
    *small_grads, loss = _unpack(s_small, small_shapes + [()])
    gs = dict(zip(SMALL, small_grads))
    gs["conv_w"] = lax.dynamic_slice_in_dim(gs["conv_w"], me * CONV_SHARD, CONV_SHARD, axis=2)

    grad_in_t = _interleave_layers(grad_in_layers, tc=2 * LANE, name="grad_in_layers")
    d_in, nm_in, nv_in = (from_t(o) for o in _adamw(to_t(w_in), grad_in_t, to_t(m_w_in), to_t(v_w_in),
                                                    (IN_SHARD // 6, DEPTH, D_MODEL), name="adamw_in"))
    grad_in = from_t(grad_in_t)
    d_out, nm_out, nv_out = _adamw(w_out, grad_out, m_w_out, v_w_out, out_blk, name="adamw_out")
    ws = dict(conv_w=conv_w, a_log=a_log, dt_bias=dt_bias, norm_w=norm_w, sinks=sinks, ln_g=ln_g, ln_b=ln_b)
    ms = dict(conv_w=m_conv_w, a_log=m_a_log, dt_bias=m_dt_bias, norm_w=m_norm_w, sinks=m_sinks, ln_g=m_ln_g, ln_b=m_ln_b)
    vs = dict(conv_w=v_conv_w, a_log=v_a_log, dt_bias=v_dt_bias, norm_w=v_norm_w, sinks=v_sinks, ln_g=v_ln_g, ln_b=v_ln_b)
    d_s, nm_s, nv_s = (dict(zip(SMALL, o)) for o in _adamw_small(*[[d[nm] for nm in SMALL] for d in (ws, gs, ms, vs)],
                                                                 name="adamw_small"))

    def in_order(big_in, small, big_out):
        return (big_in, small["conv_w"], small["a_log"], small["dt_bias"], small["norm_w"], small["sinks"], big_out,
                small["ln_g"], small["ln_b"])

    return (loss, dx[None], *in_order(grad_in, gs, grad_out), *in_order(d_in, d_s, d_out),
            *in_order(nm_in, nm_s, nm_out), *in_order(nv_in, nv_s, nv_out))
```

```python
import math

import jax
import jax.numpy as jnp
from jax import lax
from jax.experimental import pallas as pl
from jax.experimental.pallas import tpu as pltpu

F32 = jnp.float32
BF16 = jnp.bfloat16
HI = lax.Precision.HIGHEST

D_MODEL = 1024
DEPTH = 2
A_HEADS = 4
A_HEAD_DIM = 128
A_WIDTH = 512
CONV_K = 4
CHUNK = 64
B_Q_HEADS = 8
B_KV_HEADS = 2
B_HEAD_DIM = 64
B_GROUP = 4
B_WIDTH = 512
B_KV_WIDTH = 128
BLOCK = 128
IN_COLS = 3336
DEEPNORM_ALPHA = (2 * DEPTH) ** 0.25
LN_EPS = 1e-5
RMS_EPS = 1e-6
L2_EPS = 1e-6
ADAM_LR = 0.001
ADAM_B1 = 0.9
ADAM_B2 = 0.999
ADAM_EPS = 1e-08
ADAM_WD = 0.01
ADAM_STEP = 10

N_SHARD = 4
IN_SHARD = IN_COLS // N_SHARD
OUT_SHARD = D_MODEL // N_SHARD
CONV_SHARD = 3 * A_WIDTH // N_SHARD
IN_PAD = -(-IN_SHARD // 96) * 96

P_COLS = 3456
C_PRE = 0
C_ZA = 1536
C_QB = 2048
C_ZB = 2560
C_KB = 3072
C_VB = 3200
C_BG = 3328
DH_MAIN = C_KB
LANE = 128
SUBLANE = 8
HALO = 16
VMEM_LIMIT = 56 * 1024 * 1024
ALIBI = tuple(2.0 ** (-8.0 * (h + 1) / B_Q_HEADS) for h in range(B_Q_HEADS))
NEG = -1e30


def _cp(*sem):
    return pltpu.CompilerParams(dimension_semantics=sem, vmem_limit_bytes=VMEM_LIMIT)


def _dot(a, b):
    return jnp.dot(a.astype(BF16), b.astype(BF16), preferred_element_type=F32)


def _dot_nt(a, b):
    return lax.dot_general(a.astype(BF16), b.astype(BF16), (((1,), (1,)), ((), ())),
                           preferred_element_type=F32)


def _dot_tn(a, b):
    return lax.dot_general(a.astype(BF16), b.astype(BF16), (((0,), (0,)), ((), ())),
                           preferred_element_type=F32)


def _dot_hi(a, b):
    return jnp.dot(a, b, precision=HI, preferred_element_type=F32)


def _sigmoid(x):
    return jax.nn.sigmoid(x)


def _silu(x):
    return x * _sigmoid(x)


def _silu_and_grad(x):
    s = _sigmoid(x)
    return x * s, s * (1.0 + x * (1.0 - s))


def _softplus(x):
    return jnp.maximum(x, 0.0) + jnp.log(1.0 + jnp.exp(-jnp.abs(x)))


def _shift_down(cur, before, s):
    if s == 0:
        return cur
    r = pltpu.roll(cur, s, 0)
    rb = pltpu.roll(before, s, 0)
    row = lax.broadcasted_iota(jnp.int32, before.shape, 0)
    head = jnp.where(row < s, rb, r[0:SUBLANE])
    return jnp.concatenate([head, r[SUBLANE:]], axis=0)


def _shift_up(cur, after, s):
    if s == 0:
        return cur
    n = cur.shape[0]
    r = pltpu.roll(cur, n - s, 0)
    ra = pltpu.roll(after, SUBLANE - s, 0)
    row = lax.broadcasted_iota(jnp.int32, after.shape, 0)
    tail = jnp.where(row >= SUBLANE - s, ra, r[n - SUBLANE:])
    return jnp.concatenate([r[:n - SUBLANE], tail], axis=0)


def _conv_fwd(cur, before, w):
    acc = cur * w[CONV_K - 1:CONV_K, :]
    for s in range(1, CONV_K):
        acc = acc + _shift_down(cur, before, s) * w[CONV_K - 1 - s:CONV_K - s, :]
    return acc


def _matmul_nt(a, bt, *, tm, name, carry=None):
    m, k = a.shape
    n = bt.shape[0]
    c_ins, c_in_specs, c_out_specs, c_outs, c_scratch = _carry_specs(carry)

    def body(*refs):
        a_ref, b_ref, o_ref = _carried(carry, refs, 2, 1, m // tm)
        o_ref[...] = _dot_nt(a_ref[...], b_ref[...]).astype(o_ref.dtype)

    outs = pl.pallas_call(
        body, name=name, grid=(m // tm,),
        in_specs=[pl.BlockSpec((tm, k), lambda i: (i, 0)), pl.BlockSpec((n, k), lambda i: (0, 0))] + c_in_specs,
        out_specs=[pl.BlockSpec((tm, n), lambda i: (i, 0))] + c_out_specs,
        out_shape=[jax.ShapeDtypeStruct((m, n), BF16)] + c_outs,
        scratch_shapes=c_scratch,
        compiler_params=_cp("arbitrary"))(a, bt, *c_ins)
    return outs[0], outs[1:]


def _dn_pre(h, conv_w, par, *, tt, name, carry=None):
    t = h.shape[0]
    cw = 3 * A_WIDTH
    hb = tt // HALO

    c_ins, c_in_specs, c_out_specs, c_outs, c_scratch = _carry_specs(carry)

    def body(*refs):
        (pre_ref, halo_ref, bgi_ref, cw_ref, par_ref,
         q_ref, k_ref, v_ref, bg_ref, bgt_ref) = _carried(carry, refs, 5, 5, t // tt)
        i = pl.program_id(0)
        cur = pre_ref[...].astype(F32)
        before = jnp.where(i > 0, halo_ref[...].astype(F32)[HALO - SUBLANE:], 0.0)
        s = _silu(_conv_fwd(cur, before, cw_ref[...]))
        for hd in range(A_HEADS):
            sl = slice(hd * LANE, (hd + 1) * LANE)
            tq = s[:, hd * LANE:(hd + 1) * LANE]
            q_ref[:, sl] = tq * (lax.rsqrt(jnp.sum(tq * tq, -1, keepdims=True) + L2_EPS) * (A_HEAD_DIM ** -0.5))
            tk = s[:, A_WIDTH + hd * LANE:A_WIDTH + (hd + 1) * LANE]
            k_ref[:, sl] = tk * lax.rsqrt(jnp.sum(tk * tk, -1, keepdims=True) + L2_EPS)
        v_ref[...] = s[:, 2 * A_WIDTH:]
        raw = bgi_ref[...].astype(F32)
        lane = lax.broadcasted_iota(jnp.int32, raw.shape, 1)
        is_a = (lane >= A_HEADS) & (lane < 2 * A_HEADS)
        g = jnp.where(is_a, -jnp.exp(par_ref[0:1, :]) * _softplus(raw + par_ref[1:2, :]), 0.0)
        gc = _dot_hi(_chunk_tri(tt, lower=True), g)
        bg = jnp.where(lane < A_HEADS, _sigmoid(raw), gc)
        bg_ref[...] = bg
        bgt_ref[...] = jnp.transpose(bg)[0:SUBLANE, :]

    wide = jax.ShapeDtypeStruct((t, A_WIDTH), F32)
    outs = pl.pallas_call(
        body, name=name, grid=(t // tt,),
        in_specs=[pl.BlockSpec((tt, cw), lambda i: (i, 0)),
                  pl.BlockSpec((HALO, cw), lambda i: (jnp.maximum(i * hb - 1, 0), 0)),
                  pl.BlockSpec((tt, LANE), lambda i: (i, C_BG // LANE)),
                  pl.BlockSpec((CONV_K, cw), lambda i: (0, 0)),
                  pl.BlockSpec((SUBLANE, LANE), lambda i: (0, 0))] + c_in_specs,
        out_specs=[pl.BlockSpec((tt, A_WIDTH), lambda i: (i, 0))] * 3
        + [pl.BlockSpec((tt, LANE), lambda i: (i, 0)), pl.BlockSpec((SUBLANE, tt), lambda i: (0, i))] + c_out_specs,
        out_shape=[wide, wide, wide, jax.ShapeDtypeStruct((t, LANE), F32),
                   jax.ShapeDtypeStruct((SUBLANE, t), F32)] + c_outs,
        scratch_shapes=c_scratch,
        compiler_params=_cp("arbitrary"))(h, h, h, conv_w, par, *c_ins)
    return outs[:5], outs[5:]


def _chunk_tri(n, lower):
    r = lax.broadcasted_iota(jnp.int32, (n, n), 0)
    c = lax.broadcasted_iota(jnp.int32, (n, n), 1)
    shift = CHUNK.bit_length() - 1
    same = jnp.right_shift(r, shift) == jnp.right_shift(c, shift)
    return (same & ((c <= r) if lower else (c >= r))).astype(F32)


def _chunk_masks():
    r = lax.broadcasted_iota(jnp.int32, (CHUNK, CHUNK), 0)
    c = lax.broadcasted_iota(jnp.int32, (CHUNK, CHUNK), 1)
    return r >= c, r > c, r == c


def _split(a):
    hi = a.astype(BF16)
    return hi, (a - hi.astype(F32)).astype(BF16)


def _dot3(a, b):
    (ah, al), (bh, bl) = a, b
    d = lambda p, q: jnp.dot(p, q, preferred_element_type=F32)
    return d(ah, bh) + (d(ah, bl) + d(al, bh))


def _tri_inv_many(a_list, eye):
    d = lambda p, q: jnp.dot(p, q, preferred_element_type=F32)
    p = [(-a).astype(BF16) for a in a_list]
    tm = [eye - a for a in a_list]
    for _ in range(5):
        pf = [d(pi, pi) for pi in p]
        p = [x.astype(BF16) for x in pf]
        tm = [t + d(t.astype(BF16), pi) for t, pi in zip(tm, p)]
    ms = [_split(eye + a) for a in a_list]
    res = [eye - _dot3(m, _split(t)) for m, t in zip(ms, tm)]
    return [t + d(t.astype(BF16), r.astype(BF16)) for t, r in zip(tm, res)]


def _chunk_gates(bg_v, bgt_v, hd):
    return (bg_v[:, hd:hd + 1], bg_v[:, A_HEADS + hd:A_HEADS + hd + 1],
            None if bgt_v is None else bgt_v[A_HEADS + hd:A_HEADS + hd + 1, :])


WY_ROWS = 512
SCAN_ROWS = 512
WY_GROUP = 8


def _dn_wy(q, k, v, bg, bgt, *, name, carry=None):
    t = q.shape[0]
    rows = WY_ROWS

    c_ins, c_in_specs, c_out_specs, c_outs, c_scratch = _carry_specs(carry)

    def body(*refs):
        q_ref, k_ref, v_ref, bg_ref, bgt_ref, u_ref, w_ref, tm_ref, qk_ref = _carried(carry, refs, 5, 4, t // rows)
        causal, strict, diag = _chunk_masks()
        eye = diag.astype(F32)
        for c0 in range(0, rows // CHUNK, WY_GROUP):
            items = [(c, hd) for c in range(c0, c0 + WY_GROUP) for hd in range(A_HEADS)]
            rs = lambda c: slice(c * CHUNK, (c + 1) * CHUNK)
            sl = lambda hd: slice(hd * LANE, (hd + 1) * LANE)
            hs = lambda hd: slice(hd * CHUNK, (hd + 1) * CHUNK)
            gates = [_chunk_gates(bg_ref[rs(c), :], bgt_ref[:, rs(c)], hd) for c, hd in items]
            dms = [jnp.exp(jnp.where(causal, gcol - grow, NEG)) for _, gcol, grow in gates]
            kbs = [k_ref[rs(c), sl(hd)] * g[0] for (c, hd), g in zip(items, gates)]
            a_list = [jnp.where(strict, _dot_nt(kb, k_ref[rs(c), sl(hd)]) * dm, 0.0)
                      for (c, hd), kb, dm in zip(items, kbs, dms)]
            for (c, hd), dm in zip(items, dms):
                qk_ref[rs(c), hs(hd)] = jnp.where(
                    causal, _dot_nt(q_ref[rs(c), sl(hd)], k_ref[rs(c), sl(hd)]) * dm, 0.0)
            tms = _tri_inv_many(a_list, eye)
            for (c, hd), g, kb, tmat in zip(items, gates, kbs, tms):
                tm_ref[rs(c), hs(hd)] = tmat
                u_ref[rs(c), sl(hd)] = _dot(tmat, v_ref[rs(c), sl(hd)] * g[0])
                w_ref[rs(c), sl(hd)] = _dot(tmat, kb * jnp.exp(g[1])).astype(BF16)

    blk = pl.BlockSpec((rows, A_WIDTH), lambda i: (i, 0))
    half = pl.BlockSpec((rows, A_HEADS * CHUNK), lambda i: (i, 0))
    outs = pl.pallas_call(
        body, name=name, grid=(t // rows,),
        in_specs=[blk, blk, blk, pl.BlockSpec((rows, LANE), lambda i: (i, 0)),
                  pl.BlockSpec((SUBLANE, rows), lambda i: (0, i))] + c_in_specs,
        out_specs=[blk, blk, half, half] + c_out_specs,
        out_shape=[jax.ShapeDtypeStruct((t, A_WIDTH), F32), jax.ShapeDtypeStruct((t, A_WIDTH), BF16),
                   jax.ShapeDtypeStruct((t, A_HEADS * CHUNK), F32),
                   jax.ShapeDtypeStruct((t, A_HEADS * CHUNK), F32)] + c_outs,
        scratch_shapes=c_scratch,
        compiler_params=_cp("arbitrary"))(q, k, v, bg, bgt, *c_ins)
    return outs[:4], outs[4:]


def _dn_scan_fwd(q, k, u, w, qk, bg, *, name, carry=None):
    t = q.shape[0]
    rows = SCAN_ROWS
    per = rows // CHUNK
    c_ins, c_in_specs, c_out_specs, c_outs, c_scratch = _carry_specs(carry)

    def body(*refs):
        q_ref, k_ref, u_ref, w_ref, qk_ref, bg_ref, o_ref, vn_ref, s_ref, state = _carried(carry, refs, 6, 3, t // rows)

        @pl.when(pl.program_id(0) == 0)
        def _():
            state[...] = jnp.zeros_like(state)

        heads = range(A_HEADS)
        sl = lambda hd: slice(hd * LANE, (hd + 1) * LANE)
        s_cur = [state[hd] for hd in heads]
        for c in range(per):
            rs = slice(c * CHUNK, (c + 1) * CHUNK)
            bg_v = bg_ref[rs, :]
            gcols = [_chunk_gates(bg_v, None, hd)[1] for hd in heads]
            glasts = [gc[CHUNK - 1:CHUNK, :] for gc in gcols]
            for hd in heads:
                s_ref[c, hd] = s_cur[hd].astype(BF16)
            vns = [u_ref[rs, sl(hd)] - _dot(w_ref[rs, sl(hd)], s_cur[hd]) for hd in heads]
            qss = [_dot(q_ref[rs, sl(hd)] * jnp.exp(gcols[hd]), s_cur[hd]) for hd in heads]
            s_cur = [s_cur[hd] * jnp.exp(glasts[hd])
                     + _dot_tn(k_ref[rs, sl(hd)] * jnp.exp(glasts[hd] - gcols[hd]), vns[hd]) for hd in heads]
            for hd in heads:
                vn_ref[rs, sl(hd)] = vns[hd]
                o_ref[rs, sl(hd)] = qss[hd] + _dot(qk_ref[rs, hd * CHUNK:(hd + 1) * CHUNK], vns[hd])
        for hd in heads:
            state[hd] = s_cur[hd]

    blk = pl.BlockSpec((rows, A_WIDTH), lambda i: (i, 0))
    half = pl.BlockSpec((rows, A_HEADS * CHUNK), lambda i: (i, 0))
    wide = jax.ShapeDtypeStruct((t, A_WIDTH), F32)
    outs = pl.pallas_call(
        body, name=name, grid=(t // rows,),
        in_specs=[blk, blk, blk, blk, half, pl.BlockSpec((rows, LANE), lambda i: (i, 0))] + c_in_specs,
        out_specs=[blk, blk, pl.BlockSpec((per, A_HEADS, LANE, LANE), lambda i: (i, 0, 0, 0))] + c_out_specs,
        out_shape=[wide, wide, jax.ShapeDtypeStruct((t // CHUNK, A_HEADS, LANE, LANE), BF16)] + c_outs,
        scratch_shapes=[pltpu.VMEM((A_HEADS, LANE, LANE), F32)] + c_scratch,
        compiler_params=_cp("arbitrary"))(q, k, u, w, qk, bg, *c_ins)
    return outs[:3], outs[3:]


def _swa_neg_dist(n_blk):
    qi = lax.broadcasted_iota(jnp.int32, (BLOCK, 2 * BLOCK), 0)
    si = lax.broadcasted_iota(jnp.int32, (BLOCK, 2 * BLOCK), 1)
    dist = qi + BLOCK - si
    mask = (dist >= 0) & (dist < BLOCK) & ((si >= BLOCK) | (n_blk > 0))
    return jnp.where(mask, -dist.astype(F32), NEG)


def _stack_heads(ref, hk):
    return jnp.concatenate([ref[:, h * B_HEAD_DIM:(h + 1) * B_HEAD_DIM].astype(F32)
                            for h in range(hk * B_GROUP, (hk + 1) * B_GROUP)], axis=0)


def _swa_group_probs(q_ref, sk_ref, kband, vband, neg_dist):
    hks = range(B_KV_HEADS)
    heads = lambda hk: range(hk * B_GROUP, (hk + 1) * B_GROUP)
    ksl = lambda hk: slice(hk * B_HEAD_DIM, (hk + 1) * B_HEAD_DIM)
    ones = jnp.ones((2 * BLOCK, B_HEAD_DIM), BF16)
    qs = [_stack_heads(q_ref, hk) * (B_HEAD_DIM ** -0.5) for hk in hks]
    sink = [jnp.concatenate([jnp.broadcast_to(sk_ref[h:h + 1, 0:1], (BLOCK, 1)) for h in heads(hk)], axis=0)
            for hk in hks]
    s = [_dot_nt(qs[hk], kband[:, ksl(hk)]) + jnp.concatenate([ALIBI[h] * neg_dist for h in heads(hk)], axis=0)
         for hk in hks]
    m = [jnp.maximum(jnp.max(s[hk], axis=-1, keepdims=True), sink[hk]) for hk in hks]
    p = [jnp.exp(s[hk] - m[hk]) for hk in hks]
    oe = [jnp.dot(p[hk].astype(BF16), jnp.concatenate([vband[:, ksl(hk)].astype(BF16), ones], axis=1),
                  preferred_element_type=F32) for hk in hks]
    ps = [jnp.exp(sink[hk] - m[hk]) for hk in hks]
    inv = [1.0 / (oe[hk][:, B_HEAD_DIM:B_HEAD_DIM + 1] + ps[hk]) for hk in hks]
    return [(qs[hk], p[hk] * inv[hk], ps[hk] * inv[hk], oe[hk][:, :B_HEAD_DIM] * inv[hk]) for hk in hks]


def _swa_specs():
    qspec = lambda c0: pl.BlockSpec((BLOCK, B_WIDTH), lambda i: (i, c0 // B_WIDTH))
    cur = lambda c0: pl.BlockSpec((BLOCK, LANE), lambda i: (i, c0 // LANE))
    prev = lambda c0: pl.BlockSpec((BLOCK, LANE), lambda i: (jnp.maximum(i - 1, 0), c0 // LANE))
    return qspec, cur, prev


def _carried(carry, refs, n_in, n_out, steps):
    if carry is None:
        return refs
    ci, co = len(carry.ins), len(carry.outs)
    own = refs[:n_in] + refs[n_in + ci:n_in + ci + n_out] + refs[n_in + ci + n_out + co:len(refs) - 3]
    parts = refs[n_in:n_in + ci], refs[n_in + ci + n_out:n_in + ci + n_out + co], refs[len(refs) - 3:]

    @pl.when(pl.program_id(0) == 0)
    def _():
        carry.start(*parts)

    @pl.when(pl.program_id(0) == steps - 1)
    def _():
        carry.finish(*parts)

    return own


def _carry_specs(carry):
    if carry is None:
        return [], [], [], [], []
    return (list(carry.ins), [_ANY] * len(carry.ins), [_ANY] * len(carry.outs), list(carry.outs), carry.scratch())


def _swa_fwd(h, sinks_b, *, name, carry=None):
    t = h.shape[0]
    qspec, cur, prev = _swa_specs()
    c_ins, c_in_specs, c_out_specs, c_outs, c_scratch = _carry_specs(carry)

    def body(*refs):
        q_ref, kc_ref, kp_ref, vc_ref, vp_ref, sk_ref, o_ref = _carried(carry, refs, 6, 1, t // BLOCK)
        n_blk = pl.program_id(0)
        kband = jnp.concatenate([kp_ref[...], kc_ref[...]], axis=0)
        vband = jnp.concatenate([vp_ref[...], vc_ref[...]], axis=0)
        groups = _swa_group_probs(q_ref, sk_ref, kband, vband, _swa_neg_dist(n_blk))
        for hk, (_, _, _, o) in enumerate(groups):
            for g in range(B_GROUP):
                hq = hk * B_GROUP + g
                o_ref[:, hq * B_HEAD_DIM:(hq + 1) * B_HEAD_DIM] = o[g * BLOCK:(g + 1) * BLOCK]

    outs = pl.pallas_call(
        body, name=name, grid=(t // BLOCK,),
        in_specs=[qspec(C_QB), cur(C_KB), prev(C_KB), cur(C_VB), prev(C_VB),
                  pl.BlockSpec((B_Q_HEADS, LANE), lambda i: (0, 0))] + c_in_specs,
        out_specs=[pl.BlockSpec((BLOCK, B_WIDTH), lambda i: (i, 0))] + c_out_specs,
        out_shape=[jax.ShapeDtypeStruct((t, B_WIDTH), F32)] + c_outs,
        scratch_shapes=c_scratch,
        compiler_params=_cp("arbitrary"))(h, h, h, h, h, sinks_b, *c_ins)
    return outs[0], outs[1:]


def _rms_gate(o, za, nw):
    outs = []
    for hd in range(A_HEADS):
        oh = o[:, hd * LANE:(hd + 1) * LANE]
        r = lax.rsqrt(jnp.mean(oh * oh, -1, keepdims=True) + RMS_EPS)
        outs.append(oh * r * nw)
    return jnp.concatenate(outs, axis=1) * _silu(za)


def _out_ln(x, oa, ob, h, norm_w, w_out, ln_g, ln_b, *, tm, name, target=None):
    t = x.shape[0]
    last = target is not None

    def body(*refs):
        x_ref, oa_ref, ob_ref, za_ref, zb_ref, nw_ref, w_ref, g_ref, b_ref = refs[:9]
        xn_ref, mx_ref, r_ref = refs[9 + last:12 + last]
        ya = _rms_gate(oa_ref[...], za_ref[...].astype(F32), nw_ref[...])
        yb = ob_ref[...] * _silu(zb_ref[...].astype(F32))
        mixed = jnp.concatenate([ya, yb], axis=1).astype(BF16)
        mx_ref[...] = mixed
        r = DEEPNORM_ALPHA * x_ref[...] + jnp.dot(mixed, w_ref[...], preferred_element_type=F32)
        r_ref[...] = r
        mu = jnp.mean(r, -1, keepdims=True)
        xc = r - mu
        var = jnp.mean(xc * xc, -1, keepdims=True)
        xn = xc * lax.rsqrt(var + LN_EPS) * g_ref[...] + b_ref[...]
        if not last:
            xn_ref[...] = xn
            return
        loss_ref = refs[13]

        @pl.when(pl.program_id(0) == 0)
        def _():
            loss_ref[...] = jnp.zeros_like(loss_ref)

        err = xn - refs[9][...]
        xn_ref[...] = err * (1.0 / D_MODEL)
        loss_ref[...] += 0.5 / D_MODEL * jnp.sum(err * err)

    row = lambda w, c: pl.BlockSpec((tm, w), lambda i: (i, c))
    full = lambda a, b: pl.BlockSpec((a, b), lambda i: (0, 0))
    wide = jax.ShapeDtypeStruct((t, D_MODEL), F32)
    return pl.pallas_call(
        body, name=name, grid=(t // tm,),
        in_specs=[row(D_MODEL, 0), row(A_WIDTH, 0), row(B_WIDTH, 0), row(A_WIDTH, C_ZA // A_WIDTH),
                  row(B_WIDTH, C_ZB // B_WIDTH), full(1, LANE), full(D_MODEL, D_MODEL), full(1, D_MODEL),
                  full(1, D_MODEL)] + [row(D_MODEL, 0)] * last,
        out_specs=[row(D_MODEL, 0), row(D_MODEL, 0), row(D_MODEL, 0)] + [full(SUBLANE, LANE)] * last,
        out_shape=[wide, jax.ShapeDtypeStruct((t, D_MODEL), BF16), wide]
        + [jax.ShapeDtypeStruct((SUBLANE, LANE), F32)] * last,
        compiler_params=_cp("arbitrary" if last else "parallel"))(
        x, oa, ob, h, h, norm_w, w_out, ln_g, ln_b, *([target] if last else []))


def _layer_fwd(x, wt, conv_w, par, sinks_b, norm_w, w_out_bf, ln_g, ln_b, l, carries=None, target=None):
    carries = carries or {}
    h, got_in = _matmul_nt(x, wt, tm=512, name=f"in_proj_{l}", carry=carries.get("in_proj"))
    if callable(w_out_bf):
        w_out_bf = w_out_bf(got_in)
    (q, k, v, bg, bgt), got_pre = _dn_pre(h, conv_w, par, tt=512, name=f"dn_pre_{l}", carry=carries.get("dn_pre"))
    (u, w, tmat, qk), got_wy = _dn_wy(q, k, v, bg, bgt, name=f"dn_wy_{l}", carry=carries.get("dn_wy"))
    (oa, vn, s_all), got_scan = _dn_scan_fwd(q, k, u, w, qk, bg, name=f"dn_scan_{l}", carry=carries.get("dn_scan"))
    ob, got_swa = _swa_fwd(h, sinks_b, name=f"swa_fwd_{l}", carry=carries.get("swa"))
    xn, mixed, r, *loss = _out_ln(x, oa, ob, h, norm_w, w_out_bf, ln_g, ln_b, tm=512, name=f"out_ln_{l}", target=target)
    if loss:
        xn = (xn, loss[0])
    res = dict(x=x, h=h, q=q, k=k, v=v, bg=bg, bgt=bgt, w=w, tmat=tmat, qk=qk, vn=vn, oa=oa, s_all=s_all,
               mixed=mixed, r=r, w_out=w_out_bf)
    return xn, res, dict(in_proj=got_in, dn_pre=got_pre, dn_wy=got_wy, dn_scan=got_scan, swa=got_swa)


def _ln_out_bwd(dxn, r, mixed, ln_g, w_out, *, tm, name):
    t = dxn.shape[0]

    def body(dxn_ref, r_ref, mx_ref, g_ref, w_ref, dr_ref, dm_ref, dw_ref, dg_ref, db_ref):
        @pl.when(pl.program_id(0) == 0)
        def _():
            dw_ref[...] = jnp.zeros_like(dw_ref)
            dg_ref[...] = jnp.zeros_like(dg_ref)
            db_ref[...] = jnp.zeros_like(db_ref)

        rr = r_ref[...]
        xc = rr - jnp.mean(rr, -1, keepdims=True)
        rstd = lax.rsqrt(jnp.mean(xc * xc, -1, keepdims=True) + LN_EPS)
        xhat = xc * rstd
        dxn_v = dxn_ref[...]
        dxh = dxn_v * g_ref[...]
        dr = rstd * (dxh - jnp.mean(dxh, -1, keepdims=True) - xhat * jnp.mean(dxh * xhat, -1, keepdims=True))
        dr_ref[...] = dr
        dg_ref[...] += jnp.sum(dxn_v * xhat, axis=0, keepdims=True)
        db_ref[...] += jnp.sum(dxn_v, axis=0, keepdims=True)
        drb = dr.astype(BF16)
        dm_ref[...] = _dot_nt(drb, w_ref[...])
        dw_ref[...] += _dot_tn(mx_ref[...], drb)

    row = pl.BlockSpec((tm, D_MODEL), lambda i: (i, 0))
    full = lambda a, b: pl.BlockSpec((a, b), lambda i: (0, 0))
    big = jax.ShapeDtypeStruct((t, D_MODEL), F32)
    vec = jax.ShapeDtypeStruct((1, D_MODEL), F32)
    return pl.pallas_call(
        body, name=name, grid=(t // tm,),
        in_specs=[row, row, row, full(1, D_MODEL), full(D_MODEL, D_MODEL)],
        out_specs=[row, row, full(D_MODEL, D_MODEL), full(1, D_MODEL), full(1, D_MODEL)],
        out_shape=[big, big, jax.ShapeDtypeStruct((D_MODEL, D_MODEL), F32), vec, vec],
        compiler_params=_cp("arbitrary"))(dxn, r, mixed, ln_g, w_out)


def _dn_post_bwd(dm, oa, h, norm_w, *, tm, name):
    t = oa.shape[0]

    def body(dy_ref, o_ref, za_ref, nw_ref, do_ref, dza_ref, dnw_ref):
        @pl.when(pl.program_id(0) == 0)
        def _():
            dnw_ref[...] = jnp.zeros_like(dnw_ref)

        nw = nw_ref[...]
        dnw = jnp.zeros_like(nw)
        for hd in range(A_HEADS):
            sl = slice(hd * LANE, (hd + 1) * LANE)
            oh, za, dy = o_ref[:, sl], za_ref[:, sl].astype(F32), dy_ref[:, sl]
            rs = lax.rsqrt(jnp.mean(oh * oh, -1, keepdims=True) + RMS_EPS)
            nrm = oh * rs
            gate, dgate = _silu_and_grad(za)
            dza_ref[:, sl] = dy * nrm * nw * dgate
            dn = dy * gate
            dnw = dnw + jnp.sum(dn * nrm, axis=0, keepdims=True)
            dnn = dn * nw
            do_ref[:, sl] = rs * dnn - oh * (rs * rs * rs) * jnp.mean(dnn * oh, -1, keepdims=True)
        dnw_ref[...] += dnw

    row = lambda c: pl.BlockSpec((tm, A_WIDTH), lambda i: (i, c))
    wide = jax.ShapeDtypeStruct((t, A_WIDTH), F32)
    return pl.pallas_call(
        body, name=name, grid=(t // tm,),
        in_specs=[row(0), row(0), row(C_ZA // A_WIDTH), pl.BlockSpec((1, LANE), lambda i: (0, 0))],
        out_specs=[row(0), row(C_ZA // A_WIDTH), pl.BlockSpec((1, LANE), lambda i: (0, 0))],
        out_shape=[wide, jax.ShapeDtypeStruct((t, DH_MAIN), F32), jax.ShapeDtypeStruct((1, LANE), F32)],
        compiler_params=_cp("arbitrary"))(dm, oa, h, norm_w)


def _dn_scan_bwd(q, k, w, qk, bg, do, *, name):
    t = q.shape[0]
    rows = SCAN_ROWS
    per = rows // CHUNK
    n = t // rows

    def body(q_ref, k_ref, w_ref, qk_ref, bg_ref, do_ref, dvn_ref, ds_ref, dstate):
        @pl.when(pl.program_id(0) == 0)
        def _():
            dstate[...] = jnp.zeros_like(dstate)

        heads = range(A_HEADS)
        sl = lambda hd: slice(hd * LANE, (hd + 1) * LANE)
        ds_cur = [dstate[hd] for hd in heads]
        for c in reversed(range(per)):
            rs = slice(c * CHUNK, (c + 1) * CHUNK)
            bg_v = bg_ref[rs, :]
            gcols = [_chunk_gates(bg_v, None, hd)[1] for hd in heads]
            glasts = [gc[CHUNK - 1:CHUNK, :] for gc in gcols]
            for hd in heads:
                ds_ref[c, hd] = ds_cur[hd].astype(BF16)
            pdo = [_dot_tn(qk_ref[rs, hd * CHUNK:(hd + 1) * CHUNK], do_ref[rs, sl(hd)]) for hd in heads]
            qdo = [_dot_tn(q_ref[rs, sl(hd)] * jnp.exp(gcols[hd]), do_ref[rs, sl(hd)]) for hd in heads]
            dvns = [pdo[hd] + _dot(k_ref[rs, sl(hd)] * jnp.exp(glasts[hd] - gcols[hd]), ds_cur[hd]) for hd in heads]
            ds_cur = [qdo[hd] + jnp.exp(glasts[hd]) * ds_cur[hd] - _dot_tn(w_ref[rs, sl(hd)], dvns[hd])
                      for hd in heads]
            for hd in heads:
                dvn_ref[rs, sl(hd)] = dvns[hd]
        for hd in heads:
            dstate[hd] = ds_cur[hd]

    blk = pl.BlockSpec((rows, A_WIDTH), lambda i: (n - 1 - i, 0))
    return pl.pallas_call(
        body, name=name, grid=(n,),
        in_specs=[blk, blk, blk, pl.BlockSpec((rows, A_HEADS * CHUNK), lambda i: (n - 1 - i, 0)),
                  pl.BlockSpec((rows, LANE), lambda i: (n - 1 - i, 0)), blk],
        out_specs=[blk, pl.BlockSpec((per, A_HEADS, LANE, LANE), lambda i: (n - 1 - i, 0, 0, 0))],
        out_shape=[jax.ShapeDtypeStruct((t, A_WIDTH), F32),
                   jax.ShapeDtypeStruct((t // CHUNK, A_HEADS, LANE, LANE), BF16)],
        scratch_shapes=[pltpu.VMEM((A_HEADS, LANE, LANE), F32)],
        compiler_params=_cp("arbitrary"))(q, k, w, qk, bg, do)


def _dn_chunk_bwd(q, k, v, vn, tmat, qk, bg, bgt, s_all, ds_all, dvn, do, *, name):
    t = q.shape[0]
    rows = WY_ROWS
    per = rows // CHUNK

    def body(q_ref, k_ref, v_ref, vn_ref, tm_ref, qk_ref, bg_ref, bgt_ref, s_ref, ds_ref, dvn_ref, do_ref,
             dq_ref, dk_ref, dv_ref, dbg_ref, dbgt_ref):
        causal, strict, _ = _chunk_masks()
        lane = lax.broadcasted_iota(jnp.int32, (CHUNK, LANE), 1)
        rowi = lax.broadcasted_iota(jnp.int32, (CHUNK, 1), 0)
        sub = lax.broadcasted_iota(jnp.int32, (SUBLANE, CHUNK), 0)
        rs = lambda c: slice(c * CHUNK, (c + 1) * CHUNK)
        sl = lambda hd: slice(hd * LANE, (hd + 1) * LANE)
        hs = lambda hd: slice(hd * CHUNK, (hd + 1) * CHUNK)
        for c0 in range(0, per, WY_GROUP):
            items = [(c, hd) for c in range(c0, c0 + WY_GROUP) for hd in range(A_HEADS)]
            at = lambda ref: [ref[rs(c), sl(hd)] for c, hd in items]
            qs, ks, vs, dos, vns, dvns = at(q_ref), at(k_ref), at(v_ref), at(do_ref), at(vn_ref), at(dvn_ref)
            tmhs = [tm_ref[rs(c), hs(hd)] for c, hd in items]
            ps = [qk_ref[rs(c), hs(hd)] for c, hd in items]
            gates = [_chunk_gates(bg_ref[rs(c), :], bgt_ref[:, rs(c)], hd) for c, hd in items]
            betas = [g[0] for g in gates]
            gcols = [g[1] for g in gates]
            dmats = [jnp.exp(jnp.where(causal, g[1] - g[2], NEG)) for g in gates]
            es = [jnp.exp(gc) for gc in gcols]
            glasts = [gc[CHUNK - 1:CHUNK, :] for gc in gcols]
            eks = [jnp.exp(gl - gc) for gl, gc in zip(glasts, gcols)]
            kbs = [kh * b for kh, b in zip(ks, betas)]
            vbs = [vh * b for vh, b in zip(vs, betas)]
            kbes = [kb * e for kb, e in zip(kbs, es)]

            a_s = [jnp.where(strict, _dot_nt(kb, kh) * dm, 0.0) for kb, kh, dm in zip(kbs, ks, dmats)]
            dps = [jnp.where(causal, _dot_nt(doh, vnh), 0.0) for doh, vnh in zip(dos, vns)]
            dqds = [_dot_nt(doh, s_ref[c, hd]) for doh, (c, hd) in zip(dos, items)]
            dkds = [_dot_nt(vnh, ds_ref[c, hd]) for vnh, (c, hd) in zip(vns, items)]
            dws = [-_dot_nt(dvnh, s_ref[c, hd]) for dvnh, (c, hd) in zip(dvns, items)]
            dvbs = [_dot_tn(tmh, dvnh) for tmh, dvnh in zip(tmhs, dvns)]
            dgts = [jnp.sum(s_ref[c, hd].astype(F32) * ds_ref[c, hd].astype(F32), keepdims=True) for c, hd in items]
            dts = [_dot_nt(dvnh, vb) + _dot_nt(dw, kbe) for dvnh, vb, dw, kbe in zip(dvns, vbs, dws, kbes)]
            dkbes = [_dot_tn(tmh, dw) for tmh, dw in zip(tmhs, dws)]
            xs = [_dot_nt(dt, tmh) for dt, tmh in zip(dts, tmhs)]
            das = [jnp.where(strict, -_dot_tn(tmh, x), 0.0) for tmh, x in zip(tmhs, xs)]
            dmas = [da * dm for da, dm in zip(das, dmats)]
            dmps = [dp * dm for dp, dm in zip(dps, dmats)]
            dkbs = [_dot(dma, kh) + dkbe * e for dma, kh, dkbe, e in zip(dmas, ks, dkbes, es)]
            for i, (c, hd) in enumerate(items):
                dq_ref[rs(c), sl(hd)] = _dot(dmps[i], ks[i]) + dqds[i] * es[i]
                dk_ref[rs(c), sl(hd)] = (_dot_tn(dmas[i], kbs[i]) + _dot_tn(dmps[i], qs[i]) + dkds[i] * eks[i]
                                         + dkbs[i] * betas[i])
                dv_ref[rs(c), sl(hd)] = dvbs[i] * betas[i]
            for c in range(c0, c0 + WY_GROUP):
                acc = jnp.zeros((CHUNK, LANE), F32)
                acc_t = jnp.zeros((SUBLANE, CHUNK), F32)
                for i, (ci, hd) in enumerate(items):
                    if ci != c:
                        continue
                    gmat = das[i] * a_s[i] + dps[i] * ps[i]
                    rk = jnp.sum(dkds[i] * ks[i], -1, keepdims=True) * eks[i]
                    de = (jnp.sum(dqds[i] * qs[i], -1, keepdims=True)
                          + jnp.sum(dkbes[i] * kbs[i], -1, keepdims=True))
                    dglast = jnp.sum(rk, keepdims=True) + dgts[i] * jnp.exp(glasts[i])
                    dgc = (jnp.sum(gmat, -1, keepdims=True) + de * es[i] - rk
                           + jnp.where(rowi == CHUNK - 1, dglast, 0.0))
                    dbeta = (jnp.sum(dkbs[i] * ks[i], -1, keepdims=True)
                             + jnp.sum(dvbs[i] * vs[i], -1, keepdims=True))
                    acc = acc + jnp.where(lane == hd, dbeta, 0.0) + jnp.where(lane == A_HEADS + hd, dgc, 0.0)
                    acc_t = acc_t + jnp.where(sub == A_HEADS + hd, -jnp.sum(gmat, axis=0, keepdims=True), 0.0)
                dbg_ref[rs(c), :] = acc
                dbgt_ref[:, rs(c)] = acc_t

    blk = pl.BlockSpec((rows, A_WIDTH), lambda i: (i, 0))
    half = pl.BlockSpec((rows, A_HEADS * CHUNK), lambda i: (i, 0))
    col = pl.BlockSpec((rows, LANE), lambda i: (i, 0))
    rowf = pl.BlockSpec((SUBLANE, rows), lambda i: (0, i))
    st = pl.BlockSpec((per, A_HEADS, LANE, LANE), lambda i: (i, 0, 0, 0))
    wide = jax.ShapeDtypeStruct((t, A_WIDTH), F32)
    return pl.pallas_call(
        body, name=name, grid=(t // rows,),
        in_specs=[blk, blk, blk, blk, half, half, col, rowf, st, st, blk, blk],
        out_specs=[blk, blk, blk, col, rowf],
        out_shape=[wide, wide, wide, jax.ShapeDtypeStruct((t, LANE), F32), jax.ShapeDtypeStruct((SUBLANE, t), F32)],
        compiler_params=_cp("parallel"))(q, k, v, vn, tmat, qk, bg, bgt, s_all, ds_all, dvn, do)


def _dn_pre_bwd(h, conv_w, par, dq, dk, dv, dbg, dbgt, *, tt, name):
    t = h.shape[0]
    cw = 3 * A_WIDTH
    hb = tt // HALO

    def body(pre_ref, halo_ref, bgi_ref, cw_ref, par_ref, dq_ref, dk_ref, dv_ref, dbg_ref, dbgt_ref,
             dc_ref, dbgi_ref, dpar_ref):
        i = pl.program_id(0)

        @pl.when(i == 0)
        def _():
            dpar_ref[...] = jnp.zeros_like(dpar_ref)

        cur = pre_ref[...].astype(F32)
        before = jnp.where(i > 0, halo_ref[...].astype(F32)[HALO - SUBLANE:], 0.0)
        c = _conv_fwd(cur, before, cw_ref[...])
        s, ds = _silu_and_grad(c)
        for hd in range(A_HEADS):
            sl = slice(hd * LANE, (hd + 1) * LANE)
            for base, d_ref, scale in ((0, dq_ref, A_HEAD_DIM ** -0.5), (A_WIDTH, dk_ref, 1.0)):
                csl = slice(base + hd * LANE, base + (hd + 1) * LANE)
                tq = s[:, base + hd * LANE:base + (hd + 1) * LANE]
                dy = d_ref[:, sl]
                rq = lax.rsqrt(jnp.sum(tq * tq, -1, keepdims=True) + L2_EPS)
                dtq = scale * (rq * dy - tq * (rq * rq * rq) * jnp.sum(dy * tq, -1, keepdims=True))
                dc_ref[:, csl] = dtq * ds[:, base + hd * LANE:base + (hd + 1) * LANE]
        dc_ref[:, 2 * A_WIDTH:] = dv_ref[...] * ds[:, 2 * A_WIDTH:]
        raw = bgi_ref[...].astype(F32)
        lane = lax.broadcasted_iota(jnp.int32, raw.shape, 1)
        is_b = lane < A_HEADS
        is_a = (lane >= A_HEADS) & (lane < 2 * A_HEADS)
        rows_t = jnp.concatenate([dbgt_ref[...], jnp.zeros((LANE - SUBLANE, tt), F32)], axis=0)
        dbg_v = dbg_ref[...] + jnp.where(is_a, jnp.transpose(rows_t), 0.0)
        dbg_v = jnp.where(is_a, _dot_hi(_chunk_tri(tt, lower=False), jnp.where(is_a, dbg_v, 0.0)), dbg_v)
        beta = _sigmoid(raw)
        z = raw + par_ref[1:2, :]
        neg_ea = -jnp.exp(par_ref[0:1, :])
        g = neg_ea * _softplus(z)
        da = dbg_v * neg_ea * _sigmoid(z)
        dbgi_ref[...] = jnp.where(is_b, dbg_v * beta * (1.0 - beta), jnp.where(is_a, da, 0.0))
        dpar_ref[0:1, :] += jnp.sum(jnp.where(is_a, dbg_v * g, 0.0), axis=0, keepdims=True)
        dpar_ref[1:2, :] += jnp.sum(jnp.where(is_a, da, 0.0), axis=0, keepdims=True)

    wide = pl.BlockSpec((tt, A_WIDTH), lambda i: (i, 0))
    return pl.pallas_call(
        body, name=name, grid=(t // tt,),
        in_specs=[pl.BlockSpec((tt, cw), lambda i: (i, 0)),
                  pl.BlockSpec((HALO, cw), lambda i: (jnp.maximum(i * hb - 1, 0), 0)),
                  pl.BlockSpec((tt, LANE), lambda i: (i, C_BG // LANE)),
                  pl.BlockSpec((CONV_K, cw), lambda i: (0, 0)),
                  pl.BlockSpec((SUBLANE, LANE), lambda i: (0, 0)),
                  wide, wide, wide, pl.BlockSpec((tt, LANE), lambda i: (i, 0)),
                  pl.BlockSpec((SUBLANE, tt), lambda i: (0, i))],
        out_specs=[pl.BlockSpec((tt, cw), lambda i: (i, 0)), pl.BlockSpec((tt, LANE), lambda i: (i, 0)),
                   pl.BlockSpec((SUBLANE, LANE), lambda i: (0, 0))],
        out_shape=[jax.ShapeDtypeStruct((t, cw), F32), jax.ShapeDtypeStruct((t, LANE), F32),
                   jax.ShapeDtypeStruct((SUBLANE, LANE), F32)],
        compiler_params=_cp("arbitrary"))(h, h, h, conv_w, par, dq, dk, dv, dbg, dbgt)


def _conv_bwd(dc, h, conv_w, dh, *, tt, name):
    t = dc.shape[0]
    cw = 3 * A_WIDTH
    hb = tt // HALO
    nb = t // tt

    def body(dc_ref, after_ref, pre_ref, before_ref, cw_ref, dh_in_ref, dpre_ref, dcw_ref):
        i = pl.program_id(0)

        @pl.when(i == 0)
        def _():
            dcw_ref[...] = jnp.zeros_like(dcw_ref)

        dcv = dc_ref[...]
        after = jnp.where(i < nb - 1, after_ref[...], 0.0)
        cur = pre_ref[...].astype(F32)
        before = jnp.where(i > 0, before_ref[...].astype(F32)[HALO - SUBLANE:], 0.0)
        w = cw_ref[...]
        acc = dcv * w[CONV_K - 1:CONV_K, :]
        dcw_ref[CONV_K - 1:CONV_K, :] += jnp.sum(dcv * cur, axis=0, keepdims=True)
        for s in range(1, CONV_K):
            j = CONV_K - 1 - s
            acc = acc + _shift_up(dcv, after, s) * w[j:j + 1, :]
            dcw_ref[j:j + 1, :] += jnp.sum(dcv * _shift_down(cur, before, s), axis=0, keepdims=True)
        dpre_ref[...] = acc

    return pl.pallas_call(
        body, name=name, grid=(nb,),
        in_specs=[pl.BlockSpec((tt, cw), lambda i: (i, 0)),
                  pl.BlockSpec((SUBLANE, cw), lambda i: (jnp.minimum((i + 1) * (tt // SUBLANE), t // SUBLANE - 1), 0)),
                  pl.BlockSpec((tt, cw), lambda i: (i, 0)),
                  pl.BlockSpec((HALO, cw), lambda i: (jnp.maximum(i * hb - 1, 0), 0)),
                  pl.BlockSpec((CONV_K, cw), lambda i: (0, 0)), _ANY],
        out_specs=[pl.BlockSpec((tt, cw), lambda i: (i, 0)), pl.BlockSpec((SUBLANE, cw), lambda i: (0, 0))],
        out_shape=[jax.ShapeDtypeStruct(dh.shape, F32), jax.ShapeDtypeStruct((SUBLANE, cw), F32)],
        input_output_aliases={5: 0},
        compiler_params=_cp("arbitrary"))(dc, dc, h, h, conv_w, dh)


def _swa_bwd(h, dm, sinks_b, dh, *, name, carry=None):
    t = h.shape[0]
    qspec, cur, prev = _swa_specs()
    c_ins, c_in_specs, c_out_specs, c_outs, c_scratch = _carry_specs(carry)

    def body(*refs):
        (q_ref, kc_ref, kp_ref, vc_ref, vp_ref, zb_ref, dy_ref, sk_ref, dh_in_ref,
         dqz_ref, dk_ref, dv_ref, dsk_ref) = _carried(carry, refs, 9, 4, t // BLOCK)
        n_blk = pl.program_id(0)

        @pl.when(n_blk == 0)
        def _():
            dk_ref[...] = jnp.zeros_like(dk_ref)
            dv_ref[...] = jnp.zeros_like(dv_ref)
            dsk_ref[...] = jnp.zeros_like(dsk_ref)

        kband = jnp.concatenate([kp_ref[...], kc_ref[...]], axis=0)
        vband = jnp.concatenate([vp_ref[...], vc_ref[...]], axis=0)
        scale = B_HEAD_DIM ** -0.5
        hks = range(B_KV_HEADS)
        ksl = lambda hk: slice(hk * B_HEAD_DIM, (hk + 1) * B_HEAD_DIM)
        groups = _swa_group_probs(q_ref, sk_ref, kband, vband, _swa_neg_dist(n_blk))
        zbs = [_stack_heads(zb_ref, hk) for hk in hks]
        dys = [_stack_heads(dy_ref, hk) for hk in hks]
        gates = [_silu_and_grad(zbs[hk]) for hk in hks]
        dos = [dys[hk] * gates[hk][0] for hk in hks]
        deltas = [jnp.sum(dos[hk] * groups[hk][3], -1, keepdims=True) for hk in hks]
        dss = [groups[hk][1] * (_dot_nt(dos[hk], vband[:, ksl(hk)]) - deltas[hk]) for hk in hks]
        dqs = [_dot(dss[hk], kband[:, ksl(hk)]) * scale for hk in hks]
        dk_acc = [_dot_tn(dss[hk], groups[hk][0]) for hk in hks]
        dv_acc = [_dot_tn(groups[hk][1], dos[hk]) for hk in hks]
        for hk in hks:
            dzb = dys[hk] * groups[hk][3] * gates[hk][1]
            dsink = groups[hk][2] * deltas[hk]
            for g in range(B_GROUP):
                hq = hk * B_GROUP + g
                rows = slice(g * BLOCK, (g + 1) * BLOCK)
                qsl = slice(hq * B_HEAD_DIM, (hq + 1) * B_HEAD_DIM)
                dqz_ref[:, qsl] = dqs[hk][rows]
                dqz_ref[:, B_WIDTH + hq * B_HEAD_DIM:B_WIDTH + (hq + 1) * B_HEAD_DIM] = dzb[rows]
                dsk_ref[hq:hq + 1, :] += -jnp.sum(dsink[rows], keepdims=True)
        dkb = jnp.concatenate(dk_acc, axis=1)
        dvb = jnp.concatenate(dv_acc, axis=1)
        at_cur = pl.ds(pl.multiple_of(n_blk * BLOCK, BLOCK), BLOCK)
        at_prev = pl.ds(pl.multiple_of(jnp.maximum(n_blk - 1, 0) * BLOCK, BLOCK), BLOCK)
        dk_ref[at_prev, :] += dkb[:BLOCK]
        dv_ref[at_prev, :] += dvb[:BLOCK]
        dk_ref[at_cur, :] += dkb[BLOCK:]
        dv_ref[at_cur, :] += dvb[BLOCK:]

    narrow = jax.ShapeDtypeStruct((t, B_KV_WIDTH), F32)
    res = lambda a, b: pl.BlockSpec((a, b), lambda i: (0, 0))
    outs = pl.pallas_call(
        body, name=name, grid=(t // BLOCK,),
        in_specs=[qspec(C_QB), cur(C_KB), prev(C_KB), cur(C_VB), prev(C_VB), qspec(C_ZB),
                  pl.BlockSpec((BLOCK, B_WIDTH), lambda i: (i, 1)), res(B_Q_HEADS, LANE), _ANY] + c_in_specs,
        out_specs=[pl.BlockSpec((BLOCK, 2 * B_WIDTH), lambda i: (i, C_QB // (2 * B_WIDTH))),
                   res(t, B_KV_WIDTH), res(t, B_KV_WIDTH), res(B_Q_HEADS, LANE)] + c_out_specs,
        out_shape=[jax.ShapeDtypeStruct(dh.shape, F32), narrow, narrow,
                   jax.ShapeDtypeStruct((B_Q_HEADS, LANE), F32)] + c_outs,
        scratch_shapes=c_scratch,
        input_output_aliases={8: 0},
        compiler_params=_cp("arbitrary"))(h, h, h, h, h, h, dm, sinks_b, dh, *c_ins)
    return outs[:4], outs[4:]


def _in_proj_dw(dh_main, dh_tail, x, *, tk, name):
    t, n = x.shape

    def body(a_ref, t_ref, x_ref, o_ref, ot_ref):
        @pl.when(pl.program_id(0) == 0)
        def _():
            o_ref[...] = jnp.zeros_like(o_ref)
            ot_ref[...] = jnp.zeros_like(ot_ref)

        xb = x_ref[...].astype(BF16)
        o_ref[...] += _dot_tn(a_ref[...], xb)
        ot_ref[...] += _dot_tn(t_ref[...], xb)

    row = lambda a: pl.BlockSpec((tk, a.shape[1]), lambda kk: (kk, 0))
    acc = lambda a: pl.BlockSpec((a.shape[1], n), lambda kk: (0, 0))
    return pl.pallas_call(
        body, name=name, grid=(t // tk,), in_specs=[row(dh_main), row(dh_tail), row(x)],
        out_specs=[acc(dh_main), acc(dh_tail)],
        out_shape=[jax.ShapeDtypeStruct((a.shape[1], n), F32) for a in (dh_main, dh_tail)],
        compiler_params=_cp("arbitrary"))(dh_main, dh_tail, x)


def _in_proj_dx(dh_main, dh_tail, wt, dr, *, tm, name, carry=None):
    t, n_main = dh_main.shape
    n_tail = dh_tail.shape[1]
    c_ins, c_in_specs, c_out_specs, c_outs, c_scratch = _carry_specs(carry)

    def body(*refs):
        a_ref, t_ref, wa_ref, wt_ref, r_ref, o_ref = _carried(carry, refs, 5, 1, t // tm)
        o_ref[...] = _dot(a_ref[...], wa_ref[...]) + _dot(t_ref[...], wt_ref[...]) + DEEPNORM_ALPHA * r_ref[...]

    row = lambda w: pl.BlockSpec((tm, w), lambda i: (i, 0))
    outs = pl.pallas_call(
        body, name=name, grid=(t // tm,),
        in_specs=[row(n_main), row(n_tail), pl.BlockSpec((n_main, D_MODEL), lambda i: (0, 0)),
                  pl.BlockSpec((n_tail, D_MODEL), lambda i: (n_main // n_tail, 0)), row(D_MODEL)] + c_in_specs,
        out_specs=[row(D_MODEL)] + c_out_specs,
        out_shape=[jax.ShapeDtypeStruct((t, D_MODEL), F32)] + c_outs,
        scratch_shapes=c_scratch,
        compiler_params=_cp("arbitrary"))(dh_main, dh_tail, wt, wt, dr, *c_ins)
    return outs[0], outs[1:]


def _layer_bwd(dxn, res, wt, conv_w, par, sinks_b, norm_w, w_out_bf, ln_g, l, carry=None, carry_dx=None):
    w_out_bf = res["w_out"]
    dr, dm, dw_out, dln_g, dln_b = _ln_out_bwd(dxn, res["r"], res["mixed"], ln_g, w_out_bf, tm=512, name=f"ln_out_bwd_{l}")
    h = res["h"]
    do, dh, dnw = _dn_post_bwd(dm, res["oa"], h, norm_w, tm=512, name=f"dn_post_bwd_{l}")
    dvn, ds_all = _dn_scan_bwd(res["q"], res["k"], res["w"], res["qk"], res["bg"], do, name=f"dn_scan_bwd_{l}")
    dq, dk, dv, dbg, dbgt = _dn_chunk_bwd(res["q"], res["k"], res["v"], res["vn"], res["tmat"], res["qk"], res["bg"],
                                          res["bgt"], res["s_all"], ds_all, dvn, do, name=f"dn_chunk_bwd_{l}")
    dc, dbgi, dpar = _dn_pre_bwd(h, conv_w, par, dq, dk, dv, dbg, dbgt, tt=512, name=f"dn_pre_bwd_{l}")
    dh, dcw = _conv_bwd(dc, h, conv_w, dh, tt=512, name=f"conv_bwd_{l}")
    (dh, dkb, dvb, dsk), carried = _swa_bwd(h, dm, sinks_b, dh, name=f"swa_bwd_{l}", carry=carry)
    dh_tail = jnp.concatenate([dkb, dvb, dbgi], axis=1)
    dwt_main, dwt_tail = _in_proj_dw(dh, dh_tail, res["x"], tk=512, name=f"in_proj_dw_{l}")
    grads = dict(w_in=(dwt_main, dwt_tail), conv_w=dcw[:CONV_K], a_log=dpar[0, A_HEADS:2 * A_HEADS],
                 dt_bias=dpar[1, A_HEADS:2 * A_HEADS], norm_w=dnw[0], sinks=dsk[:, 0], w_out=dw_out,
                 ln_g=dln_g[0], ln_b=dln_b[0])
    dx, carried_dx = _in_proj_dx(dh, dh_tail, wt, dr, tm=512, name=f"in_proj_dx_{l}",
                                 carry=None if carry_dx is None else carry_dx(grads))
    return dx, grads, carried, carried_dx


def _layer_args(wt, conv_w, a_log, dt_bias, sinks, norm_w, w_out_bf):
    return (wt, conv_w, _gate_params(a_log, dt_bias), jnp.broadcast_to(sinks[:, None], (B_Q_HEADS, LANE)),
            norm_w[None], w_out_bf)


def _local_step(x, target, args0, args1, ln_g, ln_b, gathers=None, reduce1=None, reduce0=None):
    assert DEPTH == 2
    x1, res0, got = _layer_fwd(x, *args0, ln_g[0][None], ln_b[0][None], 0, carries=gathers)
    if gathers is not None:
        args1 = args1(got)
    (dx, loss_tile), res1, _ = _layer_fwd(x1, *args1, ln_g[1][None], ln_b[1][None], 1, target=target)
    dx, grads1, _, _ = _layer_bwd(dx, res1, *args1, ln_g[1][None], 1)
    carry = None if reduce1 is None else reduce1(grads1)
    carry_dx = None if reduce0 is None else (lambda grads0: reduce0(grads0, grads1, loss_tile))
    dx, grads0, landed1, landed0 = _layer_bwd(dx, res0, *args0, ln_g[0][None], 0, carry=carry, carry_dx=carry_dx)
    return loss_tile, dx, [grads0, grads1], landed1, landed0


_ANY = pl.BlockSpec(memory_space=pl.ANY)
_MESH = pl.DeviceIdType.MESH


HALF = D_MODEL // 2


class _Exchange:
    def __init__(self, ins, outs, n_remote, n_local, plan):
        self.ins, self.outs, self.n_remote, self.n_local, self.plan = tuple(ins), tuple(outs), n_remote, n_local, plan

    def scratch(self):
        return [pltpu.SemaphoreType.DMA((self.n_remote,)), pltpu.SemaphoreType.DMA((self.n_remote,)),
                pltpu.SemaphoreType.DMA((max(self.n_local, 1),))]

    def _copies(self, in_refs, out_refs, sems, arriving):
        send_sems, recv_sems, local_sems = sems
        local, sends, recvs = self.plan(in_refs, out_refs)
        loc = [pltpu.make_async_copy(s, d, local_sems.at[i]) for i, (s, d) in enumerate(local)]
        rem = [pltpu.make_async_remote_copy(src_ref=s, dst_ref=recvs[i] if arriving else d, send_sem=send_sems.at[i],
                                            recv_sem=recv_sems.at[i], device_id=peer, device_id_type=_MESH)
               for i, (s, d, peer) in enumerate(sends)]
        return loc, rem

    def start(self, in_refs, out_refs, sems):
        loc, rem = self._copies(in_refs, out_refs, sems, arriving=False)
        for cp in loc + rem:
            cp.start()

    def finish(self, in_refs, out_refs, sems):
        loc, rem = self._copies(in_refs, out_refs, sems, arriving=True)
        for cp in rem:
            cp.wait_recv()
        for cp in rem:
            cp.wait_send()
        for cp in loc:
            cp.wait()


def _run_exchange(ex, *, name):
    n_in, n_out = len(ex.ins), len(ex.outs)

    def body(*refs):
        parts = refs[:n_in], refs[n_in:n_in + n_out], refs[n_in + n_out:]
        ex.start(*parts)
        ex.finish(*parts)

    return pl.pallas_call(body, name=name, in_specs=[_ANY] * n_in, out_specs=[_ANY] * n_out, out_shape=list(ex.outs),
                          scratch_shapes=ex.scratch())(*ex.ins)


def _place():
    x, y, c = lax.axis_index("x"), lax.axis_index("y"), lax.axis_index("c")
    return x, y, c, [(1 - x, y), (x, 1 - y), (1 - x, 1 - y)]


def _gather_exchange(arrays):
    n = len(arrays)

    def plan(src, dst):
        x, y, c, chips = _place()
        me = 2 * x + y
        local = [(src[k], dst[k].at[me]) for k in range(n)]
        sends = [(src[k], dst[k].at[me], (px, py, c)) for k in range(n) for px, py in chips]
        recvs = [dst[k].at[2 * px + py] for k in range(n) for px, py in chips]
        return local, sends, recvs

    return _Exchange(arrays, [jax.ShapeDtypeStruct((N_SHARD,) + a.shape, a.dtype) for a in arrays], 3 * n, n, plan)


def _gather_two_level(pack, conv_w, *, name):
    rows = pack.shape[0]
    part_rows = rows // 2

    def body(pack_ref, conv_ref, land_ref, conv_land_ref, send1, recv1, send2, recv2, csend, crecv, local_sems):
        x, y, c, chips = _place()
        me = 2 * x + y
        sibling = (x, y, 1 - c)
        part = lambda core: pl.ds(pl.multiple_of(core * part_rows, 16), part_rows)
        remote = lambda src, dst, ss, rs, to: pltpu.make_async_remote_copy(
            src_ref=src, dst_ref=dst, send_sem=ss, recv_sem=rs, device_id=to, device_id_type=_MESH)
        local = [pltpu.make_async_copy(pack_ref, land_ref.at[me], local_sems.at[0]),
                 pltpu.make_async_copy(conv_ref, conv_land_ref.at[me], local_sems.at[1])]
        for cp in local:
            cp.start()
        first = [remote(pack_ref.at[part(c)], land_ref.at[me, part(c)], send1.at[j], recv1.at[j], (px, py, c))
                 for j, (px, py) in enumerate(chips)]
        convs = [remote(conv_ref, conv_land_ref.at[me], csend.at[j], crecv.at[j], (px, py, c))
                 for j, (px, py) in enumerate(chips)]
        for cp in first + convs:
            cp.start()
        passed = []
        for j, (px, py) in enumerate(chips):
            slot = 2 * px + py
            remote(pack_ref.at[part(c)], land_ref.at[slot, part(c)], send1.at[j], recv1.at[j], (px, py, c)).wait_recv()
            cp = remote(land_ref.at[slot, part(c)], land_ref.at[slot, part(c)], send2.at[j], recv2.at[j], sibling)
            cp.start()
            passed.append(cp)
        for j, (px, py) in enumerate(chips):
            slot = 2 * px + py
            remote(land_ref.at[slot, part(1 - c)], land_ref.at[slot, part(1 - c)], send2.at[j], recv2.at[j],
                   sibling).wait_recv()
            remote(conv_ref, conv_land_ref.at[slot], csend.at[j], crecv.at[j], (px, py, c)).wait_recv()
        for cp in first + convs + passed:
            cp.wait_send()
        for cp in local:
            cp.wait()

    sems = [pltpu.SemaphoreType.DMA((3,))] * 6 + [pltpu.SemaphoreType.DMA((2,))]
    return pl.pallas_call(
        body, name=name, in_specs=[_ANY, _ANY], out_specs=[_ANY, _ANY],
        out_shape=[jax.ShapeDtypeStruct((N_SHARD,) + pack.shape, pack.dtype),
                   jax.ShapeDtypeStruct((N_SHARD,) + conv_w.shape, conv_w.dtype)],
        scratch_shapes=sems)(pack, conv_w)


def _half(core):
    return pl.ds(pl.multiple_of(core * HALF, HALF), HALF)


def _reduce_scatter_exchange(g):
    def plan(src, dst):
        x, y, c, chips = _place()
        peers = [(px, py, c if t == 0 else 1 - c) for px, py in chips for t in (0, 1)] + [(x, y, 1 - c)]
        sends = [(src[0].at[2 * px + py, :, _half(pc)], dst[0].at[k], (px, py, pc)) for k, (px, py, pc) in enumerate(peers)]
        return [], sends, [dst[0].at[k] for k in range(7)]

    return _Exchange([g], [jax.ShapeDtypeStruct((7,) + g.shape[1:2] + (HALF,), g.dtype)], 7, 0, plan)


def _pair_window_exchange(g):
    def plan(src, dst):
        x, y, c, _ = _place()
        return [], [(src[0].at[:, :, _half(1 - c)], dst[0], (x, y, 1 - c))], [dst[0]]

    return _Exchange([g], [jax.ShapeDtypeStruct(g.shape[:2] + (HALF,), g.dtype)], 1, 0, plan)


def _chip_scatter_exchange(p, small):
    def plan(src, dst):
        x, y, c, chips = _place()
        mine = 4 * x + 2 * y + c
        peers = [(px, py, c if t == 0 else 1 - c) for px, py in chips for t in (0, 1)] + [(x, y, 1 - c)]
        sends = [(src[0].at[2 * px + py], dst[0].at[j], (px, py, c)) for j, (px, py) in enumerate(chips)]
        recvs = [dst[0].at[j] for j in range(3)]
        sends += [(src[1], dst[1].at[mine], peer) for peer in peers]
        recvs += [dst[1].at[4 * px + 2 * py + pc] for px, py, pc in peers]
        return [(src[1], dst[1].at[mine])], sends, recvs

    outs = [jax.ShapeDtypeStruct((3,) + p.shape[1:], p.dtype), jax.ShapeDtypeStruct((8,) + small.shape, small.dtype)]
    return _Exchange([p, small], outs, 10, 1, plan)


def _share_exchange(arrays):
    n = len(arrays)

    def plan(src, dst):
        x, y, c, _ = _place()
        return [], [(src[k], dst[k], (x, y, 1 - c)) for k in range(n)], [dst[k] for k in range(n)]

    return _Exchange(arrays, [jax.ShapeDtypeStruct(a.shape, a.dtype) for a in arrays], n, 0, plan)


def _sum_scatter(g, land, me, core, *, tc, name):
    rows = g.shape[1]
    per = HALF // tc

    def body(where_ref, g_ref, land_ref, o_ref):
        acc = g_ref[...]
        for k in range(7):
            acc = acc + land_ref[k].astype(F32)
        o_ref[...] = acc

    return pl.pallas_call(
        body, name=name, out_shape=jax.ShapeDtypeStruct((rows, HALF), F32), compiler_params=_cp("parallel"),
        grid_spec=pltpu.PrefetchScalarGridSpec(
            num_scalar_prefetch=1, grid=(per,),
            in_specs=[pl.BlockSpec((None, rows, tc), lambda i, w: (w[0], 0, w[1] * per + i)),
                      pl.BlockSpec((7, rows, tc), lambda i, w: (0, 0, i))],
            out_specs=pl.BlockSpec((rows, tc), lambda i, w: (0, i))))(
        jnp.stack([me, core]).astype(jnp.int32), g, land)


def _pair_add(g, land, core, *, name):
    n, rows, _ = g.shape

    def body(core_ref, g_ref, land_ref, o_ref):
        o_ref[...] = (g_ref[...].astype(F32) + land_ref[...].astype(F32)).astype(o_ref.dtype)

    blk = pl.BlockSpec((1, rows, HALF), lambda i, w: (i, 0, 0))
    return pl.pallas_call(
        body, name=name, out_shape=jax.ShapeDtypeStruct((n, rows, HALF), g.dtype), compiler_params=_cp("parallel"),
        grid_spec=pltpu.PrefetchScalarGridSpec(
            num_scalar_prefetch=1, grid=(n,),
            in_specs=[pl.BlockSpec((1, rows, HALF), lambda i, w: (i, 0, w[0])), blk], out_specs=blk))(
        jnp.reshape(core, (1,)).astype(jnp.int32), g, land)


def _sum_chips(p, land, me, *, tc, name):
    rows = p.shape[1]

    def body(me_ref, p_ref, land_ref, o_ref):
        acc = p_ref[...].astype(F32)
        for k in range(3):
            acc = acc + land_ref[k].astype(F32)
        o_ref[...] = acc

    return pl.pallas_call(
        body, name=name, out_shape=jax.ShapeDtypeStruct((rows, HALF), F32), compiler_params=_cp("parallel"),
        grid_spec=pltpu.PrefetchScalarGridSpec(
            num_scalar_prefetch=1, grid=(HALF // tc,),
            in_specs=[pl.BlockSpec((None, rows, tc), lambda i, w: (w[0], 0, i)),
                      pl.BlockSpec((3, rows, tc), lambda i, w: (0, 0, i))],
            out_specs=pl.BlockSpec((rows, tc), lambda i, w: (0, i))))(
        jnp.reshape(me, (1,)).astype(jnp.int32), p, land)


def _sum_slots(a, *, name):
    n = a.shape[0]

    def body(a_ref, o_ref):
        acc = a_ref[0]
        for k in range(1, n):
            acc = acc + a_ref[k]
        o_ref[...] = acc

    return pl.pallas_call(body, name=name, out_shape=jax.ShapeDtypeStruct(a.shape[1:], a.dtype))(a)


def _elementwise(fn, ins, n_out, block, *, name):
    shape = ins[0].shape
    grid = tuple(s // b for s, b in zip(shape, block))
    n_in = len(ins)

    def body(*refs):
        outs = fn(*[r[...] for r in refs[:n_in]])
        for o_ref, val in zip(refs[n_in:], outs):
            o_ref[...] = val

    spec = pl.BlockSpec(block, lambda i, j, k: (i, j, k))
    return pl.pallas_call(body, name=name, grid=grid, in_specs=[spec] * n_in, out_specs=[spec] * n_out,
                          out_shape=[jax.ShapeDtypeStruct(shape, F32)] * n_out,
                          compiler_params=_cp(*["parallel"] * 3))(*ins)


def _adamw_math(w, g, m, v):
    mn = ADAM_B1 * m + (1.0 - ADAM_B1) * g
    vn = ADAM_B2 * v + (1.0 - ADAM_B2) * (g * g)
    m_hat = mn / (1.0 - ADAM_B1 ** ADAM_STEP)
    v_hat = vn / (1.0 - ADAM_B2 ** ADAM_STEP)
    return -ADAM_LR * (m_hat / (jnp.sqrt(v_hat) + ADAM_EPS) + ADAM_WD * w), mn, vn


def _adamw(w, g, m, v, block, *, name):
    return _elementwise(_adamw_math, [w, g, m, v], 3, block, name=name)


def _interleave_layers(layers, *, tc, name):
    rows, cols = layers[0].shape
    n = len(layers)

    def body(*refs):
        for l in range(n):
            refs[n][:, l, :] = refs[l][...]

    return pl.pallas_call(body, name=name, grid=(cols // tc,),
                          in_specs=[pl.BlockSpec((rows, tc), lambda i: (0, i))] * n,
                          out_specs=pl.BlockSpec((rows, n, tc), lambda i: (0, 0, i)),
                          out_shape=jax.ShapeDtypeStruct((rows, n, cols), layers[0].dtype),
                          compiler_params=_cp("parallel"))(*layers)


def _adamw_small(ws, gs, ms, vs, *, name):
    n = len(ws)

    def body(*refs):
        w, g, m, v, outs = refs[:n], refs[n:2 * n], refs[2 * n:3 * n], refs[3 * n:4 * n], refs[4 * n:]
        for k in range(n):
            for slot, val in enumerate(_adamw_math(w[k][...], g[k][...], m[k][...], v[k][...])):
                outs[slot * n + k][...] = val

    outs = pl.pallas_call(body, name=name, out_shape=[jax.ShapeDtypeStruct(a.shape, F32) for a in ws] * 3)(
        *ws, *gs, *ms, *vs)
    return outs[:n], outs[n:2 * n], outs[2 * n:]


def _to_kernel_order(wt):
    gates = jnp.pad(wt[2048:2056], ((0, LANE - 2 * A_HEADS), (0, 0)))
    return jnp.concatenate([wt[0:2048], wt[2056:2568], wt[2824:3336], wt[2568:2696], wt[2696:2824], gates], axis=0)


def _from_kernel_order(main, tail):
    return jnp.concatenate([main[0:2048], tail[C_BG - DH_MAIN:C_BG - DH_MAIN + 2 * A_HEADS],
                            main[C_QB:C_QB + B_WIDTH], tail[0:B_KV_WIDTH], tail[B_KV_WIDTH:2 * B_KV_WIDTH],
                            main[C_ZB:C_ZB + B_WIDTH]], axis=0)


def _gate_params(a_log, dt_bias):
    return jnp.pad(jnp.stack([a_log, dt_bias]), ((0, SUBLANE - 2), (A_HEADS, LANE - 2 * A_HEADS)))


SMALL = ("conv_w", "a_log", "dt_bias", "norm_w", "sinks", "ln_g", "ln_b")


def _pack(parts, cols):
    flat = jnp.concatenate([p.reshape(-1) for p in parts])
    rows = -(-flat.shape[0] // cols)
    return jnp.pad(flat, (0, rows * cols - flat.shape[0])).reshape(rows, cols)


def _unpack(packed, shapes):
    flat = packed.reshape(-1)
    out, at = [], 0
    for s in shapes:
        n = math.prod(s)
        out.append(flat[at:at + n].reshape(s))
        at += n
    return out


def kernel(x, w_in, conv_w, a_log, dt_bias, norm_w, sinks, w_out, ln_g, ln_b, loss_target, m_w_in, m_conv_w, m_a_log, m_dt_bias, m_norm_w, m_sinks, m_w_out, m_ln_g, m_ln_b, v_w_in, v_conv_w, v_a_log, v_dt_bias, v_norm_w, v_sinks, v_w_out, v_ln_g, v_ln_b):
    xi, yi, ci = lax.axis_index("x"), lax.axis_index("y"), lax.axis_index("c")
    me = 2 * xi + yi

    to_t = lambda a: jnp.transpose(a, (2, 0, 1))
    from_t = lambda a: jnp.transpose(a, (1, 2, 0))

    wt_shard = to_t(w_in)

    def pack_weights(l):
        rows = jnp.pad(wt_shard[:, l], ((0, IN_PAD - IN_SHARD), (0, 0)))
        return jnp.concatenate([rows, w_out[l]], axis=0).astype(BF16)

    pack0, pack1 = pack_weights(0), pack_weights(1)
    got_in0, g_conv = _gather_two_level(pack0[:IN_PAD], conv_w, name="gather_weights_0")
    conv_full = jnp.moveaxis(g_conv, 0, 2).reshape(DEPTH, CONV_K, 3 * A_WIDTH)
    piece = IN_PAD // 3
    carriers = ("dn_pre", "dn_wy", "dn_scan")
    gathers = {nm: _gather_exchange([pack1[i * piece:(i + 1) * piece]]) for i, nm in enumerate(carriers)}
    gathers.update(in_proj=_gather_exchange([pack0[IN_PAD:]]), swa=_gather_exchange([pack1[IN_PAD:]]))
    w_in_of = lambda rows: _to_kernel_order(rows[:, :IN_SHARD].reshape(IN_COLS, D_MODEL))
    w_out_of = lambda rows: rows.reshape(D_MODEL, D_MODEL)
    args0 = _layer_args(w_in_of(got_in0), conv_full[0], a_log[0], dt_bias[0], sinks[0], norm_w[0],
                        lambda got: w_out_of(got[0]))

    def args1(got):
        rows = jnp.concatenate([got[nm][0] for nm in carriers], axis=1)
        return _layer_args(w_in_of(rows), conv_full[1], a_log[1], dt_bias[1], sinks[1], norm_w[1],
                           w_out_of(got["swa"][0]))

    def pack_grads(g):
        gin = _from_kernel_order(*g["w_in"]).reshape(N_SHARD, IN_SHARD, D_MODEL)
        gin = jnp.pad(gin, ((0, 0), (0, IN_PAD - IN_SHARD), (0, 0)))
        return jnp.concatenate([gin, g["w_out"].reshape(N_SHARD, OUT_SHARD, D_MODEL)], axis=1).astype(BF16)

    packed = {}

    def reduce1(grads1):
        packed[1] = pack_grads(grads1)
        return _reduce_scatter_exchange(packed[1])

    def reduce0(grads0, grads1, loss_tile):
        g0 = pack_grads(grads0)
        from_sibling = _run_exchange(_pair_window_exchange(g0), name="pair_reduce_0")[0]
        packed[0] = _pair_add(g0, from_sibling, ci, name="pair_add_0")
        gsmall = _pack([jnp.stack([g[nm] for g in (grads0, grads1)]) for nm in SMALL] + [loss_tile[0, 0:1]], D_MODEL)
        return _chip_scatter_exchange(packed[0], gsmall)

    _, dx, grads, landed1, (landed0, landed_small) = _local_step(
        x[0], loss_target[0], args0, args1, ln_g, ln_b, gathers=gathers, reduce1=reduce1, reduce0=reduce0)

    small_shapes = [(DEPTH,) + grads[0][nm].shape for nm in SMALL]
    halves = [_sum_chips(packed[0], landed0, me, tc=2 * LANE, name="reduce_sum_0"),
              _sum_scatter(packed[1], landed1[0], me, ci, tc=2 * LANE, name="reduce_sum_1")]
    s_small = _sum_slots(landed_small, name="reduce_sum_small")
    others = _run_exchange(_share_exchange(halves), name="pair_share")
    full = [jnp.where(ci == 0, jnp.concatenate([mine, other], axis=1), jnp.concatenate([other, mine], axis=1))
            for mine, other in zip(halves, others)]
    grad_in_layers = [f[:IN_SHARD] for f in full]
    grad_out = jnp.stack([f[IN_PAD:] for f in full])
    out_blk = (1, OUT_SHARD, D_MODEL)
    *small_grads, loss = _unpack(s_small, small_shapes + [()])
    gs = dict(zip(SMALL, small_grads))
    gs["conv_w"] = lax.dynamic_slice_in_dim(gs["conv_w"], me * CONV_SHARD, CONV_SHARD, axis=2)

    grad_in_t = _interleave_layers(grad_in_layers, tc=2 * LANE, name="grad_in_layers")
    d_in, nm_in, nv_in = (from_t(o) for o in _adamw(to_t(w_in), grad_in_t, to_t(m_w_in), to_t(v_w_in),
                                                    (IN_SHARD // 6, DEPTH, D_MODEL), name="adamw_in"))
    grad_in = from_t(grad_in_t)
    d_out, nm_out, nv_out = _adamw(w_out, grad_out, m_w_out, v_w_out, out_blk, name="adamw_out")
    ws = dict(conv_w=conv_w, a_log=a_log, dt_bias=dt_bias, norm_w=norm_w, sinks=sinks, ln_g=ln_g, ln_b=ln_b)
    ms = dict(conv_w=m_conv_w, a_log=m_a_log, dt_bias=m_dt_bias, norm_w=m_norm_w, sinks=m_sinks, ln_g=m_ln_g, ln_b=m_ln_b)
    vs = dict(conv_w=v_conv_w, a_log=v_a_log, dt_bias=v_dt_bias, norm_w=v_norm_w, sinks=v_sinks, ln_g=v_ln_g, ln_b=v_ln_b)
    d_s, nm_s, nv_s = (dict(zip(SMALL, o)) for o in _adamw_small(*[[d[nm] for nm in SMALL] for d in (ws, gs, ms, vs)],
                                                                 name="adamw_small"))

    def in_order(big_in, small, big_out):
        return (big_in, small["conv_w"], small["a_log"], small["dt_bias"], small["norm_w"], small["sinks"], big_out,
                small["ln_g"], small["ln_b"])

    return (loss, dx[None], *in_order(grad_in, gs, grad_out), *in_order(d_in, d_s, d_out),
            *in_order(nm_in, nm_s, nm_out), *in_order(nv_in, nv_s, nv_out))
```

```python
import math

import jax
import jax.numpy as jnp
from jax import lax
from jax.experimental import pallas as pl
from jax.experimental.pallas import tpu as pltpu

F32 = jnp.float32
BF16 = jnp.bfloat16
HI = lax.Precision.HIGHEST

D_MODEL = 1024
DEPTH = 2
A_HEADS = 4
A_HEAD_DIM = 128
A_WIDTH = 512
CONV_K = 4
CHUNK = 64
B_Q_HEADS = 8
B_KV_HEADS = 2
B_HEAD_DIM = 64
B_GROUP = 4
B_WIDTH = 512
B_KV_WIDTH = 128
BLOCK = 128
IN_COLS = 3336
DEEPNORM_ALPHA = (2 * DEPTH) ** 0.25
LN_EPS = 1e-5
RMS_EPS = 1e-6
L2_EPS = 1e-6
ADAM_LR = 0.001
ADAM_B1 = 0.9
ADAM_B2 = 0.999
ADAM_EPS = 1e-08
ADAM_WD = 0.01
ADAM_STEP = 10

N_SHARD = 4
IN_SHARD = IN_COLS // N_SHARD
OUT_SHARD = D_MODEL // N_SHARD
CONV_SHARD = 3 * A_WIDTH // N_SHARD
IN_PAD = -(-IN_SHARD // 96) * 96

P_COLS = 3456
C_PRE = 0
C_ZA = 1536
C_QB = 2048
C_ZB = 2560
C_KB = 3072
C_VB = 3200
C_BG = 3328
DH_MAIN = C_KB
LANE = 128
SUBLANE = 8
HALO = 16
VMEM_LIMIT = 56 * 1024 * 1024
ALIBI = tuple(2.0 ** (-8.0 * (h + 1) / B_Q_HEADS) for h in range(B_Q_HEADS))
NEG = -1e30


def _cp(*sem):
    return pltpu.CompilerParams(dimension_semantics=sem, vmem_limit_bytes=VMEM_LIMIT)


def _dot(a, b):
    return jnp.dot(a.astype(BF16), b.astype(BF16), preferred_element_type=F32)


def _dot_nt(a, b):
    return lax.dot_general(a.astype(BF16), b.astype(BF16), (((1,), (1,)), ((), ())),
                           preferred_element_type=F32)


def _dot_tn(a, b):
    return lax.dot_general(a.astype(BF16), b.astype(BF16), (((0,), (0,)), ((), ())),
                           preferred_element_type=F32)


def _dot_hi(a, b):
    return jnp.dot(a, b, precision=HI, preferred_element_type=F32)


def _sigmoid(x):
    return jax.nn.sigmoid(x)


def _silu(x):
    return x * _sigmoid(x)


def _silu_and_grad(x):
    s = _sigmoid(x)
    return x * s, s * (1.0 + x * (1.0 - s))


def _softplus(x):
    return jnp.maximum(x, 0.0) + jnp.log(1.0 + jnp.exp(-jnp.abs(x)))


def _shift_down(cur, before, s):
    if s == 0:
        return cur
    r = pltpu.roll(cur, s, 0)
    rb = pltpu.roll(before, s, 0)
    row = lax.broadcasted_iota(jnp.int32, before.shape, 0)
    head = jnp.where(row < s, rb, r[0:SUBLANE])
    return jnp.concatenate([head, r[SUBLANE:]], axis=0)


def _shift_up(cur, after, s):
    if s == 0:
        return cur
    n = cur.shape[0]
    r = pltpu.roll(cur, n - s, 0)
    ra = pltpu.roll(after, SUBLANE - s, 0)
    row = lax.broadcasted_iota(jnp.int32, after.shape, 0)
    tail = jnp.where(row >= SUBLANE - s, ra, r[n - SUBLANE:])
    return jnp.concatenate([r[:n - SUBLANE], tail], axis=0)


def _conv_fwd(cur, before, w):
    acc = cur * w[CONV_K - 1:CONV_K, :]
    for s in range(1, CONV_K):
        acc = acc + _shift_down(cur, before, s) * w[CONV_K - 1 - s:CONV_K - s, :]
    return acc


def _matmul_nt(a, bt, *, tm, name, carry=None):
    m, k = a.shape
    n = bt.shape[0]
    c_ins, c_in_specs, c_out_specs, c_outs, c_scratch = _carry_specs(carry)

    def body(*refs):
        a_ref, b_ref, o_ref = _carried(carry, refs, 2, 1, m // tm)
        o_ref[...] = _dot_nt(a_ref[...], b_ref[...]).astype(o_ref.dtype)

    outs = pl.pallas_call(
        body, name=name, grid=(m // tm,),
        in_specs=[pl.BlockSpec((tm, k), lambda i: (i, 0)), pl.BlockSpec((n, k), lambda i: (0, 0))] + c_in_specs,
        out_specs=[pl.BlockSpec((tm, n), lambda i: (i, 0))] + c_out_specs,
        out_shape=[jax.ShapeDtypeStruct((m, n), BF16)] + c_outs,
        scratch_shapes=c_scratch,
        compiler_params=_cp("arbitrary"))(a, bt, *c_ins)
    return outs[0], outs[1:]


def _dn_pre(h, conv_w, par, *, tt, name, carry=None):
    t = h.shape[0]
    cw = 3 * A_WIDTH
    hb = tt // HALO

    c_ins, c_in_specs, c_out_specs, c_outs, c_scratch = _carry_specs(carry)

    def body(*refs):
        (pre_ref, halo_ref, bgi_ref, cw_ref, par_ref,
         q_ref, k_ref, v_ref, bg_ref, bgt_ref) = _carried(carry, refs, 5, 5, t // tt)
        i = pl.program_id(0)
        cur = pre_ref[...].astype(F32)
        before = jnp.where(i > 0, halo_ref[...].astype(F32)[HALO - SUBLANE:], 0.0)
        s = _silu(_conv_fwd(cur, before, cw_ref[...]))
        for hd in range(A_HEADS):
            sl = slice(hd * LANE, (hd + 1) * LANE)
            tq = s[:, hd * LANE:(hd + 1) * LANE]
            q_ref[:, sl] = tq * (lax.rsqrt(jnp.sum(tq * tq, -1, keepdims=True) + L2_EPS) * (A_HEAD_DIM ** -0.5))
            tk = s[:, A_WIDTH + hd * LANE:A_WIDTH + (hd + 1) * LANE]
            k_ref[:, sl] = tk * lax.rsqrt(jnp.sum(tk * tk, -1, keepdims=True) + L2_EPS)
        v_ref[...] = s[:, 2 * A_WIDTH:]
        raw = bgi_ref[...].astype(F32)
        lane = lax.broadcasted_iota(jnp.int32, raw.shape, 1)
        is_a = (lane >= A_HEADS) & (lane < 2 * A_HEADS)
        g = jnp.where(is_a, -jnp.exp(par_ref[0:1, :]) * _softplus(raw + par_ref[1:2, :]), 0.0)
        gc = _dot_hi(_chunk_tri(tt, lower=True), g)
        bg = jnp.where(lane < A_HEADS, _sigmoid(raw), gc)
        bg_ref[...] = bg
        bgt_ref[...] = jnp.transpose(bg)[0:SUBLANE, :]

    wide = jax.ShapeDtypeStruct((t, A_WIDTH), F32)
    outs = pl.pallas_call(
        body, name=name, grid=(t // tt,),
        in_specs=[pl.BlockSpec((tt, cw), lambda i: (i, 0)),
                  pl.BlockSpec((HALO, cw), lambda i: (jnp.maximum(i * hb - 1, 0), 0)),
                  pl.BlockSpec((tt, LANE), lambda i: (i, C_BG // LANE)),
                  pl.BlockSpec((CONV_K, cw), lambda i: (0, 0)),
                  pl.BlockSpec((SUBLANE, LANE), lambda i: (0, 0))] + c_in_specs,
        out_specs=[pl.BlockSpec((tt, A_WIDTH), lambda i: (i, 0))] * 3
        + [pl.BlockSpec((tt, LANE), lambda i: (i, 0)), pl.BlockSpec((SUBLANE, tt), lambda i: (0, i))] + c_out_specs,
        out_shape=[wide, wide, wide, jax.ShapeDtypeStruct((t, LANE), F32),
                   jax.ShapeDtypeStruct((SUBLANE, t), F32)] + c_outs,
        scratch_shapes=c_scratch,
        compiler_params=_cp("arbitrary"))(h, h, h, conv_w, par, *c_ins)
    return outs[:5], outs[5:]


def _chunk_tri(n, lower):
    r = lax.broadcasted_iota(jnp.int32, (n, n), 0)
    c = lax.broadcasted_iota(jnp.int32, (n, n), 1)
    shift = CHUNK.bit_length() - 1
    same = jnp.right_shift(r, shift) == jnp.right_shift(c, shift)
    return (same & ((c <= r) if lower else (c >= r))).astype(F32)


def _chunk_masks():
    r = lax.broadcasted_iota(jnp.int32, (CHUNK, CHUNK), 0)
    c = lax.broadcasted_iota(jnp.int32, (CHUNK, CHUNK), 1)
    return r >= c, r > c, r == c


def _split(a):
    hi = a.astype(BF16)
    return hi, (a - hi.astype(F32)).astype(BF16)


def _dot3(a, b):
    (ah, al), (bh, bl) = a, b
    d = lambda p, q: jnp.dot(p, q, preferred_element_type=F32)
    return d(ah, bh) + (d(ah, bl) + d(al, bh))


def _tri_inv_many(a_list, eye):
    d = lambda p, q: jnp.dot(p, q, preferred_element_type=F32)
    p = [(-a).astype(BF16) for a in a_list]
    tm = [eye - a for a in a_list]
    for _ in range(5):
        pf = [d(pi, pi) for pi in p]
        p = [x.astype(BF16) for x in pf]
        tm = [t + d(t.astype(BF16), pi) for t, pi in zip(tm, p)]
    ms = [_split(eye + a) for a in a_list]
    res = [eye - _dot3(m, _split(t)) for m, t in zip(ms, tm)]
    return [t + d(t.astype(BF16), r.astype(BF16)) for t, r in zip(tm, res)]


def _chunk_gates(bg_v, bgt_v, hd):
    return (bg_v[:, hd:hd + 1], bg_v[:, A_HEADS + hd:A_HEADS + hd + 1],
            None if bgt_v is None else bgt_v[A_HEADS + hd:A_HEADS + hd + 1, :])


WY_ROWS = 512
SCAN_ROWS = 512
WY_GROUP = 8


def _dn_wy(q, k, v, bg, bgt, *, name, carry=None):
    t = q.shape[0]
    rows = WY_ROWS

    c_ins, c_in_specs, c_out_specs, c_outs, c_scratch = _carry_specs(carry)

    def body(*refs):
        q_ref, k_ref, v_ref, bg_ref, bgt_ref, u_ref, w_ref, tm_ref, qk_ref = _carried(carry, refs, 5, 4, t // rows)
        causal, strict, diag = _chunk_masks()
        eye = diag.astype(F32)
        for c0 in range(0, rows // CHUNK, WY_GROUP):
            items = [(c, hd) for c in range(c0, c0 + WY_GROUP) for hd in range(A_HEADS)]
            rs = lambda c: slice(c * CHUNK, (c + 1) * CHUNK)
            sl = lambda hd: slice(hd * LANE, (hd + 1) * LANE)
            hs = lambda hd: slice(hd * CHUNK, (hd + 1) * CHUNK)
            gates = [_chunk_gates(bg_ref[rs(c), :], bgt_ref[:, rs(c)], hd) for c, hd in items]
            dms = [jnp.exp(jnp.where(causal, gcol - grow, NEG)) for _, gcol, grow in gates]
            kbs = [k_ref[rs(c), sl(hd)] * g[0] for (c, hd), g in zip(items, gates)]
            a_list = [jnp.where(strict, _dot_nt(kb, k_ref[rs(c), sl(hd)]) * dm, 0.0)
                      for (c, hd), kb, dm in zip(items, kbs, dms)]
            for (c, hd), dm in zip(items, dms):
                qk_ref[rs(c), hs(hd)] = jnp.where(
                    causal, _dot_nt(q_ref[rs(c), sl(hd)], k_ref[rs(c), sl(hd)]) * dm, 0.0)
            tms = _tri_inv_many(a_list, eye)
            for (c, hd), g, kb, tmat in zip(items, gates, kbs, tms):
                tm_ref[rs(c), hs(hd)] = tmat
                u_ref[rs(c), sl(hd)] = _dot(tmat, v_ref[rs(c), sl(hd)] * g[0])
                w_ref[rs(c), sl(hd)] = _dot(tmat, kb * jnp.exp(g[1])).astype(BF16)

    blk = pl.BlockSpec((rows, A_WIDTH), lambda i: (i, 0))
    half = pl.BlockSpec((rows, A_HEADS * CHUNK), lambda i: (i, 0))
    outs = pl.pallas_call(
        body, name=name, grid=(t // rows,),
        in_specs=[blk, blk, blk, pl.BlockSpec((rows, LANE), lambda i: (i, 0)),
                  pl.BlockSpec((SUBLANE, rows), lambda i: (0, i))] + c_in_specs,
        out_specs=[blk, blk, half, half] + c_out_specs,
        out_shape=[jax.ShapeDtypeStruct((t, A_WIDTH), F32), jax.ShapeDtypeStruct((t, A_WIDTH), BF16),
                   jax.ShapeDtypeStruct((t, A_HEADS * CHUNK), F32),
                   jax.ShapeDtypeStruct((t, A_HEADS * CHUNK), F32)] + c_outs,
        scratch_shapes=c_scratch,
        compiler_params=_cp("arbitrary"))(q, k, v, bg, bgt, *c_ins)
    return outs[:4], outs[4:]


def _dn_scan_fwd(q, k, u, w, qk, bg, *, name, carry=None):
    t = q.shape[0]
    rows = SCAN_ROWS
    per = rows // CHUNK
    c_ins, c_in_specs, c_out_specs, c_outs, c_scratch = _carry_specs(carry)

    def body(*refs):
        q_ref, k_ref, u_ref, w_ref, qk_ref, bg_ref, o_ref, vn_ref, s_ref, state = _carried(carry, refs, 6, 3, t // rows)

        @pl.when(pl.program_id(0) == 0)
        def _():
            state[...] = jnp.zeros_like(state)

        heads = range(A_HEADS)
        sl = lambda hd: slice(hd * LANE, (hd + 1) * LANE)
        s_cur = [state[hd] for hd in heads]
        for c in range(per):
            rs = slice(c * CHUNK, (c + 1) * CHUNK)
            bg_v = bg_ref[rs, :]
            gcols = [_chunk_gates(bg_v, None, hd)[1] for hd in heads]
            glasts = [gc[CHUNK - 1:CHUNK, :] for gc in gcols]
            for hd in heads:
                s_ref[c, hd] = s_cur[hd].astype(BF16)
            vns = [u_ref[rs, sl(hd)] - _dot(w_ref[rs, sl(hd)], s_cur[hd]) for hd in heads]
            qss = [_dot(q_ref[rs, sl(hd)] * jnp.exp(gcols[hd]), s_cur[hd]) for hd in heads]
            s_cur = [s_cur[hd] * jnp.exp(glasts[hd])
                     + _dot_tn(k_ref[rs, sl(hd)] * jnp.exp(glasts[hd] - gcols[hd]), vns[hd]) for hd in heads]
            for hd in heads:
                vn_ref[rs, sl(hd)] = vns[hd]
                o_ref[rs, sl(hd)] = qss[hd] + _dot(qk_ref[rs, hd * CHUNK:(hd + 1) * CHUNK], vns[hd])
        for hd in heads:
            state[hd] = s_cur[hd]

    blk = pl.BlockSpec((rows, A_WIDTH), lambda i: (i, 0))
    half = pl.BlockSpec((rows, A_HEADS * CHUNK), lambda i: (i, 0))
    wide = jax.ShapeDtypeStruct((t, A_WIDTH), F32)
    outs = pl.pallas_call(
        body, name=name, grid=(t // rows,),
        in_specs=[blk, blk, blk, blk, half, pl.BlockSpec((rows, LANE), lambda i: (i, 0))] + c_in_specs,
        out_specs=[blk, blk, pl.BlockSpec((per, A_HEADS, LANE, LANE), lambda i: (i, 0, 0, 0))] + c_out_specs,
        out_shape=[wide, wide, jax.ShapeDtypeStruct((t // CHUNK, A_HEADS, LANE, LANE), BF16)] + c_outs,
        scratch_shapes=[pltpu.VMEM((A_HEADS, LANE, LANE), F32)] + c_scratch,
        compiler_params=_cp("arbitrary"))(q, k, u, w, qk, bg, *c_ins)
    return outs[:3], outs[3:]


def _stack_heads(ref, hk):
    return jnp.concatenate([ref[:, h * B_HEAD_DIM:(h + 1) * B_HEAD_DIM].astype(F32)
                            for h in range(hk * B_GROUP, (hk + 1) * B_GROUP)], axis=0)


def _swa_window():
    qi = lax.broadcasted_iota(jnp.int32, (BLOCK, BLOCK), 0)
    kj = lax.broadcasted_iota(jnp.int32, (BLOCK, BLOCK), 1)
    dist = jnp.where(kj > qi, qi + BLOCK - kj, qi - kj).astype(F32)
    rows = lax.broadcasted_iota(jnp.int32, (B_GROUP * BLOCK, BLOCK), 0)
    cols = lax.broadcasted_iota(jnp.int32, (B_GROUP * BLOCK, BLOCK), 1)
    return cols > jnp.bitwise_and(rows, BLOCK - 1), dist


def _swa_group_probs(q_ref, sk_ref, kp, kc, vp, vc, n_blk):
    hks = range(B_KV_HEADS)
    heads = lambda hk: range(hk * B_GROUP, (hk + 1) * B_GROUP)
    ksl = lambda hk: slice(hk * B_HEAD_DIM, (hk + 1) * B_HEAD_DIM)
    upper, dist = _swa_window()
    no_prev = jnp.where(n_blk > 0, 0.0, NEG)
    ones = jnp.ones((BLOCK, B_HEAD_DIM), BF16)
    with_ones = lambda v, hk: jnp.concatenate([v[:, ksl(hk)].astype(BF16), ones], axis=1)
    qs = [_stack_heads(q_ref, hk) * (B_HEAD_DIM ** -0.5) for hk in hks]
    sink = [jnp.concatenate([jnp.broadcast_to(sk_ref[h:h + 1, 0:1], (BLOCK, 1)) for h in heads(hk)], axis=0)
            for hk in hks]
    s = [jnp.where(upper, _dot_nt(qs[hk], kp[:, ksl(hk)]) + no_prev, _dot_nt(qs[hk], kc[:, ksl(hk)]))
         - jnp.concatenate([ALIBI[h] * dist for h in heads(hk)], axis=0) for hk in hks]
    m = [jnp.maximum(jnp.max(s[hk], axis=-1, keepdims=True), sink[hk]) for hk in hks]
    p = [jnp.exp(s[hk] - m[hk]) for hk in hks]
    p_up = [jnp.where(upper, p[hk], 0.0) for hk in hks]
    oe = [jnp.dot(p_up[hk].astype(BF16), with_ones(vp, hk), preferred_element_type=F32)
          + jnp.dot((p[hk] - p_up[hk]).astype(BF16), with_ones(vc, hk), preferred_element_type=F32) for hk in hks]
    ps = [jnp.exp(sink[hk] - m[hk]) for hk in hks]
    inv = [1.0 / (oe[hk][:, B_HEAD_DIM:B_HEAD_DIM + 1] + ps[hk]) for hk in hks]
    return upper, [(qs[hk], p[hk] * inv[hk], ps[hk] * inv[hk], oe[hk][:, :B_HEAD_DIM] * inv[hk]) for hk in hks]


def _swa_specs():
    qspec = lambda c0: pl.BlockSpec((BLOCK, B_WIDTH), lambda i: (i, c0 // B_WIDTH))
    cur = lambda c0: pl.BlockSpec((BLOCK, LANE), lambda i: (i, c0 // LANE))
    prev = lambda c0: pl.BlockSpec((BLOCK, LANE), lambda i: (jnp.maximum(i - 1, 0), c0 // LANE))
    return qspec, cur, prev


def _carried(carry, refs, n_in, n_out, steps):
    if carry is None:
        return refs
    ci, co = len(carry.ins), len(carry.outs)
    own = refs[:n_in] + refs[n_in + ci:n_in + ci + n_out] + refs[n_in + ci + n_out + co:len(refs) - 3]
    parts = refs[n_in:n_in + ci], refs[n_in + ci + n_out:n_in + ci + n_out + co], refs[len(refs) - 3:]

    @pl.when(pl.program_id(0) == 0)
    def _():
        carry.start(*parts)

    @pl.when(pl.program_id(0) == steps - 1)
    def _():
        carry.finish(*parts)

    return own


def _carry_specs(carry):
    if carry is None:
        return [], [], [], [], []
    return (list(carry.ins), [_ANY] * len(carry.ins), [_ANY] * len(carry.outs), list(carry.outs), carry.scratch())


def _swa_fwd(h, sinks_b, *, name, carry=None):
    t = h.shape[0]
    qspec, cur, prev = _swa_specs()
    c_ins, c_in_specs, c_out_specs, c_outs, c_scratch = _carry_specs(carry)

    def body(*refs):
        q_ref, kc_ref, kp_ref, vc_ref, vp_ref, sk_ref, o_ref = _carried(carry, refs, 6, 1, t // BLOCK)
        n_blk = pl.program_id(0)
        _, groups = _swa_group_probs(q_ref, sk_ref, kp_ref[...], kc_ref[...], vp_ref[...], vc_ref[...], n_blk)
        for hk, (_, _, _, o) in enumerate(groups):
            for g in range(B_GROUP):
                hq = hk * B_GROUP + g
                o_ref[:, hq * B_HEAD_DIM:(hq + 1) * B_HEAD_DIM] = o[g * BLOCK:(g + 1) * BLOCK]

    outs = pl.pallas_call(
        body, name=name, grid=(t // BLOCK,),
        in_specs=[qspec(C_QB), cur(C_KB), prev(C_KB), cur(C_VB), prev(C_VB),
                  pl.BlockSpec((B_Q_HEADS, LANE), lambda i: (0, 0))] + c_in_specs,
        out_specs=[pl.BlockSpec((BLOCK, B_WIDTH), lambda i: (i, 0))] + c_out_specs,
        out_shape=[jax.ShapeDtypeStruct((t, B_WIDTH), F32)] + c_outs,
        scratch_shapes=c_scratch,
        compiler_params=_cp("arbitrary"))(h, h, h, h, h, sinks_b, *c_ins)
    return outs[0], outs[1:]


def _rms_gate(o, za, nw):
    outs = []
    for hd in range(A_HEADS):
        oh = o[:, hd * LANE:(hd + 1) * LANE]
        r = lax.rsqrt(jnp.mean(oh * oh, -1, keepdims=True) + RMS_EPS)
        outs.append(oh * r * nw)
    return jnp.concatenate(outs, axis=1) * _silu(za)


def _out_ln(x, oa, ob, h, norm_w, w_out, ln_g, ln_b, *, tm, name, target=None):
    t = x.shape[0]
    last = target is not None

    def body(*refs):
        x_ref, oa_ref, ob_ref, za_ref, zb_ref, nw_ref, w_ref, g_ref, b_ref = refs[:9]
        xn_ref, mx_ref, r_ref = refs[9 + last:12 + last]
        ya = _rms_gate(oa_ref[...], za_ref[...].astype(F32), nw_ref[...])
        yb = ob_ref[...] * _silu(zb_ref[...].astype(F32))
        mixed = jnp.concatenate([ya, yb], axis=1).astype(BF16)
        mx_ref[...] = mixed
        r = DEEPNORM_ALPHA * x_ref[...] + jnp.dot(mixed, w_ref[...], preferred_element_type=F32)
        r_ref[...] = r
        mu = jnp.mean(r, -1, keepdims=True)
        xc = r - mu
        var = jnp.mean(xc * xc, -1, keepdims=True)
        xn = xc * lax.rsqrt(var + LN_EPS) * g_ref[...] + b_ref[...]
        if not last:
            xn_ref[...] = xn
            return
        loss_ref = refs[13]

        @pl.when(pl.program_id(0) == 0)
        def _():
            loss_ref[...] = jnp.zeros_like(loss_ref)

        err = xn - refs[9][...]
        xn_ref[...] = err * (1.0 / D_MODEL)
        loss_ref[...] += 0.5 / D_MODEL * jnp.sum(err * err)

    row = lambda w, c: pl.BlockSpec((tm, w), lambda i: (i, c))
    full = lambda a, b: pl.BlockSpec((a, b), lambda i: (0, 0))
    wide = jax.ShapeDtypeStruct((t, D_MODEL), F32)
    return pl.pallas_call(
        body, name=name, grid=(t // tm,),
        in_specs=[row(D_MODEL, 0), row(A_WIDTH, 0), row(B_WIDTH, 0), row(A_WIDTH, C_ZA // A_WIDTH),
                  row(B_WIDTH, C_ZB // B_WIDTH), full(1, LANE), full(D_MODEL, D_MODEL), full(1, D_MODEL),
                  full(1, D_MODEL)] + [row(D_MODEL, 0)] * last,
        out_specs=[row(D_MODEL, 0), row(D_MODEL, 0), row(D_MODEL, 0)] + [full(SUBLANE, LANE)] * last,
        out_shape=[wide, jax.ShapeDtypeStruct((t, D_MODEL), BF16), wide]
        + [jax.ShapeDtypeStruct((SUBLANE, LANE), F32)] * last,
        compiler_params=_cp("arbitrary" if last else "parallel"))(
        x, oa, ob, h, h, norm_w, w_out, ln_g, ln_b, *([target] if last else []))


def _layer_fwd(x, wt, conv_w, par, sinks_b, norm_w, w_out_bf, ln_g, ln_b, l, carries=None, target=None):
    carries = carries or {}
    h, got_in = _matmul_nt(x, wt, tm=512, name=f"in_proj_{l}", carry=carries.get("in_proj"))
    if callable(w_out_bf):
        w_out_bf = w_out_bf(got_in)
    (q, k, v, bg, bgt), got_pre = _dn_pre(h, conv_w, par, tt=512, name=f"dn_pre_{l}", carry=carries.get("dn_pre"))
    (u, w, tmat, qk), got_wy = _dn_wy(q, k, v, bg, bgt, name=f"dn_wy_{l}", carry=carries.get("dn_wy"))
    (oa, vn, s_all), got_scan = _dn_scan_fwd(q, k, u, w, qk, bg, name=f"dn_scan_{l}", carry=carries.get("dn_scan"))
    ob, got_swa = _swa_fwd(h, sinks_b, name=f"swa_fwd_{l}", carry=carries.get("swa"))
    xn, mixed, r, *loss = _out_ln(x, oa, ob, h, norm_w, w_out_bf, ln_g, ln_b, tm=512, name=f"out_ln_{l}", target=target)
    if loss:
        xn = (xn, loss[0])
    res = dict(x=x, h=h, q=q, k=k, v=v, bg=bg, bgt=bgt, w=w, tmat=tmat, qk=qk, vn=vn, oa=oa, s_all=s_all,
               mixed=mixed, r=r, w_out=w_out_bf)
    return xn, res, dict(in_proj=got_in, dn_pre=got_pre, dn_wy=got_wy, dn_scan=got_scan, swa=got_swa)


def _ln_out_bwd(dxn, r, mixed, ln_g, w_out, *, tm, name):
    t = dxn.shape[0]

    def body(dxn_ref, r_ref, mx_ref, g_ref, w_ref, dr_ref, dm_ref, dw_ref, dg_ref, db_ref):
        @pl.when(pl.program_id(0) == 0)
        def _():
            dw_ref[...] = jnp.zeros_like(dw_ref)
            dg_ref[...] = jnp.zeros_like(dg_ref)
            db_ref[...] = jnp.zeros_like(db_ref)

        rr = r_ref[...]
        xc = rr - jnp.mean(rr, -1, keepdims=True)
        rstd = lax.rsqrt(jnp.mean(xc * xc, -1, keepdims=True) + LN_EPS)
        xhat = xc * rstd
        dxn_v = dxn_ref[...]
        dxh = dxn_v * g_ref[...]
        dr = rstd * (dxh - jnp.mean(dxh, -1, keepdims=True) - xhat * jnp.mean(dxh * xhat, -1, keepdims=True))
        dr_ref[...] = dr
        dg_ref[...] += jnp.sum(dxn_v * xhat, axis=0, keepdims=True)
        db_ref[...] += jnp.sum(dxn_v, axis=0, keepdims=True)
        drb = dr.astype(BF16)
        dm_ref[...] = _dot_nt(drb, w_ref[...])
        dw_ref[...] += _dot_tn(mx_ref[...], drb)

    row = pl.BlockSpec((tm, D_MODEL), lambda i: (i, 0))
    full = lambda a, b: pl.BlockSpec((a, b), lambda i: (0, 0))
    big = jax.ShapeDtypeStruct((t, D_MODEL), F32)
    vec = jax.ShapeDtypeStruct((1, D_MODEL), F32)
    return pl.pallas_call(
        body, name=name, grid=(t // tm,),
        in_specs=[row, row, row, full(1, D_MODEL), full(D_MODEL, D_MODEL)],
        out_specs=[row, row, full(D_MODEL, D_MODEL), full(1, D_MODEL), full(1, D_MODEL)],
        out_shape=[big, big, jax.ShapeDtypeStruct((D_MODEL, D_MODEL), F32), vec, vec],
        compiler_params=_cp("arbitrary"))(dxn, r, mixed, ln_g, w_out)


def _dn_post_bwd(dm, oa, h, norm_w, *, tm, name):
    t = oa.shape[0]

    def body(dy_ref, o_ref, za_ref, nw_ref, do_ref, dza_ref, dnw_ref):
        @pl.when(pl.program_id(0) == 0)
        def _():
            dnw_ref[...] = jnp.zeros_like(dnw_ref)

        nw = nw_ref[...]
        dnw = jnp.zeros_like(nw)
        for hd in range(A_HEADS):
            sl = slice(hd * LANE, (hd + 1) * LANE)
            oh, za, dy = o_ref[:, sl], za_ref[:, sl].astype(F32), dy_ref[:, sl]
            rs = lax.rsqrt(jnp.mean(oh * oh, -1, keepdims=True) + RMS_EPS)
            nrm = oh * rs
            gate, dgate = _silu_and_grad(za)
            dza_ref[:, sl] = dy * nrm * nw * dgate
            dn = dy * gate
            dnw = dnw + jnp.sum(dn * nrm, axis=0, keepdims=True)
            dnn = dn * nw
            do_ref[:, sl] = rs * dnn - oh * (rs * rs * rs) * jnp.mean(dnn * oh, -1, keepdims=True)
        dnw_ref[...] += dnw

    row = lambda c: pl.BlockSpec((tm, A_WIDTH), lambda i: (i, c))
    wide = jax.ShapeDtypeStruct((t, A_WIDTH), F32)
    return pl.pallas_call(
        body, name=name, grid=(t // tm,),
        in_specs=[row(0), row(0), row(C_ZA // A_WIDTH), pl.BlockSpec((1, LANE), lambda i: (0, 0))],
        out_specs=[row(0), row(C_ZA // A_WIDTH), pl.BlockSpec((1, LANE), lambda i: (0, 0))],
        out_shape=[wide, jax.ShapeDtypeStruct((t, DH_MAIN), F32), jax.ShapeDtypeStruct((1, LANE), F32)],
        compiler_params=_cp("arbitrary"))(dm, oa, h, norm_w)


def _dn_scan_bwd(q, k, w, qk, bg, do, *, name):
    t = q.shape[0]
    rows = SCAN_ROWS
    per = rows // CHUNK
    n = t // rows

    def body(q_ref, k_ref, w_ref, qk_ref, bg_ref, do_ref, dvn_ref, ds_ref, dstate):
        @pl.when(pl.program_id(0) == 0)
        def _():
            dstate[...] = jnp.zeros_like(dstate)

        heads = range(A_HEADS)
        sl = lambda hd: slice(hd * LANE, (hd + 1) * LANE)
        ds_cur = [dstate[hd] for hd in heads]
        for c in reversed(range(per)):
            rs = slice(c * CHUNK, (c + 1) * CHUNK)
            bg_v = bg_ref[rs, :]
            gcols = [_chunk_gates(bg_v, None, hd)[1] for hd in heads]
            glasts = [gc[CHUNK - 1:CHUNK, :] for gc in gcols]
            for hd in heads:
                ds_ref[c, hd] = ds_cur[hd].astype(BF16)
            pdo = [_dot_tn(qk_ref[rs, hd * CHUNK:(hd + 1) * CHUNK], do_ref[rs, sl(hd)]) for hd in heads]
            qdo = [_dot_tn(q_ref[rs, sl(hd)] * jnp.exp(gcols[hd]), do_ref[rs, sl(hd)]) for hd in heads]
            dvns = [pdo[hd] + _dot(k_ref[rs, sl(hd)] * jnp.exp(glasts[hd] - gcols[hd]), ds_cur[hd]) for hd in heads]
            ds_cur = [qdo[hd] + jnp.exp(glasts[hd]) * ds_cur[hd] - _dot_tn(w_ref[rs, sl(hd)], dvns[hd])
                      for hd in heads]
            for hd in heads:
                dvn_ref[rs, sl(hd)] = dvns[hd]
        for hd in heads:
            dstate[hd] = ds_cur[hd]

    blk = pl.BlockSpec((rows, A_WIDTH), lambda i: (n - 1 - i, 0))
    return pl.pallas_call(
        body, name=name, grid=(n,),
        in_specs=[blk, blk, blk, pl.BlockSpec((rows, A_HEADS * CHUNK), lambda i: (n - 1 - i, 0)),
                  pl.BlockSpec((rows, LANE), lambda i: (n - 1 - i, 0)), blk],
        out_specs=[blk, pl.BlockSpec((per, A_HEADS, LANE, LANE), lambda i: (n - 1 - i, 0, 0, 0))],
        out_shape=[jax.ShapeDtypeStruct((t, A_WIDTH), F32),
                   jax.ShapeDtypeStruct((t // CHUNK, A_HEADS, LANE, LANE), BF16)],
        scratch_shapes=[pltpu.VMEM((A_HEADS, LANE, LANE), F32)],
        compiler_params=_cp("arbitrary"))(q, k, w, qk, bg, do)


def _dn_chunk_bwd(q, k, v, vn, tmat, qk, bg, bgt, s_all, ds_all, dvn, do, *, name):
    t = q.shape[0]
    rows = WY_ROWS
    per = rows // CHUNK

    def body(q_ref, k_ref, v_ref, vn_ref, tm_ref, qk_ref, bg_ref, bgt_ref, s_ref, ds_ref, dvn_ref, do_ref,
             dq_ref, dk_ref, dv_ref, dbg_ref, dbgt_ref):
        causal, strict, _ = _chunk_masks()
        lane = lax.broadcasted_iota(jnp.int32, (CHUNK, LANE), 1)
        rowi = lax.broadcasted_iota(jnp.int32, (CHUNK, 1), 0)
        sub = lax.broadcasted_iota(jnp.int32, (SUBLANE, CHUNK), 0)
        rs = lambda c: slice(c * CHUNK, (c + 1) * CHUNK)
        sl = lambda hd: slice(hd * LANE, (hd + 1) * LANE)
        hs = lambda hd: slice(hd * CHUNK, (hd + 1) * CHUNK)
        for c0 in range(0, per, WY_GROUP):
            items = [(c, hd) for c in range(c0, c0 + WY_GROUP) for hd in range(A_HEADS)]
            at = lambda ref: [ref[rs(c), sl(hd)] for c, hd in items]
            qs, ks, vs, dos, vns, dvns = at(q_ref), at(k_ref), at(v_ref), at(do_ref), at(vn_ref), at(dvn_ref)
            tmhs = [tm_ref[rs(c), hs(hd)] for c, hd in items]
            ps = [qk_ref[rs(c), hs(hd)] for c, hd in items]
            gates = [_chunk_gates(bg_ref[rs(c), :], bgt_ref[:, rs(c)], hd) for c, hd in items]
            betas = [g[0] for g in gates]
            gcols = [g[1] for g in gates]
            dmats = [jnp.exp(jnp.where(causal, g[1] - g[2], NEG)) for g in gates]
            es = [jnp.exp(gc) for gc in gcols]
            glasts = [gc[CHUNK - 1:CHUNK, :] for gc in gcols]
            eks = [jnp.exp(gl - gc) for gl, gc in zip(glasts, gcols)]
            kbs = [kh * b for kh, b in zip(ks, betas)]
            vbs = [vh * b for vh, b in zip(vs, betas)]
            kbes = [kb * e for kb, e in zip(kbs, es)]

            a_s = [jnp.where(strict, _dot_nt(kb, kh) * dm, 0.0) for kb, kh, dm in zip(kbs, ks, dmats)]
            dps = [jnp.where(causal, _dot_nt(doh, vnh), 0.0) for doh, vnh in zip(dos, vns)]
            dqds = [_dot_nt(doh, s_ref[c, hd]) for doh, (c, hd) in zip(dos, items)]
            dkds = [_dot_nt(vnh, ds_ref[c, hd]) for vnh, (c, hd) in zip(vns, items)]
            dws = [-_dot_nt(dvnh, s_ref[c, hd]) for dvnh, (c, hd) in zip(dvns, items)]
            dvbs = [_dot_tn(tmh, dvnh) for tmh, dvnh in zip(tmhs, dvns)]
            dgts = [jnp.sum(s_ref[c, hd].astype(F32) * ds_ref[c, hd].astype(F32), keepdims=True) for c, hd in items]
            dts = [_dot_nt(dvnh, vb) + _dot_nt(dw, kbe) for dvnh, vb, dw, kbe in zip(dvns, vbs, dws, kbes)]
            dkbes = [_dot_tn(tmh, dw) for tmh, dw in zip(tmhs, dws)]
            xs = [_dot_nt(dt, tmh) for dt, tmh in zip(dts, tmhs)]
            das = [jnp.where(strict, -_dot_tn(tmh, x), 0.0) for tmh, x in zip(tmhs, xs)]
            dmas = [da * dm for da, dm in zip(das, dmats)]
            dmps = [dp * dm for dp, dm in zip(dps, dmats)]
            dkbs = [_dot(dma, kh) + dkbe * e for dma, kh, dkbe, e in zip(dmas, ks, dkbes, es)]
            for i, (c, hd) in enumerate(items):
                dq_ref[rs(c), sl(hd)] = _dot(dmps[i], ks[i]) + dqds[i] * es[i]
                dk_ref[rs(c), sl(hd)] = (_dot_tn(dmas[i], kbs[i]) + _dot_tn(dmps[i], qs[i]) + dkds[i] * eks[i]
                                         + dkbs[i] * betas[i])
                dv_ref[rs(c), sl(hd)] = dvbs[i] * betas[i]
            for c in range(c0, c0 + WY_GROUP):
                acc = jnp.zeros((CHUNK, LANE), F32)
                acc_t = jnp.zeros((SUBLANE, CHUNK), F32)
                for i, (ci, hd) in enumerate(items):
                    if ci != c:
                        continue
                    gmat = das[i] * a_s[i] + dps[i] * ps[i]
                    rk = jnp.sum(dkds[i] * ks[i], -1, keepdims=True) * eks[i]
                    de = (jnp.sum(dqds[i] * qs[i], -1, keepdims=True)
                          + jnp.sum(dkbes[i] * kbs[i], -1, keepdims=True))
                    dglast = jnp.sum(rk, keepdims=True) + dgts[i] * jnp.exp(glasts[i])
                    dgc = (jnp.sum(gmat, -1, keepdims=True) + de * es[i] - rk
                           + jnp.where(rowi == CHUNK - 1, dglast, 0.0))
                    dbeta = (jnp.sum(dkbs[i] * ks[i], -1, keepdims=True)
                             + jnp.sum(dvbs[i] * vs[i], -1, keepdims=True))
                    acc = acc + jnp.where(lane == hd, dbeta, 0.0) + jnp.where(lane == A_HEADS + hd, dgc, 0.0)
                    acc_t = acc_t + jnp.where(sub == A_HEADS + hd, -jnp.sum(gmat, axis=0, keepdims=True), 0.0)
                dbg_ref[rs(c), :] = acc
                dbgt_ref[:, rs(c)] = acc_t

    blk = pl.BlockSpec((rows, A_WIDTH), lambda i: (i, 0))
    half = pl.BlockSpec((rows, A_HEADS * CHUNK), lambda i: (i, 0))
    col = pl.BlockSpec((rows, LANE), lambda i: (i, 0))
    rowf = pl.BlockSpec((SUBLANE, rows), lambda i: (0, i))
    st = pl.BlockSpec((per, A_HEADS, LANE, LANE), lambda i: (i, 0, 0, 0))
    wide = jax.ShapeDtypeStruct((t, A_WIDTH), F32)
    return pl.pallas_call(
        body, name=name, grid=(t // rows,),
        in_specs=[blk, blk, blk, blk, half, half, col, rowf, st, st, blk, blk],
        out_specs=[blk, blk, blk, col, rowf],
        out_shape=[wide, wide, wide, jax.ShapeDtypeStruct((t, LANE), F32), jax.ShapeDtypeStruct((SUBLANE, t), F32)],
        compiler_params=_cp("parallel"))(q, k, v, vn, tmat, qk, bg, bgt, s_all, ds_all, dvn, do)


def _dn_pre_bwd(h, conv_w, par, dq, dk, dv, dbg, dbgt, *, tt, name):
    t = h.shape[0]
    cw = 3 * A_WIDTH
    hb = tt // HALO

    def body(pre_ref, halo_ref, bgi_ref, cw_ref, par_ref, dq_ref, dk_ref, dv_ref, dbg_ref, dbgt_ref,
             dc_ref, dbgi_ref, dpar_ref):
        i = pl.program_id(0)

        @pl.when(i == 0)
        def _():
            dpar_ref[...] = jnp.zeros_like(dpar_ref)

        cur = pre_ref[...].astype(F32)
        before = jnp.where(i > 0, halo_ref[...].astype(F32)[HALO - SUBLANE:], 0.0)
        c = _conv_fwd(cur, before, cw_ref[...])
        s, ds = _silu_and_grad(c)
        for hd in range(A_HEADS):
            sl = slice(hd * LANE, (hd + 1) * LANE)
            for base, d_ref, scale in ((0, dq_ref, A_HEAD_DIM ** -0.5), (A_WIDTH, dk_ref, 1.0)):
                csl = slice(base + hd * LANE, base + (hd + 1) * LANE)
                tq = s[:, base + hd * LANE:base + (hd + 1) * LANE]
                dy = d_ref[:, sl]
                rq = lax.rsqrt(jnp.sum(tq * tq, -1, keepdims=True) + L2_EPS)
                dtq = scale * (rq * dy - tq * (rq * rq * rq) * jnp.sum(dy * tq, -1, keepdims=True))
                dc_ref[:, csl] = dtq * ds[:, base + hd * LANE:base + (hd + 1) * LANE]
        dc_ref[:, 2 * A_WIDTH:] = dv_ref[...] * ds[:, 2 * A_WIDTH:]
        raw = bgi_ref[...].astype(F32)
        lane = lax.broadcasted_iota(jnp.int32, raw.shape, 1)
        is_b = lane < A_HEADS
        is_a = (lane >= A_HEADS) & (lane < 2 * A_HEADS)
        rows_t = jnp.concatenate([dbgt_ref[...], jnp.zeros((LANE - SUBLANE, tt), F32)], axis=0)
        dbg_v = dbg_ref[...] + jnp.where(is_a, jnp.transpose(rows_t), 0.0)
        dbg_v = jnp.where(is_a, _dot_hi(_chunk_tri(tt, lower=False), jnp.where(is_a, dbg_v, 0.0)), dbg_v)
        beta = _sigmoid(raw)
        z = raw + par_ref[1:2, :]
        neg_ea = -jnp.exp(par_ref[0:1, :])
        g = neg_ea * _softplus(z)
        da = dbg_v * neg_ea * _sigmoid(z)
        dbgi_ref[...] = jnp.where(is_b, dbg_v * beta * (1.0 - beta), jnp.where(is_a, da, 0.0))
        dpar_ref[0:1, :] += jnp.sum(jnp.where(is_a, dbg_v * g, 0.0), axis=0, keepdims=True)
        dpar_ref[1:2, :] += jnp.sum(jnp.where(is_a, da, 0.0), axis=0, keepdims=True)

    wide = pl.BlockSpec((tt, A_WIDTH), lambda i: (i, 0))
    return pl.pallas_call(
        body, name=name, grid=(t // tt,),
        in_specs=[pl.BlockSpec((tt, cw), lambda i: (i, 0)),
                  pl.BlockSpec((HALO, cw), lambda i: (jnp.maximum(i * hb - 1, 0), 0)),
                  pl.BlockSpec((tt, LANE), lambda i: (i, C_BG // LANE)),
                  pl.BlockSpec((CONV_K, cw), lambda i: (0, 0)),
                  pl.BlockSpec((SUBLANE, LANE), lambda i: (0, 0)),
                  wide, wide, wide, pl.BlockSpec((tt, LANE), lambda i: (i, 0)),
                  pl.BlockSpec((SUBLANE, tt), lambda i: (0, i))],
        out_specs=[pl.BlockSpec((tt, cw), lambda i: (i, 0)), pl.BlockSpec((tt, LANE), lambda i: (i, 0)),
                   pl.BlockSpec((SUBLANE, LANE), lambda i: (0, 0))],
        out_shape=[jax.ShapeDtypeStruct((t, cw), F32), jax.ShapeDtypeStruct((t, LANE), F32),
                   jax.ShapeDtypeStruct((SUBLANE, LANE), F32)],
        compiler_params=_cp("arbitrary"))(h, h, h, conv_w, par, dq, dk, dv, dbg, dbgt)


def _conv_bwd(dc, h, conv_w, dh, *, tt, name):
    t = dc.shape[0]
    cw = 3 * A_WIDTH
    hb = tt // HALO
    nb = t // tt

    def body(dc_ref, after_ref, pre_ref, before_ref, cw_ref, dh_in_ref, dpre_ref, dcw_ref):
        i = pl.program_id(0)

        @pl.when(i == 0)
        def _():
            dcw_ref[...] = jnp.zeros_like(dcw_ref)

        dcv = dc_ref[...]
        after = jnp.where(i < nb - 1, after_ref[...], 0.0)
        cur = pre_ref[...].astype(F32)
        before = jnp.where(i > 0, before_ref[...].astype(F32)[HALO - SUBLANE:], 0.0)
        w = cw_ref[...]
        acc = dcv * w[CONV_K - 1:CONV_K, :]
        dcw_ref[CONV_K - 1:CONV_K, :] += jnp.sum(dcv * cur, axis=0, keepdims=True)
        for s in range(1, CONV_K):
            j = CONV_K - 1 - s
            acc = acc + _shift_up(dcv, after, s) * w[j:j + 1, :]
            dcw_ref[j:j + 1, :] += jnp.sum(dcv * _shift_down(cur, before, s), axis=0, keepdims=True)
        dpre_ref[...] = acc

    return pl.pallas_call(
        body, name=name, grid=(nb,),
        in_specs=[pl.BlockSpec((tt, cw), lambda i: (i, 0)),
                  pl.BlockSpec((SUBLANE, cw), lambda i: (jnp.minimum((i + 1) * (tt // SUBLANE), t // SUBLANE - 1), 0)),
                  pl.BlockSpec((tt, cw), lambda i: (i, 0)),
                  pl.BlockSpec((HALO, cw), lambda i: (jnp.maximum(i * hb - 1, 0), 0)),
                  pl.BlockSpec((CONV_K, cw), lambda i: (0, 0)), _ANY],
        out_specs=[pl.BlockSpec((tt, cw), lambda i: (i, 0)), pl.BlockSpec((SUBLANE, cw), lambda i: (0, 0))],
        out_shape=[jax.ShapeDtypeStruct(dh.shape, F32), jax.ShapeDtypeStruct((SUBLANE, cw), F32)],
        input_output_aliases={5: 0},
        compiler_params=_cp("arbitrary"))(dc, dc, h, h, conv_w, dh)


def _swa_bwd(h, dm, sinks_b, dh, *, name, carry=None):
    t = h.shape[0]
    qspec, cur, prev = _swa_specs()
    c_ins, c_in_specs, c_out_specs, c_outs, c_scratch = _carry_specs(carry)

    def body(*refs):
        (q_ref, kc_ref, kp_ref, vc_ref, vp_ref, zb_ref, dy_ref, sk_ref, dh_in_ref,
         dqz_ref, dk_ref, dv_ref, dsk_ref) = _carried(carry, refs, 9, 4, t // BLOCK)
        n_blk = pl.program_id(0)

        @pl.when(n_blk == 0)
        def _():
            dk_ref[...] = jnp.zeros_like(dk_ref)
            dv_ref[...] = jnp.zeros_like(dv_ref)
            dsk_ref[...] = jnp.zeros_like(dsk_ref)

        kp, kc, vp, vc = kp_ref[...], kc_ref[...], vp_ref[...], vc_ref[...]
        scale = B_HEAD_DIM ** -0.5
        hks = range(B_KV_HEADS)
        ksl = lambda hk: slice(hk * B_HEAD_DIM, (hk + 1) * B_HEAD_DIM)
        upper, groups = _swa_group_probs(q_ref, sk_ref, kp, kc, vp, vc, n_blk)
        zbs = [_stack_heads(zb_ref, hk) for hk in hks]
        dys = [_stack_heads(dy_ref, hk) for hk in hks]
        gates = [_silu_and_grad(zbs[hk]) for hk in hks]
        dos = [dys[hk] * gates[hk][0] for hk in hks]
        deltas = [jnp.sum(dos[hk] * groups[hk][3], -1, keepdims=True) for hk in hks]
        dps = [jnp.where(upper, _dot_nt(dos[hk], vp[:, ksl(hk)]), _dot_nt(dos[hk], vc[:, ksl(hk)])) for hk in hks]
        dss = [groups[hk][1] * (dps[hk] - deltas[hk]) for hk in hks]
        ds_up = [jnp.where(upper, dss[hk], 0.0) for hk in hks]
        ds_lo = [dss[hk] - ds_up[hk] for hk in hks]
        p_up = [jnp.where(upper, groups[hk][1], 0.0) for hk in hks]
        p_lo = [groups[hk][1] - p_up[hk] for hk in hks]
        dqs = [(_dot(ds_up[hk], kp[:, ksl(hk)]) + _dot(ds_lo[hk], kc[:, ksl(hk)])) * scale for hk in hks]
        dk_prev = [_dot_tn(ds_up[hk], groups[hk][0]) for hk in hks]
        dk_cur = [_dot_tn(ds_lo[hk], groups[hk][0]) for hk in hks]
        dv_prev = [_dot_tn(p_up[hk], dos[hk]) for hk in hks]
        dv_cur = [_dot_tn(p_lo[hk], dos[hk]) for hk in hks]
        for hk in hks:
            dzb = dys[hk] * groups[hk][3] * gates[hk][1]
            dsink = groups[hk][2] * deltas[hk]
            for g in range(B_GROUP):
                hq = hk * B_GROUP + g
                rows = slice(g * BLOCK, (g + 1) * BLOCK)
                qsl = slice(hq * B_HEAD_DIM, (hq + 1) * B_HEAD_DIM)
                dqz_ref[:, qsl] = dqs[hk][rows]
                dqz_ref[:, B_WIDTH + hq * B_HEAD_DIM:B_WIDTH + (hq + 1) * B_HEAD_DIM] = dzb[rows]
                dsk_ref[hq:hq + 1, :] += -jnp.sum(dsink[rows], keepdims=True)
        at_cur = pl.ds(pl.multiple_of(n_blk * BLOCK, BLOCK), BLOCK)
        at_prev = pl.ds(pl.multiple_of(jnp.maximum(n_blk - 1, 0) * BLOCK, BLOCK), BLOCK)
        dk_ref[at_prev, :] += jnp.concatenate(dk_prev, axis=1)
        dv_ref[at_prev, :] += jnp.concatenate(dv_prev, axis=1)
        dk_ref[at_cur, :] += jnp.concatenate(dk_cur, axis=1)
        dv_ref[at_cur, :] += jnp.concatenate(dv_cur, axis=1)

    narrow = jax.ShapeDtypeStruct((t, B_KV_WIDTH), F32)
    res = lambda a, b: pl.BlockSpec((a, b), lambda i: (0, 0))
    outs = pl.pallas_call(
        body, name=name, grid=(t // BLOCK,),
        in_specs=[qspec(C_QB), cur(C_KB), prev(C_KB), cur(C_VB), prev(C_VB), qspec(C_ZB),
                  pl.BlockSpec((BLOCK, B_WIDTH), lambda i: (i, 1)), res(B_Q_HEADS, LANE), _ANY] + c_in_specs,
        out_specs=[pl.BlockSpec((BLOCK, 2 * B_WIDTH), lambda i: (i, C_QB // (2 * B_WIDTH))),
                   res(t, B_KV_WIDTH), res(t, B_KV_WIDTH), res(B_Q_HEADS, LANE)] + c_out_specs,
        out_shape=[jax.ShapeDtypeStruct(dh.shape, F32), narrow, narrow,
                   jax.ShapeDtypeStruct((B_Q_HEADS, LANE), F32)] + c_outs,
        scratch_shapes=c_scratch,
        input_output_aliases={8: 0},
        compiler_params=_cp("arbitrary"))(h, h, h, h, h, h, dm, sinks_b, dh, *c_ins)
    return outs[:4], outs[4:]


def _in_proj_dw(dh_main, dh_tail, x, *, tk, name):
    t, n = x.shape

    def body(a_ref, t_ref, x_ref, o_ref, ot_ref):
        @pl.when(pl.program_id(0) == 0)
        def _():
            o_ref[...] = jnp.zeros_like(o_ref)
            ot_ref[...] = jnp.zeros_like(ot_ref)

        xb = x_ref[...].astype(BF16)
        o_ref[...] += _dot_tn(a_ref[...], xb)
        ot_ref[...] += _dot_tn(t_ref[...], xb)

    row = lambda a: pl.BlockSpec((tk, a.shape[1]), lambda kk: (kk, 0))
    acc = lambda a: pl.BlockSpec((a.shape[1], n), lambda kk: (0, 0))
    return pl.pallas_call(
        body, name=name, grid=(t // tk,), in_specs=[row(dh_main), row(dh_tail), row(x)],
        out_specs=[acc(dh_main), acc(dh_tail)],
        out_shape=[jax.ShapeDtypeStruct((a.shape[1], n), F32) for a in (dh_main, dh_tail)],
        compiler_params=_cp("arbitrary"))(dh_main, dh_tail, x)


def _in_proj_dx(dh_main, dh_tail, wt, dr, *, tm, name, carry=None):
    t, n_main = dh_main.shape
    n_tail = dh_tail.shape[1]
    c_ins, c_in_specs, c_out_specs, c_outs, c_scratch = _carry_specs(carry)

    def body(*refs):
        a_ref, t_ref, wa_ref, wt_ref, r_ref, o_ref = _carried(carry, refs, 5, 1, t // tm)
        o_ref[...] = _dot(a_ref[...], wa_ref[...]) + _dot(t_ref[...], wt_ref[...]) + DEEPNORM_ALPHA * r_ref[...]

    row = lambda w: pl.BlockSpec((tm, w), lambda i: (i, 0))
    outs = pl.pallas_call(
        body, name=name, grid=(t // tm,),
        in_specs=[row(n_main), row(n_tail), pl.BlockSpec((n_main, D_MODEL), lambda i: (0, 0)),
                  pl.BlockSpec((n_tail, D_MODEL), lambda i: (n_main // n_tail, 0)), row(D_MODEL)] + c_in_specs,
        out_specs=[row(D_MODEL)] + c_out_specs,
        out_shape=[jax.ShapeDtypeStruct((t, D_MODEL), F32)] + c_outs,
        scratch_shapes=c_scratch,
        compiler_params=_cp("arbitrary"))(dh_main, dh_tail, wt, wt, dr, *c_ins)
    return outs[0], outs[1:]


def _layer_bwd(dxn, res, wt, conv_w, par, sinks_b, norm_w, w_out_bf, ln_g, l, carry=None, carry_dx=None):
    w_out_bf = res["w_out"]
    dr, dm, dw_out, dln_g, dln_b = _ln_out_bwd(dxn, res["r"], res["mixed"], ln_g, w_out_bf, tm=512, name=f"ln_out_bwd_{l}")
    h = res["h"]
    do, dh, dnw = _dn_post_bwd(dm, res["oa"], h, norm_w, tm=512, name=f"dn_post_bwd_{l}")
    dvn, ds_all = _dn_scan_bwd(res["q"], res["k"], res["w"], res["qk"], res["bg"], do, name=f"dn_scan_bwd_{l}")
    dq, dk, dv, dbg, dbgt = _dn_chunk_bwd(res["q"], res["k"], res["v"], res["vn"], res["tmat"], res["qk"], res["bg"],
                                          res["bgt"], res["s_all"], ds_all, dvn, do, name=f"dn_chunk_bwd_{l}")
    dc, dbgi, dpar = _dn_pre_bwd(h, conv_w, par, dq, dk, dv, dbg, dbgt, tt=512, name=f"dn_pre_bwd_{l}")
    dh, dcw = _conv_bwd(dc, h, conv_w, dh, tt=512, name=f"conv_bwd_{l}")
    (dh, dkb, dvb, dsk), carried = _swa_bwd(h, dm, sinks_b, dh, name=f"swa_bwd_{l}", carry=carry)
    dh_tail = jnp.concatenate([dkb, dvb, dbgi], axis=1)
    dwt_main, dwt_tail = _in_proj_dw(dh, dh_tail, res["x"], tk=512, name=f"in_proj_dw_{l}")
    grads = dict(w_in=(dwt_main, dwt_tail), conv_w=dcw[:CONV_K], a_log=dpar[0, A_HEADS:2 * A_HEADS],
                 dt_bias=dpar[1, A_HEADS:2 * A_HEADS], norm_w=dnw[0], sinks=dsk[:, 0], w_out=dw_out,
                 ln_g=dln_g[0], ln_b=dln_b[0])
    dx, carried_dx = _in_proj_dx(dh, dh_tail, wt, dr, tm=512, name=f"in_proj_dx_{l}",
                                 carry=None if carry_dx is None else carry_dx(grads))
    return dx, grads, carried, carried_dx


def _layer_args(wt, conv_w, a_log, dt_bias, sinks, norm_w, w_out_bf):
    return (wt, conv_w, _gate_params(a_log, dt_bias), jnp.broadcast_to(sinks[:, None], (B_Q_HEADS, LANE)),
            norm_w[None], w_out_bf)


def _local_step(x, target, args0, args1, ln_g, ln_b, gathers=None, reduce1=None, reduce0=None):
    assert DEPTH == 2
    x1, res0, got = _layer_fwd(x, *args0, ln_g[0][None], ln_b[0][None], 0, carries=gathers)
    if gathers is not None:
        args1 = args1(got)
    (dx, loss_tile), res1, _ = _layer_fwd(x1, *args1, ln_g[1][None], ln_b[1][None], 1, target=target)
    dx, grads1, _, _ = _layer_bwd(dx, res1, *args1, ln_g[1][None], 1)
    carry = None if reduce1 is None else reduce1(grads1)
    carry_dx = None if reduce0 is None else (lambda grads0: reduce0(grads0, grads1, loss_tile))
    dx, grads0, landed1, landed0 = _layer_bwd(dx, res0, *args0, ln_g[0][None], 0, carry=carry, carry_dx=carry_dx)
    return loss_tile, dx, [grads0, grads1], landed1, landed0


_ANY = pl.BlockSpec(memory_space=pl.ANY)
_MESH = pl.DeviceIdType.MESH


HALF = D_MODEL // 2


class _Exchange:
    def __init__(self, ins, outs, n_remote, n_local, plan):
        self.ins, self.outs, self.n_remote, self.n_local, self.plan = tuple(ins), tuple(outs), n_remote, n_local, plan

    def scratch(self):
        return [pltpu.SemaphoreType.DMA((self.n_remote,)), pltpu.SemaphoreType.DMA((self.n_remote,)),
                pltpu.SemaphoreType.DMA((max(self.n_local, 1),))]

    def _copies(self, in_refs, out_refs, sems, arriving):
        send_sems, recv_sems, local_sems = sems
        local, sends, recvs = self.plan(in_refs, out_refs)
        loc = [pltpu.make_async_copy(s, d, local_sems.at[i]) for i, (s, d) in enumerate(local)]
        rem = [pltpu.make_async_remote_copy(src_ref=s, dst_ref=recvs[i] if arriving else d, send_sem=send_sems.at[i],
                                            recv_sem=recv_sems.at[i], device_id=peer, device_id_type=_MESH)
               for i, (s, d, peer) in enumerate(sends)]
        return loc, rem

    def start(self, in_refs, out_refs, sems):
        loc, rem = self._copies(in_refs, out_refs, sems, arriving=False)
        for cp in loc + rem:
            cp.start()

    def finish(self, in_refs, out_refs, sems):
        loc, rem = self._copies(in_refs, out_refs, sems, arriving=True)
        for cp in rem:
            cp.wait_recv()
        for cp in rem:
            cp.wait_send()
        for cp in loc:
            cp.wait()


def _run_exchange(ex, *, name):
    n_in, n_out = len(ex.ins), len(ex.outs)

    def body(*refs):
        parts = refs[:n_in], refs[n_in:n_in + n_out], refs[n_in + n_out:]
        ex.start(*parts)
        ex.finish(*parts)

    return pl.pallas_call(body, name=name, in_specs=[_ANY] * n_in, out_specs=[_ANY] * n_out, out_shape=list(ex.outs),
                          scratch_shapes=ex.scratch())(*ex.ins)


def _place():
    x, y, c = lax.axis_index("x"), lax.axis_index("y"), lax.axis_index("c")
    return x, y, c, [(1 - x, y), (x, 1 - y), (1 - x, 1 - y)]


def _gather_exchange(arrays):
    n = len(arrays)

    def plan(src, dst):
        x, y, c, chips = _place()
        me = 2 * x + y
        local = [(src[k], dst[k].at[me]) for k in range(n)]
        sends = [(src[k], dst[k].at[me], (px, py, c)) for k in range(n) for px, py in chips]
        recvs = [dst[k].at[2 * px + py] for k in range(n) for px, py in chips]
        return local, sends, recvs

    return _Exchange(arrays, [jax.ShapeDtypeStruct((N_SHARD,) + a.shape, a.dtype) for a in arrays], 3 * n, n, plan)


def _gather_two_level(pack, conv_w, *, name):
    rows = pack.shape[0]
    part_rows = rows // 2

    def body(pack_ref, conv_ref, land_ref, conv_land_ref, send1, recv1, send2, recv2, csend, crecv, local_sems):
        x, y, c, chips = _place()
        me = 2 * x + y
        sibling = (x, y, 1 - c)
        part = lambda core: pl.ds(pl.multiple_of(core * part_rows, 16), part_rows)
        remote = lambda src, dst, ss, rs, to: pltpu.make_async_remote_copy(
            src_ref=src, dst_ref=dst, send_sem=ss, recv_sem=rs, device_id=to, device_id_type=_MESH)
        local = [pltpu.make_async_copy(pack_ref, land_ref.at[me], local_sems.at[0]),
                 pltpu.make_async_copy(conv_ref, conv_land_ref.at[me], local_sems.at[1])]
        for cp in local:
            cp.start()
        first = [remote(pack_ref.at[part(c)], land_ref.at[me, part(c)], send1.at[j], recv1.at[j], (px, py, c))
                 for j, (px, py) in enumerate(chips)]
        convs = [remote(conv_ref, conv_land_ref.at[me], csend.at[j], crecv.at[j], (px, py, c))
                 for j, (px, py) in enumerate(chips)]
        for cp in first + convs:
            cp.start()
        passed = []
        for j, (px, py) in enumerate(chips):
            slot = 2 * px + py
            remote(pack_ref.at[part(c)], land_ref.at[slot, part(c)], send1.at[j], recv1.at[j], (px, py, c)).wait_recv()
            cp = remote(land_ref.at[slot, part(c)], land_ref.at[slot, part(c)], send2.at[j], recv2.at[j], sibling)
            cp.start()
            passed.append(cp)
        for j, (px, py) in enumerate(chips):
            slot = 2 * px + py
            remote(land_ref.at[slot, part(1 - c)], land_ref.at[slot, part(1 - c)], send2.at[j], recv2.at[j],
                   sibling).wait_recv()
            remote(conv_ref, conv_land_ref.at[slot], csend.at[j], crecv.at[j], (px, py, c)).wait_recv()
        for cp in first + convs + passed:
            cp.wait_send()
        for cp in local:
            cp.wait()

    sems = [pltpu.SemaphoreType.DMA((3,))] * 6 + [pltpu.SemaphoreType.DMA((2,))]
    return pl.pallas_call(
        body, name=name, in_specs=[_ANY, _ANY], out_specs=[_ANY, _ANY],
        out_shape=[jax.ShapeDtypeStruct((N_SHARD,) + pack.shape, pack.dtype),
                   jax.ShapeDtypeStruct((N_SHARD,) + conv_w.shape, conv_w.dtype)],
        scratch_shapes=sems)(pack, conv_w)


def _half(core):
    return pl.ds(pl.multiple_of(core * HALF, HALF), HALF)


def _reduce_scatter_exchange(g):
    def plan(src, dst):
        x, y, c, chips = _place()
        peers = [(px, py, c if t == 0 else 1 - c) for px, py in chips for t in (0, 1)] + [(x, y, 1 - c)]
        sends = [(src[0].at[2 * px + py, :, _half(pc)], dst[0].at[k], (px, py, pc)) for k, (px, py, pc) in enumerate(peers)]
        return [], sends, [dst[0].at[k] for k in range(7)]

    return _Exchange([g], [jax.ShapeDtypeStruct((7,) + g.shape[1:2] + (HALF,), g.dtype)], 7, 0, plan)


def _pair_window_exchange(g):
    def plan(src, dst):
        x, y, c, _ = _place()
        return [], [(src[0].at[:, :, _half(1 - c)], dst[0], (x, y, 1 - c))], [dst[0]]

    return _Exchange([g], [jax.ShapeDtypeStruct(g.shape[:2] + (HALF,), g.dtype)], 1, 0, plan)


def _chip_scatter_exchange(p, small):
    def plan(src, dst):
        x, y, c, chips = _place()
        mine = 4 * x + 2 * y + c
        peers = [(px, py, c if t == 0 else 1 - c) for px, py in chips for t in (0, 1)] + [(x, y, 1 - c)]
        sends = [(src[0].at[2 * px + py], dst[0].at[j], (px, py, c)) for j, (px, py) in enumerate(chips)]
        recvs = [dst[0].at[j] for j in range(3)]
        sends += [(src[1], dst[1].at[mine], peer) for peer in peers]
        recvs += [dst[1].at[4 * px + 2 * py + pc] for px, py, pc in peers]
        return [(src[1], dst[1].at[mine])], sends, recvs

    outs = [jax.ShapeDtypeStruct((3,) + p.shape[1:], p.dtype), jax.ShapeDtypeStruct((8,) + small.shape, small.dtype)]
    return _Exchange([p, small], outs, 10, 1, plan)


def _share_exchange(arrays):
    n = len(arrays)

    def plan(src, dst):
        x, y, c, _ = _place()
        return [], [(src[k], dst[k], (x, y, 1 - c)) for k in range(n)], [dst[k] for k in range(n)]

    return _Exchange(arrays, [jax.ShapeDtypeStruct(a.shape, a.dtype) for a in arrays], n, 0, plan)


def _sum_scatter(g, land, me, core, *, tc, name):
    rows = g.shape[1]
    per = HALF // tc

    def body(where_ref, g_ref, land_ref, o_ref):
        acc = g_ref[...]
        for k in range(7):
            acc = acc + land_ref[k].astype(F32)
        o_ref[...] = acc

    return pl.pallas_call(
        body, name=name, out_shape=jax.ShapeDtypeStruct((rows, HALF), F32), compiler_params=_cp("parallel"),
        grid_spec=pltpu.PrefetchScalarGridSpec(
            num_scalar_prefetch=1, grid=(per,),
            in_specs=[pl.BlockSpec((None, rows, tc), lambda i, w: (w[0], 0, w[1] * per + i)),
                      pl.BlockSpec((7, rows, tc), lambda i, w: (0, 0, i))],
            out_specs=pl.BlockSpec((rows, tc), lambda i, w: (0, i))))(
        jnp.stack([me, core]).astype(jnp.int32), g, land)


def _pair_add(g, land, core, *, name):
    n, rows, _ = g.shape

    def body(core_ref, g_ref, land_ref, o_ref):
        o_ref[...] = (g_ref[...].astype(F32) + land_ref[...].astype(F32)).astype(o_ref.dtype)

    blk = pl.BlockSpec((1, rows, HALF), lambda i, w: (i, 0, 0))
    return pl.pallas_call(
        body, name=name, out_shape=jax.ShapeDtypeStruct((n, rows, HALF), g.dtype), compiler_params=_cp("parallel"),
        grid_spec=pltpu.PrefetchScalarGridSpec(
            num_scalar_prefetch=1, grid=(n,),
            in_specs=[pl.BlockSpec((1, rows, HALF), lambda i, w: (i, 0, w[0])), blk], out_specs=blk))(
        jnp.reshape(core, (1,)).astype(jnp.int32), g, land)


def _sum_chips(p, land, me, *, tc, name):
    rows = p.shape[1]

    def body(me_ref, p_ref, land_ref, o_ref):
        acc = p_ref[...].astype(F32)
        for k in range(3):
            acc = acc + land_ref[k].astype(F32)
        o_ref[...] = acc

    return pl.pallas_call(
        body, name=name, out_shape=jax.ShapeDtypeStruct((rows, HALF), F32), compiler_params=_cp("parallel"),
        grid_spec=pltpu.PrefetchScalarGridSpec(
            num_scalar_prefetch=1, grid=(HALF // tc,),
            in_specs=[pl.BlockSpec((None, rows, tc), lambda i, w: (w[0], 0, i)),
                      pl.BlockSpec((3, rows, tc), lambda i, w: (0, 0, i))],
            out_specs=pl.BlockSpec((rows, tc), lambda i, w: (0, i))))(
        jnp.reshape(me, (1,)).astype(jnp.int32), p, land)


def _sum_slots(a, *, name):
    n = a.shape[0]

    def body(a_ref, o_ref):
        acc = a_ref[0]
        for k in range(1, n):
            acc = acc + a_ref[k]
        o_ref[...] = acc

    return pl.pallas_call(body, name=name, out_shape=jax.ShapeDtypeStruct(a.shape[1:], a.dtype))(a)


def _elementwise(fn, ins, n_out, block, *, name):
    shape = ins[0].shape
    grid = tuple(s // b for s, b in zip(shape, block))
    n_in = len(ins)

    def body(*refs):
        outs = fn(*[r[...] for r in refs[:n_in]])
        for o_ref, val in zip(refs[n_in:], outs):
            o_ref[...] = val

    spec = pl.BlockSpec(block, lambda i, j, k: (i, j, k))
    return pl.pallas_call(body, name=name, grid=grid, in_specs=[spec] * n_in, out_specs=[spec] * n_out,
                          out_shape=[jax.ShapeDtypeStruct(shape, F32)] * n_out,
                          compiler_params=_cp(*["parallel"] * 3))(*ins)


def _adamw_math(w, g, m, v):
    mn = ADAM_B1 * m + (1.0 - ADAM_B1) * g
    vn = ADAM_B2 * v + (1.0 - ADAM_B2) * (g * g)
    m_hat = mn / (1.0 - ADAM_B1 ** ADAM_STEP)
    v_hat = vn / (1.0 - ADAM_B2 ** ADAM_STEP)
    return -ADAM_LR * (m_hat / (jnp.sqrt(v_hat) + ADAM_EPS) + ADAM_WD * w), mn, vn


def _adamw(w, g, m, v, block, *, name):
    return _elementwise(_adamw_math, [w, g, m, v], 3, block, name=name)


def _interleave_layers(layers, *, tc, name):
    rows, cols = layers[0].shape
    n = len(layers)

    def body(*refs):
        for l in range(n):
            refs[n][:, l, :] = refs[l][...]

    return pl.pallas_call(body, name=name, grid=(cols // tc,),
                          in_specs=[pl.BlockSpec((rows, tc), lambda i: (0, i))] * n,
                          out_specs=pl.BlockSpec((rows, n, tc), lambda i: (0, 0, i)),
                          out_shape=jax.ShapeDtypeStruct((rows, n, cols), layers[0].dtype),
                          compiler_params=_cp("parallel"))(*layers)


def _adamw_small(ws, gs, ms, vs, *, name):
    n = len(ws)

    def body(*refs):
        w, g, m, v, outs = refs[:n], refs[n:2 * n], refs[2 * n:3 * n], refs[3 * n:4 * n], refs[4 * n:]
        for k in range(n):
            for slot, val in enumerate(_adamw_math(w[k][...], g[k][...], m[k][...], v[k][...])):
                outs[slot * n + k][...] = val

    outs = pl.pallas_call(body, name=name, out_shape=[jax.ShapeDtypeStruct(a.shape, F32) for a in ws] * 3)(
        *ws, *gs, *ms, *vs)
    return outs[:n], outs[n:2 * n], outs[2 * n:]


def _to_kernel_order(wt):
    gates = jnp.pad(wt[2048:2056], ((0, LANE - 2 * A_HEADS), (0, 0)))
    return jnp.concatenate([wt[0:2048], wt[2056:2568], wt[2824:3336], wt[2568:2696], wt[2696:2824], gates], axis=0)


def _from_kernel_order(main, tail):
    return jnp.concatenate([main[0:2048], tail[C_BG - DH_MAIN:C_BG - DH_MAIN + 2 * A_HEADS],
                            main[C_QB:C_QB + B_WIDTH], tail[0:B_KV_WIDTH], tail[B_KV_WIDTH:2 * B_KV_WIDTH],
                            main[C_ZB:C_ZB + B_WIDTH]], axis=0)


def _gate_params(a_log, dt_bias):
    return jnp.pad(jnp.stack([a_log, dt_bias]), ((0, SUBLANE - 2), (A_HEADS, LANE - 2 * A_HEADS)))


SMALL = ("conv_w", "a_log", "dt_bias", "norm_w", "sinks", "ln_g", "ln_b")


def _pack(parts, cols):
    flat = jnp.concatenate([p.reshape(-1) for p in parts])
    rows = -(-flat.shape[0] // cols)
    return jnp.pad(flat, (0, rows * cols - flat.shape[0])).reshape(rows, cols)


def _unpack(packed, shapes):
    flat = packed.reshape(-1)
    out, at = [], 0
    for s in shapes:
        n = math.prod(s)
        out.append(flat[at:at + n].reshape(s))
        at += n
    return out


def kernel(x, w_in, conv_w, a_log, dt_bias, norm_w, sinks, w_out, ln_g, ln_b, loss_target, m_w_in, m_conv_w, m_a_log, m_dt_bias, m_norm_w, m_sinks, m_w_out, m_ln_g, m_ln_b, v_w_in, v_conv_w, v_a_log, v_dt_bias, v_norm_w, v_sinks, v_w_out, v_ln_g, v_ln_b):
    xi, yi, ci = lax.axis_index("x"), lax.axis_index("y"), lax.axis_index("c")
    me = 2 * xi + yi

    to_t = lambda a: jnp.transpose(a, (2, 0, 1))
    from_t = lambda a: jnp.transpose(a, (1, 2, 0))

    wt_shard = to_t(w_in)

    def pack_weights(l):
        rows = jnp.pad(wt_shard[:, l], ((0, IN_PAD - IN_SHARD), (0, 0)))
        return jnp.concatenate([rows, w_out[l]], axis=0).astype(BF16)

    pack0, pack1 = pack_weights(0), pack_weights(1)
    got_in0, g_conv = _gather_two_level(pack0[:IN_PAD], conv_w, name="gather_weights_0")
    conv_full = jnp.moveaxis(g_conv, 0, 2).reshape(DEPTH, CONV_K, 3 * A_WIDTH)
    piece = IN_PAD // 3
    carriers = ("dn_pre", "dn_wy", "dn_scan")
    gathers = {nm: _gather_exchange([pack1[i * piece:(i + 1) * piece]]) for i, nm in enumerate(carriers)}
    gathers.update(in_proj=_gather_exchange([pack0[IN_PAD:]]), swa=_gather_exchange([pack1[IN_PAD:]]))
    w_in_of = lambda rows: _to_kernel_order(rows[:, :IN_SHARD].reshape(IN_COLS, D_MODEL))
    w_out_of = lambda rows: rows.reshape(D_MODEL, D_MODEL)
    args0 = _layer_args(w_in_of(got_in0), conv_full[0], a_log[0], dt_bias[0], sinks[0], norm_w[0],
                        lambda got: w_out_of(got[0]))

    def args1(got):
        rows = jnp.concatenate([got[nm][0] for nm in carriers], axis=1)
        return _layer_args(w_in_of(rows), conv_full[1], a_log[1], dt_bias[1], sinks[1], norm_w[1],
                           w_out_of(got["swa"][0]))

    def pack_grads(g):
        gin = _from_kernel_order(*g["w_in"]).reshape(N_SHARD, IN_SHARD, D_MODEL)
        gin = jnp.pad(gin, ((0, 0), (0, IN_PAD - IN_SHARD), (0, 0)))
        return jnp.concatenate([gin, g["w_out"].reshape(N_SHARD, OUT_SHARD, D_MODEL)], axis=1).astype(BF16)

    packed = {}

    def reduce1(grads1):
        packed[1] = pack_grads(grads1)
        return _reduce_scatter_exchange(packed[1])

    def reduce0(grads0, grads1, loss_tile):
        g0 = pack_grads(grads0)
        from_sibling = _run_exchange(_pair_window_exchange(g0), name="pair_reduce_0")[0]
        packed[0] = _pair_add(g0, from_sibling, ci, name="pair_add_0")
        gsmall = _pack([jnp.stack([g[nm] for g in (grads0, grads1)]) for nm in SMALL] + [loss_tile[0, 0:1]], D_MODEL)
        return _chip_scatter_exchange(packed[0], gsmall)

    _, dx, grads, landed1, (landed0, landed_small) = _local_step(
        x[0], loss_target[0], args0, args1, ln_g, ln_b, gathers=gathers, reduce1=reduce1, reduce0=reduce0)

    small_shapes = [(DEPTH,) + grads[0][nm].shape for nm in SMALL]
    halves = [_sum_chips(packed[0], landed0, me, tc=2 * LANE, name="reduce_sum_0"),
              _sum_scatter(packed[1], landed1[0], me, ci, tc=2 * LANE, name="reduce_sum_1")]
    s_small = _sum_slots(landed_small, name="reduce_sum_small")
    others = _run_exchange(_share_exchange(halves), name="pair_share")
    full = [jnp.where(ci == 0, jnp.concatenate([mine, other], axis=1), jnp.concatenate([other, mine], axis=1))
            for mine, other in zip(halves, others)]
    grad_in_layers = [f[:IN_SHARD] for f in full]
    grad_out = jnp.stack([f[IN_PAD:] for f in full])
    out_blk = (1, OUT_SHARD, D_MODEL)
    *small_grads, loss = _unpack(s_small, small_shapes + [()])
    gs = dict(zip(SMALL, small_grads))
    gs["conv_w"] = lax.dynamic_slice_in_dim(gs["conv_w"], me * CONV_SHARD, CONV_SHARD, axis=2)

    grad_in_t = _interleave_layers(grad_in_layers, tc=2 * LANE, name="grad_in_layers")
    d_in, nm_in, nv_in = (from_t(o) for o in _adamw(to_t(w_in), grad_in_t, to_t(m_w_in), to_t(v_w_in),
                                                    (IN_SHARD // 6, DEPTH, D_MODEL), name="adamw_in"))
    grad_in = from_t(grad_in_t)
    d_out, nm_out, nv_out = _adamw(w_out, grad_out, m_w_out, v_w_out, out_blk, name="adamw_out")
    ws = dict(conv_w=conv_w, a_log=a_log, dt_bias=dt_bias, norm_w=norm_w, sinks=sinks, ln_g=ln_g, ln_b=ln_b)
    ms = dict(conv_w=m_conv_w, a_log=m_a_log, dt_bias=m_dt_bias, norm_w=m_norm_w, sinks=m_sinks, ln_g=m_ln_g, ln_b=m_ln_b)
    vs = dict(conv_w=v_conv_w, a_log=v_a_log, dt_bias=v_dt_bias, norm_w=v_norm_w, sinks=v_sinks, ln_g=v_ln_g, ln_b=v_ln_b)
    d_s, nm_s, nv_s = (dict(zip(SMALL, o)) for o in _adamw_small(*[[d[nm] for nm in SMALL] for d in (ws, gs, ms, vs)],
                                                                 name="adamw_small"))

    def in_order(big_in, small, big_out):
        return (big_in, small["conv_w"], small["a_log"], small["dt_bias"], small["norm_w"], small["sinks"], big_out,
                small["ln_g"], small["ln_b"])

    return (loss, dx[None], *in_order(grad_in, gs, grad_out), *in_order(d_in, d_s, d_out),
            *in_order(nm_in, nm_s, nm_out), *in_order(nv_in, nv_s, nv_out))
```

```python
import math

import jax
import jax.numpy as jnp
from jax import lax
from jax.experimental import pallas as pl
from jax.experimental.pallas import tpu as pltpu

F32 = jnp.float32
BF16 = jnp.bfloat16
HI = lax.Precision.HIGHEST

D_MODEL = 1024
DEPTH = 2
A_HEADS = 4
A_HEAD_DIM = 128
A_WIDTH = 512
CONV_K = 4
CHUNK = 64
B_Q_HEADS = 8
B_KV_HEADS = 2
B_HEAD_DIM = 64
B_GROUP = 4
B_WIDTH = 512
B_KV_WIDTH = 128
BLOCK = 128
IN_COLS = 3336
DEEPNORM_ALPHA = (2 * DEPTH) ** 0.25
LN_EPS = 1e-5
RMS_EPS = 1e-6
L2_EPS = 1e-6
ADAM_LR = 0.001
ADAM_B1 = 0.9
ADAM_B2 = 0.999
ADAM_EPS = 1e-08
ADAM_WD = 0.01
ADAM_STEP = 10

N_SHARD = 4
IN_SHARD = IN_COLS // N_SHARD
OUT_SHARD = D_MODEL // N_SHARD
CONV_SHARD = 3 * A_WIDTH // N_SHARD
IN_PAD = -(-IN_SHARD // 96) * 96

P_COLS = 3456
C_PRE = 0
C_ZA = 1536
C_QB = 2048
C_ZB = 2560
C_KB = 3072
C_VB = 3200
C_BG = 3328
DH_MAIN = C_KB
LANE = 128
SUBLANE = 8
HALO = 16
VMEM_LIMIT = 56 * 1024 * 1024
ALIBI = tuple(2.0 ** (-8.0 * (h + 1) / B_Q_HEADS) for h in range(B_Q_HEADS))
NEG = -1e30


def _cp(*sem):
    return pltpu.CompilerParams(dimension_semantics=sem, vmem_limit_bytes=VMEM_LIMIT)


def _dot(a, b):
    return jnp.dot(a.astype(BF16), b.astype(BF16), preferred_element_type=F32)


def _dot_nt(a, b):
    return lax.dot_general(a.astype(BF16), b.astype(BF16), (((1,), (1,)), ((), ())),
                           preferred_element_type=F32)


def _dot_tn(a, b):
    return lax.dot_general(a.astype(BF16), b.astype(BF16), (((0,), (0,)), ((), ())),
                           preferred_element_type=F32)


def _dot_hi(a, b):
    return jnp.dot(a, b, precision=HI, preferred_element_type=F32)


def _sigmoid(x):
    return jax.nn.sigmoid(x)


def _silu(x):
    return x * _sigmoid(x)


def _silu_and_grad(x):
    s = _sigmoid(x)
    return x * s, s * (1.0 + x * (1.0 - s))


def _softplus(x):
    return jnp.maximum(x, 0.0) + jnp.log(1.0 + jnp.exp(-jnp.abs(x)))


def _shift_down(cur, before, s):
    if s == 0:
        return cur
    r = pltpu.roll(cur, s, 0)
    rb = pltpu.roll(before, s, 0)
    row = lax.broadcasted_iota(jnp.int32, before.shape, 0)
    head = jnp.where(row < s, rb, r[0:SUBLANE])
    return jnp.concatenate([head, r[SUBLANE:]], axis=0)


def _shift_up(cur, after, s):
    if s == 0:
        return cur
    n = cur.shape[0]
    r = pltpu.roll(cur, n - s, 0)
    ra = pltpu.roll(after, SUBLANE - s, 0)
    row = lax.broadcasted_iota(jnp.int32, after.shape, 0)
    tail = jnp.where(row >= SUBLANE - s, ra, r[n - SUBLANE:])
    return jnp.concatenate([r[:n - SUBLANE], tail], axis=0)


def _conv_fwd(cur, before, w):
    acc = cur * w[CONV_K - 1:CONV_K, :]
    for s in range(1, CONV_K):
        acc = acc + _shift_down(cur, before, s) * w[CONV_K - 1 - s:CONV_K - s, :]
    return acc


def _matmul_nt(a, bt, *, tm, name, carry=None):
    m, k = a.shape
    n = bt.shape[0]
    c_ins, c_in_specs, c_out_specs, c_outs, c_scratch = _carry_specs(carry)

    def body(*refs):
        a_ref, b_ref, o_ref = _carried(carry, refs, 2, 1, m // tm)
        o_ref[...] = _dot_nt(a_ref[...], b_ref[...]).astype(o_ref.dtype)

    outs = pl.pallas_call(
        body, name=name, grid=(m // tm,),
        in_specs=[pl.BlockSpec((tm, k), lambda i: (i, 0)), pl.BlockSpec((n, k), lambda i: (0, 0))] + c_in_specs,
        out_specs=[pl.BlockSpec((tm, n), lambda i: (i, 0))] + c_out_specs,
        out_shape=[jax.ShapeDtypeStruct((m, n), BF16)] + c_outs,
        scratch_shapes=c_scratch,
        compiler_params=_cp("arbitrary"))(a, bt, *c_ins)
    return outs[0], outs[1:]


def _dn_pre(h, conv_w, par, *, tt, name, carry=None):
    t = h.shape[0]
    cw = 3 * A_WIDTH
    hb = tt // HALO

    c_ins, c_in_specs, c_out_specs, c_outs, c_scratch = _carry_specs(carry)

    def body(*refs):
        (pre_ref, halo_ref, bgi_ref, cw_ref, par_ref,
         q_ref, k_ref, v_ref, bg_ref, bgt_ref) = _carried(carry, refs, 5, 5, t // tt)
        i = pl.program_id(0)
        cur = pre_ref[...].astype(F32)
        before = jnp.where(i > 0, halo_ref[...].astype(F32)[HALO - SUBLANE:], 0.0)
        s = _silu(_conv_fwd(cur, before, cw_ref[...]))
        for hd in range(A_HEADS):
            sl = slice(hd * LANE, (hd + 1) * LANE)
            tq = s[:, hd * LANE:(hd + 1) * LANE]
            q_ref[:, sl] = tq * (lax.rsqrt(jnp.sum(tq * tq, -1, keepdims=True) + L2_EPS) * (A_HEAD_DIM ** -0.5))
            tk = s[:, A_WIDTH + hd * LANE:A_WIDTH + (hd + 1) * LANE]
            k_ref[:, sl] = tk * lax.rsqrt(jnp.sum(tk * tk, -1, keepdims=True) + L2_EPS)
        v_ref[...] = s[:, 2 * A_WIDTH:]
        raw = bgi_ref[...].astype(F32)
        lane = lax.broadcasted_iota(jnp.int32, raw.shape, 1)
        is_a = (lane >= A_HEADS) & (lane < 2 * A_HEADS)
        g = jnp.where(is_a, -jnp.exp(par_ref[0:1, :]) * _softplus(raw + par_ref[1:2, :]), 0.0)
        gc = _dot_hi(_chunk_tri(tt, lower=True), g)
        bg = jnp.where(lane < A_HEADS, _sigmoid(raw), gc)
        bg_ref[...] = bg
        bgt_ref[...] = jnp.transpose(bg)[0:SUBLANE, :]

    wide = jax.ShapeDtypeStruct((t, A_WIDTH), F32)
    outs = pl.pallas_call(
        body, name=name, grid=(t // tt,),
        in_specs=[pl.BlockSpec((tt, cw), lambda i: (i, 0)),
                  pl.BlockSpec((HALO, cw), lambda i: (jnp.maximum(i * hb - 1, 0), 0)),
                  pl.BlockSpec((tt, LANE), lambda i: (i, C_BG // LANE)),
                  pl.BlockSpec((CONV_K, cw), lambda i: (0, 0)),
                  pl.BlockSpec((SUBLANE, LANE), lambda i: (0, 0))] + c_in_specs,
        out_specs=[pl.BlockSpec((tt, A_WIDTH), lambda i: (i, 0))] * 3
        + [pl.BlockSpec((tt, LANE), lambda i: (i, 0)), pl.BlockSpec((SUBLANE, tt), lambda i: (0, i))] + c_out_specs,
        out_shape=[wide, wide, wide, jax.ShapeDtypeStruct((t, LANE), F32),
                   jax.ShapeDtypeStruct((SUBLANE, t), F32)] + c_outs,
        scratch_shapes=c_scratch,
        compiler_params=_cp("arbitrary"))(h, h, h, conv_w, par, *c_ins)
    return outs[:5], outs[5:]


def _chunk_tri(n, lower):
    r = lax.broadcasted_iota(jnp.int32, (n, n), 0)
    c = lax.broadcasted_iota(jnp.int32, (n, n), 1)
    shift = CHUNK.bit_length() - 1
    same = jnp.right_shift(r, shift) == jnp.right_shift(c, shift)
    return (same & ((c <= r) if lower else (c >= r))).astype(F32)


def _chunk_masks():
    r = lax.broadcasted_iota(jnp.int32, (CHUNK, CHUNK), 0)
    c = lax.broadcasted_iota(jnp.int32, (CHUNK, CHUNK), 1)
    return r >= c, r > c, r == c


def _split(a):
    hi = a.astype(BF16)
    return hi, (a - hi.astype(F32)).astype(BF16)


def _dot3(a, b):
    (ah, al), (bh, bl) = a, b
    d = lambda p, q: jnp.dot(p, q, preferred_element_type=F32)
    return d(ah, bh) + (d(ah, bl) + d(al, bh))


def _tri_inv_many(a_list, eye):
    d = lambda p, q: jnp.dot(p, q, preferred_element_type=F32)
    p = [(-a).astype(BF16) for a in a_list]
    tm = [eye - a for a in a_list]
    for _ in range(5):
        pf = [d(pi, pi) for pi in p]
        p = [x.astype(BF16) for x in pf]
        tm = [t + d(t.astype(BF16), pi) for t, pi in zip(tm, p)]
    ms = [_split(eye + a) for a in a_list]
    res = [eye - _dot3(m, _split(t)) for m, t in zip(ms, tm)]
    return [t + d(t.astype(BF16), r.astype(BF16)) for t, r in zip(tm, res)]


def _chunk_gates(bg_v, bgt_v, hd):
    return (bg_v[:, hd:hd + 1], bg_v[:, A_HEADS + hd:A_HEADS + hd + 1],
            None if bgt_v is None else bgt_v[A_HEADS + hd:A_HEADS + hd + 1, :])


WY_ROWS = 512
SCAN_ROWS = 512
WY_GROUP = 8


def _dn_wy(q, k, v, bg, bgt, *, name, carry=None):
    t = q.shape[0]
    rows = WY_ROWS

    c_ins, c_in_specs, c_out_specs, c_outs, c_scratch = _carry_specs(carry)

    def body(*refs):
        q_ref, k_ref, v_ref, bg_ref, bgt_ref, u_ref, w_ref, tm_ref, qk_ref = _carried(carry, refs, 5, 4, t // rows)
        causal, strict, diag = _chunk_masks()
        eye = diag.astype(F32)
        for c0 in range(0, rows // CHUNK, WY_GROUP):
            items = [(c, hd) for c in range(c0, c0 + WY_GROUP) for hd in range(A_HEADS)]
            rs = lambda c: slice(c * CHUNK, (c + 1) * CHUNK)
            sl = lambda hd: slice(hd * LANE, (hd + 1) * LANE)
            hs = lambda hd: slice(hd * CHUNK, (hd + 1) * CHUNK)
            gates = [_chunk_gates(bg_ref[rs(c), :], bgt_ref[:, rs(c)], hd) for c, hd in items]
            dms = [jnp.exp(jnp.where(causal, gcol - grow, NEG)) for _, gcol, grow in gates]
            kbs = [k_ref[rs(c), sl(hd)] * g[0] for (c, hd), g in zip(items, gates)]
            a_list = [jnp.where(strict, _dot_nt(kb, k_ref[rs(c), sl(hd)]) * dm, 0.0)
                      for (c, hd), kb, dm in zip(items, kbs, dms)]
            for (c, hd), dm in zip(items, dms):
                qk_ref[rs(c), hs(hd)] = jnp.where(
                    causal, _dot_nt(q_ref[rs(c), sl(hd)], k_ref[rs(c), sl(hd)]) * dm, 0.0)
            tms = _tri_inv_many(a_list, eye)
            for (c, hd), g, kb, tmat in zip(items, gates, kbs, tms):
                tm_ref[rs(c), hs(hd)] = tmat
                u_ref[rs(c), sl(hd)] = _dot(tmat, v_ref[rs(c), sl(hd)] * g[0])
                w_ref[rs(c), sl(hd)] = _dot(tmat, kb * jnp.exp(g[1])).astype(BF16)

    blk = pl.BlockSpec((rows, A_WIDTH), lambda i: (i, 0))
    half = pl.BlockSpec((rows, A_HEADS * CHUNK), lambda i: (i, 0))
    outs = pl.pallas_call(
        body, name=name, grid=(t // rows,),
        in_specs=[blk, blk, blk, pl.BlockSpec((rows, LANE), lambda i: (i, 0)),
                  pl.BlockSpec((SUBLANE, rows), lambda i: (0, i))] + c_in_specs,
        out_specs=[blk, blk, half, half] + c_out_specs,
        out_shape=[jax.ShapeDtypeStruct((t, A_WIDTH), F32), jax.ShapeDtypeStruct((t, A_WIDTH), BF16),
                   jax.ShapeDtypeStruct((t, A_HEADS * CHUNK), F32),
                   jax.ShapeDtypeStruct((t, A_HEADS * CHUNK), F32)] + c_outs,
        scratch_shapes=c_scratch,
        compiler_params=_cp("arbitrary"))(q, k, v, bg, bgt, *c_ins)
    return outs[:4], outs[4:]


def _dn_scan_fwd(q, k, u, w, qk, bg, *, name, carry=None):
    t = q.shape[0]
    rows = SCAN_ROWS
    per = rows // CHUNK
    c_ins, c_in_specs, c_out_specs, c_outs, c_scratch = _carry_specs(carry)

    def body(*refs):
        q_ref, k_ref, u_ref, w_ref, qk_ref, bg_ref, o_ref, vn_ref, s_ref, state = _carried(carry, refs, 6, 3, t // rows)

        @pl.when(pl.program_id(0) == 0)
        def _():
            state[...] = jnp.zeros_like(state)

        heads = range(A_HEADS)
        sl = lambda hd: slice(hd * LANE, (hd + 1) * LANE)
        s_cur = [state[hd] for hd in heads]
        for c in range(per):
            rs = slice(c * CHUNK, (c + 1) * CHUNK)
            bg_v = bg_ref[rs, :]
            gcols = [_chunk_gates(bg_v, None, hd)[1] for hd in heads]
            glasts = [gc[CHUNK - 1:CHUNK, :] for gc in gcols]
            for hd in heads:
                s_ref[c, hd] = s_cur[hd].astype(BF16)
            vns = [u_ref[rs, sl(hd)] - _dot(w_ref[rs, sl(hd)], s_cur[hd]) for hd in heads]
            qss = [_dot(q_ref[rs, sl(hd)] * jnp.exp(gcols[hd]), s_cur[hd]) for hd in heads]
            s_cur = [s_cur[hd] * jnp.exp(glasts[hd])
                     + _dot_tn(k_ref[rs, sl(hd)] * jnp.exp(glasts[hd] - gcols[hd]), vns[hd]) for hd in heads]
            for hd in heads:
                vn_ref[rs, sl(hd)] = vns[hd]
                o_ref[rs, sl(hd)] = qss[hd] + _dot(qk_ref[rs, hd * CHUNK:(hd + 1) * CHUNK], vns[hd])
        for hd in heads:
            state[hd] = s_cur[hd]

    blk = pl.BlockSpec((rows, A_WIDTH), lambda i: (i, 0))
    half = pl.BlockSpec((rows, A_HEADS * CHUNK), lambda i: (i, 0))
    wide = jax.ShapeDtypeStruct((t, A_WIDTH), F32)
    outs = pl.pallas_call(
        body, name=name, grid=(t // rows,),
        in_specs=[blk, blk, blk, blk, half, pl.BlockSpec((rows, LANE), lambda i: (i, 0))] + c_in_specs,
        out_specs=[blk, blk, pl.BlockSpec((per, A_HEADS, LANE, LANE), lambda i: (i, 0, 0, 0))] + c_out_specs,
        out_shape=[wide, wide, jax.ShapeDtypeStruct((t // CHUNK, A_HEADS, LANE, LANE), BF16)] + c_outs,
        scratch_shapes=[pltpu.VMEM((A_HEADS, LANE, LANE), F32)] + c_scratch,
        compiler_params=_cp("arbitrary"))(q, k, u, w, qk, bg, *c_ins)
    return outs[:3], outs[3:]


def _stack_heads(ref, hk):
    return jnp.concatenate([ref[:, h * B_HEAD_DIM:(h + 1) * B_HEAD_DIM].astype(F32)
                            for h in range(hk * B_GROUP, (hk + 1) * B_GROUP)], axis=0)


def _swa_window():
    qi = lax.broadcasted_iota(jnp.int32, (BLOCK, BLOCK), 0)
    kj = lax.broadcasted_iota(jnp.int32, (BLOCK, BLOCK), 1)
    dist = jnp.where(kj > qi, qi + BLOCK - kj, qi - kj).astype(F32)
    rows = lax.broadcasted_iota(jnp.int32, (B_GROUP * BLOCK, BLOCK), 0)
    cols = lax.broadcasted_iota(jnp.int32, (B_GROUP * BLOCK, BLOCK), 1)
    return cols > jnp.bitwise_and(rows, BLOCK - 1), dist


def _swa_group_probs(q_ref, sk_ref, kp, kc, vp, vc, n_blk):
    hks = range(B_KV_HEADS)
    heads = lambda hk: range(hk * B_GROUP, (hk + 1) * B_GROUP)
    ksl = lambda hk: slice(hk * B_HEAD_DIM, (hk + 1) * B_HEAD_DIM)
    upper, dist = _swa_window()
    no_prev = jnp.where(n_blk > 0, 0.0, NEG)
    ones = jnp.ones((BLOCK, B_HEAD_DIM), BF16)
    with_ones = lambda v, hk: jnp.concatenate([v[:, ksl(hk)].astype(BF16), ones], axis=1)
    qs = [_stack_heads(q_ref, hk) * (B_HEAD_DIM ** -0.5) for hk in hks]
    sink = [jnp.concatenate([jnp.broadcast_to(sk_ref[h:h + 1, 0:1], (BLOCK, 1)) for h in heads(hk)], axis=0)
            for hk in hks]
    s = [jnp.where(upper, _dot_nt(qs[hk], kp[:, ksl(hk)]) + no_prev, _dot_nt(qs[hk], kc[:, ksl(hk)]))
         - jnp.concatenate([ALIBI[h] * dist for h in heads(hk)], axis=0) for hk in hks]
    m = [jnp.maximum(jnp.max(s[hk], axis=-1, keepdims=True), sink[hk]) for hk in hks]
    p = [jnp.exp(s[hk] - m[hk]) for hk in hks]
    p_up = [jnp.where(upper, p[hk], 0.0) for hk in hks]
    oe = [jnp.dot(p_up[hk].astype(BF16), with_ones(vp, hk), preferred_element_type=F32)
          + jnp.dot((p[hk] - p_up[hk]).astype(BF16), with_ones(vc, hk), preferred_element_type=F32) for hk in hks]
    ps = [jnp.exp(sink[hk] - m[hk]) for hk in hks]
    inv = [1.0 / (oe[hk][:, B_HEAD_DIM:B_HEAD_DIM + 1] + ps[hk]) for hk in hks]
    return upper, [(qs[hk], p[hk] * inv[hk], ps[hk] * inv[hk], oe[hk][:, :B_HEAD_DIM] * inv[hk]) for hk in hks]


def _swa_specs():
    qspec = lambda c0: pl.BlockSpec((BLOCK, B_WIDTH), lambda i: (i, c0 // B_WIDTH))
    cur = lambda c0: pl.BlockSpec((BLOCK, LANE), lambda i: (i, c0 // LANE))
    prev = lambda c0: pl.BlockSpec((BLOCK, LANE), lambda i: (jnp.maximum(i - 1, 0), c0 // LANE))
    return qspec, cur, prev


def _carried(carry, refs, n_in, n_out, steps):
    if carry is None:
        return refs
    ci, co = len(carry.ins), len(carry.outs)
    own = refs[:n_in] + refs[n_in + ci:n_in + ci + n_out] + refs[n_in + ci + n_out + co:len(refs) - 3]
    parts = refs[n_in:n_in + ci], refs[n_in + ci + n_out:n_in + ci + n_out + co], refs[len(refs) - 3:]

    @pl.when(pl.program_id(0) == 0)
    def _():
        carry.start(*parts)

    @pl.when(pl.program_id(0) == steps - 1)
    def _():
        carry.finish(*parts)

    return own


def _carry_specs(carry):
    if carry is None:
        return [], [], [], [], []
    return (list(carry.ins), [_ANY] * len(carry.ins), [_ANY] * len(carry.outs), list(carry.outs), carry.scratch())


def _swa_fwd(h, sinks_b, *, name, carry=None):
    t = h.shape[0]
    qspec, cur, prev = _swa_specs()
    c_ins, c_in_specs, c_out_specs, c_outs, c_scratch = _carry_specs(carry)

    def body(*refs):
        q_ref, kc_ref, kp_ref, vc_ref, vp_ref, sk_ref, o_ref = _carried(carry, refs, 6, 1, t // BLOCK)
        n_blk = pl.program_id(0)
        _, groups = _swa_group_probs(q_ref, sk_ref, kp_ref[...], kc_ref[...], vp_ref[...], vc_ref[...], n_blk)
        for hk, (_, _, _, o) in enumerate(groups):
            for g in range(B_GROUP):
                hq = hk * B_GROUP + g
                o_ref[:, hq * B_HEAD_DIM:(hq + 1) * B_HEAD_DIM] = o[g * BLOCK:(g + 1) * BLOCK]

    outs = pl.pallas_call(
        body, name=name, grid=(t // BLOCK,),
        in_specs=[qspec(C_QB), cur(C_KB), prev(C_KB), cur(C_VB), prev(C_VB),
                  pl.BlockSpec((B_Q_HEADS, LANE), lambda i: (0, 0))] + c_in_specs,
        out_specs=[pl.BlockSpec((BLOCK, B_WIDTH), lambda i: (i, 0))] + c_out_specs,
        out_shape=[jax.ShapeDtypeStruct((t, B_WIDTH), F32)] + c_outs,
        scratch_shapes=c_scratch,
        compiler_params=_cp("arbitrary"))(h, h, h, h, h, sinks_b, *c_ins)
    return outs[0], outs[1:]


def _rms_gate(o, za, nw):
    outs = []
    for hd in range(A_HEADS):
        oh = o[:, hd * LANE:(hd + 1) * LANE]
        r = lax.rsqrt(jnp.mean(oh * oh, -1, keepdims=True) + RMS_EPS)
        outs.append(oh * r * nw)
    return jnp.concatenate(outs, axis=1) * _silu(za)


def _out_ln(x, oa, ob, h, norm_w, w_out, ln_g, ln_b, *, tm, name, target=None):
    t = x.shape[0]
    last = target is not None

    def body(*refs):
        x_ref, oa_ref, ob_ref, za_ref, zb_ref, nw_ref, w_ref, g_ref, b_ref = refs[:9]
        xn_ref, mx_ref, r_ref = refs[9 + last:12 + last]
        ya = _rms_gate(oa_ref[...], za_ref[...].astype(F32), nw_ref[...])
        yb = ob_ref[...] * _silu(zb_ref[...].astype(F32))
        mixed = jnp.concatenate([ya, yb], axis=1).astype(BF16)
        mx_ref[...] = mixed
        r = DEEPNORM_ALPHA * x_ref[...] + jnp.dot(mixed, w_ref[...], preferred_element_type=F32)
        r_ref[...] = r
        mu = jnp.mean(r, -1, keepdims=True)
        xc = r - mu
        var = jnp.mean(xc * xc, -1, keepdims=True)
        xn = xc * lax.rsqrt(var + LN_EPS) * g_ref[...] + b_ref[...]
        if not last:
            xn_ref[...] = xn
            return
        loss_ref = refs[13]

        @pl.when(pl.program_id(0) == 0)
        def _():
            loss_ref[...] = jnp.zeros_like(loss_ref)

        err = xn - refs[9][...]
        xn_ref[...] = err * (1.0 / D_MODEL)
        loss_ref[...] += 0.5 / D_MODEL * jnp.sum(err * err)

    row = lambda w, c: pl.BlockSpec((tm, w), lambda i: (i, c))
    full = lambda a, b: pl.BlockSpec((a, b), lambda i: (0, 0))
    wide = jax.ShapeDtypeStruct((t, D_MODEL), F32)
    return pl.pallas_call(
        body, name=name, grid=(t // tm,),
        in_specs=[row(D_MODEL, 0), row(A_WIDTH, 0), row(B_WIDTH, 0), row(A_WIDTH, C_ZA // A_WIDTH),
                  row(B_WIDTH, C_ZB // B_WIDTH), full(1, LANE), full(D_MODEL, D_MODEL), full(1, D_MODEL),
                  full(1, D_MODEL)] + [row(D_MODEL, 0)] * last,
        out_specs=[row(D_MODEL, 0), row(D_MODEL, 0), row(D_MODEL, 0)] + [full(SUBLANE, LANE)] * last,
        out_shape=[wide, jax.ShapeDtypeStruct((t, D_MODEL), BF16), wide]
        + [jax.ShapeDtypeStruct((SUBLANE, LANE), F32)] * last,
        compiler_params=_cp("arbitrary" if last else "parallel"))(
        x, oa, ob, h, h, norm_w, w_out, ln_g, ln_b, *([target] if last else []))


def _layer_fwd(x, wt, conv_w, par, sinks_b, norm_w, w_out_bf, ln_g, ln_b, l, carries=None, target=None):
    carries = carries or {}
    h, got_in = _matmul_nt(x, wt, tm=512, name=f"in_proj_{l}", carry=carries.get("in_proj"))
    if callable(w_out_bf):
        w_out_bf = w_out_bf(got_in)
    (q, k, v, bg, bgt), got_pre = _dn_pre(h, conv_w, par, tt=512, name=f"dn_pre_{l}", carry=carries.get("dn_pre"))
    (u, w, tmat, qk), got_wy = _dn_wy(q, k, v, bg, bgt, name=f"dn_wy_{l}", carry=carries.get("dn_wy"))
    (oa, vn, s_all), got_scan = _dn_scan_fwd(q, k, u, w, qk, bg, name=f"dn_scan_{l}", carry=carries.get("dn_scan"))
    ob, got_swa = _swa_fwd(h, sinks_b, name=f"swa_fwd_{l}", carry=carries.get("swa"))
    xn, mixed, r, *loss = _out_ln(x, oa, ob, h, norm_w, w_out_bf, ln_g, ln_b, tm=512, name=f"out_ln_{l}", target=target)
    if loss:
        xn = (xn, loss[0])
    res = dict(x=x, h=h, q=q, k=k, v=v, bg=bg, bgt=bgt, w=w, tmat=tmat, qk=qk, vn=vn, oa=oa, s_all=s_all,
               mixed=mixed, r=r, w_out=w_out_bf)
    return xn, res, dict(in_proj=got_in, dn_pre=got_pre, dn_wy=got_wy, dn_scan=got_scan, swa=got_swa)


def _ln_out_bwd(dxn, r, mixed, ln_g, w_out, *, tm, name):
    t = dxn.shape[0]

    def body(dxn_ref, r_ref, mx_ref, g_ref, w_ref, dr_ref, dm_ref, dw_ref, dg_ref, db_ref):
        @pl.when(pl.program_id(0) == 0)
        def _():
            dw_ref[...] = jnp.zeros_like(dw_ref)
            dg_ref[...] = jnp.zeros_like(dg_ref)
            db_ref[...] = jnp.zeros_like(db_ref)

        rr = r_ref[...]
        xc = rr - jnp.mean(rr, -1, keepdims=True)
        rstd = lax.rsqrt(jnp.mean(xc * xc, -1, keepdims=True) + LN_EPS)
        xhat = xc * rstd
        dxn_v = dxn_ref[...]
        dxh = dxn_v * g_ref[...]
        dr = rstd * (dxh - jnp.mean(dxh, -1, keepdims=True) - xhat * jnp.mean(dxh * xhat, -1, keepdims=True))
        dr_ref[...] = dr
        dg_ref[...] += jnp.sum(dxn_v * xhat, axis=0, keepdims=True)
        db_ref[...] += jnp.sum(dxn_v, axis=0, keepdims=True)
        drb = dr.astype(BF16)
        dm_ref[...] = _dot_nt(drb, w_ref[...])
        dw_ref[...] += _dot_tn(mx_ref[...], drb)

    row = pl.BlockSpec((tm, D_MODEL), lambda i: (i, 0))
    full = lambda a, b: pl.BlockSpec((a, b), lambda i: (0, 0))
    big = jax.ShapeDtypeStruct((t, D_MODEL), F32)
    vec = jax.ShapeDtypeStruct((1, D_MODEL), F32)
    return pl.pallas_call(
        body, name=name, grid=(t // tm,),
        in_specs=[row, row, row, full(1, D_MODEL), full(D_MODEL, D_MODEL)],
        out_specs=[row, row, full(D_MODEL, D_MODEL), full(1, D_MODEL), full(1, D_MODEL)],
        out_shape=[big, big, jax.ShapeDtypeStruct((D_MODEL, D_MODEL), F32), vec, vec],
        compiler_params=_cp("arbitrary"))(dxn, r, mixed, ln_g, w_out)


def _dn_post_bwd(dm, oa, h, norm_w, *, tm, name):
    t = oa.shape[0]

    def body(dy_ref, o_ref, za_ref, nw_ref, do_ref, dza_ref, dnw_ref):
        @pl.when(pl.program_id(0) == 0)
        def _():
            dnw_ref[...] = jnp.zeros_like(dnw_ref)

        nw = nw_ref[...]
        dnw = jnp.zeros_like(nw)
        for hd in range(A_HEADS):
            sl = slice(hd * LANE, (hd + 1) * LANE)
            oh, za, dy = o_ref[:, sl], za_ref[:, sl].astype(F32), dy_ref[:, sl]
            rs = lax.rsqrt(jnp.mean(oh * oh, -1, keepdims=True) + RMS_EPS)
            nrm = oh * rs
            gate, dgate = _silu_and_grad(za)
            dza_ref[:, sl] = dy * nrm * nw * dgate
            dn = dy * gate
            dnw = dnw + jnp.sum(dn * nrm, axis=0, keepdims=True)
            dnn = dn * nw
            do_ref[:, sl] = rs * dnn - oh * (rs * rs * rs) * jnp.mean(dnn * oh, -1, keepdims=True)
        dnw_ref[...] += dnw

    row = lambda c: pl.BlockSpec((tm, A_WIDTH), lambda i: (i, c))
    wide = jax.ShapeDtypeStruct((t, A_WIDTH), F32)
    return pl.pallas_call(
        body, name=name, grid=(t // tm,),
        in_specs=[row(0), row(0), row(C_ZA // A_WIDTH), pl.BlockSpec((1, LANE), lambda i: (0, 0))],
        out_specs=[row(0), row(C_ZA // A_WIDTH), pl.BlockSpec((1, LANE), lambda i: (0, 0))],
        out_shape=[wide, jax.ShapeDtypeStruct((t, DH_MAIN), F32), jax.ShapeDtypeStruct((1, LANE), F32)],
        compiler_params=_cp("arbitrary"))(dm, oa, h, norm_w)


def _dn_scan_bwd(q, k, w, qk, bg, do, *, name):
    t = q.shape[0]
    rows = SCAN_ROWS
    per = rows // CHUNK
    n = t // rows

    def body(q_ref, k_ref, w_ref, qk_ref, bg_ref, do_ref, dvn_ref, ds_ref, dstate):
        @pl.when(pl.program_id(0) == 0)
        def _():
            dstate[...] = jnp.zeros_like(dstate)

        heads = range(A_HEADS)
        sl = lambda hd: slice(hd * LANE, (hd + 1) * LANE)
        ds_cur = [dstate[hd] for hd in heads]
        for c in reversed(range(per)):
            rs = slice(c * CHUNK, (c + 1) * CHUNK)
            bg_v = bg_ref[rs, :]
            gcols = [_chunk_gates(bg_v, None, hd)[1] for hd in heads]
            glasts = [gc[CHUNK - 1:CHUNK, :] for gc in gcols]
            for hd in heads:
                ds_ref[c, hd] = ds_cur[hd].astype(BF16)
            pdo = [_dot_tn(qk_ref[rs, hd * CHUNK:(hd + 1) * CHUNK], do_ref[rs, sl(hd)]) for hd in heads]
            qdo = [_dot_tn(q_ref[rs, sl(hd)] * jnp.exp(gcols[hd]), do_ref[rs, sl(hd)]) for hd in heads]
            dvns = [pdo[hd] + _dot(k_ref[rs, sl(hd)] * jnp.exp(glasts[hd] - gcols[hd]), ds_cur[hd]) for hd in heads]
            ds_cur = [qdo[hd] + jnp.exp(glasts[hd]) * ds_cur[hd] - _dot_tn(w_ref[rs, sl(hd)], dvns[hd])
                      for hd in heads]
            for hd in heads:
                dvn_ref[rs, sl(hd)] = dvns[hd]
        for hd in heads:
            dstate[hd] = ds_cur[hd]

    blk = pl.BlockSpec((rows, A_WIDTH), lambda i: (n - 1 - i, 0))
    return pl.pallas_call(
        body, name=name, grid=(n,),
        in_specs=[blk, blk, blk, pl.BlockSpec((rows, A_HEADS * CHUNK), lambda i: (n - 1 - i, 0)),
                  pl.BlockSpec((rows, LANE), lambda i: (n - 1 - i, 0)), blk],
        out_specs=[blk, pl.BlockSpec((per, A_HEADS, LANE, LANE), lambda i: (n - 1 - i, 0, 0, 0))],
        out_shape=[jax.ShapeDtypeStruct((t, A_WIDTH), F32),
                   jax.ShapeDtypeStruct((t // CHUNK, A_HEADS, LANE, LANE), BF16)],
        scratch_shapes=[pltpu.VMEM((A_HEADS, LANE, LANE), F32)],
        compiler_params=_cp("arbitrary"))(q, k, w, qk, bg, do)


def _dn_chunk_bwd(q, k, v, vn, tmat, qk, bg, bgt, s_all, ds_all, dvn, do, *, name, carry=None):
    t = q.shape[0]
    rows = WY_ROWS
    per = rows // CHUNK

    c_ins, c_in_specs, c_out_specs, c_outs, c_scratch = _carry_specs(carry)

    def body(*refs):
        (q_ref, k_ref, v_ref, vn_ref, tm_ref, qk_ref, bg_ref, bgt_ref, s_ref, ds_ref, dvn_ref, do_ref,
         dq_ref, dk_ref, dv_ref, dbg_ref, dbgt_ref) = _carried(carry, refs, 12, 5, t // rows)
        causal, strict, _ = _chunk_masks()
        lane = lax.broadcasted_iota(jnp.int32, (CHUNK, LANE), 1)
        rowi = lax.broadcasted_iota(jnp.int32, (CHUNK, 1), 0)
        sub = lax.broadcasted_iota(jnp.int32, (SUBLANE, CHUNK), 0)
        rs = lambda c: slice(c * CHUNK, (c + 1) * CHUNK)
        sl = lambda hd: slice(hd * LANE, (hd + 1) * LANE)
        hs = lambda hd: slice(hd * CHUNK, (hd + 1) * CHUNK)
        for c0 in range(0, per, WY_GROUP):
            items = [(c, hd) for c in range(c0, c0 + WY_GROUP) for hd in range(A_HEADS)]
            at = lambda ref: [ref[rs(c), sl(hd)] for c, hd in items]
            qs, ks, vs, dos, vns, dvns = at(q_ref), at(k_ref), at(v_ref), at(do_ref), at(vn_ref), at(dvn_ref)
            tmhs = [tm_ref[rs(c), hs(hd)] for c, hd in items]
            ps = [qk_ref[rs(c), hs(hd)] for c, hd in items]
            gates = [_chunk_gates(bg_ref[rs(c), :], bgt_ref[:, rs(c)], hd) for c, hd in items]
            betas = [g[0] for g in gates]
            gcols = [g[1] for g in gates]
            dmats = [jnp.exp(jnp.where(causal, g[1] - g[2], NEG)) for g in gates]
            es = [jnp.exp(gc) for gc in gcols]
            glasts = [gc[CHUNK - 1:CHUNK, :] for gc in gcols]
            eks = [jnp.exp(gl - gc) for gl, gc in zip(glasts, gcols)]
            kbs = [kh * b for kh, b in zip(ks, betas)]
            vbs = [vh * b for vh, b in zip(vs, betas)]
            kbes = [kb * e for kb, e in zip(kbs, es)]

            a_s = [jnp.where(strict, _dot_nt(kb, kh) * dm, 0.0) for kb, kh, dm in zip(kbs, ks, dmats)]
            dps = [jnp.where(causal, _dot_nt(doh, vnh), 0.0) for doh, vnh in zip(dos, vns)]
            dqds = [_dot_nt(doh, s_ref[c, hd]) for doh, (c, hd) in zip(dos, items)]
            dkds = [_dot_nt(vnh, ds_ref[c, hd]) for vnh, (c, hd) in zip(vns, items)]
            dws = [-_dot_nt(dvnh, s_ref[c, hd]) for dvnh, (c, hd) in zip(dvns, items)]
            dvbs = [_dot_tn(tmh, dvnh) for tmh, dvnh in zip(tmhs, dvns)]
            dgts = [jnp.sum(s_ref[c, hd].astype(F32) * ds_ref[c, hd].astype(F32), keepdims=True) for c, hd in items]
            dts = [_dot_nt(dvnh, vb) + _dot_nt(dw, kbe) for dvnh, vb, dw, kbe in zip(dvns, vbs, dws, kbes)]
            dkbes = [_dot_tn(tmh, dw) for tmh, dw in zip(tmhs, dws)]
            xs = [_dot_nt(dt, tmh) for dt, tmh in zip(dts, tmhs)]
            das = [jnp.where(strict, -_dot_tn(tmh, x), 0.0) for tmh, x in zip(tmhs, xs)]
            dmas = [da * dm for da, dm in zip(das, dmats)]
            dmps = [dp * dm for dp, dm in zip(dps, dmats)]
            dkbs = [_dot(dma, kh) + dkbe * e for dma, kh, dkbe, e in zip(dmas, ks, dkbes, es)]
            for i, (c, hd) in enumerate(items):
                dq_ref[rs(c), sl(hd)] = _dot(dmps[i], ks[i]) + dqds[i] * es[i]
                dk_ref[rs(c), sl(hd)] = (_dot_tn(dmas[i], kbs[i]) + _dot_tn(dmps[i], qs[i]) + dkds[i] * eks[i]
                                         + dkbs[i] * betas[i])
                dv_ref[rs(c), sl(hd)] = dvbs[i] * betas[i]
            for c in range(c0, c0 + WY_GROUP):
                acc = jnp.zeros((CHUNK, LANE), F32)
                acc_t = jnp.zeros((SUBLANE, CHUNK), F32)
                for i, (ci, hd) in enumerate(items):
                    if ci != c:
                        continue
                    gmat = das[i] * a_s[i] + dps[i] * ps[i]
                    rk = jnp.sum(dkds[i] * ks[i], -1, keepdims=True) * eks[i]
                    de = (jnp.sum(dqds[i] * qs[i], -1, keepdims=True)
                          + jnp.sum(dkbes[i] * kbs[i], -1, keepdims=True))
                    dglast = jnp.sum(rk, keepdims=True) + dgts[i] * jnp.exp(glasts[i])
                    dgc = (jnp.sum(gmat, -1, keepdims=True) + de * es[i] - rk
                           + jnp.where(rowi == CHUNK - 1, dglast, 0.0))
                    dbeta = (jnp.sum(dkbs[i] * ks[i], -1, keepdims=True)
                             + jnp.sum(dvbs[i] * vs[i], -1, keepdims=True))
                    acc = acc + jnp.where(lane == hd, dbeta, 0.0) + jnp.where(lane == A_HEADS + hd, dgc, 0.0)
                    acc_t = acc_t + jnp.where(sub == A_HEADS + hd, -jnp.sum(gmat, axis=0, keepdims=True), 0.0)
                dbg_ref[rs(c), :] = acc
                dbgt_ref[:, rs(c)] = acc_t

    blk = pl.BlockSpec((rows, A_WIDTH), lambda i: (i, 0))
    half = pl.BlockSpec((rows, A_HEADS * CHUNK), lambda i: (i, 0))
    col = pl.BlockSpec((rows, LANE), lambda i: (i, 0))
    rowf = pl.BlockSpec((SUBLANE, rows), lambda i: (0, i))
    st = pl.BlockSpec((per, A_HEADS, LANE, LANE), lambda i: (i, 0, 0, 0))
    wide = jax.ShapeDtypeStruct((t, A_WIDTH), F32)
    outs = pl.pallas_call(
        body, name=name, grid=(t // rows,),
        in_specs=[blk, blk, blk, blk, half, half, col, rowf, st, st, blk, blk] + c_in_specs,
        out_specs=[blk, blk, blk, col, rowf] + c_out_specs,
        out_shape=[wide, wide, wide, jax.ShapeDtypeStruct((t, LANE), F32),
                   jax.ShapeDtypeStruct((SUBLANE, t), F32)] + c_outs,
        scratch_shapes=c_scratch,
        compiler_params=_cp("arbitrary"))(q, k, v, vn, tmat, qk, bg, bgt, s_all, ds_all, dvn, do, *c_ins)
    return outs[:5], outs[5:]


def _dn_pre_bwd(h, conv_w, par, dq, dk, dv, dbg, dbgt, *, tt, name):
    t = h.shape[0]
    cw = 3 * A_WIDTH
    hb = tt // HALO

    def body(pre_ref, halo_ref, bgi_ref, cw_ref, par_ref, dq_ref, dk_ref, dv_ref, dbg_ref, dbgt_ref,
             dc_ref, dbgi_ref, dpar_ref):
        i = pl.program_id(0)

        @pl.when(i == 0)
        def _():
            dpar_ref[...] = jnp.zeros_like(dpar_ref)

        cur = pre_ref[...].astype(F32)
        before = jnp.where(i > 0, halo_ref[...].astype(F32)[HALO - SUBLANE:], 0.0)
        c = _conv_fwd(cur, before, cw_ref[...])
        s, ds = _silu_and_grad(c)
        for hd in range(A_HEADS):
            sl = slice(hd * LANE, (hd + 1) * LANE)
            for base, d_ref, scale in ((0, dq_ref, A_HEAD_DIM ** -0.5), (A_WIDTH, dk_ref, 1.0)):
                csl = slice(base + hd * LANE, base + (hd + 1) * LANE)
                tq = s[:, base + hd * LANE:base + (hd + 1) * LANE]
                dy = d_ref[:, sl]
                rq = lax.rsqrt(jnp.sum(tq * tq, -1, keepdims=True) + L2_EPS)
                dtq = scale * (rq * dy - tq * (rq * rq * rq) * jnp.sum(dy * tq, -1, keepdims=True))
                dc_ref[:, csl] = dtq * ds[:, base + hd * LANE:base + (hd + 1) * LANE]
        dc_ref[:, 2 * A_WIDTH:] = dv_ref[...] * ds[:, 2 * A_WIDTH:]
        raw = bgi_ref[...].astype(F32)
        lane = lax.broadcasted_iota(jnp.int32, raw.shape, 1)
        is_b = lane < A_HEADS
        is_a = (lane >= A_HEADS) & (lane < 2 * A_HEADS)
        rows_t = jnp.concatenate([dbgt_ref[...], jnp.zeros((LANE - SUBLANE, tt), F32)], axis=0)
        dbg_v = dbg_ref[...] + jnp.where(is_a, jnp.transpose(rows_t), 0.0)
        dbg_v = jnp.where(is_a, _dot_hi(_chunk_tri(tt, lower=False), jnp.where(is_a, dbg_v, 0.0)), dbg_v)
        beta = _sigmoid(raw)
        z = raw + par_ref[1:2, :]
        neg_ea = -jnp.exp(par_ref[0:1, :])
        g = neg_ea * _softplus(z)
        da = dbg_v * neg_ea * _sigmoid(z)
        dbgi_ref[...] = jnp.where(is_b, dbg_v * beta * (1.0 - beta), jnp.where(is_a, da, 0.0))
        dpar_ref[0:1, :] += jnp.sum(jnp.where(is_a, dbg_v * g, 0.0), axis=0, keepdims=True)
        dpar_ref[1:2, :] += jnp.sum(jnp.where(is_a, da, 0.0), axis=0, keepdims=True)

    wide = pl.BlockSpec((tt, A_WIDTH), lambda i: (i, 0))
    return pl.pallas_call(
        body, name=name, grid=(t // tt,),
        in_specs=[pl.BlockSpec((tt, cw), lambda i: (i, 0)),
                  pl.BlockSpec((HALO, cw), lambda i: (jnp.maximum(i * hb - 1, 0), 0)),
                  pl.BlockSpec((tt, LANE), lambda i: (i, C_BG // LANE)),
                  pl.BlockSpec((CONV_K, cw), lambda i: (0, 0)),
                  pl.BlockSpec((SUBLANE, LANE), lambda i: (0, 0)),
                  wide, wide, wide, pl.BlockSpec((tt, LANE), lambda i: (i, 0)),
                  pl.BlockSpec((SUBLANE, tt), lambda i: (0, i))],
        out_specs=[pl.BlockSpec((tt, cw), lambda i: (i, 0)), pl.BlockSpec((tt, LANE), lambda i: (i, 0)),
                   pl.BlockSpec((SUBLANE, LANE), lambda i: (0, 0))],
        out_shape=[jax.ShapeDtypeStruct((t, cw), F32), jax.ShapeDtypeStruct((t, LANE), F32),
                   jax.ShapeDtypeStruct((SUBLANE, LANE), F32)],
        compiler_params=_cp("arbitrary"))(h, h, h, conv_w, par, dq, dk, dv, dbg, dbgt)


def _conv_bwd(dc, h, conv_w, dh, *, tt, name):
    t = dc.shape[0]
    cw = 3 * A_WIDTH
    hb = tt // HALO
    nb = t // tt

    def body(dc_ref, after_ref, pre_ref, before_ref, cw_ref, dh_in_ref, dpre_ref, dcw_ref):
        i = pl.program_id(0)

        @pl.when(i == 0)
        def _():
            dcw_ref[...] = jnp.zeros_like(dcw_ref)

        dcv = dc_ref[...]
        after = jnp.where(i < nb - 1, after_ref[...], 0.0)
        cur = pre_ref[...].astype(F32)
        before = jnp.where(i > 0, before_ref[...].astype(F32)[HALO - SUBLANE:], 0.0)
        w = cw_ref[...]
        acc = dcv * w[CONV_K - 1:CONV_K, :]
        dcw_ref[CONV_K - 1:CONV_K, :] += jnp.sum(dcv * cur, axis=0, keepdims=True)
        for s in range(1, CONV_K):
            j = CONV_K - 1 - s
            acc = acc + _shift_up(dcv, after, s) * w[j:j + 1, :]
            dcw_ref[j:j + 1, :] += jnp.sum(dcv * _shift_down(cur, before, s), axis=0, keepdims=True)
        dpre_ref[...] = acc

    return pl.pallas_call(
        body, name=name, grid=(nb,),
        in_specs=[pl.BlockSpec((tt, cw), lambda i: (i, 0)),
                  pl.BlockSpec((SUBLANE, cw), lambda i: (jnp.minimum((i + 1) * (tt // SUBLANE), t // SUBLANE - 1), 0)),
                  pl.BlockSpec((tt, cw), lambda i: (i, 0)),
                  pl.BlockSpec((HALO, cw), lambda i: (jnp.maximum(i * hb - 1, 0), 0)),
                  pl.BlockSpec((CONV_K, cw), lambda i: (0, 0)), _ANY],
        out_specs=[pl.BlockSpec((tt, cw), lambda i: (i, 0)), pl.BlockSpec((SUBLANE, cw), lambda i: (0, 0))],
        out_shape=[jax.ShapeDtypeStruct(dh.shape, F32), jax.ShapeDtypeStruct((SUBLANE, cw), F32)],
        input_output_aliases={5: 0},
        compiler_params=_cp("arbitrary"))(dc, dc, h, h, conv_w, dh)


def _swa_bwd(h, dm, sinks_b, dh, *, name, carry=None):
    t = h.shape[0]
    qspec, cur, prev = _swa_specs()
    c_ins, c_in_specs, c_out_specs, c_outs, c_scratch = _carry_specs(carry)

    def body(*refs):
        (q_ref, kc_ref, kp_ref, vc_ref, vp_ref, zb_ref, dy_ref, sk_ref, dh_in_ref,
         dqz_ref, dk_ref, dv_ref, dsk_ref) = _carried(carry, refs, 9, 4, t // BLOCK)
        n_blk = pl.program_id(0)

        @pl.when(n_blk == 0)
        def _():
            dk_ref[...] = jnp.zeros_like(dk_ref)
            dv_ref[...] = jnp.zeros_like(dv_ref)
            dsk_ref[...] = jnp.zeros_like(dsk_ref)

        kp, kc, vp, vc = kp_ref[...], kc_ref[...], vp_ref[...], vc_ref[...]
        scale = B_HEAD_DIM ** -0.5
        hks = range(B_KV_HEADS)
        ksl = lambda hk: slice(hk * B_HEAD_DIM, (hk + 1) * B_HEAD_DIM)
        upper, groups = _swa_group_probs(q_ref, sk_ref, kp, kc, vp, vc, n_blk)
        zbs = [_stack_heads(zb_ref, hk) for hk in hks]
        dys = [_stack_heads(dy_ref, hk) for hk in hks]
        gates = [_silu_and_grad(zbs[hk]) for hk in hks]
        dos = [dys[hk] * gates[hk][0] for hk in hks]
        deltas = [jnp.sum(dos[hk] * groups[hk][3], -1, keepdims=True) for hk in hks]
        dps = [jnp.where(upper, _dot_nt(dos[hk], vp[:, ksl(hk)]), _dot_nt(dos[hk], vc[:, ksl(hk)])) for hk in hks]
        dss = [groups[hk][1] * (dps[hk] - deltas[hk]) for hk in hks]
        ds_up = [jnp.where(upper, dss[hk], 0.0) for hk in hks]
        ds_lo = [dss[hk] - ds_up[hk] for hk in hks]
        p_up = [jnp.where(upper, groups[hk][1], 0.0) for hk in hks]
        p_lo = [groups[hk][1] - p_up[hk] for hk in hks]
        dqs = [(_dot(ds_up[hk], kp[:, ksl(hk)]) + _dot(ds_lo[hk], kc[:, ksl(hk)])) * scale for hk in hks]
        dk_prev = [_dot_tn(ds_up[hk], groups[hk][0]) for hk in hks]
        dk_cur = [_dot_tn(ds_lo[hk], groups[hk][0]) for hk in hks]
        dv_prev = [_dot_tn(p_up[hk], dos[hk]) for hk in hks]
        dv_cur = [_dot_tn(p_lo[hk], dos[hk]) for hk in hks]
        for hk in hks:
            dzb = dys[hk] * groups[hk][3] * gates[hk][1]
            dsink = groups[hk][2] * deltas[hk]
            for g in range(B_GROUP):
                hq = hk * B_GROUP + g
                rows = slice(g * BLOCK, (g + 1) * BLOCK)
                qsl = slice(hq * B_HEAD_DIM, (hq + 1) * B_HEAD_DIM)
                dqz_ref[:, qsl] = dqs[hk][rows]
                dqz_ref[:, B_WIDTH + hq * B_HEAD_DIM:B_WIDTH + (hq + 1) * B_HEAD_DIM] = dzb[rows]
                dsk_ref[hq:hq + 1, :] += -jnp.sum(dsink[rows], keepdims=True)
        at_cur = pl.ds(pl.multiple_of(n_blk * BLOCK, BLOCK), BLOCK)
        at_prev = pl.ds(pl.multiple_of(jnp.maximum(n_blk - 1, 0) * BLOCK, BLOCK), BLOCK)
        dk_ref[at_prev, :] += jnp.concatenate(dk_prev, axis=1)
        dv_ref[at_prev, :] += jnp.concatenate(dv_prev, axis=1)
        dk_ref[at_cur, :] += jnp.concatenate(dk_cur, axis=1)
        dv_ref[at_cur, :] += jnp.concatenate(dv_cur, axis=1)

    narrow = jax.ShapeDtypeStruct((t, B_KV_WIDTH), F32)
    res = lambda a, b: pl.BlockSpec((a, b), lambda i: (0, 0))
    outs = pl.pallas_call(
        body, name=name, grid=(t // BLOCK,),
        in_specs=[qspec(C_QB), cur(C_KB), prev(C_KB), cur(C_VB), prev(C_VB), qspec(C_ZB),
                  pl.BlockSpec((BLOCK, B_WIDTH), lambda i: (i, 1)), res(B_Q_HEADS, LANE), _ANY] + c_in_specs,
        out_specs=[pl.BlockSpec((BLOCK, 2 * B_WIDTH), lambda i: (i, C_QB // (2 * B_WIDTH))),
                   res(t, B_KV_WIDTH), res(t, B_KV_WIDTH), res(B_Q_HEADS, LANE)] + c_out_specs,
        out_shape=[jax.ShapeDtypeStruct(dh.shape, F32), narrow, narrow,
                   jax.ShapeDtypeStruct((B_Q_HEADS, LANE), F32)] + c_outs,
        scratch_shapes=c_scratch,
        input_output_aliases={8: 0},
        compiler_params=_cp("arbitrary"))(h, h, h, h, h, h, dm, sinks_b, dh, *c_ins)
    return outs[:4], outs[4:]


def _in_proj_dw(dh_main, dh_tail, x, *, tk, name):
    t, n = x.shape

    def body(a_ref, t_ref, x_ref, o_ref, ot_ref):
        @pl.when(pl.program_id(0) == 0)
        def _():
            o_ref[...] = jnp.zeros_like(o_ref)
            ot_ref[...] = jnp.zeros_like(ot_ref)

        xb = x_ref[...].astype(BF16)
        o_ref[...] += _dot_tn(a_ref[...], xb)
        ot_ref[...] += _dot_tn(t_ref[...], xb)

    row = lambda a: pl.BlockSpec((tk, a.shape[1]), lambda kk: (kk, 0))
    acc = lambda a: pl.BlockSpec((a.shape[1], n), lambda kk: (0, 0))
    return pl.pallas_call(
        body, name=name, grid=(t // tk,), in_specs=[row(dh_main), row(dh_tail), row(x)],
        out_specs=[acc(dh_main), acc(dh_tail)],
        out_shape=[jax.ShapeDtypeStruct((a.shape[1], n), F32) for a in (dh_main, dh_tail)],
        compiler_params=_cp("arbitrary"))(dh_main, dh_tail, x)


def _in_proj_dx(dh_main, dh_tail, wt, dr, *, tm, name, carry=None):
    t, n_main = dh_main.shape
    n_tail = dh_tail.shape[1]
    c_ins, c_in_specs, c_out_specs, c_outs, c_scratch = _carry_specs(carry)

    def body(*refs):
        a_ref, t_ref, wa_ref, wt_ref, r_ref, o_ref = _carried(carry, refs, 5, 1, t // tm)
        o_ref[...] = _dot(a_ref[...], wa_ref[...]) + _dot(t_ref[...], wt_ref[...]) + DEEPNORM_ALPHA * r_ref[...]

    row = lambda w: pl.BlockSpec((tm, w), lambda i: (i, 0))
    outs = pl.pallas_call(
        body, name=name, grid=(t // tm,),
        in_specs=[row(n_main), row(n_tail), pl.BlockSpec((n_main, D_MODEL), lambda i: (0, 0)),
                  pl.BlockSpec((n_tail, D_MODEL), lambda i: (n_main // n_tail, 0)), row(D_MODEL)] + c_in_specs,
        out_specs=[row(D_MODEL)] + c_out_specs,
        out_shape=[jax.ShapeDtypeStruct((t, D_MODEL), F32)] + c_outs,
        scratch_shapes=c_scratch,
        compiler_params=_cp("arbitrary"))(dh_main, dh_tail, wt, wt, dr, *c_ins)
    return outs[0], outs[1:]


def _layer_bwd(dxn, res, wt, conv_w, par, sinks_b, norm_w, w_out_bf, ln_g, l, carries=None, carry_dx=None):
    carries = carries or {}
    w_out_bf = res["w_out"]
    dr, dm, dw_out, dln_g, dln_b = _ln_out_bwd(dxn, res["r"], res["mixed"], ln_g, w_out_bf, tm=512, name=f"ln_out_bwd_{l}")
    h = res["h"]
    do, dh, dnw = _dn_post_bwd(dm, res["oa"], h, norm_w, tm=512, name=f"dn_post_bwd_{l}")
    dvn, ds_all = _dn_scan_bwd(res["q"], res["k"], res["w"], res["qk"], res["bg"], do, name=f"dn_scan_bwd_{l}")
    (dq, dk, dv, dbg, dbgt), got_chunk = _dn_chunk_bwd(
        res["q"], res["k"], res["v"], res["vn"], res["tmat"], res["qk"], res["bg"], res["bgt"], res["s_all"], ds_all,
        dvn, do, name=f"dn_chunk_bwd_{l}", carry=carries.get("dn_chunk"))
    dc, dbgi, dpar = _dn_pre_bwd(h, conv_w, par, dq, dk, dv, dbg, dbgt, tt=512, name=f"dn_pre_bwd_{l}")
    dh, dcw = _conv_bwd(dc, h, conv_w, dh, tt=512, name=f"conv_bwd_{l}")
    (dh, dkb, dvb, dsk), got_swa = _swa_bwd(h, dm, sinks_b, dh, name=f"swa_bwd_{l}", carry=carries.get("swa"))
    carried = dict(dn_chunk=got_chunk, swa=got_swa)
    dh_tail = jnp.concatenate([dkb, dvb, dbgi], axis=1)
    dwt_main, dwt_tail = _in_proj_dw(dh, dh_tail, res["x"], tk=512, name=f"in_proj_dw_{l}")
    grads = dict(w_in=(dwt_main, dwt_tail), conv_w=dcw[:CONV_K], a_log=dpar[0, A_HEADS:2 * A_HEADS],
                 dt_bias=dpar[1, A_HEADS:2 * A_HEADS], norm_w=dnw[0], sinks=dsk[:, 0], w_out=dw_out,
                 ln_g=dln_g[0], ln_b=dln_b[0])
    dx, carried_dx = _in_proj_dx(dh, dh_tail, wt, dr, tm=512, name=f"in_proj_dx_{l}",
                                 carry=None if carry_dx is None else carry_dx(grads))
    return dx, grads, carried, carried_dx


def _layer_args(wt, conv_w, a_log, dt_bias, sinks, norm_w, w_out_bf):
    return (wt, conv_w, _gate_params(a_log, dt_bias), jnp.broadcast_to(sinks[:, None], (B_Q_HEADS, LANE)),
            norm_w[None], w_out_bf)


def _local_step(x, target, args0, args1, ln_g, ln_b, gathers=None, reduce1=None, reduce0=None):
    assert DEPTH == 2
    x1, res0, got = _layer_fwd(x, *args0, ln_g[0][None], ln_b[0][None], 0, carries=gathers)
    if gathers is not None:
        args1 = args1(got)
    (dx, loss_tile), res1, _ = _layer_fwd(x1, *args1, ln_g[1][None], ln_b[1][None], 1, target=target)
    dx, grads1, _, _ = _layer_bwd(dx, res1, *args1, ln_g[1][None], 1)
    carries = None if reduce1 is None else reduce1(grads1)
    carry_dx = None if reduce0 is None else (lambda grads0: reduce0(grads0, grads1, loss_tile))
    dx, grads0, landed1, landed0 = _layer_bwd(dx, res0, *args0, ln_g[0][None], 0, carries=carries, carry_dx=carry_dx)
    return loss_tile, dx, [grads0, grads1], landed1, landed0


_ANY = pl.BlockSpec(memory_space=pl.ANY)
_MESH = pl.DeviceIdType.MESH


HALF = D_MODEL // 2


class _Exchange:
    def __init__(self, ins, outs, n_remote, n_local, plan):
        self.ins, self.outs, self.n_remote, self.n_local, self.plan = tuple(ins), tuple(outs), n_remote, n_local, plan

    def scratch(self):
        return [pltpu.SemaphoreType.DMA((self.n_remote,)), pltpu.SemaphoreType.DMA((self.n_remote,)),
                pltpu.SemaphoreType.DMA((max(self.n_local, 1),))]

    def _copies(self, in_refs, out_refs, sems, arriving):
        send_sems, recv_sems, local_sems = sems
        local, sends, recvs = self.plan(in_refs, out_refs)
        loc = [pltpu.make_async_copy(s, d, local_sems.at[i]) for i, (s, d) in enumerate(local)]
        rem = [pltpu.make_async_remote_copy(src_ref=s, dst_ref=recvs[i] if arriving else d, send_sem=send_sems.at[i],
                                            recv_sem=recv_sems.at[i], device_id=peer, device_id_type=_MESH)
               for i, (s, d, peer) in enumerate(sends)]
        return loc, rem

    def start(self, in_refs, out_refs, sems):
        loc, rem = self._copies(in_refs, out_refs, sems, arriving=False)
        for cp in loc + rem:
            cp.start()

    def finish(self, in_refs, out_refs, sems):
        loc, rem = self._copies(in_refs, out_refs, sems, arriving=True)
        for cp in rem:
            cp.wait_recv()
        for cp in rem:
            cp.wait_send()
        for cp in loc:
            cp.wait()


def _run_exchange(ex, *, name):
    n_in, n_out = len(ex.ins), len(ex.outs)

    def body(*refs):
        parts = refs[:n_in], refs[n_in:n_in + n_out], refs[n_in + n_out:]
        ex.start(*parts)
        ex.finish(*parts)

    return pl.pallas_call(body, name=name, in_specs=[_ANY] * n_in, out_specs=[_ANY] * n_out, out_shape=list(ex.outs),
                          scratch_shapes=ex.scratch())(*ex.ins)


def _place():
    x, y, c = lax.axis_index("x"), lax.axis_index("y"), lax.axis_index("c")
    return x, y, c, [(1 - x, y), (x, 1 - y), (1 - x, 1 - y)]


def _gather_exchange(arrays):
    n = len(arrays)

    def plan(src, dst):
        x, y, c, chips = _place()
        me = 2 * x + y
        local = [(src[k], dst[k].at[me]) for k in range(n)]
        sends = [(src[k], dst[k].at[me], (px, py, c)) for k in range(n) for px, py in chips]
        recvs = [dst[k].at[2 * px + py] for k in range(n) for px, py in chips]
        return local, sends, recvs

    return _Exchange(arrays, [jax.ShapeDtypeStruct((N_SHARD,) + a.shape, a.dtype) for a in arrays], 3 * n, n, plan)


def _gather_two_level(pack, conv_w, *, name):
    rows = pack.shape[0]
    part_rows = rows // 2

    def body(pack_ref, conv_ref, land_ref, conv_land_ref, send1, recv1, send2, recv2, csend, crecv, local_sems):
        x, y, c, chips = _place()
        me = 2 * x + y
        sibling = (x, y, 1 - c)
        part = lambda core: pl.ds(pl.multiple_of(core * part_rows, 16), part_rows)
        remote = lambda src, dst, ss, rs, to: pltpu.make_async_remote_copy(
            src_ref=src, dst_ref=dst, send_sem=ss, recv_sem=rs, device_id=to, device_id_type=_MESH)
        local = [pltpu.make_async_copy(pack_ref, land_ref.at[me], local_sems.at[0]),
                 pltpu.make_async_copy(conv_ref, conv_land_ref.at[me], local_sems.at[1])]
        for cp in local:
            cp.start()
        first = [remote(pack_ref.at[part(c)], land_ref.at[me, part(c)], send1.at[j], recv1.at[j], (px, py, c))
                 for j, (px, py) in enumerate(chips)]
        convs = [remote(conv_ref, conv_land_ref.at[me], csend.at[j], crecv.at[j], (px, py, c))
                 for j, (px, py) in enumerate(chips)]
        for cp in first + convs:
            cp.start()
        passed = []
        for j, (px, py) in enumerate(chips):
            slot = 2 * px + py
            remote(pack_ref.at[part(c)], land_ref.at[slot, part(c)], send1.at[j], recv1.at[j], (px, py, c)).wait_recv()
            cp = remote(land_ref.at[slot, part(c)], land_ref.at[slot, part(c)], send2.at[j], recv2.at[j], sibling)
            cp.start()
            passed.append(cp)
        for j, (px, py) in enumerate(chips):
            slot = 2 * px + py
            remote(land_ref.at[slot, part(1 - c)], land_ref.at[slot, part(1 - c)], send2.at[j], recv2.at[j],
                   sibling).wait_recv()
            remote(conv_ref, conv_land_ref.at[slot], csend.at[j], crecv.at[j], (px, py, c)).wait_recv()
        for cp in first + convs + passed:
            cp.wait_send()
        for cp in local:
            cp.wait()

    sems = [pltpu.SemaphoreType.DMA((3,))] * 6 + [pltpu.SemaphoreType.DMA((2,))]
    return pl.pallas_call(
        body, name=name, in_specs=[_ANY, _ANY], out_specs=[_ANY, _ANY],
        out_shape=[jax.ShapeDtypeStruct((N_SHARD,) + pack.shape, pack.dtype),
                   jax.ShapeDtypeStruct((N_SHARD,) + conv_w.shape, conv_w.dtype)],
        scratch_shapes=sems)(pack, conv_w)


def _half(core):
    return pl.ds(pl.multiple_of(core * HALF, HALF), HALF)


def _reduce_scatter_exchange(g, row0, rows):
    def plan(src, dst):
        x, y, c, chips = _place()
        peers = [(px, py, c if t == 0 else 1 - c) for px, py in chips for t in (0, 1)] + [(x, y, 1 - c)]
        sends = [(src[0].at[2 * px + py, pl.ds(row0, rows), _half(pc)], dst[0].at[k], (px, py, pc))
                 for k, (px, py, pc) in enumerate(peers)]
        return [], sends, [dst[0].at[k] for k in range(7)]

    return _Exchange([g], [jax.ShapeDtypeStruct((7, rows, HALF), g.dtype)], 7, 0, plan)


def _pair_window_exchange(g):
    def plan(src, dst):
        x, y, c, _ = _place()
        return [], [(src[0].at[:, :, _half(1 - c)], dst[0], (x, y, 1 - c))], [dst[0]]

    return _Exchange([g], [jax.ShapeDtypeStruct(g.shape[:2] + (HALF,), g.dtype)], 1, 0, plan)


def _chip_scatter_exchange(p, small):
    def plan(src, dst):
        x, y, c, chips = _place()
        mine = 4 * x + 2 * y + c
        peers = [(px, py, c if t == 0 else 1 - c) for px, py in chips for t in (0, 1)] + [(x, y, 1 - c)]
        sends = [(src[0].at[2 * px + py], dst[0].at[j], (px, py, c)) for j, (px, py) in enumerate(chips)]
        recvs = [dst[0].at[j] for j in range(3)]
        sends += [(src[1], dst[1].at[mine], peer) for peer in peers]
        recvs += [dst[1].at[4 * px + 2 * py + pc] for px, py, pc in peers]
        return [(src[1], dst[1].at[mine])], sends, recvs

    outs = [jax.ShapeDtypeStruct((3,) + p.shape[1:], p.dtype), jax.ShapeDtypeStruct((8,) + small.shape, small.dtype)]
    return _Exchange([p, small], outs, 10, 1, plan)


def _share_exchange(arrays):
    n = len(arrays)

    def plan(src, dst):
        x, y, c, _ = _place()
        return [], [(src[k], dst[k], (x, y, 1 - c)) for k in range(n)], [dst[k] for k in range(n)]

    return _Exchange(arrays, [jax.ShapeDtypeStruct(a.shape, a.dtype) for a in arrays], n, 0, plan)


def _sum_scatter(g, lands, me, core, *, tc, name):
    rows = g.shape[1]
    per = HALF // tc
    n = len(lands)

    def body(*refs):
        g_ref, land_refs, o_ref = refs[1], refs[2:2 + n], refs[2 + n]
        at = 0
        for land_ref in land_refs:
            run = slice(at, at + land_ref.shape[1])
            acc = g_ref[run, :].astype(F32)
            for k in range(7):
                acc = acc + land_ref[k].astype(F32)
            o_ref[run, :] = acc
            at = run.stop

    return pl.pallas_call(
        body, name=name, out_shape=jax.ShapeDtypeStruct((rows, HALF), F32), compiler_params=_cp("parallel"),
        grid_spec=pltpu.PrefetchScalarGridSpec(
            num_scalar_prefetch=1, grid=(per,),
            in_specs=[pl.BlockSpec((None, rows, tc), lambda i, w: (w[0], 0, w[1] * per + i))]
            + [pl.BlockSpec((7, a.shape[1], tc), lambda i, w: (0, 0, i)) for a in lands],
            out_specs=pl.BlockSpec((rows, tc), lambda i, w: (0, i))))(
        jnp.stack([me, core]).astype(jnp.int32), g, *lands)


def _pair_add(g, land, core, *, name):
    n, rows, _ = g.shape

    def body(core_ref, g_ref, land_ref, o_ref):
        o_ref[...] = (g_ref[...].astype(F32) + land_ref[...].astype(F32)).astype(o_ref.dtype)

    blk = pl.BlockSpec((1, rows, HALF), lambda i, w: (i, 0, 0))
    return pl.pallas_call(
        body, name=name, out_shape=jax.ShapeDtypeStruct((n, rows, HALF), g.dtype), compiler_params=_cp("parallel"),
        grid_spec=pltpu.PrefetchScalarGridSpec(
            num_scalar_prefetch=1, grid=(n,),
            in_specs=[pl.BlockSpec((1, rows, HALF), lambda i, w: (i, 0, w[0])), blk], out_specs=blk))(
        jnp.reshape(core, (1,)).astype(jnp.int32), g, land)


def _sum_chips(p, land, me, *, tc, name):
    rows = p.shape[1]

    def body(me_ref, p_ref, land_ref, o_ref):
        acc = p_ref[...].astype(F32)
        for k in range(3):
            acc = acc + land_ref[k].astype(F32)
        o_ref[...] = acc

    return pl.pallas_call(
        body, name=name, out_shape=jax.ShapeDtypeStruct((rows, HALF), F32), compiler_params=_cp("parallel"),
        grid_spec=pltpu.PrefetchScalarGridSpec(
            num_scalar_prefetch=1, grid=(HALF // tc,),
            in_specs=[pl.BlockSpec((None, rows, tc), lambda i, w: (w[0], 0, i)),
                      pl.BlockSpec((3, rows, tc), lambda i, w: (0, 0, i))],
            out_specs=pl.BlockSpec((rows, tc), lambda i, w: (0, i))))(
        jnp.reshape(me, (1,)).astype(jnp.int32), p, land)


def _sum_slots(a, *, name):
    n = a.shape[0]

    def body(a_ref, o_ref):
        acc = a_ref[0]
        for k in range(1, n):
            acc = acc + a_ref[k]
        o_ref[...] = acc

    return pl.pallas_call(body, name=name, out_shape=jax.ShapeDtypeStruct(a.shape[1:], a.dtype))(a)


def _elementwise(fn, ins, n_out, block, *, name):
    shape = ins[0].shape
    grid = tuple(s // b for s, b in zip(shape, block))
    n_in = len(ins)

    def body(*refs):
        outs = fn(*[r[...] for r in refs[:n_in]])
        for o_ref, val in zip(refs[n_in:], outs):
            o_ref[...] = val

    spec = pl.BlockSpec(block, lambda i, j, k: (i, j, k))
    return pl.pallas_call(body, name=name, grid=grid, in_specs=[spec] * n_in, out_specs=[spec] * n_out,
                          out_shape=[jax.ShapeDtypeStruct(shape, F32)] * n_out,
                          compiler_params=_cp(*["parallel"] * 3))(*ins)


def _adamw_math(w, g, m, v):
    mn = ADAM_B1 * m + (1.0 - ADAM_B1) * g
    vn = ADAM_B2 * v + (1.0 - ADAM_B2) * (g * g)
    m_hat = mn / (1.0 - ADAM_B1 ** ADAM_STEP)
    v_hat = vn / (1.0 - ADAM_B2 ** ADAM_STEP)
    return -ADAM_LR * (m_hat / (jnp.sqrt(v_hat) + ADAM_EPS) + ADAM_WD * w), mn, vn


def _adamw(w, g, m, v, block, *, name):
    return _elementwise(_adamw_math, [w, g, m, v], 3, block, name=name)


def _interleave_layers(layers, *, tc, name):
    rows, cols = layers[0].shape
    n = len(layers)

    def body(*refs):
        for l in range(n):
            refs[n][:, l, :] = refs[l][...]

    return pl.pallas_call(body, name=name, grid=(cols // tc,),
                          in_specs=[pl.BlockSpec((rows, tc), lambda i: (0, i))] * n,
                          out_specs=pl.BlockSpec((rows, n, tc), lambda i: (0, 0, i)),
                          out_shape=jax.ShapeDtypeStruct((rows, n, cols), layers[0].dtype),
                          compiler_params=_cp("parallel"))(*layers)


def _adamw_small(ws, gs, ms, vs, *, name):
    n = len(ws)

    def body(*refs):
        w, g, m, v, outs = refs[:n], refs[n:2 * n], refs[2 * n:3 * n], refs[3 * n:4 * n], refs[4 * n:]
        for k in range(n):
            for slot, val in enumerate(_adamw_math(w[k][...], g[k][...], m[k][...], v[k][...])):
                outs[slot * n + k][...] = val

    outs = pl.pallas_call(body, name=name, out_shape=[jax.ShapeDtypeStruct(a.shape, F32) for a in ws] * 3)(
        *ws, *gs, *ms, *vs)
    return outs[:n], outs[n:2 * n], outs[2 * n:]


def _to_kernel_order(wt):
    gates = jnp.pad(wt[2048:2056], ((0, LANE - 2 * A_HEADS), (0, 0)))
    return jnp.concatenate([wt[0:2048], wt[2056:2568], wt[2824:3336], wt[2568:2696], wt[2696:2824], gates], axis=0)


def _from_kernel_order(main, tail):
    return jnp.concatenate([main[0:2048], tail[C_BG - DH_MAIN:C_BG - DH_MAIN + 2 * A_HEADS],
                            main[C_QB:C_QB + B_WIDTH], tail[0:B_KV_WIDTH], tail[B_KV_WIDTH:2 * B_KV_WIDTH],
                            main[C_ZB:C_ZB + B_WIDTH]], axis=0)


def _gate_params(a_log, dt_bias):
    return jnp.pad(jnp.stack([a_log, dt_bias]), ((0, SUBLANE - 2), (A_HEADS, LANE - 2 * A_HEADS)))


SMALL = ("conv_w", "a_log", "dt_bias", "norm_w", "sinks", "ln_g", "ln_b")


def _pack(parts, cols):
    flat = jnp.concatenate([p.reshape(-1) for p in parts])
    rows = -(-flat.shape[0] // cols)
    return jnp.pad(flat, (0, rows * cols - flat.shape[0])).reshape(rows, cols)


def _unpack(packed, shapes):
    flat = packed.reshape(-1)
    out, at = [], 0
    for s in shapes:
        n = math.prod(s)
        out.append(flat[at:at + n].reshape(s))
        at += n
    return out


def kernel(x, w_in, conv_w, a_log, dt_bias, norm_w, sinks, w_out, ln_g, ln_b, loss_target, m_w_in, m_conv_w, m_a_log, m_dt_bias, m_norm_w, m_sinks, m_w_out, m_ln_g, m_ln_b, v_w_in, v_conv_w, v_a_log, v_dt_bias, v_norm_w, v_sinks, v_w_out, v_ln_g, v_ln_b):
    xi, yi, ci = lax.axis_index("x"), lax.axis_index("y"), lax.axis_index("c")
    me = 2 * xi + yi

    to_t = lambda a: jnp.transpose(a, (2, 0, 1))
    from_t = lambda a: jnp.transpose(a, (1, 2, 0))

    wt_shard = to_t(w_in)

    def pack_weights(l):
        rows = jnp.pad(wt_shard[:, l], ((0, IN_PAD - IN_SHARD), (0, 0)))
        return jnp.concatenate([rows, w_out[l]], axis=0).astype(BF16)

    pack0, pack1 = pack_weights(0), pack_weights(1)
    got_in0, g_conv = _gather_two_level(pack0[:IN_PAD], conv_w, name="gather_weights_0")
    conv_full = jnp.moveaxis(g_conv, 0, 2).reshape(DEPTH, CONV_K, 3 * A_WIDTH)
    piece = IN_PAD // 3
    carriers = ("dn_pre", "dn_wy", "dn_scan")
    gathers = {nm: _gather_exchange([pack1[i * piece:(i + 1) * piece]]) for i, nm in enumerate(carriers)}
    gathers.update(in_proj=_gather_exchange([pack0[IN_PAD:]]), swa=_gather_exchange([pack1[IN_PAD:]]))
    w_in_of = lambda rows: _to_kernel_order(rows[:, :IN_SHARD].reshape(IN_COLS, D_MODEL))
    w_out_of = lambda rows: rows.reshape(D_MODEL, D_MODEL)
    args0 = _layer_args(w_in_of(got_in0), conv_full[0], a_log[0], dt_bias[0], sinks[0], norm_w[0],
                        lambda got: w_out_of(got[0]))

    def args1(got):
        rows = jnp.concatenate([got[nm][0] for nm in carriers], axis=1)
        return _layer_args(w_in_of(rows), conv_full[1], a_log[1], dt_bias[1], sinks[1], norm_w[1],
                           w_out_of(got["swa"][0]))

    def pack_grads(g):
        gin = _from_kernel_order(*g["w_in"]).reshape(N_SHARD, IN_SHARD, D_MODEL)
        gin = jnp.pad(gin, ((0, 0), (0, IN_PAD - IN_SHARD), (0, 0)))
        return jnp.concatenate([gin, g["w_out"].reshape(N_SHARD, OUT_SHARD, D_MODEL)], axis=1).astype(BF16)

    packed = {}

    def reduce1(grads1):
        packed[1] = pack_grads(grads1)
        half_rows = packed[1].shape[1] // 2
        return dict(dn_chunk=_reduce_scatter_exchange(packed[1], 0, half_rows),
                    swa=_reduce_scatter_exchange(packed[1], half_rows, half_rows))

    def reduce0(grads0, grads1, loss_tile):
        g0 = pack_grads(grads0)
        from_sibling = _run_exchange(_pair_window_exchange(g0), name="pair_reduce_0")[0]
        packed[0] = _pair_add(g0, from_sibling, ci, name="pair_add_0")
        gsmall = _pack([jnp.stack([g[nm] for g in (grads0, grads1)]) for nm in SMALL] + [loss_tile[0, 0:1]], D_MODEL)
        return _chip_scatter_exchange(packed[0], gsmall)

    _, dx, grads, landed1, (landed0, landed_small) = _local_step(
        x[0], loss_target[0], args0, args1, ln_g, ln_b, gathers=gathers, reduce1=reduce1, reduce0=reduce0)

    small_shapes = [(DEPTH,) + grads[0][nm].shape for nm in SMALL]
    halves = [_sum_chips(packed[0], landed0, me, tc=2 * LANE, name="reduce_sum_0"),
              _sum_scatter(packed[1], [landed1["dn_chunk"][0], landed1["swa"][0]], me, ci, tc=2 * LANE,
                           name="reduce_sum_1")]
    s_small = _sum_slots(landed_small, name="reduce_sum_small")
    others = _run_exchange(_share_exchange(halves), name="pair_share")
    full = [jnp.where(ci == 0, jnp.concatenate([mine, other], axis=1), jnp.concatenate([other, mine], axis=1))
            for mine, other in zip(halves, others)]
    grad_in_layers = [f[:IN_SHARD] for f in full]
    grad_out = jnp.stack([f[IN_PAD:] for f in full])
    out_blk = (1, OUT_SHARD, D_MODEL)
    *small_grads, loss = _unpack(s_small, small_shapes + [()])
    gs = dict(zip(SMALL, small_grads))
    gs["conv_w"] = lax.dynamic_slice_in_dim(gs["conv_w"], me * CONV_SHARD, CONV_SHARD, axis=2)

    grad_in_t = _interleave_layers(grad_in_layers, tc=2 * LANE, name="grad_in_layers")
    d_in, nm_in, nv_in = (from_t(o) for o in _adamw(to_t(w_in), grad_in_t, to_t(m_w_in), to_t(v_w_in),
                                                    (IN_SHARD // 6, DEPTH, D_MODEL), name="adamw_in"))
    grad_in = from_t(grad_in_t)
    d_out, nm_out, nv_out = _adamw(w_out, grad_out, m_w_out, v_w_out, out_blk, name="adamw_out")
    ws = dict(conv_w=conv_w, a_log=a_log, dt_bias=dt_bias, norm_w=norm_w, sinks=sinks, ln_g=ln_g, ln_b=ln_b)
    ms = dict(conv_w=m_conv_w, a_log=m_a_log, dt_bias=m_dt_bias, norm_w=m_norm_w, sinks=m_sinks, ln_g=m_ln_g, ln_b=m_ln_b)
    vs = dict(conv_w=v_conv_w, a_log=v_a_log, dt_bias=v_dt_bias, norm_w=v_norm_w, sinks=v_sinks, ln_g=v_ln_g, ln_b=v_ln_b)
    d_s, nm_s, nv_s = (dict(zip(SMALL, o)) for o in _adamw_small(*[[d[nm] for nm in SMALL] for d in (ws, gs, ms, vs)],
                                                                 name="adamw_small"))

    def in_order(big_in, small, big_out):
        return (big_in, small["conv_w"], small["a_log"], small["dt_bias"], small["norm_w"], small["sinks"], big_out,
                small["ln_g"], small["ln_b"])

    return (loss, dx[None], *in_order(grad_in, gs, grad_out), *in_order(d_in, d_s, d_out),
            *in_order(nm_in, nm_s, nm_out), *in_order(nv_in, nv_s, nv_out))
```

```python
import math

import jax
import jax.numpy as jnp
from jax import lax
from jax.experimental import pallas as pl
from jax.experimental.pallas import tpu as pltpu

F32 = jnp.float32
BF16 = jnp.bfloat16
HI = lax.Precision.HIGHEST

D_MODEL = 1024
DEPTH = 2
A_HEADS = 4
A_HEAD_DIM = 128
A_WIDTH = 512
CONV_K = 4
CHUNK = 64
B_Q_HEADS = 8
B_KV_HEADS = 2
B_HEAD_DIM = 64
B_GROUP = 4
B_WIDTH = 512
B_KV_WIDTH = 128
BLOCK = 128
IN_COLS = 3336
DEEPNORM_ALPHA = (2 * DEPTH) ** 0.25
LN_EPS = 1e-5
RMS_EPS = 1e-6
L2_EPS = 1e-6
ADAM_LR = 0.001
ADAM_B1 = 0.9
ADAM_B2 = 0.999
ADAM_EPS = 1e-08
ADAM_WD = 0.01
ADAM_STEP = 10

N_SHARD = 4
IN_SHARD = IN_COLS // N_SHARD
OUT_SHARD = D_MODEL // N_SHARD
CONV_SHARD = 3 * A_WIDTH // N_SHARD
IN_PAD = -(-IN_SHARD // 96) * 96

P_COLS = 3456
C_PRE = 0
C_ZA = 1536
C_QB = 2048
C_ZB = 2560
C_KB = 3072
C_VB = 3200
C_BG = 3328
DH_MAIN = C_KB
LANE = 128
SUBLANE = 8
HALO = 16
VMEM_LIMIT = 56 * 1024 * 1024
ALIBI = tuple(2.0 ** (-8.0 * (h + 1) / B_Q_HEADS) for h in range(B_Q_HEADS))
NEG = -1e30


def _cp(*sem):
    return pltpu.CompilerParams(dimension_semantics=sem, vmem_limit_bytes=VMEM_LIMIT)


def _dot(a, b):
    return jnp.dot(a.astype(BF16), b.astype(BF16), preferred_element_type=F32)


def _dot_nt(a, b):
    return lax.dot_general(a.astype(BF16), b.astype(BF16), (((1,), (1,)), ((), ())),
                           preferred_element_type=F32)


def _dot_tn(a, b):
    return lax.dot_general(a.astype(BF16), b.astype(BF16), (((0,), (0,)), ((), ())),
                           preferred_element_type=F32)


def _dot_hi(a, b):
    return jnp.dot(a, b, precision=HI, preferred_element_type=F32)


def _sigmoid(x):
    return jax.nn.sigmoid(x)


def _silu(x):
    return x * _sigmoid(x)


def _silu_and_grad(x):
    s = _sigmoid(x)
    return x * s, s * (1.0 + x * (1.0 - s))


def _softplus(x):
    return jnp.maximum(x, 0.0) + jnp.log(1.0 + jnp.exp(-jnp.abs(x)))


def _shift_down(cur, before, s):
    if s == 0:
        return cur
    r = pltpu.roll(cur, s, 0)
    rb = pltpu.roll(before, s, 0)
    row = lax.broadcasted_iota(jnp.int32, before.shape, 0)
    head = jnp.where(row < s, rb, r[0:SUBLANE])
    return jnp.concatenate([head, r[SUBLANE:]], axis=0)


def _shift_up(cur, after, s):
    if s == 0:
        return cur
    n = cur.shape[0]
    r = pltpu.roll(cur, n - s, 0)
    ra = pltpu.roll(after, SUBLANE - s, 0)
    row = lax.broadcasted_iota(jnp.int32, after.shape, 0)
    tail = jnp.where(row >= SUBLANE - s, ra, r[n - SUBLANE:])
    return jnp.concatenate([r[:n - SUBLANE], tail], axis=0)


def _conv_fwd(cur, before, w):
    acc = cur * w[CONV_K - 1:CONV_K, :]
    for s in range(1, CONV_K):
        acc = acc + _shift_down(cur, before, s) * w[CONV_K - 1 - s:CONV_K - s, :]
    return acc


def _matmul_nt(a, bt, *, tm, name, carry=None):
    m, k = a.shape
    n = bt.shape[0]
    c_ins, c_in_specs, c_out_specs, c_outs, c_scratch = _carry_specs(carry)

    def body(*refs):
        a_ref, b_ref, o_ref = _carried(carry, refs, 2, 1, m // tm)
        o_ref[...] = _dot_nt(a_ref[...], b_ref[...]).astype(o_ref.dtype)

    outs = pl.pallas_call(
        body, name=name, grid=(m // tm,),
        in_specs=[pl.BlockSpec((tm, k), lambda i: (i, 0)), pl.BlockSpec((n, k), lambda i: (0, 0))] + c_in_specs,
        out_specs=[pl.BlockSpec((tm, n), lambda i: (i, 0))] + c_out_specs,
        out_shape=[jax.ShapeDtypeStruct((m, n), BF16)] + c_outs,
        scratch_shapes=c_scratch,
        compiler_params=_cp("arbitrary"))(a, bt, *c_ins)
    return outs[0], outs[1:]


def _dn_pre(h, conv_w, par, *, tt, name, carry=None):
    t = h.shape[0]
    cw = 3 * A_WIDTH
    hb = tt // HALO

    c_ins, c_in_specs, c_out_specs, c_outs, c_scratch = _carry_specs(carry)

    def body(*refs):
        (pre_ref, halo_ref, bgi_ref, cw_ref, par_ref,
         q_ref, k_ref, v_ref, bg_ref, bgt_ref) = _carried(carry, refs, 5, 5, t // tt)
        i = pl.program_id(0)
        cur = pre_ref[...].astype(F32)
        before = jnp.where(i > 0, halo_ref[...].astype(F32)[HALO - SUBLANE:], 0.0)
        s = _silu(_conv_fwd(cur, before, cw_ref[...]))
        for hd in range(A_HEADS):
            sl = slice(hd * LANE, (hd + 1) * LANE)
            tq = s[:, hd * LANE:(hd + 1) * LANE]
            q_ref[:, sl] = tq * (lax.rsqrt(jnp.sum(tq * tq, -1, keepdims=True) + L2_EPS) * (A_HEAD_DIM ** -0.5))
            tk = s[:, A_WIDTH + hd * LANE:A_WIDTH + (hd + 1) * LANE]
            k_ref[:, sl] = tk * lax.rsqrt(jnp.sum(tk * tk, -1, keepdims=True) + L2_EPS)
        v_ref[...] = s[:, 2 * A_WIDTH:]
        raw = bgi_ref[...].astype(F32)
        lane = lax.broadcasted_iota(jnp.int32, raw.shape, 1)
        is_a = (lane >= A_HEADS) & (lane < 2 * A_HEADS)
        g = jnp.where(is_a, -jnp.exp(par_ref[0:1, :]) * _softplus(raw + par_ref[1:2, :]), 0.0)
        gc = _dot_hi(_chunk_tri(tt, lower=True), g)
        bg = jnp.where(lane < A_HEADS, _sigmoid(raw), gc)
        bg_ref[...] = bg
        bgt_ref[...] = jnp.transpose(bg)[0:SUBLANE, :]

    wide = jax.ShapeDtypeStruct((t, A_WIDTH), F32)
    outs = pl.pallas_call(
        body, name=name, grid=(t // tt,),
        in_specs=[pl.BlockSpec((tt, cw), lambda i: (i, 0)),
                  pl.BlockSpec((HALO, cw), lambda i: (jnp.maximum(i * hb - 1, 0), 0)),
                  pl.BlockSpec((tt, LANE), lambda i: (i, C_BG // LANE)),
                  pl.BlockSpec((CONV_K, cw), lambda i: (0, 0)),
                  pl.BlockSpec((SUBLANE, LANE), lambda i: (0, 0))] + c_in_specs,
        out_specs=[pl.BlockSpec((tt, A_WIDTH), lambda i: (i, 0))] * 3
        + [pl.BlockSpec((tt, LANE), lambda i: (i, 0)), pl.BlockSpec((SUBLANE, tt), lambda i: (0, i))] + c_out_specs,
        out_shape=[wide, wide, wide, jax.ShapeDtypeStruct((t, LANE), F32),
                   jax.ShapeDtypeStruct((SUBLANE, t), F32)] + c_outs,
        scratch_shapes=c_scratch,
        compiler_params=_cp("arbitrary"))(h, h, h, conv_w, par, *c_ins)
    return outs[:5], outs[5:]


def _chunk_tri(n, lower):
    r = lax.broadcasted_iota(jnp.int32, (n, n), 0)
    c = lax.broadcasted_iota(jnp.int32, (n, n), 1)
    shift = CHUNK.bit_length() - 1
    same = jnp.right_shift(r, shift) == jnp.right_shift(c, shift)
    return (same & ((c <= r) if lower else (c >= r))).astype(F32)


def _chunk_masks():
    r = lax.broadcasted_iota(jnp.int32, (CHUNK, CHUNK), 0)
    c = lax.broadcasted_iota(jnp.int32, (CHUNK, CHUNK), 1)
    return r >= c, r > c, r == c


def _split(a):
    hi = a.astype(BF16)
    return hi, (a - hi.astype(F32)).astype(BF16)


def _dot3(a, b):
    (ah, al), (bh, bl) = a, b
    d = lambda p, q: jnp.dot(p, q, preferred_element_type=F32)
    return d(ah, bh) + (d(ah, bl) + d(al, bh))


def _tri_inv_many(a_list, eye):
    d = lambda p, q: jnp.dot(p, q, preferred_element_type=F32)
    p = [(-a).astype(BF16) for a in a_list]
    tm = [eye - a for a in a_list]
    for _ in range(5):
        pf = [d(pi, pi) for pi in p]
        p = [x.astype(BF16) for x in pf]
        tm = [t + d(t.astype(BF16), pi) for t, pi in zip(tm, p)]
    ms = [_split(eye + a) for a in a_list]
    res = [eye - _dot3(m, _split(t)) for m, t in zip(ms, tm)]
    return [t + d(t.astype(BF16), r.astype(BF16)) for t, r in zip(tm, res)]


def _chunk_gates(bg_v, bgt_v, hd):
    return (bg_v[:, hd:hd + 1], bg_v[:, A_HEADS + hd:A_HEADS + hd + 1],
            None if bgt_v is None else bgt_v[A_HEADS + hd:A_HEADS + hd + 1, :])


WY_ROWS = 512
SCAN_ROWS = 512
WY_GROUP = 8


def _dn_wy(q, k, v, bg, bgt, *, name, carry=None):
    t = q.shape[0]
    rows = WY_ROWS

    c_ins, c_in_specs, c_out_specs, c_outs, c_scratch = _carry_specs(carry)

    def body(*refs):
        q_ref, k_ref, v_ref, bg_ref, bgt_ref, u_ref, w_ref, tm_ref, qk_ref = _carried(carry, refs, 5, 4, t // rows)
        causal, strict, diag = _chunk_masks()
        eye = diag.astype(F32)
        for c0 in range(0, rows // CHUNK, WY_GROUP):
            items = [(c, hd) for c in range(c0, c0 + WY_GROUP) for hd in range(A_HEADS)]
            rs = lambda c: slice(c * CHUNK, (c + 1) * CHUNK)
            sl = lambda hd: slice(hd * LANE, (hd + 1) * LANE)
            hs = lambda hd: slice(hd * CHUNK, (hd + 1) * CHUNK)
            gates = [_chunk_gates(bg_ref[rs(c), :], bgt_ref[:, rs(c)], hd) for c, hd in items]
            dms = [jnp.exp(jnp.where(causal, gcol - grow, NEG)) for _, gcol, grow in gates]
            kbs = [k_ref[rs(c), sl(hd)] * g[0] for (c, hd), g in zip(items, gates)]
            a_list = [jnp.where(strict, _dot_nt(kb, k_ref[rs(c), sl(hd)]) * dm, 0.0)
                      for (c, hd), kb, dm in zip(items, kbs, dms)]
            for (c, hd), dm in zip(items, dms):
                qk_ref[rs(c), hs(hd)] = jnp.where(
                    causal, _dot_nt(q_ref[rs(c), sl(hd)], k_ref[rs(c), sl(hd)]) * dm, 0.0)
            tms = _tri_inv_many(a_list, eye)
            for (c, hd), g, kb, tmat in zip(items, gates, kbs, tms):
                tm_ref[rs(c), hs(hd)] = tmat
                u_ref[rs(c), sl(hd)] = _dot(tmat, v_ref[rs(c), sl(hd)] * g[0])
                w_ref[rs(c), sl(hd)] = _dot(tmat, kb * jnp.exp(g[1])).astype(BF16)

    blk = pl.BlockSpec((rows, A_WIDTH), lambda i: (i, 0))
    half = pl.BlockSpec((rows, A_HEADS * CHUNK), lambda i: (i, 0))
    outs = pl.pallas_call(
        body, name=name, grid=(t // rows,),
        in_specs=[blk, blk, blk, pl.BlockSpec((rows, LANE), lambda i: (i, 0)),
                  pl.BlockSpec((SUBLANE, rows), lambda i: (0, i))] + c_in_specs,
        out_specs=[blk, blk, half, half] + c_out_specs,
        out_shape=[jax.ShapeDtypeStruct((t, A_WIDTH), F32), jax.ShapeDtypeStruct((t, A_WIDTH), BF16),
                   jax.ShapeDtypeStruct((t, A_HEADS * CHUNK), F32),
                   jax.ShapeDtypeStruct((t, A_HEADS * CHUNK), F32)] + c_outs,
        scratch_shapes=c_scratch,
        compiler_params=_cp("arbitrary"))(q, k, v, bg, bgt, *c_ins)
    return outs[:4], outs[4:]


def _dn_scan_fwd(q, k, u, w, qk, bg, *, name, carry=None):
    t = q.shape[0]
    rows = SCAN_ROWS
    per = rows // CHUNK
    c_ins, c_in_specs, c_out_specs, c_outs, c_scratch = _carry_specs(carry)

    def body(*refs):
        q_ref, k_ref, u_ref, w_ref, qk_ref, bg_ref, o_ref, vn_ref, s_ref, state = _carried(carry, refs, 6, 3, t // rows)

        @pl.when(pl.program_id(0) == 0)
        def _():
            state[...] = jnp.zeros_like(state)

        heads = range(A_HEADS)
        sl = lambda hd: slice(hd * LANE, (hd + 1) * LANE)
        s_cur = [state[hd] for hd in heads]
        for c in range(per):
            rs = slice(c * CHUNK, (c + 1) * CHUNK)
            bg_v = bg_ref[rs, :]
            gcols = [_chunk_gates(bg_v, None, hd)[1] for hd in heads]
            glasts = [gc[CHUNK - 1:CHUNK, :] for gc in gcols]
            for hd in heads:
                s_ref[c, hd] = s_cur[hd].astype(BF16)
            vns = [u_ref[rs, sl(hd)] - _dot(w_ref[rs, sl(hd)], s_cur[hd]) for hd in heads]
            qss = [_dot(q_ref[rs, sl(hd)] * jnp.exp(gcols[hd]), s_cur[hd]) for hd in heads]
            s_cur = [s_cur[hd] * jnp.exp(glasts[hd])
                     + _dot_tn(k_ref[rs, sl(hd)] * jnp.exp(glasts[hd] - gcols[hd]), vns[hd]) for hd in heads]
            for hd in heads:
                vn_ref[rs, sl(hd)] = vns[hd]
                o_ref[rs, sl(hd)] = qss[hd] + _dot(qk_ref[rs, hd * CHUNK:(hd + 1) * CHUNK], vns[hd])
        for hd in heads:
            state[hd] = s_cur[hd]

    blk = pl.BlockSpec((rows, A_WIDTH), lambda i: (i, 0))
    half = pl.BlockSpec((rows, A_HEADS * CHUNK), lambda i: (i, 0))
    wide = jax.ShapeDtypeStruct((t, A_WIDTH), F32)
    outs = pl.pallas_call(
        body, name=name, grid=(t // rows,),
        in_specs=[blk, blk, blk, blk, half, pl.BlockSpec((rows, LANE), lambda i: (i, 0))] + c_in_specs,
        out_specs=[blk, blk, pl.BlockSpec((per, A_HEADS, LANE, LANE), lambda i: (i, 0, 0, 0))] + c_out_specs,
        out_shape=[wide, wide, jax.ShapeDtypeStruct((t // CHUNK, A_HEADS, LANE, LANE), BF16)] + c_outs,
        scratch_shapes=[pltpu.VMEM((A_HEADS, LANE, LANE), F32)] + c_scratch,
        compiler_params=_cp("arbitrary"))(q, k, u, w, qk, bg, *c_ins)
    return outs[:3], outs[3:]


def _stack_heads(ref, hk):
    return jnp.concatenate([ref[:, h * B_HEAD_DIM:(h + 1) * B_HEAD_DIM].astype(F32)
                            for h in range(hk * B_GROUP, (hk + 1) * B_GROUP)], axis=0)


def _swa_window():
    qi = lax.broadcasted_iota(jnp.int32, (BLOCK, BLOCK), 0)
    kj = lax.broadcasted_iota(jnp.int32, (BLOCK, BLOCK), 1)
    dist = jnp.where(kj > qi, qi + BLOCK - kj, qi - kj).astype(F32)
    rows = lax.broadcasted_iota(jnp.int32, (B_GROUP * BLOCK, BLOCK), 0)
    cols = lax.broadcasted_iota(jnp.int32, (B_GROUP * BLOCK, BLOCK), 1)
    return cols > jnp.bitwise_and(rows, BLOCK - 1), dist


def _swa_group_probs(q_ref, sk_ref, kp, kc, vp, vc, n_blk):
    hks = range(B_KV_HEADS)
    heads = lambda hk: range(hk * B_GROUP, (hk + 1) * B_GROUP)
    ksl = lambda hk: slice(hk * B_HEAD_DIM, (hk + 1) * B_HEAD_DIM)
    upper, dist = _swa_window()
    no_prev = jnp.where(n_blk > 0, 0.0, NEG)
    ones = jnp.ones((BLOCK, B_HEAD_DIM), BF16)
    with_ones = lambda v, hk: jnp.concatenate([v[:, ksl(hk)].astype(BF16), ones], axis=1)
    qs = [_stack_heads(q_ref, hk) * (B_HEAD_DIM ** -0.5) for hk in hks]
    sink = [jnp.concatenate([jnp.broadcast_to(sk_ref[h:h + 1, 0:1], (BLOCK, 1)) for h in heads(hk)], axis=0)
            for hk in hks]
    s = [jnp.where(upper, _dot_nt(qs[hk], kp[:, ksl(hk)]) + no_prev, _dot_nt(qs[hk], kc[:, ksl(hk)]))
         - jnp.concatenate([ALIBI[h] * dist for h in heads(hk)], axis=0) for hk in hks]
    m = [jnp.maximum(jnp.max(s[hk], axis=-1, keepdims=True), sink[hk]) for hk in hks]
    p = [jnp.exp(s[hk] - m[hk]) for hk in hks]
    p_up = [jnp.where(upper, p[hk], 0.0) for hk in hks]
    oe = [jnp.dot(p_up[hk].astype(BF16), with_ones(vp, hk), preferred_element_type=F32)
          + jnp.dot((p[hk] - p_up[hk]).astype(BF16), with_ones(vc, hk), preferred_element_type=F32) for hk in hks]
    ps = [jnp.exp(sink[hk] - m[hk]) for hk in hks]
    inv = [1.0 / (oe[hk][:, B_HEAD_DIM:B_HEAD_DIM + 1] + ps[hk]) for hk in hks]
    return upper, [(qs[hk], p[hk] * inv[hk], ps[hk] * inv[hk], oe[hk][:, :B_HEAD_DIM] * inv[hk]) for hk in hks]


def _swa_specs():
    qspec = lambda c0: pl.BlockSpec((BLOCK, B_WIDTH), lambda i: (i, c0 // B_WIDTH))
    cur = lambda c0: pl.BlockSpec((BLOCK, LANE), lambda i: (i, c0 // LANE))
    prev = lambda c0: pl.BlockSpec((BLOCK, LANE), lambda i: (jnp.maximum(i - 1, 0), c0 // LANE))
    return qspec, cur, prev


def _carried(carry, refs, n_in, n_out, steps):
    if carry is None:
        return refs
    ci, co = len(carry.ins), len(carry.outs)
    own = refs[:n_in] + refs[n_in + ci:n_in + ci + n_out] + refs[n_in + ci + n_out + co:len(refs) - 3]
    parts = refs[n_in:n_in + ci], refs[n_in + ci + n_out:n_in + ci + n_out + co], refs[len(refs) - 3:]

    @pl.when(pl.program_id(0) == 0)
    def _():
        carry.start(*parts)

    @pl.when(pl.program_id(0) == steps - 1)
    def _():
        carry.finish(*parts)

    return own


def _carry_specs(carry):
    if carry is None:
        return [], [], [], [], []
    return (list(carry.ins), [_ANY] * len(carry.ins), [_ANY] * len(carry.outs), list(carry.outs), carry.scratch())


def _swa_fwd(h, sinks_b, *, name, carry=None):
    t = h.shape[0]
    qspec, cur, prev = _swa_specs()
    c_ins, c_in_specs, c_out_specs, c_outs, c_scratch = _carry_specs(carry)

    def body(*refs):
        q_ref, kc_ref, kp_ref, vc_ref, vp_ref, sk_ref, o_ref = _carried(carry, refs, 6, 1, t // BLOCK)
        n_blk = pl.program_id(0)
        _, groups = _swa_group_probs(q_ref, sk_ref, kp_ref[...], kc_ref[...], vp_ref[...], vc_ref[...], n_blk)
        for hk, (_, _, _, o) in enumerate(groups):
            for g in range(B_GROUP):
                hq = hk * B_GROUP + g
                o_ref[:, hq * B_HEAD_DIM:(hq + 1) * B_HEAD_DIM] = o[g * BLOCK:(g + 1) * BLOCK]

    outs = pl.pallas_call(
        body, name=name, grid=(t // BLOCK,),
        in_specs=[qspec(C_QB), cur(C_KB), prev(C_KB), cur(C_VB), prev(C_VB),
                  pl.BlockSpec((B_Q_HEADS, LANE), lambda i: (0, 0))] + c_in_specs,
        out_specs=[pl.BlockSpec((BLOCK, B_WIDTH), lambda i: (i, 0))] + c_out_specs,
        out_shape=[jax.ShapeDtypeStruct((t, B_WIDTH), F32)] + c_outs,
        scratch_shapes=c_scratch,
        compiler_params=_cp("arbitrary"))(h, h, h, h, h, sinks_b, *c_ins)
    return outs[0], outs[1:]


def _rms_gate(o, za, nw):
    outs = []
    for hd in range(A_HEADS):
        oh = o[:, hd * LANE:(hd + 1) * LANE]
        r = lax.rsqrt(jnp.mean(oh * oh, -1, keepdims=True) + RMS_EPS)
        outs.append(oh * r * nw)
    return jnp.concatenate(outs, axis=1) * _silu(za)


def _out_ln(x, oa, ob, h, norm_w, w_out, ln_g, ln_b, *, tm, name, target=None):
    t = x.shape[0]
    last = target is not None

    def body(*refs):
        x_ref, oa_ref, ob_ref, za_ref, zb_ref, nw_ref, w_ref, g_ref, b_ref = refs[:9]
        xn_ref, mx_ref, r_ref = refs[9 + last:12 + last]
        ya = _rms_gate(oa_ref[...], za_ref[...].astype(F32), nw_ref[...])
        yb = ob_ref[...] * _silu(zb_ref[...].astype(F32))
        mixed = jnp.concatenate([ya, yb], axis=1).astype(BF16)
        mx_ref[...] = mixed
        r = DEEPNORM_ALPHA * x_ref[...] + jnp.dot(mixed, w_ref[...], preferred_element_type=F32)
        r_ref[...] = r
        mu = jnp.mean(r, -1, keepdims=True)
        xc = r - mu
        var = jnp.mean(xc * xc, -1, keepdims=True)
        xn = xc * lax.rsqrt(var + LN_EPS) * g_ref[...] + b_ref[...]
        if not last:
            xn_ref[...] = xn
            return
        loss_ref = refs[13]

        @pl.when(pl.program_id(0) == 0)
        def _():
            loss_ref[...] = jnp.zeros_like(loss_ref)

        err = xn - refs[9][...]
        xn_ref[...] = err * (1.0 / D_MODEL)
        loss_ref[...] += 0.5 / D_MODEL * jnp.sum(err * err)

    row = lambda w, c: pl.BlockSpec((tm, w), lambda i: (i, c))
    full = lambda a, b: pl.BlockSpec((a, b), lambda i: (0, 0))
    wide = jax.ShapeDtypeStruct((t, D_MODEL), F32)
    return pl.pallas_call(
        body, name=name, grid=(t // tm,),
        in_specs=[row(D_MODEL, 0), row(A_WIDTH, 0), row(B_WIDTH, 0), row(A_WIDTH, C_ZA // A_WIDTH),
                  row(B_WIDTH, C_ZB // B_WIDTH), full(1, LANE), full(D_MODEL, D_MODEL), full(1, D_MODEL),
                  full(1, D_MODEL)] + [row(D_MODEL, 0)] * last,
        out_specs=[row(D_MODEL, 0), row(D_MODEL, 0), row(D_MODEL, 0)] + [full(SUBLANE, LANE)] * last,
        out_shape=[wide, jax.ShapeDtypeStruct((t, D_MODEL), BF16), wide]
        + [jax.ShapeDtypeStruct((SUBLANE, LANE), F32)] * last,
        compiler_params=_cp("arbitrary" if last else "parallel"))(
        x, oa, ob, h, h, norm_w, w_out, ln_g, ln_b, *([target] if last else []))


def _layer_fwd(x, wt, conv_w, par, sinks_b, norm_w, w_out_bf, ln_g, ln_b, l, carries=None, target=None):
    carries = carries or {}
    h, got_in = _matmul_nt(x, wt, tm=512, name=f"in_proj_{l}", carry=carries.get("in_proj"))
    if callable(w_out_bf):
        w_out_bf = w_out_bf(got_in)
    (q, k, v, bg, bgt), got_pre = _dn_pre(h, conv_w, par, tt=512, name=f"dn_pre_{l}", carry=carries.get("dn_pre"))
    (u, w, tmat, qk), got_wy = _dn_wy(q, k, v, bg, bgt, name=f"dn_wy_{l}", carry=carries.get("dn_wy"))
    (oa, vn, s_all), got_scan = _dn_scan_fwd(q, k, u, w, qk, bg, name=f"dn_scan_{l}", carry=carries.get("dn_scan"))
    ob, got_swa = _swa_fwd(h, sinks_b, name=f"swa_fwd_{l}", carry=carries.get("swa"))
    xn, mixed, r, *loss = _out_ln(x, oa, ob, h, norm_w, w_out_bf, ln_g, ln_b, tm=512, name=f"out_ln_{l}", target=target)
    if loss:
        xn = (xn, loss[0])
    res = dict(x=x, h=h, q=q, k=k, v=v, bg=bg, bgt=bgt, w=w, tmat=tmat, qk=qk, vn=vn, oa=oa, s_all=s_all,
               mixed=mixed, r=r, w_out=w_out_bf)
    return xn, res, dict(in_proj=got_in, dn_pre=got_pre, dn_wy=got_wy, dn_scan=got_scan, swa=got_swa)


def _ln_out_bwd(dxn, r, mixed, ln_g, w_out, *, tm, name):
    t = dxn.shape[0]

    def body(dxn_ref, r_ref, mx_ref, g_ref, w_ref, dr_ref, dm_ref, dw_ref, dg_ref, db_ref):
        @pl.when(pl.program_id(0) == 0)
        def _():
            dw_ref[...] = jnp.zeros_like(dw_ref)
            dg_ref[...] = jnp.zeros_like(dg_ref)
            db_ref[...] = jnp.zeros_like(db_ref)

        rr = r_ref[...]
        xc = rr - jnp.mean(rr, -1, keepdims=True)
        rstd = lax.rsqrt(jnp.mean(xc * xc, -1, keepdims=True) + LN_EPS)
        xhat = xc * rstd
        dxn_v = dxn_ref[...]
        dxh = dxn_v * g_ref[...]
        dr = rstd * (dxh - jnp.mean(dxh, -1, keepdims=True) - xhat * jnp.mean(dxh * xhat, -1, keepdims=True))
        dr_ref[...] = dr
        dg_ref[...] += jnp.sum(dxn_v * xhat, axis=0, keepdims=True)
        db_ref[...] += jnp.sum(dxn_v, axis=0, keepdims=True)
        drb = dr.astype(BF16)
        dm_ref[...] = _dot_nt(drb, w_ref[...])
        dw_ref[...] += _dot_tn(mx_ref[...], drb)

    row = pl.BlockSpec((tm, D_MODEL), lambda i: (i, 0))
    full = lambda a, b: pl.BlockSpec((a, b), lambda i: (0, 0))
    big = jax.ShapeDtypeStruct((t, D_MODEL), F32)
    vec = jax.ShapeDtypeStruct((1, D_MODEL), F32)
    return pl.pallas_call(
        body, name=name, grid=(t // tm,),
        in_specs=[row, row, row, full(1, D_MODEL), full(D_MODEL, D_MODEL)],
        out_specs=[row, row, full(D_MODEL, D_MODEL), full(1, D_MODEL), full(1, D_MODEL)],
        out_shape=[big, big, jax.ShapeDtypeStruct((D_MODEL, D_MODEL), F32), vec, vec],
        compiler_params=_cp("arbitrary"))(dxn, r, mixed, ln_g, w_out)


def _dn_post_bwd(dm, oa, h, norm_w, *, tm, name):
    t = oa.shape[0]

    def body(dy_ref, o_ref, za_ref, nw_ref, do_ref, dza_ref, dnw_ref):
        @pl.when(pl.program_id(0) == 0)
        def _():
            dnw_ref[...] = jnp.zeros_like(dnw_ref)

        nw = nw_ref[...]
        dnw = jnp.zeros_like(nw)
        for hd in range(A_HEADS):
            sl = slice(hd * LANE, (hd + 1) * LANE)
            oh, za, dy = o_ref[:, sl], za_ref[:, sl].astype(F32), dy_ref[:, sl]
            rs = lax.rsqrt(jnp.mean(oh * oh, -1, keepdims=True) + RMS_EPS)
            nrm = oh * rs
            gate, dgate = _silu_and_grad(za)
            dza_ref[:, sl] = dy * nrm * nw * dgate
            dn = dy * gate
            dnw = dnw + jnp.sum(dn * nrm, axis=0, keepdims=True)
            dnn = dn * nw
            do_ref[:, sl] = rs * dnn - oh * (rs * rs * rs) * jnp.mean(dnn * oh, -1, keepdims=True)
        dnw_ref[...] += dnw

    row = lambda c: pl.BlockSpec((tm, A_WIDTH), lambda i: (i, c))
    wide = jax.ShapeDtypeStruct((t, A_WIDTH), F32)
    return pl.pallas_call(
        body, name=name, grid=(t // tm,),
        in_specs=[row(0), row(0), row(C_ZA // A_WIDTH), pl.BlockSpec((1, LANE), lambda i: (0, 0))],
        out_specs=[row(0), row(C_ZA // A_WIDTH), pl.BlockSpec((1, LANE), lambda i: (0, 0))],
        out_shape=[wide, jax.ShapeDtypeStruct((t, DH_MAIN), F32), jax.ShapeDtypeStruct((1, LANE), F32)],
        compiler_params=_cp("arbitrary"))(dm, oa, h, norm_w)


def _dn_scan_bwd(q, k, w, qk, bg, do, *, name):
    t = q.shape[0]
    rows = SCAN_ROWS
    per = rows // CHUNK
    n = t // rows

    def body(q_ref, k_ref, w_ref, qk_ref, bg_ref, do_ref, dvn_ref, ds_ref, dstate):
        @pl.when(pl.program_id(0) == 0)
        def _():
            dstate[...] = jnp.zeros_like(dstate)

        heads = range(A_HEADS)
        sl = lambda hd: slice(hd * LANE, (hd + 1) * LANE)
        ds_cur = [dstate[hd] for hd in heads]
        for c in reversed(range(per)):
            rs = slice(c * CHUNK, (c + 1) * CHUNK)
            bg_v = bg_ref[rs, :]
            gcols = [_chunk_gates(bg_v, None, hd)[1] for hd in heads]
            glasts = [gc[CHUNK - 1:CHUNK, :] for gc in gcols]
            for hd in heads:
                ds_ref[c, hd] = ds_cur[hd].astype(BF16)
            pdo = [_dot_tn(qk_ref[rs, hd * CHUNK:(hd + 1) * CHUNK], do_ref[rs, sl(hd)]) for hd in heads]
            qdo = [_dot_tn(q_ref[rs, sl(hd)] * jnp.exp(gcols[hd]), do_ref[rs, sl(hd)]) for hd in heads]
            dvns = [pdo[hd] + _dot(k_ref[rs, sl(hd)] * jnp.exp(glasts[hd] - gcols[hd]), ds_cur[hd]) for hd in heads]
            ds_cur = [qdo[hd] + jnp.exp(glasts[hd]) * ds_cur[hd] - _dot_tn(w_ref[rs, sl(hd)], dvns[hd])
                      for hd in heads]
            for hd in heads:
                dvn_ref[rs, sl(hd)] = dvns[hd]
        for hd in heads:
            dstate[hd] = ds_cur[hd]

    blk = pl.BlockSpec((rows, A_WIDTH), lambda i: (n - 1 - i, 0))
    return pl.pallas_call(
        body, name=name, grid=(n,),
        in_specs=[blk, blk, blk, pl.BlockSpec((rows, A_HEADS * CHUNK), lambda i: (n - 1 - i, 0)),
                  pl.BlockSpec((rows, LANE), lambda i: (n - 1 - i, 0)), blk],
        out_specs=[blk, pl.BlockSpec((per, A_HEADS, LANE, LANE), lambda i: (n - 1 - i, 0, 0, 0))],
        out_shape=[jax.ShapeDtypeStruct((t, A_WIDTH), F32),
                   jax.ShapeDtypeStruct((t // CHUNK, A_HEADS, LANE, LANE), BF16)],
        scratch_shapes=[pltpu.VMEM((A_HEADS, LANE, LANE), F32)],
        compiler_params=_cp("arbitrary"))(q, k, w, qk, bg, do)


def _dn_chunk_bwd(q, k, v, vn, tmat, qk, bg, bgt, s_all, ds_all, dvn, do, *, name, carry=None):
    t = q.shape[0]
    rows = WY_ROWS
    per = rows // CHUNK

    c_ins, c_in_specs, c_out_specs, c_outs, c_scratch = _carry_specs(carry)

    def body(*refs):
        (q_ref, k_ref, v_ref, vn_ref, tm_ref, qk_ref, bg_ref, bgt_ref, s_ref, ds_ref, dvn_ref, do_ref,
         dq_ref, dk_ref, dv_ref, dbg_ref, dbgt_ref) = _carried(carry, refs, 12, 5, t // rows)
        causal, strict, _ = _chunk_masks()
        lane = lax.broadcasted_iota(jnp.int32, (CHUNK, LANE), 1)
        rowi = lax.broadcasted_iota(jnp.int32, (CHUNK, 1), 0)
        sub = lax.broadcasted_iota(jnp.int32, (SUBLANE, CHUNK), 0)
        rs = lambda c: slice(c * CHUNK, (c + 1) * CHUNK)
        sl = lambda hd: slice(hd * LANE, (hd + 1) * LANE)
        hs = lambda hd: slice(hd * CHUNK, (hd + 1) * CHUNK)
        for c0 in range(0, per, WY_GROUP):
            items = [(c, hd) for c in range(c0, c0 + WY_GROUP) for hd in range(A_HEADS)]
            at = lambda ref: [ref[rs(c), sl(hd)] for c, hd in items]
            qs, ks, vs, dos, vns, dvns = at(q_ref), at(k_ref), at(v_ref), at(do_ref), at(vn_ref), at(dvn_ref)
            tmhs = [tm_ref[rs(c), hs(hd)] for c, hd in items]
            ps = [qk_ref[rs(c), hs(hd)] for c, hd in items]
            gates = [_chunk_gates(bg_ref[rs(c), :], bgt_ref[:, rs(c)], hd) for c, hd in items]
            betas = [g[0] for g in gates]
            gcols = [g[1] for g in gates]
            dmats = [jnp.exp(jnp.where(causal, g[1] - g[2], NEG)) for g in gates]
            es = [jnp.exp(gc) for gc in gcols]
            glasts = [gc[CHUNK - 1:CHUNK, :] for gc in gcols]
            eks = [jnp.exp(gl - gc) for gl, gc in zip(glasts, gcols)]
            kbs = [kh * b for kh, b in zip(ks, betas)]
            vbs = [vh * b for vh, b in zip(vs, betas)]
            kbes = [kb * e for kb, e in zip(kbs, es)]

            a_s = [jnp.where(strict, _dot_nt(kb, kh) * dm, 0.0) for kb, kh, dm in zip(kbs, ks, dmats)]
            dps = [jnp.where(causal, _dot_nt(doh, vnh), 0.0) for doh, vnh in zip(dos, vns)]
            rows2 = lambda a, b: jnp.concatenate([a, b], axis=0)
            cols2 = lambda a, b: jnp.concatenate([a, b], axis=1)
            by_s = [_dot_nt(rows2(doh, dvnh), s_ref[c, hd]) for doh, dvnh, (c, hd) in zip(dos, dvns, items)]
            dqds = [m[:CHUNK] for m in by_s]
            dws = [-m[CHUNK:] for m in by_s]
            dkds = [_dot_nt(vnh, ds_ref[c, hd]) for vnh, (c, hd) in zip(vns, items)]
            dgts = [jnp.sum(s_ref[c, hd].astype(F32) * ds_ref[c, hd].astype(F32), keepdims=True) for c, hd in items]
            pairs = [cols2(dvnh, dw) for dvnh, dw in zip(dvns, dws)]
            by_t = [_dot_tn(tmh, pr) for tmh, pr in zip(tmhs, pairs)]
            dvbs = [m[:, :LANE] for m in by_t]
            dkbes = [m[:, LANE:] for m in by_t]
            dts = [_dot_nt(pr, cols2(vb, kbe)) for pr, vb, kbe in zip(pairs, vbs, kbes)]
            xs = [_dot_nt(dt, tmh) for dt, tmh in zip(dts, tmhs)]
            das = [jnp.where(strict, -_dot_tn(tmh, x), 0.0) for tmh, x in zip(tmhs, xs)]
            dmas = [da * dm for da, dm in zip(das, dmats)]
            dmps = [dp * dm for dp, dm in zip(dps, dmats)]
            stacked = [rows2(dma, dmp) for dma, dmp in zip(dmas, dmps)]
            by_k = [_dot(st, kh) for st, kh in zip(stacked, ks)]
            dkbs = [m[:CHUNK] + dkbe * e for m, dkbe, e in zip(by_k, dkbes, es)]
            for i, (c, hd) in enumerate(items):
                dq_ref[rs(c), sl(hd)] = by_k[i][CHUNK:] + dqds[i] * es[i]
                dk_ref[rs(c), sl(hd)] = (_dot_tn(stacked[i], rows2(kbs[i], qs[i])) + dkds[i] * eks[i]
                                         + dkbs[i] * betas[i])
                dv_ref[rs(c), sl(hd)] = dvbs[i] * betas[i]
            for c in range(c0, c0 + WY_GROUP):
                acc = jnp.zeros((CHUNK, LANE), F32)
                acc_t = jnp.zeros((SUBLANE, CHUNK), F32)
                for i, (ci, hd) in enumerate(items):
                    if ci != c:
                        continue
                    gmat = das[i] * a_s[i] + dps[i] * ps[i]
                    rk = jnp.sum(dkds[i] * ks[i], -1, keepdims=True) * eks[i]
                    de = jnp.sum(dqds[i] * qs[i] + dkbes[i] * kbs[i], -1, keepdims=True)
                    dglast = jnp.sum(rk, keepdims=True) + dgts[i] * jnp.exp(glasts[i])
                    dgc = (jnp.sum(gmat, -1, keepdims=True) + de * es[i] - rk
                           + jnp.where(rowi == CHUNK - 1, dglast, 0.0))
                    dbeta = jnp.sum(dkbs[i] * ks[i] + dvbs[i] * vs[i], -1, keepdims=True)
                    acc = acc + jnp.where(lane == hd, dbeta, 0.0) + jnp.where(lane == A_HEADS + hd, dgc, 0.0)
                    acc_t = acc_t + jnp.where(sub == A_HEADS + hd, -jnp.sum(gmat, axis=0, keepdims=True), 0.0)
                dbg_ref[rs(c), :] = acc
                dbgt_ref[:, rs(c)] = acc_t

    blk = pl.BlockSpec((rows, A_WIDTH), lambda i: (i, 0))
    half = pl.BlockSpec((rows, A_HEADS * CHUNK), lambda i: (i, 0))
    col = pl.BlockSpec((rows, LANE), lambda i: (i, 0))
    rowf = pl.BlockSpec((SUBLANE, rows), lambda i: (0, i))
    st = pl.BlockSpec((per, A_HEADS, LANE, LANE), lambda i: (i, 0, 0, 0))
    wide = jax.ShapeDtypeStruct((t, A_WIDTH), F32)
    outs = pl.pallas_call(
        body, name=name, grid=(t // rows,),
        in_specs=[blk, blk, blk, blk, half, half, col, rowf, st, st, blk, blk] + c_in_specs,
        out_specs=[blk, blk, blk, col, rowf] + c_out_specs,
        out_shape=[wide, wide, wide, jax.ShapeDtypeStruct((t, LANE), F32),
                   jax.ShapeDtypeStruct((SUBLANE, t), F32)] + c_outs,
        scratch_shapes=c_scratch,
        compiler_params=_cp("arbitrary"))(q, k, v, vn, tmat, qk, bg, bgt, s_all, ds_all, dvn, do, *c_ins)
    return outs[:5], outs[5:]


def _dn_pre_bwd(h, conv_w, par, dq, dk, dv, dbg, dbgt, *, tt, name):
    t = h.shape[0]
    cw = 3 * A_WIDTH
    hb = tt // HALO

    def body(pre_ref, halo_ref, bgi_ref, cw_ref, par_ref, dq_ref, dk_ref, dv_ref, dbg_ref, dbgt_ref,
             dc_ref, dbgi_ref, dpar_ref):
        i = pl.program_id(0)

        @pl.when(i == 0)
        def _():
            dpar_ref[...] = jnp.zeros_like(dpar_ref)

        cur = pre_ref[...].astype(F32)
        before = jnp.where(i > 0, halo_ref[...].astype(F32)[HALO - SUBLANE:], 0.0)
        c = _conv_fwd(cur, before, cw_ref[...])
        s, ds = _silu_and_grad(c)
        for hd in range(A_HEADS):
            sl = slice(hd * LANE, (hd + 1) * LANE)
            for base, d_ref, scale in ((0, dq_ref, A_HEAD_DIM ** -0.5), (A_WIDTH, dk_ref, 1.0)):
                csl = slice(base + hd * LANE, base + (hd + 1) * LANE)
                tq = s[:, base + hd * LANE:base + (hd + 1) * LANE]
                dy = d_ref[:, sl]
                rq = lax.rsqrt(jnp.sum(tq * tq, -1, keepdims=True) + L2_EPS)
                dtq = scale * (rq * dy - tq * (rq * rq * rq) * jnp.sum(dy * tq, -1, keepdims=True))
                dc_ref[:, csl] = dtq * ds[:, base + hd * LANE:base + (hd + 1) * LANE]
        dc_ref[:, 2 * A_WIDTH:] = dv_ref[...] * ds[:, 2 * A_WIDTH:]
        raw = bgi_ref[...].astype(F32)
        lane = lax.broadcasted_iota(jnp.int32, raw.shape, 1)
        is_b = lane < A_HEADS
        is_a = (lane >= A_HEADS) & (lane < 2 * A_HEADS)
        rows_t = jnp.concatenate([dbgt_ref[...], jnp.zeros((LANE - SUBLANE, tt), F32)], axis=0)
        dbg_v = dbg_ref[...] + jnp.where(is_a, jnp.transpose(rows_t), 0.0)
        dbg_v = jnp.where(is_a, _dot_hi(_chunk_tri(tt, lower=False), jnp.where(is_a, dbg_v, 0.0)), dbg_v)
        beta = _sigmoid(raw)
        z = raw + par_ref[1:2, :]
        neg_ea = -jnp.exp(par_ref[0:1, :])
        g = neg_ea * _softplus(z)
        da = dbg_v * neg_ea * _sigmoid(z)
        dbgi_ref[...] = jnp.where(is_b, dbg_v * beta * (1.0 - beta), jnp.where(is_a, da, 0.0))
        dpar_ref[0:1, :] += jnp.sum(jnp.where(is_a, dbg_v * g, 0.0), axis=0, keepdims=True)
        dpar_ref[1:2, :] += jnp.sum(jnp.where(is_a, da, 0.0), axis=0, keepdims=True)

    wide = pl.BlockSpec((tt, A_WIDTH), lambda i: (i, 0))
    return pl.pallas_call(
        body, name=name, grid=(t // tt,),
        in_specs=[pl.BlockSpec((tt, cw), lambda i: (i, 0)),
                  pl.BlockSpec((HALO, cw), lambda i: (jnp.maximum(i * hb - 1, 0), 0)),
                  pl.BlockSpec((tt, LANE), lambda i: (i, C_BG // LANE)),
                  pl.BlockSpec((CONV_K, cw), lambda i: (0, 0)),
                  pl.BlockSpec((SUBLANE, LANE), lambda i: (0, 0)),
                  wide, wide, wide, pl.BlockSpec((tt, LANE), lambda i: (i, 0)),
                  pl.BlockSpec((SUBLANE, tt), lambda i: (0, i))],
        out_specs=[pl.BlockSpec((tt, cw), lambda i: (i, 0)), pl.BlockSpec((tt, LANE), lambda i: (i, 0)),
                   pl.BlockSpec((SUBLANE, LANE), lambda i: (0, 0))],
        out_shape=[jax.ShapeDtypeStruct((t, cw), F32), jax.ShapeDtypeStruct((t, LANE), F32),
                   jax.ShapeDtypeStruct((SUBLANE, LANE), F32)],
        compiler_params=_cp("arbitrary"))(h, h, h, conv_w, par, dq, dk, dv, dbg, dbgt)


def _conv_bwd(dc, h, conv_w, dh, *, tt, name):
    t = dc.shape[0]
    cw = 3 * A_WIDTH
    hb = tt // HALO
    nb = t // tt

    def body(dc_ref, after_ref, pre_ref, before_ref, cw_ref, dh_in_ref, dpre_ref, dcw_ref):
        i = pl.program_id(0)

        @pl.when(i == 0)
        def _():
            dcw_ref[...] = jnp.zeros_like(dcw_ref)

        dcv = dc_ref[...]
        after = jnp.where(i < nb - 1, after_ref[...], 0.0)
        cur = pre_ref[...].astype(F32)
        before = jnp.where(i > 0, before_ref[...].astype(F32)[HALO - SUBLANE:], 0.0)
        w = cw_ref[...]
        acc = dcv * w[CONV_K - 1:CONV_K, :]
        dcw_ref[CONV_K - 1:CONV_K, :] += jnp.sum(dcv * cur, axis=0, keepdims=True)
        for s in range(1, CONV_K):
            j = CONV_K - 1 - s
            acc = acc + _shift_up(dcv, after, s) * w[j:j + 1, :]
            dcw_ref[j:j + 1, :] += jnp.sum(dcv * _shift_down(cur, before, s), axis=0, keepdims=True)
        dpre_ref[...] = acc

    return pl.pallas_call(
        body, name=name, grid=(nb,),
        in_specs=[pl.BlockSpec((tt, cw), lambda i: (i, 0)),
                  pl.BlockSpec((SUBLANE, cw), lambda i: (jnp.minimum((i + 1) * (tt // SUBLANE), t // SUBLANE - 1), 0)),
                  pl.BlockSpec((tt, cw), lambda i: (i, 0)),
                  pl.BlockSpec((HALO, cw), lambda i: (jnp.maximum(i * hb - 1, 0), 0)),
                  pl.BlockSpec((CONV_K, cw), lambda i: (0, 0)), _ANY],
        out_specs=[pl.BlockSpec((tt, cw), lambda i: (i, 0)), pl.BlockSpec((SUBLANE, cw), lambda i: (0, 0))],
        out_shape=[jax.ShapeDtypeStruct(dh.shape, F32), jax.ShapeDtypeStruct((SUBLANE, cw), F32)],
        input_output_aliases={5: 0},
        compiler_params=_cp("arbitrary"))(dc, dc, h, h, conv_w, dh)


def _swa_bwd(h, dm, sinks_b, dh, *, name, carry=None):
    t = h.shape[0]
    qspec, cur, prev = _swa_specs()
    c_ins, c_in_specs, c_out_specs, c_outs, c_scratch = _carry_specs(carry)

    def body(*refs):
        (q_ref, kc_ref, kp_ref, vc_ref, vp_ref, zb_ref, dy_ref, sk_ref, dh_in_ref,
         dqz_ref, dk_ref, dv_ref, dsk_ref) = _carried(carry, refs, 9, 4, t // BLOCK)
        n_blk = pl.program_id(0)

        @pl.when(n_blk == 0)
        def _():
            dk_ref[...] = jnp.zeros_like(dk_ref)
            dv_ref[...] = jnp.zeros_like(dv_ref)
            dsk_ref[...] = jnp.zeros_like(dsk_ref)

        kp, kc, vp, vc = kp_ref[...], kc_ref[...], vp_ref[...], vc_ref[...]
        scale = B_HEAD_DIM ** -0.5
        hks = range(B_KV_HEADS)
        ksl = lambda hk: slice(hk * B_HEAD_DIM, (hk + 1) * B_HEAD_DIM)
        upper, groups = _swa_group_probs(q_ref, sk_ref, kp, kc, vp, vc, n_blk)
        zbs = [_stack_heads(zb_ref, hk) for hk in hks]
        dys = [_stack_heads(dy_ref, hk) for hk in hks]
        gates = [_silu_and_grad(zbs[hk]) for hk in hks]
        dos = [dys[hk] * gates[hk][0] for hk in hks]
        deltas = [jnp.sum(dos[hk] * groups[hk][3], -1, keepdims=True) for hk in hks]
        dps = [jnp.where(upper, _dot_nt(dos[hk], vp[:, ksl(hk)]), _dot_nt(dos[hk], vc[:, ksl(hk)])) for hk in hks]
        dss = [groups[hk][1] * (dps[hk] - deltas[hk]) for hk in hks]
        ds_up = [jnp.where(upper, dss[hk], 0.0) for hk in hks]
        ds_lo = [dss[hk] - ds_up[hk] for hk in hks]
        p_up = [jnp.where(upper, groups[hk][1], 0.0) for hk in hks]
        p_lo = [groups[hk][1] - p_up[hk] for hk in hks]
        dqs = [(_dot(ds_up[hk], kp[:, ksl(hk)]) + _dot(ds_lo[hk], kc[:, ksl(hk)])) * scale for hk in hks]
        dk_prev = [_dot_tn(ds_up[hk], groups[hk][0]) for hk in hks]
        dk_cur = [_dot_tn(ds_lo[hk], groups[hk][0]) for hk in hks]
        dv_prev = [_dot_tn(p_up[hk], dos[hk]) for hk in hks]
        dv_cur = [_dot_tn(p_lo[hk], dos[hk]) for hk in hks]
        for hk in hks:
            dzb = dys[hk] * groups[hk][3] * gates[hk][1]
            dsink = groups[hk][2] * deltas[hk]
            for g in range(B_GROUP):
                hq = hk * B_GROUP + g
                rows = slice(g * BLOCK, (g + 1) * BLOCK)
                qsl = slice(hq * B_HEAD_DIM, (hq + 1) * B_HEAD_DIM)
                dqz_ref[:, qsl] = dqs[hk][rows]
                dqz_ref[:, B_WIDTH + hq * B_HEAD_DIM:B_WIDTH + (hq + 1) * B_HEAD_DIM] = dzb[rows]
                dsk_ref[hq:hq + 1, :] += -jnp.sum(dsink[rows], keepdims=True)
        at_cur = pl.ds(pl.multiple_of(n_blk * BLOCK, BLOCK), BLOCK)
        at_prev = pl.ds(pl.multiple_of(jnp.maximum(n_blk - 1, 0) * BLOCK, BLOCK), BLOCK)
        dk_ref[at_prev, :] += jnp.concatenate(dk_prev, axis=1)
        dv_ref[at_prev, :] += jnp.concatenate(dv_prev, axis=1)
        dk_ref[at_cur, :] += jnp.concatenate(dk_cur, axis=1)
        dv_ref[at_cur, :] += jnp.concatenate(dv_cur, axis=1)

    narrow = jax.ShapeDtypeStruct((t, B_KV_WIDTH), F32)
    res = lambda a, b: pl.BlockSpec((a, b), lambda i: (0, 0))
    outs = pl.pallas_call(
        body, name=name, grid=(t // BLOCK,),
        in_specs=[qspec(C_QB), cur(C_KB), prev(C_KB), cur(C_VB), prev(C_VB), qspec(C_ZB),
                  pl.BlockSpec((BLOCK, B_WIDTH), lambda i: (i, 1)), res(B_Q_HEADS, LANE), _ANY] + c_in_specs,
        out_specs=[pl.BlockSpec((BLOCK, 2 * B_WIDTH), lambda i: (i, C_QB // (2 * B_WIDTH))),
                   res(t, B_KV_WIDTH), res(t, B_KV_WIDTH), res(B_Q_HEADS, LANE)] + c_out_specs,
        out_shape=[jax.ShapeDtypeStruct(dh.shape, F32), narrow, narrow,
                   jax.ShapeDtypeStruct((B_Q_HEADS, LANE), F32)] + c_outs,
        scratch_shapes=c_scratch,
        input_output_aliases={8: 0},
        compiler_params=_cp("arbitrary"))(h, h, h, h, h, h, dm, sinks_b, dh, *c_ins)
    return outs[:4], outs[4:]


def _in_proj_dw(dh_main, dh_tail, x, *, tk, name):
    t, n = x.shape

    def body(a_ref, t_ref, x_ref, o_ref, ot_ref):
        @pl.when(pl.program_id(0) == 0)
        def _():
            o_ref[...] = jnp.zeros_like(o_ref)
            ot_ref[...] = jnp.zeros_like(ot_ref)

        xb = x_ref[...].astype(BF16)
        o_ref[...] += _dot_tn(a_ref[...], xb)
        ot_ref[...] += _dot_tn(t_ref[...], xb)

    row = lambda a: pl.BlockSpec((tk, a.shape[1]), lambda kk: (kk, 0))
    acc = lambda a: pl.BlockSpec((a.shape[1], n), lambda kk: (0, 0))
    return pl.pallas_call(
        body, name=name, grid=(t // tk,), in_specs=[row(dh_main), row(dh_tail), row(x)],
        out_specs=[acc(dh_main), acc(dh_tail)],
        out_shape=[jax.ShapeDtypeStruct((a.shape[1], n), F32) for a in (dh_main, dh_tail)],
        compiler_params=_cp("arbitrary"))(dh_main, dh_tail, x)


def _in_proj_dx(dh_main, dh_tail, wt, dr, *, tm, name, carry=None):
    t, n_main = dh_main.shape
    n_tail = dh_tail.shape[1]
    c_ins, c_in_specs, c_out_specs, c_outs, c_scratch = _carry_specs(carry)

    def body(*refs):
        a_ref, t_ref, wa_ref, wt_ref, r_ref, o_ref = _carried(carry, refs, 5, 1, t // tm)
        o_ref[...] = _dot(a_ref[...], wa_ref[...]) + _dot(t_ref[...], wt_ref[...]) + DEEPNORM_ALPHA * r_ref[...]

    row = lambda w: pl.BlockSpec((tm, w), lambda i: (i, 0))
    outs = pl.pallas_call(
        body, name=name, grid=(t // tm,),
        in_specs=[row(n_main), row(n_tail), pl.BlockSpec((n_main, D_MODEL), lambda i: (0, 0)),
                  pl.BlockSpec((n_tail, D_MODEL), lambda i: (n_main // n_tail, 0)), row(D_MODEL)] + c_in_specs,
        out_specs=[row(D_MODEL)] + c_out_specs,
        out_shape=[jax.ShapeDtypeStruct((t, D_MODEL), F32)] + c_outs,
        scratch_shapes=c_scratch,
        compiler_params=_cp("arbitrary"))(dh_main, dh_tail, wt, wt, dr, *c_ins)
    return outs[0], outs[1:]


def _layer_bwd(dxn, res, wt, conv_w, par, sinks_b, norm_w, w_out_bf, ln_g, l, carries=None, carry_dx=None):
    carries = carries or {}
    w_out_bf = res["w_out"]
    dr, dm, dw_out, dln_g, dln_b = _ln_out_bwd(dxn, res["r"], res["mixed"], ln_g, w_out_bf, tm=512, name=f"ln_out_bwd_{l}")
    h = res["h"]
    do, dh, dnw = _dn_post_bwd(dm, res["oa"], h, norm_w, tm=512, name=f"dn_post_bwd_{l}")
    dvn, ds_all = _dn_scan_bwd(res["q"], res["k"], res["w"], res["qk"], res["bg"], do, name=f"dn_scan_bwd_{l}")
    (dq, dk, dv, dbg, dbgt), got_chunk = _dn_chunk_bwd(
        res["q"], res["k"], res["v"], res["vn"], res["tmat"], res["qk"], res["bg"], res["bgt"], res["s_all"], ds_all,
        dvn, do, name=f"dn_chunk_bwd_{l}", carry=carries.get("dn_chunk"))
    dc, dbgi, dpar = _dn_pre_bwd(h, conv_w, par, dq, dk, dv, dbg, dbgt, tt=512, name=f"dn_pre_bwd_{l}")
    dh, dcw = _conv_bwd(dc, h, conv_w, dh, tt=512, name=f"conv_bwd_{l}")
    (dh, dkb, dvb, dsk), got_swa = _swa_bwd(h, dm, sinks_b, dh, name=f"swa_bwd_{l}", carry=carries.get("swa"))
    carried = dict(dn_chunk=got_chunk, swa=got_swa)
    dh_tail = jnp.concatenate([dkb, dvb, dbgi], axis=1)
    dwt_main, dwt_tail = _in_proj_dw(dh, dh_tail, res["x"], tk=512, name=f"in_proj_dw_{l}")
    grads = dict(w_in=(dwt_main, dwt_tail), conv_w=dcw[:CONV_K], a_log=dpar[0, A_HEADS:2 * A_HEADS],
                 dt_bias=dpar[1, A_HEADS:2 * A_HEADS], norm_w=dnw[0], sinks=dsk[:, 0], w_out=dw_out,
                 ln_g=dln_g[0], ln_b=dln_b[0])
    dx, carried_dx = _in_proj_dx(dh, dh_tail, wt, dr, tm=512, name=f"in_proj_dx_{l}",
                                 carry=None if carry_dx is None else carry_dx(grads))
    return dx, grads, carried, carried_dx


def _layer_args(wt, conv_w, a_log, dt_bias, sinks, norm_w, w_out_bf):
    return (wt, conv_w, _gate_params(a_log, dt_bias), jnp.broadcast_to(sinks[:, None], (B_Q_HEADS, LANE)),
            norm_w[None], w_out_bf)


def _local_step(x, target, args0, args1, ln_g, ln_b, gathers=None, reduce1=None, reduce0=None):
    assert DEPTH == 2
    x1, res0, got = _layer_fwd(x, *args0, ln_g[0][None], ln_b[0][None], 0, carries=gathers)
    if gathers is not None:
        args1 = args1(got)
    (dx, loss_tile), res1, _ = _layer_fwd(x1, *args1, ln_g[1][None], ln_b[1][None], 1, target=target)
    dx, grads1, _, _ = _layer_bwd(dx, res1, *args1, ln_g[1][None], 1)
    carries = None if reduce1 is None else reduce1(grads1)
    carry_dx = None if reduce0 is None else (lambda grads0: reduce0(grads0, grads1, loss_tile))
    dx, grads0, landed1, landed0 = _layer_bwd(dx, res0, *args0, ln_g[0][None], 0, carries=carries, carry_dx=carry_dx)
    return loss_tile, dx, [grads0, grads1], landed1, landed0


_ANY = pl.BlockSpec(memory_space=pl.ANY)
_MESH = pl.DeviceIdType.MESH


HALF = D_MODEL // 2


class _Exchange:
    def __init__(self, ins, outs, n_remote, n_local, plan):
        self.ins, self.outs, self.n_remote, self.n_local, self.plan = tuple(ins), tuple(outs), n_remote, n_local, plan

    def scratch(self):
        return [pltpu.SemaphoreType.DMA((self.n_remote,)), pltpu.SemaphoreType.DMA((self.n_remote,)),
                pltpu.SemaphoreType.DMA((max(self.n_local, 1),))]

    def _copies(self, in_refs, out_refs, sems, arriving):
        send_sems, recv_sems, local_sems = sems
        local, sends, recvs = self.plan(in_refs, out_refs)
        loc = [pltpu.make_async_copy(s, d, local_sems.at[i]) for i, (s, d) in enumerate(local)]
        rem = [pltpu.make_async_remote_copy(src_ref=s, dst_ref=recvs[i] if arriving else d, send_sem=send_sems.at[i],
                                            recv_sem=recv_sems.at[i], device_id=peer, device_id_type=_MESH)
               for i, (s, d, peer) in enumerate(sends)]
        return loc, rem

    def start(self, in_refs, out_refs, sems):
        loc, rem = self._copies(in_refs, out_refs, sems, arriving=False)
        for cp in loc + rem:
            cp.start()

    def finish(self, in_refs, out_refs, sems):
        loc, rem = self._copies(in_refs, out_refs, sems, arriving=True)
        for cp in rem:
            cp.wait_recv()
        for cp in rem:
            cp.wait_send()
        for cp in loc:
            cp.wait()


def _run_exchange(ex, *, name):
    n_in, n_out = len(ex.ins), len(ex.outs)

    def body(*refs):
        parts = refs[:n_in], refs[n_in:n_in + n_out], refs[n_in + n_out:]
        ex.start(*parts)
        ex.finish(*parts)

    return pl.pallas_call(body, name=name, in_specs=[_ANY] * n_in, out_specs=[_ANY] * n_out, out_shape=list(ex.outs),
                          scratch_shapes=ex.scratch())(*ex.ins)


def _place():
    x, y, c = lax.axis_index("x"), lax.axis_index("y"), lax.axis_index("c")
    return x, y, c, [(1 - x, y), (x, 1 - y), (1 - x, 1 - y)]


def _gather_exchange(arrays):
    n = len(arrays)

    def plan(src, dst):
        x, y, c, chips = _place()
        me = 2 * x + y
        local = [(src[k], dst[k].at[me]) for k in range(n)]
        sends = [(src[k], dst[k].at[me], (px, py, c)) for k in range(n) for px, py in chips]
        recvs = [dst[k].at[2 * px + py] for k in range(n) for px, py in chips]
        return local, sends, recvs

    return _Exchange(arrays, [jax.ShapeDtypeStruct((N_SHARD,) + a.shape, a.dtype) for a in arrays], 3 * n, n, plan)


def _gather_two_level(pack, conv_w, *, name):
    rows = pack.shape[0]
    part_rows = rows // 2

    def body(pack_ref, conv_ref, land_ref, conv_land_ref, send1, recv1, send2, recv2, csend, crecv, local_sems):
        x, y, c, chips = _place()
        me = 2 * x + y
        sibling = (x, y, 1 - c)
        part = lambda core: pl.ds(pl.multiple_of(core * part_rows, 16), part_rows)
        remote = lambda src, dst, ss, rs, to: pltpu.make_async_remote_copy(
            src_ref=src, dst_ref=dst, send_sem=ss, recv_sem=rs, device_id=to, device_id_type=_MESH)
        local = [pltpu.make_async_copy(pack_ref, land_ref.at[me], local_sems.at[0]),
                 pltpu.make_async_copy(conv_ref, conv_land_ref.at[me], local_sems.at[1])]
        for cp in local:
            cp.start()
        first = [remote(pack_ref.at[part(c)], land_ref.at[me, part(c)], send1.at[j], recv1.at[j], (px, py, c))
                 for j, (px, py) in enumerate(chips)]
        convs = [remote(conv_ref, conv_land_ref.at[me], csend.at[j], crecv.at[j], (px, py, c))
                 for j, (px, py) in enumerate(chips)]
        for cp in first + convs:
            cp.start()
        passed = []
        for j, (px, py) in enumerate(chips):
            slot = 2 * px + py
            remote(pack_ref.at[part(c)], land_ref.at[slot, part(c)], send1.at[j], recv1.at[j], (px, py, c)).wait_recv()
            cp = remote(land_ref.at[slot, part(c)], land_ref.at[slot, part(c)], send2.at[j], recv2.at[j], sibling)
            cp.start()
            passed.append(cp)
        for j, (px, py) in enumerate(chips):
            slot = 2 * px + py
            remote(land_ref.at[slot, part(1 - c)], land_ref.at[slot, part(1 - c)], send2.at[j], recv2.at[j],
                   sibling).wait_recv()
            remote(conv_ref, conv_land_ref.at[slot], csend.at[j], crecv.at[j], (px, py, c)).wait_recv()
        for cp in first + convs + passed:
            cp.wait_send()
        for cp in local:
            cp.wait()

    sems = [pltpu.SemaphoreType.DMA((3,))] * 6 + [pltpu.SemaphoreType.DMA((2,))]
    return pl.pallas_call(
        body, name=name, in_specs=[_ANY, _ANY], out_specs=[_ANY, _ANY],
        out_shape=[jax.ShapeDtypeStruct((N_SHARD,) + pack.shape, pack.dtype),
                   jax.ShapeDtypeStruct((N_SHARD,) + conv_w.shape, conv_w.dtype)],
        scratch_shapes=sems)(pack, conv_w)


def _half(core):
    return pl.ds(pl.multiple_of(core * HALF, HALF), HALF)


def _reduce_scatter_exchange(g, row0, rows):
    def plan(src, dst):
        x, y, c, chips = _place()
        peers = [(px, py, c if t == 0 else 1 - c) for px, py in chips for t in (0, 1)] + [(x, y, 1 - c)]
        sends = [(src[0].at[2 * px + py, pl.ds(row0, rows), _half(pc)], dst[0].at[k], (px, py, pc))
                 for k, (px, py, pc) in enumerate(peers)]
        return [], sends, [dst[0].at[k] for k in range(7)]

    return _Exchange([g], [jax.ShapeDtypeStruct((7, rows, HALF), g.dtype)], 7, 0, plan)


def _pair_window_exchange(g):
    def plan(src, dst):
        x, y, c, _ = _place()
        return [], [(src[0].at[:, :, _half(1 - c)], dst[0], (x, y, 1 - c))], [dst[0]]

    return _Exchange([g], [jax.ShapeDtypeStruct(g.shape[:2] + (HALF,), g.dtype)], 1, 0, plan)


def _chip_scatter_exchange(p, small):
    def plan(src, dst):
        x, y, c, chips = _place()
        mine = 4 * x + 2 * y + c
        peers = [(px, py, c if t == 0 else 1 - c) for px, py in chips for t in (0, 1)] + [(x, y, 1 - c)]
        sends = [(src[0].at[2 * px + py], dst[0].at[j], (px, py, c)) for j, (px, py) in enumerate(chips)]
        recvs = [dst[0].at[j] for j in range(3)]
        sends += [(src[1], dst[1].at[mine], peer) for peer in peers]
        recvs += [dst[1].at[4 * px + 2 * py + pc] for px, py, pc in peers]
        return [(src[1], dst[1].at[mine])], sends, recvs

    outs = [jax.ShapeDtypeStruct((3,) + p.shape[1:], p.dtype), jax.ShapeDtypeStruct((8,) + small.shape, small.dtype)]
    return _Exchange([p, small], outs, 10, 1, plan)


def _share_exchange(arrays):
    n = len(arrays)

    def plan(src, dst):
        x, y, c, _ = _place()
        return [], [(src[k], dst[k], (x, y, 1 - c)) for k in range(n)], [dst[k] for k in range(n)]

    return _Exchange(arrays, [jax.ShapeDtypeStruct(a.shape, a.dtype) for a in arrays], n, 0, plan)


def _sum_scatter(g, lands, me, core, *, tc, name):
    rows = g.shape[1]
    per = HALF // tc
    n = len(lands)

    def body(*refs):
        g_ref, land_refs, o_ref = refs[1], refs[2:2 + n], refs[2 + n]
        at = 0
        for land_ref in land_refs:
            run = slice(at, at + land_ref.shape[1])
            acc = g_ref[run, :].astype(F32)
            for k in range(7):
                acc = acc + land_ref[k].astype(F32)
            o_ref[run, :] = acc
            at = run.stop

    return pl.pallas_call(
        body, name=name, out_shape=jax.ShapeDtypeStruct((rows, HALF), F32), compiler_params=_cp("parallel"),
        grid_spec=pltpu.PrefetchScalarGridSpec(
            num_scalar_prefetch=1, grid=(per,),
            in_specs=[pl.BlockSpec((None, rows, tc), lambda i, w: (w[0], 0, w[1] * per + i))]
            + [pl.BlockSpec((7, a.shape[1], tc), lambda i, w: (0, 0, i)) for a in lands],
            out_specs=pl.BlockSpec((rows, tc), lambda i, w: (0, i))))(
        jnp.stack([me, core]).astype(jnp.int32), g, *lands)


def _pair_add(g, land, core, *, name):
    n, rows, _ = g.shape

    def body(core_ref, g_ref, land_ref, o_ref):
        o_ref[...] = (g_ref[...].astype(F32) + land_ref[...].astype(F32)).astype(o_ref.dtype)

    blk = pl.BlockSpec((1, rows, HALF), lambda i, w: (i, 0, 0))
    return pl.pallas_call(
        body, name=name, out_shape=jax.ShapeDtypeStruct((n, rows, HALF), g.dtype), compiler_params=_cp("parallel"),
        grid_spec=pltpu.PrefetchScalarGridSpec(
            num_scalar_prefetch=1, grid=(n,),
            in_specs=[pl.BlockSpec((1, rows, HALF), lambda i, w: (i, 0, w[0])), blk], out_specs=blk))(
        jnp.reshape(core, (1,)).astype(jnp.int32), g, land)


def _sum_chips(p, land, me, *, tc, name):
    rows = p.shape[1]

    def body(me_ref, p_ref, land_ref, o_ref):
        acc = p_ref[...].astype(F32)
        for k in range(3):
            acc = acc + land_ref[k].astype(F32)
        o_ref[...] = acc

    return pl.pallas_call(
        body, name=name, out_shape=jax.ShapeDtypeStruct((rows, HALF), F32), compiler_params=_cp("parallel"),
        grid_spec=pltpu.PrefetchScalarGridSpec(
            num_scalar_prefetch=1, grid=(HALF // tc,),
            in_specs=[pl.BlockSpec((None, rows, tc), lambda i, w: (w[0], 0, i)),
                      pl.BlockSpec((3, rows, tc), lambda i, w: (0, 0, i))],
            out_specs=pl.BlockSpec((rows, tc), lambda i, w: (0, i))))(
        jnp.reshape(me, (1,)).astype(jnp.int32), p, land)


def _sum_slots(a, *, name):
    n = a.shape[0]

    def body(a_ref, o_ref):
        acc = a_ref[0]
        for k in range(1, n):
            acc = acc + a_ref[k]
        o_ref[...] = acc

    return pl.pallas_call(body, name=name, out_shape=jax.ShapeDtypeStruct(a.shape[1:], a.dtype))(a)


def _elementwise(fn, ins, n_out, block, *, name):
    shape = ins[0].shape
    grid = tuple(s // b for s, b in zip(shape, block))
    n_in = len(ins)

    def body(*refs):
        outs = fn(*[r[...] for r in refs[:n_in]])
        for o_ref, val in zip(refs[n_in:], outs):
            o_ref[...] = val

    spec = pl.BlockSpec(block, lambda i, j, k: (i, j, k))
    return pl.pallas_call(body, name=name, grid=grid, in_specs=[spec] * n_in, out_specs=[spec] * n_out,
                          out_shape=[jax.ShapeDtypeStruct(shape, F32)] * n_out,
                          compiler_params=_cp(*["parallel"] * 3))(*ins)


def _adamw_math(w, g, m, v):
    mn = ADAM_B1 * m + (1.0 - ADAM_B1) * g
    vn = ADAM_B2 * v + (1.0 - ADAM_B2) * (g * g)
    m_hat = mn / (1.0 - ADAM_B1 ** ADAM_STEP)
    v_hat = vn / (1.0 - ADAM_B2 ** ADAM_STEP)
    return -ADAM_LR * (m_hat / (jnp.sqrt(v_hat) + ADAM_EPS) + ADAM_WD * w), mn, vn


def _adamw(w, g, m, v, block, *, name):
    return _elementwise(_adamw_math, [w, g, m, v], 3, block, name=name)


def _interleave_layers(layers, *, tc, name):
    rows, cols = layers[0].shape
    n = len(layers)

    def body(*refs):
        for l in range(n):
            refs[n][:, l, :] = refs[l][...]

    return pl.pallas_call(body, name=name, grid=(cols // tc,),
                          in_specs=[pl.BlockSpec((rows, tc), lambda i: (0, i))] * n,
                          out_specs=pl.BlockSpec((rows, n, tc), lambda i: (0, 0, i)),
                          out_shape=jax.ShapeDtypeStruct((rows, n, cols), layers[0].dtype),
                          compiler_params=_cp("parallel"))(*layers)


def _adamw_small(ws, gs, ms, vs, *, name):
    n = len(ws)

    def body(*refs):
        w, g, m, v, outs = refs[:n], refs[n:2 * n], refs[2 * n:3 * n], refs[3 * n:4 * n], refs[4 * n:]
        for k in range(n):
            for slot, val in enumerate(_adamw_math(w[k][...], g[k][...], m[k][...], v[k][...])):
                outs[slot * n + k][...] = val

    outs = pl.pallas_call(body, name=name, out_shape=[jax.ShapeDtypeStruct(a.shape, F32) for a in ws] * 3)(
        *ws, *gs, *ms, *vs)
    return outs[:n], outs[n:2 * n], outs[2 * n:]


def _to_kernel_order(wt):
    gates = jnp.pad(wt[2048:2056], ((0, LANE - 2 * A_HEADS), (0, 0)))
    return jnp.concatenate([wt[0:2048], wt[2056:2568], wt[2824:3336], wt[2568:2696], wt[2696:2824], gates], axis=0)


def _from_kernel_order(main, tail):
    return jnp.concatenate([main[0:2048], tail[C_BG - DH_MAIN:C_BG - DH_MAIN + 2 * A_HEADS],
                            main[C_QB:C_QB + B_WIDTH], tail[0:B_KV_WIDTH], tail[B_KV_WIDTH:2 * B_KV_WIDTH],
                            main[C_ZB:C_ZB + B_WIDTH]], axis=0)


def _gate_params(a_log, dt_bias):
    return jnp.pad(jnp.stack([a_log, dt_bias]), ((0, SUBLANE - 2), (A_HEADS, LANE - 2 * A_HEADS)))


SMALL = ("conv_w", "a_log", "dt_bias", "norm_w", "sinks", "ln_g", "ln_b")


def _pack(parts, cols):
    flat = jnp.concatenate([p.reshape(-1) for p in parts])
    rows = -(-flat.shape[0] // cols)
    return jnp.pad(flat, (0, rows * cols - flat.shape[0])).reshape(rows, cols)


def _unpack(packed, shapes):
    flat = packed.reshape(-1)
    out, at = [], 0
    for s in shapes:
        n = math.prod(s)
        out.append(flat[at:at + n].reshape(s))
        at += n
    return out


def kernel(x, w_in, conv_w, a_log, dt_bias, norm_w, sinks, w_out, ln_g, ln_b, loss_target, m_w_in, m_conv_w, m_a_log, m_dt_bias, m_norm_w, m_sinks, m_w_out, m_ln_g, m_ln_b, v_w_in, v_conv_w, v_a_log, v_dt_bias, v_norm_w, v_sinks, v_w_out, v_ln_g, v_ln_b):
    xi, yi, ci = lax.axis_index("x"), lax.axis_index("y"), lax.axis_index("c")
    me = 2 * xi + yi

    to_t = lambda a: jnp.transpose(a, (2, 0, 1))
    from_t = lambda a: jnp.transpose(a, (1, 2, 0))

    wt_shard = to_t(w_in)

    def pack_weights(l):
        rows = jnp.pad(wt_shard[:, l], ((0, IN_PAD - IN_SHARD), (0, 0)))
        return jnp.concatenate([rows, w_out[l]], axis=0).astype(BF16)

    pack0, pack1 = pack_weights(0), pack_weights(1)
    got_in0, g_conv = _gather_two_level(pack0[:IN_PAD], conv_w, name="gather_weights_0")
    conv_full = jnp.moveaxis(g_conv, 0, 2).reshape(DEPTH, CONV_K, 3 * A_WIDTH)
    piece = IN_PAD // 3
    carriers = ("dn_pre", "dn_wy", "dn_scan")
    gathers = {nm: _gather_exchange([pack1[i * piece:(i + 1) * piece]]) for i, nm in enumerate(carriers)}
    gathers.update(in_proj=_gather_exchange([pack0[IN_PAD:]]), swa=_gather_exchange([pack1[IN_PAD:]]))
    w_in_of = lambda rows: _to_kernel_order(rows[:, :IN_SHARD].reshape(IN_COLS, D_MODEL))
    w_out_of = lambda rows: rows.reshape(D_MODEL, D_MODEL)
    args0 = _layer_args(w_in_of(got_in0), conv_full[0], a_log[0], dt_bias[0], sinks[0], norm_w[0],
                        lambda got: w_out_of(got[0]))

    def args1(got):
        rows = jnp.concatenate([got[nm][0] for nm in carriers], axis=1)
        return _layer_args(w_in_of(rows), conv_full[1], a_log[1], dt_bias[1], sinks[1], norm_w[1],
                           w_out_of(got["swa"][0]))

    def pack_grads(g):
        gin = _from_kernel_order(*g["w_in"]).reshape(N_SHARD, IN_SHARD, D_MODEL)
        gin = jnp.pad(gin, ((0, 0), (0, IN_PAD - IN_SHARD), (0, 0)))
        return jnp.concatenate([gin, g["w_out"].reshape(N_SHARD, OUT_SHARD, D_MODEL)], axis=1).astype(BF16)

    packed = {}

    def reduce1(grads1):
        packed[1] = pack_grads(grads1)
        half_rows = packed[1].shape[1] // 2
        return dict(dn_chunk=_reduce_scatter_exchange(packed[1], 0, half_rows),
                    swa=_reduce_scatter_exchange(packed[1], half_rows, half_rows))

    def reduce0(grads0, grads1, loss_tile):
        g0 = pack_grads(grads0)
        from_sibling = _run_exchange(_pair_window_exchange(g0), name="pair_reduce_0")[0]
        packed[0] = _pair_add(g0, from_sibling, ci, name="pair_add_0")
        gsmall = _pack([jnp.stack([g[nm] for g in (grads0, grads1)]) for nm in SMALL] + [loss_tile[0, 0:1]], D_MODEL)
        return _chip_scatter_exchange(packed[0], gsmall)

    _, dx, grads, landed1, (landed0, landed_small) = _local_step(
        x[0], loss_target[0], args0, args1, ln_g, ln_b, gathers=gathers, reduce1=reduce1, reduce0=reduce0)

    small_shapes = [(DEPTH,) + grads[0][nm].shape for nm in SMALL]
    halves = [_sum_chips(packed[0], landed0, me, tc=2 * LANE, name="reduce_sum_0"),
              _sum_scatter(packed[1], [landed1["dn_chunk"][0], landed1["swa"][0]], me, ci, tc=2 * LANE,
                           name="reduce_sum_1")]
    s_small = _sum_slots(landed_small, name="reduce_sum_small")
    others = _run_exchange(_share_exchange(halves), name="pair_share")
    full = [jnp.where(ci == 0, jnp.concatenate([mine, other], axis=1), jnp.concatenate([other, mine], axis=1))
            for mine, other in zip(halves, others)]
    grad_in_layers = [f[:IN_SHARD] for f in full]
    grad_out = jnp.stack([f[IN_PAD:] for f in full])
    out_blk = (1, OUT_SHARD, D_MODEL)
    *small_grads, loss = _unpack(s_small, small_shapes + [()])
    gs = dict(zip(SMALL, small_grads))
    gs["conv_w"] = lax.dynamic_slice_in_dim(gs["conv_w"], me * CONV_SHARD, CONV_SHARD, axis=2)

    grad_in_t = _interleave_layers(grad_in_layers, tc=2 * LANE, name="grad_in_layers")
    d_in, nm_in, nv_in = (from_t(o) for o in _adamw(to_t(w_in), grad_in_t, to_t(m_w_in), to_t(v_w_in),
                                                    (IN_SHARD // 6, DEPTH, D_MODEL), name="adamw_in"))
    grad_in = from_t(grad_in_t)
    d_out, nm_out, nv_out = _adamw(w_out, grad_out, m_w_out, v_w_out, out_blk, name="adamw_out")
    ws = dict(conv_w=conv_w, a_log=a_log, dt_bias=dt_bias, norm_w=norm_w, sinks=sinks, ln_g=ln_g, ln_b=ln_b)
    ms = dict(conv_w=m_conv_w, a_log=m_a_log, dt_bias=m_dt_bias, norm_w=m_norm_w, sinks=m_sinks, ln_g=m_ln_g, ln_b=m_ln_b)
    vs = dict(conv_w=v_conv_w, a_log=v_a_log, dt_bias=v_dt_bias, norm_w=v_norm_w, sinks=v_sinks, ln_g=v_ln_g, ln_b=v_ln_b)
    d_s, nm_s, nv_s = (dict(zip(SMALL, o)) for o in _adamw_small(*[[d[nm] for nm in SMALL] for d in (ws, gs, ms, vs)],
                                                                 name="adamw_small"))

    def in_order(big_in, small, big_out):
        return (big_in, small["conv_w"], small["a_log"], small["dt_bias"], small["norm_w"], small["sinks"], big_out,
                small["ln_g"], small["ln_b"])

    return (loss, dx[None], *in_order(grad_in, gs, grad_out), *in_order(d_in, d_s, d_out),
            *in_order(nm_in, nm_s, nm_out), *in_order(nv_in, nv_s, nv_out))
```

```python
import math

import jax
import jax.numpy as jnp
from jax import lax
from jax.experimental import pallas as pl
from jax.experimental.pallas import tpu as pltpu

F32 = jnp.float32
BF16 = jnp.bfloat16
HI = lax.Precision.HIGHEST

D_MODEL = 1024
DEPTH = 2
A_HEADS = 4
A_HEAD_DIM = 128
A_WIDTH = 512
CONV_K = 4
CHUNK = 64
B_Q_HEADS = 8
B_KV_HEADS = 2
B_HEAD_DIM = 64
B_GROUP = 4
B_WIDTH = 512
B_KV_WIDTH = 128
BLOCK = 128
IN_COLS = 3336
DEEPNORM_ALPHA = (2 * DEPTH) ** 0.25
LN_EPS = 1e-5
RMS_EPS = 1e-6
L2_EPS = 1e-6
ADAM_LR = 0.001
ADAM_B1 = 0.9
ADAM_B2 = 0.999
ADAM_EPS = 1e-08
ADAM_WD = 0.01
ADAM_STEP = 10

N_SHARD = 4
IN_SHARD = IN_COLS // N_SHARD
OUT_SHARD = D_MODEL // N_SHARD
CONV_SHARD = 3 * A_WIDTH // N_SHARD
IN_PAD = -(-IN_SHARD // 96) * 96

P_COLS = 3456
C_PRE = 0
C_ZA = 1536
C_QB = 2048
C_ZB = 2560
C_KB = 3072
C_VB = 3200
C_BG = 3328
DH_MAIN = C_KB
LANE = 128
SUBLANE = 8
HALO = 16
VMEM_LIMIT = 56 * 1024 * 1024
ALIBI = tuple(2.0 ** (-8.0 * (h + 1) / B_Q_HEADS) for h in range(B_Q_HEADS))
NEG = -1e30


def _cp(*sem):
    return pltpu.CompilerParams(dimension_semantics=sem, vmem_limit_bytes=VMEM_LIMIT)


def _dot(a, b):
    return jnp.dot(a.astype(BF16), b.astype(BF16), preferred_element_type=F32)


def _dot_nt(a, b):
    return lax.dot_general(a.astype(BF16), b.astype(BF16), (((1,), (1,)), ((), ())),
                           preferred_element_type=F32)


def _dot_tn(a, b):
    return lax.dot_general(a.astype(BF16), b.astype(BF16), (((0,), (0,)), ((), ())),
                           preferred_element_type=F32)


def _dot_hi(a, b):
    return jnp.dot(a, b, precision=HI, preferred_element_type=F32)


def _sigmoid(x):
    return jax.nn.sigmoid(x)


def _silu(x):
    return x * _sigmoid(x)


def _silu_and_grad(x):
    s = _sigmoid(x)
    return x * s, s * (1.0 + x * (1.0 - s))


def _softplus(x):
    return jnp.maximum(x, 0.0) + jnp.log(1.0 + jnp.exp(-jnp.abs(x)))


def _shift_down(cur, before, s):
    if s == 0:
        return cur
    r = pltpu.roll(cur, s, 0)
    rb = pltpu.roll(before, s, 0)
    row = lax.broadcasted_iota(jnp.int32, before.shape, 0)
    head = jnp.where(row < s, rb, r[0:SUBLANE])
    return jnp.concatenate([head, r[SUBLANE:]], axis=0)


def _shift_up(cur, after, s):
    if s == 0:
        return cur
    n = cur.shape[0]
    r = pltpu.roll(cur, n - s, 0)
    ra = pltpu.roll(after, SUBLANE - s, 0)
    row = lax.broadcasted_iota(jnp.int32, after.shape, 0)
    tail = jnp.where(row >= SUBLANE - s, ra, r[n - SUBLANE:])
    return jnp.concatenate([r[:n - SUBLANE], tail], axis=0)


def _conv_fwd(cur, before, w):
    acc = cur * w[CONV_K - 1:CONV_K, :]
    for s in range(1, CONV_K):
        acc = acc + _shift_down(cur, before, s) * w[CONV_K - 1 - s:CONV_K - s, :]
    return acc


def _matmul_nt(a, bt, *, tm, name, carry=None):
    m, k = a.shape
    n = bt.shape[0]
    c_ins, c_in_specs, c_out_specs, c_outs, c_scratch = _carry_specs(carry)

    def body(*refs):
        a_ref, b_ref, o_ref = _carried(carry, refs, 2, 1, m // tm)
        o_ref[...] = _dot_nt(a_ref[...], b_ref[...]).astype(o_ref.dtype)

    outs = pl.pallas_call(
        body, name=name, grid=(m // tm,),
        in_specs=[pl.BlockSpec((tm, k), lambda i: (i, 0)), pl.BlockSpec((n, k), lambda i: (0, 0))] + c_in_specs,
        out_specs=[pl.BlockSpec((tm, n), lambda i: (i, 0))] + c_out_specs,
        out_shape=[jax.ShapeDtypeStruct((m, n), BF16)] + c_outs,
        scratch_shapes=c_scratch,
        compiler_params=_cp("arbitrary"))(a, bt, *c_ins)
    return outs[0], outs[1:]


def _dn_pre(h, conv_w, par, *, tt, name, carry=None):
    t = h.shape[0]
    cw = 3 * A_WIDTH
    hb = tt // HALO

    c_ins, c_in_specs, c_out_specs, c_outs, c_scratch = _carry_specs(carry)

    def body(*refs):
        (pre_ref, halo_ref, bgi_ref, cw_ref, par_ref,
         q_ref, k_ref, v_ref, bg_ref, bgt_ref) = _carried(carry, refs, 5, 5, t // tt)
        i = pl.program_id(0)
        cur = pre_ref[...].astype(F32)
        before = jnp.where(i > 0, halo_ref[...].astype(F32)[HALO - SUBLANE:], 0.0)
        s = _silu(_conv_fwd(cur, before, cw_ref[...]))
        for hd in range(A_HEADS):
            sl = slice(hd * LANE, (hd + 1) * LANE)
            tq = s[:, hd * LANE:(hd + 1) * LANE]
            q_ref[:, sl] = tq * (lax.rsqrt(jnp.sum(tq * tq, -1, keepdims=True) + L2_EPS) * (A_HEAD_DIM ** -0.5))
            tk = s[:, A_WIDTH + hd * LANE:A_WIDTH + (hd + 1) * LANE]
            k_ref[:, sl] = tk * lax.rsqrt(jnp.sum(tk * tk, -1, keepdims=True) + L2_EPS)
        v_ref[...] = s[:, 2 * A_WIDTH:]
        raw = bgi_ref[...].astype(F32)
        lane = lax.broadcasted_iota(jnp.int32, raw.shape, 1)
        is_a = (lane >= A_HEADS) & (lane < 2 * A_HEADS)
        g = jnp.where(is_a, -jnp.exp(par_ref[0:1, :]) * _softplus(raw + par_ref[1:2, :]), 0.0)
        gc = _dot_hi(_chunk_tri(tt, lower=True), g)
        bg = jnp.where(lane < A_HEADS, _sigmoid(raw), gc)
        bg_ref[...] = bg
        bgt_ref[...] = jnp.transpose(bg)[0:SUBLANE, :]

    wide = jax.ShapeDtypeStruct((t, A_WIDTH), F32)
    outs = pl.pallas_call(
        body, name=name, grid=(t // tt,),
        in_specs=[pl.BlockSpec((tt, cw), lambda i: (i, 0)),
                  pl.BlockSpec((HALO, cw), lambda i: (jnp.maximum(i * hb - 1, 0), 0)),
                  pl.BlockSpec((tt, LANE), lambda i: (i, C_BG // LANE)),
                  pl.BlockSpec((CONV_K, cw), lambda i: (0, 0)),
                  pl.BlockSpec((SUBLANE, LANE), lambda i: (0, 0))] + c_in_specs,
        out_specs=[pl.BlockSpec((tt, A_WIDTH), lambda i: (i, 0))] * 3
        + [pl.BlockSpec((tt, LANE), lambda i: (i, 0)), pl.BlockSpec((SUBLANE, tt), lambda i: (0, i))] + c_out_specs,
        out_shape=[wide, wide, wide, jax.ShapeDtypeStruct((t, LANE), F32),
                   jax.ShapeDtypeStruct((SUBLANE, t), F32)] + c_outs,
        scratch_shapes=c_scratch,
        compiler_params=_cp("arbitrary"))(h, h, h, conv_w, par, *c_ins)
    return outs[:5], outs[5:]


def _chunk_tri(n, lower):
    r = lax.broadcasted_iota(jnp.int32, (n, n), 0)
    c = lax.broadcasted_iota(jnp.int32, (n, n), 1)
    shift = CHUNK.bit_length() - 1
    same = jnp.right_shift(r, shift) == jnp.right_shift(c, shift)
    return (same & ((c <= r) if lower else (c >= r))).astype(F32)


def _chunk_masks():
    r = lax.broadcasted_iota(jnp.int32, (CHUNK, CHUNK), 0)
    c = lax.broadcasted_iota(jnp.int32, (CHUNK, CHUNK), 1)
    return r >= c, r > c, r == c


def _split(a):
    hi = a.astype(BF16)
    return hi, (a - hi.astype(F32)).astype(BF16)


def _dot3(a, b):
    (ah, al), (bh, bl) = a, b
    d = lambda p, q: jnp.dot(p, q, preferred_element_type=F32)
    return d(ah, bh) + (d(ah, bl) + d(al, bh))


def _tri_inv_many(a_list, eye):
    d = lambda p, q: jnp.dot(p, q, preferred_element_type=F32)
    p = [(-a).astype(BF16) for a in a_list]
    tm = [eye - a for a in a_list]
    p = [d(pi, pi).astype(BF16) for pi in p]
    for _ in range(4):
        both = [d(jnp.concatenate([t.astype(BF16), pi], axis=0), pi) for t, pi in zip(tm, p)]
        tm = [t + b[:CHUNK] for t, b in zip(tm, both)]
        p = [b[CHUNK:].astype(BF16) for b in both]
    tm = [t + d(t.astype(BF16), pi) for t, pi in zip(tm, p)]
    ms = [_split(eye + a) for a in a_list]
    res = [eye - _dot3(m, _split(t)) for m, t in zip(ms, tm)]
    return [t + d(t.astype(BF16), r.astype(BF16)) for t, r in zip(tm, res)]


def _chunk_gates(bg_v, bgt_v, hd):
    return (bg_v[:, hd:hd + 1], bg_v[:, A_HEADS + hd:A_HEADS + hd + 1],
            None if bgt_v is None else bgt_v[A_HEADS + hd:A_HEADS + hd + 1, :])


WY_ROWS = 512
SCAN_ROWS = 512
WY_GROUP = 8


def _dn_wy(q, k, v, bg, bgt, *, name, carry=None):
    t = q.shape[0]
    rows = WY_ROWS

    c_ins, c_in_specs, c_out_specs, c_outs, c_scratch = _carry_specs(carry)

    def body(*refs):
        q_ref, k_ref, v_ref, bg_ref, bgt_ref, u_ref, w_ref, tm_ref, qk_ref = _carried(carry, refs, 5, 4, t // rows)
        causal, strict, diag = _chunk_masks()
        eye = diag.astype(F32)
        for c0 in range(0, rows // CHUNK, WY_GROUP):
            items = [(c, hd) for c in range(c0, c0 + WY_GROUP) for hd in range(A_HEADS)]
            rs = lambda c: slice(c * CHUNK, (c + 1) * CHUNK)
            sl = lambda hd: slice(hd * LANE, (hd + 1) * LANE)
            hs = lambda hd: slice(hd * CHUNK, (hd + 1) * CHUNK)
            gates = [_chunk_gates(bg_ref[rs(c), :], bgt_ref[:, rs(c)], hd) for c, hd in items]
            dms = [jnp.exp(jnp.where(causal, gcol - grow, NEG)) for _, gcol, grow in gates]
            kbs = [k_ref[rs(c), sl(hd)] * g[0] for (c, hd), g in zip(items, gates)]
            by_k = [_dot_nt(jnp.concatenate([kb, q_ref[rs(c), sl(hd)]], axis=0), k_ref[rs(c), sl(hd)])
                    for (c, hd), kb in zip(items, kbs)]
            a_list = [jnp.where(strict, m[:CHUNK] * dm, 0.0) for m, dm in zip(by_k, dms)]
            for (c, hd), m, dm in zip(items, by_k, dms):
                qk_ref[rs(c), hs(hd)] = jnp.where(causal, m[CHUNK:] * dm, 0.0)
            tms = _tri_inv_many(a_list, eye)
            for (c, hd), g, kb, tmat in zip(items, gates, kbs, tms):
                tm_ref[rs(c), hs(hd)] = tmat
                uw = _dot(tmat, jnp.concatenate([v_ref[rs(c), sl(hd)] * g[0], kb * jnp.exp(g[1])], axis=1))
                u_ref[rs(c), sl(hd)] = uw[:, :LANE]
                w_ref[rs(c), sl(hd)] = uw[:, LANE:].astype(BF16)

    blk = pl.BlockSpec((rows, A_WIDTH), lambda i: (i, 0))
    half = pl.BlockSpec((rows, A_HEADS * CHUNK), lambda i: (i, 0))
    outs = pl.pallas_call(
        body, name=name, grid=(t // rows,),
        in_specs=[blk, blk, blk, pl.BlockSpec((rows, LANE), lambda i: (i, 0)),
                  pl.BlockSpec((SUBLANE, rows), lambda i: (0, i))] + c_in_specs,
        out_specs=[blk, blk, half, half] + c_out_specs,
        out_shape=[jax.ShapeDtypeStruct((t, A_WIDTH), F32), jax.ShapeDtypeStruct((t, A_WIDTH), BF16),
                   jax.ShapeDtypeStruct((t, A_HEADS * CHUNK), F32),
                   jax.ShapeDtypeStruct((t, A_HEADS * CHUNK), F32)] + c_outs,
        scratch_shapes=c_scratch,
        compiler_params=_cp("arbitrary"))(q, k, v, bg, bgt, *c_ins)
    return outs[:4], outs[4:]


def _dn_scan_fwd(q, k, u, w, qk, bg, *, name, carry=None):
    t = q.shape[0]
    rows = SCAN_ROWS
    per = rows // CHUNK
    c_ins, c_in_specs, c_out_specs, c_outs, c_scratch = _carry_specs(carry)

    def body(*refs):
        q_ref, k_ref, u_ref, w_ref, qk_ref, bg_ref, o_ref, vn_ref, s_ref, state = _carried(carry, refs, 6, 3, t // rows)

        @pl.when(pl.program_id(0) == 0)
        def _():
            state[...] = jnp.zeros_like(state)

        heads = range(A_HEADS)
        sl = lambda hd: slice(hd * LANE, (hd + 1) * LANE)
        s_cur = [state[hd] for hd in heads]
        for c in range(per):
            rs = slice(c * CHUNK, (c + 1) * CHUNK)
            bg_v = bg_ref[rs, :]
            gcols = [_chunk_gates(bg_v, None, hd)[1] for hd in heads]
            glasts = [gc[CHUNK - 1:CHUNK, :] for gc in gcols]
            for hd in heads:
                s_ref[c, hd] = s_cur[hd].astype(BF16)
            vns = [u_ref[rs, sl(hd)] - _dot(w_ref[rs, sl(hd)], s_cur[hd]) for hd in heads]
            qss = [_dot(q_ref[rs, sl(hd)] * jnp.exp(gcols[hd]), s_cur[hd]) for hd in heads]
            s_cur = [s_cur[hd] * jnp.exp(glasts[hd])
                     + _dot_tn(k_ref[rs, sl(hd)] * jnp.exp(glasts[hd] - gcols[hd]), vns[hd]) for hd in heads]
            for hd in heads:
                vn_ref[rs, sl(hd)] = vns[hd]
                o_ref[rs, sl(hd)] = qss[hd] + _dot(qk_ref[rs, hd * CHUNK:(hd + 1) * CHUNK], vns[hd])
        for hd in heads:
            state[hd] = s_cur[hd]

    blk = pl.BlockSpec((rows, A_WIDTH), lambda i: (i, 0))
    half = pl.BlockSpec((rows, A_HEADS * CHUNK), lambda i: (i, 0))
    wide = jax.ShapeDtypeStruct((t, A_WIDTH), F32)
    outs = pl.pallas_call(
        body, name=name, grid=(t // rows,),
        in_specs=[blk, blk, blk, blk, half, pl.BlockSpec((rows, LANE), lambda i: (i, 0))] + c_in_specs,
        out_specs=[blk, blk, pl.BlockSpec((per, A_HEADS, LANE, LANE), lambda i: (i, 0, 0, 0))] + c_out_specs,
        out_shape=[wide, wide, jax.ShapeDtypeStruct((t // CHUNK, A_HEADS, LANE, LANE), BF16)] + c_outs,
        scratch_shapes=[pltpu.VMEM((A_HEADS, LANE, LANE), F32)] + c_scratch,
        compiler_params=_cp("arbitrary"))(q, k, u, w, qk, bg, *c_ins)
    return outs[:3], outs[3:]


def _stack_heads(ref, hk):
    return jnp.concatenate([ref[:, h * B_HEAD_DIM:(h + 1) * B_HEAD_DIM].astype(F32)
                            for h in range(hk * B_GROUP, (hk + 1) * B_GROUP)], axis=0)


def _swa_window():
    qi = lax.broadcasted_iota(jnp.int32, (BLOCK, BLOCK), 0)
    kj = lax.broadcasted_iota(jnp.int32, (BLOCK, BLOCK), 1)
    dist = jnp.where(kj > qi, qi + BLOCK - kj, qi - kj).astype(F32)
    rows = lax.broadcasted_iota(jnp.int32, (B_GROUP * BLOCK, BLOCK), 0)
    cols = lax.broadcasted_iota(jnp.int32, (B_GROUP * BLOCK, BLOCK), 1)
    return cols > jnp.bitwise_and(rows, BLOCK - 1), dist


def _swa_group_probs(q_ref, sk_ref, kp, kc, vp, vc, n_blk):
    hks = range(B_KV_HEADS)
    heads = lambda hk: range(hk * B_GROUP, (hk + 1) * B_GROUP)
    ksl = lambda hk: slice(hk * B_HEAD_DIM, (hk + 1) * B_HEAD_DIM)
    upper, dist = _swa_window()
    no_prev = jnp.where(n_blk > 0, 0.0, NEG)
    ones = jnp.ones((BLOCK, B_HEAD_DIM), BF16)
    with_ones = lambda v, hk: jnp.concatenate([v[:, ksl(hk)].astype(BF16), ones], axis=1)
    qs = [_stack_heads(q_ref, hk) * (B_HEAD_DIM ** -0.5) for hk in hks]
    sink = [jnp.concatenate([jnp.broadcast_to(sk_ref[h:h + 1, 0:1], (BLOCK, 1)) for h in heads(hk)], axis=0)
            for hk in hks]
    s = [jnp.where(upper, _dot_nt(qs[hk], kp[:, ksl(hk)]) + no_prev, _dot_nt(qs[hk], kc[:, ksl(hk)]))
         - jnp.concatenate([ALIBI[h] * dist for h in heads(hk)], axis=0) for hk in hks]
    m = [jnp.maximum(jnp.max(s[hk], axis=-1, keepdims=True), sink[hk]) for hk in hks]
    p = [jnp.exp(s[hk] - m[hk]) for hk in hks]
    p_up = [jnp.where(upper, p[hk], 0.0) for hk in hks]
    oe = [jnp.dot(p_up[hk].astype(BF16), with_ones(vp, hk), preferred_element_type=F32)
          + jnp.dot((p[hk] - p_up[hk]).astype(BF16), with_ones(vc, hk), preferred_element_type=F32) for hk in hks]
    ps = [jnp.exp(sink[hk] - m[hk]) for hk in hks]
    inv = [1.0 / (oe[hk][:, B_HEAD_DIM:B_HEAD_DIM + 1] + ps[hk]) for hk in hks]
    return upper, [(qs[hk], p[hk] * inv[hk], ps[hk] * inv[hk], oe[hk][:, :B_HEAD_DIM] * inv[hk]) for hk in hks]


def _swa_specs():
    qspec = lambda c0: pl.BlockSpec((BLOCK, B_WIDTH), lambda i: (i, c0 // B_WIDTH))
    cur = lambda c0: pl.BlockSpec((BLOCK, LANE), lambda i: (i, c0 // LANE))
    prev = lambda c0: pl.BlockSpec((BLOCK, LANE), lambda i: (jnp.maximum(i - 1, 0), c0 // LANE))
    return qspec, cur, prev


def _carried(carry, refs, n_in, n_out, steps):
    if carry is None:
        return refs
    ci, co = len(carry.ins), len(carry.outs)
    own = refs[:n_in] + refs[n_in + ci:n_in + ci + n_out] + refs[n_in + ci + n_out + co:len(refs) - 3]
    parts = refs[n_in:n_in + ci], refs[n_in + ci + n_out:n_in + ci + n_out + co], refs[len(refs) - 3:]

    @pl.when(pl.program_id(0) == 0)
    def _():
        carry.start(*parts)

    @pl.when(pl.program_id(0) == steps - 1)
    def _():
        carry.finish(*parts)

    return own


def _carry_specs(carry):
    if carry is None:
        return [], [], [], [], []
    return (list(carry.ins), [_ANY] * len(carry.ins), [_ANY] * len(carry.outs), list(carry.outs), carry.scratch())


def _swa_fwd(h, sinks_b, *, name, carry=None):
    t = h.shape[0]
    qspec, cur, prev = _swa_specs()
    c_ins, c_in_specs, c_out_specs, c_outs, c_scratch = _carry_specs(carry)

    def body(*refs):
        q_ref, kc_ref, kp_ref, vc_ref, vp_ref, sk_ref, o_ref = _carried(carry, refs, 6, 1, t // BLOCK)
        n_blk = pl.program_id(0)
        _, groups = _swa_group_probs(q_ref, sk_ref, kp_ref[...], kc_ref[...], vp_ref[...], vc_ref[...], n_blk)
        for hk, (_, _, _, o) in enumerate(groups):
            for g in range(B_GROUP):
                hq = hk * B_GROUP + g
                o_ref[:, hq * B_HEAD_DIM:(hq + 1) * B_HEAD_DIM] = o[g * BLOCK:(g + 1) * BLOCK]

    outs = pl.pallas_call(
        body, name=name, grid=(t // BLOCK,),
        in_specs=[qspec(C_QB), cur(C_KB), prev(C_KB), cur(C_VB), prev(C_VB),
                  pl.BlockSpec((B_Q_HEADS, LANE), lambda i: (0, 0))] + c_in_specs,
        out_specs=[pl.BlockSpec((BLOCK, B_WIDTH), lambda i: (i, 0))] + c_out_specs,
        out_shape=[jax.ShapeDtypeStruct((t, B_WIDTH), F32)] + c_outs,
        scratch_shapes=c_scratch,
        compiler_params=_cp("arbitrary"))(h, h, h, h, h, sinks_b, *c_ins)
    return outs[0], outs[1:]


def _rms_gate(o, za, nw):
    outs = []
    for hd in range(A_HEADS):
        oh = o[:, hd * LANE:(hd + 1) * LANE]
        r = lax.rsqrt(jnp.mean(oh * oh, -1, keepdims=True) + RMS_EPS)
        outs.append(oh * r * nw)
    return jnp.concatenate(outs, axis=1) * _silu(za)


def _out_ln(x, oa, ob, h, norm_w, w_out, ln_g, ln_b, *, tm, name, target=None):
    t = x.shape[0]
    last = target is not None

    def body(*refs):
        x_ref, oa_ref, ob_ref, za_ref, zb_ref, nw_ref, w_ref, g_ref, b_ref = refs[:9]
        xn_ref, mx_ref, r_ref = refs[9 + last:12 + last]
        ya = _rms_gate(oa_ref[...], za_ref[...].astype(F32), nw_ref[...])
        yb = ob_ref[...] * _silu(zb_ref[...].astype(F32))
        mixed = jnp.concatenate([ya, yb], axis=1).astype(BF16)
        mx_ref[...] = mixed
        r = DEEPNORM_ALPHA * x_ref[...] + jnp.dot(mixed, w_ref[...], preferred_element_type=F32)
        r_ref[...] = r
        mu = jnp.mean(r, -1, keepdims=True)
        xc = r - mu
        var = jnp.mean(xc * xc, -1, keepdims=True)
        xn = xc * lax.rsqrt(var + LN_EPS) * g_ref[...] + b_ref[...]
        if not last:
            xn_ref[...] = xn
            return
        loss_ref = refs[13]

        @pl.when(pl.program_id(0) == 0)
        def _():
            loss_ref[...] = jnp.zeros_like(loss_ref)

        err = xn - refs[9][...]
        xn_ref[...] = err * (1.0 / D_MODEL)
        loss_ref[...] += 0.5 / D_MODEL * jnp.sum(err * err)

    row = lambda w, c: pl.BlockSpec((tm, w), lambda i: (i, c))
    full = lambda a, b: pl.BlockSpec((a, b), lambda i: (0, 0))
    wide = jax.ShapeDtypeStruct((t, D_MODEL), F32)
    return pl.pallas_call(
        body, name=name, grid=(t // tm,),
        in_specs=[row(D_MODEL, 0), row(A_WIDTH, 0), row(B_WIDTH, 0), row(A_WIDTH, C_ZA // A_WIDTH),
                  row(B_WIDTH, C_ZB // B_WIDTH), full(1, LANE), full(D_MODEL, D_MODEL), full(1, D_MODEL),
                  full(1, D_MODEL)] + [row(D_MODEL, 0)] * last,
        out_specs=[row(D_MODEL, 0), row(D_MODEL, 0), row(D_MODEL, 0)] + [full(SUBLANE, LANE)] * last,
        out_shape=[wide, jax.ShapeDtypeStruct((t, D_MODEL), BF16), wide]
        + [jax.ShapeDtypeStruct((SUBLANE, LANE), F32)] * last,
        compiler_params=_cp("arbitrary" if last else "parallel"))(
        x, oa, ob, h, h, norm_w, w_out, ln_g, ln_b, *([target] if last else []))


def _layer_fwd(x, wt, conv_w, par, sinks_b, norm_w, w_out_bf, ln_g, ln_b, l, carries=None, target=None):
    carries = carries or {}
    h, got_in = _matmul_nt(x, wt, tm=512, name=f"in_proj_{l}", carry=carries.get("in_proj"))
    if callable(w_out_bf):
        w_out_bf = w_out_bf(got_in)
    (q, k, v, bg, bgt), got_pre = _dn_pre(h, conv_w, par, tt=512, name=f"dn_pre_{l}", carry=carries.get("dn_pre"))
    (u, w, tmat, qk), got_wy = _dn_wy(q, k, v, bg, bgt, name=f"dn_wy_{l}", carry=carries.get("dn_wy"))
    (oa, vn, s_all), got_scan = _dn_scan_fwd(q, k, u, w, qk, bg, name=f"dn_scan_{l}", carry=carries.get("dn_scan"))
    ob, got_swa = _swa_fwd(h, sinks_b, name=f"swa_fwd_{l}", carry=carries.get("swa"))
    xn, mixed, r, *loss = _out_ln(x, oa, ob, h, norm_w, w_out_bf, ln_g, ln_b, tm=512, name=f"out_ln_{l}", target=target)
    if loss:
        xn = (xn, loss[0])
    res = dict(x=x, h=h, q=q, k=k, v=v, bg=bg, bgt=bgt, w=w, tmat=tmat, qk=qk, vn=vn, oa=oa, s_all=s_all,
               mixed=mixed, r=r, w_out=w_out_bf)
    return xn, res, dict(in_proj=got_in, dn_pre=got_pre, dn_wy=got_wy, dn_scan=got_scan, swa=got_swa)


def _ln_out_bwd(dxn, r, mixed, ln_g, w_out, *, tm, name):
    t = dxn.shape[0]

    def body(dxn_ref, r_ref, mx_ref, g_ref, w_ref, dr_ref, dm_ref, dw_ref, dg_ref, db_ref):
        @pl.when(pl.program_id(0) == 0)
        def _():
            dw_ref[...] = jnp.zeros_like(dw_ref)
            dg_ref[...] = jnp.zeros_like(dg_ref)
            db_ref[...] = jnp.zeros_like(db_ref)

        rr = r_ref[...]
        xc = rr - jnp.mean(rr, -1, keepdims=True)
        rstd = lax.rsqrt(jnp.mean(xc * xc, -1, keepdims=True) + LN_EPS)
        xhat = xc * rstd
        dxn_v = dxn_ref[...]
        dxh = dxn_v * g_ref[...]
        dr = rstd * (dxh - jnp.mean(dxh, -1, keepdims=True) - xhat * jnp.mean(dxh * xhat, -1, keepdims=True))
        dr_ref[...] = dr
        dg_ref[...] += jnp.sum(dxn_v * xhat, axis=0, keepdims=True)
        db_ref[...] += jnp.sum(dxn_v, axis=0, keepdims=True)
        drb = dr.astype(BF16)
        dm_ref[...] = _dot_nt(drb, w_ref[...])
        dw_ref[...] += _dot_tn(mx_ref[...], drb)

    row = pl.BlockSpec((tm, D_MODEL), lambda i: (i, 0))
    full = lambda a, b: pl.BlockSpec((a, b), lambda i: (0, 0))
    big = jax.ShapeDtypeStruct((t, D_MODEL), F32)
    vec = jax.ShapeDtypeStruct((1, D_MODEL), F32)
    return pl.pallas_call(
        body, name=name, grid=(t // tm,),
        in_specs=[row, row, row, full(1, D_MODEL), full(D_MODEL, D_MODEL)],
        out_specs=[row, row, full(D_MODEL, D_MODEL), full(1, D_MODEL), full(1, D_MODEL)],
        out_shape=[big, big, jax.ShapeDtypeStruct((D_MODEL, D_MODEL), F32), vec, vec],
        compiler_params=_cp("arbitrary"))(dxn, r, mixed, ln_g, w_out)


def _dn_post_bwd(dm, oa, h, norm_w, *, tm, name):
    t = oa.shape[0]

    def body(dy_ref, o_ref, za_ref, nw_ref, do_ref, dza_ref, dnw_ref):
        @pl.when(pl.program_id(0) == 0)
        def _():
            dnw_ref[...] = jnp.zeros_like(dnw_ref)

        nw = nw_ref[...]
        dnw = jnp.zeros_like(nw)
        for hd in range(A_HEADS):
            sl = slice(hd * LANE, (hd + 1) * LANE)
            oh, za, dy = o_ref[:, sl], za_ref[:, sl].astype(F32), dy_ref[:, sl]
            rs = lax.rsqrt(jnp.mean(oh * oh, -1, keepdims=True) + RMS_EPS)
            nrm = oh * rs
            gate, dgate = _silu_and_grad(za)
            dza_ref[:, sl] = dy * nrm * nw * dgate
            dn = dy * gate
            dnw = dnw + jnp.sum(dn * nrm, axis=0, keepdims=True)
            dnn = dn * nw
            do_ref[:, sl] = rs * dnn - oh * (rs * rs * rs) * jnp.mean(dnn * oh, -1, keepdims=True)
        dnw_ref[...] += dnw

    row = lambda c: pl.BlockSpec((tm, A_WIDTH), lambda i: (i, c))
    wide = jax.ShapeDtypeStruct((t, A_WIDTH), F32)
    return pl.pallas_call(
        body, name=name, grid=(t // tm,),
        in_specs=[row(0), row(0), row(C_ZA // A_WIDTH), pl.BlockSpec((1, LANE), lambda i: (0, 0))],
        out_specs=[row(0), row(C_ZA // A_WIDTH), pl.BlockSpec((1, LANE), lambda i: (0, 0))],
        out_shape=[wide, jax.ShapeDtypeStruct((t, DH_MAIN), F32), jax.ShapeDtypeStruct((1, LANE), F32)],
        compiler_params=_cp("arbitrary"))(dm, oa, h, norm_w)


def _dn_scan_bwd(q, k, w, qk, bg, do, *, name):
    t = q.shape[0]
    rows = SCAN_ROWS
    per = rows // CHUNK
    n = t // rows

    def body(q_ref, k_ref, w_ref, qk_ref, bg_ref, do_ref, dvn_ref, ds_ref, dstate):
        @pl.when(pl.program_id(0) == 0)
        def _():
            dstate[...] = jnp.zeros_like(dstate)

        heads = range(A_HEADS)
        sl = lambda hd: slice(hd * LANE, (hd + 1) * LANE)
        ds_cur = [dstate[hd] for hd in heads]
        for c in reversed(range(per)):
            rs = slice(c * CHUNK, (c + 1) * CHUNK)
            bg_v = bg_ref[rs, :]
            gcols = [_chunk_gates(bg_v, None, hd)[1] for hd in heads]
            glasts = [gc[CHUNK - 1:CHUNK, :] for gc in gcols]
            for hd in heads:
                ds_ref[c, hd] = ds_cur[hd].astype(BF16)
            pdo = [_dot_tn(qk_ref[rs, hd * CHUNK:(hd + 1) * CHUNK], do_ref[rs, sl(hd)]) for hd in heads]
            qdo = [_dot_tn(q_ref[rs, sl(hd)] * jnp.exp(gcols[hd]), do_ref[rs, sl(hd)]) for hd in heads]
            dvns = [pdo[hd] + _dot(k_ref[rs, sl(hd)] * jnp.exp(glasts[hd] - gcols[hd]), ds_cur[hd]) for hd in heads]
            ds_cur = [qdo[hd] + jnp.exp(glasts[hd]) * ds_cur[hd] - _dot_tn(w_ref[rs, sl(hd)], dvns[hd])
                      for hd in heads]
            for hd in heads:
                dvn_ref[rs, sl(hd)] = dvns[hd]
        for hd in heads:
            dstate[hd] = ds_cur[hd]

    blk = pl.BlockSpec((rows, A_WIDTH), lambda i: (n - 1 - i, 0))
    return pl.pallas_call(
        body, name=name, grid=(n,),
        in_specs=[blk, blk, blk, pl.BlockSpec((rows, A_HEADS * CHUNK), lambda i: (n - 1 - i, 0)),
                  pl.BlockSpec((rows, LANE), lambda i: (n - 1 - i, 0)), blk],
        out_specs=[blk, pl.BlockSpec((per, A_HEADS, LANE, LANE), lambda i: (n - 1 - i, 0, 0, 0))],
        out_shape=[jax.ShapeDtypeStruct((t, A_WIDTH), F32),
                   jax.ShapeDtypeStruct((t // CHUNK, A_HEADS, LANE, LANE), BF16)],
        scratch_shapes=[pltpu.VMEM((A_HEADS, LANE, LANE), F32)],
        compiler_params=_cp("arbitrary"))(q, k, w, qk, bg, do)


def _dn_chunk_bwd(q, k, v, vn, tmat, qk, bg, bgt, s_all, ds_all, dvn, do, *, name, carry=None):
    t = q.shape[0]
    rows = WY_ROWS
    per = rows // CHUNK

    c_ins, c_in_specs, c_out_specs, c_outs, c_scratch = _carry_specs(carry)

    def body(*refs):
        (q_ref, k_ref, v_ref, vn_ref, tm_ref, qk_ref, bg_ref, bgt_ref, s_ref, ds_ref, dvn_ref, do_ref,
         dq_ref, dk_ref, dv_ref, dbg_ref, dbgt_ref) = _carried(carry, refs, 12, 5, t // rows)
        causal, strict, _ = _chunk_masks()
        lane = lax.broadcasted_iota(jnp.int32, (CHUNK, LANE), 1)
        rowi = lax.broadcasted_iota(jnp.int32, (CHUNK, 1), 0)
        sub = lax.broadcasted_iota(jnp.int32, (SUBLANE, CHUNK), 0)
        rs = lambda c: slice(c * CHUNK, (c + 1) * CHUNK)
        sl = lambda hd: slice(hd * LANE, (hd + 1) * LANE)
        hs = lambda hd: slice(hd * CHUNK, (hd + 1) * CHUNK)
        for c0 in range(0, per, WY_GROUP):
            items = [(c, hd) for c in range(c0, c0 + WY_GROUP) for hd in range(A_HEADS)]
            at = lambda ref: [ref[rs(c), sl(hd)] for c, hd in items]
            qs, ks, vs, dos, vns, dvns = at(q_ref), at(k_ref), at(v_ref), at(do_ref), at(vn_ref), at(dvn_ref)
            tmhs = [tm_ref[rs(c), hs(hd)] for c, hd in items]
            ps = [qk_ref[rs(c), hs(hd)] for c, hd in items]
            gates = [_chunk_gates(bg_ref[rs(c), :], bgt_ref[:, rs(c)], hd) for c, hd in items]
            betas = [g[0] for g in gates]
            gcols = [g[1] for g in gates]
            dmats = [jnp.exp(jnp.where(causal, g[1] - g[2], NEG)) for g in gates]
            es = [jnp.exp(gc) for gc in gcols]
            glasts = [gc[CHUNK - 1:CHUNK, :] for gc in gcols]
            eks = [jnp.exp(gl - gc) for gl, gc in zip(glasts, gcols)]
            kbs = [kh * b for kh, b in zip(ks, betas)]
            vbs = [vh * b for vh, b in zip(vs, betas)]
            kbes = [kb * e for kb, e in zip(kbs, es)]

            a_s = [jnp.where(strict, _dot_nt(kb, kh) * dm, 0.0) for kb, kh, dm in zip(kbs, ks, dmats)]
            dps = [jnp.where(causal, _dot_nt(doh, vnh), 0.0) for doh, vnh in zip(dos, vns)]
            rows2 = lambda a, b: jnp.concatenate([a, b], axis=0)
            cols2 = lambda a, b: jnp.concatenate([a, b], axis=1)
            by_s = [_dot_nt(rows2(doh, dvnh), s_ref[c, hd]) for doh, dvnh, (c, hd) in zip(dos, dvns, items)]
            dqds = [m[:CHUNK] for m in by_s]
            dws = [-m[CHUNK:] for m in by_s]
            dkds = [_dot_nt(vnh, ds_ref[c, hd]) for vnh, (c, hd) in zip(vns, items)]
            dgts = [jnp.sum(s_ref[c, hd].astype(F32) * ds_ref[c, hd].astype(F32), keepdims=True) for c, hd in items]
            pairs = [cols2(dvnh, dw) for dvnh, dw in zip(dvns, dws)]
            by_t = [_dot_tn(tmh, pr) for tmh, pr in zip(tmhs, pairs)]
            dvbs = [m[:, :LANE] for m in by_t]
            dkbes = [m[:, LANE:] for m in by_t]
            dts = [_dot_nt(pr, cols2(vb, kbe)) for pr, vb, kbe in zip(pairs, vbs, kbes)]
            xs = [_dot_nt(dt, tmh) for dt, tmh in zip(dts, tmhs)]
            das = [jnp.where(strict, -_dot_tn(tmh, x), 0.0) for tmh, x in zip(tmhs, xs)]
            dmas = [da * dm for da, dm in zip(das, dmats)]
            dmps = [dp * dm for dp, dm in zip(dps, dmats)]
            stacked = [rows2(dma, dmp) for dma, dmp in zip(dmas, dmps)]
            by_k = [_dot(st, kh) for st, kh in zip(stacked, ks)]
            dkbs = [m[:CHUNK] + dkbe * e for m, dkbe, e in zip(by_k, dkbes, es)]
            for i, (c, hd) in enumerate(items):
                dq_ref[rs(c), sl(hd)] = by_k[i][CHUNK:] + dqds[i] * es[i]
                dk_ref[rs(c), sl(hd)] = (_dot_tn(stacked[i], rows2(kbs[i], qs[i])) + dkds[i] * eks[i]
                                         + dkbs[i] * betas[i])
                dv_ref[rs(c), sl(hd)] = dvbs[i] * betas[i]
            for c in range(c0, c0 + WY_GROUP):
                acc = jnp.zeros((CHUNK, LANE), F32)
                acc_t = jnp.zeros((SUBLANE, CHUNK), F32)
                for i, (ci, hd) in enumerate(items):
                    if ci != c:
                        continue
                    gmat = das[i] * a_s[i] + dps[i] * ps[i]
                    rk = jnp.sum(dkds[i] * ks[i], -1, keepdims=True) * eks[i]
                    de = jnp.sum(dqds[i] * qs[i] + dkbes[i] * kbs[i], -1, keepdims=True)
                    dglast = jnp.sum(rk, keepdims=True) + dgts[i] * jnp.exp(glasts[i])
                    dgc = (jnp.sum(gmat, -1, keepdims=True) + de * es[i] - rk
                           + jnp.where(rowi == CHUNK - 1, dglast, 0.0))
                    dbeta = jnp.sum(dkbs[i] * ks[i] + dvbs[i] * vs[i], -1, keepdims=True)
                    acc = acc + jnp.where(lane == hd, dbeta, 0.0) + jnp.where(lane == A_HEADS + hd, dgc, 0.0)
                    acc_t = acc_t + jnp.where(sub == A_HEADS + hd, -jnp.sum(gmat, axis=0, keepdims=True), 0.0)
                dbg_ref[rs(c), :] = acc
                dbgt_ref[:, rs(c)] = acc_t

    blk = pl.BlockSpec((rows, A_WIDTH), lambda i: (i, 0))
    half = pl.BlockSpec((rows, A_HEADS * CHUNK), lambda i: (i, 0))
    col = pl.BlockSpec((rows, LANE), lambda i: (i, 0))
    rowf = pl.BlockSpec((SUBLANE, rows), lambda i: (0, i))
    st = pl.BlockSpec((per, A_HEADS, LANE, LANE), lambda i: (i, 0, 0, 0))
    wide = jax.ShapeDtypeStruct((t, A_WIDTH), F32)
    outs = pl.pallas_call(
        body, name=name, grid=(t // rows,),
        in_specs=[blk, blk, blk, blk, half, half, col, rowf, st, st, blk, blk] + c_in_specs,
        out_specs=[blk, blk, blk, col, rowf] + c_out_specs,
        out_shape=[wide, wide, wide, jax.ShapeDtypeStruct((t, LANE), F32),
                   jax.ShapeDtypeStruct((SUBLANE, t), F32)] + c_outs,
        scratch_shapes=c_scratch,
        compiler_params=_cp("arbitrary"))(q, k, v, vn, tmat, qk, bg, bgt, s_all, ds_all, dvn, do, *c_ins)
    return outs[:5], outs[5:]


def _dn_pre_bwd(h, conv_w, par, dq, dk, dv, dbg, dbgt, *, tt, name):
    t = h.shape[0]
    cw = 3 * A_WIDTH
    hb = tt // HALO

    def body(pre_ref, halo_ref, bgi_ref, cw_ref, par_ref, dq_ref, dk_ref, dv_ref, dbg_ref, dbgt_ref,
             dc_ref, dbgi_ref, dpar_ref):
        i = pl.program_id(0)

        @pl.when(i == 0)
        def _():
            dpar_ref[...] = jnp.zeros_like(dpar_ref)

        cur = pre_ref[...].astype(F32)
        before = jnp.where(i > 0, halo_ref[...].astype(F32)[HALO - SUBLANE:], 0.0)
        c = _conv_fwd(cur, before, cw_ref[...])
        s, ds = _silu_and_grad(c)
        for hd in range(A_HEADS):
            sl = slice(hd * LANE, (hd + 1) * LANE)
            for base, d_ref, scale in ((0, dq_ref, A_HEAD_DIM ** -0.5), (A_WIDTH, dk_ref, 1.0)):
                csl = slice(base + hd * LANE, base + (hd + 1) * LANE)
                tq = s[:, base + hd * LANE:base + (hd + 1) * LANE]
                dy = d_ref[:, sl]
                rq = lax.rsqrt(jnp.sum(tq * tq, -1, keepdims=True) + L2_EPS)
                dtq = scale * (rq * dy - tq * (rq * rq * rq) * jnp.sum(dy * tq, -1, keepdims=True))
                dc_ref[:, csl] = dtq * ds[:, base + hd * LANE:base + (hd + 1) * LANE]
        dc_ref[:, 2 * A_WIDTH:] = dv_ref[...] * ds[:, 2 * A_WIDTH:]
        raw = bgi_ref[...].astype(F32)
        lane = lax.broadcasted_iota(jnp.int32, raw.shape, 1)
        is_b = lane < A_HEADS
        is_a = (lane >= A_HEADS) & (lane < 2 * A_HEADS)
        rows_t = jnp.concatenate([dbgt_ref[...], jnp.zeros((LANE - SUBLANE, tt), F32)], axis=0)
        dbg_v = dbg_ref[...] + jnp.where(is_a, jnp.transpose(rows_t), 0.0)
        dbg_v = jnp.where(is_a, _dot_hi(_chunk_tri(tt, lower=False), jnp.where(is_a, dbg_v, 0.0)), dbg_v)
        beta = _sigmoid(raw)
        z = raw + par_ref[1:2, :]
        neg_ea = -jnp.exp(par_ref[0:1, :])
        g = neg_ea * _softplus(z)
        da = dbg_v * neg_ea * _sigmoid(z)
        dbgi_ref[...] = jnp.where(is_b, dbg_v * beta * (1.0 - beta), jnp.where(is_a, da, 0.0))
        dpar_ref[0:1, :] += jnp.sum(jnp.where(is_a, dbg_v * g, 0.0), axis=0, keepdims=True)
        dpar_ref[1:2, :] += jnp.sum(jnp.where(is_a, da, 0.0), axis=0, keepdims=True)

    wide = pl.BlockSpec((tt, A_WIDTH), lambda i: (i, 0))
    return pl.pallas_call(
        body, name=name, grid=(t // tt,),
        in_specs=[pl.BlockSpec((tt, cw), lambda i: (i, 0)),
                  pl.BlockSpec((HALO, cw), lambda i: (jnp.maximum(i * hb - 1, 0), 0)),
                  pl.BlockSpec((tt, LANE), lambda i: (i, C_BG // LANE)),
                  pl.BlockSpec((CONV_K, cw), lambda i: (0, 0)),
                  pl.BlockSpec((SUBLANE, LANE), lambda i: (0, 0)),
                  wide, wide, wide, pl.BlockSpec((tt, LANE), lambda i: (i, 0)),
                  pl.BlockSpec((SUBLANE, tt), lambda i: (0, i))],
        out_specs=[pl.BlockSpec((tt, cw), lambda i: (i, 0)), pl.BlockSpec((tt, LANE), lambda i: (i, 0)),
                   pl.BlockSpec((SUBLANE, LANE), lambda i: (0, 0))],
        out_shape=[jax.ShapeDtypeStruct((t, cw), F32), jax.ShapeDtypeStruct((t, LANE), F32),
                   jax.ShapeDtypeStruct((SUBLANE, LANE), F32)],
        compiler_params=_cp("arbitrary"))(h, h, h, conv_w, par, dq, dk, dv, dbg, dbgt)


def _conv_bwd(dc, h, conv_w, dh, *, tt, name):
    t = dc.shape[0]
    cw = 3 * A_WIDTH
    hb = tt // HALO
    nb = t // tt

    def body(dc_ref, after_ref, pre_ref, before_ref, cw_ref, dh_in_ref, dpre_ref, dcw_ref):
        i = pl.program_id(0)

        @pl.when(i == 0)
        def _():
            dcw_ref[...] = jnp.zeros_like(dcw_ref)

        dcv = dc_ref[...]
        after = jnp.where(i < nb - 1, after_ref[...], 0.0)
        cur = pre_ref[...].astype(F32)
        before = jnp.where(i > 0, before_ref[...].astype(F32)[HALO - SUBLANE:], 0.0)
        w = cw_ref[...]
        acc = dcv * w[CONV_K - 1:CONV_K, :]
        dcw_ref[CONV_K - 1:CONV_K, :] += jnp.sum(dcv * cur, axis=0, keepdims=True)
        for s in range(1, CONV_K):
            j = CONV_K - 1 - s
            acc = acc + _shift_up(dcv, after, s) * w[j:j + 1, :]
            dcw_ref[j:j + 1, :] += jnp.sum(dcv * _shift_down(cur, before, s), axis=0, keepdims=True)
        dpre_ref[...] = acc

    return pl.pallas_call(
        body, name=name, grid=(nb,),
        in_specs=[pl.BlockSpec((tt, cw), lambda i: (i, 0)),
                  pl.BlockSpec((SUBLANE, cw), lambda i: (jnp.minimum((i + 1) * (tt // SUBLANE), t // SUBLANE - 1), 0)),
                  pl.BlockSpec((tt, cw), lambda i: (i, 0)),
                  pl.BlockSpec((HALO, cw), lambda i: (jnp.maximum(i * hb - 1, 0), 0)),
                  pl.BlockSpec((CONV_K, cw), lambda i: (0, 0)), _ANY],
        out_specs=[pl.BlockSpec((tt, cw), lambda i: (i, 0)), pl.BlockSpec((SUBLANE, cw), lambda i: (0, 0))],
        out_shape=[jax.ShapeDtypeStruct(dh.shape, F32), jax.ShapeDtypeStruct((SUBLANE, cw), F32)],
        input_output_aliases={5: 0},
        compiler_params=_cp("arbitrary"))(dc, dc, h, h, conv_w, dh)


def _swa_bwd(h, dm, sinks_b, dh, *, name, carry=None):
    t = h.shape[0]
    qspec, cur, prev = _swa_specs()
    c_ins, c_in_specs, c_out_specs, c_outs, c_scratch = _carry_specs(carry)

    def body(*refs):
        (q_ref, kc_ref, kp_ref, vc_ref, vp_ref, zb_ref, dy_ref, sk_ref, dh_in_ref,
         dqz_ref, dk_ref, dv_ref, dsk_ref) = _carried(carry, refs, 9, 4, t // BLOCK)
        n_blk = pl.program_id(0)

        @pl.when(n_blk == 0)
        def _():
            dk_ref[...] = jnp.zeros_like(dk_ref)
            dv_ref[...] = jnp.zeros_like(dv_ref)
            dsk_ref[...] = jnp.zeros_like(dsk_ref)

        kp, kc, vp, vc = kp_ref[...], kc_ref[...], vp_ref[...], vc_ref[...]
        scale = B_HEAD_DIM ** -0.5
        hks = range(B_KV_HEADS)
        ksl = lambda hk: slice(hk * B_HEAD_DIM, (hk + 1) * B_HEAD_DIM)
        upper, groups = _swa_group_probs(q_ref, sk_ref, kp, kc, vp, vc, n_blk)
        zbs = [_stack_heads(zb_ref, hk) for hk in hks]
        dys = [_stack_heads(dy_ref, hk) for hk in hks]
        gates = [_silu_and_grad(zbs[hk]) for hk in hks]
        dos = [dys[hk] * gates[hk][0] for hk in hks]
        deltas = [jnp.sum(dos[hk] * groups[hk][3], -1, keepdims=True) for hk in hks]
        dps = [jnp.where(upper, _dot_nt(dos[hk], vp[:, ksl(hk)]), _dot_nt(dos[hk], vc[:, ksl(hk)])) for hk in hks]
        dss = [groups[hk][1] * (dps[hk] - deltas[hk]) for hk in hks]
        ds_up = [jnp.where(upper, dss[hk], 0.0) for hk in hks]
        ds_lo = [dss[hk] - ds_up[hk] for hk in hks]
        p_up = [jnp.where(upper, groups[hk][1], 0.0) for hk in hks]
        p_lo = [groups[hk][1] - p_up[hk] for hk in hks]
        dqs = [(_dot(ds_up[hk], kp[:, ksl(hk)]) + _dot(ds_lo[hk], kc[:, ksl(hk)])) * scale for hk in hks]
        dk_prev = [_dot_tn(ds_up[hk], groups[hk][0]) for hk in hks]
        dk_cur = [_dot_tn(ds_lo[hk], groups[hk][0]) for hk in hks]
        dv_prev = [_dot_tn(p_up[hk], dos[hk]) for hk in hks]
        dv_cur = [_dot_tn(p_lo[hk], dos[hk]) for hk in hks]
        for hk in hks:
            dzb = dys[hk] * groups[hk][3] * gates[hk][1]
            dsink = groups[hk][2] * deltas[hk]
            for g in range(B_GROUP):
                hq = hk * B_GROUP + g
                rows = slice(g * BLOCK, (g + 1) * BLOCK)
                qsl = slice(hq * B_HEAD_DIM, (hq + 1) * B_HEAD_DIM)
                dqz_ref[:, qsl] = dqs[hk][rows]
                dqz_ref[:, B_WIDTH + hq * B_HEAD_DIM:B_WIDTH + (hq + 1) * B_HEAD_DIM] = dzb[rows]
                dsk_ref[hq:hq + 1, :] += -jnp.sum(dsink[rows], keepdims=True)
        at_cur = pl.ds(pl.multiple_of(n_blk * BLOCK, BLOCK), BLOCK)
        at_prev = pl.ds(pl.multiple_of(jnp.maximum(n_blk - 1, 0) * BLOCK, BLOCK), BLOCK)
        dk_ref[at_prev, :] += jnp.concatenate(dk_prev, axis=1)
        dv_ref[at_prev, :] += jnp.concatenate(dv_prev, axis=1)
        dk_ref[at_cur, :] += jnp.concatenate(dk_cur, axis=1)
        dv_ref[at_cur, :] += jnp.concatenate(dv_cur, axis=1)

    narrow = jax.ShapeDtypeStruct((t, B_KV_WIDTH), F32)
    res = lambda a, b: pl.BlockSpec((a, b), lambda i: (0, 0))
    outs = pl.pallas_call(
        body, name=name, grid=(t // BLOCK,),
        in_specs=[qspec(C_QB), cur(C_KB), prev(C_KB), cur(C_VB), prev(C_VB), qspec(C_ZB),
                  pl.BlockSpec((BLOCK, B_WIDTH), lambda i: (i, 1)), res(B_Q_HEADS, LANE), _ANY] + c_in_specs,
        out_specs=[pl.BlockSpec((BLOCK, 2 * B_WIDTH), lambda i: (i, C_QB // (2 * B_WIDTH))),
                   res(t, B_KV_WIDTH), res(t, B_KV_WIDTH), res(B_Q_HEADS, LANE)] + c_out_specs,
        out_shape=[jax.ShapeDtypeStruct(dh.shape, F32), narrow, narrow,
                   jax.ShapeDtypeStruct((B_Q_HEADS, LANE), F32)] + c_outs,
        scratch_shapes=c_scratch,
        input_output_aliases={8: 0},
        compiler_params=_cp("arbitrary"))(h, h, h, h, h, h, dm, sinks_b, dh, *c_ins)
    return outs[:4], outs[4:]


def _in_proj_dw(dh_main, dh_tail, x, *, tk, name):
    t, n = x.shape

    def body(a_ref, t_ref, x_ref, o_ref, ot_ref):
        @pl.when(pl.program_id(0) == 0)
        def _():
            o_ref[...] = jnp.zeros_like(o_ref)
            ot_ref[...] = jnp.zeros_like(ot_ref)

        xb = x_ref[...].astype(BF16)
        o_ref[...] += _dot_tn(a_ref[...], xb)
        ot_ref[...] += _dot_tn(t_ref[...], xb)

    row = lambda a: pl.BlockSpec((tk, a.shape[1]), lambda kk: (kk, 0))
    acc = lambda a: pl.BlockSpec((a.shape[1], n), lambda kk: (0, 0))
    return pl.pallas_call(
        body, name=name, grid=(t // tk,), in_specs=[row(dh_main), row(dh_tail), row(x)],
        out_specs=[acc(dh_main), acc(dh_tail)],
        out_shape=[jax.ShapeDtypeStruct((a.shape[1], n), F32) for a in (dh_main, dh_tail)],
        compiler_params=_cp("arbitrary"))(dh_main, dh_tail, x)


def _in_proj_dx(dh_main, dh_tail, wt, dr, *, tm, name, carry=None):
    t, n_main = dh_main.shape
    n_tail = dh_tail.shape[1]
    c_ins, c_in_specs, c_out_specs, c_outs, c_scratch = _carry_specs(carry)

    def body(*refs):
        a_ref, t_ref, wa_ref, wt_ref, r_ref, o_ref = _carried(carry, refs, 5, 1, t // tm)
        o_ref[...] = _dot(a_ref[...], wa_ref[...]) + _dot(t_ref[...], wt_ref[...]) + DEEPNORM_ALPHA * r_ref[...]

    row = lambda w: pl.BlockSpec((tm, w), lambda i: (i, 0))
    outs = pl.pallas_call(
        body, name=name, grid=(t // tm,),
        in_specs=[row(n_main), row(n_tail), pl.BlockSpec((n_main, D_MODEL), lambda i: (0, 0)),
                  pl.BlockSpec((n_tail, D_MODEL), lambda i: (n_main // n_tail, 0)), row(D_MODEL)] + c_in_specs,
        out_specs=[row(D_MODEL)] + c_out_specs,
        out_shape=[jax.ShapeDtypeStruct((t, D_MODEL), F32)] + c_outs,
        scratch_shapes=c_scratch,
        compiler_params=_cp("arbitrary"))(dh_main, dh_tail, wt, wt, dr, *c_ins)
    return outs[0], outs[1:]


def _layer_bwd(dxn, res, wt, conv_w, par, sinks_b, norm_w, w_out_bf, ln_g, l, carries=None, carry_dx=None):
    carries = carries or {}
    w_out_bf = res["w_out"]
    dr, dm, dw_out, dln_g, dln_b = _ln_out_bwd(dxn, res["r"], res["mixed"], ln_g, w_out_bf, tm=512, name=f"ln_out_bwd_{l}")
    h = res["h"]
    do, dh, dnw = _dn_post_bwd(dm, res["oa"], h, norm_w, tm=512, name=f"dn_post_bwd_{l}")
    dvn, ds_all = _dn_scan_bwd(res["q"], res["k"], res["w"], res["qk"], res["bg"], do, name=f"dn_scan_bwd_{l}")
    (dq, dk, dv, dbg, dbgt), got_chunk = _dn_chunk_bwd(
        res["q"], res["k"], res["v"], res["vn"], res["tmat"], res["qk"], res["bg"], res["bgt"], res["s_all"], ds_all,
        dvn, do, name=f"dn_chunk_bwd_{l}", carry=carries.get("dn_chunk"))
    dc, dbgi, dpar = _dn_pre_bwd(h, conv_w, par, dq, dk, dv, dbg, dbgt, tt=512, name=f"dn_pre_bwd_{l}")
    dh, dcw = _conv_bwd(dc, h, conv_w, dh, tt=512, name=f"conv_bwd_{l}")
    (dh, dkb, dvb, dsk), got_swa = _swa_bwd(h, dm, sinks_b, dh, name=f"swa_bwd_{l}", carry=carries.get("swa"))
    carried = dict(dn_chunk=got_chunk, swa=got_swa)
    dh_tail = jnp.concatenate([dkb, dvb, dbgi], axis=1)
    dwt_main, dwt_tail = _in_proj_dw(dh, dh_tail, res["x"], tk=512, name=f"in_proj_dw_{l}")
    grads = dict(w_in=(dwt_main, dwt_tail), conv_w=dcw[:CONV_K], a_log=dpar[0, A_HEADS:2 * A_HEADS],
                 dt_bias=dpar[1, A_HEADS:2 * A_HEADS], norm_w=dnw[0], sinks=dsk[:, 0], w_out=dw_out,
                 ln_g=dln_g[0], ln_b=dln_b[0])
    dx, carried_dx = _in_proj_dx(dh, dh_tail, wt, dr, tm=512, name=f"in_proj_dx_{l}",
                                 carry=None if carry_dx is None else carry_dx(grads))
    return dx, grads, carried, carried_dx


def _layer_args(wt, conv_w, a_log, dt_bias, sinks, norm_w, w_out_bf):
    return (wt, conv_w, _gate_params(a_log, dt_bias), jnp.broadcast_to(sinks[:, None], (B_Q_HEADS, LANE)),
            norm_w[None], w_out_bf)


def _local_step(x, target, args0, args1, ln_g, ln_b, gathers=None, reduce1=None, reduce0=None):
    assert DEPTH == 2
    x1, res0, got = _layer_fwd(x, *args0, ln_g[0][None], ln_b[0][None], 0, carries=gathers)
    if gathers is not None:
        args1 = args1(got)
    (dx, loss_tile), res1, _ = _layer_fwd(x1, *args1, ln_g[1][None], ln_b[1][None], 1, target=target)
    dx, grads1, _, _ = _layer_bwd(dx, res1, *args1, ln_g[1][None], 1)
    carries = None if reduce1 is None else reduce1(grads1)
    carry_dx = None if reduce0 is None else (lambda grads0: reduce0(grads0, grads1, loss_tile))
    dx, grads0, landed1, landed0 = _layer_bwd(dx, res0, *args0, ln_g[0][None], 0, carries=carries, carry_dx=carry_dx)
    return loss_tile, dx, [grads0, grads1], landed1, landed0


_ANY = pl.BlockSpec(memory_space=pl.ANY)
_MESH = pl.DeviceIdType.MESH


HALF = D_MODEL // 2


class _Exchange:
    def __init__(self, ins, outs, n_remote, n_local, plan):
        self.ins, self.outs, self.n_remote, self.n_local, self.plan = tuple(ins), tuple(outs), n_remote, n_local, plan

    def scratch(self):
        return [pltpu.SemaphoreType.DMA((self.n_remote,)), pltpu.SemaphoreType.DMA((self.n_remote,)),
                pltpu.SemaphoreType.DMA((max(self.n_local, 1),))]

    def _copies(self, in_refs, out_refs, sems, arriving):
        send_sems, recv_sems, local_sems = sems
        local, sends, recvs = self.plan(in_refs, out_refs)
        loc = [pltpu.make_async_copy(s, d, local_sems.at[i]) for i, (s, d) in enumerate(local)]
        rem = [pltpu.make_async_remote_copy(src_ref=s, dst_ref=recvs[i] if arriving else d, send_sem=send_sems.at[i],
                                            recv_sem=recv_sems.at[i], device_id=peer, device_id_type=_MESH)
               for i, (s, d, peer) in enumerate(sends)]
        return loc, rem

    def start(self, in_refs, out_refs, sems):
        loc, rem = self._copies(in_refs, out_refs, sems, arriving=False)
        for cp in loc + rem:
            cp.start()

    def finish(self, in_refs, out_refs, sems):
        loc, rem = self._copies(in_refs, out_refs, sems, arriving=True)
        for cp in rem:
            cp.wait_recv()
        for cp in rem:
            cp.wait_send()
        for cp in loc:
            cp.wait()


def _run_exchange(ex, *, name):
    n_in, n_out = len(ex.ins), len(ex.outs)

    def body(*refs):
        parts = refs[:n_in], refs[n_in:n_in + n_out], refs[n_in + n_out:]
        ex.start(*parts)
        ex.finish(*parts)

    return pl.pallas_call(body, name=name, in_specs=[_ANY] * n_in, out_specs=[_ANY] * n_out, out_shape=list(ex.outs),
                          scratch_shapes=ex.scratch())(*ex.ins)


def _place():
    x, y, c = lax.axis_index("x"), lax.axis_index("y"), lax.axis_index("c")
    return x, y, c, [(1 - x, y), (x, 1 - y), (1 - x, 1 - y)]


def _gather_exchange(arrays):
    n = len(arrays)

    def plan(src, dst):
        x, y, c, chips = _place()
        me = 2 * x + y
        local = [(src[k], dst[k].at[me]) for k in range(n)]
        sends = [(src[k], dst[k].at[me], (px, py, c)) for k in range(n) for px, py in chips]
        recvs = [dst[k].at[2 * px + py] for k in range(n) for px, py in chips]
        return local, sends, recvs

    return _Exchange(arrays, [jax.ShapeDtypeStruct((N_SHARD,) + a.shape, a.dtype) for a in arrays], 3 * n, n, plan)


def _gather_two_level(pack, conv_w, *, name):
    rows = pack.shape[0]
    part_rows = rows // 2

    def body(pack_ref, conv_ref, land_ref, conv_land_ref, send1, recv1, send2, recv2, csend, crecv, local_sems):
        x, y, c, chips = _place()
        me = 2 * x + y
        sibling = (x, y, 1 - c)
        part = lambda core: pl.ds(pl.multiple_of(core * part_rows, 16), part_rows)
        remote = lambda src, dst, ss, rs, to: pltpu.make_async_remote_copy(
            src_ref=src, dst_ref=dst, send_sem=ss, recv_sem=rs, device_id=to, device_id_type=_MESH)
        local = [pltpu.make_async_copy(pack_ref, land_ref.at[me], local_sems.at[0]),
                 pltpu.make_async_copy(conv_ref, conv_land_ref.at[me], local_sems.at[1])]
        for cp in local:
            cp.start()
        first = [remote(pack_ref.at[part(c)], land_ref.at[me, part(c)], send1.at[j], recv1.at[j], (px, py, c))
                 for j, (px, py) in enumerate(chips)]
        convs = [remote(conv_ref, conv_land_ref.at[me], csend.at[j], crecv.at[j], (px, py, c))
                 for j, (px, py) in enumerate(chips)]
        for cp in first + convs:
            cp.start()
        passed = []
        for j, (px, py) in enumerate(chips):
            slot = 2 * px + py
            remote(pack_ref.at[part(c)], land_ref.at[slot, part(c)], send1.at[j], recv1.at[j], (px, py, c)).wait_recv()
            cp = remote(land_ref.at[slot, part(c)], land_ref.at[slot, part(c)], send2.at[j], recv2.at[j], sibling)
            cp.start()
            passed.append(cp)
        for j, (px, py) in enumerate(chips):
            slot = 2 * px + py
            remote(land_ref.at[slot, part(1 - c)], land_ref.at[slot, part(1 - c)], send2.at[j], recv2.at[j],
                   sibling).wait_recv()
            remote(conv_ref, conv_land_ref.at[slot], csend.at[j], crecv.at[j], (px, py, c)).wait_recv()
        for cp in first + convs + passed:
            cp.wait_send()
        for cp in local:
            cp.wait()

    sems = [pltpu.SemaphoreType.DMA((3,))] * 6 + [pltpu.SemaphoreType.DMA((2,))]
    return pl.pallas_call(
        body, name=name, in_specs=[_ANY, _ANY], out_specs=[_ANY, _ANY],
        out_shape=[jax.ShapeDtypeStruct((N_SHARD,) + pack.shape, pack.dtype),
                   jax.ShapeDtypeStruct((N_SHARD,) + conv_w.shape, conv_w.dtype)],
        scratch_shapes=sems)(pack, conv_w)


def _half(core):
    return pl.ds(pl.multiple_of(core * HALF, HALF), HALF)


def _reduce_scatter_exchange(g, row0, rows):
    def plan(src, dst):
        x, y, c, chips = _place()
        peers = [(px, py, c if t == 0 else 1 - c) for px, py in chips for t in (0, 1)] + [(x, y, 1 - c)]
        sends = [(src[0].at[2 * px + py, pl.ds(row0, rows), _half(pc)], dst[0].at[k], (px, py, pc))
                 for k, (px, py, pc) in enumerate(peers)]
        return [], sends, [dst[0].at[k] for k in range(7)]

    return _Exchange([g], [jax.ShapeDtypeStruct((7, rows, HALF), g.dtype)], 7, 0, plan)


def _pair_window_exchange(g):
    def plan(src, dst):
        x, y, c, _ = _place()
        return [], [(src[0].at[:, :, _half(1 - c)], dst[0], (x, y, 1 - c))], [dst[0]]

    return _Exchange([g], [jax.ShapeDtypeStruct(g.shape[:2] + (HALF,), g.dtype)], 1, 0, plan)


def _chip_scatter_exchange(p, small):
    def plan(src, dst):
        x, y, c, chips = _place()
        mine = 4 * x + 2 * y + c
        peers = [(px, py, c if t == 0 else 1 - c) for px, py in chips for t in (0, 1)] + [(x, y, 1 - c)]
        sends = [(src[0].at[2 * px + py], dst[0].at[j], (px, py, c)) for j, (px, py) in enumerate(chips)]
        recvs = [dst[0].at[j] for j in range(3)]
        sends += [(src[1], dst[1].at[mine], peer) for peer in peers]
        recvs += [dst[1].at[4 * px + 2 * py + pc] for px, py, pc in peers]
        return [(src[1], dst[1].at[mine])], sends, recvs

    outs = [jax.ShapeDtypeStruct((3,) + p.shape[1:], p.dtype), jax.ShapeDtypeStruct((8,) + small.shape, small.dtype)]
    return _Exchange([p, small], outs, 10, 1, plan)


def _share_exchange(arrays):
    n = len(arrays)

    def plan(src, dst):
        x, y, c, _ = _place()
        return [], [(src[k], dst[k], (x, y, 1 - c)) for k in range(n)], [dst[k] for k in range(n)]

    return _Exchange(arrays, [jax.ShapeDtypeStruct(a.shape, a.dtype) for a in arrays], n, 0, plan)


def _sum_scatter(g, lands, me, core, *, tc, name):
    rows = g.shape[1]
    per = HALF // tc
    n = len(lands)

    def body(*refs):
        g_ref, land_refs, o_ref = refs[1], refs[2:2 + n], refs[2 + n]
        at = 0
        for land_ref in land_refs:
            run = slice(at, at + land_ref.shape[1])
            acc = g_ref[run, :].astype(F32)
            for k in range(7):
                acc = acc + land_ref[k].astype(F32)
            o_ref[run, :] = acc
            at = run.stop

    return pl.pallas_call(
        body, name=name, out_shape=jax.ShapeDtypeStruct((rows, HALF), F32), compiler_params=_cp("parallel"),
        grid_spec=pltpu.PrefetchScalarGridSpec(
            num_scalar_prefetch=1, grid=(per,),
            in_specs=[pl.BlockSpec((None, rows, tc), lambda i, w: (w[0], 0, w[1] * per + i))]
            + [pl.BlockSpec((7, a.shape[1], tc), lambda i, w: (0, 0, i)) for a in lands],
            out_specs=pl.BlockSpec((rows, tc), lambda i, w: (0, i))))(
        jnp.stack([me, core]).astype(jnp.int32), g, *lands)


def _pair_add(g, land, core, *, name):
    n, rows, _ = g.shape

    def body(core_ref, g_ref, land_ref, o_ref):
        o_ref[...] = (g_ref[...].astype(F32) + land_ref[...].astype(F32)).astype(o_ref.dtype)

    blk = pl.BlockSpec((1, rows, HALF), lambda i, w: (i, 0, 0))
    return pl.pallas_call(
        body, name=name, out_shape=jax.ShapeDtypeStruct((n, rows, HALF), g.dtype), compiler_params=_cp("parallel"),
        grid_spec=pltpu.PrefetchScalarGridSpec(
            num_scalar_prefetch=1, grid=(n,),
            in_specs=[pl.BlockSpec((1, rows, HALF), lambda i, w: (i, 0, w[0])), blk], out_specs=blk))(
        jnp.reshape(core, (1,)).astype(jnp.int32), g, land)


def _sum_chips(p, land, me, *, tc, name):
    rows = p.shape[1]

    def body(me_ref, p_ref, land_ref, o_ref):
        acc = p_ref[...].astype(F32)
        for k in range(3):
            acc = acc + land_ref[k].astype(F32)
        o_ref[...] = acc

    return pl.pallas_call(
        body, name=name, out_shape=jax.ShapeDtypeStruct((rows, HALF), F32), compiler_params=_cp("parallel"),
        grid_spec=pltpu.PrefetchScalarGridSpec(
            num_scalar_prefetch=1, grid=(HALF // tc,),
            in_specs=[pl.BlockSpec((None, rows, tc), lambda i, w: (w[0], 0, i)),
                      pl.BlockSpec((3, rows, tc), lambda i, w: (0, 0, i))],
            out_specs=pl.BlockSpec((rows, tc), lambda i, w: (0, i))))(
        jnp.reshape(me, (1,)).astype(jnp.int32), p, land)


def _sum_slots(a, *, name):
    n = a.shape[0]

    def body(a_ref, o_ref):
        acc = a_ref[0]
        for k in range(1, n):
            acc = acc + a_ref[k]
        o_ref[...] = acc

    return pl.pallas_call(body, name=name, out_shape=jax.ShapeDtypeStruct(a.shape[1:], a.dtype))(a)


def _elementwise(fn, ins, n_out, block, *, name):
    shape = ins[0].shape
    grid = tuple(s // b for s, b in zip(shape, block))
    n_in = len(ins)

    def body(*refs):
        outs = fn(*[r[...] for r in refs[:n_in]])
        for o_ref, val in zip(refs[n_in:], outs):
            o_ref[...] = val

    spec = pl.BlockSpec(block, lambda i, j, k: (i, j, k))
    return pl.pallas_call(body, name=name, grid=grid, in_specs=[spec] * n_in, out_specs=[spec] * n_out,
                          out_shape=[jax.ShapeDtypeStruct(shape, F32)] * n_out,
                          compiler_params=_cp(*["parallel"] * 3))(*ins)


def _adamw_math(w, g, m, v):
    mn = ADAM_B1 * m + (1.0 - ADAM_B1) * g
    vn = ADAM_B2 * v + (1.0 - ADAM_B2) * (g * g)
    m_hat = mn / (1.0 - ADAM_B1 ** ADAM_STEP)
    v_hat = vn / (1.0 - ADAM_B2 ** ADAM_STEP)
    return -ADAM_LR * (m_hat / (jnp.sqrt(v_hat) + ADAM_EPS) + ADAM_WD * w), mn, vn


def _adamw(w, g, m, v, block, *, name):
    return _elementwise(_adamw_math, [w, g, m, v], 3, block, name=name)


def _interleave_layers(layers, *, tc, name):
    rows, cols = layers[0].shape
    n = len(layers)

    def body(*refs):
        for l in range(n):
            refs[n][:, l, :] = refs[l][...]

    return pl.pallas_call(body, name=name, grid=(cols // tc,),
                          in_specs=[pl.BlockSpec((rows, tc), lambda i: (0, i))] * n,
                          out_specs=pl.BlockSpec((rows, n, tc), lambda i: (0, 0, i)),
                          out_shape=jax.ShapeDtypeStruct((rows, n, cols), layers[0].dtype),
                          compiler_params=_cp("parallel"))(*layers)


def _adamw_small(ws, gs, ms, vs, *, name):
    n = len(ws)

    def body(*refs):
        w, g, m, v, outs = refs[:n], refs[n:2 * n], refs[2 * n:3 * n], refs[3 * n:4 * n], refs[4 * n:]
        for k in range(n):
            for slot, val in enumerate(_adamw_math(w[k][...], g[k][...], m[k][...], v[k][...])):
                outs[slot * n + k][...] = val

    outs = pl.pallas_call(body, name=name, out_shape=[jax.ShapeDtypeStruct(a.shape, F32) for a in ws] * 3)(
        *ws, *gs, *ms, *vs)
    return outs[:n], outs[n:2 * n], outs[2 * n:]


def _to_kernel_order(wt):
    gates = jnp.pad(wt[2048:2056], ((0, LANE - 2 * A_HEADS), (0, 0)))
    return jnp.concatenate([wt[0:2048], wt[2056:2568], wt[2824:3336], wt[2568:2696], wt[2696:2824], gates], axis=0)


def _from_kernel_order(main, tail):
    return jnp.concatenate([main[0:2048], tail[C_BG - DH_MAIN:C_BG - DH_MAIN + 2 * A_HEADS],
                            main[C_QB:C_QB + B_WIDTH], tail[0:B_KV_WIDTH], tail[B_KV_WIDTH:2 * B_KV_WIDTH],
                            main[C_ZB:C_ZB + B_WIDTH]], axis=0)


def _gate_params(a_log, dt_bias):
    return jnp.pad(jnp.stack([a_log, dt_bias]), ((0, SUBLANE - 2), (A_HEADS, LANE - 2 * A_HEADS)))


SMALL = ("conv_w", "a_log", "dt_bias", "norm_w", "sinks", "ln_g", "ln_b")


def _pack(parts, cols):
    flat = jnp.concatenate([p.reshape(-1) for p in parts])
    rows = -(-flat.shape[0] // cols)
    return jnp.pad(flat, (0, rows * cols - flat.shape[0])).reshape(rows, cols)


def _unpack(packed, shapes):
    flat = packed.reshape(-1)
    out, at = [], 0
    for s in shapes:
        n = math.prod(s)
        out.append(flat[at:at + n].reshape(s))
        at += n
    return out


def kernel(x, w_in, conv_w, a_log, dt_bias, norm_w, sinks, w_out, ln_g, ln_b, loss_target, m_w_in, m_conv_w, m_a_log, m_dt_bias, m_norm_w, m_sinks, m_w_out, m_ln_g, m_ln_b, v_w_in, v_conv_w, v_a_log, v_dt_bias, v_norm_w, v_sinks, v_w_out, v_ln_g, v_ln_b):
    xi, yi, ci = lax.axis_index("x"), lax.axis_index("y"), lax.axis_index("c")
    me = 2 * xi + yi

    to_t = lambda a: jnp.transpose(a, (2, 0, 1))
    from_t = lambda a: jnp.transpose(a, (1, 2, 0))

    wt_shard = to_t(w_in)

    def pack_weights(l):
        rows = jnp.pad(wt_shard[:, l], ((0, IN_PAD - IN_SHARD), (0, 0)))
        return jnp.concatenate([rows, w_out[l]], axis=0).astype(BF16)

    pack0, pack1 = pack_weights(0), pack_weights(1)
    got_in0, g_conv = _gather_two_level(pack0[:IN_PAD], conv_w, name="gather_weights_0")
    conv_full = jnp.moveaxis(g_conv, 0, 2).reshape(DEPTH, CONV_K, 3 * A_WIDTH)
    piece = IN_PAD // 3
    carriers = ("dn_pre", "dn_wy", "dn_scan")
    gathers = {nm: _gather_exchange([pack1[i * piece:(i + 1) * piece]]) for i, nm in enumerate(carriers)}
    gathers.update(in_proj=_gather_exchange([pack0[IN_PAD:]]), swa=_gather_exchange([pack1[IN_PAD:]]))
    w_in_of = lambda rows: _to_kernel_order(rows[:, :IN_SHARD].reshape(IN_COLS, D_MODEL))
    w_out_of = lambda rows: rows.reshape(D_MODEL, D_MODEL)
    args0 = _layer_args(w_in_of(got_in0), conv_full[0], a_log[0], dt_bias[0], sinks[0], norm_w[0],
                        lambda got: w_out_of(got[0]))

    def args1(got):
        rows = jnp.concatenate([got[nm][0] for nm in carriers], axis=1)
        return _layer_args(w_in_of(rows), conv_full[1], a_log[1], dt_bias[1], sinks[1], norm_w[1],
                           w_out_of(got["swa"][0]))

    def pack_grads(g):
        gin = _from_kernel_order(*g["w_in"]).reshape(N_SHARD, IN_SHARD, D_MODEL)
        gin = jnp.pad(gin, ((0, 0), (0, IN_PAD - IN_SHARD), (0, 0)))
        return jnp.concatenate([gin, g["w_out"].reshape(N_SHARD, OUT_SHARD, D_MODEL)], axis=1).astype(BF16)

    packed = {}

    def reduce1(grads1):
        packed[1] = pack_grads(grads1)
        half_rows = packed[1].shape[1] // 2
        return dict(dn_chunk=_reduce_scatter_exchange(packed[1], 0, half_rows),
                    swa=_reduce_scatter_exchange(packed[1], half_rows, half_rows))

    def reduce0(grads0, grads1, loss_tile):
        g0 = pack_grads(grads0)
        from_sibling = _run_exchange(_pair_window_exchange(g0), name="pair_reduce_0")[0]
        packed[0] = _pair_add(g0, from_sibling, ci, name="pair_add_0")
        gsmall = _pack([jnp.stack([g[nm] for g in (grads0, grads1)]) for nm in SMALL] + [loss_tile[0, 0:1]], D_MODEL)
        return _chip_scatter_exchange(packed[0], gsmall)

    _, dx, grads, landed1, (landed0, landed_small) = _local_step(
        x[0], loss_target[0], args0, args1, ln_g, ln_b, gathers=gathers, reduce1=reduce1, reduce0=reduce0)

    small_shapes = [(DEPTH,) + grads[0][nm].shape for nm in SMALL]
    halves = [_sum_chips(packed[0], landed0, me, tc=2 * LANE, name="reduce_sum_0"),
              _sum_scatter(packed[1], [landed1["dn_chunk"][0], landed1["swa"][0]], me, ci, tc=2 * LANE,
                           name="reduce_sum_1")]
    s_small = _sum_slots(landed_small, name="reduce_sum_small")
    others = _run_exchange(_share_exchange(halves), name="pair_share")
    full = [jnp.where(ci == 0, jnp.concatenate([mine, other], axis=1), jnp.concatenate([other, mine], axis=1))
            for mine, other in zip(halves, others)]
    grad_in_layers = [f[:IN_SHARD] for f in full]
    grad_out = jnp.stack([f[IN_PAD:] for f in full])
    out_blk = (1, OUT_SHARD, D_MODEL)
    *small_grads, loss = _unpack(s_small, small_shapes + [()])
    gs = dict(zip(SMALL, small_grads))
    gs["conv_w"] = lax.dynamic_slice_in_dim(gs["conv_w"], me * CONV_SHARD, CONV_SHARD, axis=2)

    grad_in_t = _interleave_layers(grad_in_layers, tc=2 * LANE, name="grad_in_layers")
    d_in, nm_in, nv_in = (from_t(o) for o in _adamw(to_t(w_in), grad_in_t, to_t(m_w_in), to_t(v_w_in),
                                                    (IN_SHARD // 6, DEPTH, D_MODEL), name="adamw_in"))
    grad_in = from_t(grad_in_t)
    d_out, nm_out, nv_out = _adamw(w_out, grad_out, m_w_out, v_w_out, out_blk, name="adamw_out")
    ws = dict(conv_w=conv_w, a_log=a_log, dt_bias=dt_bias, norm_w=norm_w, sinks=sinks, ln_g=ln_g, ln_b=ln_b)
    ms = dict(conv_w=m_conv_w, a_log=m_a_log, dt_bias=m_dt_bias, norm_w=m_norm_w, sinks=m_sinks, ln_g=m_ln_g, ln_b=m_ln_b)
    vs = dict(conv_w=v_conv_w, a_log=v_a_log, dt_bias=v_dt_bias, norm_w=v_norm_w, sinks=v_sinks, ln_g=v_ln_g, ln_b=v_ln_b)
    d_s, nm_s, nv_s = (dict(zip(SMALL, o)) for o in _adamw_small(*[[d[nm] for nm in SMALL] for d in (ws, gs, ms, vs)],
                                                                 name="adamw_small"))

    def in_order(big_in, small, big_out):
        return (big_in, small["conv_w"], small["a_log"], small["dt_bias"], small["norm_w"], small["sinks"], big_out,
                small["ln_g"], small["ln_b"])

    return (loss, dx[None], *in_order(grad_in, gs, grad_out), *in_order(d_in, d_s, d_out),
            *in_order(nm_in, nm_s, nm_out), *in_order(nv_in, nv_s, nv_out))
```

```python
import math

import jax
import jax.numpy as jnp
from jax import lax
from jax.experimental import pallas as pl
from jax.experimental.pallas import tpu as pltpu

F32 = jnp.float32
BF16 = jnp.bfloat16
HI = lax.Precision.HIGHEST

D_MODEL = 1024
DEPTH = 2
A_HEADS = 4
A_HEAD_DIM = 128
A_WIDTH = 512
CONV_K = 4
CHUNK = 64
B_Q_HEADS = 8
B_KV_HEADS = 2
B_HEAD_DIM = 64
B_GROUP = 4
B_WIDTH = 512
B_KV_WIDTH = 128
BLOCK = 128
IN_COLS = 3336
DEEPNORM_ALPHA = (2 * DEPTH) ** 0.25
LN_EPS = 1e-5
RMS_EPS = 1e-6
L2_EPS = 1e-6
ADAM_LR = 0.001
ADAM_B1 = 0.9
ADAM_B2 = 0.999
ADAM_EPS = 1e-08
ADAM_WD = 0.01
ADAM_STEP = 10

N_SHARD = 4
IN_SHARD = IN_COLS // N_SHARD
OUT_SHARD = D_MODEL // N_SHARD
CONV_SHARD = 3 * A_WIDTH // N_SHARD
IN_PAD = -(-IN_SHARD // 96) * 96

P_COLS = 3456
C_PRE = 0
C_ZA = 1536
C_QB = 2048
C_ZB = 2560
C_KB = 3072
C_VB = 3200
C_BG = 3328
DH_MAIN = C_KB
LANE = 128
SUBLANE = 8
HALO = 16
VMEM_LIMIT = 56 * 1024 * 1024
ALIBI = tuple(2.0 ** (-8.0 * (h + 1) / B_Q_HEADS) for h in range(B_Q_HEADS))
NEG = -1e30


def _cp(*sem):
    return pltpu.CompilerParams(dimension_semantics=sem, vmem_limit_bytes=VMEM_LIMIT)


def _dot(a, b):
    return jnp.dot(a.astype(BF16), b.astype(BF16), preferred_element_type=F32)


def _dot_nt(a, b):
    return lax.dot_general(a.astype(BF16), b.astype(BF16), (((1,), (1,)), ((), ())),
                           preferred_element_type=F32)


def _dot_tn(a, b):
    return lax.dot_general(a.astype(BF16), b.astype(BF16), (((0,), (0,)), ((), ())),
                           preferred_element_type=F32)


def _dot_hi(a, b):
    return jnp.dot(a, b, precision=HI, preferred_element_type=F32)


def _sigmoid(x):
    return jax.nn.sigmoid(x)


def _silu(x):
    return x * _sigmoid(x)


def _silu_and_grad(x):
    s = _sigmoid(x)
    return x * s, s * (1.0 + x * (1.0 - s))


def _softplus(x):
    return jnp.maximum(x, 0.0) + jnp.log(1.0 + jnp.exp(-jnp.abs(x)))


def _shift_down(cur, before, s):
    if s == 0:
        return cur
    r = pltpu.roll(cur, s, 0)
    rb = pltpu.roll(before, s, 0)
    row = lax.broadcasted_iota(jnp.int32, before.shape, 0)
    head = jnp.where(row < s, rb, r[0:SUBLANE])
    return jnp.concatenate([head, r[SUBLANE:]], axis=0)


def _shift_up(cur, after, s):
    if s == 0:
        return cur
    n = cur.shape[0]
    r = pltpu.roll(cur, n - s, 0)
    ra = pltpu.roll(after, SUBLANE - s, 0)
    row = lax.broadcasted_iota(jnp.int32, after.shape, 0)
    tail = jnp.where(row >= SUBLANE - s, ra, r[n - SUBLANE:])
    return jnp.concatenate([r[:n - SUBLANE], tail], axis=0)


def _conv_fwd(cur, before, w):
    acc = cur * w[CONV_K - 1:CONV_K, :]
    for s in range(1, CONV_K):
        acc = acc + _shift_down(cur, before, s) * w[CONV_K - 1 - s:CONV_K - s, :]
    return acc


def _matmul_nt(a, bt, *, tm, name, carry=None):
    m, k = a.shape
    n = bt.shape[0]
    c_ins, c_in_specs, c_out_specs, c_outs, c_scratch = _carry_specs(carry)

    def body(*refs):
        a_ref, b_ref, o_ref = _carried(carry, refs, 2, 1, m // tm)
        o_ref[...] = _dot_nt(a_ref[...], b_ref[...]).astype(o_ref.dtype)

    outs = pl.pallas_call(
        body, name=name, grid=(m // tm,),
        in_specs=[pl.BlockSpec((tm, k), lambda i: (i, 0)), pl.BlockSpec((n, k), lambda i: (0, 0))] + c_in_specs,
        out_specs=[pl.BlockSpec((tm, n), lambda i: (i, 0))] + c_out_specs,
        out_shape=[jax.ShapeDtypeStruct((m, n), BF16)] + c_outs,
        scratch_shapes=c_scratch,
        compiler_params=_cp("arbitrary"))(a, bt, *c_ins)
    return outs[0], outs[1:]


def _dn_pre(h, conv_w, par, *, tt, name, carry=None):
    t = h.shape[0]
    cw = 3 * A_WIDTH
    hb = tt // HALO

    c_ins, c_in_specs, c_out_specs, c_outs, c_scratch = _carry_specs(carry)

    def body(*refs):
        (pre_ref, halo_ref, bgi_ref, cw_ref, par_ref,
         q_ref, k_ref, v_ref, bg_ref, bgt_ref) = _carried(carry, refs, 5, 5, t // tt)
        i = pl.program_id(0)
        cur = pre_ref[...].astype(F32)
        before = jnp.where(i > 0, halo_ref[...].astype(F32)[HALO - SUBLANE:], 0.0)
        s = _silu(_conv_fwd(cur, before, cw_ref[...]))
        for hd in range(A_HEADS):
            sl = slice(hd * LANE, (hd + 1) * LANE)
            tq = s[:, hd * LANE:(hd + 1) * LANE]
            q_ref[:, sl] = tq * (lax.rsqrt(jnp.sum(tq * tq, -1, keepdims=True) + L2_EPS) * (A_HEAD_DIM ** -0.5))
            tk = s[:, A_WIDTH + hd * LANE:A_WIDTH + (hd + 1) * LANE]
            k_ref[:, sl] = tk * lax.rsqrt(jnp.sum(tk * tk, -1, keepdims=True) + L2_EPS)
        v_ref[...] = s[:, 2 * A_WIDTH:]
        raw = bgi_ref[...].astype(F32)
        lane = lax.broadcasted_iota(jnp.int32, raw.shape, 1)
        is_a = (lane >= A_HEADS) & (lane < 2 * A_HEADS)
        g = jnp.where(is_a, -jnp.exp(par_ref[0:1, :]) * _softplus(raw + par_ref[1:2, :]), 0.0)
        gc = _dot_hi(_chunk_tri(tt, lower=True), g)
        bg = jnp.where(lane < A_HEADS, _sigmoid(raw), gc)
        bg_ref[...] = bg
        bgt_ref[...] = jnp.transpose(bg)[0:SUBLANE, :]

    wide = jax.ShapeDtypeStruct((t, A_WIDTH), F32)
    outs = pl.pallas_call(
        body, name=name, grid=(t // tt,),
        in_specs=[pl.BlockSpec((tt, cw), lambda i: (i, 0)),
                  pl.BlockSpec((HALO, cw), lambda i: (jnp.maximum(i * hb - 1, 0), 0)),
                  pl.BlockSpec((tt, LANE), lambda i: (i, C_BG // LANE)),
                  pl.BlockSpec((CONV_K, cw), lambda i: (0, 0)),
                  pl.BlockSpec((SUBLANE, LANE), lambda i: (0, 0))] + c_in_specs,
        out_specs=[pl.BlockSpec((tt, A_WIDTH), lambda i: (i, 0))] * 3
        + [pl.BlockSpec((tt, LANE), lambda i: (i, 0)), pl.BlockSpec((SUBLANE, tt), lambda i: (0, i))] + c_out_specs,
        out_shape=[wide, wide, wide, jax.ShapeDtypeStruct((t, LANE), F32),
                   jax.ShapeDtypeStruct((SUBLANE, t), F32)] + c_outs,
        scratch_shapes=c_scratch,
        compiler_params=_cp("arbitrary"))(h, h, h, conv_w, par, *c_ins)
    return outs[:5], outs[5:]


def _chunk_tri(n, lower):
    r = lax.broadcasted_iota(jnp.int32, (n, n), 0)
    c = lax.broadcasted_iota(jnp.int32, (n, n), 1)
    shift = CHUNK.bit_length() - 1
    same = jnp.right_shift(r, shift) == jnp.right_shift(c, shift)
    return (same & ((c <= r) if lower else (c >= r))).astype(F32)


def _chunk_masks():
    r = lax.broadcasted_iota(jnp.int32, (CHUNK, CHUNK), 0)
    c = lax.broadcasted_iota(jnp.int32, (CHUNK, CHUNK), 1)
    return r >= c, r > c, r == c


def _split(a):
    hi = a.astype(BF16)
    return hi, (a - hi.astype(F32)).astype(BF16)


def _dot3(a, b):
    (ah, al), (bh, bl) = a, b
    d = lambda p, q: jnp.dot(p, q, preferred_element_type=F32)
    return d(ah, bh) + (d(ah, bl) + d(al, bh))


def _tri_inv_many(a_list, eye):
    d = lambda p, q: jnp.dot(p.astype(BF16), q.astype(BF16), preferred_element_type=F32)
    r = lax.broadcasted_iota(jnp.int32, (CHUNK, CHUNK), 0)
    c = lax.broadcasted_iota(jnp.int32, (CHUNK, CHUNK), 1)
    same = lambda b: jnp.right_shift(r, b.bit_length() - 1) == jnp.right_shift(c, b.bit_length() - 1)
    x = [jnp.where(same(8), -a, 0.0) for a in a_list]
    tm = [eye + xi for xi in x]
    for _ in range(2):
        x = [d(xi, xi) for xi in x]
        tm = [t + d(t, xi) for t, xi in zip(tm, x)]
    for b in (16, 32, 64):
        low = [jnp.where(same(b) & ~same(b // 2), a, 0.0) for a in a_list]
        tm = [t - d(t, d(lo, t)) for t, lo in zip(tm, low)]
    ms = [_split(eye + a) for a in a_list]
    for _ in range(2):
        res = [eye - _dot3(m, _split(t)) for m, t in zip(ms, tm)]
        tm = [t + d(t, rs) for t, rs in zip(tm, res)]
    return tm


def _chunk_gates(bg_v, bgt_v, hd):
    return (bg_v[:, hd:hd + 1], bg_v[:, A_HEADS + hd:A_HEADS + hd + 1],
            None if bgt_v is None else bgt_v[A_HEADS + hd:A_HEADS + hd + 1, :])


WY_ROWS = 512
SCAN_ROWS = 512
WY_GROUP = 8


def _dn_wy(q, k, v, bg, bgt, *, name, carry=None):
    t = q.shape[0]
    rows = WY_ROWS

    c_ins, c_in_specs, c_out_specs, c_outs, c_scratch = _carry_specs(carry)

    def body(*refs):
        q_ref, k_ref, v_ref, bg_ref, bgt_ref, u_ref, w_ref, tm_ref, qk_ref = _carried(carry, refs, 5, 4, t // rows)
        causal, strict, diag = _chunk_masks()
        eye = diag.astype(F32)
        for c0 in range(0, rows // CHUNK, WY_GROUP):
            items = [(c, hd) for c in range(c0, c0 + WY_GROUP) for hd in range(A_HEADS)]
            rs = lambda c: slice(c * CHUNK, (c + 1) * CHUNK)
            sl = lambda hd: slice(hd * LANE, (hd + 1) * LANE)
            hs = lambda hd: slice(hd * CHUNK, (hd + 1) * CHUNK)
            gates = [_chunk_gates(bg_ref[rs(c), :], bgt_ref[:, rs(c)], hd) for c, hd in items]
            dms = [jnp.exp(jnp.where(causal, gcol - grow, NEG)) for _, gcol, grow in gates]
            kbs = [k_ref[rs(c), sl(hd)] * g[0] for (c, hd), g in zip(items, gates)]
            a_list = [jnp.where(strict, _dot_nt(kb, k_ref[rs(c), sl(hd)]) * dm, 0.0)
                      for (c, hd), kb, dm in zip(items, kbs, dms)]
            for (c, hd), dm in zip(items, dms):
                qk_ref[rs(c), hs(hd)] = jnp.where(
                    causal, _dot_nt(q_ref[rs(c), sl(hd)], k_ref[rs(c), sl(hd)]) * dm, 0.0)
            tms = _tri_inv_many(a_list, eye)
            for (c, hd), g, kb, tmat in zip(items, gates, kbs, tms):
                tm_ref[rs(c), hs(hd)] = tmat
                u_ref[rs(c), sl(hd)] = _dot(tmat, v_ref[rs(c), sl(hd)] * g[0])
                w_ref[rs(c), sl(hd)] = _dot(tmat, kb * jnp.exp(g[1])).astype(BF16)

    blk = pl.BlockSpec((rows, A_WIDTH), lambda i: (i, 0))
    half = pl.BlockSpec((rows, A_HEADS * CHUNK), lambda i: (i, 0))
    outs = pl.pallas_call(
        body, name=name, grid=(t // rows,),
        in_specs=[blk, blk, blk, pl.BlockSpec((rows, LANE), lambda i: (i, 0)),
                  pl.BlockSpec((SUBLANE, rows), lambda i: (0, i))] + c_in_specs,
        out_specs=[blk, blk, half, half] + c_out_specs,
        out_shape=[jax.ShapeDtypeStruct((t, A_WIDTH), F32), jax.ShapeDtypeStruct((t, A_WIDTH), BF16),
                   jax.ShapeDtypeStruct((t, A_HEADS * CHUNK), F32),
                   jax.ShapeDtypeStruct((t, A_HEADS * CHUNK), F32)] + c_outs,
        scratch_shapes=c_scratch,
        compiler_params=_cp("arbitrary"))(q, k, v, bg, bgt, *c_ins)
    return outs[:4], outs[4:]


def _dn_scan_fwd(q, k, u, w, qk, bg, *, name, carry=None):
    t = q.shape[0]
    rows = SCAN_ROWS
    per = rows // CHUNK
    c_ins, c_in_specs, c_out_specs, c_outs, c_scratch = _carry_specs(carry)

    def body(*refs):
        q_ref, k_ref, u_ref, w_ref, qk_ref, bg_ref, o_ref, vn_ref, s_ref, state = _carried(carry, refs, 6, 3, t // rows)

        @pl.when(pl.program_id(0) == 0)
        def _():
            state[...] = jnp.zeros_like(state)

        heads = range(A_HEADS)
        sl = lambda hd: slice(hd * LANE, (hd + 1) * LANE)
        s_cur = [state[hd] for hd in heads]
        for c in range(per):
            rs = slice(c * CHUNK, (c + 1) * CHUNK)
            bg_v = bg_ref[rs, :]
            gcols = [_chunk_gates(bg_v, None, hd)[1] for hd in heads]
            glasts = [gc[CHUNK - 1:CHUNK, :] for gc in gcols]
            for hd in heads:
                s_ref[c, hd] = s_cur[hd].astype(BF16)
            vns = [u_ref[rs, sl(hd)] - _dot(w_ref[rs, sl(hd)], s_cur[hd]) for hd in heads]
            qss = [_dot(q_ref[rs, sl(hd)] * jnp.exp(gcols[hd]), s_cur[hd]) for hd in heads]
            s_cur = [s_cur[hd] * jnp.exp(glasts[hd])
                     + _dot_tn(k_ref[rs, sl(hd)] * jnp.exp(glasts[hd] - gcols[hd]), vns[hd]) for hd in heads]
            for hd in heads:
                vn_ref[rs, sl(hd)] = vns[hd]
                o_ref[rs, sl(hd)] = qss[hd] + _dot(qk_ref[rs, hd * CHUNK:(hd + 1) * CHUNK], vns[hd])
        for hd in heads:
            state[hd] = s_cur[hd]

    blk = pl.BlockSpec((rows, A_WIDTH), lambda i: (i, 0))
    half = pl.BlockSpec((rows, A_HEADS * CHUNK), lambda i: (i, 0))
    wide = jax.ShapeDtypeStruct((t, A_WIDTH), F32)
    outs = pl.pallas_call(
        body, name=name, grid=(t // rows,),
        in_specs=[blk, blk, blk, blk, half, pl.BlockSpec((rows, LANE), lambda i: (i, 0))] + c_in_specs,
        out_specs=[blk, blk, pl.BlockSpec((per, A_HEADS, LANE, LANE), lambda i: (i, 0, 0, 0))] + c_out_specs,
        out_shape=[wide, wide, jax.ShapeDtypeStruct((t // CHUNK, A_HEADS, LANE, LANE), BF16)] + c_outs,
        scratch_shapes=[pltpu.VMEM((A_HEADS, LANE, LANE), F32)] + c_scratch,
        compiler_params=_cp("arbitrary"))(q, k, u, w, qk, bg, *c_ins)
    return outs[:3], outs[3:]


def _stack_heads(ref, hk):
    return jnp.concatenate([ref[:, h * B_HEAD_DIM:(h + 1) * B_HEAD_DIM].astype(F32)
                            for h in range(hk * B_GROUP, (hk + 1) * B_GROUP)], axis=0)


def _swa_window():
    qi = lax.broadcasted_iota(jnp.int32, (BLOCK, BLOCK), 0)
    kj = lax.broadcasted_iota(jnp.int32, (BLOCK, BLOCK), 1)
    dist = jnp.where(kj > qi, qi + BLOCK - kj, qi - kj).astype(F32)
    rows = lax.broadcasted_iota(jnp.int32, (B_GROUP * BLOCK, BLOCK), 0)
    cols = lax.broadcasted_iota(jnp.int32, (B_GROUP * BLOCK, BLOCK), 1)
    return cols > jnp.bitwise_and(rows, BLOCK - 1), dist


def _swa_group_probs(q_ref, sk_ref, kp, kc, vp, vc, n_blk):
    hks = range(B_KV_HEADS)
    heads = lambda hk: range(hk * B_GROUP, (hk + 1) * B_GROUP)
    ksl = lambda hk: slice(hk * B_HEAD_DIM, (hk + 1) * B_HEAD_DIM)
    upper, dist = _swa_window()
    no_prev = jnp.where(n_blk > 0, 0.0, NEG)
    ones = jnp.ones((BLOCK, B_HEAD_DIM), BF16)
    with_ones = lambda v, hk: jnp.concatenate([v[:, ksl(hk)].astype(BF16), ones], axis=1)
    qs = [_stack_heads(q_ref, hk) * (B_HEAD_DIM ** -0.5) for hk in hks]
    sink = [jnp.concatenate([jnp.broadcast_to(sk_ref[h:h + 1, 0:1], (BLOCK, 1)) for h in heads(hk)], axis=0)
            for hk in hks]
    s = [jnp.where(upper, _dot_nt(qs[hk], kp[:, ksl(hk)]) + no_prev, _dot_nt(qs[hk], kc[:, ksl(hk)]))
         - jnp.concatenate([ALIBI[h] * dist for h in heads(hk)], axis=0) for hk in hks]
    m = [jnp.maximum(jnp.max(s[hk], axis=-1, keepdims=True), sink[hk]) for hk in hks]
    p = [jnp.exp(s[hk] - m[hk]) for hk in hks]
    p_up = [jnp.where(upper, p[hk], 0.0) for hk in hks]
    oe = [jnp.dot(p_up[hk].astype(BF16), with_ones(vp, hk), preferred_element_type=F32)
          + jnp.dot((p[hk] - p_up[hk]).astype(BF16), with_ones(vc, hk), preferred_element_type=F32) for hk in hks]
    ps = [jnp.exp(sink[hk] - m[hk]) for hk in hks]
    inv = [1.0 / (oe[hk][:, B_HEAD_DIM:B_HEAD_DIM + 1] + ps[hk]) for hk in hks]
    return upper, [(qs[hk], p[hk] * inv[hk], ps[hk] * inv[hk], oe[hk][:, :B_HEAD_DIM] * inv[hk]) for hk in hks]


def _swa_specs():
    qspec = lambda c0: pl.BlockSpec((BLOCK, B_WIDTH), lambda i: (i, c0 // B_WIDTH))
    cur = lambda c0: pl.BlockSpec((BLOCK, LANE), lambda i: (i, c0 // LANE))
    prev = lambda c0: pl.BlockSpec((BLOCK, LANE), lambda i: (jnp.maximum(i - 1, 0), c0 // LANE))
    return qspec, cur, prev


def _carried(carry, refs, n_in, n_out, steps):
    if carry is None:
        return refs
    ci, co = len(carry.ins), len(carry.outs)
    own = refs[:n_in] + refs[n_in + ci:n_in + ci + n_out] + refs[n_in + ci + n_out + co:len(refs) - 3]
    parts = refs[n_in:n_in + ci], refs[n_in + ci + n_out:n_in + ci + n_out + co], refs[len(refs) - 3:]

    @pl.when(pl.program_id(0) == 0)
    def _():
        carry.start(*parts)

    @pl.when(pl.program_id(0) == steps - 1)
    def _():
        carry.finish(*parts)

    return own


def _carry_specs(carry):
    if carry is None:
        return [], [], [], [], []
    return (list(carry.ins), [_ANY] * len(carry.ins), [_ANY] * len(carry.outs), list(carry.outs), carry.scratch())


def _swa_fwd(h, sinks_b, *, name, carry=None):
    t = h.shape[0]
    qspec, cur, prev = _swa_specs()
    c_ins, c_in_specs, c_out_specs, c_outs, c_scratch = _carry_specs(carry)

    def body(*refs):
        q_ref, kc_ref, kp_ref, vc_ref, vp_ref, sk_ref, o_ref = _carried(carry, refs, 6, 1, t // BLOCK)
        n_blk = pl.program_id(0)
        _, groups = _swa_group_probs(q_ref, sk_ref, kp_ref[...], kc_ref[...], vp_ref[...], vc_ref[...], n_blk)
        for hk, (_, _, _, o) in enumerate(groups):
            for g in range(B_GROUP):
                hq = hk * B_GROUP + g
                o_ref[:, hq * B_HEAD_DIM:(hq + 1) * B_HEAD_DIM] = o[g * BLOCK:(g + 1) * BLOCK]

    outs = pl.pallas_call(
        body, name=name, grid=(t // BLOCK,),
        in_specs=[qspec(C_QB), cur(C_KB), prev(C_KB), cur(C_VB), prev(C_VB),
                  pl.BlockSpec((B_Q_HEADS, LANE), lambda i: (0, 0))] + c_in_specs,
        out_specs=[pl.BlockSpec((BLOCK, B_WIDTH), lambda i: (i, 0))] + c_out_specs,
        out_shape=[jax.ShapeDtypeStruct((t, B_WIDTH), F32)] + c_outs,
        scratch_shapes=c_scratch,
        compiler_params=_cp("arbitrary"))(h, h, h, h, h, sinks_b, *c_ins)
    return outs[0], outs[1:]


def _rms_gate(o, za, nw):
    outs = []
    for hd in range(A_HEADS):
        oh = o[:, hd * LANE:(hd + 1) * LANE]
        r = lax.rsqrt(jnp.mean(oh * oh, -1, keepdims=True) + RMS_EPS)
        outs.append(oh * r * nw)
    return jnp.concatenate(outs, axis=1) * _silu(za)


def _out_ln(x, oa, ob, h, norm_w, w_out, ln_g, ln_b, *, tm, name, target=None):
    t = x.shape[0]
    last = target is not None

    def body(*refs):
        x_ref, oa_ref, ob_ref, za_ref, zb_ref, nw_ref, w_ref, g_ref, b_ref = refs[:9]
        xn_ref, mx_ref, r_ref = refs[9 + last:12 + last]
        ya = _rms_gate(oa_ref[...], za_ref[...].astype(F32), nw_ref[...])
        yb = ob_ref[...] * _silu(zb_ref[...].astype(F32))
        mixed = jnp.concatenate([ya, yb], axis=1).astype(BF16)
        mx_ref[...] = mixed
        r = DEEPNORM_ALPHA * x_ref[...] + jnp.dot(mixed, w_ref[...], preferred_element_type=F32)
        r_ref[...] = r
        mu = jnp.mean(r, -1, keepdims=True)
        xc = r - mu
        var = jnp.mean(xc * xc, -1, keepdims=True)
        xn = xc * lax.rsqrt(var + LN_EPS) * g_ref[...] + b_ref[...]
        if not last:
            xn_ref[...] = xn
            return
        loss_ref = refs[13]

        @pl.when(pl.program_id(0) == 0)
        def _():
            loss_ref[...] = jnp.zeros_like(loss_ref)

        err = xn - refs[9][...]
        xn_ref[...] = err * (1.0 / D_MODEL)
        loss_ref[...] += 0.5 / D_MODEL * jnp.sum(err * err)

    row = lambda w, c: pl.BlockSpec((tm, w), lambda i: (i, c))
    full = lambda a, b: pl.BlockSpec((a, b), lambda i: (0, 0))
    wide = jax.ShapeDtypeStruct((t, D_MODEL), F32)
    return pl.pallas_call(
        body, name=name, grid=(t // tm,),
        in_specs=[row(D_MODEL, 0), row(A_WIDTH, 0), row(B_WIDTH, 0), row(A_WIDTH, C_ZA // A_WIDTH),
                  row(B_WIDTH, C_ZB // B_WIDTH), full(1, LANE), full(D_MODEL, D_MODEL), full(1, D_MODEL),
                  full(1, D_MODEL)] + [row(D_MODEL, 0)] * last,
        out_specs=[row(D_MODEL, 0), row(D_MODEL, 0), row(D_MODEL, 0)] + [full(SUBLANE, LANE)] * last,
        out_shape=[wide, jax.ShapeDtypeStruct((t, D_MODEL), BF16), wide]
        + [jax.ShapeDtypeStruct((SUBLANE, LANE), F32)] * last,
        compiler_params=_cp("arbitrary" if last else "parallel"))(
        x, oa, ob, h, h, norm_w, w_out, ln_g, ln_b, *([target] if last else []))


def _layer_fwd(x, wt, conv_w, par, sinks_b, norm_w, w_out_bf, ln_g, ln_b, l, carries=None, target=None):
    carries = carries or {}
    h, got_in = _matmul_nt(x, wt, tm=512, name=f"in_proj_{l}", carry=carries.get("in_proj"))
    if callable(w_out_bf):
        w_out_bf = w_out_bf(got_in)
    (q, k, v, bg, bgt), got_pre = _dn_pre(h, conv_w, par, tt=512, name=f"dn_pre_{l}", carry=carries.get("dn_pre"))
    (u, w, tmat, qk), got_wy = _dn_wy(q, k, v, bg, bgt, name=f"dn_wy_{l}", carry=carries.get("dn_wy"))
    (oa, vn, s_all), got_scan = _dn_scan_fwd(q, k, u, w, qk, bg, name=f"dn_scan_{l}", carry=carries.get("dn_scan"))
    ob, got_swa = _swa_fwd(h, sinks_b, name=f"swa_fwd_{l}", carry=carries.get("swa"))
    xn, mixed, r, *loss = _out_ln(x, oa, ob, h, norm_w, w_out_bf, ln_g, ln_b, tm=512, name=f"out_ln_{l}", target=target)
    if loss:
        xn = (xn, loss[0])
    res = dict(x=x, h=h, q=q, k=k, v=v, bg=bg, bgt=bgt, w=w, tmat=tmat, qk=qk, vn=vn, oa=oa, s_all=s_all,
               mixed=mixed, r=r, w_out=w_out_bf)
    return xn, res, dict(in_proj=got_in, dn_pre=got_pre, dn_wy=got_wy, dn_scan=got_scan, swa=got_swa)


def _ln_out_bwd(dxn, r, mixed, ln_g, w_out, *, tm, name):
    t = dxn.shape[0]

    def body(dxn_ref, r_ref, mx_ref, g_ref, w_ref, dr_ref, dm_ref, dw_ref, dg_ref, db_ref):
        @pl.when(pl.program_id(0) == 0)
        def _():
            dw_ref[...] = jnp.zeros_like(dw_ref)
            dg_ref[...] = jnp.zeros_like(dg_ref)
            db_ref[...] = jnp.zeros_like(db_ref)

        rr = r_ref[...]
        xc = rr - jnp.mean(rr, -1, keepdims=True)
        rstd = lax.rsqrt(jnp.mean(xc * xc, -1, keepdims=True) + LN_EPS)
        xhat = xc * rstd
        dxn_v = dxn_ref[...]
        dxh = dxn_v * g_ref[...]
        dr = rstd * (dxh - jnp.mean(dxh, -1, keepdims=True) - xhat * jnp.mean(dxh * xhat, -1, keepdims=True))
        dr_ref[...] = dr
        dg_ref[...] += jnp.sum(dxn_v * xhat, axis=0, keepdims=True)
        db_ref[...] += jnp.sum(dxn_v, axis=0, keepdims=True)
        drb = dr.astype(BF16)
        dm_ref[...] = _dot_nt(drb, w_ref[...])
        dw_ref[...] += _dot_tn(mx_ref[...], drb)

    row = pl.BlockSpec((tm, D_MODEL), lambda i: (i, 0))
    full = lambda a, b: pl.BlockSpec((a, b), lambda i: (0, 0))
    big = jax.ShapeDtypeStruct((t, D_MODEL), F32)
    vec = jax.ShapeDtypeStruct((1, D_MODEL), F32)
    return pl.pallas_call(
        body, name=name, grid=(t // tm,),
        in_specs=[row, row, row, full(1, D_MODEL), full(D_MODEL, D_MODEL)],
        out_specs=[row, row, full(D_MODEL, D_MODEL), full(1, D_MODEL), full(1, D_MODEL)],
        out_shape=[big, big, jax.ShapeDtypeStruct((D_MODEL, D_MODEL), F32), vec, vec],
        compiler_params=_cp("arbitrary"))(dxn, r, mixed, ln_g, w_out)


def _dn_post_bwd(dm, oa, h, norm_w, *, tm, name):
    t = oa.shape[0]

    def body(dy_ref, o_ref, za_ref, nw_ref, do_ref, dza_ref, dnw_ref):
        @pl.when(pl.program_id(0) == 0)
        def _():
            dnw_ref[...] = jnp.zeros_like(dnw_ref)

        nw = nw_ref[...]
        dnw = jnp.zeros_like(nw)
        for hd in range(A_HEADS):
            sl = slice(hd * LANE, (hd + 1) * LANE)
            oh, za, dy = o_ref[:, sl], za_ref[:, sl].astype(F32), dy_ref[:, sl]
            rs = lax.rsqrt(jnp.mean(oh * oh, -1, keepdims=True) + RMS_EPS)
            nrm = oh * rs
            gate, dgate = _silu_and_grad(za)
            dza_ref[:, sl] = dy * nrm * nw * dgate
            dn = dy * gate
            dnw = dnw + jnp.sum(dn * nrm, axis=0, keepdims=True)
            dnn = dn * nw
            do_ref[:, sl] = rs * dnn - oh * (rs * rs * rs) * jnp.mean(dnn * oh, -1, keepdims=True)
        dnw_ref[...] += dnw

    row = lambda c: pl.BlockSpec((tm, A_WIDTH), lambda i: (i, c))
    wide = jax.ShapeDtypeStruct((t, A_WIDTH), F32)
    return pl.pallas_call(
        body, name=name, grid=(t // tm,),
        in_specs=[row(0), row(0), row(C_ZA // A_WIDTH), pl.BlockSpec((1, LANE), lambda i: (0, 0))],
        out_specs=[row(0), row(C_ZA // A_WIDTH), pl.BlockSpec((1, LANE), lambda i: (0, 0))],
        out_shape=[wide, jax.ShapeDtypeStruct((t, DH_MAIN), F32), jax.ShapeDtypeStruct((1, LANE), F32)],
        compiler_params=_cp("arbitrary"))(dm, oa, h, norm_w)


def _dn_scan_bwd(q, k, w, qk, bg, do, *, name):
    t = q.shape[0]
    rows = SCAN_ROWS
    per = rows // CHUNK
    n = t // rows

    def body(q_ref, k_ref, w_ref, qk_ref, bg_ref, do_ref, dvn_ref, ds_ref, dstate):
        @pl.when(pl.program_id(0) == 0)
        def _():
            dstate[...] = jnp.zeros_like(dstate)

        heads = range(A_HEADS)
        sl = lambda hd: slice(hd * LANE, (hd + 1) * LANE)
        ds_cur = [dstate[hd] for hd in heads]
        for c in reversed(range(per)):
            rs = slice(c * CHUNK, (c + 1) * CHUNK)
            bg_v = bg_ref[rs, :]
            gcols = [_chunk_gates(bg_v, None, hd)[1] for hd in heads]
            glasts = [gc[CHUNK - 1:CHUNK, :] for gc in gcols]
            for hd in heads:
                ds_ref[c, hd] = ds_cur[hd].astype(BF16)
            pdo = [_dot_tn(qk_ref[rs, hd * CHUNK:(hd + 1) * CHUNK], do_ref[rs, sl(hd)]) for hd in heads]
            qdo = [_dot_tn(q_ref[rs, sl(hd)] * jnp.exp(gcols[hd]), do_ref[rs, sl(hd)]) for hd in heads]
            dvns = [pdo[hd] + _dot(k_ref[rs, sl(hd)] * jnp.exp(glasts[hd] - gcols[hd]), ds_cur[hd]) for hd in heads]
            ds_cur = [qdo[hd] + jnp.exp(glasts[hd]) * ds_cur[hd] - _dot_tn(w_ref[rs, sl(hd)], dvns[hd])
                      for hd in heads]
            for hd in heads:
                dvn_ref[rs, sl(hd)] = dvns[hd]
        for hd in heads:
            dstate[hd] = ds_cur[hd]

    blk = pl.BlockSpec((rows, A_WIDTH), lambda i: (n - 1 - i, 0))
    return pl.pallas_call(
        body, name=name, grid=(n,),
        in_specs=[blk, blk, blk, pl.BlockSpec((rows, A_HEADS * CHUNK), lambda i: (n - 1 - i, 0)),
                  pl.BlockSpec((rows, LANE), lambda i: (n - 1 - i, 0)), blk],
        out_specs=[blk, pl.BlockSpec((per, A_HEADS, LANE, LANE), lambda i: (n - 1 - i, 0, 0, 0))],
        out_shape=[jax.ShapeDtypeStruct((t, A_WIDTH), F32),
                   jax.ShapeDtypeStruct((t // CHUNK, A_HEADS, LANE, LANE), BF16)],
        scratch_shapes=[pltpu.VMEM((A_HEADS, LANE, LANE), F32)],
        compiler_params=_cp("arbitrary"))(q, k, w, qk, bg, do)


def _dn_chunk_bwd(q, k, v, vn, tmat, qk, bg, bgt, s_all, ds_all, dvn, do, *, name, carry=None):
    t = q.shape[0]
    rows = WY_ROWS
    per = rows // CHUNK

    c_ins, c_in_specs, c_out_specs, c_outs, c_scratch = _carry_specs(carry)

    def body(*refs):
        (q_ref, k_ref, v_ref, vn_ref, tm_ref, qk_ref, bg_ref, bgt_ref, s_ref, ds_ref, dvn_ref, do_ref,
         dq_ref, dk_ref, dv_ref, dbg_ref, dbgt_ref) = _carried(carry, refs, 12, 5, t // rows)
        causal, strict, _ = _chunk_masks()
        lane = lax.broadcasted_iota(jnp.int32, (CHUNK, LANE), 1)
        rowi = lax.broadcasted_iota(jnp.int32, (CHUNK, 1), 0)
        sub = lax.broadcasted_iota(jnp.int32, (SUBLANE, CHUNK), 0)
        rs = lambda c: slice(c * CHUNK, (c + 1) * CHUNK)
        sl = lambda hd: slice(hd * LANE, (hd + 1) * LANE)
        hs = lambda hd: slice(hd * CHUNK, (hd + 1) * CHUNK)
        for c0 in range(0, per, WY_GROUP):
            items = [(c, hd) for c in range(c0, c0 + WY_GROUP) for hd in range(A_HEADS)]
            at = lambda ref: [ref[rs(c), sl(hd)] for c, hd in items]
            qs, ks, vs, dos, vns, dvns = at(q_ref), at(k_ref), at(v_ref), at(do_ref), at(vn_ref), at(dvn_ref)
            tmhs = [tm_ref[rs(c), hs(hd)] for c, hd in items]
            ps = [qk_ref[rs(c), hs(hd)] for c, hd in items]
            gates = [_chunk_gates(bg_ref[rs(c), :], bgt_ref[:, rs(c)], hd) for c, hd in items]
            betas = [g[0] for g in gates]
            gcols = [g[1] for g in gates]
            dmats = [jnp.exp(jnp.where(causal, g[1] - g[2], NEG)) for g in gates]
            es = [jnp.exp(gc) for gc in gcols]
            glasts = [gc[CHUNK - 1:CHUNK, :] for gc in gcols]
            eks = [jnp.exp(gl - gc) for gl, gc in zip(glasts, gcols)]
            kbs = [kh * b for kh, b in zip(ks, betas)]
            vbs = [vh * b for vh, b in zip(vs, betas)]
            kbes = [kb * e for kb, e in zip(kbs, es)]

            a_s = [jnp.where(strict, _dot_nt(kb, kh) * dm, 0.0) for kb, kh, dm in zip(kbs, ks, dmats)]
            dps = [jnp.where(causal, _dot_nt(doh, vnh), 0.0) for doh, vnh in zip(dos, vns)]
            rows2 = lambda a, b: jnp.concatenate([a, b], axis=0)
            cols2 = lambda a, b: jnp.concatenate([a, b], axis=1)
            by_s = [_dot_nt(rows2(doh, dvnh), s_ref[c, hd]) for doh, dvnh, (c, hd) in zip(dos, dvns, items)]
            dqds = [m[:CHUNK] for m in by_s]
            dws = [-m[CHUNK:] for m in by_s]
            dkds = [_dot_nt(vnh, ds_ref[c, hd]) for vnh, (c, hd) in zip(vns, items)]
            dgts = [jnp.sum(s_ref[c, hd].astype(F32) * ds_ref[c, hd].astype(F32), keepdims=True) for c, hd in items]
            pairs = [cols2(dvnh, dw) for dvnh, dw in zip(dvns, dws)]
            by_t = [_dot_tn(tmh, pr) for tmh, pr in zip(tmhs, pairs)]
            dvbs = [m[:, :LANE] for m in by_t]
            dkbes = [m[:, LANE:] for m in by_t]
            dts = [_dot_nt(pr, cols2(vb, kbe)) for pr, vb, kbe in zip(pairs, vbs, kbes)]
            xs = [_dot_nt(dt, tmh) for dt, tmh in zip(dts, tmhs)]
            das = [jnp.where(strict, -_dot_tn(tmh, x), 0.0) for tmh, x in zip(tmhs, xs)]
            dmas = [da * dm for da, dm in zip(das, dmats)]
            dmps = [dp * dm for dp, dm in zip(dps, dmats)]
            stacked = [rows2(dma, dmp) for dma, dmp in zip(dmas, dmps)]
            by_k = [_dot(st, kh) for st, kh in zip(stacked, ks)]
            dkbs = [m[:CHUNK] + dkbe * e for m, dkbe, e in zip(by_k, dkbes, es)]
            for i, (c, hd) in enumerate(items):
                dq_ref[rs(c), sl(hd)] = by_k[i][CHUNK:] + dqds[i] * es[i]
                dk_ref[rs(c), sl(hd)] = (_dot_tn(stacked[i], rows2(kbs[i], qs[i])) + dkds[i] * eks[i]
                                         + dkbs[i] * betas[i])
                dv_ref[rs(c), sl(hd)] = dvbs[i] * betas[i]
            for c in range(c0, c0 + WY_GROUP):
                acc = jnp.zeros((CHUNK, LANE), F32)
                acc_t = jnp.zeros((SUBLANE, CHUNK), F32)
                for i, (ci, hd) in enumerate(items):
                    if ci != c:
                        continue
                    gmat = das[i] * a_s[i] + dps[i] * ps[i]
                    rk = jnp.sum(dkds[i] * ks[i], -1, keepdims=True) * eks[i]
                    de = jnp.sum(dqds[i] * qs[i] + dkbes[i] * kbs[i], -1, keepdims=True)
                    dglast = jnp.sum(rk, keepdims=True) + dgts[i] * jnp.exp(glasts[i])
                    dgc = (jnp.sum(gmat, -1, keepdims=True) + de * es[i] - rk
                           + jnp.where(rowi == CHUNK - 1, dglast, 0.0))
                    dbeta = jnp.sum(dkbs[i] * ks[i] + dvbs[i] * vs[i], -1, keepdims=True)
                    acc = acc + jnp.where(lane == hd, dbeta, 0.0) + jnp.where(lane == A_HEADS + hd, dgc, 0.0)
                    acc_t = acc_t + jnp.where(sub == A_HEADS + hd, -jnp.sum(gmat, axis=0, keepdims=True), 0.0)
                dbg_ref[rs(c), :] = acc
                dbgt_ref[:, rs(c)] = acc_t

    blk = pl.BlockSpec((rows, A_WIDTH), lambda i: (i, 0))
    half = pl.BlockSpec((rows, A_HEADS * CHUNK), lambda i: (i, 0))
    col = pl.BlockSpec((rows, LANE), lambda i: (i, 0))
    rowf = pl.BlockSpec((SUBLANE, rows), lambda i: (0, i))
    st = pl.BlockSpec((per, A_HEADS, LANE, LANE), lambda i: (i, 0, 0, 0))
    wide = jax.ShapeDtypeStruct((t, A_WIDTH), F32)
    outs = pl.pallas_call(
        body, name=name, grid=(t // rows,),
        in_specs=[blk, blk, blk, blk, half, half, col, rowf, st, st, blk, blk] + c_in_specs,
        out_specs=[blk, blk, blk, col, rowf] + c_out_specs,
        out_shape=[wide, wide, wide, jax.ShapeDtypeStruct((t, LANE), F32),
                   jax.ShapeDtypeStruct((SUBLANE, t), F32)] + c_outs,
        scratch_shapes=c_scratch,
        compiler_params=_cp("arbitrary"))(q, k, v, vn, tmat, qk, bg, bgt, s_all, ds_all, dvn, do, *c_ins)
    return outs[:5], outs[5:]


def _dn_pre_bwd(h, conv_w, par, dq, dk, dv, dbg, dbgt, *, tt, name):
    t = h.shape[0]
    cw = 3 * A_WIDTH
    hb = tt // HALO

    def body(pre_ref, halo_ref, bgi_ref, cw_ref, par_ref, dq_ref, dk_ref, dv_ref, dbg_ref, dbgt_ref,
             dc_ref, dbgi_ref, dpar_ref):
        i = pl.program_id(0)

        @pl.when(i == 0)
        def _():
            dpar_ref[...] = jnp.zeros_like(dpar_ref)

        cur = pre_ref[...].astype(F32)
        before = jnp.where(i > 0, halo_ref[...].astype(F32)[HALO - SUBLANE:], 0.0)
        c = _conv_fwd(cur, before, cw_ref[...])
        s, ds = _silu_and_grad(c)
        for hd in range(A_HEADS):
            sl = slice(hd * LANE, (hd + 1) * LANE)
            for base, d_ref, scale in ((0, dq_ref, A_HEAD_DIM ** -0.5), (A_WIDTH, dk_ref, 1.0)):
                csl = slice(base + hd * LANE, base + (hd + 1) * LANE)
                tq = s[:, base + hd * LANE:base + (hd + 1) * LANE]
                dy = d_ref[:, sl]
                rq = lax.rsqrt(jnp.sum(tq * tq, -1, keepdims=True) + L2_EPS)
                dtq = scale * (rq * dy - tq * (rq * rq * rq) * jnp.sum(dy * tq, -1, keepdims=True))
                dc_ref[:, csl] = dtq * ds[:, base + hd * LANE:base + (hd + 1) * LANE]
        dc_ref[:, 2 * A_WIDTH:] = dv_ref[...] * ds[:, 2 * A_WIDTH:]
        raw = bgi_ref[...].astype(F32)
        lane = lax.broadcasted_iota(jnp.int32, raw.shape, 1)
        is_b = lane < A_HEADS
        is_a = (lane >= A_HEADS) & (lane < 2 * A_HEADS)
        rows_t = jnp.concatenate([dbgt_ref[...], jnp.zeros((LANE - SUBLANE, tt), F32)], axis=0)
        dbg_v = dbg_ref[...] + jnp.where(is_a, jnp.transpose(rows_t), 0.0)
        dbg_v = jnp.where(is_a, _dot_hi(_chunk_tri(tt, lower=False), jnp.where(is_a, dbg_v, 0.0)), dbg_v)
        beta = _sigmoid(raw)
        z = raw + par_ref[1:2, :]
        neg_ea = -jnp.exp(par_ref[0:1, :])
        g = neg_ea * _softplus(z)
        da = dbg_v * neg_ea * _sigmoid(z)
        dbgi_ref[...] = jnp.where(is_b, dbg_v * beta * (1.0 - beta), jnp.where(is_a, da, 0.0))
        dpar_ref[0:1, :] += jnp.sum(jnp.where(is_a, dbg_v * g, 0.0), axis=0, keepdims=True)
        dpar_ref[1:2, :] += jnp.sum(jnp.where(is_a, da, 0.0), axis=0, keepdims=True)

    wide = pl.BlockSpec((tt, A_WIDTH), lambda i: (i, 0))
    return pl.pallas_call(
        body, name=name, grid=(t // tt,),
        in_specs=[pl.BlockSpec((tt, cw), lambda i: (i, 0)),
                  pl.BlockSpec((HALO, cw), lambda i: (jnp.maximum(i * hb - 1, 0), 0)),
                  pl.BlockSpec((tt, LANE), lambda i: (i, C_BG // LANE)),
                  pl.BlockSpec((CONV_K, cw), lambda i: (0, 0)),
                  pl.BlockSpec((SUBLANE, LANE), lambda i: (0, 0)),
                  wide, wide, wide, pl.BlockSpec((tt, LANE), lambda i: (i, 0)),
                  pl.BlockSpec((SUBLANE, tt), lambda i: (0, i))],
        out_specs=[pl.BlockSpec((tt, cw), lambda i: (i, 0)), pl.BlockSpec((tt, LANE), lambda i: (i, 0)),
                   pl.BlockSpec((SUBLANE, LANE), lambda i: (0, 0))],
        out_shape=[jax.ShapeDtypeStruct((t, cw), F32), jax.ShapeDtypeStruct((t, LANE), F32),
                   jax.ShapeDtypeStruct((SUBLANE, LANE), F32)],
        compiler_params=_cp("arbitrary"))(h, h, h, conv_w, par, dq, dk, dv, dbg, dbgt)


def _conv_bwd(dc, h, conv_w, dh, *, tt, name):
    t = dc.shape[0]
    cw = 3 * A_WIDTH
    hb = tt // HALO
    nb = t // tt

    def body(dc_ref, after_ref, pre_ref, before_ref, cw_ref, dh_in_ref, dpre_ref, dcw_ref):
        i = pl.program_id(0)

        @pl.when(i == 0)
        def _():
            dcw_ref[...] = jnp.zeros_like(dcw_ref)

        dcv = dc_ref[...]
        after = jnp.where(i < nb - 1, after_ref[...], 0.0)
        cur = pre_ref[...].astype(F32)
        before = jnp.where(i > 0, before_ref[...].astype(F32)[HALO - SUBLANE:], 0.0)
        w = cw_ref[...]
        acc = dcv * w[CONV_K - 1:CONV_K, :]
        dcw_ref[CONV_K - 1:CONV_K, :] += jnp.sum(dcv * cur, axis=0, keepdims=True)
        for s in range(1, CONV_K):
            j = CONV_K - 1 - s
            acc = acc + _shift_up(dcv, after, s) * w[j:j + 1, :]
            dcw_ref[j:j + 1, :] += jnp.sum(dcv * _shift_down(cur, before, s), axis=0, keepdims=True)
        dpre_ref[...] = acc

    return pl.pallas_call(
        body, name=name, grid=(nb,),
        in_specs=[pl.BlockSpec((tt, cw), lambda i: (i, 0)),
                  pl.BlockSpec((SUBLANE, cw), lambda i: (jnp.minimum((i + 1) * (tt // SUBLANE), t // SUBLANE - 1), 0)),
                  pl.BlockSpec((tt, cw), lambda i: (i, 0)),
                  pl.BlockSpec((HALO, cw), lambda i: (jnp.maximum(i * hb - 1, 0), 0)),
                  pl.BlockSpec((CONV_K, cw), lambda i: (0, 0)), _ANY],
        out_specs=[pl.BlockSpec((tt, cw), lambda i: (i, 0)), pl.BlockSpec((SUBLANE, cw), lambda i: (0, 0))],
        out_shape=[jax.ShapeDtypeStruct(dh.shape, F32), jax.ShapeDtypeStruct((SUBLANE, cw), F32)],
        input_output_aliases={5: 0},
        compiler_params=_cp("arbitrary"))(dc, dc, h, h, conv_w, dh)


def _swa_bwd(h, dm, sinks_b, dh, *, name, carry=None):
    t = h.shape[0]
    qspec, cur, prev = _swa_specs()
    c_ins, c_in_specs, c_out_specs, c_outs, c_scratch = _carry_specs(carry)

    def body(*refs):
        (q_ref, kc_ref, kp_ref, vc_ref, vp_ref, zb_ref, dy_ref, sk_ref, dh_in_ref,
         dqz_ref, dk_ref, dv_ref, dsk_ref) = _carried(carry, refs, 9, 4, t // BLOCK)
        n_blk = pl.program_id(0)

        @pl.when(n_blk == 0)
        def _():
            dk_ref[...] = jnp.zeros_like(dk_ref)
            dv_ref[...] = jnp.zeros_like(dv_ref)
            dsk_ref[...] = jnp.zeros_like(dsk_ref)

        kp, kc, vp, vc = kp_ref[...], kc_ref[...], vp_ref[...], vc_ref[...]
        scale = B_HEAD_DIM ** -0.5
        hks = range(B_KV_HEADS)
        ksl = lambda hk: slice(hk * B_HEAD_DIM, (hk + 1) * B_HEAD_DIM)
        upper, groups = _swa_group_probs(q_ref, sk_ref, kp, kc, vp, vc, n_blk)
        zbs = [_stack_heads(zb_ref, hk) for hk in hks]
        dys = [_stack_heads(dy_ref, hk) for hk in hks]
        gates = [_silu_and_grad(zbs[hk]) for hk in hks]
        dos = [dys[hk] * gates[hk][0] for hk in hks]
        deltas = [jnp.sum(dos[hk] * groups[hk][3], -1, keepdims=True) for hk in hks]
        dps = [jnp.where(upper, _dot_nt(dos[hk], vp[:, ksl(hk)]), _dot_nt(dos[hk], vc[:, ksl(hk)])) for hk in hks]
        dss = [groups[hk][1] * (dps[hk] - deltas[hk]) for hk in hks]
        ds_up = [jnp.where(upper, dss[hk], 0.0) for hk in hks]
        ds_lo = [dss[hk] - ds_up[hk] for hk in hks]
        p_up = [jnp.where(upper, groups[hk][1], 0.0) for hk in hks]
        p_lo = [groups[hk][1] - p_up[hk] for hk in hks]
        dqs = [(_dot(ds_up[hk], kp[:, ksl(hk)]) + _dot(ds_lo[hk], kc[:, ksl(hk)])) * scale for hk in hks]
        dk_prev = [_dot_tn(ds_up[hk], groups[hk][0]) for hk in hks]
        dk_cur = [_dot_tn(ds_lo[hk], groups[hk][0]) for hk in hks]
        dv_prev = [_dot_tn(p_up[hk], dos[hk]) for hk in hks]
        dv_cur = [_dot_tn(p_lo[hk], dos[hk]) for hk in hks]
        for hk in hks:
            dzb = dys[hk] * groups[hk][3] * gates[hk][1]
            dsink = groups[hk][2] * deltas[hk]
            for g in range(B_GROUP):
                hq = hk * B_GROUP + g
                rows = slice(g * BLOCK, (g + 1) * BLOCK)
                qsl = slice(hq * B_HEAD_DIM, (hq + 1) * B_HEAD_DIM)
                dqz_ref[:, qsl] = dqs[hk][rows]
                dqz_ref[:, B_WIDTH + hq * B_HEAD_DIM:B_WIDTH + (hq + 1) * B_HEAD_DIM] = dzb[rows]
                dsk_ref[hq:hq + 1, :] += -jnp.sum(dsink[rows], keepdims=True)
        at_cur = pl.ds(pl.multiple_of(n_blk * BLOCK, BLOCK), BLOCK)
        at_prev = pl.ds(pl.multiple_of(jnp.maximum(n_blk - 1, 0) * BLOCK, BLOCK), BLOCK)
        dk_ref[at_prev, :] += jnp.concatenate(dk_prev, axis=1)
        dv_ref[at_prev, :] += jnp.concatenate(dv_prev, axis=1)
        dk_ref[at_cur, :] += jnp.concatenate(dk_cur, axis=1)
        dv_ref[at_cur, :] += jnp.concatenate(dv_cur, axis=1)

    narrow = jax.ShapeDtypeStruct((t, B_KV_WIDTH), F32)
    res = lambda a, b: pl.BlockSpec((a, b), lambda i: (0, 0))
    outs = pl.pallas_call(
        body, name=name, grid=(t // BLOCK,),
        in_specs=[qspec(C_QB), cur(C_KB), prev(C_KB), cur(C_VB), prev(C_VB), qspec(C_ZB),
                  pl.BlockSpec((BLOCK, B_WIDTH), lambda i: (i, 1)), res(B_Q_HEADS, LANE), _ANY] + c_in_specs,
        out_specs=[pl.BlockSpec((BLOCK, 2 * B_WIDTH), lambda i: (i, C_QB // (2 * B_WIDTH))),
                   res(t, B_KV_WIDTH), res(t, B_KV_WIDTH), res(B_Q_HEADS, LANE)] + c_out_specs,
        out_shape=[jax.ShapeDtypeStruct(dh.shape, F32), narrow, narrow,
                   jax.ShapeDtypeStruct((B_Q_HEADS, LANE), F32)] + c_outs,
        scratch_shapes=c_scratch,
        input_output_aliases={8: 0},
        compiler_params=_cp("arbitrary"))(h, h, h, h, h, h, dm, sinks_b, dh, *c_ins)
    return outs[:4], outs[4:]


def _in_proj_dw(dh_main, dh_tail, x, *, tk, name):
    t, n = x.shape

    def body(a_ref, t_ref, x_ref, o_ref, ot_ref):
        @pl.when(pl.program_id(0) == 0)
        def _():
            o_ref[...] = jnp.zeros_like(o_ref)
            ot_ref[...] = jnp.zeros_like(ot_ref)

        xb = x_ref[...].astype(BF16)
        o_ref[...] += _dot_tn(a_ref[...], xb)
        ot_ref[...] += _dot_tn(t_ref[...], xb)

    row = lambda a: pl.BlockSpec((tk, a.shape[1]), lambda kk: (kk, 0))
    acc = lambda a: pl.BlockSpec((a.shape[1], n), lambda kk: (0, 0))
    return pl.pallas_call(
        body, name=name, grid=(t // tk,), in_specs=[row(dh_main), row(dh_tail), row(x)],
        out_specs=[acc(dh_main), acc(dh_tail)],
        out_shape=[jax.ShapeDtypeStruct((a.shape[1], n), F32) for a in (dh_main, dh_tail)],
        compiler_params=_cp("arbitrary"))(dh_main, dh_tail, x)


def _in_proj_dx(dh_main, dh_tail, wt, dr, *, tm, name, carry=None):
    t, n_main = dh_main.shape
    n_tail = dh_tail.shape[1]
    c_ins, c_in_specs, c_out_specs, c_outs, c_scratch = _carry_specs(carry)

    def body(*refs):
        a_ref, t_ref, wa_ref, wt_ref, r_ref, o_ref = _carried(carry, refs, 5, 1, t // tm)
        o_ref[...] = _dot(a_ref[...], wa_ref[...]) + _dot(t_ref[...], wt_ref[...]) + DEEPNORM_ALPHA * r_ref[...]

    row = lambda w: pl.BlockSpec((tm, w), lambda i: (i, 0))
    outs = pl.pallas_call(
        body, name=name, grid=(t // tm,),
        in_specs=[row(n_main), row(n_tail), pl.BlockSpec((n_main, D_MODEL), lambda i: (0, 0)),
                  pl.BlockSpec((n_tail, D_MODEL), lambda i: (n_main // n_tail, 0)), row(D_MODEL)] + c_in_specs,
        out_specs=[row(D_MODEL)] + c_out_specs,
        out_shape=[jax.ShapeDtypeStruct((t, D_MODEL), F32)] + c_outs,
        scratch_shapes=c_scratch,
        compiler_params=_cp("arbitrary"))(dh_main, dh_tail, wt, wt, dr, *c_ins)
    return outs[0], outs[1:]


def _layer_bwd(dxn, res, wt, conv_w, par, sinks_b, norm_w, w_out_bf, ln_g, l, carries=None, carry_dx=None):
    carries = carries or {}
    w_out_bf = res["w_out"]
    dr, dm, dw_out, dln_g, dln_b = _ln_out_bwd(dxn, res["r"], res["mixed"], ln_g, w_out_bf, tm=512, name=f"ln_out_bwd_{l}")
    h = res["h"]
    do, dh, dnw = _dn_post_bwd(dm, res["oa"], h, norm_w, tm=512, name=f"dn_post_bwd_{l}")
    dvn, ds_all = _dn_scan_bwd(res["q"], res["k"], res["w"], res["qk"], res["bg"], do, name=f"dn_scan_bwd_{l}")
    (dq, dk, dv, dbg, dbgt), got_chunk = _dn_chunk_bwd(
        res["q"], res["k"], res["v"], res["vn"], res["tmat"], res["qk"], res["bg"], res["bgt"], res["s_all"], ds_all,
        dvn, do, name=f"dn_chunk_bwd_{l}", carry=carries.get("dn_chunk"))
    dc, dbgi, dpar = _dn_pre_bwd(h, conv_w, par, dq, dk, dv, dbg, dbgt, tt=512, name=f"dn_pre_bwd_{l}")
    dh, dcw = _conv_bwd(dc, h, conv_w, dh, tt=512, name=f"conv_bwd_{l}")
    (dh, dkb, dvb, dsk), got_swa = _swa_bwd(h, dm, sinks_b, dh, name=f"swa_bwd_{l}", carry=carries.get("swa"))
    carried = dict(dn_chunk=got_chunk, swa=got_swa)
    dh_tail = jnp.concatenate([dkb, dvb, dbgi], axis=1)
    dwt_main, dwt_tail = _in_proj_dw(dh, dh_tail, res["x"], tk=512, name=f"in_proj_dw_{l}")
    grads = dict(w_in=(dwt_main, dwt_tail), conv_w=dcw[:CONV_K], a_log=dpar[0, A_HEADS:2 * A_HEADS],
                 dt_bias=dpar[1, A_HEADS:2 * A_HEADS], norm_w=dnw[0], sinks=dsk[:, 0], w_out=dw_out,
                 ln_g=dln_g[0], ln_b=dln_b[0])
    dx, carried_dx = _in_proj_dx(dh, dh_tail, wt, dr, tm=512, name=f"in_proj_dx_{l}",
                                 carry=None if carry_dx is None else carry_dx(grads))
    return dx, grads, carried, carried_dx


def _layer_args(wt, conv_w, a_log, dt_bias, sinks, norm_w, w_out_bf):
    return (wt, conv_w, _gate_params(a_log, dt_bias), jnp.broadcast_to(sinks[:, None], (B_Q_HEADS, LANE)),
            norm_w[None], w_out_bf)


def _local_step(x, target, args0, args1, ln_g, ln_b, gathers=None, reduce1=None, reduce0=None):
    assert DEPTH == 2
    x1, res0, got = _layer_fwd(x, *args0, ln_g[0][None], ln_b[0][None], 0, carries=gathers)
    if gathers is not None:
        args1 = args1(got)
    (dx, loss_tile), res1, _ = _layer_fwd(x1, *args1, ln_g[1][None], ln_b[1][None], 1, target=target)
    dx, grads1, _, _ = _layer_bwd(dx, res1, *args1, ln_g[1][None], 1)
    carries = None if reduce1 is None else reduce1(grads1)
    carry_dx = None if reduce0 is None else (lambda grads0: reduce0(grads0, grads1, loss_tile))
    dx, grads0, landed1, landed0 = _layer_bwd(dx, res0, *args0, ln_g[0][None], 0, carries=carries, carry_dx=carry_dx)
    return loss_tile, dx, [grads0, grads1], landed1, landed0


_ANY = pl.BlockSpec(memory_space=pl.ANY)
_MESH = pl.DeviceIdType.MESH


HALF = D_MODEL // 2


class _Exchange:
    def __init__(self, ins, outs, n_remote, n_local, plan):
        self.ins, self.outs, self.n_remote, self.n_local, self.plan = tuple(ins), tuple(outs), n_remote, n_local, plan

    def scratch(self):
        return [pltpu.SemaphoreType.DMA((self.n_remote,)), pltpu.SemaphoreType.DMA((self.n_remote,)),
                pltpu.SemaphoreType.DMA((max(self.n_local, 1),))]

    def _copies(self, in_refs, out_refs, sems, arriving):
        send_sems, recv_sems, local_sems = sems
        local, sends, recvs = self.plan(in_refs, out_refs)
        loc = [pltpu.make_async_copy(s, d, local_sems.at[i]) for i, (s, d) in enumerate(local)]
        rem = [pltpu.make_async_remote_copy(src_ref=s, dst_ref=recvs[i] if arriving else d, send_sem=send_sems.at[i],
                                            recv_sem=recv_sems.at[i], device_id=peer, device_id_type=_MESH)
               for i, (s, d, peer) in enumerate(sends)]
        return loc, rem

    def start(self, in_refs, out_refs, sems):
        loc, rem = self._copies(in_refs, out_refs, sems, arriving=False)
        for cp in loc + rem:
            cp.start()

    def finish(self, in_refs, out_refs, sems):
        loc, rem = self._copies(in_refs, out_refs, sems, arriving=True)
        for cp in rem:
            cp.wait_recv()
        for cp in rem:
            cp.wait_send()
        for cp in loc:
            cp.wait()


def _run_exchange(ex, *, name):
    n_in, n_out = len(ex.ins), len(ex.outs)

    def body(*refs):
        parts = refs[:n_in], refs[n_in:n_in + n_out], refs[n_in + n_out:]
        ex.start(*parts)
        ex.finish(*parts)

    return pl.pallas_call(body, name=name, in_specs=[_ANY] * n_in, out_specs=[_ANY] * n_out, out_shape=list(ex.outs),
                          scratch_shapes=ex.scratch())(*ex.ins)


def _place():
    x, y, c = lax.axis_index("x"), lax.axis_index("y"), lax.axis_index("c")
    return x, y, c, [(1 - x, y), (x, 1 - y), (1 - x, 1 - y)]


def _gather_exchange(arrays):
    n = len(arrays)

    def plan(src, dst):
        x, y, c, chips = _place()
        me = 2 * x + y
        local = [(src[k], dst[k].at[me]) for k in range(n)]
        sends = [(src[k], dst[k].at[me], (px, py, c)) for k in range(n) for px, py in chips]
        recvs = [dst[k].at[2 * px + py] for k in range(n) for px, py in chips]
        return local, sends, recvs

    return _Exchange(arrays, [jax.ShapeDtypeStruct((N_SHARD,) + a.shape, a.dtype) for a in arrays], 3 * n, n, plan)


def _gather_two_level(pack, conv_w, *, name):
    rows = pack.shape[0]
    part_rows = rows // 2

    def body(pack_ref, conv_ref, land_ref, conv_land_ref, send1, recv1, send2, recv2, csend, crecv, local_sems):
        x, y, c, chips = _place()
        me = 2 * x + y
        sibling = (x, y, 1 - c)
        part = lambda core: pl.ds(pl.multiple_of(core * part_rows, 16), part_rows)
        remote = lambda src, dst, ss, rs, to: pltpu.make_async_remote_copy(
            src_ref=src, dst_ref=dst, send_sem=ss, recv_sem=rs, device_id=to, device_id_type=_MESH)
        local = [pltpu.make_async_copy(pack_ref, land_ref.at[me], local_sems.at[0]),
                 pltpu.make_async_copy(conv_ref, conv_land_ref.at[me], local_sems.at[1])]
        for cp in local:
            cp.start()
        first = [remote(pack_ref.at[part(c)], land_ref.at[me, part(c)], send1.at[j], recv1.at[j], (px, py, c))
                 for j, (px, py) in enumerate(chips)]
        convs = [remote(conv_ref, conv_land_ref.at[me], csend.at[j], crecv.at[j], (px, py, c))
                 for j, (px, py) in enumerate(chips)]
        for cp in first + convs:
            cp.start()
        passed = []
        for j, (px, py) in enumerate(chips):
            slot = 2 * px + py
            remote(pack_ref.at[part(c)], land_ref.at[slot, part(c)], send1.at[j], recv1.at[j], (px, py, c)).wait_recv()
            cp = remote(land_ref.at[slot, part(c)], land_ref.at[slot, part(c)], send2.at[j], recv2.at[j], sibling)
            cp.start()
            passed.append(cp)
        for j, (px, py) in enumerate(chips):
            slot = 2 * px + py
            remote(land_ref.at[slot, part(1 - c)], land_ref.at[slot, part(1 - c)], send2.at[j], recv2.at[j],
                   sibling).wait_recv()
            remote(conv_ref, conv_land_ref.at[slot], csend.at[j], crecv.at[j], (px, py, c)).wait_recv()
        for cp in first + convs + passed:
            cp.wait_send()
        for cp in local:
            cp.wait()

    sems = [pltpu.SemaphoreType.DMA((3,))] * 6 + [pltpu.SemaphoreType.DMA((2,))]
    return pl.pallas_call(
        body, name=name, in_specs=[_ANY, _ANY], out_specs=[_ANY, _ANY],
        out_shape=[jax.ShapeDtypeStruct((N_SHARD,) + pack.shape, pack.dtype),
                   jax.ShapeDtypeStruct((N_SHARD,) + conv_w.shape, conv_w.dtype)],
        scratch_shapes=sems)(pack, conv_w)


def _half(core):
    return pl.ds(pl.multiple_of(core * HALF, HALF), HALF)


def _reduce_scatter_exchange(g, row0, rows):
    def plan(src, dst):
        x, y, c, chips = _place()
        peers = [(px, py, c if t == 0 else 1 - c) for px, py in chips for t in (0, 1)] + [(x, y, 1 - c)]
        sends = [(src[0].at[2 * px + py, pl.ds(row0, rows), _half(pc)], dst[0].at[k], (px, py, pc))
                 for k, (px, py, pc) in enumerate(peers)]
        return [], sends, [dst[0].at[k] for k in range(7)]

    return _Exchange([g], [jax.ShapeDtypeStruct((7, rows, HALF), g.dtype)], 7, 0, plan)


def _pair_window_exchange(g):
    def plan(src, dst):
        x, y, c, _ = _place()
        return [], [(src[0].at[:, :, _half(1 - c)], dst[0], (x, y, 1 - c))], [dst[0]]

    return _Exchange([g], [jax.ShapeDtypeStruct(g.shape[:2] + (HALF,), g.dtype)], 1, 0, plan)


def _chip_scatter_exchange(p, small):
    def plan(src, dst):
        x, y, c, chips = _place()
        mine = 4 * x + 2 * y + c
        peers = [(px, py, c if t == 0 else 1 - c) for px, py in chips for t in (0, 1)] + [(x, y, 1 - c)]
        sends = [(src[0].at[2 * px + py], dst[0].at[j], (px, py, c)) for j, (px, py) in enumerate(chips)]
        recvs = [dst[0].at[j] for j in range(3)]
        sends += [(src[1], dst[1].at[mine], peer) for peer in peers]
        recvs += [dst[1].at[4 * px + 2 * py + pc] for px, py, pc in peers]
        return [(src[1], dst[1].at[mine])], sends, recvs

    outs = [jax.ShapeDtypeStruct((3,) + p.shape[1:], p.dtype), jax.ShapeDtypeStruct((8,) + small.shape, small.dtype)]
    return _Exchange([p, small], outs, 10, 1, plan)


def _share_exchange(arrays):
    n = len(arrays)

    def plan(src, dst):
        x, y, c, _ = _place()
        return [], [(src[k], dst[k], (x, y, 1 - c)) for k in range(n)], [dst[k] for k in range(n)]

    return _Exchange(arrays, [jax.ShapeDtypeStruct(a.shape, a.dtype) for a in arrays], n, 0, plan)


def _sum_scatter(g, lands, me, core, *, tc, name):
    rows = g.shape[1]
    per = HALF // tc
    n = len(lands)

    def body(*refs):
        g_ref, land_refs, o_ref = refs[1], refs[2:2 + n], refs[2 + n]
        at = 0
        for land_ref in land_refs:
            run = slice(at, at + land_ref.shape[1])
            acc = g_ref[run, :].astype(F32)
            for k in range(7):
                acc = acc + land_ref[k].astype(F32)
            o_ref[run, :] = acc
            at = run.stop

    return pl.pallas_call(
        body, name=name, out_shape=jax.ShapeDtypeStruct((rows, HALF), F32), compiler_params=_cp("parallel"),
        grid_spec=pltpu.PrefetchScalarGridSpec(
            num_scalar_prefetch=1, grid=(per,),
            in_specs=[pl.BlockSpec((None, rows, tc), lambda i, w: (w[0], 0, w[1] * per + i))]
            + [pl.BlockSpec((7, a.shape[1], tc), lambda i, w: (0, 0, i)) for a in lands],
            out_specs=pl.BlockSpec((rows, tc), lambda i, w: (0, i))))(
        jnp.stack([me, core]).astype(jnp.int32), g, *lands)


def _pair_add(g, land, core, *, name):
    n, rows, _ = g.shape

    def body(core_ref, g_ref, land_ref, o_ref):
        o_ref[...] = (g_ref[...].astype(F32) + land_ref[...].astype(F32)).astype(o_ref.dtype)

    blk = pl.BlockSpec((1, rows, HALF), lambda i, w: (i, 0, 0))
    return pl.pallas_call(
        body, name=name, out_shape=jax.ShapeDtypeStruct((n, rows, HALF), g.dtype), compiler_params=_cp("parallel"),
        grid_spec=pltpu.PrefetchScalarGridSpec(
            num_scalar_prefetch=1, grid=(n,),
            in_specs=[pl.BlockSpec((1, rows, HALF), lambda i, w: (i, 0, w[0])), blk], out_specs=blk))(
        jnp.reshape(core, (1,)).astype(jnp.int32), g, land)


def _sum_chips(p, land, me, *, tc, name):
    rows = p.shape[1]

    def body(me_ref, p_ref, land_ref, o_ref):
        acc = p_ref[...].astype(F32)
        for k in range(3):
            acc = acc + land_ref[k].astype(F32)
        o_ref[...] = acc

    return pl.pallas_call(
        body, name=name, out_shape=jax.ShapeDtypeStruct((rows, HALF), F32), compiler_params=_cp("parallel"),
        grid_spec=pltpu.PrefetchScalarGridSpec(
            num_scalar_prefetch=1, grid=(HALF // tc,),
            in_specs=[pl.BlockSpec((None, rows, tc), lambda i, w: (w[0], 0, i)),
                      pl.BlockSpec((3, rows, tc), lambda i, w: (0, 0, i))],
            out_specs=pl.BlockSpec((rows, tc), lambda i, w: (0, i))))(
        jnp.reshape(me, (1,)).astype(jnp.int32), p, land)


def _sum_slots(a, *, name):
    n = a.shape[0]

    def body(a_ref, o_ref):
        acc = a_ref[0]
        for k in range(1, n):
            acc = acc + a_ref[k]
        o_ref[...] = acc

    return pl.pallas_call(body, name=name, out_shape=jax.ShapeDtypeStruct(a.shape[1:], a.dtype))(a)


def _elementwise(fn, ins, n_out, block, *, name):
    shape = ins[0].shape
    grid = tuple(s // b for s, b in zip(shape, block))
    n_in = len(ins)

    def body(*refs):
        outs = fn(*[r[...] for r in refs[:n_in]])
        for o_ref, val in zip(refs[n_in:], outs):
            o_ref[...] = val

    spec = pl.BlockSpec(block, lambda i, j, k: (i, j, k))
    return pl.pallas_call(body, name=name, grid=grid, in_specs=[spec] * n_in, out_specs=[spec] * n_out,
                          out_shape=[jax.ShapeDtypeStruct(shape, F32)] * n_out,
                          compiler_params=_cp(*["parallel"] * 3))(*ins)


def _adamw_math(w, g, m, v):
    mn = ADAM_B1 * m + (1.0 - ADAM_B1) * g
    vn = ADAM_B2 * v + (1.0 - ADAM_B2) * (g * g)
    m_hat = mn / (1.0 - ADAM_B1 ** ADAM_STEP)
    v_hat = vn / (1.0 - ADAM_B2 ** ADAM_STEP)
    return -ADAM_LR * (m_hat / (jnp.sqrt(v_hat) + ADAM_EPS) + ADAM_WD * w), mn, vn


def _adamw(w, g, m, v, block, *, name):
    return _elementwise(_adamw_math, [w, g, m, v], 3, block, name=name)


def _interleave_layers(layers, *, tc, name):
    rows, cols = layers[0].shape
    n = len(layers)

    def body(*refs):
        for l in range(n):
            refs[n][:, l, :] = refs[l][...]

    return pl.pallas_call(body, name=name, grid=(cols // tc,),
                          in_specs=[pl.BlockSpec((rows, tc), lambda i: (0, i))] * n,
                          out_specs=pl.BlockSpec((rows, n, tc), lambda i: (0, 0, i)),
                          out_shape=jax.ShapeDtypeStruct((rows, n, cols), layers[0].dtype),
                          compiler_params=_cp("parallel"))(*layers)


def _adamw_small(ws, gs, ms, vs, *, name):
    n = len(ws)

    def body(*refs):
        w, g, m, v, outs = refs[:n], refs[n:2 * n], refs[2 * n:3 * n], refs[3 * n:4 * n], refs[4 * n:]
        for k in range(n):
            for slot, val in enumerate(_adamw_math(w[k][...], g[k][...], m[k][...], v[k][...])):
                outs[slot * n + k][...] = val

    outs = pl.pallas_call(body, name=name, out_shape=[jax.ShapeDtypeStruct(a.shape, F32) for a in ws] * 3)(
        *ws, *gs, *ms, *vs)
    return outs[:n], outs[n:2 * n], outs[2 * n:]


def _to_kernel_order(wt):
    gates = jnp.pad(wt[2048:2056], ((0, LANE - 2 * A_HEADS), (0, 0)))
    return jnp.concatenate([wt[0:2048], wt[2056:2568], wt[2824:3336], wt[2568:2696], wt[2696:2824], gates], axis=0)


def _from_kernel_order(main, tail):
    return jnp.concatenate([main[0:2048], tail[C_BG - DH_MAIN:C_BG - DH_MAIN + 2 * A_HEADS],
                            main[C_QB:C_QB + B_WIDTH], tail[0:B_KV_WIDTH], tail[B_KV_WIDTH:2 * B_KV_WIDTH],
                            main[C_ZB:C_ZB + B_WIDTH]], axis=0)


def _gate_params(a_log, dt_bias):
    return jnp.pad(jnp.stack([a_log, dt_bias]), ((0, SUBLANE - 2), (A_HEADS, LANE - 2 * A_HEADS)))


SMALL = ("conv_w", "a_log", "dt_bias", "norm_w", "sinks", "ln_g", "ln_b")


def _pack(parts, cols):
    flat = jnp.concatenate([p.reshape(-1) for p in parts])
    rows = -(-flat.shape[0] // cols)
    return jnp.pad(flat, (0, rows * cols - flat.shape[0])).reshape(rows, cols)


def _unpack(packed, shapes):
    flat = packed.reshape(-1)
    out, at = [], 0
    for s in shapes:
        n = math.prod(s)
        out.append(flat[at:at + n].reshape(s))
        at += n
    return out


def kernel(x, w_in, conv_w, a_log, dt_bias, norm_w, sinks, w_out, ln_g, ln_b, loss_target, m_w_in, m_conv_w, m_a_log, m_dt_bias, m_norm_w, m_sinks, m_w_out, m_ln_g, m_ln_b, v_w_in, v_conv_w, v_a_log, v_dt_bias, v_norm_w, v_sinks, v_w_out, v_ln_g, v_ln_b):
    xi, yi, ci = lax.axis_index("x"), lax.axis_index("y"), lax.axis_index("c")
    me = 2 * xi + yi

    to_t = lambda a: jnp.transpose(a, (2, 0, 1))
    from_t = lambda a: jnp.transpose(a, (1, 2, 0))

    wt_shard = to_t(w_in)

    def pack_weights(l):
        rows = jnp.pad(wt_shard[:, l], ((0, IN_PAD - IN_SHARD), (0, 0)))
        return jnp.concatenate([rows, w_out[l]], axis=0).astype(BF16)

    pack0, pack1 = pack_weights(0), pack_weights(1)
    got_in0, g_conv = _gather_two_level(pack0[:IN_PAD], conv_w, name="gather_weights_0")
    conv_full = jnp.moveaxis(g_conv, 0, 2).reshape(DEPTH, CONV_K, 3 * A_WIDTH)
    piece = IN_PAD // 3
    carriers = ("dn_pre", "dn_wy", "dn_scan")
    gathers = {nm: _gather_exchange([pack1[i * piece:(i + 1) * piece]]) for i, nm in enumerate(carriers)}
    gathers.update(in_proj=_gather_exchange([pack0[IN_PAD:]]), swa=_gather_exchange([pack1[IN_PAD:]]))
    w_in_of = lambda rows: _to_kernel_order(rows[:, :IN_SHARD].reshape(IN_COLS, D_MODEL))
    w_out_of = lambda rows: rows.reshape(D_MODEL, D_MODEL)
    args0 = _layer_args(w_in_of(got_in0), conv_full[0], a_log[0], dt_bias[0], sinks[0], norm_w[0],
                        lambda got: w_out_of(got[0]))

    def args1(got):
        rows = jnp.concatenate([got[nm][0] for nm in carriers], axis=1)
        return _layer_args(w_in_of(rows), conv_full[1], a_log[1], dt_bias[1], sinks[1], norm_w[1],
                           w_out_of(got["swa"][0]))

    def pack_grads(g):
        gin = _from_kernel_order(*g["w_in"]).reshape(N_SHARD, IN_SHARD, D_MODEL)
        gin = jnp.pad(gin, ((0, 0), (0, IN_PAD - IN_SHARD), (0, 0)))
        return jnp.concatenate([gin, g["w_out"].reshape(N_SHARD, OUT_SHARD, D_MODEL)], axis=1).astype(BF16)

    packed = {}

    def reduce1(grads1):
        packed[1] = pack_grads(grads1)
        half_rows = packed[1].shape[1] // 2
        return dict(dn_chunk=_reduce_scatter_exchange(packed[1], 0, half_rows),
                    swa=_reduce_scatter_exchange(packed[1], half_rows, half_rows))

    def reduce0(grads0, grads1, loss_tile):
        g0 = pack_grads(grads0)
        from_sibling = _run_exchange(_pair_window_exchange(g0), name="pair_reduce_0")[0]
        packed[0] = _pair_add(g0, from_sibling, ci, name="pair_add_0")
        gsmall = _pack([jnp.stack([g[nm] for g in (grads0, grads1)]) for nm in SMALL] + [loss_tile[0, 0:1]], D_MODEL)
        return _chip_scatter_exchange(packed[0], gsmall)

    _, dx, grads, landed1, (landed0, landed_small) = _local_step(
        x[0], loss_target[0], args0, args1, ln_g, ln_b, gathers=gathers, reduce1=reduce1, reduce0=reduce0)

    small_shapes = [(DEPTH,) + grads[0][nm].shape for nm in SMALL]
    halves = [_sum_chips(packed[0], landed0, me, tc=2 * LANE, name="reduce_sum_0"),
              _sum_scatter(packed[1], [landed1["dn_chunk"][0], landed1["swa"][0]], me, ci, tc=2 * LANE,
                           name="reduce_sum_1")]
    s_small = _sum_slots(landed_small, name="reduce_sum_small")
    others = _run_exchange(_share_exchange(halves), name="pair_share")
    full = [jnp.where(ci == 0, jnp.concatenate([mine, other], axis=1), jnp.concatenate([other, mine], axis=1))
            for mine, other in zip(halves, others)]
    grad_in_layers = [f[:IN_SHARD] for f in full]
    grad_out = jnp.stack([f[IN_PAD:] for f in full])
    out_blk = (1, OUT_SHARD, D_MODEL)
    *small_grads, loss = _unpack(s_small, small_shapes + [()])
    gs = dict(zip(SMALL, small_grads))
    gs["conv_w"] = lax.dynamic_slice_in_dim(gs["conv_w"], me * CONV_SHARD, CONV_SHARD, axis=2)

    grad_in_t = _interleave_layers(grad_in_layers, tc=2 * LANE, name="grad_in_layers")
    d_in, nm_in, nv_in = (from_t(o) for o in _adamw(to_t(w_in), grad_in_t, to_t(m_w_in), to_t(v_w_in),
                                                    (IN_SHARD // 6, DEPTH, D_MODEL), name="adamw_in"))
    grad_in = from_t(grad_in_t)
    d_out, nm_out, nv_out = _adamw(w_out, grad_out, m_w_out, v_w_out, out_blk, name="adamw_out")
    ws = dict(conv_w=conv_w, a_log=a_log, dt_bias=dt_bias, norm_w=norm_w, sinks=sinks, ln_g=ln_g, ln_b=ln_b)
    ms = dict(conv_w=m_conv_w, a_log=m_a_log, dt_bias=m_dt_bias, norm_w=m_norm_w, sinks=m_sinks, ln_g=m_ln_g, ln_b=m_ln_b)
    vs = dict(conv_w=v_conv_w, a_log=v_a_log, dt_bias=v_dt_bias, norm_w=v_norm_w, sinks=v_sinks, ln_g=v_ln_g, ln_b=v_ln_b)
    d_s, nm_s, nv_s = (dict(zip(SMALL, o)) for o in _adamw_small(*[[d[nm] for nm in SMALL] for d in (ws, gs, ms, vs)],
                                                                 name="adamw_small"))

    def in_order(big_in, small, big_out):
        return (big_in, small["conv_w"], small["a_log"], small["dt_bias"], small["norm_w"], small["sinks"], big_out,
                small["ln_g"], small["ln_b"])

    return (loss, dx[None], *in_order(grad_in, gs, grad_out), *in_order(d_in, d_s, d_out),
            *in_order(nm_in, nm_s, nm_out), *in_order(nv_in, nv_s, nv_out))
```

```python
import math

import jax
import jax.numpy as jnp
from jax import lax
from jax.experimental import pallas as pl
from jax.experimental.pallas import tpu as pltpu

F32 = jnp.float32
BF16 = jnp.bfloat16
HI = lax.Precision.HIGHEST

D_MODEL = 1024
DEPTH = 2
A_HEADS = 4
A_HEAD_DIM = 128
A_WIDTH = 512
CONV_K = 4
CHUNK = 64
B_Q_HEADS = 8
B_KV_HEADS = 2
B_HEAD_DIM = 64
B_GROUP = 4
B_WIDTH = 512
B_KV_WIDTH = 128
BLOCK = 128
IN_COLS = 3336
DEEPNORM_ALPHA = (2 * DEPTH) ** 0.25
LN_EPS = 1e-5
RMS_EPS = 1e-6
L2_EPS = 1e-6
ADAM_LR = 0.001
ADAM_B1 = 0.9
ADAM_B2 = 0.999
ADAM_EPS = 1e-08
ADAM_WD = 0.01
ADAM_STEP = 10

N_SHARD = 4
IN_SHARD = IN_COLS // N_SHARD
OUT_SHARD = D_MODEL // N_SHARD
CONV_SHARD = 3 * A_WIDTH // N_SHARD
IN_PAD = -(-IN_SHARD // 96) * 96

P_COLS = 3456
C_PRE = 0
C_ZA = 1536
C_QB = 2048
C_ZB = 2560
C_KB = 3072
C_VB = 3200
C_BG = 3328
DH_MAIN = C_KB
LANE = 128
SUBLANE = 8
HALO = 16
VMEM_LIMIT = 56 * 1024 * 1024
ALIBI = tuple(2.0 ** (-8.0 * (h + 1) / B_Q_HEADS) for h in range(B_Q_HEADS))
NEG = -1e30


def _cp(*sem):
    return pltpu.CompilerParams(dimension_semantics=sem, vmem_limit_bytes=VMEM_LIMIT)


def _dot(a, b):
    return jnp.dot(a.astype(BF16), b.astype(BF16), preferred_element_type=F32)


def _dot_nt(a, b):
    return lax.dot_general(a.astype(BF16), b.astype(BF16), (((1,), (1,)), ((), ())),
                           preferred_element_type=F32)


def _dot_tn(a, b):
    return lax.dot_general(a.astype(BF16), b.astype(BF16), (((0,), (0,)), ((), ())),
                           preferred_element_type=F32)


def _dot_hi(a, b):
    return jnp.dot(a, b, precision=HI, preferred_element_type=F32)


def _sigmoid(x):
    return jax.nn.sigmoid(x)


def _silu(x):
    return x * _sigmoid(x)


def _silu_and_grad(x):
    s = _sigmoid(x)
    return x * s, s * (1.0 + x * (1.0 - s))


def _softplus(x):
    return jnp.maximum(x, 0.0) + jnp.log(1.0 + jnp.exp(-jnp.abs(x)))


def _shift_down(cur, before, s):
    if s == 0:
        return cur
    r = pltpu.roll(cur, s, 0)
    rb = pltpu.roll(before, s, 0)
    row = lax.broadcasted_iota(jnp.int32, before.shape, 0)
    head = jnp.where(row < s, rb, r[0:SUBLANE])
    return jnp.concatenate([head, r[SUBLANE:]], axis=0)


def _shift_up(cur, after, s):
    if s == 0:
        return cur
    n = cur.shape[0]
    r = pltpu.roll(cur, n - s, 0)
    ra = pltpu.roll(after, SUBLANE - s, 0)
    row = lax.broadcasted_iota(jnp.int32, after.shape, 0)
    tail = jnp.where(row >= SUBLANE - s, ra, r[n - SUBLANE:])
    return jnp.concatenate([r[:n - SUBLANE], tail], axis=0)


def _conv_fwd(cur, before, w):
    acc = cur * w[CONV_K - 1:CONV_K, :]
    for s in range(1, CONV_K):
        acc = acc + _shift_down(cur, before, s) * w[CONV_K - 1 - s:CONV_K - s, :]
    return acc


def _matmul_nt(a, bt, *, tm, name, carry=None):
    m, k = a.shape
    n = bt.shape[0]
    c_ins, c_in_specs, c_out_specs, c_outs, c_scratch = _carry_specs(carry)

    def body(*refs):
        a_ref, b_ref, o_ref = _carried(carry, refs, 2, 1, m // tm)
        o_ref[...] = _dot_nt(a_ref[...], b_ref[...]).astype(o_ref.dtype)

    outs = pl.pallas_call(
        body, name=name, grid=(m // tm,),
        in_specs=[pl.BlockSpec((tm, k), lambda i: (i, 0)), pl.BlockSpec((n, k), lambda i: (0, 0))] + c_in_specs,
        out_specs=[pl.BlockSpec((tm, n), lambda i: (i, 0))] + c_out_specs,
        out_shape=[jax.ShapeDtypeStruct((m, n), BF16)] + c_outs,
        scratch_shapes=c_scratch,
        compiler_params=_cp("arbitrary"))(a, bt, *c_ins)
    return outs[0], outs[1:]


def _dn_pre(h, conv_w, par, *, tt, name, carry=None):
    t = h.shape[0]
    cw = 3 * A_WIDTH
    hb = tt // HALO

    c_ins, c_in_specs, c_out_specs, c_outs, c_scratch = _carry_specs(carry)

    def body(*refs):
        (pre_ref, halo_ref, bgi_ref, cw_ref, par_ref,
         q_ref, k_ref, v_ref, bg_ref, bgt_ref) = _carried(carry, refs, 5, 5, t // tt)
        i = pl.program_id(0)
        cur = pre_ref[...].astype(F32)
        before = jnp.where(i > 0, halo_ref[...].astype(F32)[HALO - SUBLANE:], 0.0)
        s = _silu(_conv_fwd(cur, before, cw_ref[...]))
        for hd in range(A_HEADS):
            sl = slice(hd * LANE, (hd + 1) * LANE)
            tq = s[:, hd * LANE:(hd + 1) * LANE]
            q_ref[:, sl] = tq * (lax.rsqrt(jnp.sum(tq * tq, -1, keepdims=True) + L2_EPS) * (A_HEAD_DIM ** -0.5))
            tk = s[:, A_WIDTH + hd * LANE:A_WIDTH + (hd + 1) * LANE]
            k_ref[:, sl] = tk * lax.rsqrt(jnp.sum(tk * tk, -1, keepdims=True) + L2_EPS)
        v_ref[...] = s[:, 2 * A_WIDTH:]
        raw = bgi_ref[...].astype(F32)
        lane = lax.broadcasted_iota(jnp.int32, raw.shape, 1)
        is_a = (lane >= A_HEADS) & (lane < 2 * A_HEADS)
        g = jnp.where(is_a, -jnp.exp(par_ref[0:1, :]) * _softplus(raw + par_ref[1:2, :]), 0.0)
        gc = _dot_hi(_chunk_tri(tt, lower=True), g)
        bg = jnp.where(lane < A_HEADS, _sigmoid(raw), gc)
        bg_ref[...] = bg
        bgt_ref[...] = jnp.transpose(bg)[0:SUBLANE, :]

    wide = jax.ShapeDtypeStruct((t, A_WIDTH), F32)
    outs = pl.pallas_call(
        body, name=name, grid=(t // tt,),
        in_specs=[pl.BlockSpec((tt, cw), lambda i: (i, 0)),
                  pl.BlockSpec((HALO, cw), lambda i: (jnp.maximum(i * hb - 1, 0), 0)),
                  pl.BlockSpec((tt, LANE), lambda i: (i, C_BG // LANE)),
                  pl.BlockSpec((CONV_K, cw), lambda i: (0, 0)),
                  pl.BlockSpec((SUBLANE, LANE), lambda i: (0, 0))] + c_in_specs,
        out_specs=[pl.BlockSpec((tt, A_WIDTH), lambda i: (i, 0))] * 3
        + [pl.BlockSpec((tt, LANE), lambda i: (i, 0)), pl.BlockSpec((SUBLANE, tt), lambda i: (0, i))] + c_out_specs,
        out_shape=[wide, wide, wide, jax.ShapeDtypeStruct((t, LANE), F32),
                   jax.ShapeDtypeStruct((SUBLANE, t), F32)] + c_outs,
        scratch_shapes=c_scratch,
        compiler_params=_cp("arbitrary"))(h, h, h, conv_w, par, *c_ins)
    return outs[:5], outs[5:]


def _chunk_tri(n, lower):
    r = lax.broadcasted_iota(jnp.int32, (n, n), 0)
    c = lax.broadcasted_iota(jnp.int32, (n, n), 1)
    shift = CHUNK.bit_length() - 1
    same = jnp.right_shift(r, shift) == jnp.right_shift(c, shift)
    return (same & ((c <= r) if lower else (c >= r))).astype(F32)


def _chunk_masks():
    r = lax.broadcasted_iota(jnp.int32, (CHUNK, CHUNK), 0)
    c = lax.broadcasted_iota(jnp.int32, (CHUNK, CHUNK), 1)
    return r >= c, r > c, r == c


def _split(a):
    hi = a.astype(BF16)
    return hi, (a - hi.astype(F32)).astype(BF16)


def _dot3(a, b):
    (ah, al), (bh, bl) = a, b
    d = lambda p, q: jnp.dot(p, q, preferred_element_type=F32)
    return d(ah, bh) + (d(ah, bl) + d(al, bh))


def _tri_inv_many(a_list, eye):
    d = lambda p, q: jnp.dot(p.astype(BF16), q.astype(BF16), preferred_element_type=F32)
    r = lax.broadcasted_iota(jnp.int32, (CHUNK, CHUNK), 0)
    c = lax.broadcasted_iota(jnp.int32, (CHUNK, CHUNK), 1)
    same = lambda b: jnp.right_shift(r, b.bit_length() - 1) == jnp.right_shift(c, b.bit_length() - 1)
    x = [jnp.where(same(8), -a, 0.0) for a in a_list]
    tm = [eye + xi for xi in x]
    for _ in range(2):
        x = [d(xi, xi) for xi in x]
        tm = [t + d(t, xi) for t, xi in zip(tm, x)]
    for b in (16, 32, 64):
        low = [jnp.where(same(b) & ~same(b // 2), a, 0.0) for a in a_list]
        tm = [t - d(t, d(lo, t)) for t, lo in zip(tm, low)]
    res = [eye - _dot3(_split(eye + a), _split(t)) for a, t in zip(a_list, tm)]
    return [t + d(t, rs) for t, rs in zip(tm, res)]


def _chunk_gates(bg_v, bgt_v, hd):
    return (bg_v[:, hd:hd + 1], bg_v[:, A_HEADS + hd:A_HEADS + hd + 1],
            None if bgt_v is None else bgt_v[A_HEADS + hd:A_HEADS + hd + 1, :])


WY_ROWS = 512
SCAN_ROWS = 512
WY_GROUP = 8


def _dn_wy(q, k, v, bg, bgt, *, name, carry=None):
    t = q.shape[0]
    rows = WY_ROWS

    c_ins, c_in_specs, c_out_specs, c_outs, c_scratch = _carry_specs(carry)

    def body(*refs):
        q_ref, k_ref, v_ref, bg_ref, bgt_ref, u_ref, w_ref, tm_ref, qk_ref = _carried(carry, refs, 5, 4, t // rows)
        causal, strict, diag = _chunk_masks()
        eye = diag.astype(F32)
        for c0 in range(0, rows // CHUNK, WY_GROUP):
            items = [(c, hd) for c in range(c0, c0 + WY_GROUP) for hd in range(A_HEADS)]
            rs = lambda c: slice(c * CHUNK, (c + 1) * CHUNK)
            sl = lambda hd: slice(hd * LANE, (hd + 1) * LANE)
            hs = lambda hd: slice(hd * CHUNK, (hd + 1) * CHUNK)
            gates = [_chunk_gates(bg_ref[rs(c), :], bgt_ref[:, rs(c)], hd) for c, hd in items]
            dms = [jnp.exp(jnp.where(causal, gcol - grow, NEG)) for _, gcol, grow in gates]
            kbs = [k_ref[rs(c), sl(hd)] * g[0] for (c, hd), g in zip(items, gates)]
            a_list = [jnp.where(strict, _dot_nt(kb, k_ref[rs(c), sl(hd)]) * dm, 0.0)
                      for (c, hd), kb, dm in zip(items, kbs, dms)]
            for (c, hd), dm in zip(items, dms):
                qk_ref[rs(c), hs(hd)] = jnp.where(
                    causal, _dot_nt(q_ref[rs(c), sl(hd)], k_ref[rs(c), sl(hd)]) * dm, 0.0)
            tms = _tri_inv_many(a_list, eye)
            for (c, hd), g, kb, tmat in zip(items, gates, kbs, tms):
                tm_ref[rs(c), hs(hd)] = tmat
                u_ref[rs(c), sl(hd)] = _dot(tmat, v_ref[rs(c), sl(hd)] * g[0])
                w_ref[rs(c), sl(hd)] = _dot(tmat, kb * jnp.exp(g[1])).astype(BF16)

    blk = pl.BlockSpec((rows, A_WIDTH), lambda i: (i, 0))
    half = pl.BlockSpec((rows, A_HEADS * CHUNK), lambda i: (i, 0))
    outs = pl.pallas_call(
        body, name=name, grid=(t // rows,),
        in_specs=[blk, blk, blk, pl.BlockSpec((rows, LANE), lambda i: (i, 0)),
                  pl.BlockSpec((SUBLANE, rows), lambda i: (0, i))] + c_in_specs,
        out_specs=[blk, blk, half, half] + c_out_specs,
        out_shape=[jax.ShapeDtypeStruct((t, A_WIDTH), F32), jax.ShapeDtypeStruct((t, A_WIDTH), BF16),
                   jax.ShapeDtypeStruct((t, A_HEADS * CHUNK), F32),
                   jax.ShapeDtypeStruct((t, A_HEADS * CHUNK), F32)] + c_outs,
        scratch_shapes=c_scratch,
        compiler_params=_cp("arbitrary"))(q, k, v, bg, bgt, *c_ins)
    return outs[:4], outs[4:]


def _dn_scan_fwd(q, k, u, w, qk, bg, *, name, carry=None):
    t = q.shape[0]
    rows = SCAN_ROWS
    per = rows // CHUNK
    c_ins, c_in_specs, c_out_specs, c_outs, c_scratch = _carry_specs(carry)

    def body(*refs):
        q_ref, k_ref, u_ref, w_ref, qk_ref, bg_ref, o_ref, vn_ref, s_ref, state = _carried(carry, refs, 6, 3, t // rows)

        @pl.when(pl.program_id(0) == 0)
        def _():
            state[...] = jnp.zeros_like(state)

        heads = range(A_HEADS)
        sl = lambda hd: slice(hd * LANE, (hd + 1) * LANE)
        s_cur = [state[hd] for hd in heads]
        for c in range(per):
            rs = slice(c * CHUNK, (c + 1) * CHUNK)
            bg_v = bg_ref[rs, :]
            gcols = [_chunk_gates(bg_v, None, hd)[1] for hd in heads]
            glasts = [gc[CHUNK - 1:CHUNK, :] for gc in gcols]
            for hd in heads:
                s_ref[c, hd] = s_cur[hd].astype(BF16)
            vns = [u_ref[rs, sl(hd)] - _dot(w_ref[rs, sl(hd)], s_cur[hd]) for hd in heads]
            qss = [_dot(q_ref[rs, sl(hd)] * jnp.exp(gcols[hd]), s_cur[hd]) for hd in heads]
            s_cur = [s_cur[hd] * jnp.exp(glasts[hd])
                     + _dot_tn(k_ref[rs, sl(hd)] * jnp.exp(glasts[hd] - gcols[hd]), vns[hd]) for hd in heads]
            for hd in heads:
                vn_ref[rs, sl(hd)] = vns[hd]
                o_ref[rs, sl(hd)] = qss[hd] + _dot(qk_ref[rs, hd * CHUNK:(hd + 1) * CHUNK], vns[hd])
        for hd in heads:
            state[hd] = s_cur[hd]

    blk = pl.BlockSpec((rows, A_WIDTH), lambda i: (i, 0))
    half = pl.BlockSpec((rows, A_HEADS * CHUNK), lambda i: (i, 0))
    wide = jax.ShapeDtypeStruct((t, A_WIDTH), F32)
    outs = pl.pallas_call(
        body, name=name, grid=(t // rows,),
        in_specs=[blk, blk, blk, blk, half, pl.BlockSpec((rows, LANE), lambda i: (i, 0))] + c_in_specs,
        out_specs=[blk, blk, pl.BlockSpec((per, A_HEADS, LANE, LANE), lambda i: (i, 0, 0, 0))] + c_out_specs,
        out_shape=[wide, wide, jax.ShapeDtypeStruct((t // CHUNK, A_HEADS, LANE, LANE), BF16)] + c_outs,
        scratch_shapes=[pltpu.VMEM((A_HEADS, LANE, LANE), F32)] + c_scratch,
        compiler_params=_cp("arbitrary"))(q, k, u, w, qk, bg, *c_ins)
    return outs[:3], outs[3:]


def _stack_heads(ref, hk):
    return jnp.concatenate([ref[:, h * B_HEAD_DIM:(h + 1) * B_HEAD_DIM].astype(F32)
                            for h in range(hk * B_GROUP, (hk + 1) * B_GROUP)], axis=0)


def _swa_window():
    qi = lax.broadcasted_iota(jnp.int32, (BLOCK, BLOCK), 0)
    kj = lax.broadcasted_iota(jnp.int32, (BLOCK, BLOCK), 1)
    dist = jnp.where(kj > qi, qi + BLOCK - kj, qi - kj).astype(F32)
    rows = lax.broadcasted_iota(jnp.int32, (B_GROUP * BLOCK, BLOCK), 0)
    cols = lax.broadcasted_iota(jnp.int32, (B_GROUP * BLOCK, BLOCK), 1)
    return cols > jnp.bitwise_and(rows, BLOCK - 1), dist


def _swa_group_probs(q_ref, sk_ref, kp, kc, vp, vc, n_blk):
    hks = range(B_KV_HEADS)
    heads = lambda hk: range(hk * B_GROUP, (hk + 1) * B_GROUP)
    ksl = lambda hk: slice(hk * B_HEAD_DIM, (hk + 1) * B_HEAD_DIM)
    upper, dist = _swa_window()
    no_prev = jnp.where(n_blk > 0, 0.0, NEG)
    ones = jnp.ones((BLOCK, B_HEAD_DIM), BF16)
    with_ones = lambda v, hk: jnp.concatenate([v[:, ksl(hk)].astype(BF16), ones], axis=1)
    qs = [_stack_heads(q_ref, hk) * (B_HEAD_DIM ** -0.5) for hk in hks]
    sink = [jnp.concatenate([jnp.broadcast_to(sk_ref[h:h + 1, 0:1], (BLOCK, 1)) for h in heads(hk)], axis=0)
            for hk in hks]
    s = [jnp.where(upper, _dot_nt(qs[hk], kp[:, ksl(hk)]) + no_prev, _dot_nt(qs[hk], kc[:, ksl(hk)]))
         - jnp.concatenate([ALIBI[h] * dist for h in heads(hk)], axis=0) for hk in hks]
    m = [jnp.maximum(jnp.max(s[hk], axis=-1, keepdims=True), sink[hk]) for hk in hks]
    p = [jnp.exp(s[hk] - m[hk]) for hk in hks]
    p_up = [jnp.where(upper, p[hk], 0.0) for hk in hks]
    oe = [jnp.dot(p_up[hk].astype(BF16), with_ones(vp, hk), preferred_element_type=F32)
          + jnp.dot((p[hk] - p_up[hk]).astype(BF16), with_ones(vc, hk), preferred_element_type=F32) for hk in hks]
    ps = [jnp.exp(sink[hk] - m[hk]) for hk in hks]
    inv = [1.0 / (oe[hk][:, B_HEAD_DIM:B_HEAD_DIM + 1] + ps[hk]) for hk in hks]
    return upper, [(qs[hk], p[hk] * inv[hk], ps[hk] * inv[hk], oe[hk][:, :B_HEAD_DIM] * inv[hk]) for hk in hks]


def _swa_specs():
    qspec = lambda c0: pl.BlockSpec((BLOCK, B_WIDTH), lambda i: (i, c0 // B_WIDTH))
    cur = lambda c0: pl.BlockSpec((BLOCK, LANE), lambda i: (i, c0 // LANE))
    prev = lambda c0: pl.BlockSpec((BLOCK, LANE), lambda i: (jnp.maximum(i - 1, 0), c0 // LANE))
    return qspec, cur, prev


def _carried(carry, refs, n_in, n_out, steps):
    if carry is None:
        return refs
    ci, co = len(carry.ins), len(carry.outs)
    own = refs[:n_in] + refs[n_in + ci:n_in + ci + n_out] + refs[n_in + ci + n_out + co:len(refs) - 3]
    parts = refs[n_in:n_in + ci], refs[n_in + ci + n_out:n_in + ci + n_out + co], refs[len(refs) - 3:]

    @pl.when(pl.program_id(0) == 0)
    def _():
        carry.start(*parts)

    @pl.when(pl.program_id(0) == steps - 1)
    def _():
        carry.finish(*parts)

    return own


def _carry_specs(carry):
    if carry is None:
        return [], [], [], [], []
    return (list(carry.ins), [_ANY] * len(carry.ins), [_ANY] * len(carry.outs), list(carry.outs), carry.scratch())


def _swa_fwd(h, sinks_b, *, name, carry=None):
    t = h.shape[0]
    qspec, cur, prev = _swa_specs()
    c_ins, c_in_specs, c_out_specs, c_outs, c_scratch = _carry_specs(carry)

    def body(*refs):
        q_ref, kc_ref, kp_ref, vc_ref, vp_ref, sk_ref, o_ref = _carried(carry, refs, 6, 1, t // BLOCK)
        n_blk = pl.program_id(0)
        _, groups = _swa_group_probs(q_ref, sk_ref, kp_ref[...], kc_ref[...], vp_ref[...], vc_ref[...], n_blk)
        for hk, (_, _, _, o) in enumerate(groups):
            for g in range(B_GROUP):
                hq = hk * B_GROUP + g
                o_ref[:, hq * B_HEAD_DIM:(hq + 1) * B_HEAD_DIM] = o[g * BLOCK:(g + 1) * BLOCK]

    outs = pl.pallas_call(
        body, name=name, grid=(t // BLOCK,),
        in_specs=[qspec(C_QB), cur(C_KB), prev(C_KB), cur(C_VB), prev(C_VB),
                  pl.BlockSpec((B_Q_HEADS, LANE), lambda i: (0, 0))] + c_in_specs,
        out_specs=[pl.BlockSpec((BLOCK, B_WIDTH), lambda i: (i, 0))] + c_out_specs,
        out_shape=[jax.ShapeDtypeStruct((t, B_WIDTH), F32)] + c_outs,
        scratch_shapes=c_scratch,
        compiler_params=_cp("arbitrary"))(h, h, h, h, h, sinks_b, *c_ins)
    return outs[0], outs[1:]


def _rms_gate(o, za, nw):
    outs = []
    for hd in range(A_HEADS):
        oh = o[:, hd * LANE:(hd + 1) * LANE]
        r = lax.rsqrt(jnp.mean(oh * oh, -1, keepdims=True) + RMS_EPS)
        outs.append(oh * r * nw)
    return jnp.concatenate(outs, axis=1) * _silu(za)


def _out_ln(x, oa, ob, h, norm_w, w_out, ln_g, ln_b, *, tm, name, target=None):
    t = x.shape[0]
    last = target is not None

    def body(*refs):
        x_ref, oa_ref, ob_ref, za_ref, zb_ref, nw_ref, w_ref, g_ref, b_ref = refs[:9]
        xn_ref, mx_ref, r_ref = refs[9 + last:12 + last]
        ya = _rms_gate(oa_ref[...], za_ref[...].astype(F32), nw_ref[...])
        yb = ob_ref[...] * _silu(zb_ref[...].astype(F32))
        mixed = jnp.concatenate([ya, yb], axis=1).astype(BF16)
        mx_ref[...] = mixed
        r = DEEPNORM_ALPHA * x_ref[...] + jnp.dot(mixed, w_ref[...], preferred_element_type=F32)
        r_ref[...] = r
        mu = jnp.mean(r, -1, keepdims=True)
        xc = r - mu
        var = jnp.mean(xc * xc, -1, keepdims=True)
        xn = xc * lax.rsqrt(var + LN_EPS) * g_ref[...] + b_ref[...]
        if not last:
            xn_ref[...] = xn
            return
        loss_ref = refs[13]

        @pl.when(pl.program_id(0) == 0)
        def _():
            loss_ref[...] = jnp.zeros_like(loss_ref)

        err = xn - refs[9][...]
        xn_ref[...] = err * (1.0 / D_MODEL)
        loss_ref[...] += 0.5 / D_MODEL * jnp.sum(err * err)

    row = lambda w, c: pl.BlockSpec((tm, w), lambda i: (i, c))
    full = lambda a, b: pl.BlockSpec((a, b), lambda i: (0, 0))
    wide = jax.ShapeDtypeStruct((t, D_MODEL), F32)
    return pl.pallas_call(
        body, name=name, grid=(t // tm,),
        in_specs=[row(D_MODEL, 0), row(A_WIDTH, 0), row(B_WIDTH, 0), row(A_WIDTH, C_ZA // A_WIDTH),
                  row(B_WIDTH, C_ZB // B_WIDTH), full(1, LANE), full(D_MODEL, D_MODEL), full(1, D_MODEL),
                  full(1, D_MODEL)] + [row(D_MODEL, 0)] * last,
        out_specs=[row(D_MODEL, 0), row(D_MODEL, 0), row(D_MODEL, 0)] + [full(SUBLANE, LANE)] * last,
        out_shape=[wide, jax.ShapeDtypeStruct((t, D_MODEL), BF16), wide]
        + [jax.ShapeDtypeStruct((SUBLANE, LANE), F32)] * last,
        compiler_params=_cp("arbitrary" if last else "parallel"))(
        x, oa, ob, h, h, norm_w, w_out, ln_g, ln_b, *([target] if last else []))


def _layer_fwd(x, wt, conv_w, par, sinks_b, norm_w, w_out_bf, ln_g, ln_b, l, carries=None, target=None):
    carries = carries or {}
    h, got_in = _matmul_nt(x, wt, tm=512, name=f"in_proj_{l}", carry=carries.get("in_proj"))
    if callable(w_out_bf):
        w_out_bf = w_out_bf(got_in)
    (q, k, v, bg, bgt), got_pre = _dn_pre(h, conv_w, par, tt=512, name=f"dn_pre_{l}", carry=carries.get("dn_pre"))
    (u, w, tmat, qk), got_wy = _dn_wy(q, k, v, bg, bgt, name=f"dn_wy_{l}", carry=carries.get("dn_wy"))
    (oa, vn, s_all), got_scan = _dn_scan_fwd(q, k, u, w, qk, bg, name=f"dn_scan_{l}", carry=carries.get("dn_scan"))
    ob, got_swa = _swa_fwd(h, sinks_b, name=f"swa_fwd_{l}", carry=carries.get("swa"))
    xn, mixed, r, *loss = _out_ln(x, oa, ob, h, norm_w, w_out_bf, ln_g, ln_b, tm=512, name=f"out_ln_{l}", target=target)
    if loss:
        xn = (xn, loss[0])
    res = dict(x=x, h=h, q=q, k=k, v=v, bg=bg, bgt=bgt, w=w, tmat=tmat, qk=qk, vn=vn, oa=oa, s_all=s_all,
               mixed=mixed, r=r, w_out=w_out_bf)
    return xn, res, dict(in_proj=got_in, dn_pre=got_pre, dn_wy=got_wy, dn_scan=got_scan, swa=got_swa)


def _ln_out_bwd(dxn, r, mixed, ln_g, w_out, *, tm, name):
    t = dxn.shape[0]

    def body(dxn_ref, r_ref, mx_ref, g_ref, w_ref, dr_ref, dm_ref, dw_ref, dg_ref, db_ref):
        @pl.when(pl.program_id(0) == 0)
        def _():
            dw_ref[...] = jnp.zeros_like(dw_ref)
            dg_ref[...] = jnp.zeros_like(dg_ref)
            db_ref[...] = jnp.zeros_like(db_ref)

        rr = r_ref[...]
        xc = rr - jnp.mean(rr, -1, keepdims=True)
        rstd = lax.rsqrt(jnp.mean(xc * xc, -1, keepdims=True) + LN_EPS)
        xhat = xc * rstd
        dxn_v = dxn_ref[...]
        dxh = dxn_v * g_ref[...]
        dr = rstd * (dxh - jnp.mean(dxh, -1, keepdims=True) - xhat * jnp.mean(dxh * xhat, -1, keepdims=True))
        dr_ref[...] = dr
        dg_ref[...] += jnp.sum(dxn_v * xhat, axis=0, keepdims=True)
        db_ref[...] += jnp.sum(dxn_v, axis=0, keepdims=True)
        drb = dr.astype(BF16)
        dm_ref[...] = _dot_nt(drb, w_ref[...])
        dw_ref[...] += _dot_tn(mx_ref[...], drb)

    row = pl.BlockSpec((tm, D_MODEL), lambda i: (i, 0))
    full = lambda a, b: pl.BlockSpec((a, b), lambda i: (0, 0))
    big = jax.ShapeDtypeStruct((t, D_MODEL), F32)
    vec = jax.ShapeDtypeStruct((1, D_MODEL), F32)
    return pl.pallas_call(
        body, name=name, grid=(t // tm,),
        in_specs=[row, row, row, full(1, D_MODEL), full(D_MODEL, D_MODEL)],
        out_specs=[row, row, full(D_MODEL, D_MODEL), full(1, D_MODEL), full(1, D_MODEL)],
        out_shape=[big, big, jax.ShapeDtypeStruct((D_MODEL, D_MODEL), F32), vec, vec],
        compiler_params=_cp("arbitrary"))(dxn, r, mixed, ln_g, w_out)


def _dn_post_bwd(dm, oa, h, norm_w, *, tm, name):
    t = oa.shape[0]

    def body(dy_ref, o_ref, za_ref, nw_ref, do_ref, dza_ref, dnw_ref):
        @pl.when(pl.program_id(0) == 0)
        def _():
            dnw_ref[...] = jnp.zeros_like(dnw_ref)

        nw = nw_ref[...]
        dnw = jnp.zeros_like(nw)
        for hd in range(A_HEADS):
            sl = slice(hd * LANE, (hd + 1) * LANE)
            oh, za, dy = o_ref[:, sl], za_ref[:, sl].astype(F32), dy_ref[:, sl]
            rs = lax.rsqrt(jnp.mean(oh * oh, -1, keepdims=True) + RMS_EPS)
            nrm = oh * rs
            gate, dgate = _silu_and_grad(za)
            dza_ref[:, sl] = dy * nrm * nw * dgate
            dn = dy * gate
            dnw = dnw + jnp.sum(dn * nrm, axis=0, keepdims=True)
            dnn = dn * nw
            do_ref[:, sl] = rs * dnn - oh * (rs * rs * rs) * jnp.mean(dnn * oh, -1, keepdims=True)
        dnw_ref[...] += dnw

    row = lambda c: pl.BlockSpec((tm, A_WIDTH), lambda i: (i, c))
    wide = jax.ShapeDtypeStruct((t, A_WIDTH), F32)
    return pl.pallas_call(
        body, name=name, grid=(t // tm,),
        in_specs=[row(0), row(0), row(C_ZA // A_WIDTH), pl.BlockSpec((1, LANE), lambda i: (0, 0))],
        out_specs=[row(0), row(C_ZA // A_WIDTH), pl.BlockSpec((1, LANE), lambda i: (0, 0))],
        out_shape=[wide, jax.ShapeDtypeStruct((t, DH_MAIN), F32), jax.ShapeDtypeStruct((1, LANE), F32)],
        compiler_params=_cp("arbitrary"))(dm, oa, h, norm_w)


def _dn_scan_bwd(q, k, w, qk, bg, do, *, name):
    t = q.shape[0]
    rows = SCAN_ROWS
    per = rows // CHUNK
    n = t // rows

    def body(q_ref, k_ref, w_ref, qk_ref, bg_ref, do_ref, dvn_ref, ds_ref, dstate):
        @pl.when(pl.program_id(0) == 0)
        def _():
            dstate[...] = jnp.zeros_like(dstate)

        heads = range(A_HEADS)
        sl = lambda hd: slice(hd * LANE, (hd + 1) * LANE)
        ds_cur = [dstate[hd] for hd in heads]
        for c in reversed(range(per)):
            rs = slice(c * CHUNK, (c + 1) * CHUNK)
            bg_v = bg_ref[rs, :]
            gcols = [_chunk_gates(bg_v, None, hd)[1] for hd in heads]
            glasts = [gc[CHUNK - 1:CHUNK, :] for gc in gcols]
            for hd in heads:
                ds_ref[c, hd] = ds_cur[hd].astype(BF16)
            pdo = [_dot_tn(qk_ref[rs, hd * CHUNK:(hd + 1) * CHUNK], do_ref[rs, sl(hd)]) for hd in heads]
            qdo = [_dot_tn(q_ref[rs, sl(hd)] * jnp.exp(gcols[hd]), do_ref[rs, sl(hd)]) for hd in heads]
            dvns = [pdo[hd] + _dot(k_ref[rs, sl(hd)] * jnp.exp(glasts[hd] - gcols[hd]), ds_cur[hd]) for hd in heads]
            ds_cur = [qdo[hd] + jnp.exp(glasts[hd]) * ds_cur[hd] - _dot_tn(w_ref[rs, sl(hd)], dvns[hd])
                      for hd in heads]
            for hd in heads:
                dvn_ref[rs, sl(hd)] = dvns[hd]
        for hd in heads:
            dstate[hd] = ds_cur[hd]

    blk = pl.BlockSpec((rows, A_WIDTH), lambda i: (n - 1 - i, 0))
    return pl.pallas_call(
        body, name=name, grid=(n,),
        in_specs=[blk, blk, blk, pl.BlockSpec((rows, A_HEADS * CHUNK), lambda i: (n - 1 - i, 0)),
                  pl.BlockSpec((rows, LANE), lambda i: (n - 1 - i, 0)), blk],
        out_specs=[blk, pl.BlockSpec((per, A_HEADS, LANE, LANE), lambda i: (n - 1 - i, 0, 0, 0))],
        out_shape=[jax.ShapeDtypeStruct((t, A_WIDTH), F32),
                   jax.ShapeDtypeStruct((t // CHUNK, A_HEADS, LANE, LANE), BF16)],
        scratch_shapes=[pltpu.VMEM((A_HEADS, LANE, LANE), F32)],
        compiler_params=_cp("arbitrary"))(q, k, w, qk, bg, do)


def _dn_chunk_bwd(q, k, v, vn, tmat, qk, bg, bgt, s_all, ds_all, dvn, do, *, name, carry=None):
    t = q.shape[0]
    rows = WY_ROWS
    per = rows // CHUNK

    c_ins, c_in_specs, c_out_specs, c_outs, c_scratch = _carry_specs(carry)

    def body(*refs):
        (q_ref, k_ref, v_ref, vn_ref, tm_ref, qk_ref, bg_ref, bgt_ref, s_ref, ds_ref, dvn_ref, do_ref,
         dq_ref, dk_ref, dv_ref, dbg_ref, dbgt_ref) = _carried(carry, refs, 12, 5, t // rows)
        causal, strict, _ = _chunk_masks()
        lane = lax.broadcasted_iota(jnp.int32, (CHUNK, LANE), 1)
        rowi = lax.broadcasted_iota(jnp.int32, (CHUNK, 1), 0)
        sub = lax.broadcasted_iota(jnp.int32, (SUBLANE, CHUNK), 0)
        rs = lambda c: slice(c * CHUNK, (c + 1) * CHUNK)
        sl = lambda hd: slice(hd * LANE, (hd + 1) * LANE)
        hs = lambda hd: slice(hd * CHUNK, (hd + 1) * CHUNK)
        for c0 in range(0, per, WY_GROUP):
            items = [(c, hd) for c in range(c0, c0 + WY_GROUP) for hd in range(A_HEADS)]
            at = lambda ref: [ref[rs(c), sl(hd)] for c, hd in items]
            qs, ks, vs, dos, vns, dvns = at(q_ref), at(k_ref), at(v_ref), at(do_ref), at(vn_ref), at(dvn_ref)
            tmhs = [tm_ref[rs(c), hs(hd)] for c, hd in items]
            ps = [qk_ref[rs(c), hs(hd)] for c, hd in items]
            gates = [_chunk_gates(bg_ref[rs(c), :], bgt_ref[:, rs(c)], hd) for c, hd in items]
            betas = [g[0] for g in gates]
            gcols = [g[1] for g in gates]
            dmats = [jnp.exp(jnp.where(causal, g[1] - g[2], NEG)) for g in gates]
            es = [jnp.exp(gc) for gc in gcols]
            glasts = [gc[CHUNK - 1:CHUNK, :] for gc in gcols]
            eks = [jnp.exp(gl - gc) for gl, gc in zip(glasts, gcols)]
            kbs = [kh * b for kh, b in zip(ks, betas)]
            vbs = [vh * b for vh, b in zip(vs, betas)]
            kbes = [kb * e for kb, e in zip(kbs, es)]

            a_s = [jnp.where(strict, _dot_nt(kb, kh) * dm, 0.0) for kb, kh, dm in zip(kbs, ks, dmats)]
            dps = [jnp.where(causal, _dot_nt(doh, vnh), 0.0) for doh, vnh in zip(dos, vns)]
            rows2 = lambda a, b: jnp.concatenate([a, b], axis=0)
            cols2 = lambda a, b: jnp.concatenate([a, b], axis=1)
            by_s = [_dot_nt(rows2(doh, dvnh), s_ref[c, hd]) for doh, dvnh, (c, hd) in zip(dos, dvns, items)]
            dqds = [m[:CHUNK] for m in by_s]
            dws = [-m[CHUNK:] for m in by_s]
            dkds = [_dot_nt(vnh, ds_ref[c, hd]) for vnh, (c, hd) in zip(vns, items)]
            dgts = [jnp.sum(s_ref[c, hd].astype(F32) * ds_ref[c, hd].astype(F32), keepdims=True) for c, hd in items]
            pairs = [cols2(dvnh, dw) for dvnh, dw in zip(dvns, dws)]
            by_t = [_dot_tn(tmh, pr) for tmh, pr in zip(tmhs, pairs)]
            dvbs = [m[:, :LANE] for m in by_t]
            dkbes = [m[:, LANE:] for m in by_t]
            dts = [_dot_nt(pr, cols2(vb, kbe)) for pr, vb, kbe in zip(pairs, vbs, kbes)]
            xs = [_dot_nt(dt, tmh) for dt, tmh in zip(dts, tmhs)]
            das = [jnp.where(strict, -_dot_tn(tmh, x), 0.0) for tmh, x in zip(tmhs, xs)]
            dmas = [da * dm for da, dm in zip(das, dmats)]
            dmps = [dp * dm for dp, dm in zip(dps, dmats)]
            stacked = [rows2(dma, dmp) for dma, dmp in zip(dmas, dmps)]
            by_k = [_dot(st, kh) for st, kh in zip(stacked, ks)]
            dkbs = [m[:CHUNK] + dkbe * e for m, dkbe, e in zip(by_k, dkbes, es)]
            for i, (c, hd) in enumerate(items):
                dq_ref[rs(c), sl(hd)] = by_k[i][CHUNK:] + dqds[i] * es[i]
                dk_ref[rs(c), sl(hd)] = (_dot_tn(stacked[i], rows2(kbs[i], qs[i])) + dkds[i] * eks[i]
                                         + dkbs[i] * betas[i])
                dv_ref[rs(c), sl(hd)] = dvbs[i] * betas[i]
            for c in range(c0, c0 + WY_GROUP):
                acc = jnp.zeros((CHUNK, LANE), F32)
                acc_t = jnp.zeros((SUBLANE, CHUNK), F32)
                for i, (ci, hd) in enumerate(items):
                    if ci != c:
                        continue
                    gmat = das[i] * a_s[i] + dps[i] * ps[i]
                    rk = jnp.sum(dkds[i] * ks[i], -1, keepdims=True) * eks[i]
                    de = jnp.sum(dqds[i] * qs[i] + dkbes[i] * kbs[i], -1, keepdims=True)
                    dglast = jnp.sum(rk, keepdims=True) + dgts[i] * jnp.exp(glasts[i])
                    dgc = (jnp.sum(gmat, -1, keepdims=True) + de * es[i] - rk
                           + jnp.where(rowi == CHUNK - 1, dglast, 0.0))
                    dbeta = jnp.sum(dkbs[i] * ks[i] + dvbs[i] * vs[i], -1, keepdims=True)
                    acc = acc + jnp.where(lane == hd, dbeta, 0.0) + jnp.where(lane == A_HEADS + hd, dgc, 0.0)
                    acc_t = acc_t + jnp.where(sub == A_HEADS + hd, -jnp.sum(gmat, axis=0, keepdims=True), 0.0)
                dbg_ref[rs(c), :] = acc
                dbgt_ref[:, rs(c)] = acc_t

    blk = pl.BlockSpec((rows, A_WIDTH), lambda i: (i, 0))
    half = pl.BlockSpec((rows, A_HEADS * CHUNK), lambda i: (i, 0))
    col = pl.BlockSpec((rows, LANE), lambda i: (i, 0))
    rowf = pl.BlockSpec((SUBLANE, rows), lambda i: (0, i))
    st = pl.BlockSpec((per, A_HEADS, LANE, LANE), lambda i: (i, 0, 0, 0))
    wide = jax.ShapeDtypeStruct((t, A_WIDTH), F32)
    outs = pl.pallas_call(
        body, name=name, grid=(t // rows,),
        in_specs=[blk, blk, blk, blk, half, half, col, rowf, st, st, blk, blk] + c_in_specs,
        out_specs=[blk, blk, blk, col, rowf] + c_out_specs,
        out_shape=[wide, wide, wide, jax.ShapeDtypeStruct((t, LANE), F32),
                   jax.ShapeDtypeStruct((SUBLANE, t), F32)] + c_outs,
        scratch_shapes=c_scratch,
        compiler_params=_cp("arbitrary"))(q, k, v, vn, tmat, qk, bg, bgt, s_all, ds_all, dvn, do, *c_ins)
    return outs[:5], outs[5:]


def _dn_pre_bwd(h, conv_w, par, dq, dk, dv, dbg, dbgt, *, tt, name):
    t = h.shape[0]
    cw = 3 * A_WIDTH
    hb = tt // HALO

    def body(pre_ref, halo_ref, bgi_ref, cw_ref, par_ref, dq_ref, dk_ref, dv_ref, dbg_ref, dbgt_ref,
             dc_ref, dbgi_ref, dpar_ref):
        i = pl.program_id(0)

        @pl.when(i == 0)
        def _():
            dpar_ref[...] = jnp.zeros_like(dpar_ref)

        cur = pre_ref[...].astype(F32)
        before = jnp.where(i > 0, halo_ref[...].astype(F32)[HALO - SUBLANE:], 0.0)
        c = _conv_fwd(cur, before, cw_ref[...])
        s, ds = _silu_and_grad(c)
        for hd in range(A_HEADS):
            sl = slice(hd * LANE, (hd + 1) * LANE)
            for base, d_ref, scale in ((0, dq_ref, A_HEAD_DIM ** -0.5), (A_WIDTH, dk_ref, 1.0)):
                csl = slice(base + hd * LANE, base + (hd + 1) * LANE)
                tq = s[:, base + hd * LANE:base + (hd + 1) * LANE]
                dy = d_ref[:, sl]
                rq = lax.rsqrt(jnp.sum(tq * tq, -1, keepdims=True) + L2_EPS)
                dtq = scale * (rq * dy - tq * (rq * rq * rq) * jnp.sum(dy * tq, -1, keepdims=True))
                dc_ref[:, csl] = dtq * ds[:, base + hd * LANE:base + (hd + 1) * LANE]
        dc_ref[:, 2 * A_WIDTH:] = dv_ref[...] * ds[:, 2 * A_WIDTH:]
        raw = bgi_ref[...].astype(F32)
        lane = lax.broadcasted_iota(jnp.int32, raw.shape, 1)
        is_b = lane < A_HEADS
        is_a = (lane >= A_HEADS) & (lane < 2 * A_HEADS)
        rows_t = jnp.concatenate([dbgt_ref[...], jnp.zeros((LANE - SUBLANE, tt), F32)], axis=0)
        dbg_v = dbg_ref[...] + jnp.where(is_a, jnp.transpose(rows_t), 0.0)
        dbg_v = jnp.where(is_a, _dot_hi(_chunk_tri(tt, lower=False), jnp.where(is_a, dbg_v, 0.0)), dbg_v)
        beta = _sigmoid(raw)
        z = raw + par_ref[1:2, :]
        neg_ea = -jnp.exp(par_ref[0:1, :])
        g = neg_ea * _softplus(z)
        da = dbg_v * neg_ea * _sigmoid(z)
        dbgi_ref[...] = jnp.where(is_b, dbg_v * beta * (1.0 - beta), jnp.where(is_a, da, 0.0))
        dpar_ref[0:1, :] += jnp.sum(jnp.where(is_a, dbg_v * g, 0.0), axis=0, keepdims=True)
        dpar_ref[1:2, :] += jnp.sum(jnp.where(is_a, da, 0.0), axis=0, keepdims=True)

    wide = pl.BlockSpec((tt, A_WIDTH), lambda i: (i, 0))
    return pl.pallas_call(
        body, name=name, grid=(t // tt,),
        in_specs=[pl.BlockSpec((tt, cw), lambda i: (i, 0)),
                  pl.BlockSpec((HALO, cw), lambda i: (jnp.maximum(i * hb - 1, 0), 0)),
                  pl.BlockSpec((tt, LANE), lambda i: (i, C_BG // LANE)),
                  pl.BlockSpec((CONV_K, cw), lambda i: (0, 0)),
                  pl.BlockSpec((SUBLANE, LANE), lambda i: (0, 0)),
                  wide, wide, wide, pl.BlockSpec((tt, LANE), lambda i: (i, 0)),
                  pl.BlockSpec((SUBLANE, tt), lambda i: (0, i))],
        out_specs=[pl.BlockSpec((tt, cw), lambda i: (i, 0)), pl.BlockSpec((tt, LANE), lambda i: (i, 0)),
                   pl.BlockSpec((SUBLANE, LANE), lambda i: (0, 0))],
        out_shape=[jax.ShapeDtypeStruct((t, cw), F32), jax.ShapeDtypeStruct((t, LANE), F32),
                   jax.ShapeDtypeStruct((SUBLANE, LANE), F32)],
        compiler_params=_cp("arbitrary"))(h, h, h, conv_w, par, dq, dk, dv, dbg, dbgt)


def _conv_bwd(dc, h, conv_w, dh, *, tt, name):
    t = dc.shape[0]
    cw = 3 * A_WIDTH
    hb = tt // HALO
    nb = t // tt

    def body(dc_ref, after_ref, pre_ref, before_ref, cw_ref, dh_in_ref, dpre_ref, dcw_ref):
        i = pl.program_id(0)

        @pl.when(i == 0)
        def _():
            dcw_ref[...] = jnp.zeros_like(dcw_ref)

        dcv = dc_ref[...]
        after = jnp.where(i < nb - 1, after_ref[...], 0.0)
        cur = pre_ref[...].astype(F32)
        before = jnp.where(i > 0, before_ref[...].astype(F32)[HALO - SUBLANE:], 0.0)
        w = cw_ref[...]
        acc = dcv * w[CONV_K - 1:CONV_K, :]
        dcw_ref[CONV_K - 1:CONV_K, :] += jnp.sum(dcv * cur, axis=0, keepdims=True)
        for s in range(1, CONV_K):
            j = CONV_K - 1 - s
            acc = acc + _shift_up(dcv, after, s) * w[j:j + 1, :]
            dcw_ref[j:j + 1, :] += jnp.sum(dcv * _shift_down(cur, before, s), axis=0, keepdims=True)
        dpre_ref[...] = acc

    return pl.pallas_call(
        body, name=name, grid=(nb,),
        in_specs=[pl.BlockSpec((tt, cw), lambda i: (i, 0)),
                  pl.BlockSpec((SUBLANE, cw), lambda i: (jnp.minimum((i + 1) * (tt // SUBLANE), t // SUBLANE - 1), 0)),
                  pl.BlockSpec((tt, cw), lambda i: (i, 0)),
                  pl.BlockSpec((HALO, cw), lambda i: (jnp.maximum(i * hb - 1, 0), 0)),
                  pl.BlockSpec((CONV_K, cw), lambda i: (0, 0)), _ANY],
        out_specs=[pl.BlockSpec((tt, cw), lambda i: (i, 0)), pl.BlockSpec((SUBLANE, cw), lambda i: (0, 0))],
        out_shape=[jax.ShapeDtypeStruct(dh.shape, F32), jax.ShapeDtypeStruct((SUBLANE, cw), F32)],
        input_output_aliases={5: 0},
        compiler_params=_cp("arbitrary"))(dc, dc, h, h, conv_w, dh)


def _swa_bwd(h, dm, sinks_b, dh, *, name, carry=None):
    t = h.shape[0]
    qspec, cur, prev = _swa_specs()
    c_ins, c_in_specs, c_out_specs, c_outs, c_scratch = _carry_specs(carry)

    def body(*refs):
        (q_ref, kc_ref, kp_ref, vc_ref, vp_ref, zb_ref, dy_ref, sk_ref, dh_in_ref,
         dqz_ref, dk_ref, dv_ref, dsk_ref) = _carried(carry, refs, 9, 4, t // BLOCK)
        n_blk = pl.program_id(0)

        @pl.when(n_blk == 0)
        def _():
            dk_ref[...] = jnp.zeros_like(dk_ref)
            dv_ref[...] = jnp.zeros_like(dv_ref)
            dsk_ref[...] = jnp.zeros_like(dsk_ref)

        kp, kc, vp, vc = kp_ref[...], kc_ref[...], vp_ref[...], vc_ref[...]
        scale = B_HEAD_DIM ** -0.5
        hks = range(B_KV_HEADS)
        ksl = lambda hk: slice(hk * B_HEAD_DIM, (hk + 1) * B_HEAD_DIM)
        upper, groups = _swa_group_probs(q_ref, sk_ref, kp, kc, vp, vc, n_blk)
        zbs = [_stack_heads(zb_ref, hk) for hk in hks]
        dys = [_stack_heads(dy_ref, hk) for hk in hks]
        gates = [_silu_and_grad(zbs[hk]) for hk in hks]
        dos = [dys[hk] * gates[hk][0] for hk in hks]
        deltas = [jnp.sum(dos[hk] * groups[hk][3], -1, keepdims=True) for hk in hks]
        dps = [jnp.where(upper, _dot_nt(dos[hk], vp[:, ksl(hk)]), _dot_nt(dos[hk], vc[:, ksl(hk)])) for hk in hks]
        dss = [groups[hk][1] * (dps[hk] - deltas[hk]) for hk in hks]
        ds_up = [jnp.where(upper, dss[hk], 0.0) for hk in hks]
        ds_lo = [dss[hk] - ds_up[hk] for hk in hks]
        p_up = [jnp.where(upper, groups[hk][1], 0.0) for hk in hks]
        p_lo = [groups[hk][1] - p_up[hk] for hk in hks]
        dqs = [(_dot(ds_up[hk], kp[:, ksl(hk)]) + _dot(ds_lo[hk], kc[:, ksl(hk)])) * scale for hk in hks]
        dk_prev = [_dot_tn(ds_up[hk], groups[hk][0]) for hk in hks]
        dk_cur = [_dot_tn(ds_lo[hk], groups[hk][0]) for hk in hks]
        dv_prev = [_dot_tn(p_up[hk], dos[hk]) for hk in hks]
        dv_cur = [_dot_tn(p_lo[hk], dos[hk]) for hk in hks]
        for hk in hks:
            dzb = dys[hk] * groups[hk][3] * gates[hk][1]
            dsink = groups[hk][2] * deltas[hk]
            for g in range(B_GROUP):
                hq = hk * B_GROUP + g
                rows = slice(g * BLOCK, (g + 1) * BLOCK)
                qsl = slice(hq * B_HEAD_DIM, (hq + 1) * B_HEAD_DIM)
                dqz_ref[:, qsl] = dqs[hk][rows]
                dqz_ref[:, B_WIDTH + hq * B_HEAD_DIM:B_WIDTH + (hq + 1) * B_HEAD_DIM] = dzb[rows]
                dsk_ref[hq:hq + 1, :] += -jnp.sum(dsink[rows], keepdims=True)
        at_cur = pl.ds(pl.multiple_of(n_blk * BLOCK, BLOCK), BLOCK)
        at_prev = pl.ds(pl.multiple_of(jnp.maximum(n_blk - 1, 0) * BLOCK, BLOCK), BLOCK)
        dk_ref[at_prev, :] += jnp.concatenate(dk_prev, axis=1)
        dv_ref[at_prev, :] += jnp.concatenate(dv_prev, axis=1)
        dk_ref[at_cur, :] += jnp.concatenate(dk_cur, axis=1)
        dv_ref[at_cur, :] += jnp.concatenate(dv_cur, axis=1)

    narrow = jax.ShapeDtypeStruct((t, B_KV_WIDTH), F32)
    res = lambda a, b: pl.BlockSpec((a, b), lambda i: (0, 0))
    outs = pl.pallas_call(
        body, name=name, grid=(t // BLOCK,),
        in_specs=[qspec(C_QB), cur(C_KB), prev(C_KB), cur(C_VB), prev(C_VB), qspec(C_ZB),
                  pl.BlockSpec((BLOCK, B_WIDTH), lambda i: (i, 1)), res(B_Q_HEADS, LANE), _ANY] + c_in_specs,
        out_specs=[pl.BlockSpec((BLOCK, 2 * B_WIDTH), lambda i: (i, C_QB // (2 * B_WIDTH))),
                   res(t, B_KV_WIDTH), res(t, B_KV_WIDTH), res(B_Q_HEADS, LANE)] + c_out_specs,
        out_shape=[jax.ShapeDtypeStruct(dh.shape, F32), narrow, narrow,
                   jax.ShapeDtypeStruct((B_Q_HEADS, LANE), F32)] + c_outs,
        scratch_shapes=c_scratch,
        input_output_aliases={8: 0},
        compiler_params=_cp("arbitrary"))(h, h, h, h, h, h, dm, sinks_b, dh, *c_ins)
    return outs[:4], outs[4:]


def _in_proj_dw(dh_main, dh_tail, x, *, tk, name):
    t, n = x.shape

    def body(a_ref, t_ref, x_ref, o_ref, ot_ref):
        @pl.when(pl.program_id(0) == 0)
        def _():
            o_ref[...] = jnp.zeros_like(o_ref)
            ot_ref[...] = jnp.zeros_like(ot_ref)

        xb = x_ref[...].astype(BF16)
        o_ref[...] += _dot_tn(a_ref[...], xb)
        ot_ref[...] += _dot_tn(t_ref[...], xb)

    row = lambda a: pl.BlockSpec((tk, a.shape[1]), lambda kk: (kk, 0))
    acc = lambda a: pl.BlockSpec((a.shape[1], n), lambda kk: (0, 0))
    return pl.pallas_call(
        body, name=name, grid=(t // tk,), in_specs=[row(dh_main), row(dh_tail), row(x)],
        out_specs=[acc(dh_main), acc(dh_tail)],
        out_shape=[jax.ShapeDtypeStruct((a.shape[1], n), F32) for a in (dh_main, dh_tail)],
        compiler_params=_cp("arbitrary"))(dh_main, dh_tail, x)


def _in_proj_dx(dh_main, dh_tail, wt, dr, *, tm, name, carry=None):
    t, n_main = dh_main.shape
    n_tail = dh_tail.shape[1]
    c_ins, c_in_specs, c_out_specs, c_outs, c_scratch = _carry_specs(carry)

    def body(*refs):
        a_ref, t_ref, wa_ref, wt_ref, r_ref, o_ref = _carried(carry, refs, 5, 1, t // tm)
        o_ref[...] = _dot(a_ref[...], wa_ref[...]) + _dot(t_ref[...], wt_ref[...]) + DEEPNORM_ALPHA * r_ref[...]

    row = lambda w: pl.BlockSpec((tm, w), lambda i: (i, 0))
    outs = pl.pallas_call(
        body, name=name, grid=(t // tm,),
        in_specs=[row(n_main), row(n_tail), pl.BlockSpec((n_main, D_MODEL), lambda i: (0, 0)),
                  pl.BlockSpec((n_tail, D_MODEL), lambda i: (n_main // n_tail, 0)), row(D_MODEL)] + c_in_specs,
        out_specs=[row(D_MODEL)] + c_out_specs,
        out_shape=[jax.ShapeDtypeStruct((t, D_MODEL), F32)] + c_outs,
        scratch_shapes=c_scratch,
        compiler_params=_cp("arbitrary"))(dh_main, dh_tail, wt, wt, dr, *c_ins)
    return outs[0], outs[1:]


def _layer_bwd(dxn, res, wt, conv_w, par, sinks_b, norm_w, w_out_bf, ln_g, l, carries=None, carry_dx=None):
    carries = carries or {}
    w_out_bf = res["w_out"]
    dr, dm, dw_out, dln_g, dln_b = _ln_out_bwd(dxn, res["r"], res["mixed"], ln_g, w_out_bf, tm=512, name=f"ln_out_bwd_{l}")
    h = res["h"]
    do, dh, dnw = _dn_post_bwd(dm, res["oa"], h, norm_w, tm=512, name=f"dn_post_bwd_{l}")
    dvn, ds_all = _dn_scan_bwd(res["q"], res["k"], res["w"], res["qk"], res["bg"], do, name=f"dn_scan_bwd_{l}")
    (dq, dk, dv, dbg, dbgt), got_chunk = _dn_chunk_bwd(
        res["q"], res["k"], res["v"], res["vn"], res["tmat"], res["qk"], res["bg"], res["bgt"], res["s_all"], ds_all,
        dvn, do, name=f"dn_chunk_bwd_{l}", carry=carries.get("dn_chunk"))
    dc, dbgi, dpar = _dn_pre_bwd(h, conv_w, par, dq, dk, dv, dbg, dbgt, tt=512, name=f"dn_pre_bwd_{l}")
    dh, dcw = _conv_bwd(dc, h, conv_w, dh, tt=512, name=f"conv_bwd_{l}")
    (dh, dkb, dvb, dsk), got_swa = _swa_bwd(h, dm, sinks_b, dh, name=f"swa_bwd_{l}", carry=carries.get("swa"))
    carried = dict(dn_chunk=got_chunk, swa=got_swa)
    dh_tail = jnp.concatenate([dkb, dvb, dbgi], axis=1)
    dwt_main, dwt_tail = _in_proj_dw(dh, dh_tail, res["x"], tk=512, name=f"in_proj_dw_{l}")
    grads = dict(w_in=(dwt_main, dwt_tail), conv_w=dcw[:CONV_K], a_log=dpar[0, A_HEADS:2 * A_HEADS],
                 dt_bias=dpar[1, A_HEADS:2 * A_HEADS], norm_w=dnw[0], sinks=dsk[:, 0], w_out=dw_out,
                 ln_g=dln_g[0], ln_b=dln_b[0])
    dx, carried_dx = _in_proj_dx(dh, dh_tail, wt, dr, tm=512, name=f"in_proj_dx_{l}",
                                 carry=None if carry_dx is None else carry_dx(grads))
    return dx, grads, carried, carried_dx


def _layer_args(wt, conv_w, a_log, dt_bias, sinks, norm_w, w_out_bf):
    return (wt, conv_w, _gate_params(a_log, dt_bias), jnp.broadcast_to(sinks[:, None], (B_Q_HEADS, LANE)),
            norm_w[None], w_out_bf)


def _local_step(x, target, args0, args1, ln_g, ln_b, gathers=None, reduce1=None, reduce0=None):
    assert DEPTH == 2
    x1, res0, got = _layer_fwd(x, *args0, ln_g[0][None], ln_b[0][None], 0, carries=gathers)
    if gathers is not None:
        args1 = args1(got)
    (dx, loss_tile), res1, _ = _layer_fwd(x1, *args1, ln_g[1][None], ln_b[1][None], 1, target=target)
    dx, grads1, _, _ = _layer_bwd(dx, res1, *args1, ln_g[1][None], 1)
    carries = None if reduce1 is None else reduce1(grads1)
    carry_dx = None if reduce0 is None else (lambda grads0: reduce0(grads0, grads1, loss_tile))
    dx, grads0, landed1, landed0 = _layer_bwd(dx, res0, *args0, ln_g[0][None], 0, carries=carries, carry_dx=carry_dx)
    return loss_tile, dx, [grads0, grads1], landed1, landed0


_ANY = pl.BlockSpec(memory_space=pl.ANY)
_MESH = pl.DeviceIdType.MESH


HALF = D_MODEL // 2


class _Exchange:
    def __init__(self, ins, outs, n_remote, n_local, plan):
        self.ins, self.outs, self.n_remote, self.n_local, self.plan = tuple(ins), tuple(outs), n_remote, n_local, plan

    def scratch(self):
        return [pltpu.SemaphoreType.DMA((self.n_remote,)), pltpu.SemaphoreType.DMA((self.n_remote,)),
                pltpu.SemaphoreType.DMA((max(self.n_local, 1),))]

    def _copies(self, in_refs, out_refs, sems, arriving):
        send_sems, recv_sems, local_sems = sems
        local, sends, recvs = self.plan(in_refs, out_refs)
        loc = [pltpu.make_async_copy(s, d, local_sems.at[i]) for i, (s, d) in enumerate(local)]
        rem = [pltpu.make_async_remote_copy(src_ref=s, dst_ref=recvs[i] if arriving else d, send_sem=send_sems.at[i],
                                            recv_sem=recv_sems.at[i], device_id=peer, device_id_type=_MESH)
               for i, (s, d, peer) in enumerate(sends)]
        return loc, rem

    def start(self, in_refs, out_refs, sems):
        loc, rem = self._copies(in_refs, out_refs, sems, arriving=False)
        for cp in loc + rem:
            cp.start()

    def finish(self, in_refs, out_refs, sems):
        loc, rem = self._copies(in_refs, out_refs, sems, arriving=True)
        for cp in rem:
            cp.wait_recv()
        for cp in rem:
            cp.wait_send()
        for cp in loc:
            cp.wait()


def _run_exchange(ex, *, name):
    n_in, n_out = len(ex.ins), len(ex.outs)

    def body(*refs):
        parts = refs[:n_in], refs[n_in:n_in + n_out], refs[n_in + n_out:]
        ex.start(*parts)
        ex.finish(*parts)

    return pl.pallas_call(body, name=name, in_specs=[_ANY] * n_in, out_specs=[_ANY] * n_out, out_shape=list(ex.outs),
                          scratch_shapes=ex.scratch())(*ex.ins)


def _place():
    x, y, c = lax.axis_index("x"), lax.axis_index("y"), lax.axis_index("c")
    return x, y, c, [(1 - x, y), (x, 1 - y), (1 - x, 1 - y)]


def _gather_exchange(arrays):
    n = len(arrays)

    def plan(src, dst):
        x, y, c, chips = _place()
        me = 2 * x + y
        local = [(src[k], dst[k].at[me]) for k in range(n)]
        sends = [(src[k], dst[k].at[me], (px, py, c)) for k in range(n) for px, py in chips]
        recvs = [dst[k].at[2 * px + py] for k in range(n) for px, py in chips]
        return local, sends, recvs

    return _Exchange(arrays, [jax.ShapeDtypeStruct((N_SHARD,) + a.shape, a.dtype) for a in arrays], 3 * n, n, plan)


def _gather_two_level(pack, conv_w, *, name):
    rows = pack.shape[0]
    part_rows = rows // 2

    def body(pack_ref, conv_ref, land_ref, conv_land_ref, send1, recv1, send2, recv2, csend, crecv, local_sems):
        x, y, c, chips = _place()
        me = 2 * x + y
        sibling = (x, y, 1 - c)
        part = lambda core: pl.ds(pl.multiple_of(core * part_rows, 16), part_rows)
        remote = lambda src, dst, ss, rs, to: pltpu.make_async_remote_copy(
            src_ref=src, dst_ref=dst, send_sem=ss, recv_sem=rs, device_id=to, device_id_type=_MESH)
        local = [pltpu.make_async_copy(pack_ref, land_ref.at[me], local_sems.at[0]),
                 pltpu.make_async_copy(conv_ref, conv_land_ref.at[me], local_sems.at[1])]
        for cp in local:
            cp.start()
        first = [remote(pack_ref.at[part(c)], land_ref.at[me, part(c)], send1.at[j], recv1.at[j], (px, py, c))
                 for j, (px, py) in enumerate(chips)]
        convs = [remote(conv_ref, conv_land_ref.at[me], csend.at[j], crecv.at[j], (px, py, c))
                 for j, (px, py) in enumerate(chips)]
        for cp in first + convs:
            cp.start()
        passed = []
        for j, (px, py) in enumerate(chips):
            slot = 2 * px + py
            remote(pack_ref.at[part(c)], land_ref.at[slot, part(c)], send1.at[j], recv1.at[j], (px, py, c)).wait_recv()
            cp = remote(land_ref.at[slot, part(c)], land_ref.at[slot, part(c)], send2.at[j], recv2.at[j], sibling)
            cp.start()
            passed.append(cp)
        for j, (px, py) in enumerate(chips):
            slot = 2 * px + py
            remote(land_ref.at[slot, part(1 - c)], land_ref.at[slot, part(1 - c)], send2.at[j], recv2.at[j],
                   sibling).wait_recv()
            remote(conv_ref, conv_land_ref.at[slot], csend.at[j], crecv.at[j], (px, py, c)).wait_recv()
        for cp in first + convs + passed:
            cp.wait_send()
        for cp in local:
            cp.wait()

    sems = [pltpu.SemaphoreType.DMA((3,))] * 6 + [pltpu.SemaphoreType.DMA((2,))]
    return pl.pallas_call(
        body, name=name, in_specs=[_ANY, _ANY], out_specs=[_ANY, _ANY],
        out_shape=[jax.ShapeDtypeStruct((N_SHARD,) + pack.shape, pack.dtype),
                   jax.ShapeDtypeStruct((N_SHARD,) + conv_w.shape, conv_w.dtype)],
        scratch_shapes=sems)(pack, conv_w)


def _half(core):
    return pl.ds(pl.multiple_of(core * HALF, HALF), HALF)


def _reduce_scatter_exchange(g, row0, rows):
    def plan(src, dst):
        x, y, c, chips = _place()
        peers = [(px, py, c if t == 0 else 1 - c) for px, py in chips for t in (0, 1)] + [(x, y, 1 - c)]
        sends = [(src[0].at[2 * px + py, pl.ds(row0, rows), _half(pc)], dst[0].at[k], (px, py, pc))
                 for k, (px, py, pc) in enumerate(peers)]
        return [], sends, [dst[0].at[k] for k in range(7)]

    return _Exchange([g], [jax.ShapeDtypeStruct((7, rows, HALF), g.dtype)], 7, 0, plan)


def _pair_window_exchange(g):
    def plan(src, dst):
        x, y, c, _ = _place()
        return [], [(src[0].at[:, :, _half(1 - c)], dst[0], (x, y, 1 - c))], [dst[0]]

    return _Exchange([g], [jax.ShapeDtypeStruct(g.shape[:2] + (HALF,), g.dtype)], 1, 0, plan)


def _chip_scatter_exchange(p, small):
    def plan(src, dst):
        x, y, c, chips = _place()
        mine = 4 * x + 2 * y + c
        peers = [(px, py, c if t == 0 else 1 - c) for px, py in chips for t in (0, 1)] + [(x, y, 1 - c)]
        sends = [(src[0].at[2 * px + py], dst[0].at[j], (px, py, c)) for j, (px, py) in enumerate(chips)]
        recvs = [dst[0].at[j] for j in range(3)]
        sends += [(src[1], dst[1].at[mine], peer) for peer in peers]
        recvs += [dst[1].at[4 * px + 2 * py + pc] for px, py, pc in peers]
        return [(src[1], dst[1].at[mine])], sends, recvs

    outs = [jax.ShapeDtypeStruct((3,) + p.shape[1:], p.dtype), jax.ShapeDtypeStruct((8,) + small.shape, small.dtype)]
    return _Exchange([p, small], outs, 10, 1, plan)


def _share_exchange(arrays):
    n = len(arrays)

    def plan(src, dst):
        x, y, c, _ = _place()
        return [], [(src[k], dst[k], (x, y, 1 - c)) for k in range(n)], [dst[k] for k in range(n)]

    return _Exchange(arrays, [jax.ShapeDtypeStruct(a.shape, a.dtype) for a in arrays], n, 0, plan)


def _sum_scatter(g, lands, me, core, *, tc, name):
    rows = g.shape[1]
    per = HALF // tc
    n = len(lands)

    def body(*refs):
        g_ref, land_refs, o_ref = refs[1], refs[2:2 + n], refs[2 + n]
        at = 0
        for land_ref in land_refs:
            run = slice(at, at + land_ref.shape[1])
            acc = g_ref[run, :].astype(F32)
            for k in range(7):
                acc = acc + land_ref[k].astype(F32)
            o_ref[run, :] = acc
            at = run.stop

    return pl.pallas_call(
        body, name=name, out_shape=jax.ShapeDtypeStruct((rows, HALF), F32), compiler_params=_cp("parallel"),
        grid_spec=pltpu.PrefetchScalarGridSpec(
            num_scalar_prefetch=1, grid=(per,),
            in_specs=[pl.BlockSpec((None, rows, tc), lambda i, w: (w[0], 0, w[1] * per + i))]
            + [pl.BlockSpec((7, a.shape[1], tc), lambda i, w: (0, 0, i)) for a in lands],
            out_specs=pl.BlockSpec((rows, tc), lambda i, w: (0, i))))(
        jnp.stack([me, core]).astype(jnp.int32), g, *lands)


def _pair_add(g, land, core, *, name):
    n, rows, _ = g.shape

    def body(core_ref, g_ref, land_ref, o_ref):
        o_ref[...] = (g_ref[...].astype(F32) + land_ref[...].astype(F32)).astype(o_ref.dtype)

    blk = pl.BlockSpec((1, rows, HALF), lambda i, w: (i, 0, 0))
    return pl.pallas_call(
        body, name=name, out_shape=jax.ShapeDtypeStruct((n, rows, HALF), g.dtype), compiler_params=_cp("parallel"),
        grid_spec=pltpu.PrefetchScalarGridSpec(
            num_scalar_prefetch=1, grid=(n,),
            in_specs=[pl.BlockSpec((1, rows, HALF), lambda i, w: (i, 0, w[0])), blk], out_specs=blk))(
        jnp.reshape(core, (1,)).astype(jnp.int32), g, land)


def _sum_chips(p, land, me, *, tc, name):
    rows = p.shape[1]

    def body(me_ref, p_ref, land_ref, o_ref):
        acc = p_ref[...].astype(F32)
        for k in range(3):
            acc = acc + land_ref[k].astype(F32)
        o_ref[...] = acc

    return pl.pallas_call(
        body, name=name, out_shape=jax.ShapeDtypeStruct((rows, HALF), F32), compiler_params=_cp("parallel"),
        grid_spec=pltpu.PrefetchScalarGridSpec(
            num_scalar_prefetch=1, grid=(HALF // tc,),
            in_specs=[pl.BlockSpec((None, rows, tc), lambda i, w: (w[0], 0, i)),
                      pl.BlockSpec((3, rows, tc), lambda i, w: (0, 0, i))],
            out_specs=pl.BlockSpec((rows, tc), lambda i, w: (0, i))))(
        jnp.reshape(me, (1,)).astype(jnp.int32), p, land)


def _sum_slots(a, *, name):
    n = a.shape[0]

    def body(a_ref, o_ref):
        acc = a_ref[0]
        for k in range(1, n):
            acc = acc + a_ref[k]
        o_ref[...] = acc

    return pl.pallas_call(body, name=name, out_shape=jax.ShapeDtypeStruct(a.shape[1:], a.dtype))(a)


def _elementwise(fn, ins, n_out, block, *, name):
    shape = ins[0].shape
    grid = tuple(s // b for s, b in zip(shape, block))
    n_in = len(ins)

    def body(*refs):
        outs = fn(*[r[...] for r in refs[:n_in]])
        for o_ref, val in zip(refs[n_in:], outs):
            o_ref[...] = val

    spec = pl.BlockSpec(block, lambda i, j, k: (i, j, k))
    return pl.pallas_call(body, name=name, grid=grid, in_specs=[spec] * n_in, out_specs=[spec] * n_out,
                          out_shape=[jax.ShapeDtypeStruct(shape, F32)] * n_out,
                          compiler_params=_cp(*["parallel"] * 3))(*ins)


def _adamw_math(w, g, m, v):
    mn = ADAM_B1 * m + (1.0 - ADAM_B1) * g
    vn = ADAM_B2 * v + (1.0 - ADAM_B2) * (g * g)
    m_hat = mn / (1.0 - ADAM_B1 ** ADAM_STEP)
    v_hat = vn / (1.0 - ADAM_B2 ** ADAM_STEP)
    return -ADAM_LR * (m_hat / (jnp.sqrt(v_hat) + ADAM_EPS) + ADAM_WD * w), mn, vn


def _adamw(w, g, m, v, block, *, name):
    return _elementwise(_adamw_math, [w, g, m, v], 3, block, name=name)


def _interleave_layers(layers, *, tc, name):
    rows, cols = layers[0].shape
    n = len(layers)

    def body(*refs):
        for l in range(n):
            refs[n][:, l, :] = refs[l][...]

    return pl.pallas_call(body, name=name, grid=(cols // tc,),
                          in_specs=[pl.BlockSpec((rows, tc), lambda i: (0, i))] * n,
                          out_specs=pl.BlockSpec((rows, n, tc), lambda i: (0, 0, i)),
                          out_shape=jax.ShapeDtypeStruct((rows, n, cols), layers[0].dtype),
                          compiler_params=_cp("parallel"))(*layers)


def _adamw_small(ws, gs, ms, vs, *, name):
    n = len(ws)

    def body(*refs):
        w, g, m, v, outs = refs[:n], refs[n:2 * n], refs[2 * n:3 * n], refs[3 * n:4 * n], refs[4 * n:]
        for k in range(n):
            for slot, val in enumerate(_adamw_math(w[k][...], g[k][...], m[k][...], v[k][...])):
                outs[slot * n + k][...] = val

    outs = pl.pallas_call(body, name=name, out_shape=[jax.ShapeDtypeStruct(a.shape, F32) for a in ws] * 3)(
        *ws, *gs, *ms, *vs)
    return outs[:n], outs[n:2 * n], outs[2 * n:]


def _to_kernel_order(wt):
    gates = jnp.pad(wt[2048:2056], ((0, LANE - 2 * A_HEADS), (0, 0)))
    return jnp.concatenate([wt[0:2048], wt[2056:2568], wt[2824:3336], wt[2568:2696], wt[2696:2824], gates], axis=0)


def _from_kernel_order(main, tail):
    return jnp.concatenate([main[0:2048], tail[C_BG - DH_MAIN:C_BG - DH_MAIN + 2 * A_HEADS],
                            main[C_QB:C_QB + B_WIDTH], tail[0:B_KV_WIDTH], tail[B_KV_WIDTH:2 * B_KV_WIDTH],
                            main[C_ZB:C_ZB + B_WIDTH]], axis=0)


def _gate_params(a_log, dt_bias):
    return jnp.pad(jnp.stack([a_log, dt_bias]), ((0, SUBLANE - 2), (A_HEADS, LANE - 2 * A_HEADS)))


SMALL = ("conv_w", "a_log", "dt_bias", "norm_w", "sinks", "ln_g", "ln_b")


def _pack(parts, cols):
    flat = jnp.concatenate([p.reshape(-1) for p in parts])
    rows = -(-flat.shape[0] // cols)
    return jnp.pad(flat, (0, rows * cols - flat.shape[0])).reshape(rows, cols)


def _unpack(packed, shapes):
    flat = packed.reshape(-1)
    out, at = [], 0
    for s in shapes:
        n = math.prod(s)
        out.append(flat[at:at + n].reshape(s))
        at += n
    return out


def kernel(x, w_in, conv_w, a_log, dt_bias, norm_w, sinks, w_out, ln_g, ln_b, loss_target, m_w_in, m_conv_w, m_a_log, m_dt_bias, m_norm_w, m_sinks, m_w_out, m_ln_g, m_ln_b, v_w_in, v_conv_w, v_a_log, v_dt_bias, v_norm_w, v_sinks, v_w_out, v_ln_g, v_ln_b):
    xi, yi, ci = lax.axis_index("x"), lax.axis_index("y"), lax.axis_index("c")
    me = 2 * xi + yi

    to_t = lambda a: jnp.transpose(a, (2, 0, 1))
    from_t = lambda a: jnp.transpose(a, (1, 2, 0))

    wt_shard = to_t(w_in)

    def pack_weights(l):
        rows = jnp.pad(wt_shard[:, l], ((0, IN_PAD - IN_SHARD), (0, 0)))
        return jnp.concatenate([rows, w_out[l]], axis=0).astype(BF16)

    pack0, pack1 = pack_weights(0), pack_weights(1)
    got_in0, g_conv = _gather_two_level(pack0[:IN_PAD], conv_w, name="gather_weights_0")
    conv_full = jnp.moveaxis(g_conv, 0, 2).reshape(DEPTH, CONV_K, 3 * A_WIDTH)
    carriers = ("dn_pre", "dn_wy", "dn_scan")
    cuts = (0, 288, 624, IN_PAD)
    gathers = {nm: _gather_exchange([pack1[cuts[i]:cuts[i + 1]]]) for i, nm in enumerate(carriers)}
    gathers.update(in_proj=_gather_exchange([pack0[IN_PAD:]]), swa=_gather_exchange([pack1[IN_PAD:]]))
    w_in_of = lambda rows: _to_kernel_order(rows[:, :IN_SHARD].reshape(IN_COLS, D_MODEL))
    w_out_of = lambda rows: rows.reshape(D_MODEL, D_MODEL)
    args0 = _layer_args(w_in_of(got_in0), conv_full[0], a_log[0], dt_bias[0], sinks[0], norm_w[0],
                        lambda got: w_out_of(got[0]))

    def args1(got):
        rows = jnp.concatenate([got[nm][0] for nm in carriers], axis=1)
        return _layer_args(w_in_of(rows), conv_full[1], a_log[1], dt_bias[1], sinks[1], norm_w[1],
                           w_out_of(got["swa"][0]))

    def pack_grads(g):
        gin = _from_kernel_order(*g["w_in"]).reshape(N_SHARD, IN_SHARD, D_MODEL)
        gin = jnp.pad(gin, ((0, 0), (0, IN_PAD - IN_SHARD), (0, 0)))
        return jnp.concatenate([gin, g["w_out"].reshape(N_SHARD, OUT_SHARD, D_MODEL)], axis=1).astype(BF16)

    packed = {}

    def reduce1(grads1):
        packed[1] = pack_grads(grads1)
        half_rows = packed[1].shape[1] // 2
        return dict(dn_chunk=_reduce_scatter_exchange(packed[1], 0, half_rows),
                    swa=_reduce_scatter_exchange(packed[1], half_rows, half_rows))

    def reduce0(grads0, grads1, loss_tile):
        g0 = pack_grads(grads0)
        from_sibling = _run_exchange(_pair_window_exchange(g0), name="pair_reduce_0")[0]
        packed[0] = _pair_add(g0, from_sibling, ci, name="pair_add_0")
        gsmall = _pack([jnp.stack([g[nm] for g in (grads0, grads1)]) for nm in SMALL] + [loss_tile[0, 0:1]], D_MODEL)
        return _chip_scatter_exchange(packed[0], gsmall)

    _, dx, grads, landed1, (landed0, landed_small) = _local_step(
        x[0], loss_target[0], args0, args1, ln_g, ln_b, gathers=gathers, reduce1=reduce1, reduce0=reduce0)

    small_shapes = [(DEPTH,) + grads[0][nm].shape for nm in SMALL]
    halves = [_sum_chips(packed[0], landed0, me, tc=2 * LANE, name="reduce_sum_0"),
              _sum_scatter(packed[1], [landed1["dn_chunk"][0], landed1["swa"][0]], me, ci, tc=2 * LANE,
                           name="reduce_sum_1")]
    s_small = _sum_slots(landed_small, name="reduce_sum_small")
    others = _run_exchange(_share_exchange(halves), name="pair_share")
    full = [jnp.where(ci == 0, jnp.concatenate([mine, other], axis=1), jnp.concatenate([other, mine], axis=1))
            for mine, other in zip(halves, others)]
    grad_in_layers = [f[:IN_SHARD] for f in full]
    grad_out = jnp.stack([f[IN_PAD:] for f in full])
    out_blk = (1, OUT_SHARD, D_MODEL)
    *small_grads, loss = _unpack(s_small, small_shapes + [()])
    gs = dict(zip(SMALL, small_grads))
    gs["conv_w"] = lax.dynamic_slice_in_dim(gs["conv_w"], me * CONV_SHARD, CONV_SHARD, axis=2)

    grad_in_t = _interleave_layers(grad_in_layers, tc=2 * LANE, name="grad_in_layers")
    d_in, nm_in, nv_in = (from_t(o) for o in _adamw(to_t(w_in), grad_in_t, to_t(m_w_in), to_t(v_w_in),
                                                    (IN_SHARD // 6, DEPTH, D_MODEL), name="adamw_in"))
    grad_in = from_t(grad_in_t)
    d_out, nm_out, nv_out = _adamw(w_out, grad_out, m_w_out, v_w_out, out_blk, name="adamw_out")
    ws = dict(conv_w=conv_w, a_log=a_log, dt_bias=dt_bias, norm_w=norm_w, sinks=sinks, ln_g=ln_g, ln_b=ln_b)
    ms = dict(conv_w=m_conv_w, a_log=m_a_log, dt_bias=m_dt_bias, norm_w=m_norm_w, sinks=m_sinks, ln_g=m_ln_g, ln_b=m_ln_b)
    vs = dict(conv_w=v_conv_w, a_log=v_a_log, dt_bias=v_dt_bias, norm_w=v_norm_w, sinks=v_sinks, ln_g=v_ln_g, ln_b=v_ln_b)
    d_s, nm_s, nv_s = (dict(zip(SMALL, o)) for o in _adamw_small(*[[d[nm] for nm in SMALL] for d in (ws, gs, ms, vs)],
                                                                 name="adamw_small"))

    def in_order(big_in, small, big_out):
        return (big_in, small["conv_w"], small["a_log"], small["dt_bias"], small["norm_w"], small["sinks"], big_out,
                small["ln_g"], small["ln_b"])

    return (loss, dx[None], *in_order(grad_in, gs, grad_out), *in_order(d_in, d_s, d_out),
            *in_order(nm_in, nm_s, nm_out), *in_order(nv_in, nv_s, nv_out))
```

```python
import math

import jax
import jax.numpy as jnp
from jax import lax
from jax.experimental import pallas as pl
from jax.experimental.pallas import tpu as pltpu

F32 = jnp.float32
BF16 = jnp.bfloat16
HI = lax.Precision.HIGHEST

D_MODEL = 1024
DEPTH = 2
A_HEADS = 4
A_HEAD_DIM = 128
A_WIDTH = 512
CONV_K = 4
CHUNK = 64
B_Q_HEADS = 8
B_KV_HEADS = 2
B_HEAD_DIM = 64
B_GROUP = 4
B_WIDTH = 512
B_KV_WIDTH = 128
BLOCK = 128
IN_COLS = 3336
DEEPNORM_ALPHA = (2 * DEPTH) ** 0.25
LN_EPS = 1e-5
RMS_EPS = 1e-6
L2_EPS = 1e-6
ADAM_LR = 0.001
ADAM_B1 = 0.9
ADAM_B2 = 0.999
ADAM_EPS = 1e-08
ADAM_WD = 0.01
ADAM_STEP = 10

N_SHARD = 4
IN_SHARD = IN_COLS // N_SHARD
OUT_SHARD = D_MODEL // N_SHARD
CONV_SHARD = 3 * A_WIDTH // N_SHARD
IN_PAD = -(-IN_SHARD // 96) * 96

P_COLS = 3456
C_PRE = 0
C_ZA = 1536
C_QB = 2048
C_ZB = 2560
C_KB = 3072
C_VB = 3200
C_BG = 3328
DH_MAIN = C_KB
LANE = 128
SUBLANE = 8
HALO = 16
VMEM_LIMIT = 56 * 1024 * 1024
ALIBI = tuple(2.0 ** (-8.0 * (h + 1) / B_Q_HEADS) for h in range(B_Q_HEADS))
NEG = -1e30


def _cp(*sem):
    return pltpu.CompilerParams(dimension_semantics=sem, vmem_limit_bytes=VMEM_LIMIT)


def _dot(a, b):
    return jnp.dot(a.astype(BF16), b.astype(BF16), preferred_element_type=F32)


def _dot_nt(a, b):
    return lax.dot_general(a.astype(BF16), b.astype(BF16), (((1,), (1,)), ((), ())),
                           preferred_element_type=F32)


def _dot_tn(a, b):
    return lax.dot_general(a.astype(BF16), b.astype(BF16), (((0,), (0,)), ((), ())),
                           preferred_element_type=F32)


def _dot_hi(a, b):
    return jnp.dot(a, b, precision=HI, preferred_element_type=F32)


def _sigmoid(x):
    return jax.nn.sigmoid(x)


def _silu(x):
    return x * _sigmoid(x)


def _silu_and_grad(x):
    s = _sigmoid(x)
    return x * s, s * (1.0 + x * (1.0 - s))


def _softplus(x):
    return jnp.maximum(x, 0.0) + jnp.log(1.0 + jnp.exp(-jnp.abs(x)))


def _shift_down(cur, before, s):
    if s == 0:
        return cur
    r = pltpu.roll(cur, s, 0)
    rb = pltpu.roll(before, s, 0)
    row = lax.broadcasted_iota(jnp.int32, before.shape, 0)
    head = jnp.where(row < s, rb, r[0:SUBLANE])
    return jnp.concatenate([head, r[SUBLANE:]], axis=0)


def _shift_up(cur, after, s):
    if s == 0:
        return cur
    n = cur.shape[0]
    r = pltpu.roll(cur, n - s, 0)
    ra = pltpu.roll(after, SUBLANE - s, 0)
    row = lax.broadcasted_iota(jnp.int32, after.shape, 0)
    tail = jnp.where(row >= SUBLANE - s, ra, r[n - SUBLANE:])
    return jnp.concatenate([r[:n - SUBLANE], tail], axis=0)


def _conv_fwd(cur, before, w):
    acc = cur * w[CONV_K - 1:CONV_K, :]
    for s in range(1, CONV_K):
        acc = acc + _shift_down(cur, before, s) * w[CONV_K - 1 - s:CONV_K - s, :]
    return acc


def _matmul_nt(a, bt, *, tm, name, carry=None):
    m, k = a.shape
    n = bt.shape[0]
    c_ins, c_in_specs, c_out_specs, c_outs, c_scratch = _carry_specs(carry)

    def body(*refs):
        a_ref, b_ref, o_ref = _carried(carry, refs, 2, 1, m // tm)
        o_ref[...] = _dot_nt(a_ref[...], b_ref[...]).astype(o_ref.dtype)

    outs = pl.pallas_call(
        body, name=name, grid=(m // tm,),
        in_specs=[pl.BlockSpec((tm, k), lambda i: (i, 0)), pl.BlockSpec((n, k), lambda i: (0, 0))] + c_in_specs,
        out_specs=[pl.BlockSpec((tm, n), lambda i: (i, 0))] + c_out_specs,
        out_shape=[jax.ShapeDtypeStruct((m, n), BF16)] + c_outs,
        scratch_shapes=c_scratch,
        compiler_params=_cp("arbitrary"))(a, bt, *c_ins)
    return outs[0], outs[1:]


def _dn_pre(h, conv_w, par, *, tt, name, carry=None):
    t = h.shape[0]
    cw = 3 * A_WIDTH
    hb = tt // HALO

    c_ins, c_in_specs, c_out_specs, c_outs, c_scratch = _carry_specs(carry)

    def body(*refs):
        (pre_ref, halo_ref, bgi_ref, cw_ref, par_ref,
         q_ref, k_ref, v_ref, bg_ref, bgt_ref) = _carried(carry, refs, 5, 5, t // tt)
        i = pl.program_id(0)
        cur = pre_ref[...].astype(F32)
        before = jnp.where(i > 0, halo_ref[...].astype(F32)[HALO - SUBLANE:], 0.0)
        s = _silu(_conv_fwd(cur, before, cw_ref[...]))
        for hd in range(A_HEADS):
            sl = slice(hd * LANE, (hd + 1) * LANE)
            tq = s[:, hd * LANE:(hd + 1) * LANE]
            q_ref[:, sl] = tq * (lax.rsqrt(jnp.sum(tq * tq, -1, keepdims=True) + L2_EPS) * (A_HEAD_DIM ** -0.5))
            tk = s[:, A_WIDTH + hd * LANE:A_WIDTH + (hd + 1) * LANE]
            k_ref[:, sl] = tk * lax.rsqrt(jnp.sum(tk * tk, -1, keepdims=True) + L2_EPS)
        v_ref[...] = s[:, 2 * A_WIDTH:]
        raw = bgi_ref[...].astype(F32)
        lane = lax.broadcasted_iota(jnp.int32, raw.shape, 1)
        is_a = (lane >= A_HEADS) & (lane < 2 * A_HEADS)
        g = jnp.where(is_a, -jnp.exp(par_ref[0:1, :]) * _softplus(raw + par_ref[1:2, :]), 0.0)
        gc = _dot_hi(_chunk_tri(tt, lower=True), g)
        bg = jnp.where(lane < A_HEADS, _sigmoid(raw), gc)
        bg_ref[...] = bg
        bgt_ref[...] = jnp.transpose(bg)[0:SUBLANE, :]

    wide = jax.ShapeDtypeStruct((t, A_WIDTH), F32)
    outs = pl.pallas_call(
        body, name=name, grid=(t // tt,),
        in_specs=[pl.BlockSpec((tt, cw), lambda i: (i, 0)),
                  pl.BlockSpec((HALO, cw), lambda i: (jnp.maximum(i * hb - 1, 0), 0)),
                  pl.BlockSpec((tt, LANE), lambda i: (i, C_BG // LANE)),
                  pl.BlockSpec((CONV_K, cw), lambda i: (0, 0)),
                  pl.BlockSpec((SUBLANE, LANE), lambda i: (0, 0))] + c_in_specs,
        out_specs=[pl.BlockSpec((tt, A_WIDTH), lambda i: (i, 0))] * 3
        + [pl.BlockSpec((tt, LANE), lambda i: (i, 0)), pl.BlockSpec((SUBLANE, tt), lambda i: (0, i))] + c_out_specs,
        out_shape=[wide, wide, wide, jax.ShapeDtypeStruct((t, LANE), F32),
                   jax.ShapeDtypeStruct((SUBLANE, t), F32)] + c_outs,
        scratch_shapes=c_scratch,
        compiler_params=_cp("arbitrary"))(h, h, h, conv_w, par, *c_ins)
    return outs[:5], outs[5:]


def _chunk_tri(n, lower):
    r = lax.broadcasted_iota(jnp.int32, (n, n), 0)
    c = lax.broadcasted_iota(jnp.int32, (n, n), 1)
    shift = CHUNK.bit_length() - 1
    same = jnp.right_shift(r, shift) == jnp.right_shift(c, shift)
    return (same & ((c <= r) if lower else (c >= r))).astype(F32)


def _chunk_masks():
    r = lax.broadcasted_iota(jnp.int32, (CHUNK, CHUNK), 0)
    c = lax.broadcasted_iota(jnp.int32, (CHUNK, CHUNK), 1)
    return r >= c, r > c, r == c


def _split(a):
    hi = a.astype(BF16)
    return hi, (a - hi.astype(F32)).astype(BF16)


def _dot3(a, b):
    (ah, al), (bh, bl) = a, b
    d = lambda p, q: jnp.dot(p, q, preferred_element_type=F32)
    return d(ah, bh) + (d(ah, bl) + d(al, bh))


def _tri_inv_many(a_list, eye):
    d = lambda p, q: jnp.dot(p.astype(BF16), q.astype(BF16), preferred_element_type=F32)
    r = lax.broadcasted_iota(jnp.int32, (CHUNK, CHUNK), 0)
    c = lax.broadcasted_iota(jnp.int32, (CHUNK, CHUNK), 1)
    same = lambda b: jnp.right_shift(r, b.bit_length() - 1) == jnp.right_shift(c, b.bit_length() - 1)
    x = [jnp.where(same(8), -a, 0.0) for a in a_list]
    tm = [eye + xi for xi in x]
    for _ in range(2):
        x = [d(xi, xi) for xi in x]
        tm = [t + d(t, xi) for t, xi in zip(tm, x)]
    for b in (16, 32, 64):
        low = [jnp.where(same(b) & ~same(b // 2), a, 0.0) for a in a_list]
        tm = [t - d(t, d(lo, t)) for t, lo in zip(tm, low)]
    res = [eye - _dot3(_split(eye + a), _split(t)) for a, t in zip(a_list, tm)]
    return [t + d(t, rs) for t, rs in zip(tm, res)]


def _chunk_gates(bg_v, bgt_v, hd):
    return (bg_v[:, hd:hd + 1], bg_v[:, A_HEADS + hd:A_HEADS + hd + 1],
            None if bgt_v is None else bgt_v[A_HEADS + hd:A_HEADS + hd + 1, :])


WY_ROWS = 512
SCAN_ROWS = 512
WY_GROUP = 8


def _dn_wy(q, k, v, bg, bgt, *, name, carry=None):
    t = q.shape[0]
    rows = WY_ROWS

    c_ins, c_in_specs, c_out_specs, c_outs, c_scratch = _carry_specs(carry)

    def body(*refs):
        q_ref, k_ref, v_ref, bg_ref, bgt_ref, u_ref, w_ref, tm_ref, qk_ref = _carried(carry, refs, 5, 4, t // rows)
        causal, strict, diag = _chunk_masks()
        eye = diag.astype(F32)
        for c0 in range(0, rows // CHUNK, WY_GROUP):
            items = [(c, hd) for c in range(c0, c0 + WY_GROUP) for hd in range(A_HEADS)]
            rs = lambda c: slice(c * CHUNK, (c + 1) * CHUNK)
            sl = lambda hd: slice(hd * LANE, (hd + 1) * LANE)
            hs = lambda hd: slice(hd * CHUNK, (hd + 1) * CHUNK)
            gates = [_chunk_gates(bg_ref[rs(c), :], bgt_ref[:, rs(c)], hd) for c, hd in items]
            dms = [jnp.exp(jnp.where(causal, gcol - grow, NEG)) for _, gcol, grow in gates]
            kbs = [k_ref[rs(c), sl(hd)] * g[0] for (c, hd), g in zip(items, gates)]
            a_list = [jnp.where(strict, _dot_nt(kb, k_ref[rs(c), sl(hd)]) * dm, 0.0)
                      for (c, hd), kb, dm in zip(items, kbs, dms)]
            for (c, hd), dm in zip(items, dms):
                qk_ref[rs(c), hs(hd)] = jnp.where(
                    causal, _dot_nt(q_ref[rs(c), sl(hd)], k_ref[rs(c), sl(hd)]) * dm, 0.0)
            tms = _tri_inv_many(a_list, eye)
            for (c, hd), g, kb, tmat in zip(items, gates, kbs, tms):
                tm_ref[rs(c), hs(hd)] = tmat
                u_ref[rs(c), sl(hd)] = _dot(tmat, v_ref[rs(c), sl(hd)] * g[0])
                w_ref[rs(c), sl(hd)] = _dot(tmat, kb * jnp.exp(g[1])).astype(BF16)

    blk = pl.BlockSpec((rows, A_WIDTH), lambda i: (i, 0))
    half = pl.BlockSpec((rows, A_HEADS * CHUNK), lambda i: (i, 0))
    outs = pl.pallas_call(
        body, name=name, grid=(t // rows,),
        in_specs=[blk, blk, blk, pl.BlockSpec((rows, LANE), lambda i: (i, 0)),
                  pl.BlockSpec((SUBLANE, rows), lambda i: (0, i))] + c_in_specs,
        out_specs=[blk, blk, half, half] + c_out_specs,
        out_shape=[jax.ShapeDtypeStruct((t, A_WIDTH), F32), jax.ShapeDtypeStruct((t, A_WIDTH), BF16),
                   jax.ShapeDtypeStruct((t, A_HEADS * CHUNK), F32),
                   jax.ShapeDtypeStruct((t, A_HEADS * CHUNK), F32)] + c_outs,
        scratch_shapes=c_scratch,
        compiler_params=_cp("arbitrary"))(q, k, v, bg, bgt, *c_ins)
    return outs[:4], outs[4:]


def _dn_scan_fwd(q, k, u, w, qk, bg, *, name, carry=None):
    t = q.shape[0]
    rows = SCAN_ROWS
    per = rows // CHUNK
    c_ins, c_in_specs, c_out_specs, c_outs, c_scratch = _carry_specs(carry)

    def body(*refs):
        q_ref, k_ref, u_ref, w_ref, qk_ref, bg_ref, o_ref, vn_ref, s_ref, state = _carried(carry, refs, 6, 3, t // rows)

        @pl.when(pl.program_id(0) == 0)
        def _():
            state[...] = jnp.zeros_like(state)

        heads = range(A_HEADS)
        sl = lambda hd: slice(hd * LANE, (hd + 1) * LANE)
        s_cur = [state[hd] for hd in heads]
        for c in range(per):
            rs = slice(c * CHUNK, (c + 1) * CHUNK)
            bg_v = bg_ref[rs, :]
            gcols = [_chunk_gates(bg_v, None, hd)[1] for hd in heads]
            glasts = [gc[CHUNK - 1:CHUNK, :] for gc in gcols]
            for hd in heads:
                s_ref[c, hd] = s_cur[hd].astype(BF16)
            vns = [u_ref[rs, sl(hd)] - _dot(w_ref[rs, sl(hd)], s_cur[hd]) for hd in heads]
            qss = [_dot(q_ref[rs, sl(hd)] * jnp.exp(gcols[hd]), s_cur[hd]) for hd in heads]
            s_cur = [s_cur[hd] * jnp.exp(glasts[hd])
                     + _dot_tn(k_ref[rs, sl(hd)] * jnp.exp(glasts[hd] - gcols[hd]), vns[hd]) for hd in heads]
            for hd in heads:
                vn_ref[rs, sl(hd)] = vns[hd]
                o_ref[rs, sl(hd)] = qss[hd] + _dot(qk_ref[rs, hd * CHUNK:(hd + 1) * CHUNK], vns[hd])
        for hd in heads:
            state[hd] = s_cur[hd]

    blk = pl.BlockSpec((rows, A_WIDTH), lambda i: (i, 0))
    half = pl.BlockSpec((rows, A_HEADS * CHUNK), lambda i: (i, 0))
    wide = jax.ShapeDtypeStruct((t, A_WIDTH), F32)
    outs = pl.pallas_call(
        body, name=name, grid=(t // rows,),
        in_specs=[blk, blk, blk, blk, half, pl.BlockSpec((rows, LANE), lambda i: (i, 0))] + c_in_specs,
        out_specs=[blk, blk, pl.BlockSpec((per, A_HEADS, LANE, LANE), lambda i: (i, 0, 0, 0))] + c_out_specs,
        out_shape=[wide, wide, jax.ShapeDtypeStruct((t // CHUNK, A_HEADS, LANE, LANE), BF16)] + c_outs,
        scratch_shapes=[pltpu.VMEM((A_HEADS, LANE, LANE), F32)] + c_scratch,
        compiler_params=_cp("arbitrary"))(q, k, u, w, qk, bg, *c_ins)
    return outs[:3], outs[3:]


def _stack_heads(ref, hk):
    return jnp.concatenate([ref[:, h * B_HEAD_DIM:(h + 1) * B_HEAD_DIM].astype(F32)
                            for h in range(hk * B_GROUP, (hk + 1) * B_GROUP)], axis=0)


def _swa_window():
    qi = lax.broadcasted_iota(jnp.int32, (BLOCK, BLOCK), 0)
    kj = lax.broadcasted_iota(jnp.int32, (BLOCK, BLOCK), 1)
    dist = jnp.where(kj > qi, qi + BLOCK - kj, qi - kj).astype(F32)
    rows = lax.broadcasted_iota(jnp.int32, (B_GROUP * BLOCK, BLOCK), 0)
    cols = lax.broadcasted_iota(jnp.int32, (B_GROUP * BLOCK, BLOCK), 1)
    return cols > jnp.bitwise_and(rows, BLOCK - 1), dist


def _swa_group_probs(q_ref, sk_ref, kp, kc, vp, vc, n_blk):
    hks = range(B_KV_HEADS)
    heads = lambda hk: range(hk * B_GROUP, (hk + 1) * B_GROUP)
    ksl = lambda hk: slice(hk * B_HEAD_DIM, (hk + 1) * B_HEAD_DIM)
    upper, dist = _swa_window()
    no_prev = jnp.where(n_blk > 0, 0.0, NEG)
    ones = jnp.ones((BLOCK, B_HEAD_DIM), BF16)
    with_ones = lambda v, hk: jnp.concatenate([v[:, ksl(hk)].astype(BF16), ones], axis=1)
    qs = [_stack_heads(q_ref, hk) * (B_HEAD_DIM ** -0.5) for hk in hks]
    sink = [jnp.concatenate([jnp.broadcast_to(sk_ref[h:h + 1, 0:1], (BLOCK, 1)) for h in heads(hk)], axis=0)
            for hk in hks]
    s = [jnp.where(upper, _dot_nt(qs[hk], kp[:, ksl(hk)]) + no_prev, _dot_nt(qs[hk], kc[:, ksl(hk)]))
         - jnp.concatenate([ALIBI[h] * dist for h in heads(hk)], axis=0) for hk in hks]
    m = [jnp.maximum(jnp.max(s[hk], axis=-1, keepdims=True), sink[hk]) for hk in hks]
    p = [jnp.exp(s[hk] - m[hk]) for hk in hks]
    p_up = [jnp.where(upper, p[hk], 0.0) for hk in hks]
    oe = [jnp.dot(p_up[hk].astype(BF16), with_ones(vp, hk), preferred_element_type=F32)
          + jnp.dot((p[hk] - p_up[hk]).astype(BF16), with_ones(vc, hk), preferred_element_type=F32) for hk in hks]
    ps = [jnp.exp(sink[hk] - m[hk]) for hk in hks]
    inv = [1.0 / (oe[hk][:, B_HEAD_DIM:B_HEAD_DIM + 1] + ps[hk]) for hk in hks]
    return upper, [(qs[hk], p[hk] * inv[hk], ps[hk] * inv[hk], oe[hk][:, :B_HEAD_DIM] * inv[hk]) for hk in hks]


def _swa_specs():
    qspec = lambda c0: pl.BlockSpec((BLOCK, B_WIDTH), lambda i: (i, c0 // B_WIDTH))
    cur = lambda c0: pl.BlockSpec((BLOCK, LANE), lambda i: (i, c0 // LANE))
    prev = lambda c0: pl.BlockSpec((BLOCK, LANE), lambda i: (jnp.maximum(i - 1, 0), c0 // LANE))
    return qspec, cur, prev


def _carried(carry, refs, n_in, n_out, steps):
    if carry is None:
        return refs
    ci, co = len(carry.ins), len(carry.outs)
    own = refs[:n_in] + refs[n_in + ci:n_in + ci + n_out] + refs[n_in + ci + n_out + co:len(refs) - 3]
    parts = refs[n_in:n_in + ci], refs[n_in + ci + n_out:n_in + ci + n_out + co], refs[len(refs) - 3:]

    @pl.when(pl.program_id(0) == 0)
    def _():
        carry.start(*parts)

    @pl.when(pl.program_id(0) == steps - 1)
    def _():
        carry.finish(*parts)

    return own


def _carry_specs(carry):
    if carry is None:
        return [], [], [], [], []
    return (list(carry.ins), [_ANY] * len(carry.ins), [_ANY] * len(carry.outs), list(carry.outs), carry.scratch())


def _swa_fwd(h, sinks_b, *, name, carry=None):
    t = h.shape[0]
    qspec, cur, prev = _swa_specs()
    c_ins, c_in_specs, c_out_specs, c_outs, c_scratch = _carry_specs(carry)

    def body(*refs):
        q_ref, kc_ref, kp_ref, vc_ref, vp_ref, sk_ref, o_ref, p_ref, ps_ref = _carried(carry, refs, 6, 3, t // BLOCK)
        n_blk = pl.program_id(0)
        _, groups = _swa_group_probs(q_ref, sk_ref, kp_ref[...], kc_ref[...], vp_ref[...], vc_ref[...], n_blk)
        lane = lax.broadcasted_iota(jnp.int32, (BLOCK, LANE), 1)
        sink_probs = jnp.zeros((BLOCK, LANE), F32)
        for hk, (_, p, ps, o) in enumerate(groups):
            for g in range(B_GROUP):
                hq = hk * B_GROUP + g
                rows = slice(g * BLOCK, (g + 1) * BLOCK)
                o_ref[:, hq * B_HEAD_DIM:(hq + 1) * B_HEAD_DIM] = o[rows]
                p_ref[:, hq * BLOCK:(hq + 1) * BLOCK] = p[rows].astype(BF16)
                sink_probs = sink_probs + jnp.where(lane == hq, ps[rows], 0.0)
        ps_ref[...] = sink_probs

    row = lambda w: pl.BlockSpec((BLOCK, w), lambda i: (i, 0))
    outs = pl.pallas_call(
        body, name=name, grid=(t // BLOCK,),
        in_specs=[qspec(C_QB), cur(C_KB), prev(C_KB), cur(C_VB), prev(C_VB),
                  pl.BlockSpec((B_Q_HEADS, LANE), lambda i: (0, 0))] + c_in_specs,
        out_specs=[row(B_WIDTH), row(B_Q_HEADS * BLOCK), row(LANE)] + c_out_specs,
        out_shape=[jax.ShapeDtypeStruct((t, B_WIDTH), F32), jax.ShapeDtypeStruct((t, B_Q_HEADS * BLOCK), BF16),
                   jax.ShapeDtypeStruct((t, LANE), F32)] + c_outs,
        scratch_shapes=c_scratch,
        compiler_params=_cp("arbitrary"))(h, h, h, h, h, sinks_b, *c_ins)
    return outs[:3], outs[3:]


def _rms_gate(o, za, nw):
    outs = []
    for hd in range(A_HEADS):
        oh = o[:, hd * LANE:(hd + 1) * LANE]
        r = lax.rsqrt(jnp.mean(oh * oh, -1, keepdims=True) + RMS_EPS)
        outs.append(oh * r * nw)
    return jnp.concatenate(outs, axis=1) * _silu(za)


def _out_ln(x, oa, ob, h, norm_w, w_out, ln_g, ln_b, *, tm, name, target=None):
    t = x.shape[0]
    last = target is not None

    def body(*refs):
        x_ref, oa_ref, ob_ref, za_ref, zb_ref, nw_ref, w_ref, g_ref, b_ref = refs[:9]
        xn_ref, mx_ref, r_ref = refs[9 + last:12 + last]
        ya = _rms_gate(oa_ref[...], za_ref[...].astype(F32), nw_ref[...])
        yb = ob_ref[...] * _silu(zb_ref[...].astype(F32))
        mixed = jnp.concatenate([ya, yb], axis=1).astype(BF16)
        mx_ref[...] = mixed
        r = DEEPNORM_ALPHA * x_ref[...] + jnp.dot(mixed, w_ref[...], preferred_element_type=F32)
        r_ref[...] = r
        mu = jnp.mean(r, -1, keepdims=True)
        xc = r - mu
        var = jnp.mean(xc * xc, -1, keepdims=True)
        xn = xc * lax.rsqrt(var + LN_EPS) * g_ref[...] + b_ref[...]
        if not last:
            xn_ref[...] = xn
            return
        loss_ref = refs[13]

        @pl.when(pl.program_id(0) == 0)
        def _():
            loss_ref[...] = jnp.zeros_like(loss_ref)

        err = xn - refs[9][...]
        xn_ref[...] = err * (1.0 / D_MODEL)
        loss_ref[...] += 0.5 / D_MODEL * jnp.sum(err * err)

    row = lambda w, c: pl.BlockSpec((tm, w), lambda i: (i, c))
    full = lambda a, b: pl.BlockSpec((a, b), lambda i: (0, 0))
    wide = jax.ShapeDtypeStruct((t, D_MODEL), F32)
    return pl.pallas_call(
        body, name=name, grid=(t // tm,),
        in_specs=[row(D_MODEL, 0), row(A_WIDTH, 0), row(B_WIDTH, 0), row(A_WIDTH, C_ZA // A_WIDTH),
                  row(B_WIDTH, C_ZB // B_WIDTH), full(1, LANE), full(D_MODEL, D_MODEL), full(1, D_MODEL),
                  full(1, D_MODEL)] + [row(D_MODEL, 0)] * last,
        out_specs=[row(D_MODEL, 0), row(D_MODEL, 0), row(D_MODEL, 0)] + [full(SUBLANE, LANE)] * last,
        out_shape=[wide, jax.ShapeDtypeStruct((t, D_MODEL), BF16), wide]
        + [jax.ShapeDtypeStruct((SUBLANE, LANE), F32)] * last,
        compiler_params=_cp("arbitrary" if last else "parallel"))(
        x, oa, ob, h, h, norm_w, w_out, ln_g, ln_b, *([target] if last else []))


def _layer_fwd(x, wt, conv_w, par, sinks_b, norm_w, w_out_bf, ln_g, ln_b, l, carries=None, target=None):
    carries = carries or {}
    h, got_in = _matmul_nt(x, wt, tm=512, name=f"in_proj_{l}", carry=carries.get("in_proj"))
    if callable(w_out_bf):
        w_out_bf = w_out_bf(got_in)
    (q, k, v, bg, bgt), got_pre = _dn_pre(h, conv_w, par, tt=512, name=f"dn_pre_{l}", carry=carries.get("dn_pre"))
    (u, w, tmat, qk), got_wy = _dn_wy(q, k, v, bg, bgt, name=f"dn_wy_{l}", carry=carries.get("dn_wy"))
    (oa, vn, s_all), got_scan = _dn_scan_fwd(q, k, u, w, qk, bg, name=f"dn_scan_{l}", carry=carries.get("dn_scan"))
    (ob, swa_p, swa_ps), got_swa = _swa_fwd(h, sinks_b, name=f"swa_fwd_{l}", carry=carries.get("swa"))
    xn, mixed, r, *loss = _out_ln(x, oa, ob, h, norm_w, w_out_bf, ln_g, ln_b, tm=512, name=f"out_ln_{l}", target=target)
    if loss:
        xn = (xn, loss[0])
    res = dict(x=x, h=h, q=q, k=k, v=v, bg=bg, bgt=bgt, w=w, tmat=tmat, qk=qk, vn=vn, oa=oa, s_all=s_all,
               mixed=mixed, r=r, w_out=w_out_bf, ob=ob, swa_p=swa_p, swa_ps=swa_ps)
    return xn, res, dict(in_proj=got_in, dn_pre=got_pre, dn_wy=got_wy, dn_scan=got_scan, swa=got_swa)


def _ln_out_bwd(dxn, r, mixed, ln_g, w_out, *, tm, name):
    t = dxn.shape[0]

    def body(dxn_ref, r_ref, mx_ref, g_ref, w_ref, dr_ref, dm_ref, dw_ref, dg_ref, db_ref):
        @pl.when(pl.program_id(0) == 0)
        def _():
            dw_ref[...] = jnp.zeros_like(dw_ref)
            dg_ref[...] = jnp.zeros_like(dg_ref)
            db_ref[...] = jnp.zeros_like(db_ref)

        rr = r_ref[...]
        xc = rr - jnp.mean(rr, -1, keepdims=True)
        rstd = lax.rsqrt(jnp.mean(xc * xc, -1, keepdims=True) + LN_EPS)
        xhat = xc * rstd
        dxn_v = dxn_ref[...]
        dxh = dxn_v * g_ref[...]
        dr = rstd * (dxh - jnp.mean(dxh, -1, keepdims=True) - xhat * jnp.mean(dxh * xhat, -1, keepdims=True))
        dr_ref[...] = dr
        dg_ref[...] += jnp.sum(dxn_v * xhat, axis=0, keepdims=True)
        db_ref[...] += jnp.sum(dxn_v, axis=0, keepdims=True)
        drb = dr.astype(BF16)
        dm_ref[...] = _dot_nt(drb, w_ref[...])
        dw_ref[...] += _dot_tn(mx_ref[...], drb)

    row = pl.BlockSpec((tm, D_MODEL), lambda i: (i, 0))
    full = lambda a, b: pl.BlockSpec((a, b), lambda i: (0, 0))
    big = jax.ShapeDtypeStruct((t, D_MODEL), F32)
    vec = jax.ShapeDtypeStruct((1, D_MODEL), F32)
    return pl.pallas_call(
        body, name=name, grid=(t // tm,),
        in_specs=[row, row, row, full(1, D_MODEL), full(D_MODEL, D_MODEL)],
        out_specs=[row, row, full(D_MODEL, D_MODEL), full(1, D_MODEL), full(1, D_MODEL)],
        out_shape=[big, big, jax.ShapeDtypeStruct((D_MODEL, D_MODEL), F32), vec, vec],
        compiler_params=_cp("arbitrary"))(dxn, r, mixed, ln_g, w_out)


def _dn_post_bwd(dm, oa, h, norm_w, *, tm, name):
    t = oa.shape[0]

    def body(dy_ref, o_ref, za_ref, nw_ref, do_ref, dza_ref, dnw_ref):
        @pl.when(pl.program_id(0) == 0)
        def _():
            dnw_ref[...] = jnp.zeros_like(dnw_ref)

        nw = nw_ref[...]
        dnw = jnp.zeros_like(nw)
        for hd in range(A_HEADS):
            sl = slice(hd * LANE, (hd + 1) * LANE)
            oh, za, dy = o_ref[:, sl], za_ref[:, sl].astype(F32), dy_ref[:, sl]
            rs = lax.rsqrt(jnp.mean(oh * oh, -1, keepdims=True) + RMS_EPS)
            nrm = oh * rs
            gate, dgate = _silu_and_grad(za)
            dza_ref[:, sl] = dy * nrm * nw * dgate
            dn = dy * gate
            dnw = dnw + jnp.sum(dn * nrm, axis=0, keepdims=True)
            dnn = dn * nw
            do_ref[:, sl] = rs * dnn - oh * (rs * rs * rs) * jnp.mean(dnn * oh, -1, keepdims=True)
        dnw_ref[...] += dnw

    row = lambda c: pl.BlockSpec((tm, A_WIDTH), lambda i: (i, c))
    wide = jax.ShapeDtypeStruct((t, A_WIDTH), F32)
    return pl.pallas_call(
        body, name=name, grid=(t // tm,),
        in_specs=[row(0), row(0), row(C_ZA // A_WIDTH), pl.BlockSpec((1, LANE), lambda i: (0, 0))],
        out_specs=[row(0), row(C_ZA // A_WIDTH), pl.BlockSpec((1, LANE), lambda i: (0, 0))],
        out_shape=[wide, jax.ShapeDtypeStruct((t, DH_MAIN), F32), jax.ShapeDtypeStruct((1, LANE), F32)],
        compiler_params=_cp("arbitrary"))(dm, oa, h, norm_w)


def _dn_scan_bwd(q, k, w, qk, bg, do, *, name):
    t = q.shape[0]
    rows = SCAN_ROWS
    per = rows // CHUNK
    n = t // rows

    def body(q_ref, k_ref, w_ref, qk_ref, bg_ref, do_ref, dvn_ref, ds_ref, dstate):
        @pl.when(pl.program_id(0) == 0)
        def _():
            dstate[...] = jnp.zeros_like(dstate)

        heads = range(A_HEADS)
        sl = lambda hd: slice(hd * LANE, (hd + 1) * LANE)
        ds_cur = [dstate[hd] for hd in heads]
        for c in reversed(range(per)):
            rs = slice(c * CHUNK, (c + 1) * CHUNK)
            bg_v = bg_ref[rs, :]
            gcols = [_chunk_gates(bg_v, None, hd)[1] for hd in heads]
            glasts = [gc[CHUNK - 1:CHUNK, :] for gc in gcols]
            for hd in heads:
                ds_ref[c, hd] = ds_cur[hd].astype(BF16)
            pdo = [_dot_tn(qk_ref[rs, hd * CHUNK:(hd + 1) * CHUNK], do_ref[rs, sl(hd)]) for hd in heads]
            qdo = [_dot_tn(q_ref[rs, sl(hd)] * jnp.exp(gcols[hd]), do_ref[rs, sl(hd)]) for hd in heads]
            dvns = [pdo[hd] + _dot(k_ref[rs, sl(hd)] * jnp.exp(glasts[hd] - gcols[hd]), ds_cur[hd]) for hd in heads]
            ds_cur = [qdo[hd] + jnp.exp(glasts[hd]) * ds_cur[hd] - _dot_tn(w_ref[rs, sl(hd)], dvns[hd])
                      for hd in heads]
            for hd in heads:
                dvn_ref[rs, sl(hd)] = dvns[hd]
        for hd in heads:
            dstate[hd] = ds_cur[hd]

    blk = pl.BlockSpec((rows, A_WIDTH), lambda i: (n - 1 - i, 0))
    return pl.pallas_call(
        body, name=name, grid=(n,),
        in_specs=[blk, blk, blk, pl.BlockSpec((rows, A_HEADS * CHUNK), lambda i: (n - 1 - i, 0)),
                  pl.BlockSpec((rows, LANE), lambda i: (n - 1 - i, 0)), blk],
        out_specs=[blk, pl.BlockSpec((per, A_HEADS, LANE, LANE), lambda i: (n - 1 - i, 0, 0, 0))],
        out_shape=[jax.ShapeDtypeStruct((t, A_WIDTH), F32),
                   jax.ShapeDtypeStruct((t // CHUNK, A_HEADS, LANE, LANE), BF16)],
        scratch_shapes=[pltpu.VMEM((A_HEADS, LANE, LANE), F32)],
        compiler_params=_cp("arbitrary"))(q, k, w, qk, bg, do)


def _dn_chunk_bwd(q, k, v, vn, tmat, qk, bg, bgt, s_all, ds_all, dvn, do, *, name, carry=None):
    t = q.shape[0]
    rows = WY_ROWS
    per = rows // CHUNK

    c_ins, c_in_specs, c_out_specs, c_outs, c_scratch = _carry_specs(carry)

    def body(*refs):
        (q_ref, k_ref, v_ref, vn_ref, tm_ref, qk_ref, bg_ref, bgt_ref, s_ref, ds_ref, dvn_ref, do_ref,
         dq_ref, dk_ref, dv_ref, dbg_ref, dbgt_ref) = _carried(carry, refs, 12, 5, t // rows)
        causal, strict, _ = _chunk_masks()
        lane = lax.broadcasted_iota(jnp.int32, (CHUNK, LANE), 1)
        rowi = lax.broadcasted_iota(jnp.int32, (CHUNK, 1), 0)
        sub = lax.broadcasted_iota(jnp.int32, (SUBLANE, CHUNK), 0)
        rs = lambda c: slice(c * CHUNK, (c + 1) * CHUNK)
        sl = lambda hd: slice(hd * LANE, (hd + 1) * LANE)
        hs = lambda hd: slice(hd * CHUNK, (hd + 1) * CHUNK)
        for c0 in range(0, per, WY_GROUP):
            items = [(c, hd) for c in range(c0, c0 + WY_GROUP) for hd in range(A_HEADS)]
            at = lambda ref: [ref[rs(c), sl(hd)] for c, hd in items]
            qs, ks, vs, dos, vns, dvns = at(q_ref), at(k_ref), at(v_ref), at(do_ref), at(vn_ref), at(dvn_ref)
            tmhs = [tm_ref[rs(c), hs(hd)] for c, hd in items]
            ps = [qk_ref[rs(c), hs(hd)] for c, hd in items]
            gates = [_chunk_gates(bg_ref[rs(c), :], bgt_ref[:, rs(c)], hd) for c, hd in items]
            betas = [g[0] for g in gates]
            gcols = [g[1] for g in gates]
            dmats = [jnp.exp(jnp.where(causal, g[1] - g[2], NEG)) for g in gates]
            es = [jnp.exp(gc) for gc in gcols]
            glasts = [gc[CHUNK - 1:CHUNK, :] for gc in gcols]
            eks = [jnp.exp(gl - gc) for gl, gc in zip(glasts, gcols)]
            kbs = [kh * b for kh, b in zip(ks, betas)]
            vbs = [vh * b for vh, b in zip(vs, betas)]
            kbes = [kb * e for kb, e in zip(kbs, es)]

            a_s = [jnp.where(strict, _dot_nt(kb, kh) * dm, 0.0) for kb, kh, dm in zip(kbs, ks, dmats)]
            dps = [jnp.where(causal, _dot_nt(doh, vnh), 0.0) for doh, vnh in zip(dos, vns)]
            rows2 = lambda a, b: jnp.concatenate([a, b], axis=0)
            cols2 = lambda a, b: jnp.concatenate([a, b], axis=1)
            by_s = [_dot_nt(rows2(doh, dvnh), s_ref[c, hd]) for doh, dvnh, (c, hd) in zip(dos, dvns, items)]
            dqds = [m[:CHUNK] for m in by_s]
            dws = [-m[CHUNK:] for m in by_s]
            dkds = [_dot_nt(vnh, ds_ref[c, hd]) for vnh, (c, hd) in zip(vns, items)]
            dgts = [jnp.sum(s_ref[c, hd].astype(F32) * ds_ref[c, hd].astype(F32), keepdims=True) for c, hd in items]
            pairs = [cols2(dvnh, dw) for dvnh, dw in zip(dvns, dws)]
            by_t = [_dot_tn(tmh, pr) for tmh, pr in zip(tmhs, pairs)]
            dvbs = [m[:, :LANE] for m in by_t]
            dkbes = [m[:, LANE:] for m in by_t]
            dts = [_dot_nt(pr, cols2(vb, kbe)) for pr, vb, kbe in zip(pairs, vbs, kbes)]
            xs = [_dot_nt(dt, tmh) for dt, tmh in zip(dts, tmhs)]
            das = [jnp.where(strict, -_dot_tn(tmh, x), 0.0) for tmh, x in zip(tmhs, xs)]
            dmas = [da * dm for da, dm in zip(das, dmats)]
            dmps = [dp * dm for dp, dm in zip(dps, dmats)]
            stacked = [rows2(dma, dmp) for dma, dmp in zip(dmas, dmps)]
            by_k = [_dot(st, kh) for st, kh in zip(stacked, ks)]
            dkbs = [m[:CHUNK] + dkbe * e for m, dkbe, e in zip(by_k, dkbes, es)]
            for i, (c, hd) in enumerate(items):
                dq_ref[rs(c), sl(hd)] = by_k[i][CHUNK:] + dqds[i] * es[i]
                dk_ref[rs(c), sl(hd)] = (_dot_tn(stacked[i], rows2(kbs[i], qs[i])) + dkds[i] * eks[i]
                                         + dkbs[i] * betas[i])
                dv_ref[rs(c), sl(hd)] = dvbs[i] * betas[i]
            for c in range(c0, c0 + WY_GROUP):
                acc = jnp.zeros((CHUNK, LANE), F32)
                acc_t = jnp.zeros((SUBLANE, CHUNK), F32)
                for i, (ci, hd) in enumerate(items):
                    if ci != c:
                        continue
                    gmat = das[i] * a_s[i] + dps[i] * ps[i]
                    rk = jnp.sum(dkds[i] * ks[i], -1, keepdims=True) * eks[i]
                    de = jnp.sum(dqds[i] * qs[i] + dkbes[i] * kbs[i], -1, keepdims=True)
                    dglast = jnp.sum(rk, keepdims=True) + dgts[i] * jnp.exp(glasts[i])
                    dgc = (jnp.sum(gmat, -1, keepdims=True) + de * es[i] - rk
                           + jnp.where(rowi == CHUNK - 1, dglast, 0.0))
                    dbeta = jnp.sum(dkbs[i] * ks[i] + dvbs[i] * vs[i], -1, keepdims=True)
                    acc = acc + jnp.where(lane == hd, dbeta, 0.0) + jnp.where(lane == A_HEADS + hd, dgc, 0.0)
                    acc_t = acc_t + jnp.where(sub == A_HEADS + hd, -jnp.sum(gmat, axis=0, keepdims=True), 0.0)
                dbg_ref[rs(c), :] = acc
                dbgt_ref[:, rs(c)] = acc_t

    blk = pl.BlockSpec((rows, A_WIDTH), lambda i: (i, 0))
    half = pl.BlockSpec((rows, A_HEADS * CHUNK), lambda i: (i, 0))
    col = pl.BlockSpec((rows, LANE), lambda i: (i, 0))
    rowf = pl.BlockSpec((SUBLANE, rows), lambda i: (0, i))
    st = pl.BlockSpec((per, A_HEADS, LANE, LANE), lambda i: (i, 0, 0, 0))
    wide = jax.ShapeDtypeStruct((t, A_WIDTH), F32)
    outs = pl.pallas_call(
        body, name=name, grid=(t // rows,),
        in_specs=[blk, blk, blk, blk, half, half, col, rowf, st, st, blk, blk] + c_in_specs,
        out_specs=[blk, blk, blk, col, rowf] + c_out_specs,
        out_shape=[wide, wide, wide, jax.ShapeDtypeStruct((t, LANE), F32),
                   jax.ShapeDtypeStruct((SUBLANE, t), F32)] + c_outs,
        scratch_shapes=c_scratch,
        compiler_params=_cp("arbitrary"))(q, k, v, vn, tmat, qk, bg, bgt, s_all, ds_all, dvn, do, *c_ins)
    return outs[:5], outs[5:]


def _dn_pre_bwd(h, conv_w, par, dq, dk, dv, dbg, dbgt, *, tt, name):
    t = h.shape[0]
    cw = 3 * A_WIDTH
    hb = tt // HALO

    def body(pre_ref, halo_ref, bgi_ref, cw_ref, par_ref, dq_ref, dk_ref, dv_ref, dbg_ref, dbgt_ref,
             dc_ref, dbgi_ref, dpar_ref):
        i = pl.program_id(0)

        @pl.when(i == 0)
        def _():
            dpar_ref[...] = jnp.zeros_like(dpar_ref)

        cur = pre_ref[...].astype(F32)
        before = jnp.where(i > 0, halo_ref[...].astype(F32)[HALO - SUBLANE:], 0.0)
        c = _conv_fwd(cur, before, cw_ref[...])
        s, ds = _silu_and_grad(c)
        for hd in range(A_HEADS):
            sl = slice(hd * LANE, (hd + 1) * LANE)
            for base, d_ref, scale in ((0, dq_ref, A_HEAD_DIM ** -0.5), (A_WIDTH, dk_ref, 1.0)):
                csl = slice(base + hd * LANE, base + (hd + 1) * LANE)
                tq = s[:, base + hd * LANE:base + (hd + 1) * LANE]
                dy = d_ref[:, sl]
                rq = lax.rsqrt(jnp.sum(tq * tq, -1, keepdims=True) + L2_EPS)
                dtq = scale * (rq * dy - tq * (rq * rq * rq) * jnp.sum(dy * tq, -1, keepdims=True))
                dc_ref[:, csl] = dtq * ds[:, base + hd * LANE:base + (hd + 1) * LANE]
        dc_ref[:, 2 * A_WIDTH:] = dv_ref[...] * ds[:, 2 * A_WIDTH:]
        raw = bgi_ref[...].astype(F32)
        lane = lax.broadcasted_iota(jnp.int32, raw.shape, 1)
        is_b = lane < A_HEADS
        is_a = (lane >= A_HEADS) & (lane < 2 * A_HEADS)
        rows_t = jnp.concatenate([dbgt_ref[...], jnp.zeros((LANE - SUBLANE, tt), F32)], axis=0)
        dbg_v = dbg_ref[...] + jnp.where(is_a, jnp.transpose(rows_t), 0.0)
        dbg_v = jnp.where(is_a, _dot_hi(_chunk_tri(tt, lower=False), jnp.where(is_a, dbg_v, 0.0)), dbg_v)
        beta = _sigmoid(raw)
        z = raw + par_ref[1:2, :]
        neg_ea = -jnp.exp(par_ref[0:1, :])
        g = neg_ea * _softplus(z)
        da = dbg_v * neg_ea * _sigmoid(z)
        dbgi_ref[...] = jnp.where(is_b, dbg_v * beta * (1.0 - beta), jnp.where(is_a, da, 0.0))
        dpar_ref[0:1, :] += jnp.sum(jnp.where(is_a, dbg_v * g, 0.0), axis=0, keepdims=True)
        dpar_ref[1:2, :] += jnp.sum(jnp.where(is_a, da, 0.0), axis=0, keepdims=True)

    wide = pl.BlockSpec((tt, A_WIDTH), lambda i: (i, 0))
    return pl.pallas_call(
        body, name=name, grid=(t // tt,),
        in_specs=[pl.BlockSpec((tt, cw), lambda i: (i, 0)),
                  pl.BlockSpec((HALO, cw), lambda i: (jnp.maximum(i * hb - 1, 0), 0)),
                  pl.BlockSpec((tt, LANE), lambda i: (i, C_BG // LANE)),
                  pl.BlockSpec((CONV_K, cw), lambda i: (0, 0)),
                  pl.BlockSpec((SUBLANE, LANE), lambda i: (0, 0)),
                  wide, wide, wide, pl.BlockSpec((tt, LANE), lambda i: (i, 0)),
                  pl.BlockSpec((SUBLANE, tt), lambda i: (0, i))],
        out_specs=[pl.BlockSpec((tt, cw), lambda i: (i, 0)), pl.BlockSpec((tt, LANE), lambda i: (i, 0)),
                   pl.BlockSpec((SUBLANE, LANE), lambda i: (0, 0))],
        out_shape=[jax.ShapeDtypeStruct((t, cw), F32), jax.ShapeDtypeStruct((t, LANE), F32),
                   jax.ShapeDtypeStruct((SUBLANE, LANE), F32)],
        compiler_params=_cp("arbitrary"))(h, h, h, conv_w, par, dq, dk, dv, dbg, dbgt)


def _conv_bwd(dc, h, conv_w, dh, *, tt, name):
    t = dc.shape[0]
    cw = 3 * A_WIDTH
    hb = tt // HALO
    nb = t // tt

    def body(dc_ref, after_ref, pre_ref, before_ref, cw_ref, dh_in_ref, dpre_ref, dcw_ref):
        i = pl.program_id(0)

        @pl.when(i == 0)
        def _():
            dcw_ref[...] = jnp.zeros_like(dcw_ref)

        dcv = dc_ref[...]
        after = jnp.where(i < nb - 1, after_ref[...], 0.0)
        cur = pre_ref[...].astype(F32)
        before = jnp.where(i > 0, before_ref[...].astype(F32)[HALO - SUBLANE:], 0.0)
        w = cw_ref[...]
        acc = dcv * w[CONV_K - 1:CONV_K, :]
        dcw_ref[CONV_K - 1:CONV_K, :] += jnp.sum(dcv * cur, axis=0, keepdims=True)
        for s in range(1, CONV_K):
            j = CONV_K - 1 - s
            acc = acc + _shift_up(dcv, after, s) * w[j:j + 1, :]
            dcw_ref[j:j + 1, :] += jnp.sum(dcv * _shift_down(cur, before, s), axis=0, keepdims=True)
        dpre_ref[...] = acc

    return pl.pallas_call(
        body, name=name, grid=(nb,),
        in_specs=[pl.BlockSpec((tt, cw), lambda i: (i, 0)),
                  pl.BlockSpec((SUBLANE, cw), lambda i: (jnp.minimum((i + 1) * (tt // SUBLANE), t // SUBLANE - 1), 0)),
                  pl.BlockSpec((tt, cw), lambda i: (i, 0)),
                  pl.BlockSpec((HALO, cw), lambda i: (jnp.maximum(i * hb - 1, 0), 0)),
                  pl.BlockSpec((CONV_K, cw), lambda i: (0, 0)), _ANY],
        out_specs=[pl.BlockSpec((tt, cw), lambda i: (i, 0)), pl.BlockSpec((SUBLANE, cw), lambda i: (0, 0))],
        out_shape=[jax.ShapeDtypeStruct(dh.shape, F32), jax.ShapeDtypeStruct((SUBLANE, cw), F32)],
        input_output_aliases={5: 0},
        compiler_params=_cp("arbitrary"))(dc, dc, h, h, conv_w, dh)


def _swa_bwd(h, dm, ob, probs, sink_probs, dh, *, name, carry=None):
    t = h.shape[0]
    qspec, cur, prev = _swa_specs()
    c_ins, c_in_specs, c_out_specs, c_outs, c_scratch = _carry_specs(carry)

    def body(*refs):
        (q_ref, kc_ref, kp_ref, vc_ref, vp_ref, zb_ref, dy_ref, ob_ref, p_ref, ps_ref, dh_in_ref,
         dqz_ref, dk_ref, dv_ref, dsk_ref) = _carried(carry, refs, 11, 4, t // BLOCK)
        n_blk = pl.program_id(0)

        @pl.when(n_blk == 0)
        def _():
            dk_ref[...] = jnp.zeros_like(dk_ref)
            dv_ref[...] = jnp.zeros_like(dv_ref)
            dsk_ref[...] = jnp.zeros_like(dsk_ref)

        kp, kc, vp, vc = kp_ref[...], kc_ref[...], vp_ref[...], vc_ref[...]
        scale = B_HEAD_DIM ** -0.5
        hks = range(B_KV_HEADS)
        ksl = lambda hk: slice(hk * B_HEAD_DIM, (hk + 1) * B_HEAD_DIM)
        heads = lambda hk: range(hk * B_GROUP, (hk + 1) * B_GROUP)
        upper, _ = _swa_window()
        groups = [(_stack_heads(q_ref, hk) * scale,
                   jnp.concatenate([p_ref[:, h * BLOCK:(h + 1) * BLOCK].astype(F32) for h in heads(hk)], axis=0),
                   jnp.concatenate([ps_ref[:, h:h + 1] for h in heads(hk)], axis=0),
                   _stack_heads(ob_ref, hk)) for hk in hks]
        zbs = [_stack_heads(zb_ref, hk) for hk in hks]
        dys = [_stack_heads(dy_ref, hk) for hk in hks]
        gates = [_silu_and_grad(zbs[hk]) for hk in hks]
        dos = [dys[hk] * gates[hk][0] for hk in hks]
        deltas = [jnp.sum(dos[hk] * groups[hk][3], -1, keepdims=True) for hk in hks]
        dps = [jnp.where(upper, _dot_nt(dos[hk], vp[:, ksl(hk)]), _dot_nt(dos[hk], vc[:, ksl(hk)])) for hk in hks]
        dss = [groups[hk][1] * (dps[hk] - deltas[hk]) for hk in hks]
        ds_up = [jnp.where(upper, dss[hk], 0.0) for hk in hks]
        ds_lo = [dss[hk] - ds_up[hk] for hk in hks]
        p_up = [jnp.where(upper, groups[hk][1], 0.0) for hk in hks]
        p_lo = [groups[hk][1] - p_up[hk] for hk in hks]
        dqs = [(_dot(ds_up[hk], kp[:, ksl(hk)]) + _dot(ds_lo[hk], kc[:, ksl(hk)])) * scale for hk in hks]
        dk_prev = [_dot_tn(ds_up[hk], groups[hk][0]) for hk in hks]
        dk_cur = [_dot_tn(ds_lo[hk], groups[hk][0]) for hk in hks]
        dv_prev = [_dot_tn(p_up[hk], dos[hk]) for hk in hks]
        dv_cur = [_dot_tn(p_lo[hk], dos[hk]) for hk in hks]
        for hk in hks:
            dzb = dys[hk] * groups[hk][3] * gates[hk][1]
            dsink = groups[hk][2] * deltas[hk]
            for g in range(B_GROUP):
                hq = hk * B_GROUP + g
                rows = slice(g * BLOCK, (g + 1) * BLOCK)
                qsl = slice(hq * B_HEAD_DIM, (hq + 1) * B_HEAD_DIM)
                dqz_ref[:, qsl] = dqs[hk][rows]
                dqz_ref[:, B_WIDTH + hq * B_HEAD_DIM:B_WIDTH + (hq + 1) * B_HEAD_DIM] = dzb[rows]
                dsk_ref[hq:hq + 1, :] += -jnp.sum(dsink[rows], keepdims=True)
        at_cur = pl.ds(pl.multiple_of(n_blk * BLOCK, BLOCK), BLOCK)
        at_prev = pl.ds(pl.multiple_of(jnp.maximum(n_blk - 1, 0) * BLOCK, BLOCK), BLOCK)
        dk_ref[at_prev, :] += jnp.concatenate(dk_prev, axis=1)
        dv_ref[at_prev, :] += jnp.concatenate(dv_prev, axis=1)
        dk_ref[at_cur, :] += jnp.concatenate(dk_cur, axis=1)
        dv_ref[at_cur, :] += jnp.concatenate(dv_cur, axis=1)

    narrow = jax.ShapeDtypeStruct((t, B_KV_WIDTH), F32)
    res = lambda a, b: pl.BlockSpec((a, b), lambda i: (0, 0))
    row = lambda w: pl.BlockSpec((BLOCK, w), lambda i: (i, 0))
    outs = pl.pallas_call(
        body, name=name, grid=(t // BLOCK,),
        in_specs=[qspec(C_QB), cur(C_KB), prev(C_KB), cur(C_VB), prev(C_VB), qspec(C_ZB),
                  pl.BlockSpec((BLOCK, B_WIDTH), lambda i: (i, 1)), row(B_WIDTH), row(B_Q_HEADS * BLOCK), row(LANE),
                  _ANY] + c_in_specs,
        out_specs=[pl.BlockSpec((BLOCK, 2 * B_WIDTH), lambda i: (i, C_QB // (2 * B_WIDTH))),
                   res(t, B_KV_WIDTH), res(t, B_KV_WIDTH), res(B_Q_HEADS, LANE)] + c_out_specs,
        out_shape=[jax.ShapeDtypeStruct(dh.shape, F32), narrow, narrow,
                   jax.ShapeDtypeStruct((B_Q_HEADS, LANE), F32)] + c_outs,
        scratch_shapes=c_scratch,
        input_output_aliases={10: 0},
        compiler_params=_cp("arbitrary"))(h, h, h, h, h, h, dm, ob, probs, sink_probs, dh, *c_ins)
    return outs[:4], outs[4:]


def _in_proj_dw(dh_main, dh_tail, x, *, tk, name):
    t, n = x.shape

    def body(a_ref, t_ref, x_ref, o_ref, ot_ref):
        @pl.when(pl.program_id(0) == 0)
        def _():
            o_ref[...] = jnp.zeros_like(o_ref)
            ot_ref[...] = jnp.zeros_like(ot_ref)

        xb = x_ref[...].astype(BF16)
        o_ref[...] += _dot_tn(a_ref[...], xb)
        ot_ref[...] += _dot_tn(t_ref[...], xb)

    row = lambda a: pl.BlockSpec((tk, a.shape[1]), lambda kk: (kk, 0))
    acc = lambda a: pl.BlockSpec((a.shape[1], n), lambda kk: (0, 0))
    return pl.pallas_call(
        body, name=name, grid=(t // tk,), in_specs=[row(dh_main), row(dh_tail), row(x)],
        out_specs=[acc(dh_main), acc(dh_tail)],
        out_shape=[jax.ShapeDtypeStruct((a.shape[1], n), F32) for a in (dh_main, dh_tail)],
        compiler_params=_cp("arbitrary"))(dh_main, dh_tail, x)


def _in_proj_dx(dh_main, dh_tail, wt, dr, *, tm, name, carry=None):
    t, n_main = dh_main.shape
    n_tail = dh_tail.shape[1]
    c_ins, c_in_specs, c_out_specs, c_outs, c_scratch = _carry_specs(carry)

    def body(*refs):
        a_ref, t_ref, wa_ref, wt_ref, r_ref, o_ref = _carried(carry, refs, 5, 1, t // tm)
        o_ref[...] = _dot(a_ref[...], wa_ref[...]) + _dot(t_ref[...], wt_ref[...]) + DEEPNORM_ALPHA * r_ref[...]

    row = lambda w: pl.BlockSpec((tm, w), lambda i: (i, 0))
    outs = pl.pallas_call(
        body, name=name, grid=(t // tm,),
        in_specs=[row(n_main), row(n_tail), pl.BlockSpec((n_main, D_MODEL), lambda i: (0, 0)),
                  pl.BlockSpec((n_tail, D_MODEL), lambda i: (n_main // n_tail, 0)), row(D_MODEL)] + c_in_specs,
        out_specs=[row(D_MODEL)] + c_out_specs,
        out_shape=[jax.ShapeDtypeStruct((t, D_MODEL), F32)] + c_outs,
        scratch_shapes=c_scratch,
        compiler_params=_cp("arbitrary"))(dh_main, dh_tail, wt, wt, dr, *c_ins)
    return outs[0], outs[1:]


def _layer_bwd(dxn, res, wt, conv_w, par, sinks_b, norm_w, w_out_bf, ln_g, l, carries=None, carry_dx=None):
    carries = carries or {}
    w_out_bf = res["w_out"]
    dr, dm, dw_out, dln_g, dln_b = _ln_out_bwd(dxn, res["r"], res["mixed"], ln_g, w_out_bf, tm=512, name=f"ln_out_bwd_{l}")
    h = res["h"]
    do, dh, dnw = _dn_post_bwd(dm, res["oa"], h, norm_w, tm=512, name=f"dn_post_bwd_{l}")
    dvn, ds_all = _dn_scan_bwd(res["q"], res["k"], res["w"], res["qk"], res["bg"], do, name=f"dn_scan_bwd_{l}")
    (dq, dk, dv, dbg, dbgt), got_chunk = _dn_chunk_bwd(
        res["q"], res["k"], res["v"], res["vn"], res["tmat"], res["qk"], res["bg"], res["bgt"], res["s_all"], ds_all,
        dvn, do, name=f"dn_chunk_bwd_{l}", carry=carries.get("dn_chunk"))
    dc, dbgi, dpar = _dn_pre_bwd(h, conv_w, par, dq, dk, dv, dbg, dbgt, tt=512, name=f"dn_pre_bwd_{l}")
    dh, dcw = _conv_bwd(dc, h, conv_w, dh, tt=512, name=f"conv_bwd_{l}")
    (dh, dkb, dvb, dsk), got_swa = _swa_bwd(h, dm, res["ob"], res["swa_p"], res["swa_ps"], dh, name=f"swa_bwd_{l}",
                                            carry=carries.get("swa"))
    carried = dict(dn_chunk=got_chunk, swa=got_swa)
    dh_tail = jnp.concatenate([dkb, dvb, dbgi], axis=1)
    dwt_main, dwt_tail = _in_proj_dw(dh, dh_tail, res["x"], tk=512, name=f"in_proj_dw_{l}")
    grads = dict(w_in=(dwt_main, dwt_tail), conv_w=dcw[:CONV_K], a_log=dpar[0, A_HEADS:2 * A_HEADS],
                 dt_bias=dpar[1, A_HEADS:2 * A_HEADS], norm_w=dnw[0], sinks=dsk[:, 0], w_out=dw_out,
                 ln_g=dln_g[0], ln_b=dln_b[0])
    dx, carried_dx = _in_proj_dx(dh, dh_tail, wt, dr, tm=512, name=f"in_proj_dx_{l}",
                                 carry=None if carry_dx is None else carry_dx(grads))
    return dx, grads, carried, carried_dx


def _layer_args(wt, conv_w, a_log, dt_bias, sinks, norm_w, w_out_bf):
    return (wt, conv_w, _gate_params(a_log, dt_bias), jnp.broadcast_to(sinks[:, None], (B_Q_HEADS, LANE)),
            norm_w[None], w_out_bf)


def _local_step(x, target, args0, args1, ln_g, ln_b, gathers=None, reduce1=None, reduce0=None):
    assert DEPTH == 2
    x1, res0, got = _layer_fwd(x, *args0, ln_g[0][None], ln_b[0][None], 0, carries=gathers)
    if gathers is not None:
        args1 = args1(got)
    (dx, loss_tile), res1, _ = _layer_fwd(x1, *args1, ln_g[1][None], ln_b[1][None], 1, target=target)
    dx, grads1, _, _ = _layer_bwd(dx, res1, *args1, ln_g[1][None], 1)
    carries = None if reduce1 is None else reduce1(grads1)
    carry_dx = None if reduce0 is None else (lambda grads0: reduce0(grads0, grads1, loss_tile))
    dx, grads0, landed1, landed0 = _layer_bwd(dx, res0, *args0, ln_g[0][None], 0, carries=carries, carry_dx=carry_dx)
    return loss_tile, dx, [grads0, grads1], landed1, landed0


_ANY = pl.BlockSpec(memory_space=pl.ANY)
_MESH = pl.DeviceIdType.MESH


HALF = D_MODEL // 2


class _Exchange:
    def __init__(self, ins, outs, n_remote, n_local, plan):
        self.ins, self.outs, self.n_remote, self.n_local, self.plan = tuple(ins), tuple(outs), n_remote, n_local, plan

    def scratch(self):
        return [pltpu.SemaphoreType.DMA((self.n_remote,)), pltpu.SemaphoreType.DMA((self.n_remote,)),
                pltpu.SemaphoreType.DMA((max(self.n_local, 1),))]

    def _copies(self, in_refs, out_refs, sems, arriving):
        send_sems, recv_sems, local_sems = sems
        local, sends, recvs = self.plan(in_refs, out_refs)
        loc = [pltpu.make_async_copy(s, d, local_sems.at[i]) for i, (s, d) in enumerate(local)]
        rem = [pltpu.make_async_remote_copy(src_ref=s, dst_ref=recvs[i] if arriving else d, send_sem=send_sems.at[i],
                                            recv_sem=recv_sems.at[i], device_id=peer, device_id_type=_MESH)
               for i, (s, d, peer) in enumerate(sends)]
        return loc, rem

    def start(self, in_refs, out_refs, sems):
        loc, rem = self._copies(in_refs, out_refs, sems, arriving=False)
        for cp in loc + rem:
            cp.start()

    def finish(self, in_refs, out_refs, sems):
        loc, rem = self._copies(in_refs, out_refs, sems, arriving=True)
        for cp in rem:
            cp.wait_recv()
        for cp in rem:
            cp.wait_send()
        for cp in loc:
            cp.wait()


def _run_exchange(ex, *, name):
    n_in, n_out = len(ex.ins), len(ex.outs)

    def body(*refs):
        parts = refs[:n_in], refs[n_in:n_in + n_out], refs[n_in + n_out:]
        ex.start(*parts)
        ex.finish(*parts)

    return pl.pallas_call(body, name=name, in_specs=[_ANY] * n_in, out_specs=[_ANY] * n_out, out_shape=list(ex.outs),
                          scratch_shapes=ex.scratch())(*ex.ins)


def _place():
    x, y, c = lax.axis_index("x"), lax.axis_index("y"), lax.axis_index("c")
    return x, y, c, [(1 - x, y), (x, 1 - y), (1 - x, 1 - y)]


def _gather_exchange(arrays):
    n = len(arrays)

    def plan(src, dst):
        x, y, c, chips = _place()
        me = 2 * x + y
        local = [(src[k], dst[k].at[me]) for k in range(n)]
        sends = [(src[k], dst[k].at[me], (px, py, c)) for k in range(n) for px, py in chips]
        recvs = [dst[k].at[2 * px + py] for k in range(n) for px, py in chips]
        return local, sends, recvs

    return _Exchange(arrays, [jax.ShapeDtypeStruct((N_SHARD,) + a.shape, a.dtype) for a in arrays], 3 * n, n, plan)


def _gather_two_level(pack, conv_w, *, name):
    rows = pack.shape[0]
    part_rows = rows // 2

    def body(pack_ref, conv_ref, land_ref, conv_land_ref, send1, recv1, send2, recv2, csend, crecv, local_sems):
        x, y, c, chips = _place()
        me = 2 * x + y
        sibling = (x, y, 1 - c)
        part = lambda core: pl.ds(pl.multiple_of(core * part_rows, 16), part_rows)
        remote = lambda src, dst, ss, rs, to: pltpu.make_async_remote_copy(
            src_ref=src, dst_ref=dst, send_sem=ss, recv_sem=rs, device_id=to, device_id_type=_MESH)
        local = [pltpu.make_async_copy(pack_ref, land_ref.at[me], local_sems.at[0]),
                 pltpu.make_async_copy(conv_ref, conv_land_ref.at[me], local_sems.at[1])]
        for cp in local:
            cp.start()
        first = [remote(pack_ref.at[part(c)], land_ref.at[me, part(c)], send1.at[j], recv1.at[j], (px, py, c))
                 for j, (px, py) in enumerate(chips)]
        convs = [remote(conv_ref, conv_land_ref.at[me], csend.at[j], crecv.at[j], (px, py, c))
                 for j, (px, py) in enumerate(chips)]
        for cp in first + convs:
            cp.start()
        passed = []
        for j, (px, py) in enumerate(chips):
            slot = 2 * px + py
            remote(pack_ref.at[part(c)], land_ref.at[slot, part(c)], send1.at[j], recv1.at[j], (px, py, c)).wait_recv()
            cp = remote(land_ref.at[slot, part(c)], land_ref.at[slot, part(c)], send2.at[j], recv2.at[j], sibling)
            cp.start()
            passed.append(cp)
        for j, (px, py) in enumerate(chips):
            slot = 2 * px + py
            remote(land_ref.at[slot, part(1 - c)], land_ref.at[slot, part(1 - c)], send2.at[j], recv2.at[j],
                   sibling).wait_recv()
            remote(conv_ref, conv_land_ref.at[slot], csend.at[j], crecv.at[j], (px, py, c)).wait_recv()
        for cp in first + convs + passed:
            cp.wait_send()
        for cp in local:
            cp.wait()

    sems = [pltpu.SemaphoreType.DMA((3,))] * 6 + [pltpu.SemaphoreType.DMA((2,))]
    return pl.pallas_call(
        body, name=name, in_specs=[_ANY, _ANY], out_specs=[_ANY, _ANY],
        out_shape=[jax.ShapeDtypeStruct((N_SHARD,) + pack.shape, pack.dtype),
                   jax.ShapeDtypeStruct((N_SHARD,) + conv_w.shape, conv_w.dtype)],
        scratch_shapes=sems)(pack, conv_w)


def _half(core):
    return pl.ds(pl.multiple_of(core * HALF, HALF), HALF)


def _reduce_scatter_exchange(g, row0, rows):
    def plan(src, dst):
        x, y, c, chips = _place()
        peers = [(px, py, c if t == 0 else 1 - c) for px, py in chips for t in (0, 1)] + [(x, y, 1 - c)]
        sends = [(src[0].at[2 * px + py, pl.ds(row0, rows), _half(pc)], dst[0].at[k], (px, py, pc))
                 for k, (px, py, pc) in enumerate(peers)]
        return [], sends, [dst[0].at[k] for k in range(7)]

    return _Exchange([g], [jax.ShapeDtypeStruct((7, rows, HALF), g.dtype)], 7, 0, plan)


def _pair_window_exchange(g):
    def plan(src, dst):
        x, y, c, _ = _place()
        return [], [(src[0].at[:, :, _half(1 - c)], dst[0], (x, y, 1 - c))], [dst[0]]

    return _Exchange([g], [jax.ShapeDtypeStruct(g.shape[:2] + (HALF,), g.dtype)], 1, 0, plan)


def _chip_scatter_exchange(p, small):
    def plan(src, dst):
        x, y, c, chips = _place()
        mine = 4 * x + 2 * y + c
        peers = [(px, py, c if t == 0 else 1 - c) for px, py in chips for t in (0, 1)] + [(x, y, 1 - c)]
        sends = [(src[0].at[2 * px + py], dst[0].at[j], (px, py, c)) for j, (px, py) in enumerate(chips)]
        recvs = [dst[0].at[j] for j in range(3)]
        sends += [(src[1], dst[1].at[mine], peer) for peer in peers]
        recvs += [dst[1].at[4 * px + 2 * py + pc] for px, py, pc in peers]
        return [(src[1], dst[1].at[mine])], sends, recvs

    outs = [jax.ShapeDtypeStruct((3,) + p.shape[1:], p.dtype), jax.ShapeDtypeStruct((8,) + small.shape, small.dtype)]
    return _Exchange([p, small], outs, 10, 1, plan)


def _share_exchange(arrays):
    n = len(arrays)

    def plan(src, dst):
        x, y, c, _ = _place()
        return [], [(src[k], dst[k], (x, y, 1 - c)) for k in range(n)], [dst[k] for k in range(n)]

    return _Exchange(arrays, [jax.ShapeDtypeStruct(a.shape, a.dtype) for a in arrays], n, 0, plan)


def _sum_scatter(g, lands, me, core, *, tc, name):
    rows = g.shape[1]
    per = HALF // tc
    n = len(lands)

    def body(*refs):
        g_ref, land_refs, o_ref = refs[1], refs[2:2 + n], refs[2 + n]
        at = 0
        for land_ref in land_refs:
            run = slice(at, at + land_ref.shape[1])
            acc = g_ref[run, :].astype(F32)
            for k in range(7):
                acc = acc + land_ref[k].astype(F32)
            o_ref[run, :] = acc
            at = run.stop

    return pl.pallas_call(
        body, name=name, out_shape=jax.ShapeDtypeStruct((rows, HALF), F32), compiler_params=_cp("parallel"),
        grid_spec=pltpu.PrefetchScalarGridSpec(
            num_scalar_prefetch=1, grid=(per,),
            in_specs=[pl.BlockSpec((None, rows, tc), lambda i, w: (w[0], 0, w[1] * per + i))]
            + [pl.BlockSpec((7, a.shape[1], tc), lambda i, w: (0, 0, i)) for a in lands],
            out_specs=pl.BlockSpec((rows, tc), lambda i, w: (0, i))))(
        jnp.stack([me, core]).astype(jnp.int32), g, *lands)


def _pair_add(g, land, core, *, name):
    n, rows, _ = g.shape

    def body(core_ref, g_ref, land_ref, o_ref):
        o_ref[...] = (g_ref[...].astype(F32) + land_ref[...].astype(F32)).astype(o_ref.dtype)

    blk = pl.BlockSpec((1, rows, HALF), lambda i, w: (i, 0, 0))
    return pl.pallas_call(
        body, name=name, out_shape=jax.ShapeDtypeStruct((n, rows, HALF), g.dtype), compiler_params=_cp("parallel"),
        grid_spec=pltpu.PrefetchScalarGridSpec(
            num_scalar_prefetch=1, grid=(n,),
            in_specs=[pl.BlockSpec((1, rows, HALF), lambda i, w: (i, 0, w[0])), blk], out_specs=blk))(
        jnp.reshape(core, (1,)).astype(jnp.int32), g, land)


def _sum_chips(p, land, me, *, tc, name):
    rows = p.shape[1]

    def body(me_ref, p_ref, land_ref, o_ref):
        acc = p_ref[...].astype(F32)
        for k in range(3):
            acc = acc + land_ref[k].astype(F32)
        o_ref[...] = acc

    return pl.pallas_call(
        body, name=name, out_shape=jax.ShapeDtypeStruct((rows, HALF), F32), compiler_params=_cp("parallel"),
        grid_spec=pltpu.PrefetchScalarGridSpec(
            num_scalar_prefetch=1, grid=(HALF // tc,),
            in_specs=[pl.BlockSpec((None, rows, tc), lambda i, w: (w[0], 0, i)),
                      pl.BlockSpec((3, rows, tc), lambda i, w: (0, 0, i))],
            out_specs=pl.BlockSpec((rows, tc), lambda i, w: (0, i))))(
        jnp.reshape(me, (1,)).astype(jnp.int32), p, land)


def _sum_slots(a, *, name):
    n = a.shape[0]

    def body(a_ref, o_ref):
        acc = a_ref[0]
        for k in range(1, n):
            acc = acc + a_ref[k]
        o_ref[...] = acc

    return pl.pallas_call(body, name=name, out_shape=jax.ShapeDtypeStruct(a.shape[1:], a.dtype))(a)


def _elementwise(fn, ins, n_out, block, *, name):
    shape = ins[0].shape
    grid = tuple(s // b for s, b in zip(shape, block))
    n_in = len(ins)

    def body(*refs):
        outs = fn(*[r[...] for r in refs[:n_in]])
        for o_ref, val in zip(refs[n_in:], outs):
            o_ref[...] = val

    spec = pl.BlockSpec(block, lambda i, j, k: (i, j, k))
    return pl.pallas_call(body, name=name, grid=grid, in_specs=[spec] * n_in, out_specs=[spec] * n_out,
                          out_shape=[jax.ShapeDtypeStruct(shape, F32)] * n_out,
                          compiler_params=_cp(*["parallel"] * 3))(*ins)


def _adamw_math(w, g, m, v):
    mn = ADAM_B1 * m + (1.0 - ADAM_B1) * g
    vn = ADAM_B2 * v + (1.0 - ADAM_B2) * (g * g)
    m_hat = mn / (1.0 - ADAM_B1 ** ADAM_STEP)
    v_hat = vn / (1.0 - ADAM_B2 ** ADAM_STEP)
    return -ADAM_LR * (m_hat / (jnp.sqrt(v_hat) + ADAM_EPS) + ADAM_WD * w), mn, vn


def _adamw(w, g, m, v, block, *, name):
    return _elementwise(_adamw_math, [w, g, m, v], 3, block, name=name)


def _interleave_layers(layers, *, tc, name):
    rows, cols = layers[0].shape
    n = len(layers)

    def body(*refs):
        for l in range(n):
            refs[n][:, l, :] = refs[l][...]

    return pl.pallas_call(body, name=name, grid=(cols // tc,),
                          in_specs=[pl.BlockSpec((rows, tc), lambda i: (0, i))] * n,
                          out_specs=pl.BlockSpec((rows, n, tc), lambda i: (0, 0, i)),
                          out_shape=jax.ShapeDtypeStruct((rows, n, cols), layers[0].dtype),
                          compiler_params=_cp("parallel"))(*layers)


def _adamw_small(ws, gs, ms, vs, *, name):
    n = len(ws)

    def body(*refs):
        w, g, m, v, outs = refs[:n], refs[n:2 * n], refs[2 * n:3 * n], refs[3 * n:4 * n], refs[4 * n:]
        for k in range(n):
            for slot, val in enumerate(_adamw_math(w[k][...], g[k][...], m[k][...], v[k][...])):
                outs[slot * n + k][...] = val

    outs = pl.pallas_call(body, name=name, out_shape=[jax.ShapeDtypeStruct(a.shape, F32) for a in ws] * 3)(
        *ws, *gs, *ms, *vs)
    return outs[:n], outs[n:2 * n], outs[2 * n:]


def _to_kernel_order(wt):
    gates = jnp.pad(wt[2048:2056], ((0, LANE - 2 * A_HEADS), (0, 0)))
    return jnp.concatenate([wt[0:2048], wt[2056:2568], wt[2824:3336], wt[2568:2696], wt[2696:2824], gates], axis=0)


def _from_kernel_order(main, tail):
    return jnp.concatenate([main[0:2048], tail[C_BG - DH_MAIN:C_BG - DH_MAIN + 2 * A_HEADS],
                            main[C_QB:C_QB + B_WIDTH], tail[0:B_KV_WIDTH], tail[B_KV_WIDTH:2 * B_KV_WIDTH],
                            main[C_ZB:C_ZB + B_WIDTH]], axis=0)


def _gate_params(a_log, dt_bias):
    return jnp.pad(jnp.stack([a_log, dt_bias]), ((0, SUBLANE - 2), (A_HEADS, LANE - 2 * A_HEADS)))


SMALL = ("conv_w", "a_log", "dt_bias", "norm_w", "sinks", "ln_g", "ln_b")


def _pack(parts, cols):
    flat = jnp.concatenate([p.reshape(-1) for p in parts])
    rows = -(-flat.shape[0] // cols)
    return jnp.pad(flat, (0, rows * cols - flat.shape[0])).reshape(rows, cols)


def _unpack(packed, shapes):
    flat = packed.reshape(-1)
    out, at = [], 0
    for s in shapes:
        n = math.prod(s)
        out.append(flat[at:at + n].reshape(s))
        at += n
    return out


def kernel(x, w_in, conv_w, a_log, dt_bias, norm_w, sinks, w_out, ln_g, ln_b, loss_target, m_w_in, m_conv_w, m_a_log, m_dt_bias, m_norm_w, m_sinks, m_w_out, m_ln_g, m_ln_b, v_w_in, v_conv_w, v_a_log, v_dt_bias, v_norm_w, v_sinks, v_w_out, v_ln_g, v_ln_b):
    xi, yi, ci = lax.axis_index("x"), lax.axis_index("y"), lax.axis_index("c")
    me = 2 * xi + yi

    to_t = lambda a: jnp.transpose(a, (2, 0, 1))
    from_t = lambda a: jnp.transpose(a, (1, 2, 0))

    wt_shard = to_t(w_in)

    def pack_weights(l):
        rows = jnp.pad(wt_shard[:, l], ((0, IN_PAD - IN_SHARD), (0, 0)))
        return jnp.concatenate([rows, w_out[l]], axis=0).astype(BF16)

    pack0, pack1 = pack_weights(0), pack_weights(1)
    got_in0, g_conv = _gather_two_level(pack0[:IN_PAD], conv_w, name="gather_weights_0")
    conv_full = jnp.moveaxis(g_conv, 0, 2).reshape(DEPTH, CONV_K, 3 * A_WIDTH)
    carriers = ("dn_pre", "dn_wy", "dn_scan")
    cuts = (0, 288, 624, IN_PAD)
    gathers = {nm: _gather_exchange([pack1[cuts[i]:cuts[i + 1]]]) for i, nm in enumerate(carriers)}
    gathers.update(in_proj=_gather_exchange([pack0[IN_PAD:]]), swa=_gather_exchange([pack1[IN_PAD:]]))
    w_in_of = lambda rows: _to_kernel_order(rows[:, :IN_SHARD].reshape(IN_COLS, D_MODEL))
    w_out_of = lambda rows: rows.reshape(D_MODEL, D_MODEL)
    args0 = _layer_args(w_in_of(got_in0), conv_full[0], a_log[0], dt_bias[0], sinks[0], norm_w[0],
                        lambda got: w_out_of(got[0]))

    def args1(got):
        rows = jnp.concatenate([got[nm][0] for nm in carriers], axis=1)
        return _layer_args(w_in_of(rows), conv_full[1], a_log[1], dt_bias[1], sinks[1], norm_w[1],
                           w_out_of(got["swa"][0]))

    def pack_grads(g):
        gin = _from_kernel_order(*g["w_in"]).reshape(N_SHARD, IN_SHARD, D_MODEL)
        gin = jnp.pad(gin, ((0, 0), (0, IN_PAD - IN_SHARD), (0, 0)))
        return jnp.concatenate([gin, g["w_out"].reshape(N_SHARD, OUT_SHARD, D_MODEL)], axis=1).astype(BF16)

    packed = {}

    def reduce1(grads1):
        packed[1] = pack_grads(grads1)
        half_rows = packed[1].shape[1] // 2
        return dict(dn_chunk=_reduce_scatter_exchange(packed[1], 0, half_rows),
                    swa=_reduce_scatter_exchange(packed[1], half_rows, half_rows))

    def reduce0(grads0, grads1, loss_tile):
        g0 = pack_grads(grads0)
        from_sibling = _run_exchange(_pair_window_exchange(g0), name="pair_reduce_0")[0]
        packed[0] = _pair_add(g0, from_sibling, ci, name="pair_add_0")
        gsmall = _pack([jnp.stack([g[nm] for g in (grads0, grads1)]) for nm in SMALL] + [loss_tile[0, 0:1]], D_MODEL)
        return _chip_scatter_exchange(packed[0], gsmall)

    _, dx, grads, landed1, (landed0, landed_small) = _local_step(
        x[0], loss_target[0], args0, args1, ln_g, ln_b, gathers=gathers, reduce1=reduce1, reduce0=reduce0)

    small_shapes = [(DEPTH,) + grads[0][nm].shape for nm in SMALL]
    halves = [_sum_chips(packed[0], landed0, me, tc=2 * LANE, name="reduce_sum_0"),
              _sum_scatter(packed[1], [landed1["dn_chunk"][0], landed1["swa"][0]], me, ci, tc=2 * LANE,
                           name="reduce_sum_1")]
    s_small = _sum_slots(landed_small, name="reduce_sum_small")
    others = _run_exchange(_share_exchange(halves), name="pair_share")
    full = [jnp.where(ci == 0, jnp.concatenate([mine, other], axis=1), jnp.concatenate([other, mine], axis=1))
            for mine, other in zip(halves, others)]
    grad_in_layers = [f[:IN_SHARD] for f in full]
    grad_out = jnp.stack([f[IN_PAD:] for f in full])
    out_blk = (1, OUT_SHARD, D_MODEL)
    *small_grads, loss = _unpack(s_small, small_shapes + [()])
    gs = dict(zip(SMALL, small_grads))
    gs["conv_w"] = lax.dynamic_slice_in_dim(gs["conv_w"], me * CONV_SHARD, CONV_SHARD, axis=2)

    grad_in_t = _interleave_layers(grad_in_layers, tc=2 * LANE, name="grad_in_layers")
    d_in, nm_in, nv_in = (from_t(o) for o in _adamw(to_t(w_in), grad_in_t, to_t(m_w_in), to_t(v_w_in),
                                                    (IN_SHARD // 6, DEPTH, D_MODEL), name="adamw_in"))
    grad_in = from_t(grad_in_t)
    d_out, nm_out, nv_out = _adamw(w_out, grad_out, m_w_out, v_w_out, out_blk, name="adamw_out")
    ws = dict(conv_w=conv_w, a_log=a_log, dt_bias=dt_bias, norm_w=norm_w, sinks=sinks, ln_g=ln_g, ln_b=ln_b)
    ms = dict(conv_w=m_conv_w, a_log=m_a_log, dt_bias=m_dt_bias, norm_w=m_norm_w, sinks=m_sinks, ln_g=m_ln_g, ln_b=m_ln_b)
    vs = dict(conv_w=v_conv_w, a_log=v_a_log, dt_bias=v_dt_bias, norm_w=v_norm_w, sinks=v_sinks, ln_g=v_ln_g, ln_b=v_ln_b)
    d_s, nm_s, nv_s = (dict(zip(SMALL, o)) for o in _adamw_small(*[[d[nm] for nm in SMALL] for d in (ws, gs, ms, vs)],
                                                                 name="adamw_small"))

    def in_order(big_in, small, big_out):
        return (big_in, small["conv_w"], small["a_log"], small["dt_bias"], small["norm_w"], small["sinks"], big_out,
                small["ln_g"], small["ln_b"])

    return (loss, dx[None], *in_order(grad_in, gs, grad_out), *in_order(d_in, d_s, d_out),
            *in_order(nm_in, nm_s, nm_out), *in_order(nv_in, nv_s, nv_out))
```

```python
import math

import jax
import jax.numpy as jnp
from jax import lax
from jax.experimental import pallas as pl
from jax.experimental.pallas import tpu as pltpu

F32 = jnp.float32
BF16 = jnp.bfloat16
HI = lax.Precision.HIGHEST

D_MODEL = 1024
DEPTH = 2
A_HEADS = 4
A_HEAD_DIM = 128
A_WIDTH = 512
CONV_K = 4
CHUNK = 64
B_Q_HEADS = 8
B_KV_HEADS = 2
B_HEAD_DIM = 64
B_GROUP = 4
B_WIDTH = 512
B_KV_WIDTH = 128
BLOCK = 128
IN_COLS = 3336
DEEPNORM_ALPHA = (2 * DEPTH) ** 0.25
LN_EPS = 1e-5
RMS_EPS = 1e-6
L2_EPS = 1e-6
ADAM_LR = 0.001
ADAM_B1 = 0.9
ADAM_B2 = 0.999
ADAM_EPS = 1e-08
ADAM_WD = 0.01
ADAM_STEP = 10

N_SHARD = 4
IN_SHARD = IN_COLS // N_SHARD
OUT_SHARD = D_MODEL // N_SHARD
CONV_SHARD = 3 * A_WIDTH // N_SHARD
IN_PAD = -(-IN_SHARD // 96) * 96

P_COLS = 3456
C_PRE = 0
C_ZA = 1536
C_QB = 2048
C_ZB = 2560
C_KB = 3072
C_VB = 3200
C_BG = 3328
DH_MAIN = C_KB
LANE = 128
SUBLANE = 8
HALO = 16
VMEM_LIMIT = 56 * 1024 * 1024
ALIBI = tuple(2.0 ** (-8.0 * (h + 1) / B_Q_HEADS) for h in range(B_Q_HEADS))
NEG = -1e30


def _cp(*sem):
    return pltpu.CompilerParams(dimension_semantics=sem, vmem_limit_bytes=VMEM_LIMIT)


def _dot(a, b):
    return jnp.dot(a.astype(BF16), b.astype(BF16), preferred_element_type=F32)


def _dot_nt(a, b):
    return lax.dot_general(a.astype(BF16), b.astype(BF16), (((1,), (1,)), ((), ())),
                           preferred_element_type=F32)


def _dot_tn(a, b):
    return lax.dot_general(a.astype(BF16), b.astype(BF16), (((0,), (0,)), ((), ())),
                           preferred_element_type=F32)


def _dot_hi(a, b):
    return jnp.dot(a, b, precision=HI, preferred_element_type=F32)


def _sigmoid(x):
    return jax.nn.sigmoid(x)


def _silu(x):
    return x * _sigmoid(x)


def _silu_and_grad(x):
    s = _sigmoid(x)
    return x * s, s * (1.0 + x * (1.0 - s))


def _softplus(x):
    return jnp.maximum(x, 0.0) + jnp.log(1.0 + jnp.exp(-jnp.abs(x)))


def _shift_down(cur, before, s):
    if s == 0:
        return cur
    r = pltpu.roll(cur, s, 0)
    rb = pltpu.roll(before, s, 0)
    row = lax.broadcasted_iota(jnp.int32, before.shape, 0)
    head = jnp.where(row < s, rb, r[0:SUBLANE])
    return jnp.concatenate([head, r[SUBLANE:]], axis=0)


def _shift_up(cur, after, s):
    if s == 0:
        return cur
    n = cur.shape[0]
    r = pltpu.roll(cur, n - s, 0)
    ra = pltpu.roll(after, SUBLANE - s, 0)
    row = lax.broadcasted_iota(jnp.int32, after.shape, 0)
    tail = jnp.where(row >= SUBLANE - s, ra, r[n - SUBLANE:])
    return jnp.concatenate([r[:n - SUBLANE], tail], axis=0)


def _conv_fwd(cur, before, w):
    acc = cur * w[CONV_K - 1:CONV_K, :]
    for s in range(1, CONV_K):
        acc = acc + _shift_down(cur, before, s) * w[CONV_K - 1 - s:CONV_K - s, :]
    return acc


def _matmul_nt(a, bt, *, tm, name, carry=None):
    m, k = a.shape
    n = bt.shape[0]
    c_ins, c_in_specs, c_out_specs, c_outs, c_scratch = _carry_specs(carry)

    def body(*refs):
        a_ref, b_ref, o_ref = _carried(carry, refs, 2, 1, m // tm)
        o_ref[...] = _dot_nt(a_ref[...], b_ref[...]).astype(o_ref.dtype)

    outs = pl.pallas_call(
        body, name=name, grid=(m // tm,),
        in_specs=[pl.BlockSpec((tm, k), lambda i: (i, 0)), pl.BlockSpec((n, k), lambda i: (0, 0))] + c_in_specs,
        out_specs=[pl.BlockSpec((tm, n), lambda i: (i, 0))] + c_out_specs,
        out_shape=[jax.ShapeDtypeStruct((m, n), BF16)] + c_outs,
        scratch_shapes=c_scratch,
        compiler_params=_cp("arbitrary"))(a, bt, *c_ins)
    return outs[0], outs[1:]


def _dn_pre(h, conv_w, par, *, tt, name, carry=None):
    t = h.shape[0]
    cw = 3 * A_WIDTH
    hb = tt // HALO

    c_ins, c_in_specs, c_out_specs, c_outs, c_scratch = _carry_specs(carry)

    def body(*refs):
        (pre_ref, halo_ref, bgi_ref, cw_ref, par_ref,
         q_ref, k_ref, v_ref, bg_ref, bgt_ref, c_ref) = _carried(carry, refs, 5, 6, t // tt)
        i = pl.program_id(0)
        cur = pre_ref[...].astype(F32)
        before = jnp.where(i > 0, halo_ref[...].astype(F32)[HALO - SUBLANE:], 0.0)
        conv = _conv_fwd(cur, before, cw_ref[...])
        c_ref[...] = conv
        s = _silu(conv)
        for hd in range(A_HEADS):
            sl = slice(hd * LANE, (hd + 1) * LANE)
            tq = s[:, hd * LANE:(hd + 1) * LANE]
            q_ref[:, sl] = tq * (lax.rsqrt(jnp.sum(tq * tq, -1, keepdims=True) + L2_EPS) * (A_HEAD_DIM ** -0.5))
            tk = s[:, A_WIDTH + hd * LANE:A_WIDTH + (hd + 1) * LANE]
            k_ref[:, sl] = tk * lax.rsqrt(jnp.sum(tk * tk, -1, keepdims=True) + L2_EPS)
        v_ref[...] = s[:, 2 * A_WIDTH:]
        raw = bgi_ref[...].astype(F32)
        lane = lax.broadcasted_iota(jnp.int32, raw.shape, 1)
        is_a = (lane >= A_HEADS) & (lane < 2 * A_HEADS)
        g = jnp.where(is_a, -jnp.exp(par_ref[0:1, :]) * _softplus(raw + par_ref[1:2, :]), 0.0)
        gc = _dot_hi(_chunk_tri(tt, lower=True), g)
        bg = jnp.where(lane < A_HEADS, _sigmoid(raw), gc)
        bg_ref[...] = bg
        bgt_ref[...] = jnp.transpose(bg)[0:SUBLANE, :]

    wide = jax.ShapeDtypeStruct((t, A_WIDTH), F32)
    outs = pl.pallas_call(
        body, name=name, grid=(t // tt,),
        in_specs=[pl.BlockSpec((tt, cw), lambda i: (i, 0)),
                  pl.BlockSpec((HALO, cw), lambda i: (jnp.maximum(i * hb - 1, 0), 0)),
                  pl.BlockSpec((tt, LANE), lambda i: (i, C_BG // LANE)),
                  pl.BlockSpec((CONV_K, cw), lambda i: (0, 0)),
                  pl.BlockSpec((SUBLANE, LANE), lambda i: (0, 0))] + c_in_specs,
        out_specs=[pl.BlockSpec((tt, A_WIDTH), lambda i: (i, 0))] * 3
        + [pl.BlockSpec((tt, LANE), lambda i: (i, 0)), pl.BlockSpec((SUBLANE, tt), lambda i: (0, i)),
           pl.BlockSpec((tt, cw), lambda i: (i, 0))] + c_out_specs,
        out_shape=[wide, wide, wide, jax.ShapeDtypeStruct((t, LANE), F32),
                   jax.ShapeDtypeStruct((SUBLANE, t), F32), jax.ShapeDtypeStruct((t, cw), F32)] + c_outs,
        scratch_shapes=c_scratch,
        compiler_params=_cp("arbitrary"))(h, h, h, conv_w, par, *c_ins)
    return outs[:6], outs[6:]


def _chunk_tri(n, lower):
    r = lax.broadcasted_iota(jnp.int32, (n, n), 0)
    c = lax.broadcasted_iota(jnp.int32, (n, n), 1)
    shift = CHUNK.bit_length() - 1
    same = jnp.right_shift(r, shift) == jnp.right_shift(c, shift)
    return (same & ((c <= r) if lower else (c >= r))).astype(F32)


def _chunk_masks():
    r = lax.broadcasted_iota(jnp.int32, (CHUNK, CHUNK), 0)
    c = lax.broadcasted_iota(jnp.int32, (CHUNK, CHUNK), 1)
    return r >= c, r > c, r == c


def _split(a):
    hi = a.astype(BF16)
    return hi, (a - hi.astype(F32)).astype(BF16)


def _dot3(a, b):
    (ah, al), (bh, bl) = a, b
    d = lambda p, q: jnp.dot(p, q, preferred_element_type=F32)
    return d(ah, bh) + (d(ah, bl) + d(al, bh))


def _tri_inv_many(a_list, eye):
    d = lambda p, q: jnp.dot(p.astype(BF16), q.astype(BF16), preferred_element_type=F32)
    r = lax.broadcasted_iota(jnp.int32, (CHUNK, CHUNK), 0)
    c = lax.broadcasted_iota(jnp.int32, (CHUNK, CHUNK), 1)
    same = lambda b: jnp.right_shift(r, b.bit_length() - 1) == jnp.right_shift(c, b.bit_length() - 1)
    x = [jnp.where(same(8), -a, 0.0) for a in a_list]
    tm = [eye + xi for xi in x]
    for _ in range(2):
        x = [d(xi, xi) for xi in x]
        tm = [t + d(t, xi) for t, xi in zip(tm, x)]
    for b in (16, 32, 64):
        low = [jnp.where(same(b) & ~same(b // 2), a, 0.0) for a in a_list]
        tm = [t - d(t, d(lo, t)) for t, lo in zip(tm, low)]
    res = [eye - _dot3(_split(eye + a), _split(t)) for a, t in zip(a_list, tm)]
    return [t + d(t, rs) for t, rs in zip(tm, res)]


def _chunk_gates(bg_v, bgt_v, hd):
    return (bg_v[:, hd:hd + 1], bg_v[:, A_HEADS + hd:A_HEADS + hd + 1],
            None if bgt_v is None else bgt_v[A_HEADS + hd:A_HEADS + hd + 1, :])


WY_ROWS = 512
SCAN_ROWS = 512
WY_GROUP = 8


def _dn_wy(q, k, v, bg, bgt, *, name, carry=None):
    t = q.shape[0]
    rows = WY_ROWS

    c_ins, c_in_specs, c_out_specs, c_outs, c_scratch = _carry_specs(carry)

    def body(*refs):
        q_ref, k_ref, v_ref, bg_ref, bgt_ref, u_ref, w_ref, tm_ref, qk_ref = _carried(carry, refs, 5, 4, t // rows)
        causal, strict, diag = _chunk_masks()
        eye = diag.astype(F32)
        for c0 in range(0, rows // CHUNK, WY_GROUP):
            items = [(c, hd) for c in range(c0, c0 + WY_GROUP) for hd in range(A_HEADS)]
            rs = lambda c: slice(c * CHUNK, (c + 1) * CHUNK)
            sl = lambda hd: slice(hd * LANE, (hd + 1) * LANE)
            hs = lambda hd: slice(hd * CHUNK, (hd + 1) * CHUNK)
            gates = [_chunk_gates(bg_ref[rs(c), :], bgt_ref[:, rs(c)], hd) for c, hd in items]
            dms = [jnp.exp(jnp.where(causal, gcol - grow, NEG)) for _, gcol, grow in gates]
            kbs = [k_ref[rs(c), sl(hd)] * g[0] for (c, hd), g in zip(items, gates)]
            a_list = [jnp.where(strict, _dot_nt(kb, k_ref[rs(c), sl(hd)]) * dm, 0.0)
                      for (c, hd), kb, dm in zip(items, kbs, dms)]
            for (c, hd), dm in zip(items, dms):
                qk_ref[rs(c), hs(hd)] = jnp.where(
                    causal, _dot_nt(q_ref[rs(c), sl(hd)], k_ref[rs(c), sl(hd)]) * dm, 0.0)
            tms = _tri_inv_many(a_list, eye)
            for (c, hd), g, kb, tmat in zip(items, gates, kbs, tms):
                tm_ref[rs(c), hs(hd)] = tmat
                u_ref[rs(c), sl(hd)] = _dot(tmat, v_ref[rs(c), sl(hd)] * g[0])
                w_ref[rs(c), sl(hd)] = _dot(tmat, kb * jnp.exp(g[1])).astype(BF16)

    blk = pl.BlockSpec((rows, A_WIDTH), lambda i: (i, 0))
    half = pl.BlockSpec((rows, A_HEADS * CHUNK), lambda i: (i, 0))
    outs = pl.pallas_call(
        body, name=name, grid=(t // rows,),
        in_specs=[blk, blk, blk, pl.BlockSpec((rows, LANE), lambda i: (i, 0)),
                  pl.BlockSpec((SUBLANE, rows), lambda i: (0, i))] + c_in_specs,
        out_specs=[blk, blk, half, half] + c_out_specs,
        out_shape=[jax.ShapeDtypeStruct((t, A_WIDTH), F32), jax.ShapeDtypeStruct((t, A_WIDTH), BF16),
                   jax.ShapeDtypeStruct((t, A_HEADS * CHUNK), F32),
                   jax.ShapeDtypeStruct((t, A_HEADS * CHUNK), F32)] + c_outs,
        scratch_shapes=c_scratch,
        compiler_params=_cp("arbitrary"))(q, k, v, bg, bgt, *c_ins)
    return outs[:4], outs[4:]


def _dn_scan_fwd(q, k, u, w, qk, bg, *, name, carry=None):
    t = q.shape[0]
    rows = SCAN_ROWS
    per = rows // CHUNK
    c_ins, c_in_specs, c_out_specs, c_outs, c_scratch = _carry_specs(carry)

    def body(*refs):
        q_ref, k_ref, u_ref, w_ref, qk_ref, bg_ref, o_ref, vn_ref, s_ref, state = _carried(carry, refs, 6, 3, t // rows)

        @pl.when(pl.program_id(0) == 0)
        def _():
            state[...] = jnp.zeros_like(state)

        heads = range(A_HEADS)
        sl = lambda hd: slice(hd * LANE, (hd + 1) * LANE)
        s_cur = [state[hd] for hd in heads]
        for c in range(per):
            rs = slice(c * CHUNK, (c + 1) * CHUNK)
            bg_v = bg_ref[rs, :]
            gcols = [_chunk_gates(bg_v, None, hd)[1] for hd in heads]
            glasts = [gc[CHUNK - 1:CHUNK, :] for gc in gcols]
            for hd in heads:
                s_ref[c, hd] = s_cur[hd].astype(BF16)
            vns = [u_ref[rs, sl(hd)] - _dot(w_ref[rs, sl(hd)], s_cur[hd]) for hd in heads]
            qss = [_dot(q_ref[rs, sl(hd)] * jnp.exp(gcols[hd]), s_cur[hd]) for hd in heads]
            s_cur = [s_cur[hd] * jnp.exp(glasts[hd])
                     + _dot_tn(k_ref[rs, sl(hd)] * jnp.exp(glasts[hd] - gcols[hd]), vns[hd]) for hd in heads]
            for hd in heads:
                vn_ref[rs, sl(hd)] = vns[hd]
                o_ref[rs, sl(hd)] = qss[hd] + _dot(qk_ref[rs, hd * CHUNK:(hd + 1) * CHUNK], vns[hd])
        for hd in heads:
            state[hd] = s_cur[hd]

    blk = pl.BlockSpec((rows, A_WIDTH), lambda i: (i, 0))
    half = pl.BlockSpec((rows, A_HEADS * CHUNK), lambda i: (i, 0))
    wide = jax.ShapeDtypeStruct((t, A_WIDTH), F32)
    outs = pl.pallas_call(
        body, name=name, grid=(t // rows,),
        in_specs=[blk, blk, blk, blk, half, pl.BlockSpec((rows, LANE), lambda i: (i, 0))] + c_in_specs,
        out_specs=[blk, blk, pl.BlockSpec((per, A_HEADS, LANE, LANE), lambda i: (i, 0, 0, 0))] + c_out_specs,
        out_shape=[wide, wide, jax.ShapeDtypeStruct((t // CHUNK, A_HEADS, LANE, LANE), BF16)] + c_outs,
        scratch_shapes=[pltpu.VMEM((A_HEADS, LANE, LANE), F32)] + c_scratch,
        compiler_params=_cp("arbitrary"))(q, k, u, w, qk, bg, *c_ins)
    return outs[:3], outs[3:]


def _stack_heads(ref, hk):
    return jnp.concatenate([ref[:, h * B_HEAD_DIM:(h + 1) * B_HEAD_DIM].astype(F32)
                            for h in range(hk * B_GROUP, (hk + 1) * B_GROUP)], axis=0)


def _swa_window():
    qi = lax.broadcasted_iota(jnp.int32, (BLOCK, BLOCK), 0)
    kj = lax.broadcasted_iota(jnp.int32, (BLOCK, BLOCK), 1)
    dist = jnp.where(kj > qi, qi + BLOCK - kj, qi - kj).astype(F32)
    rows = lax.broadcasted_iota(jnp.int32, (B_GROUP * BLOCK, BLOCK), 0)
    cols = lax.broadcasted_iota(jnp.int32, (B_GROUP * BLOCK, BLOCK), 1)
    return cols > jnp.bitwise_and(rows, BLOCK - 1), dist


def _swa_group_probs(q_ref, sk_ref, kp, kc, vp, vc, n_blk):
    hks = range(B_KV_HEADS)
    heads = lambda hk: range(hk * B_GROUP, (hk + 1) * B_GROUP)
    ksl = lambda hk: slice(hk * B_HEAD_DIM, (hk + 1) * B_HEAD_DIM)
    upper, dist = _swa_window()
    no_prev = jnp.where(n_blk > 0, 0.0, NEG)
    ones = jnp.ones((BLOCK, B_HEAD_DIM), BF16)
    with_ones = lambda v, hk: jnp.concatenate([v[:, ksl(hk)].astype(BF16), ones], axis=1)
    qs = [_stack_heads(q_ref, hk) * (B_HEAD_DIM ** -0.5) for hk in hks]
    sink = [jnp.concatenate([jnp.broadcast_to(sk_ref[h:h + 1, 0:1], (BLOCK, 1)) for h in heads(hk)], axis=0)
            for hk in hks]
    s = [jnp.where(upper, _dot_nt(qs[hk], kp[:, ksl(hk)]) + no_prev, _dot_nt(qs[hk], kc[:, ksl(hk)]))
         - jnp.concatenate([ALIBI[h] * dist for h in heads(hk)], axis=0) for hk in hks]
    m = [jnp.maximum(jnp.max(s[hk], axis=-1, keepdims=True), sink[hk]) for hk in hks]
    p = [jnp.exp(s[hk] - m[hk]) for hk in hks]
    p_up = [jnp.where(upper, p[hk], 0.0) for hk in hks]
    oe = [jnp.dot(p_up[hk].astype(BF16), with_ones(vp, hk), preferred_element_type=F32)
          + jnp.dot((p[hk] - p_up[hk]).astype(BF16), with_ones(vc, hk), preferred_element_type=F32) for hk in hks]
    ps = [jnp.exp(sink[hk] - m[hk]) for hk in hks]
    inv = [1.0 / (oe[hk][:, B_HEAD_DIM:B_HEAD_DIM + 1] + ps[hk]) for hk in hks]
    return upper, [(qs[hk], p[hk] * inv[hk], ps[hk] * inv[hk], oe[hk][:, :B_HEAD_DIM] * inv[hk]) for hk in hks]


def _swa_specs():
    qspec = lambda c0: pl.BlockSpec((BLOCK, B_WIDTH), lambda i: (i, c0 // B_WIDTH))
    cur = lambda c0: pl.BlockSpec((BLOCK, LANE), lambda i: (i, c0 // LANE))
    prev = lambda c0: pl.BlockSpec((BLOCK, LANE), lambda i: (jnp.maximum(i - 1, 0), c0 // LANE))
    return qspec, cur, prev


def _carried(carry, refs, n_in, n_out, steps):
    if carry is None:
        return refs
    ci, co = len(carry.ins), len(carry.outs)
    own = refs[:n_in] + refs[n_in + ci:n_in + ci + n_out] + refs[n_in + ci + n_out + co:len(refs) - 3]
    parts = refs[n_in:n_in + ci], refs[n_in + ci + n_out:n_in + ci + n_out + co], refs[len(refs) - 3:]

    @pl.when(pl.program_id(0) == 0)
    def _():
        carry.start(*parts)

    @pl.when(pl.program_id(0) == steps - 1)
    def _():
        carry.finish(*parts)

    return own


def _carry_specs(carry):
    if carry is None:
        return [], [], [], [], []
    return (list(carry.ins), [_ANY] * len(carry.ins), [_ANY] * len(carry.outs), list(carry.outs), carry.scratch())


def _swa_fwd(h, sinks_b, *, name, carry=None):
    t = h.shape[0]
    qspec, cur, prev = _swa_specs()
    c_ins, c_in_specs, c_out_specs, c_outs, c_scratch = _carry_specs(carry)

    def body(*refs):
        q_ref, kc_ref, kp_ref, vc_ref, vp_ref, sk_ref, o_ref, p_ref, ps_ref = _carried(carry, refs, 6, 3, t // BLOCK)
        n_blk = pl.program_id(0)
        _, groups = _swa_group_probs(q_ref, sk_ref, kp_ref[...], kc_ref[...], vp_ref[...], vc_ref[...], n_blk)
        lane = lax.broadcasted_iota(jnp.int32, (BLOCK, LANE), 1)
        sink_probs = jnp.zeros((BLOCK, LANE), F32)
        for hk, (_, p, ps, o) in enumerate(groups):
            for g in range(B_GROUP):
                hq = hk * B_GROUP + g
                rows = slice(g * BLOCK, (g + 1) * BLOCK)
                o_ref[:, hq * B_HEAD_DIM:(hq + 1) * B_HEAD_DIM] = o[rows]
                p_ref[:, hq * BLOCK:(hq + 1) * BLOCK] = p[rows].astype(BF16)
                sink_probs = sink_probs + jnp.where(lane == hq, ps[rows], 0.0)
        ps_ref[...] = sink_probs

    row = lambda w: pl.BlockSpec((BLOCK, w), lambda i: (i, 0))
    outs = pl.pallas_call(
        body, name=name, grid=(t // BLOCK,),
        in_specs=[qspec(C_QB), cur(C_KB), prev(C_KB), cur(C_VB), prev(C_VB),
                  pl.BlockSpec((B_Q_HEADS, LANE), lambda i: (0, 0))] + c_in_specs,
        out_specs=[row(B_WIDTH), row(B_Q_HEADS * BLOCK), row(LANE)] + c_out_specs,
        out_shape=[jax.ShapeDtypeStruct((t, B_WIDTH), F32), jax.ShapeDtypeStruct((t, B_Q_HEADS * BLOCK), BF16),
                   jax.ShapeDtypeStruct((t, LANE), F32)] + c_outs,
        scratch_shapes=c_scratch,
        compiler_params=_cp("arbitrary"))(h, h, h, h, h, sinks_b, *c_ins)
    return outs[:3], outs[3:]


def _rms_gate(o, za, nw):
    outs = []
    for hd in range(A_HEADS):
        oh = o[:, hd * LANE:(hd + 1) * LANE]
        r = lax.rsqrt(jnp.mean(oh * oh, -1, keepdims=True) + RMS_EPS)
        outs.append(oh * r * nw)
    return jnp.concatenate(outs, axis=1) * _silu(za)


def _out_ln(x, oa, ob, h, norm_w, w_out, ln_g, ln_b, *, tm, name, target=None):
    t = x.shape[0]
    last = target is not None

    def body(*refs):
        x_ref, oa_ref, ob_ref, za_ref, zb_ref, nw_ref, w_ref, g_ref, b_ref = refs[:9]
        xn_ref, mx_ref, r_ref = refs[9 + last:12 + last]
        ya = _rms_gate(oa_ref[...], za_ref[...].astype(F32), nw_ref[...])
        yb = ob_ref[...] * _silu(zb_ref[...].astype(F32))
        mixed = jnp.concatenate([ya, yb], axis=1).astype(BF16)
        mx_ref[...] = mixed
        r = DEEPNORM_ALPHA * x_ref[...] + jnp.dot(mixed, w_ref[...], preferred_element_type=F32)
        r_ref[...] = r
        mu = jnp.mean(r, -1, keepdims=True)
        xc = r - mu
        var = jnp.mean(xc * xc, -1, keepdims=True)
        xn = xc * lax.rsqrt(var + LN_EPS) * g_ref[...] + b_ref[...]
        if not last:
            xn_ref[...] = xn
            return
        loss_ref = refs[13]

        @pl.when(pl.program_id(0) == 0)
        def _():
            loss_ref[...] = jnp.zeros_like(loss_ref)

        err = xn - refs[9][...]
        xn_ref[...] = err * (1.0 / D_MODEL)
        loss_ref[...] += 0.5 / D_MODEL * jnp.sum(err * err)

    row = lambda w, c: pl.BlockSpec((tm, w), lambda i: (i, c))
    full = lambda a, b: pl.BlockSpec((a, b), lambda i: (0, 0))
    wide = jax.ShapeDtypeStruct((t, D_MODEL), F32)
    return pl.pallas_call(
        body, name=name, grid=(t // tm,),
        in_specs=[row(D_MODEL, 0), row(A_WIDTH, 0), row(B_WIDTH, 0), row(A_WIDTH, C_ZA // A_WIDTH),
                  row(B_WIDTH, C_ZB // B_WIDTH), full(1, LANE), full(D_MODEL, D_MODEL), full(1, D_MODEL),
                  full(1, D_MODEL)] + [row(D_MODEL, 0)] * last,
        out_specs=[row(D_MODEL, 0), row(D_MODEL, 0), row(D_MODEL, 0)] + [full(SUBLANE, LANE)] * last,
        out_shape=[wide, jax.ShapeDtypeStruct((t, D_MODEL), BF16), wide]
        + [jax.ShapeDtypeStruct((SUBLANE, LANE), F32)] * last,
        compiler_params=_cp("arbitrary" if last else "parallel"))(
        x, oa, ob, h, h, norm_w, w_out, ln_g, ln_b, *([target] if last else []))


def _layer_fwd(x, wt, conv_w, par, sinks_b, norm_w, w_out_bf, ln_g, ln_b, l, carries=None, target=None):
    carries = carries or {}
    h, got_in = _matmul_nt(x, wt, tm=512, name=f"in_proj_{l}", carry=carries.get("in_proj"))
    if callable(w_out_bf):
        w_out_bf = w_out_bf(got_in)
    (q, k, v, bg, bgt, conv), got_pre = _dn_pre(h, conv_w, par, tt=512, name=f"dn_pre_{l}",
                                                carry=carries.get("dn_pre"))
    (u, w, tmat, qk), got_wy = _dn_wy(q, k, v, bg, bgt, name=f"dn_wy_{l}", carry=carries.get("dn_wy"))
    (oa, vn, s_all), got_scan = _dn_scan_fwd(q, k, u, w, qk, bg, name=f"dn_scan_{l}", carry=carries.get("dn_scan"))
    (ob, swa_p, swa_ps), got_swa = _swa_fwd(h, sinks_b, name=f"swa_fwd_{l}", carry=carries.get("swa"))
    xn, mixed, r, *loss = _out_ln(x, oa, ob, h, norm_w, w_out_bf, ln_g, ln_b, tm=512, name=f"out_ln_{l}", target=target)
    if loss:
        xn = (xn, loss[0])
    res = dict(x=x, h=h, q=q, k=k, v=v, bg=bg, bgt=bgt, w=w, tmat=tmat, qk=qk, vn=vn, oa=oa, s_all=s_all,
               mixed=mixed, r=r, w_out=w_out_bf, ob=ob, swa_p=swa_p, swa_ps=swa_ps, conv=conv)
    return xn, res, dict(in_proj=got_in, dn_pre=got_pre, dn_wy=got_wy, dn_scan=got_scan, swa=got_swa)


def _ln_out_bwd(dxn, r, mixed, ln_g, w_out, *, tm, name):
    t = dxn.shape[0]

    def body(dxn_ref, r_ref, mx_ref, g_ref, w_ref, dr_ref, dm_ref, dw_ref, dg_ref, db_ref):
        @pl.when(pl.program_id(0) == 0)
        def _():
            dw_ref[...] = jnp.zeros_like(dw_ref)
            dg_ref[...] = jnp.zeros_like(dg_ref)
            db_ref[...] = jnp.zeros_like(db_ref)

        rr = r_ref[...]
        xc = rr - jnp.mean(rr, -1, keepdims=True)
        rstd = lax.rsqrt(jnp.mean(xc * xc, -1, keepdims=True) + LN_EPS)
        xhat = xc * rstd
        dxn_v = dxn_ref[...]
        dxh = dxn_v * g_ref[...]
        dr = rstd * (dxh - jnp.mean(dxh, -1, keepdims=True) - xhat * jnp.mean(dxh * xhat, -1, keepdims=True))
        dr_ref[...] = dr
        dg_ref[...] += jnp.sum(dxn_v * xhat, axis=0, keepdims=True)
        db_ref[...] += jnp.sum(dxn_v, axis=0, keepdims=True)
        drb = dr.astype(BF16)
        dm_ref[...] = _dot_nt(drb, w_ref[...])
        dw_ref[...] += _dot_tn(mx_ref[...], drb)

    row = pl.BlockSpec((tm, D_MODEL), lambda i: (i, 0))
    full = lambda a, b: pl.BlockSpec((a, b), lambda i: (0, 0))
    big = jax.ShapeDtypeStruct((t, D_MODEL), F32)
    vec = jax.ShapeDtypeStruct((1, D_MODEL), F32)
    return pl.pallas_call(
        body, name=name, grid=(t // tm,),
        in_specs=[row, row, row, full(1, D_MODEL), full(D_MODEL, D_MODEL)],
        out_specs=[row, row, full(D_MODEL, D_MODEL), full(1, D_MODEL), full(1, D_MODEL)],
        out_shape=[big, big, jax.ShapeDtypeStruct((D_MODEL, D_MODEL), F32), vec, vec],
        compiler_params=_cp("arbitrary"))(dxn, r, mixed, ln_g, w_out)


def _dn_post_bwd(dm, oa, h, norm_w, *, tm, name):
    t = oa.shape[0]

    def body(dy_ref, o_ref, za_ref, nw_ref, do_ref, dza_ref, dnw_ref):
        @pl.when(pl.program_id(0) == 0)
        def _():
            dnw_ref[...] = jnp.zeros_like(dnw_ref)

        nw = nw_ref[...]
        dnw = jnp.zeros_like(nw)
        for hd in range(A_HEADS):
            sl = slice(hd * LANE, (hd + 1) * LANE)
            oh, za, dy = o_ref[:, sl], za_ref[:, sl].astype(F32), dy_ref[:, sl]
            rs = lax.rsqrt(jnp.mean(oh * oh, -1, keepdims=True) + RMS_EPS)
            nrm = oh * rs
            gate, dgate = _silu_and_grad(za)
            dza_ref[:, sl] = dy * nrm * nw * dgate
            dn = dy * gate
            dnw = dnw + jnp.sum(dn * nrm, axis=0, keepdims=True)
            dnn = dn * nw
            do_ref[:, sl] = rs * dnn - oh * (rs * rs * rs) * jnp.mean(dnn * oh, -1, keepdims=True)
        dnw_ref[...] += dnw

    row = lambda c: pl.BlockSpec((tm, A_WIDTH), lambda i: (i, c))
    wide = jax.ShapeDtypeStruct((t, A_WIDTH), F32)
    return pl.pallas_call(
        body, name=name, grid=(t // tm,),
        in_specs=[row(0), row(0), row(C_ZA // A_WIDTH), pl.BlockSpec((1, LANE), lambda i: (0, 0))],
        out_specs=[row(0), row(C_ZA // A_WIDTH), pl.BlockSpec((1, LANE), lambda i: (0, 0))],
        out_shape=[wide, jax.ShapeDtypeStruct((t, DH_MAIN), F32), jax.ShapeDtypeStruct((1, LANE), F32)],
        compiler_params=_cp("arbitrary"))(dm, oa, h, norm_w)


def _dn_scan_bwd(q, k, w, qk, bg, do, *, name):
    t = q.shape[0]
    rows = SCAN_ROWS
    per = rows // CHUNK
    n = t // rows

    def body(q_ref, k_ref, w_ref, qk_ref, bg_ref, do_ref, dvn_ref, ds_ref, dstate):
        @pl.when(pl.program_id(0) == 0)
        def _():
            dstate[...] = jnp.zeros_like(dstate)

        heads = range(A_HEADS)
        sl = lambda hd: slice(hd * LANE, (hd + 1) * LANE)
        ds_cur = [dstate[hd] for hd in heads]
        for c in reversed(range(per)):
            rs = slice(c * CHUNK, (c + 1) * CHUNK)
            bg_v = bg_ref[rs, :]
            gcols = [_chunk_gates(bg_v, None, hd)[1] for hd in heads]
            glasts = [gc[CHUNK - 1:CHUNK, :] for gc in gcols]
            for hd in heads:
                ds_ref[c, hd] = ds_cur[hd].astype(BF16)
            pdo = [_dot_tn(qk_ref[rs, hd * CHUNK:(hd + 1) * CHUNK], do_ref[rs, sl(hd)]) for hd in heads]
            qdo = [_dot_tn(q_ref[rs, sl(hd)] * jnp.exp(gcols[hd]), do_ref[rs, sl(hd)]) for hd in heads]
            dvns = [pdo[hd] + _dot(k_ref[rs, sl(hd)] * jnp.exp(glasts[hd] - gcols[hd]), ds_cur[hd]) for hd in heads]
            ds_cur = [qdo[hd] + jnp.exp(glasts[hd]) * ds_cur[hd] - _dot_tn(w_ref[rs, sl(hd)], dvns[hd])
                      for hd in heads]
            for hd in heads:
                dvn_ref[rs, sl(hd)] = dvns[hd]
        for hd in heads:
            dstate[hd] = ds_cur[hd]

    blk = pl.BlockSpec((rows, A_WIDTH), lambda i: (n - 1 - i, 0))
    return pl.pallas_call(
        body, name=name, grid=(n,),
        in_specs=[blk, blk, blk, pl.BlockSpec((rows, A_HEADS * CHUNK), lambda i: (n - 1 - i, 0)),
                  pl.BlockSpec((rows, LANE), lambda i: (n - 1 - i, 0)), blk],
        out_specs=[blk, pl.BlockSpec((per, A_HEADS, LANE, LANE), lambda i: (n - 1 - i, 0, 0, 0))],
        out_shape=[jax.ShapeDtypeStruct((t, A_WIDTH), F32),
                   jax.ShapeDtypeStruct((t // CHUNK, A_HEADS, LANE, LANE), BF16)],
        scratch_shapes=[pltpu.VMEM((A_HEADS, LANE, LANE), F32)],
        compiler_params=_cp("arbitrary"))(q, k, w, qk, bg, do)


def _dn_chunk_bwd(q, k, v, vn, tmat, qk, bg, bgt, s_all, ds_all, dvn, do, *, name, carry=None):
    t = q.shape[0]
    rows = WY_ROWS
    per = rows // CHUNK

    c_ins, c_in_specs, c_out_specs, c_outs, c_scratch = _carry_specs(carry)

    def body(*refs):
        (q_ref, k_ref, v_ref, vn_ref, tm_ref, qk_ref, bg_ref, bgt_ref, s_ref, ds_ref, dvn_ref, do_ref,
         dq_ref, dk_ref, dv_ref, dbg_ref, dbgt_ref) = _carried(carry, refs, 12, 5, t // rows)
        causal, strict, _ = _chunk_masks()
        lane = lax.broadcasted_iota(jnp.int32, (CHUNK, LANE), 1)
        rowi = lax.broadcasted_iota(jnp.int32, (CHUNK, 1), 0)
        sub = lax.broadcasted_iota(jnp.int32, (SUBLANE, CHUNK), 0)
        rs = lambda c: slice(c * CHUNK, (c + 1) * CHUNK)
        sl = lambda hd: slice(hd * LANE, (hd + 1) * LANE)
        hs = lambda hd: slice(hd * CHUNK, (hd + 1) * CHUNK)
        for c0 in range(0, per, WY_GROUP):
            items = [(c, hd) for c in range(c0, c0 + WY_GROUP) for hd in range(A_HEADS)]
            at = lambda ref: [ref[rs(c), sl(hd)] for c, hd in items]
            qs, ks, vs, dos, vns, dvns = at(q_ref), at(k_ref), at(v_ref), at(do_ref), at(vn_ref), at(dvn_ref)
            tmhs = [tm_ref[rs(c), hs(hd)] for c, hd in items]
            ps = [qk_ref[rs(c), hs(hd)] for c, hd in items]
            gates = [_chunk_gates(bg_ref[rs(c), :], bgt_ref[:, rs(c)], hd) for c, hd in items]
            betas = [g[0] for g in gates]
            gcols = [g[1] for g in gates]
            dmats = [jnp.exp(jnp.where(causal, g[1] - g[2], NEG)) for g in gates]
            es = [jnp.exp(gc) for gc in gcols]
            glasts = [gc[CHUNK - 1:CHUNK, :] for gc in gcols]
            eks = [jnp.exp(gl - gc) for gl, gc in zip(glasts, gcols)]
            kbs = [kh * b for kh, b in zip(ks, betas)]
            vbs = [vh * b for vh, b in zip(vs, betas)]
            kbes = [kb * e for kb, e in zip(kbs, es)]

            a_s = [jnp.where(strict, _dot_nt(kb, kh) * dm, 0.0) for kb, kh, dm in zip(kbs, ks, dmats)]
            dps = [jnp.where(causal, _dot_nt(doh, vnh), 0.0) for doh, vnh in zip(dos, vns)]
            rows2 = lambda a, b: jnp.concatenate([a, b], axis=0)
            cols2 = lambda a, b: jnp.concatenate([a, b], axis=1)
            by_s = [_dot_nt(rows2(doh, dvnh), s_ref[c, hd]) for doh, dvnh, (c, hd) in zip(dos, dvns, items)]
            dqds = [m[:CHUNK] for m in by_s]
            dws = [-m[CHUNK:] for m in by_s]
            dkds = [_dot_nt(vnh, ds_ref[c, hd]) for vnh, (c, hd) in zip(vns, items)]
            dgts = [jnp.sum(s_ref[c, hd].astype(F32) * ds_ref[c, hd].astype(F32), keepdims=True) for c, hd in items]
            pairs = [cols2(dvnh, dw) for dvnh, dw in zip(dvns, dws)]
            by_t = [_dot_tn(tmh, pr) for tmh, pr in zip(tmhs, pairs)]
            dvbs = [m[:, :LANE] for m in by_t]
            dkbes = [m[:, LANE:] for m in by_t]
            dts = [_dot_nt(pr, cols2(vb, kbe)) for pr, vb, kbe in zip(pairs, vbs, kbes)]
            xs = [_dot_nt(dt, tmh) for dt, tmh in zip(dts, tmhs)]
            das = [jnp.where(strict, -_dot_tn(tmh, x), 0.0) for tmh, x in zip(tmhs, xs)]
            dmas = [da * dm for da, dm in zip(das, dmats)]
            dmps = [dp * dm for dp, dm in zip(dps, dmats)]
            stacked = [rows2(dma, dmp) for dma, dmp in zip(dmas, dmps)]
            by_k = [_dot(st, kh) for st, kh in zip(stacked, ks)]
            dkbs = [m[:CHUNK] + dkbe * e for m, dkbe, e in zip(by_k, dkbes, es)]
            for i, (c, hd) in enumerate(items):
                dq_ref[rs(c), sl(hd)] = by_k[i][CHUNK:] + dqds[i] * es[i]
                dk_ref[rs(c), sl(hd)] = (_dot_tn(stacked[i], rows2(kbs[i], qs[i])) + dkds[i] * eks[i]
                                         + dkbs[i] * betas[i])
                dv_ref[rs(c), sl(hd)] = dvbs[i] * betas[i]
            for c in range(c0, c0 + WY_GROUP):
                acc = jnp.zeros((CHUNK, LANE), F32)
                acc_t = jnp.zeros((SUBLANE, CHUNK), F32)
                for i, (ci, hd) in enumerate(items):
                    if ci != c:
                        continue
                    gmat = das[i] * a_s[i] + dps[i] * ps[i]
                    rk = jnp.sum(dkds[i] * ks[i], -1, keepdims=True) * eks[i]
                    de = jnp.sum(dqds[i] * qs[i] + dkbes[i] * kbs[i], -1, keepdims=True)
                    dglast = jnp.sum(rk, keepdims=True) + dgts[i] * jnp.exp(glasts[i])
                    dgc = (jnp.sum(gmat, -1, keepdims=True) + de * es[i] - rk
                           + jnp.where(rowi == CHUNK - 1, dglast, 0.0))
                    dbeta = jnp.sum(dkbs[i] * ks[i] + dvbs[i] * vs[i], -1, keepdims=True)
                    acc = acc + jnp.where(lane == hd, dbeta, 0.0) + jnp.where(lane == A_HEADS + hd, dgc, 0.0)
                    acc_t = acc_t + jnp.where(sub == A_HEADS + hd, -jnp.sum(gmat, axis=0, keepdims=True), 0.0)
                dbg_ref[rs(c), :] = acc
                dbgt_ref[:, rs(c)] = acc_t

    blk = pl.BlockSpec((rows, A_WIDTH), lambda i: (i, 0))
    half = pl.BlockSpec((rows, A_HEADS * CHUNK), lambda i: (i, 0))
    col = pl.BlockSpec((rows, LANE), lambda i: (i, 0))
    rowf = pl.BlockSpec((SUBLANE, rows), lambda i: (0, i))
    st = pl.BlockSpec((per, A_HEADS, LANE, LANE), lambda i: (i, 0, 0, 0))
    wide = jax.ShapeDtypeStruct((t, A_WIDTH), F32)
    outs = pl.pallas_call(
        body, name=name, grid=(t // rows,),
        in_specs=[blk, blk, blk, blk, half, half, col, rowf, st, st, blk, blk] + c_in_specs,
        out_specs=[blk, blk, blk, col, rowf] + c_out_specs,
        out_shape=[wide, wide, wide, jax.ShapeDtypeStruct((t, LANE), F32),
                   jax.ShapeDtypeStruct((SUBLANE, t), F32)] + c_outs,
        scratch_shapes=c_scratch,
        compiler_params=_cp("arbitrary"))(q, k, v, vn, tmat, qk, bg, bgt, s_all, ds_all, dvn, do, *c_ins)
    return outs[:5], outs[5:]


def _dn_pre_bwd(h, conv, par, dq, dk, dv, dbg, dbgt, *, tt, name):
    t = h.shape[0]
    cw = 3 * A_WIDTH

    def body(conv_ref, bgi_ref, par_ref, dq_ref, dk_ref, dv_ref, dbg_ref, dbgt_ref, dc_ref, dbgi_ref, dpar_ref):
        i = pl.program_id(0)

        @pl.when(i == 0)
        def _():
            dpar_ref[...] = jnp.zeros_like(dpar_ref)

        s, ds = _silu_and_grad(conv_ref[...])
        for hd in range(A_HEADS):
            sl = slice(hd * LANE, (hd + 1) * LANE)
            for base, d_ref, scale in ((0, dq_ref, A_HEAD_DIM ** -0.5), (A_WIDTH, dk_ref, 1.0)):
                csl = slice(base + hd * LANE, base + (hd + 1) * LANE)
                tq = s[:, base + hd * LANE:base + (hd + 1) * LANE]
                dy = d_ref[:, sl]
                rq = lax.rsqrt(jnp.sum(tq * tq, -1, keepdims=True) + L2_EPS)
                dtq = scale * (rq * dy - tq * (rq * rq * rq) * jnp.sum(dy * tq, -1, keepdims=True))
                dc_ref[:, csl] = dtq * ds[:, base + hd * LANE:base + (hd + 1) * LANE]
        dc_ref[:, 2 * A_WIDTH:] = dv_ref[...] * ds[:, 2 * A_WIDTH:]
        raw = bgi_ref[...].astype(F32)
        lane = lax.broadcasted_iota(jnp.int32, raw.shape, 1)
        is_b = lane < A_HEADS
        is_a = (lane >= A_HEADS) & (lane < 2 * A_HEADS)
        rows_t = jnp.concatenate([dbgt_ref[...], jnp.zeros((LANE - SUBLANE, tt), F32)], axis=0)
        dbg_v = dbg_ref[...] + jnp.where(is_a, jnp.transpose(rows_t), 0.0)
        dbg_v = jnp.where(is_a, _dot_hi(_chunk_tri(tt, lower=False), jnp.where(is_a, dbg_v, 0.0)), dbg_v)
        beta = _sigmoid(raw)
        z = raw + par_ref[1:2, :]
        neg_ea = -jnp.exp(par_ref[0:1, :])
        g = neg_ea * _softplus(z)
        da = dbg_v * neg_ea * _sigmoid(z)
        dbgi_ref[...] = jnp.where(is_b, dbg_v * beta * (1.0 - beta), jnp.where(is_a, da, 0.0))
        dpar_ref[0:1, :] += jnp.sum(jnp.where(is_a, dbg_v * g, 0.0), axis=0, keepdims=True)
        dpar_ref[1:2, :] += jnp.sum(jnp.where(is_a, da, 0.0), axis=0, keepdims=True)

    wide = pl.BlockSpec((tt, A_WIDTH), lambda i: (i, 0))
    return pl.pallas_call(
        body, name=name, grid=(t // tt,),
        in_specs=[pl.BlockSpec((tt, cw), lambda i: (i, 0)),
                  pl.BlockSpec((tt, LANE), lambda i: (i, C_BG // LANE)),
                  pl.BlockSpec((SUBLANE, LANE), lambda i: (0, 0)),
                  wide, wide, wide, pl.BlockSpec((tt, LANE), lambda i: (i, 0)),
                  pl.BlockSpec((SUBLANE, tt), lambda i: (0, i))],
        out_specs=[pl.BlockSpec((tt, cw), lambda i: (i, 0)), pl.BlockSpec((tt, LANE), lambda i: (i, 0)),
                   pl.BlockSpec((SUBLANE, LANE), lambda i: (0, 0))],
        out_shape=[jax.ShapeDtypeStruct((t, cw), F32), jax.ShapeDtypeStruct((t, LANE), F32),
                   jax.ShapeDtypeStruct((SUBLANE, LANE), F32)],
        compiler_params=_cp("arbitrary"))(conv, h, par, dq, dk, dv, dbg, dbgt)


def _conv_bwd(dc, h, conv_w, dh, *, tt, name):
    t = dc.shape[0]
    cw = 3 * A_WIDTH
    hb = tt // HALO
    nb = t // tt

    def body(dc_ref, after_ref, pre_ref, before_ref, cw_ref, dh_in_ref, dpre_ref, dcw_ref):
        i = pl.program_id(0)

        @pl.when(i == 0)
        def _():
            dcw_ref[...] = jnp.zeros_like(dcw_ref)

        dcv = dc_ref[...]
        after = jnp.where(i < nb - 1, after_ref[...], 0.0)
        cur = pre_ref[...].astype(F32)
        before = jnp.where(i > 0, before_ref[...].astype(F32)[HALO - SUBLANE:], 0.0)
        w = cw_ref[...]
        acc = dcv * w[CONV_K - 1:CONV_K, :]
        dcw_ref[CONV_K - 1:CONV_K, :] += jnp.sum(dcv * cur, axis=0, keepdims=True)
        for s in range(1, CONV_K):
            j = CONV_K - 1 - s
            acc = acc + _shift_up(dcv, after, s) * w[j:j + 1, :]
            dcw_ref[j:j + 1, :] += jnp.sum(dcv * _shift_down(cur, before, s), axis=0, keepdims=True)
        dpre_ref[...] = acc

    return pl.pallas_call(
        body, name=name, grid=(nb,),
        in_specs=[pl.BlockSpec((tt, cw), lambda i: (i, 0)),
                  pl.BlockSpec((SUBLANE, cw), lambda i: (jnp.minimum((i + 1) * (tt // SUBLANE), t // SUBLANE - 1), 0)),
                  pl.BlockSpec((tt, cw), lambda i: (i, 0)),
                  pl.BlockSpec((HALO, cw), lambda i: (jnp.maximum(i * hb - 1, 0), 0)),
                  pl.BlockSpec((CONV_K, cw), lambda i: (0, 0)), _ANY],
        out_specs=[pl.BlockSpec((tt, cw), lambda i: (i, 0)), pl.BlockSpec((SUBLANE, cw), lambda i: (0, 0))],
        out_shape=[jax.ShapeDtypeStruct(dh.shape, F32), jax.ShapeDtypeStruct((SUBLANE, cw), F32)],
        input_output_aliases={5: 0},
        compiler_params=_cp("arbitrary"))(dc, dc, h, h, conv_w, dh)


def _swa_bwd(h, dm, ob, probs, sink_probs, dh, *, name, carry=None):
    t = h.shape[0]
    qspec, cur, prev = _swa_specs()
    c_ins, c_in_specs, c_out_specs, c_outs, c_scratch = _carry_specs(carry)

    def body(*refs):
        (q_ref, kc_ref, kp_ref, vc_ref, vp_ref, zb_ref, dy_ref, ob_ref, p_ref, ps_ref, dh_in_ref,
         dqz_ref, dk_ref, dv_ref, dsk_ref) = _carried(carry, refs, 11, 4, t // BLOCK)
        n_blk = pl.program_id(0)

        @pl.when(n_blk == 0)
        def _():
            dk_ref[...] = jnp.zeros_like(dk_ref)
            dv_ref[...] = jnp.zeros_like(dv_ref)
            dsk_ref[...] = jnp.zeros_like(dsk_ref)

        kp, kc, vp, vc = kp_ref[...], kc_ref[...], vp_ref[...], vc_ref[...]
        scale = B_HEAD_DIM ** -0.5
        hks = range(B_KV_HEADS)
        ksl = lambda hk: slice(hk * B_HEAD_DIM, (hk + 1) * B_HEAD_DIM)
        heads = lambda hk: range(hk * B_GROUP, (hk + 1) * B_GROUP)
        upper, _ = _swa_window()
        groups = [(_stack_heads(q_ref, hk) * scale,
                   jnp.concatenate([p_ref[:, h * BLOCK:(h + 1) * BLOCK].astype(F32) for h in heads(hk)], axis=0),
                   jnp.concatenate([ps_ref[:, h:h + 1] for h in heads(hk)], axis=0),
                   _stack_heads(ob_ref, hk)) for hk in hks]
        zbs = [_stack_heads(zb_ref, hk) for hk in hks]
        dys = [_stack_heads(dy_ref, hk) for hk in hks]
        gates = [_silu_and_grad(zbs[hk]) for hk in hks]
        dos = [dys[hk] * gates[hk][0] for hk in hks]
        deltas = [jnp.sum(dos[hk] * groups[hk][3], -1, keepdims=True) for hk in hks]
        dps = [jnp.where(upper, _dot_nt(dos[hk], vp[:, ksl(hk)]), _dot_nt(dos[hk], vc[:, ksl(hk)])) for hk in hks]
        dss = [groups[hk][1] * (dps[hk] - deltas[hk]) for hk in hks]
        ds_up = [jnp.where(upper, dss[hk], 0.0) for hk in hks]
        ds_lo = [dss[hk] - ds_up[hk] for hk in hks]
        p_up = [jnp.where(upper, groups[hk][1], 0.0) for hk in hks]
        p_lo = [groups[hk][1] - p_up[hk] for hk in hks]
        dqs = [(_dot(ds_up[hk], kp[:, ksl(hk)]) + _dot(ds_lo[hk], kc[:, ksl(hk)])) * scale for hk in hks]
        dk_prev = [_dot_tn(ds_up[hk], groups[hk][0]) for hk in hks]
        dk_cur = [_dot_tn(ds_lo[hk], groups[hk][0]) for hk in hks]
        dv_prev = [_dot_tn(p_up[hk], dos[hk]) for hk in hks]
        dv_cur = [_dot_tn(p_lo[hk], dos[hk]) for hk in hks]
        for hk in hks:
            dzb = dys[hk] * groups[hk][3] * gates[hk][1]
            dsink = groups[hk][2] * deltas[hk]
            for g in range(B_GROUP):
                hq = hk * B_GROUP + g
                rows = slice(g * BLOCK, (g + 1) * BLOCK)
                qsl = slice(hq * B_HEAD_DIM, (hq + 1) * B_HEAD_DIM)
                dqz_ref[:, qsl] = dqs[hk][rows]
                dqz_ref[:, B_WIDTH + hq * B_HEAD_DIM:B_WIDTH + (hq + 1) * B_HEAD_DIM] = dzb[rows]
                dsk_ref[hq:hq + 1, :] += -jnp.sum(dsink[rows], keepdims=True)
        at_cur = pl.ds(pl.multiple_of(n_blk * BLOCK, BLOCK), BLOCK)
        at_prev = pl.ds(pl.multiple_of(jnp.maximum(n_blk - 1, 0) * BLOCK, BLOCK), BLOCK)
        dk_ref[at_prev, :] += jnp.concatenate(dk_prev, axis=1)
        dv_ref[at_prev, :] += jnp.concatenate(dv_prev, axis=1)
        dk_ref[at_cur, :] += jnp.concatenate(dk_cur, axis=1)
        dv_ref[at_cur, :] += jnp.concatenate(dv_cur, axis=1)

    narrow = jax.ShapeDtypeStruct((t, B_KV_WIDTH), F32)
    res = lambda a, b: pl.BlockSpec((a, b), lambda i: (0, 0))
    row = lambda w: pl.BlockSpec((BLOCK, w), lambda i: (i, 0))
    outs = pl.pallas_call(
        body, name=name, grid=(t // BLOCK,),
        in_specs=[qspec(C_QB), cur(C_KB), prev(C_KB), cur(C_VB), prev(C_VB), qspec(C_ZB),
                  pl.BlockSpec((BLOCK, B_WIDTH), lambda i: (i, 1)), row(B_WIDTH), row(B_Q_HEADS * BLOCK), row(LANE),
                  _ANY] + c_in_specs,
        out_specs=[pl.BlockSpec((BLOCK, 2 * B_WIDTH), lambda i: (i, C_QB // (2 * B_WIDTH))),
                   res(t, B_KV_WIDTH), res(t, B_KV_WIDTH), res(B_Q_HEADS, LANE)] + c_out_specs,
        out_shape=[jax.ShapeDtypeStruct(dh.shape, F32), narrow, narrow,
                   jax.ShapeDtypeStruct((B_Q_HEADS, LANE), F32)] + c_outs,
        scratch_shapes=c_scratch,
        input_output_aliases={10: 0},
        compiler_params=_cp("arbitrary"))(h, h, h, h, h, h, dm, ob, probs, sink_probs, dh, *c_ins)
    return outs[:4], outs[4:]


def _in_proj_dw(dh_main, dh_tail, x, *, tk, name):
    t, n = x.shape

    def body(a_ref, t_ref, x_ref, o_ref, ot_ref):
        @pl.when(pl.program_id(0) == 0)
        def _():
            o_ref[...] = jnp.zeros_like(o_ref)
            ot_ref[...] = jnp.zeros_like(ot_ref)

        xb = x_ref[...].astype(BF16)
        o_ref[...] += _dot_tn(a_ref[...], xb)
        ot_ref[...] += _dot_tn(t_ref[...], xb)

    row = lambda a: pl.BlockSpec((tk, a.shape[1]), lambda kk: (kk, 0))
    acc = lambda a: pl.BlockSpec((a.shape[1], n), lambda kk: (0, 0))
    return pl.pallas_call(
        body, name=name, grid=(t // tk,), in_specs=[row(dh_main), row(dh_tail), row(x)],
        out_specs=[acc(dh_main), acc(dh_tail)],
        out_shape=[jax.ShapeDtypeStruct((a.shape[1], n), F32) for a in (dh_main, dh_tail)],
        compiler_params=_cp("arbitrary"))(dh_main, dh_tail, x)


def _in_proj_dx(dh_main, dh_tail, wt, dr, *, tm, name, carry=None):
    t, n_main = dh_main.shape
    n_tail = dh_tail.shape[1]
    c_ins, c_in_specs, c_out_specs, c_outs, c_scratch = _carry_specs(carry)

    def body(*refs):
        a_ref, t_ref, wa_ref, wt_ref, r_ref, o_ref = _carried(carry, refs, 5, 1, t // tm)
        o_ref[...] = _dot(a_ref[...], wa_ref[...]) + _dot(t_ref[...], wt_ref[...]) + DEEPNORM_ALPHA * r_ref[...]

    row = lambda w: pl.BlockSpec((tm, w), lambda i: (i, 0))
    outs = pl.pallas_call(
        body, name=name, grid=(t // tm,),
        in_specs=[row(n_main), row(n_tail), pl.BlockSpec((n_main, D_MODEL), lambda i: (0, 0)),
                  pl.BlockSpec((n_tail, D_MODEL), lambda i: (n_main // n_tail, 0)), row(D_MODEL)] + c_in_specs,
        out_specs=[row(D_MODEL)] + c_out_specs,
        out_shape=[jax.ShapeDtypeStruct((t, D_MODEL), F32)] + c_outs,
        scratch_shapes=c_scratch,
        compiler_params=_cp("arbitrary"))(dh_main, dh_tail, wt, wt, dr, *c_ins)
    return outs[0], outs[1:]


def _layer_bwd(dxn, res, wt, conv_w, par, sinks_b, norm_w, w_out_bf, ln_g, l, carries=None, carry_dx=None):
    carries = carries or {}
    w_out_bf = res["w_out"]
    dr, dm, dw_out, dln_g, dln_b = _ln_out_bwd(dxn, res["r"], res["mixed"], ln_g, w_out_bf, tm=512, name=f"ln_out_bwd_{l}")
    h = res["h"]
    do, dh, dnw = _dn_post_bwd(dm, res["oa"], h, norm_w, tm=512, name=f"dn_post_bwd_{l}")
    dvn, ds_all = _dn_scan_bwd(res["q"], res["k"], res["w"], res["qk"], res["bg"], do, name=f"dn_scan_bwd_{l}")
    (dq, dk, dv, dbg, dbgt), got_chunk = _dn_chunk_bwd(
        res["q"], res["k"], res["v"], res["vn"], res["tmat"], res["qk"], res["bg"], res["bgt"], res["s_all"], ds_all,
        dvn, do, name=f"dn_chunk_bwd_{l}", carry=carries.get("dn_chunk"))
    dc, dbgi, dpar = _dn_pre_bwd(h, res["conv"], par, dq, dk, dv, dbg, dbgt, tt=512, name=f"dn_pre_bwd_{l}")
    dh, dcw = _conv_bwd(dc, h, conv_w, dh, tt=512, name=f"conv_bwd_{l}")
    (dh, dkb, dvb, dsk), got_swa = _swa_bwd(h, dm, res["ob"], res["swa_p"], res["swa_ps"], dh, name=f"swa_bwd_{l}",
                                            carry=carries.get("swa"))
    carried = dict(dn_chunk=got_chunk, swa=got_swa)
    dh_tail = jnp.concatenate([dkb, dvb, dbgi], axis=1)
    dwt_main, dwt_tail = _in_proj_dw(dh, dh_tail, res["x"], tk=512, name=f"in_proj_dw_{l}")
    grads = dict(w_in=(dwt_main, dwt_tail), conv_w=dcw[:CONV_K], a_log=dpar[0, A_HEADS:2 * A_HEADS],
                 dt_bias=dpar[1, A_HEADS:2 * A_HEADS], norm_w=dnw[0], sinks=dsk[:, 0], w_out=dw_out,
                 ln_g=dln_g[0], ln_b=dln_b[0])
    dx, carried_dx = _in_proj_dx(dh, dh_tail, wt, dr, tm=512, name=f"in_proj_dx_{l}",
                                 carry=None if carry_dx is None else carry_dx(grads))
    return dx, grads, carried, carried_dx


def _layer_args(wt, conv_w, a_log, dt_bias, sinks, norm_w, w_out_bf):
    return (wt, conv_w, _gate_params(a_log, dt_bias), jnp.broadcast_to(sinks[:, None], (B_Q_HEADS, LANE)),
            norm_w[None], w_out_bf)


def _local_step(x, target, args0, args1, ln_g, ln_b, gathers=None, reduce1=None, reduce0=None):
    assert DEPTH == 2
    x1, res0, got = _layer_fwd(x, *args0, ln_g[0][None], ln_b[0][None], 0, carries=gathers)
    if gathers is not None:
        args1 = args1(got)
    (dx, loss_tile), res1, _ = _layer_fwd(x1, *args1, ln_g[1][None], ln_b[1][None], 1, target=target)
    dx, grads1, _, _ = _layer_bwd(dx, res1, *args1, ln_g[1][None], 1)
    carries = None if reduce1 is None else reduce1(grads1)
    carry_dx = None if reduce0 is None else (lambda grads0: reduce0(grads0, grads1, loss_tile))
    dx, grads0, landed1, landed0 = _layer_bwd(dx, res0, *args0, ln_g[0][None], 0, carries=carries, carry_dx=carry_dx)
    return loss_tile, dx, [grads0, grads1], landed1, landed0


_ANY = pl.BlockSpec(memory_space=pl.ANY)
_MESH = pl.DeviceIdType.MESH


HALF = D_MODEL // 2


class _Exchange:
    def __init__(self, ins, outs, n_remote, n_local, plan):
        self.ins, self.outs, self.n_remote, self.n_local, self.plan = tuple(ins), tuple(outs), n_remote, n_local, plan

    def scratch(self):
        return [pltpu.SemaphoreType.DMA((self.n_remote,)), pltpu.SemaphoreType.DMA((self.n_remote,)),
                pltpu.SemaphoreType.DMA((max(self.n_local, 1),))]

    def _copies(self, in_refs, out_refs, sems, arriving):
        send_sems, recv_sems, local_sems = sems
        local, sends, recvs = self.plan(in_refs, out_refs)
        loc = [pltpu.make_async_copy(s, d, local_sems.at[i]) for i, (s, d) in enumerate(local)]
        rem = [pltpu.make_async_remote_copy(src_ref=s, dst_ref=recvs[i] if arriving else d, send_sem=send_sems.at[i],
                                            recv_sem=recv_sems.at[i], device_id=peer, device_id_type=_MESH)
               for i, (s, d, peer) in enumerate(sends)]
        return loc, rem

    def start(self, in_refs, out_refs, sems):
        loc, rem = self._copies(in_refs, out_refs, sems, arriving=False)
        for cp in loc + rem:
            cp.start()

    def finish(self, in_refs, out_refs, sems):
        loc, rem = self._copies(in_refs, out_refs, sems, arriving=True)
        for cp in rem:
            cp.wait_recv()
        for cp in rem:
            cp.wait_send()
        for cp in loc:
            cp.wait()


def _run_exchange(ex, *, name):
    n_in, n_out = len(ex.ins), len(ex.outs)

    def body(*refs):
        parts = refs[:n_in], refs[n_in:n_in + n_out], refs[n_in + n_out:]
        ex.start(*parts)
        ex.finish(*parts)

    return pl.pallas_call(body, name=name, in_specs=[_ANY] * n_in, out_specs=[_ANY] * n_out, out_shape=list(ex.outs),
                          scratch_shapes=ex.scratch())(*ex.ins)


def _place():
    x, y, c = lax.axis_index("x"), lax.axis_index("y"), lax.axis_index("c")
    return x, y, c, [(1 - x, y), (x, 1 - y), (1 - x, 1 - y)]


def _gather_exchange(arrays):
    n = len(arrays)

    def plan(src, dst):
        x, y, c, chips = _place()
        me = 2 * x + y
        local = [(src[k], dst[k].at[me]) for k in range(n)]
        sends = [(src[k], dst[k].at[me], (px, py, c)) for k in range(n) for px, py in chips]
        recvs = [dst[k].at[2 * px + py] for k in range(n) for px, py in chips]
        return local, sends, recvs

    return _Exchange(arrays, [jax.ShapeDtypeStruct((N_SHARD,) + a.shape, a.dtype) for a in arrays], 3 * n, n, plan)


def _gather_two_level(pack, conv_w, *, name):
    rows = pack.shape[0]
    part_rows = rows // 2

    def body(pack_ref, conv_ref, land_ref, conv_land_ref, send1, recv1, send2, recv2, csend, crecv, local_sems):
        x, y, c, chips = _place()
        me = 2 * x + y
        sibling = (x, y, 1 - c)
        part = lambda core: pl.ds(pl.multiple_of(core * part_rows, 16), part_rows)
        remote = lambda src, dst, ss, rs, to: pltpu.make_async_remote_copy(
            src_ref=src, dst_ref=dst, send_sem=ss, recv_sem=rs, device_id=to, device_id_type=_MESH)
        local = [pltpu.make_async_copy(pack_ref, land_ref.at[me], local_sems.at[0]),
                 pltpu.make_async_copy(conv_ref, conv_land_ref.at[me], local_sems.at[1])]
        for cp in local:
            cp.start()
        first = [remote(pack_ref.at[part(c)], land_ref.at[me, part(c)], send1.at[j], recv1.at[j], (px, py, c))
                 for j, (px, py) in enumerate(chips)]
        convs = [remote(conv_ref, conv_land_ref.at[me], csend.at[j], crecv.at[j], (px, py, c))
                 for j, (px, py) in enumerate(chips)]
        for cp in first + convs:
            cp.start()
        passed = []
        for j, (px, py) in enumerate(chips):
            slot = 2 * px + py
            remote(pack_ref.at[part(c)], land_ref.at[slot, part(c)], send1.at[j], recv1.at[j], (px, py, c)).wait_recv()
            cp = remote(land_ref.at[slot, part(c)], land_ref.at[slot, part(c)], send2.at[j], recv2.at[j], sibling)
            cp.start()
            passed.append(cp)
        for j, (px, py) in enumerate(chips):
            slot = 2 * px + py
            remote(land_ref.at[slot, part(1 - c)], land_ref.at[slot, part(1 - c)], send2.at[j], recv2.at[j],
                   sibling).wait_recv()
            remote(conv_ref, conv_land_ref.at[slot], csend.at[j], crecv.at[j], (px, py, c)).wait_recv()
        for cp in first + convs + passed:
            cp.wait_send()
        for cp in local:
            cp.wait()

    sems = [pltpu.SemaphoreType.DMA((3,))] * 6 + [pltpu.SemaphoreType.DMA((2,))]
    return pl.pallas_call(
        body, name=name, in_specs=[_ANY, _ANY], out_specs=[_ANY, _ANY],
        out_shape=[jax.ShapeDtypeStruct((N_SHARD,) + pack.shape, pack.dtype),
                   jax.ShapeDtypeStruct((N_SHARD,) + conv_w.shape, conv_w.dtype)],
        scratch_shapes=sems)(pack, conv_w)


def _half(core):
    return pl.ds(pl.multiple_of(core * HALF, HALF), HALF)


def _reduce_scatter_exchange(g, row0, rows):
    def plan(src, dst):
        x, y, c, chips = _place()
        peers = [(px, py, c if t == 0 else 1 - c) for px, py in chips for t in (0, 1)] + [(x, y, 1 - c)]
        sends = [(src[0].at[2 * px + py, pl.ds(row0, rows), _half(pc)], dst[0].at[k], (px, py, pc))
                 for k, (px, py, pc) in enumerate(peers)]
        return [], sends, [dst[0].at[k] for k in range(7)]

    return _Exchange([g], [jax.ShapeDtypeStruct((7, rows, HALF), g.dtype)], 7, 0, plan)


def _pair_window_exchange(g):
    def plan(src, dst):
        x, y, c, _ = _place()
        return [], [(src[0].at[:, :, _half(1 - c)], dst[0], (x, y, 1 - c))], [dst[0]]

    return _Exchange([g], [jax.ShapeDtypeStruct(g.shape[:2] + (HALF,), g.dtype)], 1, 0, plan)


def _chip_scatter_exchange(p, small):
    def plan(src, dst):
        x, y, c, chips = _place()
        mine = 4 * x + 2 * y + c
        peers = [(px, py, c if t == 0 else 1 - c) for px, py in chips for t in (0, 1)] + [(x, y, 1 - c)]
        sends = [(src[0].at[2 * px + py], dst[0].at[j], (px, py, c)) for j, (px, py) in enumerate(chips)]
        recvs = [dst[0].at[j] for j in range(3)]
        sends += [(src[1], dst[1].at[mine], peer) for peer in peers]
        recvs += [dst[1].at[4 * px + 2 * py + pc] for px, py, pc in peers]
        return [(src[1], dst[1].at[mine])], sends, recvs

    outs = [jax.ShapeDtypeStruct((3,) + p.shape[1:], p.dtype), jax.ShapeDtypeStruct((8,) + small.shape, small.dtype)]
    return _Exchange([p, small], outs, 10, 1, plan)


def _share_exchange(arrays):
    n = len(arrays)

    def plan(src, dst):
        x, y, c, _ = _place()
        return [], [(src[k], dst[k], (x, y, 1 - c)) for k in range(n)], [dst[k] for k in range(n)]

    return _Exchange(arrays, [jax.ShapeDtypeStruct(a.shape, a.dtype) for a in arrays], n, 0, plan)


def _sum_scatter(g, lands, me, core, *, tc, name):
    rows = g.shape[1]
    per = HALF // tc
    n = len(lands)

    def body(*refs):
        g_ref, land_refs, o_ref = refs[1], refs[2:2 + n], refs[2 + n]
        at = 0
        for land_ref in land_refs:
            run = slice(at, at + land_ref.shape[1])
            acc = g_ref[run, :].astype(F32)
            for k in range(7):
                acc = acc + land_ref[k].astype(F32)
            o_ref[run, :] = acc
            at = run.stop

    return pl.pallas_call(
        body, name=name, out_shape=jax.ShapeDtypeStruct((rows, HALF), F32), compiler_params=_cp("parallel"),
        grid_spec=pltpu.PrefetchScalarGridSpec(
            num_scalar_prefetch=1, grid=(per,),
            in_specs=[pl.BlockSpec((None, rows, tc), lambda i, w: (w[0], 0, w[1] * per + i))]
            + [pl.BlockSpec((7, a.shape[1], tc), lambda i, w: (0, 0, i)) for a in lands],
            out_specs=pl.BlockSpec((rows, tc), lambda i, w: (0, i))))(
        jnp.stack([me, core]).astype(jnp.int32), g, *lands)


def _pair_add(g, land, core, *, name):
    n, rows, _ = g.shape

    def body(core_ref, g_ref, land_ref, o_ref):
        o_ref[...] = (g_ref[...].astype(F32) + land_ref[...].astype(F32)).astype(o_ref.dtype)

    blk = pl.BlockSpec((1, rows, HALF), lambda i, w: (i, 0, 0))
    return pl.pallas_call(
        body, name=name, out_shape=jax.ShapeDtypeStruct((n, rows, HALF), g.dtype), compiler_params=_cp("parallel"),
        grid_spec=pltpu.PrefetchScalarGridSpec(
            num_scalar_prefetch=1, grid=(n,),
            in_specs=[pl.BlockSpec((1, rows, HALF), lambda i, w: (i, 0, w[0])), blk], out_specs=blk))(
        jnp.reshape(core, (1,)).astype(jnp.int32), g, land)


def _sum_chips(p, land, me, *, tc, name):
    rows = p.shape[1]

    def body(me_ref, p_ref, land_ref, o_ref):
        acc = p_ref[...].astype(F32)
        for k in range(3):
            acc = acc + land_ref[k].astype(F32)
        o_ref[...] = acc

    return pl.pallas_call(
        body, name=name, out_shape=jax.ShapeDtypeStruct((rows, HALF), F32), compiler_params=_cp("parallel"),
        grid_spec=pltpu.PrefetchScalarGridSpec(
            num_scalar_prefetch=1, grid=(HALF // tc,),
            in_specs=[pl.BlockSpec((None, rows, tc), lambda i, w: (w[0], 0, i)),
                      pl.BlockSpec((3, rows, tc), lambda i, w: (0, 0, i))],
            out_specs=pl.BlockSpec((rows, tc), lambda i, w: (0, i))))(
        jnp.reshape(me, (1,)).astype(jnp.int32), p, land)


def _sum_slots(a, *, name):
    n = a.shape[0]

    def body(a_ref, o_ref):
        acc = a_ref[0]
        for k in range(1, n):
            acc = acc + a_ref[k]
        o_ref[...] = acc

    return pl.pallas_call(body, name=name, out_shape=jax.ShapeDtypeStruct(a.shape[1:], a.dtype))(a)


def _elementwise(fn, ins, n_out, block, *, name):
    shape = ins[0].shape
    grid = tuple(s // b for s, b in zip(shape, block))
    n_in = len(ins)

    def body(*refs):
        outs = fn(*[r[...] for r in refs[:n_in]])
        for o_ref, val in zip(refs[n_in:], outs):
            o_ref[...] = val

    spec = pl.BlockSpec(block, lambda i, j, k: (i, j, k))
    return pl.pallas_call(body, name=name, grid=grid, in_specs=[spec] * n_in, out_specs=[spec] * n_out,
                          out_shape=[jax.ShapeDtypeStruct(shape, F32)] * n_out,
                          compiler_params=_cp(*["parallel"] * 3))(*ins)


def _adamw_math(w, g, m, v):
    mn = ADAM_B1 * m + (1.0 - ADAM_B1) * g
    vn = ADAM_B2 * v + (1.0 - ADAM_B2) * (g * g)
    m_hat = mn / (1.0 - ADAM_B1 ** ADAM_STEP)
    v_hat = vn / (1.0 - ADAM_B2 ** ADAM_STEP)
    return -ADAM_LR * (m_hat / (jnp.sqrt(v_hat) + ADAM_EPS) + ADAM_WD * w), mn, vn


def _adamw(w, g, m, v, block, *, name):
    return _elementwise(_adamw_math, [w, g, m, v], 3, block, name=name)


def _interleave_layers(layers, *, tc, name):
    rows, cols = layers[0].shape
    n = len(layers)

    def body(*refs):
        for l in range(n):
            refs[n][:, l, :] = refs[l][...]

    return pl.pallas_call(body, name=name, grid=(cols // tc,),
                          in_specs=[pl.BlockSpec((rows, tc), lambda i: (0, i))] * n,
                          out_specs=pl.BlockSpec((rows, n, tc), lambda i: (0, 0, i)),
                          out_shape=jax.ShapeDtypeStruct((rows, n, cols), layers[0].dtype),
                          compiler_params=_cp("parallel"))(*layers)


def _adamw_small(ws, gs, ms, vs, *, name):
    n = len(ws)

    def body(*refs):
        w, g, m, v, outs = refs[:n], refs[n:2 * n], refs[2 * n:3 * n], refs[3 * n:4 * n], refs[4 * n:]
        for k in range(n):
            for slot, val in enumerate(_adamw_math(w[k][...], g[k][...], m[k][...], v[k][...])):
                outs[slot * n + k][...] = val

    outs = pl.pallas_call(body, name=name, out_shape=[jax.ShapeDtypeStruct(a.shape, F32) for a in ws] * 3)(
        *ws, *gs, *ms, *vs)
    return outs[:n], outs[n:2 * n], outs[2 * n:]


def _to_kernel_order(wt):
    gates = jnp.pad(wt[2048:2056], ((0, LANE - 2 * A_HEADS), (0, 0)))
    return jnp.concatenate([wt[0:2048], wt[2056:2568], wt[2824:3336], wt[2568:2696], wt[2696:2824], gates], axis=0)


def _from_kernel_order(main, tail):
    return jnp.concatenate([main[0:2048], tail[C_BG - DH_MAIN:C_BG - DH_MAIN + 2 * A_HEADS],
                            main[C_QB:C_QB + B_WIDTH], tail[0:B_KV_WIDTH], tail[B_KV_WIDTH:2 * B_KV_WIDTH],
                            main[C_ZB:C_ZB + B_WIDTH]], axis=0)


def _gate_params(a_log, dt_bias):
    return jnp.pad(jnp.stack([a_log, dt_bias]), ((0, SUBLANE - 2), (A_HEADS, LANE - 2 * A_HEADS)))


SMALL = ("conv_w", "a_log", "dt_bias", "norm_w", "sinks", "ln_g", "ln_b")


def _pack(parts, cols):
    flat = jnp.concatenate([p.reshape(-1) for p in parts])
    rows = -(-flat.shape[0] // cols)
    return jnp.pad(flat, (0, rows * cols - flat.shape[0])).reshape(rows, cols)


def _unpack(packed, shapes):
    flat = packed.reshape(-1)
    out, at = [], 0
    for s in shapes:
        n = math.prod(s)
        out.append(flat[at:at + n].reshape(s))
        at += n
    return out


def kernel(x, w_in, conv_w, a_log, dt_bias, norm_w, sinks, w_out, ln_g, ln_b, loss_target, m_w_in, m_conv_w, m_a_log, m_dt_bias, m_norm_w, m_sinks, m_w_out, m_ln_g, m_ln_b, v_w_in, v_conv_w, v_a_log, v_dt_bias, v_norm_w, v_sinks, v_w_out, v_ln_g, v_ln_b):
    xi, yi, ci = lax.axis_index("x"), lax.axis_index("y"), lax.axis_index("c")
    me = 2 * xi + yi

    to_t = lambda a: jnp.transpose(a, (2, 0, 1))
    from_t = lambda a: jnp.transpose(a, (1, 2, 0))

    wt_shard = to_t(w_in)

    def pack_weights(l):
        rows = jnp.pad(wt_shard[:, l], ((0, IN_PAD - IN_SHARD), (0, 0)))
        return jnp.concatenate([rows, w_out[l]], axis=0).astype(BF16)

    pack0, pack1 = pack_weights(0), pack_weights(1)
    got_in0, g_conv = _gather_two_level(pack0[:IN_PAD], conv_w, name="gather_weights_0")
    conv_full = jnp.moveaxis(g_conv, 0, 2).reshape(DEPTH, CONV_K, 3 * A_WIDTH)
    carriers = ("dn_pre", "dn_wy", "dn_scan")
    cuts = (0, 288, 624, IN_PAD)
    gathers = {nm: _gather_exchange([pack1[cuts[i]:cuts[i + 1]]]) for i, nm in enumerate(carriers)}
    gathers.update(in_proj=_gather_exchange([pack0[IN_PAD:]]), swa=_gather_exchange([pack1[IN_PAD:]]))
    w_in_of = lambda rows: _to_kernel_order(rows[:, :IN_SHARD].reshape(IN_COLS, D_MODEL))
    w_out_of = lambda rows: rows.reshape(D_MODEL, D_MODEL)
    args0 = _layer_args(w_in_of(got_in0), conv_full[0], a_log[0], dt_bias[0], sinks[0], norm_w[0],
                        lambda got: w_out_of(got[0]))

    def args1(got):
        rows = jnp.concatenate([got[nm][0] for nm in carriers], axis=1)
        return _layer_args(w_in_of(rows), conv_full[1], a_log[1], dt_bias[1], sinks[1], norm_w[1],
                           w_out_of(got["swa"][0]))

    def pack_grads(g):
        gin = _from_kernel_order(*g["w_in"]).reshape(N_SHARD, IN_SHARD, D_MODEL)
        gin = jnp.pad(gin, ((0, 0), (0, IN_PAD - IN_SHARD), (0, 0)))
        return jnp.concatenate([gin, g["w_out"].reshape(N_SHARD, OUT_SHARD, D_MODEL)], axis=1).astype(BF16)

    packed = {}

    def reduce1(grads1):
        packed[1] = pack_grads(grads1)
        half_rows = packed[1].shape[1] // 2
        return dict(dn_chunk=_reduce_scatter_exchange(packed[1], 0, half_rows),
                    swa=_reduce_scatter_exchange(packed[1], half_rows, half_rows))

    def reduce0(grads0, grads1, loss_tile):
        g0 = pack_grads(grads0)
        from_sibling = _run_exchange(_pair_window_exchange(g0), name="pair_reduce_0")[0]
        packed[0] = _pair_add(g0, from_sibling, ci, name="pair_add_0")
        gsmall = _pack([jnp.stack([g[nm] for g in (grads0, grads1)]) for nm in SMALL] + [loss_tile[0, 0:1]], D_MODEL)
        return _chip_scatter_exchange(packed[0], gsmall)

    _, dx, grads, landed1, (landed0, landed_small) = _local_step(
        x[0], loss_target[0], args0, args1, ln_g, ln_b, gathers=gathers, reduce1=reduce1, reduce0=reduce0)

    small_shapes = [(DEPTH,) + grads[0][nm].shape for nm in SMALL]
    halves = [_sum_chips(packed[0], landed0, me, tc=2 * LANE, name="reduce_sum_0"),
              _sum_scatter(packed[1], [landed1["dn_chunk"][0], landed1["swa"][0]], me, ci, tc=2 * LANE,
                           name="reduce_sum_1")]
    s_small = _sum_slots(landed_small, name="reduce_sum_small")
    others = _run_exchange(_share_exchange(halves), name="pair_share")
    full = [jnp.where(ci == 0, jnp.concatenate([mine, other], axis=1), jnp.concatenate([other, mine], axis=1))
            for mine, other in zip(halves, others)]
    grad_in_layers = [f[:IN_SHARD] for f in full]
    grad_out = jnp.stack([f[IN_PAD:] for f in full])
    out_blk = (1, OUT_SHARD, D_MODEL)
    *small_grads, loss = _unpack(s_small, small_shapes + [()])
    gs = dict(zip(SMALL, small_grads))
    gs["conv_w"] = lax.dynamic_slice_in_dim(gs["conv_w"], me * CONV_SHARD, CONV_SHARD, axis=2)

    grad_in_t = _interleave_layers(grad_in_layers, tc=2 * LANE, name="grad_in_layers")
    d_in, nm_in, nv_in = (from_t(o) for o in _adamw(to_t(w_in), grad_in_t, to_t(m_w_in), to_t(v_w_in),
                                                    (IN_SHARD // 6, DEPTH, D_MODEL), name="adamw_in"))
    grad_in = from_t(grad_in_t)
    d_out, nm_out, nv_out = _adamw(w_out, grad_out, m_w_out, v_w_out, out_blk, name="adamw_out")
    ws = dict(conv_w=conv_w, a_log=a_log, dt_bias=dt_bias, norm_w=norm_w, sinks=sinks, ln_g=ln_g, ln_b=ln_b)
    ms = dict(conv_w=m_conv_w, a_log=m_a_log, dt_bias=m_dt_bias, norm_w=m_norm_w, sinks=m_sinks, ln_g=m_ln_g, ln_b=m_ln_b)
    vs = dict(conv_w=v_conv_w, a_log=v_a_log, dt_bias=v_dt_bias, norm_w=v_norm_w, sinks=v_sinks, ln_g=v_ln_g, ln_b=v_ln_b)
    d_s, nm_s, nv_s = (dict(zip(SMALL, o)) for o in _adamw_small(*[[d[nm] for nm in SMALL] for d in (ws, gs, ms, vs)],
                                                                 name="adamw_small"))

    def in_order(big_in, small, big_out):
        return (big_in, small["conv_w"], small["a_log"], small["dt_bias"], small["norm_w"], small["sinks"], big_out,
                small["ln_g"], small["ln_b"])

    return (loss, dx[None], *in_order(grad_in, gs, grad_out), *in_order(d_in, d_s, d_out),
            *in_order(nm_in, nm_s, nm_out), *in_order(nv_in, nv_s, nv_out))
```

```python
import math

import jax
import jax.numpy as jnp
from jax import lax
from jax.experimental import pallas as pl
from jax.experimental.pallas import tpu as pltpu

F32 = jnp.float32
BF16 = jnp.bfloat16
HI = lax.Precision.HIGHEST

D_MODEL = 1024
DEPTH = 2
A_HEADS = 4
A_HEAD_DIM = 128
A_WIDTH = 512
CONV_K = 4
CHUNK = 64
B_Q_HEADS = 8
B_KV_HEADS = 2
B_HEAD_DIM = 64
B_GROUP = 4
B_WIDTH = 512
B_KV_WIDTH = 128
BLOCK = 128
IN_COLS = 3336
DEEPNORM_ALPHA = (2 * DEPTH) ** 0.25
LN_EPS = 1e-5
RMS_EPS = 1e-6
L2_EPS = 1e-6
ADAM_LR = 0.001
ADAM_B1 = 0.9
ADAM_B2 = 0.999
ADAM_EPS = 1e-08
ADAM_WD = 0.01
ADAM_STEP = 10

N_SHARD = 4
IN_SHARD = IN_COLS // N_SHARD
OUT_SHARD = D_MODEL // N_SHARD
CONV_SHARD = 3 * A_WIDTH // N_SHARD
IN_PAD = -(-IN_SHARD // 96) * 96

P_COLS = 3456
C_PRE = 0
C_ZA = 1536
C_QB = 2048
C_ZB = 2560
C_KB = 3072
C_VB = 3200
C_BG = 3328
DH_MAIN = C_KB
LANE = 128
SUBLANE = 8
HALO = 16
VMEM_LIMIT = 56 * 1024 * 1024
ALIBI = tuple(2.0 ** (-8.0 * (h + 1) / B_Q_HEADS) for h in range(B_Q_HEADS))
NEG = -1e30


def _cp(*sem):
    return pltpu.CompilerParams(dimension_semantics=sem, vmem_limit_bytes=VMEM_LIMIT)


def _dot(a, b):
    return jnp.dot(a.astype(BF16), b.astype(BF16), preferred_element_type=F32)


def _dot_nt(a, b):
    return lax.dot_general(a.astype(BF16), b.astype(BF16), (((1,), (1,)), ((), ())),
                           preferred_element_type=F32)


def _dot_tn(a, b):
    return lax.dot_general(a.astype(BF16), b.astype(BF16), (((0,), (0,)), ((), ())),
                           preferred_element_type=F32)


def _dot_hi(a, b):
    return jnp.dot(a, b, precision=HI, preferred_element_type=F32)


def _sigmoid(x):
    return jax.nn.sigmoid(x)


def _silu(x):
    return x * _sigmoid(x)


def _silu_and_grad(x):
    s = _sigmoid(x)
    return x * s, s * (1.0 + x * (1.0 - s))


def _softplus(x):
    return jnp.maximum(x, 0.0) + jnp.log(1.0 + jnp.exp(-jnp.abs(x)))


def _shift_down(cur, before, s):
    if s == 0:
        return cur
    r = pltpu.roll(cur, s, 0)
    rb = pltpu.roll(before, s, 0)
    row = lax.broadcasted_iota(jnp.int32, before.shape, 0)
    head = jnp.where(row < s, rb, r[0:SUBLANE])
    return jnp.concatenate([head, r[SUBLANE:]], axis=0)


def _shift_up(cur, after, s):
    if s == 0:
        return cur
    n = cur.shape[0]
    r = pltpu.roll(cur, n - s, 0)
    ra = pltpu.roll(after, SUBLANE - s, 0)
    row = lax.broadcasted_iota(jnp.int32, after.shape, 0)
    tail = jnp.where(row >= SUBLANE - s, ra, r[n - SUBLANE:])
    return jnp.concatenate([r[:n - SUBLANE], tail], axis=0)


def _conv_fwd(cur, before, w):
    acc = cur * w[CONV_K - 1:CONV_K, :]
    for s in range(1, CONV_K):
        acc = acc + _shift_down(cur, before, s) * w[CONV_K - 1 - s:CONV_K - s, :]
    return acc


def _matmul_nt(a, bt, *, tm, name, carry=None):
    m, k = a.shape
    n = bt.shape[0]
    c_ins, c_in_specs, c_out_specs, c_outs, c_scratch = _carry_specs(carry)

    def body(*refs):
        a_ref, b_ref, o_ref = _carried(carry, refs, 2, 1, m // tm)
        o_ref[...] = _dot_nt(a_ref[...], b_ref[...]).astype(o_ref.dtype)

    outs = pl.pallas_call(
        body, name=name, grid=(m // tm,),
        in_specs=[pl.BlockSpec((tm, k), lambda i: (i, 0)), pl.BlockSpec((n, k), lambda i: (0, 0))] + c_in_specs,
        out_specs=[pl.BlockSpec((tm, n), lambda i: (i, 0))] + c_out_specs,
        out_shape=[jax.ShapeDtypeStruct((m, n), BF16)] + c_outs,
        scratch_shapes=c_scratch,
        compiler_params=_cp("arbitrary"))(a, bt, *c_ins)
    return outs[0], outs[1:]


def _dn_pre(h, conv_w, par, *, tt, name, carry=None):
    t = h.shape[0]
    cw = 3 * A_WIDTH
    hb = tt // HALO

    c_ins, c_in_specs, c_out_specs, c_outs, c_scratch = _carry_specs(carry)

    def body(*refs):
        (pre_ref, halo_ref, bgi_ref, cw_ref, par_ref,
         q_ref, k_ref, v_ref, bg_ref, bgt_ref, c_ref) = _carried(carry, refs, 5, 6, t // tt)
        i = pl.program_id(0)
        cur = pre_ref[...].astype(F32)
        before = jnp.where(i > 0, halo_ref[...].astype(F32)[HALO - SUBLANE:], 0.0)
        conv = _conv_fwd(cur, before, cw_ref[...])
        c_ref[...] = conv
        s = _silu(conv)
        for hd in range(A_HEADS):
            sl = slice(hd * LANE, (hd + 1) * LANE)
            tq = s[:, hd * LANE:(hd + 1) * LANE]
            q_ref[:, sl] = tq * (lax.rsqrt(jnp.sum(tq * tq, -1, keepdims=True) + L2_EPS) * (A_HEAD_DIM ** -0.5))
            tk = s[:, A_WIDTH + hd * LANE:A_WIDTH + (hd + 1) * LANE]
            k_ref[:, sl] = tk * lax.rsqrt(jnp.sum(tk * tk, -1, keepdims=True) + L2_EPS)
        v_ref[...] = s[:, 2 * A_WIDTH:]
        raw = bgi_ref[...].astype(F32)
        lane = lax.broadcasted_iota(jnp.int32, raw.shape, 1)
        is_a = (lane >= A_HEADS) & (lane < 2 * A_HEADS)
        g = jnp.where(is_a, -jnp.exp(par_ref[0:1, :]) * _softplus(raw + par_ref[1:2, :]), 0.0)
        gc = _dot_hi(_chunk_tri(tt, lower=True), g)
        bg = jnp.where(lane < A_HEADS, _sigmoid(raw), gc)
        bg_ref[...] = bg
        bgt_ref[...] = jnp.transpose(bg)[0:SUBLANE, :]

    wide = jax.ShapeDtypeStruct((t, A_WIDTH), F32)
    outs = pl.pallas_call(
        body, name=name, grid=(t // tt,),
        in_specs=[pl.BlockSpec((tt, cw), lambda i: (i, 0)),
                  pl.BlockSpec((HALO, cw), lambda i: (jnp.maximum(i * hb - 1, 0), 0)),
                  pl.BlockSpec((tt, LANE), lambda i: (i, C_BG // LANE)),
                  pl.BlockSpec((CONV_K, cw), lambda i: (0, 0)),
                  pl.BlockSpec((SUBLANE, LANE), lambda i: (0, 0))] + c_in_specs,
        out_specs=[pl.BlockSpec((tt, A_WIDTH), lambda i: (i, 0))] * 3
        + [pl.BlockSpec((tt, LANE), lambda i: (i, 0)), pl.BlockSpec((SUBLANE, tt), lambda i: (0, i)),
           pl.BlockSpec((tt, cw), lambda i: (i, 0))] + c_out_specs,
        out_shape=[wide, wide, wide, jax.ShapeDtypeStruct((t, LANE), F32),
                   jax.ShapeDtypeStruct((SUBLANE, t), F32), jax.ShapeDtypeStruct((t, cw), F32)] + c_outs,
        scratch_shapes=c_scratch,
        compiler_params=_cp("arbitrary"))(h, h, h, conv_w, par, *c_ins)
    return outs[:6], outs[6:]


def _chunk_tri(n, lower):
    r = lax.broadcasted_iota(jnp.int32, (n, n), 0)
    c = lax.broadcasted_iota(jnp.int32, (n, n), 1)
    shift = CHUNK.bit_length() - 1
    same = jnp.right_shift(r, shift) == jnp.right_shift(c, shift)
    return (same & ((c <= r) if lower else (c >= r))).astype(F32)


def _chunk_masks():
    r = lax.broadcasted_iota(jnp.int32, (CHUNK, CHUNK), 0)
    c = lax.broadcasted_iota(jnp.int32, (CHUNK, CHUNK), 1)
    return r >= c, r > c, r == c


def _split(a):
    hi = a.astype(BF16)
    return hi, (a - hi.astype(F32)).astype(BF16)


def _dot3(a, b):
    (ah, al), (bh, bl) = a, b
    d = lambda p, q: jnp.dot(p, q, preferred_element_type=F32)
    return d(ah, bh) + (d(ah, bl) + d(al, bh))


def _tri_inv_many(a_list, eye):
    d = lambda p, q: jnp.dot(p.astype(BF16), q.astype(BF16), preferred_element_type=F32)
    r = lax.broadcasted_iota(jnp.int32, (CHUNK, CHUNK), 0)
    c = lax.broadcasted_iota(jnp.int32, (CHUNK, CHUNK), 1)
    same = lambda b: jnp.right_shift(r, b.bit_length() - 1) == jnp.right_shift(c, b.bit_length() - 1)
    x = [jnp.where(same(8), -a, 0.0) for a in a_list]
    tm = [eye + xi for xi in x]
    for _ in range(2):
        x = [d(xi, xi) for xi in x]
        tm = [t + d(t, xi) for t, xi in zip(tm, x)]
    for b in (16, 32, 64):
        low = [jnp.where(same(b) & ~same(b // 2), a, 0.0) for a in a_list]
        tm = [t - d(t, d(lo, t)) for t, lo in zip(tm, low)]
    res = [eye - _dot3(_split(eye + a), _split(t)) for a, t in zip(a_list, tm)]
    return [t + d(t, rs) for t, rs in zip(tm, res)]


def _chunk_gates(bg_v, bgt_v, hd):
    return (bg_v[:, hd:hd + 1], bg_v[:, A_HEADS + hd:A_HEADS + hd + 1],
            None if bgt_v is None else bgt_v[A_HEADS + hd:A_HEADS + hd + 1, :])


WY_ROWS = 512
SCAN_ROWS = 512
WY_GROUP = 8


def _dn_wy(q, k, v, bg, bgt, *, name, carry=None):
    t = q.shape[0]
    rows = WY_ROWS

    c_ins, c_in_specs, c_out_specs, c_outs, c_scratch = _carry_specs(carry)

    def body(*refs):
        q_ref, k_ref, v_ref, bg_ref, bgt_ref, u_ref, w_ref, tm_ref, qk_ref = _carried(carry, refs, 5, 4, t // rows)
        causal, strict, diag = _chunk_masks()
        eye = diag.astype(F32)
        for c0 in range(0, rows // CHUNK, WY_GROUP):
            items = [(c, hd) for c in range(c0, c0 + WY_GROUP) for hd in range(A_HEADS)]
            rs = lambda c: slice(c * CHUNK, (c + 1) * CHUNK)
            sl = lambda hd: slice(hd * LANE, (hd + 1) * LANE)
            hs = lambda hd: slice(hd * CHUNK, (hd + 1) * CHUNK)
            gates = [_chunk_gates(bg_ref[rs(c), :], bgt_ref[:, rs(c)], hd) for c, hd in items]
            dms = [jnp.exp(jnp.where(causal, gcol - grow, NEG)) for _, gcol, grow in gates]
            kbs = [k_ref[rs(c), sl(hd)] * g[0] for (c, hd), g in zip(items, gates)]
            a_list = [jnp.where(strict, _dot_nt(kb, k_ref[rs(c), sl(hd)]) * dm, 0.0)
                      for (c, hd), kb, dm in zip(items, kbs, dms)]
            for (c, hd), dm in zip(items, dms):
                qk_ref[rs(c), hs(hd)] = jnp.where(
                    causal, _dot_nt(q_ref[rs(c), sl(hd)], k_ref[rs(c), sl(hd)]) * dm, 0.0)
            tms = _tri_inv_many(a_list, eye)
            for (c, hd), g, kb, tmat in zip(items, gates, kbs, tms):
                tm_ref[rs(c), hs(hd)] = tmat
                u_ref[rs(c), sl(hd)] = _dot(tmat, v_ref[rs(c), sl(hd)] * g[0])
                w_ref[rs(c), sl(hd)] = _dot(tmat, kb * jnp.exp(g[1])).astype(BF16)

    blk = pl.BlockSpec((rows, A_WIDTH), lambda i: (i, 0))
    half = pl.BlockSpec((rows, A_HEADS * CHUNK), lambda i: (i, 0))
    outs = pl.pallas_call(
        body, name=name, grid=(t // rows,),
        in_specs=[blk, blk, blk, pl.BlockSpec((rows, LANE), lambda i: (i, 0)),
                  pl.BlockSpec((SUBLANE, rows), lambda i: (0, i))] + c_in_specs,
        out_specs=[blk, blk, half, half] + c_out_specs,
        out_shape=[jax.ShapeDtypeStruct((t, A_WIDTH), F32), jax.ShapeDtypeStruct((t, A_WIDTH), BF16),
                   jax.ShapeDtypeStruct((t, A_HEADS * CHUNK), F32),
                   jax.ShapeDtypeStruct((t, A_HEADS * CHUNK), F32)] + c_outs,
        scratch_shapes=c_scratch,
        compiler_params=_cp("arbitrary"))(q, k, v, bg, bgt, *c_ins)
    return outs[:4], outs[4:]


def _dn_scan_fwd(q, k, u, w, qk, bg, *, name, carry=None):
    t = q.shape[0]
    rows = SCAN_ROWS
    per = rows // CHUNK
    c_ins, c_in_specs, c_out_specs, c_outs, c_scratch = _carry_specs(carry)

    def body(*refs):
        q_ref, k_ref, u_ref, w_ref, qk_ref, bg_ref, o_ref, vn_ref, s_ref, state = _carried(carry, refs, 6, 3, t // rows)

        @pl.when(pl.program_id(0) == 0)
        def _():
            state[...] = jnp.zeros_like(state)

        heads = range(A_HEADS)
        sl = lambda hd: slice(hd * LANE, (hd + 1) * LANE)
        s_cur = [state[hd] for hd in heads]
        for c in range(per):
            rs = slice(c * CHUNK, (c + 1) * CHUNK)
            bg_v = bg_ref[rs, :]
            gcols = [_chunk_gates(bg_v, None, hd)[1] for hd in heads]
            glasts = [gc[CHUNK - 1:CHUNK, :] for gc in gcols]
            for hd in heads:
                s_ref[c, hd] = s_cur[hd].astype(BF16)
            vns = [u_ref[rs, sl(hd)] - _dot(w_ref[rs, sl(hd)], s_cur[hd]) for hd in heads]
            qss = [_dot(q_ref[rs, sl(hd)] * jnp.exp(gcols[hd]), s_cur[hd]) for hd in heads]
            s_cur = [s_cur[hd] * jnp.exp(glasts[hd])
                     + _dot_tn(k_ref[rs, sl(hd)] * jnp.exp(glasts[hd] - gcols[hd]), vns[hd]) for hd in heads]
            for hd in heads:
                vn_ref[rs, sl(hd)] = vns[hd]
                o_ref[rs, sl(hd)] = qss[hd] + _dot(qk_ref[rs, hd * CHUNK:(hd + 1) * CHUNK], vns[hd])
        for hd in heads:
            state[hd] = s_cur[hd]

    blk = pl.BlockSpec((rows, A_WIDTH), lambda i: (i, 0))
    half = pl.BlockSpec((rows, A_HEADS * CHUNK), lambda i: (i, 0))
    wide = jax.ShapeDtypeStruct((t, A_WIDTH), F32)
    outs = pl.pallas_call(
        body, name=name, grid=(t // rows,),
        in_specs=[blk, blk, blk, blk, half, pl.BlockSpec((rows, LANE), lambda i: (i, 0))] + c_in_specs,
        out_specs=[blk, blk, pl.BlockSpec((per, A_HEADS, LANE, LANE), lambda i: (i, 0, 0, 0))] + c_out_specs,
        out_shape=[wide, wide, jax.ShapeDtypeStruct((t // CHUNK, A_HEADS, LANE, LANE), BF16)] + c_outs,
        scratch_shapes=[pltpu.VMEM((A_HEADS, LANE, LANE), F32)] + c_scratch,
        compiler_params=_cp("arbitrary"))(q, k, u, w, qk, bg, *c_ins)
    return outs[:3], outs[3:]


def _stack_heads(ref, hk):
    return jnp.concatenate([ref[:, h * B_HEAD_DIM:(h + 1) * B_HEAD_DIM].astype(F32)
                            for h in range(hk * B_GROUP, (hk + 1) * B_GROUP)], axis=0)


def _swa_window():
    qi = lax.broadcasted_iota(jnp.int32, (BLOCK, BLOCK), 0)
    kj = lax.broadcasted_iota(jnp.int32, (BLOCK, BLOCK), 1)
    dist = jnp.where(kj > qi, qi + BLOCK - kj, qi - kj).astype(F32)
    rows = lax.broadcasted_iota(jnp.int32, (B_GROUP * BLOCK, BLOCK), 0)
    cols = lax.broadcasted_iota(jnp.int32, (B_GROUP * BLOCK, BLOCK), 1)
    return cols > jnp.bitwise_and(rows, BLOCK - 1), dist


def _swa_group_probs(q_ref, sk_ref, kp, kc, vp, vc, n_blk):
    hks = range(B_KV_HEADS)
    heads = lambda hk: range(hk * B_GROUP, (hk + 1) * B_GROUP)
    ksl = lambda hk: slice(hk * B_HEAD_DIM, (hk + 1) * B_HEAD_DIM)
    upper, dist = _swa_window()
    no_prev = jnp.where(n_blk > 0, 0.0, NEG)
    ones = jnp.ones((BLOCK, B_HEAD_DIM), BF16)
    with_ones = lambda v, hk: jnp.concatenate([v[:, ksl(hk)].astype(BF16), ones], axis=1)
    qs = [_stack_heads(q_ref, hk) * (B_HEAD_DIM ** -0.5) for hk in hks]
    sink = [jnp.concatenate([jnp.broadcast_to(sk_ref[h:h + 1, 0:1], (BLOCK, 1)) for h in heads(hk)], axis=0)
            for hk in hks]
    s = [jnp.where(upper, _dot_nt(qs[hk], kp[:, ksl(hk)]) + no_prev, _dot_nt(qs[hk], kc[:, ksl(hk)]))
         - jnp.concatenate([ALIBI[h] * dist for h in heads(hk)], axis=0) for hk in hks]
    m = [jnp.maximum(jnp.max(s[hk], axis=-1, keepdims=True), sink[hk]) for hk in hks]
    p = [jnp.exp(s[hk] - m[hk]) for hk in hks]
    p_up = [jnp.where(upper, p[hk], 0.0) for hk in hks]
    oe = [jnp.dot(p_up[hk].astype(BF16), with_ones(vp, hk), preferred_element_type=F32)
          + jnp.dot((p[hk] - p_up[hk]).astype(BF16), with_ones(vc, hk), preferred_element_type=F32) for hk in hks]
    ps = [jnp.exp(sink[hk] - m[hk]) for hk in hks]
    inv = [1.0 / (oe[hk][:, B_HEAD_DIM:B_HEAD_DIM + 1] + ps[hk]) for hk in hks]
    return upper, [(qs[hk], p[hk] * inv[hk], ps[hk] * inv[hk], oe[hk][:, :B_HEAD_DIM] * inv[hk]) for hk in hks]


def _swa_specs():
    qspec = lambda c0: pl.BlockSpec((BLOCK, B_WIDTH), lambda i: (i, c0 // B_WIDTH))
    cur = lambda c0: pl.BlockSpec((BLOCK, LANE), lambda i: (i, c0 // LANE))
    prev = lambda c0: pl.BlockSpec((BLOCK, LANE), lambda i: (jnp.maximum(i - 1, 0), c0 // LANE))
    return qspec, cur, prev


def _carried(carry, refs, n_in, n_out, steps):
    if carry is None:
        return refs
    ci, co = len(carry.ins), len(carry.outs)
    own = refs[:n_in] + refs[n_in + ci:n_in + ci + n_out] + refs[n_in + ci + n_out + co:len(refs) - 3]
    parts = refs[n_in:n_in + ci], refs[n_in + ci + n_out:n_in + ci + n_out + co], refs[len(refs) - 3:]

    @pl.when(pl.program_id(0) == 0)
    def _():
        carry.start(*parts)

    @pl.when(pl.program_id(0) == steps - 1)
    def _():
        carry.finish(*parts)

    return own


def _carry_specs(carry):
    if carry is None:
        return [], [], [], [], []
    return (list(carry.ins), [_ANY] * len(carry.ins), [_ANY] * len(carry.outs), list(carry.outs), carry.scratch())


def _swa_fwd(h, sinks_b, *, name, carry=None):
    t = h.shape[0]
    qspec, cur, prev = _swa_specs()
    c_ins, c_in_specs, c_out_specs, c_outs, c_scratch = _carry_specs(carry)

    def body(*refs):
        q_ref, kc_ref, kp_ref, vc_ref, vp_ref, sk_ref, o_ref, p_ref, ps_ref = _carried(carry, refs, 6, 3, t // BLOCK)
        n_blk = pl.program_id(0)
        _, groups = _swa_group_probs(q_ref, sk_ref, kp_ref[...], kc_ref[...], vp_ref[...], vc_ref[...], n_blk)
        lane = lax.broadcasted_iota(jnp.int32, (BLOCK, LANE), 1)
        sink_probs = jnp.zeros((BLOCK, LANE), F32)
        for hk, (_, p, ps, o) in enumerate(groups):
            for g in range(B_GROUP):
                hq = hk * B_GROUP + g
                rows = slice(g * BLOCK, (g + 1) * BLOCK)
                o_ref[:, hq * B_HEAD_DIM:(hq + 1) * B_HEAD_DIM] = o[rows]
                p_ref[:, hq * BLOCK:(hq + 1) * BLOCK] = p[rows].astype(BF16)
                sink_probs = sink_probs + jnp.where(lane == hq, ps[rows], 0.0)
        ps_ref[...] = sink_probs

    row = lambda w: pl.BlockSpec((BLOCK, w), lambda i: (i, 0))
    outs = pl.pallas_call(
        body, name=name, grid=(t // BLOCK,),
        in_specs=[qspec(C_QB), cur(C_KB), prev(C_KB), cur(C_VB), prev(C_VB),
                  pl.BlockSpec((B_Q_HEADS, LANE), lambda i: (0, 0))] + c_in_specs,
        out_specs=[row(B_WIDTH), row(B_Q_HEADS * BLOCK), row(LANE)] + c_out_specs,
        out_shape=[jax.ShapeDtypeStruct((t, B_WIDTH), F32), jax.ShapeDtypeStruct((t, B_Q_HEADS * BLOCK), BF16),
                   jax.ShapeDtypeStruct((t, LANE), F32)] + c_outs,
        scratch_shapes=c_scratch,
        compiler_params=_cp("arbitrary"))(h, h, h, h, h, sinks_b, *c_ins)
    return outs[:3], outs[3:]


def _rms_gate(o, za, nw):
    outs = []
    for hd in range(A_HEADS):
        oh = o[:, hd * LANE:(hd + 1) * LANE]
        r = lax.rsqrt(jnp.mean(oh * oh, -1, keepdims=True) + RMS_EPS)
        outs.append(oh * r * nw)
    return jnp.concatenate(outs, axis=1) * _silu(za)


def _out_ln(x, oa, ob, h, norm_w, w_out, ln_g, ln_b, *, tm, name, target=None):
    t = x.shape[0]
    last = target is not None

    def body(*refs):
        x_ref, oa_ref, ob_ref, za_ref, zb_ref, nw_ref, w_ref, g_ref, b_ref = refs[:9]
        xn_ref, mx_ref, r_ref = refs[9 + last:12 + last]
        ya = _rms_gate(oa_ref[...], za_ref[...].astype(F32), nw_ref[...])
        yb = ob_ref[...] * _silu(zb_ref[...].astype(F32))
        mixed = jnp.concatenate([ya, yb], axis=1).astype(BF16)
        mx_ref[...] = mixed
        r = DEEPNORM_ALPHA * x_ref[...] + jnp.dot(mixed, w_ref[...], preferred_element_type=F32)
        r_ref[...] = r
        mu = jnp.mean(r, -1, keepdims=True)
        xc = r - mu
        var = jnp.mean(xc * xc, -1, keepdims=True)
        xn = xc * lax.rsqrt(var + LN_EPS) * g_ref[...] + b_ref[...]
        if not last:
            xn_ref[...] = xn
            return
        loss_ref = refs[13]

        @pl.when(pl.program_id(0) == 0)
        def _():
            loss_ref[...] = jnp.zeros_like(loss_ref)

        err = xn - refs[9][...]
        xn_ref[...] = err * (1.0 / D_MODEL)
        loss_ref[...] += 0.5 / D_MODEL * jnp.sum(err * err)

    row = lambda w, c: pl.BlockSpec((tm, w), lambda i: (i, c))
    full = lambda a, b: pl.BlockSpec((a, b), lambda i: (0, 0))
    wide = jax.ShapeDtypeStruct((t, D_MODEL), F32)
    return pl.pallas_call(
        body, name=name, grid=(t // tm,),
        in_specs=[row(D_MODEL, 0), row(A_WIDTH, 0), row(B_WIDTH, 0), row(A_WIDTH, C_ZA // A_WIDTH),
                  row(B_WIDTH, C_ZB // B_WIDTH), full(1, LANE), full(D_MODEL, D_MODEL), full(1, D_MODEL),
                  full(1, D_MODEL)] + [row(D_MODEL, 0)] * last,
        out_specs=[row(D_MODEL, 0), row(D_MODEL, 0), row(D_MODEL, 0)] + [full(SUBLANE, LANE)] * last,
        out_shape=[wide, jax.ShapeDtypeStruct((t, D_MODEL), BF16), wide]
        + [jax.ShapeDtypeStruct((SUBLANE, LANE), F32)] * last,
        compiler_params=_cp("arbitrary" if last else "parallel"))(
        x, oa, ob, h, h, norm_w, w_out, ln_g, ln_b, *([target] if last else []))


def _layer_fwd(x, wt, conv_w, par, sinks_b, norm_w, w_out_bf, ln_g, ln_b, l, carries=None, target=None):
    carries = carries or {}
    h, got_in = _matmul_nt(x, wt, tm=512, name=f"in_proj_{l}", carry=carries.get("in_proj"))
    if callable(w_out_bf):
        w_out_bf = w_out_bf(got_in)
    (q, k, v, bg, bgt, conv), got_pre = _dn_pre(h, conv_w, par, tt=512, name=f"dn_pre_{l}",
                                                carry=carries.get("dn_pre"))
    (u, w, tmat, qk), got_wy = _dn_wy(q, k, v, bg, bgt, name=f"dn_wy_{l}", carry=carries.get("dn_wy"))
    (oa, vn, s_all), got_scan = _dn_scan_fwd(q, k, u, w, qk, bg, name=f"dn_scan_{l}", carry=carries.get("dn_scan"))
    (ob, swa_p, swa_ps), got_swa = _swa_fwd(h, sinks_b, name=f"swa_fwd_{l}", carry=carries.get("swa"))
    xn, mixed, r, *loss = _out_ln(x, oa, ob, h, norm_w, w_out_bf, ln_g, ln_b, tm=512, name=f"out_ln_{l}", target=target)
    if loss:
        xn = (xn, loss[0])
    res = dict(x=x, h=h, q=q, k=k, v=v, bg=bg, bgt=bgt, w=w, tmat=tmat, qk=qk, vn=vn, oa=oa, s_all=s_all,
               mixed=mixed, r=r, w_out=w_out_bf, ob=ob, swa_p=swa_p, swa_ps=swa_ps, conv=conv)
    return xn, res, dict(in_proj=got_in, dn_pre=got_pre, dn_wy=got_wy, dn_scan=got_scan, swa=got_swa)


def _ln_out_bwd(dxn, r, mixed, ln_g, w_out, *, tm, name):
    t = dxn.shape[0]

    def body(dxn_ref, r_ref, mx_ref, g_ref, w_ref, dr_ref, dm_ref, dw_ref, dg_ref, db_ref):
        @pl.when(pl.program_id(0) == 0)
        def _():
            dw_ref[...] = jnp.zeros_like(dw_ref)
            dg_ref[...] = jnp.zeros_like(dg_ref)
            db_ref[...] = jnp.zeros_like(db_ref)

        rr = r_ref[...]
        xc = rr - jnp.mean(rr, -1, keepdims=True)
        rstd = lax.rsqrt(jnp.mean(xc * xc, -1, keepdims=True) + LN_EPS)
        xhat = xc * rstd
        dxn_v = dxn_ref[...]
        dxh = dxn_v * g_ref[...]
        dr = rstd * (dxh - jnp.mean(dxh, -1, keepdims=True) - xhat * jnp.mean(dxh * xhat, -1, keepdims=True))
        dr_ref[...] = dr
        dg_ref[...] += jnp.sum(dxn_v * xhat, axis=0, keepdims=True)
        db_ref[...] += jnp.sum(dxn_v, axis=0, keepdims=True)
        drb = dr.astype(BF16)
        dm_ref[...] = _dot_nt(drb, w_ref[...])
        dw_ref[...] += _dot_tn(mx_ref[...], drb)

    row = pl.BlockSpec((tm, D_MODEL), lambda i: (i, 0))
    full = lambda a, b: pl.BlockSpec((a, b), lambda i: (0, 0))
    big = jax.ShapeDtypeStruct((t, D_MODEL), F32)
    vec = jax.ShapeDtypeStruct((1, D_MODEL), F32)
    return pl.pallas_call(
        body, name=name, grid=(t // tm,),
        in_specs=[row, row, row, full(1, D_MODEL), full(D_MODEL, D_MODEL)],
        out_specs=[row, row, full(D_MODEL, D_MODEL), full(1, D_MODEL), full(1, D_MODEL)],
        out_shape=[big, big, jax.ShapeDtypeStruct((D_MODEL, D_MODEL), F32), vec, vec],
        compiler_params=_cp("arbitrary"))(dxn, r, mixed, ln_g, w_out)


def _dn_post_bwd(dm, oa, h, norm_w, *, tm, name):
    t = oa.shape[0]

    def body(dy_ref, o_ref, za_ref, nw_ref, do_ref, dza_ref, dnw_ref):
        @pl.when(pl.program_id(0) == 0)
        def _():
            dnw_ref[...] = jnp.zeros_like(dnw_ref)

        nw = nw_ref[...]
        dnw = jnp.zeros_like(nw)
        for hd in range(A_HEADS):
            sl = slice(hd * LANE, (hd + 1) * LANE)
            oh, za, dy = o_ref[:, sl], za_ref[:, sl].astype(F32), dy_ref[:, sl]
            rs = lax.rsqrt(jnp.mean(oh * oh, -1, keepdims=True) + RMS_EPS)
            nrm = oh * rs
            gate, dgate = _silu_and_grad(za)
            dza_ref[:, sl] = dy * nrm * nw * dgate
            dn = dy * gate
            dnw = dnw + jnp.sum(dn * nrm, axis=0, keepdims=True)
            dnn = dn * nw
            do_ref[:, sl] = rs * dnn - oh * (rs * rs * rs) * jnp.mean(dnn * oh, -1, keepdims=True)
        dnw_ref[...] += dnw

    row = lambda c: pl.BlockSpec((tm, A_WIDTH), lambda i: (i, c))
    wide = jax.ShapeDtypeStruct((t, A_WIDTH), F32)
    return pl.pallas_call(
        body, name=name, grid=(t // tm,),
        in_specs=[row(0), row(0), row(C_ZA // A_WIDTH), pl.BlockSpec((1, LANE), lambda i: (0, 0))],
        out_specs=[row(0), row(C_ZA // A_WIDTH), pl.BlockSpec((1, LANE), lambda i: (0, 0))],
        out_shape=[wide, jax.ShapeDtypeStruct((t, DH_MAIN), F32), jax.ShapeDtypeStruct((1, LANE), F32)],
        compiler_params=_cp("arbitrary"))(dm, oa, h, norm_w)


def _dn_scan_bwd(q, k, w, qk, bg, do, *, name):
    t = q.shape[0]
    rows = SCAN_ROWS
    per = rows // CHUNK
    n = t // rows

    def body(q_ref, k_ref, w_ref, qk_ref, bg_ref, do_ref, dvn_ref, ds_ref, dstate):
        @pl.when(pl.program_id(0) == 0)
        def _():
            dstate[...] = jnp.zeros_like(dstate)

        heads = range(A_HEADS)
        sl = lambda hd: slice(hd * LANE, (hd + 1) * LANE)
        ds_cur = [dstate[hd] for hd in heads]
        for c in reversed(range(per)):
            rs = slice(c * CHUNK, (c + 1) * CHUNK)
            bg_v = bg_ref[rs, :]
            gcols = [_chunk_gates(bg_v, None, hd)[1] for hd in heads]
            glasts = [gc[CHUNK - 1:CHUNK, :] for gc in gcols]
            for hd in heads:
                ds_ref[c, hd] = ds_cur[hd].astype(BF16)
            pdo = [_dot_tn(qk_ref[rs, hd * CHUNK:(hd + 1) * CHUNK], do_ref[rs, sl(hd)]) for hd in heads]
            qdo = [_dot_tn(q_ref[rs, sl(hd)] * jnp.exp(gcols[hd]), do_ref[rs, sl(hd)]) for hd in heads]
            dvns = [pdo[hd] + _dot(k_ref[rs, sl(hd)] * jnp.exp(glasts[hd] - gcols[hd]), ds_cur[hd]) for hd in heads]
            ds_cur = [qdo[hd] + jnp.exp(glasts[hd]) * ds_cur[hd] - _dot_tn(w_ref[rs, sl(hd)], dvns[hd])
                      for hd in heads]
            for hd in heads:
                dvn_ref[rs, sl(hd)] = dvns[hd]
        for hd in heads:
            dstate[hd] = ds_cur[hd]

    blk = pl.BlockSpec((rows, A_WIDTH), lambda i: (n - 1 - i, 0))
    return pl.pallas_call(
        body, name=name, grid=(n,),
        in_specs=[blk, blk, blk, pl.BlockSpec((rows, A_HEADS * CHUNK), lambda i: (n - 1 - i, 0)),
                  pl.BlockSpec((rows, LANE), lambda i: (n - 1 - i, 0)), blk],
        out_specs=[blk, pl.BlockSpec((per, A_HEADS, LANE, LANE), lambda i: (n - 1 - i, 0, 0, 0))],
        out_shape=[jax.ShapeDtypeStruct((t, A_WIDTH), F32),
                   jax.ShapeDtypeStruct((t // CHUNK, A_HEADS, LANE, LANE), BF16)],
        scratch_shapes=[pltpu.VMEM((A_HEADS, LANE, LANE), F32)],
        compiler_params=_cp("arbitrary"))(q, k, w, qk, bg, do)


def _dn_chunk_bwd(q, k, v, vn, tmat, qk, bg, bgt, s_all, ds_all, dvn, do, *, name, carry=None):
    t = q.shape[0]
    rows = WY_ROWS
    per = rows // CHUNK

    c_ins, c_in_specs, c_out_specs, c_outs, c_scratch = _carry_specs(carry)

    def body(*refs):
        (q_ref, k_ref, v_ref, vn_ref, tm_ref, qk_ref, bg_ref, bgt_ref, s_ref, ds_ref, dvn_ref, do_ref,
         dq_ref, dk_ref, dv_ref, dbg_ref, dbgt_ref) = _carried(carry, refs, 12, 5, t // rows)
        causal, strict, _ = _chunk_masks()
        lane = lax.broadcasted_iota(jnp.int32, (CHUNK, LANE), 1)
        rowi = lax.broadcasted_iota(jnp.int32, (CHUNK, 1), 0)
        sub = lax.broadcasted_iota(jnp.int32, (SUBLANE, CHUNK), 0)
        rs = lambda c: slice(c * CHUNK, (c + 1) * CHUNK)
        sl = lambda hd: slice(hd * LANE, (hd + 1) * LANE)
        hs = lambda hd: slice(hd * CHUNK, (hd + 1) * CHUNK)
        for c0 in range(0, per, WY_GROUP):
            items = [(c, hd) for c in range(c0, c0 + WY_GROUP) for hd in range(A_HEADS)]
            at = lambda ref: [ref[rs(c), sl(hd)] for c, hd in items]
            qs, ks, vs, dos, vns, dvns = at(q_ref), at(k_ref), at(v_ref), at(do_ref), at(vn_ref), at(dvn_ref)
            tmhs = [tm_ref[rs(c), hs(hd)] for c, hd in items]
            ps = [qk_ref[rs(c), hs(hd)] for c, hd in items]
            gates = [_chunk_gates(bg_ref[rs(c), :], bgt_ref[:, rs(c)], hd) for c, hd in items]
            betas = [g[0] for g in gates]
            gcols = [g[1] for g in gates]
            dmats = [jnp.exp(jnp.where(causal, g[1] - g[2], NEG)) for g in gates]
            es = [jnp.exp(gc) for gc in gcols]
            glasts = [gc[CHUNK - 1:CHUNK, :] for gc in gcols]
            eks = [jnp.exp(gl - gc) for gl, gc in zip(glasts, gcols)]
            kbs = [kh * b for kh, b in zip(ks, betas)]
            vbs = [vh * b for vh, b in zip(vs, betas)]
            kbes = [kb * e for kb, e in zip(kbs, es)]

            a_s = [jnp.where(strict, _dot_nt(kb, kh) * dm, 0.0) for kb, kh, dm in zip(kbs, ks, dmats)]
            dps = [jnp.where(causal, _dot_nt(doh, vnh), 0.0) for doh, vnh in zip(dos, vns)]
            rows2 = lambda a, b: jnp.concatenate([a, b], axis=0)
            cols2 = lambda a, b: jnp.concatenate([a, b], axis=1)
            by_s = [_dot_nt(rows2(doh, dvnh), s_ref[c, hd]) for doh, dvnh, (c, hd) in zip(dos, dvns, items)]
            dqds = [m[:CHUNK] for m in by_s]
            dws = [-m[CHUNK:] for m in by_s]
            dkds = [_dot_nt(vnh, ds_ref[c, hd]) for vnh, (c, hd) in zip(vns, items)]
            dgts = [jnp.sum(s_ref[c, hd].astype(F32) * ds_ref[c, hd].astype(F32), keepdims=True) for c, hd in items]
            pairs = [cols2(dvnh, dw) for dvnh, dw in zip(dvns, dws)]
            by_t = [_dot_tn(tmh, pr) for tmh, pr in zip(tmhs, pairs)]
            dvbs = [m[:, :LANE] for m in by_t]
            dkbes = [m[:, LANE:] for m in by_t]
            dts = [_dot_nt(pr, cols2(vb, kbe)) for pr, vb, kbe in zip(pairs, vbs, kbes)]
            xs = [_dot_nt(dt, tmh) for dt, tmh in zip(dts, tmhs)]
            das = [jnp.where(strict, -_dot_tn(tmh, x), 0.0) for tmh, x in zip(tmhs, xs)]
            dmas = [da * dm for da, dm in zip(das, dmats)]
            dmps = [dp * dm for dp, dm in zip(dps, dmats)]
            stacked = [rows2(dma, dmp) for dma, dmp in zip(dmas, dmps)]
            by_k = [_dot(st, kh) for st, kh in zip(stacked, ks)]
            dkbs = [m[:CHUNK] + dkbe * e for m, dkbe, e in zip(by_k, dkbes, es)]
            for i, (c, hd) in enumerate(items):
                dq_ref[rs(c), sl(hd)] = by_k[i][CHUNK:] + dqds[i] * es[i]
                dk_ref[rs(c), sl(hd)] = (_dot_tn(stacked[i], rows2(kbs[i], qs[i])) + dkds[i] * eks[i]
                                         + dkbs[i] * betas[i])
                dv_ref[rs(c), sl(hd)] = dvbs[i] * betas[i]
            for c in range(c0, c0 + WY_GROUP):
                acc = jnp.zeros((CHUNK, LANE), F32)
                acc_t = jnp.zeros((SUBLANE, CHUNK), F32)
                for i, (ci, hd) in enumerate(items):
                    if ci != c:
                        continue
                    gmat = das[i] * a_s[i] + dps[i] * ps[i]
                    rk = jnp.sum(dkds[i] * ks[i], -1, keepdims=True) * eks[i]
                    de = jnp.sum(dqds[i] * qs[i] + dkbes[i] * kbs[i], -1, keepdims=True)
                    dglast = jnp.sum(rk, keepdims=True) + dgts[i] * jnp.exp(glasts[i])
                    dgc = (jnp.sum(gmat, -1, keepdims=True) + de * es[i] - rk
                           + jnp.where(rowi == CHUNK - 1, dglast, 0.0))
                    dbeta = jnp.sum(dkbs[i] * ks[i] + dvbs[i] * vs[i], -1, keepdims=True)
                    acc = acc + jnp.where(lane == hd, dbeta, 0.0) + jnp.where(lane == A_HEADS + hd, dgc, 0.0)
                    acc_t = acc_t + jnp.where(sub == A_HEADS + hd, -jnp.sum(gmat, axis=0, keepdims=True), 0.0)
                dbg_ref[rs(c), :] = acc
                dbgt_ref[:, rs(c)] = acc_t

    blk = pl.BlockSpec((rows, A_WIDTH), lambda i: (i, 0))
    half = pl.BlockSpec((rows, A_HEADS * CHUNK), lambda i: (i, 0))
    col = pl.BlockSpec((rows, LANE), lambda i: (i, 0))
    rowf = pl.BlockSpec((SUBLANE, rows), lambda i: (0, i))
    st = pl.BlockSpec((per, A_HEADS, LANE, LANE), lambda i: (i, 0, 0, 0))
    wide = jax.ShapeDtypeStruct((t, A_WIDTH), F32)
    outs = pl.pallas_call(
        body, name=name, grid=(t // rows,),
        in_specs=[blk, blk, blk, blk, half, half, col, rowf, st, st, blk, blk] + c_in_specs,
        out_specs=[blk, blk, blk, col, rowf] + c_out_specs,
        out_shape=[wide, wide, wide, jax.ShapeDtypeStruct((t, LANE), F32),
                   jax.ShapeDtypeStruct((SUBLANE, t), F32)] + c_outs,
        scratch_shapes=c_scratch,
        compiler_params=_cp("arbitrary"))(q, k, v, vn, tmat, qk, bg, bgt, s_all, ds_all, dvn, do, *c_ins)
    return outs[:5], outs[5:]


def _dn_pre_bwd(h, conv, par, dq, dk, dv, dbg, dbgt, *, tt, name):
    t = h.shape[0]
    cw = 3 * A_WIDTH

    def body(conv_ref, bgi_ref, par_ref, dq_ref, dk_ref, dv_ref, dbg_ref, dbgt_ref, dc_ref, dbgi_ref, dpar_ref):
        i = pl.program_id(0)

        @pl.when(i == 0)
        def _():
            dpar_ref[...] = jnp.zeros_like(dpar_ref)

        s, ds = _silu_and_grad(conv_ref[...])
        for hd in range(A_HEADS):
            sl = slice(hd * LANE, (hd + 1) * LANE)
            for base, d_ref, scale in ((0, dq_ref, A_HEAD_DIM ** -0.5), (A_WIDTH, dk_ref, 1.0)):
                csl = slice(base + hd * LANE, base + (hd + 1) * LANE)
                tq = s[:, base + hd * LANE:base + (hd + 1) * LANE]
                dy = d_ref[:, sl]
                rq = lax.rsqrt(jnp.sum(tq * tq, -1, keepdims=True) + L2_EPS)
                dtq = scale * (rq * dy - tq * (rq * rq * rq) * jnp.sum(dy * tq, -1, keepdims=True))
                dc_ref[:, csl] = dtq * ds[:, base + hd * LANE:base + (hd + 1) * LANE]
        dc_ref[:, 2 * A_WIDTH:] = dv_ref[...] * ds[:, 2 * A_WIDTH:]
        raw = bgi_ref[...].astype(F32)
        lane = lax.broadcasted_iota(jnp.int32, raw.shape, 1)
        is_b = lane < A_HEADS
        is_a = (lane >= A_HEADS) & (lane < 2 * A_HEADS)
        rows_t = jnp.concatenate([dbgt_ref[...], jnp.zeros((LANE - SUBLANE, tt), F32)], axis=0)
        dbg_v = dbg_ref[...] + jnp.where(is_a, jnp.transpose(rows_t), 0.0)
        dbg_v = jnp.where(is_a, _dot_hi(_chunk_tri(tt, lower=False), jnp.where(is_a, dbg_v, 0.0)), dbg_v)
        beta = _sigmoid(raw)
        z = raw + par_ref[1:2, :]
        neg_ea = -jnp.exp(par_ref[0:1, :])
        g = neg_ea * _softplus(z)
        da = dbg_v * neg_ea * _sigmoid(z)
        dbgi_ref[...] = jnp.where(is_b, dbg_v * beta * (1.0 - beta), jnp.where(is_a, da, 0.0))
        dpar_ref[0:1, :] += jnp.sum(jnp.where(is_a, dbg_v * g, 0.0), axis=0, keepdims=True)
        dpar_ref[1:2, :] += jnp.sum(jnp.where(is_a, da, 0.0), axis=0, keepdims=True)

    wide = pl.BlockSpec((tt, A_WIDTH), lambda i: (i, 0))
    return pl.pallas_call(
        body, name=name, grid=(t // tt,),
        in_specs=[pl.BlockSpec((tt, cw), lambda i: (i, 0)),
                  pl.BlockSpec((tt, LANE), lambda i: (i, C_BG // LANE)),
                  pl.BlockSpec((SUBLANE, LANE), lambda i: (0, 0)),
                  wide, wide, wide, pl.BlockSpec((tt, LANE), lambda i: (i, 0)),
                  pl.BlockSpec((SUBLANE, tt), lambda i: (0, i))],
        out_specs=[pl.BlockSpec((tt, cw), lambda i: (i, 0)), pl.BlockSpec((tt, LANE), lambda i: (i, 0)),
                   pl.BlockSpec((SUBLANE, LANE), lambda i: (0, 0))],
        out_shape=[jax.ShapeDtypeStruct((t, cw), F32), jax.ShapeDtypeStruct((t, LANE), F32),
                   jax.ShapeDtypeStruct((SUBLANE, LANE), F32)],
        compiler_params=_cp("arbitrary"))(conv, h, par, dq, dk, dv, dbg, dbgt)


def _conv_bwd(dc, h, conv_w, dh, *, tt, name):
    t = dc.shape[0]
    cw = 3 * A_WIDTH
    nb = t // tt

    def body(dc_ref, after_ref, pre_ref, cw_ref, dh_in_ref, dpre_ref, dcw_ref):
        i = pl.program_id(0)

        @pl.when(i == 0)
        def _():
            dcw_ref[...] = jnp.zeros_like(dcw_ref)

        dcv = dc_ref[...]
        after = jnp.where(i < nb - 1, after_ref[...], 0.0)
        cur = pre_ref[...].astype(F32)
        w = cw_ref[...]
        acc = dcv * w[CONV_K - 1:CONV_K, :]
        dcw_ref[CONV_K - 1:CONV_K, :] += jnp.sum(dcv * cur, axis=0, keepdims=True)
        for s in range(1, CONV_K):
            j = CONV_K - 1 - s
            up = _shift_up(dcv, after, s)
            acc = acc + up * w[j:j + 1, :]
            dcw_ref[j:j + 1, :] += jnp.sum(up * cur, axis=0, keepdims=True)
        dpre_ref[...] = acc

    return pl.pallas_call(
        body, name=name, grid=(nb,),
        in_specs=[pl.BlockSpec((tt, cw), lambda i: (i, 0)),
                  pl.BlockSpec((SUBLANE, cw), lambda i: (jnp.minimum((i + 1) * (tt // SUBLANE), t // SUBLANE - 1), 0)),
                  pl.BlockSpec((tt, cw), lambda i: (i, 0)),
                  pl.BlockSpec((CONV_K, cw), lambda i: (0, 0)), _ANY],
        out_specs=[pl.BlockSpec((tt, cw), lambda i: (i, 0)), pl.BlockSpec((SUBLANE, cw), lambda i: (0, 0))],
        out_shape=[jax.ShapeDtypeStruct(dh.shape, F32), jax.ShapeDtypeStruct((SUBLANE, cw), F32)],
        input_output_aliases={4: 0},
        compiler_params=_cp("arbitrary"))(dc, dc, h, conv_w, dh)


def _swa_bwd(h, dm, ob, probs, sink_probs, dh, *, name, carry=None):
    t = h.shape[0]
    qspec, cur, prev = _swa_specs()
    c_ins, c_in_specs, c_out_specs, c_outs, c_scratch = _carry_specs(carry)

    def body(*refs):
        (q_ref, kc_ref, kp_ref, vc_ref, vp_ref, zb_ref, dy_ref, ob_ref, p_ref, ps_ref, dh_in_ref,
         dqz_ref, dk_ref, dv_ref, dsk_ref) = _carried(carry, refs, 11, 4, t // BLOCK)
        n_blk = pl.program_id(0)

        @pl.when(n_blk == 0)
        def _():
            dk_ref[...] = jnp.zeros_like(dk_ref)
            dv_ref[...] = jnp.zeros_like(dv_ref)
            dsk_ref[...] = jnp.zeros_like(dsk_ref)

        kp, kc, vp, vc = kp_ref[...], kc_ref[...], vp_ref[...], vc_ref[...]
        scale = B_HEAD_DIM ** -0.5
        hks = range(B_KV_HEADS)
        ksl = lambda hk: slice(hk * B_HEAD_DIM, (hk + 1) * B_HEAD_DIM)
        heads = lambda hk: range(hk * B_GROUP, (hk + 1) * B_GROUP)
        upper, _ = _swa_window()
        groups = [(_stack_heads(q_ref, hk) * scale,
                   jnp.concatenate([p_ref[:, h * BLOCK:(h + 1) * BLOCK].astype(F32) for h in heads(hk)], axis=0),
                   jnp.concatenate([ps_ref[:, h:h + 1] for h in heads(hk)], axis=0),
                   _stack_heads(ob_ref, hk)) for hk in hks]
        zbs = [_stack_heads(zb_ref, hk) for hk in hks]
        dys = [_stack_heads(dy_ref, hk) for hk in hks]
        gates = [_silu_and_grad(zbs[hk]) for hk in hks]
        dos = [dys[hk] * gates[hk][0] for hk in hks]
        deltas = [jnp.sum(dos[hk] * groups[hk][3], -1, keepdims=True) for hk in hks]
        dps = [jnp.where(upper, _dot_nt(dos[hk], vp[:, ksl(hk)]), _dot_nt(dos[hk], vc[:, ksl(hk)])) for hk in hks]
        dss = [groups[hk][1] * (dps[hk] - deltas[hk]) for hk in hks]
        ds_up = [jnp.where(upper, dss[hk], 0.0) for hk in hks]
        ds_lo = [dss[hk] - ds_up[hk] for hk in hks]
        p_up = [jnp.where(upper, groups[hk][1], 0.0) for hk in hks]
        p_lo = [groups[hk][1] - p_up[hk] for hk in hks]
        dqs = [(_dot(ds_up[hk], kp[:, ksl(hk)]) + _dot(ds_lo[hk], kc[:, ksl(hk)])) * scale for hk in hks]
        dk_prev = [_dot_tn(ds_up[hk], groups[hk][0]) for hk in hks]
        dk_cur = [_dot_tn(ds_lo[hk], groups[hk][0]) for hk in hks]
        dv_prev = [_dot_tn(p_up[hk], dos[hk]) for hk in hks]
        dv_cur = [_dot_tn(p_lo[hk], dos[hk]) for hk in hks]
        for hk in hks:
            dzb = dys[hk] * groups[hk][3] * gates[hk][1]
            dsink = groups[hk][2] * deltas[hk]
            for g in range(B_GROUP):
                hq = hk * B_GROUP + g
                rows = slice(g * BLOCK, (g + 1) * BLOCK)
                qsl = slice(hq * B_HEAD_DIM, (hq + 1) * B_HEAD_DIM)
                dqz_ref[:, qsl] = dqs[hk][rows]
                dqz_ref[:, B_WIDTH + hq * B_HEAD_DIM:B_WIDTH + (hq + 1) * B_HEAD_DIM] = dzb[rows]
                dsk_ref[hq:hq + 1, :] += -jnp.sum(dsink[rows], keepdims=True)
        at_cur = pl.ds(pl.multiple_of(n_blk * BLOCK, BLOCK), BLOCK)
        at_prev = pl.ds(pl.multiple_of(jnp.maximum(n_blk - 1, 0) * BLOCK, BLOCK), BLOCK)
        dk_ref[at_prev, :] += jnp.concatenate(dk_prev, axis=1)
        dv_ref[at_prev, :] += jnp.concatenate(dv_prev, axis=1)
        dk_ref[at_cur, :] += jnp.concatenate(dk_cur, axis=1)
        dv_ref[at_cur, :] += jnp.concatenate(dv_cur, axis=1)

    narrow = jax.ShapeDtypeStruct((t, B_KV_WIDTH), F32)
    res = lambda a, b: pl.BlockSpec((a, b), lambda i: (0, 0))
    row = lambda w: pl.BlockSpec((BLOCK, w), lambda i: (i, 0))
    outs = pl.pallas_call(
        body, name=name, grid=(t // BLOCK,),
        in_specs=[qspec(C_QB), cur(C_KB), prev(C_KB), cur(C_VB), prev(C_VB), qspec(C_ZB),
                  pl.BlockSpec((BLOCK, B_WIDTH), lambda i: (i, 1)), row(B_WIDTH), row(B_Q_HEADS * BLOCK), row(LANE),
                  _ANY] + c_in_specs,
        out_specs=[pl.BlockSpec((BLOCK, 2 * B_WIDTH), lambda i: (i, C_QB // (2 * B_WIDTH))),
                   res(t, B_KV_WIDTH), res(t, B_KV_WIDTH), res(B_Q_HEADS, LANE)] + c_out_specs,
        out_shape=[jax.ShapeDtypeStruct(dh.shape, F32), narrow, narrow,
                   jax.ShapeDtypeStruct((B_Q_HEADS, LANE), F32)] + c_outs,
        scratch_shapes=c_scratch,
        input_output_aliases={10: 0},
        compiler_params=_cp("arbitrary"))(h, h, h, h, h, h, dm, ob, probs, sink_probs, dh, *c_ins)
    return outs[:4], outs[4:]


def _in_proj_dw(dh_main, dh_tail, x, *, tk, name):
    t, n = x.shape

    def body(a_ref, t_ref, x_ref, o_ref, ot_ref):
        @pl.when(pl.program_id(0) == 0)
        def _():
            o_ref[...] = jnp.zeros_like(o_ref)
            ot_ref[...] = jnp.zeros_like(ot_ref)

        xb = x_ref[...].astype(BF16)
        o_ref[...] += _dot_tn(a_ref[...], xb)
        ot_ref[...] += _dot_tn(t_ref[...], xb)

    row = lambda a: pl.BlockSpec((tk, a.shape[1]), lambda kk: (kk, 0))
    acc = lambda a: pl.BlockSpec((a.shape[1], n), lambda kk: (0, 0))
    return pl.pallas_call(
        body, name=name, grid=(t // tk,), in_specs=[row(dh_main), row(dh_tail), row(x)],
        out_specs=[acc(dh_main), acc(dh_tail)],
        out_shape=[jax.ShapeDtypeStruct((a.shape[1], n), F32) for a in (dh_main, dh_tail)],
        compiler_params=_cp("arbitrary"))(dh_main, dh_tail, x)


def _in_proj_dx(dh_main, dh_tail, wt, dr, *, tm, name, carry=None):
    t, n_main = dh_main.shape
    n_tail = dh_tail.shape[1]
    c_ins, c_in_specs, c_out_specs, c_outs, c_scratch = _carry_specs(carry)

    def body(*refs):
        a_ref, t_ref, wa_ref, wt_ref, r_ref, o_ref = _carried(carry, refs, 5, 1, t // tm)
        o_ref[...] = _dot(a_ref[...], wa_ref[...]) + _dot(t_ref[...], wt_ref[...]) + DEEPNORM_ALPHA * r_ref[...]

    row = lambda w: pl.BlockSpec((tm, w), lambda i: (i, 0))
    outs = pl.pallas_call(
        body, name=name, grid=(t // tm,),
        in_specs=[row(n_main), row(n_tail), pl.BlockSpec((n_main, D_MODEL), lambda i: (0, 0)),
                  pl.BlockSpec((n_tail, D_MODEL), lambda i: (n_main // n_tail, 0)), row(D_MODEL)] + c_in_specs,
        out_specs=[row(D_MODEL)] + c_out_specs,
        out_shape=[jax.ShapeDtypeStruct((t, D_MODEL), F32)] + c_outs,
        scratch_shapes=c_scratch,
        compiler_params=_cp("arbitrary"))(dh_main, dh_tail, wt, wt, dr, *c_ins)
    return outs[0], outs[1:]


def _layer_bwd(dxn, res, wt, conv_w, par, sinks_b, norm_w, w_out_bf, ln_g, l, carries=None, carry_dx=None):
    carries = carries or {}
    w_out_bf = res["w_out"]
    dr, dm, dw_out, dln_g, dln_b = _ln_out_bwd(dxn, res["r"], res["mixed"], ln_g, w_out_bf, tm=512, name=f"ln_out_bwd_{l}")
    h = res["h"]
    do, dh, dnw = _dn_post_bwd(dm, res["oa"], h, norm_w, tm=512, name=f"dn_post_bwd_{l}")
    dvn, ds_all = _dn_scan_bwd(res["q"], res["k"], res["w"], res["qk"], res["bg"], do, name=f"dn_scan_bwd_{l}")
    (dq, dk, dv, dbg, dbgt), got_chunk = _dn_chunk_bwd(
        res["q"], res["k"], res["v"], res["vn"], res["tmat"], res["qk"], res["bg"], res["bgt"], res["s_all"], ds_all,
        dvn, do, name=f"dn_chunk_bwd_{l}", carry=carries.get("dn_chunk"))
    dc, dbgi, dpar = _dn_pre_bwd(h, res["conv"], par, dq, dk, dv, dbg, dbgt, tt=512, name=f"dn_pre_bwd_{l}")
    dh, dcw = _conv_bwd(dc, h, conv_w, dh, tt=512, name=f"conv_bwd_{l}")
    (dh, dkb, dvb, dsk), got_swa = _swa_bwd(h, dm, res["ob"], res["swa_p"], res["swa_ps"], dh, name=f"swa_bwd_{l}",
                                            carry=carries.get("swa"))
    carried = dict(dn_chunk=got_chunk, swa=got_swa)
    dh_tail = jnp.concatenate([dkb, dvb, dbgi], axis=1)
    dwt_main, dwt_tail = _in_proj_dw(dh, dh_tail, res["x"], tk=512, name=f"in_proj_dw_{l}")
    grads = dict(w_in=(dwt_main, dwt_tail), conv_w=dcw[:CONV_K], a_log=dpar[0, A_HEADS:2 * A_HEADS],
                 dt_bias=dpar[1, A_HEADS:2 * A_HEADS], norm_w=dnw[0], sinks=dsk[:, 0], w_out=dw_out,
                 ln_g=dln_g[0], ln_b=dln_b[0])
    dx, carried_dx = _in_proj_dx(dh, dh_tail, wt, dr, tm=512, name=f"in_proj_dx_{l}",
                                 carry=None if carry_dx is None else carry_dx(grads))
    return dx, grads, carried, carried_dx


def _layer_args(wt, conv_w, a_log, dt_bias, sinks, norm_w, w_out_bf):
    return (wt, conv_w, _gate_params(a_log, dt_bias), jnp.broadcast_to(sinks[:, None], (B_Q_HEADS, LANE)),
            norm_w[None], w_out_bf)


def _local_step(x, target, args0, args1, ln_g, ln_b, gathers=None, reduce1=None, reduce0=None):
    assert DEPTH == 2
    x1, res0, got = _layer_fwd(x, *args0, ln_g[0][None], ln_b[0][None], 0, carries=gathers)
    if gathers is not None:
        args1 = args1(got)
    (dx, loss_tile), res1, _ = _layer_fwd(x1, *args1, ln_g[1][None], ln_b[1][None], 1, target=target)
    dx, grads1, _, _ = _layer_bwd(dx, res1, *args1, ln_g[1][None], 1)
    carries = None if reduce1 is None else reduce1(grads1)
    carry_dx = None if reduce0 is None else (lambda grads0: reduce0(grads0, grads1, loss_tile))
    dx, grads0, landed1, landed0 = _layer_bwd(dx, res0, *args0, ln_g[0][None], 0, carries=carries, carry_dx=carry_dx)
    return loss_tile, dx, [grads0, grads1], landed1, landed0


_ANY = pl.BlockSpec(memory_space=pl.ANY)
_MESH = pl.DeviceIdType.MESH


HALF = D_MODEL // 2


class _Exchange:
    def __init__(self, ins, outs, n_remote, n_local, plan):
        self.ins, self.outs, self.n_remote, self.n_local, self.plan = tuple(ins), tuple(outs), n_remote, n_local, plan

    def scratch(self):
        return [pltpu.SemaphoreType.DMA((self.n_remote,)), pltpu.SemaphoreType.DMA((self.n_remote,)),
                pltpu.SemaphoreType.DMA((max(self.n_local, 1),))]

    def _copies(self, in_refs, out_refs, sems, arriving):
        send_sems, recv_sems, local_sems = sems
        local, sends, recvs = self.plan(in_refs, out_refs)
        loc = [pltpu.make_async_copy(s, d, local_sems.at[i]) for i, (s, d) in enumerate(local)]
        rem = [pltpu.make_async_remote_copy(src_ref=s, dst_ref=recvs[i] if arriving else d, send_sem=send_sems.at[i],
                                            recv_sem=recv_sems.at[i], device_id=peer, device_id_type=_MESH)
               for i, (s, d, peer) in enumerate(sends)]
        return loc, rem

    def start(self, in_refs, out_refs, sems):
        loc, rem = self._copies(in_refs, out_refs, sems, arriving=False)
        for cp in loc + rem:
            cp.start()

    def finish(self, in_refs, out_refs, sems):
        loc, rem = self._copies(in_refs, out_refs, sems, arriving=True)
        for cp in rem:
            cp.wait_recv()
        for cp in rem:
            cp.wait_send()
        for cp in loc:
            cp.wait()


def _run_exchange(ex, *, name):
    n_in, n_out = len(ex.ins), len(ex.outs)

    def body(*refs):
        parts = refs[:n_in], refs[n_in:n_in + n_out], refs[n_in + n_out:]
        ex.start(*parts)
        ex.finish(*parts)

    return pl.pallas_call(body, name=name, in_specs=[_ANY] * n_in, out_specs=[_ANY] * n_out, out_shape=list(ex.outs),
                          scratch_shapes=ex.scratch())(*ex.ins)


def _place():
    x, y, c = lax.axis_index("x"), lax.axis_index("y"), lax.axis_index("c")
    return x, y, c, [(1 - x, y), (x, 1 - y), (1 - x, 1 - y)]


def _gather_exchange(arrays):
    n = len(arrays)

    def plan(src, dst):
        x, y, c, chips = _place()
        me = 2 * x + y
        local = [(src[k], dst[k].at[me]) for k in range(n)]
        sends = [(src[k], dst[k].at[me], (px, py, c)) for k in range(n) for px, py in chips]
        recvs = [dst[k].at[2 * px + py] for k in range(n) for px, py in chips]
        return local, sends, recvs

    return _Exchange(arrays, [jax.ShapeDtypeStruct((N_SHARD,) + a.shape, a.dtype) for a in arrays], 3 * n, n, plan)


def _gather_two_level(pack, conv_w, *, name):
    rows = pack.shape[0]
    part_rows = rows // 2

    def body(pack_ref, conv_ref, land_ref, conv_land_ref, send1, recv1, send2, recv2, csend, crecv, local_sems):
        x, y, c, chips = _place()
        me = 2 * x + y
        sibling = (x, y, 1 - c)
        part = lambda core: pl.ds(pl.multiple_of(core * part_rows, 16), part_rows)
        remote = lambda src, dst, ss, rs, to: pltpu.make_async_remote_copy(
            src_ref=src, dst_ref=dst, send_sem=ss, recv_sem=rs, device_id=to, device_id_type=_MESH)
        local = [pltpu.make_async_copy(pack_ref, land_ref.at[me], local_sems.at[0]),
                 pltpu.make_async_copy(conv_ref, conv_land_ref.at[me], local_sems.at[1])]
        for cp in local:
            cp.start()
        first = [remote(pack_ref.at[part(c)], land_ref.at[me, part(c)], send1.at[j], recv1.at[j], (px, py, c))
                 for j, (px, py) in enumerate(chips)]
        convs = [remote(conv_ref, conv_land_ref.at[me], csend.at[j], crecv.at[j], (px, py, c))
                 for j, (px, py) in enumerate(chips)]
        for cp in first + convs:
            cp.start()
        passed = []
        for j, (px, py) in enumerate(chips):
            slot = 2 * px + py
            remote(pack_ref.at[part(c)], land_ref.at[slot, part(c)], send1.at[j], recv1.at[j], (px, py, c)).wait_recv()
            cp = remote(land_ref.at[slot, part(c)], land_ref.at[slot, part(c)], send2.at[j], recv2.at[j], sibling)
            cp.start()
            passed.append(cp)
        for j, (px, py) in enumerate(chips):
            slot = 2 * px + py
            remote(land_ref.at[slot, part(1 - c)], land_ref.at[slot, part(1 - c)], send2.at[j], recv2.at[j],
                   sibling).wait_recv()
            remote(conv_ref, conv_land_ref.at[slot], csend.at[j], crecv.at[j], (px, py, c)).wait_recv()
        for cp in first + convs + passed:
            cp.wait_send()
        for cp in local:
            cp.wait()

    sems = [pltpu.SemaphoreType.DMA((3,))] * 6 + [pltpu.SemaphoreType.DMA((2,))]
    return pl.pallas_call(
        body, name=name, in_specs=[_ANY, _ANY], out_specs=[_ANY, _ANY],
        out_shape=[jax.ShapeDtypeStruct((N_SHARD,) + pack.shape, pack.dtype),
                   jax.ShapeDtypeStruct((N_SHARD,) + conv_w.shape, conv_w.dtype)],
        scratch_shapes=sems)(pack, conv_w)


def _half(core):
    return pl.ds(pl.multiple_of(core * HALF, HALF), HALF)


def _reduce_scatter_exchange(g, row0, rows):
    def plan(src, dst):
        x, y, c, chips = _place()
        peers = [(px, py, c if t == 0 else 1 - c) for px, py in chips for t in (0, 1)] + [(x, y, 1 - c)]
        sends = [(src[0].at[2 * px + py, pl.ds(row0, rows), _half(pc)], dst[0].at[k], (px, py, pc))
                 for k, (px, py, pc) in enumerate(peers)]
        return [], sends, [dst[0].at[k] for k in range(7)]

    return _Exchange([g], [jax.ShapeDtypeStruct((7, rows, HALF), g.dtype)], 7, 0, plan)


def _pair_window_exchange(g):
    def plan(src, dst):
        x, y, c, _ = _place()
        return [], [(src[0].at[:, :, _half(1 - c)], dst[0], (x, y, 1 - c))], [dst[0]]

    return _Exchange([g], [jax.ShapeDtypeStruct(g.shape[:2] + (HALF,), g.dtype)], 1, 0, plan)


def _chip_scatter_exchange(p, small):
    def plan(src, dst):
        x, y, c, chips = _place()
        mine = 4 * x + 2 * y + c
        peers = [(px, py, c if t == 0 else 1 - c) for px, py in chips for t in (0, 1)] + [(x, y, 1 - c)]
        sends = [(src[0].at[2 * px + py], dst[0].at[j], (px, py, c)) for j, (px, py) in enumerate(chips)]
        recvs = [dst[0].at[j] for j in range(3)]
        sends += [(src[1], dst[1].at[mine], peer) for peer in peers]
        recvs += [dst[1].at[4 * px + 2 * py + pc] for px, py, pc in peers]
        return [(src[1], dst[1].at[mine])], sends, recvs

    outs = [jax.ShapeDtypeStruct((3,) + p.shape[1:], p.dtype), jax.ShapeDtypeStruct((8,) + small.shape, small.dtype)]
    return _Exchange([p, small], outs, 10, 1, plan)


def _share_exchange(arrays):
    n = len(arrays)

    def plan(src, dst):
        x, y, c, _ = _place()
        return [], [(src[k], dst[k], (x, y, 1 - c)) for k in range(n)], [dst[k] for k in range(n)]

    return _Exchange(arrays, [jax.ShapeDtypeStruct(a.shape, a.dtype) for a in arrays], n, 0, plan)


def _sum_scatter(g, lands, me, core, *, tc, name):
    rows = g.shape[1]
    per = HALF // tc
    n = len(lands)

    def body(*refs):
        g_ref, land_refs, o_ref = refs[1], refs[2:2 + n], refs[2 + n]
        at = 0
        for land_ref in land_refs:
            run = slice(at, at + land_ref.shape[1])
            acc = g_ref[run, :].astype(F32)
            for k in range(7):
                acc = acc + land_ref[k].astype(F32)
            o_ref[run, :] = acc
            at = run.stop

    return pl.pallas_call(
        body, name=name, out_shape=jax.ShapeDtypeStruct((rows, HALF), F32), compiler_params=_cp("parallel"),
        grid_spec=pltpu.PrefetchScalarGridSpec(
            num_scalar_prefetch=1, grid=(per,),
            in_specs=[pl.BlockSpec((None, rows, tc), lambda i, w: (w[0], 0, w[1] * per + i))]
            + [pl.BlockSpec((7, a.shape[1], tc), lambda i, w: (0, 0, i)) for a in lands],
            out_specs=pl.BlockSpec((rows, tc), lambda i, w: (0, i))))(
        jnp.stack([me, core]).astype(jnp.int32), g, *lands)


def _pair_add(g, land, core, *, name):
    n, rows, _ = g.shape

    def body(core_ref, g_ref, land_ref, o_ref):
        o_ref[...] = (g_ref[...].astype(F32) + land_ref[...].astype(F32)).astype(o_ref.dtype)

    blk = pl.BlockSpec((1, rows, HALF), lambda i, w: (i, 0, 0))
    return pl.pallas_call(
        body, name=name, out_shape=jax.ShapeDtypeStruct((n, rows, HALF), g.dtype), compiler_params=_cp("parallel"),
        grid_spec=pltpu.PrefetchScalarGridSpec(
            num_scalar_prefetch=1, grid=(n,),
            in_specs=[pl.BlockSpec((1, rows, HALF), lambda i, w: (i, 0, w[0])), blk], out_specs=blk))(
        jnp.reshape(core, (1,)).astype(jnp.int32), g, land)


def _sum_chips(p, land, me, *, tc, name):
    rows = p.shape[1]

    def body(me_ref, p_ref, land_ref, o_ref):
        acc = p_ref[...].astype(F32)
        for k in range(3):
            acc = acc + land_ref[k].astype(F32)
        o_ref[...] = acc

    return pl.pallas_call(
        body, name=name, out_shape=jax.ShapeDtypeStruct((rows, HALF), F32), compiler_params=_cp("parallel"),
        grid_spec=pltpu.PrefetchScalarGridSpec(
            num_scalar_prefetch=1, grid=(HALF // tc,),
            in_specs=[pl.BlockSpec((None, rows, tc), lambda i, w: (w[0], 0, i)),
                      pl.BlockSpec((3, rows, tc), lambda i, w: (0, 0, i))],
            out_specs=pl.BlockSpec((rows, tc), lambda i, w: (0, i))))(
        jnp.reshape(me, (1,)).astype(jnp.int32), p, land)


def _sum_slots(a, *, name):
    n = a.shape[0]

    def body(a_ref, o_ref):
        acc = a_ref[0]
        for k in range(1, n):
            acc = acc + a_ref[k]
        o_ref[...] = acc

    return pl.pallas_call(body, name=name, out_shape=jax.ShapeDtypeStruct(a.shape[1:], a.dtype))(a)


def _elementwise(fn, ins, n_out, block, *, name):
    shape = ins[0].shape
    grid = tuple(s // b for s, b in zip(shape, block))
    n_in = len(ins)

    def body(*refs):
        outs = fn(*[r[...] for r in refs[:n_in]])
        for o_ref, val in zip(refs[n_in:], outs):
            o_ref[...] = val

    spec = pl.BlockSpec(block, lambda i, j, k: (i, j, k))
    return pl.pallas_call(body, name=name, grid=grid, in_specs=[spec] * n_in, out_specs=[spec] * n_out,
                          out_shape=[jax.ShapeDtypeStruct(shape, F32)] * n_out,
                          compiler_params=_cp(*["parallel"] * 3))(*ins)


def _adamw_math(w, g, m, v):
    mn = ADAM_B1 * m + (1.0 - ADAM_B1) * g
    vn = ADAM_B2 * v + (1.0 - ADAM_B2) * (g * g)
    m_hat = mn / (1.0 - ADAM_B1 ** ADAM_STEP)
    v_hat = vn / (1.0 - ADAM_B2 ** ADAM_STEP)
    return -ADAM_LR * (m_hat / (jnp.sqrt(v_hat) + ADAM_EPS) + ADAM_WD * w), mn, vn


def _adamw(w, g, m, v, block, *, name):
    return _elementwise(_adamw_math, [w, g, m, v], 3, block, name=name)


def _interleave_layers(layers, *, tc, name):
    rows, cols = layers[0].shape
    n = len(layers)

    def body(*refs):
        for l in range(n):
            refs[n][:, l, :] = refs[l][...]

    return pl.pallas_call(body, name=name, grid=(cols // tc,),
                          in_specs=[pl.BlockSpec((rows, tc), lambda i: (0, i))] * n,
                          out_specs=pl.BlockSpec((rows, n, tc), lambda i: (0, 0, i)),
                          out_shape=jax.ShapeDtypeStruct((rows, n, cols), layers[0].dtype),
                          compiler_params=_cp("parallel"))(*layers)


def _adamw_small(ws, gs, ms, vs, *, name):
    n = len(ws)

    def body(*refs):
        w, g, m, v, outs = refs[:n], refs[n:2 * n], refs[2 * n:3 * n], refs[3 * n:4 * n], refs[4 * n:]
        for k in range(n):
            for slot, val in enumerate(_adamw_math(w[k][...], g[k][...], m[k][...], v[k][...])):
                outs[slot * n + k][...] = val

    outs = pl.pallas_call(body, name=name, out_shape=[jax.ShapeDtypeStruct(a.shape, F32) for a in ws] * 3)(
        *ws, *gs, *ms, *vs)
    return outs[:n], outs[n:2 * n], outs[2 * n:]


def _to_kernel_order(wt):
    gates = jnp.pad(wt[2048:2056], ((0, LANE - 2 * A_HEADS), (0, 0)))
    return jnp.concatenate([wt[0:2048], wt[2056:2568], wt[2824:3336], wt[2568:2696], wt[2696:2824], gates], axis=0)


def _from_kernel_order(main, tail):
    return jnp.concatenate([main[0:2048], tail[C_BG - DH_MAIN:C_BG - DH_MAIN + 2 * A_HEADS],
                            main[C_QB:C_QB + B_WIDTH], tail[0:B_KV_WIDTH], tail[B_KV_WIDTH:2 * B_KV_WIDTH],
                            main[C_ZB:C_ZB + B_WIDTH]], axis=0)


def _gate_params(a_log, dt_bias):
    return jnp.pad(jnp.stack([a_log, dt_bias]), ((0, SUBLANE - 2), (A_HEADS, LANE - 2 * A_HEADS)))


SMALL = ("conv_w", "a_log", "dt_bias", "norm_w", "sinks", "ln_g", "ln_b")


def _pack(parts, cols):
    flat = jnp.concatenate([p.reshape(-1) for p in parts])
    rows = -(-flat.shape[0] // cols)
    return jnp.pad(flat, (0, rows * cols - flat.shape[0])).reshape(rows, cols)


def _unpack(packed, shapes):
    flat = packed.reshape(-1)
    out, at = [], 0
    for s in shapes:
        n = math.prod(s)
        out.append(flat[at:at + n].reshape(s))
        at += n
    return out


def kernel(x, w_in, conv_w, a_log, dt_bias, norm_w, sinks, w_out, ln_g, ln_b, loss_target, m_w_in, m_conv_w, m_a_log, m_dt_bias, m_norm_w, m_sinks, m_w_out, m_ln_g, m_ln_b, v_w_in, v_conv_w, v_a_log, v_dt_bias, v_norm_w, v_sinks, v_w_out, v_ln_g, v_ln_b):
    xi, yi, ci = lax.axis_index("x"), lax.axis_index("y"), lax.axis_index("c")
    me = 2 * xi + yi

    to_t = lambda a: jnp.transpose(a, (2, 0, 1))
    from_t = lambda a: jnp.transpose(a, (1, 2, 0))

    wt_shard = to_t(w_in)

    def pack_weights(l):
        rows = jnp.pad(wt_shard[:, l], ((0, IN_PAD - IN_SHARD), (0, 0)))
        return jnp.concatenate([rows, w_out[l]], axis=0).astype(BF16)

    pack0, pack1 = pack_weights(0), pack_weights(1)
    got_in0, g_conv = _gather_two_level(pack0[:IN_PAD], conv_w, name="gather_weights_0")
    conv_full = jnp.moveaxis(g_conv, 0, 2).reshape(DEPTH, CONV_K, 3 * A_WIDTH)
    carriers = ("dn_pre", "dn_wy", "dn_scan")
    cuts = (0, 288, 624, IN_PAD)
    gathers = {nm: _gather_exchange([pack1[cuts[i]:cuts[i + 1]]]) for i, nm in enumerate(carriers)}
    gathers.update(in_proj=_gather_exchange([pack0[IN_PAD:]]), swa=_gather_exchange([pack1[IN_PAD:]]))
    w_in_of = lambda rows: _to_kernel_order(rows[:, :IN_SHARD].reshape(IN_COLS, D_MODEL))
    w_out_of = lambda rows: rows.reshape(D_MODEL, D_MODEL)
    args0 = _layer_args(w_in_of(got_in0), conv_full[0], a_log[0], dt_bias[0], sinks[0], norm_w[0],
                        lambda got: w_out_of(got[0]))

    def args1(got):
        rows = jnp.concatenate([got[nm][0] for nm in carriers], axis=1)
        return _layer_args(w_in_of(rows), conv_full[1], a_log[1], dt_bias[1], sinks[1], norm_w[1],
                           w_out_of(got["swa"][0]))

    def pack_grads(g):
        gin = _from_kernel_order(*g["w_in"]).reshape(N_SHARD, IN_SHARD, D_MODEL)
        gin = jnp.pad(gin, ((0, 0), (0, IN_PAD - IN_SHARD), (0, 0)))
        return jnp.concatenate([gin, g["w_out"].reshape(N_SHARD, OUT_SHARD, D_MODEL)], axis=1).astype(BF16)

    packed = {}

    def reduce1(grads1):
        packed[1] = pack_grads(grads1)
        half_rows = packed[1].shape[1] // 2
        return dict(dn_chunk=_reduce_scatter_exchange(packed[1], 0, half_rows),
                    swa=_reduce_scatter_exchange(packed[1], half_rows, half_rows))

    def reduce0(grads0, grads1, loss_tile):
        g0 = pack_grads(grads0)
        from_sibling = _run_exchange(_pair_window_exchange(g0), name="pair_reduce_0")[0]
        packed[0] = _pair_add(g0, from_sibling, ci, name="pair_add_0")
        gsmall = _pack([jnp.stack([g[nm] for g in (grads0, grads1)]) for nm in SMALL] + [loss_tile[0, 0:1]], D_MODEL)
        return _chip_scatter_exchange(packed[0], gsmall)

    _, dx, grads, landed1, (landed0, landed_small) = _local_step(
        x[0], loss_target[0], args0, args1, ln_g, ln_b, gathers=gathers, reduce1=reduce1, reduce0=reduce0)

    small_shapes = [(DEPTH,) + grads[0][nm].shape for nm in SMALL]
    halves = [_sum_chips(packed[0], landed0, me, tc=2 * LANE, name="reduce_sum_0"),
              _sum_scatter(packed[1], [landed1["dn_chunk"][0], landed1["swa"][0]], me, ci, tc=2 * LANE,
                           name="reduce_sum_1")]
    s_small = _sum_slots(landed_small, name="reduce_sum_small")
    others = _run_exchange(_share_exchange(halves), name="pair_share")
    full = [jnp.where(ci == 0, jnp.concatenate([mine, other], axis=1), jnp.concatenate([other, mine], axis=1))
            for mine, other in zip(halves, others)]
    grad_in_layers = [f[:IN_SHARD] for f in full]
    grad_out = jnp.stack([f[IN_PAD:] for f in full])
    out_blk = (1, OUT_SHARD, D_MODEL)
    *small_grads, loss = _unpack(s_small, small_shapes + [()])
    gs = dict(zip(SMALL, small_grads))
    gs["conv_w"] = lax.dynamic_slice_in_dim(gs["conv_w"], me * CONV_SHARD, CONV_SHARD, axis=2)

    grad_in_t = _interleave_layers(grad_in_layers, tc=2 * LANE, name="grad_in_layers")
    d_in, nm_in, nv_in = (from_t(o) for o in _adamw(to_t(w_in), grad_in_t, to_t(m_w_in), to_t(v_w_in),
                                                    (IN_SHARD // 6, DEPTH, D_MODEL), name="adamw_in"))
    grad_in = from_t(grad_in_t)
    d_out, nm_out, nv_out = _adamw(w_out, grad_out, m_w_out, v_w_out, out_blk, name="adamw_out")
    ws = dict(conv_w=conv_w, a_log=a_log, dt_bias=dt_bias, norm_w=norm_w, sinks=sinks, ln_g=ln_g, ln_b=ln_b)
    ms = dict(conv_w=m_conv_w, a_log=m_a_log, dt_bias=m_dt_bias, norm_w=m_norm_w, sinks=m_sinks, ln_g=m_ln_g, ln_b=m_ln_b)
    vs = dict(conv_w=v_conv_w, a_log=v_a_log, dt_bias=v_dt_bias, norm_w=v_norm_w, sinks=v_sinks, ln_g=v_ln_g, ln_b=v_ln_b)
    d_s, nm_s, nv_s = (dict(zip(SMALL, o)) for o in _adamw_small(*[[d[nm] for nm in SMALL] for d in (ws, gs, ms, vs)],
                                                                 name="adamw_small"))

    def in_order(big_in, small, big_out):
        return (big_in, small["conv_w"], small["a_log"], small["dt_bias"], small["norm_w"], small["sinks"], big_out,
                small["ln_g"], small["ln_b"])

    return (loss, dx[None], *in_order(grad_in, gs, grad_out), *in_order(d_in, d_s, d_out),
            *in_order(nm_in, nm_s, nm_out), *in_order(nv_in, nv_s, nv_out))
```

```python
import math

import jax
import jax.numpy as jnp
from jax import lax
from jax.experimental import pallas as pl
from jax.experimental.pallas import tpu as pltpu

F32 = jnp.float32
BF16 = jnp.bfloat16
HI = lax.Precision.HIGHEST

D_MODEL = 1024
DEPTH = 2
A_HEADS = 4
A_HEAD_DIM = 128
A_WIDTH = 512
CONV_K = 4
CHUNK = 64
B_Q_HEADS = 8
B_KV_HEADS = 2
B_HEAD_DIM = 64
B_GROUP = 4
B_WIDTH = 512
B_KV_WIDTH = 128
BLOCK = 128
IN_COLS = 3336
DEEPNORM_ALPHA = (2 * DEPTH) ** 0.25
LN_EPS = 1e-5
RMS_EPS = 1e-6
L2_EPS = 1e-6
ADAM_LR = 0.001
ADAM_B1 = 0.9
ADAM_B2 = 0.999
ADAM_EPS = 1e-08
ADAM_WD = 0.01
ADAM_STEP = 10

N_SHARD = 4
IN_SHARD = IN_COLS // N_SHARD
OUT_SHARD = D_MODEL // N_SHARD
CONV_SHARD = 3 * A_WIDTH // N_SHARD
IN_PAD = -(-IN_SHARD // 96) * 96

P_COLS = 3456
C_PRE = 0
C_ZA = 1536
C_QB = 2048
C_ZB = 2560
C_KB = 3072
C_VB = 3200
C_BG = 3328
DH_MAIN = C_KB
LANE = 128
SUBLANE = 8
HALO = 16
VMEM_LIMIT = 56 * 1024 * 1024
ALIBI = tuple(2.0 ** (-8.0 * (h + 1) / B_Q_HEADS) for h in range(B_Q_HEADS))
NEG = -1e30


def _cp(*sem):
    return pltpu.CompilerParams(dimension_semantics=sem, vmem_limit_bytes=VMEM_LIMIT)


def _dot(a, b):
    return jnp.dot(a.astype(BF16), b.astype(BF16), preferred_element_type=F32)


def _dot_nt(a, b):
    return lax.dot_general(a.astype(BF16), b.astype(BF16), (((1,), (1,)), ((), ())),
                           preferred_element_type=F32)


def _dot_tn(a, b):
    return lax.dot_general(a.astype(BF16), b.astype(BF16), (((0,), (0,)), ((), ())),
                           preferred_element_type=F32)


def _dot_hi(a, b):
    return jnp.dot(a, b, precision=HI, preferred_element_type=F32)


def _sigmoid(x):
    return jax.nn.sigmoid(x)


def _silu(x):
    return x * _sigmoid(x)


def _silu_and_grad(x):
    s = _sigmoid(x)
    return x * s, s * (1.0 + x * (1.0 - s))


def _softplus(x):
    return jnp.maximum(x, 0.0) + jnp.log(1.0 + jnp.exp(-jnp.abs(x)))


def _shift_down(cur, before, s):
    if s == 0:
        return cur
    r = pltpu.roll(cur, s, 0)
    rb = pltpu.roll(before, s, 0)
    row = lax.broadcasted_iota(jnp.int32, before.shape, 0)
    head = jnp.where(row < s, rb, r[0:SUBLANE])
    return jnp.concatenate([head, r[SUBLANE:]], axis=0)


def _shift_up(cur, after, s):
    if s == 0:
        return cur
    n = cur.shape[0]
    r = pltpu.roll(cur, n - s, 0)
    ra = pltpu.roll(after, SUBLANE - s, 0)
    row = lax.broadcasted_iota(jnp.int32, after.shape, 0)
    tail = jnp.where(row >= SUBLANE - s, ra, r[n - SUBLANE:])
    return jnp.concatenate([r[:n - SUBLANE], tail], axis=0)


def _conv_fwd(cur, before, w):
    acc = cur * w[CONV_K - 1:CONV_K, :]
    for s in range(1, CONV_K):
        acc = acc + _shift_down(cur, before, s) * w[CONV_K - 1 - s:CONV_K - s, :]
    return acc


def _matmul_nt(a, bt, *, tm, name, carry=None):
    m, k = a.shape
    n = bt.shape[0]
    c_ins, c_in_specs, c_out_specs, c_outs, c_scratch = _carry_specs(carry)

    def body(*refs):
        a_ref, b_ref, o_ref = _carried(carry, refs, 2, 1, m // tm)
        o_ref[...] = _dot_nt(a_ref[...], b_ref[...]).astype(o_ref.dtype)

    outs = pl.pallas_call(
        body, name=name, grid=(m // tm,),
        in_specs=[pl.BlockSpec((tm, k), lambda i: (i, 0)), pl.BlockSpec((n, k), lambda i: (0, 0))] + c_in_specs,
        out_specs=[pl.BlockSpec((tm, n), lambda i: (i, 0))] + c_out_specs,
        out_shape=[jax.ShapeDtypeStruct((m, n), BF16)] + c_outs,
        scratch_shapes=c_scratch,
        compiler_params=_cp("arbitrary"))(a, bt, *c_ins)
    return outs[0], outs[1:]


def _dn_pre(h, conv_w, par, *, tt, name, carry=None):
    t = h.shape[0]
    cw = 3 * A_WIDTH
    hb = tt // HALO

    c_ins, c_in_specs, c_out_specs, c_outs, c_scratch = _carry_specs(carry)

    def body(*refs):
        (pre_ref, halo_ref, bgi_ref, cw_ref, par_ref,
         q_ref, k_ref, v_ref, bg_ref, bgt_ref, c_ref) = _carried(carry, refs, 5, 6, t // tt)
        i = pl.program_id(0)
        cur = pre_ref[...].astype(F32)
        before = jnp.where(i > 0, halo_ref[...].astype(F32)[HALO - SUBLANE:], 0.0)
        conv = _conv_fwd(cur, before, cw_ref[...])
        c_ref[...] = conv
        s = _silu(conv)
        for hd in range(A_HEADS):
            sl = slice(hd * LANE, (hd + 1) * LANE)
            tq = s[:, hd * LANE:(hd + 1) * LANE]
            q_ref[:, sl] = tq * (lax.rsqrt(jnp.sum(tq * tq, -1, keepdims=True) + L2_EPS) * (A_HEAD_DIM ** -0.5))
            tk = s[:, A_WIDTH + hd * LANE:A_WIDTH + (hd + 1) * LANE]
            k_ref[:, sl] = tk * lax.rsqrt(jnp.sum(tk * tk, -1, keepdims=True) + L2_EPS)
        v_ref[...] = s[:, 2 * A_WIDTH:]
        raw = bgi_ref[...].astype(F32)
        lane = lax.broadcasted_iota(jnp.int32, raw.shape, 1)
        is_a = (lane >= A_HEADS) & (lane < 2 * A_HEADS)
        g = jnp.where(is_a, -jnp.exp(par_ref[0:1, :]) * _softplus(raw + par_ref[1:2, :]), 0.0)
        gc = _dot_hi(_chunk_tri(tt, lower=True), g)
        bg = jnp.where(lane < A_HEADS, _sigmoid(raw), gc)
        bg_ref[...] = bg
        bgt_ref[...] = jnp.transpose(bg)[0:SUBLANE, :]

    wide = jax.ShapeDtypeStruct((t, A_WIDTH), F32)
    outs = pl.pallas_call(
        body, name=name, grid=(t // tt,),
        in_specs=[pl.BlockSpec((tt, cw), lambda i: (i, 0)),
                  pl.BlockSpec((HALO, cw), lambda i: (jnp.maximum(i * hb - 1, 0), 0)),
                  pl.BlockSpec((tt, LANE), lambda i: (i, C_BG // LANE)),
                  pl.BlockSpec((CONV_K, cw), lambda i: (0, 0)),
                  pl.BlockSpec((SUBLANE, LANE), lambda i: (0, 0))] + c_in_specs,
        out_specs=[pl.BlockSpec((tt, A_WIDTH), lambda i: (i, 0))] * 3
        + [pl.BlockSpec((tt, LANE), lambda i: (i, 0)), pl.BlockSpec((SUBLANE, tt), lambda i: (0, i)),
           pl.BlockSpec((tt, cw), lambda i: (i, 0))] + c_out_specs,
        out_shape=[wide, wide, wide, jax.ShapeDtypeStruct((t, LANE), F32),
                   jax.ShapeDtypeStruct((SUBLANE, t), F32), jax.ShapeDtypeStruct((t, cw), F32)] + c_outs,
        scratch_shapes=c_scratch,
        compiler_params=_cp("arbitrary"))(h, h, h, conv_w, par, *c_ins)
    return outs[:6], outs[6:]


def _chunk_tri(n, lower):
    r = lax.broadcasted_iota(jnp.int32, (n, n), 0)
    c = lax.broadcasted_iota(jnp.int32, (n, n), 1)
    shift = CHUNK.bit_length() - 1
    same = jnp.right_shift(r, shift) == jnp.right_shift(c, shift)
    return (same & ((c <= r) if lower else (c >= r))).astype(F32)


def _chunk_masks():
    r = lax.broadcasted_iota(jnp.int32, (CHUNK, CHUNK), 0)
    c = lax.broadcasted_iota(jnp.int32, (CHUNK, CHUNK), 1)
    return r >= c, r > c, r == c


def _split(a):
    hi = a.astype(BF16)
    return hi, (a - hi.astype(F32)).astype(BF16)


def _dot3(a, b):
    (ah, al), (bh, bl) = a, b
    d = lambda p, q: jnp.dot(p, q, preferred_element_type=F32)
    return d(ah, bh) + (d(ah, bl) + d(al, bh))


def _tri_inv_many(a_list, eye):
    d = lambda p, q: jnp.dot(p.astype(BF16), q.astype(BF16), preferred_element_type=F32)
    r = lax.broadcasted_iota(jnp.int32, (CHUNK, CHUNK), 0)
    c = lax.broadcasted_iota(jnp.int32, (CHUNK, CHUNK), 1)
    same = lambda b: jnp.right_shift(r, b.bit_length() - 1) == jnp.right_shift(c, b.bit_length() - 1)
    x = [jnp.where(same(8), -a, 0.0) for a in a_list]
    tm = [eye + xi for xi in x]
    for _ in range(2):
        x = [d(xi, xi) for xi in x]
        tm = [t + d(t, xi) for t, xi in zip(tm, x)]
    for b in (16, 32, 64):
        low = [jnp.where(same(b) & ~same(b // 2), a, 0.0) for a in a_list]
        tm = [t - d(t, d(lo, t)) for t, lo in zip(tm, low)]
    res = [eye - _dot3(_split(eye + a), _split(t)) for a, t in zip(a_list, tm)]
    return [t + d(t, rs) for t, rs in zip(tm, res)]


def _chunk_gates(bg_v, bgt_v, hd):
    return (bg_v[:, hd:hd + 1], bg_v[:, A_HEADS + hd:A_HEADS + hd + 1],
            None if bgt_v is None else bgt_v[A_HEADS + hd:A_HEADS + hd + 1, :])


WY_ROWS = 512
SCAN_ROWS = 512
WY_GROUP = 8


def _dn_wy(q, k, v, bg, bgt, *, name, carry=None):
    t = q.shape[0]
    rows = WY_ROWS

    c_ins, c_in_specs, c_out_specs, c_outs, c_scratch = _carry_specs(carry)

    def body(*refs):
        q_ref, k_ref, v_ref, bg_ref, bgt_ref, u_ref, w_ref, tm_ref, qk_ref = _carried(carry, refs, 5, 4, t // rows)
        causal, strict, diag = _chunk_masks()
        eye = diag.astype(F32)
        for c0 in range(0, rows // CHUNK, WY_GROUP):
            items = [(c, hd) for c in range(c0, c0 + WY_GROUP) for hd in range(A_HEADS)]
            rs = lambda c: slice(c * CHUNK, (c + 1) * CHUNK)
            sl = lambda hd: slice(hd * LANE, (hd + 1) * LANE)
            hs = lambda hd: slice(hd * CHUNK, (hd + 1) * CHUNK)
            gates = [_chunk_gates(bg_ref[rs(c), :], bgt_ref[:, rs(c)], hd) for c, hd in items]
            dms = [jnp.exp(jnp.where(causal, gcol - grow, NEG)) for _, gcol, grow in gates]
            kbs = [k_ref[rs(c), sl(hd)] * g[0] for (c, hd), g in zip(items, gates)]
            a_list = [jnp.where(strict, _dot_nt(kb, k_ref[rs(c), sl(hd)]) * dm, 0.0)
                      for (c, hd), kb, dm in zip(items, kbs, dms)]
            for (c, hd), dm in zip(items, dms):
                qk_ref[rs(c), hs(hd)] = jnp.where(
                    causal, _dot_nt(q_ref[rs(c), sl(hd)], k_ref[rs(c), sl(hd)]) * dm, 0.0)
            tms = _tri_inv_many(a_list, eye)
            for (c, hd), g, kb, tmat in zip(items, gates, kbs, tms):
                tm_ref[rs(c), hs(hd)] = tmat
                u_ref[rs(c), sl(hd)] = _dot(tmat, v_ref[rs(c), sl(hd)] * g[0])
                w_ref[rs(c), sl(hd)] = _dot(tmat, kb * jnp.exp(g[1])).astype(BF16)

    blk = pl.BlockSpec((rows, A_WIDTH), lambda i: (i, 0))
    half = pl.BlockSpec((rows, A_HEADS * CHUNK), lambda i: (i, 0))
    outs = pl.pallas_call(
        body, name=name, grid=(t // rows,),
        in_specs=[blk, blk, blk, pl.BlockSpec((rows, LANE), lambda i: (i, 0)),
                  pl.BlockSpec((SUBLANE, rows), lambda i: (0, i))] + c_in_specs,
        out_specs=[blk, blk, half, half] + c_out_specs,
        out_shape=[jax.ShapeDtypeStruct((t, A_WIDTH), F32), jax.ShapeDtypeStruct((t, A_WIDTH), BF16),
                   jax.ShapeDtypeStruct((t, A_HEADS * CHUNK), F32),
                   jax.ShapeDtypeStruct((t, A_HEADS * CHUNK), F32)] + c_outs,
        scratch_shapes=c_scratch,
        compiler_params=_cp("arbitrary"))(q, k, v, bg, bgt, *c_ins)
    return outs[:4], outs[4:]


def _dn_scan_fwd(q, k, u, w, qk, bg, *, name, carry=None):
    t = q.shape[0]
    rows = SCAN_ROWS
    per = rows // CHUNK
    c_ins, c_in_specs, c_out_specs, c_outs, c_scratch = _carry_specs(carry)

    def body(*refs):
        q_ref, k_ref, u_ref, w_ref, qk_ref, bg_ref, o_ref, vn_ref, s_ref, state = _carried(carry, refs, 6, 3, t // rows)

        @pl.when(pl.program_id(0) == 0)
        def _():
            state[...] = jnp.zeros_like(state)

        heads = range(A_HEADS)
        sl = lambda hd: slice(hd * LANE, (hd + 1) * LANE)
        s_cur = [state[hd] for hd in heads]
        for c in range(per):
            rs = slice(c * CHUNK, (c + 1) * CHUNK)
            bg_v = bg_ref[rs, :]
            gcols = [_chunk_gates(bg_v, None, hd)[1] for hd in heads]
            glasts = [gc[CHUNK - 1:CHUNK, :] for gc in gcols]
            for hd in heads:
                s_ref[c, hd] = s_cur[hd].astype(BF16)
            vns = [u_ref[rs, sl(hd)] - _dot(w_ref[rs, sl(hd)], s_cur[hd]) for hd in heads]
            qss = [_dot(q_ref[rs, sl(hd)] * jnp.exp(gcols[hd]), s_cur[hd]) for hd in heads]
            s_cur = [s_cur[hd] * jnp.exp(glasts[hd])
                     + _dot_tn(k_ref[rs, sl(hd)] * jnp.exp(glasts[hd] - gcols[hd]), vns[hd]) for hd in heads]
            for hd in heads:
                vn_ref[rs, sl(hd)] = vns[hd]
                o_ref[rs, sl(hd)] = qss[hd] + _dot(qk_ref[rs, hd * CHUNK:(hd + 1) * CHUNK], vns[hd])
        for hd in heads:
            state[hd] = s_cur[hd]

    blk = pl.BlockSpec((rows, A_WIDTH), lambda i: (i, 0))
    half = pl.BlockSpec((rows, A_HEADS * CHUNK), lambda i: (i, 0))
    wide = jax.ShapeDtypeStruct((t, A_WIDTH), F32)
    outs = pl.pallas_call(
        body, name=name, grid=(t // rows,),
        in_specs=[blk, blk, blk, blk, half, pl.BlockSpec((rows, LANE), lambda i: (i, 0))] + c_in_specs,
        out_specs=[blk, blk, pl.BlockSpec((per, A_HEADS, LANE, LANE), lambda i: (i, 0, 0, 0))] + c_out_specs,
        out_shape=[wide, wide, jax.ShapeDtypeStruct((t // CHUNK, A_HEADS, LANE, LANE), BF16)] + c_outs,
        scratch_shapes=[pltpu.VMEM((A_HEADS, LANE, LANE), F32)] + c_scratch,
        compiler_params=_cp("arbitrary"))(q, k, u, w, qk, bg, *c_ins)
    return outs[:3], outs[3:]


def _stack_heads(ref, hk):
    return jnp.concatenate([ref[:, h * B_HEAD_DIM:(h + 1) * B_HEAD_DIM].astype(F32)
                            for h in range(hk * B_GROUP, (hk + 1) * B_GROUP)], axis=0)


def _swa_window():
    qi = lax.broadcasted_iota(jnp.int32, (BLOCK, BLOCK), 0)
    kj = lax.broadcasted_iota(jnp.int32, (BLOCK, BLOCK), 1)
    dist = jnp.where(kj > qi, qi + BLOCK - kj, qi - kj).astype(F32)
    rows = lax.broadcasted_iota(jnp.int32, (B_GROUP * BLOCK, BLOCK), 0)
    cols = lax.broadcasted_iota(jnp.int32, (B_GROUP * BLOCK, BLOCK), 1)
    return cols > jnp.bitwise_and(rows, BLOCK - 1), dist


def _swa_group_probs(q_ref, sk_ref, kp, kc, vp, vc, n_blk):
    hks = range(B_KV_HEADS)
    heads = lambda hk: range(hk * B_GROUP, (hk + 1) * B_GROUP)
    ksl = lambda hk: slice(hk * B_HEAD_DIM, (hk + 1) * B_HEAD_DIM)
    upper, dist = _swa_window()
    no_prev = jnp.where(n_blk > 0, 0.0, NEG)
    ones = jnp.ones((BLOCK, B_HEAD_DIM), BF16)
    with_ones = lambda v, hk: jnp.concatenate([v[:, ksl(hk)].astype(BF16), ones], axis=1)
    qs = [_stack_heads(q_ref, hk) * (B_HEAD_DIM ** -0.5) for hk in hks]
    sink = [jnp.concatenate([jnp.broadcast_to(sk_ref[h:h + 1, 0:1], (BLOCK, 1)) for h in heads(hk)], axis=0)
            for hk in hks]
    s = [jnp.where(upper, _dot_nt(qs[hk], kp[:, ksl(hk)]) + no_prev, _dot_nt(qs[hk], kc[:, ksl(hk)]))
         - jnp.concatenate([ALIBI[h] * dist for h in heads(hk)], axis=0) for hk in hks]
    m = [jnp.maximum(jnp.max(s[hk], axis=-1, keepdims=True), sink[hk]) for hk in hks]
    p = [jnp.exp(s[hk] - m[hk]) for hk in hks]
    p_up = [jnp.where(upper, p[hk], 0.0) for hk in hks]
    oe = [jnp.dot(p_up[hk].astype(BF16), with_ones(vp, hk), preferred_element_type=F32)
          + jnp.dot((p[hk] - p_up[hk]).astype(BF16), with_ones(vc, hk), preferred_element_type=F32) for hk in hks]
    ps = [jnp.exp(sink[hk] - m[hk]) for hk in hks]
    inv = [1.0 / (oe[hk][:, B_HEAD_DIM:B_HEAD_DIM + 1] + ps[hk]) for hk in hks]
    return upper, [(qs[hk], p[hk] * inv[hk], ps[hk] * inv[hk], oe[hk][:, :B_HEAD_DIM] * inv[hk]) for hk in hks]


def _swa_specs():
    qspec = lambda c0: pl.BlockSpec((BLOCK, B_WIDTH), lambda i: (i, c0 // B_WIDTH))
    cur = lambda c0: pl.BlockSpec((BLOCK, LANE), lambda i: (i, c0 // LANE))
    prev = lambda c0: pl.BlockSpec((BLOCK, LANE), lambda i: (jnp.maximum(i - 1, 0), c0 // LANE))
    return qspec, cur, prev


def _carried(carry, refs, n_in, n_out, steps):
    if carry is None:
        return refs
    ci, co = len(carry.ins), len(carry.outs)
    own = refs[:n_in] + refs[n_in + ci:n_in + ci + n_out] + refs[n_in + ci + n_out + co:len(refs) - 3]
    parts = refs[n_in:n_in + ci], refs[n_in + ci + n_out:n_in + ci + n_out + co], refs[len(refs) - 3:]

    @pl.when(pl.program_id(0) == 0)
    def _():
        carry.start(*parts)

    @pl.when(pl.program_id(0) == steps - 1)
    def _():
        carry.finish(*parts)

    return own


def _carry_specs(carry):
    if carry is None:
        return [], [], [], [], []
    return (list(carry.ins), [_ANY] * len(carry.ins), [_ANY] * len(carry.outs), list(carry.outs), carry.scratch())


def _swa_fwd(h, sinks_b, *, name, carry=None):
    t = h.shape[0]
    qspec, cur, prev = _swa_specs()
    c_ins, c_in_specs, c_out_specs, c_outs, c_scratch = _carry_specs(carry)

    def body(*refs):
        q_ref, kc_ref, kp_ref, vc_ref, vp_ref, sk_ref, o_ref, p_ref, ps_ref = _carried(carry, refs, 6, 3, t // BLOCK)
        n_blk = pl.program_id(0)
        _, groups = _swa_group_probs(q_ref, sk_ref, kp_ref[...], kc_ref[...], vp_ref[...], vc_ref[...], n_blk)
        lane = lax.broadcasted_iota(jnp.int32, (BLOCK, LANE), 1)
        sink_probs = jnp.zeros((BLOCK, LANE), F32)
        for hk, (_, p, ps, o) in enumerate(groups):
            for g in range(B_GROUP):
                hq = hk * B_GROUP + g
                rows = slice(g * BLOCK, (g + 1) * BLOCK)
                o_ref[:, hq * B_HEAD_DIM:(hq + 1) * B_HEAD_DIM] = o[rows]
                p_ref[:, hq * BLOCK:(hq + 1) * BLOCK] = p[rows].astype(BF16)
                sink_probs = sink_probs + jnp.where(lane == hq, ps[rows], 0.0)
        ps_ref[...] = sink_probs

    row = lambda w: pl.BlockSpec((BLOCK, w), lambda i: (i, 0))
    outs = pl.pallas_call(
        body, name=name, grid=(t // BLOCK,),
        in_specs=[qspec(C_QB), cur(C_KB), prev(C_KB), cur(C_VB), prev(C_VB),
                  pl.BlockSpec((B_Q_HEADS, LANE), lambda i: (0, 0))] + c_in_specs,
        out_specs=[row(B_WIDTH), row(B_Q_HEADS * BLOCK), row(LANE)] + c_out_specs,
        out_shape=[jax.ShapeDtypeStruct((t, B_WIDTH), F32), jax.ShapeDtypeStruct((t, B_Q_HEADS * BLOCK), BF16),
                   jax.ShapeDtypeStruct((t, LANE), F32)] + c_outs,
        scratch_shapes=c_scratch,
        compiler_params=_cp("arbitrary"))(h, h, h, h, h, sinks_b, *c_ins)
    return outs[:3], outs[3:]


def _rms_gate(o, za, nw):
    outs = []
    for hd in range(A_HEADS):
        oh = o[:, hd * LANE:(hd + 1) * LANE]
        r = lax.rsqrt(jnp.mean(oh * oh, -1, keepdims=True) + RMS_EPS)
        outs.append(oh * r * nw)
    return jnp.concatenate(outs, axis=1) * _silu(za)


def _out_ln(x, oa, ob, h, norm_w, w_out, ln_g, ln_b, *, tm, name, target=None):
    t = x.shape[0]
    last = target is not None

    def body(*refs):
        x_ref, oa_ref, ob_ref, za_ref, zb_ref, nw_ref, w_ref, g_ref, b_ref = refs[:9]
        xn_ref, mx_ref, r_ref = refs[9 + last:12 + last]
        ya = _rms_gate(oa_ref[...], za_ref[...].astype(F32), nw_ref[...])
        yb = ob_ref[...] * _silu(zb_ref[...].astype(F32))
        mixed = jnp.concatenate([ya, yb], axis=1).astype(BF16)
        mx_ref[...] = mixed
        r = DEEPNORM_ALPHA * x_ref[...] + jnp.dot(mixed, w_ref[...], preferred_element_type=F32)
        r_ref[...] = r
        mu = jnp.mean(r, -1, keepdims=True)
        xc = r - mu
        var = jnp.mean(xc * xc, -1, keepdims=True)
        xn = xc * lax.rsqrt(var + LN_EPS) * g_ref[...] + b_ref[...]
        if not last:
            xn_ref[...] = xn
            return
        loss_ref = refs[13]

        @pl.when(pl.program_id(0) == 0)
        def _():
            loss_ref[...] = jnp.zeros_like(loss_ref)

        err = xn - refs[9][...]
        xn_ref[...] = err * (1.0 / D_MODEL)
        loss_ref[...] += 0.5 / D_MODEL * jnp.sum(err * err)

    row = lambda w, c: pl.BlockSpec((tm, w), lambda i: (i, c))
    full = lambda a, b: pl.BlockSpec((a, b), lambda i: (0, 0))
    wide = jax.ShapeDtypeStruct((t, D_MODEL), F32)
    return pl.pallas_call(
        body, name=name, grid=(t // tm,),
        in_specs=[row(D_MODEL, 0), row(A_WIDTH, 0), row(B_WIDTH, 0), row(A_WIDTH, C_ZA // A_WIDTH),
                  row(B_WIDTH, C_ZB // B_WIDTH), full(1, LANE), full(D_MODEL, D_MODEL), full(1, D_MODEL),
                  full(1, D_MODEL)] + [row(D_MODEL, 0)] * last,
        out_specs=[row(D_MODEL, 0), row(D_MODEL, 0), row(D_MODEL, 0)] + [full(SUBLANE, LANE)] * last,
        out_shape=[wide, jax.ShapeDtypeStruct((t, D_MODEL), BF16), wide]
        + [jax.ShapeDtypeStruct((SUBLANE, LANE), F32)] * last,
        compiler_params=_cp("arbitrary" if last else "parallel"))(
        x, oa, ob, h, h, norm_w, w_out, ln_g, ln_b, *([target] if last else []))


def _layer_fwd(x, wt, conv_w, par, sinks_b, norm_w, w_out_bf, ln_g, ln_b, l, carries=None, target=None):
    carries = carries or {}
    h, got_in = _matmul_nt(x, wt, tm=512, name=f"in_proj_{l}", carry=carries.get("in_proj"))
    if callable(w_out_bf):
        w_out_bf = w_out_bf(got_in)
    (q, k, v, bg, bgt, conv), got_pre = _dn_pre(h, conv_w, par, tt=512, name=f"dn_pre_{l}",
                                                carry=carries.get("dn_pre"))
    (u, w, tmat, qk), got_wy = _dn_wy(q, k, v, bg, bgt, name=f"dn_wy_{l}", carry=carries.get("dn_wy"))
    (oa, vn, s_all), got_scan = _dn_scan_fwd(q, k, u, w, qk, bg, name=f"dn_scan_{l}", carry=carries.get("dn_scan"))
    (ob, swa_p, swa_ps), got_swa = _swa_fwd(h, sinks_b, name=f"swa_fwd_{l}", carry=carries.get("swa"))
    xn, mixed, r, *loss = _out_ln(x, oa, ob, h, norm_w, w_out_bf, ln_g, ln_b, tm=512, name=f"out_ln_{l}", target=target)
    if loss:
        xn = (xn, loss[0])
    res = dict(x=x, h=h, q=q, k=k, v=v, bg=bg, bgt=bgt, w=w, tmat=tmat, qk=qk, vn=vn, oa=oa, s_all=s_all,
               mixed=mixed, r=r, w_out=w_out_bf, ob=ob, swa_p=swa_p, swa_ps=swa_ps, conv=conv)
    return xn, res, dict(in_proj=got_in, dn_pre=got_pre, dn_wy=got_wy, dn_scan=got_scan, swa=got_swa)


def _ln_out_bwd(dxn, r, mixed, ln_g, w_out, *, tm, name):
    t = dxn.shape[0]

    def body(dxn_ref, r_ref, mx_ref, g_ref, w_ref, dr_ref, dm_ref, dw_ref, dg_ref, db_ref):
        @pl.when(pl.program_id(0) == 0)
        def _():
            dw_ref[...] = jnp.zeros_like(dw_ref)
            dg_ref[...] = jnp.zeros_like(dg_ref)
            db_ref[...] = jnp.zeros_like(db_ref)

        rr = r_ref[...]
        xc = rr - jnp.mean(rr, -1, keepdims=True)
        rstd = lax.rsqrt(jnp.mean(xc * xc, -1, keepdims=True) + LN_EPS)
        xhat = xc * rstd
        dxn_v = dxn_ref[...]
        dxh = dxn_v * g_ref[...]
        dr = rstd * (dxh - jnp.mean(dxh, -1, keepdims=True) - xhat * jnp.mean(dxh * xhat, -1, keepdims=True))
        dr_ref[...] = dr
        dg_ref[...] += jnp.sum(dxn_v * xhat, axis=0, keepdims=True)
        db_ref[...] += jnp.sum(dxn_v, axis=0, keepdims=True)
        drb = dr.astype(BF16)
        dm_ref[...] = _dot_nt(drb, w_ref[...])
        dw_ref[...] += _dot_tn(mx_ref[...], drb)

    row = pl.BlockSpec((tm, D_MODEL), lambda i: (i, 0))
    full = lambda a, b: pl.BlockSpec((a, b), lambda i: (0, 0))
    big = jax.ShapeDtypeStruct((t, D_MODEL), F32)
    vec = jax.ShapeDtypeStruct((1, D_MODEL), F32)
    return pl.pallas_call(
        body, name=name, grid=(t // tm,),
        in_specs=[row, row, row, full(1, D_MODEL), full(D_MODEL, D_MODEL)],
        out_specs=[row, row, full(D_MODEL, D_MODEL), full(1, D_MODEL), full(1, D_MODEL)],
        out_shape=[big, big, jax.ShapeDtypeStruct((D_MODEL, D_MODEL), F32), vec, vec],
        compiler_params=_cp("arbitrary"))(dxn, r, mixed, ln_g, w_out)


def _dn_post_bwd(dm, oa, h, norm_w, *, tm, name):
    t = oa.shape[0]

    def body(dy_ref, o_ref, za_ref, nw_ref, do_ref, dza_ref, dnw_ref):
        @pl.when(pl.program_id(0) == 0)
        def _():
            dnw_ref[...] = jnp.zeros_like(dnw_ref)

        nw = nw_ref[...]
        dnw = jnp.zeros_like(nw)
        for hd in range(A_HEADS):
            sl = slice(hd * LANE, (hd + 1) * LANE)
            oh, za, dy = o_ref[:, sl], za_ref[:, sl].astype(F32), dy_ref[:, sl]
            rs = lax.rsqrt(jnp.mean(oh * oh, -1, keepdims=True) + RMS_EPS)
            nrm = oh * rs
            gate, dgate = _silu_and_grad(za)
            dza_ref[:, sl] = (dy * nrm * nw * dgate).astype(dza_ref.dtype)
            dn = dy * gate
            dnw = dnw + jnp.sum(dn * nrm, axis=0, keepdims=True)
            dnn = dn * nw
            do_ref[:, sl] = rs * dnn - oh * (rs * rs * rs) * jnp.mean(dnn * oh, -1, keepdims=True)
        dnw_ref[...] += dnw

    row = lambda c: pl.BlockSpec((tm, A_WIDTH), lambda i: (i, c))
    wide = jax.ShapeDtypeStruct((t, A_WIDTH), F32)
    return pl.pallas_call(
        body, name=name, grid=(t // tm,),
        in_specs=[row(0), row(0), row(C_ZA // A_WIDTH), pl.BlockSpec((1, LANE), lambda i: (0, 0))],
        out_specs=[row(0), row(C_ZA // A_WIDTH), pl.BlockSpec((1, LANE), lambda i: (0, 0))],
        out_shape=[wide, jax.ShapeDtypeStruct((t, DH_MAIN), BF16), jax.ShapeDtypeStruct((1, LANE), F32)],
        compiler_params=_cp("arbitrary"))(dm, oa, h, norm_w)


def _dn_scan_bwd(q, k, w, qk, bg, do, *, name):
    t = q.shape[0]
    rows = SCAN_ROWS
    per = rows // CHUNK
    n = t // rows

    def body(q_ref, k_ref, w_ref, qk_ref, bg_ref, do_ref, dvn_ref, ds_ref, dstate):
        @pl.when(pl.program_id(0) == 0)
        def _():
            dstate[...] = jnp.zeros_like(dstate)

        heads = range(A_HEADS)
        sl = lambda hd: slice(hd * LANE, (hd + 1) * LANE)
        ds_cur = [dstate[hd] for hd in heads]
        for c in reversed(range(per)):
            rs = slice(c * CHUNK, (c + 1) * CHUNK)
            bg_v = bg_ref[rs, :]
            gcols = [_chunk_gates(bg_v, None, hd)[1] for hd in heads]
            glasts = [gc[CHUNK - 1:CHUNK, :] for gc in gcols]
            for hd in heads:
                ds_ref[c, hd] = ds_cur[hd].astype(BF16)
            pdo = [_dot_tn(qk_ref[rs, hd * CHUNK:(hd + 1) * CHUNK], do_ref[rs, sl(hd)]) for hd in heads]
            qdo = [_dot_tn(q_ref[rs, sl(hd)] * jnp.exp(gcols[hd]), do_ref[rs, sl(hd)]) for hd in heads]
            dvns = [pdo[hd] + _dot(k_ref[rs, sl(hd)] * jnp.exp(glasts[hd] - gcols[hd]), ds_cur[hd]) for hd in heads]
            ds_cur = [qdo[hd] + jnp.exp(glasts[hd]) * ds_cur[hd] - _dot_tn(w_ref[rs, sl(hd)], dvns[hd])
                      for hd in heads]
            for hd in heads:
                dvn_ref[rs, sl(hd)] = dvns[hd]
        for hd in heads:
            dstate[hd] = ds_cur[hd]

    blk = pl.BlockSpec((rows, A_WIDTH), lambda i: (n - 1 - i, 0))
    return pl.pallas_call(
        body, name=name, grid=(n,),
        in_specs=[blk, blk, blk, pl.BlockSpec((rows, A_HEADS * CHUNK), lambda i: (n - 1 - i, 0)),
                  pl.BlockSpec((rows, LANE), lambda i: (n - 1 - i, 0)), blk],
        out_specs=[blk, pl.BlockSpec((per, A_HEADS, LANE, LANE), lambda i: (n - 1 - i, 0, 0, 0))],
        out_shape=[jax.ShapeDtypeStruct((t, A_WIDTH), F32),
                   jax.ShapeDtypeStruct((t // CHUNK, A_HEADS, LANE, LANE), BF16)],
        scratch_shapes=[pltpu.VMEM((A_HEADS, LANE, LANE), F32)],
        compiler_params=_cp("arbitrary"))(q, k, w, qk, bg, do)


def _dn_chunk_bwd(q, k, v, vn, tmat, qk, bg, bgt, s_all, ds_all, dvn, do, *, name, carry=None):
    t = q.shape[0]
    rows = WY_ROWS
    per = rows // CHUNK

    c_ins, c_in_specs, c_out_specs, c_outs, c_scratch = _carry_specs(carry)

    def body(*refs):
        (q_ref, k_ref, v_ref, vn_ref, tm_ref, qk_ref, bg_ref, bgt_ref, s_ref, ds_ref, dvn_ref, do_ref,
         dq_ref, dk_ref, dv_ref, dbg_ref, dbgt_ref) = _carried(carry, refs, 12, 5, t // rows)
        causal, strict, _ = _chunk_masks()
        lane = lax.broadcasted_iota(jnp.int32, (CHUNK, LANE), 1)
        rowi = lax.broadcasted_iota(jnp.int32, (CHUNK, 1), 0)
        sub = lax.broadcasted_iota(jnp.int32, (SUBLANE, CHUNK), 0)
        rs = lambda c: slice(c * CHUNK, (c + 1) * CHUNK)
        sl = lambda hd: slice(hd * LANE, (hd + 1) * LANE)
        hs = lambda hd: slice(hd * CHUNK, (hd + 1) * CHUNK)
        for c0 in range(0, per, WY_GROUP):
            items = [(c, hd) for c in range(c0, c0 + WY_GROUP) for hd in range(A_HEADS)]
            at = lambda ref: [ref[rs(c), sl(hd)] for c, hd in items]
            qs, ks, vs, dos, vns, dvns = at(q_ref), at(k_ref), at(v_ref), at(do_ref), at(vn_ref), at(dvn_ref)
            tmhs = [tm_ref[rs(c), hs(hd)] for c, hd in items]
            ps = [qk_ref[rs(c), hs(hd)] for c, hd in items]
            gates = [_chunk_gates(bg_ref[rs(c), :], bgt_ref[:, rs(c)], hd) for c, hd in items]
            betas = [g[0] for g in gates]
            gcols = [g[1] for g in gates]
            dmats = [jnp.exp(jnp.where(causal, g[1] - g[2], NEG)) for g in gates]
            es = [jnp.exp(gc) for gc in gcols]
            glasts = [gc[CHUNK - 1:CHUNK, :] for gc in gcols]
            eks = [jnp.exp(gl - gc) for gl, gc in zip(glasts, gcols)]
            kbs = [kh * b for kh, b in zip(ks, betas)]
            vbs = [vh * b for vh, b in zip(vs, betas)]
            kbes = [kb * e for kb, e in zip(kbs, es)]

            a_s = [jnp.where(strict, _dot_nt(kb, kh) * dm, 0.0) for kb, kh, dm in zip(kbs, ks, dmats)]
            dps = [jnp.where(causal, _dot_nt(doh, vnh), 0.0) for doh, vnh in zip(dos, vns)]
            rows2 = lambda a, b: jnp.concatenate([a, b], axis=0)
            cols2 = lambda a, b: jnp.concatenate([a, b], axis=1)
            by_s = [_dot_nt(rows2(doh, dvnh), s_ref[c, hd]) for doh, dvnh, (c, hd) in zip(dos, dvns, items)]
            dqds = [m[:CHUNK] for m in by_s]
            dws = [-m[CHUNK:] for m in by_s]
            dkds = [_dot_nt(vnh, ds_ref[c, hd]) for vnh, (c, hd) in zip(vns, items)]
            dgts = [jnp.sum(s_ref[c, hd].astype(F32) * ds_ref[c, hd].astype(F32), keepdims=True) for c, hd in items]
            pairs = [cols2(dvnh, dw) for dvnh, dw in zip(dvns, dws)]
            by_t = [_dot_tn(tmh, pr) for tmh, pr in zip(tmhs, pairs)]
            dvbs = [m[:, :LANE] for m in by_t]
            dkbes = [m[:, LANE:] for m in by_t]
            dts = [_dot_nt(pr, cols2(vb, kbe)) for pr, vb, kbe in zip(pairs, vbs, kbes)]
            xs = [_dot_nt(dt, tmh) for dt, tmh in zip(dts, tmhs)]
            das = [jnp.where(strict, -_dot_tn(tmh, x), 0.0) for tmh, x in zip(tmhs, xs)]
            dmas = [da * dm for da, dm in zip(das, dmats)]
            dmps = [dp * dm for dp, dm in zip(dps, dmats)]
            stacked = [rows2(dma, dmp) for dma, dmp in zip(dmas, dmps)]
            by_k = [_dot(st, kh) for st, kh in zip(stacked, ks)]
            dkbs = [m[:CHUNK] + dkbe * e for m, dkbe, e in zip(by_k, dkbes, es)]
            for i, (c, hd) in enumerate(items):
                dq_ref[rs(c), sl(hd)] = by_k[i][CHUNK:] + dqds[i] * es[i]
                dk_ref[rs(c), sl(hd)] = (_dot_tn(stacked[i], rows2(kbs[i], qs[i])) + dkds[i] * eks[i]
                                         + dkbs[i] * betas[i])
                dv_ref[rs(c), sl(hd)] = dvbs[i] * betas[i]
            for c in range(c0, c0 + WY_GROUP):
                acc = jnp.zeros((CHUNK, LANE), F32)
                acc_t = jnp.zeros((SUBLANE, CHUNK), F32)
                for i, (ci, hd) in enumerate(items):
                    if ci != c:
                        continue
                    gmat = das[i] * a_s[i] + dps[i] * ps[i]
                    rk = jnp.sum(dkds[i] * ks[i], -1, keepdims=True) * eks[i]
                    de = jnp.sum(dqds[i] * qs[i] + dkbes[i] * kbs[i], -1, keepdims=True)
                    dglast = jnp.sum(rk, keepdims=True) + dgts[i] * jnp.exp(glasts[i])
                    dgc = (jnp.sum(gmat, -1, keepdims=True) + de * es[i] - rk
                           + jnp.where(rowi == CHUNK - 1, dglast, 0.0))
                    dbeta = jnp.sum(dkbs[i] * ks[i] + dvbs[i] * vs[i], -1, keepdims=True)
                    acc = acc + jnp.where(lane == hd, dbeta, 0.0) + jnp.where(lane == A_HEADS + hd, dgc, 0.0)
                    acc_t = acc_t + jnp.where(sub == A_HEADS + hd, -jnp.sum(gmat, axis=0, keepdims=True), 0.0)
                dbg_ref[rs(c), :] = acc
                dbgt_ref[:, rs(c)] = acc_t

    blk = pl.BlockSpec((rows, A_WIDTH), lambda i: (i, 0))
    half = pl.BlockSpec((rows, A_HEADS * CHUNK), lambda i: (i, 0))
    col = pl.BlockSpec((rows, LANE), lambda i: (i, 0))
    rowf = pl.BlockSpec((SUBLANE, rows), lambda i: (0, i))
    st = pl.BlockSpec((per, A_HEADS, LANE, LANE), lambda i: (i, 0, 0, 0))
    wide = jax.ShapeDtypeStruct((t, A_WIDTH), F32)
    outs = pl.pallas_call(
        body, name=name, grid=(t // rows,),
        in_specs=[blk, blk, blk, blk, half, half, col, rowf, st, st, blk, blk] + c_in_specs,
        out_specs=[blk, blk, blk, col, rowf] + c_out_specs,
        out_shape=[wide, wide, wide, jax.ShapeDtypeStruct((t, LANE), F32),
                   jax.ShapeDtypeStruct((SUBLANE, t), F32)] + c_outs,
        scratch_shapes=c_scratch,
        compiler_params=_cp("arbitrary"))(q, k, v, vn, tmat, qk, bg, bgt, s_all, ds_all, dvn, do, *c_ins)
    return outs[:5], outs[5:]


def _dn_pre_bwd(h, conv, par, dq, dk, dv, dbg, dbgt, *, tt, name):
    t = h.shape[0]
    cw = 3 * A_WIDTH

    def body(conv_ref, bgi_ref, par_ref, dq_ref, dk_ref, dv_ref, dbg_ref, dbgt_ref, dc_ref, dbgi_ref, dpar_ref):
        i = pl.program_id(0)

        @pl.when(i == 0)
        def _():
            dpar_ref[...] = jnp.zeros_like(dpar_ref)

        s, ds = _silu_and_grad(conv_ref[...])
        for hd in range(A_HEADS):
            sl = slice(hd * LANE, (hd + 1) * LANE)
            for base, d_ref, scale in ((0, dq_ref, A_HEAD_DIM ** -0.5), (A_WIDTH, dk_ref, 1.0)):
                csl = slice(base + hd * LANE, base + (hd + 1) * LANE)
                tq = s[:, base + hd * LANE:base + (hd + 1) * LANE]
                dy = d_ref[:, sl]
                rq = lax.rsqrt(jnp.sum(tq * tq, -1, keepdims=True) + L2_EPS)
                dtq = scale * (rq * dy - tq * (rq * rq * rq) * jnp.sum(dy * tq, -1, keepdims=True))
                dc_ref[:, csl] = dtq * ds[:, base + hd * LANE:base + (hd + 1) * LANE]
        dc_ref[:, 2 * A_WIDTH:] = dv_ref[...] * ds[:, 2 * A_WIDTH:]
        raw = bgi_ref[...].astype(F32)
        lane = lax.broadcasted_iota(jnp.int32, raw.shape, 1)
        is_b = lane < A_HEADS
        is_a = (lane >= A_HEADS) & (lane < 2 * A_HEADS)
        rows_t = jnp.concatenate([dbgt_ref[...], jnp.zeros((LANE - SUBLANE, tt), F32)], axis=0)
        dbg_v = dbg_ref[...] + jnp.where(is_a, jnp.transpose(rows_t), 0.0)
        dbg_v = jnp.where(is_a, _dot_hi(_chunk_tri(tt, lower=False), jnp.where(is_a, dbg_v, 0.0)), dbg_v)
        beta = _sigmoid(raw)
        z = raw + par_ref[1:2, :]
        neg_ea = -jnp.exp(par_ref[0:1, :])
        g = neg_ea * _softplus(z)
        da = dbg_v * neg_ea * _sigmoid(z)
        dbgi_ref[...] = jnp.where(is_b, dbg_v * beta * (1.0 - beta), jnp.where(is_a, da, 0.0))
        dpar_ref[0:1, :] += jnp.sum(jnp.where(is_a, dbg_v * g, 0.0), axis=0, keepdims=True)
        dpar_ref[1:2, :] += jnp.sum(jnp.where(is_a, da, 0.0), axis=0, keepdims=True)

    wide = pl.BlockSpec((tt, A_WIDTH), lambda i: (i, 0))
    return pl.pallas_call(
        body, name=name, grid=(t // tt,),
        in_specs=[pl.BlockSpec((tt, cw), lambda i: (i, 0)),
                  pl.BlockSpec((tt, LANE), lambda i: (i, C_BG // LANE)),
                  pl.BlockSpec((SUBLANE, LANE), lambda i: (0, 0)),
                  wide, wide, wide, pl.BlockSpec((tt, LANE), lambda i: (i, 0)),
                  pl.BlockSpec((SUBLANE, tt), lambda i: (0, i))],
        out_specs=[pl.BlockSpec((tt, cw), lambda i: (i, 0)), pl.BlockSpec((tt, LANE), lambda i: (i, 0)),
                   pl.BlockSpec((SUBLANE, LANE), lambda i: (0, 0))],
        out_shape=[jax.ShapeDtypeStruct((t, cw), F32), jax.ShapeDtypeStruct((t, LANE), F32),
                   jax.ShapeDtypeStruct((SUBLANE, LANE), F32)],
        compiler_params=_cp("arbitrary"))(conv, h, par, dq, dk, dv, dbg, dbgt)


def _conv_bwd(dc, h, conv_w, dh, *, tt, name):
    t = dc.shape[0]
    cw = 3 * A_WIDTH
    nb = t // tt

    def body(dc_ref, after_ref, pre_ref, cw_ref, dh_in_ref, dpre_ref, dcw_ref):
        i = pl.program_id(0)

        @pl.when(i == 0)
        def _():
            dcw_ref[...] = jnp.zeros_like(dcw_ref)

        dcv = dc_ref[...]
        after = jnp.where(i < nb - 1, after_ref[...], 0.0)
        cur = pre_ref[...].astype(F32)
        w = cw_ref[...]
        acc = dcv * w[CONV_K - 1:CONV_K, :]
        dcw_ref[CONV_K - 1:CONV_K, :] += jnp.sum(dcv * cur, axis=0, keepdims=True)
        for s in range(1, CONV_K):
            j = CONV_K - 1 - s
            up = _shift_up(dcv, after, s)
            acc = acc + up * w[j:j + 1, :]
            dcw_ref[j:j + 1, :] += jnp.sum(up * cur, axis=0, keepdims=True)
        dpre_ref[...] = acc.astype(dpre_ref.dtype)

    return pl.pallas_call(
        body, name=name, grid=(nb,),
        in_specs=[pl.BlockSpec((tt, cw), lambda i: (i, 0)),
                  pl.BlockSpec((SUBLANE, cw), lambda i: (jnp.minimum((i + 1) * (tt // SUBLANE), t // SUBLANE - 1), 0)),
                  pl.BlockSpec((tt, cw), lambda i: (i, 0)),
                  pl.BlockSpec((CONV_K, cw), lambda i: (0, 0)), _ANY],
        out_specs=[pl.BlockSpec((tt, cw), lambda i: (i, 0)), pl.BlockSpec((SUBLANE, cw), lambda i: (0, 0))],
        out_shape=[jax.ShapeDtypeStruct(dh.shape, dh.dtype), jax.ShapeDtypeStruct((SUBLANE, cw), F32)],
        input_output_aliases={4: 0},
        compiler_params=_cp("arbitrary"))(dc, dc, h, conv_w, dh)


def _swa_bwd(h, dm, ob, probs, sink_probs, dh, *, name, carry=None):
    t = h.shape[0]
    qspec, cur, prev = _swa_specs()
    c_ins, c_in_specs, c_out_specs, c_outs, c_scratch = _carry_specs(carry)

    def body(*refs):
        (q_ref, kc_ref, kp_ref, vc_ref, vp_ref, zb_ref, dy_ref, ob_ref, p_ref, ps_ref, dh_in_ref,
         dqz_ref, dk_ref, dv_ref, dsk_ref) = _carried(carry, refs, 11, 4, t // BLOCK)
        n_blk = pl.program_id(0)

        @pl.when(n_blk == 0)
        def _():
            dk_ref[...] = jnp.zeros_like(dk_ref)
            dv_ref[...] = jnp.zeros_like(dv_ref)
            dsk_ref[...] = jnp.zeros_like(dsk_ref)

        kp, kc, vp, vc = kp_ref[...], kc_ref[...], vp_ref[...], vc_ref[...]
        scale = B_HEAD_DIM ** -0.5
        hks = range(B_KV_HEADS)
        ksl = lambda hk: slice(hk * B_HEAD_DIM, (hk + 1) * B_HEAD_DIM)
        heads = lambda hk: range(hk * B_GROUP, (hk + 1) * B_GROUP)
        upper, _ = _swa_window()
        groups = [(_stack_heads(q_ref, hk) * scale,
                   jnp.concatenate([p_ref[:, h * BLOCK:(h + 1) * BLOCK].astype(F32) for h in heads(hk)], axis=0),
                   jnp.concatenate([ps_ref[:, h:h + 1] for h in heads(hk)], axis=0),
                   _stack_heads(ob_ref, hk)) for hk in hks]
        zbs = [_stack_heads(zb_ref, hk) for hk in hks]
        dys = [_stack_heads(dy_ref, hk) for hk in hks]
        gates = [_silu_and_grad(zbs[hk]) for hk in hks]
        dos = [dys[hk] * gates[hk][0] for hk in hks]
        deltas = [jnp.sum(dos[hk] * groups[hk][3], -1, keepdims=True) for hk in hks]
        dps = [jnp.where(upper, _dot_nt(dos[hk], vp[:, ksl(hk)]), _dot_nt(dos[hk], vc[:, ksl(hk)])) for hk in hks]
        dss = [groups[hk][1] * (dps[hk] - deltas[hk]) for hk in hks]
        ds_up = [jnp.where(upper, dss[hk], 0.0) for hk in hks]
        ds_lo = [dss[hk] - ds_up[hk] for hk in hks]
        p_up = [jnp.where(upper, groups[hk][1], 0.0) for hk in hks]
        p_lo = [groups[hk][1] - p_up[hk] for hk in hks]
        dqs = [(_dot(ds_up[hk], kp[:, ksl(hk)]) + _dot(ds_lo[hk], kc[:, ksl(hk)])) * scale for hk in hks]
        dk_prev = [_dot_tn(ds_up[hk], groups[hk][0]) for hk in hks]
        dk_cur = [_dot_tn(ds_lo[hk], groups[hk][0]) for hk in hks]
        dv_prev = [_dot_tn(p_up[hk], dos[hk]) for hk in hks]
        dv_cur = [_dot_tn(p_lo[hk], dos[hk]) for hk in hks]
        for hk in hks:
            dzb = dys[hk] * groups[hk][3] * gates[hk][1]
            dsink = groups[hk][2] * deltas[hk]
            for g in range(B_GROUP):
                hq = hk * B_GROUP + g
                rows = slice(g * BLOCK, (g + 1) * BLOCK)
                qsl = slice(hq * B_HEAD_DIM, (hq + 1) * B_HEAD_DIM)
                dqz_ref[:, qsl] = dqs[hk][rows].astype(dqz_ref.dtype)
                dqz_ref[:, B_WIDTH + hq * B_HEAD_DIM:B_WIDTH + (hq + 1) * B_HEAD_DIM] = dzb[rows].astype(dqz_ref.dtype)
                dsk_ref[hq:hq + 1, :] += -jnp.sum(dsink[rows], keepdims=True)
        at_cur = pl.ds(pl.multiple_of(n_blk * BLOCK, BLOCK), BLOCK)
        at_prev = pl.ds(pl.multiple_of(jnp.maximum(n_blk - 1, 0) * BLOCK, BLOCK), BLOCK)
        dk_ref[at_prev, :] += jnp.concatenate(dk_prev, axis=1)
        dv_ref[at_prev, :] += jnp.concatenate(dv_prev, axis=1)
        dk_ref[at_cur, :] += jnp.concatenate(dk_cur, axis=1)
        dv_ref[at_cur, :] += jnp.concatenate(dv_cur, axis=1)

    narrow = jax.ShapeDtypeStruct((t, B_KV_WIDTH), F32)
    res = lambda a, b: pl.BlockSpec((a, b), lambda i: (0, 0))
    row = lambda w: pl.BlockSpec((BLOCK, w), lambda i: (i, 0))
    outs = pl.pallas_call(
        body, name=name, grid=(t // BLOCK,),
        in_specs=[qspec(C_QB), cur(C_KB), prev(C_KB), cur(C_VB), prev(C_VB), qspec(C_ZB),
                  pl.BlockSpec((BLOCK, B_WIDTH), lambda i: (i, 1)), row(B_WIDTH), row(B_Q_HEADS * BLOCK), row(LANE),
                  _ANY] + c_in_specs,
        out_specs=[pl.BlockSpec((BLOCK, 2 * B_WIDTH), lambda i: (i, C_QB // (2 * B_WIDTH))),
                   res(t, B_KV_WIDTH), res(t, B_KV_WIDTH), res(B_Q_HEADS, LANE)] + c_out_specs,
        out_shape=[jax.ShapeDtypeStruct(dh.shape, dh.dtype), narrow, narrow,
                   jax.ShapeDtypeStruct((B_Q_HEADS, LANE), F32)] + c_outs,
        scratch_shapes=c_scratch,
        input_output_aliases={10: 0},
        compiler_params=_cp("arbitrary"))(h, h, h, h, h, h, dm, ob, probs, sink_probs, dh, *c_ins)
    return outs[:4], outs[4:]


def _in_proj_dw(dh_main, dh_tail, x, *, tk, name):
    t, n = x.shape

    def body(a_ref, t_ref, x_ref, o_ref, ot_ref):
        @pl.when(pl.program_id(0) == 0)
        def _():
            o_ref[...] = jnp.zeros_like(o_ref)
            ot_ref[...] = jnp.zeros_like(ot_ref)

        xb = x_ref[...].astype(BF16)
        o_ref[...] += _dot_tn(a_ref[...], xb)
        ot_ref[...] += _dot_tn(t_ref[...], xb)

    row = lambda a: pl.BlockSpec((tk, a.shape[1]), lambda kk: (kk, 0))
    acc = lambda a: pl.BlockSpec((a.shape[1], n), lambda kk: (0, 0))
    return pl.pallas_call(
        body, name=name, grid=(t // tk,), in_specs=[row(dh_main), row(dh_tail), row(x)],
        out_specs=[acc(dh_main), acc(dh_tail)],
        out_shape=[jax.ShapeDtypeStruct((a.shape[1], n), F32) for a in (dh_main, dh_tail)],
        compiler_params=_cp("arbitrary"))(dh_main, dh_tail, x)


def _in_proj_dx(dh_main, dh_tail, wt, dr, *, tm, name, carry=None):
    t, n_main = dh_main.shape
    n_tail = dh_tail.shape[1]
    c_ins, c_in_specs, c_out_specs, c_outs, c_scratch = _carry_specs(carry)

    def body(*refs):
        a_ref, t_ref, wa_ref, wt_ref, r_ref, o_ref = _carried(carry, refs, 5, 1, t // tm)
        o_ref[...] = _dot(a_ref[...], wa_ref[...]) + _dot(t_ref[...], wt_ref[...]) + DEEPNORM_ALPHA * r_ref[...]

    row = lambda w: pl.BlockSpec((tm, w), lambda i: (i, 0))
    outs = pl.pallas_call(
        body, name=name, grid=(t // tm,),
        in_specs=[row(n_main), row(n_tail), pl.BlockSpec((n_main, D_MODEL), lambda i: (0, 0)),
                  pl.BlockSpec((n_tail, D_MODEL), lambda i: (n_main // n_tail, 0)), row(D_MODEL)] + c_in_specs,
        out_specs=[row(D_MODEL)] + c_out_specs,
        out_shape=[jax.ShapeDtypeStruct((t, D_MODEL), F32)] + c_outs,
        scratch_shapes=c_scratch,
        compiler_params=_cp("arbitrary"))(dh_main, dh_tail, wt, wt, dr, *c_ins)
    return outs[0], outs[1:]


def _layer_bwd(dxn, res, wt, conv_w, par, sinks_b, norm_w, w_out_bf, ln_g, l, carries=None, carry_dx=None):
    carries = carries or {}
    w_out_bf = res["w_out"]
    dr, dm, dw_out, dln_g, dln_b = _ln_out_bwd(dxn, res["r"], res["mixed"], ln_g, w_out_bf, tm=512, name=f"ln_out_bwd_{l}")
    h = res["h"]
    do, dh, dnw = _dn_post_bwd(dm, res["oa"], h, norm_w, tm=512, name=f"dn_post_bwd_{l}")
    dvn, ds_all = _dn_scan_bwd(res["q"], res["k"], res["w"], res["qk"], res["bg"], do, name=f"dn_scan_bwd_{l}")
    (dq, dk, dv, dbg, dbgt), got_chunk = _dn_chunk_bwd(
        res["q"], res["k"], res["v"], res["vn"], res["tmat"], res["qk"], res["bg"], res["bgt"], res["s_all"], ds_all,
        dvn, do, name=f"dn_chunk_bwd_{l}", carry=carries.get("dn_chunk"))
    dc, dbgi, dpar = _dn_pre_bwd(h, res["conv"], par, dq, dk, dv, dbg, dbgt, tt=512, name=f"dn_pre_bwd_{l}")
    dh, dcw = _conv_bwd(dc, h, conv_w, dh, tt=512, name=f"conv_bwd_{l}")
    (dh, dkb, dvb, dsk), got_swa = _swa_bwd(h, dm, res["ob"], res["swa_p"], res["swa_ps"], dh, name=f"swa_bwd_{l}",
                                            carry=carries.get("swa"))
    carried = dict(dn_chunk=got_chunk, swa=got_swa)
    dh_tail = jnp.concatenate([dkb, dvb, dbgi], axis=1).astype(BF16)
    dwt_main, dwt_tail = _in_proj_dw(dh, dh_tail, res["x"], tk=512, name=f"in_proj_dw_{l}")
    grads = dict(w_in=(dwt_main, dwt_tail), conv_w=dcw[:CONV_K], a_log=dpar[0, A_HEADS:2 * A_HEADS],
                 dt_bias=dpar[1, A_HEADS:2 * A_HEADS], norm_w=dnw[0], sinks=dsk[:, 0], w_out=dw_out,
                 ln_g=dln_g[0], ln_b=dln_b[0])
    dx, carried_dx = _in_proj_dx(dh, dh_tail, wt, dr, tm=512, name=f"in_proj_dx_{l}",
                                 carry=None if carry_dx is None else carry_dx(grads))
    return dx, grads, carried, carried_dx


def _layer_args(wt, conv_w, a_log, dt_bias, sinks, norm_w, w_out_bf):
    return (wt, conv_w, _gate_params(a_log, dt_bias), jnp.broadcast_to(sinks[:, None], (B_Q_HEADS, LANE)),
            norm_w[None], w_out_bf)


def _local_step(x, target, args0, args1, ln_g, ln_b, gathers=None, reduce1=None, reduce0=None):
    assert DEPTH == 2
    x1, res0, got = _layer_fwd(x, *args0, ln_g[0][None], ln_b[0][None], 0, carries=gathers)
    if gathers is not None:
        args1 = args1(got)
    (dx, loss_tile), res1, _ = _layer_fwd(x1, *args1, ln_g[1][None], ln_b[1][None], 1, target=target)
    dx, grads1, _, _ = _layer_bwd(dx, res1, *args1, ln_g[1][None], 1)
    carries = None if reduce1 is None else reduce1(grads1)
    carry_dx = None if reduce0 is None else (lambda grads0: reduce0(grads0, grads1, loss_tile))
    dx, grads0, landed1, landed0 = _layer_bwd(dx, res0, *args0, ln_g[0][None], 0, carries=carries, carry_dx=carry_dx)
    return loss_tile, dx, [grads0, grads1], landed1, landed0


_ANY = pl.BlockSpec(memory_space=pl.ANY)
_MESH = pl.DeviceIdType.MESH


HALF = D_MODEL // 2


class _Exchange:
    def __init__(self, ins, outs, n_remote, n_local, plan):
        self.ins, self.outs, self.n_remote, self.n_local, self.plan = tuple(ins), tuple(outs), n_remote, n_local, plan

    def scratch(self):
        return [pltpu.SemaphoreType.DMA((self.n_remote,)), pltpu.SemaphoreType.DMA((self.n_remote,)),
                pltpu.SemaphoreType.DMA((max(self.n_local, 1),))]

    def _copies(self, in_refs, out_refs, sems, arriving):
        send_sems, recv_sems, local_sems = sems
        local, sends, recvs = self.plan(in_refs, out_refs)
        loc = [pltpu.make_async_copy(s, d, local_sems.at[i]) for i, (s, d) in enumerate(local)]
        rem = [pltpu.make_async_remote_copy(src_ref=s, dst_ref=recvs[i] if arriving else d, send_sem=send_sems.at[i],
                                            recv_sem=recv_sems.at[i], device_id=peer, device_id_type=_MESH)
               for i, (s, d, peer) in enumerate(sends)]
        return loc, rem

    def start(self, in_refs, out_refs, sems):
        loc, rem = self._copies(in_refs, out_refs, sems, arriving=False)
        for cp in loc + rem:
            cp.start()

    def finish(self, in_refs, out_refs, sems):
        loc, rem = self._copies(in_refs, out_refs, sems, arriving=True)
        for cp in rem:
            cp.wait_recv()
        for cp in rem:
            cp.wait_send()
        for cp in loc:
            cp.wait()


def _run_exchange(ex, *, name):
    n_in, n_out = len(ex.ins), len(ex.outs)

    def body(*refs):
        parts = refs[:n_in], refs[n_in:n_in + n_out], refs[n_in + n_out:]
        ex.start(*parts)
        ex.finish(*parts)

    return pl.pallas_call(body, name=name, in_specs=[_ANY] * n_in, out_specs=[_ANY] * n_out, out_shape=list(ex.outs),
                          scratch_shapes=ex.scratch())(*ex.ins)


def _place():
    x, y, c = lax.axis_index("x"), lax.axis_index("y"), lax.axis_index("c")
    return x, y, c, [(1 - x, y), (x, 1 - y), (1 - x, 1 - y)]


def _gather_exchange(arrays):
    n = len(arrays)

    def plan(src, dst):
        x, y, c, chips = _place()
        me = 2 * x + y
        local = [(src[k], dst[k].at[me]) for k in range(n)]
        sends = [(src[k], dst[k].at[me], (px, py, c)) for k in range(n) for px, py in chips]
        recvs = [dst[k].at[2 * px + py] for k in range(n) for px, py in chips]
        return local, sends, recvs

    return _Exchange(arrays, [jax.ShapeDtypeStruct((N_SHARD,) + a.shape, a.dtype) for a in arrays], 3 * n, n, plan)


def _gather_two_level(pack, conv_w, *, name):
    rows = pack.shape[0]
    part_rows = rows // 2

    def body(pack_ref, conv_ref, land_ref, conv_land_ref, send1, recv1, send2, recv2, csend, crecv, local_sems):
        x, y, c, chips = _place()
        me = 2 * x + y
        sibling = (x, y, 1 - c)
        part = lambda core: pl.ds(pl.multiple_of(core * part_rows, 16), part_rows)
        remote = lambda src, dst, ss, rs, to: pltpu.make_async_remote_copy(
            src_ref=src, dst_ref=dst, send_sem=ss, recv_sem=rs, device_id=to, device_id_type=_MESH)
        local = [pltpu.make_async_copy(pack_ref, land_ref.at[me], local_sems.at[0]),
                 pltpu.make_async_copy(conv_ref, conv_land_ref.at[me], local_sems.at[1])]
        for cp in local:
            cp.start()
        first = [remote(pack_ref.at[part(c)], land_ref.at[me, part(c)], send1.at[j], recv1.at[j], (px, py, c))
                 for j, (px, py) in enumerate(chips)]
        convs = [remote(conv_ref, conv_land_ref.at[me], csend.at[j], crecv.at[j], (px, py, c))
                 for j, (px, py) in enumerate(chips)]
        for cp in first + convs:
            cp.start()
        passed = []
        for j, (px, py) in enumerate(chips):
            slot = 2 * px + py
            remote(pack_ref.at[part(c)], land_ref.at[slot, part(c)], send1.at[j], recv1.at[j], (px, py, c)).wait_recv()
            cp = remote(land_ref.at[slot, part(c)], land_ref.at[slot, part(c)], send2.at[j], recv2.at[j], sibling)
            cp.start()
            passed.append(cp)
        for j, (px, py) in enumerate(chips):
            slot = 2 * px + py
            remote(land_ref.at[slot, part(1 - c)], land_ref.at[slot, part(1 - c)], send2.at[j], recv2.at[j],
                   sibling).wait_recv()
            remote(conv_ref, conv_land_ref.at[slot], csend.at[j], crecv.at[j], (px, py, c)).wait_recv()
        for cp in first + convs + passed:
            cp.wait_send()
        for cp in local:
            cp.wait()

    sems = [pltpu.SemaphoreType.DMA((3,))] * 6 + [pltpu.SemaphoreType.DMA((2,))]
    return pl.pallas_call(
        body, name=name, in_specs=[_ANY, _ANY], out_specs=[_ANY, _ANY],
        out_shape=[jax.ShapeDtypeStruct((N_SHARD,) + pack.shape, pack.dtype),
                   jax.ShapeDtypeStruct((N_SHARD,) + conv_w.shape, conv_w.dtype)],
        scratch_shapes=sems)(pack, conv_w)


def _half(core):
    return pl.ds(pl.multiple_of(core * HALF, HALF), HALF)


def _reduce_scatter_exchange(g, row0, rows):
    def plan(src, dst):
        x, y, c, chips = _place()
        peers = [(px, py, c if t == 0 else 1 - c) for px, py in chips for t in (0, 1)] + [(x, y, 1 - c)]
        sends = [(src[0].at[2 * px + py, pl.ds(row0, rows), _half(pc)], dst[0].at[k], (px, py, pc))
                 for k, (px, py, pc) in enumerate(peers)]
        return [], sends, [dst[0].at[k] for k in range(7)]

    return _Exchange([g], [jax.ShapeDtypeStruct((7, rows, HALF), g.dtype)], 7, 0, plan)


def _pair_window_exchange(g):
    def plan(src, dst):
        x, y, c, _ = _place()
        return [], [(src[0].at[:, :, _half(1 - c)], dst[0], (x, y, 1 - c))], [dst[0]]

    return _Exchange([g], [jax.ShapeDtypeStruct(g.shape[:2] + (HALF,), g.dtype)], 1, 0, plan)


def _chip_scatter_exchange(p, small):
    def plan(src, dst):
        x, y, c, chips = _place()
        mine = 4 * x + 2 * y + c
        peers = [(px, py, c if t == 0 else 1 - c) for px, py in chips for t in (0, 1)] + [(x, y, 1 - c)]
        sends = [(src[0].at[2 * px + py], dst[0].at[j], (px, py, c)) for j, (px, py) in enumerate(chips)]
        recvs = [dst[0].at[j] for j in range(3)]
        sends += [(src[1], dst[1].at[mine], peer) for peer in peers]
        recvs += [dst[1].at[4 * px + 2 * py + pc] for px, py, pc in peers]
        return [(src[1], dst[1].at[mine])], sends, recvs

    outs = [jax.ShapeDtypeStruct((3,) + p.shape[1:], p.dtype), jax.ShapeDtypeStruct((8,) + small.shape, small.dtype)]
    return _Exchange([p, small], outs, 10, 1, plan)


def _share_exchange(arrays):
    n = len(arrays)

    def plan(src, dst):
        x, y, c, _ = _place()
        return [], [(src[k], dst[k], (x, y, 1 - c)) for k in range(n)], [dst[k] for k in range(n)]

    return _Exchange(arrays, [jax.ShapeDtypeStruct(a.shape, a.dtype) for a in arrays], n, 0, plan)


def _sum_scatter(g, lands, me, core, *, tc, name):
    rows = g.shape[1]
    per = HALF // tc
    n = len(lands)

    def body(*refs):
        g_ref, land_refs, o_ref = refs[1], refs[2:2 + n], refs[2 + n]
        at = 0
        for land_ref in land_refs:
            run = slice(at, at + land_ref.shape[1])
            acc = g_ref[run, :].astype(F32)
            for k in range(7):
                acc = acc + land_ref[k].astype(F32)
            o_ref[run, :] = acc
            at = run.stop

    return pl.pallas_call(
        body, name=name, out_shape=jax.ShapeDtypeStruct((rows, HALF), F32), compiler_params=_cp("parallel"),
        grid_spec=pltpu.PrefetchScalarGridSpec(
            num_scalar_prefetch=1, grid=(per,),
            in_specs=[pl.BlockSpec((None, rows, tc), lambda i, w: (w[0], 0, w[1] * per + i))]
            + [pl.BlockSpec((7, a.shape[1], tc), lambda i, w: (0, 0, i)) for a in lands],
            out_specs=pl.BlockSpec((rows, tc), lambda i, w: (0, i))))(
        jnp.stack([me, core]).astype(jnp.int32), g, *lands)


def _pair_add(g, land, core, *, name):
    n, rows, _ = g.shape

    def body(core_ref, g_ref, land_ref, o_ref):
        o_ref[...] = (g_ref[...].astype(F32) + land_ref[...].astype(F32)).astype(o_ref.dtype)

    blk = pl.BlockSpec((1, rows, HALF), lambda i, w: (i, 0, 0))
    return pl.pallas_call(
        body, name=name, out_shape=jax.ShapeDtypeStruct((n, rows, HALF), g.dtype), compiler_params=_cp("parallel"),
        grid_spec=pltpu.PrefetchScalarGridSpec(
            num_scalar_prefetch=1, grid=(n,),
            in_specs=[pl.BlockSpec((1, rows, HALF), lambda i, w: (i, 0, w[0])), blk], out_specs=blk))(
        jnp.reshape(core, (1,)).astype(jnp.int32), g, land)


def _sum_chips(p, land, me, *, tc, name):
    rows = p.shape[1]

    def body(me_ref, p_ref, land_ref, o_ref):
        acc = p_ref[...].astype(F32)
        for k in range(3):
            acc = acc + land_ref[k].astype(F32)
        o_ref[...] = acc

    return pl.pallas_call(
        body, name=name, out_shape=jax.ShapeDtypeStruct((rows, HALF), F32), compiler_params=_cp("parallel"),
        grid_spec=pltpu.PrefetchScalarGridSpec(
            num_scalar_prefetch=1, grid=(HALF // tc,),
            in_specs=[pl.BlockSpec((None, rows, tc), lambda i, w: (w[0], 0, i)),
                      pl.BlockSpec((3, rows, tc), lambda i, w: (0, 0, i))],
            out_specs=pl.BlockSpec((rows, tc), lambda i, w: (0, i))))(
        jnp.reshape(me, (1,)).astype(jnp.int32), p, land)


def _sum_slots(a, *, name):
    n = a.shape[0]

    def body(a_ref, o_ref):
        acc = a_ref[0]
        for k in range(1, n):
            acc = acc + a_ref[k]
        o_ref[...] = acc

    return pl.pallas_call(body, name=name, out_shape=jax.ShapeDtypeStruct(a.shape[1:], a.dtype))(a)


def _elementwise(fn, ins, n_out, block, *, name):
    shape = ins[0].shape
    grid = tuple(s // b for s, b in zip(shape, block))
    n_in = len(ins)

    def body(*refs):
        outs = fn(*[r[...] for r in refs[:n_in]])
        for o_ref, val in zip(refs[n_in:], outs):
            o_ref[...] = val

    spec = pl.BlockSpec(block, lambda i, j, k: (i, j, k))
    return pl.pallas_call(body, name=name, grid=grid, in_specs=[spec] * n_in, out_specs=[spec] * n_out,
                          out_shape=[jax.ShapeDtypeStruct(shape, F32)] * n_out,
                          compiler_params=_cp(*["parallel"] * 3))(*ins)


def _adamw_math(w, g, m, v):
    mn = ADAM_B1 * m + (1.0 - ADAM_B1) * g
    vn = ADAM_B2 * v + (1.0 - ADAM_B2) * (g * g)
    m_hat = mn / (1.0 - ADAM_B1 ** ADAM_STEP)
    v_hat = vn / (1.0 - ADAM_B2 ** ADAM_STEP)
    return -ADAM_LR * (m_hat / (jnp.sqrt(v_hat) + ADAM_EPS) + ADAM_WD * w), mn, vn


def _adamw(w, g, m, v, block, *, name):
    return _elementwise(_adamw_math, [w, g, m, v], 3, block, name=name)


def _interleave_layers(layers, *, tc, name):
    rows, cols = layers[0].shape
    n = len(layers)

    def body(*refs):
        for l in range(n):
            refs[n][:, l, :] = refs[l][...]

    return pl.pallas_call(body, name=name, grid=(cols // tc,),
                          in_specs=[pl.BlockSpec((rows, tc), lambda i: (0, i))] * n,
                          out_specs=pl.BlockSpec((rows, n, tc), lambda i: (0, 0, i)),
                          out_shape=jax.ShapeDtypeStruct((rows, n, cols), layers[0].dtype),
                          compiler_params=_cp("parallel"))(*layers)


def _adamw_small(ws, gs, ms, vs, *, name):
    n = len(ws)

    def body(*refs):
        w, g, m, v, outs = refs[:n], refs[n:2 * n], refs[2 * n:3 * n], refs[3 * n:4 * n], refs[4 * n:]
        for k in range(n):
            for slot, val in enumerate(_adamw_math(w[k][...], g[k][...], m[k][...], v[k][...])):
                outs[slot * n + k][...] = val

    outs = pl.pallas_call(body, name=name, out_shape=[jax.ShapeDtypeStruct(a.shape, F32) for a in ws] * 3)(
        *ws, *gs, *ms, *vs)
    return outs[:n], outs[n:2 * n], outs[2 * n:]


def _to_kernel_order(wt):
    gates = jnp.pad(wt[2048:2056], ((0, LANE - 2 * A_HEADS), (0, 0)))
    return jnp.concatenate([wt[0:2048], wt[2056:2568], wt[2824:3336], wt[2568:2696], wt[2696:2824], gates], axis=0)


def _from_kernel_order(main, tail):
    return jnp.concatenate([main[0:2048], tail[C_BG - DH_MAIN:C_BG - DH_MAIN + 2 * A_HEADS],
                            main[C_QB:C_QB + B_WIDTH], tail[0:B_KV_WIDTH], tail[B_KV_WIDTH:2 * B_KV_WIDTH],
                            main[C_ZB:C_ZB + B_WIDTH]], axis=0)


def _gate_params(a_log, dt_bias):
    return jnp.pad(jnp.stack([a_log, dt_bias]), ((0, SUBLANE - 2), (A_HEADS, LANE - 2 * A_HEADS)))


SMALL = ("conv_w", "a_log", "dt_bias", "norm_w", "sinks", "ln_g", "ln_b")


def _pack(parts, cols):
    flat = jnp.concatenate([p.reshape(-1) for p in parts])
    rows = -(-flat.shape[0] // cols)
    return jnp.pad(flat, (0, rows * cols - flat.shape[0])).reshape(rows, cols)


def _unpack(packed, shapes):
    flat = packed.reshape(-1)
    out, at = [], 0
    for s in shapes:
        n = math.prod(s)
        out.append(flat[at:at + n].reshape(s))
        at += n
    return out


def kernel(x, w_in, conv_w, a_log, dt_bias, norm_w, sinks, w_out, ln_g, ln_b, loss_target, m_w_in, m_conv_w, m_a_log, m_dt_bias, m_norm_w, m_sinks, m_w_out, m_ln_g, m_ln_b, v_w_in, v_conv_w, v_a_log, v_dt_bias, v_norm_w, v_sinks, v_w_out, v_ln_g, v_ln_b):
    xi, yi, ci = lax.axis_index("x"), lax.axis_index("y"), lax.axis_index("c")
    me = 2 * xi + yi

    to_t = lambda a: jnp.transpose(a, (2, 0, 1))
    from_t = lambda a: jnp.transpose(a, (1, 2, 0))

    wt_shard = to_t(w_in)

    def pack_weights(l):
        rows = jnp.pad(wt_shard[:, l], ((0, IN_PAD - IN_SHARD), (0, 0)))
        return jnp.concatenate([rows, w_out[l]], axis=0).astype(BF16)

    pack0, pack1 = pack_weights(0), pack_weights(1)
    got_in0, g_conv = _gather_two_level(pack0[:IN_PAD], conv_w, name="gather_weights_0")
    conv_full = jnp.moveaxis(g_conv, 0, 2).reshape(DEPTH, CONV_K, 3 * A_WIDTH)
    carriers = ("dn_pre", "dn_wy", "dn_scan")
    cuts = (0, 288, 624, IN_PAD)
    gathers = {nm: _gather_exchange([pack1[cuts[i]:cuts[i + 1]]]) for i, nm in enumerate(carriers)}
    gathers.update(in_proj=_gather_exchange([pack0[IN_PAD:]]), swa=_gather_exchange([pack1[IN_PAD:]]))
    w_in_of = lambda rows: _to_kernel_order(rows[:, :IN_SHARD].reshape(IN_COLS, D_MODEL))
    w_out_of = lambda rows: rows.reshape(D_MODEL, D_MODEL)
    args0 = _layer_args(w_in_of(got_in0), conv_full[0], a_log[0], dt_bias[0], sinks[0], norm_w[0],
                        lambda got: w_out_of(got[0]))

    def args1(got):
        rows = jnp.concatenate([got[nm][0] for nm in carriers], axis=1)
        return _layer_args(w_in_of(rows), conv_full[1], a_log[1], dt_bias[1], sinks[1], norm_w[1],
                           w_out_of(got["swa"][0]))

    def pack_grads(g):
        gin = _from_kernel_order(*g["w_in"]).reshape(N_SHARD, IN_SHARD, D_MODEL)
        gin = jnp.pad(gin, ((0, 0), (0, IN_PAD - IN_SHARD), (0, 0)))
        return jnp.concatenate([gin, g["w_out"].reshape(N_SHARD, OUT_SHARD, D_MODEL)], axis=1).astype(BF16)

    packed = {}

    def reduce1(grads1):
        packed[1] = pack_grads(grads1)
        half_rows = packed[1].shape[1] // 2
        return dict(dn_chunk=_reduce_scatter_exchange(packed[1], 0, half_rows),
                    swa=_reduce_scatter_exchange(packed[1], half_rows, half_rows))

    def reduce0(grads0, grads1, loss_tile):
        g0 = pack_grads(grads0)
        from_sibling = _run_exchange(_pair_window_exchange(g0), name="pair_reduce_0")[0]
        packed[0] = _pair_add(g0, from_sibling, ci, name="pair_add_0")
        gsmall = _pack([jnp.stack([g[nm] for g in (grads0, grads1)]) for nm in SMALL] + [loss_tile[0, 0:1]], D_MODEL)
        return _chip_scatter_exchange(packed[0], gsmall)

    _, dx, grads, landed1, (landed0, landed_small) = _local_step(
        x[0], loss_target[0], args0, args1, ln_g, ln_b, gathers=gathers, reduce1=reduce1, reduce0=reduce0)

    small_shapes = [(DEPTH,) + grads[0][nm].shape for nm in SMALL]
    halves = [_sum_chips(packed[0], landed0, me, tc=2 * LANE, name="reduce_sum_0"),
              _sum_scatter(packed[1], [landed1["dn_chunk"][0], landed1["swa"][0]], me, ci, tc=2 * LANE,
                           name="reduce_sum_1")]
    s_small = _sum_slots(landed_small, name="reduce_sum_small")
    others = _run_exchange(_share_exchange(halves), name="pair_share")
    full = [jnp.where(ci == 0, jnp.concatenate([mine, other], axis=1), jnp.concatenate([other, mine], axis=1))
            for mine, other in zip(halves, others)]
    grad_in_layers = [f[:IN_SHARD] for f in full]
    grad_out = jnp.stack([f[IN_PAD:] for f in full])
    out_blk = (1, OUT_SHARD, D_MODEL)
    *small_grads, loss = _unpack(s_small, small_shapes + [()])
    gs = dict(zip(SMALL, small_grads))
    gs["conv_w"] = lax.dynamic_slice_in_dim(gs["conv_w"], me * CONV_SHARD, CONV_SHARD, axis=2)

    grad_in_t = _interleave_layers(grad_in_layers, tc=2 * LANE, name="grad_in_layers")
    d_in, nm_in, nv_in = (from_t(o) for o in _adamw(to_t(w_in), grad_in_t, to_t(m_w_in), to_t(v_w_in),
                                                    (IN_SHARD // 6, DEPTH, D_MODEL), name="adamw_in"))
    grad_in = from_t(grad_in_t)
    d_out, nm_out, nv_out = _adamw(w_out, grad_out, m_w_out, v_w_out, out_blk, name="adamw_out")
    ws = dict(conv_w=conv_w, a_log=a_log, dt_bias=dt_bias, norm_w=norm_w, sinks=sinks, ln_g=ln_g, ln_b=ln_b)
    ms = dict(conv_w=m_conv_w, a_log=m_a_log, dt_bias=m_dt_bias, norm_w=m_norm_w, sinks=m_sinks, ln_g=m_ln_g, ln_b=m_ln_b)
    vs = dict(conv_w=v_conv_w, a_log=v_a_log, dt_bias=v_dt_bias, norm_w=v_norm_w, sinks=v_sinks, ln_g=v_ln_g, ln_b=v_ln_b)
    d_s, nm_s, nv_s = (dict(zip(SMALL, o)) for o in _adamw_small(*[[d[nm] for nm in SMALL] for d in (ws, gs, ms, vs)],
                                                                 name="adamw_small"))

    def in_order(big_in, small, big_out):
        return (big_in, small["conv_w"], small["a_log"], small["dt_bias"], small["norm_w"], small["sinks"], big_out,
                small["ln_g"], small["ln_b"])

    return (loss, dx[None], *in_order(grad_in, gs, grad_out), *in_order(d_in, d_s, d_out),
            *in_order(nm_in, nm_s, nm_out), *in_order(nv_in, nv_s, nv_out))
```

```python
import math

import jax
import jax.numpy as jnp
from jax import lax
from jax.experimental import pallas as pl
from jax.experimental.pallas import tpu as pltpu

F32 = jnp.float32
BF16 = jnp.bfloat16
HI = lax.Precision.HIGHEST

D_MODEL = 1024
DEPTH = 2
A_HEADS = 4
A_HEAD_DIM = 128
A_WIDTH = 512
CONV_K = 4
CHUNK = 64
B_Q_HEADS = 8
B_KV_HEADS = 2
B_HEAD_DIM = 64
B_GROUP = 4
B_WIDTH = 512
B_KV_WIDTH = 128
BLOCK = 128
IN_COLS = 3336
DEEPNORM_ALPHA = (2 * DEPTH) ** 0.25
LN_EPS = 1e-5
RMS_EPS = 1e-6
L2_EPS = 1e-6
ADAM_LR = 0.001
ADAM_B1 = 0.9
ADAM_B2 = 0.999
ADAM_EPS = 1e-08
ADAM_WD = 0.01
ADAM_STEP = 10

N_SHARD = 4
IN_SHARD = IN_COLS // N_SHARD
OUT_SHARD = D_MODEL // N_SHARD
CONV_SHARD = 3 * A_WIDTH // N_SHARD
IN_PAD = -(-IN_SHARD // 96) * 96

P_COLS = 3456
C_PRE = 0
C_ZA = 1536
C_QB = 2048
C_ZB = 2560
C_KB = 3072
C_VB = 3200
C_BG = 3328
DH_MAIN = C_KB
LANE = 128
SUBLANE = 8
HALO = 16
VMEM_LIMIT = 56 * 1024 * 1024
ALIBI = tuple(2.0 ** (-8.0 * (h + 1) / B_Q_HEADS) for h in range(B_Q_HEADS))
NEG = -1e30


def _cp(*sem):
    return pltpu.CompilerParams(dimension_semantics=sem, vmem_limit_bytes=VMEM_LIMIT)


def _dot(a, b):
    return jnp.dot(a.astype(BF16), b.astype(BF16), preferred_element_type=F32)


def _dot_nt(a, b):
    return lax.dot_general(a.astype(BF16), b.astype(BF16), (((1,), (1,)), ((), ())),
                           preferred_element_type=F32)


def _dot_tn(a, b):
    return lax.dot_general(a.astype(BF16), b.astype(BF16), (((0,), (0,)), ((), ())),
                           preferred_element_type=F32)


def _dot_hi(a, b):
    return jnp.dot(a, b, precision=HI, preferred_element_type=F32)


def _sigmoid(x):
    return jax.nn.sigmoid(x)


def _silu(x):
    return x * _sigmoid(x)


def _silu_and_grad(x):
    s = _sigmoid(x)
    return x * s, s * (1.0 + x * (1.0 - s))


def _softplus(x):
    return jnp.maximum(x, 0.0) + jnp.log(1.0 + jnp.exp(-jnp.abs(x)))


def _shift_down(cur, before, s):
    if s == 0:
        return cur
    r = pltpu.roll(cur, s, 0)
    rb = pltpu.roll(before, s, 0)
    row = lax.broadcasted_iota(jnp.int32, before.shape, 0)
    head = jnp.where(row < s, rb, r[0:SUBLANE])
    return jnp.concatenate([head, r[SUBLANE:]], axis=0)


def _shift_up(cur, after, s):
    if s == 0:
        return cur
    n = cur.shape[0]
    r = pltpu.roll(cur, n - s, 0)
    ra = pltpu.roll(after, SUBLANE - s, 0)
    row = lax.broadcasted_iota(jnp.int32, after.shape, 0)
    tail = jnp.where(row >= SUBLANE - s, ra, r[n - SUBLANE:])
    return jnp.concatenate([r[:n - SUBLANE], tail], axis=0)


def _conv_fwd(cur, before, w):
    acc = cur * w[CONV_K - 1:CONV_K, :]
    for s in range(1, CONV_K):
        acc = acc + _shift_down(cur, before, s) * w[CONV_K - 1 - s:CONV_K - s, :]
    return acc


def _matmul_nt(a, bt, *, tm, name, carry=None):
    m, k = a.shape
    n = bt.shape[0]
    c_ins, c_in_specs, c_out_specs, c_outs, c_scratch = _carry_specs(carry)

    def body(*refs):
        a_ref, b_ref, o_ref = _carried(carry, refs, 2, 1, m // tm)
        o_ref[...] = _dot_nt(a_ref[...], b_ref[...]).astype(o_ref.dtype)

    outs = pl.pallas_call(
        body, name=name, grid=(m // tm,),
        in_specs=[pl.BlockSpec((tm, k), lambda i: (i, 0)), pl.BlockSpec((n, k), lambda i: (0, 0))] + c_in_specs,
        out_specs=[pl.BlockSpec((tm, n), lambda i: (i, 0))] + c_out_specs,
        out_shape=[jax.ShapeDtypeStruct((m, n), BF16)] + c_outs,
        scratch_shapes=c_scratch,
        compiler_params=_cp("arbitrary"))(a, bt, *c_ins)
    return outs[0], outs[1:]


def _dn_pre(h, conv_w, par, *, tt, name, carry=None):
    t = h.shape[0]
    cw = 3 * A_WIDTH
    hb = tt // HALO

    c_ins, c_in_specs, c_out_specs, c_outs, c_scratch = _carry_specs(carry)

    def body(*refs):
        (pre_ref, halo_ref, bgi_ref, cw_ref, par_ref,
         q_ref, k_ref, v_ref, bg_ref, bgt_ref, c_ref) = _carried(carry, refs, 5, 6, t // tt)
        i = pl.program_id(0)
        cur = pre_ref[...].astype(F32)
        before = jnp.where(i > 0, halo_ref[...].astype(F32)[HALO - SUBLANE:], 0.0)
        conv = _conv_fwd(cur, before, cw_ref[...])
        c_ref[...] = conv
        s = _silu(conv)
        for hd in range(A_HEADS):
            sl = slice(hd * LANE, (hd + 1) * LANE)
            tq = s[:, hd * LANE:(hd + 1) * LANE]
            q_ref[:, sl] = tq * (lax.rsqrt(jnp.sum(tq * tq, -1, keepdims=True) + L2_EPS) * (A_HEAD_DIM ** -0.5))
            tk = s[:, A_WIDTH + hd * LANE:A_WIDTH + (hd + 1) * LANE]
            k_ref[:, sl] = tk * lax.rsqrt(jnp.sum(tk * tk, -1, keepdims=True) + L2_EPS)
        v_ref[...] = s[:, 2 * A_WIDTH:]
        raw = bgi_ref[...].astype(F32)
        lane = lax.broadcasted_iota(jnp.int32, raw.shape, 1)
        is_a = (lane >= A_HEADS) & (lane < 2 * A_HEADS)
        g = jnp.where(is_a, -jnp.exp(par_ref[0:1, :]) * _softplus(raw + par_ref[1:2, :]), 0.0)
        gc = _dot_hi(_chunk_tri(tt, lower=True), g)
        bg = jnp.where(lane < A_HEADS, _sigmoid(raw), gc)
        bg_ref[...] = bg
        bgt_ref[...] = jnp.transpose(bg)[0:SUBLANE, :]

    wide = jax.ShapeDtypeStruct((t, A_WIDTH), F32)
    outs = pl.pallas_call(
        body, name=name, grid=(t // tt,),
        in_specs=[pl.BlockSpec((tt, cw), lambda i: (i, 0)),
                  pl.BlockSpec((HALO, cw), lambda i: (jnp.maximum(i * hb - 1, 0), 0)),
                  pl.BlockSpec((tt, LANE), lambda i: (i, C_BG // LANE)),
                  pl.BlockSpec((CONV_K, cw), lambda i: (0, 0)),
                  pl.BlockSpec((SUBLANE, LANE), lambda i: (0, 0))] + c_in_specs,
        out_specs=[pl.BlockSpec((tt, A_WIDTH), lambda i: (i, 0))] * 3
        + [pl.BlockSpec((tt, LANE), lambda i: (i, 0)), pl.BlockSpec((SUBLANE, tt), lambda i: (0, i)),
           pl.BlockSpec((tt, cw), lambda i: (i, 0))] + c_out_specs,
        out_shape=[wide, wide, wide, jax.ShapeDtypeStruct((t, LANE), F32),
                   jax.ShapeDtypeStruct((SUBLANE, t), F32), jax.ShapeDtypeStruct((t, cw), F32)] + c_outs,
        scratch_shapes=c_scratch,
        compiler_params=_cp("arbitrary"))(h, h, h, conv_w, par, *c_ins)
    return outs[:6], outs[6:]


def _chunk_tri(n, lower):
    r = lax.broadcasted_iota(jnp.int32, (n, n), 0)
    c = lax.broadcasted_iota(jnp.int32, (n, n), 1)
    shift = CHUNK.bit_length() - 1
    same = jnp.right_shift(r, shift) == jnp.right_shift(c, shift)
    return (same & ((c <= r) if lower else (c >= r))).astype(F32)


def _chunk_masks():
    r = lax.broadcasted_iota(jnp.int32, (CHUNK, CHUNK), 0)
    c = lax.broadcasted_iota(jnp.int32, (CHUNK, CHUNK), 1)
    return r >= c, r > c, r == c


def _split(a):
    hi = a.astype(BF16)
    return hi, (a - hi.astype(F32)).astype(BF16)


def _dot3(a, b):
    (ah, al), (bh, bl) = a, b
    d = lambda p, q: jnp.dot(p, q, preferred_element_type=F32)
    return d(ah, bh) + (d(ah, bl) + d(al, bh))


def _tri_inv_many(a_list, eye):
    d = lambda p, q: jnp.dot(p.astype(BF16), q.astype(BF16), preferred_element_type=F32)
    r = lax.broadcasted_iota(jnp.int32, (CHUNK, CHUNK), 0)
    c = lax.broadcasted_iota(jnp.int32, (CHUNK, CHUNK), 1)
    same = lambda b: jnp.right_shift(r, b.bit_length() - 1) == jnp.right_shift(c, b.bit_length() - 1)
    x = [jnp.where(same(8), -a, 0.0) for a in a_list]
    tm = [eye + xi for xi in x]
    for _ in range(2):
        x = [d(xi, xi) for xi in x]
        tm = [t + d(t, xi) for t, xi in zip(tm, x)]
    for b in (16, 32, 64):
        low = [jnp.where(same(b) & ~same(b // 2), a, 0.0) for a in a_list]
        tm = [t - d(t, d(lo, t)) for t, lo in zip(tm, low)]
    res = [eye - _dot3(_split(eye + a), _split(t)) for a, t in zip(a_list, tm)]
    return [t + d(t, rs) for t, rs in zip(tm, res)]


def _chunk_gates(bg_v, bgt_v, hd):
    return (bg_v[:, hd:hd + 1], bg_v[:, A_HEADS + hd:A_HEADS + hd + 1],
            None if bgt_v is None else bgt_v[A_HEADS + hd:A_HEADS + hd + 1, :])


WY_ROWS = 512
SCAN_ROWS = 512
WY_GROUP = 8


def _dn_wy(q, k, v, bg, bgt, *, name, carry=None):
    t = q.shape[0]
    rows = WY_ROWS

    c_ins, c_in_specs, c_out_specs, c_outs, c_scratch = _carry_specs(carry)

    def body(*refs):
        q_ref, k_ref, v_ref, bg_ref, bgt_ref, u_ref, w_ref, tm_ref, qk_ref = _carried(carry, refs, 5, 4, t // rows)
        causal, strict, diag = _chunk_masks()
        eye = diag.astype(F32)
        for c0 in range(0, rows // CHUNK, WY_GROUP):
            items = [(c, hd) for c in range(c0, c0 + WY_GROUP) for hd in range(A_HEADS)]
            rs = lambda c: slice(c * CHUNK, (c + 1) * CHUNK)
            sl = lambda hd: slice(hd * LANE, (hd + 1) * LANE)
            hs = lambda hd: slice(hd * CHUNK, (hd + 1) * CHUNK)
            gates = [_chunk_gates(bg_ref[rs(c), :], bgt_ref[:, rs(c)], hd) for c, hd in items]
            dms = [jnp.exp(jnp.where(causal, gcol - grow, NEG)) for _, gcol, grow in gates]
            kbs = [k_ref[rs(c), sl(hd)] * g[0] for (c, hd), g in zip(items, gates)]
            a_list = [jnp.where(strict, _dot_nt(kb, k_ref[rs(c), sl(hd)]) * dm, 0.0)
                      for (c, hd), kb, dm in zip(items, kbs, dms)]
            for (c, hd), dm in zip(items, dms):
                qk_ref[rs(c), hs(hd)] = jnp.where(
                    causal, _dot_nt(q_ref[rs(c), sl(hd)], k_ref[rs(c), sl(hd)]) * dm, 0.0)
            tms = _tri_inv_many(a_list, eye)
            for (c, hd), g, kb, tmat in zip(items, gates, kbs, tms):
                tm_ref[rs(c), hs(hd)] = tmat
                u_ref[rs(c), sl(hd)] = _dot(tmat, v_ref[rs(c), sl(hd)] * g[0])
                w_ref[rs(c), sl(hd)] = _dot(tmat, kb * jnp.exp(g[1])).astype(BF16)

    blk = pl.BlockSpec((rows, A_WIDTH), lambda i: (i, 0))
    half = pl.BlockSpec((rows, A_HEADS * CHUNK), lambda i: (i, 0))
    outs = pl.pallas_call(
        body, name=name, grid=(t // rows,),
        in_specs=[blk, blk, blk, pl.BlockSpec((rows, LANE), lambda i: (i, 0)),
                  pl.BlockSpec((SUBLANE, rows), lambda i: (0, i))] + c_in_specs,
        out_specs=[blk, blk, half, half] + c_out_specs,
        out_shape=[jax.ShapeDtypeStruct((t, A_WIDTH), F32), jax.ShapeDtypeStruct((t, A_WIDTH), BF16),
                   jax.ShapeDtypeStruct((t, A_HEADS * CHUNK), F32),
                   jax.ShapeDtypeStruct((t, A_HEADS * CHUNK), F32)] + c_outs,
        scratch_shapes=c_scratch,
        compiler_params=_cp("arbitrary"))(q, k, v, bg, bgt, *c_ins)
    return outs[:4], outs[4:]


def _dn_scan_fwd(q, k, u, w, qk, bg, *, name, carry=None):
    t = q.shape[0]
    rows = SCAN_ROWS
    per = rows // CHUNK
    c_ins, c_in_specs, c_out_specs, c_outs, c_scratch = _carry_specs(carry)

    def body(*refs):
        q_ref, k_ref, u_ref, w_ref, qk_ref, bg_ref, o_ref, vn_ref, s_ref, state = _carried(carry, refs, 6, 3, t // rows)

        @pl.when(pl.program_id(0) == 0)
        def _():
            state[...] = jnp.zeros_like(state)

        heads = range(A_HEADS)
        sl = lambda hd: slice(hd * LANE, (hd + 1) * LANE)
        s_cur = [state[hd] for hd in heads]
        for c in range(per):
            rs = slice(c * CHUNK, (c + 1) * CHUNK)
            bg_v = bg_ref[rs, :]
            gcols = [_chunk_gates(bg_v, None, hd)[1] for hd in heads]
            glasts = [gc[CHUNK - 1:CHUNK, :] for gc in gcols]
            for hd in heads:
                s_ref[c, hd] = s_cur[hd].astype(BF16)
            vns = [u_ref[rs, sl(hd)] - _dot(w_ref[rs, sl(hd)], s_cur[hd]) for hd in heads]
            qss = [_dot(q_ref[rs, sl(hd)] * jnp.exp(gcols[hd]), s_cur[hd]) for hd in heads]
            s_cur = [s_cur[hd] * jnp.exp(glasts[hd])
                     + _dot_tn(k_ref[rs, sl(hd)] * jnp.exp(glasts[hd] - gcols[hd]), vns[hd]) for hd in heads]
            for hd in heads:
                vn_ref[rs, sl(hd)] = vns[hd].astype(BF16)
                o_ref[rs, sl(hd)] = qss[hd] + _dot(qk_ref[rs, hd * CHUNK:(hd + 1) * CHUNK], vns[hd])
        for hd in heads:
            state[hd] = s_cur[hd]

    blk = pl.BlockSpec((rows, A_WIDTH), lambda i: (i, 0))
    half = pl.BlockSpec((rows, A_HEADS * CHUNK), lambda i: (i, 0))
    wide = jax.ShapeDtypeStruct((t, A_WIDTH), F32)
    outs = pl.pallas_call(
        body, name=name, grid=(t // rows,),
        in_specs=[blk, blk, blk, blk, half, pl.BlockSpec((rows, LANE), lambda i: (i, 0))] + c_in_specs,
        out_specs=[blk, blk, pl.BlockSpec((per, A_HEADS, LANE, LANE), lambda i: (i, 0, 0, 0))] + c_out_specs,
        out_shape=[wide, jax.ShapeDtypeStruct((t, A_WIDTH), BF16),
                   jax.ShapeDtypeStruct((t // CHUNK, A_HEADS, LANE, LANE), BF16)] + c_outs,
        scratch_shapes=[pltpu.VMEM((A_HEADS, LANE, LANE), F32)] + c_scratch,
        compiler_params=_cp("arbitrary"))(q, k, u, w, qk, bg, *c_ins)
    return outs[:3], outs[3:]


def _stack_heads(ref, hk):
    return jnp.concatenate([ref[:, h * B_HEAD_DIM:(h + 1) * B_HEAD_DIM].astype(F32)
                            for h in range(hk * B_GROUP, (hk + 1) * B_GROUP)], axis=0)


def _swa_window():
    qi = lax.broadcasted_iota(jnp.int32, (BLOCK, BLOCK), 0)
    kj = lax.broadcasted_iota(jnp.int32, (BLOCK, BLOCK), 1)
    dist = jnp.where(kj > qi, qi + BLOCK - kj, qi - kj).astype(F32)
    rows = lax.broadcasted_iota(jnp.int32, (B_GROUP * BLOCK, BLOCK), 0)
    cols = lax.broadcasted_iota(jnp.int32, (B_GROUP * BLOCK, BLOCK), 1)
    return cols > jnp.bitwise_and(rows, BLOCK - 1), dist


def _swa_group_probs(q_ref, sk_ref, kp, kc, vp, vc, n_blk):
    hks = range(B_KV_HEADS)
    heads = lambda hk: range(hk * B_GROUP, (hk + 1) * B_GROUP)
    ksl = lambda hk: slice(hk * B_HEAD_DIM, (hk + 1) * B_HEAD_DIM)
    upper, dist = _swa_window()
    no_prev = jnp.where(n_blk > 0, 0.0, NEG)
    ones = jnp.ones((BLOCK, B_HEAD_DIM), BF16)
    with_ones = lambda v, hk: jnp.concatenate([v[:, ksl(hk)].astype(BF16), ones], axis=1)
    qs = [_stack_heads(q_ref, hk) * (B_HEAD_DIM ** -0.5) for hk in hks]
    sink = [jnp.concatenate([jnp.broadcast_to(sk_ref[h:h + 1, 0:1], (BLOCK, 1)) for h in heads(hk)], axis=0)
            for hk in hks]
    s = [jnp.where(upper, _dot_nt(qs[hk], kp[:, ksl(hk)]) + no_prev, _dot_nt(qs[hk], kc[:, ksl(hk)]))
         - jnp.concatenate([ALIBI[h] * dist for h in heads(hk)], axis=0) for hk in hks]
    m = [jnp.maximum(jnp.max(s[hk], axis=-1, keepdims=True), sink[hk]) for hk in hks]
    p = [jnp.exp(s[hk] - m[hk]) for hk in hks]
    p_up = [jnp.where(upper, p[hk], 0.0) for hk in hks]
    oe = [jnp.dot(p_up[hk].astype(BF16), with_ones(vp, hk), preferred_element_type=F32)
          + jnp.dot((p[hk] - p_up[hk]).astype(BF16), with_ones(vc, hk), preferred_element_type=F32) for hk in hks]
    ps = [jnp.exp(sink[hk] - m[hk]) for hk in hks]
    inv = [1.0 / (oe[hk][:, B_HEAD_DIM:B_HEAD_DIM + 1] + ps[hk]) for hk in hks]
    return upper, [(qs[hk], p[hk] * inv[hk], ps[hk] * inv[hk], oe[hk][:, :B_HEAD_DIM] * inv[hk]) for hk in hks]


def _swa_specs():
    qspec = lambda c0: pl.BlockSpec((BLOCK, B_WIDTH), lambda i: (i, c0 // B_WIDTH))
    cur = lambda c0: pl.BlockSpec((BLOCK, LANE), lambda i: (i, c0 // LANE))
    prev = lambda c0: pl.BlockSpec((BLOCK, LANE), lambda i: (jnp.maximum(i - 1, 0), c0 // LANE))
    return qspec, cur, prev


def _carried(carry, refs, n_in, n_out, steps):
    if carry is None:
        return refs
    ci, co = len(carry.ins), len(carry.outs)
    own = refs[:n_in] + refs[n_in + ci:n_in + ci + n_out] + refs[n_in + ci + n_out + co:len(refs) - 3]
    parts = refs[n_in:n_in + ci], refs[n_in + ci + n_out:n_in + ci + n_out + co], refs[len(refs) - 3:]

    @pl.when(pl.program_id(0) == 0)
    def _():
        carry.start(*parts)

    @pl.when(pl.program_id(0) == steps - 1)
    def _():
        carry.finish(*parts)

    return own


def _carry_specs(carry):
    if carry is None:
        return [], [], [], [], []
    return (list(carry.ins), [_ANY] * len(carry.ins), [_ANY] * len(carry.outs), list(carry.outs), carry.scratch())


def _swa_fwd(h, sinks_b, *, name, carry=None):
    t = h.shape[0]
    qspec, cur, prev = _swa_specs()
    c_ins, c_in_specs, c_out_specs, c_outs, c_scratch = _carry_specs(carry)

    def body(*refs):
        q_ref, kc_ref, kp_ref, vc_ref, vp_ref, sk_ref, o_ref, p_ref, ps_ref = _carried(carry, refs, 6, 3, t // BLOCK)
        n_blk = pl.program_id(0)
        _, groups = _swa_group_probs(q_ref, sk_ref, kp_ref[...], kc_ref[...], vp_ref[...], vc_ref[...], n_blk)
        lane = lax.broadcasted_iota(jnp.int32, (BLOCK, LANE), 1)
        sink_probs = jnp.zeros((BLOCK, LANE), F32)
        for hk, (_, p, ps, o) in enumerate(groups):
            for g in range(B_GROUP):
                hq = hk * B_GROUP + g
                rows = slice(g * BLOCK, (g + 1) * BLOCK)
                o_ref[:, hq * B_HEAD_DIM:(hq + 1) * B_HEAD_DIM] = o[rows]
                p_ref[:, hq * BLOCK:(hq + 1) * BLOCK] = p[rows].astype(BF16)
                sink_probs = sink_probs + jnp.where(lane == hq, ps[rows], 0.0)
        ps_ref[...] = sink_probs

    row = lambda w: pl.BlockSpec((BLOCK, w), lambda i: (i, 0))
    outs = pl.pallas_call(
        body, name=name, grid=(t // BLOCK,),
        in_specs=[qspec(C_QB), cur(C_KB), prev(C_KB), cur(C_VB), prev(C_VB),
                  pl.BlockSpec((B_Q_HEADS, LANE), lambda i: (0, 0))] + c_in_specs,
        out_specs=[row(B_WIDTH), row(B_Q_HEADS * BLOCK), row(LANE)] + c_out_specs,
        out_shape=[jax.ShapeDtypeStruct((t, B_WIDTH), F32), jax.ShapeDtypeStruct((t, B_Q_HEADS * BLOCK), BF16),
                   jax.ShapeDtypeStruct((t, LANE), F32)] + c_outs,
        scratch_shapes=c_scratch,
        compiler_params=_cp("arbitrary"))(h, h, h, h, h, sinks_b, *c_ins)
    return outs[:3], outs[3:]


def _rms_gate(o, za, nw):
    outs = []
    for hd in range(A_HEADS):
        oh = o[:, hd * LANE:(hd + 1) * LANE]
        r = lax.rsqrt(jnp.mean(oh * oh, -1, keepdims=True) + RMS_EPS)
        outs.append(oh * r * nw)
    return jnp.concatenate(outs, axis=1) * _silu(za)


def _out_ln(x, oa, ob, h, norm_w, w_out, ln_g, ln_b, *, tm, name, target=None):
    t = x.shape[0]
    last = target is not None

    def body(*refs):
        x_ref, oa_ref, ob_ref, za_ref, zb_ref, nw_ref, w_ref, g_ref, b_ref = refs[:9]
        xn_ref, mx_ref, r_ref = refs[9 + last:12 + last]
        ya = _rms_gate(oa_ref[...], za_ref[...].astype(F32), nw_ref[...])
        yb = ob_ref[...] * _silu(zb_ref[...].astype(F32))
        mixed = jnp.concatenate([ya, yb], axis=1).astype(BF16)
        mx_ref[...] = mixed
        r = DEEPNORM_ALPHA * x_ref[...] + jnp.dot(mixed, w_ref[...], preferred_element_type=F32)
        r_ref[...] = r
        mu = jnp.mean(r, -1, keepdims=True)
        xc = r - mu
        var = jnp.mean(xc * xc, -1, keepdims=True)
        xn = xc * lax.rsqrt(var + LN_EPS) * g_ref[...] + b_ref[...]
        if not last:
            xn_ref[...] = xn
            return
        loss_ref = refs[13]

        @pl.when(pl.program_id(0) == 0)
        def _():
            loss_ref[...] = jnp.zeros_like(loss_ref)

        err = xn - refs[9][...]
        xn_ref[...] = err * (1.0 / D_MODEL)
        loss_ref[...] += 0.5 / D_MODEL * jnp.sum(err * err)

    row = lambda w, c: pl.BlockSpec((tm, w), lambda i: (i, c))
    full = lambda a, b: pl.BlockSpec((a, b), lambda i: (0, 0))
    wide = jax.ShapeDtypeStruct((t, D_MODEL), F32)
    return pl.pallas_call(
        body, name=name, grid=(t // tm,),
        in_specs=[row(D_MODEL, 0), row(A_WIDTH, 0), row(B_WIDTH, 0), row(A_WIDTH, C_ZA // A_WIDTH),
                  row(B_WIDTH, C_ZB // B_WIDTH), full(1, LANE), full(D_MODEL, D_MODEL), full(1, D_MODEL),
                  full(1, D_MODEL)] + [row(D_MODEL, 0)] * last,
        out_specs=[row(D_MODEL, 0), row(D_MODEL, 0), row(D_MODEL, 0)] + [full(SUBLANE, LANE)] * last,
        out_shape=[wide, jax.ShapeDtypeStruct((t, D_MODEL), BF16), wide]
        + [jax.ShapeDtypeStruct((SUBLANE, LANE), F32)] * last,
        compiler_params=_cp("arbitrary" if last else "parallel"))(
        x, oa, ob, h, h, norm_w, w_out, ln_g, ln_b, *([target] if last else []))


def _layer_fwd(x, wt, conv_w, par, sinks_b, norm_w, w_out_bf, ln_g, ln_b, l, carries=None, target=None):
    carries = carries or {}
    h, got_in = _matmul_nt(x, wt, tm=512, name=f"in_proj_{l}", carry=carries.get("in_proj"))
    if callable(w_out_bf):
        w_out_bf = w_out_bf(got_in)
    (q, k, v, bg, bgt, conv), got_pre = _dn_pre(h, conv_w, par, tt=512, name=f"dn_pre_{l}",
                                                carry=carries.get("dn_pre"))
    (u, w, tmat, qk), got_wy = _dn_wy(q, k, v, bg, bgt, name=f"dn_wy_{l}", carry=carries.get("dn_wy"))
    (oa, vn, s_all), got_scan = _dn_scan_fwd(q, k, u, w, qk, bg, name=f"dn_scan_{l}", carry=carries.get("dn_scan"))
    (ob, swa_p, swa_ps), got_swa = _swa_fwd(h, sinks_b, name=f"swa_fwd_{l}", carry=carries.get("swa"))
    xn, mixed, r, *loss = _out_ln(x, oa, ob, h, norm_w, w_out_bf, ln_g, ln_b, tm=512, name=f"out_ln_{l}", target=target)
    if loss:
        xn = (xn, loss[0])
    res = dict(x=x, h=h, q=q, k=k, v=v, bg=bg, bgt=bgt, w=w, tmat=tmat, qk=qk, vn=vn, oa=oa, s_all=s_all,
               mixed=mixed, r=r, w_out=w_out_bf, ob=ob, swa_p=swa_p, swa_ps=swa_ps, conv=conv)
    return xn, res, dict(in_proj=got_in, dn_pre=got_pre, dn_wy=got_wy, dn_scan=got_scan, swa=got_swa)


def _ln_out_bwd(dxn, r, mixed, ln_g, w_out, *, tm, name):
    t = dxn.shape[0]

    def body(dxn_ref, r_ref, mx_ref, g_ref, w_ref, dr_ref, dm_ref, dw_ref, dg_ref, db_ref):
        @pl.when(pl.program_id(0) == 0)
        def _():
            dw_ref[...] = jnp.zeros_like(dw_ref)
            dg_ref[...] = jnp.zeros_like(dg_ref)
            db_ref[...] = jnp.zeros_like(db_ref)

        rr = r_ref[...]
        xc = rr - jnp.mean(rr, -1, keepdims=True)
        rstd = lax.rsqrt(jnp.mean(xc * xc, -1, keepdims=True) + LN_EPS)
        xhat = xc * rstd
        dxn_v = dxn_ref[...]
        dxh = dxn_v * g_ref[...]
        dr = rstd * (dxh - jnp.mean(dxh, -1, keepdims=True) - xhat * jnp.mean(dxh * xhat, -1, keepdims=True))
        dr_ref[...] = dr
        dg_ref[...] += jnp.sum(dxn_v * xhat, axis=0, keepdims=True)
        db_ref[...] += jnp.sum(dxn_v, axis=0, keepdims=True)
        drb = dr.astype(BF16)
        dm_ref[...] = _dot_nt(drb, w_ref[...])
        dw_ref[...] += _dot_tn(mx_ref[...], drb)

    row = pl.BlockSpec((tm, D_MODEL), lambda i: (i, 0))
    full = lambda a, b: pl.BlockSpec((a, b), lambda i: (0, 0))
    big = jax.ShapeDtypeStruct((t, D_MODEL), F32)
    vec = jax.ShapeDtypeStruct((1, D_MODEL), F32)
    return pl.pallas_call(
        body, name=name, grid=(t // tm,),
        in_specs=[row, row, row, full(1, D_MODEL), full(D_MODEL, D_MODEL)],
        out_specs=[row, row, full(D_MODEL, D_MODEL), full(1, D_MODEL), full(1, D_MODEL)],
        out_shape=[big, big, jax.ShapeDtypeStruct((D_MODEL, D_MODEL), F32), vec, vec],
        compiler_params=_cp("arbitrary"))(dxn, r, mixed, ln_g, w_out)


def _dn_post_bwd(dm, oa, h, norm_w, *, tm, name):
    t = oa.shape[0]

    def body(dy_ref, o_ref, za_ref, nw_ref, do_ref, dza_ref, dnw_ref):
        @pl.when(pl.program_id(0) == 0)
        def _():
            dnw_ref[...] = jnp.zeros_like(dnw_ref)

        nw = nw_ref[...]
        dnw = jnp.zeros_like(nw)
        for hd in range(A_HEADS):
            sl = slice(hd * LANE, (hd + 1) * LANE)
            oh, za, dy = o_ref[:, sl], za_ref[:, sl].astype(F32), dy_ref[:, sl]
            rs = lax.rsqrt(jnp.mean(oh * oh, -1, keepdims=True) + RMS_EPS)
            nrm = oh * rs
            gate, dgate = _silu_and_grad(za)
            dza_ref[:, sl] = (dy * nrm * nw * dgate).astype(dza_ref.dtype)
            dn = dy * gate
            dnw = dnw + jnp.sum(dn * nrm, axis=0, keepdims=True)
            dnn = dn * nw
            do_ref[:, sl] = (rs * dnn - oh * (rs * rs * rs) * jnp.mean(dnn * oh, -1, keepdims=True)).astype(BF16)
        dnw_ref[...] += dnw

    row = lambda c: pl.BlockSpec((tm, A_WIDTH), lambda i: (i, c))
    wide = jax.ShapeDtypeStruct((t, A_WIDTH), F32)
    return pl.pallas_call(
        body, name=name, grid=(t // tm,),
        in_specs=[row(0), row(0), row(C_ZA // A_WIDTH), pl.BlockSpec((1, LANE), lambda i: (0, 0))],
        out_specs=[row(0), row(C_ZA // A_WIDTH), pl.BlockSpec((1, LANE), lambda i: (0, 0))],
        out_shape=[jax.ShapeDtypeStruct((t, A_WIDTH), BF16), jax.ShapeDtypeStruct((t, DH_MAIN), BF16),
                   jax.ShapeDtypeStruct((1, LANE), F32)],
        compiler_params=_cp("arbitrary"))(dm, oa, h, norm_w)


def _dn_scan_bwd(q, k, w, qk, bg, do, *, name):
    t = q.shape[0]
    rows = SCAN_ROWS
    per = rows // CHUNK
    n = t // rows

    def body(q_ref, k_ref, w_ref, qk_ref, bg_ref, do_ref, dvn_ref, ds_ref, dstate):
        @pl.when(pl.program_id(0) == 0)
        def _():
            dstate[...] = jnp.zeros_like(dstate)

        heads = range(A_HEADS)
        sl = lambda hd: slice(hd * LANE, (hd + 1) * LANE)
        ds_cur = [dstate[hd] for hd in heads]
        for c in reversed(range(per)):
            rs = slice(c * CHUNK, (c + 1) * CHUNK)
            bg_v = bg_ref[rs, :]
            gcols = [_chunk_gates(bg_v, None, hd)[1] for hd in heads]
            glasts = [gc[CHUNK - 1:CHUNK, :] for gc in gcols]
            for hd in heads:
                ds_ref[c, hd] = ds_cur[hd].astype(BF16)
            pdo = [_dot_tn(qk_ref[rs, hd * CHUNK:(hd + 1) * CHUNK], do_ref[rs, sl(hd)]) for hd in heads]
            qdo = [_dot_tn(q_ref[rs, sl(hd)] * jnp.exp(gcols[hd]), do_ref[rs, sl(hd)]) for hd in heads]
            dvns = [pdo[hd] + _dot(k_ref[rs, sl(hd)] * jnp.exp(glasts[hd] - gcols[hd]), ds_cur[hd]) for hd in heads]
            ds_cur = [qdo[hd] + jnp.exp(glasts[hd]) * ds_cur[hd] - _dot_tn(w_ref[rs, sl(hd)], dvns[hd])
                      for hd in heads]
            for hd in heads:
                dvn_ref[rs, sl(hd)] = dvns[hd].astype(BF16)
        for hd in heads:
            dstate[hd] = ds_cur[hd]

    blk = pl.BlockSpec((rows, A_WIDTH), lambda i: (n - 1 - i, 0))
    return pl.pallas_call(
        body, name=name, grid=(n,),
        in_specs=[blk, blk, blk, pl.BlockSpec((rows, A_HEADS * CHUNK), lambda i: (n - 1 - i, 0)),
                  pl.BlockSpec((rows, LANE), lambda i: (n - 1 - i, 0)), blk],
        out_specs=[blk, pl.BlockSpec((per, A_HEADS, LANE, LANE), lambda i: (n - 1 - i, 0, 0, 0))],
        out_shape=[jax.ShapeDtypeStruct((t, A_WIDTH), BF16),
                   jax.ShapeDtypeStruct((t // CHUNK, A_HEADS, LANE, LANE), BF16)],
        scratch_shapes=[pltpu.VMEM((A_HEADS, LANE, LANE), F32)],
        compiler_params=_cp("arbitrary"))(q, k, w, qk, bg, do)


def _dn_chunk_bwd(q, k, v, vn, tmat, qk, bg, bgt, s_all, ds_all, dvn, do, *, name, carry=None):
    t = q.shape[0]
    rows = WY_ROWS
    per = rows // CHUNK

    c_ins, c_in_specs, c_out_specs, c_outs, c_scratch = _carry_specs(carry)

    def body(*refs):
        (q_ref, k_ref, v_ref, vn_ref, tm_ref, qk_ref, bg_ref, bgt_ref, s_ref, ds_ref, dvn_ref, do_ref,
         dq_ref, dk_ref, dv_ref, dbg_ref, dbgt_ref) = _carried(carry, refs, 12, 5, t // rows)
        causal, strict, _ = _chunk_masks()
        lane = lax.broadcasted_iota(jnp.int32, (CHUNK, LANE), 1)
        rowi = lax.broadcasted_iota(jnp.int32, (CHUNK, 1), 0)
        sub = lax.broadcasted_iota(jnp.int32, (SUBLANE, CHUNK), 0)
        rs = lambda c: slice(c * CHUNK, (c + 1) * CHUNK)
        sl = lambda hd: slice(hd * LANE, (hd + 1) * LANE)
        hs = lambda hd: slice(hd * CHUNK, (hd + 1) * CHUNK)
        for c0 in range(0, per, WY_GROUP):
            items = [(c, hd) for c in range(c0, c0 + WY_GROUP) for hd in range(A_HEADS)]
            at = lambda ref: [ref[rs(c), sl(hd)] for c, hd in items]
            qs, ks, vs, dos, vns, dvns = at(q_ref), at(k_ref), at(v_ref), at(do_ref), at(vn_ref), at(dvn_ref)
            tmhs = [tm_ref[rs(c), hs(hd)] for c, hd in items]
            ps = [qk_ref[rs(c), hs(hd)] for c, hd in items]
            gates = [_chunk_gates(bg_ref[rs(c), :], bgt_ref[:, rs(c)], hd) for c, hd in items]
            betas = [g[0] for g in gates]
            gcols = [g[1] for g in gates]
            dmats = [jnp.exp(jnp.where(causal, g[1] - g[2], NEG)) for g in gates]
            es = [jnp.exp(gc) for gc in gcols]
            glasts = [gc[CHUNK - 1:CHUNK, :] for gc in gcols]
            eks = [jnp.exp(gl - gc) for gl, gc in zip(glasts, gcols)]
            kbs = [kh * b for kh, b in zip(ks, betas)]
            vbs = [vh * b for vh, b in zip(vs, betas)]
            kbes = [kb * e for kb, e in zip(kbs, es)]

            a_s = [jnp.where(strict, _dot_nt(kb, kh) * dm, 0.0) for kb, kh, dm in zip(kbs, ks, dmats)]
            dps = [jnp.where(causal, _dot_nt(doh, vnh), 0.0) for doh, vnh in zip(dos, vns)]
            rows2 = lambda a, b: jnp.concatenate([a, b], axis=0)
            cols2 = lambda a, b: jnp.concatenate([a, b], axis=1)
            by_s = [_dot_nt(rows2(doh, dvnh), s_ref[c, hd]) for doh, dvnh, (c, hd) in zip(dos, dvns, items)]
            dqds = [m[:CHUNK] for m in by_s]
            dws = [-m[CHUNK:] for m in by_s]
            dkds = [_dot_nt(vnh, ds_ref[c, hd]) for vnh, (c, hd) in zip(vns, items)]
            dgts = [jnp.sum(s_ref[c, hd].astype(F32) * ds_ref[c, hd].astype(F32), keepdims=True) for c, hd in items]
            pairs = [cols2(dvnh, dw) for dvnh, dw in zip(dvns, dws)]
            by_t = [_dot_tn(tmh, pr) for tmh, pr in zip(tmhs, pairs)]
            dvbs = [m[:, :LANE] for m in by_t]
            dkbes = [m[:, LANE:] for m in by_t]
            dts = [_dot_nt(pr, cols2(vb, kbe)) for pr, vb, kbe in zip(pairs, vbs, kbes)]
            xs = [_dot_nt(dt, tmh) for dt, tmh in zip(dts, tmhs)]
            das = [jnp.where(strict, -_dot_tn(tmh, x), 0.0) for tmh, x in zip(tmhs, xs)]
            dmas = [da * dm for da, dm in zip(das, dmats)]
            dmps = [dp * dm for dp, dm in zip(dps, dmats)]
            stacked = [rows2(dma, dmp) for dma, dmp in zip(dmas, dmps)]
            by_k = [_dot(st, kh) for st, kh in zip(stacked, ks)]
            dkbs = [m[:CHUNK] + dkbe * e for m, dkbe, e in zip(by_k, dkbes, es)]
            for i, (c, hd) in enumerate(items):
                dq_ref[rs(c), sl(hd)] = by_k[i][CHUNK:] + dqds[i] * es[i]
                dk_ref[rs(c), sl(hd)] = (_dot_tn(stacked[i], rows2(kbs[i], qs[i])) + dkds[i] * eks[i]
                                         + dkbs[i] * betas[i])
                dv_ref[rs(c), sl(hd)] = dvbs[i] * betas[i]
            for c in range(c0, c0 + WY_GROUP):
                acc = jnp.zeros((CHUNK, LANE), F32)
                acc_t = jnp.zeros((SUBLANE, CHUNK), F32)
                for i, (ci, hd) in enumerate(items):
                    if ci != c:
                        continue
                    gmat = das[i] * a_s[i] + dps[i] * ps[i]
                    rk = jnp.sum(dkds[i] * ks[i], -1, keepdims=True) * eks[i]
                    de = jnp.sum(dqds[i] * qs[i] + dkbes[i] * kbs[i], -1, keepdims=True)
                    dglast = jnp.sum(rk, keepdims=True) + dgts[i] * jnp.exp(glasts[i])
                    dgc = (jnp.sum(gmat, -1, keepdims=True) + de * es[i] - rk
                           + jnp.where(rowi == CHUNK - 1, dglast, 0.0))
                    dbeta = jnp.sum(dkbs[i] * ks[i] + dvbs[i] * vs[i], -1, keepdims=True)
                    acc = acc + jnp.where(lane == hd, dbeta, 0.0) + jnp.where(lane == A_HEADS + hd, dgc, 0.0)
                    acc_t = acc_t + jnp.where(sub == A_HEADS + hd, -jnp.sum(gmat, axis=0, keepdims=True), 0.0)
                dbg_ref[rs(c), :] = acc
                dbgt_ref[:, rs(c)] = acc_t

    blk = pl.BlockSpec((rows, A_WIDTH), lambda i: (i, 0))
    half = pl.BlockSpec((rows, A_HEADS * CHUNK), lambda i: (i, 0))
    col = pl.BlockSpec((rows, LANE), lambda i: (i, 0))
    rowf = pl.BlockSpec((SUBLANE, rows), lambda i: (0, i))
    st = pl.BlockSpec((per, A_HEADS, LANE, LANE), lambda i: (i, 0, 0, 0))
    wide = jax.ShapeDtypeStruct((t, A_WIDTH), F32)
    outs = pl.pallas_call(
        body, name=name, grid=(t // rows,),
        in_specs=[blk, blk, blk, blk, half, half, col, rowf, st, st, blk, blk] + c_in_specs,
        out_specs=[blk, blk, blk, col, rowf] + c_out_specs,
        out_shape=[wide, wide, wide, jax.ShapeDtypeStruct((t, LANE), F32),
                   jax.ShapeDtypeStruct((SUBLANE, t), F32)] + c_outs,
        scratch_shapes=c_scratch,
        compiler_params=_cp("arbitrary"))(q, k, v, vn, tmat, qk, bg, bgt, s_all, ds_all, dvn, do, *c_ins)
    return outs[:5], outs[5:]


def _dn_pre_bwd(h, conv, par, dq, dk, dv, dbg, dbgt, *, tt, name):
    t = h.shape[0]
    cw = 3 * A_WIDTH

    def body(conv_ref, bgi_ref, par_ref, dq_ref, dk_ref, dv_ref, dbg_ref, dbgt_ref, dc_ref, dbgi_ref, dpar_ref):
        i = pl.program_id(0)

        @pl.when(i == 0)
        def _():
            dpar_ref[...] = jnp.zeros_like(dpar_ref)

        s, ds = _silu_and_grad(conv_ref[...])
        for hd in range(A_HEADS):
            sl = slice(hd * LANE, (hd + 1) * LANE)
            for base, d_ref, scale in ((0, dq_ref, A_HEAD_DIM ** -0.5), (A_WIDTH, dk_ref, 1.0)):
                csl = slice(base + hd * LANE, base + (hd + 1) * LANE)
                tq = s[:, base + hd * LANE:base + (hd + 1) * LANE]
                dy = d_ref[:, sl]
                rq = lax.rsqrt(jnp.sum(tq * tq, -1, keepdims=True) + L2_EPS)
                dtq = scale * (rq * dy - tq * (rq * rq * rq) * jnp.sum(dy * tq, -1, keepdims=True))
                dc_ref[:, csl] = dtq * ds[:, base + hd * LANE:base + (hd + 1) * LANE]
        dc_ref[:, 2 * A_WIDTH:] = dv_ref[...] * ds[:, 2 * A_WIDTH:]
        raw = bgi_ref[...].astype(F32)
        lane = lax.broadcasted_iota(jnp.int32, raw.shape, 1)
        is_b = lane < A_HEADS
        is_a = (lane >= A_HEADS) & (lane < 2 * A_HEADS)
        rows_t = jnp.concatenate([dbgt_ref[...], jnp.zeros((LANE - SUBLANE, tt), F32)], axis=0)
        dbg_v = dbg_ref[...] + jnp.where(is_a, jnp.transpose(rows_t), 0.0)
        dbg_v = jnp.where(is_a, _dot_hi(_chunk_tri(tt, lower=False), jnp.where(is_a, dbg_v, 0.0)), dbg_v)
        beta = _sigmoid(raw)
        z = raw + par_ref[1:2, :]
        neg_ea = -jnp.exp(par_ref[0:1, :])
        g = neg_ea * _softplus(z)
        da = dbg_v * neg_ea * _sigmoid(z)
        dbgi_ref[...] = jnp.where(is_b, dbg_v * beta * (1.0 - beta), jnp.where(is_a, da, 0.0))
        dpar_ref[0:1, :] += jnp.sum(jnp.where(is_a, dbg_v * g, 0.0), axis=0, keepdims=True)
        dpar_ref[1:2, :] += jnp.sum(jnp.where(is_a, da, 0.0), axis=0, keepdims=True)

    wide = pl.BlockSpec((tt, A_WIDTH), lambda i: (i, 0))
    return pl.pallas_call(
        body, name=name, grid=(t // tt,),
        in_specs=[pl.BlockSpec((tt, cw), lambda i: (i, 0)),
                  pl.BlockSpec((tt, LANE), lambda i: (i, C_BG // LANE)),
                  pl.BlockSpec((SUBLANE, LANE), lambda i: (0, 0)),
                  wide, wide, wide, pl.BlockSpec((tt, LANE), lambda i: (i, 0)),
                  pl.BlockSpec((SUBLANE, tt), lambda i: (0, i))],
        out_specs=[pl.BlockSpec((tt, cw), lambda i: (i, 0)), pl.BlockSpec((tt, LANE), lambda i: (i, 0)),
                   pl.BlockSpec((SUBLANE, LANE), lambda i: (0, 0))],
        out_shape=[jax.ShapeDtypeStruct((t, cw), F32), jax.ShapeDtypeStruct((t, LANE), F32),
                   jax.ShapeDtypeStruct((SUBLANE, LANE), F32)],
        compiler_params=_cp("arbitrary"))(conv, h, par, dq, dk, dv, dbg, dbgt)


def _conv_bwd(dc, h, conv_w, dh, *, tt, name):
    t = dc.shape[0]
    cw = 3 * A_WIDTH
    nb = t // tt

    def body(dc_ref, after_ref, pre_ref, cw_ref, dh_in_ref, dpre_ref, dcw_ref):
        i = pl.program_id(0)

        @pl.when(i == 0)
        def _():
            dcw_ref[...] = jnp.zeros_like(dcw_ref)

        dcv = dc_ref[...]
        after = jnp.where(i < nb - 1, after_ref[...], 0.0)
        cur = pre_ref[...].astype(F32)
        w = cw_ref[...]
        acc = dcv * w[CONV_K - 1:CONV_K, :]
        dcw_ref[CONV_K - 1:CONV_K, :] += jnp.sum(dcv * cur, axis=0, keepdims=True)
        for s in range(1, CONV_K):
            j = CONV_K - 1 - s
            up = _shift_up(dcv, after, s)
            acc = acc + up * w[j:j + 1, :]
            dcw_ref[j:j + 1, :] += jnp.sum(up * cur, axis=0, keepdims=True)
        dpre_ref[...] = acc.astype(dpre_ref.dtype)

    return pl.pallas_call(
        body, name=name, grid=(nb,),
        in_specs=[pl.BlockSpec((tt, cw), lambda i: (i, 0)),
                  pl.BlockSpec((SUBLANE, cw), lambda i: (jnp.minimum((i + 1) * (tt // SUBLANE), t // SUBLANE - 1), 0)),
                  pl.BlockSpec((tt, cw), lambda i: (i, 0)),
                  pl.BlockSpec((CONV_K, cw), lambda i: (0, 0)), _ANY],
        out_specs=[pl.BlockSpec((tt, cw), lambda i: (i, 0)), pl.BlockSpec((SUBLANE, cw), lambda i: (0, 0))],
        out_shape=[jax.ShapeDtypeStruct(dh.shape, dh.dtype), jax.ShapeDtypeStruct((SUBLANE, cw), F32)],
        input_output_aliases={4: 0},
        compiler_params=_cp("arbitrary"))(dc, dc, h, conv_w, dh)


def _swa_bwd(h, dm, ob, probs, sink_probs, dh, *, name, carry=None):
    t = h.shape[0]
    qspec, cur, prev = _swa_specs()
    c_ins, c_in_specs, c_out_specs, c_outs, c_scratch = _carry_specs(carry)

    def body(*refs):
        (q_ref, kc_ref, kp_ref, vc_ref, vp_ref, zb_ref, dy_ref, ob_ref, p_ref, ps_ref, dh_in_ref,
         dqz_ref, dk_ref, dv_ref, dsk_ref) = _carried(carry, refs, 11, 4, t // BLOCK)
        n_blk = pl.program_id(0)

        @pl.when(n_blk == 0)
        def _():
            dk_ref[...] = jnp.zeros_like(dk_ref)
            dv_ref[...] = jnp.zeros_like(dv_ref)
            dsk_ref[...] = jnp.zeros_like(dsk_ref)

        kp, kc, vp, vc = kp_ref[...], kc_ref[...], vp_ref[...], vc_ref[...]
        scale = B_HEAD_DIM ** -0.5
        hks = range(B_KV_HEADS)
        ksl = lambda hk: slice(hk * B_HEAD_DIM, (hk + 1) * B_HEAD_DIM)
        heads = lambda hk: range(hk * B_GROUP, (hk + 1) * B_GROUP)
        upper, _ = _swa_window()
        groups = [(_stack_heads(q_ref, hk) * scale,
                   jnp.concatenate([p_ref[:, h * BLOCK:(h + 1) * BLOCK].astype(F32) for h in heads(hk)], axis=0),
                   jnp.concatenate([ps_ref[:, h:h + 1] for h in heads(hk)], axis=0),
                   _stack_heads(ob_ref, hk)) for hk in hks]
        zbs = [_stack_heads(zb_ref, hk) for hk in hks]
        dys = [_stack_heads(dy_ref, hk) for hk in hks]
        gates = [_silu_and_grad(zbs[hk]) for hk in hks]
        dos = [dys[hk] * gates[hk][0] for hk in hks]
        deltas = [jnp.sum(dos[hk] * groups[hk][3], -1, keepdims=True) for hk in hks]
        dps = [jnp.where(upper, _dot_nt(dos[hk], vp[:, ksl(hk)]), _dot_nt(dos[hk], vc[:, ksl(hk)])) for hk in hks]
        dss = [groups[hk][1] * (dps[hk] - deltas[hk]) for hk in hks]
        ds_up = [jnp.where(upper, dss[hk], 0.0) for hk in hks]
        ds_lo = [dss[hk] - ds_up[hk] for hk in hks]
        p_up = [jnp.where(upper, groups[hk][1], 0.0) for hk in hks]
        p_lo = [groups[hk][1] - p_up[hk] for hk in hks]
        dqs = [(_dot(ds_up[hk], kp[:, ksl(hk)]) + _dot(ds_lo[hk], kc[:, ksl(hk)])) * scale for hk in hks]
        dk_prev = [_dot_tn(ds_up[hk], groups[hk][0]) for hk in hks]
        dk_cur = [_dot_tn(ds_lo[hk], groups[hk][0]) for hk in hks]
        dv_prev = [_dot_tn(p_up[hk], dos[hk]) for hk in hks]
        dv_cur = [_dot_tn(p_lo[hk], dos[hk]) for hk in hks]
        for hk in hks:
            dzb = dys[hk] * groups[hk][3] * gates[hk][1]
            dsink = groups[hk][2] * deltas[hk]
            for g in range(B_GROUP):
                hq = hk * B_GROUP + g
                rows = slice(g * BLOCK, (g + 1) * BLOCK)
                qsl = slice(hq * B_HEAD_DIM, (hq + 1) * B_HEAD_DIM)
                dqz_ref[:, qsl] = dqs[hk][rows].astype(dqz_ref.dtype)
                dqz_ref[:, B_WIDTH + hq * B_HEAD_DIM:B_WIDTH + (hq + 1) * B_HEAD_DIM] = dzb[rows].astype(dqz_ref.dtype)
                dsk_ref[hq:hq + 1, :] += -jnp.sum(dsink[rows], keepdims=True)
        at_cur = pl.ds(pl.multiple_of(n_blk * BLOCK, BLOCK), BLOCK)
        at_prev = pl.ds(pl.multiple_of(jnp.maximum(n_blk - 1, 0) * BLOCK, BLOCK), BLOCK)
        dk_ref[at_prev, :] += jnp.concatenate(dk_prev, axis=1)
        dv_ref[at_prev, :] += jnp.concatenate(dv_prev, axis=1)
        dk_ref[at_cur, :] += jnp.concatenate(dk_cur, axis=1)
        dv_ref[at_cur, :] += jnp.concatenate(dv_cur, axis=1)

    narrow = jax.ShapeDtypeStruct((t, B_KV_WIDTH), F32)
    res = lambda a, b: pl.BlockSpec((a, b), lambda i: (0, 0))
    row = lambda w: pl.BlockSpec((BLOCK, w), lambda i: (i, 0))
    outs = pl.pallas_call(
        body, name=name, grid=(t // BLOCK,),
        in_specs=[qspec(C_QB), cur(C_KB), prev(C_KB), cur(C_VB), prev(C_VB), qspec(C_ZB),
                  pl.BlockSpec((BLOCK, B_WIDTH), lambda i: (i, 1)), row(B_WIDTH), row(B_Q_HEADS * BLOCK), row(LANE),
                  _ANY] + c_in_specs,
        out_specs=[pl.BlockSpec((BLOCK, 2 * B_WIDTH), lambda i: (i, C_QB // (2 * B_WIDTH))),
                   res(t, B_KV_WIDTH), res(t, B_KV_WIDTH), res(B_Q_HEADS, LANE)] + c_out_specs,
        out_shape=[jax.ShapeDtypeStruct(dh.shape, dh.dtype), narrow, narrow,
                   jax.ShapeDtypeStruct((B_Q_HEADS, LANE), F32)] + c_outs,
        scratch_shapes=c_scratch,
        input_output_aliases={10: 0},
        compiler_params=_cp("arbitrary"))(h, h, h, h, h, h, dm, ob, probs, sink_probs, dh, *c_ins)
    return outs[:4], outs[4:]


def _in_proj_dw(dh_main, dh_tail, x, *, tk, name):
    t, n = x.shape

    def body(a_ref, t_ref, x_ref, o_ref, ot_ref):
        @pl.when(pl.program_id(0) == 0)
        def _():
            o_ref[...] = jnp.zeros_like(o_ref)
            ot_ref[...] = jnp.zeros_like(ot_ref)

        xb = x_ref[...].astype(BF16)
        o_ref[...] += _dot_tn(a_ref[...], xb)
        ot_ref[...] += _dot_tn(t_ref[...], xb)

    row = lambda a: pl.BlockSpec((tk, a.shape[1]), lambda kk: (kk, 0))
    acc = lambda a: pl.BlockSpec((a.shape[1], n), lambda kk: (0, 0))
    return pl.pallas_call(
        body, name=name, grid=(t // tk,), in_specs=[row(dh_main), row(dh_tail), row(x)],
        out_specs=[acc(dh_main), acc(dh_tail)],
        out_shape=[jax.ShapeDtypeStruct((a.shape[1], n), F32) for a in (dh_main, dh_tail)],
        compiler_params=_cp("arbitrary"))(dh_main, dh_tail, x)


def _in_proj_dx(dh_main, dh_tail, wt, dr, *, tm, name, carry=None):
    t, n_main = dh_main.shape
    n_tail = dh_tail.shape[1]
    c_ins, c_in_specs, c_out_specs, c_outs, c_scratch = _carry_specs(carry)

    def body(*refs):
        a_ref, t_ref, wa_ref, wt_ref, r_ref, o_ref = _carried(carry, refs, 5, 1, t // tm)
        o_ref[...] = _dot(a_ref[...], wa_ref[...]) + _dot(t_ref[...], wt_ref[...]) + DEEPNORM_ALPHA * r_ref[...]

    row = lambda w: pl.BlockSpec((tm, w), lambda i: (i, 0))
    outs = pl.pallas_call(
        body, name=name, grid=(t // tm,),
        in_specs=[row(n_main), row(n_tail), pl.BlockSpec((n_main, D_MODEL), lambda i: (0, 0)),
                  pl.BlockSpec((n_tail, D_MODEL), lambda i: (n_main // n_tail, 0)), row(D_MODEL)] + c_in_specs,
        out_specs=[row(D_MODEL)] + c_out_specs,
        out_shape=[jax.ShapeDtypeStruct((t, D_MODEL), F32)] + c_outs,
        scratch_shapes=c_scratch,
        compiler_params=_cp("arbitrary"))(dh_main, dh_tail, wt, wt, dr, *c_ins)
    return outs[0], outs[1:]


def _layer_bwd(dxn, res, wt, conv_w, par, sinks_b, norm_w, w_out_bf, ln_g, l, carries=None, carry_dx=None):
    carries = carries or {}
    w_out_bf = res["w_out"]
    dr, dm, dw_out, dln_g, dln_b = _ln_out_bwd(dxn, res["r"], res["mixed"], ln_g, w_out_bf, tm=512, name=f"ln_out_bwd_{l}")
    h = res["h"]
    do, dh, dnw = _dn_post_bwd(dm, res["oa"], h, norm_w, tm=512, name=f"dn_post_bwd_{l}")
    dvn, ds_all = _dn_scan_bwd(res["q"], res["k"], res["w"], res["qk"], res["bg"], do, name=f"dn_scan_bwd_{l}")
    (dq, dk, dv, dbg, dbgt), got_chunk = _dn_chunk_bwd(
        res["q"], res["k"], res["v"], res["vn"], res["tmat"], res["qk"], res["bg"], res["bgt"], res["s_all"], ds_all,
        dvn, do, name=f"dn_chunk_bwd_{l}", carry=carries.get("dn_chunk"))
    dc, dbgi, dpar = _dn_pre_bwd(h, res["conv"], par, dq, dk, dv, dbg, dbgt, tt=512, name=f"dn_pre_bwd_{l}")
    dh, dcw = _conv_bwd(dc, h, conv_w, dh, tt=512, name=f"conv_bwd_{l}")
    (dh, dkb, dvb, dsk), got_swa = _swa_bwd(h, dm, res["ob"], res["swa_p"], res["swa_ps"], dh, name=f"swa_bwd_{l}",
                                            carry=carries.get("swa"))
    carried = dict(dn_chunk=got_chunk, swa=got_swa)
    dh_tail = jnp.concatenate([dkb, dvb, dbgi], axis=1).astype(BF16)
    dwt_main, dwt_tail = _in_proj_dw(dh, dh_tail, res["x"], tk=512, name=f"in_proj_dw_{l}")
    grads = dict(w_in=(dwt_main, dwt_tail), conv_w=dcw[:CONV_K], a_log=dpar[0, A_HEADS:2 * A_HEADS],
                 dt_bias=dpar[1, A_HEADS:2 * A_HEADS], norm_w=dnw[0], sinks=dsk[:, 0], w_out=dw_out,
                 ln_g=dln_g[0], ln_b=dln_b[0])
    dx, carried_dx = _in_proj_dx(dh, dh_tail, wt, dr, tm=512, name=f"in_proj_dx_{l}",
                                 carry=None if carry_dx is None else carry_dx(grads))
    return dx, grads, carried, carried_dx


def _layer_args(wt, conv_w, a_log, dt_bias, sinks, norm_w, w_out_bf):
    return (wt, conv_w, _gate_params(a_log, dt_bias), jnp.broadcast_to(sinks[:, None], (B_Q_HEADS, LANE)),
            norm_w[None], w_out_bf)


def _local_step(x, target, args0, args1, ln_g, ln_b, gathers=None, reduce1=None, reduce0=None):
    assert DEPTH == 2
    x1, res0, got = _layer_fwd(x, *args0, ln_g[0][None], ln_b[0][None], 0, carries=gathers)
    if gathers is not None:
        args1 = args1(got)
    (dx, loss_tile), res1, _ = _layer_fwd(x1, *args1, ln_g[1][None], ln_b[1][None], 1, target=target)
    dx, grads1, _, _ = _layer_bwd(dx, res1, *args1, ln_g[1][None], 1)
    carries = None if reduce1 is None else reduce1(grads1)
    carry_dx = None if reduce0 is None else (lambda grads0: reduce0(grads0, grads1, loss_tile))
    dx, grads0, landed1, landed0 = _layer_bwd(dx, res0, *args0, ln_g[0][None], 0, carries=carries, carry_dx=carry_dx)
    return loss_tile, dx, [grads0, grads1], landed1, landed0


_ANY = pl.BlockSpec(memory_space=pl.ANY)
_MESH = pl.DeviceIdType.MESH


HALF = D_MODEL // 2


class _Exchange:
    def __init__(self, ins, outs, n_remote, n_local, plan):
        self.ins, self.outs, self.n_remote, self.n_local, self.plan = tuple(ins), tuple(outs), n_remote, n_local, plan

    def scratch(self):
        return [pltpu.SemaphoreType.DMA((self.n_remote,)), pltpu.SemaphoreType.DMA((self.n_remote,)),
                pltpu.SemaphoreType.DMA((max(self.n_local, 1),))]

    def _copies(self, in_refs, out_refs, sems, arriving):
        send_sems, recv_sems, local_sems = sems
        local, sends, recvs = self.plan(in_refs, out_refs)
        loc = [pltpu.make_async_copy(s, d, local_sems.at[i]) for i, (s, d) in enumerate(local)]
        rem = [pltpu.make_async_remote_copy(src_ref=s, dst_ref=recvs[i] if arriving else d, send_sem=send_sems.at[i],
                                            recv_sem=recv_sems.at[i], device_id=peer, device_id_type=_MESH)
               for i, (s, d, peer) in enumerate(sends)]
        return loc, rem

    def start(self, in_refs, out_refs, sems):
        loc, rem = self._copies(in_refs, out_refs, sems, arriving=False)
        for cp in loc + rem:
            cp.start()

    def finish(self, in_refs, out_refs, sems):
        loc, rem = self._copies(in_refs, out_refs, sems, arriving=True)
        for cp in rem:
            cp.wait_recv()
        for cp in rem:
            cp.wait_send()
        for cp in loc:
            cp.wait()


def _run_exchange(ex, *, name):
    n_in, n_out = len(ex.ins), len(ex.outs)

    def body(*refs):
        parts = refs[:n_in], refs[n_in:n_in + n_out], refs[n_in + n_out:]
        ex.start(*parts)
        ex.finish(*parts)

    return pl.pallas_call(body, name=name, in_specs=[_ANY] * n_in, out_specs=[_ANY] * n_out, out_shape=list(ex.outs),
                          scratch_shapes=ex.scratch())(*ex.ins)


def _place():
    x, y, c = lax.axis_index("x"), lax.axis_index("y"), lax.axis_index("c")
    return x, y, c, [(1 - x, y), (x, 1 - y), (1 - x, 1 - y)]


def _gather_exchange(arrays):
    n = len(arrays)

    def plan(src, dst):
        x, y, c, chips = _place()
        me = 2 * x + y
        local = [(src[k], dst[k].at[me]) for k in range(n)]
        sends = [(src[k], dst[k].at[me], (px, py, c)) for k in range(n) for px, py in chips]
        recvs = [dst[k].at[2 * px + py] for k in range(n) for px, py in chips]
        return local, sends, recvs

    return _Exchange(arrays, [jax.ShapeDtypeStruct((N_SHARD,) + a.shape, a.dtype) for a in arrays], 3 * n, n, plan)


def _gather_two_level(pack, conv_w, *, name):
    rows = pack.shape[0]
    part_rows = rows // 2

    def body(pack_ref, conv_ref, land_ref, conv_land_ref, send1, recv1, send2, recv2, csend, crecv, local_sems):
        x, y, c, chips = _place()
        me = 2 * x + y
        sibling = (x, y, 1 - c)
        part = lambda core: pl.ds(pl.multiple_of(core * part_rows, 16), part_rows)
        remote = lambda src, dst, ss, rs, to: pltpu.make_async_remote_copy(
            src_ref=src, dst_ref=dst, send_sem=ss, recv_sem=rs, device_id=to, device_id_type=_MESH)
        local = [pltpu.make_async_copy(pack_ref, land_ref.at[me], local_sems.at[0]),
                 pltpu.make_async_copy(conv_ref, conv_land_ref.at[me], local_sems.at[1])]
        for cp in local:
            cp.start()
        first = [remote(pack_ref.at[part(c)], land_ref.at[me, part(c)], send1.at[j], recv1.at[j], (px, py, c))
                 for j, (px, py) in enumerate(chips)]
        convs = [remote(conv_ref, conv_land_ref.at[me], csend.at[j], crecv.at[j], (px, py, c))
                 for j, (px, py) in enumerate(chips)]
        for cp in first + convs:
            cp.start()
        passed = []
        for j, (px, py) in enumerate(chips):
            slot = 2 * px + py
            remote(pack_ref.at[part(c)], land_ref.at[slot, part(c)], send1.at[j], recv1.at[j], (px, py, c)).wait_recv()
            cp = remote(land_ref.at[slot, part(c)], land_ref.at[slot, part(c)], send2.at[j], recv2.at[j], sibling)
            cp.start()
            passed.append(cp)
        for j, (px, py) in enumerate(chips):
            slot = 2 * px + py
            remote(land_ref.at[slot, part(1 - c)], land_ref.at[slot, part(1 - c)], send2.at[j], recv2.at[j],
                   sibling).wait_recv()
            remote(conv_ref, conv_land_ref.at[slot], csend.at[j], crecv.at[j], (px, py, c)).wait_recv()
        for cp in first + convs + passed:
            cp.wait_send()
        for cp in local:
            cp.wait()

    sems = [pltpu.SemaphoreType.DMA((3,))] * 6 + [pltpu.SemaphoreType.DMA((2,))]
    return pl.pallas_call(
        body, name=name, in_specs=[_ANY, _ANY], out_specs=[_ANY, _ANY],
        out_shape=[jax.ShapeDtypeStruct((N_SHARD,) + pack.shape, pack.dtype),
                   jax.ShapeDtypeStruct((N_SHARD,) + conv_w.shape, conv_w.dtype)],
        scratch_shapes=sems)(pack, conv_w)


def _half(core):
    return pl.ds(pl.multiple_of(core * HALF, HALF), HALF)


def _reduce_scatter_exchange(g, row0, rows):
    def plan(src, dst):
        x, y, c, chips = _place()
        peers = [(px, py, c if t == 0 else 1 - c) for px, py in chips for t in (0, 1)] + [(x, y, 1 - c)]
        sends = [(src[0].at[2 * px + py, pl.ds(row0, rows), _half(pc)], dst[0].at[k], (px, py, pc))
                 for k, (px, py, pc) in enumerate(peers)]
        return [], sends, [dst[0].at[k] for k in range(7)]

    return _Exchange([g], [jax.ShapeDtypeStruct((7, rows, HALF), g.dtype)], 7, 0, plan)


def _pair_window_exchange(g):
    def plan(src, dst):
        x, y, c, _ = _place()
        return [], [(src[0].at[:, :, _half(1 - c)], dst[0], (x, y, 1 - c))], [dst[0]]

    return _Exchange([g], [jax.ShapeDtypeStruct(g.shape[:2] + (HALF,), g.dtype)], 1, 0, plan)


def _chip_scatter_exchange(p, small):
    def plan(src, dst):
        x, y, c, chips = _place()
        mine = 4 * x + 2 * y + c
        peers = [(px, py, c if t == 0 else 1 - c) for px, py in chips for t in (0, 1)] + [(x, y, 1 - c)]
        sends = [(src[0].at[2 * px + py], dst[0].at[j], (px, py, c)) for j, (px, py) in enumerate(chips)]
        recvs = [dst[0].at[j] for j in range(3)]
        sends += [(src[1], dst[1].at[mine], peer) for peer in peers]
        recvs += [dst[1].at[4 * px + 2 * py + pc] for px, py, pc in peers]
        return [(src[1], dst[1].at[mine])], sends, recvs

    outs = [jax.ShapeDtypeStruct((3,) + p.shape[1:], p.dtype), jax.ShapeDtypeStruct((8,) + small.shape, small.dtype)]
    return _Exchange([p, small], outs, 10, 1, plan)


def _share_exchange(arrays):
    n = len(arrays)

    def plan(src, dst):
        x, y, c, _ = _place()
        return [], [(src[k], dst[k], (x, y, 1 - c)) for k in range(n)], [dst[k] for k in range(n)]

    return _Exchange(arrays, [jax.ShapeDtypeStruct(a.shape, a.dtype) for a in arrays], n, 0, plan)


def _sum_scatter(g, lands, me, core, *, tc, name):
    rows = g.shape[1]
    per = HALF // tc
    n = len(lands)

    def body(*refs):
        g_ref, land_refs, o_ref = refs[1], refs[2:2 + n], refs[2 + n]
        at = 0
        for land_ref in land_refs:
            run = slice(at, at + land_ref.shape[1])
            acc = g_ref[run, :].astype(F32)
            for k in range(7):
                acc = acc + land_ref[k].astype(F32)
            o_ref[run, :] = acc
            at = run.stop

    return pl.pallas_call(
        body, name=name, out_shape=jax.ShapeDtypeStruct((rows, HALF), F32), compiler_params=_cp("parallel"),
        grid_spec=pltpu.PrefetchScalarGridSpec(
            num_scalar_prefetch=1, grid=(per,),
            in_specs=[pl.BlockSpec((None, rows, tc), lambda i, w: (w[0], 0, w[1] * per + i))]
            + [pl.BlockSpec((7, a.shape[1], tc), lambda i, w: (0, 0, i)) for a in lands],
            out_specs=pl.BlockSpec((rows, tc), lambda i, w: (0, i))))(
        jnp.stack([me, core]).astype(jnp.int32), g, *lands)


def _pair_add(g, land, core, *, name):
    n, rows, _ = g.shape

    def body(core_ref, g_ref, land_ref, o_ref):
        o_ref[...] = (g_ref[...].astype(F32) + land_ref[...].astype(F32)).astype(o_ref.dtype)

    blk = pl.BlockSpec((1, rows, HALF), lambda i, w: (i, 0, 0))
    return pl.pallas_call(
        body, name=name, out_shape=jax.ShapeDtypeStruct((n, rows, HALF), g.dtype), compiler_params=_cp("parallel"),
        grid_spec=pltpu.PrefetchScalarGridSpec(
            num_scalar_prefetch=1, grid=(n,),
            in_specs=[pl.BlockSpec((1, rows, HALF), lambda i, w: (i, 0, w[0])), blk], out_specs=blk))(
        jnp.reshape(core, (1,)).astype(jnp.int32), g, land)


def _sum_chips(p, land, me, *, tc, name):
    rows = p.shape[1]

    def body(me_ref, p_ref, land_ref, o_ref):
        acc = p_ref[...].astype(F32)
        for k in range(3):
            acc = acc + land_ref[k].astype(F32)
        o_ref[...] = acc

    return pl.pallas_call(
        body, name=name, out_shape=jax.ShapeDtypeStruct((rows, HALF), F32), compiler_params=_cp("parallel"),
        grid_spec=pltpu.PrefetchScalarGridSpec(
            num_scalar_prefetch=1, grid=(HALF // tc,),
            in_specs=[pl.BlockSpec((None, rows, tc), lambda i, w: (w[0], 0, i)),
                      pl.BlockSpec((3, rows, tc), lambda i, w: (0, 0, i))],
            out_specs=pl.BlockSpec((rows, tc), lambda i, w: (0, i))))(
        jnp.reshape(me, (1,)).astype(jnp.int32), p, land)


def _sum_slots(a, *, name):
    n = a.shape[0]

    def body(a_ref, o_ref):
        acc = a_ref[0]
        for k in range(1, n):
            acc = acc + a_ref[k]
        o_ref[...] = acc

    return pl.pallas_call(body, name=name, out_shape=jax.ShapeDtypeStruct(a.shape[1:], a.dtype))(a)


def _elementwise(fn, ins, n_out, block, *, name):
    shape = ins[0].shape
    grid = tuple(s // b for s, b in zip(shape, block))
    n_in = len(ins)

    def body(*refs):
        outs = fn(*[r[...] for r in refs[:n_in]])
        for o_ref, val in zip(refs[n_in:], outs):
            o_ref[...] = val

    spec = pl.BlockSpec(block, lambda i, j, k: (i, j, k))
    return pl.pallas_call(body, name=name, grid=grid, in_specs=[spec] * n_in, out_specs=[spec] * n_out,
                          out_shape=[jax.ShapeDtypeStruct(shape, F32)] * n_out,
                          compiler_params=_cp(*["parallel"] * 3))(*ins)


def _adamw_math(w, g, m, v):
    mn = ADAM_B1 * m + (1.0 - ADAM_B1) * g
    vn = ADAM_B2 * v + (1.0 - ADAM_B2) * (g * g)
    m_hat = mn / (1.0 - ADAM_B1 ** ADAM_STEP)
    v_hat = vn / (1.0 - ADAM_B2 ** ADAM_STEP)
    return -ADAM_LR * (m_hat / (jnp.sqrt(v_hat) + ADAM_EPS) + ADAM_WD * w), mn, vn


def _adamw(w, g, m, v, block, *, name):
    return _elementwise(_adamw_math, [w, g, m, v], 3, block, name=name)


def _interleave_layers(layers, *, tc, name):
    rows, cols = layers[0].shape
    n = len(layers)

    def body(*refs):
        for l in range(n):
            refs[n][:, l, :] = refs[l][...]

    return pl.pallas_call(body, name=name, grid=(cols // tc,),
                          in_specs=[pl.BlockSpec((rows, tc), lambda i: (0, i))] * n,
                          out_specs=pl.BlockSpec((rows, n, tc), lambda i: (0, 0, i)),
                          out_shape=jax.ShapeDtypeStruct((rows, n, cols), layers[0].dtype),
                          compiler_params=_cp("parallel"))(*layers)


def _adamw_small(ws, gs, ms, vs, *, name):
    n = len(ws)

    def body(*refs):
        w, g, m, v, outs = refs[:n], refs[n:2 * n], refs[2 * n:3 * n], refs[3 * n:4 * n], refs[4 * n:]
        for k in range(n):
            for slot, val in enumerate(_adamw_math(w[k][...], g[k][...], m[k][...], v[k][...])):
                outs[slot * n + k][...] = val

    outs = pl.pallas_call(body, name=name, out_shape=[jax.ShapeDtypeStruct(a.shape, F32) for a in ws] * 3)(
        *ws, *gs, *ms, *vs)
    return outs[:n], outs[n:2 * n], outs[2 * n:]


def _to_kernel_order(wt):
    gates = jnp.pad(wt[2048:2056], ((0, LANE - 2 * A_HEADS), (0, 0)))
    return jnp.concatenate([wt[0:2048], wt[2056:2568], wt[2824:3336], wt[2568:2696], wt[2696:2824], gates], axis=0)


def _from_kernel_order(main, tail):
    return jnp.concatenate([main[0:2048], tail[C_BG - DH_MAIN:C_BG - DH_MAIN + 2 * A_HEADS],
                            main[C_QB:C_QB + B_WIDTH], tail[0:B_KV_WIDTH], tail[B_KV_WIDTH:2 * B_KV_WIDTH],
                            main[C_ZB:C_ZB + B_WIDTH]], axis=0)


def _gate_params(a_log, dt_bias):
    return jnp.pad(jnp.stack([a_log, dt_bias]), ((0, SUBLANE - 2), (A_HEADS, LANE - 2 * A_HEADS)))


SMALL = ("conv_w", "a_log", "dt_bias", "norm_w", "sinks", "ln_g", "ln_b")


def _pack(parts, cols):
    flat = jnp.concatenate([p.reshape(-1) for p in parts])
    rows = -(-flat.shape[0] // cols)
    return jnp.pad(flat, (0, rows * cols - flat.shape[0])).reshape(rows, cols)


def _unpack(packed, shapes):
    flat = packed.reshape(-1)
    out, at = [], 0
    for s in shapes:
        n = math.prod(s)
        out.append(flat[at:at + n].reshape(s))
        at += n
    return out


def kernel(x, w_in, conv_w, a_log, dt_bias, norm_w, sinks, w_out, ln_g, ln_b, loss_target, m_w_in, m_conv_w, m_a_log, m_dt_bias, m_norm_w, m_sinks, m_w_out, m_ln_g, m_ln_b, v_w_in, v_conv_w, v_a_log, v_dt_bias, v_norm_w, v_sinks, v_w_out, v_ln_g, v_ln_b):
    xi, yi, ci = lax.axis_index("x"), lax.axis_index("y"), lax.axis_index("c")
    me = 2 * xi + yi

    to_t = lambda a: jnp.transpose(a, (2, 0, 1))
    from_t = lambda a: jnp.transpose(a, (1, 2, 0))

    wt_shard = to_t(w_in)

    def pack_weights(l):
        rows = jnp.pad(wt_shard[:, l], ((0, IN_PAD - IN_SHARD), (0, 0)))
        return jnp.concatenate([rows, w_out[l]], axis=0).astype(BF16)

    pack0, pack1 = pack_weights(0), pack_weights(1)
    got_in0, g_conv = _gather_two_level(pack0[:IN_PAD], conv_w, name="gather_weights_0")
    conv_full = jnp.moveaxis(g_conv, 0, 2).reshape(DEPTH, CONV_K, 3 * A_WIDTH)
    carriers = ("dn_pre", "dn_wy", "dn_scan")
    cuts = (0, 288, 624, IN_PAD)
    gathers = {nm: _gather_exchange([pack1[cuts[i]:cuts[i + 1]]]) for i, nm in enumerate(carriers)}
    gathers.update(in_proj=_gather_exchange([pack0[IN_PAD:]]), swa=_gather_exchange([pack1[IN_PAD:]]))
    w_in_of = lambda rows: _to_kernel_order(rows[:, :IN_SHARD].reshape(IN_COLS, D_MODEL))
    w_out_of = lambda rows: rows.reshape(D_MODEL, D_MODEL)
    args0 = _layer_args(w_in_of(got_in0), conv_full[0], a_log[0], dt_bias[0], sinks[0], norm_w[0],
                        lambda got: w_out_of(got[0]))

    def args1(got):
        rows = jnp.concatenate([got[nm][0] for nm in carriers], axis=1)
        return _layer_args(w_in_of(rows), conv_full[1], a_log[1], dt_bias[1], sinks[1], norm_w[1],
                           w_out_of(got["swa"][0]))

    def pack_grads(g):
        gin = _from_kernel_order(*g["w_in"]).reshape(N_SHARD, IN_SHARD, D_MODEL)
        gin = jnp.pad(gin, ((0, 0), (0, IN_PAD - IN_SHARD), (0, 0)))
        return jnp.concatenate([gin, g["w_out"].reshape(N_SHARD, OUT_SHARD, D_MODEL)], axis=1).astype(BF16)

    packed = {}

    def reduce1(grads1):
        packed[1] = pack_grads(grads1)
        half_rows = packed[1].shape[1] // 2
        return dict(dn_chunk=_reduce_scatter_exchange(packed[1], 0, half_rows),
                    swa=_reduce_scatter_exchange(packed[1], half_rows, half_rows))

    def reduce0(grads0, grads1, loss_tile):
        g0 = pack_grads(grads0)
        from_sibling = _run_exchange(_pair_window_exchange(g0), name="pair_reduce_0")[0]
        packed[0] = _pair_add(g0, from_sibling, ci, name="pair_add_0")
        gsmall = _pack([jnp.stack([g[nm] for g in (grads0, grads1)]) for nm in SMALL] + [loss_tile[0, 0:1]], D_MODEL)
        return _chip_scatter_exchange(packed[0], gsmall)

    _, dx, grads, landed1, (landed0, landed_small) = _local_step(
        x[0], loss_target[0], args0, args1, ln_g, ln_b, gathers=gathers, reduce1=reduce1, reduce0=reduce0)

    small_shapes = [(DEPTH,) + grads[0][nm].shape for nm in SMALL]
    halves = [_sum_chips(packed[0], landed0, me, tc=2 * LANE, name="reduce_sum_0"),
              _sum_scatter(packed[1], [landed1["dn_chunk"][0], landed1["swa"][0]], me, ci, tc=2 * LANE,
                           name="reduce_sum_1")]
    s_small = _sum_slots(landed_small, name="reduce_sum_small")
    others = _run_exchange(_share_exchange(halves), name="pair_share")
    full = [jnp.where(ci == 0, jnp.concatenate([mine, other], axis=1), jnp.concatenate([other, mine], axis=1))
            for mine, other in zip(halves, others)]
    grad_in_layers = [f[:IN_SHARD] for f in full]
    grad_out = jnp.stack([f[IN_PAD:] for f in full])
    out_blk = (1, OUT_SHARD, D_MODEL)
    *small_grads, loss = _unpack(s_small, small_shapes + [()])
    gs = dict(zip(SMALL, small_grads))
    gs["conv_w"] = lax.dynamic_slice_in_dim(gs["conv_w"], me * CONV_SHARD, CONV_SHARD, axis=2)

    grad_in_t = _interleave_layers(grad_in_layers, tc=2 * LANE, name="grad_in_layers")
    d_in, nm_in, nv_in = (from_t(o) for o in _adamw(to_t(w_in), grad_in_t, to_t(m_w_in), to_t(v_w_in),
                                                    (IN_SHARD // 6, DEPTH, D_MODEL), name="adamw_in"))
    grad_in = from_t(grad_in_t)
    d_out, nm_out, nv_out = _adamw(w_out, grad_out, m_w_out, v_w_out, out_blk, name="adamw_out")
    ws = dict(conv_w=conv_w, a_log=a_log, dt_bias=dt_bias, norm_w=norm_w, sinks=sinks, ln_g=ln_g, ln_b=ln_b)
    ms = dict(conv_w=m_conv_w, a_log=m_a_log, dt_bias=m_dt_bias, norm_w=m_norm_w, sinks=m_sinks, ln_g=m_ln_g, ln_b=m_ln_b)
    vs = dict(conv_w=v_conv_w, a_log=v_a_log, dt_bias=v_dt_bias, norm_w=v_norm_w, sinks=v_sinks, ln_g=v_ln_g, ln_b=v_ln_b)
    d_s, nm_s, nv_s = (dict(zip(SMALL, o)) for o in _adamw_small(*[[d[nm] for nm in SMALL] for d in (ws, gs, ms, vs)],
                                                                 name="adamw_small"))

    def in_order(big_in, small, big_out):
        return (big_in, small["conv_w"], small["a_log"], small["dt_bias"], small["norm_w"], small["sinks"], big_out,
                small["ln_g"], small["ln_b"])

    return (loss, dx[None], *in_order(grad_in, gs, grad_out), *in_order(d_in, d_s, d_out),
            *in_order(nm_in, nm_s, nm_out), *in_order(nv_in, nv_s, nv_out))
```

```python
import math

import jax
import jax.numpy as jnp
from jax import lax
from jax.experimental import pallas as pl
from jax.experimental.pallas import tpu as pltpu

F32 = jnp.float32
BF16 = jnp.bfloat16
HI = lax.Precision.HIGHEST

D_MODEL = 1024
DEPTH = 2
A_HEADS = 4
A_HEAD_DIM = 128
A_WIDTH = 512
CONV_K = 4
CHUNK = 64
B_Q_HEADS = 8
B_KV_HEADS = 2
B_HEAD_DIM = 64
B_GROUP = 4
B_WIDTH = 512
B_KV_WIDTH = 128
BLOCK = 128
IN_COLS = 3336
DEEPNORM_ALPHA = (2 * DEPTH) ** 0.25
LN_EPS = 1e-5
RMS_EPS = 1e-6
L2_EPS = 1e-6
ADAM_LR = 0.001
ADAM_B1 = 0.9
ADAM_B2 = 0.999
ADAM_EPS = 1e-08
ADAM_WD = 0.01
ADAM_STEP = 10

N_SHARD = 4
IN_SHARD = IN_COLS // N_SHARD
OUT_SHARD = D_MODEL // N_SHARD
CONV_SHARD = 3 * A_WIDTH // N_SHARD
IN_PAD = -(-IN_SHARD // 96) * 96

P_COLS = 3456
C_PRE = 0
C_ZA = 1536
C_QB = 2048
C_ZB = 2560
C_KB = 3072
C_VB = 3200
C_BG = 3328
DH_MAIN = C_KB
LANE = 128
SUBLANE = 8
HALO = 16
VMEM_LIMIT = 56 * 1024 * 1024
ALIBI = tuple(2.0 ** (-8.0 * (h + 1) / B_Q_HEADS) for h in range(B_Q_HEADS))
NEG = -1e30


def _cp(*sem):
    return pltpu.CompilerParams(dimension_semantics=sem, vmem_limit_bytes=VMEM_LIMIT)


def _dot(a, b):
    return jnp.dot(a.astype(BF16), b.astype(BF16), preferred_element_type=F32)


def _dot_nt(a, b):
    return lax.dot_general(a.astype(BF16), b.astype(BF16), (((1,), (1,)), ((), ())),
                           preferred_element_type=F32)


def _dot_tn(a, b):
    return lax.dot_general(a.astype(BF16), b.astype(BF16), (((0,), (0,)), ((), ())),
                           preferred_element_type=F32)


def _dot_hi(a, b):
    return jnp.dot(a, b, precision=HI, preferred_element_type=F32)


def _sigmoid(x):
    return jax.nn.sigmoid(x)


def _silu(x):
    return x * _sigmoid(x)


def _silu_and_grad(x):
    s = _sigmoid(x)
    return x * s, s * (1.0 + x * (1.0 - s))


def _softplus(x):
    return jnp.maximum(x, 0.0) + jnp.log(1.0 + jnp.exp(-jnp.abs(x)))


def _shift_down(cur, before, s):
    if s == 0:
        return cur
    r = pltpu.roll(cur, s, 0)
    rb = pltpu.roll(before, s, 0)
    row = lax.broadcasted_iota(jnp.int32, before.shape, 0)
    head = jnp.where(row < s, rb, r[0:SUBLANE])
    return jnp.concatenate([head, r[SUBLANE:]], axis=0)


def _shift_up(cur, after, s):
    if s == 0:
        return cur
    n = cur.shape[0]
    r = pltpu.roll(cur, n - s, 0)
    ra = pltpu.roll(after, SUBLANE - s, 0)
    row = lax.broadcasted_iota(jnp.int32, after.shape, 0)
    tail = jnp.where(row >= SUBLANE - s, ra, r[n - SUBLANE:])
    return jnp.concatenate([r[:n - SUBLANE], tail], axis=0)


def _conv_fwd(cur, before, w):
    acc = cur * w[CONV_K - 1:CONV_K, :]
    for s in range(1, CONV_K):
        acc = acc + _shift_down(cur, before, s) * w[CONV_K - 1 - s:CONV_K - s, :]
    return acc


def _matmul_nt(a, bt, *, tm, name, carry=None):
    m, k = a.shape
    n = bt.shape[0]
    c_ins, c_in_specs, c_out_specs, c_outs, c_scratch = _carry_specs(carry)

    def body(*refs):
        a_ref, b_ref, o_ref = _carried(carry, refs, 2, 1, m // tm)
        o_ref[...] = _dot_nt(a_ref[...], b_ref[...]).astype(o_ref.dtype)

    outs = pl.pallas_call(
        body, name=name, grid=(m // tm,),
        in_specs=[pl.BlockSpec((tm, k), lambda i: (i, 0)), pl.BlockSpec((n, k), lambda i: (0, 0))] + c_in_specs,
        out_specs=[pl.BlockSpec((tm, n), lambda i: (i, 0))] + c_out_specs,
        out_shape=[jax.ShapeDtypeStruct((m, n), BF16)] + c_outs,
        scratch_shapes=c_scratch,
        compiler_params=_cp("arbitrary"))(a, bt, *c_ins)
    return outs[0], outs[1:]


def _dn_pre(h, conv_w, par, *, tt, name, carry=None):
    t = h.shape[0]
    cw = 3 * A_WIDTH
    hb = tt // HALO

    c_ins, c_in_specs, c_out_specs, c_outs, c_scratch = _carry_specs(carry)

    def body(*refs):
        (pre_ref, halo_ref, bgi_ref, cw_ref, par_ref,
         q_ref, k_ref, v_ref, bg_ref, bgt_ref, c_ref) = _carried(carry, refs, 5, 6, t // tt)
        i = pl.program_id(0)
        cur = pre_ref[...].astype(F32)
        before = jnp.where(i > 0, halo_ref[...].astype(F32)[HALO - SUBLANE:], 0.0)
        conv = _conv_fwd(cur, before, cw_ref[...])
        c_ref[...] = conv
        s = _silu(conv)
        for hd in range(A_HEADS):
            sl = slice(hd * LANE, (hd + 1) * LANE)
            tq = s[:, hd * LANE:(hd + 1) * LANE]
            q_ref[:, sl] = tq * (lax.rsqrt(jnp.sum(tq * tq, -1, keepdims=True) + L2_EPS) * (A_HEAD_DIM ** -0.5))
            tk = s[:, A_WIDTH + hd * LANE:A_WIDTH + (hd + 1) * LANE]
            k_ref[:, sl] = tk * lax.rsqrt(jnp.sum(tk * tk, -1, keepdims=True) + L2_EPS)
        v_ref[...] = s[:, 2 * A_WIDTH:]
        raw = bgi_ref[...].astype(F32)
        lane = lax.broadcasted_iota(jnp.int32, raw.shape, 1)
        is_a = (lane >= A_HEADS) & (lane < 2 * A_HEADS)
        g = jnp.where(is_a, -jnp.exp(par_ref[0:1, :]) * _softplus(raw + par_ref[1:2, :]), 0.0)
        gc = _dot_hi(_chunk_tri(tt, lower=True), g)
        bg = jnp.where(lane < A_HEADS, _sigmoid(raw), gc)
        bg_ref[...] = bg
        bgt_ref[...] = jnp.transpose(bg)[0:SUBLANE, :]

    wide = jax.ShapeDtypeStruct((t, A_WIDTH), F32)
    outs = pl.pallas_call(
        body, name=name, grid=(t // tt,),
        in_specs=[pl.BlockSpec((tt, cw), lambda i: (i, 0)),
                  pl.BlockSpec((HALO, cw), lambda i: (jnp.maximum(i * hb - 1, 0), 0)),
                  pl.BlockSpec((tt, LANE), lambda i: (i, C_BG // LANE)),
                  pl.BlockSpec((CONV_K, cw), lambda i: (0, 0)),
                  pl.BlockSpec((SUBLANE, LANE), lambda i: (0, 0))] + c_in_specs,
        out_specs=[pl.BlockSpec((tt, A_WIDTH), lambda i: (i, 0))] * 3
        + [pl.BlockSpec((tt, LANE), lambda i: (i, 0)), pl.BlockSpec((SUBLANE, tt), lambda i: (0, i)),
           pl.BlockSpec((tt, cw), lambda i: (i, 0))] + c_out_specs,
        out_shape=[wide, wide, wide, jax.ShapeDtypeStruct((t, LANE), F32),
                   jax.ShapeDtypeStruct((SUBLANE, t), F32), jax.ShapeDtypeStruct((t, cw), F32)] + c_outs,
        scratch_shapes=c_scratch,
        compiler_params=_cp("arbitrary"))(h, h, h, conv_w, par, *c_ins)
    return outs[:6], outs[6:]


def _chunk_tri(n, lower):
    r = lax.broadcasted_iota(jnp.int32, (n, n), 0)
    c = lax.broadcasted_iota(jnp.int32, (n, n), 1)
    shift = CHUNK.bit_length() - 1
    same = jnp.right_shift(r, shift) == jnp.right_shift(c, shift)
    return (same & ((c <= r) if lower else (c >= r))).astype(F32)


def _chunk_masks():
    r = lax.broadcasted_iota(jnp.int32, (CHUNK, CHUNK), 0)
    c = lax.broadcasted_iota(jnp.int32, (CHUNK, CHUNK), 1)
    return r >= c, r > c, r == c


def _split(a):
    hi = a.astype(BF16)
    return hi, (a - hi.astype(F32)).astype(BF16)


def _dot3(a, b):
    (ah, al), (bh, bl) = a, b
    d = lambda p, q: jnp.dot(p, q, preferred_element_type=F32)
    return d(ah, bh) + (d(ah, bl) + d(al, bh))


def _tri_inv_many(a_list, eye):
    d = lambda p, q: jnp.dot(p.astype(BF16), q.astype(BF16), preferred_element_type=F32)
    r = lax.broadcasted_iota(jnp.int32, (CHUNK, CHUNK), 0)
    c = lax.broadcasted_iota(jnp.int32, (CHUNK, CHUNK), 1)
    same = lambda b: jnp.right_shift(r, b.bit_length() - 1) == jnp.right_shift(c, b.bit_length() - 1)
    x = [jnp.where(same(8), -a, 0.0) for a in a_list]
    tm = [eye + xi for xi in x]
    for _ in range(2):
        x = [d(xi, xi) for xi in x]
        tm = [t + d(t, xi) for t, xi in zip(tm, x)]
    for b in (16, 32, 64):
        low = [jnp.where(same(b) & ~same(b // 2), a, 0.0) for a in a_list]
        tm = [t - d(t, d(lo, t)) for t, lo in zip(tm, low)]
    res = [eye - _dot3(_split(eye + a), _split(t)) for a, t in zip(a_list, tm)]
    return [t + d(t, rs) for t, rs in zip(tm, res)]


def _chunk_gates(bg_v, bgt_v, hd):
    return (bg_v[:, hd:hd + 1], bg_v[:, A_HEADS + hd:A_HEADS + hd + 1],
            None if bgt_v is None else bgt_v[A_HEADS + hd:A_HEADS + hd + 1, :])


WY_ROWS = 512
SCAN_ROWS = 512
WY_GROUP = 8


def _dn_wy(q, k, v, bg, bgt, *, name, carry=None):
    t = q.shape[0]
    rows = WY_ROWS

    c_ins, c_in_specs, c_out_specs, c_outs, c_scratch = _carry_specs(carry)

    def body(*refs):
        q_ref, k_ref, v_ref, bg_ref, bgt_ref, u_ref, w_ref, tm_ref, qk_ref = _carried(carry, refs, 5, 4, t // rows)
        causal, strict, diag = _chunk_masks()
        eye = diag.astype(F32)
        for c0 in range(0, rows // CHUNK, WY_GROUP):
            items = [(c, hd) for c in range(c0, c0 + WY_GROUP) for hd in range(A_HEADS)]
            rs = lambda c: slice(c * CHUNK, (c + 1) * CHUNK)
            sl = lambda hd: slice(hd * LANE, (hd + 1) * LANE)
            hs = lambda hd: slice(hd * CHUNK, (hd + 1) * CHUNK)
            gates = [_chunk_gates(bg_ref[rs(c), :], bgt_ref[:, rs(c)], hd) for c, hd in items]
            dms = [jnp.exp(jnp.where(causal, gcol - grow, NEG)) for _, gcol, grow in gates]
            kbs = [k_ref[rs(c), sl(hd)] * g[0] for (c, hd), g in zip(items, gates)]
            a_list = [jnp.where(strict, _dot_nt(kb, k_ref[rs(c), sl(hd)]) * dm, 0.0)
                      for (c, hd), kb, dm in zip(items, kbs, dms)]
            for (c, hd), dm in zip(items, dms):
                qk_ref[rs(c), hs(hd)] = jnp.where(
                    causal, _dot_nt(q_ref[rs(c), sl(hd)], k_ref[rs(c), sl(hd)]) * dm, 0.0)
            tms = _tri_inv_many(a_list, eye)
            for (c, hd), g, kb, tmat in zip(items, gates, kbs, tms):
                tm_ref[rs(c), hs(hd)] = tmat
                u_ref[rs(c), sl(hd)] = _dot(tmat, v_ref[rs(c), sl(hd)] * g[0])
                w_ref[rs(c), sl(hd)] = _dot(tmat, kb * jnp.exp(g[1])).astype(BF16)

    blk = pl.BlockSpec((rows, A_WIDTH), lambda i: (i, 0))
    half = pl.BlockSpec((rows, A_HEADS * CHUNK), lambda i: (i, 0))
    outs = pl.pallas_call(
        body, name=name, grid=(t // rows,),
        in_specs=[blk, blk, blk, pl.BlockSpec((rows, LANE), lambda i: (i, 0)),
                  pl.BlockSpec((SUBLANE, rows), lambda i: (0, i))] + c_in_specs,
        out_specs=[blk, blk, half, half] + c_out_specs,
        out_shape=[jax.ShapeDtypeStruct((t, A_WIDTH), F32), jax.ShapeDtypeStruct((t, A_WIDTH), BF16),
                   jax.ShapeDtypeStruct((t, A_HEADS * CHUNK), F32),
                   jax.ShapeDtypeStruct((t, A_HEADS * CHUNK), F32)] + c_outs,
        scratch_shapes=c_scratch,
        compiler_params=_cp("arbitrary"))(q, k, v, bg, bgt, *c_ins)
    return outs[:4], outs[4:]


def _dn_scan_fwd(q, k, u, w, qk, bg, *, name, carry=None):
    t = q.shape[0]
    rows = SCAN_ROWS
    per = rows // CHUNK
    c_ins, c_in_specs, c_out_specs, c_outs, c_scratch = _carry_specs(carry)

    def body(*refs):
        q_ref, k_ref, u_ref, w_ref, qk_ref, bg_ref, o_ref, vn_ref, s_ref, state = _carried(carry, refs, 6, 3, t // rows)

        @pl.when(pl.program_id(0) == 0)
        def _():
            state[...] = jnp.zeros_like(state)

        heads = range(A_HEADS)
        sl = lambda hd: slice(hd * LANE, (hd + 1) * LANE)
        s_cur = [state[hd] for hd in heads]
        for c in range(per):
            rs = slice(c * CHUNK, (c + 1) * CHUNK)
            bg_v = bg_ref[rs, :]
            gcols = [_chunk_gates(bg_v, None, hd)[1] for hd in heads]
            glasts = [gc[CHUNK - 1:CHUNK, :] for gc in gcols]
            for hd in heads:
                s_ref[c, hd] = s_cur[hd].astype(BF16)
            vns = [u_ref[rs, sl(hd)] - _dot(w_ref[rs, sl(hd)], s_cur[hd]) for hd in heads]
            qss = [_dot(q_ref[rs, sl(hd)] * jnp.exp(gcols[hd]), s_cur[hd]) for hd in heads]
            s_cur = [s_cur[hd] * jnp.exp(glasts[hd])
                     + _dot_tn(k_ref[rs, sl(hd)] * jnp.exp(glasts[hd] - gcols[hd]), vns[hd]) for hd in heads]
            for hd in heads:
                vn_ref[rs, sl(hd)] = vns[hd]
                o_ref[rs, sl(hd)] = qss[hd] + _dot(qk_ref[rs, hd * CHUNK:(hd + 1) * CHUNK], vns[hd])
        for hd in heads:
            state[hd] = s_cur[hd]

    blk = pl.BlockSpec((rows, A_WIDTH), lambda i: (i, 0))
    half = pl.BlockSpec((rows, A_HEADS * CHUNK), lambda i: (i, 0))
    wide = jax.ShapeDtypeStruct((t, A_WIDTH), F32)
    outs = pl.pallas_call(
        body, name=name, grid=(t // rows,),
        in_specs=[blk, blk, blk, blk, half, pl.BlockSpec((rows, LANE), lambda i: (i, 0))] + c_in_specs,
        out_specs=[blk, blk, pl.BlockSpec((per, A_HEADS, LANE, LANE), lambda i: (i, 0, 0, 0))] + c_out_specs,
        out_shape=[wide, wide, jax.ShapeDtypeStruct((t // CHUNK, A_HEADS, LANE, LANE), BF16)] + c_outs,
        scratch_shapes=[pltpu.VMEM((A_HEADS, LANE, LANE), F32)] + c_scratch,
        compiler_params=_cp("arbitrary"))(q, k, u, w, qk, bg, *c_ins)
    return outs[:3], outs[3:]


def _stack_heads(ref, hk):
    return jnp.concatenate([ref[:, h * B_HEAD_DIM:(h + 1) * B_HEAD_DIM].astype(F32)
                            for h in range(hk * B_GROUP, (hk + 1) * B_GROUP)], axis=0)


def _swa_window():
    qi = lax.broadcasted_iota(jnp.int32, (BLOCK, BLOCK), 0)
    kj = lax.broadcasted_iota(jnp.int32, (BLOCK, BLOCK), 1)
    dist = jnp.where(kj > qi, qi + BLOCK - kj, qi - kj).astype(F32)
    rows = lax.broadcasted_iota(jnp.int32, (B_GROUP * BLOCK, BLOCK), 0)
    cols = lax.broadcasted_iota(jnp.int32, (B_GROUP * BLOCK, BLOCK), 1)
    return cols > jnp.bitwise_and(rows, BLOCK - 1), dist


def _swa_group_probs(q_ref, sk_ref, kp, kc, vp, vc, n_blk):
    hks = range(B_KV_HEADS)
    heads = lambda hk: range(hk * B_GROUP, (hk + 1) * B_GROUP)
    ksl = lambda hk: slice(hk * B_HEAD_DIM, (hk + 1) * B_HEAD_DIM)
    upper, dist = _swa_window()
    no_prev = jnp.where(n_blk > 0, 0.0, NEG)
    ones = jnp.ones((BLOCK, B_HEAD_DIM), BF16)
    with_ones = lambda v, hk: jnp.concatenate([v[:, ksl(hk)].astype(BF16), ones], axis=1)
    qs = [_stack_heads(q_ref, hk) * (B_HEAD_DIM ** -0.5) for hk in hks]
    sink = [jnp.concatenate([jnp.broadcast_to(sk_ref[h:h + 1, 0:1], (BLOCK, 1)) for h in heads(hk)], axis=0)
            for hk in hks]
    s = [jnp.where(upper, _dot_nt(qs[hk], kp[:, ksl(hk)]) + no_prev, _dot_nt(qs[hk], kc[:, ksl(hk)]))
         - jnp.concatenate([ALIBI[h] * dist for h in heads(hk)], axis=0) for hk in hks]
    m = [jnp.maximum(jnp.max(s[hk], axis=-1, keepdims=True), sink[hk]) for hk in hks]
    p = [jnp.exp(s[hk] - m[hk]) for hk in hks]
    p_up = [jnp.where(upper, p[hk], 0.0) for hk in hks]
    oe = [jnp.dot(p_up[hk].astype(BF16), with_ones(vp, hk), preferred_element_type=F32)
          + jnp.dot((p[hk] - p_up[hk]).astype(BF16), with_ones(vc, hk), preferred_element_type=F32) for hk in hks]
    ps = [jnp.exp(sink[hk] - m[hk]) for hk in hks]
    inv = [1.0 / (oe[hk][:, B_HEAD_DIM:B_HEAD_DIM + 1] + ps[hk]) for hk in hks]
    return upper, [(qs[hk], p[hk] * inv[hk], ps[hk] * inv[hk], oe[hk][:, :B_HEAD_DIM] * inv[hk]) for hk in hks]


def _swa_specs():
    qspec = lambda c0: pl.BlockSpec((BLOCK, B_WIDTH), lambda i: (i, c0 // B_WIDTH))
    cur = lambda c0: pl.BlockSpec((BLOCK, LANE), lambda i: (i, c0 // LANE))
    prev = lambda c0: pl.BlockSpec((BLOCK, LANE), lambda i: (jnp.maximum(i - 1, 0), c0 // LANE))
    return qspec, cur, prev


def _carried(carry, refs, n_in, n_out, steps):
    if carry is None:
        return refs
    ci, co = len(carry.ins), len(carry.outs)
    own = refs[:n_in] + refs[n_in + ci:n_in + ci + n_out] + refs[n_in + ci + n_out + co:len(refs) - 3]
    parts = refs[n_in:n_in + ci], refs[n_in + ci + n_out:n_in + ci + n_out + co], refs[len(refs) - 3:]

    @pl.when(pl.program_id(0) == 0)
    def _():
        carry.start(*parts)

    @pl.when(pl.program_id(0) == steps - 1)
    def _():
        carry.finish(*parts)

    return own


def _carry_specs(carry):
    if carry is None:
        return [], [], [], [], []
    return (list(carry.ins), [_ANY] * len(carry.ins), [_ANY] * len(carry.outs), list(carry.outs), carry.scratch())


def _swa_fwd(h, sinks_b, *, name, carry=None):
    t = h.shape[0]
    qspec, cur, prev = _swa_specs()
    c_ins, c_in_specs, c_out_specs, c_outs, c_scratch = _carry_specs(carry)

    def body(*refs):
        q_ref, kc_ref, kp_ref, vc_ref, vp_ref, sk_ref, o_ref, p_ref, ps_ref = _carried(carry, refs, 6, 3, t // BLOCK)
        n_blk = pl.program_id(0)
        _, groups = _swa_group_probs(q_ref, sk_ref, kp_ref[...], kc_ref[...], vp_ref[...], vc_ref[...], n_blk)
        lane = lax.broadcasted_iota(jnp.int32, (BLOCK, LANE), 1)
        sink_probs = jnp.zeros((BLOCK, LANE), F32)
        for hk, (_, p, ps, o) in enumerate(groups):
            for g in range(B_GROUP):
                hq = hk * B_GROUP + g
                rows = slice(g * BLOCK, (g + 1) * BLOCK)
                o_ref[:, hq * B_HEAD_DIM:(hq + 1) * B_HEAD_DIM] = o[rows]
                p_ref[:, hq * BLOCK:(hq + 1) * BLOCK] = p[rows].astype(BF16)
                sink_probs = sink_probs + jnp.where(lane == hq, ps[rows], 0.0)
        ps_ref[...] = sink_probs

    row = lambda w: pl.BlockSpec((BLOCK, w), lambda i: (i, 0))
    outs = pl.pallas_call(
        body, name=name, grid=(t // BLOCK,),
        in_specs=[qspec(C_QB), cur(C_KB), prev(C_KB), cur(C_VB), prev(C_VB),
                  pl.BlockSpec((B_Q_HEADS, LANE), lambda i: (0, 0))] + c_in_specs,
        out_specs=[row(B_WIDTH), row(B_Q_HEADS * BLOCK), row(LANE)] + c_out_specs,
        out_shape=[jax.ShapeDtypeStruct((t, B_WIDTH), F32), jax.ShapeDtypeStruct((t, B_Q_HEADS * BLOCK), BF16),
                   jax.ShapeDtypeStruct((t, LANE), F32)] + c_outs,
        scratch_shapes=c_scratch,
        compiler_params=_cp("arbitrary"))(h, h, h, h, h, sinks_b, *c_ins)
    return outs[:3], outs[3:]


def _rms_gate(o, za, nw):
    outs = []
    for hd in range(A_HEADS):
        oh = o[:, hd * LANE:(hd + 1) * LANE]
        r = lax.rsqrt(jnp.mean(oh * oh, -1, keepdims=True) + RMS_EPS)
        outs.append(oh * r * nw)
    return jnp.concatenate(outs, axis=1) * _silu(za)


def _out_ln(x, oa, ob, h, norm_w, w_out, ln_g, ln_b, *, tm, name, target=None):
    t = x.shape[0]
    last = target is not None

    def body(*refs):
        x_ref, oa_ref, ob_ref, za_ref, zb_ref, nw_ref, w_ref, g_ref, b_ref = refs[:9]
        xn_ref, mx_ref, r_ref = refs[9 + last:12 + last]
        ya = _rms_gate(oa_ref[...], za_ref[...].astype(F32), nw_ref[...])
        yb = ob_ref[...] * _silu(zb_ref[...].astype(F32))
        mixed = jnp.concatenate([ya, yb], axis=1).astype(BF16)
        mx_ref[...] = mixed
        r = DEEPNORM_ALPHA * x_ref[...] + jnp.dot(mixed, w_ref[...], preferred_element_type=F32)
        r_ref[...] = r
        mu = jnp.mean(r, -1, keepdims=True)
        xc = r - mu
        var = jnp.mean(xc * xc, -1, keepdims=True)
        xn = xc * lax.rsqrt(var + LN_EPS) * g_ref[...] + b_ref[...]
        if not last:
            xn_ref[...] = xn
            return
        loss_ref = refs[13]

        @pl.when(pl.program_id(0) == 0)
        def _():
            loss_ref[...] = jnp.zeros_like(loss_ref)

        err = xn - refs[9][...]
        xn_ref[...] = err * (1.0 / D_MODEL)
        loss_ref[...] += 0.5 / D_MODEL * jnp.sum(err * err)

    row = lambda w, c: pl.BlockSpec((tm, w), lambda i: (i, c))
    full = lambda a, b: pl.BlockSpec((a, b), lambda i: (0, 0))
    wide = jax.ShapeDtypeStruct((t, D_MODEL), F32)
    return pl.pallas_call(
        body, name=name, grid=(t // tm,),
        in_specs=[row(D_MODEL, 0), row(A_WIDTH, 0), row(B_WIDTH, 0), row(A_WIDTH, C_ZA // A_WIDTH),
                  row(B_WIDTH, C_ZB // B_WIDTH), full(1, LANE), full(D_MODEL, D_MODEL), full(1, D_MODEL),
                  full(1, D_MODEL)] + [row(D_MODEL, 0)] * last,
        out_specs=[row(D_MODEL, 0), row(D_MODEL, 0), row(D_MODEL, 0)] + [full(SUBLANE, LANE)] * last,
        out_shape=[wide, jax.ShapeDtypeStruct((t, D_MODEL), BF16), wide]
        + [jax.ShapeDtypeStruct((SUBLANE, LANE), F32)] * last,
        compiler_params=_cp("arbitrary" if last else "parallel"))(
        x, oa, ob, h, h, norm_w, w_out, ln_g, ln_b, *([target] if last else []))


def _layer_fwd(x, wt, conv_w, par, sinks_b, norm_w, w_out_bf, ln_g, ln_b, l, carries=None, target=None):
    carries = carries or {}
    h, got_in = _matmul_nt(x, wt, tm=512, name=f"in_proj_{l}", carry=carries.get("in_proj"))
    if callable(w_out_bf):
        w_out_bf = w_out_bf(got_in)
    (q, k, v, bg, bgt, conv), got_pre = _dn_pre(h, conv_w, par, tt=512, name=f"dn_pre_{l}",
                                                carry=carries.get("dn_pre"))
    (u, w, tmat, qk), got_wy = _dn_wy(q, k, v, bg, bgt, name=f"dn_wy_{l}", carry=carries.get("dn_wy"))
    (oa, vn, s_all), got_scan = _dn_scan_fwd(q, k, u, w, qk, bg, name=f"dn_scan_{l}", carry=carries.get("dn_scan"))
    (ob, swa_p, swa_ps), got_swa = _swa_fwd(h, sinks_b, name=f"swa_fwd_{l}", carry=carries.get("swa"))
    xn, mixed, r, *loss = _out_ln(x, oa, ob, h, norm_w, w_out_bf, ln_g, ln_b, tm=512, name=f"out_ln_{l}", target=target)
    if loss:
        xn = (xn, loss[0])
    res = dict(x=x, h=h, q=q, k=k, v=v, bg=bg, bgt=bgt, w=w, tmat=tmat, qk=qk, vn=vn, oa=oa, s_all=s_all,
               mixed=mixed, r=r, w_out=w_out_bf, ob=ob, swa_p=swa_p, swa_ps=swa_ps, conv=conv)
    return xn, res, dict(in_proj=got_in, dn_pre=got_pre, dn_wy=got_wy, dn_scan=got_scan, swa=got_swa)


def _ln_out_bwd(dxn, r, mixed, ln_g, w_out, *, tm, name):
    t = dxn.shape[0]

    def body(dxn_ref, r_ref, mx_ref, g_ref, w_ref, dr_ref, dm_ref, dw_ref, dg_ref, db_ref):
        @pl.when(pl.program_id(0) == 0)
        def _():
            dw_ref[...] = jnp.zeros_like(dw_ref)
            dg_ref[...] = jnp.zeros_like(dg_ref)
            db_ref[...] = jnp.zeros_like(db_ref)

        rr = r_ref[...]
        xc = rr - jnp.mean(rr, -1, keepdims=True)
        rstd = lax.rsqrt(jnp.mean(xc * xc, -1, keepdims=True) + LN_EPS)
        xhat = xc * rstd
        dxn_v = dxn_ref[...]
        dxh = dxn_v * g_ref[...]
        dr = rstd * (dxh - jnp.mean(dxh, -1, keepdims=True) - xhat * jnp.mean(dxh * xhat, -1, keepdims=True))
        dr_ref[...] = dr
        dg_ref[...] += jnp.sum(dxn_v * xhat, axis=0, keepdims=True)
        db_ref[...] += jnp.sum(dxn_v, axis=0, keepdims=True)
        drb = dr.astype(BF16)
        dm_ref[...] = _dot_nt(drb, w_ref[...])
        dw_ref[...] += _dot_tn(mx_ref[...], drb)

    row = pl.BlockSpec((tm, D_MODEL), lambda i: (i, 0))
    full = lambda a, b: pl.BlockSpec((a, b), lambda i: (0, 0))
    big = jax.ShapeDtypeStruct((t, D_MODEL), F32)
    vec = jax.ShapeDtypeStruct((1, D_MODEL), F32)
    return pl.pallas_call(
        body, name=name, grid=(t // tm,),
        in_specs=[row, row, row, full(1, D_MODEL), full(D_MODEL, D_MODEL)],
        out_specs=[row, row, full(D_MODEL, D_MODEL), full(1, D_MODEL), full(1, D_MODEL)],
        out_shape=[big, big, jax.ShapeDtypeStruct((D_MODEL, D_MODEL), F32), vec, vec],
        compiler_params=_cp("arbitrary"))(dxn, r, mixed, ln_g, w_out)


def _dn_post_bwd(dm, oa, h, norm_w, *, tm, name):
    t = oa.shape[0]

    def body(dy_ref, o_ref, za_ref, nw_ref, do_ref, dza_ref, dnw_ref):
        @pl.when(pl.program_id(0) == 0)
        def _():
            dnw_ref[...] = jnp.zeros_like(dnw_ref)

        nw = nw_ref[...]
        dnw = jnp.zeros_like(nw)
        for hd in range(A_HEADS):
            sl = slice(hd * LANE, (hd + 1) * LANE)
            oh, za, dy = o_ref[:, sl], za_ref[:, sl].astype(F32), dy_ref[:, sl]
            rs = lax.rsqrt(jnp.mean(oh * oh, -1, keepdims=True) + RMS_EPS)
            nrm = oh * rs
            gate, dgate = _silu_and_grad(za)
            dza_ref[:, sl] = (dy * nrm * nw * dgate).astype(dza_ref.dtype)
            dn = dy * gate
            dnw = dnw + jnp.sum(dn * nrm, axis=0, keepdims=True)
            dnn = dn * nw
            do_ref[:, sl] = (rs * dnn - oh * (rs * rs * rs) * jnp.mean(dnn * oh, -1, keepdims=True)).astype(BF16)
        dnw_ref[...] += dnw

    row = lambda c: pl.BlockSpec((tm, A_WIDTH), lambda i: (i, c))
    wide = jax.ShapeDtypeStruct((t, A_WIDTH), F32)
    return pl.pallas_call(
        body, name=name, grid=(t // tm,),
        in_specs=[row(0), row(0), row(C_ZA // A_WIDTH), pl.BlockSpec((1, LANE), lambda i: (0, 0))],
        out_specs=[row(0), row(C_ZA // A_WIDTH), pl.BlockSpec((1, LANE), lambda i: (0, 0))],
        out_shape=[jax.ShapeDtypeStruct((t, A_WIDTH), BF16), jax.ShapeDtypeStruct((t, DH_MAIN), BF16),
                   jax.ShapeDtypeStruct((1, LANE), F32)],
        compiler_params=_cp("arbitrary"))(dm, oa, h, norm_w)


def _dn_scan_bwd(q, k, w, qk, bg, do, *, name):
    t = q.shape[0]
    rows = SCAN_ROWS
    per = rows // CHUNK
    n = t // rows

    def body(q_ref, k_ref, w_ref, qk_ref, bg_ref, do_ref, dvn_ref, ds_ref, dstate):
        @pl.when(pl.program_id(0) == 0)
        def _():
            dstate[...] = jnp.zeros_like(dstate)

        heads = range(A_HEADS)
        sl = lambda hd: slice(hd * LANE, (hd + 1) * LANE)
        ds_cur = [dstate[hd] for hd in heads]
        for c in reversed(range(per)):
            rs = slice(c * CHUNK, (c + 1) * CHUNK)
            bg_v = bg_ref[rs, :]
            gcols = [_chunk_gates(bg_v, None, hd)[1] for hd in heads]
            glasts = [gc[CHUNK - 1:CHUNK, :] for gc in gcols]
            for hd in heads:
                ds_ref[c, hd] = ds_cur[hd].astype(BF16)
            pdo = [_dot_tn(qk_ref[rs, hd * CHUNK:(hd + 1) * CHUNK], do_ref[rs, sl(hd)]) for hd in heads]
            qdo = [_dot_tn(q_ref[rs, sl(hd)] * jnp.exp(gcols[hd]), do_ref[rs, sl(hd)]) for hd in heads]
            dvns = [pdo[hd] + _dot(k_ref[rs, sl(hd)] * jnp.exp(glasts[hd] - gcols[hd]), ds_cur[hd]) for hd in heads]
            ds_cur = [qdo[hd] + jnp.exp(glasts[hd]) * ds_cur[hd] - _dot_tn(w_ref[rs, sl(hd)], dvns[hd])
                      for hd in heads]
            for hd in heads:
                dvn_ref[rs, sl(hd)] = dvns[hd].astype(BF16)
        for hd in heads:
            dstate[hd] = ds_cur[hd]

    blk = pl.BlockSpec((rows, A_WIDTH), lambda i: (n - 1 - i, 0))
    return pl.pallas_call(
        body, name=name, grid=(n,),
        in_specs=[blk, blk, blk, pl.BlockSpec((rows, A_HEADS * CHUNK), lambda i: (n - 1 - i, 0)),
                  pl.BlockSpec((rows, LANE), lambda i: (n - 1 - i, 0)), blk],
        out_specs=[blk, pl.BlockSpec((per, A_HEADS, LANE, LANE), lambda i: (n - 1 - i, 0, 0, 0))],
        out_shape=[jax.ShapeDtypeStruct((t, A_WIDTH), BF16),
                   jax.ShapeDtypeStruct((t // CHUNK, A_HEADS, LANE, LANE), BF16)],
        scratch_shapes=[pltpu.VMEM((A_HEADS, LANE, LANE), F32)],
        compiler_params=_cp("arbitrary"))(q, k, w, qk, bg, do)


def _dn_chunk_bwd(q, k, v, vn, tmat, qk, bg, bgt, s_all, ds_all, dvn, do, *, name, carry=None):
    t = q.shape[0]
    rows = WY_ROWS
    per = rows // CHUNK

    c_ins, c_in_specs, c_out_specs, c_outs, c_scratch = _carry_specs(carry)

    def body(*refs):
        (q_ref, k_ref, v_ref, vn_ref, tm_ref, qk_ref, bg_ref, bgt_ref, s_ref, ds_ref, dvn_ref, do_ref,
         dq_ref, dk_ref, dv_ref, dbg_ref, dbgt_ref) = _carried(carry, refs, 12, 5, t // rows)
        causal, strict, _ = _chunk_masks()
        lane = lax.broadcasted_iota(jnp.int32, (CHUNK, LANE), 1)
        rowi = lax.broadcasted_iota(jnp.int32, (CHUNK, 1), 0)
        sub = lax.broadcasted_iota(jnp.int32, (SUBLANE, CHUNK), 0)
        rs = lambda c: slice(c * CHUNK, (c + 1) * CHUNK)
        sl = lambda hd: slice(hd * LANE, (hd + 1) * LANE)
        hs = lambda hd: slice(hd * CHUNK, (hd + 1) * CHUNK)
        for c0 in range(0, per, WY_GROUP):
            items = [(c, hd) for c in range(c0, c0 + WY_GROUP) for hd in range(A_HEADS)]
            at = lambda ref: [ref[rs(c), sl(hd)] for c, hd in items]
            qs, ks, vs, dos, vns, dvns = at(q_ref), at(k_ref), at(v_ref), at(do_ref), at(vn_ref), at(dvn_ref)
            tmhs = [tm_ref[rs(c), hs(hd)] for c, hd in items]
            ps = [qk_ref[rs(c), hs(hd)] for c, hd in items]
            gates = [_chunk_gates(bg_ref[rs(c), :], bgt_ref[:, rs(c)], hd) for c, hd in items]
            betas = [g[0] for g in gates]
            gcols = [g[1] for g in gates]
            dmats = [jnp.exp(jnp.where(causal, g[1] - g[2], NEG)) for g in gates]
            es = [jnp.exp(gc) for gc in gcols]
            glasts = [gc[CHUNK - 1:CHUNK, :] for gc in gcols]
            eks = [jnp.exp(gl - gc) for gl, gc in zip(glasts, gcols)]
            kbs = [kh * b for kh, b in zip(ks, betas)]
            vbs = [vh * b for vh, b in zip(vs, betas)]
            kbes = [kb * e for kb, e in zip(kbs, es)]

            a_s = [jnp.where(strict, _dot_nt(kb, kh) * dm, 0.0) for kb, kh, dm in zip(kbs, ks, dmats)]
            dps = [jnp.where(causal, _dot_nt(doh, vnh), 0.0) for doh, vnh in zip(dos, vns)]
            rows2 = lambda a, b: jnp.concatenate([a, b], axis=0)
            cols2 = lambda a, b: jnp.concatenate([a, b], axis=1)
            by_s = [_dot_nt(rows2(doh, dvnh), s_ref[c, hd]) for doh, dvnh, (c, hd) in zip(dos, dvns, items)]
            dqds = [m[:CHUNK] for m in by_s]
            dws = [-m[CHUNK:] for m in by_s]
            dkds = [_dot_nt(vnh, ds_ref[c, hd]) for vnh, (c, hd) in zip(vns, items)]
            dgts = [jnp.sum(s_ref[c, hd].astype(F32) * ds_ref[c, hd].astype(F32), keepdims=True) for c, hd in items]
            pairs = [cols2(dvnh, dw) for dvnh, dw in zip(dvns, dws)]
            by_t = [_dot_tn(tmh, pr) for tmh, pr in zip(tmhs, pairs)]
            dvbs = [m[:, :LANE] for m in by_t]
            dkbes = [m[:, LANE:] for m in by_t]
            dts = [_dot_nt(pr, cols2(vb, kbe)) for pr, vb, kbe in zip(pairs, vbs, kbes)]
            xs = [_dot_nt(dt, tmh) for dt, tmh in zip(dts, tmhs)]
            das = [jnp.where(strict, -_dot_tn(tmh, x), 0.0) for tmh, x in zip(tmhs, xs)]
            dmas = [da * dm for da, dm in zip(das, dmats)]
            dmps = [dp * dm for dp, dm in zip(dps, dmats)]
            stacked = [rows2(dma, dmp) for dma, dmp in zip(dmas, dmps)]
            by_k = [_dot(st, kh) for st, kh in zip(stacked, ks)]
            dkbs = [m[:CHUNK] + dkbe * e for m, dkbe, e in zip(by_k, dkbes, es)]
            for i, (c, hd) in enumerate(items):
                dq_ref[rs(c), sl(hd)] = by_k[i][CHUNK:] + dqds[i] * es[i]
                dk_ref[rs(c), sl(hd)] = (_dot_tn(stacked[i], rows2(kbs[i], qs[i])) + dkds[i] * eks[i]
                                         + dkbs[i] * betas[i])
                dv_ref[rs(c), sl(hd)] = dvbs[i] * betas[i]
            for c in range(c0, c0 + WY_GROUP):
                acc = jnp.zeros((CHUNK, LANE), F32)
                acc_t = jnp.zeros((SUBLANE, CHUNK), F32)
                for i, (ci, hd) in enumerate(items):
                    if ci != c:
                        continue
                    gmat = das[i] * a_s[i] + dps[i] * ps[i]
                    rk = jnp.sum(dkds[i] * ks[i], -1, keepdims=True) * eks[i]
                    de = jnp.sum(dqds[i] * qs[i] + dkbes[i] * kbs[i], -1, keepdims=True)
                    dglast = jnp.sum(rk, keepdims=True) + dgts[i] * jnp.exp(glasts[i])
                    dgc = (jnp.sum(gmat, -1, keepdims=True) + de * es[i] - rk
                           + jnp.where(rowi == CHUNK - 1, dglast, 0.0))
                    dbeta = jnp.sum(dkbs[i] * ks[i] + dvbs[i] * vs[i], -1, keepdims=True)
                    acc = acc + jnp.where(lane == hd, dbeta, 0.0) + jnp.where(lane == A_HEADS + hd, dgc, 0.0)
                    acc_t = acc_t + jnp.where(sub == A_HEADS + hd, -jnp.sum(gmat, axis=0, keepdims=True), 0.0)
                dbg_ref[rs(c), :] = acc
                dbgt_ref[:, rs(c)] = acc_t

    blk = pl.BlockSpec((rows, A_WIDTH), lambda i: (i, 0))
    half = pl.BlockSpec((rows, A_HEADS * CHUNK), lambda i: (i, 0))
    col = pl.BlockSpec((rows, LANE), lambda i: (i, 0))
    rowf = pl.BlockSpec((SUBLANE, rows), lambda i: (0, i))
    st = pl.BlockSpec((per, A_HEADS, LANE, LANE), lambda i: (i, 0, 0, 0))
    wide = jax.ShapeDtypeStruct((t, A_WIDTH), F32)
    outs = pl.pallas_call(
        body, name=name, grid=(t // rows,),
        in_specs=[blk, blk, blk, blk, half, half, col, rowf, st, st, blk, blk] + c_in_specs,
        out_specs=[blk, blk, blk, col, rowf] + c_out_specs,
        out_shape=[wide, wide, wide, jax.ShapeDtypeStruct((t, LANE), F32),
                   jax.ShapeDtypeStruct((SUBLANE, t), F32)] + c_outs,
        scratch_shapes=c_scratch,
        compiler_params=_cp("arbitrary"))(q, k, v, vn, tmat, qk, bg, bgt, s_all, ds_all, dvn, do, *c_ins)
    return outs[:5], outs[5:]


def _dn_pre_bwd(h, conv, par, dq, dk, dv, dbg, dbgt, *, tt, name):
    t = h.shape[0]
    cw = 3 * A_WIDTH

    def body(conv_ref, bgi_ref, par_ref, dq_ref, dk_ref, dv_ref, dbg_ref, dbgt_ref, dc_ref, dbgi_ref, dpar_ref):
        i = pl.program_id(0)

        @pl.when(i == 0)
        def _():
            dpar_ref[...] = jnp.zeros_like(dpar_ref)

        s, ds = _silu_and_grad(conv_ref[...])
        for hd in range(A_HEADS):
            sl = slice(hd * LANE, (hd + 1) * LANE)
            for base, d_ref, scale in ((0, dq_ref, A_HEAD_DIM ** -0.5), (A_WIDTH, dk_ref, 1.0)):
                csl = slice(base + hd * LANE, base + (hd + 1) * LANE)
                tq = s[:, base + hd * LANE:base + (hd + 1) * LANE]
                dy = d_ref[:, sl]
                rq = lax.rsqrt(jnp.sum(tq * tq, -1, keepdims=True) + L2_EPS)
                dtq = scale * (rq * dy - tq * (rq * rq * rq) * jnp.sum(dy * tq, -1, keepdims=True))
                dc_ref[:, csl] = dtq * ds[:, base + hd * LANE:base + (hd + 1) * LANE]
        dc_ref[:, 2 * A_WIDTH:] = dv_ref[...] * ds[:, 2 * A_WIDTH:]
        raw = bgi_ref[...].astype(F32)
        lane = lax.broadcasted_iota(jnp.int32, raw.shape, 1)
        is_b = lane < A_HEADS
        is_a = (lane >= A_HEADS) & (lane < 2 * A_HEADS)
        rows_t = jnp.concatenate([dbgt_ref[...], jnp.zeros((LANE - SUBLANE, tt), F32)], axis=0)
        dbg_v = dbg_ref[...] + jnp.where(is_a, jnp.transpose(rows_t), 0.0)
        dbg_v = jnp.where(is_a, _dot_hi(_chunk_tri(tt, lower=False), jnp.where(is_a, dbg_v, 0.0)), dbg_v)
        beta = _sigmoid(raw)
        z = raw + par_ref[1:2, :]
        neg_ea = -jnp.exp(par_ref[0:1, :])
        g = neg_ea * _softplus(z)
        da = dbg_v * neg_ea * _sigmoid(z)
        dbgi_ref[...] = jnp.where(is_b, dbg_v * beta * (1.0 - beta), jnp.where(is_a, da, 0.0))
        dpar_ref[0:1, :] += jnp.sum(jnp.where(is_a, dbg_v * g, 0.0), axis=0, keepdims=True)
        dpar_ref[1:2, :] += jnp.sum(jnp.where(is_a, da, 0.0), axis=0, keepdims=True)

    wide = pl.BlockSpec((tt, A_WIDTH), lambda i: (i, 0))
    return pl.pallas_call(
        body, name=name, grid=(t // tt,),
        in_specs=[pl.BlockSpec((tt, cw), lambda i: (i, 0)),
                  pl.BlockSpec((tt, LANE), lambda i: (i, C_BG // LANE)),
                  pl.BlockSpec((SUBLANE, LANE), lambda i: (0, 0)),
                  wide, wide, wide, pl.BlockSpec((tt, LANE), lambda i: (i, 0)),
                  pl.BlockSpec((SUBLANE, tt), lambda i: (0, i))],
        out_specs=[pl.BlockSpec((tt, cw), lambda i: (i, 0)), pl.BlockSpec((tt, LANE), lambda i: (i, 0)),
                   pl.BlockSpec((SUBLANE, LANE), lambda i: (0, 0))],
        out_shape=[jax.ShapeDtypeStruct((t, cw), F32), jax.ShapeDtypeStruct((t, LANE), F32),
                   jax.ShapeDtypeStruct((SUBLANE, LANE), F32)],
        compiler_params=_cp("arbitrary"))(conv, h, par, dq, dk, dv, dbg, dbgt)


def _conv_bwd(dc, h, conv_w, dh, *, tt, name):
    t = dc.shape[0]
    cw = 3 * A_WIDTH
    nb = t // tt

    def body(dc_ref, after_ref, pre_ref, cw_ref, dh_in_ref, dpre_ref, dcw_ref):
        i = pl.program_id(0)

        @pl.when(i == 0)
        def _():
            dcw_ref[...] = jnp.zeros_like(dcw_ref)

        dcv = dc_ref[...]
        after = jnp.where(i < nb - 1, after_ref[...], 0.0)
        cur = pre_ref[...].astype(F32)
        w = cw_ref[...]
        acc = dcv * w[CONV_K - 1:CONV_K, :]
        dcw_ref[CONV_K - 1:CONV_K, :] += jnp.sum(dcv * cur, axis=0, keepdims=True)
        for s in range(1, CONV_K):
            j = CONV_K - 1 - s
            up = _shift_up(dcv, after, s)
            acc = acc + up * w[j:j + 1, :]
            dcw_ref[j:j + 1, :] += jnp.sum(up * cur, axis=0, keepdims=True)
        dpre_ref[...] = acc.astype(dpre_ref.dtype)

    return pl.pallas_call(
        body, name=name, grid=(nb,),
        in_specs=[pl.BlockSpec((tt, cw), lambda i: (i, 0)),
                  pl.BlockSpec((SUBLANE, cw), lambda i: (jnp.minimum((i + 1) * (tt // SUBLANE), t // SUBLANE - 1), 0)),
                  pl.BlockSpec((tt, cw), lambda i: (i, 0)),
                  pl.BlockSpec((CONV_K, cw), lambda i: (0, 0)), _ANY],
        out_specs=[pl.BlockSpec((tt, cw), lambda i: (i, 0)), pl.BlockSpec((SUBLANE, cw), lambda i: (0, 0))],
        out_shape=[jax.ShapeDtypeStruct(dh.shape, dh.dtype), jax.ShapeDtypeStruct((SUBLANE, cw), F32)],
        input_output_aliases={4: 0},
        compiler_params=_cp("arbitrary"))(dc, dc, h, conv_w, dh)


def _swa_bwd(h, dm, ob, probs, sink_probs, dh, *, name, carry=None):
    t = h.shape[0]
    qspec, cur, prev = _swa_specs()
    c_ins, c_in_specs, c_out_specs, c_outs, c_scratch = _carry_specs(carry)

    def body(*refs):
        (q_ref, kc_ref, kp_ref, vc_ref, vp_ref, zb_ref, dy_ref, ob_ref, p_ref, ps_ref, dh_in_ref,
         dqz_ref, dk_ref, dv_ref, dsk_ref) = _carried(carry, refs, 11, 4, t // BLOCK)
        n_blk = pl.program_id(0)

        @pl.when(n_blk == 0)
        def _():
            dk_ref[...] = jnp.zeros_like(dk_ref)
            dv_ref[...] = jnp.zeros_like(dv_ref)
            dsk_ref[...] = jnp.zeros_like(dsk_ref)

        kp, kc, vp, vc = kp_ref[...], kc_ref[...], vp_ref[...], vc_ref[...]
        scale = B_HEAD_DIM ** -0.5
        hks = range(B_KV_HEADS)
        ksl = lambda hk: slice(hk * B_HEAD_DIM, (hk + 1) * B_HEAD_DIM)
        heads = lambda hk: range(hk * B_GROUP, (hk + 1) * B_GROUP)
        upper, _ = _swa_window()
        groups = [(_stack_heads(q_ref, hk) * scale,
                   jnp.concatenate([p_ref[:, h * BLOCK:(h + 1) * BLOCK].astype(F32) for h in heads(hk)], axis=0),
                   jnp.concatenate([ps_ref[:, h:h + 1] for h in heads(hk)], axis=0),
                   _stack_heads(ob_ref, hk)) for hk in hks]
        zbs = [_stack_heads(zb_ref, hk) for hk in hks]
        dys = [_stack_heads(dy_ref, hk) for hk in hks]
        gates = [_silu_and_grad(zbs[hk]) for hk in hks]
        dos = [dys[hk] * gates[hk][0] for hk in hks]
        deltas = [jnp.sum(dos[hk] * groups[hk][3], -1, keepdims=True) for hk in hks]
        dps = [jnp.where(upper, _dot_nt(dos[hk], vp[:, ksl(hk)]), _dot_nt(dos[hk], vc[:, ksl(hk)])) for hk in hks]
        dss = [groups[hk][1] * (dps[hk] - deltas[hk]) for hk in hks]
        ds_up = [jnp.where(upper, dss[hk], 0.0) for hk in hks]
        ds_lo = [dss[hk] - ds_up[hk] for hk in hks]
        p_up = [jnp.where(upper, groups[hk][1], 0.0) for hk in hks]
        p_lo = [groups[hk][1] - p_up[hk] for hk in hks]
        dqs = [(_dot(ds_up[hk], kp[:, ksl(hk)]) + _dot(ds_lo[hk], kc[:, ksl(hk)])) * scale for hk in hks]
        dk_prev = [_dot_tn(ds_up[hk], groups[hk][0]) for hk in hks]
        dk_cur = [_dot_tn(ds_lo[hk], groups[hk][0]) for hk in hks]
        dv_prev = [_dot_tn(p_up[hk], dos[hk]) for hk in hks]
        dv_cur = [_dot_tn(p_lo[hk], dos[hk]) for hk in hks]
        for hk in hks:
            dzb = dys[hk] * groups[hk][3] * gates[hk][1]
            dsink = groups[hk][2] * deltas[hk]
            for g in range(B_GROUP):
                hq = hk * B_GROUP + g
                rows = slice(g * BLOCK, (g + 1) * BLOCK)
                qsl = slice(hq * B_HEAD_DIM, (hq + 1) * B_HEAD_DIM)
                dqz_ref[:, qsl] = dqs[hk][rows].astype(dqz_ref.dtype)
                dqz_ref[:, B_WIDTH + hq * B_HEAD_DIM:B_WIDTH + (hq + 1) * B_HEAD_DIM] = dzb[rows].astype(dqz_ref.dtype)
                dsk_ref[hq:hq + 1, :] += -jnp.sum(dsink[rows], keepdims=True)
        at_cur = pl.ds(pl.multiple_of(n_blk * BLOCK, BLOCK), BLOCK)
        at_prev = pl.ds(pl.multiple_of(jnp.maximum(n_blk - 1, 0) * BLOCK, BLOCK), BLOCK)
        dk_ref[at_prev, :] += jnp.concatenate(dk_prev, axis=1)
        dv_ref[at_prev, :] += jnp.concatenate(dv_prev, axis=1)
        dk_ref[at_cur, :] += jnp.concatenate(dk_cur, axis=1)
        dv_ref[at_cur, :] += jnp.concatenate(dv_cur, axis=1)

    narrow = jax.ShapeDtypeStruct((t, B_KV_WIDTH), F32)
    res = lambda a, b: pl.BlockSpec((a, b), lambda i: (0, 0))
    row = lambda w: pl.BlockSpec((BLOCK, w), lambda i: (i, 0))
    outs = pl.pallas_call(
        body, name=name, grid=(t // BLOCK,),
        in_specs=[qspec(C_QB), cur(C_KB), prev(C_KB), cur(C_VB), prev(C_VB), qspec(C_ZB),
                  pl.BlockSpec((BLOCK, B_WIDTH), lambda i: (i, 1)), row(B_WIDTH), row(B_Q_HEADS * BLOCK), row(LANE),
                  _ANY] + c_in_specs,
        out_specs=[pl.BlockSpec((BLOCK, 2 * B_WIDTH), lambda i: (i, C_QB // (2 * B_WIDTH))),
                   res(t, B_KV_WIDTH), res(t, B_KV_WIDTH), res(B_Q_HEADS, LANE)] + c_out_specs,
        out_shape=[jax.ShapeDtypeStruct(dh.shape, dh.dtype), narrow, narrow,
                   jax.ShapeDtypeStruct((B_Q_HEADS, LANE), F32)] + c_outs,
        scratch_shapes=c_scratch,
        input_output_aliases={10: 0},
        compiler_params=_cp("arbitrary"))(h, h, h, h, h, h, dm, ob, probs, sink_probs, dh, *c_ins)
    return outs[:4], outs[4:]


def _in_proj_dw(dh_main, dh_tail, x, *, tk, name):
    t, n = x.shape

    def body(a_ref, t_ref, x_ref, o_ref, ot_ref):
        @pl.when(pl.program_id(0) == 0)
        def _():
            o_ref[...] = jnp.zeros_like(o_ref)
            ot_ref[...] = jnp.zeros_like(ot_ref)

        xb = x_ref[...].astype(BF16)
        o_ref[...] += _dot_tn(a_ref[...], xb)
        ot_ref[...] += _dot_tn(t_ref[...], xb)

    row = lambda a: pl.BlockSpec((tk, a.shape[1]), lambda kk: (kk, 0))
    acc = lambda a: pl.BlockSpec((a.shape[1], n), lambda kk: (0, 0))
    return pl.pallas_call(
        body, name=name, grid=(t // tk,), in_specs=[row(dh_main), row(dh_tail), row(x)],
        out_specs=[acc(dh_main), acc(dh_tail)],
        out_shape=[jax.ShapeDtypeStruct((a.shape[1], n), F32) for a in (dh_main, dh_tail)],
        compiler_params=_cp("arbitrary"))(dh_main, dh_tail, x)


def _in_proj_dx(dh_main, dh_tail, wt, dr, *, tm, name, carry=None):
    t, n_main = dh_main.shape
    n_tail = dh_tail.shape[1]
    c_ins, c_in_specs, c_out_specs, c_outs, c_scratch = _carry_specs(carry)

    def body(*refs):
        a_ref, t_ref, wa_ref, wt_ref, r_ref, o_ref = _carried(carry, refs, 5, 1, t // tm)
        o_ref[...] = _dot(a_ref[...], wa_ref[...]) + _dot(t_ref[...], wt_ref[...]) + DEEPNORM_ALPHA * r_ref[...]

    row = lambda w: pl.BlockSpec((tm, w), lambda i: (i, 0))
    outs = pl.pallas_call(
        body, name=name, grid=(t // tm,),
        in_specs=[row(n_main), row(n_tail), pl.BlockSpec((n_main, D_MODEL), lambda i: (0, 0)),
                  pl.BlockSpec((n_tail, D_MODEL), lambda i: (n_main // n_tail, 0)), row(D_MODEL)] + c_in_specs,
        out_specs=[row(D_MODEL)] + c_out_specs,
        out_shape=[jax.ShapeDtypeStruct((t, D_MODEL), F32)] + c_outs,
        scratch_shapes=c_scratch,
        compiler_params=_cp("arbitrary"))(dh_main, dh_tail, wt, wt, dr, *c_ins)
    return outs[0], outs[1:]


def _layer_bwd(dxn, res, wt, conv_w, par, sinks_b, norm_w, w_out_bf, ln_g, l, carries=None, carry_dx=None):
    carries = carries or {}
    w_out_bf = res["w_out"]
    dr, dm, dw_out, dln_g, dln_b = _ln_out_bwd(dxn, res["r"], res["mixed"], ln_g, w_out_bf, tm=512, name=f"ln_out_bwd_{l}")
    h = res["h"]
    do, dh, dnw = _dn_post_bwd(dm, res["oa"], h, norm_w, tm=512, name=f"dn_post_bwd_{l}")
    dvn, ds_all = _dn_scan_bwd(res["q"], res["k"], res["w"], res["qk"], res["bg"], do, name=f"dn_scan_bwd_{l}")
    (dq, dk, dv, dbg, dbgt), got_chunk = _dn_chunk_bwd(
        res["q"], res["k"], res["v"], res["vn"], res["tmat"], res["qk"], res["bg"], res["bgt"], res["s_all"], ds_all,
        dvn, do, name=f"dn_chunk_bwd_{l}", carry=carries.get("dn_chunk"))
    dc, dbgi, dpar = _dn_pre_bwd(h, res["conv"], par, dq, dk, dv, dbg, dbgt, tt=512, name=f"dn_pre_bwd_{l}")
    dh, dcw = _conv_bwd(dc, h, conv_w, dh, tt=512, name=f"conv_bwd_{l}")
    (dh, dkb, dvb, dsk), got_swa = _swa_bwd(h, dm, res["ob"], res["swa_p"], res["swa_ps"], dh, name=f"swa_bwd_{l}",
                                            carry=carries.get("swa"))
    carried = dict(dn_chunk=got_chunk, swa=got_swa)
    dh_tail = jnp.concatenate([dkb, dvb, dbgi], axis=1).astype(BF16)
    dwt_main, dwt_tail = _in_proj_dw(dh, dh_tail, res["x"], tk=512, name=f"in_proj_dw_{l}")
    grads = dict(w_in=(dwt_main, dwt_tail), conv_w=dcw[:CONV_K], a_log=dpar[0, A_HEADS:2 * A_HEADS],
                 dt_bias=dpar[1, A_HEADS:2 * A_HEADS], norm_w=dnw[0], sinks=dsk[:, 0], w_out=dw_out,
                 ln_g=dln_g[0], ln_b=dln_b[0])
    dx, carried_dx = _in_proj_dx(dh, dh_tail, wt, dr, tm=512, name=f"in_proj_dx_{l}",
                                 carry=None if carry_dx is None else carry_dx(grads))
    return dx, grads, carried, carried_dx


def _layer_args(wt, conv_w, a_log, dt_bias, sinks, norm_w, w_out_bf):
    return (wt, conv_w, _gate_params(a_log, dt_bias), jnp.broadcast_to(sinks[:, None], (B_Q_HEADS, LANE)),
            norm_w[None], w_out_bf)


def _local_step(x, target, args0, args1, ln_g, ln_b, gathers=None, reduce1=None, reduce0=None):
    assert DEPTH == 2
    x1, res0, got = _layer_fwd(x, *args0, ln_g[0][None], ln_b[0][None], 0, carries=gathers)
    if gathers is not None:
        args1 = args1(got)
    (dx, loss_tile), res1, _ = _layer_fwd(x1, *args1, ln_g[1][None], ln_b[1][None], 1, target=target)
    dx, grads1, _, _ = _layer_bwd(dx, res1, *args1, ln_g[1][None], 1)
    carries = None if reduce1 is None else reduce1(grads1)
    carry_dx = None if reduce0 is None else (lambda grads0: reduce0(grads0, grads1, loss_tile))
    dx, grads0, landed1, landed0 = _layer_bwd(dx, res0, *args0, ln_g[0][None], 0, carries=carries, carry_dx=carry_dx)
    return loss_tile, dx, [grads0, grads1], landed1, landed0


_ANY = pl.BlockSpec(memory_space=pl.ANY)
_MESH = pl.DeviceIdType.MESH


HALF = D_MODEL // 2


class _Exchange:
    def __init__(self, ins, outs, n_remote, n_local, plan):
        self.ins, self.outs, self.n_remote, self.n_local, self.plan = tuple(ins), tuple(outs), n_remote, n_local, plan

    def scratch(self):
        return [pltpu.SemaphoreType.DMA((self.n_remote,)), pltpu.SemaphoreType.DMA((self.n_remote,)),
                pltpu.SemaphoreType.DMA((max(self.n_local, 1),))]

    def _copies(self, in_refs, out_refs, sems, arriving):
        send_sems, recv_sems, local_sems = sems
        local, sends, recvs = self.plan(in_refs, out_refs)
        loc = [pltpu.make_async_copy(s, d, local_sems.at[i]) for i, (s, d) in enumerate(local)]
        rem = [pltpu.make_async_remote_copy(src_ref=s, dst_ref=recvs[i] if arriving else d, send_sem=send_sems.at[i],
                                            recv_sem=recv_sems.at[i], device_id=peer, device_id_type=_MESH)
               for i, (s, d, peer) in enumerate(sends)]
        return loc, rem

    def start(self, in_refs, out_refs, sems):
        loc, rem = self._copies(in_refs, out_refs, sems, arriving=False)
        for cp in loc + rem:
            cp.start()

    def finish(self, in_refs, out_refs, sems):
        loc, rem = self._copies(in_refs, out_refs, sems, arriving=True)
        for cp in rem:
            cp.wait_recv()
        for cp in rem:
            cp.wait_send()
        for cp in loc:
            cp.wait()


def _run_exchange(ex, *, name):
    n_in, n_out = len(ex.ins), len(ex.outs)

    def body(*refs):
        parts = refs[:n_in], refs[n_in:n_in + n_out], refs[n_in + n_out:]
        ex.start(*parts)
        ex.finish(*parts)

    return pl.pallas_call(body, name=name, in_specs=[_ANY] * n_in, out_specs=[_ANY] * n_out, out_shape=list(ex.outs),
                          scratch_shapes=ex.scratch())(*ex.ins)


def _place():
    x, y, c = lax.axis_index("x"), lax.axis_index("y"), lax.axis_index("c")
    return x, y, c, [(1 - x, y), (x, 1 - y), (1 - x, 1 - y)]


def _gather_exchange(arrays):
    n = len(arrays)

    def plan(src, dst):
        x, y, c, chips = _place()
        me = 2 * x + y
        local = [(src[k], dst[k].at[me]) for k in range(n)]
        sends = [(src[k], dst[k].at[me], (px, py, c)) for k in range(n) for px, py in chips]
        recvs = [dst[k].at[2 * px + py] for k in range(n) for px, py in chips]
        return local, sends, recvs

    return _Exchange(arrays, [jax.ShapeDtypeStruct((N_SHARD,) + a.shape, a.dtype) for a in arrays], 3 * n, n, plan)


def _gather_two_level(pack, conv_w, *, name):
    rows = pack.shape[0]
    part_rows = rows // 2

    def body(pack_ref, conv_ref, land_ref, conv_land_ref, send1, recv1, send2, recv2, csend, crecv, local_sems):
        x, y, c, chips = _place()
        me = 2 * x + y
        sibling = (x, y, 1 - c)
        part = lambda core: pl.ds(pl.multiple_of(core * part_rows, 16), part_rows)
        remote = lambda src, dst, ss, rs, to: pltpu.make_async_remote_copy(
            src_ref=src, dst_ref=dst, send_sem=ss, recv_sem=rs, device_id=to, device_id_type=_MESH)
        local = [pltpu.make_async_copy(pack_ref, land_ref.at[me], local_sems.at[0]),
                 pltpu.make_async_copy(conv_ref, conv_land_ref.at[me], local_sems.at[1])]
        for cp in local:
            cp.start()
        first = [remote(pack_ref.at[part(c)], land_ref.at[me, part(c)], send1.at[j], recv1.at[j], (px, py, c))
                 for j, (px, py) in enumerate(chips)]
        convs = [remote(conv_ref, conv_land_ref.at[me], csend.at[j], crecv.at[j], (px, py, c))
                 for j, (px, py) in enumerate(chips)]
        for cp in first + convs:
            cp.start()
        passed = []
        for j, (px, py) in enumerate(chips):
            slot = 2 * px + py
            remote(pack_ref.at[part(c)], land_ref.at[slot, part(c)], send1.at[j], recv1.at[j], (px, py, c)).wait_recv()
            cp = remote(land_ref.at[slot, part(c)], land_ref.at[slot, part(c)], send2.at[j], recv2.at[j], sibling)
            cp.start()
            passed.append(cp)
        for j, (px, py) in enumerate(chips):
            slot = 2 * px + py
            remote(land_ref.at[slot, part(1 - c)], land_ref.at[slot, part(1 - c)], send2.at[j], recv2.at[j],
                   sibling).wait_recv()
            remote(conv_ref, conv_land_ref.at[slot], csend.at[j], crecv.at[j], (px, py, c)).wait_recv()
        for cp in first + convs + passed:
            cp.wait_send()
        for cp in local:
            cp.wait()

    sems = [pltpu.SemaphoreType.DMA((3,))] * 6 + [pltpu.SemaphoreType.DMA((2,))]
    return pl.pallas_call(
        body, name=name, in_specs=[_ANY, _ANY], out_specs=[_ANY, _ANY],
        out_shape=[jax.ShapeDtypeStruct((N_SHARD,) + pack.shape, pack.dtype),
                   jax.ShapeDtypeStruct((N_SHARD,) + conv_w.shape, conv_w.dtype)],
        scratch_shapes=sems)(pack, conv_w)


def _half(core):
    return pl.ds(pl.multiple_of(core * HALF, HALF), HALF)


def _reduce_scatter_exchange(g, row0, rows):
    def plan(src, dst):
        x, y, c, chips = _place()
        peers = [(px, py, c if t == 0 else 1 - c) for px, py in chips for t in (0, 1)] + [(x, y, 1 - c)]
        sends = [(src[0].at[2 * px + py, pl.ds(row0, rows), _half(pc)], dst[0].at[k], (px, py, pc))
                 for k, (px, py, pc) in enumerate(peers)]
        return [], sends, [dst[0].at[k] for k in range(7)]

    return _Exchange([g], [jax.ShapeDtypeStruct((7, rows, HALF), g.dtype)], 7, 0, plan)


def _pair_window_exchange(g):
    def plan(src, dst):
        x, y, c, _ = _place()
        return [], [(src[0].at[:, :, _half(1 - c)], dst[0], (x, y, 1 - c))], [dst[0]]

    return _Exchange([g], [jax.ShapeDtypeStruct(g.shape[:2] + (HALF,), g.dtype)], 1, 0, plan)


def _chip_scatter_exchange(p, small):
    def plan(src, dst):
        x, y, c, chips = _place()
        mine = 4 * x + 2 * y + c
        peers = [(px, py, c if t == 0 else 1 - c) for px, py in chips for t in (0, 1)] + [(x, y, 1 - c)]
        sends = [(src[0].at[2 * px + py], dst[0].at[j], (px, py, c)) for j, (px, py) in enumerate(chips)]
        recvs = [dst[0].at[j] for j in range(3)]
        sends += [(src[1], dst[1].at[mine], peer) for peer in peers]
        recvs += [dst[1].at[4 * px + 2 * py + pc] for px, py, pc in peers]
        return [(src[1], dst[1].at[mine])], sends, recvs

    outs = [jax.ShapeDtypeStruct((3,) + p.shape[1:], p.dtype), jax.ShapeDtypeStruct((8,) + small.shape, small.dtype)]
    return _Exchange([p, small], outs, 10, 1, plan)


def _share_exchange(arrays):
    n = len(arrays)

    def plan(src, dst):
        x, y, c, _ = _place()
        return [], [(src[k], dst[k], (x, y, 1 - c)) for k in range(n)], [dst[k] for k in range(n)]

    return _Exchange(arrays, [jax.ShapeDtypeStruct(a.shape, a.dtype) for a in arrays], n, 0, plan)


def _sum_scatter(g, lands, me, core, *, tc, name):
    rows = g.shape[1]
    per = HALF // tc
    n = len(lands)

    def body(*refs):
        g_ref, land_refs, o_ref = refs[1], refs[2:2 + n], refs[2 + n]
        at = 0
        for land_ref in land_refs:
            run = slice(at, at + land_ref.shape[1])
            acc = g_ref[run, :].astype(F32)
            for k in range(7):
                acc = acc + land_ref[k].astype(F32)
            o_ref[run, :] = acc
            at = run.stop

    return pl.pallas_call(
        body, name=name, out_shape=jax.ShapeDtypeStruct((rows, HALF), F32), compiler_params=_cp("parallel"),
        grid_spec=pltpu.PrefetchScalarGridSpec(
            num_scalar_prefetch=1, grid=(per,),
            in_specs=[pl.BlockSpec((None, rows, tc), lambda i, w: (w[0], 0, w[1] * per + i))]
            + [pl.BlockSpec((7, a.shape[1], tc), lambda i, w: (0, 0, i)) for a in lands],
            out_specs=pl.BlockSpec((rows, tc), lambda i, w: (0, i))))(
        jnp.stack([me, core]).astype(jnp.int32), g, *lands)


def _pair_add(g, land, core, *, name):
    n, rows, _ = g.shape

    def body(core_ref, g_ref, land_ref, o_ref):
        o_ref[...] = (g_ref[...].astype(F32) + land_ref[...].astype(F32)).astype(o_ref.dtype)

    blk = pl.BlockSpec((1, rows, HALF), lambda i, w: (i, 0, 0))
    return pl.pallas_call(
        body, name=name, out_shape=jax.ShapeDtypeStruct((n, rows, HALF), g.dtype), compiler_params=_cp("parallel"),
        grid_spec=pltpu.PrefetchScalarGridSpec(
            num_scalar_prefetch=1, grid=(n,),
            in_specs=[pl.BlockSpec((1, rows, HALF), lambda i, w: (i, 0, w[0])), blk], out_specs=blk))(
        jnp.reshape(core, (1,)).astype(jnp.int32), g, land)


def _sum_chips(p, land, me, *, tc, name):
    rows = p.shape[1]

    def body(me_ref, p_ref, land_ref, o_ref):
        acc = p_ref[...].astype(F32)
        for k in range(3):
            acc = acc + land_ref[k].astype(F32)
        o_ref[...] = acc

    return pl.pallas_call(
        body, name=name, out_shape=jax.ShapeDtypeStruct((rows, HALF), F32), compiler_params=_cp("parallel"),
        grid_spec=pltpu.PrefetchScalarGridSpec(
            num_scalar_prefetch=1, grid=(HALF // tc,),
            in_specs=[pl.BlockSpec((None, rows, tc), lambda i, w: (w[0], 0, i)),
                      pl.BlockSpec((3, rows, tc), lambda i, w: (0, 0, i))],
            out_specs=pl.BlockSpec((rows, tc), lambda i, w: (0, i))))(
        jnp.reshape(me, (1,)).astype(jnp.int32), p, land)


def _sum_slots(a, *, name):
    n = a.shape[0]

    def body(a_ref, o_ref):
        acc = a_ref[0]
        for k in range(1, n):
            acc = acc + a_ref[k]
        o_ref[...] = acc

    return pl.pallas_call(body, name=name, out_shape=jax.ShapeDtypeStruct(a.shape[1:], a.dtype))(a)


def _elementwise(fn, ins, n_out, block, *, name):
    shape = ins[0].shape
    grid = tuple(s // b for s, b in zip(shape, block))
    n_in = len(ins)

    def body(*refs):
        outs = fn(*[r[...] for r in refs[:n_in]])
        for o_ref, val in zip(refs[n_in:], outs):
            o_ref[...] = val

    spec = pl.BlockSpec(block, lambda i, j, k: (i, j, k))
    return pl.pallas_call(body, name=name, grid=grid, in_specs=[spec] * n_in, out_specs=[spec] * n_out,
                          out_shape=[jax.ShapeDtypeStruct(shape, F32)] * n_out,
                          compiler_params=_cp(*["parallel"] * 3))(*ins)


def _adamw_math(w, g, m, v):
    mn = ADAM_B1 * m + (1.0 - ADAM_B1) * g
    vn = ADAM_B2 * v + (1.0 - ADAM_B2) * (g * g)
    m_hat = mn / (1.0 - ADAM_B1 ** ADAM_STEP)
    v_hat = vn / (1.0 - ADAM_B2 ** ADAM_STEP)
    return -ADAM_LR * (m_hat / (jnp.sqrt(v_hat) + ADAM_EPS) + ADAM_WD * w), mn, vn


def _adamw(w, g, m, v, block, *, name):
    return _elementwise(_adamw_math, [w, g, m, v], 3, block, name=name)


def _interleave_layers(layers, *, tc, name):
    rows, cols = layers[0].shape
    n = len(layers)

    def body(*refs):
        for l in range(n):
            refs[n][:, l, :] = refs[l][...]

    return pl.pallas_call(body, name=name, grid=(cols // tc,),
                          in_specs=[pl.BlockSpec((rows, tc), lambda i: (0, i))] * n,
                          out_specs=pl.BlockSpec((rows, n, tc), lambda i: (0, 0, i)),
                          out_shape=jax.ShapeDtypeStruct((rows, n, cols), layers[0].dtype),
                          compiler_params=_cp("parallel"))(*layers)


def _adamw_small(ws, gs, ms, vs, *, name):
    n = len(ws)

    def body(*refs):
        w, g, m, v, outs = refs[:n], refs[n:2 * n], refs[2 * n:3 * n], refs[3 * n:4 * n], refs[4 * n:]
        for k in range(n):
            for slot, val in enumerate(_adamw_math(w[k][...], g[k][...], m[k][...], v[k][...])):
                outs[slot * n + k][...] = val

    outs = pl.pallas_call(body, name=name, out_shape=[jax.ShapeDtypeStruct(a.shape, F32) for a in ws] * 3)(
        *ws, *gs, *ms, *vs)
    return outs[:n], outs[n:2 * n], outs[2 * n:]


def _to_kernel_order(wt):
    gates = jnp.pad(wt[2048:2056], ((0, LANE - 2 * A_HEADS), (0, 0)))
    return jnp.concatenate([wt[0:2048], wt[2056:2568], wt[2824:3336], wt[2568:2696], wt[2696:2824], gates], axis=0)


def _from_kernel_order(main, tail):
    return jnp.concatenate([main[0:2048], tail[C_BG - DH_MAIN:C_BG - DH_MAIN + 2 * A_HEADS],
                            main[C_QB:C_QB + B_WIDTH], tail[0:B_KV_WIDTH], tail[B_KV_WIDTH:2 * B_KV_WIDTH],
                            main[C_ZB:C_ZB + B_WIDTH]], axis=0)


def _gate_params(a_log, dt_bias):
    return jnp.pad(jnp.stack([a_log, dt_bias]), ((0, SUBLANE - 2), (A_HEADS, LANE - 2 * A_HEADS)))


SMALL = ("conv_w", "a_log", "dt_bias", "norm_w", "sinks", "ln_g", "ln_b")


def _pack(parts, cols):
    flat = jnp.concatenate([p.reshape(-1) for p in parts])
    rows = -(-flat.shape[0] // cols)
    return jnp.pad(flat, (0, rows * cols - flat.shape[0])).reshape(rows, cols)


def _unpack(packed, shapes):
    flat = packed.reshape(-1)
    out, at = [], 0
    for s in shapes:
        n = math.prod(s)
        out.append(flat[at:at + n].reshape(s))
        at += n
    return out


def kernel(x, w_in, conv_w, a_log, dt_bias, norm_w, sinks, w_out, ln_g, ln_b, loss_target, m_w_in, m_conv_w, m_a_log, m_dt_bias, m_norm_w, m_sinks, m_w_out, m_ln_g, m_ln_b, v_w_in, v_conv_w, v_a_log, v_dt_bias, v_norm_w, v_sinks, v_w_out, v_ln_g, v_ln_b):
    xi, yi, ci = lax.axis_index("x"), lax.axis_index("y"), lax.axis_index("c")
    me = 2 * xi + yi

    to_t = lambda a: jnp.transpose(a, (2, 0, 1))
    from_t = lambda a: jnp.transpose(a, (1, 2, 0))

    wt_shard = to_t(w_in)

    def pack_weights(l):
        rows = jnp.pad(wt_shard[:, l], ((0, IN_PAD - IN_SHARD), (0, 0)))
        return jnp.concatenate([rows, w_out[l]], axis=0).astype(BF16)

    pack0, pack1 = pack_weights(0), pack_weights(1)
    got_in0, g_conv = _gather_two_level(pack0[:IN_PAD], conv_w, name="gather_weights_0")
    conv_full = jnp.moveaxis(g_conv, 0, 2).reshape(DEPTH, CONV_K, 3 * A_WIDTH)
    carriers = ("dn_pre", "dn_wy", "dn_scan")
    cuts = (0, 288, 624, IN_PAD)
    gathers = {nm: _gather_exchange([pack1[cuts[i]:cuts[i + 1]]]) for i, nm in enumerate(carriers)}
    gathers.update(in_proj=_gather_exchange([pack0[IN_PAD:]]), swa=_gather_exchange([pack1[IN_PAD:]]))
    w_in_of = lambda rows: _to_kernel_order(rows[:, :IN_SHARD].reshape(IN_COLS, D_MODEL))
    w_out_of = lambda rows: rows.reshape(D_MODEL, D_MODEL)
    args0 = _layer_args(w_in_of(got_in0), conv_full[0], a_log[0], dt_bias[0], sinks[0], norm_w[0],
                        lambda got: w_out_of(got[0]))

    def args1(got):
        rows = jnp.concatenate([got[nm][0] for nm in carriers], axis=1)
        return _layer_args(w_in_of(rows), conv_full[1], a_log[1], dt_bias[1], sinks[1], norm_w[1],
                           w_out_of(got["swa"][0]))

    def pack_grads(g):
        gin = _from_kernel_order(*g["w_in"]).reshape(N_SHARD, IN_SHARD, D_MODEL)
        gin = jnp.pad(gin, ((0, 0), (0, IN_PAD - IN_SHARD), (0, 0)))
        return jnp.concatenate([gin, g["w_out"].reshape(N_SHARD, OUT_SHARD, D_MODEL)], axis=1).astype(BF16)

    packed = {}

    def reduce1(grads1):
        packed[1] = pack_grads(grads1)
        half_rows = packed[1].shape[1] // 2
        return dict(dn_chunk=_reduce_scatter_exchange(packed[1], 0, half_rows),
                    swa=_reduce_scatter_exchange(packed[1], half_rows, half_rows))

    def reduce0(grads0, grads1, loss_tile):
        g0 = pack_grads(grads0)
        from_sibling = _run_exchange(_pair_window_exchange(g0), name="pair_reduce_0")[0]
        packed[0] = _pair_add(g0, from_sibling, ci, name="pair_add_0")
        gsmall = _pack([jnp.stack([g[nm] for g in (grads0, grads1)]) for nm in SMALL] + [loss_tile[0, 0:1]], D_MODEL)
        return _chip_scatter_exchange(packed[0], gsmall)

    _, dx, grads, landed1, (landed0, landed_small) = _local_step(
        x[0], loss_target[0], args0, args1, ln_g, ln_b, gathers=gathers, reduce1=reduce1, reduce0=reduce0)

    small_shapes = [(DEPTH,) + grads[0][nm].shape for nm in SMALL]
    halves = [_sum_chips(packed[0], landed0, me, tc=2 * LANE, name="reduce_sum_0"),
              _sum_scatter(packed[1], [landed1["dn_chunk"][0], landed1["swa"][0]], me, ci, tc=2 * LANE,
                           name="reduce_sum_1")]
    s_small = _sum_slots(landed_small, name="reduce_sum_small")
    others = _run_exchange(_share_exchange(halves), name="pair_share")
    full = [jnp.where(ci == 0, jnp.concatenate([mine, other], axis=1), jnp.concatenate([other, mine], axis=1))
            for mine, other in zip(halves, others)]
    grad_in_layers = [f[:IN_SHARD] for f in full]
    grad_out = jnp.stack([f[IN_PAD:] for f in full])
    out_blk = (1, OUT_SHARD, D_MODEL)
    *small_grads, loss = _unpack(s_small, small_shapes + [()])
    gs = dict(zip(SMALL, small_grads))
    gs["conv_w"] = lax.dynamic_slice_in_dim(gs["conv_w"], me * CONV_SHARD, CONV_SHARD, axis=2)

    grad_in_t = _interleave_layers(grad_in_layers, tc=2 * LANE, name="grad_in_layers")
    d_in, nm_in, nv_in = (from_t(o) for o in _adamw(to_t(w_in), grad_in_t, to_t(m_w_in), to_t(v_w_in),
                                                    (IN_SHARD // 6, DEPTH, D_MODEL), name="adamw_in"))
    grad_in = from_t(grad_in_t)
    d_out, nm_out, nv_out = _adamw(w_out, grad_out, m_w_out, v_w_out, out_blk, name="adamw_out")
    ws = dict(conv_w=conv_w, a_log=a_log, dt_bias=dt_bias, norm_w=norm_w, sinks=sinks, ln_g=ln_g, ln_b=ln_b)
    ms = dict(conv_w=m_conv_w, a_log=m_a_log, dt_bias=m_dt_bias, norm_w=m_norm_w, sinks=m_sinks, ln_g=m_ln_g, ln_b=m_ln_b)
    vs = dict(conv_w=v_conv_w, a_log=v_a_log, dt_bias=v_dt_bias, norm_w=v_norm_w, sinks=v_sinks, ln_g=v_ln_g, ln_b=v_ln_b)
    d_s, nm_s, nv_s = (dict(zip(SMALL, o)) for o in _adamw_small(*[[d[nm] for nm in SMALL] for d in (ws, gs, ms, vs)],
                                                                 name="adamw_small"))

    def in_order(big_in, small, big_out):
        return (big_in, small["conv_w"], small["a_log"], small["dt_bias"], small["norm_w"], small["sinks"], big_out,
                small["ln_g"], small["ln_b"])

    return (loss, dx[None], *in_order(grad_in, gs, grad_out), *in_order(d_in, d_s, d_out),
            *in_order(nm_in, nm_s, nm_out), *in_order(nv_in, nv_s, nv_out))
```

```python
import math

import jax
import jax.numpy as jnp
from jax import lax
from jax.experimental import pallas as pl
from jax.experimental.pallas import tpu as pltpu

F32 = jnp.float32
BF16 = jnp.bfloat16
HI = lax.Precision.HIGHEST

D_MODEL = 1024
DEPTH = 2
A_HEADS = 4
A_HEAD_DIM = 128
A_WIDTH = 512
CONV_K = 4
CHUNK = 64
B_Q_HEADS = 8
B_KV_HEADS = 2
B_HEAD_DIM = 64
B_GROUP = 4
B_WIDTH = 512
B_KV_WIDTH = 128
BLOCK = 128
IN_COLS = 3336
DEEPNORM_ALPHA = (2 * DEPTH) ** 0.25
LN_EPS = 1e-5
RMS_EPS = 1e-6
L2_EPS = 1e-6
ADAM_LR = 0.001
ADAM_B1 = 0.9
ADAM_B2 = 0.999
ADAM_EPS = 1e-08
ADAM_WD = 0.01
ADAM_STEP = 10

N_SHARD = 4
IN_SHARD = IN_COLS // N_SHARD
OUT_SHARD = D_MODEL // N_SHARD
CONV_SHARD = 3 * A_WIDTH // N_SHARD
IN_PAD = -(-IN_SHARD // 96) * 96

P_COLS = 3456
C_PRE = 0
C_ZA = 1536
C_QB = 2048
C_ZB = 2560
C_KB = 3072
C_VB = 3200
C_BG = 3328
DH_MAIN = C_KB
LANE = 128
SUBLANE = 8
HALO = 16
VMEM_LIMIT = 56 * 1024 * 1024
ALIBI = tuple(2.0 ** (-8.0 * (h + 1) / B_Q_HEADS) for h in range(B_Q_HEADS))
NEG = -1e30


def _cp(*sem):
    return pltpu.CompilerParams(dimension_semantics=sem, vmem_limit_bytes=VMEM_LIMIT)


def _dot(a, b):
    return jnp.dot(a.astype(BF16), b.astype(BF16), preferred_element_type=F32)


def _dot_nt(a, b):
    return lax.dot_general(a.astype(BF16), b.astype(BF16), (((1,), (1,)), ((), ())),
                           preferred_element_type=F32)


def _dot_tn(a, b):
    return lax.dot_general(a.astype(BF16), b.astype(BF16), (((0,), (0,)), ((), ())),
                           preferred_element_type=F32)


def _dot_hi(a, b):
    return jnp.dot(a, b, precision=HI, preferred_element_type=F32)


def _sigmoid(x):
    return jax.nn.sigmoid(x)


def _silu(x):
    return x * _sigmoid(x)


def _silu_and_grad(x):
    s = _sigmoid(x)
    return x * s, s * (1.0 + x * (1.0 - s))


def _softplus(x):
    return jnp.maximum(x, 0.0) + jnp.log(1.0 + jnp.exp(-jnp.abs(x)))


def _shift_down(cur, before, s):
    if s == 0:
        return cur
    r = pltpu.roll(cur, s, 0)
    rb = pltpu.roll(before, s, 0)
    row = lax.broadcasted_iota(jnp.int32, before.shape, 0)
    head = jnp.where(row < s, rb, r[0:SUBLANE])
    return jnp.concatenate([head, r[SUBLANE:]], axis=0)


def _shift_up(cur, after, s):
    if s == 0:
        return cur
    n = cur.shape[0]
    r = pltpu.roll(cur, n - s, 0)
    ra = pltpu.roll(after, SUBLANE - s, 0)
    row = lax.broadcasted_iota(jnp.int32, after.shape, 0)
    tail = jnp.where(row >= SUBLANE - s, ra, r[n - SUBLANE:])
    return jnp.concatenate([r[:n - SUBLANE], tail], axis=0)


def _conv_fwd(cur, before, w):
    acc = cur * w[CONV_K - 1:CONV_K, :]
    for s in range(1, CONV_K):
        acc = acc + _shift_down(cur, before, s) * w[CONV_K - 1 - s:CONV_K - s, :]
    return acc


def _matmul_nt(a, bt, *, tm, name, carry=None):
    m, k = a.shape
    n = bt.shape[0]
    c_ins, c_in_specs, c_out_specs, c_outs, c_scratch = _carry_specs(carry)

    def body(*refs):
        a_ref, b_ref, o_ref = _carried(carry, refs, 2, 1, m // tm)
        o_ref[...] = _dot_nt(a_ref[...], b_ref[...]).astype(o_ref.dtype)

    outs = pl.pallas_call(
        body, name=name, grid=(m // tm,),
        in_specs=[pl.BlockSpec((tm, k), lambda i: (i, 0)), pl.BlockSpec((n, k), lambda i: (0, 0))] + c_in_specs,
        out_specs=[pl.BlockSpec((tm, n), lambda i: (i, 0))] + c_out_specs,
        out_shape=[jax.ShapeDtypeStruct((m, n), BF16)] + c_outs,
        scratch_shapes=c_scratch,
        compiler_params=_cp("arbitrary"))(a, bt, *c_ins)
    return outs[0], outs[1:]


def _dn_pre(h, conv_w, par, *, tt, name, carry=None):
    t = h.shape[0]
    cw = 3 * A_WIDTH
    hb = tt // HALO

    c_ins, c_in_specs, c_out_specs, c_outs, c_scratch = _carry_specs(carry)

    def body(*refs):
        (pre_ref, halo_ref, bgi_ref, cw_ref, par_ref,
         q_ref, k_ref, v_ref, bg_ref, bgt_ref, c_ref) = _carried(carry, refs, 5, 6, t // tt)
        i = pl.program_id(0)
        cur = pre_ref[...].astype(F32)
        before = jnp.where(i > 0, halo_ref[...].astype(F32)[HALO - SUBLANE:], 0.0)
        conv = _conv_fwd(cur, before, cw_ref[...])
        c_ref[...] = conv
        s = _silu(conv)
        for hd in range(A_HEADS):
            sl = slice(hd * LANE, (hd + 1) * LANE)
            tq = s[:, hd * LANE:(hd + 1) * LANE]
            q_ref[:, sl] = tq * (lax.rsqrt(jnp.sum(tq * tq, -1, keepdims=True) + L2_EPS) * (A_HEAD_DIM ** -0.5))
            tk = s[:, A_WIDTH + hd * LANE:A_WIDTH + (hd + 1) * LANE]
            k_ref[:, sl] = tk * lax.rsqrt(jnp.sum(tk * tk, -1, keepdims=True) + L2_EPS)
        v_ref[...] = s[:, 2 * A_WIDTH:]
        raw = bgi_ref[...].astype(F32)
        lane = lax.broadcasted_iota(jnp.int32, raw.shape, 1)
        is_a = (lane >= A_HEADS) & (lane < 2 * A_HEADS)
        g = jnp.where(is_a, -jnp.exp(par_ref[0:1, :]) * _softplus(raw + par_ref[1:2, :]), 0.0)
        gc = _dot_hi(_chunk_tri(tt, lower=True), g)
        bg = jnp.where(lane < A_HEADS, _sigmoid(raw), gc)
        bg_ref[...] = bg
        bgt_ref[...] = jnp.transpose(bg)[0:SUBLANE, :]

    wide = jax.ShapeDtypeStruct((t, A_WIDTH), F32)
    outs = pl.pallas_call(
        body, name=name, grid=(t // tt,),
        in_specs=[pl.BlockSpec((tt, cw), lambda i: (i, 0)),
                  pl.BlockSpec((HALO, cw), lambda i: (jnp.maximum(i * hb - 1, 0), 0)),
                  pl.BlockSpec((tt, LANE), lambda i: (i, C_BG // LANE)),
                  pl.BlockSpec((CONV_K, cw), lambda i: (0, 0)),
                  pl.BlockSpec((SUBLANE, LANE), lambda i: (0, 0))] + c_in_specs,
        out_specs=[pl.BlockSpec((tt, A_WIDTH), lambda i: (i, 0))] * 3
        + [pl.BlockSpec((tt, LANE), lambda i: (i, 0)), pl.BlockSpec((SUBLANE, tt), lambda i: (0, i)),
           pl.BlockSpec((tt, cw), lambda i: (i, 0))] + c_out_specs,
        out_shape=[wide, wide, wide, jax.ShapeDtypeStruct((t, LANE), F32),
                   jax.ShapeDtypeStruct((SUBLANE, t), F32), jax.ShapeDtypeStruct((t, cw), F32)] + c_outs,
        scratch_shapes=c_scratch,
        compiler_params=_cp("arbitrary"))(h, h, h, conv_w, par, *c_ins)
    return outs[:6], outs[6:]


def _chunk_tri(n, lower):
    r = lax.broadcasted_iota(jnp.int32, (n, n), 0)
    c = lax.broadcasted_iota(jnp.int32, (n, n), 1)
    shift = CHUNK.bit_length() - 1
    same = jnp.right_shift(r, shift) == jnp.right_shift(c, shift)
    return (same & ((c <= r) if lower else (c >= r))).astype(F32)


def _chunk_masks():
    r = lax.broadcasted_iota(jnp.int32, (CHUNK, CHUNK), 0)
    c = lax.broadcasted_iota(jnp.int32, (CHUNK, CHUNK), 1)
    return r >= c, r > c, r == c


def _split(a):
    hi = a.astype(BF16)
    return hi, (a - hi.astype(F32)).astype(BF16)


def _dot3(a, b):
    (ah, al), (bh, bl) = a, b
    d = lambda p, q: jnp.dot(p, q, preferred_element_type=F32)
    return d(ah, bh) + (d(ah, bl) + d(al, bh))


def _tri_inv_many(a_list, eye):
    d = lambda p, q: jnp.dot(p.astype(BF16), q.astype(BF16), preferred_element_type=F32)
    r = lax.broadcasted_iota(jnp.int32, (CHUNK, CHUNK), 0)
    c = lax.broadcasted_iota(jnp.int32, (CHUNK, CHUNK), 1)
    same = lambda b: jnp.right_shift(r, b.bit_length() - 1) == jnp.right_shift(c, b.bit_length() - 1)
    x = [jnp.where(same(8), -a, 0.0) for a in a_list]
    tm = [eye + xi for xi in x]
    for _ in range(2):
        x = [d(xi, xi) for xi in x]
        tm = [t + d(t, xi) for t, xi in zip(tm, x)]
    for b in (16, 32, 64):
        low = [jnp.where(same(b) & ~same(b // 2), a, 0.0) for a in a_list]
        tm = [t - d(t, d(lo, t)) for t, lo in zip(tm, low)]
    res = [eye - _dot3(_split(eye + a), _split(t)) for a, t in zip(a_list, tm)]
    return [t + d(t, rs) for t, rs in zip(tm, res)]


def _chunk_gates(bg_v, bgt_v, hd):
    return (bg_v[:, hd:hd + 1], bg_v[:, A_HEADS + hd:A_HEADS + hd + 1],
            None if bgt_v is None else bgt_v[A_HEADS + hd:A_HEADS + hd + 1, :])


WY_ROWS = 512
SCAN_ROWS = 512
WY_GROUP = 8


def _dn_wy(q, k, v, bg, bgt, *, name, carry=None):
    t = q.shape[0]
    rows = WY_ROWS

    c_ins, c_in_specs, c_out_specs, c_outs, c_scratch = _carry_specs(carry)

    def body(*refs):
        q_ref, k_ref, v_ref, bg_ref, bgt_ref, u_ref, w_ref, tm_ref, qk_ref = _carried(carry, refs, 5, 4, t // rows)
        causal, strict, diag = _chunk_masks()
        eye = diag.astype(F32)
        for c0 in range(0, rows // CHUNK, WY_GROUP):
            items = [(c, hd) for c in range(c0, c0 + WY_GROUP) for hd in range(A_HEADS)]
            rs = lambda c: slice(c * CHUNK, (c + 1) * CHUNK)
            sl = lambda hd: slice(hd * LANE, (hd + 1) * LANE)
            hs = lambda hd: slice(hd * CHUNK, (hd + 1) * CHUNK)
            gates = [_chunk_gates(bg_ref[rs(c), :], bgt_ref[:, rs(c)], hd) for c, hd in items]
            dms = [jnp.exp(jnp.where(causal, gcol - grow, NEG)) for _, gcol, grow in gates]
            kbs = [k_ref[rs(c), sl(hd)] * g[0] for (c, hd), g in zip(items, gates)]
            a_list = [jnp.where(strict, _dot_nt(kb, k_ref[rs(c), sl(hd)]) * dm, 0.0)
                      for (c, hd), kb, dm in zip(items, kbs, dms)]
            for (c, hd), dm in zip(items, dms):
                qk_ref[rs(c), hs(hd)] = jnp.where(
                    causal, _dot_nt(q_ref[rs(c), sl(hd)], k_ref[rs(c), sl(hd)]) * dm, 0.0)
            tms = _tri_inv_many(a_list, eye)
            for (c, hd), g, kb, tmat in zip(items, gates, kbs, tms):
                tm_ref[rs(c), hs(hd)] = tmat
                u_ref[rs(c), sl(hd)] = _dot(tmat, v_ref[rs(c), sl(hd)] * g[0])
                w_ref[rs(c), sl(hd)] = _dot(tmat, kb * jnp.exp(g[1])).astype(BF16)

    blk = pl.BlockSpec((rows, A_WIDTH), lambda i: (i, 0))
    half = pl.BlockSpec((rows, A_HEADS * CHUNK), lambda i: (i, 0))
    outs = pl.pallas_call(
        body, name=name, grid=(t // rows,),
        in_specs=[blk, blk, blk, pl.BlockSpec((rows, LANE), lambda i: (i, 0)),
                  pl.BlockSpec((SUBLANE, rows), lambda i: (0, i))] + c_in_specs,
        out_specs=[blk, blk, half, half] + c_out_specs,
        out_shape=[jax.ShapeDtypeStruct((t, A_WIDTH), F32), jax.ShapeDtypeStruct((t, A_WIDTH), BF16),
                   jax.ShapeDtypeStruct((t, A_HEADS * CHUNK), F32),
                   jax.ShapeDtypeStruct((t, A_HEADS * CHUNK), F32)] + c_outs,
        scratch_shapes=c_scratch,
        compiler_params=_cp("arbitrary"))(q, k, v, bg, bgt, *c_ins)
    return outs[:4], outs[4:]


def _dn_scan_fwd(q, k, u, w, qk, bg, *, name, carry=None):
    t = q.shape[0]
    rows = SCAN_ROWS
    per = rows // CHUNK
    c_ins, c_in_specs, c_out_specs, c_outs, c_scratch = _carry_specs(carry)

    def body(*refs):
        q_ref, k_ref, u_ref, w_ref, qk_ref, bg_ref, o_ref, vn_ref, s_ref, state = _carried(carry, refs, 6, 3, t // rows)

        @pl.when(pl.program_id(0) == 0)
        def _():
            state[...] = jnp.zeros_like(state)

        heads = range(A_HEADS)
        sl = lambda hd: slice(hd * LANE, (hd + 1) * LANE)
        s_cur = [state[hd] for hd in heads]
        for c in range(per):
            rs = slice(c * CHUNK, (c + 1) * CHUNK)
            bg_v = bg_ref[rs, :]
            gcols = [_chunk_gates(bg_v, None, hd)[1] for hd in heads]
            glasts = [gc[CHUNK - 1:CHUNK, :] for gc in gcols]
            for hd in heads:
                s_ref[c, hd] = s_cur[hd].astype(BF16)
            vns = [u_ref[rs, sl(hd)] - _dot(w_ref[rs, sl(hd)], s_cur[hd]) for hd in heads]
            qss = [_dot(q_ref[rs, sl(hd)] * jnp.exp(gcols[hd]), s_cur[hd]) for hd in heads]
            s_cur = [s_cur[hd] * jnp.exp(glasts[hd])
                     + _dot_tn(k_ref[rs, sl(hd)] * jnp.exp(glasts[hd] - gcols[hd]), vns[hd]) for hd in heads]
            for hd in heads:
                vn_ref[rs, sl(hd)] = vns[hd]
                o_ref[rs, sl(hd)] = qss[hd] + _dot(qk_ref[rs, hd * CHUNK:(hd + 1) * CHUNK], vns[hd])
        for hd in heads:
            state[hd] = s_cur[hd]

    blk = pl.BlockSpec((rows, A_WIDTH), lambda i: (i, 0))
    half = pl.BlockSpec((rows, A_HEADS * CHUNK), lambda i: (i, 0))
    wide = jax.ShapeDtypeStruct((t, A_WIDTH), F32)
    outs = pl.pallas_call(
        body, name=name, grid=(t // rows,),
        in_specs=[blk, blk, blk, blk, half, pl.BlockSpec((rows, LANE), lambda i: (i, 0))] + c_in_specs,
        out_specs=[blk, blk, pl.BlockSpec((per, A_HEADS, LANE, LANE), lambda i: (i, 0, 0, 0))] + c_out_specs,
        out_shape=[wide, wide, jax.ShapeDtypeStruct((t // CHUNK, A_HEADS, LANE, LANE), BF16)] + c_outs,
        scratch_shapes=[pltpu.VMEM((A_HEADS, LANE, LANE), F32)] + c_scratch,
        compiler_params=_cp("arbitrary"))(q, k, u, w, qk, bg, *c_ins)
    return outs[:3], outs[3:]


def _stack_heads(ref, hk):
    return jnp.concatenate([ref[:, h * B_HEAD_DIM:(h + 1) * B_HEAD_DIM].astype(F32)
                            for h in range(hk * B_GROUP, (hk + 1) * B_GROUP)], axis=0)


def _swa_window():
    qi = lax.broadcasted_iota(jnp.int32, (BLOCK, BLOCK), 0)
    kj = lax.broadcasted_iota(jnp.int32, (BLOCK, BLOCK), 1)
    dist = jnp.where(kj > qi, qi + BLOCK - kj, qi - kj).astype(F32)
    rows = lax.broadcasted_iota(jnp.int32, (B_GROUP * BLOCK, BLOCK), 0)
    cols = lax.broadcasted_iota(jnp.int32, (B_GROUP * BLOCK, BLOCK), 1)
    return cols > jnp.bitwise_and(rows, BLOCK - 1), dist


def _swa_group_probs(q_ref, sk_ref, kp, kc, vp, vc, n_blk):
    hks = range(B_KV_HEADS)
    heads = lambda hk: range(hk * B_GROUP, (hk + 1) * B_GROUP)
    ksl = lambda hk: slice(hk * B_HEAD_DIM, (hk + 1) * B_HEAD_DIM)
    upper, dist = _swa_window()
    no_prev = jnp.where(n_blk > 0, 0.0, NEG)
    ones = jnp.ones((BLOCK, B_HEAD_DIM), BF16)
    with_ones = lambda v, hk: jnp.concatenate([v[:, ksl(hk)].astype(BF16), ones], axis=1)
    qs = [_stack_heads(q_ref, hk) * (B_HEAD_DIM ** -0.5) for hk in hks]
    sink = [jnp.concatenate([jnp.broadcast_to(sk_ref[h:h + 1, 0:1], (BLOCK, 1)) for h in heads(hk)], axis=0)
            for hk in hks]
    s = [jnp.where(upper, _dot_nt(qs[hk], kp[:, ksl(hk)]) + no_prev, _dot_nt(qs[hk], kc[:, ksl(hk)]))
         - jnp.concatenate([ALIBI[h] * dist for h in heads(hk)], axis=0) for hk in hks]
    m = [jnp.maximum(jnp.max(s[hk], axis=-1, keepdims=True), sink[hk]) for hk in hks]
    p = [jnp.exp(s[hk] - m[hk]) for hk in hks]
    p_up = [jnp.where(upper, p[hk], 0.0) for hk in hks]
    oe = [jnp.dot(p_up[hk].astype(BF16), with_ones(vp, hk), preferred_element_type=F32)
          + jnp.dot((p[hk] - p_up[hk]).astype(BF16), with_ones(vc, hk), preferred_element_type=F32) for hk in hks]
    ps = [jnp.exp(sink[hk] - m[hk]) for hk in hks]
    inv = [1.0 / (oe[hk][:, B_HEAD_DIM:B_HEAD_DIM + 1] + ps[hk]) for hk in hks]
    return upper, [(qs[hk], p[hk] * inv[hk], ps[hk] * inv[hk], oe[hk][:, :B_HEAD_DIM] * inv[hk]) for hk in hks]


def _swa_specs():
    qspec = lambda c0: pl.BlockSpec((BLOCK, B_WIDTH), lambda i: (i, c0 // B_WIDTH))
    cur = lambda c0: pl.BlockSpec((BLOCK, LANE), lambda i: (i, c0 // LANE))
    prev = lambda c0: pl.BlockSpec((BLOCK, LANE), lambda i: (jnp.maximum(i - 1, 0), c0 // LANE))
    return qspec, cur, prev


def _carried(carry, refs, n_in, n_out, steps):
    if carry is None:
        return refs
    ci, co = len(carry.ins), len(carry.outs)
    own = refs[:n_in] + refs[n_in + ci:n_in + ci + n_out] + refs[n_in + ci + n_out + co:len(refs) - 3]
    parts = refs[n_in:n_in + ci], refs[n_in + ci + n_out:n_in + ci + n_out + co], refs[len(refs) - 3:]

    @pl.when(pl.program_id(0) == 0)
    def _():
        carry.start(*parts)

    @pl.when(pl.program_id(0) == steps - 1)
    def _():
        carry.finish(*parts)

    return own


def _carry_specs(carry):
    if carry is None:
        return [], [], [], [], []
    return (list(carry.ins), [_ANY] * len(carry.ins), [_ANY] * len(carry.outs), list(carry.outs), carry.scratch())


def _swa_fwd(h, sinks_b, *, name, carry=None):
    t = h.shape[0]
    qspec, cur, prev = _swa_specs()
    c_ins, c_in_specs, c_out_specs, c_outs, c_scratch = _carry_specs(carry)

    def body(*refs):
        q_ref, kc_ref, kp_ref, vc_ref, vp_ref, sk_ref, o_ref, p_ref, ps_ref = _carried(carry, refs, 6, 3, t // BLOCK)
        n_blk = pl.program_id(0)
        _, groups = _swa_group_probs(q_ref, sk_ref, kp_ref[...], kc_ref[...], vp_ref[...], vc_ref[...], n_blk)
        lane = lax.broadcasted_iota(jnp.int32, (BLOCK, LANE), 1)
        sink_probs = jnp.zeros((BLOCK, LANE), F32)
        for hk, (_, p, ps, o) in enumerate(groups):
            for g in range(B_GROUP):
                hq = hk * B_GROUP + g
                rows = slice(g * BLOCK, (g + 1) * BLOCK)
                o_ref[:, hq * B_HEAD_DIM:(hq + 1) * B_HEAD_DIM] = o[rows]
                p_ref[:, hq * BLOCK:(hq + 1) * BLOCK] = p[rows].astype(BF16)
                sink_probs = sink_probs + jnp.where(lane == hq, ps[rows], 0.0)
        ps_ref[...] = sink_probs

    row = lambda w: pl.BlockSpec((BLOCK, w), lambda i: (i, 0))
    outs = pl.pallas_call(
        body, name=name, grid=(t // BLOCK,),
        in_specs=[qspec(C_QB), cur(C_KB), prev(C_KB), cur(C_VB), prev(C_VB),
                  pl.BlockSpec((B_Q_HEADS, LANE), lambda i: (0, 0))] + c_in_specs,
        out_specs=[row(B_WIDTH), row(B_Q_HEADS * BLOCK), row(LANE)] + c_out_specs,
        out_shape=[jax.ShapeDtypeStruct((t, B_WIDTH), F32), jax.ShapeDtypeStruct((t, B_Q_HEADS * BLOCK), BF16),
                   jax.ShapeDtypeStruct((t, LANE), F32)] + c_outs,
        scratch_shapes=c_scratch,
        compiler_params=_cp("arbitrary"))(h, h, h, h, h, sinks_b, *c_ins)
    return outs[:3], outs[3:]


def _rms_gate(o, za, nw):
    outs = []
    for hd in range(A_HEADS):
        oh = o[:, hd * LANE:(hd + 1) * LANE]
        r = lax.rsqrt(jnp.mean(oh * oh, -1, keepdims=True) + RMS_EPS)
        outs.append(oh * r * nw)
    return jnp.concatenate(outs, axis=1) * _silu(za)


def _out_ln(x, oa, ob, h, norm_w, w_out, ln_g, ln_b, *, tm, name, target=None):
    t = x.shape[0]
    last = target is not None

    def body(*refs):
        x_ref, oa_ref, ob_ref, za_ref, zb_ref, nw_ref, w_ref, g_ref, b_ref = refs[:9]
        xn_ref, mx_ref, r_ref = refs[9 + last:12 + last]
        ya = _rms_gate(oa_ref[...], za_ref[...].astype(F32), nw_ref[...])
        yb = ob_ref[...] * _silu(zb_ref[...].astype(F32))
        mixed = jnp.concatenate([ya, yb], axis=1).astype(BF16)
        mx_ref[...] = mixed
        r = DEEPNORM_ALPHA * x_ref[...] + jnp.dot(mixed, w_ref[...], preferred_element_type=F32)
        r_ref[...] = r
        mu = jnp.mean(r, -1, keepdims=True)
        xc = r - mu
        var = jnp.mean(xc * xc, -1, keepdims=True)
        xn = xc * lax.rsqrt(var + LN_EPS) * g_ref[...] + b_ref[...]
        if not last:
            xn_ref[...] = xn
            return
        loss_ref = refs[13]

        @pl.when(pl.program_id(0) == 0)
        def _():
            loss_ref[...] = jnp.zeros_like(loss_ref)

        err = xn - refs[9][...]
        xn_ref[...] = err * (1.0 / D_MODEL)
        loss_ref[...] += 0.5 / D_MODEL * jnp.sum(err * err)

    row = lambda w, c: pl.BlockSpec((tm, w), lambda i: (i, c))
    full = lambda a, b: pl.BlockSpec((a, b), lambda i: (0, 0))
    wide = jax.ShapeDtypeStruct((t, D_MODEL), F32)
    return pl.pallas_call(
        body, name=name, grid=(t // tm,),
        in_specs=[row(D_MODEL, 0), row(A_WIDTH, 0), row(B_WIDTH, 0), row(A_WIDTH, C_ZA // A_WIDTH),
                  row(B_WIDTH, C_ZB // B_WIDTH), full(1, LANE), full(D_MODEL, D_MODEL), full(1, D_MODEL),
                  full(1, D_MODEL)] + [row(D_MODEL, 0)] * last,
        out_specs=[row(D_MODEL, 0), row(D_MODEL, 0), row(D_MODEL, 0)] + [full(SUBLANE, LANE)] * last,
        out_shape=[wide, jax.ShapeDtypeStruct((t, D_MODEL), BF16), wide]
        + [jax.ShapeDtypeStruct((SUBLANE, LANE), F32)] * last,
        compiler_params=_cp("arbitrary" if last else "parallel"))(
        x, oa, ob, h, h, norm_w, w_out, ln_g, ln_b, *([target] if last else []))


def _layer_fwd(x, wt, conv_w, par, sinks_b, norm_w, w_out_bf, ln_g, ln_b, l, carries=None, target=None):
    carries = carries or {}
    h, got_in = _matmul_nt(x, wt, tm=512, name=f"in_proj_{l}", carry=carries.get("in_proj"))
    if callable(w_out_bf):
        w_out_bf = w_out_bf(got_in)
    (q, k, v, bg, bgt, conv), got_pre = _dn_pre(h, conv_w, par, tt=512, name=f"dn_pre_{l}",
                                                carry=carries.get("dn_pre"))
    (u, w, tmat, qk), got_wy = _dn_wy(q, k, v, bg, bgt, name=f"dn_wy_{l}", carry=carries.get("dn_wy"))
    (oa, vn, s_all), got_scan = _dn_scan_fwd(q, k, u, w, qk, bg, name=f"dn_scan_{l}", carry=carries.get("dn_scan"))
    (ob, swa_p, swa_ps), got_swa = _swa_fwd(h, sinks_b, name=f"swa_fwd_{l}", carry=carries.get("swa"))
    xn, mixed, r, *loss = _out_ln(x, oa, ob, h, norm_w, w_out_bf, ln_g, ln_b, tm=512, name=f"out_ln_{l}", target=target)
    if loss:
        xn = (xn, loss[0])
    res = dict(x=x, h=h, q=q, k=k, v=v, bg=bg, bgt=bgt, w=w, tmat=tmat, qk=qk, vn=vn, oa=oa, s_all=s_all,
               mixed=mixed, r=r, w_out=w_out_bf, ob=ob, swa_p=swa_p, swa_ps=swa_ps, conv=conv)
    return xn, res, dict(in_proj=got_in, dn_pre=got_pre, dn_wy=got_wy, dn_scan=got_scan, swa=got_swa)


def _ln_out_bwd(dxn, r, mixed, ln_g, w_out, *, tm, name):
    t = dxn.shape[0]

    def body(dxn_ref, r_ref, mx_ref, g_ref, w_ref, dr_ref, dm_ref, dw_ref, dg_ref, db_ref):
        @pl.when(pl.program_id(0) == 0)
        def _():
            dw_ref[...] = jnp.zeros_like(dw_ref)
            dg_ref[...] = jnp.zeros_like(dg_ref)
            db_ref[...] = jnp.zeros_like(db_ref)

        rr = r_ref[...]
        xc = rr - jnp.mean(rr, -1, keepdims=True)
        rstd = lax.rsqrt(jnp.mean(xc * xc, -1, keepdims=True) + LN_EPS)
        xhat = xc * rstd
        dxn_v = dxn_ref[...]
        dxh = dxn_v * g_ref[...]
        dr = rstd * (dxh - jnp.mean(dxh, -1, keepdims=True) - xhat * jnp.mean(dxh * xhat, -1, keepdims=True))
        dr_ref[...] = dr
        dg_ref[...] += jnp.sum(dxn_v * xhat, axis=0, keepdims=True)
        db_ref[...] += jnp.sum(dxn_v, axis=0, keepdims=True)
        drb = dr.astype(BF16)
        dm_ref[...] = _dot_nt(drb, w_ref[...])
        dw_ref[...] += _dot_tn(mx_ref[...], drb)

    row = pl.BlockSpec((tm, D_MODEL), lambda i: (i, 0))
    full = lambda a, b: pl.BlockSpec((a, b), lambda i: (0, 0))
    big = jax.ShapeDtypeStruct((t, D_MODEL), F32)
    vec = jax.ShapeDtypeStruct((1, D_MODEL), F32)
    return pl.pallas_call(
        body, name=name, grid=(t // tm,),
        in_specs=[row, row, row, full(1, D_MODEL), full(D_MODEL, D_MODEL)],
        out_specs=[row, row, full(D_MODEL, D_MODEL), full(1, D_MODEL), full(1, D_MODEL)],
        out_shape=[big, big, jax.ShapeDtypeStruct((D_MODEL, D_MODEL), F32), vec, vec],
        compiler_params=_cp("arbitrary"))(dxn, r, mixed, ln_g, w_out)


def _dn_post_bwd(dm, oa, h, norm_w, *, tm, name):
    t = oa.shape[0]

    def body(dy_ref, o_ref, za_ref, nw_ref, do_ref, dza_ref, dnw_ref):
        @pl.when(pl.program_id(0) == 0)
        def _():
            dnw_ref[...] = jnp.zeros_like(dnw_ref)

        nw = nw_ref[...]
        dnw = jnp.zeros_like(nw)
        for hd in range(A_HEADS):
            sl = slice(hd * LANE, (hd + 1) * LANE)
            oh, za, dy = o_ref[:, sl], za_ref[:, sl].astype(F32), dy_ref[:, sl]
            rs = lax.rsqrt(jnp.mean(oh * oh, -1, keepdims=True) + RMS_EPS)
            nrm = oh * rs
            gate, dgate = _silu_and_grad(za)
            dza_ref[:, sl] = (dy * nrm * nw * dgate).astype(dza_ref.dtype)
            dn = dy * gate
            dnw = dnw + jnp.sum(dn * nrm, axis=0, keepdims=True)
            dnn = dn * nw
            do_ref[:, sl] = (rs * dnn - oh * (rs * rs * rs) * jnp.mean(dnn * oh, -1, keepdims=True)).astype(BF16)
        dnw_ref[...] += dnw

    row = lambda c: pl.BlockSpec((tm, A_WIDTH), lambda i: (i, c))
    wide = jax.ShapeDtypeStruct((t, A_WIDTH), F32)
    return pl.pallas_call(
        body, name=name, grid=(t // tm,),
        in_specs=[row(0), row(0), row(C_ZA // A_WIDTH), pl.BlockSpec((1, LANE), lambda i: (0, 0))],
        out_specs=[row(0), row(C_ZA // A_WIDTH), pl.BlockSpec((1, LANE), lambda i: (0, 0))],
        out_shape=[jax.ShapeDtypeStruct((t, A_WIDTH), BF16), jax.ShapeDtypeStruct((t, DH_MAIN), BF16),
                   jax.ShapeDtypeStruct((1, LANE), F32)],
        compiler_params=_cp("arbitrary"))(dm, oa, h, norm_w)


def _dn_scan_bwd(q, k, w, qk, bg, do, *, name):
    t = q.shape[0]
    rows = SCAN_ROWS
    per = rows // CHUNK
    n = t // rows

    def body(q_ref, k_ref, w_ref, qk_ref, bg_ref, do_ref, dvn_ref, ds_ref, dstate):
        @pl.when(pl.program_id(0) == 0)
        def _():
            dstate[...] = jnp.zeros_like(dstate)

        heads = range(A_HEADS)
        sl = lambda hd: slice(hd * LANE, (hd + 1) * LANE)
        ds_cur = [dstate[hd] for hd in heads]
        for c in reversed(range(per)):
            rs = slice(c * CHUNK, (c + 1) * CHUNK)
            bg_v = bg_ref[rs, :]
            gcols = [_chunk_gates(bg_v, None, hd)[1] for hd in heads]
            glasts = [gc[CHUNK - 1:CHUNK, :] for gc in gcols]
            for hd in heads:
                ds_ref[c, hd] = ds_cur[hd].astype(BF16)
            pdo = [_dot_tn(qk_ref[rs, hd * CHUNK:(hd + 1) * CHUNK], do_ref[rs, sl(hd)]) for hd in heads]
            qdo = [_dot_tn(q_ref[rs, sl(hd)] * jnp.exp(gcols[hd]), do_ref[rs, sl(hd)]) for hd in heads]
            dvns = [pdo[hd] + _dot(k_ref[rs, sl(hd)] * jnp.exp(glasts[hd] - gcols[hd]), ds_cur[hd]) for hd in heads]
            ds_cur = [qdo[hd] + jnp.exp(glasts[hd]) * ds_cur[hd] - _dot_tn(w_ref[rs, sl(hd)], dvns[hd])
                      for hd in heads]
            for hd in heads:
                dvn_ref[rs, sl(hd)] = dvns[hd]
        for hd in heads:
            dstate[hd] = ds_cur[hd]

    blk = pl.BlockSpec((rows, A_WIDTH), lambda i: (n - 1 - i, 0))
    return pl.pallas_call(
        body, name=name, grid=(n,),
        in_specs=[blk, blk, blk, pl.BlockSpec((rows, A_HEADS * CHUNK), lambda i: (n - 1 - i, 0)),
                  pl.BlockSpec((rows, LANE), lambda i: (n - 1 - i, 0)), blk],
        out_specs=[blk, pl.BlockSpec((per, A_HEADS, LANE, LANE), lambda i: (n - 1 - i, 0, 0, 0))],
        out_shape=[jax.ShapeDtypeStruct((t, A_WIDTH), F32),
                   jax.ShapeDtypeStruct((t // CHUNK, A_HEADS, LANE, LANE), BF16)],
        scratch_shapes=[pltpu.VMEM((A_HEADS, LANE, LANE), F32)],
        compiler_params=_cp("arbitrary"))(q, k, w, qk, bg, do)


def _dn_chunk_bwd(q, k, v, vn, tmat, qk, bg, bgt, s_all, ds_all, dvn, do, *, name, carry=None):
    t = q.shape[0]
    rows = WY_ROWS
    per = rows // CHUNK

    c_ins, c_in_specs, c_out_specs, c_outs, c_scratch = _carry_specs(carry)

    def body(*refs):
        (q_ref, k_ref, v_ref, vn_ref, tm_ref, qk_ref, bg_ref, bgt_ref, s_ref, ds_ref, dvn_ref, do_ref,
         dq_ref, dk_ref, dv_ref, dbg_ref, dbgt_ref) = _carried(carry, refs, 12, 5, t // rows)
        causal, strict, _ = _chunk_masks()
        lane = lax.broadcasted_iota(jnp.int32, (CHUNK, LANE), 1)
        rowi = lax.broadcasted_iota(jnp.int32, (CHUNK, 1), 0)
        sub = lax.broadcasted_iota(jnp.int32, (SUBLANE, CHUNK), 0)
        rs = lambda c: slice(c * CHUNK, (c + 1) * CHUNK)
        sl = lambda hd: slice(hd * LANE, (hd + 1) * LANE)
        hs = lambda hd: slice(hd * CHUNK, (hd + 1) * CHUNK)
        for c0 in range(0, per, WY_GROUP):
            items = [(c, hd) for c in range(c0, c0 + WY_GROUP) for hd in range(A_HEADS)]
            at = lambda ref: [ref[rs(c), sl(hd)] for c, hd in items]
            qs, ks, vs, dos, vns, dvns = at(q_ref), at(k_ref), at(v_ref), at(do_ref), at(vn_ref), at(dvn_ref)
            tmhs = [tm_ref[rs(c), hs(hd)] for c, hd in items]
            ps = [qk_ref[rs(c), hs(hd)] for c, hd in items]
            gates = [_chunk_gates(bg_ref[rs(c), :], bgt_ref[:, rs(c)], hd) for c, hd in items]
            betas = [g[0] for g in gates]
            gcols = [g[1] for g in gates]
            dmats = [jnp.exp(jnp.where(causal, g[1] - g[2], NEG)) for g in gates]
            es = [jnp.exp(gc) for gc in gcols]
            glasts = [gc[CHUNK - 1:CHUNK, :] for gc in gcols]
            eks = [jnp.exp(gl - gc) for gl, gc in zip(glasts, gcols)]
            kbs = [kh * b for kh, b in zip(ks, betas)]
            vbs = [vh * b for vh, b in zip(vs, betas)]
            kbes = [kb * e for kb, e in zip(kbs, es)]

            a_s = [jnp.where(strict, _dot_nt(kb, kh) * dm, 0.0) for kb, kh, dm in zip(kbs, ks, dmats)]
            dps = [jnp.where(causal, _dot_nt(doh, vnh), 0.0) for doh, vnh in zip(dos, vns)]
            rows2 = lambda a, b: jnp.concatenate([a, b], axis=0)
            cols2 = lambda a, b: jnp.concatenate([a, b], axis=1)
            by_s = [_dot_nt(rows2(doh, dvnh), s_ref[c, hd]) for doh, dvnh, (c, hd) in zip(dos, dvns, items)]
            dqds = [m[:CHUNK] for m in by_s]
            dws = [-m[CHUNK:] for m in by_s]
            dkds = [_dot_nt(vnh, ds_ref[c, hd]) for vnh, (c, hd) in zip(vns, items)]
            dgts = [jnp.sum(s_ref[c, hd].astype(F32) * ds_ref[c, hd].astype(F32), keepdims=True) for c, hd in items]
            pairs = [cols2(dvnh, dw) for dvnh, dw in zip(dvns, dws)]
            by_t = [_dot_tn(tmh, pr) for tmh, pr in zip(tmhs, pairs)]
            dvbs = [m[:, :LANE] for m in by_t]
            dkbes = [m[:, LANE:] for m in by_t]
            dts = [_dot_nt(pr, cols2(vb, kbe)) for pr, vb, kbe in zip(pairs, vbs, kbes)]
            xs = [_dot_nt(dt, tmh) for dt, tmh in zip(dts, tmhs)]
            das = [jnp.where(strict, -_dot_tn(tmh, x), 0.0) for tmh, x in zip(tmhs, xs)]
            dmas = [da * dm for da, dm in zip(das, dmats)]
            dmps = [dp * dm for dp, dm in zip(dps, dmats)]
            stacked = [rows2(dma, dmp) for dma, dmp in zip(dmas, dmps)]
            by_k = [_dot(st, kh) for st, kh in zip(stacked, ks)]
            dkbs = [m[:CHUNK] + dkbe * e for m, dkbe, e in zip(by_k, dkbes, es)]
            for i, (c, hd) in enumerate(items):
                dq_ref[rs(c), sl(hd)] = by_k[i][CHUNK:] + dqds[i] * es[i]
                dk_ref[rs(c), sl(hd)] = (_dot_tn(stacked[i], rows2(kbs[i], qs[i])) + dkds[i] * eks[i]
                                         + dkbs[i] * betas[i])
                dv_ref[rs(c), sl(hd)] = dvbs[i] * betas[i]
            for c in range(c0, c0 + WY_GROUP):
                acc = jnp.zeros((CHUNK, LANE), F32)
                acc_t = jnp.zeros((SUBLANE, CHUNK), F32)
                for i, (ci, hd) in enumerate(items):
                    if ci != c:
                        continue
                    gmat = das[i] * a_s[i] + dps[i] * ps[i]
                    rk = jnp.sum(dkds[i] * ks[i], -1, keepdims=True) * eks[i]
                    de = jnp.sum(dqds[i] * qs[i] + dkbes[i] * kbs[i], -1, keepdims=True)
                    dglast = jnp.sum(rk, keepdims=True) + dgts[i] * jnp.exp(glasts[i])
                    dgc = (jnp.sum(gmat, -1, keepdims=True) + de * es[i] - rk
                           + jnp.where(rowi == CHUNK - 1, dglast, 0.0))
                    dbeta = jnp.sum(dkbs[i] * ks[i] + dvbs[i] * vs[i], -1, keepdims=True)
                    acc = acc + jnp.where(lane == hd, dbeta, 0.0) + jnp.where(lane == A_HEADS + hd, dgc, 0.0)
                    acc_t = acc_t + jnp.where(sub == A_HEADS + hd, -jnp.sum(gmat, axis=0, keepdims=True), 0.0)
                dbg_ref[rs(c), :] = acc
                dbgt_ref[:, rs(c)] = acc_t

    blk = pl.BlockSpec((rows, A_WIDTH), lambda i: (i, 0))
    half = pl.BlockSpec((rows, A_HEADS * CHUNK), lambda i: (i, 0))
    col = pl.BlockSpec((rows, LANE), lambda i: (i, 0))
    rowf = pl.BlockSpec((SUBLANE, rows), lambda i: (0, i))
    st = pl.BlockSpec((per, A_HEADS, LANE, LANE), lambda i: (i, 0, 0, 0))
    wide = jax.ShapeDtypeStruct((t, A_WIDTH), F32)
    outs = pl.pallas_call(
        body, name=name, grid=(t // rows,),
        in_specs=[blk, blk, blk, blk, half, half, col, rowf, st, st, blk, blk] + c_in_specs,
        out_specs=[blk, blk, blk, col, rowf] + c_out_specs,
        out_shape=[wide, wide, wide, jax.ShapeDtypeStruct((t, LANE), F32),
                   jax.ShapeDtypeStruct((SUBLANE, t), F32)] + c_outs,
        scratch_shapes=c_scratch,
        compiler_params=_cp("arbitrary"))(q, k, v, vn, tmat, qk, bg, bgt, s_all, ds_all, dvn, do, *c_ins)
    return outs[:5], outs[5:]


def _dn_pre_bwd(h, conv, par, dq, dk, dv, dbg, dbgt, *, tt, name):
    t = h.shape[0]
    cw = 3 * A_WIDTH

    def body(conv_ref, bgi_ref, par_ref, dq_ref, dk_ref, dv_ref, dbg_ref, dbgt_ref, dc_ref, dbgi_ref, dpar_ref):
        i = pl.program_id(0)

        @pl.when(i == 0)
        def _():
            dpar_ref[...] = jnp.zeros_like(dpar_ref)

        s, ds = _silu_and_grad(conv_ref[...])
        for hd in range(A_HEADS):
            sl = slice(hd * LANE, (hd + 1) * LANE)
            for base, d_ref, scale in ((0, dq_ref, A_HEAD_DIM ** -0.5), (A_WIDTH, dk_ref, 1.0)):
                csl = slice(base + hd * LANE, base + (hd + 1) * LANE)
                tq = s[:, base + hd * LANE:base + (hd + 1) * LANE]
                dy = d_ref[:, sl]
                rq = lax.rsqrt(jnp.sum(tq * tq, -1, keepdims=True) + L2_EPS)
                dtq = scale * (rq * dy - tq * (rq * rq * rq) * jnp.sum(dy * tq, -1, keepdims=True))
                dc_ref[:, csl] = dtq * ds[:, base + hd * LANE:base + (hd + 1) * LANE]
        dc_ref[:, 2 * A_WIDTH:] = dv_ref[...] * ds[:, 2 * A_WIDTH:]
        raw = bgi_ref[...].astype(F32)
        lane = lax.broadcasted_iota(jnp.int32, raw.shape, 1)
        is_b = lane < A_HEADS
        is_a = (lane >= A_HEADS) & (lane < 2 * A_HEADS)
        rows_t = jnp.concatenate([dbgt_ref[...], jnp.zeros((LANE - SUBLANE, tt), F32)], axis=0)
        dbg_v = dbg_ref[...] + jnp.where(is_a, jnp.transpose(rows_t), 0.0)
        dbg_v = jnp.where(is_a, _dot_hi(_chunk_tri(tt, lower=False), jnp.where(is_a, dbg_v, 0.0)), dbg_v)
        beta = _sigmoid(raw)
        z = raw + par_ref[1:2, :]
        neg_ea = -jnp.exp(par_ref[0:1, :])
        g = neg_ea * _softplus(z)
        da = dbg_v * neg_ea * _sigmoid(z)
        dbgi_ref[...] = jnp.where(is_b, dbg_v * beta * (1.0 - beta), jnp.where(is_a, da, 0.0))
        dpar_ref[0:1, :] += jnp.sum(jnp.where(is_a, dbg_v * g, 0.0), axis=0, keepdims=True)
        dpar_ref[1:2, :] += jnp.sum(jnp.where(is_a, da, 0.0), axis=0, keepdims=True)

    wide = pl.BlockSpec((tt, A_WIDTH), lambda i: (i, 0))
    return pl.pallas_call(
        body, name=name, grid=(t // tt,),
        in_specs=[pl.BlockSpec((tt, cw), lambda i: (i, 0)),
                  pl.BlockSpec((tt, LANE), lambda i: (i, C_BG // LANE)),
                  pl.BlockSpec((SUBLANE, LANE), lambda i: (0, 0)),
                  wide, wide, wide, pl.BlockSpec((tt, LANE), lambda i: (i, 0)),
                  pl.BlockSpec((SUBLANE, tt), lambda i: (0, i))],
        out_specs=[pl.BlockSpec((tt, cw), lambda i: (i, 0)), pl.BlockSpec((tt, LANE), lambda i: (i, 0)),
                   pl.BlockSpec((SUBLANE, LANE), lambda i: (0, 0))],
        out_shape=[jax.ShapeDtypeStruct((t, cw), F32), jax.ShapeDtypeStruct((t, LANE), F32),
                   jax.ShapeDtypeStruct((SUBLANE, LANE), F32)],
        compiler_params=_cp("arbitrary"))(conv, h, par, dq, dk, dv, dbg, dbgt)


def _conv_bwd(dc, h, conv_w, dh, *, tt, name):
    t = dc.shape[0]
    cw = 3 * A_WIDTH
    nb = t // tt

    def body(dc_ref, after_ref, pre_ref, cw_ref, dh_in_ref, dpre_ref, dcw_ref):
        i = pl.program_id(0)

        @pl.when(i == 0)
        def _():
            dcw_ref[...] = jnp.zeros_like(dcw_ref)

        dcv = dc_ref[...]
        after = jnp.where(i < nb - 1, after_ref[...], 0.0)
        cur = pre_ref[...].astype(F32)
        w = cw_ref[...]
        acc = dcv * w[CONV_K - 1:CONV_K, :]
        dcw_ref[CONV_K - 1:CONV_K, :] += jnp.sum(dcv * cur, axis=0, keepdims=True)
        for s in range(1, CONV_K):
            j = CONV_K - 1 - s
            up = _shift_up(dcv, after, s)
            acc = acc + up * w[j:j + 1, :]
            dcw_ref[j:j + 1, :] += jnp.sum(up * cur, axis=0, keepdims=True)
        dpre_ref[...] = acc.astype(dpre_ref.dtype)

    return pl.pallas_call(
        body, name=name, grid=(nb,),
        in_specs=[pl.BlockSpec((tt, cw), lambda i: (i, 0)),
                  pl.BlockSpec((SUBLANE, cw), lambda i: (jnp.minimum((i + 1) * (tt // SUBLANE), t // SUBLANE - 1), 0)),
                  pl.BlockSpec((tt, cw), lambda i: (i, 0)),
                  pl.BlockSpec((CONV_K, cw), lambda i: (0, 0)), _ANY],
        out_specs=[pl.BlockSpec((tt, cw), lambda i: (i, 0)), pl.BlockSpec((SUBLANE, cw), lambda i: (0, 0))],
        out_shape=[jax.ShapeDtypeStruct(dh.shape, dh.dtype), jax.ShapeDtypeStruct((SUBLANE, cw), F32)],
        input_output_aliases={4: 0},
        compiler_params=_cp("arbitrary"))(dc, dc, h, conv_w, dh)


def _swa_bwd(h, dm, ob, probs, sink_probs, dh, *, name, carry=None):
    t = h.shape[0]
    qspec, cur, prev = _swa_specs()
    c_ins, c_in_specs, c_out_specs, c_outs, c_scratch = _carry_specs(carry)

    def body(*refs):
        (q_ref, kc_ref, kp_ref, vc_ref, vp_ref, zb_ref, dy_ref, ob_ref, p_ref, ps_ref, dh_in_ref,
         dqz_ref, dk_ref, dv_ref, dsk_ref) = _carried(carry, refs, 11, 4, t // BLOCK)
        n_blk = pl.program_id(0)

        @pl.when(n_blk == 0)
        def _():
            dk_ref[...] = jnp.zeros_like(dk_ref)
            dv_ref[...] = jnp.zeros_like(dv_ref)
            dsk_ref[...] = jnp.zeros_like(dsk_ref)

        kp, kc, vp, vc = kp_ref[...], kc_ref[...], vp_ref[...], vc_ref[...]
        scale = B_HEAD_DIM ** -0.5
        hks = range(B_KV_HEADS)
        ksl = lambda hk: slice(hk * B_HEAD_DIM, (hk + 1) * B_HEAD_DIM)
        heads = lambda hk: range(hk * B_GROUP, (hk + 1) * B_GROUP)
        upper, _ = _swa_window()
        groups = [(_stack_heads(q_ref, hk) * scale,
                   jnp.concatenate([p_ref[:, h * BLOCK:(h + 1) * BLOCK].astype(F32) for h in heads(hk)], axis=0),
                   jnp.concatenate([ps_ref[:, h:h + 1] for h in heads(hk)], axis=0),
                   _stack_heads(ob_ref, hk)) for hk in hks]
        zbs = [_stack_heads(zb_ref, hk) for hk in hks]
        dys = [_stack_heads(dy_ref, hk) for hk in hks]
        gates = [_silu_and_grad(zbs[hk]) for hk in hks]
        dos = [dys[hk] * gates[hk][0] for hk in hks]
        deltas = [jnp.sum(dos[hk] * groups[hk][3], -1, keepdims=True) for hk in hks]
        dps = [jnp.where(upper, _dot_nt(dos[hk], vp[:, ksl(hk)]), _dot_nt(dos[hk], vc[:, ksl(hk)])) for hk in hks]
        dss = [groups[hk][1] * (dps[hk] - deltas[hk]) for hk in hks]
        ds_up = [jnp.where(upper, dss[hk], 0.0) for hk in hks]
        ds_lo = [dss[hk] - ds_up[hk] for hk in hks]
        p_up = [jnp.where(upper, groups[hk][1], 0.0) for hk in hks]
        p_lo = [groups[hk][1] - p_up[hk] for hk in hks]
        dqs = [(_dot(ds_up[hk], kp[:, ksl(hk)]) + _dot(ds_lo[hk], kc[:, ksl(hk)])) * scale for hk in hks]
        dk_prev = [_dot_tn(ds_up[hk], groups[hk][0]) for hk in hks]
        dk_cur = [_dot_tn(ds_lo[hk], groups[hk][0]) for hk in hks]
        dv_prev = [_dot_tn(p_up[hk], dos[hk]) for hk in hks]
        dv_cur = [_dot_tn(p_lo[hk], dos[hk]) for hk in hks]
        for hk in hks:
            dzb = dys[hk] * groups[hk][3] * gates[hk][1]
            dsink = groups[hk][2] * deltas[hk]
            for g in range(B_GROUP):
                hq = hk * B_GROUP + g
                rows = slice(g * BLOCK, (g + 1) * BLOCK)
                qsl = slice(hq * B_HEAD_DIM, (hq + 1) * B_HEAD_DIM)
                dqz_ref[:, qsl] = dqs[hk][rows].astype(dqz_ref.dtype)
                dqz_ref[:, B_WIDTH + hq * B_HEAD_DIM:B_WIDTH + (hq + 1) * B_HEAD_DIM] = dzb[rows].astype(dqz_ref.dtype)
                dsk_ref[hq:hq + 1, :] += -jnp.sum(dsink[rows], keepdims=True)
        at_cur = pl.ds(pl.multiple_of(n_blk * BLOCK, BLOCK), BLOCK)
        at_prev = pl.ds(pl.multiple_of(jnp.maximum(n_blk - 1, 0) * BLOCK, BLOCK), BLOCK)
        dk_ref[at_prev, :] += jnp.concatenate(dk_prev, axis=1)
        dv_ref[at_prev, :] += jnp.concatenate(dv_prev, axis=1)
        dk_ref[at_cur, :] += jnp.concatenate(dk_cur, axis=1)
        dv_ref[at_cur, :] += jnp.concatenate(dv_cur, axis=1)

    narrow = jax.ShapeDtypeStruct((t, B_KV_WIDTH), F32)
    res = lambda a, b: pl.BlockSpec((a, b), lambda i: (0, 0))
    row = lambda w: pl.BlockSpec((BLOCK, w), lambda i: (i, 0))
    outs = pl.pallas_call(
        body, name=name, grid=(t // BLOCK,),
        in_specs=[qspec(C_QB), cur(C_KB), prev(C_KB), cur(C_VB), prev(C_VB), qspec(C_ZB),
                  pl.BlockSpec((BLOCK, B_WIDTH), lambda i: (i, 1)), row(B_WIDTH), row(B_Q_HEADS * BLOCK), row(LANE),
                  _ANY] + c_in_specs,
        out_specs=[pl.BlockSpec((BLOCK, 2 * B_WIDTH), lambda i: (i, C_QB // (2 * B_WIDTH))),
                   res(t, B_KV_WIDTH), res(t, B_KV_WIDTH), res(B_Q_HEADS, LANE)] + c_out_specs,
        out_shape=[jax.ShapeDtypeStruct(dh.shape, dh.dtype), narrow, narrow,
                   jax.ShapeDtypeStruct((B_Q_HEADS, LANE), F32)] + c_outs,
        scratch_shapes=c_scratch,
        input_output_aliases={10: 0},
        compiler_params=_cp("arbitrary"))(h, h, h, h, h, h, dm, ob, probs, sink_probs, dh, *c_ins)
    return outs[:4], outs[4:]


def _in_proj_dw(dh_main, dh_tail, x, *, tk, name):
    t, n = x.shape

    def body(a_ref, t_ref, x_ref, o_ref, ot_ref):
        @pl.when(pl.program_id(0) == 0)
        def _():
            o_ref[...] = jnp.zeros_like(o_ref)
            ot_ref[...] = jnp.zeros_like(ot_ref)

        xb = x_ref[...].astype(BF16)
        o_ref[...] += _dot_tn(a_ref[...], xb)
        ot_ref[...] += _dot_tn(t_ref[...], xb)

    row = lambda a: pl.BlockSpec((tk, a.shape[1]), lambda kk: (kk, 0))
    acc = lambda a: pl.BlockSpec((a.shape[1], n), lambda kk: (0, 0))
    return pl.pallas_call(
        body, name=name, grid=(t // tk,), in_specs=[row(dh_main), row(dh_tail), row(x)],
        out_specs=[acc(dh_main), acc(dh_tail)],
        out_shape=[jax.ShapeDtypeStruct((a.shape[1], n), F32) for a in (dh_main, dh_tail)],
        compiler_params=_cp("arbitrary"))(dh_main, dh_tail, x)


def _in_proj_dx(dh_main, dh_tail, wt, dr, *, tm, name, carry=None):
    t, n_main = dh_main.shape
    n_tail = dh_tail.shape[1]
    c_ins, c_in_specs, c_out_specs, c_outs, c_scratch = _carry_specs(carry)

    def body(*refs):
        a_ref, t_ref, wa_ref, wt_ref, r_ref, o_ref = _carried(carry, refs, 5, 1, t // tm)
        o_ref[...] = _dot(a_ref[...], wa_ref[...]) + _dot(t_ref[...], wt_ref[...]) + DEEPNORM_ALPHA * r_ref[...]

    row = lambda w: pl.BlockSpec((tm, w), lambda i: (i, 0))
    outs = pl.pallas_call(
        body, name=name, grid=(t // tm,),
        in_specs=[row(n_main), row(n_tail), pl.BlockSpec((n_main, D_MODEL), lambda i: (0, 0)),
                  pl.BlockSpec((n_tail, D_MODEL), lambda i: (n_main // n_tail, 0)), row(D_MODEL)] + c_in_specs,
        out_specs=[row(D_MODEL)] + c_out_specs,
        out_shape=[jax.ShapeDtypeStruct((t, D_MODEL), F32)] + c_outs,
        scratch_shapes=c_scratch,
        compiler_params=_cp("arbitrary"))(dh_main, dh_tail, wt, wt, dr, *c_ins)
    return outs[0], outs[1:]


def _layer_bwd(dxn, res, wt, conv_w, par, sinks_b, norm_w, w_out_bf, ln_g, l, carries=None, carry_dx=None):
    carries = carries or {}
    w_out_bf = res["w_out"]
    dr, dm, dw_out, dln_g, dln_b = _ln_out_bwd(dxn, res["r"], res["mixed"], ln_g, w_out_bf, tm=512, name=f"ln_out_bwd_{l}")
    h = res["h"]
    do, dh, dnw = _dn_post_bwd(dm, res["oa"], h, norm_w, tm=512, name=f"dn_post_bwd_{l}")
    dvn, ds_all = _dn_scan_bwd(res["q"], res["k"], res["w"], res["qk"], res["bg"], do, name=f"dn_scan_bwd_{l}")
    (dq, dk, dv, dbg, dbgt), got_chunk = _dn_chunk_bwd(
        res["q"], res["k"], res["v"], res["vn"], res["tmat"], res["qk"], res["bg"], res["bgt"], res["s_all"], ds_all,
        dvn, do, name=f"dn_chunk_bwd_{l}", carry=carries.get("dn_chunk"))
    dc, dbgi, dpar = _dn_pre_bwd(h, res["conv"], par, dq, dk, dv, dbg, dbgt, tt=512, name=f"dn_pre_bwd_{l}")
    dh, dcw = _conv_bwd(dc, h, conv_w, dh, tt=512, name=f"conv_bwd_{l}")
    (dh, dkb, dvb, dsk), got_swa = _swa_bwd(h, dm, res["ob"], res["swa_p"], res["swa_ps"], dh, name=f"swa_bwd_{l}",
                                            carry=carries.get("swa"))
    carried = dict(dn_chunk=got_chunk, swa=got_swa)
    dh_tail = jnp.concatenate([dkb, dvb, dbgi], axis=1).astype(BF16)
    dwt_main, dwt_tail = _in_proj_dw(dh, dh_tail, res["x"], tk=512, name=f"in_proj_dw_{l}")
    grads = dict(w_in=(dwt_main, dwt_tail), conv_w=dcw[:CONV_K], a_log=dpar[0, A_HEADS:2 * A_HEADS],
                 dt_bias=dpar[1, A_HEADS:2 * A_HEADS], norm_w=dnw[0], sinks=dsk[:, 0], w_out=dw_out,
                 ln_g=dln_g[0], ln_b=dln_b[0])
    dx, carried_dx = _in_proj_dx(dh, dh_tail, wt, dr, tm=512, name=f"in_proj_dx_{l}",
                                 carry=None if carry_dx is None else carry_dx(grads))
    return dx, grads, carried, carried_dx


def _layer_args(wt, conv_w, a_log, dt_bias, sinks, norm_w, w_out_bf):
    return (wt, conv_w, _gate_params(a_log, dt_bias), jnp.broadcast_to(sinks[:, None], (B_Q_HEADS, LANE)),
            norm_w[None], w_out_bf)


def _local_step(x, target, args0, args1, ln_g, ln_b, gathers=None, reduce1=None, reduce0=None):
    assert DEPTH == 2
    x1, res0, got = _layer_fwd(x, *args0, ln_g[0][None], ln_b[0][None], 0, carries=gathers)
    if gathers is not None:
        args1 = args1(got)
    (dx, loss_tile), res1, _ = _layer_fwd(x1, *args1, ln_g[1][None], ln_b[1][None], 1, target=target)
    dx, grads1, _, _ = _layer_bwd(dx, res1, *args1, ln_g[1][None], 1)
    carries = None if reduce1 is None else reduce1(grads1)
    carry_dx = None if reduce0 is None else (lambda grads0: reduce0(grads0, grads1, loss_tile))
    dx, grads0, landed1, landed0 = _layer_bwd(dx, res0, *args0, ln_g[0][None], 0, carries=carries, carry_dx=carry_dx)
    return loss_tile, dx, [grads0, grads1], landed1, landed0


_ANY = pl.BlockSpec(memory_space=pl.ANY)
_MESH = pl.DeviceIdType.MESH


HALF = D_MODEL // 2


class _Exchange:
    def __init__(self, ins, outs, n_remote, n_local, plan):
        self.ins, self.outs, self.n_remote, self.n_local, self.plan = tuple(ins), tuple(outs), n_remote, n_local, plan

    def scratch(self):
        return [pltpu.SemaphoreType.DMA((self.n_remote,)), pltpu.SemaphoreType.DMA((self.n_remote,)),
                pltpu.SemaphoreType.DMA((max(self.n_local, 1),))]

    def _copies(self, in_refs, out_refs, sems, arriving):
        send_sems, recv_sems, local_sems = sems
        local, sends, recvs = self.plan(in_refs, out_refs)
        loc = [pltpu.make_async_copy(s, d, local_sems.at[i]) for i, (s, d) in enumerate(local)]
        rem = [pltpu.make_async_remote_copy(src_ref=s, dst_ref=recvs[i] if arriving else d, send_sem=send_sems.at[i],
                                            recv_sem=recv_sems.at[i], device_id=peer, device_id_type=_MESH)
               for i, (s, d, peer) in enumerate(sends)]
        return loc, rem

    def start(self, in_refs, out_refs, sems):
        loc, rem = self._copies(in_refs, out_refs, sems, arriving=False)
        for cp in loc + rem:
            cp.start()

    def finish(self, in_refs, out_refs, sems):
        loc, rem = self._copies(in_refs, out_refs, sems, arriving=True)
        for cp in rem:
            cp.wait_recv()
        for cp in rem:
            cp.wait_send()
        for cp in loc:
            cp.wait()


def _run_exchange(ex, *, name):
    n_in, n_out = len(ex.ins), len(ex.outs)

    def body(*refs):
        parts = refs[:n_in], refs[n_in:n_in + n_out], refs[n_in + n_out:]
        ex.start(*parts)
        ex.finish(*parts)

    return pl.pallas_call(body, name=name, in_specs=[_ANY] * n_in, out_specs=[_ANY] * n_out, out_shape=list(ex.outs),
                          scratch_shapes=ex.scratch())(*ex.ins)


def _place():
    x, y, c = lax.axis_index("x"), lax.axis_index("y"), lax.axis_index("c")
    return x, y, c, [(1 - x, y), (x, 1 - y), (1 - x, 1 - y)]


def _gather_exchange(arrays):
    n = len(arrays)

    def plan(src, dst):
        x, y, c, chips = _place()
        me = 2 * x + y
        local = [(src[k], dst[k].at[me]) for k in range(n)]
        sends = [(src[k], dst[k].at[me], (px, py, c)) for k in range(n) for px, py in chips]
        recvs = [dst[k].at[2 * px + py] for k in range(n) for px, py in chips]
        return local, sends, recvs

    return _Exchange(arrays, [jax.ShapeDtypeStruct((N_SHARD,) + a.shape, a.dtype) for a in arrays], 3 * n, n, plan)


def _gather_two_level(pack, conv_w, *, name):
    rows = pack.shape[0]
    part_rows = rows // 2

    def body(pack_ref, conv_ref, land_ref, conv_land_ref, send1, recv1, send2, recv2, csend, crecv, local_sems):
        x, y, c, chips = _place()
        me = 2 * x + y
        sibling = (x, y, 1 - c)
        part = lambda core: pl.ds(pl.multiple_of(core * part_rows, 16), part_rows)
        remote = lambda src, dst, ss, rs, to: pltpu.make_async_remote_copy(
            src_ref=src, dst_ref=dst, send_sem=ss, recv_sem=rs, device_id=to, device_id_type=_MESH)
        local = [pltpu.make_async_copy(pack_ref, land_ref.at[me], local_sems.at[0]),
                 pltpu.make_async_copy(conv_ref, conv_land_ref.at[me], local_sems.at[1])]
        for cp in local:
            cp.start()
        first = [remote(pack_ref.at[part(c)], land_ref.at[me, part(c)], send1.at[j], recv1.at[j], (px, py, c))
                 for j, (px, py) in enumerate(chips)]
        convs = [remote(conv_ref, conv_land_ref.at[me], csend.at[j], crecv.at[j], (px, py, c))
                 for j, (px, py) in enumerate(chips)]
        for cp in first + convs:
            cp.start()
        passed = []
        for j, (px, py) in enumerate(chips):
            slot = 2 * px + py
            remote(pack_ref.at[part(c)], land_ref.at[slot, part(c)], send1.at[j], recv1.at[j], (px, py, c)).wait_recv()
            cp = remote(land_ref.at[slot, part(c)], land_ref.at[slot, part(c)], send2.at[j], recv2.at[j], sibling)
            cp.start()
            passed.append(cp)
        for j, (px, py) in enumerate(chips):
            slot = 2 * px + py
            remote(land_ref.at[slot, part(1 - c)], land_ref.at[slot, part(1 - c)], send2.at[j], recv2.at[j],
                   sibling).wait_recv()
            remote(conv_ref, conv_land_ref.at[slot], csend.at[j], crecv.at[j], (px, py, c)).wait_recv()
        for cp in first + convs + passed:
            cp.wait_send()
        for cp in local:
            cp.wait()

    sems = [pltpu.SemaphoreType.DMA((3,))] * 6 + [pltpu.SemaphoreType.DMA((2,))]
    return pl.pallas_call(
        body, name=name, in_specs=[_ANY, _ANY], out_specs=[_ANY, _ANY],
        out_shape=[jax.ShapeDtypeStruct((N_SHARD,) + pack.shape, pack.dtype),
                   jax.ShapeDtypeStruct((N_SHARD,) + conv_w.shape, conv_w.dtype)],
        scratch_shapes=sems)(pack, conv_w)


def _half(core):
    return pl.ds(pl.multiple_of(core * HALF, HALF), HALF)


def _reduce_scatter_exchange(g, row0, rows):
    def plan(src, dst):
        x, y, c, chips = _place()
        peers = [(px, py, c if t == 0 else 1 - c) for px, py in chips for t in (0, 1)] + [(x, y, 1 - c)]
        sends = [(src[0].at[2 * px + py, pl.ds(row0, rows), _half(pc)], dst[0].at[k], (px, py, pc))
                 for k, (px, py, pc) in enumerate(peers)]
        return [], sends, [dst[0].at[k] for k in range(7)]

    return _Exchange([g], [jax.ShapeDtypeStruct((7, rows, HALF), g.dtype)], 7, 0, plan)


def _pair_window_exchange(g):
    def plan(src, dst):
        x, y, c, _ = _place()
        return [], [(src[0].at[:, :, _half(1 - c)], dst[0], (x, y, 1 - c))], [dst[0]]

    return _Exchange([g], [jax.ShapeDtypeStruct(g.shape[:2] + (HALF,), g.dtype)], 1, 0, plan)


def _chip_scatter_exchange(p, small):
    def plan(src, dst):
        x, y, c, chips = _place()
        mine = 4 * x + 2 * y + c
        peers = [(px, py, c if t == 0 else 1 - c) for px, py in chips for t in (0, 1)] + [(x, y, 1 - c)]
        sends = [(src[0].at[2 * px + py], dst[0].at[j], (px, py, c)) for j, (px, py) in enumerate(chips)]
        recvs = [dst[0].at[j] for j in range(3)]
        sends += [(src[1], dst[1].at[mine], peer) for peer in peers]
        recvs += [dst[1].at[4 * px + 2 * py + pc] for px, py, pc in peers]
        return [(src[1], dst[1].at[mine])], sends, recvs

    outs = [jax.ShapeDtypeStruct((3,) + p.shape[1:], p.dtype), jax.ShapeDtypeStruct((8,) + small.shape, small.dtype)]
    return _Exchange([p, small], outs, 10, 1, plan)


def _share_exchange(arrays):
    n = len(arrays)

    def plan(src, dst):
        x, y, c, _ = _place()
        return [], [(src[k], dst[k], (x, y, 1 - c)) for k in range(n)], [dst[k] for k in range(n)]

    return _Exchange(arrays, [jax.ShapeDtypeStruct(a.shape, a.dtype) for a in arrays], n, 0, plan)


def _sum_scatter(g, lands, me, core, *, tc, name):
    rows = g.shape[1]
    per = HALF // tc
    n = len(lands)

    def body(*refs):
        g_ref, land_refs, o_ref = refs[1], refs[2:2 + n], refs[2 + n]
        at = 0
        for land_ref in land_refs:
            run = slice(at, at + land_ref.shape[1])
            acc = g_ref[run, :].astype(F32)
            for k in range(7):
                acc = acc + land_ref[k].astype(F32)
            o_ref[run, :] = acc
            at = run.stop

    return pl.pallas_call(
        body, name=name, out_shape=jax.ShapeDtypeStruct((rows, HALF), F32), compiler_params=_cp("parallel"),
        grid_spec=pltpu.PrefetchScalarGridSpec(
            num_scalar_prefetch=1, grid=(per,),
            in_specs=[pl.BlockSpec((None, rows, tc), lambda i, w: (w[0], 0, w[1] * per + i))]
            + [pl.BlockSpec((7, a.shape[1], tc), lambda i, w: (0, 0, i)) for a in lands],
            out_specs=pl.BlockSpec((rows, tc), lambda i, w: (0, i))))(
        jnp.stack([me, core]).astype(jnp.int32), g, *lands)


def _pair_add(g, land, core, *, name):
    n, rows, _ = g.shape

    def body(core_ref, g_ref, land_ref, o_ref):
        o_ref[...] = (g_ref[...].astype(F32) + land_ref[...].astype(F32)).astype(o_ref.dtype)

    blk = pl.BlockSpec((1, rows, HALF), lambda i, w: (i, 0, 0))
    return pl.pallas_call(
        body, name=name, out_shape=jax.ShapeDtypeStruct((n, rows, HALF), g.dtype), compiler_params=_cp("parallel"),
        grid_spec=pltpu.PrefetchScalarGridSpec(
            num_scalar_prefetch=1, grid=(n,),
            in_specs=[pl.BlockSpec((1, rows, HALF), lambda i, w: (i, 0, w[0])), blk], out_specs=blk))(
        jnp.reshape(core, (1,)).astype(jnp.int32), g, land)


def _sum_chips(p, land, me, *, tc, name):
    rows = p.shape[1]

    def body(me_ref, p_ref, land_ref, o_ref):
        acc = p_ref[...].astype(F32)
        for k in range(3):
            acc = acc + land_ref[k].astype(F32)
        o_ref[...] = acc

    return pl.pallas_call(
        body, name=name, out_shape=jax.ShapeDtypeStruct((rows, HALF), F32), compiler_params=_cp("parallel"),
        grid_spec=pltpu.PrefetchScalarGridSpec(
            num_scalar_prefetch=1, grid=(HALF // tc,),
            in_specs=[pl.BlockSpec((None, rows, tc), lambda i, w: (w[0], 0, i)),
                      pl.BlockSpec((3, rows, tc), lambda i, w: (0, 0, i))],
            out_specs=pl.BlockSpec((rows, tc), lambda i, w: (0, i))))(
        jnp.reshape(me, (1,)).astype(jnp.int32), p, land)


def _sum_slots(a, *, name):
    n = a.shape[0]

    def body(a_ref, o_ref):
        acc = a_ref[0]
        for k in range(1, n):
            acc = acc + a_ref[k]
        o_ref[...] = acc

    return pl.pallas_call(body, name=name, out_shape=jax.ShapeDtypeStruct(a.shape[1:], a.dtype))(a)


def _elementwise(fn, ins, n_out, block, *, name):
    shape = ins[0].shape
    grid = tuple(s // b for s, b in zip(shape, block))
    n_in = len(ins)

    def body(*refs):
        outs = fn(*[r[...] for r in refs[:n_in]])
        for o_ref, val in zip(refs[n_in:], outs):
            o_ref[...] = val

    spec = pl.BlockSpec(block, lambda i, j, k: (i, j, k))
    return pl.pallas_call(body, name=name, grid=grid, in_specs=[spec] * n_in, out_specs=[spec] * n_out,
                          out_shape=[jax.ShapeDtypeStruct(shape, F32)] * n_out,
                          compiler_params=_cp(*["parallel"] * 3))(*ins)


def _adamw_math(w, g, m, v):
    mn = ADAM_B1 * m + (1.0 - ADAM_B1) * g
    vn = ADAM_B2 * v + (1.0 - ADAM_B2) * (g * g)
    m_hat = mn / (1.0 - ADAM_B1 ** ADAM_STEP)
    v_hat = vn / (1.0 - ADAM_B2 ** ADAM_STEP)
    return -ADAM_LR * (m_hat / (jnp.sqrt(v_hat) + ADAM_EPS) + ADAM_WD * w), mn, vn


def _adamw(w, g, m, v, block, *, name):
    return _elementwise(_adamw_math, [w, g, m, v], 3, block, name=name)


def _interleave_layers(layers, *, tc, name):
    rows, cols = layers[0].shape
    n = len(layers)

    def body(*refs):
        for l in range(n):
            refs[n][:, l, :] = refs[l][...]

    return pl.pallas_call(body, name=name, grid=(cols // tc,),
                          in_specs=[pl.BlockSpec((rows, tc), lambda i: (0, i))] * n,
                          out_specs=pl.BlockSpec((rows, n, tc), lambda i: (0, 0, i)),
                          out_shape=jax.ShapeDtypeStruct((rows, n, cols), layers[0].dtype),
                          compiler_params=_cp("parallel"))(*layers)


def _adamw_small(ws, gs, ms, vs, *, name):
    n = len(ws)

    def body(*refs):
        w, g, m, v, outs = refs[:n], refs[n:2 * n], refs[2 * n:3 * n], refs[3 * n:4 * n], refs[4 * n:]
        for k in range(n):
            for slot, val in enumerate(_adamw_math(w[k][...], g[k][...], m[k][...], v[k][...])):
                outs[slot * n + k][...] = val

    outs = pl.pallas_call(body, name=name, out_shape=[jax.ShapeDtypeStruct(a.shape, F32) for a in ws] * 3)(
        *ws, *gs, *ms, *vs)
    return outs[:n], outs[n:2 * n], outs[2 * n:]


def _to_kernel_order(wt):
    gates = jnp.pad(wt[2048:2056], ((0, LANE - 2 * A_HEADS), (0, 0)))
    return jnp.concatenate([wt[0:2048], wt[2056:2568], wt[2824:3336], wt[2568:2696], wt[2696:2824], gates], axis=0)


def _from_kernel_order(main, tail):
    return jnp.concatenate([main[0:2048], tail[C_BG - DH_MAIN:C_BG - DH_MAIN + 2 * A_HEADS],
                            main[C_QB:C_QB + B_WIDTH], tail[0:B_KV_WIDTH], tail[B_KV_WIDTH:2 * B_KV_WIDTH],
                            main[C_ZB:C_ZB + B_WIDTH]], axis=0)


def _gate_params(a_log, dt_bias):
    return jnp.pad(jnp.stack([a_log, dt_bias]), ((0, SUBLANE - 2), (A_HEADS, LANE - 2 * A_HEADS)))


SMALL = ("conv_w", "a_log", "dt_bias", "norm_w", "sinks", "ln_g", "ln_b")


def _pack(parts, cols):
    flat = jnp.concatenate([p.reshape(-1) for p in parts])
    rows = -(-flat.shape[0] // cols)
    return jnp.pad(flat, (0, rows * cols - flat.shape[0])).reshape(rows, cols)


def _unpack(packed, shapes):
    flat = packed.reshape(-1)
    out, at = [], 0
    for s in shapes:
        n = math.prod(s)
        out.append(flat[at:at + n].reshape(s))
        at += n
    return out


def kernel(x, w_in, conv_w, a_log, dt_bias, norm_w, sinks, w_out, ln_g, ln_b, loss_target, m_w_in, m_conv_w, m_a_log, m_dt_bias, m_norm_w, m_sinks, m_w_out, m_ln_g, m_ln_b, v_w_in, v_conv_w, v_a_log, v_dt_bias, v_norm_w, v_sinks, v_w_out, v_ln_g, v_ln_b):
    xi, yi, ci = lax.axis_index("x"), lax.axis_index("y"), lax.axis_index("c")
    me = 2 * xi + yi

    to_t = lambda a: jnp.transpose(a, (2, 0, 1))
    from_t = lambda a: jnp.transpose(a, (1, 2, 0))

    wt_shard = to_t(w_in)

    def pack_weights(l):
        rows = jnp.pad(wt_shard[:, l], ((0, IN_PAD - IN_SHARD), (0, 0)))
        return jnp.concatenate([rows, w_out[l]], axis=0).astype(BF16)

    pack0, pack1 = pack_weights(0), pack_weights(1)
    got_in0, g_conv = _gather_two_level(pack0[:IN_PAD], conv_w, name="gather_weights_0")
    conv_full = jnp.moveaxis(g_conv, 0, 2).reshape(DEPTH, CONV_K, 3 * A_WIDTH)
    carriers = ("dn_pre", "dn_wy", "dn_scan")
    cuts = (0, 288, 624, IN_PAD)
    gathers = {nm: _gather_exchange([pack1[cuts[i]:cuts[i + 1]]]) for i, nm in enumerate(carriers)}
    gathers.update(in_proj=_gather_exchange([pack0[IN_PAD:]]), swa=_gather_exchange([pack1[IN_PAD:]]))
    w_in_of = lambda rows: _to_kernel_order(rows[:, :IN_SHARD].reshape(IN_COLS, D_MODEL))
    w_out_of = lambda rows: rows.reshape(D_MODEL, D_MODEL)
    args0 = _layer_args(w_in_of(got_in0), conv_full[0], a_log[0], dt_bias[0], sinks[0], norm_w[0],
                        lambda got: w_out_of(got[0]))

    def args1(got):
        rows = jnp.concatenate([got[nm][0] for nm in carriers], axis=1)
        return _layer_args(w_in_of(rows), conv_full[1], a_log[1], dt_bias[1], sinks[1], norm_w[1],
                           w_out_of(got["swa"][0]))

    def pack_grads(g):
        gin = _from_kernel_order(*g["w_in"]).reshape(N_SHARD, IN_SHARD, D_MODEL)
        gin = jnp.pad(gin, ((0, 0), (0, IN_PAD - IN_SHARD), (0, 0)))
        return jnp.concatenate([gin, g["w_out"].reshape(N_SHARD, OUT_SHARD, D_MODEL)], axis=1).astype(BF16)

    packed = {}

    def reduce1(grads1):
        packed[1] = pack_grads(grads1)
        half_rows = packed[1].shape[1] // 2
        return dict(dn_chunk=_reduce_scatter_exchange(packed[1], 0, half_rows),
                    swa=_reduce_scatter_exchange(packed[1], half_rows, half_rows))

    def reduce0(grads0, grads1, loss_tile):
        g0 = pack_grads(grads0)
        from_sibling = _run_exchange(_pair_window_exchange(g0), name="pair_reduce_0")[0]
        packed[0] = _pair_add(g0, from_sibling, ci, name="pair_add_0")
        gsmall = _pack([jnp.stack([g[nm] for g in (grads0, grads1)]) for nm in SMALL] + [loss_tile[0, 0:1]], D_MODEL)
        return _chip_scatter_exchange(packed[0], gsmall)

    _, dx, grads, landed1, (landed0, landed_small) = _local_step(
        x[0], loss_target[0], args0, args1, ln_g, ln_b, gathers=gathers, reduce1=reduce1, reduce0=reduce0)

    small_shapes = [(DEPTH,) + grads[0][nm].shape for nm in SMALL]
    halves = [_sum_chips(packed[0], landed0, me, tc=2 * LANE, name="reduce_sum_0"),
              _sum_scatter(packed[1], [landed1["dn_chunk"][0], landed1["swa"][0]], me, ci, tc=2 * LANE,
                           name="reduce_sum_1")]
    s_small = _sum_slots(landed_small, name="reduce_sum_small")
    others = _run_exchange(_share_exchange(halves), name="pair_share")
    full = [jnp.where(ci == 0, jnp.concatenate([mine, other], axis=1), jnp.concatenate([other, mine], axis=1))
            for mine, other in zip(halves, others)]
    grad_in_layers = [f[:IN_SHARD] for f in full]
    grad_out = jnp.stack([f[IN_PAD:] for f in full])
    out_blk = (1, OUT_SHARD, D_MODEL)
    *small_grads, loss = _unpack(s_small, small_shapes + [()])
    gs = dict(zip(SMALL, small_grads))
    gs["conv_w"] = lax.dynamic_slice_in_dim(gs["conv_w"], me * CONV_SHARD, CONV_SHARD, axis=2)

    grad_in_t = _interleave_layers(grad_in_layers, tc=2 * LANE, name="grad_in_layers")
    d_in, nm_in, nv_in = (from_t(o) for o in _adamw(to_t(w_in), grad_in_t, to_t(m_w_in), to_t(v_w_in),
                                                    (IN_SHARD // 6, DEPTH, D_MODEL), name="adamw_in"))
    grad_in = from_t(grad_in_t)
    d_out, nm_out, nv_out = _adamw(w_out, grad_out, m_w_out, v_w_out, out_blk, name="adamw_out")
    ws = dict(conv_w=conv_w, a_log=a_log, dt_bias=dt_bias, norm_w=norm_w, sinks=sinks, ln_g=ln_g, ln_b=ln_b)
    ms = dict(conv_w=m_conv_w, a_log=m_a_log, dt_bias=m_dt_bias, norm_w=m_norm_w, sinks=m_sinks, ln_g=m_ln_g, ln_b=m_ln_b)
    vs = dict(conv_w=v_conv_w, a_log=v_a_log, dt_bias=v_dt_bias, norm_w=v_norm_w, sinks=v_sinks, ln_g=v_ln_g, ln_b=v_ln_b)
    d_s, nm_s, nv_s = (dict(zip(SMALL, o)) for o in _adamw_small(*[[d[nm] for nm in SMALL] for d in (ws, gs, ms, vs)],
                                                                 name="adamw_small"))

    def in_order(big_in, small, big_out):
        return (big_in, small["conv_w"], small["a_log"], small["dt_bias"], small["norm_w"], small["sinks"], big_out,
                small["ln_g"], small["ln_b"])

    return (loss, dx[None], *in_order(grad_in, gs, grad_out), *in_order(d_in, d_s, d_out),
            *in_order(nm_in, nm_s, nm_out), *in_order(nv_in, nv_s, nv_out))
```
